```python
import math
import jax, jax.numpy as jnp
from jax import lax
import numpy as np

D_MODEL = 1024
BATCH = 8
SEQ = 4096
DEPTH = 2

SSD_EXPAND = 2
SSD_D_INNER = SSD_EXPAND * D_MODEL
SSD_HEAD_DIM = 64
SSD_N_HEADS = SSD_D_INNER // SSD_HEAD_DIM
SSD_N_GROUPS = 8
SSD_HEADS_PER_GROUP = SSD_N_HEADS // SSD_N_GROUPS
SSD_D_STATE = 128
SSD_CONV_WIDTH = 4
SSD_CHUNK = 128
SSD_CONV_DIM = SSD_D_INNER + 2 * SSD_N_GROUPS * SSD_D_STATE
SSD_IN_DIM = SSD_D_INNER + SSD_CONV_DIM + SSD_N_HEADS

ATTN_HEAD_DIM = 64
ATTN_N_Q_HEADS = D_MODEL // ATTN_HEAD_DIM
ATTN_N_KV_HEADS = 4
ATTN_REP = ATTN_N_Q_HEADS // ATTN_N_KV_HEADS
ATTN_WINDOW = 128
ATTN_QKV_DIM = (ATTN_N_Q_HEADS + 2 * ATTN_N_KV_HEADS) * ATTN_HEAD_DIM

D_FF = 4 * D_MODEL

N_MIXERS = 2
N_SSD_LAYERS = (DEPTH + 1) // 2
N_ATTN_LAYERS = DEPTH // 2
NORM_EPS = 1e-6

kernel_name = "hybrid_ssd_swa_sink_sqrelu_trunk"


def rms_norm(x, w):
    xf = x.astype(jnp.float32)
    y = xf * lax.rsqrt(jnp.mean(xf * xf, axis=-1, keepdims=True) + NORM_EPS)
    return (y * w.astype(jnp.float32)).astype(x.dtype)


def causal_depthwise_conv(x, w, b):
    c = x.shape[-1]
    y = lax.conv_general_dilated(
        x, w.astype(x.dtype)[:, None, :], window_strides=(1,),
        padding=[(SSD_CONV_WIDTH - 1, 0)],
        dimension_numbers=("NWC", "WIO", "NWC"), feature_group_count=c)
    return y + b.astype(x.dtype)


def ssd_chunked_scan(xs, dt, a, bmat, cmat):
    b, l, g, j, p = xs.shape
    n = bmat.shape[-1]
    nc = l // SSD_CHUNK

    def to_chunks(t):
        t = t.astype(jnp.float32).reshape((b, nc, SSD_CHUNK) + t.shape[2:])
        return jnp.moveaxis(t, 1, 0)

    xc, dtc, bc, cc = to_chunks(xs), to_chunks(dt), to_chunks(bmat), to_chunks(cmat)
    ac = dtc * a.astype(jnp.float32)
    causal = jnp.tril(jnp.ones((SSD_CHUNK, SSD_CHUNK), dtype=bool))[None, :, :, None, None]

    def step(state, inp):
        x_q, dt_q, a_q, b_q, c_q = inp
        cum = jnp.cumsum(a_q, axis=1)
        diff = cum[:, :, None] - cum[:, None, :]
        decay = jnp.exp(jnp.where(causal, diff, -jnp.inf))
        cb = jnp.einsum("btgn,bsgn->btsg", c_q, b_q)
        y_intra = jnp.einsum("btsg,btsgj,bsgj,bsgjp->btgjp", cb, decay, dt_q, x_q)
        y_inter = jnp.einsum("btgn,bgjpn->btgjp", c_q, state) * jnp.exp(cum)[..., None]
        decay_to_end = jnp.exp(cum[:, -1:] - cum)
        new_state = state * jnp.exp(cum[:, -1])[..., None, None] + jnp.einsum(
            "bsgn,bsgj,bsgjp->bgjpn", b_q, dt_q * decay_to_end, x_q)
        return new_state, y_intra + y_inter

    state0 = jnp.zeros((b, g, j, p, n), jnp.float32)
    _, ys = lax.scan(step, state0, (xc, dtc, ac, bc, cc))
    return jnp.moveaxis(ys, 0, 1).reshape(b, l, g, j, p)


def ssd_mixer(u, w_in, conv_w, conv_b, dt_bias, a_log, d_skip, norm_w, w_out):
    b, l, _ = u.shape
    zxbcdt = u @ w_in
    z = zxbcdt[..., :SSD_D_INNER]
    xbc = zxbcdt[..., SSD_D_INNER:SSD_D_INNER + SSD_CONV_DIM]
    dt_raw = zxbcdt[..., SSD_D_INNER + SSD_CONV_DIM:]
    xbc = jax.nn.silu(causal_depthwise_conv(xbc, conv_w, conv_b))
    gn = SSD_N_GROUPS * SSD_D_STATE
    xs = xbc[..., :SSD_D_INNER].reshape(b, l, SSD_N_GROUPS, SSD_HEADS_PER_GROUP, SSD_HEAD_DIM)
    bmat = xbc[..., SSD_D_INNER:SSD_D_INNER + gn].reshape(b, l, SSD_N_GROUPS, SSD_D_STATE)
    cmat = xbc[..., SSD_D_INNER + gn:].reshape(b, l, SSD_N_GROUPS, SSD_D_STATE)
    dt = jax.nn.softplus(dt_raw.astype(jnp.float32) + dt_bias.astype(jnp.float32))
    dt = dt.reshape(b, l, SSD_N_GROUPS, SSD_HEADS_PER_GROUP)
    a = -jnp.exp(a_log.astype(jnp.float32)).reshape(SSD_N_GROUPS, SSD_HEADS_PER_GROUP)
    y = ssd_chunked_scan(xs, dt, a, bmat, cmat)
    y = y + d_skip.astype(jnp.float32).reshape(SSD_N_GROUPS, SSD_HEADS_PER_GROUP, 1) * xs.astype(jnp.float32)
    y = y.reshape(b, l, SSD_D_INNER) * jax.nn.silu(z.astype(jnp.float32))
    y = y.reshape(b, l, SSD_N_GROUPS, SSD_D_INNER // SSD_N_GROUPS)
    y = y * lax.rsqrt(jnp.mean(y * y, axis=-1, keepdims=True) + NORM_EPS)
    y = (y.reshape(b, l, SSD_D_INNER) * norm_w.astype(jnp.float32)).astype(u.dtype)
    return y @ w_out


def swa_sink_attention(u, w_qkv, b_qkv, sinks, w_o, b_o):
    b, l, _ = u.shape
    nb = l // ATTN_WINDOW
    qkv = u @ w_qkv + b_qkv
    qd = ATTN_N_Q_HEADS * ATTN_HEAD_DIM
    kd = ATTN_N_KV_HEADS * ATTN_HEAD_DIM
    q = qkv[..., :qd].reshape(b, nb, ATTN_WINDOW, ATTN_N_KV_HEADS, ATTN_REP, ATTN_HEAD_DIM)
    k = qkv[..., qd:qd + kd].reshape(b, nb, ATTN_WINDOW, ATTN_N_KV_HEADS, ATTN_HEAD_DIM)
    v = qkv[..., qd + kd:].reshape(b, nb, ATTN_WINDOW, ATTN_N_KV_HEADS, ATTN_HEAD_DIM)

    def with_prev_block(t):
        prev = jnp.concatenate([jnp.zeros_like(t[:, :1]), t[:, :-1]], axis=1)
        return jnp.concatenate([prev, t], axis=2)

    kb, vb = with_prev_block(k), with_prev_block(v)
    scores = jnp.einsum("bnqkrd,bnskd->bnkrqs", q, kb).astype(jnp.float32) * (ATTN_HEAD_DIM ** -0.5)
    qpos = jnp.arange(ATTN_WINDOW) + ATTN_WINDOW
    kpos = jnp.arange(2 * ATTN_WINDOW)
    rel = qpos[:, None] - kpos[None, :]
    band = (rel >= 0) & (rel < ATTN_WINDOW)
    blk = jnp.arange(nb)[:, None, None]
    valid = band[None] & ~((blk == 0) & (kpos[None, None, :] < ATTN_WINDOW))
    scores = jnp.where(valid[None, :, None, None], scores, -jnp.inf)
    sink = sinks.astype(jnp.float32).reshape(1, 1, ATTN_N_KV_HEADS, ATTN_REP, 1, 1)
    m = jnp.maximum(jnp.max(scores, axis=-1, keepdims=True), sink)
    e = jnp.exp(scores - m)
    probs = e / (jnp.sum(e, axis=-1, keepdims=True) + jnp.exp(sink - m))
    out = jnp.einsum("bnkrqs,bnskd->bnqkrd", probs.astype(vb.dtype), vb).reshape(b, l, qd)
    return out @ w_o + b_o


def sqrelu_mlp(u, w_up, w_down):
    return jnp.square(jax.nn.relu(u @ w_up)) @ w_down


def _fwd_setup_inputs(seed: int = 0) -> dict:
    key = jax.random.key(seed)
    ks = jax.random.split(key, 24)
    f32 = jnp.float32

    def normal(k, shape, scale):
        return jax.random.normal(k, shape, f32) * scale

    def gain(k, shape):
        return 1.0 + 0.05 * jax.random.normal(k, shape, f32)

    x = jax.random.normal(ks[0], (BATCH, SEQ, D_MODEL), f32)
    ssd_w_in = normal(ks[1], (N_SSD_LAYERS, D_MODEL, SSD_IN_DIM), D_MODEL ** -0.5)
    ssd_conv_w = normal(ks[2], (N_SSD_LAYERS, SSD_CONV_WIDTH, SSD_CONV_DIM), SSD_CONV_WIDTH ** -0.5)
    ssd_conv_b = normal(ks[3], (N_SSD_LAYERS, SSD_CONV_DIM), 0.02)
    dt0 = jnp.exp(jax.random.uniform(ks[4], (N_SSD_LAYERS, SSD_N_HEADS), f32,
                                     math.log(1e-3), math.log(1e-1)))
    ssd_dt_bias = dt0 + jnp.log(-jnp.expm1(-dt0))
    ssd_a_log = jnp.log(jax.random.uniform(ks[5], (N_SSD_LAYERS, SSD_N_HEADS), f32, 1.0, 16.0))
    ssd_d = gain(ks[6], (N_SSD_LAYERS, SSD_N_HEADS))
    ssd_norm_w = gain(ks[7], (N_SSD_LAYERS, SSD_D_INNER))
    ssd_w_out = normal(ks[8], (N_SSD_LAYERS, SSD_D_INNER, D_MODEL), SSD_D_INNER ** -0.5)
    attn_w_qkv = normal(ks[9], (N_ATTN_LAYERS, D_MODEL, ATTN_QKV_DIM), D_MODEL ** -0.5)
    attn_b_qkv = normal(ks[10], (N_ATTN_LAYERS, ATTN_QKV_DIM), 0.02)
    attn_sinks = normal(ks[11], (N_ATTN_LAYERS, ATTN_N_Q_HEADS), 1.0)
    attn_w_o = normal(ks[12], (N_ATTN_LAYERS, ATTN_N_Q_HEADS * ATTN_HEAD_DIM, D_MODEL),
                      (ATTN_N_Q_HEADS * ATTN_HEAD_DIM) ** -0.5)
    attn_b_o = normal(ks[13], (N_ATTN_LAYERS, D_MODEL), 0.02)
    mlp_w_up = normal(ks[14], (DEPTH, D_MODEL, D_FF), D_MODEL ** -0.5)
    mlp_w_down = normal(ks[15], (DEPTH, D_FF, D_MODEL), D_FF ** -0.5)
    mix_pre_norm = gain(ks[16], (DEPTH, D_MODEL))
    mix_post_norm = gain(ks[17], (DEPTH, D_MODEL))
    ffn_pre_norm = gain(ks[18], (DEPTH, D_MODEL))
    ffn_post_norm = gain(ks[19], (DEPTH, D_MODEL))
    return {
        "x": x,
        "ssd_w_in": ssd_w_in, "ssd_conv_w": ssd_conv_w, "ssd_conv_b": ssd_conv_b,
        "ssd_dt_bias": ssd_dt_bias, "ssd_a_log": ssd_a_log, "ssd_d": ssd_d,
        "ssd_norm_w": ssd_norm_w, "ssd_w_out": ssd_w_out,
        "attn_w_qkv": attn_w_qkv, "attn_b_qkv": attn_b_qkv, "attn_sinks": attn_sinks,
        "attn_w_o": attn_w_o, "attn_b_o": attn_b_o,
        "mlp_w_up": mlp_w_up, "mlp_w_down": mlp_w_down,
        "mix_pre_norm": mix_pre_norm, "mix_post_norm": mix_post_norm,
        "ffn_pre_norm": ffn_pre_norm, "ffn_post_norm": ffn_post_norm,
    }


def _fwd_reference(x, ssd_w_in, ssd_conv_w, ssd_conv_b, ssd_dt_bias, ssd_a_log, ssd_d,
              ssd_norm_w, ssd_w_out, attn_w_qkv, attn_b_qkv, attn_sinks, attn_w_o,
              attn_b_o, mlp_w_up, mlp_w_down, mix_pre_norm, mix_post_norm,
              ffn_pre_norm, ffn_post_norm):
    h = x
    for i in range(DEPTH):
        u = rms_norm(h, mix_pre_norm[i])
        j = i // N_MIXERS
        if i % N_MIXERS == 0:
            mix = ssd_mixer(u, ssd_w_in[j], ssd_conv_w[j], ssd_conv_b[j], ssd_dt_bias[j],
                            ssd_a_log[j], ssd_d[j], ssd_norm_w[j], ssd_w_out[j])
        else:
            mix = swa_sink_attention(u, attn_w_qkv[j], attn_b_qkv[j], attn_sinks[j],
                                     attn_w_o[j], attn_b_o[j])
        h = h + rms_norm(mix, mix_post_norm[i])
        f = sqrelu_mlp(rms_norm(h, ffn_pre_norm[i]), mlp_w_up[i], mlp_w_down[i])
        h = h + rms_norm(f, ffn_post_norm[i])
    return h


import jax as _jax
import jax.numpy as _jnp

TWIN_FORMAT = 'train_step'
FWD_PARAMS = ['x', 'ssd_w_in', 'ssd_conv_w', 'ssd_conv_b', 'ssd_dt_bias', 'ssd_a_log', 'ssd_d', 'ssd_norm_w', 'ssd_w_out', 'attn_w_qkv', 'attn_b_qkv', 'attn_sinks', 'attn_w_o', 'attn_b_o', 'mlp_w_up', 'mlp_w_down', 'mix_pre_norm', 'mix_post_norm', 'ffn_pre_norm', 'ffn_post_norm']
TWIN_WEIGHTS = ['ssd_w_in', 'ssd_conv_w', 'ssd_conv_b', 'ssd_dt_bias', 'ssd_a_log', 'ssd_d', 'ssd_norm_w', 'ssd_w_out', 'attn_w_qkv', 'attn_b_qkv', 'attn_sinks', 'attn_w_o', 'attn_b_o', 'mlp_w_up', 'mlp_w_down', 'mix_pre_norm', 'mix_post_norm', 'ffn_pre_norm', 'ffn_post_norm']
TWIN_DIFF_INPUT = 'x'
TWIN_INPUTS = ['x', 'ssd_w_in', 'ssd_conv_w', 'ssd_conv_b', 'ssd_dt_bias', 'ssd_a_log', 'ssd_d', 'ssd_norm_w', 'ssd_w_out', 'attn_w_qkv', 'attn_b_qkv', 'attn_sinks', 'attn_w_o', 'attn_b_o', 'mlp_w_up', 'mlp_w_down', 'mix_pre_norm', 'mix_post_norm', 'ffn_pre_norm', 'ffn_post_norm', 'loss_target', 'm_ssd_w_in', 'm_ssd_conv_w', 'm_ssd_conv_b', 'm_ssd_dt_bias', 'm_ssd_a_log', 'm_ssd_d', 'm_ssd_norm_w', 'm_ssd_w_out', 'm_attn_w_qkv', 'm_attn_b_qkv', 'm_attn_sinks', 'm_attn_w_o', 'm_attn_b_o', 'm_mlp_w_up', 'm_mlp_w_down', 'm_mix_pre_norm', 'm_mix_post_norm', 'm_ffn_pre_norm', 'm_ffn_post_norm', 'v_ssd_w_in', 'v_ssd_conv_w', 'v_ssd_conv_b', 'v_ssd_dt_bias', 'v_ssd_a_log', 'v_ssd_d', 'v_ssd_norm_w', 'v_ssd_w_out', 'v_attn_w_qkv', 'v_attn_b_qkv', 'v_attn_sinks', 'v_attn_w_o', 'v_attn_b_o', 'v_mlp_w_up', 'v_mlp_w_down', 'v_mix_pre_norm', 'v_mix_post_norm', 'v_ffn_pre_norm', 'v_ffn_post_norm']
TWIN_OUTPUTS = ['loss', 'grad_x', 'grad_ssd_w_in', 'grad_ssd_conv_w', 'grad_ssd_conv_b', 'grad_ssd_dt_bias', 'grad_ssd_a_log', 'grad_ssd_d', 'grad_ssd_norm_w', 'grad_ssd_w_out', 'grad_attn_w_qkv', 'grad_attn_b_qkv', 'grad_attn_sinks', 'grad_attn_w_o', 'grad_attn_b_o', 'grad_mlp_w_up', 'grad_mlp_w_down', 'grad_mix_pre_norm', 'grad_mix_post_norm', 'grad_ffn_pre_norm', 'grad_ffn_post_norm', 'delta_ssd_w_in', 'delta_ssd_conv_w', 'delta_ssd_conv_b', 'delta_ssd_dt_bias', 'delta_ssd_a_log', 'delta_ssd_d', 'delta_ssd_norm_w', 'delta_ssd_w_out', 'delta_attn_w_qkv', 'delta_attn_b_qkv', 'delta_attn_sinks', 'delta_attn_w_o', 'delta_attn_b_o', 'delta_mlp_w_up', 'delta_mlp_w_down', 'delta_mix_pre_norm', 'delta_mix_post_norm', 'delta_ffn_pre_norm', 'delta_ffn_post_norm', 'new_m_ssd_w_in', 'new_m_ssd_conv_w', 'new_m_ssd_conv_b', 'new_m_ssd_dt_bias', 'new_m_ssd_a_log', 'new_m_ssd_d', 'new_m_ssd_norm_w', 'new_m_ssd_w_out', 'new_m_attn_w_qkv', 'new_m_attn_b_qkv', 'new_m_attn_sinks', 'new_m_attn_w_o', 'new_m_attn_b_o', 'new_m_mlp_w_up', 'new_m_mlp_w_down', 'new_m_mix_pre_norm', 'new_m_mix_post_norm', 'new_m_ffn_pre_norm', 'new_m_ffn_post_norm', 'new_v_ssd_w_in', 'new_v_ssd_conv_w', 'new_v_ssd_conv_b', 'new_v_ssd_dt_bias', 'new_v_ssd_a_log', 'new_v_ssd_d', 'new_v_ssd_norm_w', 'new_v_ssd_w_out', 'new_v_attn_w_qkv', 'new_v_attn_b_qkv', 'new_v_attn_sinks', 'new_v_attn_w_o', 'new_v_attn_b_o', 'new_v_mlp_w_up', 'new_v_mlp_w_down', 'new_v_mix_pre_norm', 'new_v_mix_post_norm', 'new_v_ffn_pre_norm', 'new_v_ffn_post_norm']
TWIN_LEAF_KINDS = {'loss': 'loss', 'grad_x': 'grad_x', 'grad_ssd_w_in': 'grad_w', 'grad_ssd_conv_w': 'grad_w', 'grad_ssd_conv_b': 'grad_w', 'grad_ssd_dt_bias': 'grad_w', 'grad_ssd_a_log': 'grad_w', 'grad_ssd_d': 'grad_w', 'grad_ssd_norm_w': 'grad_w', 'grad_ssd_w_out': 'grad_w', 'grad_attn_w_qkv': 'grad_w', 'grad_attn_b_qkv': 'grad_w', 'grad_attn_sinks': 'grad_w', 'grad_attn_w_o': 'grad_w', 'grad_attn_b_o': 'grad_w', 'grad_mlp_w_up': 'grad_w', 'grad_mlp_w_down': 'grad_w', 'grad_mix_pre_norm': 'grad_w', 'grad_mix_post_norm': 'grad_w', 'grad_ffn_pre_norm': 'grad_w', 'grad_ffn_post_norm': 'grad_w', 'delta_ssd_w_in': 'delta_w', 'delta_ssd_conv_w': 'delta_w', 'delta_ssd_conv_b': 'delta_w', 'delta_ssd_dt_bias': 'delta_w', 'delta_ssd_a_log': 'delta_w', 'delta_ssd_d': 'delta_w', 'delta_ssd_norm_w': 'delta_w', 'delta_ssd_w_out': 'delta_w', 'delta_attn_w_qkv': 'delta_w', 'delta_attn_b_qkv': 'delta_w', 'delta_attn_sinks': 'delta_w', 'delta_attn_w_o': 'delta_w', 'delta_attn_b_o': 'delta_w', 'delta_mlp_w_up': 'delta_w', 'delta_mlp_w_down': 'delta_w', 'delta_mix_pre_norm': 'delta_w', 'delta_mix_post_norm': 'delta_w', 'delta_ffn_pre_norm': 'delta_w', 'delta_ffn_post_norm': 'delta_w', 'new_m_ssd_w_in': 'new_m', 'new_m_ssd_conv_w': 'new_m', 'new_m_ssd_conv_b': 'new_m', 'new_m_ssd_dt_bias': 'new_m', 'new_m_ssd_a_log': 'new_m', 'new_m_ssd_d': 'new_m', 'new_m_ssd_norm_w': 'new_m', 'new_m_ssd_w_out': 'new_m', 'new_m_attn_w_qkv': 'new_m', 'new_m_attn_b_qkv': 'new_m', 'new_m_attn_sinks': 'new_m', 'new_m_attn_w_o': 'new_m', 'new_m_attn_b_o': 'new_m', 'new_m_mlp_w_up': 'new_m', 'new_m_mlp_w_down': 'new_m', 'new_m_mix_pre_norm': 'new_m', 'new_m_mix_post_norm': 'new_m', 'new_m_ffn_pre_norm': 'new_m', 'new_m_ffn_post_norm': 'new_m', 'new_v_ssd_w_in': 'new_v', 'new_v_ssd_conv_w': 'new_v', 'new_v_ssd_conv_b': 'new_v', 'new_v_ssd_dt_bias': 'new_v', 'new_v_ssd_a_log': 'new_v', 'new_v_ssd_d': 'new_v', 'new_v_ssd_norm_w': 'new_v', 'new_v_ssd_w_out': 'new_v', 'new_v_attn_w_qkv': 'new_v', 'new_v_attn_b_qkv': 'new_v', 'new_v_attn_sinks': 'new_v', 'new_v_attn_w_o': 'new_v', 'new_v_attn_b_o': 'new_v', 'new_v_mlp_w_up': 'new_v', 'new_v_mlp_w_down': 'new_v', 'new_v_mix_pre_norm': 'new_v', 'new_v_mix_post_norm': 'new_v', 'new_v_ffn_pre_norm': 'new_v', 'new_v_ffn_post_norm': 'new_v'}


def _forward(args):
    return _fwd_reference(*[args[k] for k in FWD_PARAMS])


def _output_shape():
    out = _jax.eval_shape(lambda: _forward(_fwd_setup_inputs(0)))
    return out.shape, out.dtype

N_MICROBATCH = 1
ADAM_LR = 0.001
ADAM_B1 = 0.9
ADAM_B2 = 0.999
ADAM_EPS = 1e-08
ADAM_WD = 0.01
ADAM_STEP = 10
PER_EXAMPLE_BATCH_AXIS = {'x': 0, 'loss_target': 0}
SHARED_INPUTS = []
_WEIGHT_DTYPES = {'ssd_w_in': _jnp.float32, 'ssd_conv_w': _jnp.float32, 'ssd_conv_b': _jnp.float32, 'ssd_dt_bias': _jnp.float32, 'ssd_a_log': _jnp.float32, 'ssd_d': _jnp.float32, 'ssd_norm_w': _jnp.float32, 'ssd_w_out': _jnp.float32, 'attn_w_qkv': _jnp.float32, 'attn_b_qkv': _jnp.float32, 'attn_sinks': _jnp.float32, 'attn_w_o': _jnp.float32, 'attn_b_o': _jnp.float32, 'mlp_w_up': _jnp.float32, 'mlp_w_down': _jnp.float32, 'mix_pre_norm': _jnp.float32, 'mix_post_norm': _jnp.float32, 'ffn_pre_norm': _jnp.float32, 'ffn_post_norm': _jnp.float32}
MOMENT_SCALE = {'ssd_w_in': 5.589422e-01, 'ssd_conv_w': 4.344514e+00, 'ssd_conv_b': 1.303929e+01, 'ssd_dt_bias': 4.173499e+00, 'ssd_a_log': 1.757547e+01, 'ssd_d': 2.552687e+01, 'ssd_norm_w': 8.524325e+00, 'ssd_w_out': 1.250919e+01, 'attn_w_qkv': 1.836303e+01, 'attn_b_qkv': 6.945690e+01, 'attn_sinks': 1.047178e+00, 'attn_w_o': 2.057222e+01, 'attn_b_o': 8.205258e+01, 'mlp_w_up': 4.532228e+00, 'mlp_w_down': 2.077624e+01, 'mix_pre_norm': 1.639361e+01, 'mix_post_norm': 4.178952e+01, 'ffn_pre_norm': 8.843305e+00, 'ffn_post_norm': 3.843289e+01}


def _to_microbatches(a, axis):
    t = _jnp.moveaxis(a, axis, 0)
    t = t.reshape((N_MICROBATCH, t.shape[0] // N_MICROBATCH) + t.shape[1:])
    return _jnp.moveaxis(t, 1, axis + 1)


def setup_inputs(seed: int = 0) -> dict:
    inp = _fwd_setup_inputs(seed)
    key = _jax.random.fold_in(_jax.random.key(seed), 7919)
    shape, _ = _output_shape()
    out = dict(inp)
    out["loss_target"] = _jax.random.normal(_jax.random.fold_in(key, 0), shape, _jnp.float32)
    for i, name in enumerate(TWIN_WEIGHTS):
        w = inp[name].astype(_jnp.float32)
        if MOMENT_SCALE is None:
            s = _jnp.sqrt(_jnp.mean(_jnp.square(w)) + 1e-30)
        else:
            s = MOMENT_SCALE[name]
        km, kv = _jax.random.split(_jax.random.fold_in(key, i + 1))
        out[name] = w
        out["m_" + name] = s * _jax.random.normal(km, w.shape, _jnp.float32)
        out["v_" + name] = (s * s) * _jax.random.uniform(kv, w.shape, _jnp.float32, 0.5, 1.5)
    if N_MICROBATCH > 1:
        for name, axis in PER_EXAMPLE_BATCH_AXIS.items():
            out[name] = _to_microbatches(out[name], axis)
    return {'x': out['x'], 'ssd_w_in': out['ssd_w_in'], 'ssd_conv_w': out['ssd_conv_w'], 'ssd_conv_b': out['ssd_conv_b'], 'ssd_dt_bias': out['ssd_dt_bias'], 'ssd_a_log': out['ssd_a_log'], 'ssd_d': out['ssd_d'], 'ssd_norm_w': out['ssd_norm_w'], 'ssd_w_out': out['ssd_w_out'], 'attn_w_qkv': out['attn_w_qkv'], 'attn_b_qkv': out['attn_b_qkv'], 'attn_sinks': out['attn_sinks'], 'attn_w_o': out['attn_w_o'], 'attn_b_o': out['attn_b_o'], 'mlp_w_up': out['mlp_w_up'], 'mlp_w_down': out['mlp_w_down'], 'mix_pre_norm': out['mix_pre_norm'], 'mix_post_norm': out['mix_post_norm'], 'ffn_pre_norm': out['ffn_pre_norm'], 'ffn_post_norm': out['ffn_post_norm'], 'loss_target': out['loss_target'], 'm_ssd_w_in': out['m_ssd_w_in'], 'm_ssd_conv_w': out['m_ssd_conv_w'], 'm_ssd_conv_b': out['m_ssd_conv_b'], 'm_ssd_dt_bias': out['m_ssd_dt_bias'], 'm_ssd_a_log': out['m_ssd_a_log'], 'm_ssd_d': out['m_ssd_d'], 'm_ssd_norm_w': out['m_ssd_norm_w'], 'm_ssd_w_out': out['m_ssd_w_out'], 'm_attn_w_qkv': out['m_attn_w_qkv'], 'm_attn_b_qkv': out['m_attn_b_qkv'], 'm_attn_sinks': out['m_attn_sinks'], 'm_attn_w_o': out['m_attn_w_o'], 'm_attn_b_o': out['m_attn_b_o'], 'm_mlp_w_up': out['m_mlp_w_up'], 'm_mlp_w_down': out['m_mlp_w_down'], 'm_mix_pre_norm': out['m_mix_pre_norm'], 'm_mix_post_norm': out['m_mix_post_norm'], 'm_ffn_pre_norm': out['m_ffn_pre_norm'], 'm_ffn_post_norm': out['m_ffn_post_norm'], 'v_ssd_w_in': out['v_ssd_w_in'], 'v_ssd_conv_w': out['v_ssd_conv_w'], 'v_ssd_conv_b': out['v_ssd_conv_b'], 'v_ssd_dt_bias': out['v_ssd_dt_bias'], 'v_ssd_a_log': out['v_ssd_a_log'], 'v_ssd_d': out['v_ssd_d'], 'v_ssd_norm_w': out['v_ssd_norm_w'], 'v_ssd_w_out': out['v_ssd_w_out'], 'v_attn_w_qkv': out['v_attn_w_qkv'], 'v_attn_b_qkv': out['v_attn_b_qkv'], 'v_attn_sinks': out['v_attn_sinks'], 'v_attn_w_o': out['v_attn_w_o'], 'v_attn_b_o': out['v_attn_b_o'], 'v_mlp_w_up': out['v_mlp_w_up'], 'v_mlp_w_down': out['v_mlp_w_down'], 'v_mix_pre_norm': out['v_mix_pre_norm'], 'v_mix_post_norm': out['v_mix_post_norm'], 'v_ffn_pre_norm': out['v_ffn_pre_norm'], 'v_ffn_post_norm': out['v_ffn_post_norm']}


def _loss(weights, diff, rest, loss_target):
    with _jax.named_scope("forward"):
        args = {**rest, TWIN_DIFF_INPUT: diff, **{k: w.astype(_WEIGHT_DTYPES[k]) for k, w in weights.items()}}
        y = _forward(args)
    with _jax.named_scope("loss_head"):
        err = _jnp.square(y.astype(_jnp.float32) - loss_target)
        return 0.5 * _jnp.sum(_jnp.mean(err, axis=-1)) if err.ndim else 0.5 * err


def _adamw(w, g, m, v):
    m = ADAM_B1 * m + (1.0 - ADAM_B1) * g
    v = ADAM_B2 * v + (1.0 - ADAM_B2) * _jnp.square(g)
    m_hat = m / (1.0 - ADAM_B1 ** ADAM_STEP)
    v_hat = v / (1.0 - ADAM_B2 ** ADAM_STEP)
    delta = -ADAM_LR * (m_hat / (_jnp.sqrt(v_hat) + ADAM_EPS) + ADAM_WD * w)
    return delta, m, v


def reference(x, ssd_w_in, ssd_conv_w, ssd_conv_b, ssd_dt_bias, ssd_a_log, ssd_d, ssd_norm_w, ssd_w_out, attn_w_qkv, attn_b_qkv, attn_sinks, attn_w_o, attn_b_o, mlp_w_up, mlp_w_down, mix_pre_norm, mix_post_norm, ffn_pre_norm, ffn_post_norm, loss_target, m_ssd_w_in, m_ssd_conv_w, m_ssd_conv_b, m_ssd_dt_bias, m_ssd_a_log, m_ssd_d, m_ssd_norm_w, m_ssd_w_out, m_attn_w_qkv, m_attn_b_qkv, m_attn_sinks, m_attn_w_o, m_attn_b_o, m_mlp_w_up, m_mlp_w_down, m_mix_pre_norm, m_mix_post_norm, m_ffn_pre_norm, m_ffn_post_norm, v_ssd_w_in, v_ssd_conv_w, v_ssd_conv_b, v_ssd_dt_bias, v_ssd_a_log, v_ssd_d, v_ssd_norm_w, v_ssd_w_out, v_attn_w_qkv, v_attn_b_qkv, v_attn_sinks, v_attn_w_o, v_attn_b_o, v_mlp_w_up, v_mlp_w_down, v_mix_pre_norm, v_mix_post_norm, v_ffn_pre_norm, v_ffn_post_norm):
    given = dict(x=x, ssd_w_in=ssd_w_in, ssd_conv_w=ssd_conv_w, ssd_conv_b=ssd_conv_b, ssd_dt_bias=ssd_dt_bias, ssd_a_log=ssd_a_log, ssd_d=ssd_d, ssd_norm_w=ssd_norm_w, ssd_w_out=ssd_w_out, attn_w_qkv=attn_w_qkv, attn_b_qkv=attn_b_qkv, attn_sinks=attn_sinks, attn_w_o=attn_w_o, attn_b_o=attn_b_o, mlp_w_up=mlp_w_up, mlp_w_down=mlp_w_down, mix_pre_norm=mix_pre_norm, mix_post_norm=mix_post_norm, ffn_pre_norm=ffn_pre_norm, ffn_post_norm=ffn_post_norm, loss_target=loss_target, m_ssd_w_in=m_ssd_w_in, m_ssd_conv_w=m_ssd_conv_w, m_ssd_conv_b=m_ssd_conv_b, m_ssd_dt_bias=m_ssd_dt_bias, m_ssd_a_log=m_ssd_a_log, m_ssd_d=m_ssd_d, m_ssd_norm_w=m_ssd_norm_w, m_ssd_w_out=m_ssd_w_out, m_attn_w_qkv=m_attn_w_qkv, m_attn_b_qkv=m_attn_b_qkv, m_attn_sinks=m_attn_sinks, m_attn_w_o=m_attn_w_o, m_attn_b_o=m_attn_b_o, m_mlp_w_up=m_mlp_w_up, m_mlp_w_down=m_mlp_w_down, m_mix_pre_norm=m_mix_pre_norm, m_mix_post_norm=m_mix_post_norm, m_ffn_pre_norm=m_ffn_pre_norm, m_ffn_post_norm=m_ffn_post_norm, v_ssd_w_in=v_ssd_w_in, v_ssd_conv_w=v_ssd_conv_w, v_ssd_conv_b=v_ssd_conv_b, v_ssd_dt_bias=v_ssd_dt_bias, v_ssd_a_log=v_ssd_a_log, v_ssd_d=v_ssd_d, v_ssd_norm_w=v_ssd_norm_w, v_ssd_w_out=v_ssd_w_out, v_attn_w_qkv=v_attn_w_qkv, v_attn_b_qkv=v_attn_b_qkv, v_attn_sinks=v_attn_sinks, v_attn_w_o=v_attn_w_o, v_attn_b_o=v_attn_b_o, v_mlp_w_up=v_mlp_w_up, v_mlp_w_down=v_mlp_w_down, v_mix_pre_norm=v_mix_pre_norm, v_mix_post_norm=v_mix_post_norm, v_ffn_pre_norm=v_ffn_pre_norm, v_ffn_post_norm=v_ffn_post_norm)
    weights = {n: given[n] for n in TWIN_WEIGHTS}
    shared = {n: given[n] for n in SHARED_INPUTS}
    per_example = {n: given[n] for n in ['x']}
    grad_fn = _jax.value_and_grad(_loss, argnums=(0, 1))

    def one_microbatch(ex, loss_target):
        ex = dict(ex)
        diff = ex.pop(TWIN_DIFF_INPUT)
        return grad_fn(weights, diff, {**shared, **ex}, loss_target)

    if N_MICROBATCH == 1:
        loss, (grad_w, grad_x) = one_microbatch(per_example, given["loss_target"])
    else:
        def body(carry, xs):
            loss_sum, grad_sum = carry
            l_k, (gw_k, gx_k) = one_microbatch(xs[0], xs[1])
            with _jax.named_scope("update"):
                return (loss_sum + l_k, _jax.tree.map(_jnp.add, grad_sum, gw_k)), gx_k

        init = (_jnp.zeros((), _jnp.float32), _jax.tree.map(_jnp.zeros_like, weights))
        (loss, grad_w), grad_x = _jax.lax.scan(body, init, (per_example, given["loss_target"]))
    with _jax.named_scope("update"):
        delta_w, new_m, new_v = {}, {}, {}
        for n in TWIN_WEIGHTS:
            delta_w[n], new_m[n], new_v[n] = _adamw(weights[n], grad_w[n], given["m_" + n], given["v_" + n])
    return (loss, grad_x, *[grad_w[n] for n in TWIN_WEIGHTS], *[delta_w[n] for n in TWIN_WEIGHTS],
            *[new_m[n] for n in TWIN_WEIGHTS], *[new_v[n] for n in TWIN_WEIGHTS])
```

```python
import functools

import jax
import jax.numpy as jnp
from jax import lax
from jax.experimental import pallas as pl
from jax.experimental.pallas import tpu as pltpu

F32 = jnp.float32
BF16 = jnp.bfloat16
PAYLOAD = jnp.bfloat16
HIGHEST = lax.Precision.HIGHEST
MESH = pl.DeviceIdType.MESH

NORM_EPS = 1e-6
SSD_HEAD_DIM = 64
SSD_N_GROUPS = 8
SSD_HPG = 4
SSD_D_STATE = 128
SSD_CONV_WIDTH = 4
SSD_CHUNK = 128
ATTN_HEAD_DIM = 64
ATTN_N_KV = 4
ATTN_REP = 4
ATTN_WINDOW = 128
ADAM_LR = 0.001
ADAM_B1 = 0.9
ADAM_B2 = 0.999
ADAM_EPS = 1e-08
ADAM_WD = 0.01
ADAM_STEP = 10

N_DEV = 8
LANES = 128
PACK_COLS = 1024
V7X_VMEM_LIMIT = 56 * 1024 * 1024

GW = SSD_HPG * SSD_HEAD_DIM
GC = GW + 2 * SSD_D_STATE


def _params(*sem):
    return pltpu.CompilerParams(dimension_semantics=sem, vmem_limit_bytes=V7X_VMEM_LIMIT)


def _tile(n, pref, mult=LANES):
    best = None
    t = mult
    while t <= min(n, pref):
        if n % t == 0:
            best = t
        t += mult
    return best if best is not None else n


def _round_up(n, m):
    return (n + m - 1) // m * m


def _acc(ref, val, first):
    @pl.when(first)
    def _():
        ref[...] = val

    @pl.when(jnp.logical_not(first))
    def _():
        ref[...] += val


def _dot(a, b):
    return lax.dot_general(a, b, (((1,), (0,)), ((), ())), preferred_element_type=F32)


def _dot_nt(a, b):
    return lax.dot_general(a, b, (((1,), (1,)), ((), ())), preferred_element_type=F32)


def _dot_tn(a, b):
    return lax.dot_general(a, b, (((0,), (0,)), ((), ())), preferred_element_type=F32)


def _dot_f32(a, b):
    return lax.dot_general(a, b, (((1,), (0,)), ((), ())), preferred_element_type=F32, precision=HIGHEST)


_DOTS = {"nn": _dot, "nt": _dot_nt, "tn": _dot_tn}


def _sigmoid(x):
    return 1.0 / (1.0 + jnp.exp(-x))


def _softplus(x):
    return jnp.maximum(x, 0.0) + jnp.log1p(jnp.exp(-jnp.abs(x)))


def _silu_grad(x, s):
    return s * (1.0 + x * (1.0 - s))


def _mm(name, a_list, b_list, mode, *, tm, tn, out_dtypes=(F32,), epilogue=None, tiles=(), rows=()):
    npair = len(a_list)
    if mode == "tn":
        m = a_list[0].shape[1]
    else:
        m = a_list[0].shape[0]
    n = b_list[0].shape[0] if mode == "nt" else b_list[0].shape[1]
    tm = _tile(m, tm, LANES if mode == "tn" else 8)
    tn = _tile(n, tn)
    assert m % tm == 0 and n % tn == 0, (name, m, n, tm, tn)
    dot = _DOTS[mode]

    def body(*refs):
        a_refs = refs[:npair]
        b_refs = refs[npair:2 * npair]
        e_refs = refs[2 * npair:2 * npair + len(tiles) + len(rows)]
        o_refs = refs[2 * npair + len(tiles) + len(rows):]
        acc = None
        for ar, br in zip(a_refs, b_refs):
            d = dot(ar[...], br[...])
            acc = d if acc is None else acc + d
        outs = epilogue(acc, *[e[...] for e in e_refs]) if epilogue is not None else (acc,)
        for o, v in zip(o_refs, outs):
            o[...] = v.astype(o.dtype)

    in_specs = []
    for a in a_list:
        if mode == "tn":
            in_specs.append(pl.BlockSpec((a.shape[0], tm), lambda i, j: (0, i)))
        else:
            in_specs.append(pl.BlockSpec((tm, a.shape[1]), lambda i, j: (i, 0)))
    for b in b_list:
        if mode == "nt":
            in_specs.append(pl.BlockSpec((tn, b.shape[1]), lambda i, j: (j, 0)))
        else:
            in_specs.append(pl.BlockSpec((b.shape[0], tn), lambda i, j: (0, j)))
    in_specs += [pl.BlockSpec((tm, tn), lambda i, j: (i, j)) for _ in tiles]
    in_specs += [pl.BlockSpec((1, tn), lambda i, j: (0, j)) for _ in rows]
    outs = pl.pallas_call(
        body,
        name=name,
        grid=(m // tm, n // tn),
        in_specs=in_specs,
        out_specs=[pl.BlockSpec((tm, tn), lambda i, j: (i, j)) for _ in out_dtypes],
        out_shape=[jax.ShapeDtypeStruct((m, n), dt) for dt in out_dtypes],
        compiler_params=_params("parallel", "parallel"),
    )(*a_list, *b_list, *tiles, *rows)
    return outs[0] if len(out_dtypes) == 1 else outs


def _rms(x, w):
    r = lax.rsqrt(jnp.mean(x * x, axis=-1, keepdims=True) + NORM_EPS)
    return x * r * w


def _rms_bwd(x, w, dy):
    r = lax.rsqrt(jnp.mean(x * x, axis=-1, keepdims=True) + NORM_EPS)
    xh = x * r
    g = dy * w
    dx = r * (g - xh * jnp.mean(g * xh, axis=-1, keepdims=True))
    return dx, dy * xh


def _row_specs(tr, d):
    return pl.BlockSpec((tr, d), lambda i: (i, 0)), pl.BlockSpec((1, d), lambda i: (0, 0))


def _prenorm(name, h, w):
    t, d = h.shape
    tr = _tile(t, 512, 8)
    row, vec = _row_specs(tr, d)

    def body(h_ref, w_ref, u_ref):
        u_ref[...] = _rms(h_ref[...], w_ref[...]).astype(BF16)

    return pl.pallas_call(body, name=name, grid=(t // tr,), in_specs=[row, vec], out_specs=row,
                          out_shape=jax.ShapeDtypeStruct((t, d), BF16), compiler_params=_params("parallel"))(h, w)


def _post_pre(name, h, m, w_post, w_pre):
    t, d = h.shape
    tr = _tile(t, 512, 8)
    row, vec = _row_specs(tr, d)

    def body(h_ref, m_ref, wq_ref, wp_ref, hn_ref, u_ref):
        hn = h_ref[...] + _rms(m_ref[...], wq_ref[...])
        hn_ref[...] = hn
        u_ref[...] = _rms(hn, wp_ref[...]).astype(BF16)

    return pl.pallas_call(body, name=name, grid=(t // tr,), in_specs=[row, row, vec, vec], out_specs=[row, row],
                          out_shape=[jax.ShapeDtypeStruct((t, d), F32), jax.ShapeDtypeStruct((t, d), BF16)],
                          compiler_params=_params("parallel"))(h, m, w_post, w_pre)


def _final_loss(name, h, m, w_post, target):
    t, d = h.shape
    tr = _tile(t, 512, 8)
    row, vec = _row_specs(tr, d)

    def body(h_ref, m_ref, wq_ref, t_ref, dh_ref, loss_ref):
        err = h_ref[...] + _rms(m_ref[...], wq_ref[...]) - t_ref[...]
        dh_ref[...] = err * (1.0 / d)
        part = 0.5 * jnp.sum(jnp.mean(err * err, axis=-1, keepdims=True), axis=0, keepdims=True)
        _acc(loss_ref, jnp.broadcast_to(part, (1, LANES)), pl.program_id(0) == 0)

    return pl.pallas_call(body, name=name, grid=(t // tr,), in_specs=[row, row, vec, row],
                          out_specs=[row, pl.BlockSpec((1, LANES), lambda i: (0, 0))],
                          out_shape=[jax.ShapeDtypeStruct((t, d), F32), jax.ShapeDtypeStruct((1, LANES), F32)],
                          compiler_params=_params("arbitrary"))(h, m, w_post, target)


def _norm_bwd(name, dh, pre=None, post=None):
    t, d = dh.shape
    tr = _tile(t, 256, 8)
    row, vec = _row_specs(tr, d)
    has_pre, has_post = pre is not None, post is not None

    def body(*refs):
        it = iter(refs)
        dh_ref = next(it)
        if has_pre:
            du_ref, x_ref, wp_ref = next(it), next(it), next(it)
        if has_post:
            m_ref, wq_ref = next(it), next(it)
        first = pl.program_id(0) == 0
        dh_v = dh_ref[...]
        if has_pre:
            dhn_ref, dwp_ref = next(it), next(it)
            dx, dwr = _rms_bwd(x_ref[...], wp_ref[...], du_ref[...])
            dh_v = dh_v + dx
            dhn_ref[...] = dh_v
            _acc(dwp_ref, jnp.sum(dwr, axis=0, keepdims=True), first)
        if has_post:
            dm_ref, dwq_ref, dms_ref = next(it), next(it), next(it)
            dm, dwr = _rms_bwd(m_ref[...], wq_ref[...], dh_v)
            dm_ref[...] = dm.astype(BF16)
            _acc(dwq_ref, jnp.sum(dwr, axis=0, keepdims=True), first)
            _acc(dms_ref, jnp.sum(dm, axis=0, keepdims=True), first)

    ins, in_specs, out_specs, out_shape = [dh], [row], [], []
    if has_pre:
        ins += list(pre)
        in_specs += [row, row, vec]
        out_specs += [row, vec]
        out_shape += [jax.ShapeDtypeStruct((t, d), F32), jax.ShapeDtypeStruct((1, d), F32)]
    if has_post:
        ins += list(post)
        in_specs += [row, vec]
        out_specs += [row, vec, vec]
        out_shape += [jax.ShapeDtypeStruct((t, d), BF16), jax.ShapeDtypeStruct((1, d), F32),
                      jax.ShapeDtypeStruct((1, d), F32)]
    return pl.pallas_call(body, name=name, grid=(t // tr,), in_specs=in_specs, out_specs=out_specs,
                          out_shape=out_shape, compiler_params=_params("arbitrary"))(*ins)


HALO = 8


def _conv_fwd(zx, col0, n_ch, conv_w, conv_b):
    t = zx.shape[0]
    tc = _tile(n_ch, 512)
    tt = _tile(t, 512, 8)
    cb0 = col0 // tc
    assert col0 % tc == 0
    kw = SSD_CONV_WIDTH

    def body(x_ref, p_ref, w_ref, b_ref, o_ref, xe_ref):
        i = pl.program_id(1)
        cur = x_ref[...]
        xe_ref[0:HALO, :] = jnp.where(i > 0, p_ref[...], 0.0)
        xe_ref[HALO:HALO + tt, :] = cur
        w = w_ref[...]
        acc = b_ref[...] + w[kw - 1:kw, :] * cur
        for k in range(kw - 1):
            acc = acc + w[k:k + 1, :] * xe_ref[pl.ds(HALO - (kw - 1) + k, tt), :]
        o_ref[...] = acc

    return pl.pallas_call(
        body, name="ssd_conv_fwd", grid=(n_ch // tc, t // tt),
        in_specs=[pl.BlockSpec((tt, tc), lambda j, i: (i, cb0 + j)),
                  pl.BlockSpec((HALO, tc), lambda j, i: (jnp.maximum(i * (tt // HALO) - 1, 0), cb0 + j)),
                  pl.BlockSpec((kw, tc), lambda j, i: (0, j)),
                  pl.BlockSpec((1, tc), lambda j, i: (0, j))],
        out_specs=pl.BlockSpec((tt, tc), lambda j, i: (i, j)),
        out_shape=jax.ShapeDtypeStruct((t, n_ch), F32),
        scratch_shapes=[pltpu.VMEM((tt + HALO, tc), F32)],
        compiler_params=_params("parallel", "parallel"))(zx, zx, conv_w, conv_b)


def _conv_bwd(dpre, zx, col0, conv_w):
    t, n_ch = dpre.shape
    tc = _tile(n_ch, 512)
    tt = _tile(t, 512, 8)
    cb0 = col0 // tc
    kw = SSD_CONV_WIDTH
    nt = t // tt

    def body(d_ref, dn_ref, x_ref, p_ref, w_ref, dx_ref, dw_ref, db_ref, de_ref, xe_ref):
        i = pl.program_id(1)
        d = d_ref[...]
        de_ref[0:tt, :] = d
        de_ref[tt:tt + HALO, :] = jnp.where(i < nt - 1, dn_ref[...], 0.0)
        xe_ref[0:HALO, :] = jnp.where(i > 0, p_ref[...], 0.0)
        xe_ref[HALO:HALO + tt, :] = x_ref[...]
        w = w_ref[...]
        dx = w[kw - 1:kw, :] * d
        for k in range(kw - 1):
            dx = dx + w[k:k + 1, :] * de_ref[pl.ds(kw - 1 - k, tt), :]
        dx_ref[...] = dx.astype(BF16)
        first = i == 0
        for k in range(kw):
            xs = xe_ref[pl.ds(HALO - (kw - 1) + k, tt), :]
            val = jnp.sum(d * xs, axis=0, keepdims=True)

            @pl.when(first)
            def _():
                dw_ref[k:k + 1, :] = val

            @pl.when(jnp.logical_not(first))
            def _():
                dw_ref[k:k + 1, :] += val
        _acc(db_ref, jnp.sum(d, axis=0, keepdims=True), first)

    return pl.pallas_call(
        body, name="ssd_conv_bwd", grid=(n_ch // tc, nt),
        in_specs=[pl.BlockSpec((tt, tc), lambda j, i: (i, j)),
                  pl.BlockSpec((HALO, tc), lambda j, i: (jnp.minimum((i + 1) * (tt // HALO), t // HALO - 1), j)),
                  pl.BlockSpec((tt, tc), lambda j, i: (i, cb0 + j)),
                  pl.BlockSpec((HALO, tc), lambda j, i: (jnp.maximum(i * (tt // HALO) - 1, 0), cb0 + j)),
                  pl.BlockSpec((kw, tc), lambda j, i: (0, j))],
        out_specs=[pl.BlockSpec((tt, tc), lambda j, i: (i, j)),
                   pl.BlockSpec((kw, tc), lambda j, i: (0, j)),
                   pl.BlockSpec((1, tc), lambda j, i: (0, j))],
        out_shape=[jax.ShapeDtypeStruct((t, n_ch), BF16), jax.ShapeDtypeStruct((kw, n_ch), F32),
                   jax.ShapeDtypeStruct((1, n_ch), F32)],
        scratch_shapes=[pltpu.VMEM((tt + HALO, tc), F32), pltpu.VMEM((tt + HALO, tc), F32)],
        compiler_params=_params("parallel", "arbitrary"))(dpre, dpre, zx, zx, conv_w)


def _head_of_lane(shape, width):
    return lax.broadcasted_iota(jnp.int32, shape, len(shape) - 1) // width


def _expand(v, n_rows):
    head = _head_of_lane((n_rows, GW), SSD_HEAD_DIM)
    out = jnp.zeros((n_rows, GW), F32)
    for j in range(SSD_HPG):
        out = jnp.where(head == j, v[:, j:j + 1], out)
    return out


def _contract(v, n_rows):
    head = _head_of_lane((n_rows, GW), SSD_HEAD_DIM)
    lane = lax.broadcasted_iota(jnp.int32, (n_rows, LANES), 1)
    out = jnp.zeros((n_rows, LANES), F32)
    for j in range(SSD_HPG):
        s = jnp.sum(jnp.where(head == j, v, 0.0), axis=1, keepdims=True)
        out = jnp.where(lane == j, s, out)
    return out


def _ssd_common(pre, dtc, bias_c, alog_c, dtr, bias_r, alog_r):
    q = SSD_CHUNK
    sg = _sigmoid(pre)
    act = pre * sg
    xa = act[:, :GW]
    bm = act[:, GW:GW + SSD_D_STATE].astype(BF16)
    cm = act[:, GW + SSD_D_STATE:].astype(BF16)
    row = lax.broadcasted_iota(jnp.int32, (q, q), 0)
    col = lax.broadcasted_iota(jnp.int32, (q, q), 1)
    tril = col <= row
    dt = _softplus(dtc + bias_c)
    a_c = -jnp.exp(alog_c)
    cum = _dot_f32(tril.astype(F32), dt * a_c)
    dt_r = _softplus(dtr + bias_r)
    cum_r = _dot_f32(dt_r * (-jnp.exp(alog_r)), (row <= col).astype(F32))
    g = _dot_nt(cm, bm)
    dt_x = _expand(dt, q)
    xdt = xa * dt_x
    cl = cum[q - 1:q, :]
    e_c = jnp.exp(cl - cum)
    lam_c = jnp.exp(cum)
    return dict(sg=sg, xa=xa, bm=bm, cm=cm, tril=tril, row=row, col=col, dt=dt, a_c=a_c, cum=cum, cum_r=cum_r,
                g=g, dt_x=dt_x, xdt=xdt, cl=cl, e_c=e_c, lam_c=lam_c)


def _ssd_specs(nc, rev):
    q = SSD_CHUNK

    def ch(c):
        return nc - 1 - c if rev else c

    chunk_grp = pl.BlockSpec((q, GC), lambda g, c: (ch(c), g))
    col_form = pl.BlockSpec((None, q, LANES), lambda g, c: (g, ch(c), 0))
    row_form = pl.BlockSpec((None, 8, q), lambda g, c: (g, 0, ch(c)))
    col_par = pl.BlockSpec((None, 1, LANES), lambda g, c: (g, 0, 0))
    row_par = pl.BlockSpec((None, 8, 1), lambda g, c: (g, 0, 0))
    y_spec = pl.BlockSpec((q, GW), lambda g, c: (ch(c), g))
    st_spec = pl.BlockSpec((None, None, GW, SSD_D_STATE), lambda g, c: (g, ch(c), 0, 0))
    return chunk_grp, col_form, row_form, col_par, row_par, y_spec, st_spec


def _ssd_fwd(pre, dtc, dtr, bias_c, alog_c, dsk_c, bias_r, alog_r):
    t = pre.shape[0]
    ng = pre.shape[1] // GC
    q = SSD_CHUNK
    nc = t // q
    chunk_grp, col_form, row_form, col_par, row_par, y_spec, st_spec = _ssd_specs(nc, False)

    def body(pre_ref, dtc_ref, dtr_ref, bc_ref, ac_ref, dk_ref, br_ref, ar_ref, y_ref, sp_ref, st_ref):
        @pl.when(pl.program_id(1) == 0)
        def _():
            st_ref[...] = jnp.zeros_like(st_ref)

        v = _ssd_common(pre_ref[...], dtc_ref[...], bc_ref[...], ac_ref[...], dtr_ref[...], br_ref[...], ar_ref[...])
        s0 = st_ref[...]
        sp_ref[...] = s0
        r = _dot_nt(v["cm"], s0.astype(BF16))
        y = _expand(v["lam_c"], q) * r + _expand(dk_ref[...], 1) * v["xa"]
        head = _head_of_lane((q, GW), SSD_HEAD_DIM)
        for j in range(SSD_HPG):
            diff = v["cum"][:, j:j + 1] - v["cum_r"][j:j + 1, :]
            w = (v["g"] * jnp.exp(jnp.where(v["tril"], diff, -jnp.inf))).astype(BF16)
            y = y + _dot(w, jnp.where(head == j, v["xdt"], 0.0).astype(BF16))
        y_ref[...] = y
        ds = _dot_tn((v["xdt"] * _expand(v["e_c"], q)).astype(BF16), v["bm"])
        for j in range(SSD_HPG):
            rows = slice(j * SSD_HEAD_DIM, (j + 1) * SSD_HEAD_DIM)
            st_ref[rows, :] = s0[rows, :] * jnp.exp(v["cum_r"][j:j + 1, q - 1:q]) + ds[rows, :]

    return pl.pallas_call(
        body, name="ssd_scan_fwd", grid=(ng, nc),
        in_specs=[chunk_grp, col_form, row_form, col_par, col_par, col_par, row_par, row_par],
        out_specs=[y_spec, st_spec],
        out_shape=[jax.ShapeDtypeStruct((t, ng * GW), F32), jax.ShapeDtypeStruct((ng, nc, GW, SSD_D_STATE), F32)],
        scratch_shapes=[pltpu.VMEM((GW, SSD_D_STATE), F32)],
        compiler_params=_params("parallel", "arbitrary"))(pre, dtc, dtr, bias_c, alog_c, dsk_c, bias_r, alog_r)


def _ssd_bwd(dy, pre, states, dtc, dtr, bias_c, alog_c, dsk_c, bias_r, alog_r):
    t = pre.shape[0]
    ng = pre.shape[1] // GC
    q = SSD_CHUNK
    nc = t // q
    chunk_grp, col_form, row_form, col_par, row_par, y_spec, st_spec = _ssd_specs(nc, True)

    def body(dy_ref, pre_ref, sp_ref, dtc_ref, dtr_ref, bc_ref, ac_ref, dk_ref, br_ref, ar_ref,
             dpre_ref, ddt_ref, dbias_ref, dalog_ref, dd_ref, ds_ref):
        first = pl.program_id(1) == 0

        @pl.when(first)
        def _():
            ds_ref[...] = jnp.zeros_like(ds_ref)

        pre_v = pre_ref[...]
        v = _ssd_common(pre_v, dtc_ref[...], bc_ref[...], ac_ref[...], dtr_ref[...], br_ref[...], ar_ref[...])
        xa, bm, cm, xdt, cum, cum_r = v["xa"], v["bm"], v["cm"], v["xdt"], v["cum"], v["cum_r"]
        xdt_b = xdt.astype(BF16)
        dy_v = dy_ref[...]
        s0 = sp_ref[...]
        ds1 = ds_ref[...]
        s0b, ds1b = s0.astype(BF16), ds1.astype(BF16)
        head = _head_of_lane((q, GW), SSD_HEAD_DIM)
        lane = lax.broadcasted_iota(jnp.int32, (q, LANES), 1)
        lane1 = lax.broadcasted_iota(jnp.int32, (1, LANES), 1)
        lam_x = _expand(v["lam_c"], q)
        e_x = _expand(v["e_c"], q)

        dxa = _expand(dk_ref[...], 1) * dy_v
        dd = _contract(jnp.sum(dy_v * xa, axis=0, keepdims=True), 1)
        r = _dot_nt(cm, s0b)
        dcum = _contract(dy_v * r * lam_x, q)
        drb = (lam_x * dy_v).astype(BF16)
        dc = _dot(drb, s0b)
        ds0 = _dot_tn(drb, cm)
        extra = jnp.zeros((1, LANES), F32)
        for j in range(SSD_HPG):
            rows = slice(j * SSD_HEAD_DIM, (j + 1) * SSD_HEAD_DIM)
            lam_last = jnp.exp(cum_r[j:j + 1, q - 1:q])
            ds_ref[rows, :] = ds0[rows, :] + lam_last * ds1[rows, :]
            tot = jnp.sum(jnp.sum(ds1[rows, :] * s0[rows, :], axis=1, keepdims=True), axis=0, keepdims=True)
            extra = jnp.where(lane1 == j, lam_last * tot, extra)
        dv = _dot_nt(bm, ds1b)
        db = _dot((xdt * e_x).astype(BF16), ds1b)
        dxdt = e_x * dv
        dee = _contract(dv * xdt, q) * v["e_c"]
        dcum = dcum - dee
        extra = extra + jnp.sum(dee, axis=0, keepdims=True)
        dg = jnp.zeros((q, q), F32)
        for j in range(SSD_HPG):
            diff = cum[:, j:j + 1] - cum_r[j:j + 1, :]
            el = jnp.exp(jnp.where(v["tril"], diff, -jnp.inf))
            gl = v["g"] * el
            dym = jnp.where(head == j, dy_v, 0.0).astype(BF16)
            dwm = _dot_nt(dym, xdt_b)
            dxdt = dxdt + _dot_tn(gl.astype(BF16), dym)
            z = dwm * gl
            rk = jnp.sum(z, axis=1, keepdims=True) - jnp.sum(z.T, axis=1, keepdims=True)
            dcum = jnp.where(lane == j, dcum + rk, dcum)
            dg = dg + dwm * el
        dgb = dg.astype(BF16)
        dc = dc + _dot(dgb, bm)
        db = db + _dot_tn(dgb, cm)
        da = _dot_f32((v["row"] <= v["col"]).astype(F32), dcum) + extra
        ddt = _contract(dxdt * xa, q) + v["a_c"] * da
        dalog = jnp.sum(v["dt"] * da, axis=0, keepdims=True) * v["a_c"]
        dxa = dxa + v["dt_x"] * dxdt
        ddt_raw = jnp.where(lane < SSD_HPG, ddt * _sigmoid(dtc_ref[...] + bc_ref[...]), 0.0)
        sgrad = _silu_grad(pre_v, v["sg"])
        dpre_ref[:, :GW] = dxa * sgrad[:, :GW]
        dpre_ref[:, GW:GW + SSD_D_STATE] = db * sgrad[:, GW:GW + SSD_D_STATE]
        dpre_ref[:, GW + SSD_D_STATE:] = dc * sgrad[:, GW + SSD_D_STATE:]
        ddt_ref[...] = ddt_raw
        _acc(dbias_ref, jnp.sum(ddt_raw, axis=0, keepdims=True), first)
        _acc(dalog_ref, jnp.where(lane1 < SSD_HPG, dalog, 0.0), first)
        _acc(dd_ref, dd, first)

    return pl.pallas_call(
        body, name="ssd_scan_bwd", grid=(ng, nc),
        in_specs=[y_spec, chunk_grp, st_spec, col_form, row_form, col_par, col_par, col_par, row_par, row_par],
        out_specs=[chunk_grp, col_form, col_par, col_par, col_par],
        out_shape=[jax.ShapeDtypeStruct((t, ng * GC), F32), jax.ShapeDtypeStruct((ng, t, LANES), F32),
                   jax.ShapeDtypeStruct((ng, 1, LANES), F32), jax.ShapeDtypeStruct((ng, 1, LANES), F32),
                   jax.ShapeDtypeStruct((ng, 1, LANES), F32)],
        scratch_shapes=[pltpu.VMEM((GW, SSD_D_STATE), F32)],
        compiler_params=_params("parallel", "arbitrary"))(dy, pre, states, dtc, dtr, bias_c, alog_c, dsk_c,
                                                           bias_r, alog_r)


def _gate_norm_fwd(y, zx, norm_w):
    t, di = y.shape
    tr = _tile(t, 256, 8)
    ng = di // GW

    def body(y_ref, z_ref, w_ref, o_ref):
        z = z_ref[...]
        gate = y_ref[...] * (z * _sigmoid(z))
        w = w_ref[...]
        for g in range(ng):
            cols = slice(g * GW, (g + 1) * GW)
            gs = gate[:, cols]
            r = lax.rsqrt(jnp.mean(gs * gs, axis=-1, keepdims=True) + NORM_EPS)
            o_ref[:, cols] = (gs * r * w[:, cols]).astype(BF16)

    row = pl.BlockSpec((tr, di), lambda i: (i, 0))
    return pl.pallas_call(body, name="ssd_gate_norm_fwd", grid=(t // tr,),
                          in_specs=[row, row, pl.BlockSpec((1, di), lambda i: (0, 0))], out_specs=row,
                          out_shape=jax.ShapeDtypeStruct((t, di), BF16), compiler_params=_params("parallel"))(
                              y, zx, norm_w)


def _gate_norm_bwd(dyn, y, zx, norm_w):
    t, di = y.shape
    tr = _tile(t, 256, 8)
    ng = di // GW

    def body(d_ref, y_ref, z_ref, w_ref, dy_ref, dz_ref, dw_ref):
        z = z_ref[...]
        yv = y_ref[...]
        sg = _sigmoid(z)
        sz = z * sg
        gate = yv * sz
        w = w_ref[...]
        d = d_ref[...]
        dsz = _silu_grad(z, sg)
        dws = []
        for g in range(ng):
            cols = slice(g * GW, (g + 1) * GW)
            dg, dwr = _rms_bwd(gate[:, cols], w[:, cols], d[:, cols])
            dy_ref[:, cols] = dg * sz[:, cols]
            dz_ref[:, cols] = (dg * yv[:, cols] * dsz[:, cols]).astype(BF16)
            dws.append(jnp.sum(dwr, axis=0, keepdims=True))
        first = pl.program_id(0) == 0
        for g in range(ng):
            cols = slice(g * GW, (g + 1) * GW)

            @pl.when(first)
            def _():
                dw_ref[:, cols] = dws[g]

            @pl.when(jnp.logical_not(first))
            def _():
                dw_ref[:, cols] += dws[g]

    row = pl.BlockSpec((tr, di), lambda i: (i, 0))
    vec = pl.BlockSpec((1, di), lambda i: (0, 0))
    return pl.pallas_call(body, name="ssd_gate_norm_bwd", grid=(t // tr,), in_specs=[row, row, row, vec],
                          out_specs=[row, row, vec],
                          out_shape=[jax.ShapeDtypeStruct((t, di), F32), jax.ShapeDtypeStruct((t, di), BF16),
                                     jax.ShapeDtypeStruct((1, di), F32)],
                          compiler_params=_params("arbitrary"))(dyn, y, zx, norm_w)


def _attn_mask(n):
    w = ATTN_WINDOW
    qpos = lax.broadcasted_iota(jnp.int32, (w, 2 * w), 0) + w
    kpos = lax.broadcasted_iota(jnp.int32, (w, 2 * w), 1)
    rel = qpos - kpos
    return (rel >= 0) & (rel < w) & jnp.logical_not((n == 0) & (kpos < w))


def _attn_probs(qh, kbh, mask, sink):
    s = _dot_nt(qh, kbh) * (ATTN_HEAD_DIM ** -0.5)
    s = jnp.where(mask, s, -jnp.inf)
    m = jnp.maximum(jnp.max(s, axis=-1, keepdims=True), sink)
    e = jnp.exp(s - m)
    es = jnp.exp(sink - m)
    inv = 1.0 / (jnp.sum(e, axis=-1, keepdims=True) + es)
    return e * inv, es * inv


def _attn_fwd(qkv, sinks):
    t = qkv.shape[0]
    w, hd = ATTN_WINDOW, ATTN_HEAD_DIM
    kd = ATTN_N_KV * hd
    qd = ATTN_REP * kd
    nb = t // w

    def body(q_ref, kc_ref, vc_ref, kp_ref, vp_ref, s_ref, o_ref):
        n = pl.program_id(0)
        mask = _attn_mask(n)
        q = q_ref[...]
        kb = jnp.concatenate([kp_ref[...], kc_ref[...]], axis=0)
        vb = jnp.concatenate([vp_ref[...], vc_ref[...]], axis=0)
        sk = s_ref[...]
        for kv in range(ATTN_N_KV):
            kbh = kb[:, kv * hd:(kv + 1) * hd]
            vbh = vb[:, kv * hd:(kv + 1) * hd]
            for rep in range(ATTN_REP):
                h = kv * ATTN_REP + rep
                p, _ = _attn_probs(q[:, h * hd:(h + 1) * hd], kbh, mask, sk[:, h:h + 1])
                o_ref[:, h * hd:(h + 1) * hd] = _dot(p.astype(BF16), vbh).astype(BF16)

    prev = lambda n: jnp.maximum(n - 1, 0)
    return pl.pallas_call(
        body, name="attn_fwd", grid=(nb,),
        in_specs=[pl.BlockSpec((w, qd), lambda n: (n, 0)),
                  pl.BlockSpec((w, kd), lambda n: (n, ATTN_REP)),
                  pl.BlockSpec((w, kd), lambda n: (n, ATTN_REP + 1)),
                  pl.BlockSpec((w, kd), lambda n: (prev(n), ATTN_REP)),
                  pl.BlockSpec((w, kd), lambda n: (prev(n), ATTN_REP + 1)),
                  pl.BlockSpec((1, sinks.shape[1]), lambda n: (0, 0))],
        out_specs=pl.BlockSpec((w, qd), lambda n: (n, 0)),
        out_shape=jax.ShapeDtypeStruct((t, qd), BF16),
        compiler_params=_params("parallel"))(qkv, qkv, qkv, qkv, qkv, sinks)


def _attn_bwd(qkv, do, sinks):
    t = qkv.shape[0]
    w, hd = ATTN_WINDOW, ATTN_HEAD_DIM
    kd = ATTN_N_KV * hd
    qd = ATTN_REP * kd
    nq = ATTN_N_KV * ATTN_REP
    nb = t // w

    def body(q_ref, kc_ref, vc_ref, kp_ref, vp_ref, do_ref, s_ref,
             dq_ref, dk_ref, dv_ref, bq_ref, bk_ref, bv_ref, dsk_ref, ck_ref, cv_ref):
        n = pl.program_id(0)
        first = n == 0

        @pl.when(first)
        def _():
            ck_ref[...] = jnp.zeros_like(ck_ref)
            cv_ref[...] = jnp.zeros_like(cv_ref)
            bq_ref[...] = jnp.zeros_like(bq_ref)
            bk_ref[...] = jnp.zeros_like(bk_ref)
            bv_ref[...] = jnp.zeros_like(bv_ref)
            dsk_ref[...] = jnp.zeros_like(dsk_ref)

        @pl.when(n < nb)
        def _():
            mask = _attn_mask(n)
            q = q_ref[...]
            dov = do_ref[...]
            kb = jnp.concatenate([kp_ref[...], kc_ref[...]], axis=0)
            vb = jnp.concatenate([vp_ref[...], vc_ref[...]], axis=0)
            sk = s_ref[...]
            lane = lax.broadcasted_iota(jnp.int32, (1, nq), 1)
            dsk = jnp.zeros((1, nq), F32)
            dq_parts, dk_parts, dv_parts = [], [], []
            for kv in range(ATTN_N_KV):
                kbh = kb[:, kv * hd:(kv + 1) * hd]
                vbh = vb[:, kv * hd:(kv + 1) * hd]
                dkh = jnp.zeros((2 * w, hd), F32)
                dvh = jnp.zeros((2 * w, hd), F32)
                for rep in range(ATTN_REP):
                    h = kv * ATTN_REP + rep
                    qh = q[:, h * hd:(h + 1) * hd]
                    doh = dov[:, h * hd:(h + 1) * hd]
                    p, ps = _attn_probs(qh, kbh, mask, sk[:, h:h + 1])
                    pb = p.astype(BF16)
                    dp = _dot_nt(doh, vbh)
                    delta = jnp.sum(p * dp, axis=-1, keepdims=True)
                    dsc = (p * (dp - delta) * (hd ** -0.5)).astype(BF16)
                    dq_parts.append(_dot(dsc, kbh))
                    dkh = dkh + _dot_tn(dsc, qh)
                    dvh = dvh + _dot_tn(pb, doh)
                    dsk = jnp.where(lane == h, -jnp.sum(ps * delta, axis=0, keepdims=True), dsk)
                dk_parts.append(dkh)
                dv_parts.append(dvh)
            dq = jnp.concatenate(dq_parts, axis=1)
            dkb = jnp.concatenate(dk_parts, axis=1)
            dvb = jnp.concatenate(dv_parts, axis=1)
            dq_ref[...] = dq.astype(BF16)
            bq_ref[...] += jnp.sum(dq, axis=0, keepdims=True)
            dsk_ref[...] += dsk
            dk_prev = ck_ref[...] + dkb[:w, :]
            dv_prev = cv_ref[...] + dvb[:w, :]
            dk_ref[...] = dk_prev.astype(BF16)
            dv_ref[...] = dv_prev.astype(BF16)

            @pl.when(n > 0)
            def _():
                bk_ref[...] += jnp.sum(dk_prev, axis=0, keepdims=True)
                bv_ref[...] += jnp.sum(dv_prev, axis=0, keepdims=True)

            ck_ref[...] = dkb[w:, :]
            cv_ref[...] = dvb[w:, :]

        @pl.when(n == nb)
        def _():
            dk_ref[...] = ck_ref[...].astype(BF16)
            dv_ref[...] = cv_ref[...].astype(BF16)
            bk_ref[...] += jnp.sum(ck_ref[...], axis=0, keepdims=True)
            bv_ref[...] += jnp.sum(cv_ref[...], axis=0, keepdims=True)

    cur = lambda n: jnp.minimum(n, nb - 1)
    prev = lambda n: jnp.maximum(jnp.minimum(n, nb - 1) - 1, 0)
    late = lambda n: jnp.maximum(n - 1, 0)
    vec = lambda width: pl.BlockSpec((1, width), lambda n: (0, 0))
    return pl.pallas_call(
        body, name="attn_bwd", grid=(nb + 1,),
        in_specs=[pl.BlockSpec((w, qd), lambda n: (cur(n), 0)),
                  pl.BlockSpec((w, kd), lambda n: (cur(n), ATTN_REP)),
                  pl.BlockSpec((w, kd), lambda n: (cur(n), ATTN_REP + 1)),
                  pl.BlockSpec((w, kd), lambda n: (prev(n), ATTN_REP)),
                  pl.BlockSpec((w, kd), lambda n: (prev(n), ATTN_REP + 1)),
                  pl.BlockSpec((w, qd), lambda n: (cur(n), 0)),
                  vec(nq)],
        out_specs=[pl.BlockSpec((w, qd), lambda n: (cur(n), 0)),
                   pl.BlockSpec((w, kd), lambda n: (late(n), 0)),
                   pl.BlockSpec((w, kd), lambda n: (late(n), 0)),
                   vec(qd), vec(kd), vec(kd), vec(nq)],
        out_shape=[jax.ShapeDtypeStruct((t, qd), BF16), jax.ShapeDtypeStruct((t, kd), BF16),
                   jax.ShapeDtypeStruct((t, kd), BF16), jax.ShapeDtypeStruct((1, qd), F32),
                   jax.ShapeDtypeStruct((1, kd), F32), jax.ShapeDtypeStruct((1, kd), F32),
                   jax.ShapeDtypeStruct((1, nq), F32)],
        scratch_shapes=[pltpu.VMEM((w, kd), F32), pltpu.VMEM((w, kd), F32)],
        compiler_params=_params("arbitrary"))(qkv, qkv, qkv, qkv, qkv, do, sinks)


HBM_SPEC = pl.BlockSpec(memory_space=pl.ANY)


def _all_gather(name, shard):
    r, c_ = shard.shape

    def body(x_ref, out_ref, send_sems, recv_sems, local_sem):
        x, y, c = lax.axis_index("x"), lax.axis_index("y"), lax.axis_index("c")
        me, sibling = (x, y, c), (x, y, 1 - c)
        chips = [(1 - x, y), (x, 1 - y), (1 - x, 1 - y)]

        def slot(px, py, pc):
            return out_ref.at[4 * px + 2 * py + pc]

        def copy(k, block, to, src=None):
            return pltpu.make_async_remote_copy(
                src_ref=slot(*block) if src is None else src, dst_ref=slot(*block),
                send_sem=send_sems.at[k], recv_sem=recv_sems.at[k], device_id=to, device_id_type=MESH)

        mine = pltpu.make_async_copy(x_ref, slot(*me), local_sem)
        mine.start()
        first = [copy(0, me, sibling, src=x_ref)]
        first += [copy(1 + j, me, (*chip, c), src=x_ref) for j, chip in enumerate(chips)]
        for cp in first:
            cp.start()
        passed = [copy(4 + j, (*chip, c), sibling) for j, chip in enumerate(chips)]
        for j, chip in enumerate(chips):
            copy(1 + j, (*chip, c), me).wait_recv()
            passed[j].start()
        copy(0, sibling, me).wait_recv()
        for j, chip in enumerate(chips):
            copy(4 + j, (*chip, 1 - c), me).wait_recv()
        for cp in first + passed:
            cp.wait_send()
        mine.wait()

    return pl.pallas_call(
        body, name=name, in_specs=[HBM_SPEC], out_specs=HBM_SPEC,
        out_shape=jax.ShapeDtypeStruct((N_DEV, r, c_), shard.dtype),
        scratch_shapes=[pltpu.SemaphoreType.DMA((7,)), pltpu.SemaphoreType.DMA((7,)), pltpu.SemaphoreType.DMA],
    )(shard)


def _pair_exchange(name, blocks):
    _, r, c_ = blocks.shape

    def body(g_ref, out_ref, send_sems, recv_sems):
        x, y, c = lax.axis_index("x"), lax.axis_index("y"), lax.axis_index("c")
        copies = [pltpu.make_async_remote_copy(
            src_ref=g_ref.at[2 * k + 1 - c], dst_ref=out_ref.at[k], send_sem=send_sems.at[k],
            recv_sem=recv_sems.at[k], device_id=(x, y, 1 - c), device_id_type=MESH) for k in range(4)]
        for cp in copies:
            cp.start()
        for cp in copies:
            cp.wait()

    return pl.pallas_call(
        body, name=name, in_specs=[HBM_SPEC], out_specs=HBM_SPEC,
        out_shape=jax.ShapeDtypeStruct((4, r, c_), blocks.dtype),
        scratch_shapes=[pltpu.SemaphoreType.DMA((4,)), pltpu.SemaphoreType.DMA((4,))],
    )(blocks)


def _chip_exchange(name, blocks):
    def body(p_ref, out_ref, send_sems, recv_sems):
        x, y, c = lax.axis_index("x"), lax.axis_index("y"), lax.axis_index("c")
        chips = [(1 - x, y), (x, 1 - y), (1 - x, 1 - y)]
        copies = [pltpu.make_async_remote_copy(
            src_ref=p_ref.at[j], dst_ref=out_ref.at[j], send_sem=send_sems.at[j], recv_sem=recv_sems.at[j],
            device_id=(*chip, c), device_id_type=MESH) for j, chip in enumerate(chips)]
        for cp in copies:
            cp.start()
        for cp in copies:
            cp.wait()

    return pl.pallas_call(
        body, name=name, in_specs=[HBM_SPEC], out_specs=HBM_SPEC,
        out_shape=jax.ShapeDtypeStruct(blocks.shape, blocks.dtype),
        scratch_shapes=[pltpu.SemaphoreType.DMA((3,)), pltpu.SemaphoreType.DMA((3,))],
    )(blocks)


def _pair_sum(name, blocks, from_sibling, g_idx, r_idx, out_dtype):
    _, r, c_ = blocks.shape
    nk = g_idx.shape[0]
    tr = _tile(r, 512, 16)

    def body(gi_ref, ri_ref, a_ref, b_ref, o_ref):
        o_ref[...] = (a_ref[...] + b_ref[...]).astype(o_ref.dtype)

    return pl.pallas_call(
        body, name=name,
        grid_spec=pltpu.PrefetchScalarGridSpec(
            num_scalar_prefetch=2, grid=(nk, r // tr),
            in_specs=[pl.BlockSpec((None, tr, c_), lambda k, i, gi, ri: (gi[k], i, 0)),
                      pl.BlockSpec((None, tr, c_), lambda k, i, gi, ri: (ri[k], i, 0))],
            out_specs=pl.BlockSpec((None, tr, c_), lambda k, i, gi, ri: (k, i, 0))),
        out_shape=jax.ShapeDtypeStruct((nk, r, c_), out_dtype),
        compiler_params=_params("parallel", "parallel"))(g_idx, r_idx, blocks, from_sibling)


def _adamw(w, g, m, v):
    m = ADAM_B1 * m + (1.0 - ADAM_B1) * g
    v = ADAM_B2 * v + (1.0 - ADAM_B2) * (g * g)
    m_hat = m / (1.0 - ADAM_B1 ** ADAM_STEP)
    v_hat = v / (1.0 - ADAM_B2 ** ADAM_STEP)
    delta = -ADAM_LR * (m_hat / (jnp.sqrt(v_hat) + ADAM_EPS) + ADAM_WD * w)
    return delta, m, v


def _sum_adamw(name, parts_f32, parts_lo, w, m, v):
    r, c_ = w.shape
    tr = _tile(r, 256, 16)
    k1 = parts_f32.shape[0]
    k2 = 0 if parts_lo is None else parts_lo.shape[0]

    def body(*refs):
        a_ref = refs[0]
        b_ref = refs[1] if k2 else None
        w_ref, m_ref, v_ref, g_ref, d_ref, nm_ref, nv_ref = refs[(2 if k2 else 1):]
        g = a_ref[0]
        for k in range(1, k1):
            g = g + a_ref[k]
        for k in range(k2):
            g = g + b_ref[k].astype(F32)
        g_ref[...] = g
        d_ref[...], nm_ref[...], nv_ref[...] = _adamw(w_ref[...], g, m_ref[...], v_ref[...])

    row = pl.BlockSpec((tr, c_), lambda i: (i, 0))
    ins = [parts_f32] + ([parts_lo] if k2 else []) + [w, m, v]
    in_specs = [pl.BlockSpec((k1, tr, c_), lambda i: (0, i, 0))]
    if k2:
        in_specs.append(pl.BlockSpec((k2, tr, c_), lambda i: (0, i, 0)))
    in_specs += [row, row, row]
    return pl.pallas_call(body, name=name, grid=(r // tr,), in_specs=in_specs, out_specs=[row] * 4,
                          out_shape=[jax.ShapeDtypeStruct((r, c_), F32)] * 4,
                          compiler_params=_params("parallel"))(*ins)


def _pack_rows(flat, n_rows, cols):
    pad = n_rows * cols - flat.shape[-1]
    flat = jnp.pad(flat, [(0, 0)] * (flat.ndim - 1) + [(0, pad)])
    return flat.reshape(flat.shape[:-1] + (n_rows, cols))


def _split_blocks(full, axis):
    shape = full.shape
    a = full.reshape(shape[:axis] + (N_DEV, shape[axis] // N_DEV) + shape[axis + 1:])
    return jnp.moveaxis(a, axis, 0).reshape(N_DEV, -1)


def _join_blocks(flat, shard_shape, axis):
    a = jnp.moveaxis(flat.reshape((N_DEV,) + tuple(shard_shape)), 0, axis)
    return a.reshape(shard_shape[:axis] + (N_DEV * shard_shape[axis],) + shard_shape[axis + 1:])


def _perm_xbc(a, ng):
    lead = a.shape[:-1]
    di, gn = ng * GW, ng * SSD_D_STATE
    xs = a[..., :di].reshape(lead + (ng, GW))
    bs = a[..., di:di + gn].reshape(lead + (ng, SSD_D_STATE))
    cs = a[..., di + gn:].reshape(lead + (ng, SSD_D_STATE))
    return jnp.concatenate([xs, bs, cs], axis=-1).reshape(lead + (ng * GC,))


def _unperm_xbc(a, ng):
    lead = a.shape[:-1]
    g = a.reshape(lead + (ng, GC))
    return jnp.concatenate([g[..., :GW].reshape(lead + (ng * GW,)),
                            g[..., GW:GW + SSD_D_STATE].reshape(lead + (ng * SSD_D_STATE,)),
                            g[..., GW + SSD_D_STATE:].reshape(lead + (ng * SSD_D_STATE,))], axis=-1)


def _heads_col(v, ng):
    return jnp.pad(v.reshape(ng, 1, SSD_HPG), ((0, 0), (0, 0), (0, LANES - SSD_HPG)))


def _heads_row(v, ng):
    return jnp.pad(v.reshape(ng, SSD_HPG, 1), ((0, 0), (0, 8 - SSD_HPG), (0, 0)))


BIG = ("ssd_w_in", "ssd_w_out", "attn_w_qkv", "attn_w_o", "mlp_w_up", "mlp_w_down")
SMALL_SHARDED = ("ssd_conv_w", "attn_b_qkv", "attn_b_o")
SHARD_AXIS = {"ssd_w_in": 2, "ssd_conv_w": 2, "ssd_w_out": 1, "attn_w_qkv": 2, "attn_b_qkv": 1, "attn_w_o": 1,
              "attn_b_o": 1, "mlp_w_up": 2, "mlp_w_down": 1}
REPLICATED = ("ssd_conv_b", "ssd_dt_bias", "ssd_a_log", "ssd_d", "ssd_norm_w", "attn_sinks", "mix_pre_norm",
              "mix_post_norm", "ffn_pre_norm", "ffn_post_norm")
WEIGHTS = ("ssd_w_in", "ssd_conv_w", "ssd_conv_b", "ssd_dt_bias", "ssd_a_log", "ssd_d", "ssd_norm_w", "ssd_w_out",
           "attn_w_qkv", "attn_b_qkv", "attn_sinks", "attn_w_o", "attn_b_o", "mlp_w_up", "mlp_w_down",
           "mix_pre_norm", "mix_post_norm", "ffn_pre_norm", "ffn_post_norm")


def _forward_backward(x, target, wts, rep):
    t, d = x.shape
    ng = rep["ssd_norm_w"].shape[1] // GW
    di = ng * GW
    n_xbc = ng * GC
    nh = ng * SSD_HPG
    grads = {}

    w_in = wts["ssd_w_in"][0]
    w_z = w_in[:, :di]
    w_xbc = _perm_xbc(w_in[:, di:di + n_xbc], ng)
    w_dt = jnp.pad(w_in[:, di + n_xbc:], ((0, 0), (0, LANES - nh)))
    w_in_k = jnp.concatenate([w_z, w_xbc, w_dt], axis=1)
    n_in = w_in_k.shape[1]
    w_out = wts["ssd_w_out"][0]
    w_qkv = wts["attn_w_qkv"][0]
    w_o = wts["attn_w_o"][0]
    conv_w = _perm_xbc(wts["ssd_conv_w"][0], ng)
    conv_b = _perm_xbc(rep["ssd_conv_b"], ng)
    bias_c, alog_c, dsk_c = (_heads_col(rep[k], ng) for k in ("ssd_dt_bias", "ssd_a_log", "ssd_d"))
    bias_r, alog_r = (_heads_row(rep[k], ng) for k in ("ssd_dt_bias", "ssd_a_log"))
    norm = {k: rep[k] for k in ("mix_pre_norm", "mix_post_norm", "ffn_pre_norm", "ffn_post_norm")}

    def nrow(name, i):
        return norm[name][i:i + 1]

    def mlp_fwd(i, u2):
        a, p = _mm(f"mlp{i}_up", [u2], [wts["mlp_w_up"][i]], "nn", tm=1024, tn=1024, out_dtypes=(F32, BF16),
                   epilogue=lambda acc: (acc, jnp.square(jnp.maximum(acc, 0.0))))
        f = _mm(f"mlp{i}_down", [p], [wts["mlp_w_down"][i]], "nn", tm=512, tn=1024)
        return a, p, f

    def mlp_bwd(i, df, u2, a, p):
        da = _mm(f"mlp{i}_dact", [df], [wts["mlp_w_down"][i]], "nt", tm=1024, tn=1024, out_dtypes=(BF16,),
                 tiles=(a,), epilogue=lambda acc, av: (acc * (2.0 * jnp.maximum(av, 0.0)),))
        dw_down = _mm(f"mlp{i}_dwdown", [p], [df], "tn", tm=512, tn=1024)
        dw_up = _mm(f"mlp{i}_dwup", [u2], [da], "tn", tm=1024, tn=512)
        du2 = _mm(f"mlp{i}_dx", [da], [wts["mlp_w_up"][i]], "nt", tm=512, tn=1024)
        return du2, dw_up, dw_down

    u0 = _prenorm("l0_prenorm", x, nrow("mix_pre_norm", 0))
    zx = _mm("ssd_in_proj", [u0], [w_in_k], "nn", tm=1024, tn=_tile(n_in, 1024))
    pre = _conv_fwd(zx, di, n_xbc, conv_w, conv_b)
    dt_raw = zx[:, di + n_xbc:di + n_xbc + nh].reshape(t, ng, SSD_HPG)
    dtc = jnp.pad(jnp.transpose(dt_raw, (1, 0, 2)), ((0, 0), (0, 0), (0, LANES - SSD_HPG)))
    dtr = jnp.pad(jnp.transpose(dt_raw, (1, 2, 0)), ((0, 0), (0, 8 - SSD_HPG), (0, 0)))
    ssd_args = (dtc, dtr, bias_c, alog_c, dsk_c, bias_r, alog_r)
    y, states = _ssd_fwd(pre, *ssd_args)
    yn = _gate_norm_fwd(y, zx, rep["ssd_norm_w"])
    mix0 = _mm("ssd_out_proj", [yn], [w_out], "nn", tm=1024, tn=1024)
    h1, u0f = _post_pre("l0_mid", x, mix0, nrow("mix_post_norm", 0), nrow("ffn_pre_norm", 0))
    a0, p0, f0 = mlp_fwd(0, u0f)
    h2, u1 = _post_pre("l1_in", h1, f0, nrow("ffn_post_norm", 0), nrow("mix_pre_norm", 1))
    qkv = _mm("attn_qkv_proj", [u1], [w_qkv], "nn", tm=1024, tn=_tile(w_qkv.shape[1], 768), out_dtypes=(BF16,),
              rows=(wts["attn_b_qkv"],), epilogue=lambda acc, b: (acc + b,))
    ao = _attn_fwd(qkv, rep["attn_sinks"])
    mix1 = _mm("attn_out_proj", [ao], [w_o], "nn", tm=1024, tn=1024, rows=(wts["attn_b_o"],),
               epilogue=lambda acc, b: (acc + b,))
    h3, u1f = _post_pre("l1_mid", h2, mix1, nrow("mix_post_norm", 1), nrow("ffn_pre_norm", 1))
    a1, p1, f1 = mlp_fwd(1, u1f)
    dh, loss_row = _final_loss("loss", h3, f1, nrow("ffn_post_norm", 1), target)

    g_norm = {k: [None, None] for k in norm}
    df1, g_norm["ffn_post_norm"][1], _ = _norm_bwd("l1_ffn_post_bwd", dh, post=(f1, nrow("ffn_post_norm", 1)))
    du, dw_up1, dw_down1 = mlp_bwd(1, df1, u1f, a1, p1)
    dh, g_norm["ffn_pre_norm"][1], dmix1, g_norm["mix_post_norm"][1], db_o = _norm_bwd(
        "l1_mid_bwd", dh, pre=(du, h3, nrow("ffn_pre_norm", 1)), post=(mix1, nrow("mix_post_norm", 1)))
    grads["attn_b_o"] = db_o
    grads["attn_w_o"] = _mm("attn_dwo", [ao], [dmix1], "tn", tm=1024, tn=512)[None]
    dao = _mm("attn_dout", [dmix1], [w_o], "nt", tm=1024, tn=1024, out_dtypes=(BF16,))
    dq, dk, dv, bq, bk, bv, dsinks = _attn_bwd(qkv, dao, rep["attn_sinks"])
    dqkv = jnp.concatenate([dq, dk, dv], axis=1)
    grads["attn_b_qkv"] = jnp.concatenate([bq, bk, bv], axis=1)
    grads["attn_sinks"] = dsinks
    grads["attn_w_qkv"] = _mm("attn_dwqkv", [u1], [dqkv], "tn", tm=1024, tn=512)[None]
    du = _mm("attn_dx", [dqkv], [w_qkv], "nt", tm=1024, tn=1024)
    dh, g_norm["mix_pre_norm"][1], df0, g_norm["ffn_post_norm"][0], _ = _norm_bwd(
        "l1_in_bwd", dh, pre=(du, h2, nrow("mix_pre_norm", 1)), post=(f0, nrow("ffn_post_norm", 0)))
    du, dw_up0, dw_down0 = mlp_bwd(0, df0, u0f, a0, p0)
    grads["mlp_w_up"] = jnp.stack([dw_up0, dw_up1])
    grads["mlp_w_down"] = jnp.stack([dw_down0, dw_down1])
    dh, g_norm["ffn_pre_norm"][0], dmix0, g_norm["mix_post_norm"][0], _ = _norm_bwd(
        "l0_mid_bwd", dh, pre=(du, h1, nrow("ffn_pre_norm", 0)), post=(mix0, nrow("mix_post_norm", 0)))
    grads["ssd_w_out"] = _mm("ssd_dwout", [yn], [dmix0], "tn", tm=512, tn=1024)[None]
    dyn = _mm("ssd_dyn", [dmix0], [w_out], "nt", tm=1024, tn=1024)
    dy, dz, grads["ssd_norm_w"] = _gate_norm_bwd(dyn, y, zx, rep["ssd_norm_w"])
    dpre, ddt_g, dbias_g, dalog_g, dd_g = _ssd_bwd(dy, pre, states, *ssd_args)
    dxbc, dconv_w, dconv_b = _conv_bwd(dpre, zx, di, conv_w)
    ddt = jnp.transpose(ddt_g[:, :, :SSD_HPG], (1, 0, 2)).reshape(t, nh)
    ddt = jnp.pad(ddt, ((0, 0), (0, LANES - nh))).astype(BF16)
    grads["ssd_conv_w"] = _unperm_xbc(dconv_w, ng)[None]
    grads["ssd_conv_b"] = _unperm_xbc(dconv_b, ng)
    for name, val in (("ssd_dt_bias", dbias_g), ("ssd_a_log", dalog_g), ("ssd_d", dd_g)):
        grads[name] = val[:, 0, :SSD_HPG].reshape(1, nh)
    dw_z = _mm("ssd_dwz", [u0], [dz], "tn", tm=1024, tn=512)
    dw_xbc = _mm("ssd_dwxbc", [u0], [dxbc], "tn", tm=1024, tn=512)
    dw_dt = _mm("ssd_dwdt", [u0], [ddt], "tn", tm=1024, tn=LANES)
    grads["ssd_w_in"] = jnp.concatenate([dw_z, _unperm_xbc(dw_xbc, ng), dw_dt[:, :nh]], axis=1)[None]
    du = _mm("ssd_dx", [dz, dxbc, ddt], [w_z, w_xbc, w_dt], "nt", tm=256, tn=1024)
    grad_x, g_norm["mix_pre_norm"][0] = _norm_bwd("l0_in_bwd", dh, pre=(du, x, nrow("mix_pre_norm", 0)))
    for k in norm:
        grads[k] = jnp.concatenate(g_norm[k], axis=0)
    return loss_row, grad_x, grads


def kernel(x, ssd_w_in, ssd_conv_w, ssd_conv_b, ssd_dt_bias, ssd_a_log, ssd_d, ssd_norm_w, ssd_w_out, attn_w_qkv, attn_b_qkv, attn_sinks, attn_w_o, attn_b_o, mlp_w_up, mlp_w_down, mix_pre_norm, mix_post_norm, ffn_pre_norm, ffn_post_norm, loss_target, m_ssd_w_in, m_ssd_conv_w, m_ssd_conv_b, m_ssd_dt_bias, m_ssd_a_log, m_ssd_d, m_ssd_norm_w, m_ssd_w_out, m_attn_w_qkv, m_attn_b_qkv, m_attn_sinks, m_attn_w_o, m_attn_b_o, m_mlp_w_up, m_mlp_w_down, m_mix_pre_norm, m_mix_post_norm, m_ffn_pre_norm, m_ffn_post_norm, v_ssd_w_in, v_ssd_conv_w, v_ssd_conv_b, v_ssd_dt_bias, v_ssd_a_log, v_ssd_d, v_ssd_norm_w, v_ssd_w_out, v_attn_w_qkv, v_attn_b_qkv, v_attn_sinks, v_attn_w_o, v_attn_b_o, v_mlp_w_up, v_mlp_w_down, v_mix_pre_norm, v_mix_post_norm, v_ffn_pre_norm, v_ffn_post_norm):
    given = dict(locals())
    w = {k: given[k] for k in WEIGHTS}
    mom_m = {k: given["m_" + k] for k in WEIGHTS}
    mom_v = {k: given["v_" + k] for k in WEIGHTS}
    sharded = BIG + SMALL_SHARDED

    small_flat = jnp.concatenate([w[k].reshape(-1) for k in SMALL_SHARDED])
    small_bits = lax.bitcast_convert_type(small_flat, PAYLOAD).reshape(-1)
    payload = jnp.concatenate([w[k].astype(PAYLOAD).reshape(-1) for k in BIG] + [small_bits])
    ag_rows = _round_up(-(-payload.shape[0] // PACK_COLS), LANES)
    gathered = _all_gather("gather_weights", _pack_rows(payload, ag_rows, PACK_COLS)).reshape(N_DEV, -1)
    full = {}
    off = 0
    for k in BIG:
        n = w[k].size
        full[k] = _join_blocks(gathered[:, off:off + n], w[k].shape, SHARD_AXIS[k])
        off += n
    n_small = small_flat.shape[0]
    small_all = lax.bitcast_convert_type(gathered[:, off:off + 2 * n_small].reshape(N_DEV, n_small, 2), F32)
    off = 0
    for k in SMALL_SHARDED:
        n = w[k].size
        full[k] = _join_blocks(small_all[:, off:off + n], w[k].shape, SHARD_AXIS[k])
        off += n

    rep = {k: w[k] for k in REPLICATED}
    loss_row, grad_x, grads = _forward_backward(x[0], loss_target[0], full, rep)

    def pack_shard(tree):
        return jnp.concatenate([tree[k].reshape(-1) for k in sharded])

    n_shard = sum(w[k].size for k in sharded)
    rs_rows = _round_up(-(-n_shard // PACK_COLS), LANES)
    blocks = _pack_rows(jnp.concatenate([_split_blocks(grads[k], SHARD_AXIS[k]) for k in sharded], axis=1),
                        rs_rows, PACK_COLS)
    from_sibling = _pair_exchange("rs_pair_exchange", blocks)
    ix, iy, ic = lax.axis_index("x"), lax.axis_index("y"), lax.axis_index("c")
    chips = [(ix, iy), (1 - ix, iy), (ix, 1 - iy), (1 - ix, 1 - iy)]
    g_idx = jnp.stack([4 * cx + 2 * cy + ic for cx, cy in chips]).astype(jnp.int32)
    r_idx = jnp.stack([2 * cx + cy for cx, cy in chips]).astype(jnp.int32)
    own = _pair_sum("rs_pair_sum_own", blocks, from_sibling, g_idx[:1], r_idx[:1], F32)
    to_chips = _pair_sum("rs_pair_sum_send", blocks, from_sibling, g_idx[1:], r_idx[1:], PAYLOAD)
    from_chips = _chip_exchange("rs_chip_exchange", to_chips)
    packed = [_pack_rows(pack_shard(tree), rs_rows, PACK_COLS) for tree in (w, mom_m, mom_v)]
    shard_out = _sum_adamw("rs_sum_adamw", own, from_chips, *packed)

    def pack_rep(tree, last):
        flat = jnp.concatenate([tree[k].reshape(-1) for k in REPLICATED] + [last])
        return _pack_rows(flat, _round_up(-(-flat.shape[0] // LANES), 8), LANES)

    partials = _all_gather("gather_small_grads", pack_rep(grads, loss_row[0, :1]))
    zero = jnp.zeros((1,), F32)
    rep_out = _sum_adamw("small_sum_adamw", partials, None, pack_rep(w, zero), pack_rep(mom_m, zero),
                         pack_rep(mom_v, zero))

    def unpack(flat, names):
        out, off = {}, 0
        for k in names:
            out[k] = flat[off:off + w[k].size].reshape(w[k].shape)
            off += w[k].size
        return out, off

    kinds = []
    for s_arr, r_arr in zip(shard_out, rep_out):
        tree, _ = unpack(s_arr.reshape(-1), sharded)
        rtree, off = unpack(r_arr.reshape(-1), REPLICATED)
        tree.update(rtree)
        kinds.append(tree)
    loss = rep_out[0].reshape(-1)[off]
    outs = [loss, grad_x[None]]
    for tree in kinds:
        outs += [tree[k] for k in WEIGHTS]
    return tuple(outs)
```

```python
import functools

import jax
import jax.numpy as jnp
from jax import lax
from jax.experimental import pallas as pl
from jax.experimental.pallas import tpu as pltpu

F32 = jnp.float32
BF16 = jnp.bfloat16
PAYLOAD = jnp.bfloat16
HIGHEST = lax.Precision.HIGHEST
MESH = pl.DeviceIdType.MESH

NORM_EPS = 1e-6
SSD_HEAD_DIM = 64
SSD_N_GROUPS = 8
SSD_HPG = 4
SSD_D_STATE = 128
SSD_CONV_WIDTH = 4
SSD_CHUNK = 128
ATTN_HEAD_DIM = 64
ATTN_N_KV = 4
ATTN_REP = 4
ATTN_WINDOW = 128
ADAM_LR = 0.001
ADAM_B1 = 0.9
ADAM_B2 = 0.999
ADAM_EPS = 1e-08
ADAM_WD = 0.01
ADAM_STEP = 10

N_DEV = 8
LANES = 128
PACK_COLS = 1024
V7X_VMEM_LIMIT = 56 * 1024 * 1024

GW = SSD_HPG * SSD_HEAD_DIM
GC = GW + 2 * SSD_D_STATE


def _params(*sem):
    return pltpu.CompilerParams(dimension_semantics=sem, vmem_limit_bytes=V7X_VMEM_LIMIT)


def _tile(n, pref, mult=LANES):
    best = None
    t = mult
    while t <= min(n, pref):
        if n % t == 0:
            best = t
        t += mult
    return best if best is not None else n


def _round_up(n, m):
    return (n + m - 1) // m * m


def _acc(ref, val, first):
    @pl.when(first)
    def _():
        ref[...] = val

    @pl.when(jnp.logical_not(first))
    def _():
        ref[...] += val


def _dot(a, b):
    return lax.dot_general(a, b, (((1,), (0,)), ((), ())), preferred_element_type=F32)


def _dot_nt(a, b):
    return lax.dot_general(a, b, (((1,), (1,)), ((), ())), preferred_element_type=F32)


def _dot_tn(a, b):
    return lax.dot_general(a, b, (((0,), (0,)), ((), ())), preferred_element_type=F32)


def _dot_f32(a, b):
    return lax.dot_general(a, b, (((1,), (0,)), ((), ())), preferred_element_type=F32, precision=HIGHEST)


_DOTS = {"nn": _dot, "nt": _dot_nt, "tn": _dot_tn}


def _sigmoid(x):
    return 1.0 / (1.0 + jnp.exp(-x))


def _softplus(x):
    return jnp.maximum(x, 0.0) + jnp.log1p(jnp.exp(-jnp.abs(x)))


def _silu_grad(x, s):
    return s * (1.0 + x * (1.0 - s))


def _mm(name, a_list, b_list, mode, *, tm, tn, out_dtypes=(F32,), epilogue=None, tiles=(), rows=(),
        col_blocks=False):
    npair = len(a_list)
    if mode == "tn":
        m = a_list[0].shape[1]
    else:
        m = a_list[0].shape[0]
    n = b_list[0].shape[0] if mode == "nt" else b_list[0].shape[1]
    tm = _tile(m, tm, LANES if mode == "tn" else 8)
    tn = _tile(n, tn)
    assert m % tm == 0 and n % tn == 0, (name, m, n, tm, tn)
    dot = _DOTS[mode]

    def body(*refs):
        a_refs = refs[:npair]
        b_refs = refs[npair:2 * npair]
        e_refs = refs[2 * npair:2 * npair + len(tiles) + len(rows)]
        o_refs = refs[2 * npair + len(tiles) + len(rows):]
        acc = None
        for ar, br in zip(a_refs, b_refs):
            d = dot(ar[...], br[...])
            acc = d if acc is None else acc + d
        outs = epilogue(acc, *[e[...] for e in e_refs]) if epilogue is not None else (acc,)
        for o, v in zip(o_refs, outs):
            o[...] = v.astype(o.dtype)

    in_specs = []
    for a in a_list:
        if mode == "tn":
            in_specs.append(pl.BlockSpec((a.shape[0], tm), lambda i, j: (0, i)))
        else:
            in_specs.append(pl.BlockSpec((tm, a.shape[1]), lambda i, j: (i, 0)))
    for b in b_list:
        if mode == "nt":
            in_specs.append(pl.BlockSpec((tn, b.shape[1]), lambda i, j: (j, 0)))
        else:
            in_specs.append(pl.BlockSpec((b.shape[0], tn), lambda i, j: (0, j)))
    in_specs += [pl.BlockSpec((tm, tn), lambda i, j: (i, j)) for _ in tiles]
    in_specs += [pl.BlockSpec((1, tn), lambda i, j: (0, j)) for _ in rows]
    outs = pl.pallas_call(
        body,
        name=name,
        grid=(m // tm, n // tn),
        in_specs=in_specs,
        out_specs=[pl.BlockSpec((None, tm, tn), lambda i, j: (j, i, 0)) if col_blocks else
                   pl.BlockSpec((tm, tn), lambda i, j: (i, j)) for _ in out_dtypes],
        out_shape=[jax.ShapeDtypeStruct((n // tn, m, tn) if col_blocks else (m, n), dt) for dt in out_dtypes],
        compiler_params=_params("parallel", "parallel"),
    )(*a_list, *b_list, *tiles, *rows)
    return outs[0] if len(out_dtypes) == 1 else outs


def _rms(x, w):
    r = lax.rsqrt(jnp.mean(x * x, axis=-1, keepdims=True) + NORM_EPS)
    return x * r * w


def _rms_bwd(x, w, dy):
    r = lax.rsqrt(jnp.mean(x * x, axis=-1, keepdims=True) + NORM_EPS)
    xh = x * r
    g = dy * w
    dx = r * (g - xh * jnp.mean(g * xh, axis=-1, keepdims=True))
    return dx, dy * xh


def _row_specs(tr, d):
    return pl.BlockSpec((tr, d), lambda i: (i, 0)), pl.BlockSpec((1, d), lambda i: (0, 0))


def _prenorm(name, h, w):
    t, d = h.shape
    tr = _tile(t, 512, 8)
    row, vec = _row_specs(tr, d)

    def body(h_ref, w_ref, u_ref):
        u_ref[...] = _rms(h_ref[...], w_ref[...]).astype(BF16)

    return pl.pallas_call(body, name=name, grid=(t // tr,), in_specs=[row, vec], out_specs=row,
                          out_shape=jax.ShapeDtypeStruct((t, d), BF16), compiler_params=_params("parallel"))(h, w)


def _post_pre(name, h, m, w_post, w_pre):
    t, d = h.shape
    tr = _tile(t, 512, 8)
    row, vec = _row_specs(tr, d)

    def body(h_ref, m_ref, wq_ref, wp_ref, hn_ref, u_ref):
        hn = h_ref[...] + _rms(m_ref[...], wq_ref[...])
        hn_ref[...] = hn
        u_ref[...] = _rms(hn, wp_ref[...]).astype(BF16)

    return pl.pallas_call(body, name=name, grid=(t // tr,), in_specs=[row, row, vec, vec], out_specs=[row, row],
                          out_shape=[jax.ShapeDtypeStruct((t, d), F32), jax.ShapeDtypeStruct((t, d), BF16)],
                          compiler_params=_params("parallel"))(h, m, w_post, w_pre)


def _final_loss(name, h, m, w_post, target):
    t, d = h.shape
    tr = _tile(t, 512, 8)
    row, vec = _row_specs(tr, d)

    def body(h_ref, m_ref, wq_ref, t_ref, dh_ref, loss_ref):
        err = h_ref[...] + _rms(m_ref[...], wq_ref[...]) - t_ref[...]
        dh_ref[...] = err * (1.0 / d)
        part = 0.5 * jnp.sum(jnp.mean(err * err, axis=-1, keepdims=True), axis=0, keepdims=True)
        _acc(loss_ref, jnp.broadcast_to(part, (1, LANES)), pl.program_id(0) == 0)

    return pl.pallas_call(body, name=name, grid=(t // tr,), in_specs=[row, row, vec, row],
                          out_specs=[row, pl.BlockSpec((1, LANES), lambda i: (0, 0))],
                          out_shape=[jax.ShapeDtypeStruct((t, d), F32), jax.ShapeDtypeStruct((1, LANES), F32)],
                          compiler_params=_params("arbitrary"))(h, m, w_post, target)


def _norm_bwd(name, dh, pre=None, post=None):
    t, d = dh.shape
    tr = _tile(t, 256, 8)
    row, vec = _row_specs(tr, d)
    has_pre, has_post = pre is not None, post is not None

    def body(*refs):
        it = iter(refs)
        dh_ref = next(it)
        if has_pre:
            du_ref, x_ref, wp_ref = next(it), next(it), next(it)
        if has_post:
            m_ref, wq_ref = next(it), next(it)
        first = pl.program_id(0) == 0
        dh_v = dh_ref[...]
        if has_pre:
            dhn_ref, dwp_ref = next(it), next(it)
            dx, dwr = _rms_bwd(x_ref[...], wp_ref[...], du_ref[...])
            dh_v = dh_v + dx
            dhn_ref[...] = dh_v
            _acc(dwp_ref, jnp.sum(dwr, axis=0, keepdims=True), first)
        if has_post:
            dm_ref, dwq_ref, dms_ref = next(it), next(it), next(it)
            dm, dwr = _rms_bwd(m_ref[...], wq_ref[...], dh_v)
            dm_ref[...] = dm.astype(BF16)
            _acc(dwq_ref, jnp.sum(dwr, axis=0, keepdims=True), first)
            _acc(dms_ref, jnp.sum(dm, axis=0, keepdims=True), first)

    ins, in_specs, out_specs, out_shape = [dh], [row], [], []
    if has_pre:
        ins += list(pre)
        in_specs += [row, row, vec]
        out_specs += [row, vec]
        out_shape += [jax.ShapeDtypeStruct((t, d), F32), jax.ShapeDtypeStruct((1, d), F32)]
    if has_post:
        ins += list(post)
        in_specs += [row, vec]
        out_specs += [row, vec, vec]
        out_shape += [jax.ShapeDtypeStruct((t, d), BF16), jax.ShapeDtypeStruct((1, d), F32),
                      jax.ShapeDtypeStruct((1, d), F32)]
    return pl.pallas_call(body, name=name, grid=(t // tr,), in_specs=in_specs, out_specs=out_specs,
                          out_shape=out_shape, compiler_params=_params("arbitrary"))(*ins)


HALO = 8


def _conv_fwd(zx, col0, n_ch, conv_w, conv_b):
    t = zx.shape[0]
    tc = _tile(n_ch, 512)
    tt = _tile(t, 512, 8)
    cb0 = col0 // tc
    assert col0 % tc == 0
    kw = SSD_CONV_WIDTH

    def body(x_ref, p_ref, w_ref, b_ref, o_ref, xe_ref):
        i = pl.program_id(1)
        cur = x_ref[...]
        xe_ref[0:HALO, :] = jnp.where(i > 0, p_ref[...], 0.0)
        xe_ref[HALO:HALO + tt, :] = cur
        w = w_ref[...]
        acc = b_ref[...] + w[kw - 1:kw, :] * cur
        for k in range(kw - 1):
            acc = acc + w[k:k + 1, :] * xe_ref[pl.ds(HALO - (kw - 1) + k, tt), :]
        o_ref[...] = acc

    return pl.pallas_call(
        body, name="ssd_conv_fwd", grid=(n_ch // tc, t // tt),
        in_specs=[pl.BlockSpec((tt, tc), lambda j, i: (i, cb0 + j)),
                  pl.BlockSpec((HALO, tc), lambda j, i: (jnp.maximum(i * (tt // HALO) - 1, 0), cb0 + j)),
                  pl.BlockSpec((kw, tc), lambda j, i: (0, j)),
                  pl.BlockSpec((1, tc), lambda j, i: (0, j))],
        out_specs=pl.BlockSpec((tt, tc), lambda j, i: (i, j)),
        out_shape=jax.ShapeDtypeStruct((t, n_ch), F32),
        scratch_shapes=[pltpu.VMEM((tt + HALO, tc), F32)],
        compiler_params=_params("parallel", "parallel"))(zx, zx, conv_w, conv_b)


def _conv_bwd(dpre, zx, col0, conv_w):
    t, n_ch = dpre.shape
    tc = _tile(n_ch, 512)
    tt = _tile(t, 512, 8)
    cb0 = col0 // tc
    kw = SSD_CONV_WIDTH
    nt = t // tt

    def body(d_ref, dn_ref, x_ref, p_ref, w_ref, dx_ref, dw_ref, db_ref, de_ref, xe_ref):
        i = pl.program_id(1)
        d = d_ref[...]
        de_ref[0:tt, :] = d
        de_ref[tt:tt + HALO, :] = jnp.where(i < nt - 1, dn_ref[...], 0.0)
        xe_ref[0:HALO, :] = jnp.where(i > 0, p_ref[...], 0.0)
        xe_ref[HALO:HALO + tt, :] = x_ref[...]
        w = w_ref[...]
        dx = w[kw - 1:kw, :] * d
        for k in range(kw - 1):
            dx = dx + w[k:k + 1, :] * de_ref[pl.ds(kw - 1 - k, tt), :]
        dx_ref[...] = dx.astype(BF16)
        first = i == 0
        for k in range(kw):
            xs = xe_ref[pl.ds(HALO - (kw - 1) + k, tt), :]
            val = jnp.sum(d * xs, axis=0, keepdims=True)

            @pl.when(first)
            def _():
                dw_ref[k:k + 1, :] = val

            @pl.when(jnp.logical_not(first))
            def _():
                dw_ref[k:k + 1, :] += val
        _acc(db_ref, jnp.sum(d, axis=0, keepdims=True), first)

    return pl.pallas_call(
        body, name="ssd_conv_bwd", grid=(n_ch // tc, nt),
        in_specs=[pl.BlockSpec((tt, tc), lambda j, i: (i, j)),
                  pl.BlockSpec((HALO, tc), lambda j, i: (jnp.minimum((i + 1) * (tt // HALO), t // HALO - 1), j)),
                  pl.BlockSpec((tt, tc), lambda j, i: (i, cb0 + j)),
                  pl.BlockSpec((HALO, tc), lambda j, i: (jnp.maximum(i * (tt // HALO) - 1, 0), cb0 + j)),
                  pl.BlockSpec((kw, tc), lambda j, i: (0, j))],
        out_specs=[pl.BlockSpec((tt, tc), lambda j, i: (i, j)),
                   pl.BlockSpec((kw, tc), lambda j, i: (0, j)),
                   pl.BlockSpec((1, tc), lambda j, i: (0, j))],
        out_shape=[jax.ShapeDtypeStruct((t, n_ch), BF16), jax.ShapeDtypeStruct((kw, n_ch), F32),
                   jax.ShapeDtypeStruct((1, n_ch), F32)],
        scratch_shapes=[pltpu.VMEM((tt + HALO, tc), F32), pltpu.VMEM((tt + HALO, tc), F32)],
        compiler_params=_params("parallel", "arbitrary"))(dpre, dpre, zx, zx, conv_w)


def _head_of_lane(shape, width):
    return lax.broadcasted_iota(jnp.int32, shape, len(shape) - 1) // width


def _expand(v, n_rows):
    head = _head_of_lane((n_rows, GW), SSD_HEAD_DIM)
    out = jnp.zeros((n_rows, GW), F32)
    for j in range(SSD_HPG):
        out = jnp.where(head == j, v[:, j:j + 1], out)
    return out


def _contract(v, n_rows):
    head = _head_of_lane((n_rows, GW), SSD_HEAD_DIM)
    lane = lax.broadcasted_iota(jnp.int32, (n_rows, LANES), 1)
    out = jnp.zeros((n_rows, LANES), F32)
    for j in range(SSD_HPG):
        s = jnp.sum(jnp.where(head == j, v, 0.0), axis=1, keepdims=True)
        out = jnp.where(lane == j, s, out)
    return out


def _ssd_common(pre, dtc, bias_c, alog_c, dtr, bias_r, alog_r):
    q = SSD_CHUNK
    sg = _sigmoid(pre)
    act = pre * sg
    xa = act[:, :GW]
    bm = act[:, GW:GW + SSD_D_STATE].astype(BF16)
    cm = act[:, GW + SSD_D_STATE:].astype(BF16)
    row = lax.broadcasted_iota(jnp.int32, (q, q), 0)
    col = lax.broadcasted_iota(jnp.int32, (q, q), 1)
    tril = col <= row
    dt = _softplus(dtc + bias_c)
    a_c = -jnp.exp(alog_c)
    cum = _dot_f32(tril.astype(F32), dt * a_c)
    dt_r = _softplus(dtr + bias_r)
    cum_r = _dot_f32(dt_r * (-jnp.exp(alog_r)), (row <= col).astype(F32))
    g = _dot_nt(cm, bm)
    dt_x = _expand(dt, q)
    xdt = xa * dt_x
    cl = cum[q - 1:q, :]
    e_c = jnp.exp(cl - cum)
    lam_c = jnp.exp(cum)
    return dict(sg=sg, xa=xa, bm=bm, cm=cm, tril=tril, row=row, col=col, dt=dt, a_c=a_c, cum=cum, cum_r=cum_r,
                g=g, dt_x=dt_x, xdt=xdt, cl=cl, e_c=e_c, lam_c=lam_c)


def _ssd_specs(nc, rev):
    q = SSD_CHUNK

    def ch(c):
        return nc - 1 - c if rev else c

    chunk_grp = pl.BlockSpec((q, GC), lambda g, c: (ch(c), g))
    col_form = pl.BlockSpec((None, q, LANES), lambda g, c: (g, ch(c), 0))
    row_form = pl.BlockSpec((None, 8, q), lambda g, c: (g, 0, ch(c)))
    col_par = pl.BlockSpec((None, 1, LANES), lambda g, c: (g, 0, 0))
    row_par = pl.BlockSpec((None, 8, 1), lambda g, c: (g, 0, 0))
    y_spec = pl.BlockSpec((q, GW), lambda g, c: (ch(c), g))
    st_spec = pl.BlockSpec((None, None, GW, SSD_D_STATE), lambda g, c: (g, ch(c), 0, 0))
    return chunk_grp, col_form, row_form, col_par, row_par, y_spec, st_spec


def _ssd_fwd(pre, dtc, dtr, bias_c, alog_c, dsk_c, bias_r, alog_r):
    t = pre.shape[0]
    ng = pre.shape[1] // GC
    q = SSD_CHUNK
    nc = t // q
    chunk_grp, col_form, row_form, col_par, row_par, y_spec, st_spec = _ssd_specs(nc, False)

    def body(pre_ref, dtc_ref, dtr_ref, bc_ref, ac_ref, dk_ref, br_ref, ar_ref, y_ref, sp_ref, st_ref):
        @pl.when(pl.program_id(1) == 0)
        def _():
            st_ref[...] = jnp.zeros_like(st_ref)

        v = _ssd_common(pre_ref[...], dtc_ref[...], bc_ref[...], ac_ref[...], dtr_ref[...], br_ref[...], ar_ref[...])
        s0 = st_ref[...]
        sp_ref[...] = s0
        r = _dot_nt(v["cm"], s0.astype(BF16))
        y = _expand(v["lam_c"], q) * r + _expand(dk_ref[...], 1) * v["xa"]
        head = _head_of_lane((q, GW), SSD_HEAD_DIM)
        for j in range(SSD_HPG):
            diff = v["cum"][:, j:j + 1] - v["cum_r"][j:j + 1, :]
            w = (v["g"] * jnp.exp(jnp.where(v["tril"], diff, -jnp.inf))).astype(BF16)
            y = y + _dot(w, jnp.where(head == j, v["xdt"], 0.0).astype(BF16))
        y_ref[...] = y
        ds = _dot_tn((v["xdt"] * _expand(v["e_c"], q)).astype(BF16), v["bm"])
        for j in range(SSD_HPG):
            rows = slice(j * SSD_HEAD_DIM, (j + 1) * SSD_HEAD_DIM)
            st_ref[rows, :] = s0[rows, :] * jnp.exp(v["cum_r"][j:j + 1, q - 1:q]) + ds[rows, :]

    return pl.pallas_call(
        body, name="ssd_scan_fwd", grid=(ng, nc),
        in_specs=[chunk_grp, col_form, row_form, col_par, col_par, col_par, row_par, row_par],
        out_specs=[y_spec, st_spec],
        out_shape=[jax.ShapeDtypeStruct((t, ng * GW), F32), jax.ShapeDtypeStruct((ng, nc, GW, SSD_D_STATE), F32)],
        scratch_shapes=[pltpu.VMEM((GW, SSD_D_STATE), F32)],
        compiler_params=_params("parallel", "arbitrary"))(pre, dtc, dtr, bias_c, alog_c, dsk_c, bias_r, alog_r)


def _ssd_bwd(dy, pre, states, dtc, dtr, bias_c, alog_c, dsk_c, bias_r, alog_r):
    t = pre.shape[0]
    ng = pre.shape[1] // GC
    q = SSD_CHUNK
    nc = t // q
    chunk_grp, col_form, row_form, col_par, row_par, y_spec, st_spec = _ssd_specs(nc, True)

    def body(dy_ref, pre_ref, sp_ref, dtc_ref, dtr_ref, bc_ref, ac_ref, dk_ref, br_ref, ar_ref,
             dpre_ref, ddt_ref, dbias_ref, dalog_ref, dd_ref, ds_ref):
        first = pl.program_id(1) == 0

        @pl.when(first)
        def _():
            ds_ref[...] = jnp.zeros_like(ds_ref)

        pre_v = pre_ref[...]
        v = _ssd_common(pre_v, dtc_ref[...], bc_ref[...], ac_ref[...], dtr_ref[...], br_ref[...], ar_ref[...])
        xa, bm, cm, xdt, cum, cum_r = v["xa"], v["bm"], v["cm"], v["xdt"], v["cum"], v["cum_r"]
        xdt_b = xdt.astype(BF16)
        dy_v = dy_ref[...]
        s0 = sp_ref[...]
        ds1 = ds_ref[...]
        s0b, ds1b = s0.astype(BF16), ds1.astype(BF16)
        head = _head_of_lane((q, GW), SSD_HEAD_DIM)
        lane = lax.broadcasted_iota(jnp.int32, (q, LANES), 1)
        lane1 = lax.broadcasted_iota(jnp.int32, (1, LANES), 1)
        lam_x = _expand(v["lam_c"], q)
        e_x = _expand(v["e_c"], q)

        dxa = _expand(dk_ref[...], 1) * dy_v
        dd = _contract(jnp.sum(dy_v * xa, axis=0, keepdims=True), 1)
        r = _dot_nt(cm, s0b)
        dcum = _contract(dy_v * r * lam_x, q)
        drb = (lam_x * dy_v).astype(BF16)
        dc = _dot(drb, s0b)
        ds0 = _dot_tn(drb, cm)
        extra = jnp.zeros((1, LANES), F32)
        for j in range(SSD_HPG):
            rows = slice(j * SSD_HEAD_DIM, (j + 1) * SSD_HEAD_DIM)
            lam_last = jnp.exp(cum_r[j:j + 1, q - 1:q])
            ds_ref[rows, :] = ds0[rows, :] + lam_last * ds1[rows, :]
            tot = jnp.sum(jnp.sum(ds1[rows, :] * s0[rows, :], axis=1, keepdims=True), axis=0, keepdims=True)
            extra = jnp.where(lane1 == j, lam_last * tot, extra)
        dv = _dot_nt(bm, ds1b)
        db = _dot((xdt * e_x).astype(BF16), ds1b)
        dxdt = e_x * dv
        dee = _contract(dv * xdt, q) * v["e_c"]
        dcum = dcum - dee
        extra = extra + jnp.sum(dee, axis=0, keepdims=True)
        dg = jnp.zeros((q, q), F32)
        for j in range(SSD_HPG):
            diff = cum[:, j:j + 1] - cum_r[j:j + 1, :]
            el = jnp.exp(jnp.where(v["tril"], diff, -jnp.inf))
            gl = v["g"] * el
            dym = jnp.where(head == j, dy_v, 0.0).astype(BF16)
            dwm = _dot_nt(dym, xdt_b)
            dxdt = dxdt + _dot_tn(gl.astype(BF16), dym)
            z = dwm * gl
            rk = jnp.sum(z, axis=1, keepdims=True) - jnp.sum(z.T, axis=1, keepdims=True)
            dcum = jnp.where(lane == j, dcum + rk, dcum)
            dg = dg + dwm * el
        dgb = dg.astype(BF16)
        dc = dc + _dot(dgb, bm)
        db = db + _dot_tn(dgb, cm)
        da = _dot_f32((v["row"] <= v["col"]).astype(F32), dcum) + extra
        ddt = _contract(dxdt * xa, q) + v["a_c"] * da
        dalog = jnp.sum(v["dt"] * da, axis=0, keepdims=True) * v["a_c"]
        dxa = dxa + v["dt_x"] * dxdt
        ddt_raw = jnp.where(lane < SSD_HPG, ddt * _sigmoid(dtc_ref[...] + bc_ref[...]), 0.0)
        sgrad = _silu_grad(pre_v, v["sg"])
        dpre_ref[:, :GW] = dxa * sgrad[:, :GW]
        dpre_ref[:, GW:GW + SSD_D_STATE] = db * sgrad[:, GW:GW + SSD_D_STATE]
        dpre_ref[:, GW + SSD_D_STATE:] = dc * sgrad[:, GW + SSD_D_STATE:]
        ddt_ref[...] = ddt_raw
        _acc(dbias_ref, jnp.sum(ddt_raw, axis=0, keepdims=True), first)
        _acc(dalog_ref, jnp.where(lane1 < SSD_HPG, dalog, 0.0), first)
        _acc(dd_ref, dd, first)

    return pl.pallas_call(
        body, name="ssd_scan_bwd", grid=(ng, nc),
        in_specs=[y_spec, chunk_grp, st_spec, col_form, row_form, col_par, col_par, col_par, row_par, row_par],
        out_specs=[chunk_grp, col_form, col_par, col_par, col_par],
        out_shape=[jax.ShapeDtypeStruct((t, ng * GC), F32), jax.ShapeDtypeStruct((ng, t, LANES), F32),
                   jax.ShapeDtypeStruct((ng, 1, LANES), F32), jax.ShapeDtypeStruct((ng, 1, LANES), F32),
                   jax.ShapeDtypeStruct((ng, 1, LANES), F32)],
        scratch_shapes=[pltpu.VMEM((GW, SSD_D_STATE), F32)],
        compiler_params=_params("parallel", "arbitrary"))(dy, pre, states, dtc, dtr, bias_c, alog_c, dsk_c,
                                                           bias_r, alog_r)


def _gate_norm_fwd(y, zx, norm_w):
    t, di = y.shape
    tr = _tile(t, 256, 8)
    ng = di // GW

    def body(y_ref, z_ref, w_ref, o_ref):
        z = z_ref[...]
        gate = y_ref[...] * (z * _sigmoid(z))
        w = w_ref[...]
        for g in range(ng):
            cols = slice(g * GW, (g + 1) * GW)
            gs = gate[:, cols]
            r = lax.rsqrt(jnp.mean(gs * gs, axis=-1, keepdims=True) + NORM_EPS)
            o_ref[:, cols] = (gs * r * w[:, cols]).astype(BF16)

    row = pl.BlockSpec((tr, di), lambda i: (i, 0))
    return pl.pallas_call(body, name="ssd_gate_norm_fwd", grid=(t // tr,),
                          in_specs=[row, row, pl.BlockSpec((1, di), lambda i: (0, 0))], out_specs=row,
                          out_shape=jax.ShapeDtypeStruct((t, di), BF16), compiler_params=_params("parallel"))(
                              y, zx, norm_w)


def _gate_norm_bwd(dyn, y, zx, norm_w):
    t, di = y.shape
    tr = _tile(t, 256, 8)
    ng = di // GW

    def body(d_ref, y_ref, z_ref, w_ref, dy_ref, dz_ref, dw_ref):
        z = z_ref[...]
        yv = y_ref[...]
        sg = _sigmoid(z)
        sz = z * sg
        gate = yv * sz
        w = w_ref[...]
        d = d_ref[...]
        dsz = _silu_grad(z, sg)
        dws = []
        for g in range(ng):
            cols = slice(g * GW, (g + 1) * GW)
            dg, dwr = _rms_bwd(gate[:, cols], w[:, cols], d[:, cols])
            dy_ref[:, cols] = dg * sz[:, cols]
            dz_ref[:, cols] = (dg * yv[:, cols] * dsz[:, cols]).astype(BF16)
            dws.append(jnp.sum(dwr, axis=0, keepdims=True))
        first = pl.program_id(0) == 0
        for g in range(ng):
            cols = slice(g * GW, (g + 1) * GW)

            @pl.when(first)
            def _():
                dw_ref[:, cols] = dws[g]

            @pl.when(jnp.logical_not(first))
            def _():
                dw_ref[:, cols] += dws[g]

    row = pl.BlockSpec((tr, di), lambda i: (i, 0))
    vec = pl.BlockSpec((1, di), lambda i: (0, 0))
    return pl.pallas_call(body, name="ssd_gate_norm_bwd", grid=(t // tr,), in_specs=[row, row, row, vec],
                          out_specs=[row, row, vec],
                          out_shape=[jax.ShapeDtypeStruct((t, di), F32), jax.ShapeDtypeStruct((t, di), BF16),
                                     jax.ShapeDtypeStruct((1, di), F32)],
                          compiler_params=_params("arbitrary"))(dyn, y, zx, norm_w)


def _attn_mask(n):
    w = ATTN_WINDOW
    qpos = lax.broadcasted_iota(jnp.int32, (w, 2 * w), 0) + w
    kpos = lax.broadcasted_iota(jnp.int32, (w, 2 * w), 1)
    rel = qpos - kpos
    return (rel >= 0) & (rel < w) & jnp.logical_not((n == 0) & (kpos < w))


def _attn_probs(qh, kbh, mask, sink):
    s = _dot_nt(qh, kbh) * (ATTN_HEAD_DIM ** -0.5)
    s = jnp.where(mask, s, -jnp.inf)
    m = jnp.maximum(jnp.max(s, axis=-1, keepdims=True), sink)
    e = jnp.exp(s - m)
    es = jnp.exp(sink - m)
    inv = 1.0 / (jnp.sum(e, axis=-1, keepdims=True) + es)
    return e * inv, es * inv


def _attn_fwd(qkv, sinks):
    t = qkv.shape[0]
    w, hd = ATTN_WINDOW, ATTN_HEAD_DIM
    kd = ATTN_N_KV * hd
    qd = ATTN_REP * kd
    nb = t // w

    def body(q_ref, kc_ref, vc_ref, kp_ref, vp_ref, s_ref, o_ref):
        n = pl.program_id(0)
        mask = _attn_mask(n)
        q = q_ref[...]
        kb = jnp.concatenate([kp_ref[...], kc_ref[...]], axis=0)
        vb = jnp.concatenate([vp_ref[...], vc_ref[...]], axis=0)
        sk = s_ref[...]
        for kv in range(ATTN_N_KV):
            kbh = kb[:, kv * hd:(kv + 1) * hd]
            vbh = vb[:, kv * hd:(kv + 1) * hd]
            for rep in range(ATTN_REP):
                h = kv * ATTN_REP + rep
                p, _ = _attn_probs(q[:, h * hd:(h + 1) * hd], kbh, mask, sk[:, h:h + 1])
                o_ref[:, h * hd:(h + 1) * hd] = _dot(p.astype(BF16), vbh).astype(BF16)

    prev = lambda n: jnp.maximum(n - 1, 0)
    return pl.pallas_call(
        body, name="attn_fwd", grid=(nb,),
        in_specs=[pl.BlockSpec((w, qd), lambda n: (n, 0)),
                  pl.BlockSpec((w, kd), lambda n: (n, ATTN_REP)),
                  pl.BlockSpec((w, kd), lambda n: (n, ATTN_REP + 1)),
                  pl.BlockSpec((w, kd), lambda n: (prev(n), ATTN_REP)),
                  pl.BlockSpec((w, kd), lambda n: (prev(n), ATTN_REP + 1)),
                  pl.BlockSpec((1, sinks.shape[1]), lambda n: (0, 0))],
        out_specs=pl.BlockSpec((w, qd), lambda n: (n, 0)),
        out_shape=jax.ShapeDtypeStruct((t, qd), BF16),
        compiler_params=_params("parallel"))(qkv, qkv, qkv, qkv, qkv, sinks)


def _attn_bwd(qkv, do, sinks):
    t = qkv.shape[0]
    w, hd = ATTN_WINDOW, ATTN_HEAD_DIM
    kd = ATTN_N_KV * hd
    qd = ATTN_REP * kd
    nq = ATTN_N_KV * ATTN_REP
    nb = t // w

    def body(q_ref, kc_ref, vc_ref, kp_ref, vp_ref, do_ref, s_ref,
             dq_ref, dk_ref, dv_ref, bq_ref, bk_ref, bv_ref, dsk_ref, ck_ref, cv_ref):
        n = pl.program_id(0)
        first = n == 0

        @pl.when(first)
        def _():
            ck_ref[...] = jnp.zeros_like(ck_ref)
            cv_ref[...] = jnp.zeros_like(cv_ref)
            bq_ref[...] = jnp.zeros_like(bq_ref)
            bk_ref[...] = jnp.zeros_like(bk_ref)
            bv_ref[...] = jnp.zeros_like(bv_ref)
            dsk_ref[...] = jnp.zeros_like(dsk_ref)

        @pl.when(n < nb)
        def _():
            mask = _attn_mask(n)
            q = q_ref[...]
            dov = do_ref[...]
            kb = jnp.concatenate([kp_ref[...], kc_ref[...]], axis=0)
            vb = jnp.concatenate([vp_ref[...], vc_ref[...]], axis=0)
            sk = s_ref[...]
            lane = lax.broadcasted_iota(jnp.int32, (1, nq), 1)
            dsk = jnp.zeros((1, nq), F32)
            dq_parts, dk_parts, dv_parts = [], [], []
            for kv in range(ATTN_N_KV):
                kbh = kb[:, kv * hd:(kv + 1) * hd]
                vbh = vb[:, kv * hd:(kv + 1) * hd]
                dkh = jnp.zeros((2 * w, hd), F32)
                dvh = jnp.zeros((2 * w, hd), F32)
                for rep in range(ATTN_REP):
                    h = kv * ATTN_REP + rep
                    qh = q[:, h * hd:(h + 1) * hd]
                    doh = dov[:, h * hd:(h + 1) * hd]
                    p, ps = _attn_probs(qh, kbh, mask, sk[:, h:h + 1])
                    pb = p.astype(BF16)
                    dp = _dot_nt(doh, vbh)
                    delta = jnp.sum(p * dp, axis=-1, keepdims=True)
                    dsc = (p * (dp - delta) * (hd ** -0.5)).astype(BF16)
                    dq_parts.append(_dot(dsc, kbh))
                    dkh = dkh + _dot_tn(dsc, qh)
                    dvh = dvh + _dot_tn(pb, doh)
                    dsk = jnp.where(lane == h, -jnp.sum(ps * delta, axis=0, keepdims=True), dsk)
                dk_parts.append(dkh)
                dv_parts.append(dvh)
            dq = jnp.concatenate(dq_parts, axis=1)
            dkb = jnp.concatenate(dk_parts, axis=1)
            dvb = jnp.concatenate(dv_parts, axis=1)
            dq_ref[...] = dq.astype(BF16)
            bq_ref[...] += jnp.sum(dq, axis=0, keepdims=True)
            dsk_ref[...] += dsk
            dk_prev = ck_ref[...] + dkb[:w, :]
            dv_prev = cv_ref[...] + dvb[:w, :]
            dk_ref[...] = dk_prev.astype(BF16)
            dv_ref[...] = dv_prev.astype(BF16)

            @pl.when(n > 0)
            def _():
                bk_ref[...] += jnp.sum(dk_prev, axis=0, keepdims=True)
                bv_ref[...] += jnp.sum(dv_prev, axis=0, keepdims=True)

            ck_ref[...] = dkb[w:, :]
            cv_ref[...] = dvb[w:, :]

        @pl.when(n == nb)
        def _():
            dk_ref[...] = ck_ref[...].astype(BF16)
            dv_ref[...] = cv_ref[...].astype(BF16)
            bk_ref[...] += jnp.sum(ck_ref[...], axis=0, keepdims=True)
            bv_ref[...] += jnp.sum(cv_ref[...], axis=0, keepdims=True)

    cur = lambda n: jnp.minimum(n, nb - 1)
    prev = lambda n: jnp.maximum(jnp.minimum(n, nb - 1) - 1, 0)
    late = lambda n: jnp.maximum(n - 1, 0)
    vec = lambda width: pl.BlockSpec((1, width), lambda n: (0, 0))
    return pl.pallas_call(
        body, name="attn_bwd", grid=(nb + 1,),
        in_specs=[pl.BlockSpec((w, qd), lambda n: (cur(n), 0)),
                  pl.BlockSpec((w, kd), lambda n: (cur(n), ATTN_REP)),
                  pl.BlockSpec((w, kd), lambda n: (cur(n), ATTN_REP + 1)),
                  pl.BlockSpec((w, kd), lambda n: (prev(n), ATTN_REP)),
                  pl.BlockSpec((w, kd), lambda n: (prev(n), ATTN_REP + 1)),
                  pl.BlockSpec((w, qd), lambda n: (cur(n), 0)),
                  vec(nq)],
        out_specs=[pl.BlockSpec((w, qd), lambda n: (cur(n), 0)),
                   pl.BlockSpec((w, kd), lambda n: (late(n), 0)),
                   pl.BlockSpec((w, kd), lambda n: (late(n), 0)),
                   vec(qd), vec(kd), vec(kd), vec(nq)],
        out_shape=[jax.ShapeDtypeStruct((t, qd), BF16), jax.ShapeDtypeStruct((t, kd), BF16),
                   jax.ShapeDtypeStruct((t, kd), BF16), jax.ShapeDtypeStruct((1, qd), F32),
                   jax.ShapeDtypeStruct((1, kd), F32), jax.ShapeDtypeStruct((1, kd), F32),
                   jax.ShapeDtypeStruct((1, nq), F32)],
        scratch_shapes=[pltpu.VMEM((w, kd), F32), pltpu.VMEM((w, kd), F32)],
        compiler_params=_params("arbitrary"))(qkv, qkv, qkv, qkv, qkv, do, sinks)


HBM_SPEC = pl.BlockSpec(memory_space=pl.ANY)


def _comm_call(name, body, ins, out_shapes, n_sems):
    return pl.pallas_call(
        body, name=name, in_specs=[HBM_SPEC] * len(ins), out_specs=[HBM_SPEC] * len(out_shapes),
        out_shape=out_shapes,
        scratch_shapes=[pltpu.SemaphoreType.DMA((s,)) for s in n_sems])(*ins)


def _all_gather(name, shards):
    n = len(shards)

    def body(*refs):
        x_refs, out_refs = refs[:n], refs[n:2 * n]
        send_sems, recv_sems, local_sems = refs[2 * n:]
        x, y, c = lax.axis_index("x"), lax.axis_index("y"), lax.axis_index("c")
        me, sibling = (x, y, c), (x, y, 1 - c)
        chips = [(1 - x, y), (x, 1 - y), (1 - x, 1 - y)]

        def slot(i, px, py, pc):
            return out_refs[i].at[4 * px + 2 * py + pc]

        def copy(k, i, block, to, src=None):
            return pltpu.make_async_remote_copy(
                src_ref=slot(i, *block) if src is None else src, dst_ref=slot(i, *block),
                send_sem=send_sems.at[k * n + i], recv_sem=recv_sems.at[k * n + i], device_id=to,
                device_id_type=MESH)

        mine = [pltpu.make_async_copy(x_refs[i], slot(i, *me), local_sems.at[i]) for i in range(n)]
        first = []
        for i in range(n):
            mine[i].start()
            first.append(copy(0, i, me, sibling, src=x_refs[i]))
            first += [copy(1 + j, i, me, (*chip, c), src=x_refs[i]) for j, chip in enumerate(chips)]
        for cp in first:
            cp.start()
        passed = []
        for i in range(n):
            for j, chip in enumerate(chips):
                copy(1 + j, i, (*chip, c), me).wait_recv()
                passed.append(copy(4 + j, i, (*chip, c), sibling))
                passed[-1].start()
        for i in range(n):
            copy(0, i, sibling, me).wait_recv()
            for j, chip in enumerate(chips):
                copy(4 + j, i, (*chip, 1 - c), me).wait_recv()
        for cp in first + passed:
            cp.wait_send()
        for cp in mine:
            cp.wait()

    outs = [jax.ShapeDtypeStruct((N_DEV,) + s.shape, s.dtype) for s in shards]
    return _comm_call(name, body, shards, outs, (7 * n, 7 * n, n))


def _pair_exchange(name, blocks):
    n = len(blocks)

    def body(*refs):
        g_refs, out_refs = refs[:n], refs[n:2 * n]
        send_sems, recv_sems = refs[2 * n:]
        x, y, c = lax.axis_index("x"), lax.axis_index("y"), lax.axis_index("c")
        copies = [pltpu.make_async_remote_copy(
            src_ref=g_refs[i].at[2 * k + 1 - c], dst_ref=out_refs[i].at[k], send_sem=send_sems.at[4 * i + k],
            recv_sem=recv_sems.at[4 * i + k], device_id=(x, y, 1 - c), device_id_type=MESH)
            for i in range(n) for k in range(4)]
        for cp in copies:
            cp.start()
        for cp in copies:
            cp.wait()

    outs = [jax.ShapeDtypeStruct((4,) + b.shape[1:], b.dtype) for b in blocks]
    return _comm_call(name, body, blocks, outs, (4 * n, 4 * n))


def _chip_exchange(name, blocks):
    n = len(blocks)

    def body(*refs):
        p_refs, out_refs = refs[:n], refs[n:2 * n]
        send_sems, recv_sems = refs[2 * n:]
        x, y, c = lax.axis_index("x"), lax.axis_index("y"), lax.axis_index("c")
        chips = [(1 - x, y), (x, 1 - y), (1 - x, 1 - y)]
        copies = [pltpu.make_async_remote_copy(
            src_ref=p_refs[i].at[j], dst_ref=out_refs[i].at[j], send_sem=send_sems.at[3 * i + j],
            recv_sem=recv_sems.at[3 * i + j], device_id=(*chip, c), device_id_type=MESH)
            for i in range(n) for j, chip in enumerate(chips)]
        for cp in copies:
            cp.start()
        for cp in copies:
            cp.wait()

    outs = [jax.ShapeDtypeStruct(b.shape, b.dtype) for b in blocks]
    return _comm_call(name, body, blocks, outs, (3 * n, 3 * n))


def _pair_sum(name, blocks, from_sibling, g_idx, r_idx):
    _, r, c_ = blocks.shape
    tr = _tile(r, 512, 16)

    def body(gi_ref, ri_ref, a_ref, b_ref, own_ref, send_ref):
        k = pl.program_id(1)
        s = a_ref[...] + b_ref[...]

        @pl.when(k == 0)
        def _():
            own_ref[...] = s

        @pl.when(k > 0)
        def _():
            send_ref[...] = s.astype(send_ref.dtype)

    return pl.pallas_call(
        body, name=name,
        grid_spec=pltpu.PrefetchScalarGridSpec(
            num_scalar_prefetch=2, grid=(r // tr, 4),
            in_specs=[pl.BlockSpec((None, tr, c_), lambda i, k, gi, ri: (gi[k], i, 0)),
                      pl.BlockSpec((None, tr, c_), lambda i, k, gi, ri: (ri[k], i, 0))],
            out_specs=[pl.BlockSpec((None, tr, c_), lambda i, k, gi, ri: (0, i, 0)),
                       pl.BlockSpec((None, tr, c_), lambda i, k, gi, ri: (jnp.maximum(k - 1, 0), i, 0))]),
        out_shape=[jax.ShapeDtypeStruct((1, r, c_), F32), jax.ShapeDtypeStruct((3, r, c_), PAYLOAD)],
        compiler_params=_params("parallel", "arbitrary"))(g_idx, r_idx, blocks, from_sibling)


def _adamw(w, g, m, v):
    m = ADAM_B1 * m + (1.0 - ADAM_B1) * g
    v = ADAM_B2 * v + (1.0 - ADAM_B2) * (g * g)
    m_hat = m / (1.0 - ADAM_B1 ** ADAM_STEP)
    v_hat = v / (1.0 - ADAM_B2 ** ADAM_STEP)
    delta = -ADAM_LR * (m_hat / (jnp.sqrt(v_hat) + ADAM_EPS) + ADAM_WD * w)
    return delta, m, v


def _sum_adamw(name, parts_f32, parts_lo, w, m, v):
    r, c_ = w.shape
    tr = _tile(r, 256, 16)
    k1 = parts_f32.shape[0]
    k2 = 0 if parts_lo is None else parts_lo.shape[0]

    def body(*refs):
        a_ref = refs[0]
        b_ref = refs[1] if k2 else None
        w_ref, m_ref, v_ref, g_ref, d_ref, nm_ref, nv_ref = refs[(2 if k2 else 1):]
        g = a_ref[0]
        for k in range(1, k1):
            g = g + a_ref[k]
        for k in range(k2):
            g = g + b_ref[k].astype(F32)
        g_ref[...] = g
        d_ref[...], nm_ref[...], nv_ref[...] = _adamw(w_ref[...], g, m_ref[...], v_ref[...])

    row = pl.BlockSpec((tr, c_), lambda i: (i, 0))
    ins = [parts_f32] + ([parts_lo] if k2 else []) + [w, m, v]
    in_specs = [pl.BlockSpec((k1, tr, c_), lambda i: (0, i, 0))]
    if k2:
        in_specs.append(pl.BlockSpec((k2, tr, c_), lambda i: (0, i, 0)))
    in_specs += [row, row, row]
    return pl.pallas_call(body, name=name, grid=(r // tr,), in_specs=in_specs, out_specs=[row] * 4,
                          out_shape=[jax.ShapeDtypeStruct((r, c_), F32)] * 4,
                          compiler_params=_params("parallel"))(*ins)


def _pack_rows(flat, n_rows, cols):
    pad = n_rows * cols - flat.shape[-1]
    flat = jnp.pad(flat, [(0, 0)] * (flat.ndim - 1) + [(0, pad)])
    return flat.reshape(flat.shape[:-1] + (n_rows, cols))


def _cols_join(blocks):
    return jnp.concatenate([blocks[d] for d in range(N_DEV)], axis=1)


def _cols_split(full):
    c = full.shape[1] // N_DEV
    return jnp.stack([full[:, d * c:(d + 1) * c] for d in range(N_DEV)])


def _rows_join(blocks):
    return blocks.reshape(N_DEV * blocks.shape[1], blocks.shape[2])


def _rows_split(full):
    return full.reshape(N_DEV, full.shape[0] // N_DEV, full.shape[1])


def _perm_xbc(a, ng):
    lead = a.shape[:-1]
    di, gn = ng * GW, ng * SSD_D_STATE
    xs = a[..., :di].reshape(lead + (ng, GW))
    bs = a[..., di:di + gn].reshape(lead + (ng, SSD_D_STATE))
    cs = a[..., di + gn:].reshape(lead + (ng, SSD_D_STATE))
    return jnp.concatenate([xs, bs, cs], axis=-1).reshape(lead + (ng * GC,))


def _unperm_xbc(a, ng):
    lead = a.shape[:-1]
    g = a.reshape(lead + (ng, GC))
    return jnp.concatenate([g[..., :GW].reshape(lead + (ng * GW,)),
                            g[..., GW:GW + SSD_D_STATE].reshape(lead + (ng * SSD_D_STATE,)),
                            g[..., GW + SSD_D_STATE:].reshape(lead + (ng * SSD_D_STATE,))], axis=-1)


def _heads_col(v, ng):
    return jnp.pad(v.reshape(ng, 1, SSD_HPG), ((0, 0), (0, 0), (0, LANES - SSD_HPG)))


def _heads_row(v, ng):
    return jnp.pad(v.reshape(ng, SSD_HPG, 1), ((0, 0), (0, 8 - SSD_HPG), (0, 0)))


MATRIX_ITEMS = ("w_in", "w_out", "up0", "down0", "w_qkv", "w_o", "up1", "down1")
VECTOR_ITEMS = ("conv_w", "b_qkv", "b_o")
ITEMS = MATRIX_ITEMS + VECTOR_ITEMS


def _items(tree, prefix=""):
    g = lambda k: tree[prefix + k]
    return {"w_in": g("ssd_w_in")[0], "w_out": g("ssd_w_out")[0], "w_qkv": g("attn_w_qkv")[0],
            "w_o": g("attn_w_o")[0], "up0": g("mlp_w_up")[0], "up1": g("mlp_w_up")[1],
            "down0": g("mlp_w_down")[0], "down1": g("mlp_w_down")[1], "conv_w": g("ssd_conv_w")[0],
            "b_qkv": g("attn_b_qkv"), "b_o": g("attn_b_o")}


def _from_items(it):
    return {"ssd_w_in": it["w_in"][None], "ssd_w_out": it["w_out"][None], "attn_w_qkv": it["w_qkv"][None],
            "attn_w_o": it["w_o"][None], "mlp_w_up": jnp.stack([it["up0"], it["up1"]]),
            "mlp_w_down": jnp.stack([it["down0"], it["down1"]]), "ssd_conv_w": it["conv_w"][None],
            "attn_b_qkv": it["b_qkv"], "attn_b_o": it["b_o"]}


REPLICATED = ("ssd_conv_b", "ssd_dt_bias", "ssd_a_log", "ssd_d", "ssd_norm_w", "attn_sinks", "mix_pre_norm",
              "mix_post_norm", "ffn_pre_norm", "ffn_post_norm")
WEIGHTS = ("ssd_w_in", "ssd_conv_w", "ssd_conv_b", "ssd_dt_bias", "ssd_a_log", "ssd_d", "ssd_norm_w", "ssd_w_out",
           "attn_w_qkv", "attn_b_qkv", "attn_sinks", "attn_w_o", "attn_b_o", "mlp_w_up", "mlp_w_down",
           "mix_pre_norm", "mix_post_norm", "ffn_pre_norm", "ffn_post_norm")


def _forward_backward(x, target, gathered, rep):
    t, d = x.shape
    ng = rep["ssd_norm_w"].shape[1] // GW
    di = ng * GW
    n_xbc = ng * GC
    nh = ng * SSD_HPG
    grads, blocks = {}, {}

    w_in = _cols_join(gathered["w_in"])
    w_z = w_in[:, :di]
    w_xbc = _perm_xbc(w_in[:, di:di + n_xbc], ng)
    w_dt = jnp.pad(w_in[:, di + n_xbc:], ((0, 0), (0, LANES - nh)))
    w_in_k = jnp.concatenate([w_z, w_xbc, w_dt], axis=1)
    n_in = w_in_k.shape[1]
    w_out = _rows_join(gathered["w_out"])
    w_qkv = _cols_join(gathered["w_qkv"])
    w_o = _rows_join(gathered["w_o"])
    w_up = [_cols_join(gathered["up0"]), _cols_join(gathered["up1"])]
    w_down = [_rows_join(gathered["down0"]), _rows_join(gathered["down1"])]
    b_qkv = _cols_join(gathered["b_qkv"])
    b_o = _cols_join(gathered["b_o"])
    conv_w = _perm_xbc(_cols_join(gathered["conv_w"]), ng)
    conv_b = _perm_xbc(rep["ssd_conv_b"], ng)
    bias_c, alog_c, dsk_c = (_heads_col(rep[k], ng) for k in ("ssd_dt_bias", "ssd_a_log", "ssd_d"))
    bias_r, alog_r = (_heads_row(rep[k], ng) for k in ("ssd_dt_bias", "ssd_a_log"))
    norm = {k: rep[k] for k in ("mix_pre_norm", "mix_post_norm", "ffn_pre_norm", "ffn_post_norm")}

    def nrow(name, i):
        return norm[name][i:i + 1]

    def mlp_fwd(i, u2):
        a, p = _mm(f"mlp{i}_up", [u2], [w_up[i]], "nn", tm=1024, tn=1024, out_dtypes=(F32, BF16),
                   epilogue=lambda acc: (acc, jnp.square(jnp.maximum(acc, 0.0))))
        f = _mm(f"mlp{i}_down", [p], [w_down[i]], "nn", tm=512, tn=1024)
        return a, p, f

    def mlp_bwd(i, df, u2, a, p):
        da = _mm(f"mlp{i}_dact", [df], [w_down[i]], "nt", tm=1024, tn=1024, out_dtypes=(BF16,),
                 tiles=(a,), epilogue=lambda acc, av: (acc * (2.0 * jnp.maximum(av, 0.0)),))
        blocks[f"down{i}"] = _rows_split(_mm(f"mlp{i}_dwdown", [p], [df], "tn", tm=512, tn=1024))
        blocks[f"up{i}"] = _mm(f"mlp{i}_dwup", [u2], [da], "tn", tm=1024, tn=da.shape[1] // N_DEV,
                               col_blocks=True)
        return _mm(f"mlp{i}_dx", [da], [w_up[i]], "nt", tm=512, tn=1024)

    u0 = _prenorm("l0_prenorm", x, nrow("mix_pre_norm", 0))
    zx = _mm("ssd_in_proj", [u0], [w_in_k], "nn", tm=1024, tn=_tile(n_in, 1024))
    pre = _conv_fwd(zx, di, n_xbc, conv_w, conv_b)
    dt_raw = zx[:, di + n_xbc:di + n_xbc + nh].reshape(t, ng, SSD_HPG)
    dtc = jnp.pad(jnp.transpose(dt_raw, (1, 0, 2)), ((0, 0), (0, 0), (0, LANES - SSD_HPG)))
    dtr = jnp.pad(jnp.transpose(dt_raw, (1, 2, 0)), ((0, 0), (0, 8 - SSD_HPG), (0, 0)))
    ssd_args = (dtc, dtr, bias_c, alog_c, dsk_c, bias_r, alog_r)
    y, states = _ssd_fwd(pre, *ssd_args)
    yn = _gate_norm_fwd(y, zx, rep["ssd_norm_w"])
    mix0 = _mm("ssd_out_proj", [yn], [w_out], "nn", tm=1024, tn=1024)
    h1, u0f = _post_pre("l0_mid", x, mix0, nrow("mix_post_norm", 0), nrow("ffn_pre_norm", 0))
    a0, p0, f0 = mlp_fwd(0, u0f)
    h2, u1 = _post_pre("l1_in", h1, f0, nrow("ffn_post_norm", 0), nrow("mix_pre_norm", 1))
    qkv = _mm("attn_qkv_proj", [u1], [w_qkv], "nn", tm=1024, tn=_tile(w_qkv.shape[1], 768), out_dtypes=(BF16,),
              rows=(b_qkv,), epilogue=lambda acc, b: (acc + b,))
    ao = _attn_fwd(qkv, rep["attn_sinks"])
    mix1 = _mm("attn_out_proj", [ao], [w_o], "nn", tm=1024, tn=1024, rows=(b_o,),
               epilogue=lambda acc, b: (acc + b,))
    h3, u1f = _post_pre("l1_mid", h2, mix1, nrow("mix_post_norm", 1), nrow("ffn_pre_norm", 1))
    a1, p1, f1 = mlp_fwd(1, u1f)
    dh, loss_row = _final_loss("loss", h3, f1, nrow("ffn_post_norm", 1), target)

    g_norm = {k: [None, None] for k in norm}
    df1, g_norm["ffn_post_norm"][1], _ = _norm_bwd("l1_ffn_post_bwd", dh, post=(f1, nrow("ffn_post_norm", 1)))
    du = mlp_bwd(1, df1, u1f, a1, p1)
    dh, g_norm["ffn_pre_norm"][1], dmix1, g_norm["mix_post_norm"][1], db_o = _norm_bwd(
        "l1_mid_bwd", dh, pre=(du, h3, nrow("ffn_pre_norm", 1)), post=(mix1, nrow("mix_post_norm", 1)))
    blocks["b_o"] = _cols_split(db_o)
    blocks["w_o"] = _rows_split(_mm("attn_dwo", [ao], [dmix1], "tn", tm=1024, tn=512))
    dao = _mm("attn_dout", [dmix1], [w_o], "nt", tm=1024, tn=1024, out_dtypes=(BF16,))
    dq, dk, dv, bq, bk, bv, dsinks = _attn_bwd(qkv, dao, rep["attn_sinks"])
    dqkv = jnp.concatenate([dq, dk, dv], axis=1)
    blocks["b_qkv"] = _cols_split(jnp.concatenate([bq, bk, bv], axis=1))
    grads["attn_sinks"] = dsinks
    blocks["w_qkv"] = _cols_split(_mm("attn_dwqkv", [u1], [dqkv], "tn", tm=1024, tn=512))
    du = _mm("attn_dx", [dqkv], [w_qkv], "nt", tm=1024, tn=1024)
    dh, g_norm["mix_pre_norm"][1], df0, g_norm["ffn_post_norm"][0], _ = _norm_bwd(
        "l1_in_bwd", dh, pre=(du, h2, nrow("mix_pre_norm", 1)), post=(f0, nrow("ffn_post_norm", 0)))
    du = mlp_bwd(0, df0, u0f, a0, p0)
    dh, g_norm["ffn_pre_norm"][0], dmix0, g_norm["mix_post_norm"][0], _ = _norm_bwd(
        "l0_mid_bwd", dh, pre=(du, h1, nrow("ffn_pre_norm", 0)), post=(mix0, nrow("mix_post_norm", 0)))
    blocks["w_out"] = _rows_split(_mm("ssd_dwout", [yn], [dmix0], "tn", tm=512, tn=1024))
    dyn = _mm("ssd_dyn", [dmix0], [w_out], "nt", tm=1024, tn=1024)
    dy, dz, grads["ssd_norm_w"] = _gate_norm_bwd(dyn, y, zx, rep["ssd_norm_w"])
    dpre, ddt_g, dbias_g, dalog_g, dd_g = _ssd_bwd(dy, pre, states, *ssd_args)
    dxbc, dconv_w, dconv_b = _conv_bwd(dpre, zx, di, conv_w)
    ddt = jnp.transpose(ddt_g[:, :, :SSD_HPG], (1, 0, 2)).reshape(t, nh)
    ddt = jnp.pad(ddt, ((0, 0), (0, LANES - nh))).astype(BF16)
    blocks["conv_w"] = _cols_split(_unperm_xbc(dconv_w, ng))
    grads["ssd_conv_b"] = _unperm_xbc(dconv_b, ng)
    for name, val in (("ssd_dt_bias", dbias_g), ("ssd_a_log", dalog_g), ("ssd_d", dd_g)):
        grads[name] = val[:, 0, :SSD_HPG].reshape(1, nh)
    dw_z = _mm("ssd_dwz", [u0], [dz], "tn", tm=1024, tn=512)
    dw_xbc = _mm("ssd_dwxbc", [u0], [dxbc], "tn", tm=1024, tn=512)
    dw_dt = _mm("ssd_dwdt", [u0], [ddt], "tn", tm=1024, tn=LANES)
    blocks["w_in"] = _cols_split(jnp.concatenate([dw_z, _unperm_xbc(dw_xbc, ng), dw_dt[:, :nh]], axis=1))
    du = _mm("ssd_dx", [dz, dxbc, ddt], [w_z, w_xbc, w_dt], "nt", tm=256, tn=1024)
    grad_x, g_norm["mix_pre_norm"][0] = _norm_bwd("l0_in_bwd", dh, pre=(du, x, nrow("mix_pre_norm", 0)))
    for k in norm:
        grads[k] = jnp.concatenate(g_norm[k], axis=0)
    return loss_row, grad_x, blocks, grads


def kernel(x, ssd_w_in, ssd_conv_w, ssd_conv_b, ssd_dt_bias, ssd_a_log, ssd_d, ssd_norm_w, ssd_w_out, attn_w_qkv, attn_b_qkv, attn_sinks, attn_w_o, attn_b_o, mlp_w_up, mlp_w_down, mix_pre_norm, mix_post_norm, ffn_pre_norm, ffn_post_norm, loss_target, m_ssd_w_in, m_ssd_conv_w, m_ssd_conv_b, m_ssd_dt_bias, m_ssd_a_log, m_ssd_d, m_ssd_norm_w, m_ssd_w_out, m_attn_w_qkv, m_attn_b_qkv, m_attn_sinks, m_attn_w_o, m_attn_b_o, m_mlp_w_up, m_mlp_w_down, m_mix_pre_norm, m_mix_post_norm, m_ffn_pre_norm, m_ffn_post_norm, v_ssd_w_in, v_ssd_conv_w, v_ssd_conv_b, v_ssd_dt_bias, v_ssd_a_log, v_ssd_d, v_ssd_norm_w, v_ssd_w_out, v_attn_w_qkv, v_attn_b_qkv, v_attn_sinks, v_attn_w_o, v_attn_b_o, v_mlp_w_up, v_mlp_w_down, v_mix_pre_norm, v_mix_post_norm, v_ffn_pre_norm, v_ffn_post_norm):
    given = dict(locals())
    w = {k: given[k] for k in WEIGHTS}
    mom_m = {k: given["m_" + k] for k in WEIGHTS}
    mom_v = {k: given["v_" + k] for k in WEIGHTS}
    w_it, m_it, v_it = _items(given), _items(given, "m_"), _items(given, "v_")

    shards = [w_it[k].astype(PAYLOAD) for k in MATRIX_ITEMS] + [w_it[k] for k in VECTOR_ITEMS]
    gathered = dict(zip(ITEMS, _all_gather("gather_weights", shards)))

    rep = {k: w[k] for k in REPLICATED}
    loss_row, grad_x, blocks, grads = _forward_backward(x[0], loss_target[0], gathered, rep)

    from_sibling = _pair_exchange("rs_pair_exchange", [blocks[k] for k in ITEMS])
    ix, iy, ic = lax.axis_index("x"), lax.axis_index("y"), lax.axis_index("c")
    chips = [(ix, iy), (1 - ix, iy), (ix, 1 - iy), (1 - ix, 1 - iy)]
    g_idx = jnp.stack([4 * cx + 2 * cy + ic for cx, cy in chips]).astype(jnp.int32)
    r_idx = jnp.stack([2 * cx + cy for cx, cy in chips]).astype(jnp.int32)
    sums = [_pair_sum(f"rs_pair_sum_{k}", blocks[k], fs, g_idx, r_idx) for k, fs in zip(ITEMS, from_sibling)]
    from_chips = _chip_exchange("rs_chip_exchange", [s[1] for s in sums])
    item_out = [_sum_adamw(f"adamw_{k}", s[0], fc, w_it[k], m_it[k], v_it[k])
                for k, s, fc in zip(ITEMS, sums, from_chips)]

    def pack_rep(tree, last):
        flat = jnp.concatenate([tree[k].reshape(-1) for k in REPLICATED] + [last])
        return _pack_rows(flat, _round_up(-(-flat.shape[0] // LANES), 8), LANES)

    partials, = _all_gather("gather_small_grads", [pack_rep(grads, loss_row[0, :1])])
    zero = jnp.zeros((1,), F32)
    rep_out = _sum_adamw("adamw_replicated", partials, None, pack_rep(w, zero), pack_rep(mom_m, zero),
                         pack_rep(mom_v, zero))

    kinds = []
    for kind, r_arr in enumerate(rep_out):
        tree = _from_items({k: out[kind] for k, out in zip(ITEMS, item_out)})
        flat, off = r_arr.reshape(-1), 0
        for k in REPLICATED:
            tree[k] = flat[off:off + w[k].size].reshape(w[k].shape)
            off += w[k].size
        kinds.append(tree)
    loss = rep_out[0].reshape(-1)[off]
    outs = [loss, grad_x[None]]
    for tree in kinds:
        outs += [tree[k] for k in WEIGHTS]
    return tuple(outs)
```

```python
import functools

import jax
import jax.numpy as jnp
from jax import lax
from jax.experimental import pallas as pl
from jax.experimental.pallas import tpu as pltpu

F32 = jnp.float32
BF16 = jnp.bfloat16
PAYLOAD = jnp.bfloat16
HIGHEST = lax.Precision.HIGHEST
MESH = pl.DeviceIdType.MESH

NORM_EPS = 1e-6
SSD_HEAD_DIM = 64
SSD_N_GROUPS = 8
SSD_HPG = 4
SSD_D_STATE = 128
SSD_CONV_WIDTH = 4
SSD_CHUNK = 128
ATTN_HEAD_DIM = 64
ATTN_N_KV = 4
ATTN_REP = 4
ATTN_WINDOW = 128
ADAM_LR = 0.001
ADAM_B1 = 0.9
ADAM_B2 = 0.999
ADAM_EPS = 1e-08
ADAM_WD = 0.01
ADAM_STEP = 10

N_DEV = 8
LANES = 128
PACK_COLS = 1024
V7X_VMEM_LIMIT = 56 * 1024 * 1024

GW = SSD_HPG * SSD_HEAD_DIM
GC = GW + 2 * SSD_D_STATE


def _params(*sem):
    return pltpu.CompilerParams(dimension_semantics=sem, vmem_limit_bytes=V7X_VMEM_LIMIT)


def _tile(n, pref, mult=LANES):
    best = None
    t = mult
    while t <= min(n, pref):
        if n % t == 0:
            best = t
        t += mult
    return best if best is not None else n


def _round_up(n, m):
    return (n + m - 1) // m * m


def _acc(ref, val, first):
    @pl.when(first)
    def _():
        ref[...] = val

    @pl.when(jnp.logical_not(first))
    def _():
        ref[...] += val


def _dot(a, b):
    return lax.dot_general(a, b, (((1,), (0,)), ((), ())), preferred_element_type=F32)


def _dot_nt(a, b):
    return lax.dot_general(a, b, (((1,), (1,)), ((), ())), preferred_element_type=F32)


def _dot_tn(a, b):
    return lax.dot_general(a, b, (((0,), (0,)), ((), ())), preferred_element_type=F32)


def _dot_f32(a, b):
    return lax.dot_general(a, b, (((1,), (0,)), ((), ())), preferred_element_type=F32, precision=HIGHEST)


_DOTS = {"nn": _dot, "nt": _dot_nt, "tn": _dot_tn}


def _sigmoid(x):
    return 1.0 / (1.0 + jnp.exp(-x))


def _softplus(x):
    return jnp.maximum(x, 0.0) + jnp.log1p(jnp.exp(-jnp.abs(x)))


def _silu_grad(x, s):
    return s * (1.0 + x * (1.0 - s))


def _mm(name, a_list, b_list, mode, *, tm, tn, out_dtypes=(F32,), epilogue=None, tiles=(), rows=(), cols=(),
        col_blocks=False):
    npair = len(a_list)
    if mode == "tn":
        m = a_list[0].shape[1]
    else:
        m = a_list[0].shape[0]
    n = b_list[0].shape[0] if mode == "nt" else b_list[0].shape[1]
    tm = _tile(m, tm, LANES if mode == "tn" else 8)
    tn = _tile(n, tn)
    assert m % tm == 0 and n % tn == 0, (name, m, n, tm, tn)
    dot = _DOTS[mode]

    def body(*refs):
        a_refs = refs[:npair]
        b_refs = refs[npair:2 * npair]
        n_extra = len(tiles) + len(rows) + len(cols)
        e_refs = refs[2 * npair:2 * npair + n_extra]
        o_refs = refs[2 * npair + n_extra:]
        acc = None
        for ar, br in zip(a_refs, b_refs):
            d = dot(ar[...], br[...])
            acc = d if acc is None else acc + d
        outs = epilogue(acc, *[e[...] for e in e_refs]) if epilogue is not None else (acc,)
        for o, v in zip(o_refs, outs):
            o[...] = v.astype(o.dtype)

    in_specs = []
    for a in a_list:
        if mode == "tn":
            in_specs.append(pl.BlockSpec((a.shape[0], tm), lambda i, j: (0, i)))
        else:
            in_specs.append(pl.BlockSpec((tm, a.shape[1]), lambda i, j: (i, 0)))
    for b in b_list:
        if mode == "nt":
            in_specs.append(pl.BlockSpec((tn, b.shape[1]), lambda i, j: (j, 0)))
        else:
            in_specs.append(pl.BlockSpec((b.shape[0], tn), lambda i, j: (0, j)))
    in_specs += [pl.BlockSpec((tm, tn), lambda i, j: (i, j)) for _ in tiles]
    in_specs += [pl.BlockSpec((1, tn), lambda i, j: (0, j)) for _ in rows]
    in_specs += [pl.BlockSpec((tm, 1), lambda i, j: (i, 0)) for _ in cols]
    outs = pl.pallas_call(
        body,
        name=name,
        grid=(m // tm, n // tn),
        in_specs=in_specs,
        out_specs=[pl.BlockSpec((None, tm, tn), lambda i, j: (j, i, 0)) if col_blocks else
                   pl.BlockSpec((tm, tn), lambda i, j: (i, j)) for _ in out_dtypes],
        out_shape=[jax.ShapeDtypeStruct((n // tn, m, tn) if col_blocks else (m, n), dt) for dt in out_dtypes],
        compiler_params=_params("parallel", "parallel"),
    )(*a_list, *b_list, *tiles, *rows, *cols)
    return outs[0] if len(out_dtypes) == 1 else outs


def _rms(x, w):
    r = lax.rsqrt(jnp.mean(x * x, axis=-1, keepdims=True) + NORM_EPS)
    return x * r * w


def _rms_bwd(x, w, dy):
    r = lax.rsqrt(jnp.mean(x * x, axis=-1, keepdims=True) + NORM_EPS)
    xh = x * r
    g = dy * w
    dx = r * (g - xh * jnp.mean(g * xh, axis=-1, keepdims=True))
    return dx, dy * xh


def _row_specs(tr, d):
    return pl.BlockSpec((tr, d), lambda i: (i, 0)), pl.BlockSpec((1, d), lambda i: (0, 0))


def _prenorm(name, h, w):
    t, d = h.shape
    tr = _tile(t, 512, 8)
    row, vec = _row_specs(tr, d)

    def body(h_ref, w_ref, u_ref):
        u_ref[...] = _rms(h_ref[...], w_ref[...]).astype(BF16)

    return pl.pallas_call(body, name=name, grid=(t // tr,), in_specs=[row, vec], out_specs=row,
                          out_shape=jax.ShapeDtypeStruct((t, d), BF16), compiler_params=_params("parallel"))(h, w)


def _post_pre(name, h, m, w_post, w_pre):
    t, d = h.shape
    tr = _tile(t, 512, 8)
    row, vec = _row_specs(tr, d)

    def body(h_ref, m_ref, wq_ref, wp_ref, hn_ref, u_ref):
        hn = h_ref[...] + _rms(m_ref[...], wq_ref[...])
        hn_ref[...] = hn
        u_ref[...] = _rms(hn, wp_ref[...]).astype(BF16)

    return pl.pallas_call(body, name=name, grid=(t // tr,), in_specs=[row, row, vec, vec], out_specs=[row, row],
                          out_shape=[jax.ShapeDtypeStruct((t, d), F32), jax.ShapeDtypeStruct((t, d), BF16)],
                          compiler_params=_params("parallel"))(h, m, w_post, w_pre)


def _final_loss(name, h, m, w_post, target):
    t, d = h.shape
    tr = _tile(t, 512, 8)
    row, vec = _row_specs(tr, d)

    def body(h_ref, m_ref, wq_ref, t_ref, dh_ref, loss_ref):
        err = h_ref[...] + _rms(m_ref[...], wq_ref[...]) - t_ref[...]
        dh_ref[...] = err * (1.0 / d)
        part = 0.5 * jnp.sum(jnp.mean(err * err, axis=-1, keepdims=True), axis=0, keepdims=True)
        _acc(loss_ref, jnp.broadcast_to(part, (1, LANES)), pl.program_id(0) == 0)

    return pl.pallas_call(body, name=name, grid=(t // tr,), in_specs=[row, row, vec, row],
                          out_specs=[row, pl.BlockSpec((1, LANES), lambda i: (0, 0))],
                          out_shape=[jax.ShapeDtypeStruct((t, d), F32), jax.ShapeDtypeStruct((1, LANES), F32)],
                          compiler_params=_params("arbitrary"))(h, m, w_post, target)


def _norm_bwd(name, dh, pre=None, post=None):
    t, d = dh.shape
    tr = _tile(t, 256, 8)
    row, vec = _row_specs(tr, d)
    has_pre, has_post = pre is not None, post is not None

    def body(*refs):
        it = iter(refs)
        dh_ref = next(it)
        if has_pre:
            du_ref, x_ref, wp_ref = next(it), next(it), next(it)
        if has_post:
            m_ref, wq_ref = next(it), next(it)
        first = pl.program_id(0) == 0
        dh_v = dh_ref[...]
        if has_pre:
            dhn_ref, dwp_ref = next(it), next(it)
            dx, dwr = _rms_bwd(x_ref[...], wp_ref[...], du_ref[...])
            dh_v = dh_v + dx
            dhn_ref[...] = dh_v
            _acc(dwp_ref, jnp.sum(dwr, axis=0, keepdims=True), first)
        if has_post:
            dm_ref, dwq_ref, dms_ref = next(it), next(it), next(it)
            dm, dwr = _rms_bwd(m_ref[...], wq_ref[...], dh_v)
            dm_ref[...] = dm.astype(BF16)
            _acc(dwq_ref, jnp.sum(dwr, axis=0, keepdims=True), first)
            _acc(dms_ref, jnp.sum(dm, axis=0, keepdims=True), first)

    ins, in_specs, out_specs, out_shape = [dh], [row], [], []
    if has_pre:
        ins += list(pre)
        in_specs += [row, row, vec]
        out_specs += [row, vec]
        out_shape += [jax.ShapeDtypeStruct((t, d), F32), jax.ShapeDtypeStruct((1, d), F32)]
    if has_post:
        ins += list(post)
        in_specs += [row, vec]
        out_specs += [row, vec, vec]
        out_shape += [jax.ShapeDtypeStruct((t, d), BF16), jax.ShapeDtypeStruct((1, d), F32),
                      jax.ShapeDtypeStruct((1, d), F32)]
    return pl.pallas_call(body, name=name, grid=(t // tr,), in_specs=in_specs, out_specs=out_specs,
                          out_shape=out_shape, compiler_params=_params("arbitrary"))(*ins)


HALO = 8


def _conv_fwd(zx, col0, n_ch, conv_w, conv_b):
    t = zx.shape[0]
    tc = _tile(n_ch, 512)
    tt = _tile(t, 512, 8)
    cb0 = col0 // tc
    assert col0 % tc == 0
    kw = SSD_CONV_WIDTH

    def body(x_ref, p_ref, w_ref, b_ref, o_ref, xe_ref):
        i = pl.program_id(1)
        cur = x_ref[...]
        xe_ref[0:HALO, :] = jnp.where(i > 0, p_ref[...], 0.0)
        xe_ref[HALO:HALO + tt, :] = cur
        w = w_ref[...]
        acc = b_ref[...] + w[kw - 1:kw, :] * cur
        for k in range(kw - 1):
            acc = acc + w[k:k + 1, :] * xe_ref[pl.ds(HALO - (kw - 1) + k, tt), :]
        o_ref[...] = acc

    return pl.pallas_call(
        body, name="ssd_conv_fwd", grid=(n_ch // tc, t // tt),
        in_specs=[pl.BlockSpec((tt, tc), lambda j, i: (i, cb0 + j)),
                  pl.BlockSpec((HALO, tc), lambda j, i: (jnp.maximum(i * (tt // HALO) - 1, 0), cb0 + j)),
                  pl.BlockSpec((kw, tc), lambda j, i: (0, j)),
                  pl.BlockSpec((1, tc), lambda j, i: (0, j))],
        out_specs=pl.BlockSpec((tt, tc), lambda j, i: (i, j)),
        out_shape=jax.ShapeDtypeStruct((t, n_ch), F32),
        scratch_shapes=[pltpu.VMEM((tt + HALO, tc), F32)],
        compiler_params=_params("parallel", "parallel"))(zx, zx, conv_w, conv_b)


def _conv_bwd(dpre, zx, col0, conv_w):
    t, n_ch = dpre.shape
    tc = _tile(n_ch, 512)
    tt = _tile(t, 512, 8)
    cb0 = col0 // tc
    kw = SSD_CONV_WIDTH
    nt = t // tt

    def body(d_ref, dn_ref, x_ref, p_ref, w_ref, dx_ref, dw_ref, db_ref, de_ref, xe_ref):
        i = pl.program_id(1)
        d = d_ref[...]
        de_ref[0:tt, :] = d
        de_ref[tt:tt + HALO, :] = jnp.where(i < nt - 1, dn_ref[...], 0.0)
        xe_ref[0:HALO, :] = jnp.where(i > 0, p_ref[...], 0.0)
        xe_ref[HALO:HALO + tt, :] = x_ref[...]
        w = w_ref[...]
        dx = w[kw - 1:kw, :] * d
        for k in range(kw - 1):
            dx = dx + w[k:k + 1, :] * de_ref[pl.ds(kw - 1 - k, tt), :]
        dx_ref[...] = dx.astype(BF16)
        first = i == 0
        for k in range(kw):
            xs = xe_ref[pl.ds(HALO - (kw - 1) + k, tt), :]
            val = jnp.sum(d * xs, axis=0, keepdims=True)

            @pl.when(first)
            def _():
                dw_ref[k:k + 1, :] = val

            @pl.when(jnp.logical_not(first))
            def _():
                dw_ref[k:k + 1, :] += val
        _acc(db_ref, jnp.sum(d, axis=0, keepdims=True), first)

    return pl.pallas_call(
        body, name="ssd_conv_bwd", grid=(n_ch // tc, nt),
        in_specs=[pl.BlockSpec((tt, tc), lambda j, i: (i, j)),
                  pl.BlockSpec((HALO, tc), lambda j, i: (jnp.minimum((i + 1) * (tt // HALO), t // HALO - 1), j)),
                  pl.BlockSpec((tt, tc), lambda j, i: (i, cb0 + j)),
                  pl.BlockSpec((HALO, tc), lambda j, i: (jnp.maximum(i * (tt // HALO) - 1, 0), cb0 + j)),
                  pl.BlockSpec((kw, tc), lambda j, i: (0, j))],
        out_specs=[pl.BlockSpec((tt, tc), lambda j, i: (i, j)),
                   pl.BlockSpec((kw, tc), lambda j, i: (0, j)),
                   pl.BlockSpec((1, tc), lambda j, i: (0, j))],
        out_shape=[jax.ShapeDtypeStruct((t, n_ch), BF16), jax.ShapeDtypeStruct((kw, n_ch), F32),
                   jax.ShapeDtypeStruct((1, n_ch), F32)],
        scratch_shapes=[pltpu.VMEM((tt + HALO, tc), F32), pltpu.VMEM((tt + HALO, tc), F32)],
        compiler_params=_params("parallel", "arbitrary"))(dpre, dpre, zx, zx, conv_w)


def _head_of_lane(shape, width):
    return lax.broadcasted_iota(jnp.int32, shape, len(shape) - 1) // width


def _expand(v, n_rows):
    head = _head_of_lane((n_rows, GW), SSD_HEAD_DIM)
    out = jnp.zeros((n_rows, GW), F32)
    for j in range(SSD_HPG):
        out = jnp.where(head == j, v[:, j:j + 1], out)
    return out


def _contract(v, n_rows):
    head = _head_of_lane((n_rows, GW), SSD_HEAD_DIM)
    lane = lax.broadcasted_iota(jnp.int32, (n_rows, LANES), 1)
    out = jnp.zeros((n_rows, LANES), F32)
    for j in range(SSD_HPG):
        s = jnp.sum(jnp.where(head == j, v, 0.0), axis=1, keepdims=True)
        out = jnp.where(lane == j, s, out)
    return out


def _ssd_common(pre, dtc, bias_c, alog_c, dtr, bias_r, alog_r):
    q = SSD_CHUNK
    sg = _sigmoid(pre)
    act = pre * sg
    xa = act[:, :GW]
    bm = act[:, GW:GW + SSD_D_STATE].astype(BF16)
    cm = act[:, GW + SSD_D_STATE:].astype(BF16)
    row = lax.broadcasted_iota(jnp.int32, (q, q), 0)
    col = lax.broadcasted_iota(jnp.int32, (q, q), 1)
    tril = col <= row
    dt = _softplus(dtc + bias_c)
    a_c = -jnp.exp(alog_c)
    cum = _dot_f32(tril.astype(F32), dt * a_c)
    dt_r = _softplus(dtr + bias_r)
    cum_r = _dot_f32(dt_r * (-jnp.exp(alog_r)), (row <= col).astype(F32))
    g = _dot_nt(cm, bm)
    dt_x = _expand(dt, q)
    xdt = xa * dt_x
    cl = cum[q - 1:q, :]
    e_c = jnp.exp(cl - cum)
    lam_c = jnp.exp(cum)
    return dict(sg=sg, xa=xa, bm=bm, cm=cm, tril=tril, row=row, col=col, dt=dt, a_c=a_c, cum=cum, cum_r=cum_r,
                g=g, dt_x=dt_x, xdt=xdt, cl=cl, e_c=e_c, lam_c=lam_c)


def _ssd_specs(nc, rev):
    q = SSD_CHUNK

    def ch(c):
        return nc - 1 - c if rev else c

    chunk_grp = pl.BlockSpec((q, GC), lambda g, c: (ch(c), g))
    col_form = pl.BlockSpec((None, q, LANES), lambda g, c: (g, ch(c), 0))
    row_form = pl.BlockSpec((None, 8, q), lambda g, c: (g, 0, ch(c)))
    col_par = pl.BlockSpec((None, 1, LANES), lambda g, c: (g, 0, 0))
    row_par = pl.BlockSpec((None, 8, 1), lambda g, c: (g, 0, 0))
    y_spec = pl.BlockSpec((q, GW), lambda g, c: (ch(c), g))
    st_spec = pl.BlockSpec((None, None, GW, SSD_D_STATE), lambda g, c: (g, ch(c), 0, 0))
    return chunk_grp, col_form, row_form, col_par, row_par, y_spec, st_spec


def _ssd_fwd(pre, dtc, dtr, bias_c, alog_c, dsk_c, bias_r, alog_r):
    t = pre.shape[0]
    ng = pre.shape[1] // GC
    q = SSD_CHUNK
    nc = t // q
    chunk_grp, col_form, row_form, col_par, row_par, y_spec, st_spec = _ssd_specs(nc, False)

    def body(pre_ref, dtc_ref, dtr_ref, bc_ref, ac_ref, dk_ref, br_ref, ar_ref, y_ref, sp_ref, st_ref):
        @pl.when(pl.program_id(1) == 0)
        def _():
            st_ref[...] = jnp.zeros_like(st_ref)

        v = _ssd_common(pre_ref[...], dtc_ref[...], bc_ref[...], ac_ref[...], dtr_ref[...], br_ref[...], ar_ref[...])
        s0 = st_ref[...]
        sp_ref[...] = s0
        r = _dot_nt(v["cm"], s0.astype(BF16))
        y = _expand(v["lam_c"], q) * r + _expand(dk_ref[...], 1) * v["xa"]
        head = _head_of_lane((q, GW), SSD_HEAD_DIM)
        for j in range(SSD_HPG):
            diff = v["cum"][:, j:j + 1] - v["cum_r"][j:j + 1, :]
            w = (v["g"] * jnp.exp(jnp.where(v["tril"], diff, -jnp.inf))).astype(BF16)
            y = y + _dot(w, jnp.where(head == j, v["xdt"], 0.0).astype(BF16))
        y_ref[...] = y
        ds = _dot_tn((v["xdt"] * _expand(v["e_c"], q)).astype(BF16), v["bm"])
        for j in range(SSD_HPG):
            rows = slice(j * SSD_HEAD_DIM, (j + 1) * SSD_HEAD_DIM)
            st_ref[rows, :] = s0[rows, :] * jnp.exp(v["cum_r"][j:j + 1, q - 1:q]) + ds[rows, :]

    return pl.pallas_call(
        body, name="ssd_scan_fwd", grid=(ng, nc),
        in_specs=[chunk_grp, col_form, row_form, col_par, col_par, col_par, row_par, row_par],
        out_specs=[y_spec, st_spec],
        out_shape=[jax.ShapeDtypeStruct((t, ng * GW), F32), jax.ShapeDtypeStruct((ng, nc, GW, SSD_D_STATE), F32)],
        scratch_shapes=[pltpu.VMEM((GW, SSD_D_STATE), F32)],
        compiler_params=_params("parallel", "arbitrary"))(pre, dtc, dtr, bias_c, alog_c, dsk_c, bias_r, alog_r)


def _ssd_bwd(dy, pre, states, dtc, dtr, bias_c, alog_c, dsk_c, bias_r, alog_r):
    t = pre.shape[0]
    ng = pre.shape[1] // GC
    q = SSD_CHUNK
    nc = t // q
    chunk_grp, col_form, row_form, col_par, row_par, y_spec, st_spec = _ssd_specs(nc, True)

    def body(dy_ref, pre_ref, sp_ref, dtc_ref, dtr_ref, bc_ref, ac_ref, dk_ref, br_ref, ar_ref,
             dpre_ref, ddt_ref, dbias_ref, dalog_ref, dd_ref, ds_ref):
        first = pl.program_id(1) == 0

        @pl.when(first)
        def _():
            ds_ref[...] = jnp.zeros_like(ds_ref)

        pre_v = pre_ref[...]
        v = _ssd_common(pre_v, dtc_ref[...], bc_ref[...], ac_ref[...], dtr_ref[...], br_ref[...], ar_ref[...])
        xa, bm, cm, xdt, cum, cum_r = v["xa"], v["bm"], v["cm"], v["xdt"], v["cum"], v["cum_r"]
        xdt_b = xdt.astype(BF16)
        dy_v = dy_ref[...]
        s0 = sp_ref[...]
        ds1 = ds_ref[...]
        s0b, ds1b = s0.astype(BF16), ds1.astype(BF16)
        head = _head_of_lane((q, GW), SSD_HEAD_DIM)
        lane = lax.broadcasted_iota(jnp.int32, (q, LANES), 1)
        lane1 = lax.broadcasted_iota(jnp.int32, (1, LANES), 1)
        lam_x = _expand(v["lam_c"], q)
        e_x = _expand(v["e_c"], q)

        dxa = _expand(dk_ref[...], 1) * dy_v
        dd = _contract(jnp.sum(dy_v * xa, axis=0, keepdims=True), 1)
        r = _dot_nt(cm, s0b)
        dcum = _contract(dy_v * r * lam_x, q)
        drb = (lam_x * dy_v).astype(BF16)
        dc = _dot(drb, s0b)
        ds0 = _dot_tn(drb, cm)
        extra = jnp.zeros((1, LANES), F32)
        for j in range(SSD_HPG):
            rows = slice(j * SSD_HEAD_DIM, (j + 1) * SSD_HEAD_DIM)
            lam_last = jnp.exp(cum_r[j:j + 1, q - 1:q])
            ds_ref[rows, :] = ds0[rows, :] + lam_last * ds1[rows, :]
            tot = jnp.sum(jnp.sum(ds1[rows, :] * s0[rows, :], axis=1, keepdims=True), axis=0, keepdims=True)
            extra = jnp.where(lane1 == j, lam_last * tot, extra)
        dv = _dot_nt(bm, ds1b)
        db = _dot((xdt * e_x).astype(BF16), ds1b)
        dxdt = e_x * dv
        dee = _contract(dv * xdt, q) * v["e_c"]
        dcum = dcum - dee
        extra = extra + jnp.sum(dee, axis=0, keepdims=True)
        dg = jnp.zeros((q, q), F32)
        for j in range(SSD_HPG):
            diff = cum[:, j:j + 1] - cum_r[j:j + 1, :]
            el = jnp.exp(jnp.where(v["tril"], diff, -jnp.inf))
            gl = v["g"] * el
            dym = jnp.where(head == j, dy_v, 0.0).astype(BF16)
            dwm = _dot_nt(dym, xdt_b)
            dxdt = dxdt + _dot_tn(gl.astype(BF16), dym)
            z = dwm * gl
            rk = jnp.sum(z, axis=1, keepdims=True) - jnp.sum(z.T, axis=1, keepdims=True)
            dcum = jnp.where(lane == j, dcum + rk, dcum)
            dg = dg + dwm * el
        dgb = dg.astype(BF16)
        dc = dc + _dot(dgb, bm)
        db = db + _dot_tn(dgb, cm)
        da = _dot_f32((v["row"] <= v["col"]).astype(F32), dcum) + extra
        ddt = _contract(dxdt * xa, q) + v["a_c"] * da
        dalog = jnp.sum(v["dt"] * da, axis=0, keepdims=True) * v["a_c"]
        dxa = dxa + v["dt_x"] * dxdt
        ddt_raw = jnp.where(lane < SSD_HPG, ddt * _sigmoid(dtc_ref[...] + bc_ref[...]), 0.0)
        sgrad = _silu_grad(pre_v, v["sg"])
        dpre_ref[:, :GW] = dxa * sgrad[:, :GW]
        dpre_ref[:, GW:GW + SSD_D_STATE] = db * sgrad[:, GW:GW + SSD_D_STATE]
        dpre_ref[:, GW + SSD_D_STATE:] = dc * sgrad[:, GW + SSD_D_STATE:]
        ddt_ref[...] = ddt_raw
        _acc(dbias_ref, jnp.sum(ddt_raw, axis=0, keepdims=True), first)
        _acc(dalog_ref, jnp.where(lane1 < SSD_HPG, dalog, 0.0), first)
        _acc(dd_ref, dd, first)

    return pl.pallas_call(
        body, name="ssd_scan_bwd", grid=(ng, nc),
        in_specs=[y_spec, chunk_grp, st_spec, col_form, row_form, col_par, col_par, col_par, row_par, row_par],
        out_specs=[chunk_grp, col_form, col_par, col_par, col_par],
        out_shape=[jax.ShapeDtypeStruct((t, ng * GC), F32), jax.ShapeDtypeStruct((ng, t, LANES), F32),
                   jax.ShapeDtypeStruct((ng, 1, LANES), F32), jax.ShapeDtypeStruct((ng, 1, LANES), F32),
                   jax.ShapeDtypeStruct((ng, 1, LANES), F32)],
        scratch_shapes=[pltpu.VMEM((GW, SSD_D_STATE), F32)],
        compiler_params=_params("parallel", "arbitrary"))(dy, pre, states, dtc, dtr, bias_c, alog_c, dsk_c,
                                                           bias_r, alog_r)


def _gate_norm_fwd(y, zx, norm_w):
    t, di = y.shape
    tr = _tile(t, 256, 8)
    ng = di // GW

    def body(y_ref, z_ref, w_ref, o_ref):
        z = z_ref[...]
        gate = y_ref[...] * (z * _sigmoid(z))
        w = w_ref[...]
        for g in range(ng):
            cols = slice(g * GW, (g + 1) * GW)
            gs = gate[:, cols]
            r = lax.rsqrt(jnp.mean(gs * gs, axis=-1, keepdims=True) + NORM_EPS)
            o_ref[:, cols] = (gs * r * w[:, cols]).astype(BF16)

    row = pl.BlockSpec((tr, di), lambda i: (i, 0))
    return pl.pallas_call(body, name="ssd_gate_norm_fwd", grid=(t // tr,),
                          in_specs=[row, row, pl.BlockSpec((1, di), lambda i: (0, 0))], out_specs=row,
                          out_shape=jax.ShapeDtypeStruct((t, di), BF16), compiler_params=_params("parallel"))(
                              y, zx, norm_w)


def _gate_norm_bwd(dyn, y, zx, norm_w):
    t, di = y.shape
    tr = _tile(t, 256, 8)
    ng = di // GW

    def body(d_ref, y_ref, z_ref, w_ref, dy_ref, dz_ref, dw_ref):
        z = z_ref[...]
        yv = y_ref[...]
        sg = _sigmoid(z)
        sz = z * sg
        gate = yv * sz
        w = w_ref[...]
        d = d_ref[...]
        dsz = _silu_grad(z, sg)
        dws = []
        for g in range(ng):
            cols = slice(g * GW, (g + 1) * GW)
            dg, dwr = _rms_bwd(gate[:, cols], w[:, cols], d[:, cols])
            dy_ref[:, cols] = dg * sz[:, cols]
            dz_ref[:, cols] = (dg * yv[:, cols] * dsz[:, cols]).astype(BF16)
            dws.append(jnp.sum(dwr, axis=0, keepdims=True))
        first = pl.program_id(0) == 0
        for g in range(ng):
            cols = slice(g * GW, (g + 1) * GW)

            @pl.when(first)
            def _():
                dw_ref[:, cols] = dws[g]

            @pl.when(jnp.logical_not(first))
            def _():
                dw_ref[:, cols] += dws[g]

    row = pl.BlockSpec((tr, di), lambda i: (i, 0))
    vec = pl.BlockSpec((1, di), lambda i: (0, 0))
    return pl.pallas_call(body, name="ssd_gate_norm_bwd", grid=(t // tr,), in_specs=[row, row, row, vec],
                          out_specs=[row, row, vec],
                          out_shape=[jax.ShapeDtypeStruct((t, di), F32), jax.ShapeDtypeStruct((t, di), BF16),
                                     jax.ShapeDtypeStruct((1, di), F32)],
                          compiler_params=_params("arbitrary"))(dyn, y, zx, norm_w)


def _attn_mask(n):
    w = ATTN_WINDOW
    qpos = lax.broadcasted_iota(jnp.int32, (w, 2 * w), 0) + w
    kpos = lax.broadcasted_iota(jnp.int32, (w, 2 * w), 1)
    rel = qpos - kpos
    return (rel >= 0) & (rel < w) & jnp.logical_not((n == 0) & (kpos < w))


def _attn_probs(qh, kbh, mask, sink):
    s = _dot_nt(qh, kbh) * (ATTN_HEAD_DIM ** -0.5)
    s = jnp.where(mask, s, -jnp.inf)
    m = jnp.maximum(jnp.max(s, axis=-1, keepdims=True), sink)
    e = jnp.exp(s - m)
    es = jnp.exp(sink - m)
    inv = 1.0 / (jnp.sum(e, axis=-1, keepdims=True) + es)
    return e * inv, es * inv


def _attn_fwd(qkv, sinks):
    t = qkv.shape[0]
    w, hd = ATTN_WINDOW, ATTN_HEAD_DIM
    kd = ATTN_N_KV * hd
    qd = ATTN_REP * kd
    nb = t // w

    def body(q_ref, kc_ref, vc_ref, kp_ref, vp_ref, s_ref, o_ref):
        n = pl.program_id(0)
        mask = _attn_mask(n)
        q = q_ref[...]
        kb = jnp.concatenate([kp_ref[...], kc_ref[...]], axis=0)
        vb = jnp.concatenate([vp_ref[...], vc_ref[...]], axis=0)
        sk = s_ref[...]
        for kv in range(ATTN_N_KV):
            kbh = kb[:, kv * hd:(kv + 1) * hd]
            vbh = vb[:, kv * hd:(kv + 1) * hd]
            for rep in range(ATTN_REP):
                h = kv * ATTN_REP + rep
                p, _ = _attn_probs(q[:, h * hd:(h + 1) * hd], kbh, mask, sk[:, h:h + 1])
                o_ref[:, h * hd:(h + 1) * hd] = _dot(p.astype(BF16), vbh).astype(BF16)

    prev = lambda n: jnp.maximum(n - 1, 0)
    return pl.pallas_call(
        body, name="attn_fwd", grid=(nb,),
        in_specs=[pl.BlockSpec((w, qd), lambda n: (n, 0)),
                  pl.BlockSpec((w, kd), lambda n: (n, ATTN_REP)),
                  pl.BlockSpec((w, kd), lambda n: (n, ATTN_REP + 1)),
                  pl.BlockSpec((w, kd), lambda n: (prev(n), ATTN_REP)),
                  pl.BlockSpec((w, kd), lambda n: (prev(n), ATTN_REP + 1)),
                  pl.BlockSpec((1, sinks.shape[1]), lambda n: (0, 0))],
        out_specs=pl.BlockSpec((w, qd), lambda n: (n, 0)),
        out_shape=jax.ShapeDtypeStruct((t, qd), BF16),
        compiler_params=_params("parallel"))(qkv, qkv, qkv, qkv, qkv, sinks)


def _attn_bwd(qkv, do, sinks):
    t = qkv.shape[0]
    w, hd = ATTN_WINDOW, ATTN_HEAD_DIM
    kd = ATTN_N_KV * hd
    qd = ATTN_REP * kd
    nq = ATTN_N_KV * ATTN_REP
    nb = t // w

    def body(q_ref, kc_ref, vc_ref, kp_ref, vp_ref, do_ref, s_ref,
             dq_ref, dk_ref, dv_ref, bq_ref, bk_ref, bv_ref, dsk_ref, ck_ref, cv_ref):
        n = pl.program_id(0)
        first = n == 0

        @pl.when(first)
        def _():
            ck_ref[...] = jnp.zeros_like(ck_ref)
            cv_ref[...] = jnp.zeros_like(cv_ref)
            bq_ref[...] = jnp.zeros_like(bq_ref)
            bk_ref[...] = jnp.zeros_like(bk_ref)
            bv_ref[...] = jnp.zeros_like(bv_ref)
            dsk_ref[...] = jnp.zeros_like(dsk_ref)

        @pl.when(n < nb)
        def _():
            mask = _attn_mask(n)
            q = q_ref[...]
            dov = do_ref[...]
            kb = jnp.concatenate([kp_ref[...], kc_ref[...]], axis=0)
            vb = jnp.concatenate([vp_ref[...], vc_ref[...]], axis=0)
            sk = s_ref[...]
            lane = lax.broadcasted_iota(jnp.int32, (1, nq), 1)
            dsk = jnp.zeros((1, nq), F32)
            dq_parts, dk_parts, dv_parts = [], [], []
            for kv in range(ATTN_N_KV):
                kbh = kb[:, kv * hd:(kv + 1) * hd]
                vbh = vb[:, kv * hd:(kv + 1) * hd]
                dkh = jnp.zeros((2 * w, hd), F32)
                dvh = jnp.zeros((2 * w, hd), F32)
                for rep in range(ATTN_REP):
                    h = kv * ATTN_REP + rep
                    qh = q[:, h * hd:(h + 1) * hd]
                    doh = dov[:, h * hd:(h + 1) * hd]
                    p, ps = _attn_probs(qh, kbh, mask, sk[:, h:h + 1])
                    pb = p.astype(BF16)
                    dp = _dot_nt(doh, vbh)
                    delta = jnp.sum(p * dp, axis=-1, keepdims=True)
                    dsc = (p * (dp - delta) * (hd ** -0.5)).astype(BF16)
                    dq_parts.append(_dot(dsc, kbh))
                    dkh = dkh + _dot_tn(dsc, qh)
                    dvh = dvh + _dot_tn(pb, doh)
                    dsk = jnp.where(lane == h, -jnp.sum(ps * delta, axis=0, keepdims=True), dsk)
                dk_parts.append(dkh)
                dv_parts.append(dvh)
            dq = jnp.concatenate(dq_parts, axis=1)
            dkb = jnp.concatenate(dk_parts, axis=1)
            dvb = jnp.concatenate(dv_parts, axis=1)
            dq_ref[...] = dq.astype(BF16)
            bq_ref[...] += jnp.sum(dq, axis=0, keepdims=True)
            dsk_ref[...] += dsk
            dk_prev = ck_ref[...] + dkb[:w, :]
            dv_prev = cv_ref[...] + dvb[:w, :]
            dk_ref[...] = dk_prev.astype(BF16)
            dv_ref[...] = dv_prev.astype(BF16)

            @pl.when(n > 0)
            def _():
                bk_ref[...] += jnp.sum(dk_prev, axis=0, keepdims=True)
                bv_ref[...] += jnp.sum(dv_prev, axis=0, keepdims=True)

            ck_ref[...] = dkb[w:, :]
            cv_ref[...] = dvb[w:, :]

        @pl.when(n == nb)
        def _():
            dk_ref[...] = ck_ref[...].astype(BF16)
            dv_ref[...] = cv_ref[...].astype(BF16)
            bk_ref[...] += jnp.sum(ck_ref[...], axis=0, keepdims=True)
            bv_ref[...] += jnp.sum(cv_ref[...], axis=0, keepdims=True)

    cur = lambda n: jnp.minimum(n, nb - 1)
    prev = lambda n: jnp.maximum(jnp.minimum(n, nb - 1) - 1, 0)
    late = lambda n: jnp.maximum(n - 1, 0)
    vec = lambda width: pl.BlockSpec((1, width), lambda n: (0, 0))
    return pl.pallas_call(
        body, name="attn_bwd", grid=(nb + 1,),
        in_specs=[pl.BlockSpec((w, qd), lambda n: (cur(n), 0)),
                  pl.BlockSpec((w, kd), lambda n: (cur(n), ATTN_REP)),
                  pl.BlockSpec((w, kd), lambda n: (cur(n), ATTN_REP + 1)),
                  pl.BlockSpec((w, kd), lambda n: (prev(n), ATTN_REP)),
                  pl.BlockSpec((w, kd), lambda n: (prev(n), ATTN_REP + 1)),
                  pl.BlockSpec((w, qd), lambda n: (cur(n), 0)),
                  vec(nq)],
        out_specs=[pl.BlockSpec((w, qd), lambda n: (cur(n), 0)),
                   pl.BlockSpec((w, kd), lambda n: (late(n), 0)),
                   pl.BlockSpec((w, kd), lambda n: (late(n), 0)),
                   vec(qd), vec(kd), vec(kd), vec(nq)],
        out_shape=[jax.ShapeDtypeStruct((t, qd), BF16), jax.ShapeDtypeStruct((t, kd), BF16),
                   jax.ShapeDtypeStruct((t, kd), BF16), jax.ShapeDtypeStruct((1, qd), F32),
                   jax.ShapeDtypeStruct((1, kd), F32), jax.ShapeDtypeStruct((1, kd), F32),
                   jax.ShapeDtypeStruct((1, nq), F32)],
        scratch_shapes=[pltpu.VMEM((w, kd), F32), pltpu.VMEM((w, kd), F32)],
        compiler_params=_params("arbitrary"))(qkv, qkv, qkv, qkv, qkv, do, sinks)


def _attn_mask_t(n):
    w = ATTN_WINDOW
    kpos = lax.broadcasted_iota(jnp.int32, (2 * w, ATTN_REP * w), 0)
    qpos = lax.broadcasted_iota(jnp.int32, (2 * w, ATTN_REP * w), 1) % w + w
    rel = qpos - kpos
    return (rel >= 0) & (rel < w) & jnp.logical_not((n == 0) & (kpos < w))


def _attn_probs_t(qts, ktb, mask, sink):
    s = _dot_tn(ktb, qts) * (ATTN_HEAD_DIM ** -0.5)
    s = jnp.where(mask, s, -jnp.inf)
    m = jnp.maximum(jnp.max(s, axis=0, keepdims=True), sink)
    e = jnp.exp(s - m)
    es = jnp.exp(sink - m)
    inv = 1.0 / (jnp.sum(e, axis=0, keepdims=True) + es)
    return e * inv, es * inv


def _attn_blocks_t(kv, q_ref, kc_ref, vc_ref, kp_ref, vp_ref):
    hd = ATTN_HEAD_DIM
    rows = slice(kv * hd, (kv + 1) * hd)
    ktb = jnp.concatenate([kp_ref[rows, :], kc_ref[rows, :]], axis=1)
    vtb = jnp.concatenate([vp_ref[rows, :], vc_ref[rows, :]], axis=1)
    qts = jnp.concatenate([q_ref[(kv * ATTN_REP + r) * hd:(kv * ATTN_REP + r + 1) * hd, :]
                           for r in range(ATTN_REP)], axis=1)
    return qts, ktb, vtb


def _attn_specs_t(nb, cur, prev):
    w, hd = ATTN_WINDOW, ATTN_HEAD_DIM
    kd = ATTN_N_KV * hd
    qd = ATTN_REP * kd
    return [pl.BlockSpec((qd, w), lambda n: (0, cur(n))),
            pl.BlockSpec((kd, w), lambda n: (ATTN_REP, cur(n))),
            pl.BlockSpec((kd, w), lambda n: (ATTN_REP + 1, cur(n))),
            pl.BlockSpec((kd, w), lambda n: (ATTN_REP, prev(n))),
            pl.BlockSpec((kd, w), lambda n: (ATTN_REP + 1, prev(n)))]


def _attn_fwd_t(qkv_t, sinks_rep):
    t = qkv_t.shape[1]
    w, hd = ATTN_WINDOW, ATTN_HEAD_DIM
    qd = ATTN_N_KV * ATTN_REP * hd
    nb = t // w

    def body(q_ref, kc_ref, vc_ref, kp_ref, vp_ref, s_ref, o_ref):
        mask = _attn_mask_t(pl.program_id(0))
        for kv in range(ATTN_N_KV):
            qts, ktb, vtb = _attn_blocks_t(kv, q_ref, kc_ref, vc_ref, kp_ref, vp_ref)
            p, _ = _attn_probs_t(qts, ktb, mask, s_ref[kv])
            ots = _dot(vtb, p.astype(BF16))
            for r in range(ATTN_REP):
                h = kv * ATTN_REP + r
                o_ref[h * hd:(h + 1) * hd, :] = ots[:, r * w:(r + 1) * w].astype(BF16)

    return pl.pallas_call(
        body, name="attn_fwd", grid=(nb,),
        in_specs=_attn_specs_t(nb, lambda n: n, lambda n: jnp.maximum(n - 1, 0)) + [
            pl.BlockSpec(sinks_rep.shape, lambda n: (0, 0, 0))],
        out_specs=pl.BlockSpec((qd, w), lambda n: (0, n)),
        out_shape=jax.ShapeDtypeStruct((qd, t), BF16),
        compiler_params=_params("parallel"))(qkv_t, qkv_t, qkv_t, qkv_t, qkv_t, sinks_rep)


def _attn_bwd_t(qkv_t, do_t, sinks_rep):
    t = qkv_t.shape[1]
    w, hd = ATTN_WINDOW, ATTN_HEAD_DIM
    kd = ATTN_N_KV * hd
    qd = ATTN_REP * kd
    nq = ATTN_N_KV * ATTN_REP
    nb = t // w
    rows_all = qd + 2 * kd

    def body(q_ref, kc_ref, vc_ref, kp_ref, vp_ref, do_ref, s_ref, dqkv_ref, bsum_ref, dsk_ref,
             carry_ref, new_ref, bacc_ref, sacc_ref):
        n = pl.program_id(0)

        @pl.when(n == 0)
        def _():
            carry_ref[...] = jnp.zeros_like(carry_ref)
            bacc_ref[...] = jnp.zeros_like(bacc_ref)
            sacc_ref[...] = jnp.zeros_like(sacc_ref)

        @pl.when(n < nb)
        def _():
            mask = _attn_mask_t(n)
            for kv in range(ATTN_N_KV):
                qts, ktb, vtb = _attn_blocks_t(kv, q_ref, kc_ref, vc_ref, kp_ref, vp_ref)
                dots = jnp.concatenate([do_ref[(kv * ATTN_REP + r) * hd:(kv * ATTN_REP + r + 1) * hd, :]
                                        for r in range(ATTN_REP)], axis=1)
                p, ps = _attn_probs_t(qts, ktb, mask, s_ref[kv])
                dpt = _dot_tn(vtb, dots)
                delta = jnp.sum(p * dpt, axis=0, keepdims=True)
                dst = (p * (dpt - delta) * (hd ** -0.5)).astype(BF16)
                dqts = _dot(ktb, dst)
                for r in range(ATTN_REP):
                    h = kv * ATTN_REP + r
                    new_ref[h * hd:(h + 1) * hd, :] = dqts[:, r * w:(r + 1) * w]
                dktb = _dot_nt(qts, dst)
                dvtb = _dot_nt(dots, p.astype(BF16))
                krows = slice(qd + kv * hd, qd + (kv + 1) * hd)
                vrows = slice(qd + kd + kv * hd, qd + kd + (kv + 1) * hd)
                carry_ref[krows, :] += dktb[:, :w]
                carry_ref[vrows, :] += dvtb[:, :w]
                new_ref[krows, :] = dktb[:, w:]
                new_ref[vrows, :] = dvtb[:, w:]
                sacc_ref[kv] += -(ps * delta)

        @pl.when(n >= 1)
        def _():
            done = carry_ref[...]
            dqkv_ref[...] = done.astype(BF16)
            bacc_ref[...] += done

        @pl.when(n < nb)
        def _():
            carry_ref[...] = new_ref[...]

        @pl.when(n == nb)
        def _():
            bsum_ref[...] = jnp.sum(bacc_ref[...], axis=1, keepdims=True)
            lane = lax.broadcasted_iota(jnp.int32, (1, nq), 1)
            dsk = jnp.zeros((1, nq), F32)
            for kv in range(ATTN_N_KV):
                acc = sacc_ref[kv]
                for r in range(ATTN_REP):
                    tot = jnp.sum(acc[:, r * w:(r + 1) * w], axis=1, keepdims=True)
                    dsk = jnp.where(lane == kv * ATTN_REP + r, tot, dsk)
            dsk_ref[...] = dsk

    cur = lambda n: jnp.minimum(n, nb - 1)
    prev = lambda n: jnp.maximum(jnp.minimum(n, nb - 1) - 1, 0)
    return pl.pallas_call(
        body, name="attn_bwd", grid=(nb + 1,),
        in_specs=_attn_specs_t(nb, cur, prev) + [pl.BlockSpec((qd, w), lambda n: (0, cur(n))),
                                                 pl.BlockSpec(sinks_rep.shape, lambda n: (0, 0, 0))],
        out_specs=[pl.BlockSpec((rows_all, w), lambda n: (0, jnp.maximum(n - 1, 0))),
                   pl.BlockSpec((rows_all, 1), lambda n: (0, 0)),
                   pl.BlockSpec((1, nq), lambda n: (0, 0))],
        out_shape=[jax.ShapeDtypeStruct((rows_all, t), BF16), jax.ShapeDtypeStruct((rows_all, 1), F32),
                   jax.ShapeDtypeStruct((1, nq), F32)],
        scratch_shapes=[pltpu.VMEM((rows_all, w), F32), pltpu.VMEM((rows_all, w), F32),
                        pltpu.VMEM((rows_all, w), F32), pltpu.VMEM(sinks_rep.shape, F32)],
        compiler_params=_params("arbitrary"))(qkv_t, qkv_t, qkv_t, qkv_t, qkv_t, do_t, sinks_rep)


HBM_SPEC = pl.BlockSpec(memory_space=pl.ANY)


def _comm_call(name, body, ins, out_shapes, n_sems):
    return pl.pallas_call(
        body, name=name, in_specs=[HBM_SPEC] * len(ins), out_specs=[HBM_SPEC] * len(out_shapes),
        out_shape=out_shapes,
        scratch_shapes=[pltpu.SemaphoreType.DMA((s,)) for s in n_sems])(*ins)


def _all_gather(name, shards):
    n = len(shards)

    def body(*refs):
        x_refs, out_refs = refs[:n], refs[n:2 * n]
        send_sems, recv_sems, local_sems = refs[2 * n:]
        x, y, c = lax.axis_index("x"), lax.axis_index("y"), lax.axis_index("c")
        me, sibling = (x, y, c), (x, y, 1 - c)
        chips = [(1 - x, y), (x, 1 - y), (1 - x, 1 - y)]

        def slot(i, px, py, pc):
            return out_refs[i].at[4 * px + 2 * py + pc]

        def copy(k, i, block, to, src=None):
            return pltpu.make_async_remote_copy(
                src_ref=slot(i, *block) if src is None else src, dst_ref=slot(i, *block),
                send_sem=send_sems.at[k * n + i], recv_sem=recv_sems.at[k * n + i], device_id=to,
                device_id_type=MESH)

        mine = [pltpu.make_async_copy(x_refs[i], slot(i, *me), local_sems.at[i]) for i in range(n)]
        first = []
        for i in range(n):
            mine[i].start()
            first.append(copy(0, i, me, sibling, src=x_refs[i]))
            first += [copy(1 + j, i, me, (*chip, c), src=x_refs[i]) for j, chip in enumerate(chips)]
        for cp in first:
            cp.start()
        passed = []
        for i in range(n):
            for j, chip in enumerate(chips):
                copy(1 + j, i, (*chip, c), me).wait_recv()
                passed.append(copy(4 + j, i, (*chip, c), sibling))
                passed[-1].start()
        for i in range(n):
            copy(0, i, sibling, me).wait_recv()
            for j, chip in enumerate(chips):
                copy(4 + j, i, (*chip, 1 - c), me).wait_recv()
        for cp in first + passed:
            cp.wait_send()
        for cp in mine:
            cp.wait()

    outs = [jax.ShapeDtypeStruct((N_DEV,) + s.shape, s.dtype) for s in shards]
    return _comm_call(name, body, shards, outs, (7 * n, 7 * n, n))


def _pair_exchange(name, blocks):
    n = len(blocks)

    def body(*refs):
        g_refs, out_refs = refs[:n], refs[n:2 * n]
        send_sems, recv_sems = refs[2 * n:]
        x, y, c = lax.axis_index("x"), lax.axis_index("y"), lax.axis_index("c")
        copies = [pltpu.make_async_remote_copy(
            src_ref=g_refs[i].at[2 * k + 1 - c], dst_ref=out_refs[i].at[k], send_sem=send_sems.at[4 * i + k],
            recv_sem=recv_sems.at[4 * i + k], device_id=(x, y, 1 - c), device_id_type=MESH)
            for i in range(n) for k in range(4)]
        for cp in copies:
            cp.start()
        for cp in copies:
            cp.wait()

    outs = [jax.ShapeDtypeStruct((4,) + b.shape[1:], b.dtype) for b in blocks]
    return _comm_call(name, body, blocks, outs, (4 * n, 4 * n))


def _chip_exchange(name, blocks):
    n = len(blocks)

    def body(*refs):
        p_refs, out_refs = refs[:n], refs[n:2 * n]
        send_sems, recv_sems = refs[2 * n:]
        x, y, c = lax.axis_index("x"), lax.axis_index("y"), lax.axis_index("c")
        chips = [(1 - x, y), (x, 1 - y), (1 - x, 1 - y)]
        copies = [pltpu.make_async_remote_copy(
            src_ref=p_refs[i].at[j], dst_ref=out_refs[i].at[j], send_sem=send_sems.at[3 * i + j],
            recv_sem=recv_sems.at[3 * i + j], device_id=(*chip, c), device_id_type=MESH)
            for i in range(n) for j, chip in enumerate(chips)]
        for cp in copies:
            cp.start()
        for cp in copies:
            cp.wait()

    outs = [jax.ShapeDtypeStruct(b.shape, b.dtype) for b in blocks]
    return _comm_call(name, body, blocks, outs, (3 * n, 3 * n))


def _pair_sum(name, blocks, from_sibling, g_idx, r_idx):
    _, r, c_ = blocks.shape
    tr = _tile(r, 512, 16)

    def body(gi_ref, ri_ref, a_ref, b_ref, own_ref, send_ref):
        k = pl.program_id(1)
        s = a_ref[...] + b_ref[...]

        @pl.when(k == 0)
        def _():
            own_ref[...] = s

        @pl.when(k > 0)
        def _():
            send_ref[...] = s.astype(send_ref.dtype)

    return pl.pallas_call(
        body, name=name,
        grid_spec=pltpu.PrefetchScalarGridSpec(
            num_scalar_prefetch=2, grid=(r // tr, 4),
            in_specs=[pl.BlockSpec((None, tr, c_), lambda i, k, gi, ri: (gi[k], i, 0)),
                      pl.BlockSpec((None, tr, c_), lambda i, k, gi, ri: (ri[k], i, 0))],
            out_specs=[pl.BlockSpec((None, tr, c_), lambda i, k, gi, ri: (0, i, 0)),
                       pl.BlockSpec((None, tr, c_), lambda i, k, gi, ri: (jnp.maximum(k - 1, 0), i, 0))]),
        out_shape=[jax.ShapeDtypeStruct((1, r, c_), F32), jax.ShapeDtypeStruct((3, r, c_), PAYLOAD)],
        compiler_params=_params("parallel", "arbitrary"))(g_idx, r_idx, blocks, from_sibling)


def _adamw(w, g, m, v):
    m = ADAM_B1 * m + (1.0 - ADAM_B1) * g
    v = ADAM_B2 * v + (1.0 - ADAM_B2) * (g * g)
    m_hat = m / (1.0 - ADAM_B1 ** ADAM_STEP)
    v_hat = v / (1.0 - ADAM_B2 ** ADAM_STEP)
    delta = -ADAM_LR * (m_hat / (jnp.sqrt(v_hat) + ADAM_EPS) + ADAM_WD * w)
    return delta, m, v


def _sum_adamw(name, parts_f32, parts_lo, w, m, v):
    r, c_ = w.shape
    tr = _tile(r, 256, 16)
    k1 = parts_f32.shape[0]
    k2 = 0 if parts_lo is None else parts_lo.shape[0]

    def body(*refs):
        a_ref = refs[0]
        b_ref = refs[1] if k2 else None
        w_ref, m_ref, v_ref, g_ref, d_ref, nm_ref, nv_ref = refs[(2 if k2 else 1):]
        g = a_ref[0]
        for k in range(1, k1):
            g = g + a_ref[k]
        for k in range(k2):
            g = g + b_ref[k].astype(F32)
        g_ref[...] = g
        d_ref[...], nm_ref[...], nv_ref[...] = _adamw(w_ref[...], g, m_ref[...], v_ref[...])

    row = pl.BlockSpec((tr, c_), lambda i: (i, 0))
    ins = [parts_f32] + ([parts_lo] if k2 else []) + [w, m, v]
    in_specs = [pl.BlockSpec((k1, tr, c_), lambda i: (0, i, 0))]
    if k2:
        in_specs.append(pl.BlockSpec((k2, tr, c_), lambda i: (0, i, 0)))
    in_specs += [row, row, row]
    return pl.pallas_call(body, name=name, grid=(r // tr,), in_specs=in_specs, out_specs=[row] * 4,
                          out_shape=[jax.ShapeDtypeStruct((r, c_), F32)] * 4,
                          compiler_params=_params("parallel"))(*ins)


def _pack_rows(flat, n_rows, cols):
    pad = n_rows * cols - flat.shape[-1]
    flat = jnp.pad(flat, [(0, 0)] * (flat.ndim - 1) + [(0, pad)])
    return flat.reshape(flat.shape[:-1] + (n_rows, cols))


def _cols_join(blocks):
    return jnp.concatenate([blocks[d] for d in range(N_DEV)], axis=1)


def _cols_split(full):
    c = full.shape[1] // N_DEV
    return jnp.stack([full[:, d * c:(d + 1) * c] for d in range(N_DEV)])


def _rows_join(blocks):
    return blocks.reshape(N_DEV * blocks.shape[1], blocks.shape[2])


def _rows_split(full):
    return full.reshape(N_DEV, full.shape[0] // N_DEV, full.shape[1])


def _perm_xbc(a, ng):
    lead = a.shape[:-1]
    di, gn = ng * GW, ng * SSD_D_STATE
    xs = a[..., :di].reshape(lead + (ng, GW))
    bs = a[..., di:di + gn].reshape(lead + (ng, SSD_D_STATE))
    cs = a[..., di + gn:].reshape(lead + (ng, SSD_D_STATE))
    return jnp.concatenate([xs, bs, cs], axis=-1).reshape(lead + (ng * GC,))


def _unperm_xbc(a, ng):
    lead = a.shape[:-1]
    g = a.reshape(lead + (ng, GC))
    return jnp.concatenate([g[..., :GW].reshape(lead + (ng * GW,)),
                            g[..., GW:GW + SSD_D_STATE].reshape(lead + (ng * SSD_D_STATE,)),
                            g[..., GW + SSD_D_STATE:].reshape(lead + (ng * SSD_D_STATE,))], axis=-1)


def _heads_col(v, ng):
    return jnp.pad(v.reshape(ng, 1, SSD_HPG), ((0, 0), (0, 0), (0, LANES - SSD_HPG)))


def _heads_row(v, ng):
    return jnp.pad(v.reshape(ng, SSD_HPG, 1), ((0, 0), (0, 8 - SSD_HPG), (0, 0)))


MATRIX_ITEMS = ("w_in", "w_out", "up0", "down0", "w_qkv", "w_o", "up1", "down1")
VECTOR_ITEMS = ("conv_w", "b_qkv", "b_o")
ITEMS = MATRIX_ITEMS + VECTOR_ITEMS


def _items(tree, prefix=""):
    g = lambda k: tree[prefix + k]
    return {"w_in": g("ssd_w_in")[0], "w_out": g("ssd_w_out")[0], "w_qkv": g("attn_w_qkv")[0].T,
            "w_o": g("attn_w_o")[0], "up0": g("mlp_w_up")[0], "up1": g("mlp_w_up")[1],
            "down0": g("mlp_w_down")[0], "down1": g("mlp_w_down")[1], "conv_w": g("ssd_conv_w")[0],
            "b_qkv": g("attn_b_qkv"), "b_o": g("attn_b_o")}


def _from_items(it):
    return {"ssd_w_in": it["w_in"][None], "ssd_w_out": it["w_out"][None], "attn_w_qkv": it["w_qkv"].T[None],
            "attn_w_o": it["w_o"][None], "mlp_w_up": jnp.stack([it["up0"], it["up1"]]),
            "mlp_w_down": jnp.stack([it["down0"], it["down1"]]), "ssd_conv_w": it["conv_w"][None],
            "attn_b_qkv": it["b_qkv"], "attn_b_o": it["b_o"]}


REPLICATED = ("ssd_conv_b", "ssd_dt_bias", "ssd_a_log", "ssd_d", "ssd_norm_w", "attn_sinks", "mix_pre_norm",
              "mix_post_norm", "ffn_pre_norm", "ffn_post_norm")
WEIGHTS = ("ssd_w_in", "ssd_conv_w", "ssd_conv_b", "ssd_dt_bias", "ssd_a_log", "ssd_d", "ssd_norm_w", "ssd_w_out",
           "attn_w_qkv", "attn_b_qkv", "attn_sinks", "attn_w_o", "attn_b_o", "mlp_w_up", "mlp_w_down",
           "mix_pre_norm", "mix_post_norm", "ffn_pre_norm", "ffn_post_norm")


def _forward_backward(x, target, gathered, rep):
    t, d = x.shape
    ng = rep["ssd_norm_w"].shape[1] // GW
    di = ng * GW
    n_xbc = ng * GC
    nh = ng * SSD_HPG
    grads, blocks = {}, {}

    w_in = _cols_join(gathered["w_in"])
    w_z = w_in[:, :di]
    w_xbc = _perm_xbc(w_in[:, di:di + n_xbc], ng)
    w_dt = jnp.pad(w_in[:, di + n_xbc:], ((0, 0), (0, LANES - nh)))
    w_in_k = jnp.concatenate([w_z, w_xbc, w_dt], axis=1)
    n_in = w_in_k.shape[1]
    w_out = _rows_join(gathered["w_out"])
    w_qkv_t = _rows_join(gathered["w_qkv"])
    w_o = _rows_join(gathered["w_o"])
    w_up = [_cols_join(gathered["up0"]), _cols_join(gathered["up1"])]
    w_down = [_rows_join(gathered["down0"]), _rows_join(gathered["down1"])]
    b_qkv_col = gathered["b_qkv"].reshape(-1, 1)
    sinks_rep = jnp.repeat(rep["attn_sinks"].reshape(ATTN_N_KV, ATTN_REP, 1), ATTN_WINDOW, axis=2).reshape(
        ATTN_N_KV, 1, ATTN_REP * ATTN_WINDOW)
    b_o = _cols_join(gathered["b_o"])
    conv_w = _perm_xbc(_cols_join(gathered["conv_w"]), ng)
    conv_b = _perm_xbc(rep["ssd_conv_b"], ng)
    bias_c, alog_c, dsk_c = (_heads_col(rep[k], ng) for k in ("ssd_dt_bias", "ssd_a_log", "ssd_d"))
    bias_r, alog_r = (_heads_row(rep[k], ng) for k in ("ssd_dt_bias", "ssd_a_log"))
    norm = {k: rep[k] for k in ("mix_pre_norm", "mix_post_norm", "ffn_pre_norm", "ffn_post_norm")}

    def nrow(name, i):
        return norm[name][i:i + 1]

    def mlp_fwd(i, u2):
        a, p = _mm(f"mlp{i}_up", [u2], [w_up[i]], "nn", tm=1024, tn=1024, out_dtypes=(F32, BF16),
                   epilogue=lambda acc: (acc, jnp.square(jnp.maximum(acc, 0.0))))
        f = _mm(f"mlp{i}_down", [p], [w_down[i]], "nn", tm=512, tn=1024)
        return a, p, f

    def mlp_bwd(i, df, u2, a, p):
        da = _mm(f"mlp{i}_dact", [df], [w_down[i]], "nt", tm=1024, tn=1024, out_dtypes=(BF16,),
                 tiles=(a,), epilogue=lambda acc, av: (acc * (2.0 * jnp.maximum(av, 0.0)),))
        blocks[f"down{i}"] = _rows_split(_mm(f"mlp{i}_dwdown", [p], [df], "tn", tm=512, tn=1024))
        blocks[f"up{i}"] = _mm(f"mlp{i}_dwup", [u2], [da], "tn", tm=1024, tn=da.shape[1] // N_DEV,
                               col_blocks=True)
        return _mm(f"mlp{i}_dx", [da], [w_up[i]], "nt", tm=512, tn=1024)

    u0 = _prenorm("l0_prenorm", x, nrow("mix_pre_norm", 0))
    zx = _mm("ssd_in_proj", [u0], [w_in_k], "nn", tm=1024, tn=_tile(n_in, 1024))
    pre = _conv_fwd(zx, di, n_xbc, conv_w, conv_b)
    dt_raw = zx[:, di + n_xbc:di + n_xbc + nh].reshape(t, ng, SSD_HPG)
    dtc = jnp.pad(jnp.transpose(dt_raw, (1, 0, 2)), ((0, 0), (0, 0), (0, LANES - SSD_HPG)))
    dtr = jnp.pad(jnp.transpose(dt_raw, (1, 2, 0)), ((0, 0), (0, 8 - SSD_HPG), (0, 0)))
    ssd_args = (dtc, dtr, bias_c, alog_c, dsk_c, bias_r, alog_r)
    y, states = _ssd_fwd(pre, *ssd_args)
    yn = _gate_norm_fwd(y, zx, rep["ssd_norm_w"])
    mix0 = _mm("ssd_out_proj", [yn], [w_out], "nn", tm=1024, tn=1024)
    h1, u0f = _post_pre("l0_mid", x, mix0, nrow("mix_post_norm", 0), nrow("ffn_pre_norm", 0))
    a0, p0, f0 = mlp_fwd(0, u0f)
    h2, u1 = _post_pre("l1_in", h1, f0, nrow("ffn_post_norm", 0), nrow("mix_pre_norm", 1))
    qkv_t = _mm("attn_qkv_proj", [w_qkv_t], [u1], "nt", tm=768, tn=1024, out_dtypes=(BF16,), cols=(b_qkv_col,),
                epilogue=lambda acc, b: (acc + b,))
    ao_t = _attn_fwd_t(qkv_t, sinks_rep)
    mix1 = _mm("attn_out_proj", [ao_t], [w_o], "tn", tm=1024, tn=1024, rows=(b_o,),
               epilogue=lambda acc, b: (acc + b,))
    h3, u1f = _post_pre("l1_mid", h2, mix1, nrow("mix_post_norm", 1), nrow("ffn_pre_norm", 1))
    a1, p1, f1 = mlp_fwd(1, u1f)
    dh, loss_row = _final_loss("loss", h3, f1, nrow("ffn_post_norm", 1), target)

    g_norm = {k: [None, None] for k in norm}
    df1, g_norm["ffn_post_norm"][1], _ = _norm_bwd("l1_ffn_post_bwd", dh, post=(f1, nrow("ffn_post_norm", 1)))
    du = mlp_bwd(1, df1, u1f, a1, p1)
    dh, g_norm["ffn_pre_norm"][1], dmix1, g_norm["mix_post_norm"][1], db_o = _norm_bwd(
        "l1_mid_bwd", dh, pre=(du, h3, nrow("ffn_pre_norm", 1)), post=(mix1, nrow("mix_post_norm", 1)))
    blocks["b_o"] = _cols_split(db_o)
    blocks["w_o"] = _rows_split(_mm("attn_dwo", [ao_t], [dmix1], "nn", tm=512, tn=1024))
    dao_t = _mm("attn_dout", [w_o], [dmix1], "nt", tm=1024, tn=1024, out_dtypes=(BF16,))
    dqkv_t, db_qkv, grads["attn_sinks"] = _attn_bwd_t(qkv_t, dao_t, sinks_rep)
    blocks["b_qkv"] = db_qkv.reshape(N_DEV, 1, -1)
    blocks["w_qkv"] = _rows_split(_mm("attn_dwqkv", [dqkv_t], [u1], "nn", tm=512, tn=1024))
    du = _mm("attn_dx", [dqkv_t], [w_qkv_t], "tn", tm=1024, tn=1024)
    dh, g_norm["mix_pre_norm"][1], df0, g_norm["ffn_post_norm"][0], _ = _norm_bwd(
        "l1_in_bwd", dh, pre=(du, h2, nrow("mix_pre_norm", 1)), post=(f0, nrow("ffn_post_norm", 0)))
    du = mlp_bwd(0, df0, u0f, a0, p0)
    dh, g_norm["ffn_pre_norm"][0], dmix0, g_norm["mix_post_norm"][0], _ = _norm_bwd(
        "l0_mid_bwd", dh, pre=(du, h1, nrow("ffn_pre_norm", 0)), post=(mix0, nrow("mix_post_norm", 0)))
    blocks["w_out"] = _rows_split(_mm("ssd_dwout", [yn], [dmix0], "tn", tm=512, tn=1024))
    dyn = _mm("ssd_dyn", [dmix0], [w_out], "nt", tm=1024, tn=1024)
    dy, dz, grads["ssd_norm_w"] = _gate_norm_bwd(dyn, y, zx, rep["ssd_norm_w"])
    dpre, ddt_g, dbias_g, dalog_g, dd_g = _ssd_bwd(dy, pre, states, *ssd_args)
    dxbc, dconv_w, dconv_b = _conv_bwd(dpre, zx, di, conv_w)
    ddt = jnp.transpose(ddt_g[:, :, :SSD_HPG], (1, 0, 2)).reshape(t, nh)
    ddt = jnp.pad(ddt, ((0, 0), (0, LANES - nh))).astype(BF16)
    blocks["conv_w"] = _cols_split(_unperm_xbc(dconv_w, ng))
    grads["ssd_conv_b"] = _unperm_xbc(dconv_b, ng)
    for name, val in (("ssd_dt_bias", dbias_g), ("ssd_a_log", dalog_g), ("ssd_d", dd_g)):
        grads[name] = val[:, 0, :SSD_HPG].reshape(1, nh)
    dw_z = _mm("ssd_dwz", [u0], [dz], "tn", tm=1024, tn=512)
    dw_xbc = _mm("ssd_dwxbc", [u0], [dxbc], "tn", tm=1024, tn=512)
    dw_dt = _mm("ssd_dwdt", [u0], [ddt], "tn", tm=1024, tn=LANES)
    blocks["w_in"] = _cols_split(jnp.concatenate([dw_z, _unperm_xbc(dw_xbc, ng), dw_dt[:, :nh]], axis=1))
    du = _mm("ssd_dx", [dz, dxbc, ddt], [w_z, w_xbc, w_dt], "nt", tm=256, tn=1024)
    grad_x, g_norm["mix_pre_norm"][0] = _norm_bwd("l0_in_bwd", dh, pre=(du, x, nrow("mix_pre_norm", 0)))
    for k in norm:
        grads[k] = jnp.concatenate(g_norm[k], axis=0)
    return loss_row, grad_x, blocks, grads


def kernel(x, ssd_w_in, ssd_conv_w, ssd_conv_b, ssd_dt_bias, ssd_a_log, ssd_d, ssd_norm_w, ssd_w_out, attn_w_qkv, attn_b_qkv, attn_sinks, attn_w_o, attn_b_o, mlp_w_up, mlp_w_down, mix_pre_norm, mix_post_norm, ffn_pre_norm, ffn_post_norm, loss_target, m_ssd_w_in, m_ssd_conv_w, m_ssd_conv_b, m_ssd_dt_bias, m_ssd_a_log, m_ssd_d, m_ssd_norm_w, m_ssd_w_out, m_attn_w_qkv, m_attn_b_qkv, m_attn_sinks, m_attn_w_o, m_attn_b_o, m_mlp_w_up, m_mlp_w_down, m_mix_pre_norm, m_mix_post_norm, m_ffn_pre_norm, m_ffn_post_norm, v_ssd_w_in, v_ssd_conv_w, v_ssd_conv_b, v_ssd_dt_bias, v_ssd_a_log, v_ssd_d, v_ssd_norm_w, v_ssd_w_out, v_attn_w_qkv, v_attn_b_qkv, v_attn_sinks, v_attn_w_o, v_attn_b_o, v_mlp_w_up, v_mlp_w_down, v_mix_pre_norm, v_mix_post_norm, v_ffn_pre_norm, v_ffn_post_norm):
    given = dict(locals())
    w = {k: given[k] for k in WEIGHTS}
    mom_m = {k: given["m_" + k] for k in WEIGHTS}
    mom_v = {k: given["v_" + k] for k in WEIGHTS}
    w_it, m_it, v_it = _items(given), _items(given, "m_"), _items(given, "v_")

    shards = [w_it[k].astype(PAYLOAD) for k in MATRIX_ITEMS] + [w_it[k] for k in VECTOR_ITEMS]
    gathered = dict(zip(ITEMS, _all_gather("gather_weights", shards)))

    rep = {k: w[k] for k in REPLICATED}
    loss_row, grad_x, blocks, grads = _forward_backward(x[0], loss_target[0], gathered, rep)

    from_sibling = _pair_exchange("rs_pair_exchange", [blocks[k] for k in ITEMS])
    ix, iy, ic = lax.axis_index("x"), lax.axis_index("y"), lax.axis_index("c")
    chips = [(ix, iy), (1 - ix, iy), (ix, 1 - iy), (1 - ix, 1 - iy)]
    g_idx = jnp.stack([4 * cx + 2 * cy + ic for cx, cy in chips]).astype(jnp.int32)
    r_idx = jnp.stack([2 * cx + cy for cx, cy in chips]).astype(jnp.int32)
    sums = [_pair_sum(f"rs_pair_sum_{k}", blocks[k], fs, g_idx, r_idx) for k, fs in zip(ITEMS, from_sibling)]
    from_chips = _chip_exchange("rs_chip_exchange", [s[1] for s in sums])
    item_out = [_sum_adamw(f"adamw_{k}", s[0], fc, w_it[k], m_it[k], v_it[k])
                for k, s, fc in zip(ITEMS, sums, from_chips)]

    def pack_rep(tree, last):
        flat = jnp.concatenate([tree[k].reshape(-1) for k in REPLICATED] + [last])
        return _pack_rows(flat, _round_up(-(-flat.shape[0] // LANES), 8), LANES)

    partials, = _all_gather("gather_small_grads", [pack_rep(grads, loss_row[0, :1])])
    zero = jnp.zeros((1,), F32)
    rep_out = _sum_adamw("adamw_replicated", partials, None, pack_rep(w, zero), pack_rep(mom_m, zero),
                         pack_rep(mom_v, zero))

    kinds = []
    for kind, r_arr in enumerate(rep_out):
        tree = _from_items({k: out[kind] for k, out in zip(ITEMS, item_out)})
        flat, off = r_arr.reshape(-1), 0
        for k in REPLICATED:
            tree[k] = flat[off:off + w[k].size].reshape(w[k].shape)
            off += w[k].size
        kinds.append(tree)
    loss = rep_out[0].reshape(-1)[off]
    outs = [loss, grad_x[None]]
    for tree in kinds:
        outs += [tree[k] for k in WEIGHTS]
    return tuple(outs)
```

```python
import functools

import jax
import jax.numpy as jnp
from jax import lax
from jax.experimental import pallas as pl
from jax.experimental.pallas import tpu as pltpu

F32 = jnp.float32
BF16 = jnp.bfloat16
PAYLOAD = jnp.bfloat16
HIGHEST = lax.Precision.HIGHEST
MESH = pl.DeviceIdType.MESH

NORM_EPS = 1e-6
SSD_HEAD_DIM = 64
SSD_N_GROUPS = 8
SSD_HPG = 4
SSD_D_STATE = 128
SSD_CONV_WIDTH = 4
SSD_CHUNK = 128
ATTN_HEAD_DIM = 64
ATTN_N_KV = 4
ATTN_REP = 4
ATTN_WINDOW = 128
ADAM_LR = 0.001
ADAM_B1 = 0.9
ADAM_B2 = 0.999
ADAM_EPS = 1e-08
ADAM_WD = 0.01
ADAM_STEP = 10

N_DEV = 8
LANES = 128
PACK_COLS = 1024
V7X_VMEM_LIMIT = 56 * 1024 * 1024

GW = SSD_HPG * SSD_HEAD_DIM
GC = GW + 2 * SSD_D_STATE


def _params(*sem):
    return pltpu.CompilerParams(dimension_semantics=sem, vmem_limit_bytes=V7X_VMEM_LIMIT)


def _tile(n, pref, mult=LANES):
    best = None
    t = mult
    while t <= min(n, pref):
        if n % t == 0:
            best = t
        t += mult
    return best if best is not None else n


def _round_up(n, m):
    return (n + m - 1) // m * m


def _acc(ref, val, first):
    @pl.when(first)
    def _():
        ref[...] = val

    @pl.when(jnp.logical_not(first))
    def _():
        ref[...] += val


def _dot(a, b):
    return lax.dot_general(a, b, (((1,), (0,)), ((), ())), preferred_element_type=F32)


def _dot_nt(a, b):
    return lax.dot_general(a, b, (((1,), (1,)), ((), ())), preferred_element_type=F32)


def _dot_tn(a, b):
    return lax.dot_general(a, b, (((0,), (0,)), ((), ())), preferred_element_type=F32)


def _dot_f32(a, b):
    return lax.dot_general(a, b, (((1,), (0,)), ((), ())), preferred_element_type=F32, precision=HIGHEST)


_DOTS = {"nn": _dot, "nt": _dot_nt, "tn": _dot_tn}


def _sigmoid(x):
    return 1.0 / (1.0 + jnp.exp(-x))


def _softplus(x):
    return jnp.maximum(x, 0.0) + jnp.log1p(jnp.exp(-jnp.abs(x)))


def _silu_grad(x, s):
    return s * (1.0 + x * (1.0 - s))


def _mm(name, a_list, b_list, mode, *, tm, tn, out_dtypes=(F32,), epilogue=None, tiles=(), rows=(), cols=(),
        col_blocks=False):
    npair = len(a_list)
    if mode == "tn":
        m = a_list[0].shape[1]
    else:
        m = a_list[0].shape[0]
    n = b_list[0].shape[0] if mode == "nt" else b_list[0].shape[1]
    tm = _tile(m, tm, LANES if mode == "tn" else 8)
    tn = _tile(n, tn)
    assert m % tm == 0 and n % tn == 0, (name, m, n, tm, tn)
    dot = _DOTS[mode]

    def body(*refs):
        a_refs = refs[:npair]
        b_refs = refs[npair:2 * npair]
        n_extra = len(tiles) + len(rows) + len(cols)
        e_refs = refs[2 * npair:2 * npair + n_extra]
        o_refs = refs[2 * npair + n_extra:]
        acc = None
        for ar, br in zip(a_refs, b_refs):
            d = dot(ar[...], br[...])
            acc = d if acc is None else acc + d
        outs = epilogue(acc, *[e[...] for e in e_refs]) if epilogue is not None else (acc,)
        for o, v in zip(o_refs, outs):
            o[...] = v.astype(o.dtype)

    in_specs = []
    for a in a_list:
        if mode == "tn":
            in_specs.append(pl.BlockSpec((a.shape[0], tm), lambda i, j: (0, i)))
        else:
            in_specs.append(pl.BlockSpec((tm, a.shape[1]), lambda i, j: (i, 0)))
    for b in b_list:
        if mode == "nt":
            in_specs.append(pl.BlockSpec((tn, b.shape[1]), lambda i, j: (j, 0)))
        else:
            in_specs.append(pl.BlockSpec((b.shape[0], tn), lambda i, j: (0, j)))
    in_specs += [pl.BlockSpec((tm, tn), lambda i, j: (i, j)) for _ in tiles]
    in_specs += [pl.BlockSpec((1, tn), lambda i, j: (0, j)) for _ in rows]
    in_specs += [pl.BlockSpec((tm, 1), lambda i, j: (i, 0)) for _ in cols]
    outs = pl.pallas_call(
        body,
        name=name,
        grid=(m // tm, n // tn),
        in_specs=in_specs,
        out_specs=[pl.BlockSpec((None, tm, tn), lambda i, j: (j, i, 0)) if col_blocks else
                   pl.BlockSpec((tm, tn), lambda i, j: (i, j)) for _ in out_dtypes],
        out_shape=[jax.ShapeDtypeStruct((n // tn, m, tn) if col_blocks else (m, n), dt) for dt in out_dtypes],
        compiler_params=_params("parallel", "parallel"),
    )(*a_list, *b_list, *tiles, *rows, *cols)
    return outs[0] if len(out_dtypes) == 1 else outs


def _rms(x, w):
    r = lax.rsqrt(jnp.mean(x * x, axis=-1, keepdims=True) + NORM_EPS)
    return x * r * w


def _rms_bwd(x, w, dy):
    r = lax.rsqrt(jnp.mean(x * x, axis=-1, keepdims=True) + NORM_EPS)
    xh = x * r
    g = dy * w
    dx = r * (g - xh * jnp.mean(g * xh, axis=-1, keepdims=True))
    return dx, dy * xh


def _row_specs(tr, d):
    return pl.BlockSpec((tr, d), lambda i: (i, 0)), pl.BlockSpec((1, d), lambda i: (0, 0))


def _prenorm(name, h, w, after):
    t, d = h.shape
    tr = _tile(t, 512, 8)
    row, vec = _row_specs(tr, d)

    def body(h_ref, w_ref, after_ref, u_ref):
        u_ref[...] = _rms(h_ref[...], w_ref[...]).astype(BF16)

    return pl.pallas_call(body, name=name, grid=(t // tr,),
                          in_specs=[row, vec, pl.BlockSpec((8, LANES), lambda i: (0, 0))], out_specs=row,
                          out_shape=jax.ShapeDtypeStruct((t, d), BF16), compiler_params=_params("parallel"))(
                              h, w, after)


def _post_pre(name, h, m, w_post, w_pre):
    t, d = h.shape
    tr = _tile(t, 512, 8)
    row, vec = _row_specs(tr, d)

    def body(h_ref, m_ref, wq_ref, wp_ref, hn_ref, u_ref):
        hn = h_ref[...] + _rms(m_ref[...], wq_ref[...])
        hn_ref[...] = hn
        u_ref[...] = _rms(hn, wp_ref[...]).astype(BF16)

    return pl.pallas_call(body, name=name, grid=(t // tr,), in_specs=[row, row, vec, vec], out_specs=[row, row],
                          out_shape=[jax.ShapeDtypeStruct((t, d), F32), jax.ShapeDtypeStruct((t, d), BF16)],
                          compiler_params=_params("parallel"))(h, m, w_post, w_pre)


def _final_loss(name, h, m, w_post, target):
    t, d = h.shape
    tr = _tile(t, 512, 8)
    row, vec = _row_specs(tr, d)

    def body(h_ref, m_ref, wq_ref, t_ref, dh_ref, loss_ref):
        err = h_ref[...] + _rms(m_ref[...], wq_ref[...]) - t_ref[...]
        dh_ref[...] = err * (1.0 / d)
        part = 0.5 * jnp.sum(jnp.mean(err * err, axis=-1, keepdims=True), axis=0, keepdims=True)
        _acc(loss_ref, jnp.broadcast_to(part, (1, LANES)), pl.program_id(0) == 0)

    return pl.pallas_call(body, name=name, grid=(t // tr,), in_specs=[row, row, vec, row],
                          out_specs=[row, pl.BlockSpec((1, LANES), lambda i: (0, 0))],
                          out_shape=[jax.ShapeDtypeStruct((t, d), F32), jax.ShapeDtypeStruct((1, LANES), F32)],
                          compiler_params=_params("arbitrary"))(h, m, w_post, target)


def _norm_bwd(name, dh, pre=None, post=None):
    t, d = dh.shape
    tr = _tile(t, 256, 8)
    row, vec = _row_specs(tr, d)
    has_pre, has_post = pre is not None, post is not None

    def body(*refs):
        it = iter(refs)
        dh_ref = next(it)
        if has_pre:
            du_ref, x_ref, wp_ref = next(it), next(it), next(it)
        if has_post:
            m_ref, wq_ref = next(it), next(it)
        first = pl.program_id(0) == 0
        dh_v = dh_ref[...]
        if has_pre:
            dhn_ref, dwp_ref = next(it), next(it)
            dx, dwr = _rms_bwd(x_ref[...], wp_ref[...], du_ref[...])
            dh_v = dh_v + dx
            dhn_ref[...] = dh_v
            _acc(dwp_ref, jnp.sum(dwr, axis=0, keepdims=True), first)
        if has_post:
            dm_ref, dwq_ref, dms_ref = next(it), next(it), next(it)
            dm, dwr = _rms_bwd(m_ref[...], wq_ref[...], dh_v)
            dm_ref[...] = dm.astype(BF16)
            _acc(dwq_ref, jnp.sum(dwr, axis=0, keepdims=True), first)
            _acc(dms_ref, jnp.sum(dm, axis=0, keepdims=True), first)

    ins, in_specs, out_specs, out_shape = [dh], [row], [], []
    if has_pre:
        ins += list(pre)
        in_specs += [row, row, vec]
        out_specs += [row, vec]
        out_shape += [jax.ShapeDtypeStruct((t, d), F32), jax.ShapeDtypeStruct((1, d), F32)]
    if has_post:
        ins += list(post)
        in_specs += [row, vec]
        out_specs += [row, vec, vec]
        out_shape += [jax.ShapeDtypeStruct((t, d), BF16), jax.ShapeDtypeStruct((1, d), F32),
                      jax.ShapeDtypeStruct((1, d), F32)]
    return pl.pallas_call(body, name=name, grid=(t // tr,), in_specs=in_specs, out_specs=out_specs,
                          out_shape=out_shape, compiler_params=_params("arbitrary"))(*ins)


HALO = 8


def _conv_fwd(zx, col0, n_ch, conv_w, conv_b):
    t = zx.shape[0]
    tc = _tile(n_ch, 512)
    tt = _tile(t, 512, 8)
    cb0 = col0 // tc
    assert col0 % tc == 0
    kw = SSD_CONV_WIDTH

    def body(x_ref, p_ref, w_ref, b_ref, o_ref, xe_ref):
        i = pl.program_id(1)
        cur = x_ref[...]
        xe_ref[0:HALO, :] = jnp.where(i > 0, p_ref[...], 0.0)
        xe_ref[HALO:HALO + tt, :] = cur
        w = w_ref[...]
        acc = b_ref[...] + w[kw - 1:kw, :] * cur
        for k in range(kw - 1):
            acc = acc + w[k:k + 1, :] * xe_ref[pl.ds(HALO - (kw - 1) + k, tt), :]
        o_ref[...] = acc

    return pl.pallas_call(
        body, name="ssd_conv_fwd", grid=(n_ch // tc, t // tt),
        in_specs=[pl.BlockSpec((tt, tc), lambda j, i: (i, cb0 + j)),
                  pl.BlockSpec((HALO, tc), lambda j, i: (jnp.maximum(i * (tt // HALO) - 1, 0), cb0 + j)),
                  pl.BlockSpec((kw, tc), lambda j, i: (0, j)),
                  pl.BlockSpec((1, tc), lambda j, i: (0, j))],
        out_specs=pl.BlockSpec((tt, tc), lambda j, i: (i, j)),
        out_shape=jax.ShapeDtypeStruct((t, n_ch), F32),
        scratch_shapes=[pltpu.VMEM((tt + HALO, tc), F32)],
        compiler_params=_params("parallel", "parallel"))(zx, zx, conv_w, conv_b)


def _conv_bwd(dpre, zx, col0, conv_w):
    t, n_ch = dpre.shape
    tc = _tile(n_ch, 512)
    tt = _tile(t, 512, 8)
    cb0 = col0 // tc
    kw = SSD_CONV_WIDTH
    nt = t // tt

    def body(d_ref, dn_ref, x_ref, p_ref, w_ref, dx_ref, dw_ref, db_ref, de_ref, xe_ref):
        i = pl.program_id(1)
        d = d_ref[...]
        de_ref[0:tt, :] = d
        de_ref[tt:tt + HALO, :] = jnp.where(i < nt - 1, dn_ref[...], 0.0)
        xe_ref[0:HALO, :] = jnp.where(i > 0, p_ref[...], 0.0)
        xe_ref[HALO:HALO + tt, :] = x_ref[...]
        w = w_ref[...]
        dx = w[kw - 1:kw, :] * d
        for k in range(kw - 1):
            dx = dx + w[k:k + 1, :] * de_ref[pl.ds(kw - 1 - k, tt), :]
        dx_ref[...] = dx.astype(BF16)
        first = i == 0
        for k in range(kw):
            xs = xe_ref[pl.ds(HALO - (kw - 1) + k, tt), :]
            val = jnp.sum(d * xs, axis=0, keepdims=True)

            @pl.when(first)
            def _():
                dw_ref[k:k + 1, :] = val

            @pl.when(jnp.logical_not(first))
            def _():
                dw_ref[k:k + 1, :] += val
        _acc(db_ref, jnp.sum(d, axis=0, keepdims=True), first)

    return pl.pallas_call(
        body, name="ssd_conv_bwd", grid=(n_ch // tc, nt),
        in_specs=[pl.BlockSpec((tt, tc), lambda j, i: (i, j)),
                  pl.BlockSpec((HALO, tc), lambda j, i: (jnp.minimum((i + 1) * (tt // HALO), t // HALO - 1), j)),
                  pl.BlockSpec((tt, tc), lambda j, i: (i, cb0 + j)),
                  pl.BlockSpec((HALO, tc), lambda j, i: (jnp.maximum(i * (tt // HALO) - 1, 0), cb0 + j)),
                  pl.BlockSpec((kw, tc), lambda j, i: (0, j))],
        out_specs=[pl.BlockSpec((tt, tc), lambda j, i: (i, j)),
                   pl.BlockSpec((kw, tc), lambda j, i: (0, j)),
                   pl.BlockSpec((1, tc), lambda j, i: (0, j))],
        out_shape=[jax.ShapeDtypeStruct((t, n_ch), BF16), jax.ShapeDtypeStruct((kw, n_ch), F32),
                   jax.ShapeDtypeStruct((1, n_ch), F32)],
        scratch_shapes=[pltpu.VMEM((tt + HALO, tc), F32), pltpu.VMEM((tt + HALO, tc), F32)],
        compiler_params=_params("parallel", "arbitrary"))(dpre, dpre, zx, zx, conv_w)


def _head_of_lane(shape, width):
    return lax.broadcasted_iota(jnp.int32, shape, len(shape) - 1) // width


def _expand(v, n_rows):
    head = _head_of_lane((n_rows, GW), SSD_HEAD_DIM)
    out = jnp.zeros((n_rows, GW), F32)
    for j in range(SSD_HPG):
        out = jnp.where(head == j, v[:, j:j + 1], out)
    return out


def _contract(v, n_rows):
    head = _head_of_lane((n_rows, GW), SSD_HEAD_DIM)
    lane = lax.broadcasted_iota(jnp.int32, (n_rows, LANES), 1)
    out = jnp.zeros((n_rows, LANES), F32)
    for j in range(SSD_HPG):
        s = jnp.sum(jnp.where(head == j, v, 0.0), axis=1, keepdims=True)
        out = jnp.where(lane == j, s, out)
    return out


def _ssd_common(pre, dtc, bias_c, alog_c, dtr, bias_r, alog_r):
    q = SSD_CHUNK
    sg = _sigmoid(pre)
    act = pre * sg
    xa = act[:, :GW]
    bm = act[:, GW:GW + SSD_D_STATE].astype(BF16)
    cm = act[:, GW + SSD_D_STATE:].astype(BF16)
    row = lax.broadcasted_iota(jnp.int32, (q, q), 0)
    col = lax.broadcasted_iota(jnp.int32, (q, q), 1)
    tril = col <= row
    dt = _softplus(dtc + bias_c)
    a_c = -jnp.exp(alog_c)
    cum = _dot_f32(tril.astype(F32), dt * a_c)
    dt_r = _softplus(dtr + bias_r)
    cum_r = _dot_f32(dt_r * (-jnp.exp(alog_r)), (row <= col).astype(F32))
    g = _dot_nt(cm, bm)
    dt_x = _expand(dt, q)
    xdt = xa * dt_x
    cl = cum[q - 1:q, :]
    e_c = jnp.exp(cl - cum)
    lam_c = jnp.exp(cum)
    return dict(sg=sg, xa=xa, bm=bm, cm=cm, tril=tril, row=row, col=col, dt=dt, a_c=a_c, cum=cum, cum_r=cum_r,
                g=g, dt_x=dt_x, xdt=xdt, cl=cl, e_c=e_c, lam_c=lam_c)


def _ssd_specs(nc, rev):
    q = SSD_CHUNK

    def ch(c):
        return nc - 1 - c if rev else c

    chunk_grp = pl.BlockSpec((q, GC), lambda g, c: (ch(c), g))
    col_form = pl.BlockSpec((None, q, LANES), lambda g, c: (g, ch(c), 0))
    row_form = pl.BlockSpec((None, 8, q), lambda g, c: (g, 0, ch(c)))
    col_par = pl.BlockSpec((None, 1, LANES), lambda g, c: (g, 0, 0))
    row_par = pl.BlockSpec((None, 8, 1), lambda g, c: (g, 0, 0))
    y_spec = pl.BlockSpec((q, GW), lambda g, c: (ch(c), g))
    st_spec = pl.BlockSpec((None, None, GW, SSD_D_STATE), lambda g, c: (g, ch(c), 0, 0))
    return chunk_grp, col_form, row_form, col_par, row_par, y_spec, st_spec


def _ssd_fwd(pre, dtc, dtr, bias_c, alog_c, dsk_c, bias_r, alog_r):
    t = pre.shape[0]
    ng = pre.shape[1] // GC
    q = SSD_CHUNK
    nc = t // q
    chunk_grp, col_form, row_form, col_par, row_par, y_spec, st_spec = _ssd_specs(nc, False)

    def body(pre_ref, dtc_ref, dtr_ref, bc_ref, ac_ref, dk_ref, br_ref, ar_ref, y_ref, sp_ref, st_ref):
        @pl.when(pl.program_id(1) == 0)
        def _():
            st_ref[...] = jnp.zeros_like(st_ref)

        v = _ssd_common(pre_ref[...], dtc_ref[...], bc_ref[...], ac_ref[...], dtr_ref[...], br_ref[...], ar_ref[...])
        s0 = st_ref[...]
        sp_ref[...] = s0
        r = _dot_nt(v["cm"], s0.astype(BF16))
        y = _expand(v["lam_c"], q) * r + _expand(dk_ref[...], 1) * v["xa"]
        head = _head_of_lane((q, GW), SSD_HEAD_DIM)
        for j in range(SSD_HPG):
            diff = v["cum"][:, j:j + 1] - v["cum_r"][j:j + 1, :]
            w = (v["g"] * jnp.exp(jnp.where(v["tril"], diff, -jnp.inf))).astype(BF16)
            y = y + _dot(w, jnp.where(head == j, v["xdt"], 0.0).astype(BF16))
        y_ref[...] = y
        ds = _dot_tn((v["xdt"] * _expand(v["e_c"], q)).astype(BF16), v["bm"])
        for j in range(SSD_HPG):
            rows = slice(j * SSD_HEAD_DIM, (j + 1) * SSD_HEAD_DIM)
            st_ref[rows, :] = s0[rows, :] * jnp.exp(v["cum_r"][j:j + 1, q - 1:q]) + ds[rows, :]

    return pl.pallas_call(
        body, name="ssd_scan_fwd", grid=(ng, nc),
        in_specs=[chunk_grp, col_form, row_form, col_par, col_par, col_par, row_par, row_par],
        out_specs=[y_spec, st_spec],
        out_shape=[jax.ShapeDtypeStruct((t, ng * GW), F32), jax.ShapeDtypeStruct((ng, nc, GW, SSD_D_STATE), F32)],
        scratch_shapes=[pltpu.VMEM((GW, SSD_D_STATE), F32)],
        compiler_params=_params("parallel", "arbitrary"))(pre, dtc, dtr, bias_c, alog_c, dsk_c, bias_r, alog_r)


def _ssd_bwd(dy, pre, states, dtc, dtr, bias_c, alog_c, dsk_c, bias_r, alog_r):
    t = pre.shape[0]
    ng = pre.shape[1] // GC
    q = SSD_CHUNK
    nc = t // q
    chunk_grp, col_form, row_form, col_par, row_par, y_spec, st_spec = _ssd_specs(nc, True)

    def body(dy_ref, pre_ref, sp_ref, dtc_ref, dtr_ref, bc_ref, ac_ref, dk_ref, br_ref, ar_ref,
             dpre_ref, ddt_ref, dbias_ref, dalog_ref, dd_ref, ds_ref):
        first = pl.program_id(1) == 0

        @pl.when(first)
        def _():
            ds_ref[...] = jnp.zeros_like(ds_ref)

        pre_v = pre_ref[...]
        v = _ssd_common(pre_v, dtc_ref[...], bc_ref[...], ac_ref[...], dtr_ref[...], br_ref[...], ar_ref[...])
        xa, bm, cm, xdt, cum, cum_r = v["xa"], v["bm"], v["cm"], v["xdt"], v["cum"], v["cum_r"]
        xdt_b = xdt.astype(BF16)
        dy_v = dy_ref[...]
        s0 = sp_ref[...]
        ds1 = ds_ref[...]
        s0b, ds1b = s0.astype(BF16), ds1.astype(BF16)
        head = _head_of_lane((q, GW), SSD_HEAD_DIM)
        lane = lax.broadcasted_iota(jnp.int32, (q, LANES), 1)
        lane1 = lax.broadcasted_iota(jnp.int32, (1, LANES), 1)
        lam_x = _expand(v["lam_c"], q)
        e_x = _expand(v["e_c"], q)

        dxa = _expand(dk_ref[...], 1) * dy_v
        dd = _contract(jnp.sum(dy_v * xa, axis=0, keepdims=True), 1)
        r = _dot_nt(cm, s0b)
        dcum = _contract(dy_v * r * lam_x, q)
        drb = (lam_x * dy_v).astype(BF16)
        dc = _dot(drb, s0b)
        ds0 = _dot_tn(drb, cm)
        extra = jnp.zeros((1, LANES), F32)
        for j in range(SSD_HPG):
            rows = slice(j * SSD_HEAD_DIM, (j + 1) * SSD_HEAD_DIM)
            lam_last = jnp.exp(cum_r[j:j + 1, q - 1:q])
            ds_ref[rows, :] = ds0[rows, :] + lam_last * ds1[rows, :]
            tot = jnp.sum(jnp.sum(ds1[rows, :] * s0[rows, :], axis=1, keepdims=True), axis=0, keepdims=True)
            extra = jnp.where(lane1 == j, lam_last * tot, extra)
        dv = _dot_nt(bm, ds1b)
        db = _dot((xdt * e_x).astype(BF16), ds1b)
        dxdt = e_x * dv
        dee = _contract(dv * xdt, q) * v["e_c"]
        dcum = dcum - dee
        extra = extra + jnp.sum(dee, axis=0, keepdims=True)
        dg = jnp.zeros((q, q), F32)
        for j in range(SSD_HPG):
            diff = cum[:, j:j + 1] - cum_r[j:j + 1, :]
            el = jnp.exp(jnp.where(v["tril"], diff, -jnp.inf))
            gl = v["g"] * el
            dym = jnp.where(head == j, dy_v, 0.0).astype(BF16)
            dwm = _dot_nt(dym, xdt_b)
            dxdt = dxdt + _dot_tn(gl.astype(BF16), dym)
            z = dwm * gl
            rk = jnp.sum(z, axis=1, keepdims=True) - jnp.sum(z.T, axis=1, keepdims=True)
            dcum = jnp.where(lane == j, dcum + rk, dcum)
            dg = dg + dwm * el
        dgb = dg.astype(BF16)
        dc = dc + _dot(dgb, bm)
        db = db + _dot_tn(dgb, cm)
        da = _dot_f32((v["row"] <= v["col"]).astype(F32), dcum) + extra
        ddt = _contract(dxdt * xa, q) + v["a_c"] * da
        dalog = jnp.sum(v["dt"] * da, axis=0, keepdims=True) * v["a_c"]
        dxa = dxa + v["dt_x"] * dxdt
        ddt_raw = jnp.where(lane < SSD_HPG, ddt * _sigmoid(dtc_ref[...] + bc_ref[...]), 0.0)
        sgrad = _silu_grad(pre_v, v["sg"])
        dpre_ref[:, :GW] = dxa * sgrad[:, :GW]
        dpre_ref[:, GW:GW + SSD_D_STATE] = db * sgrad[:, GW:GW + SSD_D_STATE]
        dpre_ref[:, GW + SSD_D_STATE:] = dc * sgrad[:, GW + SSD_D_STATE:]
        ddt_ref[...] = ddt_raw
        _acc(dbias_ref, jnp.sum(ddt_raw, axis=0, keepdims=True), first)
        _acc(dalog_ref, jnp.where(lane1 < SSD_HPG, dalog, 0.0), first)
        _acc(dd_ref, dd, first)

    return pl.pallas_call(
        body, name="ssd_scan_bwd", grid=(ng, nc),
        in_specs=[y_spec, chunk_grp, st_spec, col_form, row_form, col_par, col_par, col_par, row_par, row_par],
        out_specs=[chunk_grp, col_form, col_par, col_par, col_par],
        out_shape=[jax.ShapeDtypeStruct((t, ng * GC), F32), jax.ShapeDtypeStruct((ng, t, LANES), F32),
                   jax.ShapeDtypeStruct((ng, 1, LANES), F32), jax.ShapeDtypeStruct((ng, 1, LANES), F32),
                   jax.ShapeDtypeStruct((ng, 1, LANES), F32)],
        scratch_shapes=[pltpu.VMEM((GW, SSD_D_STATE), F32)],
        compiler_params=_params("parallel", "arbitrary"))(dy, pre, states, dtc, dtr, bias_c, alog_c, dsk_c,
                                                           bias_r, alog_r)


def _gate_norm_fwd(y, zx, norm_w):
    t, di = y.shape
    tr = _tile(t, 256, 8)
    ng = di // GW

    def body(y_ref, z_ref, w_ref, o_ref):
        z = z_ref[...]
        gate = y_ref[...] * (z * _sigmoid(z))
        w = w_ref[...]
        for g in range(ng):
            cols = slice(g * GW, (g + 1) * GW)
            gs = gate[:, cols]
            r = lax.rsqrt(jnp.mean(gs * gs, axis=-1, keepdims=True) + NORM_EPS)
            o_ref[:, cols] = (gs * r * w[:, cols]).astype(BF16)

    row = pl.BlockSpec((tr, di), lambda i: (i, 0))
    return pl.pallas_call(body, name="ssd_gate_norm_fwd", grid=(t // tr,),
                          in_specs=[row, row, pl.BlockSpec((1, di), lambda i: (0, 0))], out_specs=row,
                          out_shape=jax.ShapeDtypeStruct((t, di), BF16), compiler_params=_params("parallel"))(
                              y, zx, norm_w)


def _gate_norm_bwd(dyn, y, zx, norm_w):
    t, di = y.shape
    tr = _tile(t, 256, 8)
    ng = di // GW

    def body(d_ref, y_ref, z_ref, w_ref, dy_ref, dz_ref, dw_ref):
        z = z_ref[...]
        yv = y_ref[...]
        sg = _sigmoid(z)
        sz = z * sg
        gate = yv * sz
        w = w_ref[...]
        d = d_ref[...]
        dsz = _silu_grad(z, sg)
        dws = []
        for g in range(ng):
            cols = slice(g * GW, (g + 1) * GW)
            dg, dwr = _rms_bwd(gate[:, cols], w[:, cols], d[:, cols])
            dy_ref[:, cols] = dg * sz[:, cols]
            dz_ref[:, cols] = (dg * yv[:, cols] * dsz[:, cols]).astype(BF16)
            dws.append(jnp.sum(dwr, axis=0, keepdims=True))
        first = pl.program_id(0) == 0
        for g in range(ng):
            cols = slice(g * GW, (g + 1) * GW)

            @pl.when(first)
            def _():
                dw_ref[:, cols] = dws[g]

            @pl.when(jnp.logical_not(first))
            def _():
                dw_ref[:, cols] += dws[g]

    row = pl.BlockSpec((tr, di), lambda i: (i, 0))
    vec = pl.BlockSpec((1, di), lambda i: (0, 0))
    return pl.pallas_call(body, name="ssd_gate_norm_bwd", grid=(t // tr,), in_specs=[row, row, row, vec],
                          out_specs=[row, row, vec],
                          out_shape=[jax.ShapeDtypeStruct((t, di), F32), jax.ShapeDtypeStruct((t, di), BF16),
                                     jax.ShapeDtypeStruct((1, di), F32)],
                          compiler_params=_params("arbitrary"))(dyn, y, zx, norm_w)


def _attn_mask(n):
    w = ATTN_WINDOW
    qpos = lax.broadcasted_iota(jnp.int32, (w, 2 * w), 0) + w
    kpos = lax.broadcasted_iota(jnp.int32, (w, 2 * w), 1)
    rel = qpos - kpos
    return (rel >= 0) & (rel < w) & jnp.logical_not((n == 0) & (kpos < w))


def _attn_probs(qh, kbh, mask, sink):
    s = _dot_nt(qh, kbh) * (ATTN_HEAD_DIM ** -0.5)
    s = jnp.where(mask, s, -jnp.inf)
    m = jnp.maximum(jnp.max(s, axis=-1, keepdims=True), sink)
    e = jnp.exp(s - m)
    es = jnp.exp(sink - m)
    inv = 1.0 / (jnp.sum(e, axis=-1, keepdims=True) + es)
    return e * inv, es * inv


def _attn_fwd(qkv, sinks):
    t = qkv.shape[0]
    w, hd = ATTN_WINDOW, ATTN_HEAD_DIM
    kd = ATTN_N_KV * hd
    qd = ATTN_REP * kd
    nb = t // w

    def body(q_ref, kc_ref, vc_ref, kp_ref, vp_ref, s_ref, o_ref):
        n = pl.program_id(0)
        mask = _attn_mask(n)
        q = q_ref[...]
        kb = jnp.concatenate([kp_ref[...], kc_ref[...]], axis=0)
        vb = jnp.concatenate([vp_ref[...], vc_ref[...]], axis=0)
        sk = s_ref[...]
        for kv in range(ATTN_N_KV):
            kbh = kb[:, kv * hd:(kv + 1) * hd]
            vbh = vb[:, kv * hd:(kv + 1) * hd]
            for rep in range(ATTN_REP):
                h = kv * ATTN_REP + rep
                p, _ = _attn_probs(q[:, h * hd:(h + 1) * hd], kbh, mask, sk[:, h:h + 1])
                o_ref[:, h * hd:(h + 1) * hd] = _dot(p.astype(BF16), vbh).astype(BF16)

    prev = lambda n: jnp.maximum(n - 1, 0)
    return pl.pallas_call(
        body, name="attn_fwd", grid=(nb,),
        in_specs=[pl.BlockSpec((w, qd), lambda n: (n, 0)),
                  pl.BlockSpec((w, kd), lambda n: (n, ATTN_REP)),
                  pl.BlockSpec((w, kd), lambda n: (n, ATTN_REP + 1)),
                  pl.BlockSpec((w, kd), lambda n: (prev(n), ATTN_REP)),
                  pl.BlockSpec((w, kd), lambda n: (prev(n), ATTN_REP + 1)),
                  pl.BlockSpec((1, sinks.shape[1]), lambda n: (0, 0))],
        out_specs=pl.BlockSpec((w, qd), lambda n: (n, 0)),
        out_shape=jax.ShapeDtypeStruct((t, qd), BF16),
        compiler_params=_params("parallel"))(qkv, qkv, qkv, qkv, qkv, sinks)


def _attn_bwd(qkv, do, sinks):
    t = qkv.shape[0]
    w, hd = ATTN_WINDOW, ATTN_HEAD_DIM
    kd = ATTN_N_KV * hd
    qd = ATTN_REP * kd
    nq = ATTN_N_KV * ATTN_REP
    nb = t // w

    def body(q_ref, kc_ref, vc_ref, kp_ref, vp_ref, do_ref, s_ref,
             dq_ref, dk_ref, dv_ref, bq_ref, bk_ref, bv_ref, dsk_ref, ck_ref, cv_ref):
        n = pl.program_id(0)
        first = n == 0

        @pl.when(first)
        def _():
            ck_ref[...] = jnp.zeros_like(ck_ref)
            cv_ref[...] = jnp.zeros_like(cv_ref)
            bq_ref[...] = jnp.zeros_like(bq_ref)
            bk_ref[...] = jnp.zeros_like(bk_ref)
            bv_ref[...] = jnp.zeros_like(bv_ref)
            dsk_ref[...] = jnp.zeros_like(dsk_ref)

        @pl.when(n < nb)
        def _():
            mask = _attn_mask(n)
            q = q_ref[...]
            dov = do_ref[...]
            kb = jnp.concatenate([kp_ref[...], kc_ref[...]], axis=0)
            vb = jnp.concatenate([vp_ref[...], vc_ref[...]], axis=0)
            sk = s_ref[...]
            lane = lax.broadcasted_iota(jnp.int32, (1, nq), 1)
            dsk = jnp.zeros((1, nq), F32)
            dq_parts, dk_parts, dv_parts = [], [], []
            for kv in range(ATTN_N_KV):
                kbh = kb[:, kv * hd:(kv + 1) * hd]
                vbh = vb[:, kv * hd:(kv + 1) * hd]
                dkh = jnp.zeros((2 * w, hd), F32)
                dvh = jnp.zeros((2 * w, hd), F32)
                for rep in range(ATTN_REP):
                    h = kv * ATTN_REP + rep
                    qh = q[:, h * hd:(h + 1) * hd]
                    doh = dov[:, h * hd:(h + 1) * hd]
                    p, ps = _attn_probs(qh, kbh, mask, sk[:, h:h + 1])
                    pb = p.astype(BF16)
                    dp = _dot_nt(doh, vbh)
                    delta = jnp.sum(p * dp, axis=-1, keepdims=True)
                    dsc = (p * (dp - delta) * (hd ** -0.5)).astype(BF16)
                    dq_parts.append(_dot(dsc, kbh))
                    dkh = dkh + _dot_tn(dsc, qh)
                    dvh = dvh + _dot_tn(pb, doh)
                    dsk = jnp.where(lane == h, -jnp.sum(ps * delta, axis=0, keepdims=True), dsk)
                dk_parts.append(dkh)
                dv_parts.append(dvh)
            dq = jnp.concatenate(dq_parts, axis=1)
            dkb = jnp.concatenate(dk_parts, axis=1)
            dvb = jnp.concatenate(dv_parts, axis=1)
            dq_ref[...] = dq.astype(BF16)
            bq_ref[...] += jnp.sum(dq, axis=0, keepdims=True)
            dsk_ref[...] += dsk
            dk_prev = ck_ref[...] + dkb[:w, :]
            dv_prev = cv_ref[...] + dvb[:w, :]
            dk_ref[...] = dk_prev.astype(BF16)
            dv_ref[...] = dv_prev.astype(BF16)

            @pl.when(n > 0)
            def _():
                bk_ref[...] += jnp.sum(dk_prev, axis=0, keepdims=True)
                bv_ref[...] += jnp.sum(dv_prev, axis=0, keepdims=True)

            ck_ref[...] = dkb[w:, :]
            cv_ref[...] = dvb[w:, :]

        @pl.when(n == nb)
        def _():
            dk_ref[...] = ck_ref[...].astype(BF16)
            dv_ref[...] = cv_ref[...].astype(BF16)
            bk_ref[...] += jnp.sum(ck_ref[...], axis=0, keepdims=True)
            bv_ref[...] += jnp.sum(cv_ref[...], axis=0, keepdims=True)

    cur = lambda n: jnp.minimum(n, nb - 1)
    prev = lambda n: jnp.maximum(jnp.minimum(n, nb - 1) - 1, 0)
    late = lambda n: jnp.maximum(n - 1, 0)
    vec = lambda width: pl.BlockSpec((1, width), lambda n: (0, 0))
    return pl.pallas_call(
        body, name="attn_bwd", grid=(nb + 1,),
        in_specs=[pl.BlockSpec((w, qd), lambda n: (cur(n), 0)),
                  pl.BlockSpec((w, kd), lambda n: (cur(n), ATTN_REP)),
                  pl.BlockSpec((w, kd), lambda n: (cur(n), ATTN_REP + 1)),
                  pl.BlockSpec((w, kd), lambda n: (prev(n), ATTN_REP)),
                  pl.BlockSpec((w, kd), lambda n: (prev(n), ATTN_REP + 1)),
                  pl.BlockSpec((w, qd), lambda n: (cur(n), 0)),
                  vec(nq)],
        out_specs=[pl.BlockSpec((w, qd), lambda n: (cur(n), 0)),
                   pl.BlockSpec((w, kd), lambda n: (late(n), 0)),
                   pl.BlockSpec((w, kd), lambda n: (late(n), 0)),
                   vec(qd), vec(kd), vec(kd), vec(nq)],
        out_shape=[jax.ShapeDtypeStruct((t, qd), BF16), jax.ShapeDtypeStruct((t, kd), BF16),
                   jax.ShapeDtypeStruct((t, kd), BF16), jax.ShapeDtypeStruct((1, qd), F32),
                   jax.ShapeDtypeStruct((1, kd), F32), jax.ShapeDtypeStruct((1, kd), F32),
                   jax.ShapeDtypeStruct((1, nq), F32)],
        scratch_shapes=[pltpu.VMEM((w, kd), F32), pltpu.VMEM((w, kd), F32)],
        compiler_params=_params("arbitrary"))(qkv, qkv, qkv, qkv, qkv, do, sinks)


def _attn_mask_t(n):
    w = ATTN_WINDOW
    kpos = lax.broadcasted_iota(jnp.int32, (2 * w, ATTN_REP * w), 0)
    qpos = lax.broadcasted_iota(jnp.int32, (2 * w, ATTN_REP * w), 1) % w + w
    rel = qpos - kpos
    return (rel >= 0) & (rel < w) & jnp.logical_not((n == 0) & (kpos < w))


def _attn_probs_t(qts, ktb, mask, sink):
    s = _dot_tn(ktb, qts) * (ATTN_HEAD_DIM ** -0.5)
    s = jnp.where(mask, s, -jnp.inf)
    m = jnp.maximum(jnp.max(s, axis=0, keepdims=True), sink)
    e = jnp.exp(s - m)
    es = jnp.exp(sink - m)
    inv = 1.0 / (jnp.sum(e, axis=0, keepdims=True) + es)
    return e * inv, es * inv


def _attn_blocks_t(kv, q_ref, kc_ref, vc_ref, kp_ref, vp_ref):
    hd = ATTN_HEAD_DIM
    rows = slice(kv * hd, (kv + 1) * hd)
    ktb = jnp.concatenate([kp_ref[rows, :], kc_ref[rows, :]], axis=1)
    vtb = jnp.concatenate([vp_ref[rows, :], vc_ref[rows, :]], axis=1)
    qts = jnp.concatenate([q_ref[(kv * ATTN_REP + r) * hd:(kv * ATTN_REP + r + 1) * hd, :]
                           for r in range(ATTN_REP)], axis=1)
    return qts, ktb, vtb


def _attn_specs_t(nb, cur, prev):
    w, hd = ATTN_WINDOW, ATTN_HEAD_DIM
    kd = ATTN_N_KV * hd
    qd = ATTN_REP * kd
    return [pl.BlockSpec((qd, w), lambda n: (0, cur(n))),
            pl.BlockSpec((kd, w), lambda n: (ATTN_REP, cur(n))),
            pl.BlockSpec((kd, w), lambda n: (ATTN_REP + 1, cur(n))),
            pl.BlockSpec((kd, w), lambda n: (ATTN_REP, prev(n))),
            pl.BlockSpec((kd, w), lambda n: (ATTN_REP + 1, prev(n)))]


def _attn_fwd_t(qkv_t, sinks_rep):
    t = qkv_t.shape[1]
    w, hd = ATTN_WINDOW, ATTN_HEAD_DIM
    qd = ATTN_N_KV * ATTN_REP * hd
    nb = t // w

    def body(q_ref, kc_ref, vc_ref, kp_ref, vp_ref, s_ref, o_ref):
        mask = _attn_mask_t(pl.program_id(0))
        for kv in range(ATTN_N_KV):
            qts, ktb, vtb = _attn_blocks_t(kv, q_ref, kc_ref, vc_ref, kp_ref, vp_ref)
            p, _ = _attn_probs_t(qts, ktb, mask, s_ref[kv])
            ots = _dot(vtb, p.astype(BF16))
            for r in range(ATTN_REP):
                h = kv * ATTN_REP + r
                o_ref[h * hd:(h + 1) * hd, :] = ots[:, r * w:(r + 1) * w].astype(BF16)

    return pl.pallas_call(
        body, name="attn_fwd", grid=(nb,),
        in_specs=_attn_specs_t(nb, lambda n: n, lambda n: jnp.maximum(n - 1, 0)) + [
            pl.BlockSpec(sinks_rep.shape, lambda n: (0, 0, 0))],
        out_specs=pl.BlockSpec((qd, w), lambda n: (0, n)),
        out_shape=jax.ShapeDtypeStruct((qd, t), BF16),
        compiler_params=_params("parallel"))(qkv_t, qkv_t, qkv_t, qkv_t, qkv_t, sinks_rep)


def _attn_bwd_t(qkv_t, do_t, sinks_rep):
    t = qkv_t.shape[1]
    w, hd = ATTN_WINDOW, ATTN_HEAD_DIM
    kd = ATTN_N_KV * hd
    qd = ATTN_REP * kd
    nq = ATTN_N_KV * ATTN_REP
    nb = t // w
    rows_all = qd + 2 * kd

    def body(q_ref, kc_ref, vc_ref, kp_ref, vp_ref, do_ref, s_ref, dqkv_ref, bsum_ref, dsk_ref,
             carry_ref, new_ref, bacc_ref, sacc_ref):
        n = pl.program_id(0)

        @pl.when(n == 0)
        def _():
            carry_ref[...] = jnp.zeros_like(carry_ref)
            bacc_ref[...] = jnp.zeros_like(bacc_ref)
            sacc_ref[...] = jnp.zeros_like(sacc_ref)

        @pl.when(n < nb)
        def _():
            mask = _attn_mask_t(n)
            for kv in range(ATTN_N_KV):
                qts, ktb, vtb = _attn_blocks_t(kv, q_ref, kc_ref, vc_ref, kp_ref, vp_ref)
                dots = jnp.concatenate([do_ref[(kv * ATTN_REP + r) * hd:(kv * ATTN_REP + r + 1) * hd, :]
                                        for r in range(ATTN_REP)], axis=1)
                p, ps = _attn_probs_t(qts, ktb, mask, s_ref[kv])
                dpt = _dot_tn(vtb, dots)
                delta = jnp.sum(p * dpt, axis=0, keepdims=True)
                dst = (p * (dpt - delta) * (hd ** -0.5)).astype(BF16)
                dqts = _dot(ktb, dst)
                for r in range(ATTN_REP):
                    h = kv * ATTN_REP + r
                    new_ref[h * hd:(h + 1) * hd, :] = dqts[:, r * w:(r + 1) * w]
                dktb = _dot_nt(qts, dst)
                dvtb = _dot_nt(dots, p.astype(BF16))
                krows = slice(qd + kv * hd, qd + (kv + 1) * hd)
                vrows = slice(qd + kd + kv * hd, qd + kd + (kv + 1) * hd)
                carry_ref[krows, :] += dktb[:, :w]
                carry_ref[vrows, :] += dvtb[:, :w]
                new_ref[krows, :] = dktb[:, w:]
                new_ref[vrows, :] = dvtb[:, w:]
                sacc_ref[kv] += -(ps * delta)

        @pl.when(n >= 1)
        def _():
            done = carry_ref[...]
            dqkv_ref[...] = done.astype(BF16)
            bacc_ref[...] += done

        @pl.when(n < nb)
        def _():
            carry_ref[...] = new_ref[...]

        @pl.when(n == nb)
        def _():
            bsum_ref[...] = jnp.sum(bacc_ref[...], axis=1, keepdims=True)
            lane = lax.broadcasted_iota(jnp.int32, (1, nq), 1)
            dsk = jnp.zeros((1, nq), F32)
            for kv in range(ATTN_N_KV):
                acc = sacc_ref[kv]
                for r in range(ATTN_REP):
                    tot = jnp.sum(acc[:, r * w:(r + 1) * w], axis=1, keepdims=True)
                    dsk = jnp.where(lane == kv * ATTN_REP + r, tot, dsk)
            dsk_ref[...] = dsk

    cur = lambda n: jnp.minimum(n, nb - 1)
    prev = lambda n: jnp.maximum(jnp.minimum(n, nb - 1) - 1, 0)
    return pl.pallas_call(
        body, name="attn_bwd", grid=(nb + 1,),
        in_specs=_attn_specs_t(nb, cur, prev) + [pl.BlockSpec((qd, w), lambda n: (0, cur(n))),
                                                 pl.BlockSpec(sinks_rep.shape, lambda n: (0, 0, 0))],
        out_specs=[pl.BlockSpec((rows_all, w), lambda n: (0, jnp.maximum(n - 1, 0))),
                   pl.BlockSpec((rows_all, 1), lambda n: (0, 0)),
                   pl.BlockSpec((1, nq), lambda n: (0, 0))],
        out_shape=[jax.ShapeDtypeStruct((rows_all, t), BF16), jax.ShapeDtypeStruct((rows_all, 1), F32),
                   jax.ShapeDtypeStruct((1, nq), F32)],
        scratch_shapes=[pltpu.VMEM((rows_all, w), F32), pltpu.VMEM((rows_all, w), F32),
                        pltpu.VMEM((rows_all, w), F32), pltpu.VMEM(sinks_rep.shape, F32)],
        compiler_params=_params("arbitrary"))(qkv_t, qkv_t, qkv_t, qkv_t, qkv_t, do_t, sinks_rep)


HBM_SPEC = pl.BlockSpec(memory_space=pl.ANY)
HBM_ONLY = pl.BlockSpec(memory_space=pltpu.HBM)


def _comm_call(name, body, ins, out_shapes, n_sems):
    return pl.pallas_call(
        body, name=name, in_specs=[HBM_SPEC] * len(ins), out_specs=[HBM_SPEC] * len(out_shapes),
        out_shape=out_shapes,
        scratch_shapes=[pltpu.SemaphoreType.DMA((s,)) for s in n_sems])(*ins)


def _all_gather(name, shards):
    n = len(shards)

    def body(*refs):
        x_refs, out_refs = refs[:n], refs[n:2 * n]
        send_sems, recv_sems, local_sems = refs[2 * n:]
        x, y, c = lax.axis_index("x"), lax.axis_index("y"), lax.axis_index("c")
        me, sibling = (x, y, c), (x, y, 1 - c)
        chips = [(1 - x, y), (x, 1 - y), (1 - x, 1 - y)]

        def slot(i, px, py, pc):
            return out_refs[i].at[4 * px + 2 * py + pc]

        def copy(k, i, block, to, src=None):
            return pltpu.make_async_remote_copy(
                src_ref=slot(i, *block) if src is None else src, dst_ref=slot(i, *block),
                send_sem=send_sems.at[k * n + i], recv_sem=recv_sems.at[k * n + i], device_id=to,
                device_id_type=MESH)

        mine = [pltpu.make_async_copy(x_refs[i], slot(i, *me), local_sems.at[i]) for i in range(n)]
        first = []
        for i in range(n):
            mine[i].start()
            first.append(copy(0, i, me, sibling, src=x_refs[i]))
            first += [copy(1 + j, i, me, (*chip, c), src=x_refs[i]) for j, chip in enumerate(chips)]
        for cp in first:
            cp.start()
        passed = []
        for i in range(n):
            for j, chip in enumerate(chips):
                copy(1 + j, i, (*chip, c), me).wait_recv()
                passed.append(copy(4 + j, i, (*chip, c), sibling))
                passed[-1].start()
        for i in range(n):
            copy(0, i, sibling, me).wait_recv()
            for j, chip in enumerate(chips):
                copy(4 + j, i, (*chip, 1 - c), me).wait_recv()
        for cp in first + passed:
            cp.wait_send()
        for cp in mine:
            cp.wait()

    outs = [jax.ShapeDtypeStruct((N_DEV,) + s.shape, s.dtype) for s in shards]
    return _comm_call(name, body, shards, outs, (7 * n, 7 * n, n))


SEM_SPEC = pl.BlockSpec(memory_space=pltpu.SEMAPHORE)
SPLIT_COPY_EFFECT = pltpu.SideEffectType.DATAFLOW_SIDE_EFFECTING


def _in_hbm(a):
    return pltpu.with_memory_space_constraint(a, pltpu.HBM)


def _split_start(name, body, srcs, lands, n_sems):
    n = len(srcs)
    bufs = [_in_hbm(a) for a in list(srcs) + list(lands)]
    outs = pl.pallas_call(
        body, name=name,
        out_shape=(pltpu.SemaphoreType.DMA((n_sems,)), pltpu.SemaphoreType.DMA((n_sems,)),
                   *[pltpu.HBM(a.shape, a.dtype) for a in bufs], jax.ShapeDtypeStruct((8, LANES), F32)),
        in_specs=[HBM_ONLY] * (2 * n),
        out_specs=(SEM_SPEC, SEM_SPEC, *[HBM_ONLY] * (2 * n), pl.BlockSpec(memory_space=pltpu.VMEM)),
        input_output_aliases={i: 2 + i for i in range(2 * n)},
        compiler_params=pltpu.CompilerParams(has_side_effects=SPLIT_COPY_EFFECT))(*bufs)
    return outs[0], outs[1], list(outs[2:2 + n]), list(outs[2 + n:2 + 2 * n]), outs[-1]


def _split_wait(name, body, send_sems, recv_sems, srcs, lands, after):
    n = len(srcs)
    outs = pl.pallas_call(
        body, name=name,
        out_shape=[pltpu.HBM(a.shape, a.dtype) for a in list(srcs) + list(lands)],
        in_specs=[HBM_ONLY] * (2 * n) + [SEM_SPEC, SEM_SPEC, HBM_SPEC],
        out_specs=[HBM_ONLY] * (2 * n),
        input_output_aliases={i: i for i in range(2 * n)},
        compiler_params=pltpu.CompilerParams(has_side_effects=SPLIT_COPY_EFFECT))(
            *srcs, *lands, send_sems, recv_sems, after)
    return list(outs[:n]), list(outs[n:])


def _gather_peers():
    x, y, c = lax.axis_index("x"), lax.axis_index("y"), lax.axis_index("c")
    return [(x, y, 1 - c), (1 - x, y, c), (x, 1 - y, c), (1 - x, 1 - y, c)]


def _block_id(dev):
    return 4 * dev[0] + 2 * dev[1] + dev[2]


def _gather_start(name, shards):
    n = len(shards)

    def body(*refs):
        x_refs, land_refs = refs[:n], refs[n:2 * n]
        send_sems, recv_sems, token = refs[2 * n], refs[2 * n + 1], refs[-1]
        me = (lax.axis_index("x"), lax.axis_index("y"), lax.axis_index("c"))
        for i in range(n):
            for k, peer in enumerate(_gather_peers()):
                pltpu.make_async_remote_copy(
                    src_ref=x_refs[i], dst_ref=land_refs[i].at[_block_id(me)], send_sem=send_sems.at[4 * i + k],
                    recv_sem=recv_sems.at[4 * i + k], device_id=peer, device_id_type=MESH).start()
        token[...] = jnp.zeros_like(token)

    lands = [lax.empty((N_DEV,) + s.shape, s.dtype) for s in shards]
    return _split_start(name, body, shards, lands, 4 * n)


def _gather_wait(name, send_sems, recv_sems, first, shards, lands, after):
    n = len(shards)

    def body(*refs):
        x_refs, land_refs = refs[:n], refs[n:2 * n]
        send_sems, recv_sems = refs[2 * n], refs[2 * n + 1]
        for i in range(n):
            for k, peer in enumerate(_gather_peers()):
                cp = pltpu.make_async_remote_copy(
                    src_ref=x_refs[i], dst_ref=land_refs[i].at[_block_id(peer)],
                    send_sem=send_sems.at[4 * (first + i) + k], recv_sem=recv_sems.at[4 * (first + i) + k],
                    device_id=peer, device_id_type=MESH)
                cp.wait_send()
                cp.wait_recv()

    return _split_wait(name, body, send_sems, recv_sems, shards, lands, after)


def _gather_forward(name, lands, shards):
    n = len(shards)

    def body(*refs):
        x_refs, out_refs = refs[n:2 * n], refs[2 * n:3 * n]
        send_sems, recv_sems, local_sems = refs[3 * n:]
        x, y, c = lax.axis_index("x"), lax.axis_index("y"), lax.axis_index("c")
        chips = [(1 - x, y), (x, 1 - y), (1 - x, 1 - y)]
        mine = [pltpu.make_async_copy(x_refs[i], out_refs[i].at[_block_id((x, y, c))], local_sems.at[i])
                for i in range(n)]
        passed = [pltpu.make_async_remote_copy(
            src_ref=out_refs[i].at[_block_id((*chip, c))], dst_ref=out_refs[i].at[_block_id((*chip, c))],
            send_sem=send_sems.at[3 * i + j], recv_sem=recv_sems.at[3 * i + j], device_id=(x, y, 1 - c),
            device_id_type=MESH) for i in range(n) for j, chip in enumerate(chips)]
        for cp in mine + passed:
            cp.start()
        for i in range(n):
            for j, chip in enumerate(chips):
                pltpu.make_async_remote_copy(
                    src_ref=out_refs[i].at[_block_id((*chip, c))], dst_ref=out_refs[i].at[_block_id((*chip, 1 - c))],
                    send_sem=send_sems.at[3 * i + j], recv_sem=recv_sems.at[3 * i + j], device_id=(x, y, 1 - c),
                    device_id_type=MESH).wait()
        for cp in mine:
            cp.wait()

    return pl.pallas_call(
        body, name=name, in_specs=[HBM_SPEC] * (2 * n), out_specs=[HBM_SPEC] * n,
        out_shape=[jax.ShapeDtypeStruct(a.shape, a.dtype) for a in lands],
        input_output_aliases={i: i for i in range(n)},
        scratch_shapes=[pltpu.SemaphoreType.DMA((3 * n,)), pltpu.SemaphoreType.DMA((3 * n,)),
                        pltpu.SemaphoreType.DMA((n,))])(*lands, *shards)


def _chip_peers():
    x, y, c = lax.axis_index("x"), lax.axis_index("y"), lax.axis_index("c")
    return [(1 - x, y, c), (x, 1 - y, c), (1 - x, 1 - y, c)]


def _chip_start(name, blocks):
    n = len(blocks)

    def body(*refs):
        p_refs, land_refs = refs[:n], refs[n:2 * n]
        send_sems, recv_sems, token = refs[2 * n], refs[2 * n + 1], refs[-1]
        for i in range(n):
            for j, peer in enumerate(_chip_peers()):
                pltpu.make_async_remote_copy(
                    src_ref=p_refs[i].at[j], dst_ref=land_refs[i].at[j], send_sem=send_sems.at[3 * i + j],
                    recv_sem=recv_sems.at[3 * i + j], device_id=peer, device_id_type=MESH).start()
        token[...] = jnp.zeros_like(token)

    lands = [lax.empty(b.shape, b.dtype) for b in blocks]
    return _split_start(name, body, blocks, lands, 3 * n)


def _chip_wait(name, send_sems, recv_sems, blocks, lands, after):
    n = len(blocks)

    def body(*refs):
        p_refs, land_refs = refs[:n], refs[n:2 * n]
        send_sems, recv_sems = refs[2 * n], refs[2 * n + 1]
        for i in range(n):
            for j, peer in enumerate(_chip_peers()):
                cp = pltpu.make_async_remote_copy(
                    src_ref=p_refs[i].at[j], dst_ref=land_refs[i].at[j], send_sem=send_sems.at[3 * i + j],
                    recv_sem=recv_sems.at[3 * i + j], device_id=peer, device_id_type=MESH)
                cp.wait_send()
                cp.wait_recv()

    return _split_wait(name, body, send_sems, recv_sems, blocks, lands, after)


def _pair_exchange(name, blocks):
    n = len(blocks)

    def body(*refs):
        g_refs, out_refs = refs[:n], refs[n:2 * n]
        send_sems, recv_sems = refs[2 * n:]
        x, y, c = lax.axis_index("x"), lax.axis_index("y"), lax.axis_index("c")
        copies = [pltpu.make_async_remote_copy(
            src_ref=g_refs[i].at[2 * k + 1 - c], dst_ref=out_refs[i].at[k], send_sem=send_sems.at[4 * i + k],
            recv_sem=recv_sems.at[4 * i + k], device_id=(x, y, 1 - c), device_id_type=MESH)
            for i in range(n) for k in range(4)]
        for cp in copies:
            cp.start()
        for cp in copies:
            cp.wait()

    outs = [jax.ShapeDtypeStruct((4,) + b.shape[1:], b.dtype) for b in blocks]
    return _comm_call(name, body, blocks, outs, (4 * n, 4 * n))


def _chip_exchange(name, blocks):
    n = len(blocks)

    def body(*refs):
        p_refs, out_refs = refs[:n], refs[n:2 * n]
        send_sems, recv_sems = refs[2 * n:]
        x, y, c = lax.axis_index("x"), lax.axis_index("y"), lax.axis_index("c")
        chips = [(1 - x, y), (x, 1 - y), (1 - x, 1 - y)]
        copies = [pltpu.make_async_remote_copy(
            src_ref=p_refs[i].at[j], dst_ref=out_refs[i].at[j], send_sem=send_sems.at[3 * i + j],
            recv_sem=recv_sems.at[3 * i + j], device_id=(*chip, c), device_id_type=MESH)
            for i in range(n) for j, chip in enumerate(chips)]
        for cp in copies:
            cp.start()
        for cp in copies:
            cp.wait()

    outs = [jax.ShapeDtypeStruct(b.shape, b.dtype) for b in blocks]
    return _comm_call(name, body, blocks, outs, (3 * n, 3 * n))


def _pair_sum(name, blocks, from_sibling, g_idx, r_idx):
    _, r, c_ = blocks.shape
    tr = _tile(r, 512, 16)

    def body(gi_ref, ri_ref, a_ref, b_ref, own_ref, send_ref):
        k = pl.program_id(1)
        s = a_ref[...] + b_ref[...]

        @pl.when(k == 0)
        def _():
            own_ref[...] = s

        @pl.when(k > 0)
        def _():
            send_ref[...] = s.astype(send_ref.dtype)

    return pl.pallas_call(
        body, name=name,
        grid_spec=pltpu.PrefetchScalarGridSpec(
            num_scalar_prefetch=2, grid=(r // tr, 4),
            in_specs=[pl.BlockSpec((None, tr, c_), lambda i, k, gi, ri: (gi[k], i, 0)),
                      pl.BlockSpec((None, tr, c_), lambda i, k, gi, ri: (ri[k], i, 0))],
            out_specs=[pl.BlockSpec((None, tr, c_), lambda i, k, gi, ri: (0, i, 0)),
                       pl.BlockSpec((None, tr, c_), lambda i, k, gi, ri: (jnp.maximum(k - 1, 0), i, 0))]),
        out_shape=[jax.ShapeDtypeStruct((1, r, c_), F32), jax.ShapeDtypeStruct((3, r, c_), PAYLOAD)],
        compiler_params=_params("parallel", "arbitrary"))(g_idx, r_idx, blocks, from_sibling)


def _adamw(w, g, m, v):
    m = ADAM_B1 * m + (1.0 - ADAM_B1) * g
    v = ADAM_B2 * v + (1.0 - ADAM_B2) * (g * g)
    m_hat = m / (1.0 - ADAM_B1 ** ADAM_STEP)
    v_hat = v / (1.0 - ADAM_B2 ** ADAM_STEP)
    delta = -ADAM_LR * (m_hat / (jnp.sqrt(v_hat) + ADAM_EPS) + ADAM_WD * w)
    return delta, m, v


def _sum_adamw(name, parts_f32, parts_lo, w, m, v):
    r, c_ = w.shape
    tr = _tile(r, 256, 16)
    k1 = parts_f32.shape[0]
    k2 = 0 if parts_lo is None else parts_lo.shape[0]

    def body(*refs):
        a_ref = refs[0]
        b_ref = refs[1] if k2 else None
        w_ref, m_ref, v_ref, g_ref, d_ref, nm_ref, nv_ref = refs[(2 if k2 else 1):]
        g = a_ref[0]
        for k in range(1, k1):
            g = g + a_ref[k]
        for k in range(k2):
            g = g + b_ref[k].astype(F32)
        g_ref[...] = g
        d_ref[...], nm_ref[...], nv_ref[...] = _adamw(w_ref[...], g, m_ref[...], v_ref[...])

    row = pl.BlockSpec((tr, c_), lambda i: (i, 0))
    ins = [parts_f32] + ([parts_lo] if k2 else []) + [w, m, v]
    in_specs = [pl.BlockSpec((k1, tr, c_), lambda i: (0, i, 0))]
    if k2:
        in_specs.append(pl.BlockSpec((k2, tr, c_), lambda i: (0, i, 0)))
    in_specs += [row, row, row]
    return pl.pallas_call(body, name=name, grid=(r // tr,), in_specs=in_specs, out_specs=[row] * 4,
                          out_shape=[jax.ShapeDtypeStruct((r, c_), F32)] * 4,
                          compiler_params=_params("parallel"))(*ins)


def _pack_rows(flat, n_rows, cols):
    pad = n_rows * cols - flat.shape[-1]
    flat = jnp.pad(flat, [(0, 0)] * (flat.ndim - 1) + [(0, pad)])
    return flat.reshape(flat.shape[:-1] + (n_rows, cols))


def _cols_join(blocks):
    return jnp.concatenate([blocks[d] for d in range(N_DEV)], axis=1)


def _cols_split(full):
    c = full.shape[1] // N_DEV
    return jnp.stack([full[:, d * c:(d + 1) * c] for d in range(N_DEV)])


def _rows_join(blocks):
    return blocks.reshape(N_DEV * blocks.shape[1], blocks.shape[2])


def _rows_split(full):
    return full.reshape(N_DEV, full.shape[0] // N_DEV, full.shape[1])


def _perm_xbc(a, ng):
    lead = a.shape[:-1]
    di, gn = ng * GW, ng * SSD_D_STATE
    xs = a[..., :di].reshape(lead + (ng, GW))
    bs = a[..., di:di + gn].reshape(lead + (ng, SSD_D_STATE))
    cs = a[..., di + gn:].reshape(lead + (ng, SSD_D_STATE))
    return jnp.concatenate([xs, bs, cs], axis=-1).reshape(lead + (ng * GC,))


def _unperm_xbc(a, ng):
    lead = a.shape[:-1]
    g = a.reshape(lead + (ng, GC))
    return jnp.concatenate([g[..., :GW].reshape(lead + (ng * GW,)),
                            g[..., GW:GW + SSD_D_STATE].reshape(lead + (ng * SSD_D_STATE,)),
                            g[..., GW + SSD_D_STATE:].reshape(lead + (ng * SSD_D_STATE,))], axis=-1)


def _heads_col(v, ng):
    return jnp.pad(v.reshape(ng, 1, SSD_HPG), ((0, 0), (0, 0), (0, LANES - SSD_HPG)))


def _heads_row(v, ng):
    return jnp.pad(v.reshape(ng, SSD_HPG, 1), ((0, 0), (0, 8 - SSD_HPG), (0, 0)))


MATRIX_ITEMS = ("w_in", "w_out", "up0", "down0", "w_qkv", "w_o", "up1", "down1")
VECTOR_ITEMS = ("conv_w", "b_qkv", "b_o")
ITEMS = MATRIX_ITEMS + VECTOR_ITEMS
GATHER_STAGES = (("w_in", "conv_w"), ("w_out", "up0", "down0"), ("w_qkv", "b_qkv", "w_o", "b_o", "up1", "down1"))


def _items(tree, prefix=""):
    g = lambda k: tree[prefix + k]
    return {"w_in": g("ssd_w_in")[0], "w_out": g("ssd_w_out")[0], "w_qkv": g("attn_w_qkv")[0].T,
            "w_o": g("attn_w_o")[0], "up0": g("mlp_w_up")[0], "up1": g("mlp_w_up")[1],
            "down0": g("mlp_w_down")[0], "down1": g("mlp_w_down")[1], "conv_w": g("ssd_conv_w")[0],
            "b_qkv": g("attn_b_qkv"), "b_o": g("attn_b_o")}


def _from_items(it):
    return {"ssd_w_in": it["w_in"][None], "ssd_w_out": it["w_out"][None], "attn_w_qkv": it["w_qkv"].T[None],
            "attn_w_o": it["w_o"][None], "mlp_w_up": jnp.stack([it["up0"], it["up1"]]),
            "mlp_w_down": jnp.stack([it["down0"], it["down1"]]), "ssd_conv_w": it["conv_w"][None],
            "attn_b_qkv": it["b_qkv"], "attn_b_o": it["b_o"]}


REPLICATED = ("ssd_conv_b", "ssd_dt_bias", "ssd_a_log", "ssd_d", "ssd_norm_w", "attn_sinks", "mix_pre_norm",
              "mix_post_norm", "ffn_pre_norm", "ffn_post_norm")
WEIGHTS = ("ssd_w_in", "ssd_conv_w", "ssd_conv_b", "ssd_dt_bias", "ssd_a_log", "ssd_d", "ssd_norm_w", "ssd_w_out",
           "attn_w_qkv", "attn_b_qkv", "attn_sinks", "attn_w_o", "attn_b_o", "mlp_w_up", "mlp_w_down",
           "mix_pre_norm", "mix_post_norm", "ffn_pre_norm", "ffn_post_norm")


def _forward_backward(x, target, rep, token, weights_of_stage, reduce_grads):
    t, d = x.shape
    ng = rep["ssd_norm_w"].shape[1] // GW
    di = ng * GW
    n_xbc = ng * GC
    nh = ng * SSD_HPG
    grads, blocks = {}, {}
    w_up, w_down = [None, None], [None, None]
    sinks_rep = jnp.repeat(rep["attn_sinks"].reshape(ATTN_N_KV, ATTN_REP, 1), ATTN_WINDOW, axis=2).reshape(
        ATTN_N_KV, 1, ATTN_REP * ATTN_WINDOW)
    conv_b = _perm_xbc(rep["ssd_conv_b"], ng)
    bias_c, alog_c, dsk_c = (_heads_col(rep[k], ng) for k in ("ssd_dt_bias", "ssd_a_log", "ssd_d"))
    bias_r, alog_r = (_heads_row(rep[k], ng) for k in ("ssd_dt_bias", "ssd_a_log"))
    norm = {k: rep[k] for k in ("mix_pre_norm", "mix_post_norm", "ffn_pre_norm", "ffn_post_norm")}

    def nrow(name, i):
        return norm[name][i:i + 1]

    def mlp_fwd(i, u2):
        a, p = _mm(f"mlp{i}_up", [u2], [w_up[i]], "nn", tm=1024, tn=1024, out_dtypes=(F32, BF16),
                   epilogue=lambda acc: (acc, jnp.square(jnp.maximum(acc, 0.0))))
        f = _mm(f"mlp{i}_down", [p], [w_down[i]], "nn", tm=512, tn=1024)
        return a, p, f

    def mlp_bwd(i, df, u2, a, p):
        da = _mm(f"mlp{i}_dact", [df], [w_down[i]], "nt", tm=1024, tn=1024, out_dtypes=(BF16,),
                 tiles=(a,), epilogue=lambda acc, av: (acc * (2.0 * jnp.maximum(av, 0.0)),))
        blocks[f"down{i}"] = _rows_split(_mm(f"mlp{i}_dwdown", [p], [df], "tn", tm=512, tn=1024))
        blocks[f"up{i}"] = _mm(f"mlp{i}_dwup", [u2], [da], "tn", tm=1024, tn=da.shape[1] // N_DEV,
                               col_blocks=True)
        return _mm(f"mlp{i}_dx", [da], [w_up[i]], "nt", tm=512, tn=1024)

    u0 = _prenorm("l0_prenorm", x, nrow("mix_pre_norm", 0), token)
    got = weights_of_stage(0, u0)
    w_in = _cols_join(got["w_in"])
    w_z = w_in[:, :di]
    w_xbc = _perm_xbc(w_in[:, di:di + n_xbc], ng)
    w_dt = jnp.pad(w_in[:, di + n_xbc:], ((0, 0), (0, LANES - nh)))
    w_in_k = jnp.concatenate([w_z, w_xbc, w_dt], axis=1)
    n_in = w_in_k.shape[1]
    conv_w = _perm_xbc(_cols_join(got["conv_w"]), ng)
    zx = _mm("ssd_in_proj", [u0], [w_in_k], "nn", tm=1024, tn=_tile(n_in, 1024))
    pre = _conv_fwd(zx, di, n_xbc, conv_w, conv_b)
    dt_raw = zx[:, di + n_xbc:di + n_xbc + nh].reshape(t, ng, SSD_HPG)
    dtc = jnp.pad(jnp.transpose(dt_raw, (1, 0, 2)), ((0, 0), (0, 0), (0, LANES - SSD_HPG)))
    dtr = jnp.pad(jnp.transpose(dt_raw, (1, 2, 0)), ((0, 0), (0, 8 - SSD_HPG), (0, 0)))
    ssd_args = (dtc, dtr, bias_c, alog_c, dsk_c, bias_r, alog_r)
    y, states = _ssd_fwd(pre, *ssd_args)
    yn = _gate_norm_fwd(y, zx, rep["ssd_norm_w"])
    got = weights_of_stage(1, yn)
    w_out = _rows_join(got["w_out"])
    w_up[0], w_down[0] = _cols_join(got["up0"]), _rows_join(got["down0"])
    mix0 = _mm("ssd_out_proj", [yn], [w_out], "nn", tm=1024, tn=1024)
    h1, u0f = _post_pre("l0_mid", x, mix0, nrow("mix_post_norm", 0), nrow("ffn_pre_norm", 0))
    a0, p0, f0 = mlp_fwd(0, u0f)
    h2, u1 = _post_pre("l1_in", h1, f0, nrow("ffn_post_norm", 0), nrow("mix_pre_norm", 1))
    got = weights_of_stage(2, u1)
    w_qkv_t = _rows_join(got["w_qkv"])
    w_o = _rows_join(got["w_o"])
    b_qkv_col = got["b_qkv"].reshape(-1, 1)
    b_o = _cols_join(got["b_o"])
    w_up[1], w_down[1] = _cols_join(got["up1"]), _rows_join(got["down1"])
    qkv_t = _mm("attn_qkv_proj", [w_qkv_t], [u1], "nt", tm=768, tn=1024, out_dtypes=(BF16,), cols=(b_qkv_col,),
                epilogue=lambda acc, b: (acc + b,))
    ao_t = _attn_fwd_t(qkv_t, sinks_rep)
    mix1 = _mm("attn_out_proj", [ao_t], [w_o], "tn", tm=1024, tn=1024, rows=(b_o,),
               epilogue=lambda acc, b: (acc + b,))
    h3, u1f = _post_pre("l1_mid", h2, mix1, nrow("mix_post_norm", 1), nrow("ffn_pre_norm", 1))
    a1, p1, f1 = mlp_fwd(1, u1f)
    dh, loss_row = _final_loss("loss", h3, f1, nrow("ffn_post_norm", 1), target)

    g_norm = {k: [None, None] for k in norm}
    df1, g_norm["ffn_post_norm"][1], _ = _norm_bwd("l1_ffn_post_bwd", dh, post=(f1, nrow("ffn_post_norm", 1)))
    du = mlp_bwd(1, df1, u1f, a1, p1)
    reduce_grads("mlp1", {k: blocks[k] for k in ("up1", "down1")})
    dh, g_norm["ffn_pre_norm"][1], dmix1, g_norm["mix_post_norm"][1], db_o = _norm_bwd(
        "l1_mid_bwd", dh, pre=(du, h3, nrow("ffn_pre_norm", 1)), post=(mix1, nrow("mix_post_norm", 1)))
    blocks["b_o"] = _cols_split(db_o)
    blocks["w_o"] = _rows_split(_mm("attn_dwo", [ao_t], [dmix1], "nn", tm=512, tn=1024))
    dao_t = _mm("attn_dout", [w_o], [dmix1], "nt", tm=1024, tn=1024, out_dtypes=(BF16,))
    dqkv_t, db_qkv, grads["attn_sinks"] = _attn_bwd_t(qkv_t, dao_t, sinks_rep)
    blocks["b_qkv"] = db_qkv.reshape(N_DEV, 1, -1)
    blocks["w_qkv"] = _rows_split(_mm("attn_dwqkv", [dqkv_t], [u1], "nn", tm=512, tn=1024))
    du = _mm("attn_dx", [dqkv_t], [w_qkv_t], "tn", tm=1024, tn=1024)
    reduce_grads("attn", {k: blocks[k] for k in ("w_o", "w_qkv", "b_o", "b_qkv")})
    dh, g_norm["mix_pre_norm"][1], df0, g_norm["ffn_post_norm"][0], _ = _norm_bwd(
        "l1_in_bwd", dh, pre=(du, h2, nrow("mix_pre_norm", 1)), post=(f0, nrow("ffn_post_norm", 0)))
    du = mlp_bwd(0, df0, u0f, a0, p0)
    reduce_grads("mlp0", {k: blocks[k] for k in ("up0", "down0")})
    dh, g_norm["ffn_pre_norm"][0], dmix0, g_norm["mix_post_norm"][0], _ = _norm_bwd(
        "l0_mid_bwd", dh, pre=(du, h1, nrow("ffn_pre_norm", 0)), post=(mix0, nrow("mix_post_norm", 0)))
    blocks["w_out"] = _rows_split(_mm("ssd_dwout", [yn], [dmix0], "tn", tm=512, tn=1024))
    dyn = _mm("ssd_dyn", [dmix0], [w_out], "nt", tm=1024, tn=1024)
    dy, dz, grads["ssd_norm_w"] = _gate_norm_bwd(dyn, y, zx, rep["ssd_norm_w"])
    dpre, ddt_g, dbias_g, dalog_g, dd_g = _ssd_bwd(dy, pre, states, *ssd_args)
    dxbc, dconv_w, dconv_b = _conv_bwd(dpre, zx, di, conv_w)
    ddt = jnp.transpose(ddt_g[:, :, :SSD_HPG], (1, 0, 2)).reshape(t, nh)
    ddt = jnp.pad(ddt, ((0, 0), (0, LANES - nh))).astype(BF16)
    blocks["conv_w"] = _cols_split(_unperm_xbc(dconv_w, ng))
    grads["ssd_conv_b"] = _unperm_xbc(dconv_b, ng)
    for name, val in (("ssd_dt_bias", dbias_g), ("ssd_a_log", dalog_g), ("ssd_d", dd_g)):
        grads[name] = val[:, 0, :SSD_HPG].reshape(1, nh)
    dw_z = _mm("ssd_dwz", [u0], [dz], "tn", tm=1024, tn=512)
    dw_xbc = _mm("ssd_dwxbc", [u0], [dxbc], "tn", tm=1024, tn=512)
    dw_dt = _mm("ssd_dwdt", [u0], [ddt], "tn", tm=1024, tn=LANES)
    blocks["w_in"] = _cols_split(jnp.concatenate([dw_z, _unperm_xbc(dw_xbc, ng), dw_dt[:, :nh]], axis=1))
    reduce_grads("ssd", {k: blocks[k] for k in ("w_in", "w_out", "conv_w")})
    du = _mm("ssd_dx", [dz, dxbc, ddt], [w_z, w_xbc, w_dt], "nt", tm=256, tn=1024)
    grad_x, g_norm["mix_pre_norm"][0] = _norm_bwd("l0_in_bwd", dh, pre=(du, x, nrow("mix_pre_norm", 0)))
    for k in norm:
        grads[k] = jnp.concatenate(g_norm[k], axis=0)
    return loss_row, grad_x, grads


def kernel(x, ssd_w_in, ssd_conv_w, ssd_conv_b, ssd_dt_bias, ssd_a_log, ssd_d, ssd_norm_w, ssd_w_out, attn_w_qkv, attn_b_qkv, attn_sinks, attn_w_o, attn_b_o, mlp_w_up, mlp_w_down, mix_pre_norm, mix_post_norm, ffn_pre_norm, ffn_post_norm, loss_target, m_ssd_w_in, m_ssd_conv_w, m_ssd_conv_b, m_ssd_dt_bias, m_ssd_a_log, m_ssd_d, m_ssd_norm_w, m_ssd_w_out, m_attn_w_qkv, m_attn_b_qkv, m_attn_sinks, m_attn_w_o, m_attn_b_o, m_mlp_w_up, m_mlp_w_down, m_mix_pre_norm, m_mix_post_norm, m_ffn_pre_norm, m_ffn_post_norm, v_ssd_w_in, v_ssd_conv_w, v_ssd_conv_b, v_ssd_dt_bias, v_ssd_a_log, v_ssd_d, v_ssd_norm_w, v_ssd_w_out, v_attn_w_qkv, v_attn_b_qkv, v_attn_sinks, v_attn_w_o, v_attn_b_o, v_mlp_w_up, v_mlp_w_down, v_mix_pre_norm, v_mix_post_norm, v_ffn_pre_norm, v_ffn_post_norm):
    given = dict(locals())
    w = {k: given[k] for k in WEIGHTS}
    mom_m = {k: given["m_" + k] for k in WEIGHTS}
    mom_v = {k: given["v_" + k] for k in WEIGHTS}
    w_it, m_it, v_it = _items(given), _items(given, "m_"), _items(given, "v_")

    order = [k for stage in GATHER_STAGES for k in stage]
    shards = [w_it[k].astype(PAYLOAD) if k in MATRIX_ITEMS else w_it[k] for k in order]
    g_send, g_recv, shards, lands, token = _gather_start("gather_start", shards)

    def weights_of_stage(s, after):
        first = sum(len(stage) for stage in GATHER_STAGES[:s])
        sl = slice(first, first + len(GATHER_STAGES[s]))
        srcs, got = _gather_wait(f"gather_wait{s}", g_send, g_recv, first, shards[sl], lands[sl], after)
        return dict(zip(GATHER_STAGES[s], _gather_forward(f"gather_forward{s}", got, srcs)))

    ix, iy, ic = lax.axis_index("x"), lax.axis_index("y"), lax.axis_index("c")
    chips = [(ix, iy), (1 - ix, iy), (ix, 1 - iy), (1 - ix, 1 - iy)]
    g_idx = jnp.stack([4 * cx + 2 * cy + ic for cx, cy in chips]).astype(jnp.int32)
    r_idx = jnp.stack([2 * cx + cy for cx, cy in chips]).astype(jnp.int32)
    in_flight = []

    def reduce_grads(tag, blocks):
        keys = list(blocks)
        from_sibling = _pair_exchange(f"rs_pair_exchange_{tag}", [blocks[k] for k in keys])
        sums = [_pair_sum(f"rs_pair_sum_{k}", blocks[k], fs, g_idx, r_idx) for k, fs in zip(keys, from_sibling)]
        started = _chip_start(f"rs_chip_start_{tag}", [s[1] for s in sums])
        in_flight.append((tag, keys, [s[0] for s in sums], started))

    rep = {k: w[k] for k in REPLICATED}
    loss_row, grad_x, grads = _forward_backward(x[0], loss_target[0], rep, token, weights_of_stage, reduce_grads)

    item_out = {}
    for tag, keys, own, (c_send, c_recv, srcs, c_lands, _) in in_flight:
        _, from_chips = _chip_wait(f"rs_chip_wait_{tag}", c_send, c_recv, srcs, c_lands, grad_x)
        for k, o, fc in zip(keys, own, from_chips):
            item_out[k] = _sum_adamw(f"adamw_{k}", o, fc, w_it[k], m_it[k], v_it[k])
    item_out = [item_out[k] for k in ITEMS]

    def pack_rep(tree, last):
        flat = jnp.concatenate([tree[k].reshape(-1) for k in REPLICATED] + [last])
        return _pack_rows(flat, _round_up(-(-flat.shape[0] // LANES), 8), LANES)

    partials, = _all_gather("gather_small_grads", [pack_rep(grads, loss_row[0, :1])])
    zero = jnp.zeros((1,), F32)
    rep_out = _sum_adamw("adamw_replicated", partials, None, pack_rep(w, zero), pack_rep(mom_m, zero),
                         pack_rep(mom_v, zero))

    kinds = []
    for kind, r_arr in enumerate(rep_out):
        tree = _from_items({k: out[kind] for k, out in zip(ITEMS, item_out)})
        flat, off = r_arr.reshape(-1), 0
        for k in REPLICATED:
            tree[k] = flat[off:off + w[k].size].reshape(w[k].shape)
            off += w[k].size
        kinds.append(tree)
    loss = rep_out[0].reshape(-1)[off]
    outs = [loss, grad_x[None]]
    for tree in kinds:
        outs += [tree[k] for k in WEIGHTS]
    return tuple(outs)
```

```python
import functools

import jax
import jax.numpy as jnp
from jax import lax
from jax.experimental import pallas as pl
from jax.experimental.pallas import tpu as pltpu

F32 = jnp.float32
BF16 = jnp.bfloat16
PAYLOAD = jnp.bfloat16
HIGHEST = lax.Precision.HIGHEST
MESH = pl.DeviceIdType.MESH

NORM_EPS = 1e-6
SSD_HEAD_DIM = 64
SSD_N_GROUPS = 8
SSD_HPG = 4
SSD_D_STATE = 128
SSD_CONV_WIDTH = 4
SSD_CHUNK = 128
ATTN_HEAD_DIM = 64
ATTN_N_KV = 4
ATTN_REP = 4
ATTN_WINDOW = 128
ADAM_LR = 0.001
ADAM_B1 = 0.9
ADAM_B2 = 0.999
ADAM_EPS = 1e-08
ADAM_WD = 0.01
ADAM_STEP = 10

N_DEV = 8
LANES = 128
PACK_COLS = 1024
V7X_VMEM_LIMIT = 56 * 1024 * 1024

GW = SSD_HPG * SSD_HEAD_DIM
GC = GW + 2 * SSD_D_STATE


def _params(*sem):
    return pltpu.CompilerParams(dimension_semantics=sem, vmem_limit_bytes=V7X_VMEM_LIMIT)


def _tile(n, pref, mult=LANES):
    best = None
    t = mult
    while t <= min(n, pref):
        if n % t == 0:
            best = t
        t += mult
    return best if best is not None else n


def _round_up(n, m):
    return (n + m - 1) // m * m


def _acc(ref, val, first):
    @pl.when(first)
    def _():
        ref[...] = val

    @pl.when(jnp.logical_not(first))
    def _():
        ref[...] += val


def _dot(a, b):
    return lax.dot_general(a, b, (((1,), (0,)), ((), ())), preferred_element_type=F32)


def _dot_nt(a, b):
    return lax.dot_general(a, b, (((1,), (1,)), ((), ())), preferred_element_type=F32)


def _dot_tn(a, b):
    return lax.dot_general(a, b, (((0,), (0,)), ((), ())), preferred_element_type=F32)


def _dot_f32(a, b):
    return lax.dot_general(a, b, (((1,), (0,)), ((), ())), preferred_element_type=F32, precision=HIGHEST)


_DOTS = {"nn": _dot, "nt": _dot_nt, "tn": _dot_tn}


def _sigmoid(x):
    return 1.0 / (1.0 + jnp.exp(-x))


def _softplus(x):
    return jnp.maximum(x, 0.0) + jnp.log1p(jnp.exp(-jnp.abs(x)))


def _silu_grad(x, s):
    return s * (1.0 + x * (1.0 - s))


def _mm(name, a_list, b_list, mode, *, tm, tn, out_dtypes=(F32,), epilogue=None, tiles=(), rows=(), cols=(),
        col_blocks=False):
    npair = len(a_list)
    if mode == "tn":
        m = a_list[0].shape[1]
    else:
        m = a_list[0].shape[0]
    n = b_list[0].shape[0] if mode == "nt" else b_list[0].shape[1]
    tm = _tile(m, tm, LANES if mode == "tn" else 8)
    tn = _tile(n, tn)
    assert m % tm == 0 and n % tn == 0, (name, m, n, tm, tn)
    dot = _DOTS[mode]

    def body(*refs):
        a_refs = refs[:npair]
        b_refs = refs[npair:2 * npair]
        n_extra = len(tiles) + len(rows) + len(cols)
        e_refs = refs[2 * npair:2 * npair + n_extra]
        o_refs = refs[2 * npair + n_extra:]
        acc = None
        for ar, br in zip(a_refs, b_refs):
            d = dot(ar[...], br[...])
            acc = d if acc is None else acc + d
        outs = epilogue(acc, *[e[...] for e in e_refs]) if epilogue is not None else (acc,)
        for o, v in zip(o_refs, outs):
            o[...] = v.astype(o.dtype)

    in_specs = []
    for a in a_list:
        if mode == "tn":
            in_specs.append(pl.BlockSpec((a.shape[0], tm), lambda i, j: (0, i)))
        else:
            in_specs.append(pl.BlockSpec((tm, a.shape[1]), lambda i, j: (i, 0)))
    for b in b_list:
        if mode == "nt":
            in_specs.append(pl.BlockSpec((tn, b.shape[1]), lambda i, j: (j, 0)))
        else:
            in_specs.append(pl.BlockSpec((b.shape[0], tn), lambda i, j: (0, j)))
    in_specs += [pl.BlockSpec((tm, tn), lambda i, j: (i, j)) for _ in tiles]
    in_specs += [pl.BlockSpec((1, tn), lambda i, j: (0, j)) for _ in rows]
    in_specs += [pl.BlockSpec((tm, 1), lambda i, j: (i, 0)) for _ in cols]
    outs = pl.pallas_call(
        body,
        name=name,
        grid=(m // tm, n // tn),
        in_specs=in_specs,
        out_specs=[pl.BlockSpec((None, tm, tn), lambda i, j: (j, i, 0)) if col_blocks else
                   pl.BlockSpec((tm, tn), lambda i, j: (i, j)) for _ in out_dtypes],
        out_shape=[jax.ShapeDtypeStruct((n // tn, m, tn) if col_blocks else (m, n), dt) for dt in out_dtypes],
        compiler_params=_params("parallel", "parallel"),
    )(*a_list, *b_list, *tiles, *rows, *cols)
    return outs[0] if len(out_dtypes) == 1 else outs


def _rms(x, w):
    r = lax.rsqrt(jnp.mean(x * x, axis=-1, keepdims=True) + NORM_EPS)
    return x * r * w


def _rms_bwd(x, w, dy):
    r = lax.rsqrt(jnp.mean(x * x, axis=-1, keepdims=True) + NORM_EPS)
    xh = x * r
    g = dy * w
    dx = r * (g - xh * jnp.mean(g * xh, axis=-1, keepdims=True))
    return dx, dy * xh


def _row_specs(tr, d):
    return pl.BlockSpec((tr, d), lambda i: (i, 0)), pl.BlockSpec((1, d), lambda i: (0, 0))


def _prenorm(name, h, w, after):
    t, d = h.shape
    tr = _tile(t, 512, 8)
    row, vec = _row_specs(tr, d)

    def body(h_ref, w_ref, after_ref, u_ref):
        u_ref[...] = _rms(h_ref[...], w_ref[...]).astype(BF16)

    return pl.pallas_call(body, name=name, grid=(t // tr,),
                          in_specs=[row, vec, pl.BlockSpec((8, LANES), lambda i: (0, 0))], out_specs=row,
                          out_shape=jax.ShapeDtypeStruct((t, d), BF16), compiler_params=_params("parallel"))(
                              h, w, after)


def _post_pre(name, h, m, w_post, w_pre):
    t, d = h.shape
    tr = _tile(t, 512, 8)
    row, vec = _row_specs(tr, d)

    def body(h_ref, m_ref, wq_ref, wp_ref, hn_ref, u_ref):
        hn = h_ref[...] + _rms(m_ref[...], wq_ref[...])
        hn_ref[...] = hn
        u_ref[...] = _rms(hn, wp_ref[...]).astype(BF16)

    return pl.pallas_call(body, name=name, grid=(t // tr,), in_specs=[row, row, vec, vec], out_specs=[row, row],
                          out_shape=[jax.ShapeDtypeStruct((t, d), F32), jax.ShapeDtypeStruct((t, d), BF16)],
                          compiler_params=_params("parallel"))(h, m, w_post, w_pre)


def _final_loss(name, h, m, w_post, target):
    t, d = h.shape
    tr = _tile(t, 512, 8)
    row, vec = _row_specs(tr, d)

    def body(h_ref, m_ref, wq_ref, t_ref, dh_ref, loss_ref):
        err = h_ref[...] + _rms(m_ref[...], wq_ref[...]) - t_ref[...]
        dh_ref[...] = err * (1.0 / d)
        part = 0.5 * jnp.sum(jnp.mean(err * err, axis=-1, keepdims=True), axis=0, keepdims=True)
        _acc(loss_ref, jnp.broadcast_to(part, (1, LANES)), pl.program_id(0) == 0)

    return pl.pallas_call(body, name=name, grid=(t // tr,), in_specs=[row, row, vec, row],
                          out_specs=[row, pl.BlockSpec((1, LANES), lambda i: (0, 0))],
                          out_shape=[jax.ShapeDtypeStruct((t, d), F32), jax.ShapeDtypeStruct((1, LANES), F32)],
                          compiler_params=_params("arbitrary"))(h, m, w_post, target)


def _norm_bwd(name, dh, pre=None, post=None, after=None):
    t, d = dh.shape
    tr = _tile(t, 256, 8)
    row, vec = _row_specs(tr, d)
    has_pre, has_post = pre is not None, post is not None

    def body(*refs):
        it = iter(refs)
        dh_ref = next(it)
        if has_pre:
            du_ref, x_ref, wp_ref = next(it), next(it), next(it)
        if has_post:
            m_ref, wq_ref = next(it), next(it)
        if after is not None:
            next(it)
        first = pl.program_id(0) == 0
        dh_v = dh_ref[...]
        if has_pre:
            dhn_ref, dwp_ref = next(it), next(it)
            dx, dwr = _rms_bwd(x_ref[...], wp_ref[...], du_ref[...])
            dh_v = dh_v + dx
            dhn_ref[...] = dh_v
            _acc(dwp_ref, jnp.sum(dwr, axis=0, keepdims=True), first)
        if has_post:
            dm_ref, dwq_ref, dms_ref = next(it), next(it), next(it)
            dm, dwr = _rms_bwd(m_ref[...], wq_ref[...], dh_v)
            dm_ref[...] = dm.astype(BF16)
            _acc(dwq_ref, jnp.sum(dwr, axis=0, keepdims=True), first)
            _acc(dms_ref, jnp.sum(dm, axis=0, keepdims=True), first)

    ins, in_specs, out_specs, out_shape = [dh], [row], [], []
    if has_pre:
        ins += list(pre)
        in_specs += [row, row, vec]
        out_specs += [row, vec]
        out_shape += [jax.ShapeDtypeStruct((t, d), F32), jax.ShapeDtypeStruct((1, d), F32)]
    if has_post:
        ins += list(post)
        in_specs += [row, vec]
        out_specs += [row, vec, vec]
        out_shape += [jax.ShapeDtypeStruct((t, d), BF16), jax.ShapeDtypeStruct((1, d), F32),
                      jax.ShapeDtypeStruct((1, d), F32)]
    if after is not None:
        ins.append(after)
        in_specs.append(pl.BlockSpec((8, LANES), lambda i: (0, 0)))
    return pl.pallas_call(body, name=name, grid=(t // tr,), in_specs=in_specs, out_specs=out_specs,
                          out_shape=out_shape, compiler_params=_params("arbitrary"))(*ins)


HALO = 8


def _conv_fwd(zx, col0, n_ch, conv_w, conv_b):
    t = zx.shape[0]
    tc = _tile(n_ch, 512)
    tt = _tile(t, 512, 8)
    cb0 = col0 // tc
    assert col0 % tc == 0
    kw = SSD_CONV_WIDTH

    def body(x_ref, p_ref, w_ref, b_ref, o_ref, xe_ref):
        i = pl.program_id(1)
        cur = x_ref[...]
        xe_ref[0:HALO, :] = jnp.where(i > 0, p_ref[...], 0.0)
        xe_ref[HALO:HALO + tt, :] = cur
        w = w_ref[...]
        acc = b_ref[...] + w[kw - 1:kw, :] * cur
        for k in range(kw - 1):
            acc = acc + w[k:k + 1, :] * xe_ref[pl.ds(HALO - (kw - 1) + k, tt), :]
        o_ref[...] = acc

    return pl.pallas_call(
        body, name="ssd_conv_fwd", grid=(n_ch // tc, t // tt),
        in_specs=[pl.BlockSpec((tt, tc), lambda j, i: (i, cb0 + j)),
                  pl.BlockSpec((HALO, tc), lambda j, i: (jnp.maximum(i * (tt // HALO) - 1, 0), cb0 + j)),
                  pl.BlockSpec((kw, tc), lambda j, i: (0, j)),
                  pl.BlockSpec((1, tc), lambda j, i: (0, j))],
        out_specs=pl.BlockSpec((tt, tc), lambda j, i: (i, j)),
        out_shape=jax.ShapeDtypeStruct((t, n_ch), F32),
        scratch_shapes=[pltpu.VMEM((tt + HALO, tc), F32)],
        compiler_params=_params("parallel", "parallel"))(zx, zx, conv_w, conv_b)


def _conv_bwd(dpre, zx, col0, conv_w):
    t, n_ch = dpre.shape
    tc = _tile(n_ch, 512)
    tt = _tile(t, 512, 8)
    cb0 = col0 // tc
    kw = SSD_CONV_WIDTH
    nt = t // tt

    def body(d_ref, dn_ref, x_ref, p_ref, w_ref, dx_ref, dw_ref, db_ref, de_ref, xe_ref):
        i = pl.program_id(1)
        d = d_ref[...]
        de_ref[0:tt, :] = d
        de_ref[tt:tt + HALO, :] = jnp.where(i < nt - 1, dn_ref[...], 0.0)
        xe_ref[0:HALO, :] = jnp.where(i > 0, p_ref[...], 0.0)
        xe_ref[HALO:HALO + tt, :] = x_ref[...]
        w = w_ref[...]
        dx = w[kw - 1:kw, :] * d
        for k in range(kw - 1):
            dx = dx + w[k:k + 1, :] * de_ref[pl.ds(kw - 1 - k, tt), :]
        dx_ref[...] = dx.astype(BF16)
        first = i == 0
        for k in range(kw):
            xs = xe_ref[pl.ds(HALO - (kw - 1) + k, tt), :]
            val = jnp.sum(d * xs, axis=0, keepdims=True)

            @pl.when(first)
            def _():
                dw_ref[k:k + 1, :] = val

            @pl.when(jnp.logical_not(first))
            def _():
                dw_ref[k:k + 1, :] += val
        _acc(db_ref, jnp.sum(d, axis=0, keepdims=True), first)

    return pl.pallas_call(
        body, name="ssd_conv_bwd", grid=(n_ch // tc, nt),
        in_specs=[pl.BlockSpec((tt, tc), lambda j, i: (i, j)),
                  pl.BlockSpec((HALO, tc), lambda j, i: (jnp.minimum((i + 1) * (tt // HALO), t // HALO - 1), j)),
                  pl.BlockSpec((tt, tc), lambda j, i: (i, cb0 + j)),
                  pl.BlockSpec((HALO, tc), lambda j, i: (jnp.maximum(i * (tt // HALO) - 1, 0), cb0 + j)),
                  pl.BlockSpec((kw, tc), lambda j, i: (0, j))],
        out_specs=[pl.BlockSpec((tt, tc), lambda j, i: (i, j)),
                   pl.BlockSpec((kw, tc), lambda j, i: (0, j)),
                   pl.BlockSpec((1, tc), lambda j, i: (0, j))],
        out_shape=[jax.ShapeDtypeStruct((t, n_ch), BF16), jax.ShapeDtypeStruct((kw, n_ch), F32),
                   jax.ShapeDtypeStruct((1, n_ch), F32)],
        scratch_shapes=[pltpu.VMEM((tt + HALO, tc), F32), pltpu.VMEM((tt + HALO, tc), F32)],
        compiler_params=_params("parallel", "arbitrary"))(dpre, dpre, zx, zx, conv_w)


def _head_of_lane(shape, width):
    return lax.broadcasted_iota(jnp.int32, shape, len(shape) - 1) // width


def _expand(v, n_rows):
    head = _head_of_lane((n_rows, GW), SSD_HEAD_DIM)
    out = jnp.zeros((n_rows, GW), F32)
    for j in range(SSD_HPG):
        out = jnp.where(head == j, v[:, j:j + 1], out)
    return out


def _contract(v, n_rows):
    head = _head_of_lane((n_rows, GW), SSD_HEAD_DIM)
    lane = lax.broadcasted_iota(jnp.int32, (n_rows, LANES), 1)
    out = jnp.zeros((n_rows, LANES), F32)
    for j in range(SSD_HPG):
        s = jnp.sum(jnp.where(head == j, v, 0.0), axis=1, keepdims=True)
        out = jnp.where(lane == j, s, out)
    return out


def _ssd_common(pre, dtc, bias_c, alog_c, dtr, bias_r, alog_r):
    q = SSD_CHUNK
    sg = _sigmoid(pre)
    act = pre * sg
    xa = act[:, :GW]
    bm = act[:, GW:GW + SSD_D_STATE].astype(BF16)
    cm = act[:, GW + SSD_D_STATE:].astype(BF16)
    row = lax.broadcasted_iota(jnp.int32, (q, q), 0)
    col = lax.broadcasted_iota(jnp.int32, (q, q), 1)
    tril = col <= row
    dt = _softplus(dtc + bias_c)
    a_c = -jnp.exp(alog_c)
    cum = _dot_f32(tril.astype(F32), dt * a_c)
    dt_r = _softplus(dtr + bias_r)
    cum_r = _dot_f32(dt_r * (-jnp.exp(alog_r)), (row <= col).astype(F32))
    g = _dot_nt(cm, bm)
    dt_x = _expand(dt, q)
    xdt = xa * dt_x
    cl = cum[q - 1:q, :]
    e_c = jnp.exp(cl - cum)
    lam_c = jnp.exp(cum)
    return dict(sg=sg, xa=xa, bm=bm, cm=cm, tril=tril, row=row, col=col, dt=dt, a_c=a_c, cum=cum, cum_r=cum_r,
                g=g, dt_x=dt_x, xdt=xdt, cl=cl, e_c=e_c, lam_c=lam_c)


def _ssd_specs(nc, rev):
    q = SSD_CHUNK

    def ch(c):
        return nc - 1 - c if rev else c

    chunk_grp = pl.BlockSpec((q, GC), lambda g, c: (ch(c), g))
    col_form = pl.BlockSpec((None, q, LANES), lambda g, c: (g, ch(c), 0))
    row_form = pl.BlockSpec((None, 8, q), lambda g, c: (g, 0, ch(c)))
    col_par = pl.BlockSpec((None, 1, LANES), lambda g, c: (g, 0, 0))
    row_par = pl.BlockSpec((None, 8, 1), lambda g, c: (g, 0, 0))
    y_spec = pl.BlockSpec((q, GW), lambda g, c: (ch(c), g))
    st_spec = pl.BlockSpec((None, None, GW, SSD_D_STATE), lambda g, c: (g, ch(c), 0, 0))
    return chunk_grp, col_form, row_form, col_par, row_par, y_spec, st_spec


def _ssd_fwd(pre, dtc, dtr, bias_c, alog_c, dsk_c, bias_r, alog_r):
    t = pre.shape[0]
    ng = pre.shape[1] // GC
    q = SSD_CHUNK
    nc = t // q
    chunk_grp, col_form, row_form, col_par, row_par, y_spec, st_spec = _ssd_specs(nc, False)

    def body(pre_ref, dtc_ref, dtr_ref, bc_ref, ac_ref, dk_ref, br_ref, ar_ref, y_ref, sp_ref, st_ref):
        @pl.when(pl.program_id(1) == 0)
        def _():
            st_ref[...] = jnp.zeros_like(st_ref)

        v = _ssd_common(pre_ref[...], dtc_ref[...], bc_ref[...], ac_ref[...], dtr_ref[...], br_ref[...], ar_ref[...])
        s0 = st_ref[...]
        sp_ref[...] = s0
        r = _dot_nt(v["cm"], s0.astype(BF16))
        y = _expand(v["lam_c"], q) * r + _expand(dk_ref[...], 1) * v["xa"]
        head = _head_of_lane((q, GW), SSD_HEAD_DIM)
        for j in range(SSD_HPG):
            diff = v["cum"][:, j:j + 1] - v["cum_r"][j:j + 1, :]
            w = (v["g"] * jnp.exp(jnp.where(v["tril"], diff, -jnp.inf))).astype(BF16)
            y = y + _dot(w, jnp.where(head == j, v["xdt"], 0.0).astype(BF16))
        y_ref[...] = y
        ds = _dot_tn((v["xdt"] * _expand(v["e_c"], q)).astype(BF16), v["bm"])
        for j in range(SSD_HPG):
            rows = slice(j * SSD_HEAD_DIM, (j + 1) * SSD_HEAD_DIM)
            st_ref[rows, :] = s0[rows, :] * jnp.exp(v["cum_r"][j:j + 1, q - 1:q]) + ds[rows, :]

    return pl.pallas_call(
        body, name="ssd_scan_fwd", grid=(ng, nc),
        in_specs=[chunk_grp, col_form, row_form, col_par, col_par, col_par, row_par, row_par],
        out_specs=[y_spec, st_spec],
        out_shape=[jax.ShapeDtypeStruct((t, ng * GW), F32), jax.ShapeDtypeStruct((ng, nc, GW, SSD_D_STATE), F32)],
        scratch_shapes=[pltpu.VMEM((GW, SSD_D_STATE), F32)],
        compiler_params=_params("parallel", "arbitrary"))(pre, dtc, dtr, bias_c, alog_c, dsk_c, bias_r, alog_r)


def _ssd_bwd(dy, pre, states, dtc, dtr, bias_c, alog_c, dsk_c, bias_r, alog_r):
    t = pre.shape[0]
    ng = pre.shape[1] // GC
    q = SSD_CHUNK
    nc = t // q
    chunk_grp, col_form, row_form, col_par, row_par, y_spec, st_spec = _ssd_specs(nc, True)

    def body(dy_ref, pre_ref, sp_ref, dtc_ref, dtr_ref, bc_ref, ac_ref, dk_ref, br_ref, ar_ref,
             dpre_ref, ddt_ref, dbias_ref, dalog_ref, dd_ref, ds_ref):
        first = pl.program_id(1) == 0

        @pl.when(first)
        def _():
            ds_ref[...] = jnp.zeros_like(ds_ref)

        pre_v = pre_ref[...]
        v = _ssd_common(pre_v, dtc_ref[...], bc_ref[...], ac_ref[...], dtr_ref[...], br_ref[...], ar_ref[...])
        xa, bm, cm, xdt, cum, cum_r = v["xa"], v["bm"], v["cm"], v["xdt"], v["cum"], v["cum_r"]
        xdt_b = xdt.astype(BF16)
        dy_v = dy_ref[...]
        s0 = sp_ref[...]
        ds1 = ds_ref[...]
        s0b, ds1b = s0.astype(BF16), ds1.astype(BF16)
        head = _head_of_lane((q, GW), SSD_HEAD_DIM)
        lane = lax.broadcasted_iota(jnp.int32, (q, LANES), 1)
        lane1 = lax.broadcasted_iota(jnp.int32, (1, LANES), 1)
        lam_x = _expand(v["lam_c"], q)
        e_x = _expand(v["e_c"], q)

        dxa = _expand(dk_ref[...], 1) * dy_v
        dd = _contract(jnp.sum(dy_v * xa, axis=0, keepdims=True), 1)
        r = _dot_nt(cm, s0b)
        dcum = _contract(dy_v * r * lam_x, q)
        drb = (lam_x * dy_v).astype(BF16)
        dc = _dot(drb, s0b)
        ds0 = _dot_tn(drb, cm)
        extra = jnp.zeros((1, LANES), F32)
        for j in range(SSD_HPG):
            rows = slice(j * SSD_HEAD_DIM, (j + 1) * SSD_HEAD_DIM)
            lam_last = jnp.exp(cum_r[j:j + 1, q - 1:q])
            ds_ref[rows, :] = ds0[rows, :] + lam_last * ds1[rows, :]
            tot = jnp.sum(jnp.sum(ds1[rows, :] * s0[rows, :], axis=1, keepdims=True), axis=0, keepdims=True)
            extra = jnp.where(lane1 == j, lam_last * tot, extra)
        dv = _dot_nt(bm, ds1b)
        db = _dot((xdt * e_x).astype(BF16), ds1b)
        dxdt = e_x * dv
        dee = _contract(dv * xdt, q) * v["e_c"]
        dcum = dcum - dee
        extra = extra + jnp.sum(dee, axis=0, keepdims=True)
        dg = jnp.zeros((q, q), F32)
        for j in range(SSD_HPG):
            diff = cum[:, j:j + 1] - cum_r[j:j + 1, :]
            el = jnp.exp(jnp.where(v["tril"], diff, -jnp.inf))
            gl = v["g"] * el
            dym = jnp.where(head == j, dy_v, 0.0).astype(BF16)
            dwm = _dot_nt(dym, xdt_b)
            dxdt = dxdt + _dot_tn(gl.astype(BF16), dym)
            z = dwm * gl
            rk = jnp.sum(z, axis=1, keepdims=True) - jnp.sum(z.T, axis=1, keepdims=True)
            dcum = jnp.where(lane == j, dcum + rk, dcum)
            dg = dg + dwm * el
        dgb = dg.astype(BF16)
        dc = dc + _dot(dgb, bm)
        db = db + _dot_tn(dgb, cm)
        da = _dot_f32((v["row"] <= v["col"]).astype(F32), dcum) + extra
        ddt = _contract(dxdt * xa, q) + v["a_c"] * da
        dalog = jnp.sum(v["dt"] * da, axis=0, keepdims=True) * v["a_c"]
        dxa = dxa + v["dt_x"] * dxdt
        ddt_raw = jnp.where(lane < SSD_HPG, ddt * _sigmoid(dtc_ref[...] + bc_ref[...]), 0.0)
        sgrad = _silu_grad(pre_v, v["sg"])
        dpre_ref[:, :GW] = dxa * sgrad[:, :GW]
        dpre_ref[:, GW:GW + SSD_D_STATE] = db * sgrad[:, GW:GW + SSD_D_STATE]
        dpre_ref[:, GW + SSD_D_STATE:] = dc * sgrad[:, GW + SSD_D_STATE:]
        ddt_ref[...] = ddt_raw
        _acc(dbias_ref, jnp.sum(ddt_raw, axis=0, keepdims=True), first)
        _acc(dalog_ref, jnp.where(lane1 < SSD_HPG, dalog, 0.0), first)
        _acc(dd_ref, dd, first)

    return pl.pallas_call(
        body, name="ssd_scan_bwd", grid=(ng, nc),
        in_specs=[y_spec, chunk_grp, st_spec, col_form, row_form, col_par, col_par, col_par, row_par, row_par],
        out_specs=[chunk_grp, col_form, col_par, col_par, col_par],
        out_shape=[jax.ShapeDtypeStruct((t, ng * GC), F32), jax.ShapeDtypeStruct((ng, t, LANES), F32),
                   jax.ShapeDtypeStruct((ng, 1, LANES), F32), jax.ShapeDtypeStruct((ng, 1, LANES), F32),
                   jax.ShapeDtypeStruct((ng, 1, LANES), F32)],
        scratch_shapes=[pltpu.VMEM((GW, SSD_D_STATE), F32)],
        compiler_params=_params("parallel", "arbitrary"))(dy, pre, states, dtc, dtr, bias_c, alog_c, dsk_c,
                                                           bias_r, alog_r)


def _gate_norm_fwd(y, zx, norm_w):
    t, di = y.shape
    tr = _tile(t, 256, 8)
    ng = di // GW

    def body(y_ref, z_ref, w_ref, o_ref):
        z = z_ref[...]
        gate = y_ref[...] * (z * _sigmoid(z))
        w = w_ref[...]
        for g in range(ng):
            cols = slice(g * GW, (g + 1) * GW)
            gs = gate[:, cols]
            r = lax.rsqrt(jnp.mean(gs * gs, axis=-1, keepdims=True) + NORM_EPS)
            o_ref[:, cols] = (gs * r * w[:, cols]).astype(BF16)

    row = pl.BlockSpec((tr, di), lambda i: (i, 0))
    return pl.pallas_call(body, name="ssd_gate_norm_fwd", grid=(t // tr,),
                          in_specs=[row, row, pl.BlockSpec((1, di), lambda i: (0, 0))], out_specs=row,
                          out_shape=jax.ShapeDtypeStruct((t, di), BF16), compiler_params=_params("parallel"))(
                              y, zx, norm_w)


def _gate_norm_bwd(dyn, y, zx, norm_w):
    t, di = y.shape
    tr = _tile(t, 256, 8)
    ng = di // GW

    def body(d_ref, y_ref, z_ref, w_ref, dy_ref, dz_ref, dw_ref):
        z = z_ref[...]
        yv = y_ref[...]
        sg = _sigmoid(z)
        sz = z * sg
        gate = yv * sz
        w = w_ref[...]
        d = d_ref[...]
        dsz = _silu_grad(z, sg)
        dws = []
        for g in range(ng):
            cols = slice(g * GW, (g + 1) * GW)
            dg, dwr = _rms_bwd(gate[:, cols], w[:, cols], d[:, cols])
            dy_ref[:, cols] = dg * sz[:, cols]
            dz_ref[:, cols] = (dg * yv[:, cols] * dsz[:, cols]).astype(BF16)
            dws.append(jnp.sum(dwr, axis=0, keepdims=True))
        first = pl.program_id(0) == 0
        for g in range(ng):
            cols = slice(g * GW, (g + 1) * GW)

            @pl.when(first)
            def _():
                dw_ref[:, cols] = dws[g]

            @pl.when(jnp.logical_not(first))
            def _():
                dw_ref[:, cols] += dws[g]

    row = pl.BlockSpec((tr, di), lambda i: (i, 0))
    vec = pl.BlockSpec((1, di), lambda i: (0, 0))
    return pl.pallas_call(body, name="ssd_gate_norm_bwd", grid=(t // tr,), in_specs=[row, row, row, vec],
                          out_specs=[row, row, vec],
                          out_shape=[jax.ShapeDtypeStruct((t, di), F32), jax.ShapeDtypeStruct((t, di), BF16),
                                     jax.ShapeDtypeStruct((1, di), F32)],
                          compiler_params=_params("arbitrary"))(dyn, y, zx, norm_w)


def _attn_mask(n):
    w = ATTN_WINDOW
    qpos = lax.broadcasted_iota(jnp.int32, (w, 2 * w), 0) + w
    kpos = lax.broadcasted_iota(jnp.int32, (w, 2 * w), 1)
    rel = qpos - kpos
    return (rel >= 0) & (rel < w) & jnp.logical_not((n == 0) & (kpos < w))


def _attn_probs(qh, kbh, mask, sink):
    s = _dot_nt(qh, kbh) * (ATTN_HEAD_DIM ** -0.5)
    s = jnp.where(mask, s, -jnp.inf)
    m = jnp.maximum(jnp.max(s, axis=-1, keepdims=True), sink)
    e = jnp.exp(s - m)
    es = jnp.exp(sink - m)
    inv = 1.0 / (jnp.sum(e, axis=-1, keepdims=True) + es)
    return e * inv, es * inv


def _attn_fwd(qkv, sinks):
    t = qkv.shape[0]
    w, hd = ATTN_WINDOW, ATTN_HEAD_DIM
    kd = ATTN_N_KV * hd
    qd = ATTN_REP * kd
    nb = t // w

    def body(q_ref, kc_ref, vc_ref, kp_ref, vp_ref, s_ref, o_ref):
        n = pl.program_id(0)
        mask = _attn_mask(n)
        q = q_ref[...]
        kb = jnp.concatenate([kp_ref[...], kc_ref[...]], axis=0)
        vb = jnp.concatenate([vp_ref[...], vc_ref[...]], axis=0)
        sk = s_ref[...]
        for kv in range(ATTN_N_KV):
            kbh = kb[:, kv * hd:(kv + 1) * hd]
            vbh = vb[:, kv * hd:(kv + 1) * hd]
            for rep in range(ATTN_REP):
                h = kv * ATTN_REP + rep
                p, _ = _attn_probs(q[:, h * hd:(h + 1) * hd], kbh, mask, sk[:, h:h + 1])
                o_ref[:, h * hd:(h + 1) * hd] = _dot(p.astype(BF16), vbh).astype(BF16)

    prev = lambda n: jnp.maximum(n - 1, 0)
    return pl.pallas_call(
        body, name="attn_fwd", grid=(nb,),
        in_specs=[pl.BlockSpec((w, qd), lambda n: (n, 0)),
                  pl.BlockSpec((w, kd), lambda n: (n, ATTN_REP)),
                  pl.BlockSpec((w, kd), lambda n: (n, ATTN_REP + 1)),
                  pl.BlockSpec((w, kd), lambda n: (prev(n), ATTN_REP)),
                  pl.BlockSpec((w, kd), lambda n: (prev(n), ATTN_REP + 1)),
                  pl.BlockSpec((1, sinks.shape[1]), lambda n: (0, 0))],
        out_specs=pl.BlockSpec((w, qd), lambda n: (n, 0)),
        out_shape=jax.ShapeDtypeStruct((t, qd), BF16),
        compiler_params=_params("parallel"))(qkv, qkv, qkv, qkv, qkv, sinks)


def _attn_bwd(qkv, do, sinks):
    t = qkv.shape[0]
    w, hd = ATTN_WINDOW, ATTN_HEAD_DIM
    kd = ATTN_N_KV * hd
    qd = ATTN_REP * kd
    nq = ATTN_N_KV * ATTN_REP
    nb = t // w

    def body(q_ref, kc_ref, vc_ref, kp_ref, vp_ref, do_ref, s_ref,
             dq_ref, dk_ref, dv_ref, bq_ref, bk_ref, bv_ref, dsk_ref, ck_ref, cv_ref):
        n = pl.program_id(0)
        first = n == 0

        @pl.when(first)
        def _():
            ck_ref[...] = jnp.zeros_like(ck_ref)
            cv_ref[...] = jnp.zeros_like(cv_ref)
            bq_ref[...] = jnp.zeros_like(bq_ref)
            bk_ref[...] = jnp.zeros_like(bk_ref)
            bv_ref[...] = jnp.zeros_like(bv_ref)
            dsk_ref[...] = jnp.zeros_like(dsk_ref)

        @pl.when(n < nb)
        def _():
            mask = _attn_mask(n)
            q = q_ref[...]
            dov = do_ref[...]
            kb = jnp.concatenate([kp_ref[...], kc_ref[...]], axis=0)
            vb = jnp.concatenate([vp_ref[...], vc_ref[...]], axis=0)
            sk = s_ref[...]
            lane = lax.broadcasted_iota(jnp.int32, (1, nq), 1)
            dsk = jnp.zeros((1, nq), F32)
            dq_parts, dk_parts, dv_parts = [], [], []
            for kv in range(ATTN_N_KV):
                kbh = kb[:, kv * hd:(kv + 1) * hd]
                vbh = vb[:, kv * hd:(kv + 1) * hd]
                dkh = jnp.zeros((2 * w, hd), F32)
                dvh = jnp.zeros((2 * w, hd), F32)
                for rep in range(ATTN_REP):
                    h = kv * ATTN_REP + rep
                    qh = q[:, h * hd:(h + 1) * hd]
                    doh = dov[:, h * hd:(h + 1) * hd]
                    p, ps = _attn_probs(qh, kbh, mask, sk[:, h:h + 1])
                    pb = p.astype(BF16)
                    dp = _dot_nt(doh, vbh)
                    delta = jnp.sum(p * dp, axis=-1, keepdims=True)
                    dsc = (p * (dp - delta) * (hd ** -0.5)).astype(BF16)
                    dq_parts.append(_dot(dsc, kbh))
                    dkh = dkh + _dot_tn(dsc, qh)
                    dvh = dvh + _dot_tn(pb, doh)
                    dsk = jnp.where(lane == h, -jnp.sum(ps * delta, axis=0, keepdims=True), dsk)
                dk_parts.append(dkh)
                dv_parts.append(dvh)
            dq = jnp.concatenate(dq_parts, axis=1)
            dkb = jnp.concatenate(dk_parts, axis=1)
            dvb = jnp.concatenate(dv_parts, axis=1)
            dq_ref[...] = dq.astype(BF16)
            bq_ref[...] += jnp.sum(dq, axis=0, keepdims=True)
            dsk_ref[...] += dsk
            dk_prev = ck_ref[...] + dkb[:w, :]
            dv_prev = cv_ref[...] + dvb[:w, :]
            dk_ref[...] = dk_prev.astype(BF16)
            dv_ref[...] = dv_prev.astype(BF16)

            @pl.when(n > 0)
            def _():
                bk_ref[...] += jnp.sum(dk_prev, axis=0, keepdims=True)
                bv_ref[...] += jnp.sum(dv_prev, axis=0, keepdims=True)

            ck_ref[...] = dkb[w:, :]
            cv_ref[...] = dvb[w:, :]

        @pl.when(n == nb)
        def _():
            dk_ref[...] = ck_ref[...].astype(BF16)
            dv_ref[...] = cv_ref[...].astype(BF16)
            bk_ref[...] += jnp.sum(ck_ref[...], axis=0, keepdims=True)
            bv_ref[...] += jnp.sum(cv_ref[...], axis=0, keepdims=True)

    cur = lambda n: jnp.minimum(n, nb - 1)
    prev = lambda n: jnp.maximum(jnp.minimum(n, nb - 1) - 1, 0)
    late = lambda n: jnp.maximum(n - 1, 0)
    vec = lambda width: pl.BlockSpec((1, width), lambda n: (0, 0))
    return pl.pallas_call(
        body, name="attn_bwd", grid=(nb + 1,),
        in_specs=[pl.BlockSpec((w, qd), lambda n: (cur(n), 0)),
                  pl.BlockSpec((w, kd), lambda n: (cur(n), ATTN_REP)),
                  pl.BlockSpec((w, kd), lambda n: (cur(n), ATTN_REP + 1)),
                  pl.BlockSpec((w, kd), lambda n: (prev(n), ATTN_REP)),
                  pl.BlockSpec((w, kd), lambda n: (prev(n), ATTN_REP + 1)),
                  pl.BlockSpec((w, qd), lambda n: (cur(n), 0)),
                  vec(nq)],
        out_specs=[pl.BlockSpec((w, qd), lambda n: (cur(n), 0)),
                   pl.BlockSpec((w, kd), lambda n: (late(n), 0)),
                   pl.BlockSpec((w, kd), lambda n: (late(n), 0)),
                   vec(qd), vec(kd), vec(kd), vec(nq)],
        out_shape=[jax.ShapeDtypeStruct((t, qd), BF16), jax.ShapeDtypeStruct((t, kd), BF16),
                   jax.ShapeDtypeStruct((t, kd), BF16), jax.ShapeDtypeStruct((1, qd), F32),
                   jax.ShapeDtypeStruct((1, kd), F32), jax.ShapeDtypeStruct((1, kd), F32),
                   jax.ShapeDtypeStruct((1, nq), F32)],
        scratch_shapes=[pltpu.VMEM((w, kd), F32), pltpu.VMEM((w, kd), F32)],
        compiler_params=_params("arbitrary"))(qkv, qkv, qkv, qkv, qkv, do, sinks)


def _attn_mask_t(n):
    w = ATTN_WINDOW
    kpos = lax.broadcasted_iota(jnp.int32, (2 * w, ATTN_REP * w), 0)
    qpos = lax.broadcasted_iota(jnp.int32, (2 * w, ATTN_REP * w), 1) % w + w
    rel = qpos - kpos
    return (rel >= 0) & (rel < w) & jnp.logical_not((n == 0) & (kpos < w))


def _attn_probs_t(qts, ktb, mask, sink):
    s = _dot_tn(ktb, qts) * (ATTN_HEAD_DIM ** -0.5)
    s = jnp.where(mask, s, -jnp.inf)
    m = jnp.maximum(jnp.max(s, axis=0, keepdims=True), sink)
    e = jnp.exp(s - m)
    es = jnp.exp(sink - m)
    inv = 1.0 / (jnp.sum(e, axis=0, keepdims=True) + es)
    return e * inv, es * inv


def _attn_blocks_t(kv, q_ref, kc_ref, vc_ref, kp_ref, vp_ref):
    hd = ATTN_HEAD_DIM
    rows = slice(kv * hd, (kv + 1) * hd)
    ktb = jnp.concatenate([kp_ref[rows, :], kc_ref[rows, :]], axis=1)
    vtb = jnp.concatenate([vp_ref[rows, :], vc_ref[rows, :]], axis=1)
    qts = jnp.concatenate([q_ref[(kv * ATTN_REP + r) * hd:(kv * ATTN_REP + r + 1) * hd, :]
                           for r in range(ATTN_REP)], axis=1)
    return qts, ktb, vtb


def _attn_specs_t(nb, cur, prev):
    w, hd = ATTN_WINDOW, ATTN_HEAD_DIM
    kd = ATTN_N_KV * hd
    qd = ATTN_REP * kd
    return [pl.BlockSpec((qd, w), lambda n: (0, cur(n))),
            pl.BlockSpec((kd, w), lambda n: (ATTN_REP, cur(n))),
            pl.BlockSpec((kd, w), lambda n: (ATTN_REP + 1, cur(n))),
            pl.BlockSpec((kd, w), lambda n: (ATTN_REP, prev(n))),
            pl.BlockSpec((kd, w), lambda n: (ATTN_REP + 1, prev(n)))]


def _attn_fwd_t(qkv_t, sinks_rep):
    t = qkv_t.shape[1]
    w, hd = ATTN_WINDOW, ATTN_HEAD_DIM
    qd = ATTN_N_KV * ATTN_REP * hd
    nb = t // w

    def body(q_ref, kc_ref, vc_ref, kp_ref, vp_ref, s_ref, o_ref):
        mask = _attn_mask_t(pl.program_id(0))
        for kv in range(ATTN_N_KV):
            qts, ktb, vtb = _attn_blocks_t(kv, q_ref, kc_ref, vc_ref, kp_ref, vp_ref)
            p, _ = _attn_probs_t(qts, ktb, mask, s_ref[kv])
            ots = _dot(vtb, p.astype(BF16))
            for r in range(ATTN_REP):
                h = kv * ATTN_REP + r
                o_ref[h * hd:(h + 1) * hd, :] = ots[:, r * w:(r + 1) * w].astype(BF16)

    return pl.pallas_call(
        body, name="attn_fwd", grid=(nb,),
        in_specs=_attn_specs_t(nb, lambda n: n, lambda n: jnp.maximum(n - 1, 0)) + [
            pl.BlockSpec(sinks_rep.shape, lambda n: (0, 0, 0))],
        out_specs=pl.BlockSpec((qd, w), lambda n: (0, n)),
        out_shape=jax.ShapeDtypeStruct((qd, t), BF16),
        compiler_params=_params("parallel"))(qkv_t, qkv_t, qkv_t, qkv_t, qkv_t, sinks_rep)


def _attn_bwd_t(qkv_t, do_t, sinks_rep):
    t = qkv_t.shape[1]
    w, hd = ATTN_WINDOW, ATTN_HEAD_DIM
    kd = ATTN_N_KV * hd
    qd = ATTN_REP * kd
    nq = ATTN_N_KV * ATTN_REP
    nb = t // w
    rows_all = qd + 2 * kd

    def body(q_ref, kc_ref, vc_ref, kp_ref, vp_ref, do_ref, s_ref, dqkv_ref, bsum_ref, dsk_ref,
             carry_ref, new_ref, bacc_ref, sacc_ref):
        n = pl.program_id(0)

        @pl.when(n == 0)
        def _():
            carry_ref[...] = jnp.zeros_like(carry_ref)
            bacc_ref[...] = jnp.zeros_like(bacc_ref)
            sacc_ref[...] = jnp.zeros_like(sacc_ref)

        @pl.when(n < nb)
        def _():
            mask = _attn_mask_t(n)
            for kv in range(ATTN_N_KV):
                qts, ktb, vtb = _attn_blocks_t(kv, q_ref, kc_ref, vc_ref, kp_ref, vp_ref)
                dots = jnp.concatenate([do_ref[(kv * ATTN_REP + r) * hd:(kv * ATTN_REP + r + 1) * hd, :]
                                        for r in range(ATTN_REP)], axis=1)
                p, ps = _attn_probs_t(qts, ktb, mask, s_ref[kv])
                dpt = _dot_tn(vtb, dots)
                delta = jnp.sum(p * dpt, axis=0, keepdims=True)
                dst = (p * (dpt - delta) * (hd ** -0.5)).astype(BF16)
                dqts = _dot(ktb, dst)
                for r in range(ATTN_REP):
                    h = kv * ATTN_REP + r
                    new_ref[h * hd:(h + 1) * hd, :] = dqts[:, r * w:(r + 1) * w]
                dktb = _dot_nt(qts, dst)
                dvtb = _dot_nt(dots, p.astype(BF16))
                krows = slice(qd + kv * hd, qd + (kv + 1) * hd)
                vrows = slice(qd + kd + kv * hd, qd + kd + (kv + 1) * hd)
                carry_ref[krows, :] += dktb[:, :w]
                carry_ref[vrows, :] += dvtb[:, :w]
                new_ref[krows, :] = dktb[:, w:]
                new_ref[vrows, :] = dvtb[:, w:]
                sacc_ref[kv] += -(ps * delta)

        @pl.when(n >= 1)
        def _():
            done = carry_ref[...]
            dqkv_ref[...] = done.astype(BF16)
            bacc_ref[...] += done

        @pl.when(n < nb)
        def _():
            carry_ref[...] = new_ref[...]

        @pl.when(n == nb)
        def _():
            bsum_ref[...] = jnp.sum(bacc_ref[...], axis=1, keepdims=True)
            lane = lax.broadcasted_iota(jnp.int32, (1, nq), 1)
            dsk = jnp.zeros((1, nq), F32)
            for kv in range(ATTN_N_KV):
                acc = sacc_ref[kv]
                for r in range(ATTN_REP):
                    tot = jnp.sum(acc[:, r * w:(r + 1) * w], axis=1, keepdims=True)
                    dsk = jnp.where(lane == kv * ATTN_REP + r, tot, dsk)
            dsk_ref[...] = dsk

    cur = lambda n: jnp.minimum(n, nb - 1)
    prev = lambda n: jnp.maximum(jnp.minimum(n, nb - 1) - 1, 0)
    return pl.pallas_call(
        body, name="attn_bwd", grid=(nb + 1,),
        in_specs=_attn_specs_t(nb, cur, prev) + [pl.BlockSpec((qd, w), lambda n: (0, cur(n))),
                                                 pl.BlockSpec(sinks_rep.shape, lambda n: (0, 0, 0))],
        out_specs=[pl.BlockSpec((rows_all, w), lambda n: (0, jnp.maximum(n - 1, 0))),
                   pl.BlockSpec((rows_all, 1), lambda n: (0, 0)),
                   pl.BlockSpec((1, nq), lambda n: (0, 0))],
        out_shape=[jax.ShapeDtypeStruct((rows_all, t), BF16), jax.ShapeDtypeStruct((rows_all, 1), F32),
                   jax.ShapeDtypeStruct((1, nq), F32)],
        scratch_shapes=[pltpu.VMEM((rows_all, w), F32), pltpu.VMEM((rows_all, w), F32),
                        pltpu.VMEM((rows_all, w), F32), pltpu.VMEM(sinks_rep.shape, F32)],
        compiler_params=_params("arbitrary"))(qkv_t, qkv_t, qkv_t, qkv_t, qkv_t, do_t, sinks_rep)


HBM_SPEC = pl.BlockSpec(memory_space=pl.ANY)
HBM_ONLY = pl.BlockSpec(memory_space=pltpu.HBM)


def _comm_call(name, body, ins, out_shapes, n_sems):
    return pl.pallas_call(
        body, name=name, in_specs=[HBM_SPEC] * len(ins), out_specs=[HBM_SPEC] * len(out_shapes),
        out_shape=out_shapes,
        scratch_shapes=[pltpu.SemaphoreType.DMA((s,)) for s in n_sems])(*ins)


def _all_gather(name, shards):
    n = len(shards)

    def body(*refs):
        x_refs, out_refs = refs[:n], refs[n:2 * n]
        send_sems, recv_sems, local_sems = refs[2 * n:]
        x, y, c = lax.axis_index("x"), lax.axis_index("y"), lax.axis_index("c")
        me, sibling = (x, y, c), (x, y, 1 - c)
        chips = [(1 - x, y), (x, 1 - y), (1 - x, 1 - y)]

        def slot(i, px, py, pc):
            return out_refs[i].at[4 * px + 2 * py + pc]

        def copy(k, i, block, to, src=None):
            return pltpu.make_async_remote_copy(
                src_ref=slot(i, *block) if src is None else src, dst_ref=slot(i, *block),
                send_sem=send_sems.at[k * n + i], recv_sem=recv_sems.at[k * n + i], device_id=to,
                device_id_type=MESH)

        mine = [pltpu.make_async_copy(x_refs[i], slot(i, *me), local_sems.at[i]) for i in range(n)]
        first = []
        for i in range(n):
            mine[i].start()
            first.append(copy(0, i, me, sibling, src=x_refs[i]))
            first += [copy(1 + j, i, me, (*chip, c), src=x_refs[i]) for j, chip in enumerate(chips)]
        for cp in first:
            cp.start()
        passed = []
        for i in range(n):
            for j, chip in enumerate(chips):
                copy(1 + j, i, (*chip, c), me).wait_recv()
                passed.append(copy(4 + j, i, (*chip, c), sibling))
                passed[-1].start()
        for i in range(n):
            copy(0, i, sibling, me).wait_recv()
            for j, chip in enumerate(chips):
                copy(4 + j, i, (*chip, 1 - c), me).wait_recv()
        for cp in first + passed:
            cp.wait_send()
        for cp in mine:
            cp.wait()

    outs = [jax.ShapeDtypeStruct((N_DEV,) + s.shape, s.dtype) for s in shards]
    return _comm_call(name, body, shards, outs, (7 * n, 7 * n, n))


SEM_SPEC = pl.BlockSpec(memory_space=pltpu.SEMAPHORE)
SPLIT_COPY_EFFECT = pltpu.SideEffectType.DATAFLOW_SIDE_EFFECTING


def _in_hbm(a):
    return pltpu.with_memory_space_constraint(a, pltpu.HBM)


def _split_start(name, body, srcs, lands, n_sems):
    n = len(srcs)
    bufs = [_in_hbm(a) for a in list(srcs) + list(lands)]
    outs = pl.pallas_call(
        body, name=name,
        out_shape=(pltpu.SemaphoreType.DMA((n_sems,)), pltpu.SemaphoreType.DMA((n_sems,)),
                   *[pltpu.HBM(a.shape, a.dtype) for a in bufs], jax.ShapeDtypeStruct((8, LANES), F32)),
        in_specs=[HBM_ONLY] * (2 * n),
        out_specs=(SEM_SPEC, SEM_SPEC, *[HBM_ONLY] * (2 * n), pl.BlockSpec(memory_space=pltpu.VMEM)),
        input_output_aliases={i: 2 + i for i in range(2 * n)},
        compiler_params=pltpu.CompilerParams(has_side_effects=SPLIT_COPY_EFFECT))(*bufs)
    return outs[0], outs[1], list(outs[2:2 + n]), list(outs[2 + n:2 + 2 * n]), outs[-1]


def _split_wait(name, body, send_sems, recv_sems, srcs, lands, after):
    n = len(srcs)
    outs = pl.pallas_call(
        body, name=name,
        out_shape=[pltpu.HBM(a.shape, a.dtype) for a in list(srcs) + list(lands)],
        in_specs=[HBM_ONLY] * (2 * n) + [SEM_SPEC, SEM_SPEC, HBM_SPEC],
        out_specs=[HBM_ONLY] * (2 * n),
        input_output_aliases={i: i for i in range(2 * n)},
        compiler_params=pltpu.CompilerParams(has_side_effects=SPLIT_COPY_EFFECT))(
            *srcs, *lands, send_sems, recv_sems, after)
    return list(outs[:n]), list(outs[n:])


N_PEERS = N_DEV - 1


def _gather_peers():
    x, y, c = lax.axis_index("x"), lax.axis_index("y"), lax.axis_index("c")
    flips = [(fx, fy, fc) for fx in (0, 1) for fy in (0, 1) for fc in (0, 1) if fx or fy or fc]
    return [(1 - x if fx else x, 1 - y if fy else y, 1 - c if fc else c) for fx, fy, fc in flips]


def _block_id(dev):
    return 4 * dev[0] + 2 * dev[1] + dev[2]


def _gather_start(name, shards):
    n = len(shards)

    def body(*refs):
        x_refs, land_refs = refs[:n], refs[n:2 * n]
        send_sems, recv_sems, token = refs[2 * n], refs[2 * n + 1], refs[-1]
        me = (lax.axis_index("x"), lax.axis_index("y"), lax.axis_index("c"))
        for i in range(n):
            for k, peer in enumerate(_gather_peers()):
                pltpu.make_async_remote_copy(
                    src_ref=x_refs[i], dst_ref=land_refs[i].at[_block_id(me)],
                    send_sem=send_sems.at[N_PEERS * i + k], recv_sem=recv_sems.at[N_PEERS * i + k],
                    device_id=peer, device_id_type=MESH).start()
        token[...] = jnp.zeros_like(token)

    lands = [lax.empty((N_DEV,) + s.shape, s.dtype) for s in shards]
    return _split_start(name, body, shards, lands, N_PEERS * n)


def _gather_wait(name, send_sems, recv_sems, first, shards, lands, after):
    n = len(shards)

    def body(*refs):
        x_refs, land_refs = refs[:n], refs[n:2 * n]
        send_sems, recv_sems = refs[2 * n], refs[2 * n + 1]
        for i in range(n):
            for k, peer in enumerate(_gather_peers()):
                cp = pltpu.make_async_remote_copy(
                    src_ref=x_refs[i], dst_ref=land_refs[i].at[_block_id(peer)],
                    send_sem=send_sems.at[N_PEERS * (first + i) + k],
                    recv_sem=recv_sems.at[N_PEERS * (first + i) + k],
                    device_id=peer, device_id_type=MESH)
                cp.wait_send()
                cp.wait_recv()

    return _split_wait(name, body, send_sems, recv_sems, shards, lands, after)


def _gather_forward(name, lands, shards):
    n = len(shards)

    def body(*refs):
        x_refs, out_refs = refs[n:2 * n], refs[2 * n:3 * n]
        send_sems, recv_sems, local_sems = refs[3 * n:]
        x, y, c = lax.axis_index("x"), lax.axis_index("y"), lax.axis_index("c")
        chips = [(1 - x, y), (x, 1 - y), (1 - x, 1 - y)]
        mine = [pltpu.make_async_copy(x_refs[i], out_refs[i].at[_block_id((x, y, c))], local_sems.at[i])
                for i in range(n)]
        passed = [pltpu.make_async_remote_copy(
            src_ref=out_refs[i].at[_block_id((*chip, c))], dst_ref=out_refs[i].at[_block_id((*chip, c))],
            send_sem=send_sems.at[3 * i + j], recv_sem=recv_sems.at[3 * i + j], device_id=(x, y, 1 - c),
            device_id_type=MESH) for i in range(n) for j, chip in enumerate(chips)]
        for cp in mine + passed:
            cp.start()
        for i in range(n):
            for j, chip in enumerate(chips):
                pltpu.make_async_remote_copy(
                    src_ref=out_refs[i].at[_block_id((*chip, c))], dst_ref=out_refs[i].at[_block_id((*chip, 1 - c))],
                    send_sem=send_sems.at[3 * i + j], recv_sem=recv_sems.at[3 * i + j], device_id=(x, y, 1 - c),
                    device_id_type=MESH).wait()
        for cp in mine:
            cp.wait()

    return pl.pallas_call(
        body, name=name, in_specs=[HBM_SPEC] * (2 * n), out_specs=[HBM_SPEC] * n,
        out_shape=[jax.ShapeDtypeStruct(a.shape, a.dtype) for a in lands],
        input_output_aliases={i: i for i in range(n)},
        scratch_shapes=[pltpu.SemaphoreType.DMA((3 * n,)), pltpu.SemaphoreType.DMA((3 * n,)),
                        pltpu.SemaphoreType.DMA((n,))])(*lands, *shards)


def _chip_peers():
    x, y, c = lax.axis_index("x"), lax.axis_index("y"), lax.axis_index("c")
    return [(1 - x, y, c), (x, 1 - y, c), (1 - x, 1 - y, c)]


def _chip_start(name, blocks):
    n = len(blocks)

    def body(*refs):
        p_refs, land_refs = refs[:n], refs[n:2 * n]
        send_sems, recv_sems, token = refs[2 * n], refs[2 * n + 1], refs[-1]
        for i in range(n):
            for j, peer in enumerate(_chip_peers()):
                pltpu.make_async_remote_copy(
                    src_ref=p_refs[i].at[j], dst_ref=land_refs[i].at[j], send_sem=send_sems.at[3 * i + j],
                    recv_sem=recv_sems.at[3 * i + j], device_id=peer, device_id_type=MESH).start()
        token[...] = jnp.zeros_like(token)

    lands = [lax.empty(b.shape, b.dtype) for b in blocks]
    return _split_start(name, body, blocks, lands, 3 * n)


def _chip_wait(name, send_sems, recv_sems, blocks, lands, after):
    n = len(blocks)

    def body(*refs):
        p_refs, land_refs = refs[:n], refs[n:2 * n]
        send_sems, recv_sems = refs[2 * n], refs[2 * n + 1]
        for i in range(n):
            for j, peer in enumerate(_chip_peers()):
                cp = pltpu.make_async_remote_copy(
                    src_ref=p_refs[i].at[j], dst_ref=land_refs[i].at[j], send_sem=send_sems.at[3 * i + j],
                    recv_sem=recv_sems.at[3 * i + j], device_id=peer, device_id_type=MESH)
                cp.wait_send()
                cp.wait_recv()

    return _split_wait(name, body, send_sems, recv_sems, blocks, lands, after)


def _pair_exchange(name, blocks):
    n = len(blocks)

    def body(*refs):
        g_refs, out_refs = refs[:n], refs[n:2 * n]
        send_sems, recv_sems = refs[2 * n:]
        x, y, c = lax.axis_index("x"), lax.axis_index("y"), lax.axis_index("c")
        copies = [pltpu.make_async_remote_copy(
            src_ref=g_refs[i].at[2 * k + 1 - c], dst_ref=out_refs[i].at[k], send_sem=send_sems.at[4 * i + k],
            recv_sem=recv_sems.at[4 * i + k], device_id=(x, y, 1 - c), device_id_type=MESH)
            for i in range(n) for k in range(4)]
        for cp in copies:
            cp.start()
        for cp in copies:
            cp.wait()

    outs = [jax.ShapeDtypeStruct((4,) + b.shape[1:], b.dtype) for b in blocks]
    return _comm_call(name, body, blocks, outs, (4 * n, 4 * n))


def _chip_exchange(name, blocks):
    n = len(blocks)

    def body(*refs):
        p_refs, out_refs = refs[:n], refs[n:2 * n]
        send_sems, recv_sems = refs[2 * n:]
        x, y, c = lax.axis_index("x"), lax.axis_index("y"), lax.axis_index("c")
        chips = [(1 - x, y), (x, 1 - y), (1 - x, 1 - y)]
        copies = [pltpu.make_async_remote_copy(
            src_ref=p_refs[i].at[j], dst_ref=out_refs[i].at[j], send_sem=send_sems.at[3 * i + j],
            recv_sem=recv_sems.at[3 * i + j], device_id=(*chip, c), device_id_type=MESH)
            for i in range(n) for j, chip in enumerate(chips)]
        for cp in copies:
            cp.start()
        for cp in copies:
            cp.wait()

    outs = [jax.ShapeDtypeStruct(b.shape, b.dtype) for b in blocks]
    return _comm_call(name, body, blocks, outs, (3 * n, 3 * n))


def _pair_sum(name, blocks, from_sibling, g_idx, r_idx):
    _, r, c_ = blocks.shape
    tr = _tile(r, 512, 16)

    def body(gi_ref, ri_ref, a_ref, b_ref, own_ref, send_ref):
        k = pl.program_id(1)
        s = a_ref[...] + b_ref[...]

        @pl.when(k == 0)
        def _():
            own_ref[...] = s

        @pl.when(k > 0)
        def _():
            send_ref[...] = s.astype(send_ref.dtype)

    return pl.pallas_call(
        body, name=name,
        grid_spec=pltpu.PrefetchScalarGridSpec(
            num_scalar_prefetch=2, grid=(r // tr, 4),
            in_specs=[pl.BlockSpec((None, tr, c_), lambda i, k, gi, ri: (gi[k], i, 0)),
                      pl.BlockSpec((None, tr, c_), lambda i, k, gi, ri: (ri[k], i, 0))],
            out_specs=[pl.BlockSpec((None, tr, c_), lambda i, k, gi, ri: (0, i, 0)),
                       pl.BlockSpec((None, tr, c_), lambda i, k, gi, ri: (jnp.maximum(k - 1, 0), i, 0))]),
        out_shape=[jax.ShapeDtypeStruct((1, r, c_), F32), jax.ShapeDtypeStruct((3, r, c_), PAYLOAD)],
        compiler_params=_params("parallel", "arbitrary"))(g_idx, r_idx, blocks, from_sibling)


def _adamw(w, g, m, v):
    m = ADAM_B1 * m + (1.0 - ADAM_B1) * g
    v = ADAM_B2 * v + (1.0 - ADAM_B2) * (g * g)
    m_hat = m / (1.0 - ADAM_B1 ** ADAM_STEP)
    v_hat = v / (1.0 - ADAM_B2 ** ADAM_STEP)
    delta = -ADAM_LR * (m_hat / (jnp.sqrt(v_hat) + ADAM_EPS) + ADAM_WD * w)
    return delta, m, v


def _sum_adamw(name, parts_f32, parts_lo, w, m, v):
    r, c_ = w.shape
    tr = _tile(r, 256, 16)
    k1 = parts_f32.shape[0]
    k2 = 0 if parts_lo is None else parts_lo.shape[0]

    def body(*refs):
        a_ref = refs[0]
        b_ref = refs[1] if k2 else None
        w_ref, m_ref, v_ref, g_ref, d_ref, nm_ref, nv_ref = refs[(2 if k2 else 1):]
        g = a_ref[0]
        for k in range(1, k1):
            g = g + a_ref[k]
        for k in range(k2):
            g = g + b_ref[k].astype(F32)
        g_ref[...] = g
        d_ref[...], nm_ref[...], nv_ref[...] = _adamw(w_ref[...], g, m_ref[...], v_ref[...])

    row = pl.BlockSpec((tr, c_), lambda i: (i, 0))
    ins = [parts_f32] + ([parts_lo] if k2 else []) + [w, m, v]
    in_specs = [pl.BlockSpec((k1, tr, c_), lambda i: (0, i, 0))]
    if k2:
        in_specs.append(pl.BlockSpec((k2, tr, c_), lambda i: (0, i, 0)))
    in_specs += [row, row, row]
    return pl.pallas_call(body, name=name, grid=(r // tr,), in_specs=in_specs, out_specs=[row] * 4,
                          out_shape=[jax.ShapeDtypeStruct((r, c_), F32)] * 4,
                          compiler_params=_params("parallel"))(*ins)


def _pack_rows(flat, n_rows, cols):
    pad = n_rows * cols - flat.shape[-1]
    flat = jnp.pad(flat, [(0, 0)] * (flat.ndim - 1) + [(0, pad)])
    return flat.reshape(flat.shape[:-1] + (n_rows, cols))


def _cols_join(blocks):
    return jnp.concatenate([blocks[d] for d in range(N_DEV)], axis=1)


def _cols_split(full):
    c = full.shape[1] // N_DEV
    return jnp.stack([full[:, d * c:(d + 1) * c] for d in range(N_DEV)])


def _rows_join(blocks):
    return blocks.reshape(N_DEV * blocks.shape[1], blocks.shape[2])


def _rows_split(full):
    return full.reshape(N_DEV, full.shape[0] // N_DEV, full.shape[1])


def _perm_xbc(a, ng):
    lead = a.shape[:-1]
    di, gn = ng * GW, ng * SSD_D_STATE
    xs = a[..., :di].reshape(lead + (ng, GW))
    bs = a[..., di:di + gn].reshape(lead + (ng, SSD_D_STATE))
    cs = a[..., di + gn:].reshape(lead + (ng, SSD_D_STATE))
    return jnp.concatenate([xs, bs, cs], axis=-1).reshape(lead + (ng * GC,))


def _unperm_xbc(a, ng):
    lead = a.shape[:-1]
    g = a.reshape(lead + (ng, GC))
    return jnp.concatenate([g[..., :GW].reshape(lead + (ng * GW,)),
                            g[..., GW:GW + SSD_D_STATE].reshape(lead + (ng * SSD_D_STATE,)),
                            g[..., GW + SSD_D_STATE:].reshape(lead + (ng * SSD_D_STATE,))], axis=-1)


def _heads_col(v, ng):
    return jnp.pad(v.reshape(ng, 1, SSD_HPG), ((0, 0), (0, 0), (0, LANES - SSD_HPG)))


def _heads_row(v, ng):
    return jnp.pad(v.reshape(ng, SSD_HPG, 1), ((0, 0), (0, 8 - SSD_HPG), (0, 0)))


MATRIX_ITEMS = ("w_in", "w_out", "up0", "down0", "w_qkv", "w_o", "up1", "down1")
VECTOR_ITEMS = ("conv_w", "b_qkv", "b_o")
ITEMS = MATRIX_ITEMS + VECTOR_ITEMS
GATHER_STAGES = (("w_in", "conv_w"), ("w_out", "up0", "down0"), ("w_qkv", "b_qkv", "w_o", "b_o", "up1", "down1"))


def _items(tree, prefix=""):
    g = lambda k: tree[prefix + k]
    return {"w_in": g("ssd_w_in")[0], "w_out": g("ssd_w_out")[0], "w_qkv": g("attn_w_qkv")[0].T,
            "w_o": g("attn_w_o")[0], "up0": g("mlp_w_up")[0], "up1": g("mlp_w_up")[1],
            "down0": g("mlp_w_down")[0], "down1": g("mlp_w_down")[1], "conv_w": g("ssd_conv_w")[0],
            "b_qkv": g("attn_b_qkv"), "b_o": g("attn_b_o")}


def _from_items(it):
    return {"ssd_w_in": it["w_in"][None], "ssd_w_out": it["w_out"][None], "attn_w_qkv": it["w_qkv"].T[None],
            "attn_w_o": it["w_o"][None], "mlp_w_up": jnp.stack([it["up0"], it["up1"]]),
            "mlp_w_down": jnp.stack([it["down0"], it["down1"]]), "ssd_conv_w": it["conv_w"][None],
            "attn_b_qkv": it["b_qkv"], "attn_b_o": it["b_o"]}


REPLICATED = ("ssd_conv_b", "ssd_dt_bias", "ssd_a_log", "ssd_d", "ssd_norm_w", "attn_sinks", "mix_pre_norm",
              "mix_post_norm", "ffn_pre_norm", "ffn_post_norm")
WEIGHTS = ("ssd_w_in", "ssd_conv_w", "ssd_conv_b", "ssd_dt_bias", "ssd_a_log", "ssd_d", "ssd_norm_w", "ssd_w_out",
           "attn_w_qkv", "attn_b_qkv", "attn_sinks", "attn_w_o", "attn_b_o", "mlp_w_up", "mlp_w_down",
           "mix_pre_norm", "mix_post_norm", "ffn_pre_norm", "ffn_post_norm")


def _forward_backward(x, target, rep, token, weights_of_stage, reduce_grads):
    t, d = x.shape
    ng = rep["ssd_norm_w"].shape[1] // GW
    di = ng * GW
    n_xbc = ng * GC
    nh = ng * SSD_HPG
    grads, blocks = {}, {}
    w_up, w_down = [None, None], [None, None]
    sinks_rep = jnp.repeat(rep["attn_sinks"].reshape(ATTN_N_KV, ATTN_REP, 1), ATTN_WINDOW, axis=2).reshape(
        ATTN_N_KV, 1, ATTN_REP * ATTN_WINDOW)
    conv_b = _perm_xbc(rep["ssd_conv_b"], ng)
    bias_c, alog_c, dsk_c = (_heads_col(rep[k], ng) for k in ("ssd_dt_bias", "ssd_a_log", "ssd_d"))
    bias_r, alog_r = (_heads_row(rep[k], ng) for k in ("ssd_dt_bias", "ssd_a_log"))
    norm = {k: rep[k] for k in ("mix_pre_norm", "mix_post_norm", "ffn_pre_norm", "ffn_post_norm")}

    def nrow(name, i):
        return norm[name][i:i + 1]

    def mlp_fwd(i, u2):
        a, p = _mm(f"mlp{i}_up", [u2], [w_up[i]], "nn", tm=1024, tn=1024, out_dtypes=(F32, BF16),
                   epilogue=lambda acc: (acc, jnp.square(jnp.maximum(acc, 0.0))))
        f = _mm(f"mlp{i}_down", [p], [w_down[i]], "nn", tm=512, tn=1024)
        return a, p, f

    def mlp_bwd(i, df, u2, a, p):
        da = _mm(f"mlp{i}_dact", [df], [w_down[i]], "nt", tm=1024, tn=1024, out_dtypes=(BF16,),
                 tiles=(a,), epilogue=lambda acc, av: (acc * (2.0 * jnp.maximum(av, 0.0)),))
        blocks[f"down{i}"] = _rows_split(_mm(f"mlp{i}_dwdown", [p], [df], "tn", tm=512, tn=1024))
        blocks[f"up{i}"] = _mm(f"mlp{i}_dwup", [u2], [da], "tn", tm=1024, tn=da.shape[1] // N_DEV,
                               col_blocks=True)
        return _mm(f"mlp{i}_dx", [da], [w_up[i]], "nt", tm=512, tn=1024)

    u0 = _prenorm("l0_prenorm", x, nrow("mix_pre_norm", 0), token)
    got = weights_of_stage(0, u0)
    w_in = _cols_join(got["w_in"])
    w_z = w_in[:, :di]
    w_xbc = _perm_xbc(w_in[:, di:di + n_xbc], ng)
    w_dt = jnp.pad(w_in[:, di + n_xbc:], ((0, 0), (0, LANES - nh)))
    w_in_k = jnp.concatenate([w_z, w_xbc, w_dt], axis=1)
    n_in = w_in_k.shape[1]
    conv_w = _perm_xbc(_cols_join(got["conv_w"]), ng)
    zx = _mm("ssd_in_proj", [u0], [w_in_k], "nn", tm=1024, tn=_tile(n_in, 1024))
    pre = _conv_fwd(zx, di, n_xbc, conv_w, conv_b)
    dt_raw = zx[:, di + n_xbc:di + n_xbc + nh].reshape(t, ng, SSD_HPG)
    dtc = jnp.pad(jnp.transpose(dt_raw, (1, 0, 2)), ((0, 0), (0, 0), (0, LANES - SSD_HPG)))
    dtr = jnp.pad(jnp.transpose(dt_raw, (1, 2, 0)), ((0, 0), (0, 8 - SSD_HPG), (0, 0)))
    ssd_args = (dtc, dtr, bias_c, alog_c, dsk_c, bias_r, alog_r)
    y, states = _ssd_fwd(pre, *ssd_args)
    yn = _gate_norm_fwd(y, zx, rep["ssd_norm_w"])
    got = weights_of_stage(1, yn)
    w_out = _rows_join(got["w_out"])
    w_up[0], w_down[0] = _cols_join(got["up0"]), _rows_join(got["down0"])
    mix0 = _mm("ssd_out_proj", [yn], [w_out], "nn", tm=1024, tn=1024)
    h1, u0f = _post_pre("l0_mid", x, mix0, nrow("mix_post_norm", 0), nrow("ffn_pre_norm", 0))
    a0, p0, f0 = mlp_fwd(0, u0f)
    h2, u1 = _post_pre("l1_in", h1, f0, nrow("ffn_post_norm", 0), nrow("mix_pre_norm", 1))
    got = weights_of_stage(2, u1)
    w_qkv_t = _rows_join(got["w_qkv"])
    w_o = _rows_join(got["w_o"])
    b_qkv_col = got["b_qkv"].reshape(-1, 1)
    b_o = _cols_join(got["b_o"])
    w_up[1], w_down[1] = _cols_join(got["up1"]), _rows_join(got["down1"])
    qkv_t = _mm("attn_qkv_proj", [w_qkv_t], [u1], "nt", tm=768, tn=1024, out_dtypes=(BF16,), cols=(b_qkv_col,),
                epilogue=lambda acc, b: (acc + b,))
    ao_t = _attn_fwd_t(qkv_t, sinks_rep)
    mix1 = _mm("attn_out_proj", [ao_t], [w_o], "tn", tm=1024, tn=1024, rows=(b_o,),
               epilogue=lambda acc, b: (acc + b,))
    h3, u1f = _post_pre("l1_mid", h2, mix1, nrow("mix_post_norm", 1), nrow("ffn_pre_norm", 1))
    a1, p1, f1 = mlp_fwd(1, u1f)
    dh, loss_row = _final_loss("loss", h3, f1, nrow("ffn_post_norm", 1), target)

    g_norm = {k: [None, None] for k in norm}
    df1, g_norm["ffn_post_norm"][1], _ = _norm_bwd("l1_ffn_post_bwd", dh, post=(f1, nrow("ffn_post_norm", 1)))
    du = mlp_bwd(1, df1, u1f, a1, p1)
    sent = reduce_grads("mlp1", {k: blocks[k] for k in ("up1", "down1")})
    dh, g_norm["ffn_pre_norm"][1], dmix1, g_norm["mix_post_norm"][1], db_o = _norm_bwd(
        "l1_mid_bwd", dh, pre=(du, h3, nrow("ffn_pre_norm", 1)), post=(mix1, nrow("mix_post_norm", 1)), after=sent)
    blocks["b_o"] = _cols_split(db_o)
    blocks["w_o"] = _rows_split(_mm("attn_dwo", [ao_t], [dmix1], "nn", tm=512, tn=1024))
    dao_t = _mm("attn_dout", [w_o], [dmix1], "nt", tm=1024, tn=1024, out_dtypes=(BF16,))
    dqkv_t, db_qkv, grads["attn_sinks"] = _attn_bwd_t(qkv_t, dao_t, sinks_rep)
    blocks["b_qkv"] = db_qkv.reshape(N_DEV, 1, -1)
    blocks["w_qkv"] = _rows_split(_mm("attn_dwqkv", [dqkv_t], [u1], "nn", tm=512, tn=1024))
    du = _mm("attn_dx", [dqkv_t], [w_qkv_t], "tn", tm=1024, tn=1024)
    sent = reduce_grads("attn", {k: blocks[k] for k in ("w_o", "w_qkv", "b_o", "b_qkv")})
    dh, g_norm["mix_pre_norm"][1], df0, g_norm["ffn_post_norm"][0], _ = _norm_bwd(
        "l1_in_bwd", dh, pre=(du, h2, nrow("mix_pre_norm", 1)), post=(f0, nrow("ffn_post_norm", 0)), after=sent)
    du = mlp_bwd(0, df0, u0f, a0, p0)
    sent = reduce_grads("mlp0", {k: blocks[k] for k in ("up0", "down0")})
    dh, g_norm["ffn_pre_norm"][0], dmix0, g_norm["mix_post_norm"][0], _ = _norm_bwd(
        "l0_mid_bwd", dh, pre=(du, h1, nrow("ffn_pre_norm", 0)), post=(mix0, nrow("mix_post_norm", 0)), after=sent)
    blocks["w_out"] = _rows_split(_mm("ssd_dwout", [yn], [dmix0], "tn", tm=512, tn=1024))
    dyn = _mm("ssd_dyn", [dmix0], [w_out], "nt", tm=1024, tn=1024)
    dy, dz, grads["ssd_norm_w"] = _gate_norm_bwd(dyn, y, zx, rep["ssd_norm_w"])
    dpre, ddt_g, dbias_g, dalog_g, dd_g = _ssd_bwd(dy, pre, states, *ssd_args)
    dxbc, dconv_w, dconv_b = _conv_bwd(dpre, zx, di, conv_w)
    ddt = jnp.transpose(ddt_g[:, :, :SSD_HPG], (1, 0, 2)).reshape(t, nh)
    ddt = jnp.pad(ddt, ((0, 0), (0, LANES - nh))).astype(BF16)
    blocks["conv_w"] = _cols_split(_unperm_xbc(dconv_w, ng))
    grads["ssd_conv_b"] = _unperm_xbc(dconv_b, ng)
    for name, val in (("ssd_dt_bias", dbias_g), ("ssd_a_log", dalog_g), ("ssd_d", dd_g)):
        grads[name] = val[:, 0, :SSD_HPG].reshape(1, nh)
    dw_z = _mm("ssd_dwz", [u0], [dz], "tn", tm=1024, tn=512)
    dw_xbc = _mm("ssd_dwxbc", [u0], [dxbc], "tn", tm=1024, tn=512)
    dw_dt = _mm("ssd_dwdt", [u0], [ddt], "tn", tm=1024, tn=LANES)
    blocks["w_in"] = _cols_split(jnp.concatenate([dw_z, _unperm_xbc(dw_xbc, ng), dw_dt[:, :nh]], axis=1))
    sent = reduce_grads("ssd", {k: blocks[k] for k in ("w_in", "w_out", "conv_w")})
    du = _mm("ssd_dx", [dz, dxbc, ddt], [w_z, w_xbc, w_dt], "nt", tm=256, tn=1024)
    grad_x, g_norm["mix_pre_norm"][0] = _norm_bwd("l0_in_bwd", dh, pre=(du, x, nrow("mix_pre_norm", 0)), after=sent)
    for k in norm:
        grads[k] = jnp.concatenate(g_norm[k], axis=0)
    return loss_row, grad_x, grads


def kernel(x, ssd_w_in, ssd_conv_w, ssd_conv_b, ssd_dt_bias, ssd_a_log, ssd_d, ssd_norm_w, ssd_w_out, attn_w_qkv, attn_b_qkv, attn_sinks, attn_w_o, attn_b_o, mlp_w_up, mlp_w_down, mix_pre_norm, mix_post_norm, ffn_pre_norm, ffn_post_norm, loss_target, m_ssd_w_in, m_ssd_conv_w, m_ssd_conv_b, m_ssd_dt_bias, m_ssd_a_log, m_ssd_d, m_ssd_norm_w, m_ssd_w_out, m_attn_w_qkv, m_attn_b_qkv, m_attn_sinks, m_attn_w_o, m_attn_b_o, m_mlp_w_up, m_mlp_w_down, m_mix_pre_norm, m_mix_post_norm, m_ffn_pre_norm, m_ffn_post_norm, v_ssd_w_in, v_ssd_conv_w, v_ssd_conv_b, v_ssd_dt_bias, v_ssd_a_log, v_ssd_d, v_ssd_norm_w, v_ssd_w_out, v_attn_w_qkv, v_attn_b_qkv, v_attn_sinks, v_attn_w_o, v_attn_b_o, v_mlp_w_up, v_mlp_w_down, v_mix_pre_norm, v_mix_post_norm, v_ffn_pre_norm, v_ffn_post_norm):
    given = dict(locals())
    w = {k: given[k] for k in WEIGHTS}
    mom_m = {k: given["m_" + k] for k in WEIGHTS}
    mom_v = {k: given["v_" + k] for k in WEIGHTS}
    w_it, m_it, v_it = _items(given), _items(given, "m_"), _items(given, "v_")

    order = [k for stage in GATHER_STAGES for k in stage]
    shards = [w_it[k].astype(PAYLOAD) if k in MATRIX_ITEMS else w_it[k] for k in order]
    g_send, g_recv, shards, lands, token = _gather_start("gather_start", shards)

    def weights_of_stage(s, after):
        first = sum(len(stage) for stage in GATHER_STAGES[:s])
        sl = slice(first, first + len(GATHER_STAGES[s]))
        srcs, got = _gather_wait(f"gather_wait{s}", g_send, g_recv, first, shards[sl], lands[sl], after)
        me = 4 * ix + 2 * iy + ic
        return {k: lax.dynamic_update_slice(land, src[None], (me,) + (0,) * src.ndim)
                for k, land, src in zip(GATHER_STAGES[s], got, srcs)}

    ix, iy, ic = lax.axis_index("x"), lax.axis_index("y"), lax.axis_index("c")
    chips = [(ix, iy), (1 - ix, iy), (ix, 1 - iy), (1 - ix, 1 - iy)]
    g_idx = jnp.stack([4 * cx + 2 * cy + ic for cx, cy in chips]).astype(jnp.int32)
    r_idx = jnp.stack([2 * cx + cy for cx, cy in chips]).astype(jnp.int32)
    in_flight = []

    def reduce_grads(tag, blocks):
        keys = list(blocks)
        from_sibling = _pair_exchange(f"rs_pair_exchange_{tag}", [blocks[k] for k in keys])
        sums = [_pair_sum(f"rs_pair_sum_{k}", blocks[k], fs, g_idx, r_idx) for k, fs in zip(keys, from_sibling)]
        started = _chip_start(f"rs_chip_start_{tag}", [s[1] for s in sums])
        in_flight.append((tag, keys, [s[0] for s in sums], started))
        return started[-1]

    rep = {k: w[k] for k in REPLICATED}
    loss_row, grad_x, grads = _forward_backward(x[0], loss_target[0], rep, token, weights_of_stage, reduce_grads)

    item_out = {}
    for tag, keys, own, (c_send, c_recv, srcs, c_lands, _) in in_flight:
        _, from_chips = _chip_wait(f"rs_chip_wait_{tag}", c_send, c_recv, srcs, c_lands, grad_x)
        for k, o, fc in zip(keys, own, from_chips):
            item_out[k] = _sum_adamw(f"adamw_{k}", o, fc, w_it[k], m_it[k], v_it[k])
    item_out = [item_out[k] for k in ITEMS]

    def pack_rep(tree, last):
        flat = jnp.concatenate([tree[k].reshape(-1) for k in REPLICATED] + [last])
        return _pack_rows(flat, _round_up(-(-flat.shape[0] // LANES), 8), LANES)

    partials, = _all_gather("gather_small_grads", [pack_rep(grads, loss_row[0, :1])])
    zero = jnp.zeros((1,), F32)
    rep_out = _sum_adamw("adamw_replicated", partials, None, pack_rep(w, zero), pack_rep(mom_m, zero),
                         pack_rep(mom_v, zero))

    kinds = []
    for kind, r_arr in enumerate(rep_out):
        tree = _from_items({k: out[kind] for k, out in zip(ITEMS, item_out)})
        flat, off = r_arr.reshape(-1), 0
        for k in REPLICATED:
            tree[k] = flat[off:off + w[k].size].reshape(w[k].shape)
            off += w[k].size
        kinds.append(tree)
    loss = rep_out[0].reshape(-1)[off]
    outs = [loss, grad_x[None]]
    for tree in kinds:
        outs += [tree[k] for k in WEIGHTS]
    return tuple(outs)
```

```python
import functools

import jax
import jax.numpy as jnp
from jax import lax
from jax.experimental import pallas as pl
from jax.experimental.pallas import tpu as pltpu

F32 = jnp.float32
BF16 = jnp.bfloat16
PAYLOAD = jnp.bfloat16
HIGHEST = lax.Precision.HIGHEST
MESH = pl.DeviceIdType.MESH

NORM_EPS = 1e-6
SSD_HEAD_DIM = 64
SSD_N_GROUPS = 8
SSD_HPG = 4
SSD_D_STATE = 128
SSD_CONV_WIDTH = 4
SSD_CHUNK = 128
ATTN_HEAD_DIM = 64
ATTN_N_KV = 4
ATTN_REP = 4
ATTN_WINDOW = 128
ADAM_LR = 0.001
ADAM_B1 = 0.9
ADAM_B2 = 0.999
ADAM_EPS = 1e-08
ADAM_WD = 0.01
ADAM_STEP = 10

N_DEV = 8
LANES = 128
PACK_COLS = 1024
V7X_VMEM_LIMIT = 56 * 1024 * 1024

GW = SSD_HPG * SSD_HEAD_DIM
GC = GW + 2 * SSD_D_STATE


def _params(*sem):
    return pltpu.CompilerParams(dimension_semantics=sem, vmem_limit_bytes=V7X_VMEM_LIMIT)


def _tile(n, pref, mult=LANES):
    best = None
    t = mult
    while t <= min(n, pref):
        if n % t == 0:
            best = t
        t += mult
    return best if best is not None else n


def _round_up(n, m):
    return (n + m - 1) // m * m


def _acc(ref, val, first):
    @pl.when(first)
    def _():
        ref[...] = val

    @pl.when(jnp.logical_not(first))
    def _():
        ref[...] += val


def _dot(a, b):
    return lax.dot_general(a, b, (((1,), (0,)), ((), ())), preferred_element_type=F32)


def _dot_nt(a, b):
    return lax.dot_general(a, b, (((1,), (1,)), ((), ())), preferred_element_type=F32)


def _dot_tn(a, b):
    return lax.dot_general(a, b, (((0,), (0,)), ((), ())), preferred_element_type=F32)


def _dot_f32(a, b):
    return lax.dot_general(a, b, (((1,), (0,)), ((), ())), preferred_element_type=F32, precision=HIGHEST)


_DOTS = {"nn": _dot, "nt": _dot_nt, "tn": _dot_tn}


def _sigmoid(x):
    return 1.0 / (1.0 + jnp.exp(-x))


def _softplus(x):
    return jnp.maximum(x, 0.0) + jnp.log1p(jnp.exp(-jnp.abs(x)))


def _silu_grad(x, s):
    return s * (1.0 + x * (1.0 - s))


def _mm(name, a_list, b_list, mode, *, tm, tn, out_dtypes=(F32,), epilogue=None, tiles=(), rows=(), cols=(),
        col_blocks=False):
    npair = len(a_list)
    if mode == "tn":
        m = a_list[0].shape[1]
    else:
        m = a_list[0].shape[0]
    n = b_list[0].shape[0] if mode == "nt" else b_list[0].shape[1]
    n = _round_up(n, LANES)
    tm = _tile(m, tm, LANES if mode == "tn" else 8)
    tn = _tile(n, tn)
    assert m % tm == 0 and n % tn == 0, (name, m, n, tm, tn)
    dot = _DOTS[mode]

    def body(*refs):
        a_refs = refs[:npair]
        b_refs = refs[npair:2 * npair]
        n_extra = len(tiles) + len(rows) + len(cols)
        e_refs = refs[2 * npair:2 * npair + n_extra]
        o_refs = refs[2 * npair + n_extra:]
        acc = None
        for ar, br in zip(a_refs, b_refs):
            d = dot(ar[...], br[...])
            acc = d if acc is None else acc + d
        outs = epilogue(acc, *[e[...] for e in e_refs]) if epilogue is not None else (acc,)
        for o, v in zip(o_refs, outs):
            o[...] = v.astype(o.dtype)

    in_specs = []
    for a in a_list:
        if mode == "tn":
            in_specs.append(pl.BlockSpec((a.shape[0], tm), lambda i, j: (0, i)))
        else:
            in_specs.append(pl.BlockSpec((tm, a.shape[1]), lambda i, j: (i, 0)))
    for b in b_list:
        if mode == "nt":
            in_specs.append(pl.BlockSpec((tn, b.shape[1]), lambda i, j: (j, 0)))
        else:
            in_specs.append(pl.BlockSpec((b.shape[0], tn), lambda i, j: (0, j)))
    in_specs += [pl.BlockSpec((tm, tn), lambda i, j: (i, j)) for _ in tiles]
    in_specs += [pl.BlockSpec((1, tn), lambda i, j: (0, j)) for _ in rows]
    in_specs += [pl.BlockSpec((tm, 1), lambda i, j: (i, 0)) for _ in cols]
    outs = pl.pallas_call(
        body,
        name=name,
        grid=(m // tm, n // tn),
        in_specs=in_specs,
        out_specs=[pl.BlockSpec((None, tm, tn), lambda i, j: (j, i, 0)) if col_blocks else
                   pl.BlockSpec((tm, tn), lambda i, j: (i, j)) for _ in out_dtypes],
        out_shape=[jax.ShapeDtypeStruct((n // tn, m, tn) if col_blocks else (m, n), dt) for dt in out_dtypes],
        compiler_params=_params("parallel", "parallel"),
    )(*a_list, *b_list, *tiles, *rows, *cols)
    return outs[0] if len(out_dtypes) == 1 else outs


def _rms(x, w):
    r = lax.rsqrt(jnp.mean(x * x, axis=-1, keepdims=True) + NORM_EPS)
    return x * r * w


def _rms_bwd(x, w, dy):
    r = lax.rsqrt(jnp.mean(x * x, axis=-1, keepdims=True) + NORM_EPS)
    xh = x * r
    g = dy * w
    dx = r * (g - xh * jnp.mean(g * xh, axis=-1, keepdims=True))
    return dx, dy * xh


def _row_specs(tr, d):
    return pl.BlockSpec((tr, d), lambda i: (i, 0)), pl.BlockSpec((1, d), lambda i: (0, 0))


def _prenorm(name, h, w, after):
    t, d = h.shape
    tr = _tile(t, 512, 8)
    row, vec = _row_specs(tr, d)

    def body(h_ref, w_ref, after_ref, u_ref):
        u_ref[...] = _rms(h_ref[...], w_ref[...]).astype(BF16)

    return pl.pallas_call(body, name=name, grid=(t // tr,),
                          in_specs=[row, vec, pl.BlockSpec((8, LANES), lambda i: (0, 0))], out_specs=row,
                          out_shape=jax.ShapeDtypeStruct((t, d), BF16), compiler_params=_params("parallel"))(
                              h, w, after)


def _post_pre(name, h, m, w_post, w_pre):
    t, d = h.shape
    tr = _tile(t, 512, 8)
    row, vec = _row_specs(tr, d)

    def body(h_ref, m_ref, wq_ref, wp_ref, hn_ref, u_ref):
        hn = h_ref[...] + _rms(m_ref[...], wq_ref[...])
        hn_ref[...] = hn
        u_ref[...] = _rms(hn, wp_ref[...]).astype(BF16)

    return pl.pallas_call(body, name=name, grid=(t // tr,), in_specs=[row, row, vec, vec], out_specs=[row, row],
                          out_shape=[jax.ShapeDtypeStruct((t, d), F32), jax.ShapeDtypeStruct((t, d), BF16)],
                          compiler_params=_params("parallel"))(h, m, w_post, w_pre)


def _final_loss(name, h, m, w_post, target):
    t, d = h.shape
    tr = _tile(t, 512, 8)
    row, vec = _row_specs(tr, d)

    def body(h_ref, m_ref, wq_ref, t_ref, dh_ref, loss_ref):
        err = h_ref[...] + _rms(m_ref[...], wq_ref[...]) - t_ref[...]
        dh_ref[...] = err * (1.0 / d)
        part = 0.5 * jnp.sum(jnp.mean(err * err, axis=-1, keepdims=True), axis=0, keepdims=True)
        _acc(loss_ref, jnp.broadcast_to(part, (1, LANES)), pl.program_id(0) == 0)

    return pl.pallas_call(body, name=name, grid=(t // tr,), in_specs=[row, row, vec, row],
                          out_specs=[row, pl.BlockSpec((1, LANES), lambda i: (0, 0))],
                          out_shape=[jax.ShapeDtypeStruct((t, d), F32), jax.ShapeDtypeStruct((1, LANES), F32)],
                          compiler_params=_params("arbitrary"))(h, m, w_post, target)


def _norm_bwd(name, dh, pre=None, post=None, after=None):
    t, d = dh.shape
    tr = _tile(t, 256, 8)
    row, vec = _row_specs(tr, d)
    has_pre, has_post = pre is not None, post is not None

    def body(*refs):
        it = iter(refs)
        dh_ref = next(it)
        if has_pre:
            du_ref, x_ref, wp_ref = next(it), next(it), next(it)
        if has_post:
            m_ref, wq_ref = next(it), next(it)
        if after is not None:
            next(it)
        first = pl.program_id(0) == 0
        dh_v = dh_ref[...]
        if has_pre:
            dhn_ref, dwp_ref = next(it), next(it)
            dx, dwr = _rms_bwd(x_ref[...], wp_ref[...], du_ref[...])
            dh_v = dh_v + dx
            dhn_ref[...] = dh_v
            _acc(dwp_ref, jnp.sum(dwr, axis=0, keepdims=True), first)
        if has_post:
            dm_ref, dwq_ref, dms_ref = next(it), next(it), next(it)
            dm, dwr = _rms_bwd(m_ref[...], wq_ref[...], dh_v)
            dm_ref[...] = dm.astype(BF16)
            _acc(dwq_ref, jnp.sum(dwr, axis=0, keepdims=True), first)
            _acc(dms_ref, jnp.sum(dm, axis=0, keepdims=True), first)

    ins, in_specs, out_specs, out_shape = [dh], [row], [], []
    if has_pre:
        ins += list(pre)
        in_specs += [row, row, vec]
        out_specs += [row, vec]
        out_shape += [jax.ShapeDtypeStruct((t, d), F32), jax.ShapeDtypeStruct((1, d), F32)]
    if has_post:
        ins += list(post)
        in_specs += [row, vec]
        out_specs += [row, vec, vec]
        out_shape += [jax.ShapeDtypeStruct((t, d), BF16), jax.ShapeDtypeStruct((1, d), F32),
                      jax.ShapeDtypeStruct((1, d), F32)]
    if after is not None:
        ins.append(after)
        in_specs.append(pl.BlockSpec((8, LANES), lambda i: (0, 0)))
    return pl.pallas_call(body, name=name, grid=(t // tr,), in_specs=in_specs, out_specs=out_specs,
                          out_shape=out_shape, compiler_params=_params("arbitrary"))(*ins)


HALO = 8


def _conv_fwd(zx, col0, n_ch, conv_w, conv_b):
    t = zx.shape[0]
    tc = _tile(n_ch, 512)
    tt = _tile(t, 512, 8)
    cb0 = col0 // tc
    assert col0 % tc == 0
    kw = SSD_CONV_WIDTH

    def body(x_ref, p_ref, w_ref, b_ref, o_ref, xe_ref):
        i = pl.program_id(1)
        cur = x_ref[...]
        xe_ref[0:HALO, :] = jnp.where(i > 0, p_ref[...], 0.0)
        xe_ref[HALO:HALO + tt, :] = cur
        w = w_ref[...]
        acc = b_ref[...] + w[kw - 1:kw, :] * cur
        for k in range(kw - 1):
            acc = acc + w[k:k + 1, :] * xe_ref[pl.ds(HALO - (kw - 1) + k, tt), :]
        o_ref[...] = acc

    return pl.pallas_call(
        body, name="ssd_conv_fwd", grid=(n_ch // tc, t // tt),
        in_specs=[pl.BlockSpec((tt, tc), lambda j, i: (i, cb0 + j)),
                  pl.BlockSpec((HALO, tc), lambda j, i: (jnp.maximum(i * (tt // HALO) - 1, 0), cb0 + j)),
                  pl.BlockSpec((kw, tc), lambda j, i: (0, j)),
                  pl.BlockSpec((1, tc), lambda j, i: (0, j))],
        out_specs=pl.BlockSpec((tt, tc), lambda j, i: (i, j)),
        out_shape=jax.ShapeDtypeStruct((t, n_ch), F32),
        scratch_shapes=[pltpu.VMEM((tt + HALO, tc), F32)],
        compiler_params=_params("parallel", "parallel"))(zx, zx, conv_w, conv_b)


def _conv_bwd(name, dpre, zx, col0, conv_w):
    t, n_ch = dpre.shape
    tc = _tile(n_ch, 512)
    tt = _tile(t, 512, 8)
    cb0 = col0 // tc
    kw = SSD_CONV_WIDTH
    nt = t // tt

    def body(d_ref, dn_ref, x_ref, p_ref, w_ref, dx_ref, dw_ref, db_ref, de_ref, xe_ref):
        i = pl.program_id(1)
        d = d_ref[...]
        de_ref[0:tt, :] = d
        de_ref[tt:tt + HALO, :] = jnp.where(i < nt - 1, dn_ref[...], 0.0)
        xe_ref[0:HALO, :] = jnp.where(i > 0, p_ref[...], 0.0)
        xe_ref[HALO:HALO + tt, :] = x_ref[...]
        w = w_ref[...]
        dx = w[kw - 1:kw, :] * d
        for k in range(kw - 1):
            dx = dx + w[k:k + 1, :] * de_ref[pl.ds(kw - 1 - k, tt), :]
        dx_ref[...] = dx.astype(BF16)
        first = i == 0
        for k in range(kw):
            xs = xe_ref[pl.ds(HALO - (kw - 1) + k, tt), :]
            val = jnp.sum(d * xs, axis=0, keepdims=True)

            @pl.when(first)
            def _():
                dw_ref[k:k + 1, :] = val

            @pl.when(jnp.logical_not(first))
            def _():
                dw_ref[k:k + 1, :] += val
        _acc(db_ref, jnp.sum(d, axis=0, keepdims=True), first)

    return pl.pallas_call(
        body, name=name, grid=(n_ch // tc, nt),
        in_specs=[pl.BlockSpec((tt, tc), lambda j, i: (i, j)),
                  pl.BlockSpec((HALO, tc), lambda j, i: (jnp.minimum((i + 1) * (tt // HALO), t // HALO - 1), j)),
                  pl.BlockSpec((tt, tc), lambda j, i: (i, cb0 + j)),
                  pl.BlockSpec((HALO, tc), lambda j, i: (jnp.maximum(i * (tt // HALO) - 1, 0), cb0 + j)),
                  pl.BlockSpec((kw, tc), lambda j, i: (0, j))],
        out_specs=[pl.BlockSpec((tt, tc), lambda j, i: (i, j)),
                   pl.BlockSpec((kw, tc), lambda j, i: (0, j)),
                   pl.BlockSpec((1, tc), lambda j, i: (0, j))],
        out_shape=[jax.ShapeDtypeStruct((t, n_ch), BF16), jax.ShapeDtypeStruct((kw, n_ch), F32),
                   jax.ShapeDtypeStruct((1, n_ch), F32)],
        scratch_shapes=[pltpu.VMEM((tt + HALO, tc), F32), pltpu.VMEM((tt + HALO, tc), F32)],
        compiler_params=_params("parallel", "arbitrary"))(dpre, dpre, zx, zx, conv_w)


def _head_of_lane(shape, width):
    return lax.broadcasted_iota(jnp.int32, shape, len(shape) - 1) // width


def _expand(v, n_rows):
    head = _head_of_lane((n_rows, GW), SSD_HEAD_DIM)
    out = jnp.zeros((n_rows, GW), F32)
    for j in range(SSD_HPG):
        out = jnp.where(head == j, v[:, j:j + 1], out)
    return out


def _contract(v, n_rows):
    head = _head_of_lane((n_rows, GW), SSD_HEAD_DIM)
    lane = lax.broadcasted_iota(jnp.int32, (n_rows, LANES), 1)
    out = jnp.zeros((n_rows, LANES), F32)
    for j in range(SSD_HPG):
        s = jnp.sum(jnp.where(head == j, v, 0.0), axis=1, keepdims=True)
        out = jnp.where(lane == j, s, out)
    return out


def _ssd_common(pre, dtc, bias_c, alog_c, dtr, bias_r, alog_r):
    q = SSD_CHUNK
    sg = _sigmoid(pre)
    act = pre * sg
    xa = act[:, :GW]
    bm = act[:, GW:GW + SSD_D_STATE].astype(BF16)
    cm = act[:, GW + SSD_D_STATE:].astype(BF16)
    row = lax.broadcasted_iota(jnp.int32, (q, q), 0)
    col = lax.broadcasted_iota(jnp.int32, (q, q), 1)
    tril = col <= row
    dt = _softplus(dtc + bias_c)
    a_c = -jnp.exp(alog_c)
    cum = _dot_f32(tril.astype(F32), dt * a_c)
    dt_r = _softplus(dtr + bias_r)
    cum_r = _dot_f32(dt_r * (-jnp.exp(alog_r)), (row <= col).astype(F32))
    g = _dot_nt(cm, bm)
    dt_x = _expand(dt, q)
    xdt = xa * dt_x
    cl = cum[q - 1:q, :]
    e_c = jnp.exp(cl - cum)
    lam_c = jnp.exp(cum)
    return dict(sg=sg, xa=xa, bm=bm, cm=cm, tril=tril, row=row, col=col, dt=dt, a_c=a_c, cum=cum, cum_r=cum_r,
                g=g, dt_x=dt_x, xdt=xdt, cl=cl, e_c=e_c, lam_c=lam_c)


def _ssd_specs(nc, rev, ng):
    q = SSD_CHUNK
    b_off = ng * GW // SSD_D_STATE

    def ch(c):
        return nc - 1 - c if rev else c

    chunk_grp = [pl.BlockSpec((q, GW), lambda g, c: (ch(c), g)),
                 pl.BlockSpec((q, SSD_D_STATE), lambda g, c: (ch(c), b_off + g)),
                 pl.BlockSpec((q, SSD_D_STATE), lambda g, c: (ch(c), b_off + ng + g))]
    col_form = pl.BlockSpec((None, q, LANES), lambda g, c: (g, ch(c), 0))
    row_form = pl.BlockSpec((None, 8, q), lambda g, c: (g, 0, ch(c)))
    col_par = pl.BlockSpec((None, 1, LANES), lambda g, c: (g, 0, 0))
    row_par = pl.BlockSpec((None, 8, 1), lambda g, c: (g, 0, 0))
    y_spec = pl.BlockSpec((q, GW), lambda g, c: (ch(c), g))
    st_spec = pl.BlockSpec((None, None, GW, SSD_D_STATE), lambda g, c: (g, ch(c), 0, 0))
    bc_spec = pl.BlockSpec((q, SSD_D_STATE), lambda g, c: (ch(c), g))
    return chunk_grp, col_form, row_form, col_par, row_par, y_spec, st_spec, bc_spec


def _ssd_fwd(pre, dtc, dtr, bias_c, alog_c, dsk_c, bias_r, alog_r):
    t = pre.shape[0]
    ng = pre.shape[1] // GC
    q = SSD_CHUNK
    nc = t // q
    chunk_grp, col_form, row_form, col_par, row_par, y_spec, st_spec, _ = _ssd_specs(nc, False, ng)

    def body(px_ref, pb_ref, pc_ref, dtc_ref, dtr_ref, bc_ref, ac_ref, dk_ref, br_ref, ar_ref, y_ref, sp_ref, st_ref):
        @pl.when(pl.program_id(1) == 0)
        def _():
            st_ref[...] = jnp.zeros_like(st_ref)

        pre_v = jnp.concatenate([px_ref[...], pb_ref[...], pc_ref[...]], axis=1)
        v = _ssd_common(pre_v, dtc_ref[...], bc_ref[...], ac_ref[...], dtr_ref[...], br_ref[...], ar_ref[...])
        s0 = st_ref[...]
        sp_ref[...] = s0
        r = _dot_nt(v["cm"], s0.astype(BF16))
        y = _expand(v["lam_c"], q) * r + _expand(dk_ref[...], 1) * v["xa"]
        head = _head_of_lane((q, GW), SSD_HEAD_DIM)
        for j in range(SSD_HPG):
            diff = v["cum"][:, j:j + 1] - v["cum_r"][j:j + 1, :]
            w = (v["g"] * jnp.exp(jnp.where(v["tril"], diff, -jnp.inf))).astype(BF16)
            y = y + _dot(w, jnp.where(head == j, v["xdt"], 0.0).astype(BF16))
        y_ref[...] = y
        ds = _dot_tn((v["xdt"] * _expand(v["e_c"], q)).astype(BF16), v["bm"])
        for j in range(SSD_HPG):
            rows = slice(j * SSD_HEAD_DIM, (j + 1) * SSD_HEAD_DIM)
            st_ref[rows, :] = s0[rows, :] * jnp.exp(v["cum_r"][j:j + 1, q - 1:q]) + ds[rows, :]

    return pl.pallas_call(
        body, name="ssd_scan_fwd", grid=(ng, nc),
        in_specs=chunk_grp + [col_form, row_form, col_par, col_par, col_par, row_par, row_par],
        out_specs=[y_spec, st_spec],
        out_shape=[jax.ShapeDtypeStruct((t, ng * GW), F32), jax.ShapeDtypeStruct((ng, nc, GW, SSD_D_STATE), F32)],
        scratch_shapes=[pltpu.VMEM((GW, SSD_D_STATE), F32)],
        compiler_params=_params("parallel", "arbitrary"))(pre, pre, pre, dtc, dtr, bias_c, alog_c, dsk_c, bias_r,
                                                           alog_r)


def _ssd_bwd(dy, pre, states, dtc, dtr, bias_c, alog_c, dsk_c, bias_r, alog_r):
    t = pre.shape[0]
    ng = pre.shape[1] // GC
    q = SSD_CHUNK
    nc = t // q
    chunk_grp, col_form, row_form, col_par, row_par, y_spec, st_spec, bc_spec = _ssd_specs(nc, True, ng)

    def body(dy_ref, px_ref, pb_ref, pc_ref, sp_ref, dtc_ref, dtr_ref, bc_ref, ac_ref, dk_ref, br_ref, ar_ref,
             dpx_ref, dpb_ref, dpc_ref, ddt_ref, dbias_ref, dalog_ref, dd_ref, ds_ref):
        first = pl.program_id(1) == 0

        @pl.when(first)
        def _():
            ds_ref[...] = jnp.zeros_like(ds_ref)

        pre_v = jnp.concatenate([px_ref[...], pb_ref[...], pc_ref[...]], axis=1)
        v = _ssd_common(pre_v, dtc_ref[...], bc_ref[...], ac_ref[...], dtr_ref[...], br_ref[...], ar_ref[...])
        xa, bm, cm, xdt, cum, cum_r = v["xa"], v["bm"], v["cm"], v["xdt"], v["cum"], v["cum_r"]
        xdt_b = xdt.astype(BF16)
        dy_v = dy_ref[...]
        s0 = sp_ref[...]
        ds1 = ds_ref[...]
        s0b, ds1b = s0.astype(BF16), ds1.astype(BF16)
        head = _head_of_lane((q, GW), SSD_HEAD_DIM)
        lane = lax.broadcasted_iota(jnp.int32, (q, LANES), 1)
        lane1 = lax.broadcasted_iota(jnp.int32, (1, LANES), 1)
        lam_x = _expand(v["lam_c"], q)
        e_x = _expand(v["e_c"], q)

        dxa = _expand(dk_ref[...], 1) * dy_v
        dd = _contract(jnp.sum(dy_v * xa, axis=0, keepdims=True), 1)
        r = _dot_nt(cm, s0b)
        dcum = _contract(dy_v * r * lam_x, q)
        drb = (lam_x * dy_v).astype(BF16)
        dc = _dot(drb, s0b)
        ds0 = _dot_tn(drb, cm)
        extra = jnp.zeros((1, LANES), F32)
        for j in range(SSD_HPG):
            rows = slice(j * SSD_HEAD_DIM, (j + 1) * SSD_HEAD_DIM)
            lam_last = jnp.exp(cum_r[j:j + 1, q - 1:q])
            ds_ref[rows, :] = ds0[rows, :] + lam_last * ds1[rows, :]
            tot = jnp.sum(jnp.sum(ds1[rows, :] * s0[rows, :], axis=1, keepdims=True), axis=0, keepdims=True)
            extra = jnp.where(lane1 == j, lam_last * tot, extra)
        dv = _dot_nt(bm, ds1b)
        db = _dot((xdt * e_x).astype(BF16), ds1b)
        dxdt = e_x * dv
        dee = _contract(dv * xdt, q) * v["e_c"]
        dcum = dcum - dee
        extra = extra + jnp.sum(dee, axis=0, keepdims=True)
        dg = jnp.zeros((q, q), F32)
        for j in range(SSD_HPG):
            diff = cum[:, j:j + 1] - cum_r[j:j + 1, :]
            el = jnp.exp(jnp.where(v["tril"], diff, -jnp.inf))
            gl = v["g"] * el
            dym = jnp.where(head == j, dy_v, 0.0).astype(BF16)
            dwm = _dot_nt(dym, xdt_b)
            dxdt = dxdt + _dot_tn(gl.astype(BF16), dym)
            z = dwm * gl
            rk = jnp.sum(z, axis=1, keepdims=True) - jnp.sum(z.T, axis=1, keepdims=True)
            dcum = jnp.where(lane == j, dcum + rk, dcum)
            dg = dg + dwm * el
        dgb = dg.astype(BF16)
        dc = dc + _dot(dgb, bm)
        db = db + _dot_tn(dgb, cm)
        da = _dot_f32((v["row"] <= v["col"]).astype(F32), dcum) + extra
        ddt = _contract(dxdt * xa, q) + v["a_c"] * da
        dalog = jnp.sum(v["dt"] * da, axis=0, keepdims=True) * v["a_c"]
        dxa = dxa + v["dt_x"] * dxdt
        ddt_raw = jnp.where(lane < SSD_HPG, ddt * _sigmoid(dtc_ref[...] + bc_ref[...]), 0.0)
        sgrad = _silu_grad(pre_v, v["sg"])
        dpx_ref[...] = dxa * sgrad[:, :GW]
        dpb_ref[...] = db * sgrad[:, GW:GW + SSD_D_STATE]
        dpc_ref[...] = dc * sgrad[:, GW + SSD_D_STATE:]
        ddt_ref[...] = ddt_raw
        _acc(dbias_ref, jnp.sum(ddt_raw, axis=0, keepdims=True), first)
        _acc(dalog_ref, jnp.where(lane1 < SSD_HPG, dalog, 0.0), first)
        _acc(dd_ref, dd, first)

    return pl.pallas_call(
        body, name="ssd_scan_bwd", grid=(ng, nc),
        in_specs=[y_spec] + chunk_grp + [st_spec, col_form, row_form, col_par, col_par, col_par, row_par, row_par],
        out_specs=[y_spec, bc_spec, bc_spec, col_form, col_par, col_par, col_par],
        out_shape=[jax.ShapeDtypeStruct((t, ng * GW), F32), jax.ShapeDtypeStruct((t, ng * SSD_D_STATE), F32),
                   jax.ShapeDtypeStruct((t, ng * SSD_D_STATE), F32), jax.ShapeDtypeStruct((ng, t, LANES), F32),
                   jax.ShapeDtypeStruct((ng, 1, LANES), F32), jax.ShapeDtypeStruct((ng, 1, LANES), F32),
                   jax.ShapeDtypeStruct((ng, 1, LANES), F32)],
        scratch_shapes=[pltpu.VMEM((GW, SSD_D_STATE), F32)],
        compiler_params=_params("parallel", "arbitrary"))(dy, pre, pre, pre, states, dtc, dtr, bias_c, alog_c, dsk_c,
                                                           bias_r, alog_r)


def _gate_norm_fwd(y, zx, norm_w):
    t, di = y.shape
    tr = _tile(t, 256, 8)
    ng = di // GW

    def body(y_ref, z_ref, w_ref, o_ref):
        z = z_ref[...]
        gate = y_ref[...] * (z * _sigmoid(z))
        w = w_ref[...]
        for g in range(ng):
            cols = slice(g * GW, (g + 1) * GW)
            gs = gate[:, cols]
            r = lax.rsqrt(jnp.mean(gs * gs, axis=-1, keepdims=True) + NORM_EPS)
            o_ref[:, cols] = (gs * r * w[:, cols]).astype(BF16)

    row = pl.BlockSpec((tr, di), lambda i: (i, 0))
    return pl.pallas_call(body, name="ssd_gate_norm_fwd", grid=(t // tr,),
                          in_specs=[row, row, pl.BlockSpec((1, di), lambda i: (0, 0))], out_specs=row,
                          out_shape=jax.ShapeDtypeStruct((t, di), BF16), compiler_params=_params("parallel"))(
                              y, zx, norm_w)


def _gate_norm_bwd(dyn, y, zx, norm_w):
    t, di = y.shape
    tr = _tile(t, 256, 8)
    ng = di // GW

    def body(d_ref, y_ref, z_ref, w_ref, dy_ref, dz_ref, dw_ref):
        z = z_ref[...]
        yv = y_ref[...]
        sg = _sigmoid(z)
        sz = z * sg
        gate = yv * sz
        w = w_ref[...]
        d = d_ref[...]
        dsz = _silu_grad(z, sg)
        dws = []
        for g in range(ng):
            cols = slice(g * GW, (g + 1) * GW)
            dg, dwr = _rms_bwd(gate[:, cols], w[:, cols], d[:, cols])
            dy_ref[:, cols] = dg * sz[:, cols]
            dz_ref[:, cols] = (dg * yv[:, cols] * dsz[:, cols]).astype(BF16)
            dws.append(jnp.sum(dwr, axis=0, keepdims=True))
        first = pl.program_id(0) == 0
        for g in range(ng):
            cols = slice(g * GW, (g + 1) * GW)

            @pl.when(first)
            def _():
                dw_ref[:, cols] = dws[g]

            @pl.when(jnp.logical_not(first))
            def _():
                dw_ref[:, cols] += dws[g]

    row = pl.BlockSpec((tr, di), lambda i: (i, 0))
    vec = pl.BlockSpec((1, di), lambda i: (0, 0))
    return pl.pallas_call(body, name="ssd_gate_norm_bwd", grid=(t // tr,), in_specs=[row, row, row, vec],
                          out_specs=[row, row, vec],
                          out_shape=[jax.ShapeDtypeStruct((t, di), F32), jax.ShapeDtypeStruct((t, di), BF16),
                                     jax.ShapeDtypeStruct((1, di), F32)],
                          compiler_params=_params("arbitrary"))(dyn, y, zx, norm_w)


def _attn_mask(n):
    w = ATTN_WINDOW
    qpos = lax.broadcasted_iota(jnp.int32, (w, 2 * w), 0) + w
    kpos = lax.broadcasted_iota(jnp.int32, (w, 2 * w), 1)
    rel = qpos - kpos
    return (rel >= 0) & (rel < w) & jnp.logical_not((n == 0) & (kpos < w))


def _attn_probs(qh, kbh, mask, sink):
    s = _dot_nt(qh, kbh) * (ATTN_HEAD_DIM ** -0.5)
    s = jnp.where(mask, s, -jnp.inf)
    m = jnp.maximum(jnp.max(s, axis=-1, keepdims=True), sink)
    e = jnp.exp(s - m)
    es = jnp.exp(sink - m)
    inv = 1.0 / (jnp.sum(e, axis=-1, keepdims=True) + es)
    return e * inv, es * inv


def _attn_fwd(qkv, sinks):
    t = qkv.shape[0]
    w, hd = ATTN_WINDOW, ATTN_HEAD_DIM
    kd = ATTN_N_KV * hd
    qd = ATTN_REP * kd
    nb = t // w

    def body(q_ref, kc_ref, vc_ref, kp_ref, vp_ref, s_ref, o_ref):
        n = pl.program_id(0)
        mask = _attn_mask(n)
        q = q_ref[...]
        kb = jnp.concatenate([kp_ref[...], kc_ref[...]], axis=0)
        vb = jnp.concatenate([vp_ref[...], vc_ref[...]], axis=0)
        sk = s_ref[...]
        for kv in range(ATTN_N_KV):
            kbh = kb[:, kv * hd:(kv + 1) * hd]
            vbh = vb[:, kv * hd:(kv + 1) * hd]
            for rep in range(ATTN_REP):
                h = kv * ATTN_REP + rep
                p, _ = _attn_probs(q[:, h * hd:(h + 1) * hd], kbh, mask, sk[:, h:h + 1])
                o_ref[:, h * hd:(h + 1) * hd] = _dot(p.astype(BF16), vbh).astype(BF16)

    prev = lambda n: jnp.maximum(n - 1, 0)
    return pl.pallas_call(
        body, name="attn_fwd", grid=(nb,),
        in_specs=[pl.BlockSpec((w, qd), lambda n: (n, 0)),
                  pl.BlockSpec((w, kd), lambda n: (n, ATTN_REP)),
                  pl.BlockSpec((w, kd), lambda n: (n, ATTN_REP + 1)),
                  pl.BlockSpec((w, kd), lambda n: (prev(n), ATTN_REP)),
                  pl.BlockSpec((w, kd), lambda n: (prev(n), ATTN_REP + 1)),
                  pl.BlockSpec((1, sinks.shape[1]), lambda n: (0, 0))],
        out_specs=pl.BlockSpec((w, qd), lambda n: (n, 0)),
        out_shape=jax.ShapeDtypeStruct((t, qd), BF16),
        compiler_params=_params("parallel"))(qkv, qkv, qkv, qkv, qkv, sinks)


def _attn_bwd(qkv, do, sinks):
    t = qkv.shape[0]
    w, hd = ATTN_WINDOW, ATTN_HEAD_DIM
    kd = ATTN_N_KV * hd
    qd = ATTN_REP * kd
    nq = ATTN_N_KV * ATTN_REP
    nb = t // w

    def body(q_ref, kc_ref, vc_ref, kp_ref, vp_ref, do_ref, s_ref,
             dq_ref, dk_ref, dv_ref, bq_ref, bk_ref, bv_ref, dsk_ref, ck_ref, cv_ref):
        n = pl.program_id(0)
        first = n == 0

        @pl.when(first)
        def _():
            ck_ref[...] = jnp.zeros_like(ck_ref)
            cv_ref[...] = jnp.zeros_like(cv_ref)
            bq_ref[...] = jnp.zeros_like(bq_ref)
            bk_ref[...] = jnp.zeros_like(bk_ref)
            bv_ref[...] = jnp.zeros_like(bv_ref)
            dsk_ref[...] = jnp.zeros_like(dsk_ref)

        @pl.when(n < nb)
        def _():
            mask = _attn_mask(n)
            q = q_ref[...]
            dov = do_ref[...]
            kb = jnp.concatenate([kp_ref[...], kc_ref[...]], axis=0)
            vb = jnp.concatenate([vp_ref[...], vc_ref[...]], axis=0)
            sk = s_ref[...]
            lane = lax.broadcasted_iota(jnp.int32, (1, nq), 1)
            dsk = jnp.zeros((1, nq), F32)
            dq_parts, dk_parts, dv_parts = [], [], []
            for kv in range(ATTN_N_KV):
                kbh = kb[:, kv * hd:(kv + 1) * hd]
                vbh = vb[:, kv * hd:(kv + 1) * hd]
                dkh = jnp.zeros((2 * w, hd), F32)
                dvh = jnp.zeros((2 * w, hd), F32)
                for rep in range(ATTN_REP):
                    h = kv * ATTN_REP + rep
                    qh = q[:, h * hd:(h + 1) * hd]
                    doh = dov[:, h * hd:(h + 1) * hd]
                    p, ps = _attn_probs(qh, kbh, mask, sk[:, h:h + 1])
                    pb = p.astype(BF16)
                    dp = _dot_nt(doh, vbh)
                    delta = jnp.sum(p * dp, axis=-1, keepdims=True)
                    dsc = (p * (dp - delta) * (hd ** -0.5)).astype(BF16)
                    dq_parts.append(_dot(dsc, kbh))
                    dkh = dkh + _dot_tn(dsc, qh)
                    dvh = dvh + _dot_tn(pb, doh)
                    dsk = jnp.where(lane == h, -jnp.sum(ps * delta, axis=0, keepdims=True), dsk)
                dk_parts.append(dkh)
                dv_parts.append(dvh)
            dq = jnp.concatenate(dq_parts, axis=1)
            dkb = jnp.concatenate(dk_parts, axis=1)
            dvb = jnp.concatenate(dv_parts, axis=1)
            dq_ref[...] = dq.astype(BF16)
            bq_ref[...] += jnp.sum(dq, axis=0, keepdims=True)
            dsk_ref[...] += dsk
            dk_prev = ck_ref[...] + dkb[:w, :]
            dv_prev = cv_ref[...] + dvb[:w, :]
            dk_ref[...] = dk_prev.astype(BF16)
            dv_ref[...] = dv_prev.astype(BF16)

            @pl.when(n > 0)
            def _():
                bk_ref[...] += jnp.sum(dk_prev, axis=0, keepdims=True)
                bv_ref[...] += jnp.sum(dv_prev, axis=0, keepdims=True)

            ck_ref[...] = dkb[w:, :]
            cv_ref[...] = dvb[w:, :]

        @pl.when(n == nb)
        def _():
            dk_ref[...] = ck_ref[...].astype(BF16)
            dv_ref[...] = cv_ref[...].astype(BF16)
            bk_ref[...] += jnp.sum(ck_ref[...], axis=0, keepdims=True)
            bv_ref[...] += jnp.sum(cv_ref[...], axis=0, keepdims=True)

    cur = lambda n: jnp.minimum(n, nb - 1)
    prev = lambda n: jnp.maximum(jnp.minimum(n, nb - 1) - 1, 0)
    late = lambda n: jnp.maximum(n - 1, 0)
    vec = lambda width: pl.BlockSpec((1, width), lambda n: (0, 0))
    return pl.pallas_call(
        body, name="attn_bwd", grid=(nb + 1,),
        in_specs=[pl.BlockSpec((w, qd), lambda n: (cur(n), 0)),
                  pl.BlockSpec((w, kd), lambda n: (cur(n), ATTN_REP)),
                  pl.BlockSpec((w, kd), lambda n: (cur(n), ATTN_REP + 1)),
                  pl.BlockSpec((w, kd), lambda n: (prev(n), ATTN_REP)),
                  pl.BlockSpec((w, kd), lambda n: (prev(n), ATTN_REP + 1)),
                  pl.BlockSpec((w, qd), lambda n: (cur(n), 0)),
                  vec(nq)],
        out_specs=[pl.BlockSpec((w, qd), lambda n: (cur(n), 0)),
                   pl.BlockSpec((w, kd), lambda n: (late(n), 0)),
                   pl.BlockSpec((w, kd), lambda n: (late(n), 0)),
                   vec(qd), vec(kd), vec(kd), vec(nq)],
        out_shape=[jax.ShapeDtypeStruct((t, qd), BF16), jax.ShapeDtypeStruct((t, kd), BF16),
                   jax.ShapeDtypeStruct((t, kd), BF16), jax.ShapeDtypeStruct((1, qd), F32),
                   jax.ShapeDtypeStruct((1, kd), F32), jax.ShapeDtypeStruct((1, kd), F32),
                   jax.ShapeDtypeStruct((1, nq), F32)],
        scratch_shapes=[pltpu.VMEM((w, kd), F32), pltpu.VMEM((w, kd), F32)],
        compiler_params=_params("arbitrary"))(qkv, qkv, qkv, qkv, qkv, do, sinks)


def _attn_mask_t(n):
    w = ATTN_WINDOW
    kpos = lax.broadcasted_iota(jnp.int32, (2 * w, ATTN_REP * w), 0)
    qpos = lax.broadcasted_iota(jnp.int32, (2 * w, ATTN_REP * w), 1) % w + w
    rel = qpos - kpos
    return (rel >= 0) & (rel < w) & jnp.logical_not((n == 0) & (kpos < w))


def _attn_probs_t(qts, ktb, mask, sink):
    s = _dot_tn(ktb, qts) * (ATTN_HEAD_DIM ** -0.5)
    s = jnp.where(mask, s, -jnp.inf)
    m = jnp.maximum(jnp.max(s, axis=0, keepdims=True), sink)
    e = jnp.exp(s - m)
    es = jnp.exp(sink - m)
    inv = 1.0 / (jnp.sum(e, axis=0, keepdims=True) + es)
    return e * inv, es * inv


def _attn_blocks_t(kv, q_ref, kc_ref, vc_ref, kp_ref, vp_ref):
    hd = ATTN_HEAD_DIM
    rows = slice(kv * hd, (kv + 1) * hd)
    ktb = jnp.concatenate([kp_ref[rows, :], kc_ref[rows, :]], axis=1)
    vtb = jnp.concatenate([vp_ref[rows, :], vc_ref[rows, :]], axis=1)
    qts = jnp.concatenate([q_ref[(kv * ATTN_REP + r) * hd:(kv * ATTN_REP + r + 1) * hd, :]
                           for r in range(ATTN_REP)], axis=1)
    return qts, ktb, vtb


def _attn_specs_t(nb, cur, prev):
    w, hd = ATTN_WINDOW, ATTN_HEAD_DIM
    kd = ATTN_N_KV * hd
    qd = ATTN_REP * kd
    return [pl.BlockSpec((qd, w), lambda n: (0, cur(n))),
            pl.BlockSpec((kd, w), lambda n: (ATTN_REP, cur(n))),
            pl.BlockSpec((kd, w), lambda n: (ATTN_REP + 1, cur(n))),
            pl.BlockSpec((kd, w), lambda n: (ATTN_REP, prev(n))),
            pl.BlockSpec((kd, w), lambda n: (ATTN_REP + 1, prev(n)))]


def _attn_fwd_t(qkv_t, sinks_rep):
    t = qkv_t.shape[1]
    w, hd = ATTN_WINDOW, ATTN_HEAD_DIM
    qd = ATTN_N_KV * ATTN_REP * hd
    nb = t // w

    def body(q_ref, kc_ref, vc_ref, kp_ref, vp_ref, s_ref, o_ref):
        mask = _attn_mask_t(pl.program_id(0))
        for kv in range(ATTN_N_KV):
            qts, ktb, vtb = _attn_blocks_t(kv, q_ref, kc_ref, vc_ref, kp_ref, vp_ref)
            p, _ = _attn_probs_t(qts, ktb, mask, s_ref[kv])
            ots = _dot(vtb, p.astype(BF16))
            for r in range(ATTN_REP):
                h = kv * ATTN_REP + r
                o_ref[h * hd:(h + 1) * hd, :] = ots[:, r * w:(r + 1) * w].astype(BF16)

    return pl.pallas_call(
        body, name="attn_fwd", grid=(nb,),
        in_specs=_attn_specs_t(nb, lambda n: n, lambda n: jnp.maximum(n - 1, 0)) + [
            pl.BlockSpec(sinks_rep.shape, lambda n: (0, 0, 0))],
        out_specs=pl.BlockSpec((qd, w), lambda n: (0, n)),
        out_shape=jax.ShapeDtypeStruct((qd, t), BF16),
        compiler_params=_params("parallel"))(qkv_t, qkv_t, qkv_t, qkv_t, qkv_t, sinks_rep)


def _attn_bwd_t(qkv_t, do_t, sinks_rep):
    t = qkv_t.shape[1]
    w, hd = ATTN_WINDOW, ATTN_HEAD_DIM
    kd = ATTN_N_KV * hd
    qd = ATTN_REP * kd
    nq = ATTN_N_KV * ATTN_REP
    nb = t // w
    rows_all = qd + 2 * kd

    def body(q_ref, kc_ref, vc_ref, kp_ref, vp_ref, do_ref, s_ref, dqkv_ref, bsum_ref, dsk_ref,
             carry_ref, new_ref, bacc_ref, sacc_ref):
        n = pl.program_id(0)

        @pl.when(n == 0)
        def _():
            carry_ref[...] = jnp.zeros_like(carry_ref)
            bacc_ref[...] = jnp.zeros_like(bacc_ref)
            sacc_ref[...] = jnp.zeros_like(sacc_ref)

        @pl.when(n < nb)
        def _():
            mask = _attn_mask_t(n)
            for kv in range(ATTN_N_KV):
                qts, ktb, vtb = _attn_blocks_t(kv, q_ref, kc_ref, vc_ref, kp_ref, vp_ref)
                dots = jnp.concatenate([do_ref[(kv * ATTN_REP + r) * hd:(kv * ATTN_REP + r + 1) * hd, :]
                                        for r in range(ATTN_REP)], axis=1)
                p, ps = _attn_probs_t(qts, ktb, mask, s_ref[kv])
                dpt = _dot_tn(vtb, dots)
                delta = jnp.sum(p * dpt, axis=0, keepdims=True)
                dst = (p * (dpt - delta) * (hd ** -0.5)).astype(BF16)
                dqts = _dot(ktb, dst)
                for r in range(ATTN_REP):
                    h = kv * ATTN_REP + r
                    new_ref[h * hd:(h + 1) * hd, :] = dqts[:, r * w:(r + 1) * w]
                dktb = _dot_nt(qts, dst)
                dvtb = _dot_nt(dots, p.astype(BF16))
                krows = slice(qd + kv * hd, qd + (kv + 1) * hd)
                vrows = slice(qd + kd + kv * hd, qd + kd + (kv + 1) * hd)
                carry_ref[krows, :] += dktb[:, :w]
                carry_ref[vrows, :] += dvtb[:, :w]
                new_ref[krows, :] = dktb[:, w:]
                new_ref[vrows, :] = dvtb[:, w:]
                sacc_ref[kv] += -(ps * delta)

        @pl.when(n >= 1)
        def _():
            done = carry_ref[...]
            dqkv_ref[...] = done.astype(BF16)
            bacc_ref[...] += done

        @pl.when(n < nb)
        def _():
            carry_ref[...] = new_ref[...]

        @pl.when(n == nb)
        def _():
            bsum_ref[...] = jnp.sum(bacc_ref[...], axis=1, keepdims=True)
            lane = lax.broadcasted_iota(jnp.int32, (1, nq), 1)
            dsk = jnp.zeros((1, nq), F32)
            for kv in range(ATTN_N_KV):
                acc = sacc_ref[kv]
                for r in range(ATTN_REP):
                    tot = jnp.sum(acc[:, r * w:(r + 1) * w], axis=1, keepdims=True)
                    dsk = jnp.where(lane == kv * ATTN_REP + r, tot, dsk)
            dsk_ref[...] = dsk

    cur = lambda n: jnp.minimum(n, nb - 1)
    prev = lambda n: jnp.maximum(jnp.minimum(n, nb - 1) - 1, 0)
    return pl.pallas_call(
        body, name="attn_bwd", grid=(nb + 1,),
        in_specs=_attn_specs_t(nb, cur, prev) + [pl.BlockSpec((qd, w), lambda n: (0, cur(n))),
                                                 pl.BlockSpec(sinks_rep.shape, lambda n: (0, 0, 0))],
        out_specs=[pl.BlockSpec((rows_all, w), lambda n: (0, jnp.maximum(n - 1, 0))),
                   pl.BlockSpec((rows_all, 1), lambda n: (0, 0)),
                   pl.BlockSpec((1, nq), lambda n: (0, 0))],
        out_shape=[jax.ShapeDtypeStruct((rows_all, t), BF16), jax.ShapeDtypeStruct((rows_all, 1), F32),
                   jax.ShapeDtypeStruct((1, nq), F32)],
        scratch_shapes=[pltpu.VMEM((rows_all, w), F32), pltpu.VMEM((rows_all, w), F32),
                        pltpu.VMEM((rows_all, w), F32), pltpu.VMEM(sinks_rep.shape, F32)],
        compiler_params=_params("arbitrary"))(qkv_t, qkv_t, qkv_t, qkv_t, qkv_t, do_t, sinks_rep)


HBM_SPEC = pl.BlockSpec(memory_space=pl.ANY)
HBM_ONLY = pl.BlockSpec(memory_space=pltpu.HBM)


def _comm_call(name, body, ins, out_shapes, n_sems):
    return pl.pallas_call(
        body, name=name, in_specs=[HBM_SPEC] * len(ins), out_specs=[HBM_SPEC] * len(out_shapes),
        out_shape=out_shapes,
        scratch_shapes=[pltpu.SemaphoreType.DMA((s,)) for s in n_sems])(*ins)


def _all_gather(name, shards, after):
    n = len(shards)

    def body(*refs):
        x_refs, out_refs = refs[:n], refs[n + 1:2 * n + 1]
        send_sems, recv_sems, local_sems = refs[2 * n + 1:]
        x, y, c = lax.axis_index("x"), lax.axis_index("y"), lax.axis_index("c")
        me, sibling = (x, y, c), (x, y, 1 - c)
        chips = [(1 - x, y), (x, 1 - y), (1 - x, 1 - y)]

        def slot(i, px, py, pc):
            return out_refs[i].at[4 * px + 2 * py + pc]

        def copy(k, i, block, to, src=None):
            return pltpu.make_async_remote_copy(
                src_ref=slot(i, *block) if src is None else src, dst_ref=slot(i, *block),
                send_sem=send_sems.at[k * n + i], recv_sem=recv_sems.at[k * n + i], device_id=to,
                device_id_type=MESH)

        mine = [pltpu.make_async_copy(x_refs[i], slot(i, *me), local_sems.at[i]) for i in range(n)]
        first = []
        for i in range(n):
            mine[i].start()
            first.append(copy(0, i, me, sibling, src=x_refs[i]))
            first += [copy(1 + j, i, me, (*chip, c), src=x_refs[i]) for j, chip in enumerate(chips)]
        for cp in first:
            cp.start()
        passed = []
        for i in range(n):
            for j, chip in enumerate(chips):
                copy(1 + j, i, (*chip, c), me).wait_recv()
                passed.append(copy(4 + j, i, (*chip, c), sibling))
                passed[-1].start()
        for i in range(n):
            copy(0, i, sibling, me).wait_recv()
            for j, chip in enumerate(chips):
                copy(4 + j, i, (*chip, 1 - c), me).wait_recv()
        for cp in first + passed:
            cp.wait_send()
        for cp in mine:
            cp.wait()

    outs = [jax.ShapeDtypeStruct((N_DEV,) + s.shape, s.dtype) for s in shards]
    return _comm_call(name, body, list(shards) + [after], outs, (7 * n, 7 * n, n))


SEM_SPEC = pl.BlockSpec(memory_space=pltpu.SEMAPHORE)
SPLIT_COPY_EFFECT = pltpu.SideEffectType.DATAFLOW_SIDE_EFFECTING


def _in_hbm(a):
    return pltpu.with_memory_space_constraint(a, pltpu.HBM)


def _split_start(name, body, srcs, lands, n_sems):
    n = len(srcs)
    bufs = [_in_hbm(a) for a in list(srcs) + list(lands)]
    outs = pl.pallas_call(
        body, name=name,
        out_shape=(pltpu.SemaphoreType.DMA((n_sems,)), pltpu.SemaphoreType.DMA((n_sems,)),
                   *[pltpu.HBM(a.shape, a.dtype) for a in bufs], jax.ShapeDtypeStruct((8, LANES), F32)),
        in_specs=[HBM_ONLY] * (2 * n),
        out_specs=(SEM_SPEC, SEM_SPEC, *[HBM_ONLY] * (2 * n), pl.BlockSpec(memory_space=pltpu.VMEM)),
        input_output_aliases={i: 2 + i for i in range(2 * n)},
        compiler_params=pltpu.CompilerParams(has_side_effects=SPLIT_COPY_EFFECT))(*bufs)
    return outs[0], outs[1], list(outs[2:2 + n]), list(outs[2 + n:2 + 2 * n]), outs[-1]


def _split_wait(name, body, send_sems, recv_sems, srcs, lands, after):
    n = len(srcs)
    outs = pl.pallas_call(
        body, name=name,
        out_shape=[pltpu.HBM(a.shape, a.dtype) for a in list(srcs) + list(lands)],
        in_specs=[HBM_ONLY] * (2 * n) + [SEM_SPEC, SEM_SPEC, HBM_SPEC],
        out_specs=[HBM_ONLY] * (2 * n),
        input_output_aliases={i: i for i in range(2 * n)},
        compiler_params=pltpu.CompilerParams(has_side_effects=SPLIT_COPY_EFFECT))(
            *srcs, *lands, send_sems, recv_sems, after)
    return list(outs[:n]), list(outs[n:])


N_PEERS = N_DEV - 1


def _gather_peers():
    x, y, c = lax.axis_index("x"), lax.axis_index("y"), lax.axis_index("c")
    flips = [(fx, fy, fc) for fx in (0, 1) for fy in (0, 1) for fc in (0, 1) if fx or fy or fc]
    return [(1 - x if fx else x, 1 - y if fy else y, 1 - c if fc else c) for fx, fy, fc in flips]


def _block_id(dev):
    return 4 * dev[0] + 2 * dev[1] + dev[2]


def _gather_start(name, shards):
    n = len(shards)

    def body(*refs):
        x_refs, land_refs = refs[:n], refs[n:2 * n]
        send_sems, recv_sems, token = refs[2 * n], refs[2 * n + 1], refs[-1]
        me = (lax.axis_index("x"), lax.axis_index("y"), lax.axis_index("c"))
        for i in range(n):
            for k, peer in enumerate(_gather_peers()):
                pltpu.make_async_remote_copy(
                    src_ref=x_refs[i], dst_ref=land_refs[i].at[_block_id(me)],
                    send_sem=send_sems.at[N_PEERS * i + k], recv_sem=recv_sems.at[N_PEERS * i + k],
                    device_id=peer, device_id_type=MESH).start()
        token[...] = jnp.zeros_like(token)

    lands = [lax.empty((N_DEV,) + s.shape, s.dtype) for s in shards]
    return _split_start(name, body, shards, lands, N_PEERS * n)


def _gather_wait(name, send_sems, recv_sems, first, shards, lands, after):
    n = len(shards)

    def body(*refs):
        x_refs, land_refs = refs[:n], refs[n:2 * n]
        send_sems, recv_sems = refs[2 * n], refs[2 * n + 1]
        for i in range(n):
            for k, peer in enumerate(_gather_peers()):
                cp = pltpu.make_async_remote_copy(
                    src_ref=x_refs[i], dst_ref=land_refs[i].at[_block_id(peer)],
                    send_sem=send_sems.at[N_PEERS * (first + i) + k],
                    recv_sem=recv_sems.at[N_PEERS * (first + i) + k],
                    device_id=peer, device_id_type=MESH)
                cp.wait_send()
                cp.wait_recv()

    return _split_wait(name, body, send_sems, recv_sems, shards, lands, after)


def _gather_forward(name, lands, shards):
    n = len(shards)

    def body(*refs):
        x_refs, out_refs = refs[n:2 * n], refs[2 * n:3 * n]
        send_sems, recv_sems, local_sems = refs[3 * n:]
        x, y, c = lax.axis_index("x"), lax.axis_index("y"), lax.axis_index("c")
        chips = [(1 - x, y), (x, 1 - y), (1 - x, 1 - y)]
        mine = [pltpu.make_async_copy(x_refs[i], out_refs[i].at[_block_id((x, y, c))], local_sems.at[i])
                for i in range(n)]
        passed = [pltpu.make_async_remote_copy(
            src_ref=out_refs[i].at[_block_id((*chip, c))], dst_ref=out_refs[i].at[_block_id((*chip, c))],
            send_sem=send_sems.at[3 * i + j], recv_sem=recv_sems.at[3 * i + j], device_id=(x, y, 1 - c),
            device_id_type=MESH) for i in range(n) for j, chip in enumerate(chips)]
        for cp in mine + passed:
            cp.start()
        for i in range(n):
            for j, chip in enumerate(chips):
                pltpu.make_async_remote_copy(
                    src_ref=out_refs[i].at[_block_id((*chip, c))], dst_ref=out_refs[i].at[_block_id((*chip, 1 - c))],
                    send_sem=send_sems.at[3 * i + j], recv_sem=recv_sems.at[3 * i + j], device_id=(x, y, 1 - c),
                    device_id_type=MESH).wait()
        for cp in mine:
            cp.wait()

    return pl.pallas_call(
        body, name=name, in_specs=[HBM_SPEC] * (2 * n), out_specs=[HBM_SPEC] * n,
        out_shape=[jax.ShapeDtypeStruct(a.shape, a.dtype) for a in lands],
        input_output_aliases={i: i for i in range(n)},
        scratch_shapes=[pltpu.SemaphoreType.DMA((3 * n,)), pltpu.SemaphoreType.DMA((3 * n,)),
                        pltpu.SemaphoreType.DMA((n,))])(*lands, *shards)


def _chip_peers():
    x, y, c = lax.axis_index("x"), lax.axis_index("y"), lax.axis_index("c")
    return [(1 - x, y, c), (x, 1 - y, c), (1 - x, 1 - y, c)]


def _chip_start(name, blocks):
    n = len(blocks)

    def body(*refs):
        p_refs, land_refs = refs[:n], refs[n:2 * n]
        send_sems, recv_sems, token = refs[2 * n], refs[2 * n + 1], refs[-1]
        for i in range(n):
            for j, peer in enumerate(_chip_peers()):
                pltpu.make_async_remote_copy(
                    src_ref=p_refs[i].at[j], dst_ref=land_refs[i].at[j], send_sem=send_sems.at[3 * i + j],
                    recv_sem=recv_sems.at[3 * i + j], device_id=peer, device_id_type=MESH).start()
        token[...] = jnp.zeros_like(token)

    lands = [lax.empty(b.shape, b.dtype) for b in blocks]
    return _split_start(name, body, blocks, lands, 3 * n)


def _chip_wait(name, send_sems, recv_sems, blocks, lands, after):
    n = len(blocks)

    def body(*refs):
        p_refs, land_refs = refs[:n], refs[n:2 * n]
        send_sems, recv_sems = refs[2 * n], refs[2 * n + 1]
        for i in range(n):
            for j, peer in enumerate(_chip_peers()):
                cp = pltpu.make_async_remote_copy(
                    src_ref=p_refs[i].at[j], dst_ref=land_refs[i].at[j], send_sem=send_sems.at[3 * i + j],
                    recv_sem=recv_sems.at[3 * i + j], device_id=peer, device_id_type=MESH)
                cp.wait_send()
                cp.wait_recv()

    return _split_wait(name, body, send_sems, recv_sems, blocks, lands, after)


def _pair_exchange(name, blocks):
    n = len(blocks)

    def body(*refs):
        g_refs, out_refs = refs[:n], refs[n:2 * n]
        send_sems, recv_sems = refs[2 * n:]
        x, y, c = lax.axis_index("x"), lax.axis_index("y"), lax.axis_index("c")
        copies = [pltpu.make_async_remote_copy(
            src_ref=g_refs[i].at[2 * k + 1 - c], dst_ref=out_refs[i].at[k], send_sem=send_sems.at[4 * i + k],
            recv_sem=recv_sems.at[4 * i + k], device_id=(x, y, 1 - c), device_id_type=MESH)
            for i in range(n) for k in range(4)]
        for cp in copies:
            cp.start()
        for cp in copies:
            cp.wait()

    outs = [jax.ShapeDtypeStruct((4,) + b.shape[1:], b.dtype) for b in blocks]
    return _comm_call(name, body, blocks, outs, (4 * n, 4 * n))


def _chip_exchange(name, blocks):
    n = len(blocks)

    def body(*refs):
        p_refs, out_refs = refs[:n], refs[n:2 * n]
        send_sems, recv_sems = refs[2 * n:]
        x, y, c = lax.axis_index("x"), lax.axis_index("y"), lax.axis_index("c")
        chips = [(1 - x, y), (x, 1 - y), (1 - x, 1 - y)]
        copies = [pltpu.make_async_remote_copy(
            src_ref=p_refs[i].at[j], dst_ref=out_refs[i].at[j], send_sem=send_sems.at[3 * i + j],
            recv_sem=recv_sems.at[3 * i + j], device_id=(*chip, c), device_id_type=MESH)
            for i in range(n) for j, chip in enumerate(chips)]
        for cp in copies:
            cp.start()
        for cp in copies:
            cp.wait()

    outs = [jax.ShapeDtypeStruct(b.shape, b.dtype) for b in blocks]
    return _comm_call(name, body, blocks, outs, (3 * n, 3 * n))


def _pair_sum(name, blocks, from_sibling, g_idx, r_idx):
    _, r, c_ = blocks.shape
    tr = _tile(r, 512, 16)

    def body(gi_ref, ri_ref, a_ref, b_ref, own_ref, send_ref):
        k = pl.program_id(1)
        s = a_ref[...] + b_ref[...]

        @pl.when(k == 0)
        def _():
            own_ref[...] = s

        @pl.when(k > 0)
        def _():
            send_ref[...] = s.astype(send_ref.dtype)

    return pl.pallas_call(
        body, name=name,
        grid_spec=pltpu.PrefetchScalarGridSpec(
            num_scalar_prefetch=2, grid=(r // tr, 4),
            in_specs=[pl.BlockSpec((None, tr, c_), lambda i, k, gi, ri: (gi[k], i, 0)),
                      pl.BlockSpec((None, tr, c_), lambda i, k, gi, ri: (ri[k], i, 0))],
            out_specs=[pl.BlockSpec((None, tr, c_), lambda i, k, gi, ri: (0, i, 0)),
                       pl.BlockSpec((None, tr, c_), lambda i, k, gi, ri: (jnp.maximum(k - 1, 0), i, 0))]),
        out_shape=[jax.ShapeDtypeStruct((1, r, c_), F32), jax.ShapeDtypeStruct((3, r, c_), PAYLOAD)],
        compiler_params=_params("parallel", "arbitrary"))(g_idx, r_idx, blocks, from_sibling)


def _adamw(w, g, m, v):
    m = ADAM_B1 * m + (1.0 - ADAM_B1) * g
    v = ADAM_B2 * v + (1.0 - ADAM_B2) * (g * g)
    m_hat = m / (1.0 - ADAM_B1 ** ADAM_STEP)
    v_hat = v / (1.0 - ADAM_B2 ** ADAM_STEP)
    delta = -ADAM_LR * (m_hat / (jnp.sqrt(v_hat) + ADAM_EPS) + ADAM_WD * w)
    return delta, m, v


def _sum_adamw(name, parts_f32, parts_lo, w, m, v):
    r, c_ = w.shape
    tr = _tile(r, 256, 16)
    k1 = parts_f32.shape[0]
    k2 = 0 if parts_lo is None else parts_lo.shape[0]

    def body(*refs):
        a_ref = refs[0]
        b_ref = refs[1] if k2 else None
        w_ref, m_ref, v_ref, g_ref, d_ref, nm_ref, nv_ref = refs[(2 if k2 else 1):]
        g = a_ref[0]
        for k in range(1, k1):
            g = g + a_ref[k]
        for k in range(k2):
            g = g + b_ref[k].astype(F32)
        g_ref[...] = g
        d_ref[...], nm_ref[...], nv_ref[...] = _adamw(w_ref[...], g, m_ref[...], v_ref[...])

    row = pl.BlockSpec((tr, c_), lambda i: (i, 0))
    ins = [parts_f32] + ([parts_lo] if k2 else []) + [w, m, v]
    in_specs = [pl.BlockSpec((k1, tr, c_), lambda i: (0, i, 0))]
    if k2:
        in_specs.append(pl.BlockSpec((k2, tr, c_), lambda i: (0, i, 0)))
    in_specs += [row, row, row]
    return pl.pallas_call(body, name=name, grid=(r // tr,), in_specs=in_specs, out_specs=[row] * 4,
                          out_shape=[jax.ShapeDtypeStruct((r, c_), F32)] * 4,
                          compiler_params=_params("parallel"))(*ins)


def _sum_adamw_layers(name, owns, from_chips, w, m, v):
    n_layers, r, c_ = w.shape
    tr = _tile(r, 256, 16)

    def body(*refs):
        a_refs, b_refs = refs[:n_layers], refs[n_layers:2 * n_layers]
        w_ref, m_ref, v_ref, g_ref, d_ref, nm_ref, nv_ref = refs[2 * n_layers:]
        layer = pl.program_id(0)
        g = None
        for li in range(n_layers):
            gl = a_refs[li][0]
            for k in range(b_refs[li].shape[0]):
                gl = gl + b_refs[li][k].astype(F32)
            g = gl if g is None else jnp.where(layer == li, gl, g)
        g_ref[...] = g
        d_ref[...], nm_ref[...], nv_ref[...] = _adamw(w_ref[...], g, m_ref[...], v_ref[...])

    row = pl.BlockSpec((None, tr, c_), lambda l, i: (l, i, 0))
    parts = [pl.BlockSpec((a.shape[0], tr, c_), lambda l, i: (0, i, 0)) for a in list(owns) + list(from_chips)]
    return pl.pallas_call(body, name=name, grid=(n_layers, r // tr), in_specs=parts + [row, row, row],
                          out_specs=[row] * 4, out_shape=[jax.ShapeDtypeStruct(w.shape, F32)] * 4,
                          compiler_params=_params("parallel", "parallel"))(*owns, *from_chips, w, m, v)


def _pack_rows(flat, n_rows, cols):
    pad = n_rows * cols - flat.shape[-1]
    flat = jnp.pad(flat, [(0, 0)] * (flat.ndim - 1) + [(0, pad)])
    return flat.reshape(flat.shape[:-1] + (n_rows, cols))


def _cols_join(blocks):
    return jnp.concatenate([blocks[d] for d in range(N_DEV)], axis=1)


def _cols_split(full):
    c = full.shape[1] // N_DEV
    return jnp.stack([full[:, d * c:(d + 1) * c] for d in range(N_DEV)])


def _rows_join(blocks):
    return blocks.reshape(N_DEV * blocks.shape[1], blocks.shape[2])


def _rows_split(full):
    return full.reshape(N_DEV, full.shape[0] // N_DEV, full.shape[1])


def _perm_xbc(a, ng):
    lead = a.shape[:-1]
    di, gn = ng * GW, ng * SSD_D_STATE
    xs = a[..., :di].reshape(lead + (ng, GW))
    bs = a[..., di:di + gn].reshape(lead + (ng, SSD_D_STATE))
    cs = a[..., di + gn:].reshape(lead + (ng, SSD_D_STATE))
    return jnp.concatenate([xs, bs, cs], axis=-1).reshape(lead + (ng * GC,))


def _unperm_xbc(a, ng):
    lead = a.shape[:-1]
    g = a.reshape(lead + (ng, GC))
    return jnp.concatenate([g[..., :GW].reshape(lead + (ng * GW,)),
                            g[..., GW:GW + SSD_D_STATE].reshape(lead + (ng * SSD_D_STATE,)),
                            g[..., GW + SSD_D_STATE:].reshape(lead + (ng * SSD_D_STATE,))], axis=-1)


def _heads_col(v, ng):
    return jnp.pad(v.reshape(ng, 1, SSD_HPG), ((0, 0), (0, 0), (0, LANES - SSD_HPG)))


def _heads_row(v, ng):
    return jnp.pad(v.reshape(ng, SSD_HPG, 1), ((0, 0), (0, 8 - SSD_HPG), (0, 0)))


MATRIX_ITEMS = ("w_in", "w_out", "up0", "down0", "w_qkv", "w_o", "up1", "down1")
VECTOR_ITEMS = ("conv_w", "b_qkv", "b_o")
ITEMS = MATRIX_ITEMS + VECTOR_ITEMS
GATHER_STAGES = (("w_in", "conv_w"), ("w_out", "up0", "down0"), ("w_qkv", "b_qkv", "w_o", "b_o", "up1", "down1"))


def _items(tree, prefix=""):
    g = lambda k: tree[prefix + k]
    return {"w_in": g("ssd_w_in")[0], "w_out": g("ssd_w_out")[0], "w_qkv": g("attn_w_qkv")[0].T,
            "w_o": g("attn_w_o")[0], "up0": g("mlp_w_up")[0], "up1": g("mlp_w_up")[1],
            "down0": g("mlp_w_down")[0], "down1": g("mlp_w_down")[1], "conv_w": g("ssd_conv_w")[0],
            "b_qkv": g("attn_b_qkv"), "b_o": g("attn_b_o")}


def _from_items(it):
    return {"ssd_w_in": it["w_in"][None], "ssd_w_out": it["w_out"][None], "attn_w_qkv": it["w_qkv"].T[None],
            "attn_w_o": it["w_o"][None], "mlp_w_up": jnp.stack([it["up0"], it["up1"]]),
            "mlp_w_down": jnp.stack([it["down0"], it["down1"]]), "ssd_conv_w": it["conv_w"][None],
            "attn_b_qkv": it["b_qkv"], "attn_b_o": it["b_o"]}


REPLICATED = ("ssd_conv_b", "ssd_dt_bias", "ssd_a_log", "ssd_d", "ssd_norm_w", "attn_sinks", "mix_pre_norm",
              "mix_post_norm", "ffn_pre_norm", "ffn_post_norm")
WEIGHTS = ("ssd_w_in", "ssd_conv_w", "ssd_conv_b", "ssd_dt_bias", "ssd_a_log", "ssd_d", "ssd_norm_w", "ssd_w_out",
           "attn_w_qkv", "attn_b_qkv", "attn_sinks", "attn_w_o", "attn_b_o", "mlp_w_up", "mlp_w_down",
           "mix_pre_norm", "mix_post_norm", "ffn_pre_norm", "ffn_post_norm")


def _forward_backward(x, target, rep, token, weights_of_stage, reduce_grads):
    t, d = x.shape
    ng = rep["ssd_norm_w"].shape[1] // GW
    di = ng * GW
    n_xbc = ng * GC
    nh = ng * SSD_HPG
    grads, blocks = {}, {}
    w_up, w_down = [None, None], [None, None]
    sinks_rep = jnp.repeat(rep["attn_sinks"].reshape(ATTN_N_KV, ATTN_REP, 1), ATTN_WINDOW, axis=2).reshape(
        ATTN_N_KV, 1, ATTN_REP * ATTN_WINDOW)
    conv_b = rep["ssd_conv_b"]
    gn = ng * SSD_D_STATE
    parts = ((0, di), (di, di), (2 * di, gn), (2 * di + gn, gn), (di + n_xbc, nh))
    bias_c, alog_c, dsk_c = (_heads_col(rep[k], ng) for k in ("ssd_dt_bias", "ssd_a_log", "ssd_d"))
    bias_r, alog_r = (_heads_row(rep[k], ng) for k in ("ssd_dt_bias", "ssd_a_log"))
    norm = {k: rep[k] for k in ("mix_pre_norm", "mix_post_norm", "ffn_pre_norm", "ffn_post_norm")}

    def nrow(name, i):
        return norm[name][i:i + 1]

    def mlp_fwd(i, u2):
        a, p = _mm(f"mlp{i}_up", [u2], [w_up[i]], "nn", tm=1024, tn=1024, out_dtypes=(F32, BF16),
                   epilogue=lambda acc: (acc, jnp.square(jnp.maximum(acc, 0.0))))
        f = _mm(f"mlp{i}_down", [p], [w_down[i]], "nn", tm=512, tn=1024)
        return a, p, f

    def mlp_bwd(i, df, u2, a, p):
        da = _mm(f"mlp{i}_dact", [df], [w_down[i]], "nt", tm=1024, tn=1024, out_dtypes=(BF16,),
                 tiles=(a,), epilogue=lambda acc, av: (acc * (2.0 * jnp.maximum(av, 0.0)),))
        blocks[f"down{i}"] = _rows_split(_mm(f"mlp{i}_dwdown", [p], [df], "tn", tm=512, tn=1024))
        blocks[f"up{i}"] = _mm(f"mlp{i}_dwup", [u2], [da], "tn", tm=1024, tn=da.shape[1] // N_DEV,
                               col_blocks=True)
        return _mm(f"mlp{i}_dx", [da], [w_up[i]], "nt", tm=512, tn=1024)

    u0 = _prenorm("l0_prenorm", x, nrow("mix_pre_norm", 0), token)
    got = weights_of_stage(0, u0)
    w_in = _cols_join(got["w_in"])
    conv_w = _cols_join(got["conv_w"])
    zx = _mm("ssd_in_proj", [u0], [w_in], "nn", tm=1024, tn=_tile(_round_up(w_in.shape[1], LANES), 1024))
    pre = _conv_fwd(zx, di, n_xbc, conv_w, conv_b)
    dt_raw = zx[:, di + n_xbc:di + n_xbc + nh].reshape(t, ng, SSD_HPG)
    dtc = jnp.pad(jnp.transpose(dt_raw, (1, 0, 2)), ((0, 0), (0, 0), (0, LANES - SSD_HPG)))
    dtr = jnp.pad(jnp.transpose(dt_raw, (1, 2, 0)), ((0, 0), (0, 8 - SSD_HPG), (0, 0)))
    ssd_args = (dtc, dtr, bias_c, alog_c, dsk_c, bias_r, alog_r)
    y, states = _ssd_fwd(pre, *ssd_args)
    yn = _gate_norm_fwd(y, zx, rep["ssd_norm_w"])
    got = weights_of_stage(1, yn)
    w_out = _rows_join(got["w_out"])
    w_up[0], w_down[0] = _cols_join(got["up0"]), _rows_join(got["down0"])
    mix0 = _mm("ssd_out_proj", [yn], [w_out], "nn", tm=1024, tn=1024)
    h1, u0f = _post_pre("l0_mid", x, mix0, nrow("mix_post_norm", 0), nrow("ffn_pre_norm", 0))
    a0, p0, f0 = mlp_fwd(0, u0f)
    h2, u1 = _post_pre("l1_in", h1, f0, nrow("ffn_post_norm", 0), nrow("mix_pre_norm", 1))
    got = weights_of_stage(2, u1)
    w_qkv_t = _rows_join(got["w_qkv"])
    w_o = _rows_join(got["w_o"])
    b_qkv_col = got["b_qkv"].reshape(-1, 1)
    b_o = _cols_join(got["b_o"])
    w_up[1], w_down[1] = _cols_join(got["up1"]), _rows_join(got["down1"])
    qkv_t = _mm("attn_qkv_proj", [w_qkv_t], [u1], "nt", tm=768, tn=1024, out_dtypes=(BF16,), cols=(b_qkv_col,),
                epilogue=lambda acc, b: (acc + b,))
    ao_t = _attn_fwd_t(qkv_t, sinks_rep)
    mix1 = _mm("attn_out_proj", [ao_t], [w_o], "tn", tm=1024, tn=1024, rows=(b_o,),
               epilogue=lambda acc, b: (acc + b,))
    h3, u1f = _post_pre("l1_mid", h2, mix1, nrow("mix_post_norm", 1), nrow("ffn_pre_norm", 1))
    a1, p1, f1 = mlp_fwd(1, u1f)
    dh, loss_row = _final_loss("loss", h3, f1, nrow("ffn_post_norm", 1), target)

    g_norm = {k: [None, None] for k in norm}
    df1, g_norm["ffn_post_norm"][1], _ = _norm_bwd("l1_ffn_post_bwd", dh, post=(f1, nrow("ffn_post_norm", 1)))
    du = mlp_bwd(1, df1, u1f, a1, p1)
    sent = reduce_grads("mlp1", {k: blocks[k] for k in ("up1", "down1")})
    dh, g_norm["ffn_pre_norm"][1], dmix1, g_norm["mix_post_norm"][1], db_o = _norm_bwd(
        "l1_mid_bwd", dh, pre=(du, h3, nrow("ffn_pre_norm", 1)), post=(mix1, nrow("mix_post_norm", 1)), after=sent)
    blocks["b_o"] = _cols_split(db_o)
    blocks["w_o"] = _rows_split(_mm("attn_dwo", [ao_t], [dmix1], "nn", tm=512, tn=1024))
    dao_t = _mm("attn_dout", [w_o], [dmix1], "nt", tm=1024, tn=1024, out_dtypes=(BF16,))
    dqkv_t, db_qkv, grads["attn_sinks"] = _attn_bwd_t(qkv_t, dao_t, sinks_rep)
    blocks["b_qkv"] = db_qkv.reshape(N_DEV, 1, -1)
    blocks["w_qkv"] = _rows_split(_mm("attn_dwqkv", [dqkv_t], [u1], "nn", tm=512, tn=1024))
    du = _mm("attn_dx", [dqkv_t], [w_qkv_t], "tn", tm=1024, tn=1024)
    sent = reduce_grads("attn", {k: blocks[k] for k in ("w_o", "w_qkv", "b_o", "b_qkv")})
    dh, g_norm["mix_pre_norm"][1], df0, g_norm["ffn_post_norm"][0], _ = _norm_bwd(
        "l1_in_bwd", dh, pre=(du, h2, nrow("mix_pre_norm", 1)), post=(f0, nrow("ffn_post_norm", 0)), after=sent)
    du = mlp_bwd(0, df0, u0f, a0, p0)
    sent = reduce_grads("mlp0", {k: blocks[k] for k in ("up0", "down0")})
    dh, g_norm["ffn_pre_norm"][0], dmix0, g_norm["mix_post_norm"][0], _ = _norm_bwd(
        "l0_mid_bwd", dh, pre=(du, h1, nrow("ffn_pre_norm", 0)), post=(mix0, nrow("mix_post_norm", 0)), after=sent)
    blocks["w_out"] = _rows_split(_mm("ssd_dwout", [yn], [dmix0], "tn", tm=512, tn=1024))
    dyn = _mm("ssd_dyn", [dmix0], [w_out], "nt", tm=1024, tn=1024)
    dy, dz, grads["ssd_norm_w"] = _gate_norm_bwd(dyn, y, zx, rep["ssd_norm_w"])
    dpx, dpb, dpc, ddt_g, dbias_g, dalog_g, dd_g = _ssd_bwd(dy, pre, states, *ssd_args)
    conv_out = [_conv_bwd(f"ssd_conv_bwd_{tag}", dp, zx, c0, conv_w[:, c0 - di:c0 - di + n])
                for tag, dp, (c0, n) in zip("xbc", (dpx, dpb, dpc), parts[1:4])]
    dconv_w = jnp.concatenate([o[1] for o in conv_out], axis=1)
    dconv_b = jnp.concatenate([o[2] for o in conv_out], axis=1)
    ddt = jnp.transpose(ddt_g[:, :, :SSD_HPG], (1, 0, 2)).reshape(t, nh)
    ddt = jnp.pad(ddt, ((0, 0), (0, LANES - nh))).astype(BF16)
    blocks["conv_w"] = _cols_split(dconv_w)
    grads["ssd_conv_b"] = dconv_b
    for name, val in (("ssd_dt_bias", dbias_g), ("ssd_a_log", dalog_g), ("ssd_d", dd_g)):
        grads[name] = val[:, 0, :SSD_HPG].reshape(1, nh)
    d_zx = [dz] + [o[0] for o in conv_out] + [ddt]
    dw_parts = [_mm(f"ssd_dw_{tag}", [u0], [d], "tn", tm=1024, tn=512) for tag, d in zip("zxbct", d_zx)]
    dw_parts[-1] = dw_parts[-1][:, :nh]
    blocks["w_in"] = _cols_split(jnp.concatenate(dw_parts, axis=1))
    sent = reduce_grads("ssd", {k: blocks[k] for k in ("w_in", "w_out", "conv_w")})
    w_parts = [w_in[:, c0:c0 + n] for c0, n in parts]
    w_parts[-1] = jnp.pad(w_parts[-1], ((0, 0), (0, LANES - nh)))
    du = _mm("ssd_dx", d_zx, w_parts, "nt", tm=256, tn=1024)
    grad_x, g_norm["mix_pre_norm"][0] = _norm_bwd("l0_in_bwd", dh, pre=(du, x, nrow("mix_pre_norm", 0)), after=sent)
    for k in norm:
        grads[k] = jnp.concatenate(g_norm[k], axis=0)
    return loss_row, grad_x, grads


def kernel(x, ssd_w_in, ssd_conv_w, ssd_conv_b, ssd_dt_bias, ssd_a_log, ssd_d, ssd_norm_w, ssd_w_out, attn_w_qkv, attn_b_qkv, attn_sinks, attn_w_o, attn_b_o, mlp_w_up, mlp_w_down, mix_pre_norm, mix_post_norm, ffn_pre_norm, ffn_post_norm, loss_target, m_ssd_w_in, m_ssd_conv_w, m_ssd_conv_b, m_ssd_dt_bias, m_ssd_a_log, m_ssd_d, m_ssd_norm_w, m_ssd_w_out, m_attn_w_qkv, m_attn_b_qkv, m_attn_sinks, m_attn_w_o, m_attn_b_o, m_mlp_w_up, m_mlp_w_down, m_mix_pre_norm, m_mix_post_norm, m_ffn_pre_norm, m_ffn_post_norm, v_ssd_w_in, v_ssd_conv_w, v_ssd_conv_b, v_ssd_dt_bias, v_ssd_a_log, v_ssd_d, v_ssd_norm_w, v_ssd_w_out, v_attn_w_qkv, v_attn_b_qkv, v_attn_sinks, v_attn_w_o, v_attn_b_o, v_mlp_w_up, v_mlp_w_down, v_mix_pre_norm, v_mix_post_norm, v_ffn_pre_norm, v_ffn_post_norm):
    given = dict(locals())
    w = {k: given[k] for k in WEIGHTS}
    mom_m = {k: given["m_" + k] for k in WEIGHTS}
    mom_v = {k: given["v_" + k] for k in WEIGHTS}
    w_it, m_it, v_it = _items(given), _items(given, "m_"), _items(given, "v_")

    order = [k for stage in GATHER_STAGES for k in stage]
    shards = [w_it[k].astype(PAYLOAD) if k in MATRIX_ITEMS else w_it[k] for k in order]
    g_send, g_recv, shards, lands, token = _gather_start("gather_start", shards)

    def weights_of_stage(s, after):
        first = sum(len(stage) for stage in GATHER_STAGES[:s])
        sl = slice(first, first + len(GATHER_STAGES[s]))
        srcs, got = _gather_wait(f"gather_wait{s}", g_send, g_recv, first, shards[sl], lands[sl], after)
        me = 4 * ix + 2 * iy + ic
        return {k: lax.dynamic_update_slice(land, src[None], (me,) + (0,) * src.ndim)
                for k, land, src in zip(GATHER_STAGES[s], got, srcs)}

    ix, iy, ic = lax.axis_index("x"), lax.axis_index("y"), lax.axis_index("c")
    chips = [(ix, iy), (1 - ix, iy), (ix, 1 - iy), (1 - ix, 1 - iy)]
    g_idx = jnp.stack([4 * cx + 2 * cy + ic for cx, cy in chips]).astype(jnp.int32)
    r_idx = jnp.stack([2 * cx + cy for cx, cy in chips]).astype(jnp.int32)
    in_flight = []

    def reduce_grads(tag, blocks):
        keys = list(blocks)
        from_sibling = _pair_exchange(f"rs_pair_exchange_{tag}", [blocks[k] for k in keys])
        sums = [_pair_sum(f"rs_pair_sum_{k}", blocks[k], fs, g_idx, r_idx) for k, fs in zip(keys, from_sibling)]
        started = _chip_start(f"rs_chip_start_{tag}", [s[1] for s in sums])
        in_flight.append((tag, keys, [s[0] for s in sums], started))
        return started[-1]

    rep = {k: w[k] for k in REPLICATED}
    loss_row, grad_x, grads = _forward_backward(x[0], loss_target[0], rep, token, weights_of_stage, reduce_grads)

    def pack_rep(tree, last):
        flat = jnp.concatenate([tree[k].reshape(-1) for k in REPLICATED] + [last])
        return _pack_rows(flat, _round_up(-(-flat.shape[0] // LANES), 8), LANES)

    own, landed = {}, {}

    def wait_group(group, after):
        tag, keys, sums, (c_send, c_recv, srcs, c_lands, _) = group
        _, from_chips = _chip_wait(f"rs_chip_wait_{tag}", c_send, c_recv, srcs, c_lands, after)
        own.update(zip(keys, sums))
        landed.update(zip(keys, from_chips))

    def adamw_item(k):
        return _sum_adamw(f"adamw_{k}", own[k], landed[k], w_it[k], m_it[k], v_it[k])

    def adamw_stack(name, keys):
        return _sum_adamw_layers(f"adamw_{name}", [own[k] for k in keys], [landed[k] for k in keys], given[name],
                                 given["m_" + name], given["v_" + name])

    for group in in_flight[:-1]:
        wait_group(group, grad_x)
    done = {"mlp_w_up": adamw_stack("mlp_w_up", ("up0", "up1")),
            "mlp_w_down": adamw_stack("mlp_w_down", ("down0", "down1")),
            "attn_w_qkv": [o.T[None] for o in adamw_item("w_qkv")],
            "attn_w_o": [o[None] for o in adamw_item("w_o")],
            "attn_b_qkv": adamw_item("b_qkv"), "attn_b_o": adamw_item("b_o")}
    partials, = _all_gather("gather_small_grads", [pack_rep(grads, loss_row[0, :1])], done["mlp_w_down"][1])
    wait_group(in_flight[-1], partials)
    for name, k in (("ssd_w_in", "w_in"), ("ssd_w_out", "w_out"), ("ssd_conv_w", "conv_w")):
        done[name] = [o[None] for o in adamw_item(k)]
    zero = jnp.zeros((1,), F32)
    rep_out = _sum_adamw("adamw_replicated", partials, None, pack_rep(w, zero), pack_rep(mom_m, zero),
                         pack_rep(mom_v, zero))

    kinds = []
    for kind, r_arr in enumerate(rep_out):
        tree = {name: outs4[kind] for name, outs4 in done.items()}
        flat, off = r_arr.reshape(-1), 0
        for k in REPLICATED:
            tree[k] = flat[off:off + w[k].size].reshape(w[k].shape)
            off += w[k].size
        kinds.append(tree)
    loss = rep_out[0].reshape(-1)[off]
    outs = [loss, grad_x[None]]
    for tree in kinds:
        outs += [tree[k] for k in WEIGHTS]
    return tuple(outs)
```

```python
import functools

import jax
import jax.numpy as jnp
from jax import lax
from jax.experimental import pallas as pl
from jax.experimental.pallas import tpu as pltpu

F32 = jnp.float32
BF16 = jnp.bfloat16
PAYLOAD = jnp.bfloat16
HIGHEST = lax.Precision.HIGHEST
MESH = pl.DeviceIdType.MESH

NORM_EPS = 1e-6
SSD_HEAD_DIM = 64
SSD_N_GROUPS = 8
SSD_HPG = 4
SSD_D_STATE = 128
SSD_CONV_WIDTH = 4
SSD_CHUNK = 128
ATTN_HEAD_DIM = 64
ATTN_N_KV = 4
ATTN_REP = 4
ATTN_WINDOW = 128
ADAM_LR = 0.001
ADAM_B1 = 0.9
ADAM_B2 = 0.999
ADAM_EPS = 1e-08
ADAM_WD = 0.01
ADAM_STEP = 10

N_DEV = 8
LANES = 128
PACK_COLS = 1024
V7X_VMEM_LIMIT = 56 * 1024 * 1024

GW = SSD_HPG * SSD_HEAD_DIM
GC = GW + 2 * SSD_D_STATE


def _params(*sem):
    return pltpu.CompilerParams(dimension_semantics=sem, vmem_limit_bytes=V7X_VMEM_LIMIT)


def _tile(n, pref, mult=LANES):
    best = None
    t = mult
    while t <= min(n, pref):
        if n % t == 0:
            best = t
        t += mult
    return best if best is not None else n


def _round_up(n, m):
    return (n + m - 1) // m * m


def _acc(ref, val, first):
    @pl.when(first)
    def _():
        ref[...] = val

    @pl.when(jnp.logical_not(first))
    def _():
        ref[...] += val


def _dot(a, b):
    return lax.dot_general(a, b, (((1,), (0,)), ((), ())), preferred_element_type=F32)


def _dot_nt(a, b):
    return lax.dot_general(a, b, (((1,), (1,)), ((), ())), preferred_element_type=F32)


def _dot_tn(a, b):
    return lax.dot_general(a, b, (((0,), (0,)), ((), ())), preferred_element_type=F32)


def _dot_f32(a, b):
    return lax.dot_general(a, b, (((1,), (0,)), ((), ())), preferred_element_type=F32, precision=HIGHEST)


_DOTS = {"nn": _dot, "nt": _dot_nt, "tn": _dot_tn}


def _sigmoid(x):
    return 1.0 / (1.0 + jnp.exp(-x))


def _softplus(x):
    return jnp.maximum(x, 0.0) + jnp.log1p(jnp.exp(-jnp.abs(x)))


def _silu_grad(x, s):
    return s * (1.0 + x * (1.0 - s))


def _mm(name, a_list, b_list, mode, *, tm, tn, out_dtypes=(F32,), epilogue=None, tiles=(), rows=(), cols=(),
        col_blocks=False, n_use=None):
    npair = len(a_list)
    if mode == "tn":
        m = a_list[0].shape[1]
    else:
        m = a_list[0].shape[0]
    n = n_use if n_use is not None else (b_list[0].shape[0] if mode == "nt" else b_list[0].shape[1])
    tm = _tile(m, tm, LANES if mode == "tn" else 8)
    tn = _tile(n, tn)
    assert m % tm == 0 and n % tn == 0, (name, m, n, tm, tn)
    dot = _DOTS[mode]

    def body(*refs):
        a_refs = refs[:npair]
        b_refs = refs[npair:2 * npair]
        n_extra = len(tiles) + len(rows) + len(cols)
        e_refs = refs[2 * npair:2 * npair + n_extra]
        o_refs = refs[2 * npair + n_extra:]
        acc = None
        for ar, br in zip(a_refs, b_refs):
            d = dot(ar[...], br[...])
            acc = d if acc is None else acc + d
        outs = epilogue(acc, *[e[...] for e in e_refs]) if epilogue is not None else (acc,)
        for o, v in zip(o_refs, outs):
            o[...] = v.astype(o.dtype)

    in_specs = []
    for a in a_list:
        if mode == "tn":
            in_specs.append(pl.BlockSpec((a.shape[0], tm), lambda i, j: (0, i)))
        else:
            in_specs.append(pl.BlockSpec((tm, a.shape[1]), lambda i, j: (i, 0)))
    for b in b_list:
        if mode == "nt":
            in_specs.append(pl.BlockSpec((tn, b.shape[1]), lambda i, j: (j, 0)))
        else:
            in_specs.append(pl.BlockSpec((b.shape[0], tn), lambda i, j: (0, j)))
    in_specs += [pl.BlockSpec((tm, tn), lambda i, j: (i, j)) for _ in tiles]
    in_specs += [pl.BlockSpec((1, tn), lambda i, j: (0, j)) for _ in rows]
    in_specs += [pl.BlockSpec((tm, 1), lambda i, j: (i, 0)) for _ in cols]
    outs = pl.pallas_call(
        body,
        name=name,
        grid=(m // tm, n // tn),
        in_specs=in_specs,
        out_specs=[pl.BlockSpec((None, tm, tn), lambda i, j: (j, i, 0)) if col_blocks else
                   pl.BlockSpec((tm, tn), lambda i, j: (i, j)) for _ in out_dtypes],
        out_shape=[jax.ShapeDtypeStruct((n // tn, m, tn) if col_blocks else (m, n), dt) for dt in out_dtypes],
        compiler_params=_params("parallel", "parallel"),
    )(*a_list, *b_list, *tiles, *rows, *cols)
    return outs[0] if len(out_dtypes) == 1 else outs


def _rms(x, w):
    r = lax.rsqrt(jnp.mean(x * x, axis=-1, keepdims=True) + NORM_EPS)
    return x * r * w


def _rms_bwd(x, w, dy):
    r = lax.rsqrt(jnp.mean(x * x, axis=-1, keepdims=True) + NORM_EPS)
    xh = x * r
    g = dy * w
    dx = r * (g - xh * jnp.mean(g * xh, axis=-1, keepdims=True))
    return dx, dy * xh


def _row_specs(tr, d):
    return pl.BlockSpec((tr, d), lambda i: (i, 0)), pl.BlockSpec((1, d), lambda i: (0, 0))


def _prenorm(name, h, w, after):
    t, d = h.shape
    tr = _tile(t, 512, 8)
    row, vec = _row_specs(tr, d)

    def body(h_ref, w_ref, after_ref, u_ref):
        u_ref[...] = _rms(h_ref[...], w_ref[...]).astype(BF16)

    return pl.pallas_call(body, name=name, grid=(t // tr,),
                          in_specs=[row, vec, pl.BlockSpec((8, LANES), lambda i: (0, 0))], out_specs=row,
                          out_shape=jax.ShapeDtypeStruct((t, d), BF16), compiler_params=_params("parallel"))(
                              h, w, after)


def _post_pre(name, h, m, w_post, w_pre):
    t, d = h.shape
    tr = _tile(t, 512, 8)
    row, vec = _row_specs(tr, d)

    def body(h_ref, m_ref, wq_ref, wp_ref, hn_ref, u_ref):
        hn = h_ref[...] + _rms(m_ref[...], wq_ref[...])
        hn_ref[...] = hn
        u_ref[...] = _rms(hn, wp_ref[...]).astype(BF16)

    return pl.pallas_call(body, name=name, grid=(t // tr,), in_specs=[row, row, vec, vec], out_specs=[row, row],
                          out_shape=[jax.ShapeDtypeStruct((t, d), F32), jax.ShapeDtypeStruct((t, d), BF16)],
                          compiler_params=_params("parallel"))(h, m, w_post, w_pre)


def _final_loss(name, h, m, w_post, target):
    t, d = h.shape
    tr = _tile(t, 512, 8)
    row, vec = _row_specs(tr, d)

    def body(h_ref, m_ref, wq_ref, t_ref, dh_ref, loss_ref):
        err = h_ref[...] + _rms(m_ref[...], wq_ref[...]) - t_ref[...]
        dh_ref[...] = err * (1.0 / d)
        part = 0.5 * jnp.sum(jnp.mean(err * err, axis=-1, keepdims=True), axis=0, keepdims=True)
        _acc(loss_ref, jnp.broadcast_to(part, (1, LANES)), pl.program_id(0) == 0)

    return pl.pallas_call(body, name=name, grid=(t // tr,), in_specs=[row, row, vec, row],
                          out_specs=[row, pl.BlockSpec((1, LANES), lambda i: (0, 0))],
                          out_shape=[jax.ShapeDtypeStruct((t, d), F32), jax.ShapeDtypeStruct((1, LANES), F32)],
                          compiler_params=_params("arbitrary"))(h, m, w_post, target)


def _norm_bwd(name, dh, pre=None, post=None, after=None):
    t, d = dh.shape
    tr = _tile(t, 256, 8)
    row, vec = _row_specs(tr, d)
    has_pre, has_post = pre is not None, post is not None

    def body(*refs):
        it = iter(refs)
        dh_ref = next(it)
        if has_pre:
            du_ref, x_ref, wp_ref = next(it), next(it), next(it)
        if has_post:
            m_ref, wq_ref = next(it), next(it)
        if after is not None:
            next(it)
        first = pl.program_id(0) == 0
        dh_v = dh_ref[...]
        if has_pre:
            dhn_ref, dwp_ref = next(it), next(it)
            dx, dwr = _rms_bwd(x_ref[...], wp_ref[...], du_ref[...])
            dh_v = dh_v + dx
            dhn_ref[...] = dh_v
            _acc(dwp_ref, jnp.sum(dwr, axis=0, keepdims=True), first)
        if has_post:
            dm_ref, dwq_ref, dms_ref = next(it), next(it), next(it)
            dm, dwr = _rms_bwd(m_ref[...], wq_ref[...], dh_v)
            dm_ref[...] = dm.astype(BF16)
            _acc(dwq_ref, jnp.sum(dwr, axis=0, keepdims=True), first)
            _acc(dms_ref, jnp.sum(dm, axis=0, keepdims=True), first)

    ins, in_specs, out_specs, out_shape = [dh], [row], [], []
    if has_pre:
        ins += list(pre)
        in_specs += [row, row, vec]
        out_specs += [row, vec]
        out_shape += [jax.ShapeDtypeStruct((t, d), F32), jax.ShapeDtypeStruct((1, d), F32)]
    if has_post:
        ins += list(post)
        in_specs += [row, vec]
        out_specs += [row, vec, vec]
        out_shape += [jax.ShapeDtypeStruct((t, d), BF16), jax.ShapeDtypeStruct((1, d), F32),
                      jax.ShapeDtypeStruct((1, d), F32)]
    if after is not None:
        ins.append(after)
        in_specs.append(pl.BlockSpec((8, LANES), lambda i: (0, 0)))
    return pl.pallas_call(body, name=name, grid=(t // tr,), in_specs=in_specs, out_specs=out_specs,
                          out_shape=out_shape, compiler_params=_params("arbitrary"))(*ins)


HALO = 8


def _conv_fwd(zx, col0, n_ch, conv_w, conv_b):
    t = zx.shape[0]
    tc = _tile(n_ch, 512)
    tt = _tile(t, 512, 8)
    cb0 = col0 // tc
    assert col0 % tc == 0
    kw = SSD_CONV_WIDTH

    def body(x_ref, p_ref, w_ref, b_ref, o_ref, xe_ref):
        i = pl.program_id(1)
        cur = x_ref[...]
        xe_ref[0:HALO, :] = jnp.where(i > 0, p_ref[...], 0.0)
        xe_ref[HALO:HALO + tt, :] = cur
        w = w_ref[...]
        acc = b_ref[...] + w[kw - 1:kw, :] * cur
        for k in range(kw - 1):
            acc = acc + w[k:k + 1, :] * xe_ref[pl.ds(HALO - (kw - 1) + k, tt), :]
        o_ref[...] = acc

    return pl.pallas_call(
        body, name="ssd_conv_fwd", grid=(n_ch // tc, t // tt),
        in_specs=[pl.BlockSpec((tt, tc), lambda j, i: (i, cb0 + j)),
                  pl.BlockSpec((HALO, tc), lambda j, i: (jnp.maximum(i * (tt // HALO) - 1, 0), cb0 + j)),
                  pl.BlockSpec((kw, tc), lambda j, i: (0, j)),
                  pl.BlockSpec((1, tc), lambda j, i: (0, j))],
        out_specs=pl.BlockSpec((tt, tc), lambda j, i: (i, j)),
        out_shape=jax.ShapeDtypeStruct((t, n_ch), F32),
        scratch_shapes=[pltpu.VMEM((tt + HALO, tc), F32)],
        compiler_params=_params("parallel", "parallel"))(zx, zx, conv_w, conv_b)


def _conv_bwd(name, dpre, zx, col0, conv_w):
    t, n_ch = dpre.shape
    tc = _tile(n_ch, 512)
    tt = _tile(t, 512, 8)
    cb0 = col0 // tc
    kw = SSD_CONV_WIDTH
    nt = t // tt

    def body(d_ref, dn_ref, x_ref, p_ref, w_ref, dx_ref, dw_ref, db_ref, de_ref, xe_ref):
        i = pl.program_id(1)
        d = d_ref[...]
        de_ref[0:tt, :] = d
        de_ref[tt:tt + HALO, :] = jnp.where(i < nt - 1, dn_ref[...], 0.0)
        xe_ref[0:HALO, :] = jnp.where(i > 0, p_ref[...], 0.0)
        xe_ref[HALO:HALO + tt, :] = x_ref[...]
        w = w_ref[...]
        dx = w[kw - 1:kw, :] * d
        for k in range(kw - 1):
            dx = dx + w[k:k + 1, :] * de_ref[pl.ds(kw - 1 - k, tt), :]
        dx_ref[...] = dx.astype(BF16)
        first = i == 0
        for k in range(kw):
            xs = xe_ref[pl.ds(HALO - (kw - 1) + k, tt), :]
            val = jnp.sum(d * xs, axis=0, keepdims=True)

            @pl.when(first)
            def _():
                dw_ref[k:k + 1, :] = val

            @pl.when(jnp.logical_not(first))
            def _():
                dw_ref[k:k + 1, :] += val
        _acc(db_ref, jnp.sum(d, axis=0, keepdims=True), first)

    return pl.pallas_call(
        body, name=name, grid=(n_ch // tc, nt),
        in_specs=[pl.BlockSpec((tt, tc), lambda j, i: (i, j)),
                  pl.BlockSpec((HALO, tc), lambda j, i: (jnp.minimum((i + 1) * (tt // HALO), t // HALO - 1), j)),
                  pl.BlockSpec((tt, tc), lambda j, i: (i, cb0 + j)),
                  pl.BlockSpec((HALO, tc), lambda j, i: (jnp.maximum(i * (tt // HALO) - 1, 0), cb0 + j)),
                  pl.BlockSpec((kw, tc), lambda j, i: (0, j))],
        out_specs=[pl.BlockSpec((tt, tc), lambda j, i: (i, j)),
                   pl.BlockSpec((kw, tc), lambda j, i: (0, j)),
                   pl.BlockSpec((1, tc), lambda j, i: (0, j))],
        out_shape=[jax.ShapeDtypeStruct((t, n_ch), BF16), jax.ShapeDtypeStruct((kw, n_ch), F32),
                   jax.ShapeDtypeStruct((1, n_ch), F32)],
        scratch_shapes=[pltpu.VMEM((tt + HALO, tc), F32), pltpu.VMEM((tt + HALO, tc), F32)],
        compiler_params=_params("parallel", "arbitrary"))(dpre, dpre, zx, zx, conv_w)


def _head_of_lane(shape, width):
    return lax.broadcasted_iota(jnp.int32, shape, len(shape) - 1) // width


def _expand(v, n_rows):
    head = _head_of_lane((n_rows, GW), SSD_HEAD_DIM)
    out = jnp.zeros((n_rows, GW), F32)
    for j in range(SSD_HPG):
        out = jnp.where(head == j, v[:, j:j + 1], out)
    return out


def _contract(v, n_rows):
    head = _head_of_lane((n_rows, GW), SSD_HEAD_DIM)
    lane = lax.broadcasted_iota(jnp.int32, (n_rows, LANES), 1)
    out = jnp.zeros((n_rows, LANES), F32)
    for j in range(SSD_HPG):
        s = jnp.sum(jnp.where(head == j, v, 0.0), axis=1, keepdims=True)
        out = jnp.where(lane == j, s, out)
    return out


def _ssd_common(pre, dtc, bias_c, alog_c, dtr, bias_r, alog_r):
    q = SSD_CHUNK
    sg = _sigmoid(pre)
    act = pre * sg
    xa = act[:, :GW]
    bm = act[:, GW:GW + SSD_D_STATE].astype(BF16)
    cm = act[:, GW + SSD_D_STATE:].astype(BF16)
    row = lax.broadcasted_iota(jnp.int32, (q, q), 0)
    col = lax.broadcasted_iota(jnp.int32, (q, q), 1)
    tril = col <= row
    dt = _softplus(dtc + bias_c)
    a_c = -jnp.exp(alog_c)
    cum = _dot_f32(tril.astype(F32), dt * a_c)
    dt_r = _softplus(dtr + bias_r)
    cum_r = _dot_f32(dt_r * (-jnp.exp(alog_r)), (row <= col).astype(F32))
    g = _dot_nt(cm, bm)
    dt_x = _expand(dt, q)
    xdt = xa * dt_x
    cl = cum[q - 1:q, :]
    e_c = jnp.exp(cl - cum)
    lam_c = jnp.exp(cum)
    return dict(sg=sg, xa=xa, bm=bm, cm=cm, tril=tril, row=row, col=col, dt=dt, a_c=a_c, cum=cum, cum_r=cum_r,
                g=g, dt_x=dt_x, xdt=xdt, cl=cl, e_c=e_c, lam_c=lam_c)


def _ssd_specs(nc, rev, ng):
    q = SSD_CHUNK
    b_off = ng * GW // SSD_D_STATE

    def ch(c):
        return nc - 1 - c if rev else c

    chunk_grp = [pl.BlockSpec((q, GW), lambda g, c: (ch(c), g)),
                 pl.BlockSpec((q, SSD_D_STATE), lambda g, c: (ch(c), b_off + g)),
                 pl.BlockSpec((q, SSD_D_STATE), lambda g, c: (ch(c), b_off + ng + g))]
    col_form = pl.BlockSpec((None, q, LANES), lambda g, c: (g, ch(c), 0))
    row_form = pl.BlockSpec((None, 8, q), lambda g, c: (g, 0, ch(c)))
    col_par = pl.BlockSpec((None, 1, LANES), lambda g, c: (g, 0, 0))
    row_par = pl.BlockSpec((None, 8, 1), lambda g, c: (g, 0, 0))
    y_spec = pl.BlockSpec((q, GW), lambda g, c: (ch(c), g))
    st_spec = pl.BlockSpec((None, None, GW, SSD_D_STATE), lambda g, c: (g, ch(c), 0, 0))
    bc_spec = pl.BlockSpec((q, SSD_D_STATE), lambda g, c: (ch(c), g))
    return chunk_grp, col_form, row_form, col_par, row_par, y_spec, st_spec, bc_spec


def _ssd_fwd(pre, dtc, dtr, bias_c, alog_c, dsk_c, bias_r, alog_r):
    t = pre.shape[0]
    ng = pre.shape[1] // GC
    q = SSD_CHUNK
    nc = t // q
    chunk_grp, col_form, row_form, col_par, row_par, y_spec, st_spec, _ = _ssd_specs(nc, False, ng)

    def body(px_ref, pb_ref, pc_ref, dtc_ref, dtr_ref, bc_ref, ac_ref, dk_ref, br_ref, ar_ref, y_ref, sp_ref, st_ref):
        @pl.when(pl.program_id(1) == 0)
        def _():
            st_ref[...] = jnp.zeros_like(st_ref)

        pre_v = jnp.concatenate([px_ref[...], pb_ref[...], pc_ref[...]], axis=1)
        v = _ssd_common(pre_v, dtc_ref[...], bc_ref[...], ac_ref[...], dtr_ref[...], br_ref[...], ar_ref[...])
        s0 = st_ref[...]
        sp_ref[...] = s0
        r = _dot_nt(v["cm"], s0.astype(BF16))
        y = _expand(v["lam_c"], q) * r + _expand(dk_ref[...], 1) * v["xa"]
        head = _head_of_lane((q, GW), SSD_HEAD_DIM)
        for j in range(SSD_HPG):
            diff = v["cum"][:, j:j + 1] - v["cum_r"][j:j + 1, :]
            w = (v["g"] * jnp.exp(jnp.where(v["tril"], diff, -jnp.inf))).astype(BF16)
            y = y + _dot(w, jnp.where(head == j, v["xdt"], 0.0).astype(BF16))
        y_ref[...] = y
        ds = _dot_tn((v["xdt"] * _expand(v["e_c"], q)).astype(BF16), v["bm"])
        for j in range(SSD_HPG):
            rows = slice(j * SSD_HEAD_DIM, (j + 1) * SSD_HEAD_DIM)
            st_ref[rows, :] = s0[rows, :] * jnp.exp(v["cum_r"][j:j + 1, q - 1:q]) + ds[rows, :]

    return pl.pallas_call(
        body, name="ssd_scan_fwd", grid=(ng, nc),
        in_specs=chunk_grp + [col_form, row_form, col_par, col_par, col_par, row_par, row_par],
        out_specs=[y_spec, st_spec],
        out_shape=[jax.ShapeDtypeStruct((t, ng * GW), F32), jax.ShapeDtypeStruct((ng, nc, GW, SSD_D_STATE), F32)],
        scratch_shapes=[pltpu.VMEM((GW, SSD_D_STATE), F32)],
        compiler_params=_params("parallel", "arbitrary"))(pre, pre, pre, dtc, dtr, bias_c, alog_c, dsk_c, bias_r,
                                                           alog_r)


def _ssd_bwd(dy, pre, states, dtc, dtr, bias_c, alog_c, dsk_c, bias_r, alog_r):
    t = pre.shape[0]
    ng = pre.shape[1] // GC
    q = SSD_CHUNK
    nc = t // q
    chunk_grp, col_form, row_form, col_par, row_par, y_spec, st_spec, bc_spec = _ssd_specs(nc, True, ng)

    def body(dy_ref, px_ref, pb_ref, pc_ref, sp_ref, dtc_ref, dtr_ref, bc_ref, ac_ref, dk_ref, br_ref, ar_ref,
             dpx_ref, dpb_ref, dpc_ref, ddt_ref, dbias_ref, dalog_ref, dd_ref, ds_ref):
        first = pl.program_id(1) == 0

        @pl.when(first)
        def _():
            ds_ref[...] = jnp.zeros_like(ds_ref)

        pre_v = jnp.concatenate([px_ref[...], pb_ref[...], pc_ref[...]], axis=1)
        v = _ssd_common(pre_v, dtc_ref[...], bc_ref[...], ac_ref[...], dtr_ref[...], br_ref[...], ar_ref[...])
        xa, bm, cm, xdt, cum, cum_r = v["xa"], v["bm"], v["cm"], v["xdt"], v["cum"], v["cum_r"]
        xdt_b = xdt.astype(BF16)
        dy_v = dy_ref[...]
        s0 = sp_ref[...]
        ds1 = ds_ref[...]
        s0b, ds1b = s0.astype(BF16), ds1.astype(BF16)
        head = _head_of_lane((q, GW), SSD_HEAD_DIM)
        lane = lax.broadcasted_iota(jnp.int32, (q, LANES), 1)
        lane1 = lax.broadcasted_iota(jnp.int32, (1, LANES), 1)
        lam_x = _expand(v["lam_c"], q)
        e_x = _expand(v["e_c"], q)

        dxa = _expand(dk_ref[...], 1) * dy_v
        dd = _contract(jnp.sum(dy_v * xa, axis=0, keepdims=True), 1)
        r = _dot_nt(cm, s0b)
        dcum = _contract(dy_v * r * lam_x, q)
        drb = (lam_x * dy_v).astype(BF16)
        dc = _dot(drb, s0b)
        ds0 = _dot_tn(drb, cm)
        extra = jnp.zeros((1, LANES), F32)
        for j in range(SSD_HPG):
            rows = slice(j * SSD_HEAD_DIM, (j + 1) * SSD_HEAD_DIM)
            lam_last = jnp.exp(cum_r[j:j + 1, q - 1:q])
            ds_ref[rows, :] = ds0[rows, :] + lam_last * ds1[rows, :]
            tot = jnp.sum(jnp.sum(ds1[rows, :] * s0[rows, :], axis=1, keepdims=True), axis=0, keepdims=True)
            extra = jnp.where(lane1 == j, lam_last * tot, extra)
        dv = _dot_nt(bm, ds1b)
        db = _dot((xdt * e_x).astype(BF16), ds1b)
        dxdt = e_x * dv
        dee = _contract(dv * xdt, q) * v["e_c"]
        dcum = dcum - dee
        extra = extra + jnp.sum(dee, axis=0, keepdims=True)
        dg = jnp.zeros((q, q), F32)
        for j in range(SSD_HPG):
            diff = cum[:, j:j + 1] - cum_r[j:j + 1, :]
            el = jnp.exp(jnp.where(v["tril"], diff, -jnp.inf))
            gl = v["g"] * el
            dym = jnp.where(head == j, dy_v, 0.0).astype(BF16)
            dwm = _dot_nt(dym, xdt_b)
            dxdt = dxdt + _dot_tn(gl.astype(BF16), dym)
            z = dwm * gl
            rk = jnp.sum(z, axis=1, keepdims=True) - jnp.sum(z.T, axis=1, keepdims=True)
            dcum = jnp.where(lane == j, dcum + rk, dcum)
            dg = dg + dwm * el
        dgb = dg.astype(BF16)
        dc = dc + _dot(dgb, bm)
        db = db + _dot_tn(dgb, cm)
        da = _dot_f32((v["row"] <= v["col"]).astype(F32), dcum) + extra
        ddt = _contract(dxdt * xa, q) + v["a_c"] * da
        dalog = jnp.sum(v["dt"] * da, axis=0, keepdims=True) * v["a_c"]
        dxa = dxa + v["dt_x"] * dxdt
        ddt_raw = jnp.where(lane < SSD_HPG, ddt * _sigmoid(dtc_ref[...] + bc_ref[...]), 0.0)
        sgrad = _silu_grad(pre_v, v["sg"])
        dpx_ref[...] = dxa * sgrad[:, :GW]
        dpb_ref[...] = db * sgrad[:, GW:GW + SSD_D_STATE]
        dpc_ref[...] = dc * sgrad[:, GW + SSD_D_STATE:]
        ddt_ref[...] = ddt_raw
        _acc(dbias_ref, jnp.sum(ddt_raw, axis=0, keepdims=True), first)
        _acc(dalog_ref, jnp.where(lane1 < SSD_HPG, dalog, 0.0), first)
        _acc(dd_ref, dd, first)

    return pl.pallas_call(
        body, name="ssd_scan_bwd", grid=(ng, nc),
        in_specs=[y_spec] + chunk_grp + [st_spec, col_form, row_form, col_par, col_par, col_par, row_par, row_par],
        out_specs=[y_spec, bc_spec, bc_spec, col_form, col_par, col_par, col_par],
        out_shape=[jax.ShapeDtypeStruct((t, ng * GW), F32), jax.ShapeDtypeStruct((t, ng * SSD_D_STATE), F32),
                   jax.ShapeDtypeStruct((t, ng * SSD_D_STATE), F32), jax.ShapeDtypeStruct((ng, t, LANES), F32),
                   jax.ShapeDtypeStruct((ng, 1, LANES), F32), jax.ShapeDtypeStruct((ng, 1, LANES), F32),
                   jax.ShapeDtypeStruct((ng, 1, LANES), F32)],
        scratch_shapes=[pltpu.VMEM((GW, SSD_D_STATE), F32)],
        compiler_params=_params("parallel", "arbitrary"))(dy, pre, pre, pre, states, dtc, dtr, bias_c, alog_c, dsk_c,
                                                           bias_r, alog_r)


def _gate_norm_fwd(y, zx, norm_w):
    t, di = y.shape
    tr = _tile(t, 256, 8)
    ng = di // GW

    def body(y_ref, z_ref, w_ref, o_ref):
        z = z_ref[...]
        gate = y_ref[...] * (z * _sigmoid(z))
        w = w_ref[...]
        for g in range(ng):
            cols = slice(g * GW, (g + 1) * GW)
            gs = gate[:, cols]
            r = lax.rsqrt(jnp.mean(gs * gs, axis=-1, keepdims=True) + NORM_EPS)
            o_ref[:, cols] = (gs * r * w[:, cols]).astype(BF16)

    row = pl.BlockSpec((tr, di), lambda i: (i, 0))
    return pl.pallas_call(body, name="ssd_gate_norm_fwd", grid=(t // tr,),
                          in_specs=[row, row, pl.BlockSpec((1, di), lambda i: (0, 0))], out_specs=row,
                          out_shape=jax.ShapeDtypeStruct((t, di), BF16), compiler_params=_params("parallel"))(
                              y, zx, norm_w)


def _gate_norm_bwd(dyn, y, zx, norm_w):
    t, di = y.shape
    tr = _tile(t, 256, 8)
    ng = di // GW

    def body(d_ref, y_ref, z_ref, w_ref, dy_ref, dz_ref, dw_ref):
        z = z_ref[...]
        yv = y_ref[...]
        sg = _sigmoid(z)
        sz = z * sg
        gate = yv * sz
        w = w_ref[...]
        d = d_ref[...]
        dsz = _silu_grad(z, sg)
        dws = []
        for g in range(ng):
            cols = slice(g * GW, (g + 1) * GW)
            dg, dwr = _rms_bwd(gate[:, cols], w[:, cols], d[:, cols])
            dy_ref[:, cols] = dg * sz[:, cols]
            dz_ref[:, cols] = (dg * yv[:, cols] * dsz[:, cols]).astype(BF16)
            dws.append(jnp.sum(dwr, axis=0, keepdims=True))
        first = pl.program_id(0) == 0
        for g in range(ng):
            cols = slice(g * GW, (g + 1) * GW)

            @pl.when(first)
            def _():
                dw_ref[:, cols] = dws[g]

            @pl.when(jnp.logical_not(first))
            def _():
                dw_ref[:, cols] += dws[g]

    row = pl.BlockSpec((tr, di), lambda i: (i, 0))
    vec = pl.BlockSpec((1, di), lambda i: (0, 0))
    return pl.pallas_call(body, name="ssd_gate_norm_bwd", grid=(t // tr,), in_specs=[row, row, row, vec],
                          out_specs=[row, row, vec],
                          out_shape=[jax.ShapeDtypeStruct((t, di), F32), jax.ShapeDtypeStruct((t, di), BF16),
                                     jax.ShapeDtypeStruct((1, di), F32)],
                          compiler_params=_params("arbitrary"))(dyn, y, zx, norm_w)


def _attn_mask(n):
    w = ATTN_WINDOW
    qpos = lax.broadcasted_iota(jnp.int32, (w, 2 * w), 0) + w
    kpos = lax.broadcasted_iota(jnp.int32, (w, 2 * w), 1)
    rel = qpos - kpos
    return (rel >= 0) & (rel < w) & jnp.logical_not((n == 0) & (kpos < w))


def _attn_probs(qh, kbh, mask, sink):
    s = _dot_nt(qh, kbh) * (ATTN_HEAD_DIM ** -0.5)
    s = jnp.where(mask, s, -jnp.inf)
    m = jnp.maximum(jnp.max(s, axis=-1, keepdims=True), sink)
    e = jnp.exp(s - m)
    es = jnp.exp(sink - m)
    inv = 1.0 / (jnp.sum(e, axis=-1, keepdims=True) + es)
    return e * inv, es * inv


def _attn_fwd(qkv, sinks):
    t = qkv.shape[0]
    w, hd = ATTN_WINDOW, ATTN_HEAD_DIM
    kd = ATTN_N_KV * hd
    qd = ATTN_REP * kd
    nb = t // w

    def body(q_ref, kc_ref, vc_ref, kp_ref, vp_ref, s_ref, o_ref):
        n = pl.program_id(0)
        mask = _attn_mask(n)
        q = q_ref[...]
        kb = jnp.concatenate([kp_ref[...], kc_ref[...]], axis=0)
        vb = jnp.concatenate([vp_ref[...], vc_ref[...]], axis=0)
        sk = s_ref[...]
        for kv in range(ATTN_N_KV):
            kbh = kb[:, kv * hd:(kv + 1) * hd]
            vbh = vb[:, kv * hd:(kv + 1) * hd]
            for rep in range(ATTN_REP):
                h = kv * ATTN_REP + rep
                p, _ = _attn_probs(q[:, h * hd:(h + 1) * hd], kbh, mask, sk[:, h:h + 1])
                o_ref[:, h * hd:(h + 1) * hd] = _dot(p.astype(BF16), vbh).astype(BF16)

    prev = lambda n: jnp.maximum(n - 1, 0)
    return pl.pallas_call(
        body, name="attn_fwd", grid=(nb,),
        in_specs=[pl.BlockSpec((w, qd), lambda n: (n, 0)),
                  pl.BlockSpec((w, kd), lambda n: (n, ATTN_REP)),
                  pl.BlockSpec((w, kd), lambda n: (n, ATTN_REP + 1)),
                  pl.BlockSpec((w, kd), lambda n: (prev(n), ATTN_REP)),
                  pl.BlockSpec((w, kd), lambda n: (prev(n), ATTN_REP + 1)),
                  pl.BlockSpec((1, sinks.shape[1]), lambda n: (0, 0))],
        out_specs=pl.BlockSpec((w, qd), lambda n: (n, 0)),
        out_shape=jax.ShapeDtypeStruct((t, qd), BF16),
        compiler_params=_params("parallel"))(qkv, qkv, qkv, qkv, qkv, sinks)


def _attn_bwd(qkv, do, sinks):
    t = qkv.shape[0]
    w, hd = ATTN_WINDOW, ATTN_HEAD_DIM
    kd = ATTN_N_KV * hd
    qd = ATTN_REP * kd
    nq = ATTN_N_KV * ATTN_REP
    nb = t // w

    def body(q_ref, kc_ref, vc_ref, kp_ref, vp_ref, do_ref, s_ref,
             dq_ref, dk_ref, dv_ref, bq_ref, bk_ref, bv_ref, dsk_ref, ck_ref, cv_ref):
        n = pl.program_id(0)
        first = n == 0

        @pl.when(first)
        def _():
            ck_ref[...] = jnp.zeros_like(ck_ref)
            cv_ref[...] = jnp.zeros_like(cv_ref)
            bq_ref[...] = jnp.zeros_like(bq_ref)
            bk_ref[...] = jnp.zeros_like(bk_ref)
            bv_ref[...] = jnp.zeros_like(bv_ref)
            dsk_ref[...] = jnp.zeros_like(dsk_ref)

        @pl.when(n < nb)
        def _():
            mask = _attn_mask(n)
            q = q_ref[...]
            dov = do_ref[...]
            kb = jnp.concatenate([kp_ref[...], kc_ref[...]], axis=0)
            vb = jnp.concatenate([vp_ref[...], vc_ref[...]], axis=0)
            sk = s_ref[...]
            lane = lax.broadcasted_iota(jnp.int32, (1, nq), 1)
            dsk = jnp.zeros((1, nq), F32)
            dq_parts, dk_parts, dv_parts = [], [], []
            for kv in range(ATTN_N_KV):
                kbh = kb[:, kv * hd:(kv + 1) * hd]
                vbh = vb[:, kv * hd:(kv + 1) * hd]
                dkh = jnp.zeros((2 * w, hd), F32)
                dvh = jnp.zeros((2 * w, hd), F32)
                for rep in range(ATTN_REP):
                    h = kv * ATTN_REP + rep
                    qh = q[:, h * hd:(h + 1) * hd]
                    doh = dov[:, h * hd:(h + 1) * hd]
                    p, ps = _attn_probs(qh, kbh, mask, sk[:, h:h + 1])
                    pb = p.astype(BF16)
                    dp = _dot_nt(doh, vbh)
                    delta = jnp.sum(p * dp, axis=-1, keepdims=True)
                    dsc = (p * (dp - delta) * (hd ** -0.5)).astype(BF16)
                    dq_parts.append(_dot(dsc, kbh))
                    dkh = dkh + _dot_tn(dsc, qh)
                    dvh = dvh + _dot_tn(pb, doh)
                    dsk = jnp.where(lane == h, -jnp.sum(ps * delta, axis=0, keepdims=True), dsk)
                dk_parts.append(dkh)
                dv_parts.append(dvh)
            dq = jnp.concatenate(dq_parts, axis=1)
            dkb = jnp.concatenate(dk_parts, axis=1)
            dvb = jnp.concatenate(dv_parts, axis=1)
            dq_ref[...] = dq.astype(BF16)
            bq_ref[...] += jnp.sum(dq, axis=0, keepdims=True)
            dsk_ref[...] += dsk
            dk_prev = ck_ref[...] + dkb[:w, :]
            dv_prev = cv_ref[...] + dvb[:w, :]
            dk_ref[...] = dk_prev.astype(BF16)
            dv_ref[...] = dv_prev.astype(BF16)

            @pl.when(n > 0)
            def _():
                bk_ref[...] += jnp.sum(dk_prev, axis=0, keepdims=True)
                bv_ref[...] += jnp.sum(dv_prev, axis=0, keepdims=True)

            ck_ref[...] = dkb[w:, :]
            cv_ref[...] = dvb[w:, :]

        @pl.when(n == nb)
        def _():
            dk_ref[...] = ck_ref[...].astype(BF16)
            dv_ref[...] = cv_ref[...].astype(BF16)
            bk_ref[...] += jnp.sum(ck_ref[...], axis=0, keepdims=True)
            bv_ref[...] += jnp.sum(cv_ref[...], axis=0, keepdims=True)

    cur = lambda n: jnp.minimum(n, nb - 1)
    prev = lambda n: jnp.maximum(jnp.minimum(n, nb - 1) - 1, 0)
    late = lambda n: jnp.maximum(n - 1, 0)
    vec = lambda width: pl.BlockSpec((1, width), lambda n: (0, 0))
    return pl.pallas_call(
        body, name="attn_bwd", grid=(nb + 1,),
        in_specs=[pl.BlockSpec((w, qd), lambda n: (cur(n), 0)),
                  pl.BlockSpec((w, kd), lambda n: (cur(n), ATTN_REP)),
                  pl.BlockSpec((w, kd), lambda n: (cur(n), ATTN_REP + 1)),
                  pl.BlockSpec((w, kd), lambda n: (prev(n), ATTN_REP)),
                  pl.BlockSpec((w, kd), lambda n: (prev(n), ATTN_REP + 1)),
                  pl.BlockSpec((w, qd), lambda n: (cur(n), 0)),
                  vec(nq)],
        out_specs=[pl.BlockSpec((w, qd), lambda n: (cur(n), 0)),
                   pl.BlockSpec((w, kd), lambda n: (late(n), 0)),
                   pl.BlockSpec((w, kd), lambda n: (late(n), 0)),
                   vec(qd), vec(kd), vec(kd), vec(nq)],
        out_shape=[jax.ShapeDtypeStruct((t, qd), BF16), jax.ShapeDtypeStruct((t, kd), BF16),
                   jax.ShapeDtypeStruct((t, kd), BF16), jax.ShapeDtypeStruct((1, qd), F32),
                   jax.ShapeDtypeStruct((1, kd), F32), jax.ShapeDtypeStruct((1, kd), F32),
                   jax.ShapeDtypeStruct((1, nq), F32)],
        scratch_shapes=[pltpu.VMEM((w, kd), F32), pltpu.VMEM((w, kd), F32)],
        compiler_params=_params("arbitrary"))(qkv, qkv, qkv, qkv, qkv, do, sinks)


def _attn_mask_t(n):
    w = ATTN_WINDOW
    kpos = lax.broadcasted_iota(jnp.int32, (2 * w, ATTN_REP * w), 0)
    qpos = lax.broadcasted_iota(jnp.int32, (2 * w, ATTN_REP * w), 1) % w + w
    rel = qpos - kpos
    return (rel >= 0) & (rel < w) & jnp.logical_not((n == 0) & (kpos < w))


def _attn_probs_t(qts, ktb, mask, sink):
    s = _dot_tn(ktb, qts) * (ATTN_HEAD_DIM ** -0.5)
    s = jnp.where(mask, s, -jnp.inf)
    m = jnp.maximum(jnp.max(s, axis=0, keepdims=True), sink)
    e = jnp.exp(s - m)
    es = jnp.exp(sink - m)
    inv = 1.0 / (jnp.sum(e, axis=0, keepdims=True) + es)
    return e * inv, es * inv


def _attn_blocks_t(kv, q_ref, kc_ref, vc_ref, kp_ref, vp_ref):
    hd = ATTN_HEAD_DIM
    rows = slice(kv * hd, (kv + 1) * hd)
    ktb = jnp.concatenate([kp_ref[rows, :], kc_ref[rows, :]], axis=1)
    vtb = jnp.concatenate([vp_ref[rows, :], vc_ref[rows, :]], axis=1)
    qts = jnp.concatenate([q_ref[(kv * ATTN_REP + r) * hd:(kv * ATTN_REP + r + 1) * hd, :]
                           for r in range(ATTN_REP)], axis=1)
    return qts, ktb, vtb


def _attn_specs_t(nb, cur, prev):
    w, hd = ATTN_WINDOW, ATTN_HEAD_DIM
    kd = ATTN_N_KV * hd
    qd = ATTN_REP * kd
    return [pl.BlockSpec((qd, w), lambda n: (0, cur(n))),
            pl.BlockSpec((kd, w), lambda n: (ATTN_REP, cur(n))),
            pl.BlockSpec((kd, w), lambda n: (ATTN_REP + 1, cur(n))),
            pl.BlockSpec((kd, w), lambda n: (ATTN_REP, prev(n))),
            pl.BlockSpec((kd, w), lambda n: (ATTN_REP + 1, prev(n)))]


def _attn_fwd_t(qkv_t, sinks_rep):
    t = qkv_t.shape[1]
    w, hd = ATTN_WINDOW, ATTN_HEAD_DIM
    qd = ATTN_N_KV * ATTN_REP * hd
    nb = t // w

    def body(q_ref, kc_ref, vc_ref, kp_ref, vp_ref, s_ref, o_ref):
        mask = _attn_mask_t(pl.program_id(0))
        for kv in range(ATTN_N_KV):
            qts, ktb, vtb = _attn_blocks_t(kv, q_ref, kc_ref, vc_ref, kp_ref, vp_ref)
            p, _ = _attn_probs_t(qts, ktb, mask, s_ref[kv])
            ots = _dot(vtb, p.astype(BF16))
            for r in range(ATTN_REP):
                h = kv * ATTN_REP + r
                o_ref[h * hd:(h + 1) * hd, :] = ots[:, r * w:(r + 1) * w].astype(BF16)

    return pl.pallas_call(
        body, name="attn_fwd", grid=(nb,),
        in_specs=_attn_specs_t(nb, lambda n: n, lambda n: jnp.maximum(n - 1, 0)) + [
            pl.BlockSpec(sinks_rep.shape, lambda n: (0, 0, 0))],
        out_specs=pl.BlockSpec((qd, w), lambda n: (0, n)),
        out_shape=jax.ShapeDtypeStruct((qd, t), BF16),
        compiler_params=_params("parallel"))(qkv_t, qkv_t, qkv_t, qkv_t, qkv_t, sinks_rep)


def _attn_bwd_t(qkv_t, do_t, sinks_rep):
    t = qkv_t.shape[1]
    w, hd = ATTN_WINDOW, ATTN_HEAD_DIM
    kd = ATTN_N_KV * hd
    qd = ATTN_REP * kd
    nq = ATTN_N_KV * ATTN_REP
    nb = t // w
    rows_all = qd + 2 * kd

    def body(q_ref, kc_ref, vc_ref, kp_ref, vp_ref, do_ref, s_ref, dqkv_ref, bsum_ref, dsk_ref,
             carry_ref, new_ref, bacc_ref, sacc_ref):
        n = pl.program_id(0)

        @pl.when(n == 0)
        def _():
            carry_ref[...] = jnp.zeros_like(carry_ref)
            bacc_ref[...] = jnp.zeros_like(bacc_ref)
            sacc_ref[...] = jnp.zeros_like(sacc_ref)

        @pl.when(n < nb)
        def _():
            mask = _attn_mask_t(n)
            for kv in range(ATTN_N_KV):
                qts, ktb, vtb = _attn_blocks_t(kv, q_ref, kc_ref, vc_ref, kp_ref, vp_ref)
                dots = jnp.concatenate([do_ref[(kv * ATTN_REP + r) * hd:(kv * ATTN_REP + r + 1) * hd, :]
                                        for r in range(ATTN_REP)], axis=1)
                p, ps = _attn_probs_t(qts, ktb, mask, s_ref[kv])
                dpt = _dot_tn(vtb, dots)
                delta = jnp.sum(p * dpt, axis=0, keepdims=True)
                dst = (p * (dpt - delta) * (hd ** -0.5)).astype(BF16)
                dqts = _dot(ktb, dst)
                for r in range(ATTN_REP):
                    h = kv * ATTN_REP + r
                    new_ref[h * hd:(h + 1) * hd, :] = dqts[:, r * w:(r + 1) * w]
                dktb = _dot_nt(qts, dst)
                dvtb = _dot_nt(dots, p.astype(BF16))
                krows = slice(qd + kv * hd, qd + (kv + 1) * hd)
                vrows = slice(qd + kd + kv * hd, qd + kd + (kv + 1) * hd)
                carry_ref[krows, :] += dktb[:, :w]
                carry_ref[vrows, :] += dvtb[:, :w]
                new_ref[krows, :] = dktb[:, w:]
                new_ref[vrows, :] = dvtb[:, w:]
                sacc_ref[kv] += -(ps * delta)

        @pl.when(n >= 1)
        def _():
            done = carry_ref[...]
            dqkv_ref[...] = done.astype(BF16)
            bacc_ref[...] += done

        @pl.when(n < nb)
        def _():
            carry_ref[...] = new_ref[...]

        @pl.when(n == nb)
        def _():
            bsum_ref[...] = jnp.sum(bacc_ref[...], axis=1, keepdims=True)
            lane = lax.broadcasted_iota(jnp.int32, (1, nq), 1)
            dsk = jnp.zeros((1, nq), F32)
            for kv in range(ATTN_N_KV):
                acc = sacc_ref[kv]
                for r in range(ATTN_REP):
                    tot = jnp.sum(acc[:, r * w:(r + 1) * w], axis=1, keepdims=True)
                    dsk = jnp.where(lane == kv * ATTN_REP + r, tot, dsk)
            dsk_ref[...] = dsk

    cur = lambda n: jnp.minimum(n, nb - 1)
    prev = lambda n: jnp.maximum(jnp.minimum(n, nb - 1) - 1, 0)
    return pl.pallas_call(
        body, name="attn_bwd", grid=(nb + 1,),
        in_specs=_attn_specs_t(nb, cur, prev) + [pl.BlockSpec((qd, w), lambda n: (0, cur(n))),
                                                 pl.BlockSpec(sinks_rep.shape, lambda n: (0, 0, 0))],
        out_specs=[pl.BlockSpec((rows_all, w), lambda n: (0, jnp.maximum(n - 1, 0))),
                   pl.BlockSpec((rows_all, 1), lambda n: (0, 0)),
                   pl.BlockSpec((1, nq), lambda n: (0, 0))],
        out_shape=[jax.ShapeDtypeStruct((rows_all, t), BF16), jax.ShapeDtypeStruct((rows_all, 1), F32),
                   jax.ShapeDtypeStruct((1, nq), F32)],
        scratch_shapes=[pltpu.VMEM((rows_all, w), F32), pltpu.VMEM((rows_all, w), F32),
                        pltpu.VMEM((rows_all, w), F32), pltpu.VMEM(sinks_rep.shape, F32)],
        compiler_params=_params("arbitrary"))(qkv_t, qkv_t, qkv_t, qkv_t, qkv_t, do_t, sinks_rep)


HBM_SPEC = pl.BlockSpec(memory_space=pl.ANY)
HBM_ONLY = pl.BlockSpec(memory_space=pltpu.HBM)


def _comm_call(name, body, ins, out_shapes, n_sems):
    return pl.pallas_call(
        body, name=name, in_specs=[HBM_SPEC] * len(ins), out_specs=[HBM_SPEC] * len(out_shapes),
        out_shape=out_shapes,
        scratch_shapes=[pltpu.SemaphoreType.DMA((s,)) for s in n_sems])(*ins)


def _all_gather(name, shards, after):
    n = len(shards)

    def body(*refs):
        x_refs, out_refs = refs[:n], refs[n + 1:2 * n + 1]
        send_sems, recv_sems, local_sems = refs[2 * n + 1:]
        x, y, c = lax.axis_index("x"), lax.axis_index("y"), lax.axis_index("c")
        me, sibling = (x, y, c), (x, y, 1 - c)
        chips = [(1 - x, y), (x, 1 - y), (1 - x, 1 - y)]

        def slot(i, px, py, pc):
            return out_refs[i].at[4 * px + 2 * py + pc]

        def copy(k, i, block, to, src=None):
            return pltpu.make_async_remote_copy(
                src_ref=slot(i, *block) if src is None else src, dst_ref=slot(i, *block),
                send_sem=send_sems.at[k * n + i], recv_sem=recv_sems.at[k * n + i], device_id=to,
                device_id_type=MESH)

        mine = [pltpu.make_async_copy(x_refs[i], slot(i, *me), local_sems.at[i]) for i in range(n)]
        first = []
        for i in range(n):
            mine[i].start()
            first.append(copy(0, i, me, sibling, src=x_refs[i]))
            first += [copy(1 + j, i, me, (*chip, c), src=x_refs[i]) for j, chip in enumerate(chips)]
        for cp in first:
            cp.start()
        passed = []
        for i in range(n):
            for j, chip in enumerate(chips):
                copy(1 + j, i, (*chip, c), me).wait_recv()
                passed.append(copy(4 + j, i, (*chip, c), sibling))
                passed[-1].start()
        for i in range(n):
            copy(0, i, sibling, me).wait_recv()
            for j, chip in enumerate(chips):
                copy(4 + j, i, (*chip, 1 - c), me).wait_recv()
        for cp in first + passed:
            cp.wait_send()
        for cp in mine:
            cp.wait()

    outs = [jax.ShapeDtypeStruct((N_DEV,) + s.shape, s.dtype) for s in shards]
    return _comm_call(name, body, list(shards) + [after], outs, (7 * n, 7 * n, n))


SEM_SPEC = pl.BlockSpec(memory_space=pltpu.SEMAPHORE)
SPLIT_COPY_EFFECT = pltpu.SideEffectType.DATAFLOW_SIDE_EFFECTING


def _in_hbm(a):
    return pltpu.with_memory_space_constraint(a, pltpu.HBM)


def _split_start(name, body, srcs, lands, n_sems):
    n = len(srcs)
    bufs = [_in_hbm(a) for a in list(srcs) + list(lands)]
    outs = pl.pallas_call(
        body, name=name,
        out_shape=(pltpu.SemaphoreType.DMA((n_sems,)), pltpu.SemaphoreType.DMA((n_sems,)),
                   *[pltpu.HBM(a.shape, a.dtype) for a in bufs], jax.ShapeDtypeStruct((8, LANES), F32)),
        in_specs=[HBM_ONLY] * (2 * n),
        out_specs=(SEM_SPEC, SEM_SPEC, *[HBM_ONLY] * (2 * n), pl.BlockSpec(memory_space=pltpu.VMEM)),
        input_output_aliases={i: 2 + i for i in range(2 * n)},
        compiler_params=pltpu.CompilerParams(has_side_effects=SPLIT_COPY_EFFECT))(*bufs)
    return outs[0], outs[1], list(outs[2:2 + n]), list(outs[2 + n:2 + 2 * n]), outs[-1]


def _split_wait(name, body, send_sems, recv_sems, srcs, lands, after):
    n = len(srcs)
    outs = pl.pallas_call(
        body, name=name,
        out_shape=[pltpu.HBM(a.shape, a.dtype) for a in list(srcs) + list(lands)],
        in_specs=[HBM_ONLY] * (2 * n) + [SEM_SPEC, SEM_SPEC, HBM_SPEC],
        out_specs=[HBM_ONLY] * (2 * n),
        input_output_aliases={i: i for i in range(2 * n)},
        compiler_params=pltpu.CompilerParams(has_side_effects=SPLIT_COPY_EFFECT))(
            *srcs, *lands, send_sems, recv_sems, after)
    return list(outs[:n]), list(outs[n:])


N_PEERS = N_DEV - 1


def _gather_peers():
    x, y, c = lax.axis_index("x"), lax.axis_index("y"), lax.axis_index("c")
    flips = [(fx, fy, fc) for fx in (0, 1) for fy in (0, 1) for fc in (0, 1) if fx or fy or fc]
    return [(1 - x if fx else x, 1 - y if fy else y, 1 - c if fc else c) for fx, fy, fc in flips]


def _block_id(dev):
    return 4 * dev[0] + 2 * dev[1] + dev[2]


def _gather_start(name, shards):
    n = len(shards)

    def body(*refs):
        x_refs, land_refs = refs[:n], refs[n:2 * n]
        send_sems, recv_sems, token = refs[2 * n], refs[2 * n + 1], refs[-1]
        me = (lax.axis_index("x"), lax.axis_index("y"), lax.axis_index("c"))
        for i in range(n):
            for k, peer in enumerate(_gather_peers()):
                pltpu.make_async_remote_copy(
                    src_ref=x_refs[i], dst_ref=land_refs[i].at[_block_id(me)],
                    send_sem=send_sems.at[N_PEERS * i + k], recv_sem=recv_sems.at[N_PEERS * i + k],
                    device_id=peer, device_id_type=MESH).start()
        token[...] = jnp.zeros_like(token)

    lands = [lax.empty((N_DEV,) + s.shape, s.dtype) for s in shards]
    return _split_start(name, body, shards, lands, N_PEERS * n)


def _gather_wait(name, send_sems, recv_sems, first, shards, lands, after):
    n = len(shards)

    def body(*refs):
        x_refs, land_refs = refs[:n], refs[n:2 * n]
        send_sems, recv_sems = refs[2 * n], refs[2 * n + 1]
        for i in range(n):
            for k, peer in enumerate(_gather_peers()):
                cp = pltpu.make_async_remote_copy(
                    src_ref=x_refs[i], dst_ref=land_refs[i].at[_block_id(peer)],
                    send_sem=send_sems.at[N_PEERS * (first + i) + k],
                    recv_sem=recv_sems.at[N_PEERS * (first + i) + k],
                    device_id=peer, device_id_type=MESH)
                cp.wait_send()
                cp.wait_recv()

    return _split_wait(name, body, send_sems, recv_sems, shards, lands, after)


def _gather_forward(name, lands, shards):
    n = len(shards)

    def body(*refs):
        x_refs, out_refs = refs[n:2 * n], refs[2 * n:3 * n]
        send_sems, recv_sems, local_sems = refs[3 * n:]
        x, y, c = lax.axis_index("x"), lax.axis_index("y"), lax.axis_index("c")
        chips = [(1 - x, y), (x, 1 - y), (1 - x, 1 - y)]
        mine = [pltpu.make_async_copy(x_refs[i], out_refs[i].at[_block_id((x, y, c))], local_sems.at[i])
                for i in range(n)]
        passed = [pltpu.make_async_remote_copy(
            src_ref=out_refs[i].at[_block_id((*chip, c))], dst_ref=out_refs[i].at[_block_id((*chip, c))],
            send_sem=send_sems.at[3 * i + j], recv_sem=recv_sems.at[3 * i + j], device_id=(x, y, 1 - c),
            device_id_type=MESH) for i in range(n) for j, chip in enumerate(chips)]
        for cp in mine + passed:
            cp.start()
        for i in range(n):
            for j, chip in enumerate(chips):
                pltpu.make_async_remote_copy(
                    src_ref=out_refs[i].at[_block_id((*chip, c))], dst_ref=out_refs[i].at[_block_id((*chip, 1 - c))],
                    send_sem=send_sems.at[3 * i + j], recv_sem=recv_sems.at[3 * i + j], device_id=(x, y, 1 - c),
                    device_id_type=MESH).wait()
        for cp in mine:
            cp.wait()

    return pl.pallas_call(
        body, name=name, in_specs=[HBM_SPEC] * (2 * n), out_specs=[HBM_SPEC] * n,
        out_shape=[jax.ShapeDtypeStruct(a.shape, a.dtype) for a in lands],
        input_output_aliases={i: i for i in range(n)},
        scratch_shapes=[pltpu.SemaphoreType.DMA((3 * n,)), pltpu.SemaphoreType.DMA((3 * n,)),
                        pltpu.SemaphoreType.DMA((n,))])(*lands, *shards)


def _chip_peers():
    x, y, c = lax.axis_index("x"), lax.axis_index("y"), lax.axis_index("c")
    return [(1 - x, y, c), (x, 1 - y, c), (1 - x, 1 - y, c)]


def _chip_start(name, blocks):
    n = len(blocks)

    def body(*refs):
        p_refs, land_refs = refs[:n], refs[n:2 * n]
        send_sems, recv_sems, token = refs[2 * n], refs[2 * n + 1], refs[-1]
        for i in range(n):
            for j, peer in enumerate(_chip_peers()):
                pltpu.make_async_remote_copy(
                    src_ref=p_refs[i].at[j], dst_ref=land_refs[i].at[j], send_sem=send_sems.at[3 * i + j],
                    recv_sem=recv_sems.at[3 * i + j], device_id=peer, device_id_type=MESH).start()
        token[...] = jnp.zeros_like(token)

    lands = [lax.empty(b.shape, b.dtype) for b in blocks]
    return _split_start(name, body, blocks, lands, 3 * n)


def _chip_wait(name, send_sems, recv_sems, blocks, lands, after):
    n = len(blocks)

    def body(*refs):
        p_refs, land_refs = refs[:n], refs[n:2 * n]
        send_sems, recv_sems = refs[2 * n], refs[2 * n + 1]
        for i in range(n):
            for j, peer in enumerate(_chip_peers()):
                cp = pltpu.make_async_remote_copy(
                    src_ref=p_refs[i].at[j], dst_ref=land_refs[i].at[j], send_sem=send_sems.at[3 * i + j],
                    recv_sem=recv_sems.at[3 * i + j], device_id=peer, device_id_type=MESH)
                cp.wait_send()
                cp.wait_recv()

    return _split_wait(name, body, send_sems, recv_sems, blocks, lands, after)


def _scatter_start(name, blocks):
    n = len(blocks)

    def body(*refs):
        b_refs, land_refs = refs[:n], refs[n:2 * n]
        send_sems, recv_sems, token = refs[2 * n], refs[2 * n + 1], refs[-1]
        me = (lax.axis_index("x"), lax.axis_index("y"), lax.axis_index("c"))
        for i in range(n):
            for k, peer in enumerate(_gather_peers()):
                pltpu.make_async_remote_copy(
                    src_ref=b_refs[i].at[_block_id(peer)], dst_ref=land_refs[i].at[_block_id(me)],
                    send_sem=send_sems.at[N_PEERS * i + k], recv_sem=recv_sems.at[N_PEERS * i + k],
                    device_id=peer, device_id_type=MESH).start()
        token[...] = jnp.zeros_like(token)

    lands = [lax.empty(b.shape, b.dtype) for b in blocks]
    return _split_start(name, body, blocks, lands, N_PEERS * n)


def _scatter_wait(name, send_sems, recv_sems, blocks, lands, after):
    n = len(blocks)

    def body(*refs):
        b_refs, land_refs = refs[:n], refs[n:2 * n]
        send_sems, recv_sems = refs[2 * n], refs[2 * n + 1]
        for i in range(n):
            for k, peer in enumerate(_gather_peers()):
                cp = pltpu.make_async_remote_copy(
                    src_ref=b_refs[i].at[_block_id(peer)], dst_ref=land_refs[i].at[_block_id(peer)],
                    send_sem=send_sems.at[N_PEERS * i + k], recv_sem=recv_sems.at[N_PEERS * i + k],
                    device_id=peer, device_id_type=MESH)
                cp.wait_send()
                cp.wait_recv()

    return _split_wait(name, body, send_sems, recv_sems, blocks, lands, after)


def _pair_exchange(name, blocks):
    n = len(blocks)

    def body(*refs):
        g_refs, out_refs = refs[:n], refs[n:2 * n]
        send_sems, recv_sems = refs[2 * n:]
        x, y, c = lax.axis_index("x"), lax.axis_index("y"), lax.axis_index("c")
        copies = [pltpu.make_async_remote_copy(
            src_ref=g_refs[i].at[2 * k + 1 - c], dst_ref=out_refs[i].at[k], send_sem=send_sems.at[4 * i + k],
            recv_sem=recv_sems.at[4 * i + k], device_id=(x, y, 1 - c), device_id_type=MESH)
            for i in range(n) for k in range(4)]
        for cp in copies:
            cp.start()
        for cp in copies:
            cp.wait()

    outs = [jax.ShapeDtypeStruct((4,) + b.shape[1:], b.dtype) for b in blocks]
    return _comm_call(name, body, blocks, outs, (4 * n, 4 * n))


def _chip_exchange(name, blocks):
    n = len(blocks)

    def body(*refs):
        p_refs, out_refs = refs[:n], refs[n:2 * n]
        send_sems, recv_sems = refs[2 * n:]
        x, y, c = lax.axis_index("x"), lax.axis_index("y"), lax.axis_index("c")
        chips = [(1 - x, y), (x, 1 - y), (1 - x, 1 - y)]
        copies = [pltpu.make_async_remote_copy(
            src_ref=p_refs[i].at[j], dst_ref=out_refs[i].at[j], send_sem=send_sems.at[3 * i + j],
            recv_sem=recv_sems.at[3 * i + j], device_id=(*chip, c), device_id_type=MESH)
            for i in range(n) for j, chip in enumerate(chips)]
        for cp in copies:
            cp.start()
        for cp in copies:
            cp.wait()

    outs = [jax.ShapeDtypeStruct(b.shape, b.dtype) for b in blocks]
    return _comm_call(name, body, blocks, outs, (3 * n, 3 * n))


def _pair_sum(name, blocks, from_sibling, g_idx, r_idx):
    _, r, c_ = blocks.shape
    tr = _tile(r, 512, 16)

    def body(gi_ref, ri_ref, a_ref, b_ref, own_ref, send_ref):
        k = pl.program_id(1)
        s = a_ref[...] + b_ref[...]

        @pl.when(k == 0)
        def _():
            own_ref[...] = s

        @pl.when(k > 0)
        def _():
            send_ref[...] = s.astype(send_ref.dtype)

    return pl.pallas_call(
        body, name=name,
        grid_spec=pltpu.PrefetchScalarGridSpec(
            num_scalar_prefetch=2, grid=(r // tr, 4),
            in_specs=[pl.BlockSpec((None, tr, c_), lambda i, k, gi, ri: (gi[k], i, 0)),
                      pl.BlockSpec((None, tr, c_), lambda i, k, gi, ri: (ri[k], i, 0))],
            out_specs=[pl.BlockSpec((None, tr, c_), lambda i, k, gi, ri: (0, i, 0)),
                       pl.BlockSpec((None, tr, c_), lambda i, k, gi, ri: (jnp.maximum(k - 1, 0), i, 0))]),
        out_shape=[jax.ShapeDtypeStruct((1, r, c_), F32), jax.ShapeDtypeStruct((3, r, c_), PAYLOAD)],
        compiler_params=_params("parallel", "arbitrary"))(g_idx, r_idx, blocks, from_sibling)


def _adamw(w, g, m, v):
    m = ADAM_B1 * m + (1.0 - ADAM_B1) * g
    v = ADAM_B2 * v + (1.0 - ADAM_B2) * (g * g)
    m_hat = m / (1.0 - ADAM_B1 ** ADAM_STEP)
    v_hat = v / (1.0 - ADAM_B2 ** ADAM_STEP)
    delta = -ADAM_LR * (m_hat / (jnp.sqrt(v_hat) + ADAM_EPS) + ADAM_WD * w)
    return delta, m, v


def _adamw_tiles(r, c_):
    tr = _tile(r, 256, 16)
    return (tr, c_) if tr < r or r <= 256 else (r, _tile(c_, 256))


def _sum_parts(part):
    g = part[0].astype(F32)
    for k in range(1, part.shape[0]):
        g = g + part[k].astype(F32)
    return g


def _sum_adamw(name, parts, w, m, v):
    r, c_ = w.shape
    tr, tc = _adamw_tiles(r, c_)

    def body(p_ref, w_ref, m_ref, v_ref, g_ref, d_ref, nm_ref, nv_ref):
        g = _sum_parts(p_ref)
        g_ref[...] = g
        d_ref[...], nm_ref[...], nv_ref[...] = _adamw(w_ref[...], g, m_ref[...], v_ref[...])

    tile = pl.BlockSpec((tr, tc), lambda i, j: (i, j))
    return pl.pallas_call(body, name=name, grid=(r // tr, c_ // tc),
                          in_specs=[pl.BlockSpec((parts.shape[0], tr, tc), lambda i, j: (0, i, j)), tile, tile, tile],
                          out_specs=[tile] * 4, out_shape=[jax.ShapeDtypeStruct((r, c_), F32)] * 4,
                          compiler_params=_params("parallel", "parallel"))(parts, w, m, v)


def _sum_adamw_layers(name, parts, w, m, v):
    n_layers, r, c_ = w.shape
    tr = _tile(r, 256, 16)

    def body(*refs):
        p_refs = refs[:n_layers]
        w_ref, m_ref, v_ref, g_ref, d_ref, nm_ref, nv_ref = refs[n_layers:]
        layer = pl.program_id(0)
        g = _sum_parts(p_refs[0])
        for li in range(1, n_layers):
            g = jnp.where(layer == li, _sum_parts(p_refs[li]), g)
        g_ref[...] = g
        d_ref[...], nm_ref[...], nv_ref[...] = _adamw(w_ref[...], g, m_ref[...], v_ref[...])

    row = pl.BlockSpec((None, tr, c_), lambda l, i: (l, i, 0))
    specs = [pl.BlockSpec((p.shape[0], tr, c_), lambda l, i: (0, i, 0)) for p in parts]
    return pl.pallas_call(body, name=name, grid=(n_layers, r // tr), in_specs=specs + [row, row, row],
                          out_specs=[row] * 4, out_shape=[jax.ShapeDtypeStruct(w.shape, F32)] * 4,
                          compiler_params=_params("parallel", "parallel"))(*parts, w, m, v)


def _pack_rows(flat, n_rows, cols):
    pad = n_rows * cols - flat.shape[-1]
    flat = jnp.pad(flat, [(0, 0)] * (flat.ndim - 1) + [(0, pad)])
    return flat.reshape(flat.shape[:-1] + (n_rows, cols))


def _cols_join(blocks):
    return jnp.concatenate([blocks[d] for d in range(N_DEV)], axis=1)


def _cols_split(full):
    c = full.shape[1] // N_DEV
    return jnp.stack([full[:, d * c:(d + 1) * c] for d in range(N_DEV)])


def _rows_join(blocks):
    return blocks.reshape(N_DEV * blocks.shape[1], blocks.shape[2])


def _rows_split(full):
    return full.reshape(N_DEV, full.shape[0] // N_DEV, full.shape[1])


def _perm_xbc(a, ng):
    lead = a.shape[:-1]
    di, gn = ng * GW, ng * SSD_D_STATE
    xs = a[..., :di].reshape(lead + (ng, GW))
    bs = a[..., di:di + gn].reshape(lead + (ng, SSD_D_STATE))
    cs = a[..., di + gn:].reshape(lead + (ng, SSD_D_STATE))
    return jnp.concatenate([xs, bs, cs], axis=-1).reshape(lead + (ng * GC,))


def _unperm_xbc(a, ng):
    lead = a.shape[:-1]
    g = a.reshape(lead + (ng, GC))
    return jnp.concatenate([g[..., :GW].reshape(lead + (ng * GW,)),
                            g[..., GW:GW + SSD_D_STATE].reshape(lead + (ng * SSD_D_STATE,)),
                            g[..., GW + SSD_D_STATE:].reshape(lead + (ng * SSD_D_STATE,))], axis=-1)


def _heads_col(v, ng):
    return jnp.pad(v.reshape(ng, 1, SSD_HPG), ((0, 0), (0, 0), (0, LANES - SSD_HPG)))


def _heads_row(v, ng):
    return jnp.pad(v.reshape(ng, SSD_HPG, 1), ((0, 0), (0, 8 - SSD_HPG), (0, 0)))


MATRIX_ITEMS = ("w_in", "w_out", "up0", "down0", "w_qkv", "w_o", "up1", "down1")
VECTOR_ITEMS = ("conv_w", "b_qkv", "b_o")
ITEMS = MATRIX_ITEMS + VECTOR_ITEMS
GATHER_STAGES = (("w_in", "conv_w"), ("w_out", "up0", "down0"), ("w_qkv", "b_qkv", "w_o", "b_o", "up1", "down1"))


def _items(tree, prefix=""):
    g = lambda k: tree[prefix + k]
    return {"w_in": g("ssd_w_in")[0].T, "w_out": g("ssd_w_out")[0], "w_qkv": g("attn_w_qkv")[0].T,
            "w_o": g("attn_w_o")[0], "up0": g("mlp_w_up")[0], "up1": g("mlp_w_up")[1],
            "down0": g("mlp_w_down")[0], "down1": g("mlp_w_down")[1], "conv_w": g("ssd_conv_w")[0],
            "b_qkv": g("attn_b_qkv"), "b_o": g("attn_b_o")}


def _from_items(it):
    return {"ssd_w_in": it["w_in"][None], "ssd_w_out": it["w_out"][None], "attn_w_qkv": it["w_qkv"].T[None],
            "attn_w_o": it["w_o"][None], "mlp_w_up": jnp.stack([it["up0"], it["up1"]]),
            "mlp_w_down": jnp.stack([it["down0"], it["down1"]]), "ssd_conv_w": it["conv_w"][None],
            "attn_b_qkv": it["b_qkv"], "attn_b_o": it["b_o"]}


REPLICATED = ("ssd_conv_b", "ssd_dt_bias", "ssd_a_log", "ssd_d", "ssd_norm_w", "attn_sinks", "mix_pre_norm",
              "mix_post_norm", "ffn_pre_norm", "ffn_post_norm")
WEIGHTS = ("ssd_w_in", "ssd_conv_w", "ssd_conv_b", "ssd_dt_bias", "ssd_a_log", "ssd_d", "ssd_norm_w", "ssd_w_out",
           "attn_w_qkv", "attn_b_qkv", "attn_sinks", "attn_w_o", "attn_b_o", "mlp_w_up", "mlp_w_down",
           "mix_pre_norm", "mix_post_norm", "ffn_pre_norm", "ffn_post_norm")


def _forward_backward(x, target, rep, token, weights_of_stage, reduce_grads):
    t, d = x.shape
    ng = rep["ssd_norm_w"].shape[1] // GW
    di = ng * GW
    n_xbc = ng * GC
    nh = ng * SSD_HPG
    grads, blocks = {}, {}
    w_up, w_down = [None, None], [None, None]
    sinks_rep = jnp.repeat(rep["attn_sinks"].reshape(ATTN_N_KV, ATTN_REP, 1), ATTN_WINDOW, axis=2).reshape(
        ATTN_N_KV, 1, ATTN_REP * ATTN_WINDOW)
    conv_b = rep["ssd_conv_b"]
    gn = ng * SSD_D_STATE
    parts = ((0, di), (di, di), (2 * di, gn), (2 * di + gn, gn), (di + n_xbc, nh))
    bias_c, alog_c, dsk_c = (_heads_col(rep[k], ng) for k in ("ssd_dt_bias", "ssd_a_log", "ssd_d"))
    bias_r, alog_r = (_heads_row(rep[k], ng) for k in ("ssd_dt_bias", "ssd_a_log"))
    norm = {k: rep[k] for k in ("mix_pre_norm", "mix_post_norm", "ffn_pre_norm", "ffn_post_norm")}

    def nrow(name, i):
        return norm[name][i:i + 1]

    def mlp_fwd(i, u2):
        a, p = _mm(f"mlp{i}_up", [u2], [w_up[i]], "nn", tm=1024, tn=1024, out_dtypes=(F32, BF16),
                   epilogue=lambda acc: (acc, jnp.square(jnp.maximum(acc, 0.0))))
        f = _mm(f"mlp{i}_down", [p], [w_down[i]], "nn", tm=512, tn=1024)
        return a, p, f

    def mlp_bwd(i, df, u2, a, p):
        da = _mm(f"mlp{i}_dact", [df], [w_down[i]], "nt", tm=1024, tn=1024, out_dtypes=(BF16,),
                 tiles=(a,), epilogue=lambda acc, av: (acc * (2.0 * jnp.maximum(av, 0.0)),))
        blocks[f"down{i}"] = _rows_split(_mm(f"mlp{i}_dwdown", [p], [df], "tn", tm=512, tn=1024,
                                             out_dtypes=(PAYLOAD,)))
        blocks[f"up{i}"] = _mm(f"mlp{i}_dwup", [u2], [da], "tn", tm=1024, tn=da.shape[1] // N_DEV,
                               out_dtypes=(PAYLOAD,), col_blocks=True)
        return _mm(f"mlp{i}_dx", [da], [w_up[i]], "nt", tm=512, tn=1024)

    u0 = _prenorm("l0_prenorm", x, nrow("mix_pre_norm", 0), token)
    got = weights_of_stage(0, u0)
    w_in_t = _rows_join(got["w_in"])
    w_dt_t = jnp.pad(w_in_t[di + n_xbc:], ((0, LANES - nh), (0, 0)))
    conv_w = _cols_join(got["conv_w"])
    zx = _mm("ssd_in_proj", [u0], [w_in_t], "nt", tm=1024, tn=1024, n_use=di + n_xbc)
    zdt = _mm("ssd_dt_proj", [u0], [w_dt_t], "nt", tm=1024, tn=LANES)
    pre = _conv_fwd(zx, di, n_xbc, conv_w, conv_b)
    dt_raw = zdt[:, :nh].reshape(t, ng, SSD_HPG)
    dtc = jnp.pad(jnp.transpose(dt_raw, (1, 0, 2)), ((0, 0), (0, 0), (0, LANES - SSD_HPG)))
    dtr = jnp.pad(jnp.transpose(dt_raw, (1, 2, 0)), ((0, 0), (0, 8 - SSD_HPG), (0, 0)))
    ssd_args = (dtc, dtr, bias_c, alog_c, dsk_c, bias_r, alog_r)
    y, states = _ssd_fwd(pre, *ssd_args)
    yn = _gate_norm_fwd(y, zx, rep["ssd_norm_w"])
    got = weights_of_stage(1, yn)
    w_out = _rows_join(got["w_out"])
    w_up[0], w_down[0] = _cols_join(got["up0"]), _rows_join(got["down0"])
    mix0 = _mm("ssd_out_proj", [yn], [w_out], "nn", tm=1024, tn=1024)
    h1, u0f = _post_pre("l0_mid", x, mix0, nrow("mix_post_norm", 0), nrow("ffn_pre_norm", 0))
    a0, p0, f0 = mlp_fwd(0, u0f)
    h2, u1 = _post_pre("l1_in", h1, f0, nrow("ffn_post_norm", 0), nrow("mix_pre_norm", 1))
    got = weights_of_stage(2, u1)
    w_qkv_t = _rows_join(got["w_qkv"])
    w_o = _rows_join(got["w_o"])
    b_qkv_col = got["b_qkv"].reshape(-1, 1)
    b_o = _cols_join(got["b_o"])
    w_up[1], w_down[1] = _cols_join(got["up1"]), _rows_join(got["down1"])
    qkv_t = _mm("attn_qkv_proj", [w_qkv_t], [u1], "nt", tm=768, tn=1024, out_dtypes=(BF16,), cols=(b_qkv_col,),
                epilogue=lambda acc, b: (acc + b,))
    ao_t = _attn_fwd_t(qkv_t, sinks_rep)
    mix1 = _mm("attn_out_proj", [ao_t], [w_o], "tn", tm=1024, tn=1024, rows=(b_o,),
               epilogue=lambda acc, b: (acc + b,))
    h3, u1f = _post_pre("l1_mid", h2, mix1, nrow("mix_post_norm", 1), nrow("ffn_pre_norm", 1))
    a1, p1, f1 = mlp_fwd(1, u1f)
    dh, loss_row = _final_loss("loss", h3, f1, nrow("ffn_post_norm", 1), target)

    g_norm = {k: [None, None] for k in norm}
    df1, g_norm["ffn_post_norm"][1], _ = _norm_bwd("l1_ffn_post_bwd", dh, post=(f1, nrow("ffn_post_norm", 1)))
    du = mlp_bwd(1, df1, u1f, a1, p1)
    sent = reduce_grads("mlp1", {k: blocks[k] for k in ("up1", "down1")})
    dh, g_norm["ffn_pre_norm"][1], dmix1, g_norm["mix_post_norm"][1], db_o = _norm_bwd(
        "l1_mid_bwd", dh, pre=(du, h3, nrow("ffn_pre_norm", 1)), post=(mix1, nrow("mix_post_norm", 1)), after=sent)
    blocks["b_o"] = _cols_split(db_o)
    blocks["w_o"] = _rows_split(_mm("attn_dwo", [ao_t], [dmix1], "nn", tm=512, tn=1024, out_dtypes=(PAYLOAD,)))
    dao_t = _mm("attn_dout", [w_o], [dmix1], "nt", tm=1024, tn=1024, out_dtypes=(BF16,))
    dqkv_t, db_qkv, grads["attn_sinks"] = _attn_bwd_t(qkv_t, dao_t, sinks_rep)
    blocks["b_qkv"] = db_qkv.reshape(N_DEV, 1, -1)
    blocks["w_qkv"] = _rows_split(_mm("attn_dwqkv", [dqkv_t], [u1], "nn", tm=512, tn=1024, out_dtypes=(PAYLOAD,)))
    du = _mm("attn_dx", [dqkv_t], [w_qkv_t], "tn", tm=1024, tn=1024)
    sent = reduce_grads("attn", {k: blocks[k] for k in ("w_o", "w_qkv", "b_o", "b_qkv")})
    dh, g_norm["mix_pre_norm"][1], df0, g_norm["ffn_post_norm"][0], _ = _norm_bwd(
        "l1_in_bwd", dh, pre=(du, h2, nrow("mix_pre_norm", 1)), post=(f0, nrow("ffn_post_norm", 0)), after=sent)
    du = mlp_bwd(0, df0, u0f, a0, p0)
    sent = reduce_grads("mlp0", {k: blocks[k] for k in ("up0", "down0")})
    dh, g_norm["ffn_pre_norm"][0], dmix0, g_norm["mix_post_norm"][0], _ = _norm_bwd(
        "l0_mid_bwd", dh, pre=(du, h1, nrow("ffn_pre_norm", 0)), post=(mix0, nrow("mix_post_norm", 0)), after=sent)
    blocks["w_out"] = _rows_split(_mm("ssd_dwout", [yn], [dmix0], "tn", tm=512, tn=1024, out_dtypes=(PAYLOAD,)))
    dyn = _mm("ssd_dyn", [dmix0], [w_out], "nt", tm=1024, tn=1024)
    dy, dz, grads["ssd_norm_w"] = _gate_norm_bwd(dyn, y, zx, rep["ssd_norm_w"])
    dpx, dpb, dpc, ddt_g, dbias_g, dalog_g, dd_g = _ssd_bwd(dy, pre, states, *ssd_args)
    conv_out = [_conv_bwd(f"ssd_conv_bwd_{tag}", dp, zx, c0, conv_w[:, c0 - di:c0 - di + n])
                for tag, dp, (c0, n) in zip("xbc", (dpx, dpb, dpc), parts[1:4])]
    dconv_w = jnp.concatenate([o[1] for o in conv_out], axis=1)
    dconv_b = jnp.concatenate([o[2] for o in conv_out], axis=1)
    ddt = jnp.transpose(ddt_g[:, :, :SSD_HPG], (1, 0, 2)).reshape(t, nh)
    ddt = jnp.pad(ddt, ((0, 0), (0, LANES - nh))).astype(BF16)
    blocks["conv_w"] = _cols_split(dconv_w)
    grads["ssd_conv_b"] = dconv_b
    for name, val in (("ssd_dt_bias", dbias_g), ("ssd_a_log", dalog_g), ("ssd_d", dd_g)):
        grads[name] = val[:, 0, :SSD_HPG].reshape(1, nh)
    d_zx = [dz] + [o[0] for o in conv_out] + [ddt]
    dw_parts = [_mm(f"ssd_dw_{tag}", [d], [u0], "tn", tm=512, tn=1024, out_dtypes=(PAYLOAD,))
                for tag, d in zip("zxbct", d_zx)]
    dw_parts[-1] = dw_parts[-1][:nh]
    blocks["w_in"] = _rows_split(jnp.concatenate(dw_parts, axis=0))
    sent = reduce_grads("ssd", {k: blocks[k] for k in ("w_in", "w_out", "conv_w")})
    w_parts = [w_in_t[r0:r0 + n] for r0, n in parts[:-1]] + [w_dt_t]
    du = _mm("ssd_dx", d_zx, w_parts, "nn", tm=256, tn=1024)
    grad_x, g_norm["mix_pre_norm"][0] = _norm_bwd("l0_in_bwd", dh, pre=(du, x, nrow("mix_pre_norm", 0)), after=sent)
    for k in norm:
        grads[k] = jnp.concatenate(g_norm[k], axis=0)
    return loss_row, grad_x, grads


def kernel(x, ssd_w_in, ssd_conv_w, ssd_conv_b, ssd_dt_bias, ssd_a_log, ssd_d, ssd_norm_w, ssd_w_out, attn_w_qkv, attn_b_qkv, attn_sinks, attn_w_o, attn_b_o, mlp_w_up, mlp_w_down, mix_pre_norm, mix_post_norm, ffn_pre_norm, ffn_post_norm, loss_target, m_ssd_w_in, m_ssd_conv_w, m_ssd_conv_b, m_ssd_dt_bias, m_ssd_a_log, m_ssd_d, m_ssd_norm_w, m_ssd_w_out, m_attn_w_qkv, m_attn_b_qkv, m_attn_sinks, m_attn_w_o, m_attn_b_o, m_mlp_w_up, m_mlp_w_down, m_mix_pre_norm, m_mix_post_norm, m_ffn_pre_norm, m_ffn_post_norm, v_ssd_w_in, v_ssd_conv_w, v_ssd_conv_b, v_ssd_dt_bias, v_ssd_a_log, v_ssd_d, v_ssd_norm_w, v_ssd_w_out, v_attn_w_qkv, v_attn_b_qkv, v_attn_sinks, v_attn_w_o, v_attn_b_o, v_mlp_w_up, v_mlp_w_down, v_mix_pre_norm, v_mix_post_norm, v_ffn_pre_norm, v_ffn_post_norm):
    given = dict(locals())
    w = {k: given[k] for k in WEIGHTS}
    mom_m = {k: given["m_" + k] for k in WEIGHTS}
    mom_v = {k: given["v_" + k] for k in WEIGHTS}
    w_it, m_it, v_it = _items(given), _items(given, "m_"), _items(given, "v_")

    order = [k for stage in GATHER_STAGES for k in stage]
    shards = [w_it[k].astype(PAYLOAD) if k in MATRIX_ITEMS else w_it[k] for k in order]
    g_send, g_recv, shards, lands, token = _gather_start("gather_start", shards)

    def weights_of_stage(s, after):
        first = sum(len(stage) for stage in GATHER_STAGES[:s])
        sl = slice(first, first + len(GATHER_STAGES[s]))
        srcs, got = _gather_wait(f"gather_wait{s}", g_send, g_recv, first, shards[sl], lands[sl], after)
        me = 4 * ix + 2 * iy + ic
        return {k: lax.dynamic_update_slice(land, src[None], (me,) + (0,) * src.ndim)
                for k, land, src in zip(GATHER_STAGES[s], got, srcs)}

    ix, iy, ic = lax.axis_index("x"), lax.axis_index("y"), lax.axis_index("c")
    in_flight = []

    def reduce_grads(tag, blocks):
        keys = list(blocks)
        started = _scatter_start(f"rs_start_{tag}", [blocks[k] for k in keys])
        in_flight.append((tag, keys, started))
        return started[-1]

    rep = {k: w[k] for k in REPLICATED}
    loss_row, grad_x, grads = _forward_backward(x[0], loss_target[0], rep, token, weights_of_stage, reduce_grads)

    def pack_rep(tree, last):
        flat = jnp.concatenate([tree[k].reshape(-1) for k in REPLICATED] + [last])
        return _pack_rows(flat, _round_up(-(-flat.shape[0] // LANES), 8), LANES)

    landed = {}
    me = 4 * ix + 2 * iy + ic

    def wait_group(group, after):
        tag, keys, (s_send, s_recv, srcs, s_lands, _) = group
        srcs, got = _scatter_wait(f"rs_wait_{tag}", s_send, s_recv, srcs, s_lands, after)
        for k, src, land in zip(keys, srcs, got):
            own = lax.dynamic_index_in_dim(src, me, 0, keepdims=True)
            landed[k] = lax.dynamic_update_slice(land, own, (me,) + (0,) * (land.ndim - 1))

    def adamw_item(k):
        return _sum_adamw(f"adamw_{k}", landed[k], w_it[k], m_it[k], v_it[k])

    def adamw_stack(name, keys):
        return _sum_adamw_layers(f"adamw_{name}", [landed[k] for k in keys], given[name], given["m_" + name],
                                 given["v_" + name])

    for group in in_flight[:-1]:
        wait_group(group, grad_x)
    done = {"mlp_w_up": adamw_stack("mlp_w_up", ("up0", "up1")),
            "mlp_w_down": adamw_stack("mlp_w_down", ("down0", "down1")),
            "attn_w_qkv": [o.T[None] for o in adamw_item("w_qkv")],
            "attn_w_o": [o[None] for o in adamw_item("w_o")],
            "attn_b_qkv": adamw_item("b_qkv"), "attn_b_o": adamw_item("b_o")}
    partials, = _all_gather("gather_small_grads", [pack_rep(grads, loss_row[0, :1])], done["mlp_w_down"][1])
    wait_group(in_flight[-1], partials)
    done["ssd_w_in"] = [o.T[None] for o in adamw_item("w_in")]
    for name, k in (("ssd_w_out", "w_out"), ("ssd_conv_w", "conv_w")):
        done[name] = [o[None] for o in adamw_item(k)]
    zero = jnp.zeros((1,), F32)
    rep_out = _sum_adamw("adamw_replicated", partials, pack_rep(w, zero), pack_rep(mom_m, zero), pack_rep(mom_v, zero))

    kinds = []
    for kind, r_arr in enumerate(rep_out):
        tree = {name: outs4[kind] for name, outs4 in done.items()}
        flat, off = r_arr.reshape(-1), 0
        for k in REPLICATED:
            tree[k] = flat[off:off + w[k].size].reshape(w[k].shape)
            off += w[k].size
        kinds.append(tree)
    loss = rep_out[0].reshape(-1)[off]
    outs = [loss, grad_x[None]]
    for tree in kinds:
        outs += [tree[k] for k in WEIGHTS]
    return tuple(outs)
```

```python
import functools

import jax
import jax.numpy as jnp
from jax import lax
from jax.experimental import pallas as pl
from jax.experimental.pallas import tpu as pltpu

F32 = jnp.float32
BF16 = jnp.bfloat16
PAYLOAD = jnp.bfloat16
HIGHEST = lax.Precision.HIGHEST
MESH = pl.DeviceIdType.MESH

NORM_EPS = 1e-6
SSD_HEAD_DIM = 64
SSD_N_GROUPS = 8
SSD_HPG = 4
SSD_D_STATE = 128
SSD_CONV_WIDTH = 4
SSD_CHUNK = 128
ATTN_HEAD_DIM = 64
ATTN_N_KV = 4
ATTN_REP = 4
ATTN_WINDOW = 128
ADAM_LR = 0.001
ADAM_B1 = 0.9
ADAM_B2 = 0.999
ADAM_EPS = 1e-08
ADAM_WD = 0.01
ADAM_STEP = 10

N_DEV = 8
LANES = 128
PACK_COLS = 1024
V7X_VMEM_LIMIT = 56 * 1024 * 1024

GW = SSD_HPG * SSD_HEAD_DIM
GC = GW + 2 * SSD_D_STATE


def _params(*sem):
    return pltpu.CompilerParams(dimension_semantics=sem, vmem_limit_bytes=V7X_VMEM_LIMIT)


def _tile(n, pref, mult=LANES):
    best = None
    t = mult
    while t <= min(n, pref):
        if n % t == 0:
            best = t
        t += mult
    return best if best is not None else n


def _round_up(n, m):
    return (n + m - 1) // m * m


def _acc(ref, val, first):
    @pl.when(first)
    def _():
        ref[...] = val

    @pl.when(jnp.logical_not(first))
    def _():
        ref[...] += val


def _dot(a, b):
    return lax.dot_general(a, b, (((1,), (0,)), ((), ())), preferred_element_type=F32)


def _dot_nt(a, b):
    return lax.dot_general(a, b, (((1,), (1,)), ((), ())), preferred_element_type=F32)


def _dot_tn(a, b):
    return lax.dot_general(a, b, (((0,), (0,)), ((), ())), preferred_element_type=F32)


def _dot_f32(a, b):
    return lax.dot_general(a, b, (((1,), (0,)), ((), ())), preferred_element_type=F32, precision=HIGHEST)


_DOTS = {"nn": _dot, "nt": _dot_nt, "tn": _dot_tn}


def _sigmoid(x):
    return 1.0 / (1.0 + jnp.exp(-x))


def _softplus(x):
    return jnp.maximum(x, 0.0) + jnp.log1p(jnp.exp(-jnp.abs(x)))


def _silu_grad(x, s):
    return s * (1.0 + x * (1.0 - s))


def _mm(name, a_list, b_list, mode, *, tm, tn, out_dtypes=(F32,), epilogue=None, tiles=(), rows=(), cols=(),
        col_blocks=False, n_use=None):
    npair = len(a_list)
    if mode == "tn":
        m = a_list[0].shape[1]
    else:
        m = a_list[0].shape[0]
    n = n_use if n_use is not None else (b_list[0].shape[0] if mode == "nt" else b_list[0].shape[1])
    tm = _tile(m, tm, LANES if mode == "tn" else 8)
    tn = _tile(n, tn)
    assert m % tm == 0 and n % tn == 0, (name, m, n, tm, tn)
    dot = _DOTS[mode]

    def body(*refs):
        a_refs = refs[:npair]
        b_refs = refs[npair:2 * npair]
        n_extra = len(tiles) + len(rows) + len(cols)
        e_refs = refs[2 * npair:2 * npair + n_extra]
        o_refs = refs[2 * npair + n_extra:]
        acc = None
        for ar, br in zip(a_refs, b_refs):
            d = dot(ar[...], br[...])
            acc = d if acc is None else acc + d
        outs = epilogue(acc, *[e[...] for e in e_refs]) if epilogue is not None else (acc,)
        for o, v in zip(o_refs, outs):
            o[...] = v.astype(o.dtype)

    in_specs = []
    for a in a_list:
        if mode == "tn":
            in_specs.append(pl.BlockSpec((a.shape[0], tm), lambda i, j: (0, i)))
        else:
            in_specs.append(pl.BlockSpec((tm, a.shape[1]), lambda i, j: (i, 0)))
    for b in b_list:
        if mode == "nt":
            in_specs.append(pl.BlockSpec((tn, b.shape[1]), lambda i, j: (j, 0)))
        else:
            in_specs.append(pl.BlockSpec((b.shape[0], tn), lambda i, j: (0, j)))
    in_specs += [pl.BlockSpec((tm, tn), lambda i, j: (i, j)) for _ in tiles]
    in_specs += [pl.BlockSpec((1, tn), lambda i, j: (0, j)) for _ in rows]
    in_specs += [pl.BlockSpec((tm, 1), lambda i, j: (i, 0)) for _ in cols]
    outs = pl.pallas_call(
        body,
        name=name,
        grid=(m // tm, n // tn),
        in_specs=in_specs,
        out_specs=[pl.BlockSpec((None, tm, tn), lambda i, j: (j, i, 0)) if col_blocks else
                   pl.BlockSpec((tm, tn), lambda i, j: (i, j)) for _ in out_dtypes],
        out_shape=[jax.ShapeDtypeStruct((n // tn, m, tn) if col_blocks else (m, n), dt) for dt in out_dtypes],
        compiler_params=_params("parallel", "parallel"),
    )(*a_list, *b_list, *tiles, *rows, *cols)
    return outs[0] if len(out_dtypes) == 1 else outs


def _rms(x, w):
    r = lax.rsqrt(jnp.mean(x * x, axis=-1, keepdims=True) + NORM_EPS)
    return x * r * w


def _rms_bwd(x, w, dy):
    r = lax.rsqrt(jnp.mean(x * x, axis=-1, keepdims=True) + NORM_EPS)
    xh = x * r
    g = dy * w
    dx = r * (g - xh * jnp.mean(g * xh, axis=-1, keepdims=True))
    return dx, dy * xh


def _row_specs(tr, d):
    return pl.BlockSpec((tr, d), lambda i: (i, 0)), pl.BlockSpec((1, d), lambda i: (0, 0))


def _prenorm(name, h, w, after):
    t, d = h.shape
    tr = _tile(t, 512, 8)
    row, vec = _row_specs(tr, d)

    def body(h_ref, w_ref, after_ref, u_ref):
        u_ref[...] = _rms(h_ref[...], w_ref[...]).astype(BF16)

    return pl.pallas_call(body, name=name, grid=(t // tr,),
                          in_specs=[row, vec, pl.BlockSpec((8, LANES), lambda i: (0, 0))], out_specs=row,
                          out_shape=jax.ShapeDtypeStruct((t, d), BF16), compiler_params=_params("parallel"))(
                              h, w, after)


def _post_pre(name, h, m, w_post, w_pre):
    t, d = h.shape
    tr = _tile(t, 512, 8)
    row, vec = _row_specs(tr, d)

    def body(h_ref, m_ref, wq_ref, wp_ref, hn_ref, u_ref):
        hn = h_ref[...] + _rms(m_ref[...], wq_ref[...])
        hn_ref[...] = hn
        u_ref[...] = _rms(hn, wp_ref[...]).astype(BF16)

    return pl.pallas_call(body, name=name, grid=(t // tr,), in_specs=[row, row, vec, vec], out_specs=[row, row],
                          out_shape=[jax.ShapeDtypeStruct((t, d), F32), jax.ShapeDtypeStruct((t, d), BF16)],
                          compiler_params=_params("parallel"))(h, m, w_post, w_pre)


def _final_loss(name, h, m, w_post, target):
    t, d = h.shape
    tr = _tile(t, 512, 8)
    row, vec = _row_specs(tr, d)

    def body(h_ref, m_ref, wq_ref, t_ref, dh_ref, loss_ref):
        err = h_ref[...] + _rms(m_ref[...], wq_ref[...]) - t_ref[...]
        dh_ref[...] = err * (1.0 / d)
        part = 0.5 * jnp.sum(jnp.mean(err * err, axis=-1, keepdims=True), axis=0, keepdims=True)
        _acc(loss_ref, jnp.broadcast_to(part, (1, LANES)), pl.program_id(0) == 0)

    return pl.pallas_call(body, name=name, grid=(t // tr,), in_specs=[row, row, vec, row],
                          out_specs=[row, pl.BlockSpec((1, LANES), lambda i: (0, 0))],
                          out_shape=[jax.ShapeDtypeStruct((t, d), F32), jax.ShapeDtypeStruct((1, LANES), F32)],
                          compiler_params=_params("arbitrary"))(h, m, w_post, target)


def _norm_bwd(name, dh, pre=None, post=None, after=None):
    t, d = dh.shape
    tr = _tile(t, 256, 8)
    row, vec = _row_specs(tr, d)
    has_pre, has_post = pre is not None, post is not None

    def body(*refs):
        it = iter(refs)
        dh_ref = next(it)
        if has_pre:
            du_ref, x_ref, wp_ref = next(it), next(it), next(it)
        if has_post:
            m_ref, wq_ref = next(it), next(it)
        if after is not None:
            next(it)
        first = pl.program_id(0) == 0
        dh_v = dh_ref[...]
        if has_pre:
            dhn_ref, dwp_ref = next(it), next(it)
            dx, dwr = _rms_bwd(x_ref[...], wp_ref[...], du_ref[...])
            dh_v = dh_v + dx
            dhn_ref[...] = dh_v
            _acc(dwp_ref, jnp.sum(dwr, axis=0, keepdims=True), first)
        if has_post:
            dm_ref, dwq_ref, dms_ref = next(it), next(it), next(it)
            dm, dwr = _rms_bwd(m_ref[...], wq_ref[...], dh_v)
            dm_ref[...] = dm.astype(BF16)
            _acc(dwq_ref, jnp.sum(dwr, axis=0, keepdims=True), first)
            _acc(dms_ref, jnp.sum(dm, axis=0, keepdims=True), first)

    ins, in_specs, out_specs, out_shape = [dh], [row], [], []
    if has_pre:
        ins += list(pre)
        in_specs += [row, row, vec]
        out_specs += [row, vec]
        out_shape += [jax.ShapeDtypeStruct((t, d), F32), jax.ShapeDtypeStruct((1, d), F32)]
    if has_post:
        ins += list(post)
        in_specs += [row, vec]
        out_specs += [row, vec, vec]
        out_shape += [jax.ShapeDtypeStruct((t, d), BF16), jax.ShapeDtypeStruct((1, d), F32),
                      jax.ShapeDtypeStruct((1, d), F32)]
    if after is not None:
        ins.append(after)
        in_specs.append(pl.BlockSpec((8, LANES), lambda i: (0, 0)))
    return pl.pallas_call(body, name=name, grid=(t // tr,), in_specs=in_specs, out_specs=out_specs,
                          out_shape=out_shape, compiler_params=_params("arbitrary"))(*ins)


HALO = 8


def _shift_later(cur, prev, s):
    rolled = pltpu.roll(cur, s, 0)
    row = lax.broadcasted_iota(jnp.int32, prev.shape, 0)
    first = jnp.where(row < s, pltpu.roll(prev, s, 0), rolled[0:HALO])
    return jnp.concatenate([first, rolled[HALO:]], axis=0)


def _shift_earlier(cur, nxt, s):
    tt = cur.shape[0]
    rolled = pltpu.roll(cur, tt - s, 0)
    row = lax.broadcasted_iota(jnp.int32, nxt.shape, 0)
    last = jnp.where(row >= HALO - s, pltpu.roll(nxt, HALO - s, 0), rolled[tt - HALO:])
    return jnp.concatenate([rolled[:tt - HALO], last], axis=0)


def _conv_fwd(zx, col0, n_ch, conv_w, conv_b):
    t = zx.shape[0]
    tc = _tile(n_ch, 512)
    tt = _tile(t, 512, 8)
    cb0 = col0 // tc
    assert col0 % tc == 0
    kw = SSD_CONV_WIDTH

    def body(x_ref, p_ref, w_ref, b_ref, o_ref):
        cur = x_ref[...]
        prev = jnp.where(pl.program_id(1) > 0, p_ref[...], 0.0)
        w = w_ref[...]
        acc = b_ref[...] + w[kw - 1:kw, :] * cur
        for k in range(kw - 1):
            acc = acc + w[k:k + 1, :] * _shift_later(cur, prev, kw - 1 - k)
        o_ref[...] = acc

    return pl.pallas_call(
        body, name="ssd_conv_fwd", grid=(n_ch // tc, t // tt),
        in_specs=[pl.BlockSpec((tt, tc), lambda j, i: (i, cb0 + j)),
                  pl.BlockSpec((HALO, tc), lambda j, i: (jnp.maximum(i * (tt // HALO) - 1, 0), cb0 + j)),
                  pl.BlockSpec((kw, tc), lambda j, i: (0, j)),
                  pl.BlockSpec((1, tc), lambda j, i: (0, j))],
        out_specs=pl.BlockSpec((tt, tc), lambda j, i: (i, j)),
        out_shape=jax.ShapeDtypeStruct((t, n_ch), F32),
        compiler_params=_params("parallel", "parallel"))(zx, zx, conv_w, conv_b)


def _conv_bwd(name, dpre, zx, col0, conv_w):
    t, n_ch = dpre.shape
    tc = _tile(n_ch, 512)
    tt = _tile(t, 512, 8)
    cb0 = col0 // tc
    kw = SSD_CONV_WIDTH
    nt = t // tt

    def body(d_ref, dn_ref, x_ref, p_ref, w_ref, dx_ref, dw_ref, db_ref):
        i = pl.program_id(1)
        d = d_ref[...]
        d_next = jnp.where(i < nt - 1, dn_ref[...], 0.0)
        x = x_ref[...]
        x_prev = jnp.where(i > 0, p_ref[...], 0.0)
        w = w_ref[...]
        dx = w[kw - 1:kw, :] * d
        for k in range(kw - 1):
            dx = dx + w[k:k + 1, :] * _shift_earlier(d, d_next, kw - 1 - k)
        dx_ref[...] = dx.astype(BF16)
        first = i == 0
        for k in range(kw):
            xs = x if k == kw - 1 else _shift_later(x, x_prev, kw - 1 - k)
            val = jnp.sum(d * xs, axis=0, keepdims=True)

            @pl.when(first)
            def _():
                dw_ref[k:k + 1, :] = val

            @pl.when(jnp.logical_not(first))
            def _():
                dw_ref[k:k + 1, :] += val
        _acc(db_ref, jnp.sum(d, axis=0, keepdims=True), first)

    return pl.pallas_call(
        body, name=name, grid=(n_ch // tc, nt),
        in_specs=[pl.BlockSpec((tt, tc), lambda j, i: (i, j)),
                  pl.BlockSpec((HALO, tc), lambda j, i: (jnp.minimum((i + 1) * (tt // HALO), t // HALO - 1), j)),
                  pl.BlockSpec((tt, tc), lambda j, i: (i, cb0 + j)),
                  pl.BlockSpec((HALO, tc), lambda j, i: (jnp.maximum(i * (tt // HALO) - 1, 0), cb0 + j)),
                  pl.BlockSpec((kw, tc), lambda j, i: (0, j))],
        out_specs=[pl.BlockSpec((tt, tc), lambda j, i: (i, j)),
                   pl.BlockSpec((kw, tc), lambda j, i: (0, j)),
                   pl.BlockSpec((1, tc), lambda j, i: (0, j))],
        out_shape=[jax.ShapeDtypeStruct((t, n_ch), BF16), jax.ShapeDtypeStruct((kw, n_ch), F32),
                   jax.ShapeDtypeStruct((1, n_ch), F32)],
        compiler_params=_params("parallel", "arbitrary"))(dpre, dpre, zx, zx, conv_w)


def _head_of_lane(shape, width):
    return lax.broadcasted_iota(jnp.int32, shape, len(shape) - 1) // width


def _expand(v, n_rows):
    head = _head_of_lane((n_rows, GW), SSD_HEAD_DIM)
    out = jnp.zeros((n_rows, GW), F32)
    for j in range(SSD_HPG):
        out = jnp.where(head == j, v[:, j:j + 1], out)
    return out


def _contract(v, n_rows):
    head = _head_of_lane((n_rows, GW), SSD_HEAD_DIM)
    lane = lax.broadcasted_iota(jnp.int32, (n_rows, LANES), 1)
    out = jnp.zeros((n_rows, LANES), F32)
    for j in range(SSD_HPG):
        s = jnp.sum(jnp.where(head == j, v, 0.0), axis=1, keepdims=True)
        out = jnp.where(lane == j, s, out)
    return out


def _ssd_dt_prep(zdt, bias, alog, ng):
    t = zdt.shape[0]
    q = SSD_CHUNK

    def body(z_ref, b_ref, a_ref, dt_ref, cum_ref, cumr_ref, sg_ref):
        raw = z_ref[...] + b_ref[...]
        dt = _softplus(raw)
        sgd = _sigmoid(raw)
        row = lax.broadcasted_iota(jnp.int32, (q, q), 0)
        col = lax.broadcasted_iota(jnp.int32, (q, q), 1)
        cum = _dot_f32((col <= row).astype(F32), dt * (-jnp.exp(a_ref[...])))
        cum_t = cum.T
        lane = lax.broadcasted_iota(jnp.int32, (q, LANES), 1)
        for g in range(ng):
            shift = (LANES - g * SSD_HPG) % LANES

            def group(v):
                return jnp.where(lane < SSD_HPG, pltpu.roll(v, shift, 1) if shift else v, 0.0)

            dt_ref[g] = group(dt)
            cum_ref[g] = group(cum)
            sg_ref[g] = group(sgd)
            cumr_ref[g] = (pltpu.roll(cum_t, shift, 0) if shift else cum_t)[0:8, :]

    cols = pl.BlockSpec((ng, q, LANES), lambda c: (0, c, 0))
    vec = pl.BlockSpec((1, LANES), lambda c: (0, 0))
    col_shape = jax.ShapeDtypeStruct((ng, t, LANES), F32)
    return pl.pallas_call(body, name="ssd_dt_prep", grid=(t // q,),
                          in_specs=[pl.BlockSpec((q, LANES), lambda c: (c, 0)), vec, vec],
                          out_specs=[cols, cols, pl.BlockSpec((ng, 8, q), lambda c: (0, 0, c)), cols],
                          out_shape=[col_shape, col_shape, jax.ShapeDtypeStruct((ng, 8, t), F32), col_shape],
                          compiler_params=_params("parallel"))(zdt, bias, alog)


def _ssd_common(pre, dt, cum, cum_r, alog_c):
    q = SSD_CHUNK
    sg = _sigmoid(pre)
    act = pre * sg
    xa = act[:, :GW]
    bm = act[:, GW:GW + SSD_D_STATE].astype(BF16)
    cm = act[:, GW + SSD_D_STATE:].astype(BF16)
    row = lax.broadcasted_iota(jnp.int32, (q, q), 0)
    col = lax.broadcasted_iota(jnp.int32, (q, q), 1)
    tril = col <= row
    a_c = -jnp.exp(alog_c)
    g = _dot_nt(cm, bm)
    dt_x = _expand(dt, q)
    xdt = xa * dt_x
    cl = cum[q - 1:q, :]
    e_c = jnp.exp(cl - cum)
    lam_c = jnp.exp(cum)
    return dict(sg=sg, xa=xa, bm=bm, cm=cm, tril=tril, row=row, col=col, dt=dt, a_c=a_c, cum=cum, cum_r=cum_r,
                g=g, dt_x=dt_x, xdt=xdt, cl=cl, e_c=e_c, lam_c=lam_c)


def _ssd_specs(nc, rev, ng):
    q = SSD_CHUNK
    b_off = ng * GW // SSD_D_STATE

    def ch(c):
        return nc - 1 - c if rev else c

    chunk_grp = [pl.BlockSpec((q, GW), lambda g, c: (ch(c), g)),
                 pl.BlockSpec((q, SSD_D_STATE), lambda g, c: (ch(c), b_off + g)),
                 pl.BlockSpec((q, SSD_D_STATE), lambda g, c: (ch(c), b_off + ng + g))]
    col_form = pl.BlockSpec((None, q, LANES), lambda g, c: (g, ch(c), 0))
    row_form = pl.BlockSpec((None, 8, q), lambda g, c: (g, 0, ch(c)))
    col_par = pl.BlockSpec((None, 1, LANES), lambda g, c: (g, 0, 0))
    row_par = pl.BlockSpec((None, 8, 1), lambda g, c: (g, 0, 0))
    y_spec = pl.BlockSpec((q, GW), lambda g, c: (ch(c), g))
    st_spec = pl.BlockSpec((None, None, GW, SSD_D_STATE), lambda g, c: (g, ch(c), 0, 0))
    bc_spec = pl.BlockSpec((q, SSD_D_STATE), lambda g, c: (ch(c), g))
    return chunk_grp, col_form, row_form, col_par, row_par, y_spec, st_spec, bc_spec


def _ssd_fwd(pre, dt_c, cum_c, cum_r, alog_c, dsk_c):
    t = pre.shape[0]
    ng = pre.shape[1] // GC
    q = SSD_CHUNK
    nc = t // q
    chunk_grp, col_form, row_form, col_par, row_par, y_spec, st_spec, _ = _ssd_specs(nc, False, ng)

    def body(px_ref, pb_ref, pc_ref, dt_ref, cum_ref, cumr_ref, ac_ref, dk_ref, y_ref, sp_ref, st_ref):
        @pl.when(pl.program_id(1) == 0)
        def _():
            st_ref[...] = jnp.zeros_like(st_ref)

        pre_v = jnp.concatenate([px_ref[...], pb_ref[...], pc_ref[...]], axis=1)
        v = _ssd_common(pre_v, dt_ref[...], cum_ref[...], cumr_ref[...], ac_ref[...])
        s0 = st_ref[...]
        sp_ref[...] = s0
        r = _dot_nt(v["cm"], s0.astype(BF16))
        y = _expand(v["lam_c"], q) * r + _expand(dk_ref[...], 1) * v["xa"]
        head = _head_of_lane((q, GW), SSD_HEAD_DIM)
        for j in range(SSD_HPG):
            diff = v["cum"][:, j:j + 1] - v["cum_r"][j:j + 1, :]
            w = (v["g"] * jnp.exp(jnp.where(v["tril"], diff, -jnp.inf))).astype(BF16)
            y = y + _dot(w, jnp.where(head == j, v["xdt"], 0.0).astype(BF16))
        y_ref[...] = y
        ds = _dot_tn((v["xdt"] * _expand(v["e_c"], q)).astype(BF16), v["bm"])
        for j in range(SSD_HPG):
            rows = slice(j * SSD_HEAD_DIM, (j + 1) * SSD_HEAD_DIM)
            st_ref[rows, :] = s0[rows, :] * jnp.exp(v["cum_r"][j:j + 1, q - 1:q]) + ds[rows, :]

    return pl.pallas_call(
        body, name="ssd_scan_fwd", grid=(ng, nc),
        in_specs=chunk_grp + [col_form, col_form, row_form, col_par, col_par],
        out_specs=[y_spec, st_spec],
        out_shape=[jax.ShapeDtypeStruct((t, ng * GW), F32), jax.ShapeDtypeStruct((ng, nc, GW, SSD_D_STATE), F32)],
        scratch_shapes=[pltpu.VMEM((GW, SSD_D_STATE), F32)],
        compiler_params=_params("parallel", "arbitrary"))(pre, pre, pre, dt_c, cum_c, cum_r, alog_c, dsk_c)


def _ssd_bwd(dy, pre, states, dt_c, cum_c, cum_r, sgd_c, alog_c, dsk_c):
    t = pre.shape[0]
    ng = pre.shape[1] // GC
    q = SSD_CHUNK
    nc = t // q
    chunk_grp, col_form, row_form, col_par, row_par, y_spec, st_spec, bc_spec = _ssd_specs(nc, True, ng)

    def body(dy_ref, px_ref, pb_ref, pc_ref, sp_ref, dt_ref, cum_ref, cumr_ref, sgd_ref, ac_ref, dk_ref,
             dpx_ref, dpb_ref, dpc_ref, ddt_ref, dbias_ref, dalog_ref, dd_ref, ds_ref):
        first = pl.program_id(1) == 0

        @pl.when(first)
        def _():
            ds_ref[...] = jnp.zeros_like(ds_ref)

        pre_v = jnp.concatenate([px_ref[...], pb_ref[...], pc_ref[...]], axis=1)
        v = _ssd_common(pre_v, dt_ref[...], cum_ref[...], cumr_ref[...], ac_ref[...])
        xa, bm, cm, xdt, cum, cum_r = v["xa"], v["bm"], v["cm"], v["xdt"], v["cum"], v["cum_r"]
        xdt_b = xdt.astype(BF16)
        dy_v = dy_ref[...]
        s0 = sp_ref[...]
        ds1 = ds_ref[...]
        s0b, ds1b = s0.astype(BF16), ds1.astype(BF16)
        head = _head_of_lane((q, GW), SSD_HEAD_DIM)
        lane = lax.broadcasted_iota(jnp.int32, (q, LANES), 1)
        lane1 = lax.broadcasted_iota(jnp.int32, (1, LANES), 1)
        lam_x = _expand(v["lam_c"], q)
        e_x = _expand(v["e_c"], q)

        dxa = _expand(dk_ref[...], 1) * dy_v
        dd = _contract(jnp.sum(dy_v * xa, axis=0, keepdims=True), 1)
        r = _dot_nt(cm, s0b)
        dcum = _contract(dy_v * r * lam_x, q)
        drb = (lam_x * dy_v).astype(BF16)
        dc = _dot(drb, s0b)
        ds0 = _dot_tn(drb, cm)
        extra = jnp.zeros((1, LANES), F32)
        for j in range(SSD_HPG):
            rows = slice(j * SSD_HEAD_DIM, (j + 1) * SSD_HEAD_DIM)
            lam_last = jnp.exp(cum_r[j:j + 1, q - 1:q])
            ds_ref[rows, :] = ds0[rows, :] + lam_last * ds1[rows, :]
            tot = jnp.sum(jnp.sum(ds1[rows, :] * s0[rows, :], axis=1, keepdims=True), axis=0, keepdims=True)
            extra = jnp.where(lane1 == j, lam_last * tot, extra)
        dv = _dot_nt(bm, ds1b)
        db = _dot((xdt * e_x).astype(BF16), ds1b)
        dxdt = e_x * dv
        dee = _contract(dv * xdt, q) * v["e_c"]
        dcum = dcum - dee
        extra = extra + jnp.sum(dee, axis=0, keepdims=True)
        dg = jnp.zeros((q, q), F32)
        for j in range(SSD_HPG):
            diff = cum[:, j:j + 1] - cum_r[j:j + 1, :]
            el = jnp.exp(jnp.where(v["tril"], diff, -jnp.inf))
            gl = v["g"] * el
            dym = jnp.where(head == j, dy_v, 0.0).astype(BF16)
            dwm = _dot_nt(dym, xdt_b)
            dxdt = dxdt + _dot_tn(gl.astype(BF16), dym)
            z = dwm * gl
            rk = jnp.sum(z, axis=1, keepdims=True) - jnp.sum(z.T, axis=1, keepdims=True)
            dcum = jnp.where(lane == j, dcum + rk, dcum)
            dg = dg + dwm * el
        dgb = dg.astype(BF16)
        dc = dc + _dot(dgb, bm)
        db = db + _dot_tn(dgb, cm)
        da = _dot_f32((v["row"] <= v["col"]).astype(F32), dcum) + extra
        ddt = _contract(dxdt * xa, q) + v["a_c"] * da
        dalog = jnp.sum(v["dt"] * da, axis=0, keepdims=True) * v["a_c"]
        dxa = dxa + v["dt_x"] * dxdt
        ddt_raw = jnp.where(lane < SSD_HPG, ddt * sgd_ref[...], 0.0)
        sgrad = _silu_grad(pre_v, v["sg"])
        dpx_ref[...] = dxa * sgrad[:, :GW]
        dpb_ref[...] = db * sgrad[:, GW:GW + SSD_D_STATE]
        dpc_ref[...] = dc * sgrad[:, GW + SSD_D_STATE:]
        ddt_ref[...] = ddt_raw
        _acc(dbias_ref, jnp.sum(ddt_raw, axis=0, keepdims=True), first)
        _acc(dalog_ref, jnp.where(lane1 < SSD_HPG, dalog, 0.0), first)
        _acc(dd_ref, dd, first)

    return pl.pallas_call(
        body, name="ssd_scan_bwd", grid=(ng, nc),
        in_specs=[y_spec] + chunk_grp + [st_spec, col_form, col_form, row_form, col_form, col_par, col_par],
        out_specs=[y_spec, bc_spec, bc_spec, col_form, col_par, col_par, col_par],
        out_shape=[jax.ShapeDtypeStruct((t, ng * GW), F32), jax.ShapeDtypeStruct((t, ng * SSD_D_STATE), F32),
                   jax.ShapeDtypeStruct((t, ng * SSD_D_STATE), F32), jax.ShapeDtypeStruct((ng, t, LANES), F32),
                   jax.ShapeDtypeStruct((ng, 1, LANES), F32), jax.ShapeDtypeStruct((ng, 1, LANES), F32),
                   jax.ShapeDtypeStruct((ng, 1, LANES), F32)],
        scratch_shapes=[pltpu.VMEM((GW, SSD_D_STATE), F32)],
        compiler_params=_params("parallel", "arbitrary"))(dy, pre, pre, pre, states, dt_c, cum_c, cum_r, sgd_c, alog_c,
                                                           dsk_c)


def _gate_norm_fwd(y, zx, norm_w):
    t, di = y.shape
    tr = _tile(t, 256, 8)
    ng = di // GW

    def body(y_ref, z_ref, w_ref, o_ref):
        z = z_ref[...]
        gate = y_ref[...] * (z * _sigmoid(z))
        w = w_ref[...]
        for g in range(ng):
            cols = slice(g * GW, (g + 1) * GW)
            gs = gate[:, cols]
            r = lax.rsqrt(jnp.mean(gs * gs, axis=-1, keepdims=True) + NORM_EPS)
            o_ref[:, cols] = (gs * r * w[:, cols]).astype(BF16)

    row = pl.BlockSpec((tr, di), lambda i: (i, 0))
    return pl.pallas_call(body, name="ssd_gate_norm_fwd", grid=(t // tr,),
                          in_specs=[row, row, pl.BlockSpec((1, di), lambda i: (0, 0))], out_specs=row,
                          out_shape=jax.ShapeDtypeStruct((t, di), BF16), compiler_params=_params("parallel"))(
                              y, zx, norm_w)


def _gate_norm_bwd(dyn, y, zx, norm_w):
    t, di = y.shape
    tr = _tile(t, 256, 8)
    ng = di // GW

    def body(d_ref, y_ref, z_ref, w_ref, dy_ref, dz_ref, dw_ref):
        z = z_ref[...]
        yv = y_ref[...]
        sg = _sigmoid(z)
        sz = z * sg
        gate = yv * sz
        w = w_ref[...]
        d = d_ref[...]
        dsz = _silu_grad(z, sg)
        dws = []
        for g in range(ng):
            cols = slice(g * GW, (g + 1) * GW)
            dg, dwr = _rms_bwd(gate[:, cols], w[:, cols], d[:, cols])
            dy_ref[:, cols] = dg * sz[:, cols]
            dz_ref[:, cols] = (dg * yv[:, cols] * dsz[:, cols]).astype(BF16)
            dws.append(jnp.sum(dwr, axis=0, keepdims=True))
        first = pl.program_id(0) == 0
        for g in range(ng):
            cols = slice(g * GW, (g + 1) * GW)

            @pl.when(first)
            def _():
                dw_ref[:, cols] = dws[g]

            @pl.when(jnp.logical_not(first))
            def _():
                dw_ref[:, cols] += dws[g]

    row = pl.BlockSpec((tr, di), lambda i: (i, 0))
    vec = pl.BlockSpec((1, di), lambda i: (0, 0))
    return pl.pallas_call(body, name="ssd_gate_norm_bwd", grid=(t // tr,), in_specs=[row, row, row, vec],
                          out_specs=[row, row, vec],
                          out_shape=[jax.ShapeDtypeStruct((t, di), F32), jax.ShapeDtypeStruct((t, di), BF16),
                                     jax.ShapeDtypeStruct((1, di), F32)],
                          compiler_params=_params("arbitrary"))(dyn, y, zx, norm_w)


def _attn_mask(n):
    w = ATTN_WINDOW
    qpos = lax.broadcasted_iota(jnp.int32, (w, 2 * w), 0) + w
    kpos = lax.broadcasted_iota(jnp.int32, (w, 2 * w), 1)
    rel = qpos - kpos
    return (rel >= 0) & (rel < w) & jnp.logical_not((n == 0) & (kpos < w))


def _attn_probs(qh, kbh, mask, sink):
    s = _dot_nt(qh, kbh) * (ATTN_HEAD_DIM ** -0.5)
    s = jnp.where(mask, s, -jnp.inf)
    m = jnp.maximum(jnp.max(s, axis=-1, keepdims=True), sink)
    e = jnp.exp(s - m)
    es = jnp.exp(sink - m)
    inv = 1.0 / (jnp.sum(e, axis=-1, keepdims=True) + es)
    return e * inv, es * inv


def _attn_fwd(qkv, sinks):
    t = qkv.shape[0]
    w, hd = ATTN_WINDOW, ATTN_HEAD_DIM
    kd = ATTN_N_KV * hd
    qd = ATTN_REP * kd
    nb = t // w

    def body(q_ref, kc_ref, vc_ref, kp_ref, vp_ref, s_ref, o_ref):
        n = pl.program_id(0)
        mask = _attn_mask(n)
        q = q_ref[...]
        kb = jnp.concatenate([kp_ref[...], kc_ref[...]], axis=0)
        vb = jnp.concatenate([vp_ref[...], vc_ref[...]], axis=0)
        sk = s_ref[...]
        for kv in range(ATTN_N_KV):
            kbh = kb[:, kv * hd:(kv + 1) * hd]
            vbh = vb[:, kv * hd:(kv + 1) * hd]
            for rep in range(ATTN_REP):
                h = kv * ATTN_REP + rep
                p, _ = _attn_probs(q[:, h * hd:(h + 1) * hd], kbh, mask, sk[:, h:h + 1])
                o_ref[:, h * hd:(h + 1) * hd] = _dot(p.astype(BF16), vbh).astype(BF16)

    prev = lambda n: jnp.maximum(n - 1, 0)
    return pl.pallas_call(
        body, name="attn_fwd", grid=(nb,),
        in_specs=[pl.BlockSpec((w, qd), lambda n: (n, 0)),
                  pl.BlockSpec((w, kd), lambda n: (n, ATTN_REP)),
                  pl.BlockSpec((w, kd), lambda n: (n, ATTN_REP + 1)),
                  pl.BlockSpec((w, kd), lambda n: (prev(n), ATTN_REP)),
                  pl.BlockSpec((w, kd), lambda n: (prev(n), ATTN_REP + 1)),
                  pl.BlockSpec((1, sinks.shape[1]), lambda n: (0, 0))],
        out_specs=pl.BlockSpec((w, qd), lambda n: (n, 0)),
        out_shape=jax.ShapeDtypeStruct((t, qd), BF16),
        compiler_params=_params("parallel"))(qkv, qkv, qkv, qkv, qkv, sinks)


def _attn_bwd(qkv, do, sinks):
    t = qkv.shape[0]
    w, hd = ATTN_WINDOW, ATTN_HEAD_DIM
    kd = ATTN_N_KV * hd
    qd = ATTN_REP * kd
    nq = ATTN_N_KV * ATTN_REP
    nb = t // w

    def body(q_ref, kc_ref, vc_ref, kp_ref, vp_ref, do_ref, s_ref,
             dq_ref, dk_ref, dv_ref, bq_ref, bk_ref, bv_ref, dsk_ref, ck_ref, cv_ref):
        n = pl.program_id(0)
        first = n == 0

        @pl.when(first)
        def _():
            ck_ref[...] = jnp.zeros_like(ck_ref)
            cv_ref[...] = jnp.zeros_like(cv_ref)
            bq_ref[...] = jnp.zeros_like(bq_ref)
            bk_ref[...] = jnp.zeros_like(bk_ref)
            bv_ref[...] = jnp.zeros_like(bv_ref)
            dsk_ref[...] = jnp.zeros_like(dsk_ref)

        @pl.when(n < nb)
        def _():
            mask = _attn_mask(n)
            q = q_ref[...]
            dov = do_ref[...]
            kb = jnp.concatenate([kp_ref[...], kc_ref[...]], axis=0)
            vb = jnp.concatenate([vp_ref[...], vc_ref[...]], axis=0)
            sk = s_ref[...]
            lane = lax.broadcasted_iota(jnp.int32, (1, nq), 1)
            dsk = jnp.zeros((1, nq), F32)
            dq_parts, dk_parts, dv_parts = [], [], []
            for kv in range(ATTN_N_KV):
                kbh = kb[:, kv * hd:(kv + 1) * hd]
                vbh = vb[:, kv * hd:(kv + 1) * hd]
                dkh = jnp.zeros((2 * w, hd), F32)
                dvh = jnp.zeros((2 * w, hd), F32)
                for rep in range(ATTN_REP):
                    h = kv * ATTN_REP + rep
                    qh = q[:, h * hd:(h + 1) * hd]
                    doh = dov[:, h * hd:(h + 1) * hd]
                    p, ps = _attn_probs(qh, kbh, mask, sk[:, h:h + 1])
                    pb = p.astype(BF16)
                    dp = _dot_nt(doh, vbh)
                    delta = jnp.sum(p * dp, axis=-1, keepdims=True)
                    dsc = (p * (dp - delta) * (hd ** -0.5)).astype(BF16)
                    dq_parts.append(_dot(dsc, kbh))
                    dkh = dkh + _dot_tn(dsc, qh)
                    dvh = dvh + _dot_tn(pb, doh)
                    dsk = jnp.where(lane == h, -jnp.sum(ps * delta, axis=0, keepdims=True), dsk)
                dk_parts.append(dkh)
                dv_parts.append(dvh)
            dq = jnp.concatenate(dq_parts, axis=1)
            dkb = jnp.concatenate(dk_parts, axis=1)
            dvb = jnp.concatenate(dv_parts, axis=1)
            dq_ref[...] = dq.astype(BF16)
            bq_ref[...] += jnp.sum(dq, axis=0, keepdims=True)
            dsk_ref[...] += dsk
            dk_prev = ck_ref[...] + dkb[:w, :]
            dv_prev = cv_ref[...] + dvb[:w, :]
            dk_ref[...] = dk_prev.astype(BF16)
            dv_ref[...] = dv_prev.astype(BF16)

            @pl.when(n > 0)
            def _():
                bk_ref[...] += jnp.sum(dk_prev, axis=0, keepdims=True)
                bv_ref[...] += jnp.sum(dv_prev, axis=0, keepdims=True)

            ck_ref[...] = dkb[w:, :]
            cv_ref[...] = dvb[w:, :]

        @pl.when(n == nb)
        def _():
            dk_ref[...] = ck_ref[...].astype(BF16)
            dv_ref[...] = cv_ref[...].astype(BF16)
            bk_ref[...] += jnp.sum(ck_ref[...], axis=0, keepdims=True)
            bv_ref[...] += jnp.sum(cv_ref[...], axis=0, keepdims=True)

    cur = lambda n: jnp.minimum(n, nb - 1)
    prev = lambda n: jnp.maximum(jnp.minimum(n, nb - 1) - 1, 0)
    late = lambda n: jnp.maximum(n - 1, 0)
    vec = lambda width: pl.BlockSpec((1, width), lambda n: (0, 0))
    return pl.pallas_call(
        body, name="attn_bwd", grid=(nb + 1,),
        in_specs=[pl.BlockSpec((w, qd), lambda n: (cur(n), 0)),
                  pl.BlockSpec((w, kd), lambda n: (cur(n), ATTN_REP)),
                  pl.BlockSpec((w, kd), lambda n: (cur(n), ATTN_REP + 1)),
                  pl.BlockSpec((w, kd), lambda n: (prev(n), ATTN_REP)),
                  pl.BlockSpec((w, kd), lambda n: (prev(n), ATTN_REP + 1)),
                  pl.BlockSpec((w, qd), lambda n: (cur(n), 0)),
                  vec(nq)],
        out_specs=[pl.BlockSpec((w, qd), lambda n: (cur(n), 0)),
                   pl.BlockSpec((w, kd), lambda n: (late(n), 0)),
                   pl.BlockSpec((w, kd), lambda n: (late(n), 0)),
                   vec(qd), vec(kd), vec(kd), vec(nq)],
        out_shape=[jax.ShapeDtypeStruct((t, qd), BF16), jax.ShapeDtypeStruct((t, kd), BF16),
                   jax.ShapeDtypeStruct((t, kd), BF16), jax.ShapeDtypeStruct((1, qd), F32),
                   jax.ShapeDtypeStruct((1, kd), F32), jax.ShapeDtypeStruct((1, kd), F32),
                   jax.ShapeDtypeStruct((1, nq), F32)],
        scratch_shapes=[pltpu.VMEM((w, kd), F32), pltpu.VMEM((w, kd), F32)],
        compiler_params=_params("arbitrary"))(qkv, qkv, qkv, qkv, qkv, do, sinks)


def _attn_mask_t(n):
    w = ATTN_WINDOW
    kpos = lax.broadcasted_iota(jnp.int32, (2 * w, ATTN_REP * w), 0)
    qpos = lax.broadcasted_iota(jnp.int32, (2 * w, ATTN_REP * w), 1) % w + w
    rel = qpos - kpos
    return (rel >= 0) & (rel < w) & jnp.logical_not((n == 0) & (kpos < w))


def _attn_probs_t(qts, ktb, mask, sink):
    s = _dot_tn(ktb, qts) * (ATTN_HEAD_DIM ** -0.5)
    s = jnp.where(mask, s, -jnp.inf)
    m = jnp.maximum(jnp.max(s, axis=0, keepdims=True), sink)
    e = jnp.exp(s - m)
    es = jnp.exp(sink - m)
    inv = 1.0 / (jnp.sum(e, axis=0, keepdims=True) + es)
    return e * inv, es * inv


def _attn_blocks_t(kv, q_ref, kc_ref, vc_ref, kp_ref, vp_ref):
    hd = ATTN_HEAD_DIM
    rows = slice(kv * hd, (kv + 1) * hd)
    ktb = jnp.concatenate([kp_ref[rows, :], kc_ref[rows, :]], axis=1)
    vtb = jnp.concatenate([vp_ref[rows, :], vc_ref[rows, :]], axis=1)
    qts = jnp.concatenate([q_ref[(kv * ATTN_REP + r) * hd:(kv * ATTN_REP + r + 1) * hd, :]
                           for r in range(ATTN_REP)], axis=1)
    return qts, ktb, vtb


def _attn_specs_t(nb, cur, prev):
    w, hd = ATTN_WINDOW, ATTN_HEAD_DIM
    kd = ATTN_N_KV * hd
    qd = ATTN_REP * kd
    return [pl.BlockSpec((qd, w), lambda n: (0, cur(n))),
            pl.BlockSpec((kd, w), lambda n: (ATTN_REP, cur(n))),
            pl.BlockSpec((kd, w), lambda n: (ATTN_REP + 1, cur(n))),
            pl.BlockSpec((kd, w), lambda n: (ATTN_REP, prev(n))),
            pl.BlockSpec((kd, w), lambda n: (ATTN_REP + 1, prev(n)))]


def _attn_fwd_t(qkv_t, sinks_rep):
    t = qkv_t.shape[1]
    w, hd = ATTN_WINDOW, ATTN_HEAD_DIM
    qd = ATTN_N_KV * ATTN_REP * hd
    nb = t // w

    def body(q_ref, kc_ref, vc_ref, kp_ref, vp_ref, s_ref, o_ref):
        mask = _attn_mask_t(pl.program_id(0))
        for kv in range(ATTN_N_KV):
            qts, ktb, vtb = _attn_blocks_t(kv, q_ref, kc_ref, vc_ref, kp_ref, vp_ref)
            p, _ = _attn_probs_t(qts, ktb, mask, s_ref[kv])
            ots = _dot(vtb, p.astype(BF16))
            for r in range(ATTN_REP):
                h = kv * ATTN_REP + r
                o_ref[h * hd:(h + 1) * hd, :] = ots[:, r * w:(r + 1) * w].astype(BF16)

    return pl.pallas_call(
        body, name="attn_fwd", grid=(nb,),
        in_specs=_attn_specs_t(nb, lambda n: n, lambda n: jnp.maximum(n - 1, 0)) + [
            pl.BlockSpec(sinks_rep.shape, lambda n: (0, 0, 0))],
        out_specs=pl.BlockSpec((qd, w), lambda n: (0, n)),
        out_shape=jax.ShapeDtypeStruct((qd, t), BF16),
        compiler_params=_params("parallel"))(qkv_t, qkv_t, qkv_t, qkv_t, qkv_t, sinks_rep)


def _attn_bwd_t(qkv_t, do_t, sinks_rep):
    t = qkv_t.shape[1]
    w, hd = ATTN_WINDOW, ATTN_HEAD_DIM
    kd = ATTN_N_KV * hd
    qd = ATTN_REP * kd
    nq = ATTN_N_KV * ATTN_REP
    nb = t // w
    rows_all = qd + 2 * kd

    def body(q_ref, kc_ref, vc_ref, kp_ref, vp_ref, do_ref, s_ref, dqkv_ref, bsum_ref, dsk_ref,
             carry_ref, new_ref, bacc_ref, sacc_ref):
        n = pl.program_id(0)

        @pl.when(n == 0)
        def _():
            carry_ref[...] = jnp.zeros_like(carry_ref)
            bacc_ref[...] = jnp.zeros_like(bacc_ref)
            sacc_ref[...] = jnp.zeros_like(sacc_ref)

        @pl.when(n < nb)
        def _():
            mask = _attn_mask_t(n)
            for kv in range(ATTN_N_KV):
                qts, ktb, vtb = _attn_blocks_t(kv, q_ref, kc_ref, vc_ref, kp_ref, vp_ref)
                dots = jnp.concatenate([do_ref[(kv * ATTN_REP + r) * hd:(kv * ATTN_REP + r + 1) * hd, :]
                                        for r in range(ATTN_REP)], axis=1)
                p, ps = _attn_probs_t(qts, ktb, mask, s_ref[kv])
                dpt = _dot_tn(vtb, dots)
                delta = jnp.sum(p * dpt, axis=0, keepdims=True)
                dst = (p * (dpt - delta) * (hd ** -0.5)).astype(BF16)
                dqts = _dot(ktb, dst)
                for r in range(ATTN_REP):
                    h = kv * ATTN_REP + r
                    new_ref[h * hd:(h + 1) * hd, :] = dqts[:, r * w:(r + 1) * w]
                dktb = _dot_nt(qts, dst)
                dvtb = _dot_nt(dots, p.astype(BF16))
                krows = slice(qd + kv * hd, qd + (kv + 1) * hd)
                vrows = slice(qd + kd + kv * hd, qd + kd + (kv + 1) * hd)
                carry_ref[krows, :] += dktb[:, :w]
                carry_ref[vrows, :] += dvtb[:, :w]
                new_ref[krows, :] = dktb[:, w:]
                new_ref[vrows, :] = dvtb[:, w:]
                sacc_ref[kv] += -(ps * delta)

        @pl.when(n >= 1)
        def _():
            done = carry_ref[...]
            dqkv_ref[...] = done.astype(BF16)
            bacc_ref[...] += done

        @pl.when(n < nb)
        def _():
            carry_ref[...] = new_ref[...]

        @pl.when(n == nb)
        def _():
            bsum_ref[...] = jnp.sum(bacc_ref[...], axis=1, keepdims=True)
            lane = lax.broadcasted_iota(jnp.int32, (1, nq), 1)
            dsk = jnp.zeros((1, nq), F32)
            for kv in range(ATTN_N_KV):
                acc = sacc_ref[kv]
                for r in range(ATTN_REP):
                    tot = jnp.sum(acc[:, r * w:(r + 1) * w], axis=1, keepdims=True)
                    dsk = jnp.where(lane == kv * ATTN_REP + r, tot, dsk)
            dsk_ref[...] = dsk

    cur = lambda n: jnp.minimum(n, nb - 1)
    prev = lambda n: jnp.maximum(jnp.minimum(n, nb - 1) - 1, 0)
    return pl.pallas_call(
        body, name="attn_bwd", grid=(nb + 1,),
        in_specs=_attn_specs_t(nb, cur, prev) + [pl.BlockSpec((qd, w), lambda n: (0, cur(n))),
                                                 pl.BlockSpec(sinks_rep.shape, lambda n: (0, 0, 0))],
        out_specs=[pl.BlockSpec((rows_all, w), lambda n: (0, jnp.maximum(n - 1, 0))),
                   pl.BlockSpec((rows_all, 1), lambda n: (0, 0)),
                   pl.BlockSpec((1, nq), lambda n: (0, 0))],
        out_shape=[jax.ShapeDtypeStruct((rows_all, t), BF16), jax.ShapeDtypeStruct((rows_all, 1), F32),
                   jax.ShapeDtypeStruct((1, nq), F32)],
        scratch_shapes=[pltpu.VMEM((rows_all, w), F32), pltpu.VMEM((rows_all, w), F32),
                        pltpu.VMEM((rows_all, w), F32), pltpu.VMEM(sinks_rep.shape, F32)],
        compiler_params=_params("arbitrary"))(qkv_t, qkv_t, qkv_t, qkv_t, qkv_t, do_t, sinks_rep)


HBM_SPEC = pl.BlockSpec(memory_space=pl.ANY)
HBM_ONLY = pl.BlockSpec(memory_space=pltpu.HBM)


def _comm_call(name, body, ins, out_shapes, n_sems):
    return pl.pallas_call(
        body, name=name, in_specs=[HBM_SPEC] * len(ins), out_specs=[HBM_SPEC] * len(out_shapes),
        out_shape=out_shapes,
        scratch_shapes=[pltpu.SemaphoreType.DMA((s,)) for s in n_sems])(*ins)


def _all_gather(name, shards, after):
    n = len(shards)

    def body(*refs):
        x_refs, out_refs = refs[:n], refs[n + 1:2 * n + 1]
        send_sems, recv_sems, local_sems = refs[2 * n + 1:]
        x, y, c = lax.axis_index("x"), lax.axis_index("y"), lax.axis_index("c")
        me, sibling = (x, y, c), (x, y, 1 - c)
        chips = [(1 - x, y), (x, 1 - y), (1 - x, 1 - y)]

        def slot(i, px, py, pc):
            return out_refs[i].at[4 * px + 2 * py + pc]

        def copy(k, i, block, to, src=None):
            return pltpu.make_async_remote_copy(
                src_ref=slot(i, *block) if src is None else src, dst_ref=slot(i, *block),
                send_sem=send_sems.at[k * n + i], recv_sem=recv_sems.at[k * n + i], device_id=to,
                device_id_type=MESH)

        mine = [pltpu.make_async_copy(x_refs[i], slot(i, *me), local_sems.at[i]) for i in range(n)]
        first = []
        for i in range(n):
            mine[i].start()
            first.append(copy(0, i, me, sibling, src=x_refs[i]))
            first += [copy(1 + j, i, me, (*chip, c), src=x_refs[i]) for j, chip in enumerate(chips)]
        for cp in first:
            cp.start()
        passed = []
        for i in range(n):
            for j, chip in enumerate(chips):
                copy(1 + j, i, (*chip, c), me).wait_recv()
                passed.append(copy(4 + j, i, (*chip, c), sibling))
                passed[-1].start()
        for i in range(n):
            copy(0, i, sibling, me).wait_recv()
            for j, chip in enumerate(chips):
                copy(4 + j, i, (*chip, 1 - c), me).wait_recv()
        for cp in first + passed:
            cp.wait_send()
        for cp in mine:
            cp.wait()

    outs = [jax.ShapeDtypeStruct((N_DEV,) + s.shape, s.dtype) for s in shards]
    return _comm_call(name, body, list(shards) + [after], outs, (7 * n, 7 * n, n))


SEM_SPEC = pl.BlockSpec(memory_space=pltpu.SEMAPHORE)
SPLIT_COPY_EFFECT = pltpu.SideEffectType.DATAFLOW_SIDE_EFFECTING


def _in_hbm(a):
    return pltpu.with_memory_space_constraint(a, pltpu.HBM)


def _split_start(name, body, srcs, lands, n_sems):
    n = len(srcs)
    bufs = [_in_hbm(a) for a in list(srcs) + list(lands)]
    outs = pl.pallas_call(
        body, name=name,
        out_shape=(pltpu.SemaphoreType.DMA((n_sems,)), pltpu.SemaphoreType.DMA((n_sems,)),
                   *[pltpu.HBM(a.shape, a.dtype) for a in bufs], jax.ShapeDtypeStruct((8, LANES), F32)),
        in_specs=[HBM_ONLY] * (2 * n),
        out_specs=(SEM_SPEC, SEM_SPEC, *[HBM_ONLY] * (2 * n), pl.BlockSpec(memory_space=pltpu.VMEM)),
        input_output_aliases={i: 2 + i for i in range(2 * n)},
        compiler_params=pltpu.CompilerParams(has_side_effects=SPLIT_COPY_EFFECT))(*bufs)
    return outs[0], outs[1], list(outs[2:2 + n]), list(outs[2 + n:2 + 2 * n]), outs[-1]


def _split_wait(name, body, send_sems, recv_sems, srcs, lands, after):
    n = len(srcs)
    outs = pl.pallas_call(
        body, name=name,
        out_shape=[pltpu.HBM(a.shape, a.dtype) for a in list(srcs) + list(lands)],
        in_specs=[HBM_ONLY] * (2 * n) + [SEM_SPEC, SEM_SPEC, HBM_SPEC],
        out_specs=[HBM_ONLY] * (2 * n),
        input_output_aliases={i: i for i in range(2 * n)},
        compiler_params=pltpu.CompilerParams(has_side_effects=SPLIT_COPY_EFFECT))(
            *srcs, *lands, send_sems, recv_sems, after)
    return list(outs[:n]), list(outs[n:])


N_PEERS = N_DEV - 1


def _gather_peers():
    x, y, c = lax.axis_index("x"), lax.axis_index("y"), lax.axis_index("c")
    flips = [(fx, fy, fc) for fx in (0, 1) for fy in (0, 1) for fc in (0, 1) if fx or fy or fc]
    return [(1 - x if fx else x, 1 - y if fy else y, 1 - c if fc else c) for fx, fy, fc in flips]


def _block_id(dev):
    return 4 * dev[0] + 2 * dev[1] + dev[2]


def _gather_start(name, shards):
    n = len(shards)

    def body(*refs):
        x_refs, land_refs = refs[:n], refs[n:2 * n]
        send_sems, recv_sems, token = refs[2 * n], refs[2 * n + 1], refs[-1]
        me = (lax.axis_index("x"), lax.axis_index("y"), lax.axis_index("c"))
        for i in range(n):
            for k, peer in enumerate(_gather_peers()):
                pltpu.make_async_remote_copy(
                    src_ref=x_refs[i], dst_ref=land_refs[i].at[_block_id(me)],
                    send_sem=send_sems.at[N_PEERS * i + k], recv_sem=recv_sems.at[N_PEERS * i + k],
                    device_id=peer, device_id_type=MESH).start()
        token[...] = jnp.zeros_like(token)

    lands = [lax.empty((N_DEV,) + s.shape, s.dtype) for s in shards]
    return _split_start(name, body, shards, lands, N_PEERS * n)


def _gather_wait(name, send_sems, recv_sems, first, shards, lands, after):
    n = len(shards)

    def body(*refs):
        x_refs, land_refs = refs[:n], refs[n:2 * n]
        send_sems, recv_sems = refs[2 * n], refs[2 * n + 1]
        for i in range(n):
            for k, peer in enumerate(_gather_peers()):
                cp = pltpu.make_async_remote_copy(
                    src_ref=x_refs[i], dst_ref=land_refs[i].at[_block_id(peer)],
                    send_sem=send_sems.at[N_PEERS * (first + i) + k],
                    recv_sem=recv_sems.at[N_PEERS * (first + i) + k],
                    device_id=peer, device_id_type=MESH)
                cp.wait_send()
                cp.wait_recv()

    return _split_wait(name, body, send_sems, recv_sems, shards, lands, after)


def _gather_forward(name, lands, shards):
    n = len(shards)

    def body(*refs):
        x_refs, out_refs = refs[n:2 * n], refs[2 * n:3 * n]
        send_sems, recv_sems, local_sems = refs[3 * n:]
        x, y, c = lax.axis_index("x"), lax.axis_index("y"), lax.axis_index("c")
        chips = [(1 - x, y), (x, 1 - y), (1 - x, 1 - y)]
        mine = [pltpu.make_async_copy(x_refs[i], out_refs[i].at[_block_id((x, y, c))], local_sems.at[i])
                for i in range(n)]
        passed = [pltpu.make_async_remote_copy(
            src_ref=out_refs[i].at[_block_id((*chip, c))], dst_ref=out_refs[i].at[_block_id((*chip, c))],
            send_sem=send_sems.at[3 * i + j], recv_sem=recv_sems.at[3 * i + j], device_id=(x, y, 1 - c),
            device_id_type=MESH) for i in range(n) for j, chip in enumerate(chips)]
        for cp in mine + passed:
            cp.start()
        for i in range(n):
            for j, chip in enumerate(chips):
                pltpu.make_async_remote_copy(
                    src_ref=out_refs[i].at[_block_id((*chip, c))], dst_ref=out_refs[i].at[_block_id((*chip, 1 - c))],
                    send_sem=send_sems.at[3 * i + j], recv_sem=recv_sems.at[3 * i + j], device_id=(x, y, 1 - c),
                    device_id_type=MESH).wait()
        for cp in mine:
            cp.wait()

    return pl.pallas_call(
        body, name=name, in_specs=[HBM_SPEC] * (2 * n), out_specs=[HBM_SPEC] * n,
        out_shape=[jax.ShapeDtypeStruct(a.shape, a.dtype) for a in lands],
        input_output_aliases={i: i for i in range(n)},
        scratch_shapes=[pltpu.SemaphoreType.DMA((3 * n,)), pltpu.SemaphoreType.DMA((3 * n,)),
                        pltpu.SemaphoreType.DMA((n,))])(*lands, *shards)


def _chip_peers():
    x, y, c = lax.axis_index("x"), lax.axis_index("y"), lax.axis_index("c")
    return [(1 - x, y, c), (x, 1 - y, c), (1 - x, 1 - y, c)]


def _chip_start(name, blocks):
    n = len(blocks)

    def body(*refs):
        p_refs, land_refs = refs[:n], refs[n:2 * n]
        send_sems, recv_sems, token = refs[2 * n], refs[2 * n + 1], refs[-1]
        for i in range(n):
            for j, peer in enumerate(_chip_peers()):
                pltpu.make_async_remote_copy(
                    src_ref=p_refs[i].at[j], dst_ref=land_refs[i].at[j], send_sem=send_sems.at[3 * i + j],
                    recv_sem=recv_sems.at[3 * i + j], device_id=peer, device_id_type=MESH).start()
        token[...] = jnp.zeros_like(token)

    lands = [lax.empty(b.shape, b.dtype) for b in blocks]
    return _split_start(name, body, blocks, lands, 3 * n)


def _chip_wait(name, send_sems, recv_sems, blocks, lands, after):
    n = len(blocks)

    def body(*refs):
        p_refs, land_refs = refs[:n], refs[n:2 * n]
        send_sems, recv_sems = refs[2 * n], refs[2 * n + 1]
        for i in range(n):
            for j, peer in enumerate(_chip_peers()):
                cp = pltpu.make_async_remote_copy(
                    src_ref=p_refs[i].at[j], dst_ref=land_refs[i].at[j], send_sem=send_sems.at[3 * i + j],
                    recv_sem=recv_sems.at[3 * i + j], device_id=peer, device_id_type=MESH)
                cp.wait_send()
                cp.wait_recv()

    return _split_wait(name, body, send_sems, recv_sems, blocks, lands, after)


def _scatter_start(name, blocks):
    n = len(blocks)

    def body(*refs):
        b_refs, land_refs = refs[:n], refs[n:2 * n]
        send_sems, recv_sems, token = refs[2 * n], refs[2 * n + 1], refs[-1]
        me = (lax.axis_index("x"), lax.axis_index("y"), lax.axis_index("c"))
        for i in range(n):
            for k, peer in enumerate(_gather_peers()):
                pltpu.make_async_remote_copy(
                    src_ref=b_refs[i].at[_block_id(peer)], dst_ref=land_refs[i].at[_block_id(me)],
                    send_sem=send_sems.at[N_PEERS * i + k], recv_sem=recv_sems.at[N_PEERS * i + k],
                    device_id=peer, device_id_type=MESH).start()
        token[...] = jnp.zeros_like(token)

    lands = [lax.empty(b.shape, b.dtype) for b in blocks]
    return _split_start(name, body, blocks, lands, N_PEERS * n)


def _scatter_wait(name, send_sems, recv_sems, blocks, lands, after):
    n = len(blocks)

    def body(*refs):
        b_refs, land_refs = refs[:n], refs[n:2 * n]
        send_sems, recv_sems = refs[2 * n], refs[2 * n + 1]
        for i in range(n):
            for k, peer in enumerate(_gather_peers()):
                cp = pltpu.make_async_remote_copy(
                    src_ref=b_refs[i].at[_block_id(peer)], dst_ref=land_refs[i].at[_block_id(peer)],
                    send_sem=send_sems.at[N_PEERS * i + k], recv_sem=recv_sems.at[N_PEERS * i + k],
                    device_id=peer, device_id_type=MESH)
                cp.wait_send()
                cp.wait_recv()

    return _split_wait(name, body, send_sems, recv_sems, blocks, lands, after)


def _pair_exchange(name, blocks):
    n = len(blocks)

    def body(*refs):
        g_refs, out_refs = refs[:n], refs[n:2 * n]
        send_sems, recv_sems = refs[2 * n:]
        x, y, c = lax.axis_index("x"), lax.axis_index("y"), lax.axis_index("c")
        copies = [pltpu.make_async_remote_copy(
            src_ref=g_refs[i].at[2 * k + 1 - c], dst_ref=out_refs[i].at[k], send_sem=send_sems.at[4 * i + k],
            recv_sem=recv_sems.at[4 * i + k], device_id=(x, y, 1 - c), device_id_type=MESH)
            for i in range(n) for k in range(4)]
        for cp in copies:
            cp.start()
        for cp in copies:
            cp.wait()

    outs = [jax.ShapeDtypeStruct((4,) + b.shape[1:], b.dtype) for b in blocks]
    return _comm_call(name, body, blocks, outs, (4 * n, 4 * n))


def _chip_exchange(name, blocks):
    n = len(blocks)

    def body(*refs):
        p_refs, out_refs = refs[:n], refs[n:2 * n]
        send_sems, recv_sems = refs[2 * n:]
        x, y, c = lax.axis_index("x"), lax.axis_index("y"), lax.axis_index("c")
        chips = [(1 - x, y), (x, 1 - y), (1 - x, 1 - y)]
        copies = [pltpu.make_async_remote_copy(
            src_ref=p_refs[i].at[j], dst_ref=out_refs[i].at[j], send_sem=send_sems.at[3 * i + j],
            recv_sem=recv_sems.at[3 * i + j], device_id=(*chip, c), device_id_type=MESH)
            for i in range(n) for j, chip in enumerate(chips)]
        for cp in copies:
            cp.start()
        for cp in copies:
            cp.wait()

    outs = [jax.ShapeDtypeStruct(b.shape, b.dtype) for b in blocks]
    return _comm_call(name, body, blocks, outs, (3 * n, 3 * n))


def _pair_sum(name, blocks, from_sibling, g_idx, r_idx):
    _, r, c_ = blocks.shape
    tr = _tile(r, 512, 16)

    def body(gi_ref, ri_ref, a_ref, b_ref, own_ref, send_ref):
        k = pl.program_id(1)
        s = a_ref[...] + b_ref[...]

        @pl.when(k == 0)
        def _():
            own_ref[...] = s

        @pl.when(k > 0)
        def _():
            send_ref[...] = s.astype(send_ref.dtype)

    return pl.pallas_call(
        body, name=name,
        grid_spec=pltpu.PrefetchScalarGridSpec(
            num_scalar_prefetch=2, grid=(r // tr, 4),
            in_specs=[pl.BlockSpec((None, tr, c_), lambda i, k, gi, ri: (gi[k], i, 0)),
                      pl.BlockSpec((None, tr, c_), lambda i, k, gi, ri: (ri[k], i, 0))],
            out_specs=[pl.BlockSpec((None, tr, c_), lambda i, k, gi, ri: (0, i, 0)),
                       pl.BlockSpec((None, tr, c_), lambda i, k, gi, ri: (jnp.maximum(k - 1, 0), i, 0))]),
        out_shape=[jax.ShapeDtypeStruct((1, r, c_), F32), jax.ShapeDtypeStruct((3, r, c_), PAYLOAD)],
        compiler_params=_params("parallel", "arbitrary"))(g_idx, r_idx, blocks, from_sibling)


def _adamw(w, g, m, v):
    m = ADAM_B1 * m + (1.0 - ADAM_B1) * g
    v = ADAM_B2 * v + (1.0 - ADAM_B2) * (g * g)
    m_hat = m / (1.0 - ADAM_B1 ** ADAM_STEP)
    v_hat = v / (1.0 - ADAM_B2 ** ADAM_STEP)
    delta = -ADAM_LR * (m_hat / (jnp.sqrt(v_hat) + ADAM_EPS) + ADAM_WD * w)
    return delta, m, v


def _adamw_tiles(r, c_):
    tr = _tile(r, 256, 16)
    return (tr, c_) if tr < r or r <= 256 else (r, _tile(c_, 256))


def _sum_parts(part):
    g = part[0].astype(F32)
    for k in range(1, part.shape[0]):
        g = g + part[k].astype(F32)
    return g


def _sum_adamw(name, parts, w, m, v):
    r, c_ = w.shape
    tr, tc = _adamw_tiles(r, c_)

    def body(p_ref, w_ref, m_ref, v_ref, g_ref, d_ref, nm_ref, nv_ref):
        g = _sum_parts(p_ref)
        g_ref[...] = g
        d_ref[...], nm_ref[...], nv_ref[...] = _adamw(w_ref[...], g, m_ref[...], v_ref[...])

    tile = pl.BlockSpec((tr, tc), lambda i, j: (i, j))
    return pl.pallas_call(body, name=name, grid=(r // tr, c_ // tc),
                          in_specs=[pl.BlockSpec((parts.shape[0], tr, tc), lambda i, j: (0, i, j)), tile, tile, tile],
                          out_specs=[tile] * 4, out_shape=[jax.ShapeDtypeStruct((r, c_), F32)] * 4,
                          compiler_params=_params("parallel", "parallel"))(parts, w, m, v)


def _sum_adamw_layers(name, parts, w, m, v):
    n_layers, r, c_ = w.shape
    tr = _tile(r, 256, 16)

    def body(*refs):
        p_refs = refs[:n_layers]
        w_ref, m_ref, v_ref, g_ref, d_ref, nm_ref, nv_ref = refs[n_layers:]
        layer = pl.program_id(0)
        g = _sum_parts(p_refs[0])
        for li in range(1, n_layers):
            g = jnp.where(layer == li, _sum_parts(p_refs[li]), g)
        g_ref[...] = g
        d_ref[...], nm_ref[...], nv_ref[...] = _adamw(w_ref[...], g, m_ref[...], v_ref[...])

    row = pl.BlockSpec((None, tr, c_), lambda l, i: (l, i, 0))
    specs = [pl.BlockSpec((p.shape[0], tr, c_), lambda l, i: (0, i, 0)) for p in parts]
    return pl.pallas_call(body, name=name, grid=(n_layers, r // tr), in_specs=specs + [row, row, row],
                          out_specs=[row] * 4, out_shape=[jax.ShapeDtypeStruct(w.shape, F32)] * 4,
                          compiler_params=_params("parallel", "parallel"))(*parts, w, m, v)


def _pack_rows(flat, n_rows, cols):
    pad = n_rows * cols - flat.shape[-1]
    flat = jnp.pad(flat, [(0, 0)] * (flat.ndim - 1) + [(0, pad)])
    return flat.reshape(flat.shape[:-1] + (n_rows, cols))


def _cols_join(blocks):
    return jnp.concatenate([blocks[d] for d in range(N_DEV)], axis=1)


def _cols_split(full):
    c = full.shape[1] // N_DEV
    return jnp.stack([full[:, d * c:(d + 1) * c] for d in range(N_DEV)])


def _rows_join(blocks):
    return blocks.reshape(N_DEV * blocks.shape[1], blocks.shape[2])


def _rows_split(full):
    return full.reshape(N_DEV, full.shape[0] // N_DEV, full.shape[1])


def _perm_xbc(a, ng):
    lead = a.shape[:-1]
    di, gn = ng * GW, ng * SSD_D_STATE
    xs = a[..., :di].reshape(lead + (ng, GW))
    bs = a[..., di:di + gn].reshape(lead + (ng, SSD_D_STATE))
    cs = a[..., di + gn:].reshape(lead + (ng, SSD_D_STATE))
    return jnp.concatenate([xs, bs, cs], axis=-1).reshape(lead + (ng * GC,))


def _unperm_xbc(a, ng):
    lead = a.shape[:-1]
    g = a.reshape(lead + (ng, GC))
    return jnp.concatenate([g[..., :GW].reshape(lead + (ng * GW,)),
                            g[..., GW:GW + SSD_D_STATE].reshape(lead + (ng * SSD_D_STATE,)),
                            g[..., GW + SSD_D_STATE:].reshape(lead + (ng * SSD_D_STATE,))], axis=-1)


def _heads_col(v, ng):
    return jnp.pad(v.reshape(ng, 1, SSD_HPG), ((0, 0), (0, 0), (0, LANES - SSD_HPG)))


def _heads_row(v, ng):
    return jnp.pad(v.reshape(ng, SSD_HPG, 1), ((0, 0), (0, 8 - SSD_HPG), (0, 0)))


MATRIX_ITEMS = ("w_in", "w_out", "up0", "down0", "w_qkv", "w_o", "up1", "down1")
VECTOR_ITEMS = ("conv_w", "b_qkv", "b_o")
ITEMS = MATRIX_ITEMS + VECTOR_ITEMS
GATHER_STAGES = (("w_in", "conv_w"), ("w_out", "up0", "down0"), ("w_qkv", "b_qkv", "w_o", "b_o", "up1", "down1"))


def _items(tree, prefix=""):
    g = lambda k: tree[prefix + k]
    return {"w_in": g("ssd_w_in")[0].T, "w_out": g("ssd_w_out")[0], "w_qkv": g("attn_w_qkv")[0].T,
            "w_o": g("attn_w_o")[0], "up0": g("mlp_w_up")[0], "up1": g("mlp_w_up")[1],
            "down0": g("mlp_w_down")[0], "down1": g("mlp_w_down")[1], "conv_w": g("ssd_conv_w")[0],
            "b_qkv": g("attn_b_qkv"), "b_o": g("attn_b_o")}


def _from_items(it):
    return {"ssd_w_in": it["w_in"][None], "ssd_w_out": it["w_out"][None], "attn_w_qkv": it["w_qkv"].T[None],
            "attn_w_o": it["w_o"][None], "mlp_w_up": jnp.stack([it["up0"], it["up1"]]),
            "mlp_w_down": jnp.stack([it["down0"], it["down1"]]), "ssd_conv_w": it["conv_w"][None],
            "attn_b_qkv": it["b_qkv"], "attn_b_o": it["b_o"]}


REPLICATED = ("ssd_conv_b", "ssd_dt_bias", "ssd_a_log", "ssd_d", "ssd_norm_w", "attn_sinks", "mix_pre_norm",
              "mix_post_norm", "ffn_pre_norm", "ffn_post_norm")
WEIGHTS = ("ssd_w_in", "ssd_conv_w", "ssd_conv_b", "ssd_dt_bias", "ssd_a_log", "ssd_d", "ssd_norm_w", "ssd_w_out",
           "attn_w_qkv", "attn_b_qkv", "attn_sinks", "attn_w_o", "attn_b_o", "mlp_w_up", "mlp_w_down",
           "mix_pre_norm", "mix_post_norm", "ffn_pre_norm", "ffn_post_norm")


def _forward_backward(x, target, rep, token, weights_of_stage, reduce_grads):
    t, d = x.shape
    ng = rep["ssd_norm_w"].shape[1] // GW
    di = ng * GW
    n_xbc = ng * GC
    nh = ng * SSD_HPG
    grads, blocks = {}, {}
    w_up, w_down = [None, None], [None, None]
    sinks_rep = jnp.repeat(rep["attn_sinks"].reshape(ATTN_N_KV, ATTN_REP, 1), ATTN_WINDOW, axis=2).reshape(
        ATTN_N_KV, 1, ATTN_REP * ATTN_WINDOW)
    conv_b = rep["ssd_conv_b"]
    gn = ng * SSD_D_STATE
    parts = ((0, di), (di, di), (2 * di, gn), (2 * di + gn, gn), (di + n_xbc, nh))
    alog_c, dsk_c = (_heads_col(rep[k], ng) for k in ("ssd_a_log", "ssd_d"))
    bias_l, alog_l = (jnp.pad(rep[k], ((0, 0), (0, LANES - nh))) for k in ("ssd_dt_bias", "ssd_a_log"))
    norm = {k: rep[k] for k in ("mix_pre_norm", "mix_post_norm", "ffn_pre_norm", "ffn_post_norm")}

    def nrow(name, i):
        return norm[name][i:i + 1]

    def mlp_fwd(i, u2):
        a, p = _mm(f"mlp{i}_up", [u2], [w_up[i]], "nn", tm=1024, tn=1024, out_dtypes=(F32, BF16),
                   epilogue=lambda acc: (acc, jnp.square(jnp.maximum(acc, 0.0))))
        f = _mm(f"mlp{i}_down", [p], [w_down[i]], "nn", tm=512, tn=1024)
        return a, p, f

    def mlp_bwd(i, df, u2, a, p):
        da = _mm(f"mlp{i}_dact", [df], [w_down[i]], "nt", tm=1024, tn=1024, out_dtypes=(BF16,),
                 tiles=(a,), epilogue=lambda acc, av: (acc * (2.0 * jnp.maximum(av, 0.0)),))
        blocks[f"down{i}"] = _rows_split(_mm(f"mlp{i}_dwdown", [p], [df], "tn", tm=512, tn=1024,
                                             out_dtypes=(PAYLOAD,)))
        blocks[f"up{i}"] = _mm(f"mlp{i}_dwup", [u2], [da], "tn", tm=1024, tn=da.shape[1] // N_DEV,
                               out_dtypes=(PAYLOAD,), col_blocks=True)
        return _mm(f"mlp{i}_dx", [da], [w_up[i]], "nt", tm=512, tn=1024)

    u0 = _prenorm("l0_prenorm", x, nrow("mix_pre_norm", 0), token)
    got = weights_of_stage(0, u0)
    w_in_t = _rows_join(got["w_in"])
    w_dt_t = jnp.pad(w_in_t[di + n_xbc:], ((0, LANES - nh), (0, 0)))
    conv_w = _cols_join(got["conv_w"])
    zx = _mm("ssd_in_proj", [u0], [w_in_t], "nt", tm=1024, tn=1024, n_use=di + n_xbc)
    zdt = _mm("ssd_dt_proj", [u0], [w_dt_t], "nt", tm=1024, tn=LANES)
    pre = _conv_fwd(zx, di, n_xbc, conv_w, conv_b)
    dt_c, cum_c, cum_r, sgd_c = _ssd_dt_prep(zdt, bias_l, alog_l, ng)
    y, states = _ssd_fwd(pre, dt_c, cum_c, cum_r, alog_c, dsk_c)
    yn = _gate_norm_fwd(y, zx, rep["ssd_norm_w"])
    got = weights_of_stage(1, yn)
    w_out = _rows_join(got["w_out"])
    w_up[0], w_down[0] = _cols_join(got["up0"]), _rows_join(got["down0"])
    mix0 = _mm("ssd_out_proj", [yn], [w_out], "nn", tm=1024, tn=1024)
    h1, u0f = _post_pre("l0_mid", x, mix0, nrow("mix_post_norm", 0), nrow("ffn_pre_norm", 0))
    a0, p0, f0 = mlp_fwd(0, u0f)
    h2, u1 = _post_pre("l1_in", h1, f0, nrow("ffn_post_norm", 0), nrow("mix_pre_norm", 1))
    got = weights_of_stage(2, u1)
    w_qkv_t = _rows_join(got["w_qkv"])
    w_o = _rows_join(got["w_o"])
    b_qkv_col = got["b_qkv"].reshape(-1, 1)
    b_o = _cols_join(got["b_o"])
    w_up[1], w_down[1] = _cols_join(got["up1"]), _rows_join(got["down1"])
    qkv_t = _mm("attn_qkv_proj", [w_qkv_t], [u1], "nt", tm=768, tn=1024, out_dtypes=(BF16,), cols=(b_qkv_col,),
                epilogue=lambda acc, b: (acc + b,))
    ao_t = _attn_fwd_t(qkv_t, sinks_rep)
    mix1 = _mm("attn_out_proj", [ao_t], [w_o], "tn", tm=1024, tn=1024, rows=(b_o,),
               epilogue=lambda acc, b: (acc + b,))
    h3, u1f = _post_pre("l1_mid", h2, mix1, nrow("mix_post_norm", 1), nrow("ffn_pre_norm", 1))
    a1, p1, f1 = mlp_fwd(1, u1f)
    dh, loss_row = _final_loss("loss", h3, f1, nrow("ffn_post_norm", 1), target)

    g_norm = {k: [None, None] for k in norm}
    df1, g_norm["ffn_post_norm"][1], _ = _norm_bwd("l1_ffn_post_bwd", dh, post=(f1, nrow("ffn_post_norm", 1)))
    du = mlp_bwd(1, df1, u1f, a1, p1)
    sent = reduce_grads("mlp1", {k: blocks[k] for k in ("up1", "down1")})
    dh, g_norm["ffn_pre_norm"][1], dmix1, g_norm["mix_post_norm"][1], db_o = _norm_bwd(
        "l1_mid_bwd", dh, pre=(du, h3, nrow("ffn_pre_norm", 1)), post=(mix1, nrow("mix_post_norm", 1)), after=sent)
    blocks["b_o"] = _cols_split(db_o)
    blocks["w_o"] = _rows_split(_mm("attn_dwo", [ao_t], [dmix1], "nn", tm=512, tn=1024, out_dtypes=(PAYLOAD,)))
    dao_t = _mm("attn_dout", [w_o], [dmix1], "nt", tm=1024, tn=1024, out_dtypes=(BF16,))
    dqkv_t, db_qkv, grads["attn_sinks"] = _attn_bwd_t(qkv_t, dao_t, sinks_rep)
    blocks["b_qkv"] = db_qkv.reshape(N_DEV, 1, -1)
    blocks["w_qkv"] = _rows_split(_mm("attn_dwqkv", [dqkv_t], [u1], "nn", tm=512, tn=1024, out_dtypes=(PAYLOAD,)))
    du = _mm("attn_dx", [dqkv_t], [w_qkv_t], "tn", tm=1024, tn=1024)
    sent = reduce_grads("attn", {k: blocks[k] for k in ("w_o", "w_qkv", "b_o", "b_qkv")})
    dh, g_norm["mix_pre_norm"][1], df0, g_norm["ffn_post_norm"][0], _ = _norm_bwd(
        "l1_in_bwd", dh, pre=(du, h2, nrow("mix_pre_norm", 1)), post=(f0, nrow("ffn_post_norm", 0)), after=sent)
    du = mlp_bwd(0, df0, u0f, a0, p0)
    sent = reduce_grads("mlp0", {k: blocks[k] for k in ("up0", "down0")})
    dh, g_norm["ffn_pre_norm"][0], dmix0, g_norm["mix_post_norm"][0], _ = _norm_bwd(
        "l0_mid_bwd", dh, pre=(du, h1, nrow("ffn_pre_norm", 0)), post=(mix0, nrow("mix_post_norm", 0)), after=sent)
    blocks["w_out"] = _rows_split(_mm("ssd_dwout", [yn], [dmix0], "tn", tm=512, tn=1024, out_dtypes=(PAYLOAD,)))
    dyn = _mm("ssd_dyn", [dmix0], [w_out], "nt", tm=1024, tn=1024)
    dy, dz, grads["ssd_norm_w"] = _gate_norm_bwd(dyn, y, zx, rep["ssd_norm_w"])
    dpx, dpb, dpc, ddt_g, dbias_g, dalog_g, dd_g = _ssd_bwd(dy, pre, states, dt_c, cum_c, cum_r, sgd_c, alog_c,
                                                             dsk_c)
    conv_out = [_conv_bwd(f"ssd_conv_bwd_{tag}", dp, zx, c0, conv_w[:, c0 - di:c0 - di + n])
                for tag, dp, (c0, n) in zip("xbc", (dpx, dpb, dpc), parts[1:4])]
    dconv_w = jnp.concatenate([o[1] for o in conv_out], axis=1)
    dconv_b = jnp.concatenate([o[2] for o in conv_out], axis=1)
    ddt = jnp.transpose(ddt_g[:, :, :SSD_HPG], (1, 0, 2)).reshape(t, nh)
    ddt = jnp.pad(ddt, ((0, 0), (0, LANES - nh))).astype(BF16)
    blocks["conv_w"] = _cols_split(dconv_w)
    grads["ssd_conv_b"] = dconv_b
    for name, val in (("ssd_dt_bias", dbias_g), ("ssd_a_log", dalog_g), ("ssd_d", dd_g)):
        grads[name] = val[:, 0, :SSD_HPG].reshape(1, nh)
    d_zx = [dz] + [o[0] for o in conv_out] + [ddt]
    dw_parts = [_mm(f"ssd_dw_{tag}", [d], [u0], "tn", tm=512, tn=1024, out_dtypes=(PAYLOAD,))
                for tag, d in zip("zxbct", d_zx)]
    dw_parts[-1] = dw_parts[-1][:nh]
    blocks["w_in"] = _rows_split(jnp.concatenate(dw_parts, axis=0))
    sent = reduce_grads("ssd", {k: blocks[k] for k in ("w_in", "w_out", "conv_w")})
    w_parts = [w_in_t[r0:r0 + n] for r0, n in parts[:-1]] + [w_dt_t]
    du = _mm("ssd_dx", d_zx, w_parts, "nn", tm=256, tn=1024)
    grad_x, g_norm["mix_pre_norm"][0] = _norm_bwd("l0_in_bwd", dh, pre=(du, x, nrow("mix_pre_norm", 0)), after=sent)
    for k in norm:
        grads[k] = jnp.concatenate(g_norm[k], axis=0)
    return loss_row, grad_x, grads


def kernel(x, ssd_w_in, ssd_conv_w, ssd_conv_b, ssd_dt_bias, ssd_a_log, ssd_d, ssd_norm_w, ssd_w_out, attn_w_qkv, attn_b_qkv, attn_sinks, attn_w_o, attn_b_o, mlp_w_up, mlp_w_down, mix_pre_norm, mix_post_norm, ffn_pre_norm, ffn_post_norm, loss_target, m_ssd_w_in, m_ssd_conv_w, m_ssd_conv_b, m_ssd_dt_bias, m_ssd_a_log, m_ssd_d, m_ssd_norm_w, m_ssd_w_out, m_attn_w_qkv, m_attn_b_qkv, m_attn_sinks, m_attn_w_o, m_attn_b_o, m_mlp_w_up, m_mlp_w_down, m_mix_pre_norm, m_mix_post_norm, m_ffn_pre_norm, m_ffn_post_norm, v_ssd_w_in, v_ssd_conv_w, v_ssd_conv_b, v_ssd_dt_bias, v_ssd_a_log, v_ssd_d, v_ssd_norm_w, v_ssd_w_out, v_attn_w_qkv, v_attn_b_qkv, v_attn_sinks, v_attn_w_o, v_attn_b_o, v_mlp_w_up, v_mlp_w_down, v_mix_pre_norm, v_mix_post_norm, v_ffn_pre_norm, v_ffn_post_norm):
    given = dict(locals())
    w = {k: given[k] for k in WEIGHTS}
    mom_m = {k: given["m_" + k] for k in WEIGHTS}
    mom_v = {k: given["v_" + k] for k in WEIGHTS}
    w_it, m_it, v_it = _items(given), _items(given, "m_"), _items(given, "v_")

    order = [k for stage in GATHER_STAGES for k in stage]
    shards = [w_it[k].astype(PAYLOAD) if k in MATRIX_ITEMS else w_it[k] for k in order]
    g_send, g_recv, shards, lands, token = _gather_start("gather_start", shards)

    def weights_of_stage(s, after):
        first = sum(len(stage) for stage in GATHER_STAGES[:s])
        sl = slice(first, first + len(GATHER_STAGES[s]))
        srcs, got = _gather_wait(f"gather_wait{s}", g_send, g_recv, first, shards[sl], lands[sl], after)
        me = 4 * ix + 2 * iy + ic
        return {k: lax.dynamic_update_slice(land, src[None], (me,) + (0,) * src.ndim)
                for k, land, src in zip(GATHER_STAGES[s], got, srcs)}

    ix, iy, ic = lax.axis_index("x"), lax.axis_index("y"), lax.axis_index("c")
    in_flight = []

    def reduce_grads(tag, blocks):
        keys = list(blocks)
        started = _scatter_start(f"rs_start_{tag}", [blocks[k] for k in keys])
        in_flight.append((tag, keys, started))
        return started[-1]

    rep = {k: w[k] for k in REPLICATED}
    loss_row, grad_x, grads = _forward_backward(x[0], loss_target[0], rep, token, weights_of_stage, reduce_grads)

    def pack_rep(tree, last):
        flat = jnp.concatenate([tree[k].reshape(-1) for k in REPLICATED] + [last])
        return _pack_rows(flat, _round_up(-(-flat.shape[0] // LANES), 8), LANES)

    landed = {}
    me = 4 * ix + 2 * iy + ic

    def wait_group(group, after):
        tag, keys, (s_send, s_recv, srcs, s_lands, _) = group
        srcs, got = _scatter_wait(f"rs_wait_{tag}", s_send, s_recv, srcs, s_lands, after)
        for k, src, land in zip(keys, srcs, got):
            own = lax.dynamic_index_in_dim(src, me, 0, keepdims=True)
            landed[k] = lax.dynamic_update_slice(land, own, (me,) + (0,) * (land.ndim - 1))

    def adamw_item(k):
        return _sum_adamw(f"adamw_{k}", landed[k], w_it[k], m_it[k], v_it[k])

    def adamw_stack(name, keys):
        return _sum_adamw_layers(f"adamw_{name}", [landed[k] for k in keys], given[name], given["m_" + name],
                                 given["v_" + name])

    for group in in_flight[:-1]:
        wait_group(group, grad_x)
    done = {"mlp_w_up": adamw_stack("mlp_w_up", ("up0", "up1")),
            "mlp_w_down": adamw_stack("mlp_w_down", ("down0", "down1")),
            "attn_w_qkv": [o.T[None] for o in adamw_item("w_qkv")],
            "attn_w_o": [o[None] for o in adamw_item("w_o")],
            "attn_b_qkv": adamw_item("b_qkv"), "attn_b_o": adamw_item("b_o")}
    partials, = _all_gather("gather_small_grads", [pack_rep(grads, loss_row[0, :1])], done["mlp_w_down"][1])
    wait_group(in_flight[-1], partials)
    done["ssd_w_in"] = [o.T[None] for o in adamw_item("w_in")]
    for name, k in (("ssd_w_out", "w_out"), ("ssd_conv_w", "conv_w")):
        done[name] = [o[None] for o in adamw_item(k)]
    zero = jnp.zeros((1,), F32)
    rep_out = _sum_adamw("adamw_replicated", partials, pack_rep(w, zero), pack_rep(mom_m, zero), pack_rep(mom_v, zero))

    kinds = []
    for kind, r_arr in enumerate(rep_out):
        tree = {name: outs4[kind] for name, outs4 in done.items()}
        flat, off = r_arr.reshape(-1), 0
        for k in REPLICATED:
            tree[k] = flat[off:off + w[k].size].reshape(w[k].shape)
            off += w[k].size
        kinds.append(tree)
    loss = rep_out[0].reshape(-1)[off]
    outs = [loss, grad_x[None]]
    for tree in kinds:
        outs += [tree[k] for k in WEIGHTS]
    return tuple(outs)
```

```python
import functools

import jax
import jax.numpy as jnp
from jax import lax
from jax.experimental import pallas as pl
from jax.experimental.pallas import tpu as pltpu

F32 = jnp.float32
BF16 = jnp.bfloat16
PAYLOAD = jnp.bfloat16
HIGHEST = lax.Precision.HIGHEST
MESH = pl.DeviceIdType.MESH

NORM_EPS = 1e-6
SSD_HEAD_DIM = 64
SSD_N_GROUPS = 8
SSD_HPG = 4
SSD_D_STATE = 128
SSD_CONV_WIDTH = 4
SSD_CHUNK = 128
ATTN_HEAD_DIM = 64
ATTN_N_KV = 4
ATTN_REP = 4
ATTN_WINDOW = 128
ADAM_LR = 0.001
ADAM_B1 = 0.9
ADAM_B2 = 0.999
ADAM_EPS = 1e-08
ADAM_WD = 0.01
ADAM_STEP = 10

N_DEV = 8
LANES = 128
PACK_COLS = 1024
V7X_VMEM_LIMIT = 56 * 1024 * 1024

GW = SSD_HPG * SSD_HEAD_DIM
GC = GW + 2 * SSD_D_STATE


def _params(*sem):
    return pltpu.CompilerParams(dimension_semantics=sem, vmem_limit_bytes=V7X_VMEM_LIMIT)


def _tile(n, pref, mult=LANES):
    best = None
    t = mult
    while t <= min(n, pref):
        if n % t == 0:
            best = t
        t += mult
    return best if best is not None else n


def _round_up(n, m):
    return (n + m - 1) // m * m


def _acc(ref, val, first):
    @pl.when(first)
    def _():
        ref[...] = val

    @pl.when(jnp.logical_not(first))
    def _():
        ref[...] += val


def _dot(a, b):
    return lax.dot_general(a, b, (((1,), (0,)), ((), ())), preferred_element_type=F32)


def _dot_nt(a, b):
    return lax.dot_general(a, b, (((1,), (1,)), ((), ())), preferred_element_type=F32)


def _dot_tn(a, b):
    return lax.dot_general(a, b, (((0,), (0,)), ((), ())), preferred_element_type=F32)


def _dot_f32(a, b):
    return lax.dot_general(a, b, (((1,), (0,)), ((), ())), preferred_element_type=F32, precision=HIGHEST)


_DOTS = {"nn": _dot, "nt": _dot_nt, "tn": _dot_tn}


def _sigmoid(x):
    return 1.0 / (1.0 + jnp.exp(-x))


def _softplus(x):
    return jnp.maximum(x, 0.0) + jnp.log1p(jnp.exp(-jnp.abs(x)))


def _silu_grad(x, s):
    return s * (1.0 + x * (1.0 - s))


def _mm(name, a_list, b_list, mode, *, tm, tn, out_dtypes=(F32,), epilogue=None, tiles=(), rows=(), cols=(),
        col_blocks=False, n_use=None):
    npair = len(a_list)
    if mode == "tn":
        m = a_list[0].shape[1]
    else:
        m = a_list[0].shape[0]
    n = n_use if n_use is not None else (b_list[0].shape[0] if mode == "nt" else b_list[0].shape[1])
    tm = _tile(m, tm, LANES if mode == "tn" else 8)
    tn = _tile(n, tn)
    assert m % tm == 0 and n % tn == 0, (name, m, n, tm, tn)
    dot = _DOTS[mode]

    def body(*refs):
        a_refs = refs[:npair]
        b_refs = refs[npair:2 * npair]
        n_extra = len(tiles) + len(rows) + len(cols)
        e_refs = refs[2 * npair:2 * npair + n_extra]
        o_refs = refs[2 * npair + n_extra:]
        acc = None
        for ar, br in zip(a_refs, b_refs):
            d = dot(ar[...], br[...])
            acc = d if acc is None else acc + d
        outs = epilogue(acc, *[e[...] for e in e_refs]) if epilogue is not None else (acc,)
        for o, v in zip(o_refs, outs):
            o[...] = v.astype(o.dtype)

    in_specs = []
    for a in a_list:
        if mode == "tn":
            in_specs.append(pl.BlockSpec((a.shape[0], tm), lambda i, j: (0, i)))
        else:
            in_specs.append(pl.BlockSpec((tm, a.shape[1]), lambda i, j: (i, 0)))
    for b in b_list:
        if mode == "nt":
            in_specs.append(pl.BlockSpec((tn, b.shape[1]), lambda i, j: (j, 0)))
        else:
            in_specs.append(pl.BlockSpec((b.shape[0], tn), lambda i, j: (0, j)))
    in_specs += [pl.BlockSpec((tm, tn), lambda i, j: (i, j)) for _ in tiles]
    in_specs += [pl.BlockSpec((1, tn), lambda i, j: (0, j)) for _ in rows]
    in_specs += [pl.BlockSpec((tm, 1), lambda i, j: (i, 0)) for _ in cols]
    outs = pl.pallas_call(
        body,
        name=name,
        grid=(m // tm, n // tn),
        in_specs=in_specs,
        out_specs=[pl.BlockSpec((None, tm, tn), lambda i, j: (j, i, 0)) if col_blocks else
                   pl.BlockSpec((tm, tn), lambda i, j: (i, j)) for _ in out_dtypes],
        out_shape=[jax.ShapeDtypeStruct((n // tn, m, tn) if col_blocks else (m, n), dt) for dt in out_dtypes],
        compiler_params=_params("parallel", "parallel"),
    )(*a_list, *b_list, *tiles, *rows, *cols)
    return outs[0] if len(out_dtypes) == 1 else outs


def _rms(x, w):
    r = lax.rsqrt(jnp.mean(x * x, axis=-1, keepdims=True) + NORM_EPS)
    return x * r * w


def _rms_bwd(x, w, dy):
    r = lax.rsqrt(jnp.mean(x * x, axis=-1, keepdims=True) + NORM_EPS)
    xh = x * r
    g = dy * w
    dx = r * (g - xh * jnp.mean(g * xh, axis=-1, keepdims=True))
    return dx, dy * xh


def _row_specs(tr, d):
    return pl.BlockSpec((tr, d), lambda i: (i, 0)), pl.BlockSpec((1, d), lambda i: (0, 0))


def _prenorm(name, h, w, after):
    t, d = h.shape
    tr = _tile(t, 512, 8)
    row, vec = _row_specs(tr, d)

    def body(h_ref, w_ref, after_ref, u_ref):
        u_ref[...] = _rms(h_ref[...], w_ref[...]).astype(BF16)

    return pl.pallas_call(body, name=name, grid=(t // tr,),
                          in_specs=[row, vec, pl.BlockSpec((8, LANES), lambda i: (0, 0))], out_specs=row,
                          out_shape=jax.ShapeDtypeStruct((t, d), BF16), compiler_params=_params("parallel"))(
                              h, w, after)


def _post_pre(name, h, m, w_post, w_pre):
    t, d = h.shape
    tr = _tile(t, 512, 8)
    row, vec = _row_specs(tr, d)

    def body(h_ref, m_ref, wq_ref, wp_ref, hn_ref, u_ref):
        hn = h_ref[...] + _rms(m_ref[...], wq_ref[...])
        hn_ref[...] = hn
        u_ref[...] = _rms(hn, wp_ref[...]).astype(BF16)

    return pl.pallas_call(body, name=name, grid=(t // tr,), in_specs=[row, row, vec, vec], out_specs=[row, row],
                          out_shape=[jax.ShapeDtypeStruct((t, d), F32), jax.ShapeDtypeStruct((t, d), BF16)],
                          compiler_params=_params("parallel"))(h, m, w_post, w_pre)


def _final_loss(name, h, m, w_post, target):
    t, d = h.shape
    tr = _tile(t, 512, 8)
    row, vec = _row_specs(tr, d)

    def body(h_ref, m_ref, wq_ref, t_ref, dh_ref, loss_ref):
        err = h_ref[...] + _rms(m_ref[...], wq_ref[...]) - t_ref[...]
        dh_ref[...] = err * (1.0 / d)
        part = 0.5 * jnp.sum(jnp.mean(err * err, axis=-1, keepdims=True), axis=0, keepdims=True)
        _acc(loss_ref, jnp.broadcast_to(part, (1, LANES)), pl.program_id(0) == 0)

    return pl.pallas_call(body, name=name, grid=(t // tr,), in_specs=[row, row, vec, row],
                          out_specs=[row, pl.BlockSpec((1, LANES), lambda i: (0, 0))],
                          out_shape=[jax.ShapeDtypeStruct((t, d), F32), jax.ShapeDtypeStruct((1, LANES), F32)],
                          compiler_params=_params("arbitrary"))(h, m, w_post, target)


def _norm_bwd(name, dh, pre=None, post=None, after=None):
    t, d = dh.shape
    tr = _tile(t, 256, 8)
    row, vec = _row_specs(tr, d)
    has_pre, has_post = pre is not None, post is not None

    def body(*refs):
        it = iter(refs)
        dh_ref = next(it)
        if has_pre:
            du_ref, x_ref, wp_ref = next(it), next(it), next(it)
        if has_post:
            m_ref, wq_ref = next(it), next(it)
        if after is not None:
            next(it)
        first = pl.program_id(0) == 0
        dh_v = dh_ref[...]
        if has_pre:
            dhn_ref, dwp_ref = next(it), next(it)
            dx, dwr = _rms_bwd(x_ref[...], wp_ref[...], du_ref[...])
            dh_v = dh_v + dx
            dhn_ref[...] = dh_v
            _acc(dwp_ref, jnp.sum(dwr, axis=0, keepdims=True), first)
        if has_post:
            dm_ref, dwq_ref, dms_ref = next(it), next(it), next(it)
            dm, dwr = _rms_bwd(m_ref[...], wq_ref[...], dh_v)
            dm_ref[...] = dm.astype(BF16)
            _acc(dwq_ref, jnp.sum(dwr, axis=0, keepdims=True), first)
            _acc(dms_ref, jnp.sum(dm, axis=0, keepdims=True), first)

    ins, in_specs, out_specs, out_shape = [dh], [row], [], []
    if has_pre:
        ins += list(pre)
        in_specs += [row, row, vec]
        out_specs += [row, vec]
        out_shape += [jax.ShapeDtypeStruct((t, d), F32), jax.ShapeDtypeStruct((1, d), F32)]
    if has_post:
        ins += list(post)
        in_specs += [row, vec]
        out_specs += [row, vec, vec]
        out_shape += [jax.ShapeDtypeStruct((t, d), BF16), jax.ShapeDtypeStruct((1, d), F32),
                      jax.ShapeDtypeStruct((1, d), F32)]
    if after is not None:
        ins.append(after)
        in_specs.append(pl.BlockSpec((8, LANES), lambda i: (0, 0)))
    return pl.pallas_call(body, name=name, grid=(t // tr,), in_specs=in_specs, out_specs=out_specs,
                          out_shape=out_shape, compiler_params=_params("arbitrary"))(*ins)


HALO = 8


def _shift_later(cur, prev, s):
    rolled = pltpu.roll(cur, s, 0)
    row = lax.broadcasted_iota(jnp.int32, prev.shape, 0)
    first = jnp.where(row < s, pltpu.roll(prev, s, 0), rolled[0:HALO])
    return jnp.concatenate([first, rolled[HALO:]], axis=0)


def _shift_earlier(cur, nxt, s):
    tt = cur.shape[0]
    rolled = pltpu.roll(cur, tt - s, 0)
    row = lax.broadcasted_iota(jnp.int32, nxt.shape, 0)
    last = jnp.where(row >= HALO - s, pltpu.roll(nxt, HALO - s, 0), rolled[tt - HALO:])
    return jnp.concatenate([rolled[:tt - HALO], last], axis=0)


def _conv_fwd(zx, col0, n_ch, conv_w, conv_b):
    t = zx.shape[0]
    tc = _tile(n_ch, 512)
    tt = _tile(t, 512, 8)
    cb0 = col0 // tc
    assert col0 % tc == 0
    kw = SSD_CONV_WIDTH

    def body(x_ref, p_ref, w_ref, b_ref, o_ref):
        cur = x_ref[...]
        prev = jnp.where(pl.program_id(1) > 0, p_ref[...], 0.0)
        w = w_ref[...]
        acc = b_ref[...] + w[kw - 1:kw, :] * cur
        for k in range(kw - 1):
            acc = acc + w[k:k + 1, :] * _shift_later(cur, prev, kw - 1 - k)
        o_ref[...] = acc

    return pl.pallas_call(
        body, name="ssd_conv_fwd", grid=(n_ch // tc, t // tt),
        in_specs=[pl.BlockSpec((tt, tc), lambda j, i: (i, cb0 + j)),
                  pl.BlockSpec((HALO, tc), lambda j, i: (jnp.maximum(i * (tt // HALO) - 1, 0), cb0 + j)),
                  pl.BlockSpec((kw, tc), lambda j, i: (0, j)),
                  pl.BlockSpec((1, tc), lambda j, i: (0, j))],
        out_specs=pl.BlockSpec((tt, tc), lambda j, i: (i, j)),
        out_shape=jax.ShapeDtypeStruct((t, n_ch), F32),
        compiler_params=_params("parallel", "parallel"))(zx, zx, conv_w, conv_b)


def _conv_bwd(name, dpre, zx, col0, conv_w):
    t, n_ch = dpre.shape
    tc = _tile(n_ch, 512)
    tt = _tile(t, 512, 8)
    cb0 = col0 // tc
    kw = SSD_CONV_WIDTH
    nt = t // tt

    def body(d_ref, dn_ref, x_ref, p_ref, w_ref, dx_ref, dw_ref, db_ref):
        i = pl.program_id(1)
        d = d_ref[...]
        d_next = jnp.where(i < nt - 1, dn_ref[...], 0.0)
        x = x_ref[...]
        x_prev = jnp.where(i > 0, p_ref[...], 0.0)
        w = w_ref[...]
        dx = w[kw - 1:kw, :] * d
        for k in range(kw - 1):
            dx = dx + w[k:k + 1, :] * _shift_earlier(d, d_next, kw - 1 - k)
        dx_ref[...] = dx.astype(BF16)
        first = i == 0
        for k in range(kw):
            xs = x if k == kw - 1 else _shift_later(x, x_prev, kw - 1 - k)
            val = jnp.sum(d * xs, axis=0, keepdims=True)

            @pl.when(first)
            def _():
                dw_ref[k:k + 1, :] = val

            @pl.when(jnp.logical_not(first))
            def _():
                dw_ref[k:k + 1, :] += val
        _acc(db_ref, jnp.sum(d, axis=0, keepdims=True), first)

    return pl.pallas_call(
        body, name=name, grid=(n_ch // tc, nt),
        in_specs=[pl.BlockSpec((tt, tc), lambda j, i: (i, j)),
                  pl.BlockSpec((HALO, tc), lambda j, i: (jnp.minimum((i + 1) * (tt // HALO), t // HALO - 1), j)),
                  pl.BlockSpec((tt, tc), lambda j, i: (i, cb0 + j)),
                  pl.BlockSpec((HALO, tc), lambda j, i: (jnp.maximum(i * (tt // HALO) - 1, 0), cb0 + j)),
                  pl.BlockSpec((kw, tc), lambda j, i: (0, j))],
        out_specs=[pl.BlockSpec((tt, tc), lambda j, i: (i, j)),
                   pl.BlockSpec((kw, tc), lambda j, i: (0, j)),
                   pl.BlockSpec((1, tc), lambda j, i: (0, j))],
        out_shape=[jax.ShapeDtypeStruct((t, n_ch), BF16), jax.ShapeDtypeStruct((kw, n_ch), F32),
                   jax.ShapeDtypeStruct((1, n_ch), F32)],
        compiler_params=_params("parallel", "arbitrary"))(dpre, dpre, zx, zx, conv_w)


def _head_of_lane(shape, width):
    return lax.broadcasted_iota(jnp.int32, shape, len(shape) - 1) // width


def _expand(v, n_rows):
    head = _head_of_lane((n_rows, GW), SSD_HEAD_DIM)
    out = jnp.zeros((n_rows, GW), F32)
    for j in range(SSD_HPG):
        out = jnp.where(head == j, v[:, j:j + 1], out)
    return out


def _contract(v, n_rows):
    head = _head_of_lane((n_rows, GW), SSD_HEAD_DIM)
    lane = lax.broadcasted_iota(jnp.int32, (n_rows, LANES), 1)
    out = jnp.zeros((n_rows, LANES), F32)
    for j in range(SSD_HPG):
        s = jnp.sum(jnp.where(head == j, v, 0.0), axis=1, keepdims=True)
        out = jnp.where(lane == j, s, out)
    return out


def _ssd_dt_prep(zdt, bias, alog, ng):
    t = zdt.shape[0]
    q = SSD_CHUNK

    def body(z_ref, b_ref, a_ref, dt_ref, cum_ref, cumr_ref, sg_ref):
        raw = z_ref[...] + b_ref[...]
        dt = _softplus(raw)
        sgd = _sigmoid(raw)
        row = lax.broadcasted_iota(jnp.int32, (q, q), 0)
        col = lax.broadcasted_iota(jnp.int32, (q, q), 1)
        cum = _dot_f32((col <= row).astype(F32), dt * (-jnp.exp(a_ref[...])))
        cum_t = cum.T
        lane = lax.broadcasted_iota(jnp.int32, (q, LANES), 1)
        for g in range(ng):
            shift = (LANES - g * SSD_HPG) % LANES

            def group(v):
                return jnp.where(lane < SSD_HPG, pltpu.roll(v, shift, 1) if shift else v, 0.0)

            dt_ref[g] = group(dt)
            cum_ref[g] = group(cum)
            sg_ref[g] = group(sgd)
            cumr_ref[g] = (pltpu.roll(cum_t, shift, 0) if shift else cum_t)[0:8, :]

    cols = pl.BlockSpec((ng, q, LANES), lambda c: (0, c, 0))
    vec = pl.BlockSpec((1, LANES), lambda c: (0, 0))
    col_shape = jax.ShapeDtypeStruct((ng, t, LANES), F32)
    return pl.pallas_call(body, name="ssd_dt_prep", grid=(t // q,),
                          in_specs=[pl.BlockSpec((q, LANES), lambda c: (c, 0)), vec, vec],
                          out_specs=[cols, cols, pl.BlockSpec((ng, 8, q), lambda c: (0, 0, c)), cols],
                          out_shape=[col_shape, col_shape, jax.ShapeDtypeStruct((ng, 8, t), F32), col_shape],
                          compiler_params=_params("parallel"))(zdt, bias, alog)


def _ssd_common(pre, dt, cum, cum_r, alog_c):
    q = SSD_CHUNK
    sg = _sigmoid(pre)
    act = pre * sg
    xa = act[:, :GW]
    bm = act[:, GW:GW + SSD_D_STATE].astype(BF16)
    cm = act[:, GW + SSD_D_STATE:].astype(BF16)
    row = lax.broadcasted_iota(jnp.int32, (q, q), 0)
    col = lax.broadcasted_iota(jnp.int32, (q, q), 1)
    tril = col <= row
    a_c = -jnp.exp(alog_c)
    g = _dot_nt(cm, bm)
    dt_x = _expand(dt, q)
    xdt = xa * dt_x
    cl = cum[q - 1:q, :]
    e_c = jnp.exp(cl - cum)
    lam_c = jnp.exp(cum)
    return dict(sg=sg, xa=xa, bm=bm, cm=cm, tril=tril, row=row, col=col, dt=dt, a_c=a_c, cum=cum, cum_r=cum_r,
                g=g, dt_x=dt_x, xdt=xdt, cl=cl, e_c=e_c, lam_c=lam_c)


SSD_GPS = 2


def _ssd_specs(nc, rev, ng):
    q = SSD_CHUNK
    xw, nw = SSD_GPS * GW, SSD_GPS * SSD_D_STATE
    b_off = ng * GW // nw
    c_off = (ng * GW + ng * SSD_D_STATE) // nw
    assert ng % SSD_GPS == 0 and (ng * GW) % nw == 0 and (ng * SSD_D_STATE) % nw == 0

    def ch(c):
        return nc - 1 - c if rev else c

    chunk_grp = [pl.BlockSpec((q, xw), lambda g, c: (ch(c), g)),
                 pl.BlockSpec((q, nw), lambda g, c: (ch(c), b_off + g)),
                 pl.BlockSpec((q, nw), lambda g, c: (ch(c), c_off + g))]
    col_form = pl.BlockSpec((SSD_GPS, q, LANES), lambda g, c: (g, ch(c), 0))
    row_form = pl.BlockSpec((SSD_GPS, 8, q), lambda g, c: (g, 0, ch(c)))
    col_par = pl.BlockSpec((SSD_GPS, 1, LANES), lambda g, c: (g, 0, 0))
    y_spec = pl.BlockSpec((q, xw), lambda g, c: (ch(c), g))
    st_spec = pl.BlockSpec((SSD_GPS, None, GW, SSD_D_STATE), lambda g, c: (g, ch(c), 0, 0))
    bc_spec = pl.BlockSpec((q, nw), lambda g, c: (ch(c), g))
    return chunk_grp, col_form, row_form, col_par, y_spec, st_spec, bc_spec


def _ssd_group_views(gi, wide, narrow, stacked):
    xs, ns = pl.ds(gi * GW, GW), pl.ds(gi * SSD_D_STATE, SSD_D_STATE)
    return [r.at[:, xs] for r in wide], [r.at[:, ns] for r in narrow], [r.at[gi] for r in stacked]


def _ssd_fwd(pre, dt_c, cum_c, cum_r, alog_c, dsk_c):
    t = pre.shape[0]
    ng = pre.shape[1] // GC
    q = SSD_CHUNK
    nc = t // q
    chunk_grp, col_form, row_form, col_par, y_spec, st_spec, _ = _ssd_specs(nc, False, ng)

    def body(px_ref, pb_ref, pc_ref, dt_ref, cum_ref, cumr_ref, ac_ref, dk_ref, y_ref, sp_ref, st_ref):
        @pl.when(pl.program_id(1) == 0)
        def _():
            st_ref[...] = jnp.zeros_like(st_ref)

        for gi in range(SSD_GPS):
            (px, y), (pb, pc), rest = _ssd_group_views(
                gi, (px_ref, y_ref), (pb_ref, pc_ref), (dt_ref, cum_ref, cumr_ref, ac_ref, dk_ref, sp_ref, st_ref))
            one_group(px, pb, pc, *rest[:5], y, *rest[5:])

    def one_group(px_ref, pb_ref, pc_ref, dt_ref, cum_ref, cumr_ref, ac_ref, dk_ref, y_ref, sp_ref, st_ref):
        pre_v = jnp.concatenate([px_ref[...], pb_ref[...], pc_ref[...]], axis=1)
        v = _ssd_common(pre_v, dt_ref[...], cum_ref[...], cumr_ref[...], ac_ref[...])
        s0 = st_ref[...]
        sp_ref[...] = s0
        r = _dot_nt(v["cm"], s0.astype(BF16))
        y = _expand(v["lam_c"], q) * r + _expand(dk_ref[...], 1) * v["xa"]
        head = _head_of_lane((q, GW), SSD_HEAD_DIM)
        for j in range(SSD_HPG):
            diff = v["cum"][:, j:j + 1] - v["cum_r"][j:j + 1, :]
            w = (v["g"] * jnp.exp(jnp.where(v["tril"], diff, -jnp.inf))).astype(BF16)
            y = y + _dot(w, jnp.where(head == j, v["xdt"], 0.0).astype(BF16))
        y_ref[...] = y
        ds = _dot_tn((v["xdt"] * _expand(v["e_c"], q)).astype(BF16), v["bm"])
        for j in range(SSD_HPG):
            rows = slice(j * SSD_HEAD_DIM, (j + 1) * SSD_HEAD_DIM)
            st_ref[rows, :] = s0[rows, :] * jnp.exp(v["cum_r"][j:j + 1, q - 1:q]) + ds[rows, :]

    return pl.pallas_call(
        body, name="ssd_scan_fwd", grid=(ng // SSD_GPS, nc),
        in_specs=chunk_grp + [col_form, col_form, row_form, col_par, col_par],
        out_specs=[y_spec, st_spec],
        out_shape=[jax.ShapeDtypeStruct((t, ng * GW), F32), jax.ShapeDtypeStruct((ng, nc, GW, SSD_D_STATE), F32)],
        scratch_shapes=[pltpu.VMEM((SSD_GPS, GW, SSD_D_STATE), F32)],
        compiler_params=_params("parallel", "arbitrary"))(pre, pre, pre, dt_c, cum_c, cum_r, alog_c, dsk_c)


def _ssd_bwd(dy, pre, states, dt_c, cum_c, cum_r, sgd_c, alog_c, dsk_c):
    t = pre.shape[0]
    ng = pre.shape[1] // GC
    q = SSD_CHUNK
    nc = t // q
    chunk_grp, col_form, row_form, col_par, y_spec, st_spec, bc_spec = _ssd_specs(nc, True, ng)

    def body(dy_ref, px_ref, pb_ref, pc_ref, sp_ref, dt_ref, cum_ref, cumr_ref, sgd_ref, ac_ref, dk_ref,
             dpx_ref, dpb_ref, dpc_ref, ddt_ref, dbias_ref, dalog_ref, dd_ref, ds_ref):
        @pl.when(pl.program_id(1) == 0)
        def _():
            ds_ref[...] = jnp.zeros_like(ds_ref)

        for gi in range(SSD_GPS):
            (dy, px, dpx), (pb, pc, dpb, dpc), rest = _ssd_group_views(
                gi, (dy_ref, px_ref, dpx_ref), (pb_ref, pc_ref, dpb_ref, dpc_ref),
                (sp_ref, dt_ref, cum_ref, cumr_ref, sgd_ref, ac_ref, dk_ref, ddt_ref, dbias_ref, dalog_ref, dd_ref,
                 ds_ref))
            one_group(dy, px, pb, pc, *rest[:7], dpx, dpb, dpc, *rest[7:])

    def one_group(dy_ref, px_ref, pb_ref, pc_ref, sp_ref, dt_ref, cum_ref, cumr_ref, sgd_ref, ac_ref, dk_ref,
                  dpx_ref, dpb_ref, dpc_ref, ddt_ref, dbias_ref, dalog_ref, dd_ref, ds_ref):
        first = pl.program_id(1) == 0
        pre_v = jnp.concatenate([px_ref[...], pb_ref[...], pc_ref[...]], axis=1)
        v = _ssd_common(pre_v, dt_ref[...], cum_ref[...], cumr_ref[...], ac_ref[...])
        xa, bm, cm, xdt, cum, cum_r = v["xa"], v["bm"], v["cm"], v["xdt"], v["cum"], v["cum_r"]
        xdt_b = xdt.astype(BF16)
        dy_v = dy_ref[...]
        s0 = sp_ref[...]
        ds1 = ds_ref[...]
        s0b, ds1b = s0.astype(BF16), ds1.astype(BF16)
        head = _head_of_lane((q, GW), SSD_HEAD_DIM)
        lane = lax.broadcasted_iota(jnp.int32, (q, LANES), 1)
        lane1 = lax.broadcasted_iota(jnp.int32, (1, LANES), 1)
        lam_x = _expand(v["lam_c"], q)
        e_x = _expand(v["e_c"], q)

        dxa = _expand(dk_ref[...], 1) * dy_v
        dd = _contract(jnp.sum(dy_v * xa, axis=0, keepdims=True), 1)
        r = _dot_nt(cm, s0b)
        dcum = _contract(dy_v * r * lam_x, q)
        drb = (lam_x * dy_v).astype(BF16)
        dc = _dot(drb, s0b)
        ds0 = _dot_tn(drb, cm)
        extra = jnp.zeros((1, LANES), F32)
        for j in range(SSD_HPG):
            rows = slice(j * SSD_HEAD_DIM, (j + 1) * SSD_HEAD_DIM)
            lam_last = jnp.exp(cum_r[j:j + 1, q - 1:q])
            ds_ref[rows, :] = ds0[rows, :] + lam_last * ds1[rows, :]
            tot = jnp.sum(jnp.sum(ds1[rows, :] * s0[rows, :], axis=1, keepdims=True), axis=0, keepdims=True)
            extra = jnp.where(lane1 == j, lam_last * tot, extra)
        dv = _dot_nt(bm, ds1b)
        db = _dot((xdt * e_x).astype(BF16), ds1b)
        dxdt = e_x * dv
        dee = _contract(dv * xdt, q) * v["e_c"]
        dcum = dcum - dee
        extra = extra + jnp.sum(dee, axis=0, keepdims=True)
        dg = jnp.zeros((q, q), F32)
        for j in range(SSD_HPG):
            diff = cum[:, j:j + 1] - cum_r[j:j + 1, :]
            el = jnp.exp(jnp.where(v["tril"], diff, -jnp.inf))
            gl = v["g"] * el
            dym = jnp.where(head == j, dy_v, 0.0).astype(BF16)
            dwm = _dot_nt(dym, xdt_b)
            dxdt = dxdt + _dot_tn(gl.astype(BF16), dym)
            z = dwm * gl
            rk = jnp.sum(z, axis=1, keepdims=True) - jnp.sum(z.T, axis=1, keepdims=True)
            dcum = jnp.where(lane == j, dcum + rk, dcum)
            dg = dg + dwm * el
        dgb = dg.astype(BF16)
        dc = dc + _dot(dgb, bm)
        db = db + _dot_tn(dgb, cm)
        da = _dot_f32((v["row"] <= v["col"]).astype(F32), dcum) + extra
        ddt = _contract(dxdt * xa, q) + v["a_c"] * da
        dalog = jnp.sum(v["dt"] * da, axis=0, keepdims=True) * v["a_c"]
        dxa = dxa + v["dt_x"] * dxdt
        ddt_raw = jnp.where(lane < SSD_HPG, ddt * sgd_ref[...], 0.0)
        sgrad = _silu_grad(pre_v, v["sg"])
        dpx_ref[...] = dxa * sgrad[:, :GW]
        dpb_ref[...] = db * sgrad[:, GW:GW + SSD_D_STATE]
        dpc_ref[...] = dc * sgrad[:, GW + SSD_D_STATE:]
        ddt_ref[...] = ddt_raw
        _acc(dbias_ref, jnp.sum(ddt_raw, axis=0, keepdims=True), first)
        _acc(dalog_ref, jnp.where(lane1 < SSD_HPG, dalog, 0.0), first)
        _acc(dd_ref, dd, first)

    return pl.pallas_call(
        body, name="ssd_scan_bwd", grid=(ng // SSD_GPS, nc),
        in_specs=[y_spec] + chunk_grp + [st_spec, col_form, col_form, row_form, col_form, col_par, col_par],
        out_specs=[y_spec, bc_spec, bc_spec, col_form, col_par, col_par, col_par],
        out_shape=[jax.ShapeDtypeStruct((t, ng * GW), F32), jax.ShapeDtypeStruct((t, ng * SSD_D_STATE), F32),
                   jax.ShapeDtypeStruct((t, ng * SSD_D_STATE), F32), jax.ShapeDtypeStruct((ng, t, LANES), F32),
                   jax.ShapeDtypeStruct((ng, 1, LANES), F32), jax.ShapeDtypeStruct((ng, 1, LANES), F32),
                   jax.ShapeDtypeStruct((ng, 1, LANES), F32)],
        scratch_shapes=[pltpu.VMEM((SSD_GPS, GW, SSD_D_STATE), F32)],
        compiler_params=_params("parallel", "arbitrary"))(dy, pre, pre, pre, states, dt_c, cum_c, cum_r, sgd_c, alog_c,
                                                           dsk_c)


def _gate_norm_fwd(y, zx, norm_w):
    t, di = y.shape
    tr = _tile(t, 256, 8)
    ng = di // GW

    def body(y_ref, z_ref, w_ref, o_ref):
        z = z_ref[...]
        gate = y_ref[...] * (z * _sigmoid(z))
        w = w_ref[...]
        for g in range(ng):
            cols = slice(g * GW, (g + 1) * GW)
            gs = gate[:, cols]
            r = lax.rsqrt(jnp.mean(gs * gs, axis=-1, keepdims=True) + NORM_EPS)
            o_ref[:, cols] = (gs * r * w[:, cols]).astype(BF16)

    row = pl.BlockSpec((tr, di), lambda i: (i, 0))
    return pl.pallas_call(body, name="ssd_gate_norm_fwd", grid=(t // tr,),
                          in_specs=[row, row, pl.BlockSpec((1, di), lambda i: (0, 0))], out_specs=row,
                          out_shape=jax.ShapeDtypeStruct((t, di), BF16), compiler_params=_params("parallel"))(
                              y, zx, norm_w)


def _gate_norm_bwd(dyn, y, zx, norm_w):
    t, di = y.shape
    tr = _tile(t, 256, 8)
    ng = di // GW

    def body(d_ref, y_ref, z_ref, w_ref, dy_ref, dz_ref, dw_ref):
        z = z_ref[...]
        yv = y_ref[...]
        sg = _sigmoid(z)
        sz = z * sg
        gate = yv * sz
        w = w_ref[...]
        d = d_ref[...]
        dsz = _silu_grad(z, sg)
        dws = []
        for g in range(ng):
            cols = slice(g * GW, (g + 1) * GW)
            dg, dwr = _rms_bwd(gate[:, cols], w[:, cols], d[:, cols])
            dy_ref[:, cols] = dg * sz[:, cols]
            dz_ref[:, cols] = (dg * yv[:, cols] * dsz[:, cols]).astype(BF16)
            dws.append(jnp.sum(dwr, axis=0, keepdims=True))
        first = pl.program_id(0) == 0
        for g in range(ng):
            cols = slice(g * GW, (g + 1) * GW)

            @pl.when(first)
            def _():
                dw_ref[:, cols] = dws[g]

            @pl.when(jnp.logical_not(first))
            def _():
                dw_ref[:, cols] += dws[g]

    row = pl.BlockSpec((tr, di), lambda i: (i, 0))
    vec = pl.BlockSpec((1, di), lambda i: (0, 0))
    return pl.pallas_call(body, name="ssd_gate_norm_bwd", grid=(t // tr,), in_specs=[row, row, row, vec],
                          out_specs=[row, row, vec],
                          out_shape=[jax.ShapeDtypeStruct((t, di), F32), jax.ShapeDtypeStruct((t, di), BF16),
                                     jax.ShapeDtypeStruct((1, di), F32)],
                          compiler_params=_params("arbitrary"))(dyn, y, zx, norm_w)


def _attn_mask(n):
    w = ATTN_WINDOW
    qpos = lax.broadcasted_iota(jnp.int32, (w, 2 * w), 0) + w
    kpos = lax.broadcasted_iota(jnp.int32, (w, 2 * w), 1)
    rel = qpos - kpos
    return (rel >= 0) & (rel < w) & jnp.logical_not((n == 0) & (kpos < w))


def _attn_probs(qh, kbh, mask, sink):
    s = _dot_nt(qh, kbh) * (ATTN_HEAD_DIM ** -0.5)
    s = jnp.where(mask, s, -jnp.inf)
    m = jnp.maximum(jnp.max(s, axis=-1, keepdims=True), sink)
    e = jnp.exp(s - m)
    es = jnp.exp(sink - m)
    inv = 1.0 / (jnp.sum(e, axis=-1, keepdims=True) + es)
    return e * inv, es * inv


def _attn_fwd(qkv, sinks):
    t = qkv.shape[0]
    w, hd = ATTN_WINDOW, ATTN_HEAD_DIM
    kd = ATTN_N_KV * hd
    qd = ATTN_REP * kd
    nb = t // w

    def body(q_ref, kc_ref, vc_ref, kp_ref, vp_ref, s_ref, o_ref):
        n = pl.program_id(0)
        mask = _attn_mask(n)
        q = q_ref[...]
        kb = jnp.concatenate([kp_ref[...], kc_ref[...]], axis=0)
        vb = jnp.concatenate([vp_ref[...], vc_ref[...]], axis=0)
        sk = s_ref[...]
        for kv in range(ATTN_N_KV):
            kbh = kb[:, kv * hd:(kv + 1) * hd]
            vbh = vb[:, kv * hd:(kv + 1) * hd]
            for rep in range(ATTN_REP):
                h = kv * ATTN_REP + rep
                p, _ = _attn_probs(q[:, h * hd:(h + 1) * hd], kbh, mask, sk[:, h:h + 1])
                o_ref[:, h * hd:(h + 1) * hd] = _dot(p.astype(BF16), vbh).astype(BF16)

    prev = lambda n: jnp.maximum(n - 1, 0)
    return pl.pallas_call(
        body, name="attn_fwd", grid=(nb,),
        in_specs=[pl.BlockSpec((w, qd), lambda n: (n, 0)),
                  pl.BlockSpec((w, kd), lambda n: (n, ATTN_REP)),
                  pl.BlockSpec((w, kd), lambda n: (n, ATTN_REP + 1)),
                  pl.BlockSpec((w, kd), lambda n: (prev(n), ATTN_REP)),
                  pl.BlockSpec((w, kd), lambda n: (prev(n), ATTN_REP + 1)),
                  pl.BlockSpec((1, sinks.shape[1]), lambda n: (0, 0))],
        out_specs=pl.BlockSpec((w, qd), lambda n: (n, 0)),
        out_shape=jax.ShapeDtypeStruct((t, qd), BF16),
        compiler_params=_params("parallel"))(qkv, qkv, qkv, qkv, qkv, sinks)


def _attn_bwd(qkv, do, sinks):
    t = qkv.shape[0]
    w, hd = ATTN_WINDOW, ATTN_HEAD_DIM
    kd = ATTN_N_KV * hd
    qd = ATTN_REP * kd
    nq = ATTN_N_KV * ATTN_REP
    nb = t // w

    def body(q_ref, kc_ref, vc_ref, kp_ref, vp_ref, do_ref, s_ref,
             dq_ref, dk_ref, dv_ref, bq_ref, bk_ref, bv_ref, dsk_ref, ck_ref, cv_ref):
        n = pl.program_id(0)
        first = n == 0

        @pl.when(first)
        def _():
            ck_ref[...] = jnp.zeros_like(ck_ref)
            cv_ref[...] = jnp.zeros_like(cv_ref)
            bq_ref[...] = jnp.zeros_like(bq_ref)
            bk_ref[...] = jnp.zeros_like(bk_ref)
            bv_ref[...] = jnp.zeros_like(bv_ref)
            dsk_ref[...] = jnp.zeros_like(dsk_ref)

        @pl.when(n < nb)
        def _():
            mask = _attn_mask(n)
            q = q_ref[...]
            dov = do_ref[...]
            kb = jnp.concatenate([kp_ref[...], kc_ref[...]], axis=0)
            vb = jnp.concatenate([vp_ref[...], vc_ref[...]], axis=0)
            sk = s_ref[...]
            lane = lax.broadcasted_iota(jnp.int32, (1, nq), 1)
            dsk = jnp.zeros((1, nq), F32)
            dq_parts, dk_parts, dv_parts = [], [], []
            for kv in range(ATTN_N_KV):
                kbh = kb[:, kv * hd:(kv + 1) * hd]
                vbh = vb[:, kv * hd:(kv + 1) * hd]
                dkh = jnp.zeros((2 * w, hd), F32)
                dvh = jnp.zeros((2 * w, hd), F32)
                for rep in range(ATTN_REP):
                    h = kv * ATTN_REP + rep
                    qh = q[:, h * hd:(h + 1) * hd]
                    doh = dov[:, h * hd:(h + 1) * hd]
                    p, ps = _attn_probs(qh, kbh, mask, sk[:, h:h + 1])
                    pb = p.astype(BF16)
                    dp = _dot_nt(doh, vbh)
                    delta = jnp.sum(p * dp, axis=-1, keepdims=True)
                    dsc = (p * (dp - delta) * (hd ** -0.5)).astype(BF16)
                    dq_parts.append(_dot(dsc, kbh))
                    dkh = dkh + _dot_tn(dsc, qh)
                    dvh = dvh + _dot_tn(pb, doh)
                    dsk = jnp.where(lane == h, -jnp.sum(ps * delta, axis=0, keepdims=True), dsk)
                dk_parts.append(dkh)
                dv_parts.append(dvh)
            dq = jnp.concatenate(dq_parts, axis=1)
            dkb = jnp.concatenate(dk_parts, axis=1)
            dvb = jnp.concatenate(dv_parts, axis=1)
            dq_ref[...] = dq.astype(BF16)
            bq_ref[...] += jnp.sum(dq, axis=0, keepdims=True)
            dsk_ref[...] += dsk
            dk_prev = ck_ref[...] + dkb[:w, :]
            dv_prev = cv_ref[...] + dvb[:w, :]
            dk_ref[...] = dk_prev.astype(BF16)
            dv_ref[...] = dv_prev.astype(BF16)

            @pl.when(n > 0)
            def _():
                bk_ref[...] += jnp.sum(dk_prev, axis=0, keepdims=True)
                bv_ref[...] += jnp.sum(dv_prev, axis=0, keepdims=True)

            ck_ref[...] = dkb[w:, :]
            cv_ref[...] = dvb[w:, :]

        @pl.when(n == nb)
        def _():
            dk_ref[...] = ck_ref[...].astype(BF16)
            dv_ref[...] = cv_ref[...].astype(BF16)
            bk_ref[...] += jnp.sum(ck_ref[...], axis=0, keepdims=True)
            bv_ref[...] += jnp.sum(cv_ref[...], axis=0, keepdims=True)

    cur = lambda n: jnp.minimum(n, nb - 1)
    prev = lambda n: jnp.maximum(jnp.minimum(n, nb - 1) - 1, 0)
    late = lambda n: jnp.maximum(n - 1, 0)
    vec = lambda width: pl.BlockSpec((1, width), lambda n: (0, 0))
    return pl.pallas_call(
        body, name="attn_bwd", grid=(nb + 1,),
        in_specs=[pl.BlockSpec((w, qd), lambda n: (cur(n), 0)),
                  pl.BlockSpec((w, kd), lambda n: (cur(n), ATTN_REP)),
                  pl.BlockSpec((w, kd), lambda n: (cur(n), ATTN_REP + 1)),
                  pl.BlockSpec((w, kd), lambda n: (prev(n), ATTN_REP)),
                  pl.BlockSpec((w, kd), lambda n: (prev(n), ATTN_REP + 1)),
                  pl.BlockSpec((w, qd), lambda n: (cur(n), 0)),
                  vec(nq)],
        out_specs=[pl.BlockSpec((w, qd), lambda n: (cur(n), 0)),
                   pl.BlockSpec((w, kd), lambda n: (late(n), 0)),
                   pl.BlockSpec((w, kd), lambda n: (late(n), 0)),
                   vec(qd), vec(kd), vec(kd), vec(nq)],
        out_shape=[jax.ShapeDtypeStruct((t, qd), BF16), jax.ShapeDtypeStruct((t, kd), BF16),
                   jax.ShapeDtypeStruct((t, kd), BF16), jax.ShapeDtypeStruct((1, qd), F32),
                   jax.ShapeDtypeStruct((1, kd), F32), jax.ShapeDtypeStruct((1, kd), F32),
                   jax.ShapeDtypeStruct((1, nq), F32)],
        scratch_shapes=[pltpu.VMEM((w, kd), F32), pltpu.VMEM((w, kd), F32)],
        compiler_params=_params("arbitrary"))(qkv, qkv, qkv, qkv, qkv, do, sinks)


def _attn_mask_t(n):
    w = ATTN_WINDOW
    kpos = lax.broadcasted_iota(jnp.int32, (2 * w, ATTN_REP * w), 0)
    qpos = lax.broadcasted_iota(jnp.int32, (2 * w, ATTN_REP * w), 1) % w + w
    rel = qpos - kpos
    return (rel >= 0) & (rel < w) & jnp.logical_not((n == 0) & (kpos < w))


def _attn_probs_t(qts, ktb, mask, sink):
    s = _dot_tn(ktb, qts) * (ATTN_HEAD_DIM ** -0.5)
    s = jnp.where(mask, s, -jnp.inf)
    m = jnp.maximum(jnp.max(s, axis=0, keepdims=True), sink)
    e = jnp.exp(s - m)
    es = jnp.exp(sink - m)
    inv = 1.0 / (jnp.sum(e, axis=0, keepdims=True) + es)
    return e * inv, es * inv


def _attn_blocks_t(kv, q_ref, kc_ref, vc_ref, kp_ref, vp_ref):
    hd = ATTN_HEAD_DIM
    rows = slice(kv * hd, (kv + 1) * hd)
    ktb = jnp.concatenate([kp_ref[rows, :], kc_ref[rows, :]], axis=1)
    vtb = jnp.concatenate([vp_ref[rows, :], vc_ref[rows, :]], axis=1)
    qts = jnp.concatenate([q_ref[(kv * ATTN_REP + r) * hd:(kv * ATTN_REP + r + 1) * hd, :]
                           for r in range(ATTN_REP)], axis=1)
    return qts, ktb, vtb


def _attn_specs_t(nb, cur, prev):
    w, hd = ATTN_WINDOW, ATTN_HEAD_DIM
    kd = ATTN_N_KV * hd
    qd = ATTN_REP * kd
    return [pl.BlockSpec((qd, w), lambda n: (0, cur(n))),
            pl.BlockSpec((kd, w), lambda n: (ATTN_REP, cur(n))),
            pl.BlockSpec((kd, w), lambda n: (ATTN_REP + 1, cur(n))),
            pl.BlockSpec((kd, w), lambda n: (ATTN_REP, prev(n))),
            pl.BlockSpec((kd, w), lambda n: (ATTN_REP + 1, prev(n)))]


def _attn_fwd_t(qkv_t, sinks_rep):
    t = qkv_t.shape[1]
    w, hd = ATTN_WINDOW, ATTN_HEAD_DIM
    qd = ATTN_N_KV * ATTN_REP * hd
    nb = t // w

    def body(q_ref, kc_ref, vc_ref, kp_ref, vp_ref, s_ref, o_ref):
        mask = _attn_mask_t(pl.program_id(0))
        for kv in range(ATTN_N_KV):
            qts, ktb, vtb = _attn_blocks_t(kv, q_ref, kc_ref, vc_ref, kp_ref, vp_ref)
            p, _ = _attn_probs_t(qts, ktb, mask, s_ref[kv])
            ots = _dot(vtb, p.astype(BF16))
            for r in range(ATTN_REP):
                h = kv * ATTN_REP + r
                o_ref[h * hd:(h + 1) * hd, :] = ots[:, r * w:(r + 1) * w].astype(BF16)

    return pl.pallas_call(
        body, name="attn_fwd", grid=(nb,),
        in_specs=_attn_specs_t(nb, lambda n: n, lambda n: jnp.maximum(n - 1, 0)) + [
            pl.BlockSpec(sinks_rep.shape, lambda n: (0, 0, 0))],
        out_specs=pl.BlockSpec((qd, w), lambda n: (0, n)),
        out_shape=jax.ShapeDtypeStruct((qd, t), BF16),
        compiler_params=_params("parallel"))(qkv_t, qkv_t, qkv_t, qkv_t, qkv_t, sinks_rep)


def _attn_bwd_t(qkv_t, do_t, sinks_rep):
    t = qkv_t.shape[1]
    w, hd = ATTN_WINDOW, ATTN_HEAD_DIM
    kd = ATTN_N_KV * hd
    qd = ATTN_REP * kd
    nq = ATTN_N_KV * ATTN_REP
    nb = t // w
    rows_all = qd + 2 * kd

    def body(q_ref, kc_ref, vc_ref, kp_ref, vp_ref, do_ref, s_ref, dqkv_ref, bsum_ref, dsk_ref,
             carry_ref, new_ref, bacc_ref, sacc_ref):
        n = pl.program_id(0)

        @pl.when(n == 0)
        def _():
            carry_ref[...] = jnp.zeros_like(carry_ref)
            bacc_ref[...] = jnp.zeros_like(bacc_ref)
            sacc_ref[...] = jnp.zeros_like(sacc_ref)

        @pl.when(n < nb)
        def _():
            mask = _attn_mask_t(n)
            for kv in range(ATTN_N_KV):
                qts, ktb, vtb = _attn_blocks_t(kv, q_ref, kc_ref, vc_ref, kp_ref, vp_ref)
                dots = jnp.concatenate([do_ref[(kv * ATTN_REP + r) * hd:(kv * ATTN_REP + r + 1) * hd, :]
                                        for r in range(ATTN_REP)], axis=1)
                p, ps = _attn_probs_t(qts, ktb, mask, s_ref[kv])
                dpt = _dot_tn(vtb, dots)
                delta = jnp.sum(p * dpt, axis=0, keepdims=True)
                dst = (p * (dpt - delta) * (hd ** -0.5)).astype(BF16)
                dqts = _dot(ktb, dst)
                for r in range(ATTN_REP):
                    h = kv * ATTN_REP + r
                    new_ref[h * hd:(h + 1) * hd, :] = dqts[:, r * w:(r + 1) * w]
                dktb = _dot_nt(qts, dst)
                dvtb = _dot_nt(dots, p.astype(BF16))
                krows = slice(qd + kv * hd, qd + (kv + 1) * hd)
                vrows = slice(qd + kd + kv * hd, qd + kd + (kv + 1) * hd)
                carry_ref[krows, :] += dktb[:, :w]
                carry_ref[vrows, :] += dvtb[:, :w]
                new_ref[krows, :] = dktb[:, w:]
                new_ref[vrows, :] = dvtb[:, w:]
                sacc_ref[kv] += -(ps * delta)

        @pl.when(n >= 1)
        def _():
            done = carry_ref[...]
            dqkv_ref[...] = done.astype(BF16)
            bacc_ref[...] += done

        @pl.when(n < nb)
        def _():
            carry_ref[...] = new_ref[...]

        @pl.when(n == nb)
        def _():
            bsum_ref[...] = jnp.sum(bacc_ref[...], axis=1, keepdims=True)
            lane = lax.broadcasted_iota(jnp.int32, (1, nq), 1)
            dsk = jnp.zeros((1, nq), F32)
            for kv in range(ATTN_N_KV):
                acc = sacc_ref[kv]
                for r in range(ATTN_REP):
                    tot = jnp.sum(acc[:, r * w:(r + 1) * w], axis=1, keepdims=True)
                    dsk = jnp.where(lane == kv * ATTN_REP + r, tot, dsk)
            dsk_ref[...] = dsk

    cur = lambda n: jnp.minimum(n, nb - 1)
    prev = lambda n: jnp.maximum(jnp.minimum(n, nb - 1) - 1, 0)
    return pl.pallas_call(
        body, name="attn_bwd", grid=(nb + 1,),
        in_specs=_attn_specs_t(nb, cur, prev) + [pl.BlockSpec((qd, w), lambda n: (0, cur(n))),
                                                 pl.BlockSpec(sinks_rep.shape, lambda n: (0, 0, 0))],
        out_specs=[pl.BlockSpec((rows_all, w), lambda n: (0, jnp.maximum(n - 1, 0))),
                   pl.BlockSpec((rows_all, 1), lambda n: (0, 0)),
                   pl.BlockSpec((1, nq), lambda n: (0, 0))],
        out_shape=[jax.ShapeDtypeStruct((rows_all, t), BF16), jax.ShapeDtypeStruct((rows_all, 1), F32),
                   jax.ShapeDtypeStruct((1, nq), F32)],
        scratch_shapes=[pltpu.VMEM((rows_all, w), F32), pltpu.VMEM((rows_all, w), F32),
                        pltpu.VMEM((rows_all, w), F32), pltpu.VMEM(sinks_rep.shape, F32)],
        compiler_params=_params("arbitrary"))(qkv_t, qkv_t, qkv_t, qkv_t, qkv_t, do_t, sinks_rep)


HBM_SPEC = pl.BlockSpec(memory_space=pl.ANY)
HBM_ONLY = pl.BlockSpec(memory_space=pltpu.HBM)


def _comm_call(name, body, ins, out_shapes, n_sems):
    return pl.pallas_call(
        body, name=name, in_specs=[HBM_SPEC] * len(ins), out_specs=[HBM_SPEC] * len(out_shapes),
        out_shape=out_shapes,
        scratch_shapes=[pltpu.SemaphoreType.DMA((s,)) for s in n_sems])(*ins)


def _all_gather(name, shards, after):
    n = len(shards)

    def body(*refs):
        x_refs, out_refs = refs[:n], refs[n + 1:2 * n + 1]
        send_sems, recv_sems, local_sems = refs[2 * n + 1:]
        x, y, c = lax.axis_index("x"), lax.axis_index("y"), lax.axis_index("c")
        me, sibling = (x, y, c), (x, y, 1 - c)
        chips = [(1 - x, y), (x, 1 - y), (1 - x, 1 - y)]

        def slot(i, px, py, pc):
            return out_refs[i].at[4 * px + 2 * py + pc]

        def copy(k, i, block, to, src=None):
            return pltpu.make_async_remote_copy(
                src_ref=slot(i, *block) if src is None else src, dst_ref=slot(i, *block),
                send_sem=send_sems.at[k * n + i], recv_sem=recv_sems.at[k * n + i], device_id=to,
                device_id_type=MESH)

        mine = [pltpu.make_async_copy(x_refs[i], slot(i, *me), local_sems.at[i]) for i in range(n)]
        first = []
        for i in range(n):
            mine[i].start()
            first.append(copy(0, i, me, sibling, src=x_refs[i]))
            first += [copy(1 + j, i, me, (*chip, c), src=x_refs[i]) for j, chip in enumerate(chips)]
        for cp in first:
            cp.start()
        passed = []
        for i in range(n):
            for j, chip in enumerate(chips):
                copy(1 + j, i, (*chip, c), me).wait_recv()
                passed.append(copy(4 + j, i, (*chip, c), sibling))
                passed[-1].start()
        for i in range(n):
            copy(0, i, sibling, me).wait_recv()
            for j, chip in enumerate(chips):
                copy(4 + j, i, (*chip, 1 - c), me).wait_recv()
        for cp in first + passed:
            cp.wait_send()
        for cp in mine:
            cp.wait()

    outs = [jax.ShapeDtypeStruct((N_DEV,) + s.shape, s.dtype) for s in shards]
    return _comm_call(name, body, list(shards) + [after], outs, (7 * n, 7 * n, n))


SEM_SPEC = pl.BlockSpec(memory_space=pltpu.SEMAPHORE)
SPLIT_COPY_EFFECT = pltpu.SideEffectType.DATAFLOW_SIDE_EFFECTING


def _in_hbm(a):
    return pltpu.with_memory_space_constraint(a, pltpu.HBM)


def _split_start(name, body, srcs, lands, n_sems):
    n = len(srcs)
    bufs = [_in_hbm(a) for a in list(srcs) + list(lands)]
    outs = pl.pallas_call(
        body, name=name,
        out_shape=(pltpu.SemaphoreType.DMA((n_sems,)), pltpu.SemaphoreType.DMA((n_sems,)),
                   *[pltpu.HBM(a.shape, a.dtype) for a in bufs], jax.ShapeDtypeStruct((8, LANES), F32)),
        in_specs=[HBM_ONLY] * (2 * n),
        out_specs=(SEM_SPEC, SEM_SPEC, *[HBM_ONLY] * (2 * n), pl.BlockSpec(memory_space=pltpu.VMEM)),
        input_output_aliases={i: 2 + i for i in range(2 * n)},
        compiler_params=pltpu.CompilerParams(has_side_effects=SPLIT_COPY_EFFECT))(*bufs)
    return outs[0], outs[1], list(outs[2:2 + n]), list(outs[2 + n:2 + 2 * n]), outs[-1]


def _split_wait(name, body, send_sems, recv_sems, srcs, lands, after):
    n = len(srcs)
    outs = pl.pallas_call(
        body, name=name,
        out_shape=[pltpu.HBM(a.shape, a.dtype) for a in list(srcs) + list(lands)],
        in_specs=[HBM_ONLY] * (2 * n) + [SEM_SPEC, SEM_SPEC, HBM_SPEC],
        out_specs=[HBM_ONLY] * (2 * n),
        input_output_aliases={i: i for i in range(2 * n)},
        compiler_params=pltpu.CompilerParams(has_side_effects=SPLIT_COPY_EFFECT))(
            *srcs, *lands, send_sems, recv_sems, after)
    return list(outs[:n]), list(outs[n:])


N_PEERS = N_DEV - 1


def _gather_peers():
    x, y, c = lax.axis_index("x"), lax.axis_index("y"), lax.axis_index("c")
    flips = [(fx, fy, fc) for fx in (0, 1) for fy in (0, 1) for fc in (0, 1) if fx or fy or fc]
    return [(1 - x if fx else x, 1 - y if fy else y, 1 - c if fc else c) for fx, fy, fc in flips]


def _block_id(dev):
    return 4 * dev[0] + 2 * dev[1] + dev[2]


def _gather_start(name, shards):
    n = len(shards)

    def body(*refs):
        x_refs, land_refs = refs[:n], refs[n:2 * n]
        send_sems, recv_sems, token = refs[2 * n], refs[2 * n + 1], refs[-1]
        me = (lax.axis_index("x"), lax.axis_index("y"), lax.axis_index("c"))
        for i in range(n):
            for k, peer in enumerate(_gather_peers()):
                pltpu.make_async_remote_copy(
                    src_ref=x_refs[i], dst_ref=land_refs[i].at[_block_id(me)],
                    send_sem=send_sems.at[N_PEERS * i + k], recv_sem=recv_sems.at[N_PEERS * i + k],
                    device_id=peer, device_id_type=MESH).start()
        token[...] = jnp.zeros_like(token)

    lands = [lax.empty((N_DEV,) + s.shape, s.dtype) for s in shards]
    return _split_start(name, body, shards, lands, N_PEERS * n)


def _gather_wait(name, send_sems, recv_sems, first, shards, lands, after):
    n = len(shards)

    def body(*refs):
        x_refs, land_refs = refs[:n], refs[n:2 * n]
        send_sems, recv_sems = refs[2 * n], refs[2 * n + 1]
        for i in range(n):
            for k, peer in enumerate(_gather_peers()):
                cp = pltpu.make_async_remote_copy(
                    src_ref=x_refs[i], dst_ref=land_refs[i].at[_block_id(peer)],
                    send_sem=send_sems.at[N_PEERS * (first + i) + k],
                    recv_sem=recv_sems.at[N_PEERS * (first + i) + k],
                    device_id=peer, device_id_type=MESH)
                cp.wait_send()
                cp.wait_recv()

    return _split_wait(name, body, send_sems, recv_sems, shards, lands, after)


def _gather_forward(name, lands, shards):
    n = len(shards)

    def body(*refs):
        x_refs, out_refs = refs[n:2 * n], refs[2 * n:3 * n]
        send_sems, recv_sems, local_sems = refs[3 * n:]
        x, y, c = lax.axis_index("x"), lax.axis_index("y"), lax.axis_index("c")
        chips = [(1 - x, y), (x, 1 - y), (1 - x, 1 - y)]
        mine = [pltpu.make_async_copy(x_refs[i], out_refs[i].at[_block_id((x, y, c))], local_sems.at[i])
                for i in range(n)]
        passed = [pltpu.make_async_remote_copy(
            src_ref=out_refs[i].at[_block_id((*chip, c))], dst_ref=out_refs[i].at[_block_id((*chip, c))],
            send_sem=send_sems.at[3 * i + j], recv_sem=recv_sems.at[3 * i + j], device_id=(x, y, 1 - c),
            device_id_type=MESH) for i in range(n) for j, chip in enumerate(chips)]
        for cp in mine + passed:
            cp.start()
        for i in range(n):
            for j, chip in enumerate(chips):
                pltpu.make_async_remote_copy(
                    src_ref=out_refs[i].at[_block_id((*chip, c))], dst_ref=out_refs[i].at[_block_id((*chip, 1 - c))],
                    send_sem=send_sems.at[3 * i + j], recv_sem=recv_sems.at[3 * i + j], device_id=(x, y, 1 - c),
                    device_id_type=MESH).wait()
        for cp in mine:
            cp.wait()

    return pl.pallas_call(
        body, name=name, in_specs=[HBM_SPEC] * (2 * n), out_specs=[HBM_SPEC] * n,
        out_shape=[jax.ShapeDtypeStruct(a.shape, a.dtype) for a in lands],
        input_output_aliases={i: i for i in range(n)},
        scratch_shapes=[pltpu.SemaphoreType.DMA((3 * n,)), pltpu.SemaphoreType.DMA((3 * n,)),
                        pltpu.SemaphoreType.DMA((n,))])(*lands, *shards)


def _chip_peers():
    x, y, c = lax.axis_index("x"), lax.axis_index("y"), lax.axis_index("c")
    return [(1 - x, y, c), (x, 1 - y, c), (1 - x, 1 - y, c)]


def _chip_start(name, blocks):
    n = len(blocks)

    def body(*refs):
        p_refs, land_refs = refs[:n], refs[n:2 * n]
        send_sems, recv_sems, token = refs[2 * n], refs[2 * n + 1], refs[-1]
        for i in range(n):
            for j, peer in enumerate(_chip_peers()):
                pltpu.make_async_remote_copy(
                    src_ref=p_refs[i].at[j], dst_ref=land_refs[i].at[j], send_sem=send_sems.at[3 * i + j],
                    recv_sem=recv_sems.at[3 * i + j], device_id=peer, device_id_type=MESH).start()
        token[...] = jnp.zeros_like(token)

    lands = [lax.empty(b.shape, b.dtype) for b in blocks]
    return _split_start(name, body, blocks, lands, 3 * n)


def _chip_wait(name, send_sems, recv_sems, blocks, lands, after):
    n = len(blocks)

    def body(*refs):
        p_refs, land_refs = refs[:n], refs[n:2 * n]
        send_sems, recv_sems = refs[2 * n], refs[2 * n + 1]
        for i in range(n):
            for j, peer in enumerate(_chip_peers()):
                cp = pltpu.make_async_remote_copy(
                    src_ref=p_refs[i].at[j], dst_ref=land_refs[i].at[j], send_sem=send_sems.at[3 * i + j],
                    recv_sem=recv_sems.at[3 * i + j], device_id=peer, device_id_type=MESH)
                cp.wait_send()
                cp.wait_recv()

    return _split_wait(name, body, send_sems, recv_sems, blocks, lands, after)


def _scatter_start(name, blocks):
    n = len(blocks)

    def body(*refs):
        b_refs, land_refs = refs[:n], refs[n:2 * n]
        send_sems, recv_sems, token = refs[2 * n], refs[2 * n + 1], refs[-1]
        me = (lax.axis_index("x"), lax.axis_index("y"), lax.axis_index("c"))
        for i in range(n):
            for k, peer in enumerate(_gather_peers()):
                pltpu.make_async_remote_copy(
                    src_ref=b_refs[i].at[_block_id(peer)], dst_ref=land_refs[i].at[_block_id(me)],
                    send_sem=send_sems.at[N_PEERS * i + k], recv_sem=recv_sems.at[N_PEERS * i + k],
                    device_id=peer, device_id_type=MESH).start()
        token[...] = jnp.zeros_like(token)

    lands = [lax.empty(b.shape, b.dtype) for b in blocks]
    return _split_start(name, body, blocks, lands, N_PEERS * n)


def _scatter_wait(name, send_sems, recv_sems, blocks, lands, after):
    n = len(blocks)

    def body(*refs):
        b_refs, land_refs = refs[:n], refs[n:2 * n]
        send_sems, recv_sems = refs[2 * n], refs[2 * n + 1]
        for i in range(n):
            for k, peer in enumerate(_gather_peers()):
                cp = pltpu.make_async_remote_copy(
                    src_ref=b_refs[i].at[_block_id(peer)], dst_ref=land_refs[i].at[_block_id(peer)],
                    send_sem=send_sems.at[N_PEERS * i + k], recv_sem=recv_sems.at[N_PEERS * i + k],
                    device_id=peer, device_id_type=MESH)
                cp.wait_send()
                cp.wait_recv()

    return _split_wait(name, body, send_sems, recv_sems, blocks, lands, after)


def _pair_exchange(name, blocks):
    n = len(blocks)

    def body(*refs):
        g_refs, out_refs = refs[:n], refs[n:2 * n]
        send_sems, recv_sems = refs[2 * n:]
        x, y, c = lax.axis_index("x"), lax.axis_index("y"), lax.axis_index("c")
        copies = [pltpu.make_async_remote_copy(
            src_ref=g_refs[i].at[2 * k + 1 - c], dst_ref=out_refs[i].at[k], send_sem=send_sems.at[4 * i + k],
            recv_sem=recv_sems.at[4 * i + k], device_id=(x, y, 1 - c), device_id_type=MESH)
            for i in range(n) for k in range(4)]
        for cp in copies:
            cp.start()
        for cp in copies:
            cp.wait()

    outs = [jax.ShapeDtypeStruct((4,) + b.shape[1:], b.dtype) for b in blocks]
    return _comm_call(name, body, blocks, outs, (4 * n, 4 * n))


def _chip_exchange(name, blocks):
    n = len(blocks)

    def body(*refs):
        p_refs, out_refs = refs[:n], refs[n:2 * n]
        send_sems, recv_sems = refs[2 * n:]
        x, y, c = lax.axis_index("x"), lax.axis_index("y"), lax.axis_index("c")
        chips = [(1 - x, y), (x, 1 - y), (1 - x, 1 - y)]
        copies = [pltpu.make_async_remote_copy(
            src_ref=p_refs[i].at[j], dst_ref=out_refs[i].at[j], send_sem=send_sems.at[3 * i + j],
            recv_sem=recv_sems.at[3 * i + j], device_id=(*chip, c), device_id_type=MESH)
            for i in range(n) for j, chip in enumerate(chips)]
        for cp in copies:
            cp.start()
        for cp in copies:
            cp.wait()

    outs = [jax.ShapeDtypeStruct(b.shape, b.dtype) for b in blocks]
    return _comm_call(name, body, blocks, outs, (3 * n, 3 * n))


def _pair_sum(name, blocks, from_sibling, g_idx, r_idx):
    _, r, c_ = blocks.shape
    tr = _tile(r, 512, 16)

    def body(gi_ref, ri_ref, a_ref, b_ref, own_ref, send_ref):
        k = pl.program_id(1)
        s = a_ref[...] + b_ref[...]

        @pl.when(k == 0)
        def _():
            own_ref[...] = s

        @pl.when(k > 0)
        def _():
            send_ref[...] = s.astype(send_ref.dtype)

    return pl.pallas_call(
        body, name=name,
        grid_spec=pltpu.PrefetchScalarGridSpec(
            num_scalar_prefetch=2, grid=(r // tr, 4),
            in_specs=[pl.BlockSpec((None, tr, c_), lambda i, k, gi, ri: (gi[k], i, 0)),
                      pl.BlockSpec((None, tr, c_), lambda i, k, gi, ri: (ri[k], i, 0))],
            out_specs=[pl.BlockSpec((None, tr, c_), lambda i, k, gi, ri: (0, i, 0)),
                       pl.BlockSpec((None, tr, c_), lambda i, k, gi, ri: (jnp.maximum(k - 1, 0), i, 0))]),
        out_shape=[jax.ShapeDtypeStruct((1, r, c_), F32), jax.ShapeDtypeStruct((3, r, c_), PAYLOAD)],
        compiler_params=_params("parallel", "arbitrary"))(g_idx, r_idx, blocks, from_sibling)


def _adamw(w, g, m, v):
    m = ADAM_B1 * m + (1.0 - ADAM_B1) * g
    v = ADAM_B2 * v + (1.0 - ADAM_B2) * (g * g)
    m_hat = m / (1.0 - ADAM_B1 ** ADAM_STEP)
    v_hat = v / (1.0 - ADAM_B2 ** ADAM_STEP)
    delta = -ADAM_LR * (m_hat / (jnp.sqrt(v_hat) + ADAM_EPS) + ADAM_WD * w)
    return delta, m, v


def _adamw_tiles(r, c_):
    tr = _tile(r, 256, 16)
    return (tr, c_) if tr < r or r <= 256 else (r, _tile(c_, 256))


def _sum_parts(part):
    g = part[0].astype(F32)
    for k in range(1, part.shape[0]):
        g = g + part[k].astype(F32)
    return g


def _sum_adamw(name, parts, w, m, v):
    r, c_ = w.shape
    tr, tc = _adamw_tiles(r, c_)

    def body(p_ref, w_ref, m_ref, v_ref, g_ref, d_ref, nm_ref, nv_ref):
        g = _sum_parts(p_ref)
        g_ref[...] = g
        d_ref[...], nm_ref[...], nv_ref[...] = _adamw(w_ref[...], g, m_ref[...], v_ref[...])

    tile = pl.BlockSpec((tr, tc), lambda i, j: (i, j))
    return pl.pallas_call(body, name=name, grid=(r // tr, c_ // tc),
                          in_specs=[pl.BlockSpec((parts.shape[0], tr, tc), lambda i, j: (0, i, j)), tile, tile, tile],
                          out_specs=[tile] * 4, out_shape=[jax.ShapeDtypeStruct((r, c_), F32)] * 4,
                          compiler_params=_params("parallel", "parallel"))(parts, w, m, v)


def _sum_adamw_layers(name, parts, w, m, v):
    n_layers, r, c_ = w.shape
    tr = _tile(r, 256, 16)

    def body(*refs):
        p_refs = refs[:n_layers]
        w_ref, m_ref, v_ref, g_ref, d_ref, nm_ref, nv_ref = refs[n_layers:]
        layer = pl.program_id(0)
        g = _sum_parts(p_refs[0])
        for li in range(1, n_layers):
            g = jnp.where(layer == li, _sum_parts(p_refs[li]), g)
        g_ref[...] = g
        d_ref[...], nm_ref[...], nv_ref[...] = _adamw(w_ref[...], g, m_ref[...], v_ref[...])

    row = pl.BlockSpec((None, tr, c_), lambda l, i: (l, i, 0))
    specs = [pl.BlockSpec((p.shape[0], tr, c_), lambda l, i: (0, i, 0)) for p in parts]
    return pl.pallas_call(body, name=name, grid=(n_layers, r // tr), in_specs=specs + [row, row, row],
                          out_specs=[row] * 4, out_shape=[jax.ShapeDtypeStruct(w.shape, F32)] * 4,
                          compiler_params=_params("parallel", "parallel"))(*parts, w, m, v)


def _pack_rows(flat, n_rows, cols):
    pad = n_rows * cols - flat.shape[-1]
    flat = jnp.pad(flat, [(0, 0)] * (flat.ndim - 1) + [(0, pad)])
    return flat.reshape(flat.shape[:-1] + (n_rows, cols))


def _cols_join(blocks):
    return jnp.concatenate([blocks[d] for d in range(N_DEV)], axis=1)


def _cols_split(full):
    c = full.shape[1] // N_DEV
    return jnp.stack([full[:, d * c:(d + 1) * c] for d in range(N_DEV)])


def _rows_join(blocks):
    return blocks.reshape(N_DEV * blocks.shape[1], blocks.shape[2])


def _rows_split(full):
    return full.reshape(N_DEV, full.shape[0] // N_DEV, full.shape[1])


def _perm_xbc(a, ng):
    lead = a.shape[:-1]
    di, gn = ng * GW, ng * SSD_D_STATE
    xs = a[..., :di].reshape(lead + (ng, GW))
    bs = a[..., di:di + gn].reshape(lead + (ng, SSD_D_STATE))
    cs = a[..., di + gn:].reshape(lead + (ng, SSD_D_STATE))
    return jnp.concatenate([xs, bs, cs], axis=-1).reshape(lead + (ng * GC,))


def _unperm_xbc(a, ng):
    lead = a.shape[:-1]
    g = a.reshape(lead + (ng, GC))
    return jnp.concatenate([g[..., :GW].reshape(lead + (ng * GW,)),
                            g[..., GW:GW + SSD_D_STATE].reshape(lead + (ng * SSD_D_STATE,)),
                            g[..., GW + SSD_D_STATE:].reshape(lead + (ng * SSD_D_STATE,))], axis=-1)


def _heads_col(v, ng):
    return jnp.pad(v.reshape(ng, 1, SSD_HPG), ((0, 0), (0, 0), (0, LANES - SSD_HPG)))


def _heads_row(v, ng):
    return jnp.pad(v.reshape(ng, SSD_HPG, 1), ((0, 0), (0, 8 - SSD_HPG), (0, 0)))


MATRIX_ITEMS = ("w_in", "w_out", "up0", "down0", "w_qkv", "w_o", "up1", "down1")
VECTOR_ITEMS = ("conv_w", "b_qkv", "b_o")
ITEMS = MATRIX_ITEMS + VECTOR_ITEMS
GATHER_STAGES = (("w_in", "conv_w"), ("w_out", "up0", "down0"), ("w_qkv", "b_qkv", "w_o", "b_o", "up1", "down1"))


def _items(tree, prefix=""):
    g = lambda k: tree[prefix + k]
    return {"w_in": g("ssd_w_in")[0].T, "w_out": g("ssd_w_out")[0], "w_qkv": g("attn_w_qkv")[0].T,
            "w_o": g("attn_w_o")[0], "up0": g("mlp_w_up")[0], "up1": g("mlp_w_up")[1],
            "down0": g("mlp_w_down")[0], "down1": g("mlp_w_down")[1], "conv_w": g("ssd_conv_w")[0],
            "b_qkv": g("attn_b_qkv"), "b_o": g("attn_b_o")}


def _from_items(it):
    return {"ssd_w_in": it["w_in"][None], "ssd_w_out": it["w_out"][None], "attn_w_qkv": it["w_qkv"].T[None],
            "attn_w_o": it["w_o"][None], "mlp_w_up": jnp.stack([it["up0"], it["up1"]]),
            "mlp_w_down": jnp.stack([it["down0"], it["down1"]]), "ssd_conv_w": it["conv_w"][None],
            "attn_b_qkv": it["b_qkv"], "attn_b_o": it["b_o"]}


REPLICATED = ("ssd_conv_b", "ssd_dt_bias", "ssd_a_log", "ssd_d", "ssd_norm_w", "attn_sinks", "mix_pre_norm",
              "mix_post_norm", "ffn_pre_norm", "ffn_post_norm")
WEIGHTS = ("ssd_w_in", "ssd_conv_w", "ssd_conv_b", "ssd_dt_bias", "ssd_a_log", "ssd_d", "ssd_norm_w", "ssd_w_out",
           "attn_w_qkv", "attn_b_qkv", "attn_sinks", "attn_w_o", "attn_b_o", "mlp_w_up", "mlp_w_down",
           "mix_pre_norm", "mix_post_norm", "ffn_pre_norm", "ffn_post_norm")


def _forward_backward(x, target, rep, token, weights_of_stage, reduce_grads):
    t, d = x.shape
    ng = rep["ssd_norm_w"].shape[1] // GW
    di = ng * GW
    n_xbc = ng * GC
    nh = ng * SSD_HPG
    grads, blocks = {}, {}
    w_up, w_down = [None, None], [None, None]
    sinks_rep = jnp.repeat(rep["attn_sinks"].reshape(ATTN_N_KV, ATTN_REP, 1), ATTN_WINDOW, axis=2).reshape(
        ATTN_N_KV, 1, ATTN_REP * ATTN_WINDOW)
    conv_b = rep["ssd_conv_b"]
    gn = ng * SSD_D_STATE
    parts = ((0, di), (di, di), (2 * di, gn), (2 * di + gn, gn), (di + n_xbc, nh))
    alog_c, dsk_c = (_heads_col(rep[k], ng) for k in ("ssd_a_log", "ssd_d"))
    bias_l, alog_l = (jnp.pad(rep[k], ((0, 0), (0, LANES - nh))) for k in ("ssd_dt_bias", "ssd_a_log"))
    norm = {k: rep[k] for k in ("mix_pre_norm", "mix_post_norm", "ffn_pre_norm", "ffn_post_norm")}

    def nrow(name, i):
        return norm[name][i:i + 1]

    def mlp_fwd(i, u2):
        p = _mm(f"mlp{i}_up", [u2], [w_up[i]], "nn", tm=1024, tn=1024, out_dtypes=(BF16,),
                epilogue=lambda acc: (jnp.square(jnp.maximum(acc, 0.0)),))
        f = _mm(f"mlp{i}_down", [p], [w_down[i]], "nn", tm=512, tn=1024)
        return p, f

    def mlp_bwd(i, df, u2, p):
        da = _mm(f"mlp{i}_dact", [df], [w_down[i]], "nt", tm=1024, tn=1024, out_dtypes=(BF16,),
                 tiles=(p,), epilogue=lambda acc, pv: (acc * (2.0 * jnp.sqrt(pv.astype(F32))),))
        blocks[f"down{i}"] = _rows_split(_mm(f"mlp{i}_dwdown", [p], [df], "tn", tm=512, tn=1024,
                                             out_dtypes=(PAYLOAD,)))
        blocks[f"up{i}"] = _mm(f"mlp{i}_dwup", [u2], [da], "tn", tm=1024, tn=da.shape[1] // N_DEV,
                               out_dtypes=(PAYLOAD,), col_blocks=True)
        return _mm(f"mlp{i}_dx", [da], [w_up[i]], "nt", tm=512, tn=1024)

    u0 = _prenorm("l0_prenorm", x, nrow("mix_pre_norm", 0), token)
    got = weights_of_stage(0, u0)
    w_in_t = _rows_join(got["w_in"])
    w_dt_t = jnp.pad(w_in_t[di + n_xbc:], ((0, LANES - nh), (0, 0)))
    conv_w = _cols_join(got["conv_w"])
    zx = _mm("ssd_in_proj", [u0], [w_in_t], "nt", tm=1024, tn=1024, n_use=di + n_xbc)
    zdt = _mm("ssd_dt_proj", [u0], [w_dt_t], "nt", tm=1024, tn=LANES)
    pre = _conv_fwd(zx, di, n_xbc, conv_w, conv_b)
    dt_c, cum_c, cum_r, sgd_c = _ssd_dt_prep(zdt, bias_l, alog_l, ng)
    y, states = _ssd_fwd(pre, dt_c, cum_c, cum_r, alog_c, dsk_c)
    yn = _gate_norm_fwd(y, zx, rep["ssd_norm_w"])
    got = weights_of_stage(1, yn)
    w_out = _rows_join(got["w_out"])
    w_up[0], w_down[0] = _cols_join(got["up0"]), _rows_join(got["down0"])
    mix0 = _mm("ssd_out_proj", [yn], [w_out], "nn", tm=1024, tn=1024)
    h1, u0f = _post_pre("l0_mid", x, mix0, nrow("mix_post_norm", 0), nrow("ffn_pre_norm", 0))
    p0, f0 = mlp_fwd(0, u0f)
    h2, u1 = _post_pre("l1_in", h1, f0, nrow("ffn_post_norm", 0), nrow("mix_pre_norm", 1))
    got = weights_of_stage(2, u1)
    w_qkv_t = _rows_join(got["w_qkv"])
    w_o = _rows_join(got["w_o"])
    b_qkv_col = got["b_qkv"].reshape(-1, 1)
    b_o = _cols_join(got["b_o"])
    w_up[1], w_down[1] = _cols_join(got["up1"]), _rows_join(got["down1"])
    qkv_t = _mm("attn_qkv_proj", [w_qkv_t], [u1], "nt", tm=768, tn=1024, out_dtypes=(BF16,), cols=(b_qkv_col,),
                epilogue=lambda acc, b: (acc + b,))
    ao_t = _attn_fwd_t(qkv_t, sinks_rep)
    mix1 = _mm("attn_out_proj", [ao_t], [w_o], "tn", tm=1024, tn=1024, rows=(b_o,),
               epilogue=lambda acc, b: (acc + b,))
    h3, u1f = _post_pre("l1_mid", h2, mix1, nrow("mix_post_norm", 1), nrow("ffn_pre_norm", 1))
    p1, f1 = mlp_fwd(1, u1f)
    dh, loss_row = _final_loss("loss", h3, f1, nrow("ffn_post_norm", 1), target)

    g_norm = {k: [None, None] for k in norm}
    df1, g_norm["ffn_post_norm"][1], _ = _norm_bwd("l1_ffn_post_bwd", dh, post=(f1, nrow("ffn_post_norm", 1)))
    du = mlp_bwd(1, df1, u1f, p1)
    sent = reduce_grads("mlp1", {k: blocks[k] for k in ("up1", "down1")})
    dh, g_norm["ffn_pre_norm"][1], dmix1, g_norm["mix_post_norm"][1], db_o = _norm_bwd(
        "l1_mid_bwd", dh, pre=(du, h3, nrow("ffn_pre_norm", 1)), post=(mix1, nrow("mix_post_norm", 1)), after=sent)
    blocks["b_o"] = _cols_split(db_o)
    blocks["w_o"] = _rows_split(_mm("attn_dwo", [ao_t], [dmix1], "nn", tm=512, tn=1024, out_dtypes=(PAYLOAD,)))
    dao_t = _mm("attn_dout", [w_o], [dmix1], "nt", tm=1024, tn=1024, out_dtypes=(BF16,))
    dqkv_t, db_qkv, grads["attn_sinks"] = _attn_bwd_t(qkv_t, dao_t, sinks_rep)
    blocks["b_qkv"] = db_qkv.reshape(N_DEV, 1, -1)
    blocks["w_qkv"] = _rows_split(_mm("attn_dwqkv", [dqkv_t], [u1], "nn", tm=512, tn=1024, out_dtypes=(PAYLOAD,)))
    du = _mm("attn_dx", [dqkv_t], [w_qkv_t], "tn", tm=1024, tn=1024)
    sent = reduce_grads("attn", {k: blocks[k] for k in ("w_o", "w_qkv", "b_o", "b_qkv")})
    dh, g_norm["mix_pre_norm"][1], df0, g_norm["ffn_post_norm"][0], _ = _norm_bwd(
        "l1_in_bwd", dh, pre=(du, h2, nrow("mix_pre_norm", 1)), post=(f0, nrow("ffn_post_norm", 0)), after=sent)
    du = mlp_bwd(0, df0, u0f, p0)
    sent = reduce_grads("mlp0", {k: blocks[k] for k in ("up0", "down0")})
    dh, g_norm["ffn_pre_norm"][0], dmix0, g_norm["mix_post_norm"][0], _ = _norm_bwd(
        "l0_mid_bwd", dh, pre=(du, h1, nrow("ffn_pre_norm", 0)), post=(mix0, nrow("mix_post_norm", 0)), after=sent)
    blocks["w_out"] = _rows_split(_mm("ssd_dwout", [yn], [dmix0], "tn", tm=512, tn=1024, out_dtypes=(PAYLOAD,)))
    dyn = _mm("ssd_dyn", [dmix0], [w_out], "nt", tm=1024, tn=1024)
    dy, dz, grads["ssd_norm_w"] = _gate_norm_bwd(dyn, y, zx, rep["ssd_norm_w"])
    dpx, dpb, dpc, ddt_g, dbias_g, dalog_g, dd_g = _ssd_bwd(dy, pre, states, dt_c, cum_c, cum_r, sgd_c, alog_c,
                                                             dsk_c)
    conv_out = [_conv_bwd(f"ssd_conv_bwd_{tag}", dp, zx, c0, conv_w[:, c0 - di:c0 - di + n])
                for tag, dp, (c0, n) in zip("xbc", (dpx, dpb, dpc), parts[1:4])]
    dconv_w = jnp.concatenate([o[1] for o in conv_out], axis=1)
    dconv_b = jnp.concatenate([o[2] for o in conv_out], axis=1)
    ddt = jnp.transpose(ddt_g[:, :, :SSD_HPG], (1, 0, 2)).reshape(t, nh)
    ddt = jnp.pad(ddt, ((0, 0), (0, LANES - nh))).astype(BF16)
    blocks["conv_w"] = _cols_split(dconv_w)
    grads["ssd_conv_b"] = dconv_b
    for name, val in (("ssd_dt_bias", dbias_g), ("ssd_a_log", dalog_g), ("ssd_d", dd_g)):
        grads[name] = val[:, 0, :SSD_HPG].reshape(1, nh)
    d_zx = [dz] + [o[0] for o in conv_out] + [ddt]
    dw_parts = [_mm(f"ssd_dw_{tag}", [d], [u0], "tn", tm=512, tn=1024, out_dtypes=(PAYLOAD,))
                for tag, d in zip("zxbct", d_zx)]
    dw_parts[-1] = dw_parts[-1][:nh]
    blocks["w_in"] = _rows_split(jnp.concatenate(dw_parts, axis=0))
    sent = reduce_grads("ssd", {k: blocks[k] for k in ("w_in", "w_out", "conv_w")})
    w_parts = [w_in_t[r0:r0 + n] for r0, n in parts[:-1]] + [w_dt_t]
    du = _mm("ssd_dx", d_zx, w_parts, "nn", tm=256, tn=1024)
    grad_x, g_norm["mix_pre_norm"][0] = _norm_bwd("l0_in_bwd", dh, pre=(du, x, nrow("mix_pre_norm", 0)), after=sent)
    for k in norm:
        grads[k] = jnp.concatenate(g_norm[k], axis=0)
    return loss_row, grad_x, grads


def kernel(x, ssd_w_in, ssd_conv_w, ssd_conv_b, ssd_dt_bias, ssd_a_log, ssd_d, ssd_norm_w, ssd_w_out, attn_w_qkv, attn_b_qkv, attn_sinks, attn_w_o, attn_b_o, mlp_w_up, mlp_w_down, mix_pre_norm, mix_post_norm, ffn_pre_norm, ffn_post_norm, loss_target, m_ssd_w_in, m_ssd_conv_w, m_ssd_conv_b, m_ssd_dt_bias, m_ssd_a_log, m_ssd_d, m_ssd_norm_w, m_ssd_w_out, m_attn_w_qkv, m_attn_b_qkv, m_attn_sinks, m_attn_w_o, m_attn_b_o, m_mlp_w_up, m_mlp_w_down, m_mix_pre_norm, m_mix_post_norm, m_ffn_pre_norm, m_ffn_post_norm, v_ssd_w_in, v_ssd_conv_w, v_ssd_conv_b, v_ssd_dt_bias, v_ssd_a_log, v_ssd_d, v_ssd_norm_w, v_ssd_w_out, v_attn_w_qkv, v_attn_b_qkv, v_attn_sinks, v_attn_w_o, v_attn_b_o, v_mlp_w_up, v_mlp_w_down, v_mix_pre_norm, v_mix_post_norm, v_ffn_pre_norm, v_ffn_post_norm):
    given = dict(locals())
    w = {k: given[k] for k in WEIGHTS}
    mom_m = {k: given["m_" + k] for k in WEIGHTS}
    mom_v = {k: given["v_" + k] for k in WEIGHTS}
    w_it, m_it, v_it = _items(given), _items(given, "m_"), _items(given, "v_")

    order = [k for stage in GATHER_STAGES for k in stage]
    shards = [w_it[k].astype(PAYLOAD) if k in MATRIX_ITEMS else w_it[k] for k in order]
    g_send, g_recv, shards, lands, token = _gather_start("gather_start", shards)

    def weights_of_stage(s, after):
        first = sum(len(stage) for stage in GATHER_STAGES[:s])
        sl = slice(first, first + len(GATHER_STAGES[s]))
        srcs, got = _gather_wait(f"gather_wait{s}", g_send, g_recv, first, shards[sl], lands[sl], after)
        me = 4 * ix + 2 * iy + ic
        return {k: lax.dynamic_update_slice(land, src[None], (me,) + (0,) * src.ndim)
                for k, land, src in zip(GATHER_STAGES[s], got, srcs)}

    ix, iy, ic = lax.axis_index("x"), lax.axis_index("y"), lax.axis_index("c")
    in_flight = []

    def reduce_grads(tag, blocks):
        keys = list(blocks)
        started = _scatter_start(f"rs_start_{tag}", [blocks[k] for k in keys])
        in_flight.append((tag, keys, started))
        return started[-1]

    rep = {k: w[k] for k in REPLICATED}
    loss_row, grad_x, grads = _forward_backward(x[0], loss_target[0], rep, token, weights_of_stage, reduce_grads)

    def pack_rep(tree, last):
        flat = jnp.concatenate([tree[k].reshape(-1) for k in REPLICATED] + [last])
        return _pack_rows(flat, _round_up(-(-flat.shape[0] // LANES), 8), LANES)

    landed = {}
    me = 4 * ix + 2 * iy + ic

    def wait_group(group, after):
        tag, keys, (s_send, s_recv, srcs, s_lands, _) = group
        srcs, got = _scatter_wait(f"rs_wait_{tag}", s_send, s_recv, srcs, s_lands, after)
        for k, src, land in zip(keys, srcs, got):
            own = lax.dynamic_index_in_dim(src, me, 0, keepdims=True)
            landed[k] = lax.dynamic_update_slice(land, own, (me,) + (0,) * (land.ndim - 1))

    def adamw_item(k):
        return _sum_adamw(f"adamw_{k}", landed[k], w_it[k], m_it[k], v_it[k])

    def adamw_stack(name, keys):
        return _sum_adamw_layers(f"adamw_{name}", [landed[k] for k in keys], given[name], given["m_" + name],
                                 given["v_" + name])

    for group in in_flight[:-1]:
        wait_group(group, grad_x)
    done = {"mlp_w_up": adamw_stack("mlp_w_up", ("up0", "up1")),
            "mlp_w_down": adamw_stack("mlp_w_down", ("down0", "down1")),
            "attn_w_qkv": [o.T[None] for o in adamw_item("w_qkv")],
            "attn_w_o": [o[None] for o in adamw_item("w_o")],
            "attn_b_qkv": adamw_item("b_qkv"), "attn_b_o": adamw_item("b_o")}
    partials, = _all_gather("gather_small_grads", [pack_rep(grads, loss_row[0, :1])], done["mlp_w_down"][1])
    wait_group(in_flight[-1], partials)
    done["ssd_w_in"] = [o.T[None] for o in adamw_item("w_in")]
    for name, k in (("ssd_w_out", "w_out"), ("ssd_conv_w", "conv_w")):
        done[name] = [o[None] for o in adamw_item(k)]
    zero = jnp.zeros((1,), F32)
    rep_out = _sum_adamw("adamw_replicated", partials, pack_rep(w, zero), pack_rep(mom_m, zero), pack_rep(mom_v, zero))

    kinds = []
    for kind, r_arr in enumerate(rep_out):
        tree = {name: outs4[kind] for name, outs4 in done.items()}
        flat, off = r_arr.reshape(-1), 0
        for k in REPLICATED:
            tree[k] = flat[off:off + w[k].size].reshape(w[k].shape)
            off += w[k].size
        kinds.append(tree)
    loss = rep_out[0].reshape(-1)[off]
    outs = [loss, grad_x[None]]
    for tree in kinds:
        outs += [tree[k] for k in WEIGHTS]
    return tuple(outs)
```

```python
import functools

import jax
import jax.numpy as jnp
from jax import lax
from jax.experimental import pallas as pl
from jax.experimental.pallas import tpu as pltpu

F32 = jnp.float32
BF16 = jnp.bfloat16
PAYLOAD = jnp.bfloat16
HIGHEST = lax.Precision.HIGHEST
MESH = pl.DeviceIdType.MESH

NORM_EPS = 1e-6
SSD_HEAD_DIM = 64
SSD_N_GROUPS = 8
SSD_HPG = 4
SSD_D_STATE = 128
SSD_CONV_WIDTH = 4
SSD_CHUNK = 128
ATTN_HEAD_DIM = 64
ATTN_N_KV = 4
ATTN_REP = 4
ATTN_WINDOW = 128
ADAM_LR = 0.001
ADAM_B1 = 0.9
ADAM_B2 = 0.999
ADAM_EPS = 1e-08
ADAM_WD = 0.01
ADAM_STEP = 10

N_DEV = 8
LANES = 128
PACK_COLS = 1024
V7X_VMEM_LIMIT = 56 * 1024 * 1024

GW = SSD_HPG * SSD_HEAD_DIM
GC = GW + 2 * SSD_D_STATE


def _params(*sem):
    return pltpu.CompilerParams(dimension_semantics=sem, vmem_limit_bytes=V7X_VMEM_LIMIT)


def _tile(n, pref, mult=LANES):
    best = None
    t = mult
    while t <= min(n, pref):
        if n % t == 0:
            best = t
        t += mult
    return best if best is not None else n


def _round_up(n, m):
    return (n + m - 1) // m * m


def _acc(ref, val, first):
    @pl.when(first)
    def _():
        ref[...] = val

    @pl.when(jnp.logical_not(first))
    def _():
        ref[...] += val


def _dot(a, b):
    return lax.dot_general(a, b, (((1,), (0,)), ((), ())), preferred_element_type=F32)


def _dot_nt(a, b):
    return lax.dot_general(a, b, (((1,), (1,)), ((), ())), preferred_element_type=F32)


def _dot_tn(a, b):
    return lax.dot_general(a, b, (((0,), (0,)), ((), ())), preferred_element_type=F32)


def _dot_f32(a, b):
    return lax.dot_general(a, b, (((1,), (0,)), ((), ())), preferred_element_type=F32, precision=HIGHEST)


_DOTS = {"nn": _dot, "nt": _dot_nt, "tn": _dot_tn}


def _sigmoid(x):
    return 1.0 / (1.0 + jnp.exp(-x))


def _softplus(x):
    return jnp.maximum(x, 0.0) + jnp.log1p(jnp.exp(-jnp.abs(x)))


def _silu_grad(x, s):
    return s * (1.0 + x * (1.0 - s))


def _mm(name, a_list, b_list, mode, *, tm, tn, out_dtypes=(F32,), epilogue=None, tiles=(), rows=(), cols=(),
        col_blocks=False, n_use=None):
    npair = len(a_list)
    if mode == "tn":
        m = a_list[0].shape[1]
    else:
        m = a_list[0].shape[0]
    n = n_use if n_use is not None else (b_list[0].shape[0] if mode == "nt" else b_list[0].shape[1])
    tm = _tile(m, tm, LANES if mode == "tn" else 8)
    tn = _tile(n, tn)
    assert m % tm == 0 and n % tn == 0, (name, m, n, tm, tn)
    dot = _DOTS[mode]

    def body(*refs):
        a_refs = refs[:npair]
        b_refs = refs[npair:2 * npair]
        n_extra = len(tiles) + len(rows) + len(cols)
        e_refs = refs[2 * npair:2 * npair + n_extra]
        o_refs = refs[2 * npair + n_extra:]
        acc = None
        for ar, br in zip(a_refs, b_refs):
            d = dot(ar[...], br[...])
            acc = d if acc is None else acc + d
        outs = epilogue(acc, *[e[...] for e in e_refs]) if epilogue is not None else (acc,)
        for o, v in zip(o_refs, outs):
            o[...] = v.astype(o.dtype)

    in_specs = []
    for a in a_list:
        if mode == "tn":
            in_specs.append(pl.BlockSpec((a.shape[0], tm), lambda i, j: (0, i)))
        else:
            in_specs.append(pl.BlockSpec((tm, a.shape[1]), lambda i, j: (i, 0)))
    for b in b_list:
        if mode == "nt":
            in_specs.append(pl.BlockSpec((tn, b.shape[1]), lambda i, j: (j, 0)))
        else:
            in_specs.append(pl.BlockSpec((b.shape[0], tn), lambda i, j: (0, j)))
    in_specs += [pl.BlockSpec((tm, tn), lambda i, j: (i, j)) for _ in tiles]
    in_specs += [pl.BlockSpec((1, tn), lambda i, j: (0, j)) for _ in rows]
    in_specs += [pl.BlockSpec((tm, 1), lambda i, j: (i, 0)) for _ in cols]
    outs = pl.pallas_call(
        body,
        name=name,
        grid=(m // tm, n // tn),
        in_specs=in_specs,
        out_specs=[pl.BlockSpec((None, tm, tn), lambda i, j: (j, i, 0)) if col_blocks else
                   pl.BlockSpec((tm, tn), lambda i, j: (i, j)) for _ in out_dtypes],
        out_shape=[jax.ShapeDtypeStruct((n // tn, m, tn) if col_blocks else (m, n), dt) for dt in out_dtypes],
        compiler_params=_params("parallel", "parallel"),
    )(*a_list, *b_list, *tiles, *rows, *cols)
    return outs[0] if len(out_dtypes) == 1 else outs


def _rms(x, w):
    r = lax.rsqrt(jnp.mean(x * x, axis=-1, keepdims=True) + NORM_EPS)
    return x * r * w


def _rms_bwd(x, w, dy):
    r = lax.rsqrt(jnp.mean(x * x, axis=-1, keepdims=True) + NORM_EPS)
    xh = x * r
    g = dy * w
    dx = r * (g - xh * jnp.mean(g * xh, axis=-1, keepdims=True))
    return dx, dy * xh


def _row_specs(tr, d):
    return pl.BlockSpec((tr, d), lambda i: (i, 0)), pl.BlockSpec((1, d), lambda i: (0, 0))


def _prenorm(name, h, w, after):
    t, d = h.shape
    tr = _tile(t, 512, 8)
    row, vec = _row_specs(tr, d)

    def body(h_ref, w_ref, after_ref, u_ref):
        u_ref[...] = _rms(h_ref[...], w_ref[...]).astype(BF16)

    return pl.pallas_call(body, name=name, grid=(t // tr,),
                          in_specs=[row, vec, pl.BlockSpec((8, LANES), lambda i: (0, 0))], out_specs=row,
                          out_shape=jax.ShapeDtypeStruct((t, d), BF16), compiler_params=_params("parallel"))(
                              h, w, after)


def _post_pre(name, h, m, w_post, w_pre):
    t, d = h.shape
    tr = _tile(t, 512, 8)
    row, vec = _row_specs(tr, d)

    def body(h_ref, m_ref, wq_ref, wp_ref, hn_ref, u_ref):
        hn = h_ref[...] + _rms(m_ref[...], wq_ref[...])
        hn_ref[...] = hn
        u_ref[...] = _rms(hn, wp_ref[...]).astype(BF16)

    return pl.pallas_call(body, name=name, grid=(t // tr,), in_specs=[row, row, vec, vec], out_specs=[row, row],
                          out_shape=[jax.ShapeDtypeStruct((t, d), F32), jax.ShapeDtypeStruct((t, d), BF16)],
                          compiler_params=_params("parallel"))(h, m, w_post, w_pre)


def _final_loss(name, h, m, w_post, target):
    t, d = h.shape
    tr = _tile(t, 512, 8)
    row, vec = _row_specs(tr, d)

    def body(h_ref, m_ref, wq_ref, t_ref, dh_ref, loss_ref):
        err = h_ref[...] + _rms(m_ref[...], wq_ref[...]) - t_ref[...]
        dh_ref[...] = err * (1.0 / d)
        part = 0.5 * jnp.sum(jnp.mean(err * err, axis=-1, keepdims=True), axis=0, keepdims=True)
        _acc(loss_ref, jnp.broadcast_to(part, (1, LANES)), pl.program_id(0) == 0)

    return pl.pallas_call(body, name=name, grid=(t // tr,), in_specs=[row, row, vec, row],
                          out_specs=[row, pl.BlockSpec((1, LANES), lambda i: (0, 0))],
                          out_shape=[jax.ShapeDtypeStruct((t, d), F32), jax.ShapeDtypeStruct((1, LANES), F32)],
                          compiler_params=_params("arbitrary"))(h, m, w_post, target)


def _norm_bwd(name, dh, pre=None, post=None, after=None):
    t, d = dh.shape
    tr = _tile(t, 256, 8)
    row, vec = _row_specs(tr, d)
    has_pre, has_post = pre is not None, post is not None

    def body(*refs):
        it = iter(refs)
        dh_ref = next(it)
        if has_pre:
            du_ref, x_ref, wp_ref = next(it), next(it), next(it)
        if has_post:
            m_ref, wq_ref = next(it), next(it)
        if after is not None:
            next(it)
        first = pl.program_id(0) == 0
        dh_v = dh_ref[...]
        if has_pre:
            dhn_ref, dwp_ref = next(it), next(it)
            dx, dwr = _rms_bwd(x_ref[...], wp_ref[...], du_ref[...])
            dh_v = dh_v + dx
            dhn_ref[...] = dh_v
            _acc(dwp_ref, jnp.sum(dwr, axis=0, keepdims=True), first)
        if has_post:
            dm_ref, dwq_ref, dms_ref = next(it), next(it), next(it)
            dm, dwr = _rms_bwd(m_ref[...], wq_ref[...], dh_v)
            dm_ref[...] = dm.astype(BF16)
            _acc(dwq_ref, jnp.sum(dwr, axis=0, keepdims=True), first)
            _acc(dms_ref, jnp.sum(dm, axis=0, keepdims=True), first)

    ins, in_specs, out_specs, out_shape = [dh], [row], [], []
    if has_pre:
        ins += list(pre)
        in_specs += [row, row, vec]
        out_specs += [row, vec]
        out_shape += [jax.ShapeDtypeStruct((t, d), F32), jax.ShapeDtypeStruct((1, d), F32)]
    if has_post:
        ins += list(post)
        in_specs += [row, vec]
        out_specs += [row, vec, vec]
        out_shape += [jax.ShapeDtypeStruct((t, d), BF16), jax.ShapeDtypeStruct((1, d), F32),
                      jax.ShapeDtypeStruct((1, d), F32)]
    if after is not None:
        ins.append(after)
        in_specs.append(pl.BlockSpec((8, LANES), lambda i: (0, 0)))
    return pl.pallas_call(body, name=name, grid=(t // tr,), in_specs=in_specs, out_specs=out_specs,
                          out_shape=out_shape, compiler_params=_params("arbitrary"))(*ins)


HALO = 8


def _shift_later(cur, prev, s):
    rolled = pltpu.roll(cur, s, 0)
    row = lax.broadcasted_iota(jnp.int32, prev.shape, 0)
    first = jnp.where(row < s, pltpu.roll(prev, s, 0), rolled[0:HALO])
    return jnp.concatenate([first, rolled[HALO:]], axis=0)


def _shift_earlier(cur, nxt, s):
    tt = cur.shape[0]
    rolled = pltpu.roll(cur, tt - s, 0)
    row = lax.broadcasted_iota(jnp.int32, nxt.shape, 0)
    last = jnp.where(row >= HALO - s, pltpu.roll(nxt, HALO - s, 0), rolled[tt - HALO:])
    return jnp.concatenate([rolled[:tt - HALO], last], axis=0)


def _conv_fwd(zx, col0, n_ch, conv_w, conv_b):
    t = zx.shape[0]
    tc = _tile(n_ch, 512)
    tt = _tile(t, 512, 8)
    cb0 = col0 // tc
    assert col0 % tc == 0
    kw = SSD_CONV_WIDTH

    def body(x_ref, p_ref, w_ref, b_ref, o_ref):
        cur = x_ref[...]
        prev = jnp.where(pl.program_id(1) > 0, p_ref[...], 0.0)
        w = w_ref[...]
        acc = b_ref[...] + w[kw - 1:kw, :] * cur
        for k in range(kw - 1):
            acc = acc + w[k:k + 1, :] * _shift_later(cur, prev, kw - 1 - k)
        o_ref[...] = acc

    return pl.pallas_call(
        body, name="ssd_conv_fwd", grid=(n_ch // tc, t // tt),
        in_specs=[pl.BlockSpec((tt, tc), lambda j, i: (i, cb0 + j)),
                  pl.BlockSpec((HALO, tc), lambda j, i: (jnp.maximum(i * (tt // HALO) - 1, 0), cb0 + j)),
                  pl.BlockSpec((kw, tc), lambda j, i: (0, j)),
                  pl.BlockSpec((1, tc), lambda j, i: (0, j))],
        out_specs=pl.BlockSpec((tt, tc), lambda j, i: (i, j)),
        out_shape=jax.ShapeDtypeStruct((t, n_ch), F32),
        compiler_params=_params("parallel", "parallel"))(zx, zx, conv_w, conv_b)


def _conv_bwd(name, dpre, zx, col0, conv_w):
    t, n_ch = dpre.shape
    tc = _tile(n_ch, 512)
    tt = _tile(t, 512, 8)
    cb0 = col0 // tc
    kw = SSD_CONV_WIDTH
    nt = t // tt

    def body(d_ref, dn_ref, x_ref, p_ref, w_ref, dx_ref, dw_ref, db_ref):
        i = pl.program_id(1)
        d = d_ref[...]
        d_next = jnp.where(i < nt - 1, dn_ref[...], 0.0)
        x = x_ref[...]
        x_prev = jnp.where(i > 0, p_ref[...], 0.0)
        w = w_ref[...]
        dx = w[kw - 1:kw, :] * d
        for k in range(kw - 1):
            dx = dx + w[k:k + 1, :] * _shift_earlier(d, d_next, kw - 1 - k)
        dx_ref[...] = dx.astype(BF16)
        first = i == 0
        for k in range(kw):
            xs = x if k == kw - 1 else _shift_later(x, x_prev, kw - 1 - k)
            val = jnp.sum(d * xs, axis=0, keepdims=True)

            @pl.when(first)
            def _():
                dw_ref[k:k + 1, :] = val

            @pl.when(jnp.logical_not(first))
            def _():
                dw_ref[k:k + 1, :] += val
        _acc(db_ref, jnp.sum(d, axis=0, keepdims=True), first)

    return pl.pallas_call(
        body, name=name, grid=(n_ch // tc, nt),
        in_specs=[pl.BlockSpec((tt, tc), lambda j, i: (i, j)),
                  pl.BlockSpec((HALO, tc), lambda j, i: (jnp.minimum((i + 1) * (tt // HALO), t // HALO - 1), j)),
                  pl.BlockSpec((tt, tc), lambda j, i: (i, cb0 + j)),
                  pl.BlockSpec((HALO, tc), lambda j, i: (jnp.maximum(i * (tt // HALO) - 1, 0), cb0 + j)),
                  pl.BlockSpec((kw, tc), lambda j, i: (0, j))],
        out_specs=[pl.BlockSpec((tt, tc), lambda j, i: (i, j)),
                   pl.BlockSpec((kw, tc), lambda j, i: (0, j)),
                   pl.BlockSpec((1, tc), lambda j, i: (0, j))],
        out_shape=[jax.ShapeDtypeStruct((t, n_ch), BF16), jax.ShapeDtypeStruct((kw, n_ch), F32),
                   jax.ShapeDtypeStruct((1, n_ch), F32)],
        compiler_params=_params("parallel", "arbitrary"))(dpre, dpre, zx, zx, conv_w)


def _head_of_lane(shape, width):
    return lax.broadcasted_iota(jnp.int32, shape, len(shape) - 1) // width


def _expand(v, n_rows):
    head = _head_of_lane((n_rows, GW), SSD_HEAD_DIM)
    out = jnp.zeros((n_rows, GW), F32)
    for j in range(SSD_HPG):
        out = jnp.where(head == j, v[:, j:j + 1], out)
    return out


def _contract(v, n_rows):
    head = _head_of_lane((n_rows, GW), SSD_HEAD_DIM)
    lane = lax.broadcasted_iota(jnp.int32, (n_rows, LANES), 1)
    out = jnp.zeros((n_rows, LANES), F32)
    for j in range(SSD_HPG):
        s = jnp.sum(jnp.where(head == j, v, 0.0), axis=1, keepdims=True)
        out = jnp.where(lane == j, s, out)
    return out


def _ssd_dt_prep(zdt, bias, alog, ng):
    t = zdt.shape[0]
    q = SSD_CHUNK

    def body(z_ref, b_ref, a_ref, dt_ref, cum_ref, cumr_ref, sg_ref):
        raw = z_ref[...] + b_ref[...]
        dt = _softplus(raw)
        sgd = _sigmoid(raw)
        row = lax.broadcasted_iota(jnp.int32, (q, q), 0)
        col = lax.broadcasted_iota(jnp.int32, (q, q), 1)
        cum = _dot_f32((col <= row).astype(F32), dt * (-jnp.exp(a_ref[...])))
        cum_t = cum.T
        lane = lax.broadcasted_iota(jnp.int32, (q, LANES), 1)
        for g in range(ng):
            shift = (LANES - g * SSD_HPG) % LANES

            def group(v):
                return jnp.where(lane < SSD_HPG, pltpu.roll(v, shift, 1) if shift else v, 0.0)

            dt_ref[g] = group(dt)
            cum_ref[g] = group(cum)
            sg_ref[g] = group(sgd)
            cumr_ref[g] = (pltpu.roll(cum_t, shift, 0) if shift else cum_t)[0:8, :]

    cols = pl.BlockSpec((ng, q, LANES), lambda c: (0, c, 0))
    vec = pl.BlockSpec((1, LANES), lambda c: (0, 0))
    col_shape = jax.ShapeDtypeStruct((ng, t, LANES), F32)
    return pl.pallas_call(body, name="ssd_dt_prep", grid=(t // q,),
                          in_specs=[pl.BlockSpec((q, LANES), lambda c: (c, 0)), vec, vec],
                          out_specs=[cols, cols, pl.BlockSpec((ng, 8, q), lambda c: (0, 0, c)), cols],
                          out_shape=[col_shape, col_shape, jax.ShapeDtypeStruct((ng, 8, t), F32), col_shape],
                          compiler_params=_params("parallel"))(zdt, bias, alog)


def _ssd_common(pre, dt, cum, cum_r, alog_c):
    q = SSD_CHUNK
    sg = _sigmoid(pre)
    act = pre * sg
    xa = act[:, :GW]
    bm = act[:, GW:GW + SSD_D_STATE].astype(BF16)
    cm = act[:, GW + SSD_D_STATE:].astype(BF16)
    row = lax.broadcasted_iota(jnp.int32, (q, q), 0)
    col = lax.broadcasted_iota(jnp.int32, (q, q), 1)
    tril = col <= row
    a_c = -jnp.exp(alog_c)
    g = _dot_nt(cm, bm)
    dt_x = _expand(dt, q)
    xdt = xa * dt_x
    cl = cum[q - 1:q, :]
    e_c = jnp.exp(cl - cum)
    lam_c = jnp.exp(cum)
    return dict(sg=sg, xa=xa, bm=bm, cm=cm, tril=tril, row=row, col=col, dt=dt, a_c=a_c, cum=cum, cum_r=cum_r,
                g=g, dt_x=dt_x, xdt=xdt, cl=cl, e_c=e_c, lam_c=lam_c)


SSD_GPS = 2


def _ssd_specs(nc, rev, ng):
    q = SSD_CHUNK
    xw, nw = SSD_GPS * GW, SSD_GPS * SSD_D_STATE
    b_off = ng * GW // nw
    c_off = (ng * GW + ng * SSD_D_STATE) // nw
    assert ng % SSD_GPS == 0 and (ng * GW) % nw == 0 and (ng * SSD_D_STATE) % nw == 0

    def ch(c):
        return nc - 1 - c if rev else c

    chunk_grp = [pl.BlockSpec((q, xw), lambda g, c: (ch(c), g)),
                 pl.BlockSpec((q, nw), lambda g, c: (ch(c), b_off + g)),
                 pl.BlockSpec((q, nw), lambda g, c: (ch(c), c_off + g))]
    col_form = pl.BlockSpec((SSD_GPS, q, LANES), lambda g, c: (g, ch(c), 0))
    row_form = pl.BlockSpec((SSD_GPS, 8, q), lambda g, c: (g, 0, ch(c)))
    col_par = pl.BlockSpec((SSD_GPS, 1, LANES), lambda g, c: (g, 0, 0))
    y_spec = pl.BlockSpec((q, xw), lambda g, c: (ch(c), g))
    st_spec = pl.BlockSpec((SSD_GPS, None, GW, SSD_D_STATE), lambda g, c: (g, ch(c), 0, 0))
    bc_spec = pl.BlockSpec((q, nw), lambda g, c: (ch(c), g))
    return chunk_grp, col_form, row_form, col_par, y_spec, st_spec, bc_spec


def _ssd_group_views(gi, wide, narrow, stacked):
    xs, ns = pl.ds(gi * GW, GW), pl.ds(gi * SSD_D_STATE, SSD_D_STATE)
    return [r.at[:, xs] for r in wide], [r.at[:, ns] for r in narrow], [r.at[gi] for r in stacked]


def _ssd_fwd(pre, dt_c, cum_c, cum_r, alog_c, dsk_c):
    t = pre.shape[0]
    ng = pre.shape[1] // GC
    q = SSD_CHUNK
    nc = t // q
    chunk_grp, col_form, row_form, col_par, y_spec, st_spec, _ = _ssd_specs(nc, False, ng)

    def body(px_ref, pb_ref, pc_ref, dt_ref, cum_ref, cumr_ref, ac_ref, dk_ref, y_ref, sp_ref, st_ref):
        @pl.when(pl.program_id(1) == 0)
        def _():
            st_ref[...] = jnp.zeros_like(st_ref)

        for gi in range(SSD_GPS):
            (px, y), (pb, pc), rest = _ssd_group_views(
                gi, (px_ref, y_ref), (pb_ref, pc_ref), (dt_ref, cum_ref, cumr_ref, ac_ref, dk_ref, sp_ref, st_ref))
            one_group(px, pb, pc, *rest[:5], y, *rest[5:])

    def one_group(px_ref, pb_ref, pc_ref, dt_ref, cum_ref, cumr_ref, ac_ref, dk_ref, y_ref, sp_ref, st_ref):
        pre_v = jnp.concatenate([px_ref[...], pb_ref[...], pc_ref[...]], axis=1)
        v = _ssd_common(pre_v, dt_ref[...], cum_ref[...], cumr_ref[...], ac_ref[...])
        s0 = st_ref[...]
        sp_ref[...] = s0
        r = _dot_nt(v["cm"], s0.astype(BF16))
        y = _expand(v["lam_c"], q) * r + _expand(dk_ref[...], 1) * v["xa"]
        head = _head_of_lane((q, GW), SSD_HEAD_DIM)
        for j in range(SSD_HPG):
            diff = v["cum"][:, j:j + 1] - v["cum_r"][j:j + 1, :]
            w = (v["g"] * jnp.exp(jnp.where(v["tril"], diff, -jnp.inf))).astype(BF16)
            y = y + _dot(w, jnp.where(head == j, v["xdt"], 0.0).astype(BF16))
        y_ref[...] = y
        ds = _dot_tn((v["xdt"] * _expand(v["e_c"], q)).astype(BF16), v["bm"])
        for j in range(SSD_HPG):
            rows = slice(j * SSD_HEAD_DIM, (j + 1) * SSD_HEAD_DIM)
            st_ref[rows, :] = s0[rows, :] * jnp.exp(v["cum_r"][j:j + 1, q - 1:q]) + ds[rows, :]

    return pl.pallas_call(
        body, name="ssd_scan_fwd", grid=(ng // SSD_GPS, nc),
        in_specs=chunk_grp + [col_form, col_form, row_form, col_par, col_par],
        out_specs=[y_spec, st_spec],
        out_shape=[jax.ShapeDtypeStruct((t, ng * GW), F32), jax.ShapeDtypeStruct((ng, nc, GW, SSD_D_STATE), F32)],
        scratch_shapes=[pltpu.VMEM((SSD_GPS, GW, SSD_D_STATE), F32)],
        compiler_params=_params("parallel", "arbitrary"))(pre, pre, pre, dt_c, cum_c, cum_r, alog_c, dsk_c)


def _ssd_bwd(dy, pre, states, dt_c, cum_c, cum_r, sgd_c, alog_c, dsk_c):
    t = pre.shape[0]
    ng = pre.shape[1] // GC
    q = SSD_CHUNK
    nc = t // q
    chunk_grp, col_form, row_form, col_par, y_spec, st_spec, bc_spec = _ssd_specs(nc, True, ng)

    def body(dy_ref, px_ref, pb_ref, pc_ref, sp_ref, dt_ref, cum_ref, cumr_ref, sgd_ref, ac_ref, dk_ref,
             dpx_ref, dpb_ref, dpc_ref, ddt_ref, dbias_ref, dalog_ref, dd_ref, ds_ref):
        @pl.when(pl.program_id(1) == 0)
        def _():
            ds_ref[...] = jnp.zeros_like(ds_ref)

        for gi in range(SSD_GPS):
            (dy, px, dpx), (pb, pc, dpb, dpc), rest = _ssd_group_views(
                gi, (dy_ref, px_ref, dpx_ref), (pb_ref, pc_ref, dpb_ref, dpc_ref),
                (sp_ref, dt_ref, cum_ref, cumr_ref, sgd_ref, ac_ref, dk_ref, ddt_ref, dbias_ref, dalog_ref, dd_ref,
                 ds_ref))
            one_group(dy, px, pb, pc, *rest[:7], dpx, dpb, dpc, *rest[7:])

    def one_group(dy_ref, px_ref, pb_ref, pc_ref, sp_ref, dt_ref, cum_ref, cumr_ref, sgd_ref, ac_ref, dk_ref,
                  dpx_ref, dpb_ref, dpc_ref, ddt_ref, dbias_ref, dalog_ref, dd_ref, ds_ref):
        first = pl.program_id(1) == 0
        pre_v = jnp.concatenate([px_ref[...], pb_ref[...], pc_ref[...]], axis=1)
        v = _ssd_common(pre_v, dt_ref[...], cum_ref[...], cumr_ref[...], ac_ref[...])
        xa, bm, cm, xdt, cum, cum_r = v["xa"], v["bm"], v["cm"], v["xdt"], v["cum"], v["cum_r"]
        xdt_b = xdt.astype(BF16)
        dy_v = dy_ref[...]
        s0 = sp_ref[...]
        ds1 = ds_ref[...]
        s0b, ds1b = s0.astype(BF16), ds1.astype(BF16)
        head = _head_of_lane((q, GW), SSD_HEAD_DIM)
        lane = lax.broadcasted_iota(jnp.int32, (q, LANES), 1)
        lane1 = lax.broadcasted_iota(jnp.int32, (1, LANES), 1)
        lam_x = _expand(v["lam_c"], q)
        e_x = _expand(v["e_c"], q)

        dxa = _expand(dk_ref[...], 1) * dy_v
        dd = _contract(jnp.sum(dy_v * xa, axis=0, keepdims=True), 1)
        r = _dot_nt(cm, s0b)
        dcum = _contract(dy_v * r * lam_x, q)
        drb = (lam_x * dy_v).astype(BF16)
        dc = _dot(drb, s0b)
        ds0 = _dot_tn(drb, cm)
        extra = jnp.zeros((1, LANES), F32)
        for j in range(SSD_HPG):
            rows = slice(j * SSD_HEAD_DIM, (j + 1) * SSD_HEAD_DIM)
            lam_last = jnp.exp(cum_r[j:j + 1, q - 1:q])
            ds_ref[rows, :] = ds0[rows, :] + lam_last * ds1[rows, :]
            tot = jnp.sum(jnp.sum(ds1[rows, :] * s0[rows, :], axis=1, keepdims=True), axis=0, keepdims=True)
            extra = jnp.where(lane1 == j, lam_last * tot, extra)
        dv = _dot_nt(bm, ds1b)
        db = _dot((xdt * e_x).astype(BF16), ds1b)
        dxdt = e_x * dv
        dee = _contract(dv * xdt, q) * v["e_c"]
        dcum = dcum - dee
        extra = extra + jnp.sum(dee, axis=0, keepdims=True)
        dg = jnp.zeros((q, q), F32)
        for j in range(SSD_HPG):
            diff = cum[:, j:j + 1] - cum_r[j:j + 1, :]
            el = jnp.exp(jnp.where(v["tril"], diff, -jnp.inf))
            gl = v["g"] * el
            dym = jnp.where(head == j, dy_v, 0.0).astype(BF16)
            dwm = _dot_nt(dym, xdt_b)
            dxdt = dxdt + _dot_tn(gl.astype(BF16), dym)
            z = dwm * gl
            rk = jnp.sum(z, axis=1, keepdims=True) - jnp.sum(z.T, axis=1, keepdims=True)
            dcum = jnp.where(lane == j, dcum + rk, dcum)
            dg = dg + dwm * el
        dgb = dg.astype(BF16)
        dc = dc + _dot(dgb, bm)
        db = db + _dot_tn(dgb, cm)
        da = _dot_f32((v["row"] <= v["col"]).astype(F32), dcum) + extra
        ddt = _contract(dxdt * xa, q) + v["a_c"] * da
        dalog = jnp.sum(v["dt"] * da, axis=0, keepdims=True) * v["a_c"]
        dxa = dxa + v["dt_x"] * dxdt
        ddt_raw = jnp.where(lane < SSD_HPG, ddt * sgd_ref[...], 0.0)
        sgrad = _silu_grad(pre_v, v["sg"])
        dpx_ref[...] = dxa * sgrad[:, :GW]
        dpb_ref[...] = db * sgrad[:, GW:GW + SSD_D_STATE]
        dpc_ref[...] = dc * sgrad[:, GW + SSD_D_STATE:]
        ddt_ref[...] = ddt_raw
        _acc(dbias_ref, jnp.sum(ddt_raw, axis=0, keepdims=True), first)
        _acc(dalog_ref, jnp.where(lane1 < SSD_HPG, dalog, 0.0), first)
        _acc(dd_ref, dd, first)

    return pl.pallas_call(
        body, name="ssd_scan_bwd", grid=(ng // SSD_GPS, nc),
        in_specs=[y_spec] + chunk_grp + [st_spec, col_form, col_form, row_form, col_form, col_par, col_par],
        out_specs=[y_spec, bc_spec, bc_spec, col_form, col_par, col_par, col_par],
        out_shape=[jax.ShapeDtypeStruct((t, ng * GW), F32), jax.ShapeDtypeStruct((t, ng * SSD_D_STATE), F32),
                   jax.ShapeDtypeStruct((t, ng * SSD_D_STATE), F32), jax.ShapeDtypeStruct((ng, t, LANES), F32),
                   jax.ShapeDtypeStruct((ng, 1, LANES), F32), jax.ShapeDtypeStruct((ng, 1, LANES), F32),
                   jax.ShapeDtypeStruct((ng, 1, LANES), F32)],
        scratch_shapes=[pltpu.VMEM((SSD_GPS, GW, SSD_D_STATE), F32)],
        compiler_params=_params("parallel", "arbitrary"))(dy, pre, pre, pre, states, dt_c, cum_c, cum_r, sgd_c, alog_c,
                                                           dsk_c)


def _gate_norm_fwd(y, zx, norm_w):
    t, di = y.shape
    tr = _tile(t, 256, 8)
    ng = di // GW

    def body(y_ref, z_ref, w_ref, o_ref):
        z = z_ref[...]
        gate = y_ref[...] * (z * _sigmoid(z))
        w = w_ref[...]
        for g in range(ng):
            cols = slice(g * GW, (g + 1) * GW)
            gs = gate[:, cols]
            r = lax.rsqrt(jnp.mean(gs * gs, axis=-1, keepdims=True) + NORM_EPS)
            o_ref[:, cols] = (gs * r * w[:, cols]).astype(BF16)

    row = pl.BlockSpec((tr, di), lambda i: (i, 0))
    return pl.pallas_call(body, name="ssd_gate_norm_fwd", grid=(t // tr,),
                          in_specs=[row, row, pl.BlockSpec((1, di), lambda i: (0, 0))], out_specs=row,
                          out_shape=jax.ShapeDtypeStruct((t, di), BF16), compiler_params=_params("parallel"))(
                              y, zx, norm_w)


def _gate_norm_bwd(dyn, y, zx, norm_w, after):
    t, di = y.shape
    tr = _tile(t, 256, 8)
    ng = di // GW

    def body(d_ref, y_ref, z_ref, w_ref, after_ref, dy_ref, dz_ref, dw_ref):
        z = z_ref[...]
        yv = y_ref[...]
        sg = _sigmoid(z)
        sz = z * sg
        gate = yv * sz
        w = w_ref[...]
        d = d_ref[...]
        dsz = _silu_grad(z, sg)
        dws = []
        for g in range(ng):
            cols = slice(g * GW, (g + 1) * GW)
            dg, dwr = _rms_bwd(gate[:, cols], w[:, cols], d[:, cols])
            dy_ref[:, cols] = dg * sz[:, cols]
            dz_ref[:, cols] = (dg * yv[:, cols] * dsz[:, cols]).astype(BF16)
            dws.append(jnp.sum(dwr, axis=0, keepdims=True))
        first = pl.program_id(0) == 0
        for g in range(ng):
            cols = slice(g * GW, (g + 1) * GW)

            @pl.when(first)
            def _():
                dw_ref[:, cols] = dws[g]

            @pl.when(jnp.logical_not(first))
            def _():
                dw_ref[:, cols] += dws[g]

    row = pl.BlockSpec((tr, di), lambda i: (i, 0))
    vec = pl.BlockSpec((1, di), lambda i: (0, 0))
    return pl.pallas_call(body, name="ssd_gate_norm_bwd", grid=(t // tr,),
                          in_specs=[row, row, row, vec, pl.BlockSpec((8, LANES), lambda i: (0, 0))],
                          out_specs=[row, row, vec],
                          out_shape=[jax.ShapeDtypeStruct((t, di), F32), jax.ShapeDtypeStruct((t, di), BF16),
                                     jax.ShapeDtypeStruct((1, di), F32)],
                          compiler_params=_params("arbitrary"))(dyn, y, zx, norm_w, after)


def _attn_mask(n):
    w = ATTN_WINDOW
    qpos = lax.broadcasted_iota(jnp.int32, (w, 2 * w), 0) + w
    kpos = lax.broadcasted_iota(jnp.int32, (w, 2 * w), 1)
    rel = qpos - kpos
    return (rel >= 0) & (rel < w) & jnp.logical_not((n == 0) & (kpos < w))


def _attn_probs(qh, kbh, mask, sink):
    s = _dot_nt(qh, kbh) * (ATTN_HEAD_DIM ** -0.5)
    s = jnp.where(mask, s, -jnp.inf)
    m = jnp.maximum(jnp.max(s, axis=-1, keepdims=True), sink)
    e = jnp.exp(s - m)
    es = jnp.exp(sink - m)
    inv = 1.0 / (jnp.sum(e, axis=-1, keepdims=True) + es)
    return e * inv, es * inv


def _attn_fwd(qkv, sinks):
    t = qkv.shape[0]
    w, hd = ATTN_WINDOW, ATTN_HEAD_DIM
    kd = ATTN_N_KV * hd
    qd = ATTN_REP * kd
    nb = t // w

    def body(q_ref, kc_ref, vc_ref, kp_ref, vp_ref, s_ref, o_ref):
        n = pl.program_id(0)
        mask = _attn_mask(n)
        q = q_ref[...]
        kb = jnp.concatenate([kp_ref[...], kc_ref[...]], axis=0)
        vb = jnp.concatenate([vp_ref[...], vc_ref[...]], axis=0)
        sk = s_ref[...]
        for kv in range(ATTN_N_KV):
            kbh = kb[:, kv * hd:(kv + 1) * hd]
            vbh = vb[:, kv * hd:(kv + 1) * hd]
            for rep in range(ATTN_REP):
                h = kv * ATTN_REP + rep
                p, _ = _attn_probs(q[:, h * hd:(h + 1) * hd], kbh, mask, sk[:, h:h + 1])
                o_ref[:, h * hd:(h + 1) * hd] = _dot(p.astype(BF16), vbh).astype(BF16)

    prev = lambda n: jnp.maximum(n - 1, 0)
    return pl.pallas_call(
        body, name="attn_fwd", grid=(nb,),
        in_specs=[pl.BlockSpec((w, qd), lambda n: (n, 0)),
                  pl.BlockSpec((w, kd), lambda n: (n, ATTN_REP)),
                  pl.BlockSpec((w, kd), lambda n: (n, ATTN_REP + 1)),
                  pl.BlockSpec((w, kd), lambda n: (prev(n), ATTN_REP)),
                  pl.BlockSpec((w, kd), lambda n: (prev(n), ATTN_REP + 1)),
                  pl.BlockSpec((1, sinks.shape[1]), lambda n: (0, 0))],
        out_specs=pl.BlockSpec((w, qd), lambda n: (n, 0)),
        out_shape=jax.ShapeDtypeStruct((t, qd), BF16),
        compiler_params=_params("parallel"))(qkv, qkv, qkv, qkv, qkv, sinks)


def _attn_bwd(qkv, do, sinks):
    t = qkv.shape[0]
    w, hd = ATTN_WINDOW, ATTN_HEAD_DIM
    kd = ATTN_N_KV * hd
    qd = ATTN_REP * kd
    nq = ATTN_N_KV * ATTN_REP
    nb = t // w

    def body(q_ref, kc_ref, vc_ref, kp_ref, vp_ref, do_ref, s_ref,
             dq_ref, dk_ref, dv_ref, bq_ref, bk_ref, bv_ref, dsk_ref, ck_ref, cv_ref):
        n = pl.program_id(0)
        first = n == 0

        @pl.when(first)
        def _():
            ck_ref[...] = jnp.zeros_like(ck_ref)
            cv_ref[...] = jnp.zeros_like(cv_ref)
            bq_ref[...] = jnp.zeros_like(bq_ref)
            bk_ref[...] = jnp.zeros_like(bk_ref)
            bv_ref[...] = jnp.zeros_like(bv_ref)
            dsk_ref[...] = jnp.zeros_like(dsk_ref)

        @pl.when(n < nb)
        def _():
            mask = _attn_mask(n)
            q = q_ref[...]
            dov = do_ref[...]
            kb = jnp.concatenate([kp_ref[...], kc_ref[...]], axis=0)
            vb = jnp.concatenate([vp_ref[...], vc_ref[...]], axis=0)
            sk = s_ref[...]
            lane = lax.broadcasted_iota(jnp.int32, (1, nq), 1)
            dsk = jnp.zeros((1, nq), F32)
            dq_parts, dk_parts, dv_parts = [], [], []
            for kv in range(ATTN_N_KV):
                kbh = kb[:, kv * hd:(kv + 1) * hd]
                vbh = vb[:, kv * hd:(kv + 1) * hd]
                dkh = jnp.zeros((2 * w, hd), F32)
                dvh = jnp.zeros((2 * w, hd), F32)
                for rep in range(ATTN_REP):
                    h = kv * ATTN_REP + rep
                    qh = q[:, h * hd:(h + 1) * hd]
                    doh = dov[:, h * hd:(h + 1) * hd]
                    p, ps = _attn_probs(qh, kbh, mask, sk[:, h:h + 1])
                    pb = p.astype(BF16)
                    dp = _dot_nt(doh, vbh)
                    delta = jnp.sum(p * dp, axis=-1, keepdims=True)
                    dsc = (p * (dp - delta) * (hd ** -0.5)).astype(BF16)
                    dq_parts.append(_dot(dsc, kbh))
                    dkh = dkh + _dot_tn(dsc, qh)
                    dvh = dvh + _dot_tn(pb, doh)
                    dsk = jnp.where(lane == h, -jnp.sum(ps * delta, axis=0, keepdims=True), dsk)
                dk_parts.append(dkh)
                dv_parts.append(dvh)
            dq = jnp.concatenate(dq_parts, axis=1)
            dkb = jnp.concatenate(dk_parts, axis=1)
            dvb = jnp.concatenate(dv_parts, axis=1)
            dq_ref[...] = dq.astype(BF16)
            bq_ref[...] += jnp.sum(dq, axis=0, keepdims=True)
            dsk_ref[...] += dsk
            dk_prev = ck_ref[...] + dkb[:w, :]
            dv_prev = cv_ref[...] + dvb[:w, :]
            dk_ref[...] = dk_prev.astype(BF16)
            dv_ref[...] = dv_prev.astype(BF16)

            @pl.when(n > 0)
            def _():
                bk_ref[...] += jnp.sum(dk_prev, axis=0, keepdims=True)
                bv_ref[...] += jnp.sum(dv_prev, axis=0, keepdims=True)

            ck_ref[...] = dkb[w:, :]
            cv_ref[...] = dvb[w:, :]

        @pl.when(n == nb)
        def _():
            dk_ref[...] = ck_ref[...].astype(BF16)
            dv_ref[...] = cv_ref[...].astype(BF16)
            bk_ref[...] += jnp.sum(ck_ref[...], axis=0, keepdims=True)
            bv_ref[...] += jnp.sum(cv_ref[...], axis=0, keepdims=True)

    cur = lambda n: jnp.minimum(n, nb - 1)
    prev = lambda n: jnp.maximum(jnp.minimum(n, nb - 1) - 1, 0)
    late = lambda n: jnp.maximum(n - 1, 0)
    vec = lambda width: pl.BlockSpec((1, width), lambda n: (0, 0))
    return pl.pallas_call(
        body, name="attn_bwd", grid=(nb + 1,),
        in_specs=[pl.BlockSpec((w, qd), lambda n: (cur(n), 0)),
                  pl.BlockSpec((w, kd), lambda n: (cur(n), ATTN_REP)),
                  pl.BlockSpec((w, kd), lambda n: (cur(n), ATTN_REP + 1)),
                  pl.BlockSpec((w, kd), lambda n: (prev(n), ATTN_REP)),
                  pl.BlockSpec((w, kd), lambda n: (prev(n), ATTN_REP + 1)),
                  pl.BlockSpec((w, qd), lambda n: (cur(n), 0)),
                  vec(nq)],
        out_specs=[pl.BlockSpec((w, qd), lambda n: (cur(n), 0)),
                   pl.BlockSpec((w, kd), lambda n: (late(n), 0)),
                   pl.BlockSpec((w, kd), lambda n: (late(n), 0)),
                   vec(qd), vec(kd), vec(kd), vec(nq)],
        out_shape=[jax.ShapeDtypeStruct((t, qd), BF16), jax.ShapeDtypeStruct((t, kd), BF16),
                   jax.ShapeDtypeStruct((t, kd), BF16), jax.ShapeDtypeStruct((1, qd), F32),
                   jax.ShapeDtypeStruct((1, kd), F32), jax.ShapeDtypeStruct((1, kd), F32),
                   jax.ShapeDtypeStruct((1, nq), F32)],
        scratch_shapes=[pltpu.VMEM((w, kd), F32), pltpu.VMEM((w, kd), F32)],
        compiler_params=_params("arbitrary"))(qkv, qkv, qkv, qkv, qkv, do, sinks)


def _attn_mask_t(n):
    w = ATTN_WINDOW
    kpos = lax.broadcasted_iota(jnp.int32, (2 * w, ATTN_REP * w), 0)
    qpos = lax.broadcasted_iota(jnp.int32, (2 * w, ATTN_REP * w), 1) % w + w
    rel = qpos - kpos
    return (rel >= 0) & (rel < w) & jnp.logical_not((n == 0) & (kpos < w))


def _attn_probs_t(qts, ktb, mask, sink):
    s = _dot_tn(ktb, qts) * (ATTN_HEAD_DIM ** -0.5)
    s = jnp.where(mask, s, -jnp.inf)
    m = jnp.maximum(jnp.max(s, axis=0, keepdims=True), sink)
    e = jnp.exp(s - m)
    es = jnp.exp(sink - m)
    inv = 1.0 / (jnp.sum(e, axis=0, keepdims=True) + es)
    return e * inv, es * inv


def _attn_blocks_t(kv, q_ref, kc_ref, vc_ref, kp_ref, vp_ref):
    hd = ATTN_HEAD_DIM
    rows = slice(kv * hd, (kv + 1) * hd)
    ktb = jnp.concatenate([kp_ref[rows, :], kc_ref[rows, :]], axis=1)
    vtb = jnp.concatenate([vp_ref[rows, :], vc_ref[rows, :]], axis=1)
    qts = jnp.concatenate([q_ref[(kv * ATTN_REP + r) * hd:(kv * ATTN_REP + r + 1) * hd, :]
                           for r in range(ATTN_REP)], axis=1)
    return qts, ktb, vtb


def _attn_specs_t(nb, cur, prev):
    w, hd = ATTN_WINDOW, ATTN_HEAD_DIM
    kd = ATTN_N_KV * hd
    qd = ATTN_REP * kd
    return [pl.BlockSpec((qd, w), lambda n: (0, cur(n))),
            pl.BlockSpec((kd, w), lambda n: (ATTN_REP, cur(n))),
            pl.BlockSpec((kd, w), lambda n: (ATTN_REP + 1, cur(n))),
            pl.BlockSpec((kd, w), lambda n: (ATTN_REP, prev(n))),
            pl.BlockSpec((kd, w), lambda n: (ATTN_REP + 1, prev(n)))]


def _attn_fwd_t(qkv_t, sinks_rep):
    t = qkv_t.shape[1]
    w, hd = ATTN_WINDOW, ATTN_HEAD_DIM
    qd = ATTN_N_KV * ATTN_REP * hd
    nb = t // w

    def body(q_ref, kc_ref, vc_ref, kp_ref, vp_ref, s_ref, o_ref):
        mask = _attn_mask_t(pl.program_id(0))
        for kv in range(ATTN_N_KV):
            qts, ktb, vtb = _attn_blocks_t(kv, q_ref, kc_ref, vc_ref, kp_ref, vp_ref)
            p, _ = _attn_probs_t(qts, ktb, mask, s_ref[kv])
            ots = _dot(vtb, p.astype(BF16))
            for r in range(ATTN_REP):
                h = kv * ATTN_REP + r
                o_ref[h * hd:(h + 1) * hd, :] = ots[:, r * w:(r + 1) * w].astype(BF16)

    return pl.pallas_call(
        body, name="attn_fwd", grid=(nb,),
        in_specs=_attn_specs_t(nb, lambda n: n, lambda n: jnp.maximum(n - 1, 0)) + [
            pl.BlockSpec(sinks_rep.shape, lambda n: (0, 0, 0))],
        out_specs=pl.BlockSpec((qd, w), lambda n: (0, n)),
        out_shape=jax.ShapeDtypeStruct((qd, t), BF16),
        compiler_params=_params("parallel"))(qkv_t, qkv_t, qkv_t, qkv_t, qkv_t, sinks_rep)


def _attn_bwd_t(qkv_t, do_t, sinks_rep):
    t = qkv_t.shape[1]
    w, hd = ATTN_WINDOW, ATTN_HEAD_DIM
    kd = ATTN_N_KV * hd
    qd = ATTN_REP * kd
    nq = ATTN_N_KV * ATTN_REP
    nb = t // w
    rows_all = qd + 2 * kd

    def body(q_ref, kc_ref, vc_ref, kp_ref, vp_ref, do_ref, s_ref, dqkv_ref, bsum_ref, dsk_ref,
             carry_ref, new_ref, bacc_ref, sacc_ref):
        n = pl.program_id(0)

        @pl.when(n == 0)
        def _():
            carry_ref[...] = jnp.zeros_like(carry_ref)
            bacc_ref[...] = jnp.zeros_like(bacc_ref)
            sacc_ref[...] = jnp.zeros_like(sacc_ref)

        @pl.when(n < nb)
        def _():
            mask = _attn_mask_t(n)
            for kv in range(ATTN_N_KV):
                qts, ktb, vtb = _attn_blocks_t(kv, q_ref, kc_ref, vc_ref, kp_ref, vp_ref)
                dots = jnp.concatenate([do_ref[(kv * ATTN_REP + r) * hd:(kv * ATTN_REP + r + 1) * hd, :]
                                        for r in range(ATTN_REP)], axis=1)
                p, ps = _attn_probs_t(qts, ktb, mask, s_ref[kv])
                dpt = _dot_tn(vtb, dots)
                delta = jnp.sum(p * dpt, axis=0, keepdims=True)
                dst = (p * (dpt - delta) * (hd ** -0.5)).astype(BF16)
                dqts = _dot(ktb, dst)
                for r in range(ATTN_REP):
                    h = kv * ATTN_REP + r
                    new_ref[h * hd:(h + 1) * hd, :] = dqts[:, r * w:(r + 1) * w]
                dktb = _dot_nt(qts, dst)
                dvtb = _dot_nt(dots, p.astype(BF16))
                krows = slice(qd + kv * hd, qd + (kv + 1) * hd)
                vrows = slice(qd + kd + kv * hd, qd + kd + (kv + 1) * hd)
                carry_ref[krows, :] += dktb[:, :w]
                carry_ref[vrows, :] += dvtb[:, :w]
                new_ref[krows, :] = dktb[:, w:]
                new_ref[vrows, :] = dvtb[:, w:]
                sacc_ref[kv] += -(ps * delta)

        @pl.when(n >= 1)
        def _():
            done = carry_ref[...]
            dqkv_ref[...] = done.astype(BF16)
            bacc_ref[...] += done

        @pl.when(n < nb)
        def _():
            carry_ref[...] = new_ref[...]

        @pl.when(n == nb)
        def _():
            bsum_ref[...] = jnp.sum(bacc_ref[...], axis=1, keepdims=True)
            lane = lax.broadcasted_iota(jnp.int32, (1, nq), 1)
            dsk = jnp.zeros((1, nq), F32)
            for kv in range(ATTN_N_KV):
                acc = sacc_ref[kv]
                for r in range(ATTN_REP):
                    tot = jnp.sum(acc[:, r * w:(r + 1) * w], axis=1, keepdims=True)
                    dsk = jnp.where(lane == kv * ATTN_REP + r, tot, dsk)
            dsk_ref[...] = dsk

    cur = lambda n: jnp.minimum(n, nb - 1)
    prev = lambda n: jnp.maximum(jnp.minimum(n, nb - 1) - 1, 0)
    return pl.pallas_call(
        body, name="attn_bwd", grid=(nb + 1,),
        in_specs=_attn_specs_t(nb, cur, prev) + [pl.BlockSpec((qd, w), lambda n: (0, cur(n))),
                                                 pl.BlockSpec(sinks_rep.shape, lambda n: (0, 0, 0))],
        out_specs=[pl.BlockSpec((rows_all, w), lambda n: (0, jnp.maximum(n - 1, 0))),
                   pl.BlockSpec((rows_all, 1), lambda n: (0, 0)),
                   pl.BlockSpec((1, nq), lambda n: (0, 0))],
        out_shape=[jax.ShapeDtypeStruct((rows_all, t), BF16), jax.ShapeDtypeStruct((rows_all, 1), F32),
                   jax.ShapeDtypeStruct((1, nq), F32)],
        scratch_shapes=[pltpu.VMEM((rows_all, w), F32), pltpu.VMEM((rows_all, w), F32),
                        pltpu.VMEM((rows_all, w), F32), pltpu.VMEM(sinks_rep.shape, F32)],
        compiler_params=_params("arbitrary"))(qkv_t, qkv_t, qkv_t, qkv_t, qkv_t, do_t, sinks_rep)


HBM_SPEC = pl.BlockSpec(memory_space=pl.ANY)
HBM_ONLY = pl.BlockSpec(memory_space=pltpu.HBM)


def _comm_call(name, body, ins, out_shapes, n_sems):
    return pl.pallas_call(
        body, name=name, in_specs=[HBM_SPEC] * len(ins), out_specs=[HBM_SPEC] * len(out_shapes),
        out_shape=out_shapes,
        scratch_shapes=[pltpu.SemaphoreType.DMA((s,)) for s in n_sems])(*ins)


def _all_gather(name, shards, after):
    n = len(shards)

    def body(*refs):
        x_refs, out_refs = refs[:n], refs[n + 1:2 * n + 1]
        send_sems, recv_sems, local_sems = refs[2 * n + 1:]
        x, y, c = lax.axis_index("x"), lax.axis_index("y"), lax.axis_index("c")
        me, sibling = (x, y, c), (x, y, 1 - c)
        chips = [(1 - x, y), (x, 1 - y), (1 - x, 1 - y)]

        def slot(i, px, py, pc):
            return out_refs[i].at[4 * px + 2 * py + pc]

        def copy(k, i, block, to, src=None):
            return pltpu.make_async_remote_copy(
                src_ref=slot(i, *block) if src is None else src, dst_ref=slot(i, *block),
                send_sem=send_sems.at[k * n + i], recv_sem=recv_sems.at[k * n + i], device_id=to,
                device_id_type=MESH)

        mine = [pltpu.make_async_copy(x_refs[i], slot(i, *me), local_sems.at[i]) for i in range(n)]
        first = []
        for i in range(n):
            mine[i].start()
            first.append(copy(0, i, me, sibling, src=x_refs[i]))
            first += [copy(1 + j, i, me, (*chip, c), src=x_refs[i]) for j, chip in enumerate(chips)]
        for cp in first:
            cp.start()
        passed = []
        for i in range(n):
            for j, chip in enumerate(chips):
                copy(1 + j, i, (*chip, c), me).wait_recv()
                passed.append(copy(4 + j, i, (*chip, c), sibling))
                passed[-1].start()
        for i in range(n):
            copy(0, i, sibling, me).wait_recv()
            for j, chip in enumerate(chips):
                copy(4 + j, i, (*chip, 1 - c), me).wait_recv()
        for cp in first + passed:
            cp.wait_send()
        for cp in mine:
            cp.wait()

    outs = [jax.ShapeDtypeStruct((N_DEV,) + s.shape, s.dtype) for s in shards]
    return _comm_call(name, body, list(shards) + [after], outs, (7 * n, 7 * n, n))


SEM_SPEC = pl.BlockSpec(memory_space=pltpu.SEMAPHORE)
SPLIT_COPY_EFFECT = pltpu.SideEffectType.DATAFLOW_SIDE_EFFECTING


def _in_hbm(a):
    return pltpu.with_memory_space_constraint(a, pltpu.HBM)


def _split_start(name, body, srcs, lands, n_sems):
    n = len(srcs)
    bufs = [_in_hbm(a) for a in list(srcs) + list(lands)]
    outs = pl.pallas_call(
        body, name=name,
        out_shape=(pltpu.SemaphoreType.DMA((n_sems,)), pltpu.SemaphoreType.DMA((n_sems,)),
                   *[pltpu.HBM(a.shape, a.dtype) for a in bufs], jax.ShapeDtypeStruct((8, LANES), F32)),
        in_specs=[HBM_ONLY] * (2 * n),
        out_specs=(SEM_SPEC, SEM_SPEC, *[HBM_ONLY] * (2 * n), pl.BlockSpec(memory_space=pltpu.VMEM)),
        input_output_aliases={i: 2 + i for i in range(2 * n)},
        compiler_params=pltpu.CompilerParams(has_side_effects=SPLIT_COPY_EFFECT))(*bufs)
    return outs[0], outs[1], list(outs[2:2 + n]), list(outs[2 + n:2 + 2 * n]), outs[-1]


def _split_wait(name, body, send_sems, recv_sems, srcs, lands, after):
    n = len(srcs)
    outs = pl.pallas_call(
        body, name=name,
        out_shape=[pltpu.HBM(a.shape, a.dtype) for a in list(srcs) + list(lands)],
        in_specs=[HBM_ONLY] * (2 * n) + [SEM_SPEC, SEM_SPEC, HBM_SPEC],
        out_specs=[HBM_ONLY] * (2 * n),
        input_output_aliases={i: i for i in range(2 * n)},
        compiler_params=pltpu.CompilerParams(has_side_effects=SPLIT_COPY_EFFECT))(
            *srcs, *lands, send_sems, recv_sems, after)
    return list(outs[:n]), list(outs[n:])


N_PEERS = N_DEV - 1


def _gather_peers():
    x, y, c = lax.axis_index("x"), lax.axis_index("y"), lax.axis_index("c")
    flips = [(fx, fy, fc) for fx in (0, 1) for fy in (0, 1) for fc in (0, 1) if fx or fy or fc]
    return [(1 - x if fx else x, 1 - y if fy else y, 1 - c if fc else c) for fx, fy, fc in flips]


def _block_id(dev):
    return 4 * dev[0] + 2 * dev[1] + dev[2]


def _gather_start(name, shards):
    n = len(shards)

    def body(*refs):
        x_refs, land_refs = refs[:n], refs[n:2 * n]
        send_sems, recv_sems, token = refs[2 * n], refs[2 * n + 1], refs[-1]
        me = (lax.axis_index("x"), lax.axis_index("y"), lax.axis_index("c"))
        for i in range(n):
            for k, peer in enumerate(_gather_peers()):
                pltpu.make_async_remote_copy(
                    src_ref=x_refs[i], dst_ref=land_refs[i].at[_block_id(me)],
                    send_sem=send_sems.at[N_PEERS * i + k], recv_sem=recv_sems.at[N_PEERS * i + k],
                    device_id=peer, device_id_type=MESH).start()
        token[...] = jnp.zeros_like(token)

    lands = [lax.empty((N_DEV,) + s.shape, s.dtype) for s in shards]
    return _split_start(name, body, shards, lands, N_PEERS * n)


def _gather_wait(name, send_sems, recv_sems, first, shards, lands, after):
    n = len(shards)

    def body(*refs):
        x_refs, land_refs = refs[:n], refs[n:2 * n]
        send_sems, recv_sems = refs[2 * n], refs[2 * n + 1]
        for i in range(n):
            for k, peer in enumerate(_gather_peers()):
                cp = pltpu.make_async_remote_copy(
                    src_ref=x_refs[i], dst_ref=land_refs[i].at[_block_id(peer)],
                    send_sem=send_sems.at[N_PEERS * (first + i) + k],
                    recv_sem=recv_sems.at[N_PEERS * (first + i) + k],
                    device_id=peer, device_id_type=MESH)
                cp.wait_send()
                cp.wait_recv()

    return _split_wait(name, body, send_sems, recv_sems, shards, lands, after)


def _gather_forward(name, lands, shards):
    n = len(shards)

    def body(*refs):
        x_refs, out_refs = refs[n:2 * n], refs[2 * n:3 * n]
        send_sems, recv_sems, local_sems = refs[3 * n:]
        x, y, c = lax.axis_index("x"), lax.axis_index("y"), lax.axis_index("c")
        chips = [(1 - x, y), (x, 1 - y), (1 - x, 1 - y)]
        mine = [pltpu.make_async_copy(x_refs[i], out_refs[i].at[_block_id((x, y, c))], local_sems.at[i])
                for i in range(n)]
        passed = [pltpu.make_async_remote_copy(
            src_ref=out_refs[i].at[_block_id((*chip, c))], dst_ref=out_refs[i].at[_block_id((*chip, c))],
            send_sem=send_sems.at[3 * i + j], recv_sem=recv_sems.at[3 * i + j], device_id=(x, y, 1 - c),
            device_id_type=MESH) for i in range(n) for j, chip in enumerate(chips)]
        for cp in mine + passed:
            cp.start()
        for i in range(n):
            for j, chip in enumerate(chips):
                pltpu.make_async_remote_copy(
                    src_ref=out_refs[i].at[_block_id((*chip, c))], dst_ref=out_refs[i].at[_block_id((*chip, 1 - c))],
                    send_sem=send_sems.at[3 * i + j], recv_sem=recv_sems.at[3 * i + j], device_id=(x, y, 1 - c),
                    device_id_type=MESH).wait()
        for cp in mine:
            cp.wait()

    return pl.pallas_call(
        body, name=name, in_specs=[HBM_SPEC] * (2 * n), out_specs=[HBM_SPEC] * n,
        out_shape=[jax.ShapeDtypeStruct(a.shape, a.dtype) for a in lands],
        input_output_aliases={i: i for i in range(n)},
        scratch_shapes=[pltpu.SemaphoreType.DMA((3 * n,)), pltpu.SemaphoreType.DMA((3 * n,)),
                        pltpu.SemaphoreType.DMA((n,))])(*lands, *shards)


def _chip_peers():
    x, y, c = lax.axis_index("x"), lax.axis_index("y"), lax.axis_index("c")
    return [(1 - x, y, c), (x, 1 - y, c), (1 - x, 1 - y, c)]


def _chip_start(name, blocks):
    n = len(blocks)

    def body(*refs):
        p_refs, land_refs = refs[:n], refs[n:2 * n]
        send_sems, recv_sems, token = refs[2 * n], refs[2 * n + 1], refs[-1]
        for i in range(n):
            for j, peer in enumerate(_chip_peers()):
                pltpu.make_async_remote_copy(
                    src_ref=p_refs[i].at[j], dst_ref=land_refs[i].at[j], send_sem=send_sems.at[3 * i + j],
                    recv_sem=recv_sems.at[3 * i + j], device_id=peer, device_id_type=MESH).start()
        token[...] = jnp.zeros_like(token)

    lands = [lax.empty(b.shape, b.dtype) for b in blocks]
    return _split_start(name, body, blocks, lands, 3 * n)


def _chip_wait(name, send_sems, recv_sems, blocks, lands, after):
    n = len(blocks)

    def body(*refs):
        p_refs, land_refs = refs[:n], refs[n:2 * n]
        send_sems, recv_sems = refs[2 * n], refs[2 * n + 1]
        for i in range(n):
            for j, peer in enumerate(_chip_peers()):
                cp = pltpu.make_async_remote_copy(
                    src_ref=p_refs[i].at[j], dst_ref=land_refs[i].at[j], send_sem=send_sems.at[3 * i + j],
                    recv_sem=recv_sems.at[3 * i + j], device_id=peer, device_id_type=MESH)
                cp.wait_send()
                cp.wait_recv()

    return _split_wait(name, body, send_sems, recv_sems, blocks, lands, after)


def _scatter_start(name, blocks):
    n = len(blocks)

    def body(*refs):
        b_refs, land_refs = refs[:n], refs[n:2 * n]
        send_sems, recv_sems, token = refs[2 * n], refs[2 * n + 1], refs[-1]
        me = (lax.axis_index("x"), lax.axis_index("y"), lax.axis_index("c"))
        for i in range(n):
            for k, peer in enumerate(_gather_peers()):
                pltpu.make_async_remote_copy(
                    src_ref=b_refs[i].at[_block_id(peer)], dst_ref=land_refs[i].at[_block_id(me)],
                    send_sem=send_sems.at[N_PEERS * i + k], recv_sem=recv_sems.at[N_PEERS * i + k],
                    device_id=peer, device_id_type=MESH).start()
        token[...] = jnp.zeros_like(token)

    lands = [lax.empty(b.shape, b.dtype) for b in blocks]
    return _split_start(name, body, blocks, lands, N_PEERS * n)


def _scatter_wait(name, send_sems, recv_sems, blocks, lands, after):
    n = len(blocks)

    def body(*refs):
        b_refs, land_refs = refs[:n], refs[n:2 * n]
        send_sems, recv_sems = refs[2 * n], refs[2 * n + 1]
        for i in range(n):
            for k, peer in enumerate(_gather_peers()):
                cp = pltpu.make_async_remote_copy(
                    src_ref=b_refs[i].at[_block_id(peer)], dst_ref=land_refs[i].at[_block_id(peer)],
                    send_sem=send_sems.at[N_PEERS * i + k], recv_sem=recv_sems.at[N_PEERS * i + k],
                    device_id=peer, device_id_type=MESH)
                cp.wait_send()
                cp.wait_recv()

    return _split_wait(name, body, send_sems, recv_sems, blocks, lands, after)


def _pair_exchange(name, blocks):
    n = len(blocks)

    def body(*refs):
        g_refs, out_refs = refs[:n], refs[n:2 * n]
        send_sems, recv_sems = refs[2 * n:]
        x, y, c = lax.axis_index("x"), lax.axis_index("y"), lax.axis_index("c")
        copies = [pltpu.make_async_remote_copy(
            src_ref=g_refs[i].at[2 * k + 1 - c], dst_ref=out_refs[i].at[k], send_sem=send_sems.at[4 * i + k],
            recv_sem=recv_sems.at[4 * i + k], device_id=(x, y, 1 - c), device_id_type=MESH)
            for i in range(n) for k in range(4)]
        for cp in copies:
            cp.start()
        for cp in copies:
            cp.wait()

    outs = [jax.ShapeDtypeStruct((4,) + b.shape[1:], b.dtype) for b in blocks]
    return _comm_call(name, body, blocks, outs, (4 * n, 4 * n))


def _chip_exchange(name, blocks):
    n = len(blocks)

    def body(*refs):
        p_refs, out_refs = refs[:n], refs[n:2 * n]
        send_sems, recv_sems = refs[2 * n:]
        x, y, c = lax.axis_index("x"), lax.axis_index("y"), lax.axis_index("c")
        chips = [(1 - x, y), (x, 1 - y), (1 - x, 1 - y)]
        copies = [pltpu.make_async_remote_copy(
            src_ref=p_refs[i].at[j], dst_ref=out_refs[i].at[j], send_sem=send_sems.at[3 * i + j],
            recv_sem=recv_sems.at[3 * i + j], device_id=(*chip, c), device_id_type=MESH)
            for i in range(n) for j, chip in enumerate(chips)]
        for cp in copies:
            cp.start()
        for cp in copies:
            cp.wait()

    outs = [jax.ShapeDtypeStruct(b.shape, b.dtype) for b in blocks]
    return _comm_call(name, body, blocks, outs, (3 * n, 3 * n))


def _pair_sum(name, blocks, from_sibling, g_idx, r_idx):
    _, r, c_ = blocks.shape
    tr = _tile(r, 512, 16)

    def body(gi_ref, ri_ref, a_ref, b_ref, own_ref, send_ref):
        k = pl.program_id(1)
        s = a_ref[...] + b_ref[...]

        @pl.when(k == 0)
        def _():
            own_ref[...] = s

        @pl.when(k > 0)
        def _():
            send_ref[...] = s.astype(send_ref.dtype)

    return pl.pallas_call(
        body, name=name,
        grid_spec=pltpu.PrefetchScalarGridSpec(
            num_scalar_prefetch=2, grid=(r // tr, 4),
            in_specs=[pl.BlockSpec((None, tr, c_), lambda i, k, gi, ri: (gi[k], i, 0)),
                      pl.BlockSpec((None, tr, c_), lambda i, k, gi, ri: (ri[k], i, 0))],
            out_specs=[pl.BlockSpec((None, tr, c_), lambda i, k, gi, ri: (0, i, 0)),
                       pl.BlockSpec((None, tr, c_), lambda i, k, gi, ri: (jnp.maximum(k - 1, 0), i, 0))]),
        out_shape=[jax.ShapeDtypeStruct((1, r, c_), F32), jax.ShapeDtypeStruct((3, r, c_), PAYLOAD)],
        compiler_params=_params("parallel", "arbitrary"))(g_idx, r_idx, blocks, from_sibling)


def _adamw(w, g, m, v):
    m = ADAM_B1 * m + (1.0 - ADAM_B1) * g
    v = ADAM_B2 * v + (1.0 - ADAM_B2) * (g * g)
    m_hat = m / (1.0 - ADAM_B1 ** ADAM_STEP)
    v_hat = v / (1.0 - ADAM_B2 ** ADAM_STEP)
    delta = -ADAM_LR * (m_hat / (jnp.sqrt(v_hat) + ADAM_EPS) + ADAM_WD * w)
    return delta, m, v


def _adamw_tiles(r, c_):
    tr = _tile(r, 256, 16)
    return (tr, c_) if tr < r or r <= 256 else (r, _tile(c_, 256))


def _sum_parts(part):
    g = part[0].astype(F32)
    for k in range(1, part.shape[0]):
        g = g + part[k].astype(F32)
    return g


def _sum_adamw(name, parts, w, m, v):
    r, c_ = w.shape
    tr, tc = _adamw_tiles(r, c_)

    def body(p_ref, w_ref, m_ref, v_ref, g_ref, d_ref, nm_ref, nv_ref):
        g = _sum_parts(p_ref)
        g_ref[...] = g
        d_ref[...], nm_ref[...], nv_ref[...] = _adamw(w_ref[...], g, m_ref[...], v_ref[...])

    tile = pl.BlockSpec((tr, tc), lambda i, j: (i, j))
    return pl.pallas_call(body, name=name, grid=(r // tr, c_ // tc),
                          in_specs=[pl.BlockSpec((parts.shape[0], tr, tc), lambda i, j: (0, i, j)), tile, tile, tile],
                          out_specs=[tile] * 4, out_shape=[jax.ShapeDtypeStruct((r, c_), F32)] * 4,
                          compiler_params=_params("parallel", "parallel"))(parts, w, m, v)


def _sum_adamw_layers(name, parts, w, m, v):
    n_layers, r, c_ = w.shape
    tr = _tile(r, 256, 16)

    def body(*refs):
        p_refs = refs[:n_layers]
        w_ref, m_ref, v_ref, g_ref, d_ref, nm_ref, nv_ref = refs[n_layers:]
        layer = pl.program_id(0)
        g = _sum_parts(p_refs[0])
        for li in range(1, n_layers):
            g = jnp.where(layer == li, _sum_parts(p_refs[li]), g)
        g_ref[...] = g
        d_ref[...], nm_ref[...], nv_ref[...] = _adamw(w_ref[...], g, m_ref[...], v_ref[...])

    row = pl.BlockSpec((None, tr, c_), lambda l, i: (l, i, 0))
    specs = [pl.BlockSpec((p.shape[0], tr, c_), lambda l, i: (0, i, 0)) for p in parts]
    return pl.pallas_call(body, name=name, grid=(n_layers, r // tr), in_specs=specs + [row, row, row],
                          out_specs=[row] * 4, out_shape=[jax.ShapeDtypeStruct(w.shape, F32)] * 4,
                          compiler_params=_params("parallel", "parallel"))(*parts, w, m, v)


def _pack_rows(flat, n_rows, cols):
    pad = n_rows * cols - flat.shape[-1]
    flat = jnp.pad(flat, [(0, 0)] * (flat.ndim - 1) + [(0, pad)])
    return flat.reshape(flat.shape[:-1] + (n_rows, cols))


def _cols_join(blocks):
    return jnp.concatenate([blocks[d] for d in range(N_DEV)], axis=1)


def _cols_split(full):
    c = full.shape[1] // N_DEV
    return jnp.stack([full[:, d * c:(d + 1) * c] for d in range(N_DEV)])


def _rows_join(blocks):
    return blocks.reshape(N_DEV * blocks.shape[1], blocks.shape[2])


def _rows_split(full):
    return full.reshape(N_DEV, full.shape[0] // N_DEV, full.shape[1])


def _perm_xbc(a, ng):
    lead = a.shape[:-1]
    di, gn = ng * GW, ng * SSD_D_STATE
    xs = a[..., :di].reshape(lead + (ng, GW))
    bs = a[..., di:di + gn].reshape(lead + (ng, SSD_D_STATE))
    cs = a[..., di + gn:].reshape(lead + (ng, SSD_D_STATE))
    return jnp.concatenate([xs, bs, cs], axis=-1).reshape(lead + (ng * GC,))


def _unperm_xbc(a, ng):
    lead = a.shape[:-1]
    g = a.reshape(lead + (ng, GC))
    return jnp.concatenate([g[..., :GW].reshape(lead + (ng * GW,)),
                            g[..., GW:GW + SSD_D_STATE].reshape(lead + (ng * SSD_D_STATE,)),
                            g[..., GW + SSD_D_STATE:].reshape(lead + (ng * SSD_D_STATE,))], axis=-1)


def _heads_col(v, ng):
    return jnp.pad(v.reshape(ng, 1, SSD_HPG), ((0, 0), (0, 0), (0, LANES - SSD_HPG)))


def _heads_row(v, ng):
    return jnp.pad(v.reshape(ng, SSD_HPG, 1), ((0, 0), (0, 8 - SSD_HPG), (0, 0)))


MATRIX_ITEMS = ("w_in", "w_out", "up0", "down0", "w_qkv", "w_o", "up1", "down1")
VECTOR_ITEMS = ("conv_w", "b_qkv", "b_o")
ITEMS = MATRIX_ITEMS + VECTOR_ITEMS
GATHER_STAGES = (("w_in", "conv_w"), ("w_out", "up0", "down0"), ("w_qkv", "b_qkv", "w_o", "b_o", "up1", "down1"))


def _items(tree, prefix=""):
    g = lambda k: tree[prefix + k]
    return {"w_in": g("ssd_w_in")[0].T, "w_out": g("ssd_w_out")[0], "w_qkv": g("attn_w_qkv")[0].T,
            "w_o": g("attn_w_o")[0], "up0": g("mlp_w_up")[0], "up1": g("mlp_w_up")[1],
            "down0": g("mlp_w_down")[0], "down1": g("mlp_w_down")[1], "conv_w": g("ssd_conv_w")[0],
            "b_qkv": g("attn_b_qkv"), "b_o": g("attn_b_o")}


def _from_items(it):
    return {"ssd_w_in": it["w_in"][None], "ssd_w_out": it["w_out"][None], "attn_w_qkv": it["w_qkv"].T[None],
            "attn_w_o": it["w_o"][None], "mlp_w_up": jnp.stack([it["up0"], it["up1"]]),
            "mlp_w_down": jnp.stack([it["down0"], it["down1"]]), "ssd_conv_w": it["conv_w"][None],
            "attn_b_qkv": it["b_qkv"], "attn_b_o": it["b_o"]}


REPLICATED = ("ssd_conv_b", "ssd_dt_bias", "ssd_a_log", "ssd_d", "ssd_norm_w", "attn_sinks", "mix_pre_norm",
              "mix_post_norm", "ffn_pre_norm", "ffn_post_norm")
WEIGHTS = ("ssd_w_in", "ssd_conv_w", "ssd_conv_b", "ssd_dt_bias", "ssd_a_log", "ssd_d", "ssd_norm_w", "ssd_w_out",
           "attn_w_qkv", "attn_b_qkv", "attn_sinks", "attn_w_o", "attn_b_o", "mlp_w_up", "mlp_w_down",
           "mix_pre_norm", "mix_post_norm", "ffn_pre_norm", "ffn_post_norm")


def _forward_backward(x, target, rep, token, weights_of_stage, reduce_grads):
    t, d = x.shape
    ng = rep["ssd_norm_w"].shape[1] // GW
    di = ng * GW
    n_xbc = ng * GC
    nh = ng * SSD_HPG
    grads, blocks = {}, {}
    w_up, w_down = [None, None], [None, None]
    sinks_rep = jnp.repeat(rep["attn_sinks"].reshape(ATTN_N_KV, ATTN_REP, 1), ATTN_WINDOW, axis=2).reshape(
        ATTN_N_KV, 1, ATTN_REP * ATTN_WINDOW)
    conv_b = rep["ssd_conv_b"]
    gn = ng * SSD_D_STATE
    parts = ((0, di), (di, di), (2 * di, gn), (2 * di + gn, gn), (di + n_xbc, nh))
    alog_c, dsk_c = (_heads_col(rep[k], ng) for k in ("ssd_a_log", "ssd_d"))
    bias_l, alog_l = (jnp.pad(rep[k], ((0, 0), (0, LANES - nh))) for k in ("ssd_dt_bias", "ssd_a_log"))
    norm = {k: rep[k] for k in ("mix_pre_norm", "mix_post_norm", "ffn_pre_norm", "ffn_post_norm")}

    def nrow(name, i):
        return norm[name][i:i + 1]

    def mlp_fwd(i, u2):
        p = _mm(f"mlp{i}_up", [u2], [w_up[i]], "nn", tm=1024, tn=1024, out_dtypes=(BF16,),
                epilogue=lambda acc: (jnp.square(jnp.maximum(acc, 0.0)),))
        f = _mm(f"mlp{i}_down", [p], [w_down[i]], "nn", tm=512, tn=1024)
        return p, f

    def mlp_bwd(i, df, u2, p):
        da = _mm(f"mlp{i}_dact", [df], [w_down[i]], "nt", tm=1024, tn=1024, out_dtypes=(BF16,),
                 tiles=(p,), epilogue=lambda acc, pv: (acc * (2.0 * jnp.sqrt(pv.astype(F32))),))
        blocks[f"down{i}"] = _rows_split(_mm(f"mlp{i}_dwdown", [p], [df], "tn", tm=512, tn=1024,
                                             out_dtypes=(PAYLOAD,)))
        blocks[f"up{i}"] = _mm(f"mlp{i}_dwup", [u2], [da], "tn", tm=1024, tn=da.shape[1] // N_DEV,
                               out_dtypes=(PAYLOAD,), col_blocks=True)
        return _mm(f"mlp{i}_dx", [da], [w_up[i]], "nt", tm=512, tn=1024)

    u0 = _prenorm("l0_prenorm", x, nrow("mix_pre_norm", 0), token)
    got = weights_of_stage(0, u0)
    w_in_t = _rows_join(got["w_in"])
    w_dt_t = jnp.pad(w_in_t[di + n_xbc:], ((0, LANES - nh), (0, 0)))
    conv_w = _cols_join(got["conv_w"])
    zx = _mm("ssd_in_proj", [u0], [w_in_t], "nt", tm=1024, tn=1024, n_use=di + n_xbc)
    zdt = _mm("ssd_dt_proj", [u0], [w_dt_t], "nt", tm=1024, tn=LANES)
    pre = _conv_fwd(zx, di, n_xbc, conv_w, conv_b)
    dt_c, cum_c, cum_r, sgd_c = _ssd_dt_prep(zdt, bias_l, alog_l, ng)
    y, states = _ssd_fwd(pre, dt_c, cum_c, cum_r, alog_c, dsk_c)
    yn = _gate_norm_fwd(y, zx, rep["ssd_norm_w"])
    got = weights_of_stage(1, yn)
    w_out = _rows_join(got["w_out"])
    w_up[0], w_down[0] = _cols_join(got["up0"]), _rows_join(got["down0"])
    mix0 = _mm("ssd_out_proj", [yn], [w_out], "nn", tm=1024, tn=1024)
    h1, u0f = _post_pre("l0_mid", x, mix0, nrow("mix_post_norm", 0), nrow("ffn_pre_norm", 0))
    p0, f0 = mlp_fwd(0, u0f)
    h2, u1 = _post_pre("l1_in", h1, f0, nrow("ffn_post_norm", 0), nrow("mix_pre_norm", 1))
    got = weights_of_stage(2, u1)
    w_qkv_t = _rows_join(got["w_qkv"])
    w_o = _rows_join(got["w_o"])
    b_qkv_col = got["b_qkv"].reshape(-1, 1)
    b_o = _cols_join(got["b_o"])
    w_up[1], w_down[1] = _cols_join(got["up1"]), _rows_join(got["down1"])
    qkv_t = _mm("attn_qkv_proj", [w_qkv_t], [u1], "nt", tm=768, tn=1024, out_dtypes=(BF16,), cols=(b_qkv_col,),
                epilogue=lambda acc, b: (acc + b,))
    ao_t = _attn_fwd_t(qkv_t, sinks_rep)
    mix1 = _mm("attn_out_proj", [ao_t], [w_o], "tn", tm=1024, tn=1024, rows=(b_o,),
               epilogue=lambda acc, b: (acc + b,))
    h3, u1f = _post_pre("l1_mid", h2, mix1, nrow("mix_post_norm", 1), nrow("ffn_pre_norm", 1))
    p1, f1 = mlp_fwd(1, u1f)
    dh, loss_row = _final_loss("loss", h3, f1, nrow("ffn_post_norm", 1), target)

    g_norm = {k: [None, None] for k in norm}
    df1, g_norm["ffn_post_norm"][1], _ = _norm_bwd("l1_ffn_post_bwd", dh, post=(f1, nrow("ffn_post_norm", 1)))
    du = mlp_bwd(1, df1, u1f, p1)
    sent = reduce_grads("mlp1", {k: blocks[k] for k in ("up1", "down1")})
    dh, g_norm["ffn_pre_norm"][1], dmix1, g_norm["mix_post_norm"][1], db_o = _norm_bwd(
        "l1_mid_bwd", dh, pre=(du, h3, nrow("ffn_pre_norm", 1)), post=(mix1, nrow("mix_post_norm", 1)), after=sent)
    blocks["b_o"] = _cols_split(db_o)
    blocks["w_o"] = _rows_split(_mm("attn_dwo", [ao_t], [dmix1], "nn", tm=512, tn=1024, out_dtypes=(PAYLOAD,)))
    dao_t = _mm("attn_dout", [w_o], [dmix1], "nt", tm=1024, tn=1024, out_dtypes=(BF16,))
    dqkv_t, db_qkv, grads["attn_sinks"] = _attn_bwd_t(qkv_t, dao_t, sinks_rep)
    blocks["b_qkv"] = db_qkv.reshape(N_DEV, 1, -1)
    blocks["w_qkv"] = _rows_split(_mm("attn_dwqkv", [dqkv_t], [u1], "nn", tm=512, tn=1024, out_dtypes=(PAYLOAD,)))
    du = _mm("attn_dx", [dqkv_t], [w_qkv_t], "tn", tm=1024, tn=1024)
    sent = reduce_grads("attn", {k: blocks[k] for k in ("w_o", "w_qkv", "b_o", "b_qkv")})
    dh, g_norm["mix_pre_norm"][1], df0, g_norm["ffn_post_norm"][0], _ = _norm_bwd(
        "l1_in_bwd", dh, pre=(du, h2, nrow("mix_pre_norm", 1)), post=(f0, nrow("ffn_post_norm", 0)), after=sent)
    du = mlp_bwd(0, df0, u0f, p0)
    sent = reduce_grads("mlp0", {k: blocks[k] for k in ("up0", "down0")})
    dh, g_norm["ffn_pre_norm"][0], dmix0, g_norm["mix_post_norm"][0], _ = _norm_bwd(
        "l0_mid_bwd", dh, pre=(du, h1, nrow("ffn_pre_norm", 0)), post=(mix0, nrow("mix_post_norm", 0)), after=sent)
    blocks["w_out"] = _rows_split(_mm("ssd_dwout", [yn], [dmix0], "tn", tm=512, tn=1024, out_dtypes=(PAYLOAD,)))
    dyn = _mm("ssd_dyn", [dmix0], [w_out], "nt", tm=1024, tn=1024)
    sent = reduce_grads("ssdout", {"w_out": blocks["w_out"]})
    dy, dz, grads["ssd_norm_w"] = _gate_norm_bwd(dyn, y, zx, rep["ssd_norm_w"], sent)
    dpx, dpb, dpc, ddt_g, dbias_g, dalog_g, dd_g = _ssd_bwd(dy, pre, states, dt_c, cum_c, cum_r, sgd_c, alog_c,
                                                             dsk_c)
    conv_out = [_conv_bwd(f"ssd_conv_bwd_{tag}", dp, zx, c0, conv_w[:, c0 - di:c0 - di + n])
                for tag, dp, (c0, n) in zip("xbc", (dpx, dpb, dpc), parts[1:4])]
    dconv_w = jnp.concatenate([o[1] for o in conv_out], axis=1)
    dconv_b = jnp.concatenate([o[2] for o in conv_out], axis=1)
    ddt = jnp.transpose(ddt_g[:, :, :SSD_HPG], (1, 0, 2)).reshape(t, nh)
    ddt = jnp.pad(ddt, ((0, 0), (0, LANES - nh))).astype(BF16)
    blocks["conv_w"] = _cols_split(dconv_w)
    grads["ssd_conv_b"] = dconv_b
    for name, val in (("ssd_dt_bias", dbias_g), ("ssd_a_log", dalog_g), ("ssd_d", dd_g)):
        grads[name] = val[:, 0, :SSD_HPG].reshape(1, nh)
    d_zx = [dz] + [o[0] for o in conv_out] + [ddt]
    dw_parts = [_mm(f"ssd_dw_{tag}", [d], [u0], "tn", tm=512, tn=1024, out_dtypes=(PAYLOAD,))
                for tag, d in zip("zxbct", d_zx)]
    dw_parts[-1] = dw_parts[-1][:nh]
    blocks["w_in"] = _rows_split(jnp.concatenate(dw_parts, axis=0))
    sent = reduce_grads("ssd", {k: blocks[k] for k in ("w_in", "conv_w")})
    w_parts = [w_in_t[r0:r0 + n] for r0, n in parts[:-1]] + [w_dt_t]
    du = _mm("ssd_dx", d_zx, w_parts, "nn", tm=256, tn=1024)
    grad_x, g_norm["mix_pre_norm"][0] = _norm_bwd("l0_in_bwd", dh, pre=(du, x, nrow("mix_pre_norm", 0)), after=sent)
    for k in norm:
        grads[k] = jnp.concatenate(g_norm[k], axis=0)
    return loss_row, grad_x, grads


def kernel(x, ssd_w_in, ssd_conv_w, ssd_conv_b, ssd_dt_bias, ssd_a_log, ssd_d, ssd_norm_w, ssd_w_out, attn_w_qkv, attn_b_qkv, attn_sinks, attn_w_o, attn_b_o, mlp_w_up, mlp_w_down, mix_pre_norm, mix_post_norm, ffn_pre_norm, ffn_post_norm, loss_target, m_ssd_w_in, m_ssd_conv_w, m_ssd_conv_b, m_ssd_dt_bias, m_ssd_a_log, m_ssd_d, m_ssd_norm_w, m_ssd_w_out, m_attn_w_qkv, m_attn_b_qkv, m_attn_sinks, m_attn_w_o, m_attn_b_o, m_mlp_w_up, m_mlp_w_down, m_mix_pre_norm, m_mix_post_norm, m_ffn_pre_norm, m_ffn_post_norm, v_ssd_w_in, v_ssd_conv_w, v_ssd_conv_b, v_ssd_dt_bias, v_ssd_a_log, v_ssd_d, v_ssd_norm_w, v_ssd_w_out, v_attn_w_qkv, v_attn_b_qkv, v_attn_sinks, v_attn_w_o, v_attn_b_o, v_mlp_w_up, v_mlp_w_down, v_mix_pre_norm, v_mix_post_norm, v_ffn_pre_norm, v_ffn_post_norm):
    given = dict(locals())
    w = {k: given[k] for k in WEIGHTS}
    mom_m = {k: given["m_" + k] for k in WEIGHTS}
    mom_v = {k: given["v_" + k] for k in WEIGHTS}
    w_it, m_it, v_it = _items(given), _items(given, "m_"), _items(given, "v_")

    order = [k for stage in GATHER_STAGES for k in stage]
    shards = [w_it[k].astype(PAYLOAD) if k in MATRIX_ITEMS else w_it[k] for k in order]
    g_send, g_recv, shards, lands, token = _gather_start("gather_start", shards)

    def weights_of_stage(s, after):
        first = sum(len(stage) for stage in GATHER_STAGES[:s])
        sl = slice(first, first + len(GATHER_STAGES[s]))
        srcs, got = _gather_wait(f"gather_wait{s}", g_send, g_recv, first, shards[sl], lands[sl], after)
        me = 4 * ix + 2 * iy + ic
        return {k: lax.dynamic_update_slice(land, src[None], (me,) + (0,) * src.ndim)
                for k, land, src in zip(GATHER_STAGES[s], got, srcs)}

    ix, iy, ic = lax.axis_index("x"), lax.axis_index("y"), lax.axis_index("c")
    in_flight = []

    def reduce_grads(tag, blocks):
        keys = list(blocks)
        started = _scatter_start(f"rs_start_{tag}", [blocks[k] for k in keys])
        in_flight.append((tag, keys, started))
        return started[-1]

    rep = {k: w[k] for k in REPLICATED}
    loss_row, grad_x, grads = _forward_backward(x[0], loss_target[0], rep, token, weights_of_stage, reduce_grads)

    def pack_rep(tree, last):
        flat = jnp.concatenate([tree[k].reshape(-1) for k in REPLICATED] + [last])
        return _pack_rows(flat, _round_up(-(-flat.shape[0] // LANES), 8), LANES)

    landed = {}
    me = 4 * ix + 2 * iy + ic

    def wait_group(group, after):
        tag, keys, (s_send, s_recv, srcs, s_lands, _) = group
        srcs, got = _scatter_wait(f"rs_wait_{tag}", s_send, s_recv, srcs, s_lands, after)
        for k, src, land in zip(keys, srcs, got):
            own = lax.dynamic_index_in_dim(src, me, 0, keepdims=True)
            landed[k] = lax.dynamic_update_slice(land, own, (me,) + (0,) * (land.ndim - 1))

    def adamw_item(k):
        return _sum_adamw(f"adamw_{k}", landed[k], w_it[k], m_it[k], v_it[k])

    def adamw_stack(name, keys):
        return _sum_adamw_layers(f"adamw_{name}", [landed[k] for k in keys], given[name], given["m_" + name],
                                 given["v_" + name])

    for group in in_flight[:-1]:
        wait_group(group, grad_x)
    done = {"mlp_w_up": adamw_stack("mlp_w_up", ("up0", "up1")),
            "mlp_w_down": adamw_stack("mlp_w_down", ("down0", "down1")),
            "attn_w_qkv": [o.T[None] for o in adamw_item("w_qkv")],
            "attn_w_o": [o[None] for o in adamw_item("w_o")],
            "attn_b_qkv": adamw_item("b_qkv"), "attn_b_o": adamw_item("b_o"),
            "ssd_w_out": [o[None] for o in adamw_item("w_out")]}
    partials, = _all_gather("gather_small_grads", [pack_rep(grads, loss_row[0, :1])], done["mlp_w_down"][1])
    wait_group(in_flight[-1], partials)
    done["ssd_w_in"] = [o.T[None] for o in adamw_item("w_in")]
    done["ssd_conv_w"] = [o[None] for o in adamw_item("conv_w")]
    zero = jnp.zeros((1,), F32)
    rep_out = _sum_adamw("adamw_replicated", partials, pack_rep(w, zero), pack_rep(mom_m, zero), pack_rep(mom_v, zero))

    kinds = []
    for kind, r_arr in enumerate(rep_out):
        tree = {name: outs4[kind] for name, outs4 in done.items()}
        flat, off = r_arr.reshape(-1), 0
        for k in REPLICATED:
            tree[k] = flat[off:off + w[k].size].reshape(w[k].shape)
            off += w[k].size
        kinds.append(tree)
    loss = rep_out[0].reshape(-1)[off]
    outs = [loss, grad_x[None]]
    for tree in kinds:
        outs += [tree[k] for k in WEIGHTS]
    return tuple(outs)
```

```python
import functools

import jax
import jax.numpy as jnp
from jax import lax
from jax.experimental import pallas as pl
from jax.experimental.pallas import tpu as pltpu

F32 = jnp.float32
BF16 = jnp.bfloat16
PAYLOAD = jnp.bfloat16
HIGHEST = lax.Precision.HIGHEST
MESH = pl.DeviceIdType.MESH

NORM_EPS = 1e-6
SSD_HEAD_DIM = 64
SSD_N_GROUPS = 8
SSD_HPG = 4
SSD_D_STATE = 128
SSD_CONV_WIDTH = 4
SSD_CHUNK = 128
ATTN_HEAD_DIM = 64
ATTN_N_KV = 4
ATTN_REP = 4
ATTN_WINDOW = 128
ADAM_LR = 0.001
ADAM_B1 = 0.9
ADAM_B2 = 0.999
ADAM_EPS = 1e-08
ADAM_WD = 0.01
ADAM_STEP = 10

N_DEV = 8
LANES = 128
PACK_COLS = 1024
V7X_VMEM_LIMIT = 56 * 1024 * 1024

GW = SSD_HPG * SSD_HEAD_DIM
GC = GW + 2 * SSD_D_STATE


def _params(*sem):
    return pltpu.CompilerParams(dimension_semantics=sem, vmem_limit_bytes=V7X_VMEM_LIMIT)


def _tile(n, pref, mult=LANES):
    best = None
    t = mult
    while t <= min(n, pref):
        if n % t == 0:
            best = t
        t += mult
    return best if best is not None else n


def _round_up(n, m):
    return (n + m - 1) // m * m


def _acc(ref, val, first):
    @pl.when(first)
    def _():
        ref[...] = val

    @pl.when(jnp.logical_not(first))
    def _():
        ref[...] += val


def _dot(a, b):
    return lax.dot_general(a, b, (((1,), (0,)), ((), ())), preferred_element_type=F32)


def _dot_nt(a, b):
    return lax.dot_general(a, b, (((1,), (1,)), ((), ())), preferred_element_type=F32)


def _dot_tn(a, b):
    return lax.dot_general(a, b, (((0,), (0,)), ((), ())), preferred_element_type=F32)


def _dot_f32(a, b):
    return lax.dot_general(a, b, (((1,), (0,)), ((), ())), preferred_element_type=F32, precision=HIGHEST)


_DOTS = {"nn": _dot, "nt": _dot_nt, "tn": _dot_tn}


def _sigmoid(x):
    return 1.0 / (1.0 + jnp.exp(-x))


def _softplus(x):
    return jnp.maximum(x, 0.0) + jnp.log1p(jnp.exp(-jnp.abs(x)))


def _silu_grad(x, s):
    return s * (1.0 + x * (1.0 - s))


def _mm(name, a_list, b_list, mode, *, tm, tn, out_dtypes=(F32,), epilogue=None, tiles=(), rows=(), cols=(),
        col_blocks=False, n_use=None, after=None):
    npair = len(a_list)
    if mode == "tn":
        m = a_list[0].shape[1]
    else:
        m = a_list[0].shape[0]
    n = n_use if n_use is not None else (b_list[0].shape[0] if mode == "nt" else b_list[0].shape[1])
    tm = _tile(m, tm, LANES if mode == "tn" else 8)
    tn = _tile(n, tn)
    assert m % tm == 0 and n % tn == 0, (name, m, n, tm, tn)
    dot = _DOTS[mode]

    def body(*refs):
        a_refs = refs[:npair]
        b_refs = refs[npair:2 * npair]
        n_extra = len(tiles) + len(rows) + len(cols)
        e_refs = refs[2 * npair:2 * npair + n_extra]
        o_refs = refs[2 * npair + n_extra + len(order):]
        acc = None
        for ar, br in zip(a_refs, b_refs):
            d = dot(ar[...], br[...])
            acc = d if acc is None else acc + d
        outs = epilogue(acc, *[e[...] for e in e_refs]) if epilogue is not None else (acc,)
        for o, v in zip(o_refs, outs):
            o[...] = v.astype(o.dtype)

    in_specs = []
    for a in a_list:
        if mode == "tn":
            in_specs.append(pl.BlockSpec((a.shape[0], tm), lambda i, j: (0, i)))
        else:
            in_specs.append(pl.BlockSpec((tm, a.shape[1]), lambda i, j: (i, 0)))
    for b in b_list:
        if mode == "nt":
            in_specs.append(pl.BlockSpec((tn, b.shape[1]), lambda i, j: (j, 0)))
        else:
            in_specs.append(pl.BlockSpec((b.shape[0], tn), lambda i, j: (0, j)))
    in_specs += [pl.BlockSpec((tm, tn), lambda i, j: (i, j)) for _ in tiles]
    in_specs += [pl.BlockSpec((1, tn), lambda i, j: (0, j)) for _ in rows]
    in_specs += [pl.BlockSpec((tm, 1), lambda i, j: (i, 0)) for _ in cols]
    order = [] if after is None else [after]
    in_specs += [pl.BlockSpec((8, LANES), lambda i, j: (0, 0)) for _ in order]
    outs = pl.pallas_call(
        body,
        name=name,
        grid=(m // tm, n // tn),
        in_specs=in_specs,
        out_specs=[pl.BlockSpec((None, tm, tn), lambda i, j: (j, i, 0)) if col_blocks else
                   pl.BlockSpec((tm, tn), lambda i, j: (i, j)) for _ in out_dtypes],
        out_shape=[jax.ShapeDtypeStruct((n // tn, m, tn) if col_blocks else (m, n), dt) for dt in out_dtypes],
        compiler_params=_params("parallel", "parallel"),
    )(*a_list, *b_list, *tiles, *rows, *cols, *order)
    return outs[0] if len(out_dtypes) == 1 else outs


def _rms(x, w):
    r = lax.rsqrt(jnp.mean(x * x, axis=-1, keepdims=True) + NORM_EPS)
    return x * r * w


def _rms_bwd(x, w, dy):
    r = lax.rsqrt(jnp.mean(x * x, axis=-1, keepdims=True) + NORM_EPS)
    xh = x * r
    g = dy * w
    dx = r * (g - xh * jnp.mean(g * xh, axis=-1, keepdims=True))
    return dx, dy * xh


def _row_specs(tr, d):
    return pl.BlockSpec((tr, d), lambda i: (i, 0)), pl.BlockSpec((1, d), lambda i: (0, 0))


def _prenorm(name, h, w, after):
    t, d = h.shape
    tr = _tile(t, 512, 8)
    row, vec = _row_specs(tr, d)

    def body(h_ref, w_ref, after_ref, u_ref):
        u_ref[...] = _rms(h_ref[...], w_ref[...]).astype(BF16)

    return pl.pallas_call(body, name=name, grid=(t // tr,),
                          in_specs=[row, vec, pl.BlockSpec((8, LANES), lambda i: (0, 0))], out_specs=row,
                          out_shape=jax.ShapeDtypeStruct((t, d), BF16), compiler_params=_params("parallel"))(
                              h, w, after)


def _post_pre(name, h, m, w_post, w_pre):
    t, d = h.shape
    tr = _tile(t, 512, 8)
    row, vec = _row_specs(tr, d)

    def body(h_ref, m_ref, wq_ref, wp_ref, hn_ref, u_ref):
        hn = h_ref[...] + _rms(m_ref[...], wq_ref[...])
        hn_ref[...] = hn
        u_ref[...] = _rms(hn, wp_ref[...]).astype(BF16)

    return pl.pallas_call(body, name=name, grid=(t // tr,), in_specs=[row, row, vec, vec], out_specs=[row, row],
                          out_shape=[jax.ShapeDtypeStruct((t, d), F32), jax.ShapeDtypeStruct((t, d), BF16)],
                          compiler_params=_params("parallel"))(h, m, w_post, w_pre)


def _final_loss(name, h, m, w_post, target):
    t, d = h.shape
    tr = _tile(t, 512, 8)
    row, vec = _row_specs(tr, d)

    def body(h_ref, m_ref, wq_ref, t_ref, dh_ref, loss_ref):
        err = h_ref[...] + _rms(m_ref[...], wq_ref[...]) - t_ref[...]
        dh_ref[...] = err * (1.0 / d)
        part = 0.5 * jnp.sum(jnp.mean(err * err, axis=-1, keepdims=True), axis=0, keepdims=True)
        _acc(loss_ref, jnp.broadcast_to(part, (1, LANES)), pl.program_id(0) == 0)

    return pl.pallas_call(body, name=name, grid=(t // tr,), in_specs=[row, row, vec, row],
                          out_specs=[row, pl.BlockSpec((1, LANES), lambda i: (0, 0))],
                          out_shape=[jax.ShapeDtypeStruct((t, d), F32), jax.ShapeDtypeStruct((1, LANES), F32)],
                          compiler_params=_params("arbitrary"))(h, m, w_post, target)


def _norm_bwd(name, dh, pre=None, post=None, after=None):
    t, d = dh.shape
    tr = _tile(t, 256, 8)
    row, vec = _row_specs(tr, d)
    has_pre, has_post = pre is not None, post is not None

    def body(*refs):
        it = iter(refs)
        dh_ref = next(it)
        if has_pre:
            du_ref, x_ref, wp_ref = next(it), next(it), next(it)
        if has_post:
            m_ref, wq_ref = next(it), next(it)
        if after is not None:
            next(it)
        first = pl.program_id(0) == 0
        dh_v = dh_ref[...]
        if has_pre:
            dhn_ref, dwp_ref = next(it), next(it)
            dx, dwr = _rms_bwd(x_ref[...], wp_ref[...], du_ref[...])
            dh_v = dh_v + dx
            dhn_ref[...] = dh_v
            _acc(dwp_ref, jnp.sum(dwr, axis=0, keepdims=True), first)
        if has_post:
            dm_ref, dwq_ref, dms_ref = next(it), next(it), next(it)
            dm, dwr = _rms_bwd(m_ref[...], wq_ref[...], dh_v)
            dm_ref[...] = dm.astype(BF16)
            _acc(dwq_ref, jnp.sum(dwr, axis=0, keepdims=True), first)
            _acc(dms_ref, jnp.sum(dm, axis=0, keepdims=True), first)

    ins, in_specs, out_specs, out_shape = [dh], [row], [], []
    if has_pre:
        ins += list(pre)
        in_specs += [row, row, vec]
        out_specs += [row, vec]
        out_shape += [jax.ShapeDtypeStruct((t, d), F32), jax.ShapeDtypeStruct((1, d), F32)]
    if has_post:
        ins += list(post)
        in_specs += [row, vec]
        out_specs += [row, vec, vec]
        out_shape += [jax.ShapeDtypeStruct((t, d), BF16), jax.ShapeDtypeStruct((1, d), F32),
                      jax.ShapeDtypeStruct((1, d), F32)]
    if after is not None:
        ins.append(after)
        in_specs.append(pl.BlockSpec((8, LANES), lambda i: (0, 0)))
    return pl.pallas_call(body, name=name, grid=(t // tr,), in_specs=in_specs, out_specs=out_specs,
                          out_shape=out_shape, compiler_params=_params("arbitrary"))(*ins)


HALO = 8


def _shift_later(cur, prev, s):
    rolled = pltpu.roll(cur, s, 0)
    row = lax.broadcasted_iota(jnp.int32, prev.shape, 0)
    first = jnp.where(row < s, pltpu.roll(prev, s, 0), rolled[0:HALO])
    return jnp.concatenate([first, rolled[HALO:]], axis=0)


def _shift_earlier(cur, nxt, s):
    tt = cur.shape[0]
    rolled = pltpu.roll(cur, tt - s, 0)
    row = lax.broadcasted_iota(jnp.int32, nxt.shape, 0)
    last = jnp.where(row >= HALO - s, pltpu.roll(nxt, HALO - s, 0), rolled[tt - HALO:])
    return jnp.concatenate([rolled[:tt - HALO], last], axis=0)


def _conv_fwd(zx, col0, n_ch, conv_w, conv_b):
    t = zx.shape[0]
    tc = _tile(n_ch, 512)
    tt = _tile(t, 512, 8)
    cb0 = col0 // tc
    assert col0 % tc == 0
    kw = SSD_CONV_WIDTH

    def body(x_ref, p_ref, w_ref, b_ref, o_ref):
        cur = x_ref[...]
        prev = jnp.where(pl.program_id(1) > 0, p_ref[...], 0.0)
        w = w_ref[...]
        acc = b_ref[...] + w[kw - 1:kw, :] * cur
        for k in range(kw - 1):
            acc = acc + w[k:k + 1, :] * _shift_later(cur, prev, kw - 1 - k)
        o_ref[...] = acc

    return pl.pallas_call(
        body, name="ssd_conv_fwd", grid=(n_ch // tc, t // tt),
        in_specs=[pl.BlockSpec((tt, tc), lambda j, i: (i, cb0 + j)),
                  pl.BlockSpec((HALO, tc), lambda j, i: (jnp.maximum(i * (tt // HALO) - 1, 0), cb0 + j)),
                  pl.BlockSpec((kw, tc), lambda j, i: (0, j)),
                  pl.BlockSpec((1, tc), lambda j, i: (0, j))],
        out_specs=pl.BlockSpec((tt, tc), lambda j, i: (i, j)),
        out_shape=jax.ShapeDtypeStruct((t, n_ch), F32),
        compiler_params=_params("parallel", "parallel"))(zx, zx, conv_w, conv_b)


def _conv_bwd(name, dpre, zx, col0, conv_w):
    t, n_ch = dpre.shape
    tc = _tile(n_ch, 512)
    tt = _tile(t, 512, 8)
    cb0 = col0 // tc
    kw = SSD_CONV_WIDTH
    nt = t // tt

    def body(d_ref, dn_ref, x_ref, p_ref, w_ref, dx_ref, dw_ref, db_ref):
        i = pl.program_id(1)
        d = d_ref[...]
        d_next = jnp.where(i < nt - 1, dn_ref[...], 0.0)
        x = x_ref[...]
        x_prev = jnp.where(i > 0, p_ref[...], 0.0)
        w = w_ref[...]
        dx = w[kw - 1:kw, :] * d
        for k in range(kw - 1):
            dx = dx + w[k:k + 1, :] * _shift_earlier(d, d_next, kw - 1 - k)
        dx_ref[...] = dx.astype(BF16)
        first = i == 0
        for k in range(kw):
            xs = x if k == kw - 1 else _shift_later(x, x_prev, kw - 1 - k)
            val = jnp.sum(d * xs, axis=0, keepdims=True)

            @pl.when(first)
            def _():
                dw_ref[k:k + 1, :] = val

            @pl.when(jnp.logical_not(first))
            def _():
                dw_ref[k:k + 1, :] += val
        _acc(db_ref, jnp.sum(d, axis=0, keepdims=True), first)

    return pl.pallas_call(
        body, name=name, grid=(n_ch // tc, nt),
        in_specs=[pl.BlockSpec((tt, tc), lambda j, i: (i, j)),
                  pl.BlockSpec((HALO, tc), lambda j, i: (jnp.minimum((i + 1) * (tt // HALO), t // HALO - 1), j)),
                  pl.BlockSpec((tt, tc), lambda j, i: (i, cb0 + j)),
                  pl.BlockSpec((HALO, tc), lambda j, i: (jnp.maximum(i * (tt // HALO) - 1, 0), cb0 + j)),
                  pl.BlockSpec((kw, tc), lambda j, i: (0, j))],
        out_specs=[pl.BlockSpec((tt, tc), lambda j, i: (i, j)),
                   pl.BlockSpec((kw, tc), lambda j, i: (0, j)),
                   pl.BlockSpec((1, tc), lambda j, i: (0, j))],
        out_shape=[jax.ShapeDtypeStruct((t, n_ch), BF16), jax.ShapeDtypeStruct((kw, n_ch), F32),
                   jax.ShapeDtypeStruct((1, n_ch), F32)],
        compiler_params=_params("parallel", "arbitrary"))(dpre, dpre, zx, zx, conv_w)


def _head_of_lane(shape, width):
    return lax.broadcasted_iota(jnp.int32, shape, len(shape) - 1) // width


def _expand(v, n_rows):
    head = _head_of_lane((n_rows, GW), SSD_HEAD_DIM)
    out = jnp.zeros((n_rows, GW), F32)
    for j in range(SSD_HPG):
        out = jnp.where(head == j, v[:, j:j + 1], out)
    return out


def _contract(v, n_rows):
    head = _head_of_lane((n_rows, GW), SSD_HEAD_DIM)
    lane = lax.broadcasted_iota(jnp.int32, (n_rows, LANES), 1)
    out = jnp.zeros((n_rows, LANES), F32)
    for j in range(SSD_HPG):
        s = jnp.sum(jnp.where(head == j, v, 0.0), axis=1, keepdims=True)
        out = jnp.where(lane == j, s, out)
    return out


def _ssd_dt_prep(zdt, bias, alog, ng):
    t = zdt.shape[0]
    q = SSD_CHUNK

    def body(z_ref, b_ref, a_ref, dt_ref, cum_ref, cumr_ref, sg_ref):
        raw = z_ref[...] + b_ref[...]
        dt = _softplus(raw)
        sgd = _sigmoid(raw)
        row = lax.broadcasted_iota(jnp.int32, (q, q), 0)
        col = lax.broadcasted_iota(jnp.int32, (q, q), 1)
        cum = _dot_f32((col <= row).astype(F32), dt * (-jnp.exp(a_ref[...])))
        cum_t = cum.T
        lane = lax.broadcasted_iota(jnp.int32, (q, LANES), 1)
        for g in range(ng):
            shift = (LANES - g * SSD_HPG) % LANES

            def group(v):
                return jnp.where(lane < SSD_HPG, pltpu.roll(v, shift, 1) if shift else v, 0.0)

            dt_ref[g] = group(dt)
            cum_ref[g] = group(cum)
            sg_ref[g] = group(sgd)
            cumr_ref[g] = (pltpu.roll(cum_t, shift, 0) if shift else cum_t)[0:8, :]

    cols = pl.BlockSpec((ng, q, LANES), lambda c: (0, c, 0))
    vec = pl.BlockSpec((1, LANES), lambda c: (0, 0))
    col_shape = jax.ShapeDtypeStruct((ng, t, LANES), F32)
    return pl.pallas_call(body, name="ssd_dt_prep", grid=(t // q,),
                          in_specs=[pl.BlockSpec((q, LANES), lambda c: (c, 0)), vec, vec],
                          out_specs=[cols, cols, pl.BlockSpec((ng, 8, q), lambda c: (0, 0, c)), cols],
                          out_shape=[col_shape, col_shape, jax.ShapeDtypeStruct((ng, 8, t), F32), col_shape],
                          compiler_params=_params("parallel"))(zdt, bias, alog)


def _ssd_common(pre, dt, cum, cum_r, alog_c):
    q = SSD_CHUNK
    sg = _sigmoid(pre)
    act = pre * sg
    xa = act[:, :GW]
    bm = act[:, GW:GW + SSD_D_STATE].astype(BF16)
    cm = act[:, GW + SSD_D_STATE:].astype(BF16)
    row = lax.broadcasted_iota(jnp.int32, (q, q), 0)
    col = lax.broadcasted_iota(jnp.int32, (q, q), 1)
    tril = col <= row
    a_c = -jnp.exp(alog_c)
    g = _dot_nt(cm, bm)
    dt_x = _expand(dt, q)
    xdt = xa * dt_x
    cl = cum[q - 1:q, :]
    e_c = jnp.exp(cl - cum)
    lam_c = jnp.exp(cum)
    return dict(sg=sg, xa=xa, bm=bm, cm=cm, tril=tril, row=row, col=col, dt=dt, a_c=a_c, cum=cum, cum_r=cum_r,
                g=g, dt_x=dt_x, xdt=xdt, cl=cl, e_c=e_c, lam_c=lam_c)


SSD_GPS = 2


def _ssd_specs(nc, rev, ng):
    q = SSD_CHUNK
    xw, nw = SSD_GPS * GW, SSD_GPS * SSD_D_STATE
    b_off = ng * GW // nw
    c_off = (ng * GW + ng * SSD_D_STATE) // nw
    assert ng % SSD_GPS == 0 and (ng * GW) % nw == 0 and (ng * SSD_D_STATE) % nw == 0

    def ch(c):
        return nc - 1 - c if rev else c

    chunk_grp = [pl.BlockSpec((q, xw), lambda g, c: (ch(c), g)),
                 pl.BlockSpec((q, nw), lambda g, c: (ch(c), b_off + g)),
                 pl.BlockSpec((q, nw), lambda g, c: (ch(c), c_off + g))]
    col_form = pl.BlockSpec((SSD_GPS, q, LANES), lambda g, c: (g, ch(c), 0))
    row_form = pl.BlockSpec((SSD_GPS, 8, q), lambda g, c: (g, 0, ch(c)))
    col_par = pl.BlockSpec((SSD_GPS, 1, LANES), lambda g, c: (g, 0, 0))
    y_spec = pl.BlockSpec((q, xw), lambda g, c: (ch(c), g))
    st_spec = pl.BlockSpec((SSD_GPS, None, GW, SSD_D_STATE), lambda g, c: (g, ch(c), 0, 0))
    bc_spec = pl.BlockSpec((q, nw), lambda g, c: (ch(c), g))
    return chunk_grp, col_form, row_form, col_par, y_spec, st_spec, bc_spec


def _ssd_group_views(gi, wide, narrow, stacked):
    xs, ns = pl.ds(gi * GW, GW), pl.ds(gi * SSD_D_STATE, SSD_D_STATE)
    return [r.at[:, xs] for r in wide], [r.at[:, ns] for r in narrow], [r.at[gi] for r in stacked]


def _ssd_fwd(pre, dt_c, cum_c, cum_r, alog_c, dsk_c):
    t = pre.shape[0]
    ng = pre.shape[1] // GC
    q = SSD_CHUNK
    nc = t // q
    chunk_grp, col_form, row_form, col_par, y_spec, st_spec, _ = _ssd_specs(nc, False, ng)

    def body(px_ref, pb_ref, pc_ref, dt_ref, cum_ref, cumr_ref, ac_ref, dk_ref, y_ref, sp_ref, st_ref):
        @pl.when(pl.program_id(1) == 0)
        def _():
            st_ref[...] = jnp.zeros_like(st_ref)

        for gi in range(SSD_GPS):
            (px, y), (pb, pc), rest = _ssd_group_views(
                gi, (px_ref, y_ref), (pb_ref, pc_ref), (dt_ref, cum_ref, cumr_ref, ac_ref, dk_ref, sp_ref, st_ref))
            one_group(px, pb, pc, *rest[:5], y, *rest[5:])

    def one_group(px_ref, pb_ref, pc_ref, dt_ref, cum_ref, cumr_ref, ac_ref, dk_ref, y_ref, sp_ref, st_ref):
        pre_v = jnp.concatenate([px_ref[...], pb_ref[...], pc_ref[...]], axis=1)
        v = _ssd_common(pre_v, dt_ref[...], cum_ref[...], cumr_ref[...], ac_ref[...])
        s0 = st_ref[...]
        sp_ref[...] = s0
        r = _dot_nt(v["cm"], s0.astype(BF16))
        y = _expand(v["lam_c"], q) * r + _expand(dk_ref[...], 1) * v["xa"]
        head = _head_of_lane((q, GW), SSD_HEAD_DIM)
        for j in range(SSD_HPG):
            diff = v["cum"][:, j:j + 1] - v["cum_r"][j:j + 1, :]
            w = (v["g"] * jnp.exp(jnp.where(v["tril"], diff, -jnp.inf))).astype(BF16)
            y = y + _dot(w, jnp.where(head == j, v["xdt"], 0.0).astype(BF16))
        y_ref[...] = y
        ds = _dot_tn((v["xdt"] * _expand(v["e_c"], q)).astype(BF16), v["bm"])
        for j in range(SSD_HPG):
            rows = slice(j * SSD_HEAD_DIM, (j + 1) * SSD_HEAD_DIM)
            st_ref[rows, :] = s0[rows, :] * jnp.exp(v["cum_r"][j:j + 1, q - 1:q]) + ds[rows, :]

    return pl.pallas_call(
        body, name="ssd_scan_fwd", grid=(ng // SSD_GPS, nc),
        in_specs=chunk_grp + [col_form, col_form, row_form, col_par, col_par],
        out_specs=[y_spec, st_spec],
        out_shape=[jax.ShapeDtypeStruct((t, ng * GW), F32), jax.ShapeDtypeStruct((ng, nc, GW, SSD_D_STATE), F32)],
        scratch_shapes=[pltpu.VMEM((SSD_GPS, GW, SSD_D_STATE), F32)],
        compiler_params=_params("parallel", "arbitrary"))(pre, pre, pre, dt_c, cum_c, cum_r, alog_c, dsk_c)


def _ssd_bwd(dy, pre, states, dt_c, cum_c, cum_r, sgd_c, alog_c, dsk_c):
    t = pre.shape[0]
    ng = pre.shape[1] // GC
    q = SSD_CHUNK
    nc = t // q
    chunk_grp, col_form, row_form, col_par, y_spec, st_spec, bc_spec = _ssd_specs(nc, True, ng)

    def body(dy_ref, px_ref, pb_ref, pc_ref, sp_ref, dt_ref, cum_ref, cumr_ref, sgd_ref, ac_ref, dk_ref,
             dpx_ref, dpb_ref, dpc_ref, ddt_ref, dbias_ref, dalog_ref, dd_ref, ds_ref):
        @pl.when(pl.program_id(1) == 0)
        def _():
            ds_ref[...] = jnp.zeros_like(ds_ref)

        for gi in range(SSD_GPS):
            (dy, px, dpx), (pb, pc, dpb, dpc), rest = _ssd_group_views(
                gi, (dy_ref, px_ref, dpx_ref), (pb_ref, pc_ref, dpb_ref, dpc_ref),
                (sp_ref, dt_ref, cum_ref, cumr_ref, sgd_ref, ac_ref, dk_ref, ddt_ref, dbias_ref, dalog_ref, dd_ref,
                 ds_ref))
            one_group(dy, px, pb, pc, *rest[:7], dpx, dpb, dpc, *rest[7:])

    def one_group(dy_ref, px_ref, pb_ref, pc_ref, sp_ref, dt_ref, cum_ref, cumr_ref, sgd_ref, ac_ref, dk_ref,
                  dpx_ref, dpb_ref, dpc_ref, ddt_ref, dbias_ref, dalog_ref, dd_ref, ds_ref):
        first = pl.program_id(1) == 0
        pre_v = jnp.concatenate([px_ref[...], pb_ref[...], pc_ref[...]], axis=1)
        v = _ssd_common(pre_v, dt_ref[...], cum_ref[...], cumr_ref[...], ac_ref[...])
        xa, bm, cm, xdt, cum, cum_r = v["xa"], v["bm"], v["cm"], v["xdt"], v["cum"], v["cum_r"]
        xdt_b = xdt.astype(BF16)
        dy_v = dy_ref[...]
        s0 = sp_ref[...]
        ds1 = ds_ref[...]
        s0b, ds1b = s0.astype(BF16), ds1.astype(BF16)
        head = _head_of_lane((q, GW), SSD_HEAD_DIM)
        lane = lax.broadcasted_iota(jnp.int32, (q, LANES), 1)
        lane1 = lax.broadcasted_iota(jnp.int32, (1, LANES), 1)
        lam_x = _expand(v["lam_c"], q)
        e_x = _expand(v["e_c"], q)

        dxa = _expand(dk_ref[...], 1) * dy_v
        dd = _contract(jnp.sum(dy_v * xa, axis=0, keepdims=True), 1)
        r = _dot_nt(cm, s0b)
        dcum = _contract(dy_v * r * lam_x, q)
        drb = (lam_x * dy_v).astype(BF16)
        dc = _dot(drb, s0b)
        ds0 = _dot_tn(drb, cm)
        extra = jnp.zeros((1, LANES), F32)
        for j in range(SSD_HPG):
            rows = slice(j * SSD_HEAD_DIM, (j + 1) * SSD_HEAD_DIM)
            lam_last = jnp.exp(cum_r[j:j + 1, q - 1:q])
            ds_ref[rows, :] = ds0[rows, :] + lam_last * ds1[rows, :]
            tot = jnp.sum(jnp.sum(ds1[rows, :] * s0[rows, :], axis=1, keepdims=True), axis=0, keepdims=True)
            extra = jnp.where(lane1 == j, lam_last * tot, extra)
        dv = _dot_nt(bm, ds1b)
        db = _dot((xdt * e_x).astype(BF16), ds1b)
        dxdt = e_x * dv
        dee = _contract(dv * xdt, q) * v["e_c"]
        dcum = dcum - dee
        extra = extra + jnp.sum(dee, axis=0, keepdims=True)
        dg = jnp.zeros((q, q), F32)
        for j in range(SSD_HPG):
            diff = cum[:, j:j + 1] - cum_r[j:j + 1, :]
            el = jnp.exp(jnp.where(v["tril"], diff, -jnp.inf))
            gl = v["g"] * el
            dym = jnp.where(head == j, dy_v, 0.0).astype(BF16)
            dwm = _dot_nt(dym, xdt_b)
            dxdt = dxdt + _dot_tn(gl.astype(BF16), dym)
            z = dwm * gl
            rk = jnp.sum(z, axis=1, keepdims=True) - jnp.sum(z.T, axis=1, keepdims=True)
            dcum = jnp.where(lane == j, dcum + rk, dcum)
            dg = dg + dwm * el
        dgb = dg.astype(BF16)
        dc = dc + _dot(dgb, bm)
        db = db + _dot_tn(dgb, cm)
        da = _dot_f32((v["row"] <= v["col"]).astype(F32), dcum) + extra
        ddt = _contract(dxdt * xa, q) + v["a_c"] * da
        dalog = jnp.sum(v["dt"] * da, axis=0, keepdims=True) * v["a_c"]
        dxa = dxa + v["dt_x"] * dxdt
        ddt_raw = jnp.where(lane < SSD_HPG, ddt * sgd_ref[...], 0.0)
        sgrad = _silu_grad(pre_v, v["sg"])
        dpx_ref[...] = dxa * sgrad[:, :GW]
        dpb_ref[...] = db * sgrad[:, GW:GW + SSD_D_STATE]
        dpc_ref[...] = dc * sgrad[:, GW + SSD_D_STATE:]
        ddt_ref[...] = ddt_raw
        _acc(dbias_ref, jnp.sum(ddt_raw, axis=0, keepdims=True), first)
        _acc(dalog_ref, jnp.where(lane1 < SSD_HPG, dalog, 0.0), first)
        _acc(dd_ref, dd, first)

    return pl.pallas_call(
        body, name="ssd_scan_bwd", grid=(ng // SSD_GPS, nc),
        in_specs=[y_spec] + chunk_grp + [st_spec, col_form, col_form, row_form, col_form, col_par, col_par],
        out_specs=[y_spec, bc_spec, bc_spec, col_form, col_par, col_par, col_par],
        out_shape=[jax.ShapeDtypeStruct((t, ng * GW), F32), jax.ShapeDtypeStruct((t, ng * SSD_D_STATE), F32),
                   jax.ShapeDtypeStruct((t, ng * SSD_D_STATE), F32), jax.ShapeDtypeStruct((ng, t, LANES), F32),
                   jax.ShapeDtypeStruct((ng, 1, LANES), F32), jax.ShapeDtypeStruct((ng, 1, LANES), F32),
                   jax.ShapeDtypeStruct((ng, 1, LANES), F32)],
        scratch_shapes=[pltpu.VMEM((SSD_GPS, GW, SSD_D_STATE), F32)],
        compiler_params=_params("parallel", "arbitrary"))(dy, pre, pre, pre, states, dt_c, cum_c, cum_r, sgd_c, alog_c,
                                                           dsk_c)


def _gate_norm_fwd(y, zx, norm_w):
    t, di = y.shape
    tr = _tile(t, 256, 8)
    ng = di // GW

    def body(y_ref, z_ref, w_ref, o_ref):
        z = z_ref[...]
        gate = y_ref[...] * (z * _sigmoid(z))
        w = w_ref[...]
        for g in range(ng):
            cols = slice(g * GW, (g + 1) * GW)
            gs = gate[:, cols]
            r = lax.rsqrt(jnp.mean(gs * gs, axis=-1, keepdims=True) + NORM_EPS)
            o_ref[:, cols] = (gs * r * w[:, cols]).astype(BF16)

    row = pl.BlockSpec((tr, di), lambda i: (i, 0))
    return pl.pallas_call(body, name="ssd_gate_norm_fwd", grid=(t // tr,),
                          in_specs=[row, row, pl.BlockSpec((1, di), lambda i: (0, 0))], out_specs=row,
                          out_shape=jax.ShapeDtypeStruct((t, di), BF16), compiler_params=_params("parallel"))(
                              y, zx, norm_w)


def _gate_norm_bwd(dyn, y, zx, norm_w, after):
    t, di = y.shape
    tr = _tile(t, 256, 8)
    ng = di // GW

    def body(d_ref, y_ref, z_ref, w_ref, after_ref, dy_ref, dz_ref, dw_ref):
        z = z_ref[...]
        yv = y_ref[...]
        sg = _sigmoid(z)
        sz = z * sg
        gate = yv * sz
        w = w_ref[...]
        d = d_ref[...]
        dsz = _silu_grad(z, sg)
        dws = []
        for g in range(ng):
            cols = slice(g * GW, (g + 1) * GW)
            dg, dwr = _rms_bwd(gate[:, cols], w[:, cols], d[:, cols])
            dy_ref[:, cols] = dg * sz[:, cols]
            dz_ref[:, cols] = (dg * yv[:, cols] * dsz[:, cols]).astype(BF16)
            dws.append(jnp.sum(dwr, axis=0, keepdims=True))
        first = pl.program_id(0) == 0
        for g in range(ng):
            cols = slice(g * GW, (g + 1) * GW)

            @pl.when(first)
            def _():
                dw_ref[:, cols] = dws[g]

            @pl.when(jnp.logical_not(first))
            def _():
                dw_ref[:, cols] += dws[g]

    row = pl.BlockSpec((tr, di), lambda i: (i, 0))
    vec = pl.BlockSpec((1, di), lambda i: (0, 0))
    return pl.pallas_call(body, name="ssd_gate_norm_bwd", grid=(t // tr,),
                          in_specs=[row, row, row, vec, pl.BlockSpec((8, LANES), lambda i: (0, 0))],
                          out_specs=[row, row, vec],
                          out_shape=[jax.ShapeDtypeStruct((t, di), F32), jax.ShapeDtypeStruct((t, di), BF16),
                                     jax.ShapeDtypeStruct((1, di), F32)],
                          compiler_params=_params("arbitrary"))(dyn, y, zx, norm_w, after)


def _attn_mask(n):
    w = ATTN_WINDOW
    qpos = lax.broadcasted_iota(jnp.int32, (w, 2 * w), 0) + w
    kpos = lax.broadcasted_iota(jnp.int32, (w, 2 * w), 1)
    rel = qpos - kpos
    return (rel >= 0) & (rel < w) & jnp.logical_not((n == 0) & (kpos < w))


def _attn_probs(qh, kbh, mask, sink):
    s = _dot_nt(qh, kbh) * (ATTN_HEAD_DIM ** -0.5)
    s = jnp.where(mask, s, -jnp.inf)
    m = jnp.maximum(jnp.max(s, axis=-1, keepdims=True), sink)
    e = jnp.exp(s - m)
    es = jnp.exp(sink - m)
    inv = 1.0 / (jnp.sum(e, axis=-1, keepdims=True) + es)
    return e * inv, es * inv


def _attn_fwd(qkv, sinks):
    t = qkv.shape[0]
    w, hd = ATTN_WINDOW, ATTN_HEAD_DIM
    kd = ATTN_N_KV * hd
    qd = ATTN_REP * kd
    nb = t // w

    def body(q_ref, kc_ref, vc_ref, kp_ref, vp_ref, s_ref, o_ref):
        n = pl.program_id(0)
        mask = _attn_mask(n)
        q = q_ref[...]
        kb = jnp.concatenate([kp_ref[...], kc_ref[...]], axis=0)
        vb = jnp.concatenate([vp_ref[...], vc_ref[...]], axis=0)
        sk = s_ref[...]
        for kv in range(ATTN_N_KV):
            kbh = kb[:, kv * hd:(kv + 1) * hd]
            vbh = vb[:, kv * hd:(kv + 1) * hd]
            for rep in range(ATTN_REP):
                h = kv * ATTN_REP + rep
                p, _ = _attn_probs(q[:, h * hd:(h + 1) * hd], kbh, mask, sk[:, h:h + 1])
                o_ref[:, h * hd:(h + 1) * hd] = _dot(p.astype(BF16), vbh).astype(BF16)

    prev = lambda n: jnp.maximum(n - 1, 0)
    return pl.pallas_call(
        body, name="attn_fwd", grid=(nb,),
        in_specs=[pl.BlockSpec((w, qd), lambda n: (n, 0)),
                  pl.BlockSpec((w, kd), lambda n: (n, ATTN_REP)),
                  pl.BlockSpec((w, kd), lambda n: (n, ATTN_REP + 1)),
                  pl.BlockSpec((w, kd), lambda n: (prev(n), ATTN_REP)),
                  pl.BlockSpec((w, kd), lambda n: (prev(n), ATTN_REP + 1)),
                  pl.BlockSpec((1, sinks.shape[1]), lambda n: (0, 0))],
        out_specs=pl.BlockSpec((w, qd), lambda n: (n, 0)),
        out_shape=jax.ShapeDtypeStruct((t, qd), BF16),
        compiler_params=_params("parallel"))(qkv, qkv, qkv, qkv, qkv, sinks)


def _attn_bwd(qkv, do, sinks):
    t = qkv.shape[0]
    w, hd = ATTN_WINDOW, ATTN_HEAD_DIM
    kd = ATTN_N_KV * hd
    qd = ATTN_REP * kd
    nq = ATTN_N_KV * ATTN_REP
    nb = t // w

    def body(q_ref, kc_ref, vc_ref, kp_ref, vp_ref, do_ref, s_ref,
             dq_ref, dk_ref, dv_ref, bq_ref, bk_ref, bv_ref, dsk_ref, ck_ref, cv_ref):
        n = pl.program_id(0)
        first = n == 0

        @pl.when(first)
        def _():
            ck_ref[...] = jnp.zeros_like(ck_ref)
            cv_ref[...] = jnp.zeros_like(cv_ref)
            bq_ref[...] = jnp.zeros_like(bq_ref)
            bk_ref[...] = jnp.zeros_like(bk_ref)
            bv_ref[...] = jnp.zeros_like(bv_ref)
            dsk_ref[...] = jnp.zeros_like(dsk_ref)

        @pl.when(n < nb)
        def _():
            mask = _attn_mask(n)
            q = q_ref[...]
            dov = do_ref[...]
            kb = jnp.concatenate([kp_ref[...], kc_ref[...]], axis=0)
            vb = jnp.concatenate([vp_ref[...], vc_ref[...]], axis=0)
            sk = s_ref[...]
            lane = lax.broadcasted_iota(jnp.int32, (1, nq), 1)
            dsk = jnp.zeros((1, nq), F32)
            dq_parts, dk_parts, dv_parts = [], [], []
            for kv in range(ATTN_N_KV):
                kbh = kb[:, kv * hd:(kv + 1) * hd]
                vbh = vb[:, kv * hd:(kv + 1) * hd]
                dkh = jnp.zeros((2 * w, hd), F32)
                dvh = jnp.zeros((2 * w, hd), F32)
                for rep in range(ATTN_REP):
                    h = kv * ATTN_REP + rep
                    qh = q[:, h * hd:(h + 1) * hd]
                    doh = dov[:, h * hd:(h + 1) * hd]
                    p, ps = _attn_probs(qh, kbh, mask, sk[:, h:h + 1])
                    pb = p.astype(BF16)
                    dp = _dot_nt(doh, vbh)
                    delta = jnp.sum(p * dp, axis=-1, keepdims=True)
                    dsc = (p * (dp - delta) * (hd ** -0.5)).astype(BF16)
                    dq_parts.append(_dot(dsc, kbh))
                    dkh = dkh + _dot_tn(dsc, qh)
                    dvh = dvh + _dot_tn(pb, doh)
                    dsk = jnp.where(lane == h, -jnp.sum(ps * delta, axis=0, keepdims=True), dsk)
                dk_parts.append(dkh)
                dv_parts.append(dvh)
            dq = jnp.concatenate(dq_parts, axis=1)
            dkb = jnp.concatenate(dk_parts, axis=1)
            dvb = jnp.concatenate(dv_parts, axis=1)
            dq_ref[...] = dq.astype(BF16)
            bq_ref[...] += jnp.sum(dq, axis=0, keepdims=True)
            dsk_ref[...] += dsk
            dk_prev = ck_ref[...] + dkb[:w, :]
            dv_prev = cv_ref[...] + dvb[:w, :]
            dk_ref[...] = dk_prev.astype(BF16)
            dv_ref[...] = dv_prev.astype(BF16)

            @pl.when(n > 0)
            def _():
                bk_ref[...] += jnp.sum(dk_prev, axis=0, keepdims=True)
                bv_ref[...] += jnp.sum(dv_prev, axis=0, keepdims=True)

            ck_ref[...] = dkb[w:, :]
            cv_ref[...] = dvb[w:, :]

        @pl.when(n == nb)
        def _():
            dk_ref[...] = ck_ref[...].astype(BF16)
            dv_ref[...] = cv_ref[...].astype(BF16)
            bk_ref[...] += jnp.sum(ck_ref[...], axis=0, keepdims=True)
            bv_ref[...] += jnp.sum(cv_ref[...], axis=0, keepdims=True)

    cur = lambda n: jnp.minimum(n, nb - 1)
    prev = lambda n: jnp.maximum(jnp.minimum(n, nb - 1) - 1, 0)
    late = lambda n: jnp.maximum(n - 1, 0)
    vec = lambda width: pl.BlockSpec((1, width), lambda n: (0, 0))
    return pl.pallas_call(
        body, name="attn_bwd", grid=(nb + 1,),
        in_specs=[pl.BlockSpec((w, qd), lambda n: (cur(n), 0)),
                  pl.BlockSpec((w, kd), lambda n: (cur(n), ATTN_REP)),
                  pl.BlockSpec((w, kd), lambda n: (cur(n), ATTN_REP + 1)),
                  pl.BlockSpec((w, kd), lambda n: (prev(n), ATTN_REP)),
                  pl.BlockSpec((w, kd), lambda n: (prev(n), ATTN_REP + 1)),
                  pl.BlockSpec((w, qd), lambda n: (cur(n), 0)),
                  vec(nq)],
        out_specs=[pl.BlockSpec((w, qd), lambda n: (cur(n), 0)),
                   pl.BlockSpec((w, kd), lambda n: (late(n), 0)),
                   pl.BlockSpec((w, kd), lambda n: (late(n), 0)),
                   vec(qd), vec(kd), vec(kd), vec(nq)],
        out_shape=[jax.ShapeDtypeStruct((t, qd), BF16), jax.ShapeDtypeStruct((t, kd), BF16),
                   jax.ShapeDtypeStruct((t, kd), BF16), jax.ShapeDtypeStruct((1, qd), F32),
                   jax.ShapeDtypeStruct((1, kd), F32), jax.ShapeDtypeStruct((1, kd), F32),
                   jax.ShapeDtypeStruct((1, nq), F32)],
        scratch_shapes=[pltpu.VMEM((w, kd), F32), pltpu.VMEM((w, kd), F32)],
        compiler_params=_params("arbitrary"))(qkv, qkv, qkv, qkv, qkv, do, sinks)


def _attn_mask_t(n):
    w = ATTN_WINDOW
    kpos = lax.broadcasted_iota(jnp.int32, (2 * w, ATTN_REP * w), 0)
    qpos = lax.broadcasted_iota(jnp.int32, (2 * w, ATTN_REP * w), 1) % w + w
    rel = qpos - kpos
    return (rel >= 0) & (rel < w) & jnp.logical_not((n == 0) & (kpos < w))


def _attn_probs_t(qts, ktb, mask, sink):
    s = _dot_tn(ktb, qts) * (ATTN_HEAD_DIM ** -0.5)
    s = jnp.where(mask, s, -jnp.inf)
    m = jnp.maximum(jnp.max(s, axis=0, keepdims=True), sink)
    e = jnp.exp(s - m)
    es = jnp.exp(sink - m)
    inv = 1.0 / (jnp.sum(e, axis=0, keepdims=True) + es)
    return e * inv, es * inv


def _attn_blocks_t(kv, q_ref, kc_ref, vc_ref, kp_ref, vp_ref):
    hd = ATTN_HEAD_DIM
    rows = slice(kv * hd, (kv + 1) * hd)
    ktb = jnp.concatenate([kp_ref[rows, :], kc_ref[rows, :]], axis=1)
    vtb = jnp.concatenate([vp_ref[rows, :], vc_ref[rows, :]], axis=1)
    qts = jnp.concatenate([q_ref[(kv * ATTN_REP + r) * hd:(kv * ATTN_REP + r + 1) * hd, :]
                           for r in range(ATTN_REP)], axis=1)
    return qts, ktb, vtb


def _attn_specs_t(nb, cur, prev):
    w, hd = ATTN_WINDOW, ATTN_HEAD_DIM
    kd = ATTN_N_KV * hd
    qd = ATTN_REP * kd
    return [pl.BlockSpec((qd, w), lambda n: (0, cur(n))),
            pl.BlockSpec((kd, w), lambda n: (ATTN_REP, cur(n))),
            pl.BlockSpec((kd, w), lambda n: (ATTN_REP + 1, cur(n))),
            pl.BlockSpec((kd, w), lambda n: (ATTN_REP, prev(n))),
            pl.BlockSpec((kd, w), lambda n: (ATTN_REP + 1, prev(n)))]


def _attn_fwd_t(qkv_t, sinks_rep):
    t = qkv_t.shape[1]
    w, hd = ATTN_WINDOW, ATTN_HEAD_DIM
    qd = ATTN_N_KV * ATTN_REP * hd
    nb = t // w

    def body(q_ref, kc_ref, vc_ref, kp_ref, vp_ref, s_ref, o_ref):
        mask = _attn_mask_t(pl.program_id(0))
        for kv in range(ATTN_N_KV):
            qts, ktb, vtb = _attn_blocks_t(kv, q_ref, kc_ref, vc_ref, kp_ref, vp_ref)
            p, _ = _attn_probs_t(qts, ktb, mask, s_ref[kv])
            ots = _dot(vtb, p.astype(BF16))
            for r in range(ATTN_REP):
                h = kv * ATTN_REP + r
                o_ref[h * hd:(h + 1) * hd, :] = ots[:, r * w:(r + 1) * w].astype(BF16)

    return pl.pallas_call(
        body, name="attn_fwd", grid=(nb,),
        in_specs=_attn_specs_t(nb, lambda n: n, lambda n: jnp.maximum(n - 1, 0)) + [
            pl.BlockSpec(sinks_rep.shape, lambda n: (0, 0, 0))],
        out_specs=pl.BlockSpec((qd, w), lambda n: (0, n)),
        out_shape=jax.ShapeDtypeStruct((qd, t), BF16),
        compiler_params=_params("parallel"))(qkv_t, qkv_t, qkv_t, qkv_t, qkv_t, sinks_rep)


def _attn_bwd_t(qkv_t, do_t, sinks_rep):
    t = qkv_t.shape[1]
    w, hd = ATTN_WINDOW, ATTN_HEAD_DIM
    kd = ATTN_N_KV * hd
    qd = ATTN_REP * kd
    nq = ATTN_N_KV * ATTN_REP
    nb = t // w
    rows_all = qd + 2 * kd

    def body(q_ref, kc_ref, vc_ref, kp_ref, vp_ref, do_ref, s_ref, dqkv_ref, bsum_ref, dsk_ref,
             carry_ref, new_ref, bacc_ref, sacc_ref):
        n = pl.program_id(0)

        @pl.when(n == 0)
        def _():
            carry_ref[...] = jnp.zeros_like(carry_ref)
            bacc_ref[...] = jnp.zeros_like(bacc_ref)
            sacc_ref[...] = jnp.zeros_like(sacc_ref)

        @pl.when(n < nb)
        def _():
            mask = _attn_mask_t(n)
            for kv in range(ATTN_N_KV):
                qts, ktb, vtb = _attn_blocks_t(kv, q_ref, kc_ref, vc_ref, kp_ref, vp_ref)
                dots = jnp.concatenate([do_ref[(kv * ATTN_REP + r) * hd:(kv * ATTN_REP + r + 1) * hd, :]
                                        for r in range(ATTN_REP)], axis=1)
                p, ps = _attn_probs_t(qts, ktb, mask, s_ref[kv])
                dpt = _dot_tn(vtb, dots)
                delta = jnp.sum(p * dpt, axis=0, keepdims=True)
                dst = (p * (dpt - delta) * (hd ** -0.5)).astype(BF16)
                dqts = _dot(ktb, dst)
                for r in range(ATTN_REP):
                    h = kv * ATTN_REP + r
                    new_ref[h * hd:(h + 1) * hd, :] = dqts[:, r * w:(r + 1) * w]
                dktb = _dot_nt(qts, dst)
                dvtb = _dot_nt(dots, p.astype(BF16))
                krows = slice(qd + kv * hd, qd + (kv + 1) * hd)
                vrows = slice(qd + kd + kv * hd, qd + kd + (kv + 1) * hd)
                carry_ref[krows, :] += dktb[:, :w]
                carry_ref[vrows, :] += dvtb[:, :w]
                new_ref[krows, :] = dktb[:, w:]
                new_ref[vrows, :] = dvtb[:, w:]
                sacc_ref[kv] += -(ps * delta)

        @pl.when(n >= 1)
        def _():
            done = carry_ref[...]
            dqkv_ref[...] = done.astype(BF16)
            bacc_ref[...] += done

        @pl.when(n < nb)
        def _():
            carry_ref[...] = new_ref[...]

        @pl.when(n == nb)
        def _():
            bsum_ref[...] = jnp.sum(bacc_ref[...], axis=1, keepdims=True)
            lane = lax.broadcasted_iota(jnp.int32, (1, nq), 1)
            dsk = jnp.zeros((1, nq), F32)
            for kv in range(ATTN_N_KV):
                acc = sacc_ref[kv]
                for r in range(ATTN_REP):
                    tot = jnp.sum(acc[:, r * w:(r + 1) * w], axis=1, keepdims=True)
                    dsk = jnp.where(lane == kv * ATTN_REP + r, tot, dsk)
            dsk_ref[...] = dsk

    cur = lambda n: jnp.minimum(n, nb - 1)
    prev = lambda n: jnp.maximum(jnp.minimum(n, nb - 1) - 1, 0)
    return pl.pallas_call(
        body, name="attn_bwd", grid=(nb + 1,),
        in_specs=_attn_specs_t(nb, cur, prev) + [pl.BlockSpec((qd, w), lambda n: (0, cur(n))),
                                                 pl.BlockSpec(sinks_rep.shape, lambda n: (0, 0, 0))],
        out_specs=[pl.BlockSpec((rows_all, w), lambda n: (0, jnp.maximum(n - 1, 0))),
                   pl.BlockSpec((rows_all, 1), lambda n: (0, 0)),
                   pl.BlockSpec((1, nq), lambda n: (0, 0))],
        out_shape=[jax.ShapeDtypeStruct((rows_all, t), BF16), jax.ShapeDtypeStruct((rows_all, 1), F32),
                   jax.ShapeDtypeStruct((1, nq), F32)],
        scratch_shapes=[pltpu.VMEM((rows_all, w), F32), pltpu.VMEM((rows_all, w), F32),
                        pltpu.VMEM((rows_all, w), F32), pltpu.VMEM(sinks_rep.shape, F32)],
        compiler_params=_params("arbitrary"))(qkv_t, qkv_t, qkv_t, qkv_t, qkv_t, do_t, sinks_rep)


HBM_SPEC = pl.BlockSpec(memory_space=pl.ANY)
HBM_ONLY = pl.BlockSpec(memory_space=pltpu.HBM)


def _comm_call(name, body, ins, out_shapes, n_sems):
    return pl.pallas_call(
        body, name=name, in_specs=[HBM_SPEC] * len(ins), out_specs=[HBM_SPEC] * len(out_shapes),
        out_shape=out_shapes,
        scratch_shapes=[pltpu.SemaphoreType.DMA((s,)) for s in n_sems])(*ins)


def _all_gather(name, shards, after):
    n = len(shards)
    na = len(after)

    def body(*refs):
        x_refs, out_refs = refs[:n], refs[n + na:2 * n + na]
        send_sems, recv_sems, local_sems = refs[2 * n + na:]
        x, y, c = lax.axis_index("x"), lax.axis_index("y"), lax.axis_index("c")
        me, sibling = (x, y, c), (x, y, 1 - c)
        chips = [(1 - x, y), (x, 1 - y), (1 - x, 1 - y)]

        def slot(i, px, py, pc):
            return out_refs[i].at[4 * px + 2 * py + pc]

        def copy(k, i, block, to, src=None):
            return pltpu.make_async_remote_copy(
                src_ref=slot(i, *block) if src is None else src, dst_ref=slot(i, *block),
                send_sem=send_sems.at[k * n + i], recv_sem=recv_sems.at[k * n + i], device_id=to,
                device_id_type=MESH)

        mine = [pltpu.make_async_copy(x_refs[i], slot(i, *me), local_sems.at[i]) for i in range(n)]
        first = []
        for i in range(n):
            mine[i].start()
            first.append(copy(0, i, me, sibling, src=x_refs[i]))
            first += [copy(1 + j, i, me, (*chip, c), src=x_refs[i]) for j, chip in enumerate(chips)]
        for cp in first:
            cp.start()
        passed = []
        for i in range(n):
            for j, chip in enumerate(chips):
                copy(1 + j, i, (*chip, c), me).wait_recv()
                passed.append(copy(4 + j, i, (*chip, c), sibling))
                passed[-1].start()
        for i in range(n):
            copy(0, i, sibling, me).wait_recv()
            for j, chip in enumerate(chips):
                copy(4 + j, i, (*chip, 1 - c), me).wait_recv()
        for cp in first + passed:
            cp.wait_send()
        for cp in mine:
            cp.wait()

    outs = [jax.ShapeDtypeStruct((N_DEV,) + s.shape, s.dtype) for s in shards]
    return _comm_call(name, body, list(shards) + list(after), outs, (7 * n, 7 * n, n))


SEM_SPEC = pl.BlockSpec(memory_space=pltpu.SEMAPHORE)
SPLIT_COPY_EFFECT = pltpu.SideEffectType.DATAFLOW_SIDE_EFFECTING


def _in_hbm(a):
    return pltpu.with_memory_space_constraint(a, pltpu.HBM)


def _split_start(name, body, srcs, lands, n_sems):
    n = len(srcs)
    bufs = [_in_hbm(a) for a in list(srcs) + list(lands)]
    outs = pl.pallas_call(
        body, name=name,
        out_shape=(pltpu.SemaphoreType.DMA((n_sems,)), pltpu.SemaphoreType.DMA((n_sems,)),
                   *[pltpu.HBM(a.shape, a.dtype) for a in bufs], jax.ShapeDtypeStruct((8, LANES), F32)),
        in_specs=[HBM_ONLY] * (2 * n),
        out_specs=(SEM_SPEC, SEM_SPEC, *[HBM_ONLY] * (2 * n), pl.BlockSpec(memory_space=pltpu.VMEM)),
        input_output_aliases={i: 2 + i for i in range(2 * n)},
        compiler_params=pltpu.CompilerParams(has_side_effects=SPLIT_COPY_EFFECT))(*bufs)
    return outs[0], outs[1], list(outs[2:2 + n]), list(outs[2 + n:2 + 2 * n]), outs[-1]


def _split_wait(name, body, send_sems, recv_sems, srcs, lands, after):
    n = len(srcs)
    outs = pl.pallas_call(
        body, name=name,
        out_shape=[pltpu.HBM(a.shape, a.dtype) for a in list(srcs) + list(lands)],
        in_specs=[HBM_ONLY] * (2 * n) + [SEM_SPEC, SEM_SPEC, HBM_SPEC],
        out_specs=[HBM_ONLY] * (2 * n),
        input_output_aliases={i: i for i in range(2 * n)},
        compiler_params=pltpu.CompilerParams(has_side_effects=SPLIT_COPY_EFFECT))(
            *srcs, *lands, send_sems, recv_sems, after)
    return list(outs[:n]), list(outs[n:])


N_PEERS = N_DEV - 1


def _gather_peers():
    x, y, c = lax.axis_index("x"), lax.axis_index("y"), lax.axis_index("c")
    flips = [(fx, fy, fc) for fx in (0, 1) for fy in (0, 1) for fc in (0, 1) if fx or fy or fc]
    return [(1 - x if fx else x, 1 - y if fy else y, 1 - c if fc else c) for fx, fy, fc in flips]


def _block_id(dev):
    return 4 * dev[0] + 2 * dev[1] + dev[2]


def _gather_start(name, shards):
    n = len(shards)

    def body(*refs):
        x_refs, land_refs = refs[:n], refs[n:2 * n]
        send_sems, recv_sems, token = refs[2 * n], refs[2 * n + 1], refs[-1]
        me = (lax.axis_index("x"), lax.axis_index("y"), lax.axis_index("c"))
        for i in range(n):
            for k, peer in enumerate(_gather_peers()):
                pltpu.make_async_remote_copy(
                    src_ref=x_refs[i], dst_ref=land_refs[i].at[_block_id(me)],
                    send_sem=send_sems.at[N_PEERS * i + k], recv_sem=recv_sems.at[N_PEERS * i + k],
                    device_id=peer, device_id_type=MESH).start()
        token[...] = jnp.zeros_like(token)

    lands = [lax.empty((N_DEV,) + s.shape, s.dtype) for s in shards]
    return _split_start(name, body, shards, lands, N_PEERS * n)


def _gather_wait(name, send_sems, recv_sems, first, shards, lands, after):
    n = len(shards)

    def body(*refs):
        x_refs, land_refs = refs[:n], refs[n:2 * n]
        send_sems, recv_sems = refs[2 * n], refs[2 * n + 1]
        for i in range(n):
            for k, peer in enumerate(_gather_peers()):
                cp = pltpu.make_async_remote_copy(
                    src_ref=x_refs[i], dst_ref=land_refs[i].at[_block_id(peer)],
                    send_sem=send_sems.at[N_PEERS * (first + i) + k],
                    recv_sem=recv_sems.at[N_PEERS * (first + i) + k],
                    device_id=peer, device_id_type=MESH)
                cp.wait_send()
                cp.wait_recv()

    return _split_wait(name, body, send_sems, recv_sems, shards, lands, after)


def _gather_forward(name, lands, shards):
    n = len(shards)

    def body(*refs):
        x_refs, out_refs = refs[n:2 * n], refs[2 * n:3 * n]
        send_sems, recv_sems, local_sems = refs[3 * n:]
        x, y, c = lax.axis_index("x"), lax.axis_index("y"), lax.axis_index("c")
        chips = [(1 - x, y), (x, 1 - y), (1 - x, 1 - y)]
        mine = [pltpu.make_async_copy(x_refs[i], out_refs[i].at[_block_id((x, y, c))], local_sems.at[i])
                for i in range(n)]
        passed = [pltpu.make_async_remote_copy(
            src_ref=out_refs[i].at[_block_id((*chip, c))], dst_ref=out_refs[i].at[_block_id((*chip, c))],
            send_sem=send_sems.at[3 * i + j], recv_sem=recv_sems.at[3 * i + j], device_id=(x, y, 1 - c),
            device_id_type=MESH) for i in range(n) for j, chip in enumerate(chips)]
        for cp in mine + passed:
            cp.start()
        for i in range(n):
            for j, chip in enumerate(chips):
                pltpu.make_async_remote_copy(
                    src_ref=out_refs[i].at[_block_id((*chip, c))], dst_ref=out_refs[i].at[_block_id((*chip, 1 - c))],
                    send_sem=send_sems.at[3 * i + j], recv_sem=recv_sems.at[3 * i + j], device_id=(x, y, 1 - c),
                    device_id_type=MESH).wait()
        for cp in mine:
            cp.wait()

    return pl.pallas_call(
        body, name=name, in_specs=[HBM_SPEC] * (2 * n), out_specs=[HBM_SPEC] * n,
        out_shape=[jax.ShapeDtypeStruct(a.shape, a.dtype) for a in lands],
        input_output_aliases={i: i for i in range(n)},
        scratch_shapes=[pltpu.SemaphoreType.DMA((3 * n,)), pltpu.SemaphoreType.DMA((3 * n,)),
                        pltpu.SemaphoreType.DMA((n,))])(*lands, *shards)


def _chip_peers():
    x, y, c = lax.axis_index("x"), lax.axis_index("y"), lax.axis_index("c")
    return [(1 - x, y, c), (x, 1 - y, c), (1 - x, 1 - y, c)]


def _chip_start(name, blocks):
    n = len(blocks)

    def body(*refs):
        p_refs, land_refs = refs[:n], refs[n:2 * n]
        send_sems, recv_sems, token = refs[2 * n], refs[2 * n + 1], refs[-1]
        for i in range(n):
            for j, peer in enumerate(_chip_peers()):
                pltpu.make_async_remote_copy(
                    src_ref=p_refs[i].at[j], dst_ref=land_refs[i].at[j], send_sem=send_sems.at[3 * i + j],
                    recv_sem=recv_sems.at[3 * i + j], device_id=peer, device_id_type=MESH).start()
        token[...] = jnp.zeros_like(token)

    lands = [lax.empty(b.shape, b.dtype) for b in blocks]
    return _split_start(name, body, blocks, lands, 3 * n)


def _chip_wait(name, send_sems, recv_sems, blocks, lands, after):
    n = len(blocks)

    def body(*refs):
        p_refs, land_refs = refs[:n], refs[n:2 * n]
        send_sems, recv_sems = refs[2 * n], refs[2 * n + 1]
        for i in range(n):
            for j, peer in enumerate(_chip_peers()):
                cp = pltpu.make_async_remote_copy(
                    src_ref=p_refs[i].at[j], dst_ref=land_refs[i].at[j], send_sem=send_sems.at[3 * i + j],
                    recv_sem=recv_sems.at[3 * i + j], device_id=peer, device_id_type=MESH)
                cp.wait_send()
                cp.wait_recv()

    return _split_wait(name, body, send_sems, recv_sems, blocks, lands, after)


def _scatter_start(name, blocks):
    n = len(blocks)

    def body(*refs):
        b_refs, land_refs = refs[:n], refs[n:2 * n]
        send_sems, recv_sems, token = refs[2 * n], refs[2 * n + 1], refs[-1]
        me = (lax.axis_index("x"), lax.axis_index("y"), lax.axis_index("c"))
        for i in range(n):
            for k, peer in enumerate(_gather_peers()):
                pltpu.make_async_remote_copy(
                    src_ref=b_refs[i].at[_block_id(peer)], dst_ref=land_refs[i].at[_block_id(me)],
                    send_sem=send_sems.at[N_PEERS * i + k], recv_sem=recv_sems.at[N_PEERS * i + k],
                    device_id=peer, device_id_type=MESH).start()
        token[...] = jnp.zeros_like(token)

    lands = [lax.empty(b.shape, b.dtype) for b in blocks]
    return _split_start(name, body, blocks, lands, N_PEERS * n)


def _scatter_wait(name, send_sems, recv_sems, blocks, lands, after):
    n = len(blocks)

    def body(*refs):
        b_refs, land_refs = refs[:n], refs[n:2 * n]
        send_sems, recv_sems = refs[2 * n], refs[2 * n + 1]
        for i in range(n):
            for k, peer in enumerate(_gather_peers()):
                cp = pltpu.make_async_remote_copy(
                    src_ref=b_refs[i].at[_block_id(peer)], dst_ref=land_refs[i].at[_block_id(peer)],
                    send_sem=send_sems.at[N_PEERS * i + k], recv_sem=recv_sems.at[N_PEERS * i + k],
                    device_id=peer, device_id_type=MESH)
                cp.wait_send()
                cp.wait_recv()

    return _split_wait(name, body, send_sems, recv_sems, blocks, lands, after)


def _pair_exchange(name, blocks):
    n = len(blocks)

    def body(*refs):
        g_refs, out_refs = refs[:n], refs[n:2 * n]
        send_sems, recv_sems = refs[2 * n:]
        x, y, c = lax.axis_index("x"), lax.axis_index("y"), lax.axis_index("c")
        copies = [pltpu.make_async_remote_copy(
            src_ref=g_refs[i].at[2 * k + 1 - c], dst_ref=out_refs[i].at[k], send_sem=send_sems.at[4 * i + k],
            recv_sem=recv_sems.at[4 * i + k], device_id=(x, y, 1 - c), device_id_type=MESH)
            for i in range(n) for k in range(4)]
        for cp in copies:
            cp.start()
        for cp in copies:
            cp.wait()

    outs = [jax.ShapeDtypeStruct((4,) + b.shape[1:], b.dtype) for b in blocks]
    return _comm_call(name, body, blocks, outs, (4 * n, 4 * n))


def _chip_exchange(name, blocks):
    n = len(blocks)

    def body(*refs):
        p_refs, out_refs = refs[:n], refs[n:2 * n]
        send_sems, recv_sems = refs[2 * n:]
        x, y, c = lax.axis_index("x"), lax.axis_index("y"), lax.axis_index("c")
        chips = [(1 - x, y), (x, 1 - y), (1 - x, 1 - y)]
        copies = [pltpu.make_async_remote_copy(
            src_ref=p_refs[i].at[j], dst_ref=out_refs[i].at[j], send_sem=send_sems.at[3 * i + j],
            recv_sem=recv_sems.at[3 * i + j], device_id=(*chip, c), device_id_type=MESH)
            for i in range(n) for j, chip in enumerate(chips)]
        for cp in copies:
            cp.start()
        for cp in copies:
            cp.wait()

    outs = [jax.ShapeDtypeStruct(b.shape, b.dtype) for b in blocks]
    return _comm_call(name, body, blocks, outs, (3 * n, 3 * n))


def _pair_sum(name, blocks, from_sibling, g_idx, r_idx):
    _, r, c_ = blocks.shape
    tr = _tile(r, 512, 16)

    def body(gi_ref, ri_ref, a_ref, b_ref, own_ref, send_ref):
        k = pl.program_id(1)
        s = a_ref[...] + b_ref[...]

        @pl.when(k == 0)
        def _():
            own_ref[...] = s

        @pl.when(k > 0)
        def _():
            send_ref[...] = s.astype(send_ref.dtype)

    return pl.pallas_call(
        body, name=name,
        grid_spec=pltpu.PrefetchScalarGridSpec(
            num_scalar_prefetch=2, grid=(r // tr, 4),
            in_specs=[pl.BlockSpec((None, tr, c_), lambda i, k, gi, ri: (gi[k], i, 0)),
                      pl.BlockSpec((None, tr, c_), lambda i, k, gi, ri: (ri[k], i, 0))],
            out_specs=[pl.BlockSpec((None, tr, c_), lambda i, k, gi, ri: (0, i, 0)),
                       pl.BlockSpec((None, tr, c_), lambda i, k, gi, ri: (jnp.maximum(k - 1, 0), i, 0))]),
        out_shape=[jax.ShapeDtypeStruct((1, r, c_), F32), jax.ShapeDtypeStruct((3, r, c_), PAYLOAD)],
        compiler_params=_params("parallel", "arbitrary"))(g_idx, r_idx, blocks, from_sibling)


def _adamw(w, g, m, v):
    m = ADAM_B1 * m + (1.0 - ADAM_B1) * g
    v = ADAM_B2 * v + (1.0 - ADAM_B2) * (g * g)
    m_hat = m / (1.0 - ADAM_B1 ** ADAM_STEP)
    v_hat = v / (1.0 - ADAM_B2 ** ADAM_STEP)
    delta = -ADAM_LR * (m_hat / (jnp.sqrt(v_hat) + ADAM_EPS) + ADAM_WD * w)
    return delta, m, v


def _adamw_tiles(r, c_):
    tr = _tile(r, 256, 16)
    return (tr, c_) if tr < r or r <= 256 else (r, _tile(c_, 256))


def _sum_parts(part):
    g = part[0].astype(F32)
    for k in range(1, part.shape[0]):
        g = g + part[k].astype(F32)
    return g


def _sum_adamw(name, parts, w, m, v):
    r, c_ = w.shape
    tr, tc = _adamw_tiles(r, c_)

    def body(p_ref, w_ref, m_ref, v_ref, g_ref, d_ref, nm_ref, nv_ref):
        g = _sum_parts(p_ref)
        g_ref[...] = g
        d_ref[...], nm_ref[...], nv_ref[...] = _adamw(w_ref[...], g, m_ref[...], v_ref[...])

    tile = pl.BlockSpec((tr, tc), lambda i, j: (i, j))
    return pl.pallas_call(body, name=name, grid=(r // tr, c_ // tc),
                          in_specs=[pl.BlockSpec((parts.shape[0], tr, tc), lambda i, j: (0, i, j)), tile, tile, tile],
                          out_specs=[tile] * 4, out_shape=[jax.ShapeDtypeStruct((r, c_), F32)] * 4,
                          compiler_params=_params("parallel", "parallel"))(parts, w, m, v)


def _sum_adamw_layers(name, parts, w, m, v):
    n_layers, r, c_ = w.shape
    tr = _tile(r, 256, 16)

    def body(*refs):
        p_refs = refs[:n_layers]
        w_ref, m_ref, v_ref, g_ref, d_ref, nm_ref, nv_ref = refs[n_layers:]
        layer = pl.program_id(0)
        g = _sum_parts(p_refs[0])
        for li in range(1, n_layers):
            g = jnp.where(layer == li, _sum_parts(p_refs[li]), g)
        g_ref[...] = g
        d_ref[...], nm_ref[...], nv_ref[...] = _adamw(w_ref[...], g, m_ref[...], v_ref[...])

    row = pl.BlockSpec((None, tr, c_), lambda l, i: (l, i, 0))
    specs = [pl.BlockSpec((p.shape[0], tr, c_), lambda l, i: (0, i, 0)) for p in parts]
    return pl.pallas_call(body, name=name, grid=(n_layers, r // tr), in_specs=specs + [row, row, row],
                          out_specs=[row] * 4, out_shape=[jax.ShapeDtypeStruct(w.shape, F32)] * 4,
                          compiler_params=_params("parallel", "parallel"))(*parts, w, m, v)


def _pack_rows(flat, n_rows, cols):
    pad = n_rows * cols - flat.shape[-1]
    flat = jnp.pad(flat, [(0, 0)] * (flat.ndim - 1) + [(0, pad)])
    return flat.reshape(flat.shape[:-1] + (n_rows, cols))


def _cols_join(blocks):
    return jnp.concatenate([blocks[d] for d in range(N_DEV)], axis=1)


def _cols_split(full):
    c = full.shape[1] // N_DEV
    return jnp.stack([full[:, d * c:(d + 1) * c] for d in range(N_DEV)])


def _rows_join(blocks):
    return blocks.reshape(N_DEV * blocks.shape[1], blocks.shape[2])


def _rows_split(full):
    return full.reshape(N_DEV, full.shape[0] // N_DEV, full.shape[1])


def _perm_xbc(a, ng):
    lead = a.shape[:-1]
    di, gn = ng * GW, ng * SSD_D_STATE
    xs = a[..., :di].reshape(lead + (ng, GW))
    bs = a[..., di:di + gn].reshape(lead + (ng, SSD_D_STATE))
    cs = a[..., di + gn:].reshape(lead + (ng, SSD_D_STATE))
    return jnp.concatenate([xs, bs, cs], axis=-1).reshape(lead + (ng * GC,))


def _unperm_xbc(a, ng):
    lead = a.shape[:-1]
    g = a.reshape(lead + (ng, GC))
    return jnp.concatenate([g[..., :GW].reshape(lead + (ng * GW,)),
                            g[..., GW:GW + SSD_D_STATE].reshape(lead + (ng * SSD_D_STATE,)),
                            g[..., GW + SSD_D_STATE:].reshape(lead + (ng * SSD_D_STATE,))], axis=-1)


def _heads_col(v, ng):
    return jnp.pad(v.reshape(ng, 1, SSD_HPG), ((0, 0), (0, 0), (0, LANES - SSD_HPG)))


def _heads_row(v, ng):
    return jnp.pad(v.reshape(ng, SSD_HPG, 1), ((0, 0), (0, 8 - SSD_HPG), (0, 0)))


MATRIX_ITEMS = ("w_in", "w_out", "up0", "down0", "w_qkv", "w_o", "up1", "down1")
VECTOR_ITEMS = ("conv_w", "b_qkv", "b_o")
ITEMS = MATRIX_ITEMS + VECTOR_ITEMS
GATHER_STAGES = (("w_in", "conv_w"), ("w_out", "up0", "down0"), ("w_qkv", "b_qkv", "w_o", "b_o", "up1", "down1"))


def _items(tree, prefix=""):
    g = lambda k: tree[prefix + k]
    return {"w_in": g("ssd_w_in")[0].T, "w_out": g("ssd_w_out")[0], "w_qkv": g("attn_w_qkv")[0].T,
            "w_o": g("attn_w_o")[0], "up0": g("mlp_w_up")[0], "up1": g("mlp_w_up")[1],
            "down0": g("mlp_w_down")[0], "down1": g("mlp_w_down")[1], "conv_w": g("ssd_conv_w")[0],
            "b_qkv": g("attn_b_qkv"), "b_o": g("attn_b_o")}


def _from_items(it):
    return {"ssd_w_in": it["w_in"][None], "ssd_w_out": it["w_out"][None], "attn_w_qkv": it["w_qkv"].T[None],
            "attn_w_o": it["w_o"][None], "mlp_w_up": jnp.stack([it["up0"], it["up1"]]),
            "mlp_w_down": jnp.stack([it["down0"], it["down1"]]), "ssd_conv_w": it["conv_w"][None],
            "attn_b_qkv": it["b_qkv"], "attn_b_o": it["b_o"]}


REPLICATED = ("ssd_conv_b", "ssd_dt_bias", "ssd_a_log", "ssd_d", "ssd_norm_w", "attn_sinks", "mix_pre_norm",
              "mix_post_norm", "ffn_pre_norm", "ffn_post_norm")
WEIGHTS = ("ssd_w_in", "ssd_conv_w", "ssd_conv_b", "ssd_dt_bias", "ssd_a_log", "ssd_d", "ssd_norm_w", "ssd_w_out",
           "attn_w_qkv", "attn_b_qkv", "attn_sinks", "attn_w_o", "attn_b_o", "mlp_w_up", "mlp_w_down",
           "mix_pre_norm", "mix_post_norm", "ffn_pre_norm", "ffn_post_norm")


def _forward_backward(x, target, rep, token, weights_of_stage, reduce_grads):
    t, d = x.shape
    ng = rep["ssd_norm_w"].shape[1] // GW
    di = ng * GW
    n_xbc = ng * GC
    nh = ng * SSD_HPG
    grads, blocks = {}, {}
    w_up, w_down = [None, None], [None, None]
    sinks_rep = jnp.repeat(rep["attn_sinks"].reshape(ATTN_N_KV, ATTN_REP, 1), ATTN_WINDOW, axis=2).reshape(
        ATTN_N_KV, 1, ATTN_REP * ATTN_WINDOW)
    conv_b = rep["ssd_conv_b"]
    gn = ng * SSD_D_STATE
    parts = ((0, di), (di, di), (2 * di, gn), (2 * di + gn, gn), (di + n_xbc, nh))
    alog_c, dsk_c = (_heads_col(rep[k], ng) for k in ("ssd_a_log", "ssd_d"))
    bias_l, alog_l = (jnp.pad(rep[k], ((0, 0), (0, LANES - nh))) for k in ("ssd_dt_bias", "ssd_a_log"))
    norm = {k: rep[k] for k in ("mix_pre_norm", "mix_post_norm", "ffn_pre_norm", "ffn_post_norm")}

    def nrow(name, i):
        return norm[name][i:i + 1]

    def mlp_fwd(i, u2):
        p = _mm(f"mlp{i}_up", [u2], [w_up[i]], "nn", tm=1024, tn=1024, out_dtypes=(BF16,),
                epilogue=lambda acc: (jnp.square(jnp.maximum(acc, 0.0)),))
        f = _mm(f"mlp{i}_down", [p], [w_down[i]], "nn", tm=512, tn=1024)
        return p, f

    def mlp_bwd(i, df, u2, p):
        da = _mm(f"mlp{i}_dact", [df], [w_down[i]], "nt", tm=1024, tn=1024, out_dtypes=(BF16,),
                 tiles=(p,), epilogue=lambda acc, pv: (acc * (2.0 * jnp.sqrt(pv.astype(F32))),))
        blocks[f"down{i}"] = _rows_split(_mm(f"mlp{i}_dwdown", [p], [df], "tn", tm=512, tn=1024,
                                             out_dtypes=(PAYLOAD,)))
        blocks[f"up{i}"] = _mm(f"mlp{i}_dwup", [u2], [da], "tn", tm=1024, tn=da.shape[1] // N_DEV,
                               out_dtypes=(PAYLOAD,), col_blocks=True)
        return _mm(f"mlp{i}_dx", [da], [w_up[i]], "nt", tm=512, tn=1024)

    u0 = _prenorm("l0_prenorm", x, nrow("mix_pre_norm", 0), token)
    got = weights_of_stage(0, u0)
    w_in_t = _rows_join(got["w_in"])
    w_dt_t = jnp.pad(w_in_t[di + n_xbc:], ((0, LANES - nh), (0, 0)))
    conv_w = _cols_join(got["conv_w"])
    zx = _mm("ssd_in_proj", [u0], [w_in_t], "nt", tm=1024, tn=1024, n_use=di + n_xbc)
    zdt = _mm("ssd_dt_proj", [u0], [w_dt_t], "nt", tm=1024, tn=LANES)
    pre = _conv_fwd(zx, di, n_xbc, conv_w, conv_b)
    dt_c, cum_c, cum_r, sgd_c = _ssd_dt_prep(zdt, bias_l, alog_l, ng)
    y, states = _ssd_fwd(pre, dt_c, cum_c, cum_r, alog_c, dsk_c)
    yn = _gate_norm_fwd(y, zx, rep["ssd_norm_w"])
    got = weights_of_stage(1, yn)
    w_out = _rows_join(got["w_out"])
    w_up[0], w_down[0] = _cols_join(got["up0"]), _rows_join(got["down0"])
    mix0 = _mm("ssd_out_proj", [yn], [w_out], "nn", tm=1024, tn=1024)
    h1, u0f = _post_pre("l0_mid", x, mix0, nrow("mix_post_norm", 0), nrow("ffn_pre_norm", 0))
    p0, f0 = mlp_fwd(0, u0f)
    h2, u1 = _post_pre("l1_in", h1, f0, nrow("ffn_post_norm", 0), nrow("mix_pre_norm", 1))
    got = weights_of_stage(2, u1)
    w_qkv_t = _rows_join(got["w_qkv"])
    w_o = _rows_join(got["w_o"])
    b_qkv_col = got["b_qkv"].reshape(-1, 1)
    b_o = _cols_join(got["b_o"])
    w_up[1], w_down[1] = _cols_join(got["up1"]), _rows_join(got["down1"])
    qkv_t = _mm("attn_qkv_proj", [w_qkv_t], [u1], "nt", tm=768, tn=1024, out_dtypes=(BF16,), cols=(b_qkv_col,),
                epilogue=lambda acc, b: (acc + b,))
    ao_t = _attn_fwd_t(qkv_t, sinks_rep)
    mix1 = _mm("attn_out_proj", [ao_t], [w_o], "tn", tm=1024, tn=1024, rows=(b_o,),
               epilogue=lambda acc, b: (acc + b,))
    h3, u1f = _post_pre("l1_mid", h2, mix1, nrow("mix_post_norm", 1), nrow("ffn_pre_norm", 1))
    p1, f1 = mlp_fwd(1, u1f)
    dh, loss_row = _final_loss("loss", h3, f1, nrow("ffn_post_norm", 1), target)

    g_norm = {k: [None, None] for k in norm}
    df1, g_norm["ffn_post_norm"][1], _ = _norm_bwd("l1_ffn_post_bwd", dh, post=(f1, nrow("ffn_post_norm", 1)))
    du = mlp_bwd(1, df1, u1f, p1)
    sent = reduce_grads("mlp1", {k: blocks[k] for k in ("up1", "down1")})
    dh, g_norm["ffn_pre_norm"][1], dmix1, g_norm["mix_post_norm"][1], db_o = _norm_bwd(
        "l1_mid_bwd", dh, pre=(du, h3, nrow("ffn_pre_norm", 1)), post=(mix1, nrow("mix_post_norm", 1)), after=sent)
    blocks["b_o"] = _cols_split(db_o)
    blocks["w_o"] = _rows_split(_mm("attn_dwo", [ao_t], [dmix1], "nn", tm=512, tn=1024, out_dtypes=(PAYLOAD,)))
    dao_t = _mm("attn_dout", [w_o], [dmix1], "nt", tm=1024, tn=1024, out_dtypes=(BF16,))
    dqkv_t, db_qkv, grads["attn_sinks"] = _attn_bwd_t(qkv_t, dao_t, sinks_rep)
    blocks["b_qkv"] = db_qkv.reshape(N_DEV, 1, -1)
    blocks["w_qkv"] = _rows_split(_mm("attn_dwqkv", [dqkv_t], [u1], "nn", tm=512, tn=1024, out_dtypes=(PAYLOAD,)))
    du = _mm("attn_dx", [dqkv_t], [w_qkv_t], "tn", tm=1024, tn=1024)
    sent = reduce_grads("attn", {k: blocks[k] for k in ("w_o", "w_qkv", "b_o", "b_qkv")})
    dh, g_norm["mix_pre_norm"][1], df0, g_norm["ffn_post_norm"][0], _ = _norm_bwd(
        "l1_in_bwd", dh, pre=(du, h2, nrow("mix_pre_norm", 1)), post=(f0, nrow("ffn_post_norm", 0)), after=sent)
    du = mlp_bwd(0, df0, u0f, p0)
    sent = reduce_grads("mlp0", {k: blocks[k] for k in ("up0", "down0")})
    dh, g_norm["ffn_pre_norm"][0], dmix0, g_norm["mix_post_norm"][0], _ = _norm_bwd(
        "l0_mid_bwd", dh, pre=(du, h1, nrow("ffn_pre_norm", 0)), post=(mix0, nrow("mix_post_norm", 0)), after=sent)
    blocks["w_out"] = _rows_split(_mm("ssd_dwout", [yn], [dmix0], "tn", tm=512, tn=1024, out_dtypes=(PAYLOAD,)))
    dyn = _mm("ssd_dyn", [dmix0], [w_out], "nt", tm=1024, tn=1024)
    sent = reduce_grads("ssdout", {"w_out": blocks["w_out"]})
    dy, dz, grads["ssd_norm_w"] = _gate_norm_bwd(dyn, y, zx, rep["ssd_norm_w"], sent)
    dpx, dpb, dpc, ddt_g, dbias_g, dalog_g, dd_g = _ssd_bwd(dy, pre, states, dt_c, cum_c, cum_r, sgd_c, alog_c,
                                                             dsk_c)
    conv_out = [_conv_bwd(f"ssd_conv_bwd_{tag}", dp, zx, c0, conv_w[:, c0 - di:c0 - di + n])
                for tag, dp, (c0, n) in zip("xbc", (dpx, dpb, dpc), parts[1:4])]
    dconv_w = jnp.concatenate([o[1] for o in conv_out], axis=1)
    dconv_b = jnp.concatenate([o[2] for o in conv_out], axis=1)
    ddt = jnp.transpose(ddt_g[:, :, :SSD_HPG], (1, 0, 2)).reshape(t, nh)
    ddt = jnp.pad(ddt, ((0, 0), (0, LANES - nh))).astype(BF16)
    blocks["conv_w"] = _cols_split(dconv_w)
    grads["ssd_conv_b"] = dconv_b
    for name, val in (("ssd_dt_bias", dbias_g), ("ssd_a_log", dalog_g), ("ssd_d", dd_g)):
        grads[name] = val[:, 0, :SSD_HPG].reshape(1, nh)
    d_zx = [dz] + [o[0] for o in conv_out] + [ddt]
    dw_parts = [_mm(f"ssd_dw_{tag}", [d], [u0], "tn", tm=512, tn=1024, out_dtypes=(PAYLOAD,))
                for tag, d in zip("zxbct", d_zx)]
    dw_parts[-1] = dw_parts[-1][:nh]
    blocks["w_in"] = _rows_split(jnp.concatenate(dw_parts, axis=0))
    sent = reduce_grads("ssd", {k: blocks[k] for k in ("w_in", "conv_w")})
    w_parts = [w_in_t[r0:r0 + n] for r0, n in parts[:-1]] + [w_dt_t]
    du = _mm("ssd_dx", d_zx, w_parts, "nn", tm=256, tn=1024, after=sent)
    grad_x, g_norm["mix_pre_norm"][0] = _norm_bwd("l0_in_bwd", dh, pre=(du, x, nrow("mix_pre_norm", 0)), after=sent)
    for k in norm:
        grads[k] = jnp.concatenate(g_norm[k], axis=0)
    return loss_row, grad_x, grads


def kernel(x, ssd_w_in, ssd_conv_w, ssd_conv_b, ssd_dt_bias, ssd_a_log, ssd_d, ssd_norm_w, ssd_w_out, attn_w_qkv, attn_b_qkv, attn_sinks, attn_w_o, attn_b_o, mlp_w_up, mlp_w_down, mix_pre_norm, mix_post_norm, ffn_pre_norm, ffn_post_norm, loss_target, m_ssd_w_in, m_ssd_conv_w, m_ssd_conv_b, m_ssd_dt_bias, m_ssd_a_log, m_ssd_d, m_ssd_norm_w, m_ssd_w_out, m_attn_w_qkv, m_attn_b_qkv, m_attn_sinks, m_attn_w_o, m_attn_b_o, m_mlp_w_up, m_mlp_w_down, m_mix_pre_norm, m_mix_post_norm, m_ffn_pre_norm, m_ffn_post_norm, v_ssd_w_in, v_ssd_conv_w, v_ssd_conv_b, v_ssd_dt_bias, v_ssd_a_log, v_ssd_d, v_ssd_norm_w, v_ssd_w_out, v_attn_w_qkv, v_attn_b_qkv, v_attn_sinks, v_attn_w_o, v_attn_b_o, v_mlp_w_up, v_mlp_w_down, v_mix_pre_norm, v_mix_post_norm, v_ffn_pre_norm, v_ffn_post_norm):
    given = dict(locals())
    w = {k: given[k] for k in WEIGHTS}
    mom_m = {k: given["m_" + k] for k in WEIGHTS}
    mom_v = {k: given["v_" + k] for k in WEIGHTS}
    w_it, m_it, v_it = _items(given), _items(given, "m_"), _items(given, "v_")

    order = [k for stage in GATHER_STAGES for k in stage]
    shards = [w_it[k].astype(PAYLOAD) if k in MATRIX_ITEMS else w_it[k] for k in order]
    g_send, g_recv, shards, lands, token = _gather_start("gather_start", shards)

    def weights_of_stage(s, after):
        first = sum(len(stage) for stage in GATHER_STAGES[:s])
        sl = slice(first, first + len(GATHER_STAGES[s]))
        srcs, got = _gather_wait(f"gather_wait{s}", g_send, g_recv, first, shards[sl], lands[sl], after)
        me = 4 * ix + 2 * iy + ic
        return {k: lax.dynamic_update_slice(land, src[None], (me,) + (0,) * src.ndim)
                for k, land, src in zip(GATHER_STAGES[s], got, srcs)}

    ix, iy, ic = lax.axis_index("x"), lax.axis_index("y"), lax.axis_index("c")
    in_flight = []

    def reduce_grads(tag, blocks):
        keys = list(blocks)
        started = _scatter_start(f"rs_start_{tag}", [blocks[k] for k in keys])
        in_flight.append((tag, keys, started))
        return started[-1]

    rep = {k: w[k] for k in REPLICATED}
    loss_row, grad_x, grads = _forward_backward(x[0], loss_target[0], rep, token, weights_of_stage, reduce_grads)

    def pack_rep(tree, last):
        flat = jnp.concatenate([tree[k].reshape(-1) for k in REPLICATED] + [last])
        return _pack_rows(flat, _round_up(-(-flat.shape[0] // LANES), 8), LANES)

    landed = {}
    me = 4 * ix + 2 * iy + ic

    def wait_group(group, after):
        tag, keys, (s_send, s_recv, srcs, s_lands, _) = group
        srcs, got = _scatter_wait(f"rs_wait_{tag}", s_send, s_recv, srcs, s_lands, after)
        for k, src, land in zip(keys, srcs, got):
            own = lax.dynamic_index_in_dim(src, me, 0, keepdims=True)
            landed[k] = lax.dynamic_update_slice(land, own, (me,) + (0,) * (land.ndim - 1))

    def adamw_item(k):
        return _sum_adamw(f"adamw_{k}", landed[k], w_it[k], m_it[k], v_it[k])

    def adamw_stack(name, keys):
        return _sum_adamw_layers(f"adamw_{name}", [landed[k] for k in keys], given[name], given["m_" + name],
                                 given["v_" + name])

    for group in in_flight[:-1]:
        wait_group(group, grad_x)
    done = {"mlp_w_up": adamw_stack("mlp_w_up", ("up0", "up1")),
            "mlp_w_down": adamw_stack("mlp_w_down", ("down0", "down1")),
            "attn_w_qkv": [o.T[None] for o in adamw_item("w_qkv")],
            "attn_w_o": [o[None] for o in adamw_item("w_o")],
            "attn_b_qkv": adamw_item("b_qkv"), "attn_b_o": adamw_item("b_o"),
            "ssd_w_out": [o[None] for o in adamw_item("w_out")]}
    partials, = _all_gather("gather_small_grads", [pack_rep(grads, loss_row[0, :1])],
                            [outs4[0] for outs4 in done.values()])
    wait_group(in_flight[-1], partials)
    done["ssd_w_in"] = [o.T[None] for o in adamw_item("w_in")]
    done["ssd_conv_w"] = [o[None] for o in adamw_item("conv_w")]
    zero = jnp.zeros((1,), F32)
    rep_out = _sum_adamw("adamw_replicated", partials, pack_rep(w, zero), pack_rep(mom_m, zero), pack_rep(mom_v, zero))

    kinds = []
    for kind, r_arr in enumerate(rep_out):
        tree = {name: outs4[kind] for name, outs4 in done.items()}
        flat, off = r_arr.reshape(-1), 0
        for k in REPLICATED:
            tree[k] = flat[off:off + w[k].size].reshape(w[k].shape)
            off += w[k].size
        kinds.append(tree)
    loss = rep_out[0].reshape(-1)[off]
    outs = [loss, grad_x[None]]
    for tree in kinds:
        outs += [tree[k] for k in WEIGHTS]
    return tuple(outs)
```

```python
import functools

import jax
import jax.numpy as jnp
from jax import lax
from jax.experimental import pallas as pl
from jax.experimental.pallas import tpu as pltpu

F32 = jnp.float32
BF16 = jnp.bfloat16
PAYLOAD = jnp.bfloat16
HIGHEST = lax.Precision.HIGHEST
MESH = pl.DeviceIdType.MESH

NORM_EPS = 1e-6
SSD_HEAD_DIM = 64
SSD_N_GROUPS = 8
SSD_HPG = 4
SSD_D_STATE = 128
SSD_CONV_WIDTH = 4
SSD_CHUNK = 128
ATTN_HEAD_DIM = 64
ATTN_N_KV = 4
ATTN_REP = 4
ATTN_WINDOW = 128
ADAM_LR = 0.001
ADAM_B1 = 0.9
ADAM_B2 = 0.999
ADAM_EPS = 1e-08
ADAM_WD = 0.01
ADAM_STEP = 10

N_DEV = 8
LANES = 128
PACK_COLS = 1024
V7X_VMEM_LIMIT = 56 * 1024 * 1024

GW = SSD_HPG * SSD_HEAD_DIM
GC = GW + 2 * SSD_D_STATE


def _params(*sem):
    return pltpu.CompilerParams(dimension_semantics=sem, vmem_limit_bytes=V7X_VMEM_LIMIT)


def _tile(n, pref, mult=LANES):
    best = None
    t = mult
    while t <= min(n, pref):
        if n % t == 0:
            best = t
        t += mult
    return best if best is not None else n


def _round_up(n, m):
    return (n + m - 1) // m * m


def _acc(ref, val, first):
    @pl.when(first)
    def _():
        ref[...] = val

    @pl.when(jnp.logical_not(first))
    def _():
        ref[...] += val


def _dot(a, b):
    return lax.dot_general(a, b, (((1,), (0,)), ((), ())), preferred_element_type=F32)


def _dot_nt(a, b):
    return lax.dot_general(a, b, (((1,), (1,)), ((), ())), preferred_element_type=F32)


def _dot_tn(a, b):
    return lax.dot_general(a, b, (((0,), (0,)), ((), ())), preferred_element_type=F32)


def _dot_f32(a, b):
    return lax.dot_general(a, b, (((1,), (0,)), ((), ())), preferred_element_type=F32, precision=HIGHEST)


_DOTS = {"nn": _dot, "nt": _dot_nt, "tn": _dot_tn}


def _sigmoid(x):
    return 1.0 / (1.0 + jnp.exp(-x))


def _softplus(x):
    return jnp.maximum(x, 0.0) + jnp.log1p(jnp.exp(-jnp.abs(x)))


def _silu_grad(x, s):
    return s * (1.0 + x * (1.0 - s))


def _mm(name, a_list, b_list, mode, *, tm, tn, out_dtypes=(F32,), epilogue=None, tiles=(), rows=(), cols=(),
        col_blocks=False, n_use=None, after=None):
    npair = len(a_list)
    if mode == "tn":
        m = a_list[0].shape[1]
    else:
        m = a_list[0].shape[0]
    n = n_use if n_use is not None else (b_list[0].shape[0] if mode == "nt" else b_list[0].shape[1])
    tm = _tile(m, tm, LANES if mode == "tn" else 8)
    tn = _tile(n, tn)
    assert m % tm == 0 and n % tn == 0, (name, m, n, tm, tn)
    dot = _DOTS[mode]

    def body(*refs):
        a_refs = refs[:npair]
        b_refs = refs[npair:2 * npair]
        n_extra = len(tiles) + len(rows) + len(cols)
        e_refs = refs[2 * npair:2 * npair + n_extra]
        o_refs = refs[2 * npair + n_extra + len(order):]
        acc = None
        for ar, br in zip(a_refs, b_refs):
            d = dot(ar[...], br[...])
            acc = d if acc is None else acc + d
        outs = epilogue(acc, *[e[...] for e in e_refs]) if epilogue is not None else (acc,)
        for o, v in zip(o_refs, outs):
            o[...] = v.astype(o.dtype)

    in_specs = []
    for a in a_list:
        if mode == "tn":
            in_specs.append(pl.BlockSpec((a.shape[0], tm), lambda i, j: (0, i)))
        else:
            in_specs.append(pl.BlockSpec((tm, a.shape[1]), lambda i, j: (i, 0)))
    for b in b_list:
        if mode == "nt":
            in_specs.append(pl.BlockSpec((tn, b.shape[1]), lambda i, j: (j, 0)))
        else:
            in_specs.append(pl.BlockSpec((b.shape[0], tn), lambda i, j: (0, j)))
    in_specs += [pl.BlockSpec((tm, tn), lambda i, j: (i, j)) for _ in tiles]
    in_specs += [pl.BlockSpec((1, tn), lambda i, j: (0, j)) for _ in rows]
    in_specs += [pl.BlockSpec((tm, 1), lambda i, j: (i, 0)) for _ in cols]
    order = [] if after is None else [after]
    in_specs += [pl.BlockSpec((8, LANES), lambda i, j: (0, 0)) for _ in order]
    outs = pl.pallas_call(
        body,
        name=name,
        grid=(m // tm, n // tn),
        in_specs=in_specs,
        out_specs=[pl.BlockSpec((None, tm, tn), lambda i, j: (j, i, 0)) if col_blocks else
                   pl.BlockSpec((tm, tn), lambda i, j: (i, j)) for _ in out_dtypes],
        out_shape=[jax.ShapeDtypeStruct((n // tn, m, tn) if col_blocks else (m, n), dt) for dt in out_dtypes],
        compiler_params=_params("parallel", "parallel"),
    )(*a_list, *b_list, *tiles, *rows, *cols, *order)
    return outs[0] if len(out_dtypes) == 1 else outs


def _rms(x, w):
    r = lax.rsqrt(jnp.mean(x * x, axis=-1, keepdims=True) + NORM_EPS)
    return x * r * w


def _rms_bwd(x, w, dy):
    r = lax.rsqrt(jnp.mean(x * x, axis=-1, keepdims=True) + NORM_EPS)
    xh = x * r
    g = dy * w
    dx = r * (g - xh * jnp.mean(g * xh, axis=-1, keepdims=True))
    return dx, dy * xh


def _row_specs(tr, d):
    return pl.BlockSpec((tr, d), lambda i: (i, 0)), pl.BlockSpec((1, d), lambda i: (0, 0))


def _prenorm(name, h, w, after):
    t, d = h.shape
    tr = _tile(t, 512, 8)
    row, vec = _row_specs(tr, d)

    def body(h_ref, w_ref, after_ref, u_ref):
        u_ref[...] = _rms(h_ref[...], w_ref[...]).astype(BF16)

    return pl.pallas_call(body, name=name, grid=(t // tr,),
                          in_specs=[row, vec, pl.BlockSpec((8, LANES), lambda i: (0, 0))], out_specs=row,
                          out_shape=jax.ShapeDtypeStruct((t, d), BF16), compiler_params=_params("parallel"))(
                              h, w, after)


def _post_pre(name, h, m, w_post, w_pre):
    t, d = h.shape
    tr = _tile(t, 512, 8)
    row, vec = _row_specs(tr, d)

    def body(h_ref, m_ref, wq_ref, wp_ref, hn_ref, u_ref):
        hn = h_ref[...] + _rms(m_ref[...], wq_ref[...])
        hn_ref[...] = hn
        u_ref[...] = _rms(hn, wp_ref[...]).astype(BF16)

    return pl.pallas_call(body, name=name, grid=(t // tr,), in_specs=[row, row, vec, vec], out_specs=[row, row],
                          out_shape=[jax.ShapeDtypeStruct((t, d), F32), jax.ShapeDtypeStruct((t, d), BF16)],
                          compiler_params=_params("parallel"))(h, m, w_post, w_pre)


def _final_loss(name, h, m, w_post, target):
    t, d = h.shape
    tr = _tile(t, 512, 8)
    row, vec = _row_specs(tr, d)

    def body(h_ref, m_ref, wq_ref, t_ref, dh_ref, loss_ref):
        err = h_ref[...] + _rms(m_ref[...], wq_ref[...]) - t_ref[...]
        dh_ref[...] = err * (1.0 / d)
        part = 0.5 * jnp.sum(jnp.mean(err * err, axis=-1, keepdims=True), axis=0, keepdims=True)
        _acc(loss_ref, jnp.broadcast_to(part, (1, LANES)), pl.program_id(0) == 0)

    return pl.pallas_call(body, name=name, grid=(t // tr,), in_specs=[row, row, vec, row],
                          out_specs=[row, pl.BlockSpec((1, LANES), lambda i: (0, 0))],
                          out_shape=[jax.ShapeDtypeStruct((t, d), F32), jax.ShapeDtypeStruct((1, LANES), F32)],
                          compiler_params=_params("arbitrary"))(h, m, w_post, target)


def _norm_bwd(name, dh, pre=None, post=None, after=None):
    t, d = dh.shape
    tr = _tile(t, 256, 8)
    row, vec = _row_specs(tr, d)
    has_pre, has_post = pre is not None, post is not None

    def body(*refs):
        it = iter(refs)
        dh_ref = next(it)
        if has_pre:
            du_ref, x_ref, wp_ref = next(it), next(it), next(it)
        if has_post:
            m_ref, wq_ref = next(it), next(it)
        if after is not None:
            next(it)
        first = pl.program_id(0) == 0
        dh_v = dh_ref[...]
        if has_pre:
            dhn_ref, dwp_ref = next(it), next(it)
            dx, dwr = _rms_bwd(x_ref[...], wp_ref[...], du_ref[...])
            dh_v = dh_v + dx
            dhn_ref[...] = dh_v
            _acc(dwp_ref, jnp.sum(dwr, axis=0, keepdims=True), first)
        if has_post:
            dm_ref, dwq_ref, dms_ref = next(it), next(it), next(it)
            dm, dwr = _rms_bwd(m_ref[...], wq_ref[...], dh_v)
            dm_ref[...] = dm.astype(BF16)
            _acc(dwq_ref, jnp.sum(dwr, axis=0, keepdims=True), first)
            _acc(dms_ref, jnp.sum(dm, axis=0, keepdims=True), first)

    ins, in_specs, out_specs, out_shape = [dh], [row], [], []
    if has_pre:
        ins += list(pre)
        in_specs += [row, row, vec]
        out_specs += [row, vec]
        out_shape += [jax.ShapeDtypeStruct((t, d), F32), jax.ShapeDtypeStruct((1, d), F32)]
    if has_post:
        ins += list(post)
        in_specs += [row, vec]
        out_specs += [row, vec, vec]
        out_shape += [jax.ShapeDtypeStruct((t, d), BF16), jax.ShapeDtypeStruct((1, d), F32),
                      jax.ShapeDtypeStruct((1, d), F32)]
    if after is not None:
        ins.append(after)
        in_specs.append(pl.BlockSpec((8, LANES), lambda i: (0, 0)))
    return pl.pallas_call(body, name=name, grid=(t // tr,), in_specs=in_specs, out_specs=out_specs,
                          out_shape=out_shape, compiler_params=_params("arbitrary"))(*ins)


HALO = 8


def _shift_later(cur, prev, s):
    rolled = pltpu.roll(cur, s, 0)
    row = lax.broadcasted_iota(jnp.int32, prev.shape, 0)
    first = jnp.where(row < s, pltpu.roll(prev, s, 0), rolled[0:HALO])
    return jnp.concatenate([first, rolled[HALO:]], axis=0)


def _shift_earlier(cur, nxt, s):
    tt = cur.shape[0]
    rolled = pltpu.roll(cur, tt - s, 0)
    row = lax.broadcasted_iota(jnp.int32, nxt.shape, 0)
    last = jnp.where(row >= HALO - s, pltpu.roll(nxt, HALO - s, 0), rolled[tt - HALO:])
    return jnp.concatenate([rolled[:tt - HALO], last], axis=0)


def _conv_fwd(zx, col0, n_ch, conv_w, conv_b):
    t = zx.shape[0]
    tc = _tile(n_ch, 512)
    tt = _tile(t, 1024, 8)
    cb0 = col0 // tc
    assert col0 % tc == 0
    kw = SSD_CONV_WIDTH

    def body(x_ref, p_ref, w_ref, b_ref, o_ref):
        cur = x_ref[...]
        prev = jnp.where(pl.program_id(1) > 0, p_ref[...], 0.0)
        w = w_ref[...]
        acc = b_ref[...] + w[kw - 1:kw, :] * cur
        for k in range(kw - 1):
            acc = acc + w[k:k + 1, :] * _shift_later(cur, prev, kw - 1 - k)
        o_ref[...] = acc

    return pl.pallas_call(
        body, name="ssd_conv_fwd", grid=(n_ch // tc, t // tt),
        in_specs=[pl.BlockSpec((tt, tc), lambda j, i: (i, cb0 + j)),
                  pl.BlockSpec((HALO, tc), lambda j, i: (jnp.maximum(i * (tt // HALO) - 1, 0), cb0 + j)),
                  pl.BlockSpec((kw, tc), lambda j, i: (0, j)),
                  pl.BlockSpec((1, tc), lambda j, i: (0, j))],
        out_specs=pl.BlockSpec((tt, tc), lambda j, i: (i, j)),
        out_shape=jax.ShapeDtypeStruct((t, n_ch), F32),
        compiler_params=_params("parallel", "parallel"))(zx, zx, conv_w, conv_b)


def _conv_bwd(name, dpre, zx, col0, conv_w):
    t, n_ch = dpre.shape
    tc = _tile(n_ch, 512)
    tt = _tile(t, 1024, 8)
    cb0 = col0 // tc
    kw = SSD_CONV_WIDTH
    nt = t // tt

    def body(d_ref, dn_ref, x_ref, p_ref, w_ref, dx_ref, dw_ref, db_ref):
        i = pl.program_id(1)
        d = d_ref[...]
        d_next = jnp.where(i < nt - 1, dn_ref[...], 0.0)
        x = x_ref[...]
        x_prev = jnp.where(i > 0, p_ref[...], 0.0)
        w = w_ref[...]
        dx = w[kw - 1:kw, :] * d
        for k in range(kw - 1):
            dx = dx + w[k:k + 1, :] * _shift_earlier(d, d_next, kw - 1 - k)
        dx_ref[...] = dx.astype(BF16)
        first = i == 0
        for k in range(kw):
            xs = x if k == kw - 1 else _shift_later(x, x_prev, kw - 1 - k)
            val = jnp.sum(d * xs, axis=0, keepdims=True)

            @pl.when(first)
            def _():
                dw_ref[k:k + 1, :] = val

            @pl.when(jnp.logical_not(first))
            def _():
                dw_ref[k:k + 1, :] += val
        _acc(db_ref, jnp.sum(d, axis=0, keepdims=True), first)

    return pl.pallas_call(
        body, name=name, grid=(n_ch // tc, nt),
        in_specs=[pl.BlockSpec((tt, tc), lambda j, i: (i, j)),
                  pl.BlockSpec((HALO, tc), lambda j, i: (jnp.minimum((i + 1) * (tt // HALO), t // HALO - 1), j)),
                  pl.BlockSpec((tt, tc), lambda j, i: (i, cb0 + j)),
                  pl.BlockSpec((HALO, tc), lambda j, i: (jnp.maximum(i * (tt // HALO) - 1, 0), cb0 + j)),
                  pl.BlockSpec((kw, tc), lambda j, i: (0, j))],
        out_specs=[pl.BlockSpec((tt, tc), lambda j, i: (i, j)),
                   pl.BlockSpec((kw, tc), lambda j, i: (0, j)),
                   pl.BlockSpec((1, tc), lambda j, i: (0, j))],
        out_shape=[jax.ShapeDtypeStruct((t, n_ch), BF16), jax.ShapeDtypeStruct((kw, n_ch), F32),
                   jax.ShapeDtypeStruct((1, n_ch), F32)],
        compiler_params=_params("parallel", "arbitrary"))(dpre, dpre, zx, zx, conv_w)


def _head_of_lane(shape, width):
    return lax.broadcasted_iota(jnp.int32, shape, len(shape) - 1) // width


def _expand(v, n_rows):
    head = _head_of_lane((n_rows, GW), SSD_HEAD_DIM)
    out = jnp.zeros((n_rows, GW), F32)
    for j in range(SSD_HPG):
        out = jnp.where(head == j, v[:, j:j + 1], out)
    return out


def _contract(v, n_rows):
    head = _head_of_lane((n_rows, GW), SSD_HEAD_DIM)
    lane = lax.broadcasted_iota(jnp.int32, (n_rows, LANES), 1)
    out = jnp.zeros((n_rows, LANES), F32)
    for j in range(SSD_HPG):
        s = jnp.sum(jnp.where(head == j, v, 0.0), axis=1, keepdims=True)
        out = jnp.where(lane == j, s, out)
    return out


def _ssd_dt_prep(zdt, bias, alog, ng):
    t = zdt.shape[0]
    q = SSD_CHUNK

    def body(z_ref, b_ref, a_ref, dt_ref, cum_ref, cumr_ref, sg_ref):
        raw = z_ref[...] + b_ref[...]
        dt = _softplus(raw)
        sgd = _sigmoid(raw)
        row = lax.broadcasted_iota(jnp.int32, (q, q), 0)
        col = lax.broadcasted_iota(jnp.int32, (q, q), 1)
        cum = _dot_f32((col <= row).astype(F32), dt * (-jnp.exp(a_ref[...])))
        cum_t = cum.T
        lane = lax.broadcasted_iota(jnp.int32, (q, LANES), 1)
        for g in range(ng):
            shift = (LANES - g * SSD_HPG) % LANES

            def group(v):
                return jnp.where(lane < SSD_HPG, pltpu.roll(v, shift, 1) if shift else v, 0.0)

            dt_ref[g] = group(dt)
            cum_ref[g] = group(cum)
            sg_ref[g] = group(sgd)
            cumr_ref[g] = (pltpu.roll(cum_t, shift, 0) if shift else cum_t)[0:8, :]

    cols = pl.BlockSpec((ng, q, LANES), lambda c: (0, c, 0))
    vec = pl.BlockSpec((1, LANES), lambda c: (0, 0))
    col_shape = jax.ShapeDtypeStruct((ng, t, LANES), F32)
    return pl.pallas_call(body, name="ssd_dt_prep", grid=(t // q,),
                          in_specs=[pl.BlockSpec((q, LANES), lambda c: (c, 0)), vec, vec],
                          out_specs=[cols, cols, pl.BlockSpec((ng, 8, q), lambda c: (0, 0, c)), cols],
                          out_shape=[col_shape, col_shape, jax.ShapeDtypeStruct((ng, 8, t), F32), col_shape],
                          compiler_params=_params("parallel"))(zdt, bias, alog)


def _ssd_common(pre, dt, cum, cum_r, alog_c):
    q = SSD_CHUNK
    sg = _sigmoid(pre)
    act = pre * sg
    xa = act[:, :GW]
    bm = act[:, GW:GW + SSD_D_STATE].astype(BF16)
    cm = act[:, GW + SSD_D_STATE:].astype(BF16)
    row = lax.broadcasted_iota(jnp.int32, (q, q), 0)
    col = lax.broadcasted_iota(jnp.int32, (q, q), 1)
    tril = col <= row
    a_c = -jnp.exp(alog_c)
    g = _dot_nt(cm, bm)
    dt_x = _expand(dt, q)
    xdt = xa * dt_x
    cl = cum[q - 1:q, :]
    e_c = jnp.exp(cl - cum)
    lam_c = jnp.exp(cum)
    return dict(sg=sg, xa=xa, bm=bm, cm=cm, tril=tril, row=row, col=col, dt=dt, a_c=a_c, cum=cum, cum_r=cum_r,
                g=g, dt_x=dt_x, xdt=xdt, cl=cl, e_c=e_c, lam_c=lam_c)


SSD_GPS_FWD = 4
SSD_GPS_BWD = 2


def _ssd_specs(nc, rev, ng, gps):
    q = SSD_CHUNK
    xw, nw = gps * GW, gps * SSD_D_STATE
    b_off = ng * GW // nw
    c_off = (ng * GW + ng * SSD_D_STATE) // nw
    assert ng % gps == 0 and (ng * GW) % nw == 0 and (ng * SSD_D_STATE) % nw == 0

    def ch(c):
        return nc - 1 - c if rev else c

    chunk_grp = [pl.BlockSpec((q, xw), lambda g, c: (ch(c), g)),
                 pl.BlockSpec((q, nw), lambda g, c: (ch(c), b_off + g)),
                 pl.BlockSpec((q, nw), lambda g, c: (ch(c), c_off + g))]
    col_form = pl.BlockSpec((gps, q, LANES), lambda g, c: (g, ch(c), 0))
    row_form = pl.BlockSpec((gps, 8, q), lambda g, c: (g, 0, ch(c)))
    col_par = pl.BlockSpec((gps, 1, LANES), lambda g, c: (g, 0, 0))
    y_spec = pl.BlockSpec((q, xw), lambda g, c: (ch(c), g))
    st_spec = pl.BlockSpec((gps, None, GW, SSD_D_STATE), lambda g, c: (g, ch(c), 0, 0))
    bc_spec = pl.BlockSpec((q, nw), lambda g, c: (ch(c), g))
    return chunk_grp, col_form, row_form, col_par, y_spec, st_spec, bc_spec


def _ssd_group_views(gi, wide, narrow, stacked):
    xs, ns = pl.ds(gi * GW, GW), pl.ds(gi * SSD_D_STATE, SSD_D_STATE)
    return [r.at[:, xs] for r in wide], [r.at[:, ns] for r in narrow], [r.at[gi] for r in stacked]


def _ssd_fwd(pre, dt_c, cum_c, cum_r, alog_c, dsk_c):
    t = pre.shape[0]
    ng = pre.shape[1] // GC
    q = SSD_CHUNK
    nc = t // q
    gps = SSD_GPS_FWD if ng % SSD_GPS_FWD == 0 else SSD_GPS_BWD
    chunk_grp, col_form, row_form, col_par, y_spec, st_spec, _ = _ssd_specs(nc, False, ng, gps)

    def body(px_ref, pb_ref, pc_ref, dt_ref, cum_ref, cumr_ref, ac_ref, dk_ref, y_ref, sp_ref, st_ref):
        @pl.when(pl.program_id(1) == 0)
        def _():
            st_ref[...] = jnp.zeros_like(st_ref)

        for gi in range(gps):
            (px, y), (pb, pc), rest = _ssd_group_views(
                gi, (px_ref, y_ref), (pb_ref, pc_ref), (dt_ref, cum_ref, cumr_ref, ac_ref, dk_ref, sp_ref, st_ref))
            one_group(px, pb, pc, *rest[:5], y, *rest[5:])

    def one_group(px_ref, pb_ref, pc_ref, dt_ref, cum_ref, cumr_ref, ac_ref, dk_ref, y_ref, sp_ref, st_ref):
        pre_v = jnp.concatenate([px_ref[...], pb_ref[...], pc_ref[...]], axis=1)
        v = _ssd_common(pre_v, dt_ref[...], cum_ref[...], cumr_ref[...], ac_ref[...])
        s0 = st_ref[...]
        sp_ref[...] = s0
        r = _dot_nt(v["cm"], s0.astype(BF16))
        y = _expand(v["lam_c"], q) * r + _expand(dk_ref[...], 1) * v["xa"]
        head = _head_of_lane((q, GW), SSD_HEAD_DIM)
        for j in range(SSD_HPG):
            diff = v["cum"][:, j:j + 1] - v["cum_r"][j:j + 1, :]
            w = (v["g"] * jnp.exp(jnp.where(v["tril"], diff, -jnp.inf))).astype(BF16)
            y = y + _dot(w, jnp.where(head == j, v["xdt"], 0.0).astype(BF16))
        y_ref[...] = y
        ds = _dot_tn((v["xdt"] * _expand(v["e_c"], q)).astype(BF16), v["bm"])
        for j in range(SSD_HPG):
            rows = slice(j * SSD_HEAD_DIM, (j + 1) * SSD_HEAD_DIM)
            st_ref[rows, :] = s0[rows, :] * jnp.exp(v["cum_r"][j:j + 1, q - 1:q]) + ds[rows, :]

    return pl.pallas_call(
        body, name="ssd_scan_fwd", grid=(ng // gps, nc),
        in_specs=chunk_grp + [col_form, col_form, row_form, col_par, col_par],
        out_specs=[y_spec, st_spec],
        out_shape=[jax.ShapeDtypeStruct((t, ng * GW), F32), jax.ShapeDtypeStruct((ng, nc, GW, SSD_D_STATE), F32)],
        scratch_shapes=[pltpu.VMEM((gps, GW, SSD_D_STATE), F32)],
        compiler_params=_params("parallel", "arbitrary"))(pre, pre, pre, dt_c, cum_c, cum_r, alog_c, dsk_c)


def _ssd_bwd(dy, pre, states, dt_c, cum_c, cum_r, sgd_c, alog_c, dsk_c):
    t = pre.shape[0]
    ng = pre.shape[1] // GC
    q = SSD_CHUNK
    nc = t // q
    gps = SSD_GPS_BWD
    chunk_grp, col_form, row_form, col_par, y_spec, st_spec, bc_spec = _ssd_specs(nc, True, ng, gps)

    def body(dy_ref, px_ref, pb_ref, pc_ref, sp_ref, dt_ref, cum_ref, cumr_ref, sgd_ref, ac_ref, dk_ref,
             dpx_ref, dpb_ref, dpc_ref, ddt_ref, dbias_ref, dalog_ref, dd_ref, ds_ref):
        @pl.when(pl.program_id(1) == 0)
        def _():
            ds_ref[...] = jnp.zeros_like(ds_ref)

        for gi in range(gps):
            (dy, px, dpx), (pb, pc, dpb, dpc), rest = _ssd_group_views(
                gi, (dy_ref, px_ref, dpx_ref), (pb_ref, pc_ref, dpb_ref, dpc_ref),
                (sp_ref, dt_ref, cum_ref, cumr_ref, sgd_ref, ac_ref, dk_ref, ddt_ref, dbias_ref, dalog_ref, dd_ref,
                 ds_ref))
            one_group(dy, px, pb, pc, *rest[:7], dpx, dpb, dpc, *rest[7:])

    def one_group(dy_ref, px_ref, pb_ref, pc_ref, sp_ref, dt_ref, cum_ref, cumr_ref, sgd_ref, ac_ref, dk_ref,
                  dpx_ref, dpb_ref, dpc_ref, ddt_ref, dbias_ref, dalog_ref, dd_ref, ds_ref):
        first = pl.program_id(1) == 0
        pre_v = jnp.concatenate([px_ref[...], pb_ref[...], pc_ref[...]], axis=1)
        v = _ssd_common(pre_v, dt_ref[...], cum_ref[...], cumr_ref[...], ac_ref[...])
        xa, bm, cm, xdt, cum, cum_r = v["xa"], v["bm"], v["cm"], v["xdt"], v["cum"], v["cum_r"]
        xdt_b = xdt.astype(BF16)
        dy_v = dy_ref[...]
        s0 = sp_ref[...]
        ds1 = ds_ref[...]
        s0b, ds1b = s0.astype(BF16), ds1.astype(BF16)
        head = _head_of_lane((q, GW), SSD_HEAD_DIM)
        lane = lax.broadcasted_iota(jnp.int32, (q, LANES), 1)
        lane1 = lax.broadcasted_iota(jnp.int32, (1, LANES), 1)
        lam_x = _expand(v["lam_c"], q)
        e_x = _expand(v["e_c"], q)

        dxa = _expand(dk_ref[...], 1) * dy_v
        dd = _contract(jnp.sum(dy_v * xa, axis=0, keepdims=True), 1)
        r = _dot_nt(cm, s0b)
        dcum = _contract(dy_v * r * lam_x, q)
        drb = (lam_x * dy_v).astype(BF16)
        dc = _dot(drb, s0b)
        ds0 = _dot_tn(drb, cm)
        extra = jnp.zeros((1, LANES), F32)
        for j in range(SSD_HPG):
            rows = slice(j * SSD_HEAD_DIM, (j + 1) * SSD_HEAD_DIM)
            lam_last = jnp.exp(cum_r[j:j + 1, q - 1:q])
            ds_ref[rows, :] = ds0[rows, :] + lam_last * ds1[rows, :]
            tot = jnp.sum(jnp.sum(ds1[rows, :] * s0[rows, :], axis=1, keepdims=True), axis=0, keepdims=True)
            extra = jnp.where(lane1 == j, lam_last * tot, extra)
        dv = _dot_nt(bm, ds1b)
        db = _dot((xdt * e_x).astype(BF16), ds1b)
        dxdt = e_x * dv
        dee = _contract(dv * xdt, q) * v["e_c"]
        dcum = dcum - dee
        extra = extra + jnp.sum(dee, axis=0, keepdims=True)
        dg = jnp.zeros((q, q), F32)
        for j in range(SSD_HPG):
            diff = cum[:, j:j + 1] - cum_r[j:j + 1, :]
            el = jnp.exp(jnp.where(v["tril"], diff, -jnp.inf))
            gl = v["g"] * el
            dym = jnp.where(head == j, dy_v, 0.0).astype(BF16)
            dwm = _dot_nt(dym, xdt_b)
            dxdt = dxdt + _dot_tn(gl.astype(BF16), dym)
            z = dwm * gl
            rk = jnp.sum(z, axis=1, keepdims=True) - jnp.sum(z.T, axis=1, keepdims=True)
            dcum = jnp.where(lane == j, dcum + rk, dcum)
            dg = dg + dwm * el
        dgb = dg.astype(BF16)
        dc = dc + _dot(dgb, bm)
        db = db + _dot_tn(dgb, cm)
        da = _dot_f32((v["row"] <= v["col"]).astype(F32), dcum) + extra
        ddt = _contract(dxdt * xa, q) + v["a_c"] * da
        dalog = jnp.sum(v["dt"] * da, axis=0, keepdims=True) * v["a_c"]
        dxa = dxa + v["dt_x"] * dxdt
        ddt_raw = jnp.where(lane < SSD_HPG, ddt * sgd_ref[...], 0.0)
        sgrad = _silu_grad(pre_v, v["sg"])
        dpx_ref[...] = dxa * sgrad[:, :GW]
        dpb_ref[...] = db * sgrad[:, GW:GW + SSD_D_STATE]
        dpc_ref[...] = dc * sgrad[:, GW + SSD_D_STATE:]
        ddt_ref[...] = ddt_raw
        _acc(dbias_ref, jnp.sum(ddt_raw, axis=0, keepdims=True), first)
        _acc(dalog_ref, jnp.where(lane1 < SSD_HPG, dalog, 0.0), first)
        _acc(dd_ref, dd, first)

    return pl.pallas_call(
        body, name="ssd_scan_bwd", grid=(ng // gps, nc),
        in_specs=[y_spec] + chunk_grp + [st_spec, col_form, col_form, row_form, col_form, col_par, col_par],
        out_specs=[y_spec, bc_spec, bc_spec, col_form, col_par, col_par, col_par],
        out_shape=[jax.ShapeDtypeStruct((t, ng * GW), F32), jax.ShapeDtypeStruct((t, ng * SSD_D_STATE), F32),
                   jax.ShapeDtypeStruct((t, ng * SSD_D_STATE), F32), jax.ShapeDtypeStruct((ng, t, LANES), F32),
                   jax.ShapeDtypeStruct((ng, 1, LANES), F32), jax.ShapeDtypeStruct((ng, 1, LANES), F32),
                   jax.ShapeDtypeStruct((ng, 1, LANES), F32)],
        scratch_shapes=[pltpu.VMEM((gps, GW, SSD_D_STATE), F32)],
        compiler_params=_params("parallel", "arbitrary"))(dy, pre, pre, pre, states, dt_c, cum_c, cum_r, sgd_c, alog_c,
                                                           dsk_c)


def _gate_norm_fwd(y, zx, norm_w):
    t, di = y.shape
    tr = _tile(t, 256, 8)
    ng = di // GW

    def body(y_ref, z_ref, w_ref, o_ref):
        z = z_ref[...]
        gate = y_ref[...] * (z * _sigmoid(z))
        w = w_ref[...]
        for g in range(ng):
            cols = slice(g * GW, (g + 1) * GW)
            gs = gate[:, cols]
            r = lax.rsqrt(jnp.mean(gs * gs, axis=-1, keepdims=True) + NORM_EPS)
            o_ref[:, cols] = (gs * r * w[:, cols]).astype(BF16)

    row = pl.BlockSpec((tr, di), lambda i: (i, 0))
    return pl.pallas_call(body, name="ssd_gate_norm_fwd", grid=(t // tr,),
                          in_specs=[row, row, pl.BlockSpec((1, di), lambda i: (0, 0))], out_specs=row,
                          out_shape=jax.ShapeDtypeStruct((t, di), BF16), compiler_params=_params("parallel"))(
                              y, zx, norm_w)


def _gate_norm_bwd(dyn, y, zx, norm_w, after):
    t, di = y.shape
    tr = _tile(t, 256, 8)
    ng = di // GW

    def body(d_ref, y_ref, z_ref, w_ref, after_ref, dy_ref, dz_ref, dw_ref):
        z = z_ref[...]
        yv = y_ref[...]
        sg = _sigmoid(z)
        sz = z * sg
        gate = yv * sz
        w = w_ref[...]
        d = d_ref[...]
        dsz = _silu_grad(z, sg)
        dws = []
        for g in range(ng):
            cols = slice(g * GW, (g + 1) * GW)
            dg, dwr = _rms_bwd(gate[:, cols], w[:, cols], d[:, cols])
            dy_ref[:, cols] = dg * sz[:, cols]
            dz_ref[:, cols] = (dg * yv[:, cols] * dsz[:, cols]).astype(BF16)
            dws.append(jnp.sum(dwr, axis=0, keepdims=True))
        first = pl.program_id(0) == 0
        for g in range(ng):
            cols = slice(g * GW, (g + 1) * GW)

            @pl.when(first)
            def _():
                dw_ref[:, cols] = dws[g]

            @pl.when(jnp.logical_not(first))
            def _():
                dw_ref[:, cols] += dws[g]

    row = pl.BlockSpec((tr, di), lambda i: (i, 0))
    vec = pl.BlockSpec((1, di), lambda i: (0, 0))
    return pl.pallas_call(body, name="ssd_gate_norm_bwd", grid=(t // tr,),
                          in_specs=[row, row, row, vec, pl.BlockSpec((8, LANES), lambda i: (0, 0))],
                          out_specs=[row, row, vec],
                          out_shape=[jax.ShapeDtypeStruct((t, di), F32), jax.ShapeDtypeStruct((t, di), BF16),
                                     jax.ShapeDtypeStruct((1, di), F32)],
                          compiler_params=_params("arbitrary"))(dyn, y, zx, norm_w, after)


def _attn_mask(n):
    w = ATTN_WINDOW
    qpos = lax.broadcasted_iota(jnp.int32, (w, 2 * w), 0) + w
    kpos = lax.broadcasted_iota(jnp.int32, (w, 2 * w), 1)
    rel = qpos - kpos
    return (rel >= 0) & (rel < w) & jnp.logical_not((n == 0) & (kpos < w))


def _attn_probs(qh, kbh, mask, sink):
    s = _dot_nt(qh, kbh) * (ATTN_HEAD_DIM ** -0.5)
    s = jnp.where(mask, s, -jnp.inf)
    m = jnp.maximum(jnp.max(s, axis=-1, keepdims=True), sink)
    e = jnp.exp(s - m)
    es = jnp.exp(sink - m)
    inv = 1.0 / (jnp.sum(e, axis=-1, keepdims=True) + es)
    return e * inv, es * inv


def _attn_fwd(qkv, sinks):
    t = qkv.shape[0]
    w, hd = ATTN_WINDOW, ATTN_HEAD_DIM
    kd = ATTN_N_KV * hd
    qd = ATTN_REP * kd
    nb = t // w

    def body(q_ref, kc_ref, vc_ref, kp_ref, vp_ref, s_ref, o_ref):
        n = pl.program_id(0)
        mask = _attn_mask(n)
        q = q_ref[...]
        kb = jnp.concatenate([kp_ref[...], kc_ref[...]], axis=0)
        vb = jnp.concatenate([vp_ref[...], vc_ref[...]], axis=0)
        sk = s_ref[...]
        for kv in range(ATTN_N_KV):
            kbh = kb[:, kv * hd:(kv + 1) * hd]
            vbh = vb[:, kv * hd:(kv + 1) * hd]
            for rep in range(ATTN_REP):
                h = kv * ATTN_REP + rep
                p, _ = _attn_probs(q[:, h * hd:(h + 1) * hd], kbh, mask, sk[:, h:h + 1])
                o_ref[:, h * hd:(h + 1) * hd] = _dot(p.astype(BF16), vbh).astype(BF16)

    prev = lambda n: jnp.maximum(n - 1, 0)
    return pl.pallas_call(
        body, name="attn_fwd", grid=(nb,),
        in_specs=[pl.BlockSpec((w, qd), lambda n: (n, 0)),
                  pl.BlockSpec((w, kd), lambda n: (n, ATTN_REP)),
                  pl.BlockSpec((w, kd), lambda n: (n, ATTN_REP + 1)),
                  pl.BlockSpec((w, kd), lambda n: (prev(n), ATTN_REP)),
                  pl.BlockSpec((w, kd), lambda n: (prev(n), ATTN_REP + 1)),
                  pl.BlockSpec((1, sinks.shape[1]), lambda n: (0, 0))],
        out_specs=pl.BlockSpec((w, qd), lambda n: (n, 0)),
        out_shape=jax.ShapeDtypeStruct((t, qd), BF16),
        compiler_params=_params("parallel"))(qkv, qkv, qkv, qkv, qkv, sinks)


def _attn_bwd(qkv, do, sinks):
    t = qkv.shape[0]
    w, hd = ATTN_WINDOW, ATTN_HEAD_DIM
    kd = ATTN_N_KV * hd
    qd = ATTN_REP * kd
    nq = ATTN_N_KV * ATTN_REP
    nb = t // w

    def body(q_ref, kc_ref, vc_ref, kp_ref, vp_ref, do_ref, s_ref,
             dq_ref, dk_ref, dv_ref, bq_ref, bk_ref, bv_ref, dsk_ref, ck_ref, cv_ref):
        n = pl.program_id(0)
        first = n == 0

        @pl.when(first)
        def _():
            ck_ref[...] = jnp.zeros_like(ck_ref)
            cv_ref[...] = jnp.zeros_like(cv_ref)
            bq_ref[...] = jnp.zeros_like(bq_ref)
            bk_ref[...] = jnp.zeros_like(bk_ref)
            bv_ref[...] = jnp.zeros_like(bv_ref)
            dsk_ref[...] = jnp.zeros_like(dsk_ref)

        @pl.when(n < nb)
        def _():
            mask = _attn_mask(n)
            q = q_ref[...]
            dov = do_ref[...]
            kb = jnp.concatenate([kp_ref[...], kc_ref[...]], axis=0)
            vb = jnp.concatenate([vp_ref[...], vc_ref[...]], axis=0)
            sk = s_ref[...]
            lane = lax.broadcasted_iota(jnp.int32, (1, nq), 1)
            dsk = jnp.zeros((1, nq), F32)
            dq_parts, dk_parts, dv_parts = [], [], []
            for kv in range(ATTN_N_KV):
                kbh = kb[:, kv * hd:(kv + 1) * hd]
                vbh = vb[:, kv * hd:(kv + 1) * hd]
                dkh = jnp.zeros((2 * w, hd), F32)
                dvh = jnp.zeros((2 * w, hd), F32)
                for rep in range(ATTN_REP):
                    h = kv * ATTN_REP + rep
                    qh = q[:, h * hd:(h + 1) * hd]
                    doh = dov[:, h * hd:(h + 1) * hd]
                    p, ps = _attn_probs(qh, kbh, mask, sk[:, h:h + 1])
                    pb = p.astype(BF16)
                    dp = _dot_nt(doh, vbh)
                    delta = jnp.sum(p * dp, axis=-1, keepdims=True)
                    dsc = (p * (dp - delta) * (hd ** -0.5)).astype(BF16)
                    dq_parts.append(_dot(dsc, kbh))
                    dkh = dkh + _dot_tn(dsc, qh)
                    dvh = dvh + _dot_tn(pb, doh)
                    dsk = jnp.where(lane == h, -jnp.sum(ps * delta, axis=0, keepdims=True), dsk)
                dk_parts.append(dkh)
                dv_parts.append(dvh)
            dq = jnp.concatenate(dq_parts, axis=1)
            dkb = jnp.concatenate(dk_parts, axis=1)
            dvb = jnp.concatenate(dv_parts, axis=1)
            dq_ref[...] = dq.astype(BF16)
            bq_ref[...] += jnp.sum(dq, axis=0, keepdims=True)
            dsk_ref[...] += dsk
            dk_prev = ck_ref[...] + dkb[:w, :]
            dv_prev = cv_ref[...] + dvb[:w, :]
            dk_ref[...] = dk_prev.astype(BF16)
            dv_ref[...] = dv_prev.astype(BF16)

            @pl.when(n > 0)
            def _():
                bk_ref[...] += jnp.sum(dk_prev, axis=0, keepdims=True)
                bv_ref[...] += jnp.sum(dv_prev, axis=0, keepdims=True)

            ck_ref[...] = dkb[w:, :]
            cv_ref[...] = dvb[w:, :]

        @pl.when(n == nb)
        def _():
            dk_ref[...] = ck_ref[...].astype(BF16)
            dv_ref[...] = cv_ref[...].astype(BF16)
            bk_ref[...] += jnp.sum(ck_ref[...], axis=0, keepdims=True)
            bv_ref[...] += jnp.sum(cv_ref[...], axis=0, keepdims=True)

    cur = lambda n: jnp.minimum(n, nb - 1)
    prev = lambda n: jnp.maximum(jnp.minimum(n, nb - 1) - 1, 0)
    late = lambda n: jnp.maximum(n - 1, 0)
    vec = lambda width: pl.BlockSpec((1, width), lambda n: (0, 0))
    return pl.pallas_call(
        body, name="attn_bwd", grid=(nb + 1,),
        in_specs=[pl.BlockSpec((w, qd), lambda n: (cur(n), 0)),
                  pl.BlockSpec((w, kd), lambda n: (cur(n), ATTN_REP)),
                  pl.BlockSpec((w, kd), lambda n: (cur(n), ATTN_REP + 1)),
                  pl.BlockSpec((w, kd), lambda n: (prev(n), ATTN_REP)),
                  pl.BlockSpec((w, kd), lambda n: (prev(n), ATTN_REP + 1)),
                  pl.BlockSpec((w, qd), lambda n: (cur(n), 0)),
                  vec(nq)],
        out_specs=[pl.BlockSpec((w, qd), lambda n: (cur(n), 0)),
                   pl.BlockSpec((w, kd), lambda n: (late(n), 0)),
                   pl.BlockSpec((w, kd), lambda n: (late(n), 0)),
                   vec(qd), vec(kd), vec(kd), vec(nq)],
        out_shape=[jax.ShapeDtypeStruct((t, qd), BF16), jax.ShapeDtypeStruct((t, kd), BF16),
                   jax.ShapeDtypeStruct((t, kd), BF16), jax.ShapeDtypeStruct((1, qd), F32),
                   jax.ShapeDtypeStruct((1, kd), F32), jax.ShapeDtypeStruct((1, kd), F32),
                   jax.ShapeDtypeStruct((1, nq), F32)],
        scratch_shapes=[pltpu.VMEM((w, kd), F32), pltpu.VMEM((w, kd), F32)],
        compiler_params=_params("arbitrary"))(qkv, qkv, qkv, qkv, qkv, do, sinks)


def _attn_mask_t(n):
    w = ATTN_WINDOW
    kpos = lax.broadcasted_iota(jnp.int32, (2 * w, ATTN_REP * w), 0)
    qpos = lax.broadcasted_iota(jnp.int32, (2 * w, ATTN_REP * w), 1) % w + w
    rel = qpos - kpos
    return (rel >= 0) & (rel < w) & jnp.logical_not((n == 0) & (kpos < w))


def _attn_probs_t(qts, ktb, mask, sink):
    s = _dot_tn(ktb, qts) * (ATTN_HEAD_DIM ** -0.5)
    s = jnp.where(mask, s, -jnp.inf)
    m = jnp.maximum(jnp.max(s, axis=0, keepdims=True), sink)
    e = jnp.exp(s - m)
    es = jnp.exp(sink - m)
    inv = 1.0 / (jnp.sum(e, axis=0, keepdims=True) + es)
    return e * inv, es * inv


def _attn_blocks_t(kv, q_ref, kc_ref, vc_ref, kp_ref, vp_ref):
    hd = ATTN_HEAD_DIM
    rows = slice(kv * hd, (kv + 1) * hd)
    ktb = jnp.concatenate([kp_ref[rows, :], kc_ref[rows, :]], axis=1)
    vtb = jnp.concatenate([vp_ref[rows, :], vc_ref[rows, :]], axis=1)
    qts = jnp.concatenate([q_ref[(kv * ATTN_REP + r) * hd:(kv * ATTN_REP + r + 1) * hd, :]
                           for r in range(ATTN_REP)], axis=1)
    return qts, ktb, vtb


def _attn_specs_t(nb, cur, prev):
    w, hd = ATTN_WINDOW, ATTN_HEAD_DIM
    kd = ATTN_N_KV * hd
    qd = ATTN_REP * kd
    return [pl.BlockSpec((qd, w), lambda n: (0, cur(n))),
            pl.BlockSpec((kd, w), lambda n: (ATTN_REP, cur(n))),
            pl.BlockSpec((kd, w), lambda n: (ATTN_REP + 1, cur(n))),
            pl.BlockSpec((kd, w), lambda n: (ATTN_REP, prev(n))),
            pl.BlockSpec((kd, w), lambda n: (ATTN_REP + 1, prev(n)))]


def _attn_fwd_t(qkv_t, sinks_rep):
    t = qkv_t.shape[1]
    w, hd = ATTN_WINDOW, ATTN_HEAD_DIM
    qd = ATTN_N_KV * ATTN_REP * hd
    nb = t // w

    def body(q_ref, kc_ref, vc_ref, kp_ref, vp_ref, s_ref, o_ref):
        mask = _attn_mask_t(pl.program_id(0))
        for kv in range(ATTN_N_KV):
            qts, ktb, vtb = _attn_blocks_t(kv, q_ref, kc_ref, vc_ref, kp_ref, vp_ref)
            p, _ = _attn_probs_t(qts, ktb, mask, s_ref[kv])
            ots = _dot(vtb, p.astype(BF16))
            for r in range(ATTN_REP):
                h = kv * ATTN_REP + r
                o_ref[h * hd:(h + 1) * hd, :] = ots[:, r * w:(r + 1) * w].astype(BF16)

    return pl.pallas_call(
        body, name="attn_fwd", grid=(nb,),
        in_specs=_attn_specs_t(nb, lambda n: n, lambda n: jnp.maximum(n - 1, 0)) + [
            pl.BlockSpec(sinks_rep.shape, lambda n: (0, 0, 0))],
        out_specs=pl.BlockSpec((qd, w), lambda n: (0, n)),
        out_shape=jax.ShapeDtypeStruct((qd, t), BF16),
        compiler_params=_params("parallel"))(qkv_t, qkv_t, qkv_t, qkv_t, qkv_t, sinks_rep)


def _attn_bwd_t(qkv_t, do_t, sinks_rep):
    t = qkv_t.shape[1]
    w, hd = ATTN_WINDOW, ATTN_HEAD_DIM
    kd = ATTN_N_KV * hd
    qd = ATTN_REP * kd
    nq = ATTN_N_KV * ATTN_REP
    nb = t // w
    rows_all = qd + 2 * kd

    def body(q_ref, kc_ref, vc_ref, kp_ref, vp_ref, do_ref, s_ref, dqkv_ref, bsum_ref, dsk_ref,
             carry_ref, new_ref, bacc_ref, sacc_ref):
        n = pl.program_id(0)

        @pl.when(n == 0)
        def _():
            carry_ref[...] = jnp.zeros_like(carry_ref)
            bacc_ref[...] = jnp.zeros_like(bacc_ref)
            sacc_ref[...] = jnp.zeros_like(sacc_ref)

        @pl.when(n < nb)
        def _():
            mask = _attn_mask_t(n)
            for kv in range(ATTN_N_KV):
                qts, ktb, vtb = _attn_blocks_t(kv, q_ref, kc_ref, vc_ref, kp_ref, vp_ref)
                dots = jnp.concatenate([do_ref[(kv * ATTN_REP + r) * hd:(kv * ATTN_REP + r + 1) * hd, :]
                                        for r in range(ATTN_REP)], axis=1)
                p, ps = _attn_probs_t(qts, ktb, mask, s_ref[kv])
                dpt = _dot_tn(vtb, dots)
                delta = jnp.sum(p * dpt, axis=0, keepdims=True)
                dst = (p * (dpt - delta) * (hd ** -0.5)).astype(BF16)
                dqts = _dot(ktb, dst)
                for r in range(ATTN_REP):
                    h = kv * ATTN_REP + r
                    new_ref[h * hd:(h + 1) * hd, :] = dqts[:, r * w:(r + 1) * w]
                dktb = _dot_nt(qts, dst)
                dvtb = _dot_nt(dots, p.astype(BF16))
                krows = slice(qd + kv * hd, qd + (kv + 1) * hd)
                vrows = slice(qd + kd + kv * hd, qd + kd + (kv + 1) * hd)
                carry_ref[krows, :] += dktb[:, :w]
                carry_ref[vrows, :] += dvtb[:, :w]
                new_ref[krows, :] = dktb[:, w:]
                new_ref[vrows, :] = dvtb[:, w:]
                sacc_ref[kv] += -(ps * delta)

        @pl.when(n >= 1)
        def _():
            done = carry_ref[...]
            dqkv_ref[...] = done.astype(BF16)
            bacc_ref[...] += done

        @pl.when(n < nb)
        def _():
            carry_ref[...] = new_ref[...]

        @pl.when(n == nb)
        def _():
            bsum_ref[...] = jnp.sum(bacc_ref[...], axis=1, keepdims=True)
            lane = lax.broadcasted_iota(jnp.int32, (1, nq), 1)
            dsk = jnp.zeros((1, nq), F32)
            for kv in range(ATTN_N_KV):
                acc = sacc_ref[kv]
                for r in range(ATTN_REP):
                    tot = jnp.sum(acc[:, r * w:(r + 1) * w], axis=1, keepdims=True)
                    dsk = jnp.where(lane == kv * ATTN_REP + r, tot, dsk)
            dsk_ref[...] = dsk

    cur = lambda n: jnp.minimum(n, nb - 1)
    prev = lambda n: jnp.maximum(jnp.minimum(n, nb - 1) - 1, 0)
    return pl.pallas_call(
        body, name="attn_bwd", grid=(nb + 1,),
        in_specs=_attn_specs_t(nb, cur, prev) + [pl.BlockSpec((qd, w), lambda n: (0, cur(n))),
                                                 pl.BlockSpec(sinks_rep.shape, lambda n: (0, 0, 0))],
        out_specs=[pl.BlockSpec((rows_all, w), lambda n: (0, jnp.maximum(n - 1, 0))),
                   pl.BlockSpec((rows_all, 1), lambda n: (0, 0)),
                   pl.BlockSpec((1, nq), lambda n: (0, 0))],
        out_shape=[jax.ShapeDtypeStruct((rows_all, t), BF16), jax.ShapeDtypeStruct((rows_all, 1), F32),
                   jax.ShapeDtypeStruct((1, nq), F32)],
        scratch_shapes=[pltpu.VMEM((rows_all, w), F32), pltpu.VMEM((rows_all, w), F32),
                        pltpu.VMEM((rows_all, w), F32), pltpu.VMEM(sinks_rep.shape, F32)],
        compiler_params=_params("arbitrary"))(qkv_t, qkv_t, qkv_t, qkv_t, qkv_t, do_t, sinks_rep)


HBM_SPEC = pl.BlockSpec(memory_space=pl.ANY)
HBM_ONLY = pl.BlockSpec(memory_space=pltpu.HBM)


def _comm_call(name, body, ins, out_shapes, n_sems):
    return pl.pallas_call(
        body, name=name, in_specs=[HBM_SPEC] * len(ins), out_specs=[HBM_SPEC] * len(out_shapes),
        out_shape=out_shapes,
        scratch_shapes=[pltpu.SemaphoreType.DMA((s,)) for s in n_sems])(*ins)


def _all_gather(name, shards, after):
    n = len(shards)
    na = len(after)

    def body(*refs):
        x_refs, out_refs = refs[:n], refs[n + na:2 * n + na]
        send_sems, recv_sems, local_sems = refs[2 * n + na:]
        x, y, c = lax.axis_index("x"), lax.axis_index("y"), lax.axis_index("c")
        me, sibling = (x, y, c), (x, y, 1 - c)
        chips = [(1 - x, y), (x, 1 - y), (1 - x, 1 - y)]

        def slot(i, px, py, pc):
            return out_refs[i].at[4 * px + 2 * py + pc]

        def copy(k, i, block, to, src=None):
            return pltpu.make_async_remote_copy(
                src_ref=slot(i, *block) if src is None else src, dst_ref=slot(i, *block),
                send_sem=send_sems.at[k * n + i], recv_sem=recv_sems.at[k * n + i], device_id=to,
                device_id_type=MESH)

        mine = [pltpu.make_async_copy(x_refs[i], slot(i, *me), local_sems.at[i]) for i in range(n)]
        first = []
        for i in range(n):
            mine[i].start()
            first.append(copy(0, i, me, sibling, src=x_refs[i]))
            first += [copy(1 + j, i, me, (*chip, c), src=x_refs[i]) for j, chip in enumerate(chips)]
        for cp in first:
            cp.start()
        passed = []
        for i in range(n):
            for j, chip in enumerate(chips):
                copy(1 + j, i, (*chip, c), me).wait_recv()
                passed.append(copy(4 + j, i, (*chip, c), sibling))
                passed[-1].start()
        for i in range(n):
            copy(0, i, sibling, me).wait_recv()
            for j, chip in enumerate(chips):
                copy(4 + j, i, (*chip, 1 - c), me).wait_recv()
        for cp in first + passed:
            cp.wait_send()
        for cp in mine:
            cp.wait()

    outs = [jax.ShapeDtypeStruct((N_DEV,) + s.shape, s.dtype) for s in shards]
    return _comm_call(name, body, list(shards) + list(after), outs, (7 * n, 7 * n, n))


SEM_SPEC = pl.BlockSpec(memory_space=pltpu.SEMAPHORE)
SPLIT_COPY_EFFECT = pltpu.SideEffectType.DATAFLOW_SIDE_EFFECTING


def _in_hbm(a):
    return pltpu.with_memory_space_constraint(a, pltpu.HBM)


def _split_start(name, body, srcs, lands, n_sems):
    n = len(srcs)
    bufs = [_in_hbm(a) for a in list(srcs) + list(lands)]
    outs = pl.pallas_call(
        body, name=name,
        out_shape=(pltpu.SemaphoreType.DMA((n_sems,)), pltpu.SemaphoreType.DMA((n_sems,)),
                   *[pltpu.HBM(a.shape, a.dtype) for a in bufs], jax.ShapeDtypeStruct((8, LANES), F32)),
        in_specs=[HBM_ONLY] * (2 * n),
        out_specs=(SEM_SPEC, SEM_SPEC, *[HBM_ONLY] * (2 * n), pl.BlockSpec(memory_space=pltpu.VMEM)),
        input_output_aliases={i: 2 + i for i in range(2 * n)},
        compiler_params=pltpu.CompilerParams(has_side_effects=SPLIT_COPY_EFFECT))(*bufs)
    return outs[0], outs[1], list(outs[2:2 + n]), list(outs[2 + n:2 + 2 * n]), outs[-1]


def _split_wait(name, body, send_sems, recv_sems, srcs, lands, after):
    n = len(srcs)
    outs = pl.pallas_call(
        body, name=name,
        out_shape=[pltpu.HBM(a.shape, a.dtype) for a in list(srcs) + list(lands)],
        in_specs=[HBM_ONLY] * (2 * n) + [SEM_SPEC, SEM_SPEC, HBM_SPEC],
        out_specs=[HBM_ONLY] * (2 * n),
        input_output_aliases={i: i for i in range(2 * n)},
        compiler_params=pltpu.CompilerParams(has_side_effects=SPLIT_COPY_EFFECT))(
            *srcs, *lands, send_sems, recv_sems, after)
    return list(outs[:n]), list(outs[n:])


N_PEERS = N_DEV - 1


def _gather_peers():
    x, y, c = lax.axis_index("x"), lax.axis_index("y"), lax.axis_index("c")
    flips = [(fx, fy, fc) for fx in (0, 1) for fy in (0, 1) for fc in (0, 1) if fx or fy or fc]
    return [(1 - x if fx else x, 1 - y if fy else y, 1 - c if fc else c) for fx, fy, fc in flips]


def _block_id(dev):
    return 4 * dev[0] + 2 * dev[1] + dev[2]


def _landing_block(land_ref, shard_shape, side_by_side, dev):
    if not side_by_side:
        return land_ref.at[_block_id(dev)]
    cols = shard_shape[1]
    return land_ref.at[:, pl.ds(pl.multiple_of(_block_id(dev) * cols, LANES), cols)]


def _gather_start(name, shards, side_by_side):
    n = len(shards)

    def body(*refs):
        x_refs, land_refs = refs[:n], refs[n:2 * n]
        send_sems, recv_sems, token = refs[2 * n], refs[2 * n + 1], refs[-1]
        me = (lax.axis_index("x"), lax.axis_index("y"), lax.axis_index("c"))
        for i in range(n):
            for k, peer in enumerate(_gather_peers()):
                pltpu.make_async_remote_copy(
                    src_ref=x_refs[i], dst_ref=_landing_block(land_refs[i], shards[i].shape, side_by_side[i], me),
                    send_sem=send_sems.at[N_PEERS * i + k], recv_sem=recv_sems.at[N_PEERS * i + k],
                    device_id=peer, device_id_type=MESH).start()
        token[...] = jnp.zeros_like(token)

    lands = [lax.empty((s.shape[0], N_DEV * s.shape[1]) if wide else (N_DEV,) + s.shape, s.dtype)
             for s, wide in zip(shards, side_by_side)]
    return _split_start(name, body, shards, lands, N_PEERS * n)


def _gather_wait(name, send_sems, recv_sems, first, shards, lands, side_by_side, after):
    n = len(shards)

    def body(*refs):
        x_refs, land_refs = refs[:n], refs[n:2 * n]
        send_sems, recv_sems = refs[2 * n], refs[2 * n + 1]
        for i in range(n):
            for k, peer in enumerate(_gather_peers()):
                cp = pltpu.make_async_remote_copy(
                    src_ref=x_refs[i], dst_ref=_landing_block(land_refs[i], shards[i].shape, side_by_side[i], peer),
                    send_sem=send_sems.at[N_PEERS * (first + i) + k],
                    recv_sem=recv_sems.at[N_PEERS * (first + i) + k],
                    device_id=peer, device_id_type=MESH)
                cp.wait_send()
                cp.wait_recv()

    return _split_wait(name, body, send_sems, recv_sems, shards, lands, after)


def _gather_forward(name, lands, shards):
    n = len(shards)

    def body(*refs):
        x_refs, out_refs = refs[n:2 * n], refs[2 * n:3 * n]
        send_sems, recv_sems, local_sems = refs[3 * n:]
        x, y, c = lax.axis_index("x"), lax.axis_index("y"), lax.axis_index("c")
        chips = [(1 - x, y), (x, 1 - y), (1 - x, 1 - y)]
        mine = [pltpu.make_async_copy(x_refs[i], out_refs[i].at[_block_id((x, y, c))], local_sems.at[i])
                for i in range(n)]
        passed = [pltpu.make_async_remote_copy(
            src_ref=out_refs[i].at[_block_id((*chip, c))], dst_ref=out_refs[i].at[_block_id((*chip, c))],
            send_sem=send_sems.at[3 * i + j], recv_sem=recv_sems.at[3 * i + j], device_id=(x, y, 1 - c),
            device_id_type=MESH) for i in range(n) for j, chip in enumerate(chips)]
        for cp in mine + passed:
            cp.start()
        for i in range(n):
            for j, chip in enumerate(chips):
                pltpu.make_async_remote_copy(
                    src_ref=out_refs[i].at[_block_id((*chip, c))], dst_ref=out_refs[i].at[_block_id((*chip, 1 - c))],
                    send_sem=send_sems.at[3 * i + j], recv_sem=recv_sems.at[3 * i + j], device_id=(x, y, 1 - c),
                    device_id_type=MESH).wait()
        for cp in mine:
            cp.wait()

    return pl.pallas_call(
        body, name=name, in_specs=[HBM_SPEC] * (2 * n), out_specs=[HBM_SPEC] * n,
        out_shape=[jax.ShapeDtypeStruct(a.shape, a.dtype) for a in lands],
        input_output_aliases={i: i for i in range(n)},
        scratch_shapes=[pltpu.SemaphoreType.DMA((3 * n,)), pltpu.SemaphoreType.DMA((3 * n,)),
                        pltpu.SemaphoreType.DMA((n,))])(*lands, *shards)


def _chip_peers():
    x, y, c = lax.axis_index("x"), lax.axis_index("y"), lax.axis_index("c")
    return [(1 - x, y, c), (x, 1 - y, c), (1 - x, 1 - y, c)]


def _chip_start(name, blocks):
    n = len(blocks)

    def body(*refs):
        p_refs, land_refs = refs[:n], refs[n:2 * n]
        send_sems, recv_sems, token = refs[2 * n], refs[2 * n + 1], refs[-1]
        for i in range(n):
            for j, peer in enumerate(_chip_peers()):
                pltpu.make_async_remote_copy(
                    src_ref=p_refs[i].at[j], dst_ref=land_refs[i].at[j], send_sem=send_sems.at[3 * i + j],
                    recv_sem=recv_sems.at[3 * i + j], device_id=peer, device_id_type=MESH).start()
        token[...] = jnp.zeros_like(token)

    lands = [lax.empty(b.shape, b.dtype) for b in blocks]
    return _split_start(name, body, blocks, lands, 3 * n)


def _chip_wait(name, send_sems, recv_sems, blocks, lands, after):
    n = len(blocks)

    def body(*refs):
        p_refs, land_refs = refs[:n], refs[n:2 * n]
        send_sems, recv_sems = refs[2 * n], refs[2 * n + 1]
        for i in range(n):
            for j, peer in enumerate(_chip_peers()):
                cp = pltpu.make_async_remote_copy(
                    src_ref=p_refs[i].at[j], dst_ref=land_refs[i].at[j], send_sem=send_sems.at[3 * i + j],
                    recv_sem=recv_sems.at[3 * i + j], device_id=peer, device_id_type=MESH)
                cp.wait_send()
                cp.wait_recv()

    return _split_wait(name, body, send_sems, recv_sems, blocks, lands, after)


def _scatter_start(name, blocks):
    n = len(blocks)

    def body(*refs):
        b_refs, land_refs = refs[:n], refs[n:2 * n]
        send_sems, recv_sems, token = refs[2 * n], refs[2 * n + 1], refs[-1]
        me = (lax.axis_index("x"), lax.axis_index("y"), lax.axis_index("c"))
        for i in range(n):
            for k, peer in enumerate(_gather_peers()):
                pltpu.make_async_remote_copy(
                    src_ref=b_refs[i].at[_block_id(peer)], dst_ref=land_refs[i].at[_block_id(me)],
                    send_sem=send_sems.at[N_PEERS * i + k], recv_sem=recv_sems.at[N_PEERS * i + k],
                    device_id=peer, device_id_type=MESH).start()
        token[...] = jnp.zeros_like(token)

    lands = [lax.empty(b.shape, b.dtype) for b in blocks]
    return _split_start(name, body, blocks, lands, N_PEERS * n)


def _scatter_wait(name, send_sems, recv_sems, blocks, lands, after):
    n = len(blocks)

    def body(*refs):
        b_refs, land_refs = refs[:n], refs[n:2 * n]
        send_sems, recv_sems = refs[2 * n], refs[2 * n + 1]
        for i in range(n):
            for k, peer in enumerate(_gather_peers()):
                cp = pltpu.make_async_remote_copy(
                    src_ref=b_refs[i].at[_block_id(peer)], dst_ref=land_refs[i].at[_block_id(peer)],
                    send_sem=send_sems.at[N_PEERS * i + k], recv_sem=recv_sems.at[N_PEERS * i + k],
                    device_id=peer, device_id_type=MESH)
                cp.wait_send()
                cp.wait_recv()

    return _split_wait(name, body, send_sems, recv_sems, blocks, lands, after)


def _pair_exchange(name, blocks):
    n = len(blocks)

    def body(*refs):
        g_refs, out_refs = refs[:n], refs[n:2 * n]
        send_sems, recv_sems = refs[2 * n:]
        x, y, c = lax.axis_index("x"), lax.axis_index("y"), lax.axis_index("c")
        copies = [pltpu.make_async_remote_copy(
            src_ref=g_refs[i].at[2 * k + 1 - c], dst_ref=out_refs[i].at[k], send_sem=send_sems.at[4 * i + k],
            recv_sem=recv_sems.at[4 * i + k], device_id=(x, y, 1 - c), device_id_type=MESH)
            for i in range(n) for k in range(4)]
        for cp in copies:
            cp.start()
        for cp in copies:
            cp.wait()

    outs = [jax.ShapeDtypeStruct((4,) + b.shape[1:], b.dtype) for b in blocks]
    return _comm_call(name, body, blocks, outs, (4 * n, 4 * n))


def _chip_exchange(name, blocks):
    n = len(blocks)

    def body(*refs):
        p_refs, out_refs = refs[:n], refs[n:2 * n]
        send_sems, recv_sems = refs[2 * n:]
        x, y, c = lax.axis_index("x"), lax.axis_index("y"), lax.axis_index("c")
        chips = [(1 - x, y), (x, 1 - y), (1 - x, 1 - y)]
        copies = [pltpu.make_async_remote_copy(
            src_ref=p_refs[i].at[j], dst_ref=out_refs[i].at[j], send_sem=send_sems.at[3 * i + j],
            recv_sem=recv_sems.at[3 * i + j], device_id=(*chip, c), device_id_type=MESH)
            for i in range(n) for j, chip in enumerate(chips)]
        for cp in copies:
            cp.start()
        for cp in copies:
            cp.wait()

    outs = [jax.ShapeDtypeStruct(b.shape, b.dtype) for b in blocks]
    return _comm_call(name, body, blocks, outs, (3 * n, 3 * n))


def _pair_sum(name, blocks, from_sibling, g_idx, r_idx):
    _, r, c_ = blocks.shape
    tr = _tile(r, 512, 16)

    def body(gi_ref, ri_ref, a_ref, b_ref, own_ref, send_ref):
        k = pl.program_id(1)
        s = a_ref[...] + b_ref[...]

        @pl.when(k == 0)
        def _():
            own_ref[...] = s

        @pl.when(k > 0)
        def _():
            send_ref[...] = s.astype(send_ref.dtype)

    return pl.pallas_call(
        body, name=name,
        grid_spec=pltpu.PrefetchScalarGridSpec(
            num_scalar_prefetch=2, grid=(r // tr, 4),
            in_specs=[pl.BlockSpec((None, tr, c_), lambda i, k, gi, ri: (gi[k], i, 0)),
                      pl.BlockSpec((None, tr, c_), lambda i, k, gi, ri: (ri[k], i, 0))],
            out_specs=[pl.BlockSpec((None, tr, c_), lambda i, k, gi, ri: (0, i, 0)),
                       pl.BlockSpec((None, tr, c_), lambda i, k, gi, ri: (jnp.maximum(k - 1, 0), i, 0))]),
        out_shape=[jax.ShapeDtypeStruct((1, r, c_), F32), jax.ShapeDtypeStruct((3, r, c_), PAYLOAD)],
        compiler_params=_params("parallel", "arbitrary"))(g_idx, r_idx, blocks, from_sibling)


def _adamw(w, g, m, v):
    m = ADAM_B1 * m + (1.0 - ADAM_B1) * g
    v = ADAM_B2 * v + (1.0 - ADAM_B2) * (g * g)
    m_hat = m / (1.0 - ADAM_B1 ** ADAM_STEP)
    v_hat = v / (1.0 - ADAM_B2 ** ADAM_STEP)
    delta = -ADAM_LR * (m_hat / (jnp.sqrt(v_hat) + ADAM_EPS) + ADAM_WD * w)
    return delta, m, v


def _adamw_tiles(r, c_):
    tr = _tile(r, 256, 16)
    return (tr, c_) if tr < r or r <= 256 else (r, _tile(c_, 256))


def _sum_parts(part):
    g = part[0].astype(F32)
    for k in range(1, part.shape[0]):
        g = g + part[k].astype(F32)
    return g


def _sum_adamw(name, parts, w, m, v):
    r, c_ = w.shape
    tr, tc = _adamw_tiles(r, c_)

    def body(p_ref, w_ref, m_ref, v_ref, g_ref, d_ref, nm_ref, nv_ref):
        g = _sum_parts(p_ref)
        g_ref[...] = g
        d_ref[...], nm_ref[...], nv_ref[...] = _adamw(w_ref[...], g, m_ref[...], v_ref[...])

    tile = pl.BlockSpec((tr, tc), lambda i, j: (i, j))
    return pl.pallas_call(body, name=name, grid=(r // tr, c_ // tc),
                          in_specs=[pl.BlockSpec((parts.shape[0], tr, tc), lambda i, j: (0, i, j)), tile, tile, tile],
                          out_specs=[tile] * 4, out_shape=[jax.ShapeDtypeStruct((r, c_), F32)] * 4,
                          compiler_params=_params("parallel", "parallel"))(parts, w, m, v)


def _sum_adamw_layers(name, parts, w, m, v):
    n_layers, r, c_ = w.shape
    tr = _tile(r, 256, 16)

    def body(*refs):
        p_refs = refs[:n_layers]
        w_ref, m_ref, v_ref, g_ref, d_ref, nm_ref, nv_ref = refs[n_layers:]
        layer = pl.program_id(0)
        g = _sum_parts(p_refs[0])
        for li in range(1, n_layers):
            g = jnp.where(layer == li, _sum_parts(p_refs[li]), g)
        g_ref[...] = g
        d_ref[...], nm_ref[...], nv_ref[...] = _adamw(w_ref[...], g, m_ref[...], v_ref[...])

    row = pl.BlockSpec((None, tr, c_), lambda l, i: (l, i, 0))
    specs = [pl.BlockSpec((p.shape[0], tr, c_), lambda l, i: (0, i, 0)) for p in parts]
    return pl.pallas_call(body, name=name, grid=(n_layers, r // tr), in_specs=specs + [row, row, row],
                          out_specs=[row] * 4, out_shape=[jax.ShapeDtypeStruct(w.shape, F32)] * 4,
                          compiler_params=_params("parallel", "parallel"))(*parts, w, m, v)


def _pack_rows(flat, n_rows, cols):
    pad = n_rows * cols - flat.shape[-1]
    flat = jnp.pad(flat, [(0, 0)] * (flat.ndim - 1) + [(0, pad)])
    return flat.reshape(flat.shape[:-1] + (n_rows, cols))


def _cols_join(blocks):
    return jnp.concatenate([blocks[d] for d in range(N_DEV)], axis=1)


def _cols_split(full):
    c = full.shape[1] // N_DEV
    return jnp.stack([full[:, d * c:(d + 1) * c] for d in range(N_DEV)])


def _rows_join(blocks):
    return blocks.reshape(N_DEV * blocks.shape[1], blocks.shape[2])


def _rows_split(full):
    return full.reshape(N_DEV, full.shape[0] // N_DEV, full.shape[1])


def _perm_xbc(a, ng):
    lead = a.shape[:-1]
    di, gn = ng * GW, ng * SSD_D_STATE
    xs = a[..., :di].reshape(lead + (ng, GW))
    bs = a[..., di:di + gn].reshape(lead + (ng, SSD_D_STATE))
    cs = a[..., di + gn:].reshape(lead + (ng, SSD_D_STATE))
    return jnp.concatenate([xs, bs, cs], axis=-1).reshape(lead + (ng * GC,))


def _unperm_xbc(a, ng):
    lead = a.shape[:-1]
    g = a.reshape(lead + (ng, GC))
    return jnp.concatenate([g[..., :GW].reshape(lead + (ng * GW,)),
                            g[..., GW:GW + SSD_D_STATE].reshape(lead + (ng * SSD_D_STATE,)),
                            g[..., GW + SSD_D_STATE:].reshape(lead + (ng * SSD_D_STATE,))], axis=-1)


def _heads_col(v, ng):
    return jnp.pad(v.reshape(ng, 1, SSD_HPG), ((0, 0), (0, 0), (0, LANES - SSD_HPG)))


def _heads_row(v, ng):
    return jnp.pad(v.reshape(ng, SSD_HPG, 1), ((0, 0), (0, 8 - SSD_HPG), (0, 0)))


MATRIX_ITEMS = ("w_in", "w_out", "up0", "down0", "w_qkv", "w_o", "up1", "down1")
VECTOR_ITEMS = ("conv_w", "b_qkv", "b_o")
ITEMS = MATRIX_ITEMS + VECTOR_ITEMS
GATHER_STAGES = (("w_in", "conv_w"), ("w_out", "up0", "down0"), ("w_qkv", "b_qkv", "w_o", "b_o", "up1", "down1"))
SIDE_BY_SIDE = ("conv_w", "up0", "up1", "b_o")


def _items(tree, prefix=""):
    g = lambda k: tree[prefix + k]
    return {"w_in": g("ssd_w_in")[0].T, "w_out": g("ssd_w_out")[0], "w_qkv": g("attn_w_qkv")[0].T,
            "w_o": g("attn_w_o")[0], "up0": g("mlp_w_up")[0], "up1": g("mlp_w_up")[1],
            "down0": g("mlp_w_down")[0], "down1": g("mlp_w_down")[1], "conv_w": g("ssd_conv_w")[0],
            "b_qkv": g("attn_b_qkv"), "b_o": g("attn_b_o")}


def _from_items(it):
    return {"ssd_w_in": it["w_in"][None], "ssd_w_out": it["w_out"][None], "attn_w_qkv": it["w_qkv"].T[None],
            "attn_w_o": it["w_o"][None], "mlp_w_up": jnp.stack([it["up0"], it["up1"]]),
            "mlp_w_down": jnp.stack([it["down0"], it["down1"]]), "ssd_conv_w": it["conv_w"][None],
            "attn_b_qkv": it["b_qkv"], "attn_b_o": it["b_o"]}


REPLICATED = ("ssd_conv_b", "ssd_dt_bias", "ssd_a_log", "ssd_d", "ssd_norm_w", "attn_sinks", "mix_pre_norm",
              "mix_post_norm", "ffn_pre_norm", "ffn_post_norm")
WEIGHTS = ("ssd_w_in", "ssd_conv_w", "ssd_conv_b", "ssd_dt_bias", "ssd_a_log", "ssd_d", "ssd_norm_w", "ssd_w_out",
           "attn_w_qkv", "attn_b_qkv", "attn_sinks", "attn_w_o", "attn_b_o", "mlp_w_up", "mlp_w_down",
           "mix_pre_norm", "mix_post_norm", "ffn_pre_norm", "ffn_post_norm")


def _forward_backward(x, target, rep, token, weights_of_stage, reduce_grads):
    t, d = x.shape
    ng = rep["ssd_norm_w"].shape[1] // GW
    di = ng * GW
    n_xbc = ng * GC
    nh = ng * SSD_HPG
    grads, blocks = {}, {}
    w_up, w_down = [None, None], [None, None]
    sinks_rep = jnp.repeat(rep["attn_sinks"].reshape(ATTN_N_KV, ATTN_REP, 1), ATTN_WINDOW, axis=2).reshape(
        ATTN_N_KV, 1, ATTN_REP * ATTN_WINDOW)
    conv_b = rep["ssd_conv_b"]
    gn = ng * SSD_D_STATE
    parts = ((0, di), (di, di), (2 * di, gn), (2 * di + gn, gn), (di + n_xbc, nh))
    alog_c, dsk_c = (_heads_col(rep[k], ng) for k in ("ssd_a_log", "ssd_d"))
    bias_l, alog_l = (jnp.pad(rep[k], ((0, 0), (0, LANES - nh))) for k in ("ssd_dt_bias", "ssd_a_log"))
    norm = {k: rep[k] for k in ("mix_pre_norm", "mix_post_norm", "ffn_pre_norm", "ffn_post_norm")}

    def nrow(name, i):
        return norm[name][i:i + 1]

    def mlp_fwd(i, u2):
        p = _mm(f"mlp{i}_up", [u2], [w_up[i]], "nn", tm=1024, tn=1024, out_dtypes=(BF16,),
                epilogue=lambda acc: (jnp.square(jnp.maximum(acc, 0.0)),))
        f = _mm(f"mlp{i}_down", [p], [w_down[i]], "nn", tm=512, tn=1024)
        return p, f

    def mlp_bwd(i, df, u2, p):
        da = _mm(f"mlp{i}_dact", [df], [w_down[i]], "nt", tm=1024, tn=1024, out_dtypes=(BF16,),
                 tiles=(p,), epilogue=lambda acc, pv: (acc * (2.0 * jnp.sqrt(pv.astype(F32))),))
        blocks[f"down{i}"] = _rows_split(_mm(f"mlp{i}_dwdown", [p], [df], "tn", tm=512, tn=1024,
                                             out_dtypes=(PAYLOAD,)))
        blocks[f"up{i}"] = _mm(f"mlp{i}_dwup", [u2], [da], "tn", tm=1024, tn=da.shape[1] // N_DEV,
                               out_dtypes=(PAYLOAD,), col_blocks=True)
        return _mm(f"mlp{i}_dx", [da], [w_up[i]], "nt", tm=512, tn=1024)

    u0 = _prenorm("l0_prenorm", x, nrow("mix_pre_norm", 0), token)
    got = weights_of_stage(0, u0)
    w_in_t = _rows_join(got["w_in"])
    w_dt_t = jnp.pad(w_in_t[di + n_xbc:], ((0, LANES - nh), (0, 0)))
    conv_w = got["conv_w"]
    zx = _mm("ssd_in_proj", [u0], [w_in_t], "nt", tm=1024, tn=1024, n_use=di + n_xbc)
    zdt = _mm("ssd_dt_proj", [u0], [w_dt_t], "nt", tm=1024, tn=LANES)
    pre = _conv_fwd(zx, di, n_xbc, conv_w, conv_b)
    dt_c, cum_c, cum_r, sgd_c = _ssd_dt_prep(zdt, bias_l, alog_l, ng)
    y, states = _ssd_fwd(pre, dt_c, cum_c, cum_r, alog_c, dsk_c)
    yn = _gate_norm_fwd(y, zx, rep["ssd_norm_w"])
    got = weights_of_stage(1, yn)
    w_out = _rows_join(got["w_out"])
    w_up[0], w_down[0] = got["up0"], _rows_join(got["down0"])
    mix0 = _mm("ssd_out_proj", [yn], [w_out], "nn", tm=1024, tn=1024)
    h1, u0f = _post_pre("l0_mid", x, mix0, nrow("mix_post_norm", 0), nrow("ffn_pre_norm", 0))
    p0, f0 = mlp_fwd(0, u0f)
    h2, u1 = _post_pre("l1_in", h1, f0, nrow("ffn_post_norm", 0), nrow("mix_pre_norm", 1))
    got = weights_of_stage(2, u1)
    w_qkv_t = _rows_join(got["w_qkv"])
    w_o = _rows_join(got["w_o"])
    b_qkv_col = got["b_qkv"].reshape(-1, 1)
    b_o = got["b_o"]
    w_up[1], w_down[1] = got["up1"], _rows_join(got["down1"])
    qkv_t = _mm("attn_qkv_proj", [w_qkv_t], [u1], "nt", tm=768, tn=1024, out_dtypes=(BF16,), cols=(b_qkv_col,),
                epilogue=lambda acc, b: (acc + b,))
    ao_t = _attn_fwd_t(qkv_t, sinks_rep)
    mix1 = _mm("attn_out_proj", [ao_t], [w_o], "tn", tm=1024, tn=1024, rows=(b_o,),
               epilogue=lambda acc, b: (acc + b,))
    h3, u1f = _post_pre("l1_mid", h2, mix1, nrow("mix_post_norm", 1), nrow("ffn_pre_norm", 1))
    p1, f1 = mlp_fwd(1, u1f)
    dh, loss_row = _final_loss("loss", h3, f1, nrow("ffn_post_norm", 1), target)

    g_norm = {k: [None, None] for k in norm}
    df1, g_norm["ffn_post_norm"][1], _ = _norm_bwd("l1_ffn_post_bwd", dh, post=(f1, nrow("ffn_post_norm", 1)))
    du = mlp_bwd(1, df1, u1f, p1)
    sent = reduce_grads("mlp1", {k: blocks[k] for k in ("up1", "down1")})
    dh, g_norm["ffn_pre_norm"][1], dmix1, g_norm["mix_post_norm"][1], db_o = _norm_bwd(
        "l1_mid_bwd", dh, pre=(du, h3, nrow("ffn_pre_norm", 1)), post=(mix1, nrow("mix_post_norm", 1)), after=sent)
    blocks["b_o"] = _cols_split(db_o)
    blocks["w_o"] = _rows_split(_mm("attn_dwo", [ao_t], [dmix1], "nn", tm=512, tn=1024, out_dtypes=(PAYLOAD,)))
    dao_t = _mm("attn_dout", [w_o], [dmix1], "nt", tm=1024, tn=1024, out_dtypes=(BF16,))
    dqkv_t, db_qkv, grads["attn_sinks"] = _attn_bwd_t(qkv_t, dao_t, sinks_rep)
    blocks["b_qkv"] = db_qkv.reshape(N_DEV, 1, -1)
    blocks["w_qkv"] = _rows_split(_mm("attn_dwqkv", [dqkv_t], [u1], "nn", tm=512, tn=1024, out_dtypes=(PAYLOAD,)))
    du = _mm("attn_dx", [dqkv_t], [w_qkv_t], "tn", tm=1024, tn=1024)
    sent = reduce_grads("attn", {k: blocks[k] for k in ("w_o", "w_qkv", "b_o", "b_qkv")})
    dh, g_norm["mix_pre_norm"][1], df0, g_norm["ffn_post_norm"][0], _ = _norm_bwd(
        "l1_in_bwd", dh, pre=(du, h2, nrow("mix_pre_norm", 1)), post=(f0, nrow("ffn_post_norm", 0)), after=sent)
    du = mlp_bwd(0, df0, u0f, p0)
    sent = reduce_grads("mlp0", {k: blocks[k] for k in ("up0", "down0")})
    dh, g_norm["ffn_pre_norm"][0], dmix0, g_norm["mix_post_norm"][0], _ = _norm_bwd(
        "l0_mid_bwd", dh, pre=(du, h1, nrow("ffn_pre_norm", 0)), post=(mix0, nrow("mix_post_norm", 0)), after=sent)
    blocks["w_out"] = _rows_split(_mm("ssd_dwout", [yn], [dmix0], "tn", tm=512, tn=1024, out_dtypes=(PAYLOAD,)))
    dyn = _mm("ssd_dyn", [dmix0], [w_out], "nt", tm=1024, tn=1024)
    sent = reduce_grads("ssdout", {"w_out": blocks["w_out"]})
    dy, dz, grads["ssd_norm_w"] = _gate_norm_bwd(dyn, y, zx, rep["ssd_norm_w"], sent)
    dpx, dpb, dpc, ddt_g, dbias_g, dalog_g, dd_g = _ssd_bwd(dy, pre, states, dt_c, cum_c, cum_r, sgd_c, alog_c,
                                                             dsk_c)
    conv_out = [_conv_bwd(f"ssd_conv_bwd_{tag}", dp, zx, c0, conv_w[:, c0 - di:c0 - di + n])
                for tag, dp, (c0, n) in zip("xbc", (dpx, dpb, dpc), parts[1:4])]
    dconv_w = jnp.concatenate([o[1] for o in conv_out], axis=1)
    dconv_b = jnp.concatenate([o[2] for o in conv_out], axis=1)
    ddt = jnp.transpose(ddt_g[:, :, :SSD_HPG], (1, 0, 2)).reshape(t, nh)
    ddt = jnp.pad(ddt, ((0, 0), (0, LANES - nh))).astype(BF16)
    blocks["conv_w"] = _cols_split(dconv_w)
    grads["ssd_conv_b"] = dconv_b
    for name, val in (("ssd_dt_bias", dbias_g), ("ssd_a_log", dalog_g), ("ssd_d", dd_g)):
        grads[name] = val[:, 0, :SSD_HPG].reshape(1, nh)
    d_zx = [dz] + [o[0] for o in conv_out] + [ddt]
    dw_parts = [_mm(f"ssd_dw_{tag}", [d], [u0], "tn", tm=512, tn=1024, out_dtypes=(PAYLOAD,))
                for tag, d in zip("zxbct", d_zx)]
    dw_parts[-1] = dw_parts[-1][:nh]
    blocks["w_in"] = _rows_split(jnp.concatenate(dw_parts, axis=0))
    sent = reduce_grads("ssd", {k: blocks[k] for k in ("w_in", "conv_w")})
    w_parts = [w_in_t[r0:r0 + n] for r0, n in parts[:-1]] + [w_dt_t]
    du = _mm("ssd_dx", d_zx, w_parts, "nn", tm=256, tn=1024, after=sent)
    grad_x, g_norm["mix_pre_norm"][0] = _norm_bwd("l0_in_bwd", dh, pre=(du, x, nrow("mix_pre_norm", 0)), after=sent)
    for k in norm:
        grads[k] = jnp.concatenate(g_norm[k], axis=0)
    return loss_row, grad_x, grads


def kernel(x, ssd_w_in, ssd_conv_w, ssd_conv_b, ssd_dt_bias, ssd_a_log, ssd_d, ssd_norm_w, ssd_w_out, attn_w_qkv, attn_b_qkv, attn_sinks, attn_w_o, attn_b_o, mlp_w_up, mlp_w_down, mix_pre_norm, mix_post_norm, ffn_pre_norm, ffn_post_norm, loss_target, m_ssd_w_in, m_ssd_conv_w, m_ssd_conv_b, m_ssd_dt_bias, m_ssd_a_log, m_ssd_d, m_ssd_norm_w, m_ssd_w_out, m_attn_w_qkv, m_attn_b_qkv, m_attn_sinks, m_attn_w_o, m_attn_b_o, m_mlp_w_up, m_mlp_w_down, m_mix_pre_norm, m_mix_post_norm, m_ffn_pre_norm, m_ffn_post_norm, v_ssd_w_in, v_ssd_conv_w, v_ssd_conv_b, v_ssd_dt_bias, v_ssd_a_log, v_ssd_d, v_ssd_norm_w, v_ssd_w_out, v_attn_w_qkv, v_attn_b_qkv, v_attn_sinks, v_attn_w_o, v_attn_b_o, v_mlp_w_up, v_mlp_w_down, v_mix_pre_norm, v_mix_post_norm, v_ffn_pre_norm, v_ffn_post_norm):
    given = dict(locals())
    w = {k: given[k] for k in WEIGHTS}
    mom_m = {k: given["m_" + k] for k in WEIGHTS}
    mom_v = {k: given["v_" + k] for k in WEIGHTS}
    w_it, m_it, v_it = _items(given), _items(given, "m_"), _items(given, "v_")

    order = [k for stage in GATHER_STAGES for k in stage]
    shards = [w_it[k].astype(PAYLOAD) if k in MATRIX_ITEMS else w_it[k] for k in order]
    wide = [k in SIDE_BY_SIDE for k in order]
    g_send, g_recv, shards, lands, token = _gather_start("gather_start", shards, wide)

    def weights_of_stage(s, after):
        first = sum(len(stage) for stage in GATHER_STAGES[:s])
        sl = slice(first, first + len(GATHER_STAGES[s]))
        srcs, got = _gather_wait(f"gather_wait{s}", g_send, g_recv, first, shards[sl], lands[sl], wide[sl], after)
        me = 4 * ix + 2 * iy + ic
        return {k: (lax.dynamic_update_slice(land, src, (0, me * src.shape[1])) if k in SIDE_BY_SIDE else
                    lax.dynamic_update_slice(land, src[None], (me,) + (0,) * src.ndim))
                for k, land, src in zip(GATHER_STAGES[s], got, srcs)}

    ix, iy, ic = lax.axis_index("x"), lax.axis_index("y"), lax.axis_index("c")
    in_flight = []

    def reduce_grads(tag, blocks):
        keys = list(blocks)
        started = _scatter_start(f"rs_start_{tag}", [blocks[k] for k in keys])
        in_flight.append((tag, keys, started))
        return started[-1]

    rep = {k: w[k] for k in REPLICATED}
    loss_row, grad_x, grads = _forward_backward(x[0], loss_target[0], rep, token, weights_of_stage, reduce_grads)

    def pack_rep(tree, last):
        flat = jnp.concatenate([tree[k].reshape(-1) for k in REPLICATED] + [last])
        return _pack_rows(flat, _round_up(-(-flat.shape[0] // LANES), 8), LANES)

    landed = {}
    me = 4 * ix + 2 * iy + ic

    def wait_group(group, after):
        tag, keys, (s_send, s_recv, srcs, s_lands, _) = group
        srcs, got = _scatter_wait(f"rs_wait_{tag}", s_send, s_recv, srcs, s_lands, after)
        for k, src, land in zip(keys, srcs, got):
            own = lax.dynamic_index_in_dim(src, me, 0, keepdims=True)
            landed[k] = lax.dynamic_update_slice(land, own, (me,) + (0,) * (land.ndim - 1))

    def adamw_item(k):
        return _sum_adamw(f"adamw_{k}", landed[k], w_it[k], m_it[k], v_it[k])

    def adamw_stack(name, keys):
        return _sum_adamw_layers(f"adamw_{name}", [landed[k] for k in keys], given[name], given["m_" + name],
                                 given["v_" + name])

    for group in in_flight[:-1]:
        wait_group(group, grad_x)
    done = {"mlp_w_up": adamw_stack("mlp_w_up", ("up0", "up1")),
            "mlp_w_down": adamw_stack("mlp_w_down", ("down0", "down1")),
            "attn_w_qkv": [o.T[None] for o in adamw_item("w_qkv")],
            "attn_w_o": [o[None] for o in adamw_item("w_o")],
            "attn_b_qkv": adamw_item("b_qkv"), "attn_b_o": adamw_item("b_o"),
            "ssd_w_out": [o[None] for o in adamw_item("w_out")]}
    partials, = _all_gather("gather_small_grads", [pack_rep(grads, loss_row[0, :1])],
                            [outs4[0] for outs4 in done.values()])
    wait_group(in_flight[-1], partials)
    done["ssd_w_in"] = [o.T[None] for o in adamw_item("w_in")]
    done["ssd_conv_w"] = [o[None] for o in adamw_item("conv_w")]
    zero = jnp.zeros((1,), F32)
    rep_out = _sum_adamw("adamw_replicated", partials, pack_rep(w, zero), pack_rep(mom_m, zero), pack_rep(mom_v, zero))

    kinds = []
    for kind, r_arr in enumerate(rep_out):
        tree = {name: outs4[kind] for name, outs4 in done.items()}
        flat, off = r_arr.reshape(-1), 0
        for k in REPLICATED:
            tree[k] = flat[off:off + w[k].size].reshape(w[k].shape)
            off += w[k].size
        kinds.append(tree)
    loss = rep_out[0].reshape(-1)[off]
    outs = [loss, grad_x[None]]
    for tree in kinds:
        outs += [tree[k] for k in WEIGHTS]
    return tuple(outs)
```

```python
import functools

import jax
import jax.numpy as jnp
from jax import lax
from jax.experimental import pallas as pl
from jax.experimental.pallas import tpu as pltpu

F32 = jnp.float32
BF16 = jnp.bfloat16
PAYLOAD = jnp.bfloat16
HIGHEST = lax.Precision.HIGHEST
MESH = pl.DeviceIdType.MESH

NORM_EPS = 1e-6
SSD_HEAD_DIM = 64
SSD_N_GROUPS = 8
SSD_HPG = 4
SSD_D_STATE = 128
SSD_CONV_WIDTH = 4
SSD_CHUNK = 128
ATTN_HEAD_DIM = 64
ATTN_N_KV = 4
ATTN_REP = 4
ATTN_WINDOW = 128
ADAM_LR = 0.001
ADAM_B1 = 0.9
ADAM_B2 = 0.999
ADAM_EPS = 1e-08
ADAM_WD = 0.01
ADAM_STEP = 10

N_DEV = 8
LANES = 128
PACK_COLS = 1024
V7X_VMEM_LIMIT = 56 * 1024 * 1024

GW = SSD_HPG * SSD_HEAD_DIM
GC = GW + 2 * SSD_D_STATE


def _params(*sem):
    return pltpu.CompilerParams(dimension_semantics=sem, vmem_limit_bytes=V7X_VMEM_LIMIT)


def _tile(n, pref, mult=LANES):
    best = None
    t = mult
    while t <= min(n, pref):
        if n % t == 0:
            best = t
        t += mult
    return best if best is not None else n


def _round_up(n, m):
    return (n + m - 1) // m * m


def _acc(ref, val, first):
    @pl.when(first)
    def _():
        ref[...] = val

    @pl.when(jnp.logical_not(first))
    def _():
        ref[...] += val


def _dot(a, b):
    return lax.dot_general(a, b, (((1,), (0,)), ((), ())), preferred_element_type=F32)


def _dot_nt(a, b):
    return lax.dot_general(a, b, (((1,), (1,)), ((), ())), preferred_element_type=F32)


def _dot_tn(a, b):
    return lax.dot_general(a, b, (((0,), (0,)), ((), ())), preferred_element_type=F32)


def _dot_f32(a, b):
    return lax.dot_general(a, b, (((1,), (0,)), ((), ())), preferred_element_type=F32, precision=HIGHEST)


_DOTS = {"nn": _dot, "nt": _dot_nt, "tn": _dot_tn}


def _sigmoid(x):
    return 1.0 / (1.0 + jnp.exp(-x))


def _softplus(x):
    return jnp.maximum(x, 0.0) + jnp.log1p(jnp.exp(-jnp.abs(x)))


def _silu_grad(x, s):
    return s * (1.0 + x * (1.0 - s))


def _mm(name, a_list, b_list, mode, *, tm, tn, out_dtypes=(F32,), epilogue=None, tiles=(), rows=(), cols=(),
        col_blocks=False, n_use=None, after=None):
    npair = len(a_list)
    if mode == "tn":
        m = a_list[0].shape[1]
    else:
        m = a_list[0].shape[0]
    n = n_use if n_use is not None else (b_list[0].shape[0] if mode == "nt" else b_list[0].shape[1])
    tm = _tile(m, tm, LANES if mode == "tn" else 8)
    tn = _tile(n, tn)
    assert m % tm == 0 and n % tn == 0, (name, m, n, tm, tn)
    dot = _DOTS[mode]

    def body(*refs):
        a_refs = refs[:npair]
        b_refs = refs[npair:2 * npair]
        n_extra = len(tiles) + len(rows) + len(cols)
        e_refs = refs[2 * npair:2 * npair + n_extra]
        o_refs = refs[2 * npair + n_extra + len(order):]
        acc = None
        for ar, br in zip(a_refs, b_refs):
            d = dot(ar[...], br[...])
            acc = d if acc is None else acc + d
        outs = epilogue(acc, *[e[...] for e in e_refs]) if epilogue is not None else (acc,)
        for o, v in zip(o_refs, outs):
            o[...] = v.astype(o.dtype)

    in_specs = []
    for a in a_list:
        if mode == "tn":
            in_specs.append(pl.BlockSpec((a.shape[0], tm), lambda i, j: (0, i)))
        else:
            in_specs.append(pl.BlockSpec((tm, a.shape[1]), lambda i, j: (i, 0)))
    for b in b_list:
        if mode == "nt":
            in_specs.append(pl.BlockSpec((tn, b.shape[1]), lambda i, j: (j, 0)))
        else:
            in_specs.append(pl.BlockSpec((b.shape[0], tn), lambda i, j: (0, j)))
    in_specs += [pl.BlockSpec((tm, tn), lambda i, j: (i, j)) for _ in tiles]
    in_specs += [pl.BlockSpec((1, tn), lambda i, j: (0, j)) for _ in rows]
    in_specs += [pl.BlockSpec((tm, 1), lambda i, j: (i, 0)) for _ in cols]
    order = [] if after is None else [after]
    in_specs += [pl.BlockSpec((8, LANES), lambda i, j: (0, 0)) for _ in order]
    outs = pl.pallas_call(
        body,
        name=name,
        grid=(m // tm, n // tn),
        in_specs=in_specs,
        out_specs=[pl.BlockSpec((None, tm, tn), lambda i, j: (j, i, 0)) if col_blocks else
                   pl.BlockSpec((tm, tn), lambda i, j: (i, j)) for _ in out_dtypes],
        out_shape=[jax.ShapeDtypeStruct((n // tn, m, tn) if col_blocks else (m, n), dt) for dt in out_dtypes],
        compiler_params=_params("parallel", "parallel"),
    )(*a_list, *b_list, *tiles, *rows, *cols, *order)
    return outs[0] if len(out_dtypes) == 1 else outs


def _rms(x, w):
    r = lax.rsqrt(jnp.mean(x * x, axis=-1, keepdims=True) + NORM_EPS)
    return x * r * w


def _rms_bwd(x, w, dy):
    r = lax.rsqrt(jnp.mean(x * x, axis=-1, keepdims=True) + NORM_EPS)
    xh = x * r
    g = dy * w
    dx = r * (g - xh * jnp.mean(g * xh, axis=-1, keepdims=True))
    return dx, dy * xh


def _row_specs(tr, d):
    return pl.BlockSpec((tr, d), lambda i: (i, 0)), pl.BlockSpec((1, d), lambda i: (0, 0))


def _prenorm(name, h, w, after):
    t, d = h.shape
    tr = _tile(t, 512, 8)
    row, vec = _row_specs(tr, d)

    def body(h_ref, w_ref, after_ref, u_ref):
        u_ref[...] = _rms(h_ref[...], w_ref[...]).astype(BF16)

    return pl.pallas_call(body, name=name, grid=(t // tr,),
                          in_specs=[row, vec, pl.BlockSpec((8, LANES), lambda i: (0, 0))], out_specs=row,
                          out_shape=jax.ShapeDtypeStruct((t, d), BF16), compiler_params=_params("parallel"))(
                              h, w, after)


def _post_pre(name, h, m, w_post, w_pre):
    t, d = h.shape
    tr = _tile(t, 512, 8)
    row, vec = _row_specs(tr, d)

    def body(h_ref, m_ref, wq_ref, wp_ref, hn_ref, u_ref):
        hn = h_ref[...] + _rms(m_ref[...], wq_ref[...])
        hn_ref[...] = hn
        u_ref[...] = _rms(hn, wp_ref[...]).astype(BF16)

    return pl.pallas_call(body, name=name, grid=(t // tr,), in_specs=[row, row, vec, vec], out_specs=[row, row],
                          out_shape=[jax.ShapeDtypeStruct((t, d), F32), jax.ShapeDtypeStruct((t, d), BF16)],
                          compiler_params=_params("parallel"))(h, m, w_post, w_pre)


def _final_loss(name, h, m, w_post, target):
    t, d = h.shape
    tr = _tile(t, 512, 8)
    row, vec = _row_specs(tr, d)

    def body(h_ref, m_ref, wq_ref, t_ref, dh_ref, loss_ref):
        err = h_ref[...] + _rms(m_ref[...], wq_ref[...]) - t_ref[...]
        dh_ref[...] = err * (1.0 / d)
        part = 0.5 * jnp.sum(jnp.mean(err * err, axis=-1, keepdims=True), axis=0, keepdims=True)
        _acc(loss_ref, jnp.broadcast_to(part, (1, LANES)), pl.program_id(0) == 0)

    return pl.pallas_call(body, name=name, grid=(t // tr,), in_specs=[row, row, vec, row],
                          out_specs=[row, pl.BlockSpec((1, LANES), lambda i: (0, 0))],
                          out_shape=[jax.ShapeDtypeStruct((t, d), F32), jax.ShapeDtypeStruct((1, LANES), F32)],
                          compiler_params=_params("arbitrary"))(h, m, w_post, target)


def _norm_bwd(name, dh, pre=None, post=None, after=None):
    t, d = dh.shape
    tr = _tile(t, 256, 8)
    row, vec = _row_specs(tr, d)
    has_pre, has_post = pre is not None, post is not None

    def body(*refs):
        it = iter(refs)
        dh_ref = next(it)
        if has_pre:
            du_ref, x_ref, wp_ref = next(it), next(it), next(it)
        if has_post:
            m_ref, wq_ref = next(it), next(it)
        if after is not None:
            next(it)
        first = pl.program_id(0) == 0
        dh_v = dh_ref[...]
        if has_pre:
            dhn_ref, dwp_ref = next(it), next(it)
            dx, dwr = _rms_bwd(x_ref[...], wp_ref[...], du_ref[...])
            dh_v = dh_v + dx
            dhn_ref[...] = dh_v
            _acc(dwp_ref, jnp.sum(dwr, axis=0, keepdims=True), first)
        if has_post:
            dm_ref, dwq_ref, dms_ref = next(it), next(it), next(it)
            dm, dwr = _rms_bwd(m_ref[...], wq_ref[...], dh_v)
            dm_ref[...] = dm.astype(BF16)
            _acc(dwq_ref, jnp.sum(dwr, axis=0, keepdims=True), first)
            _acc(dms_ref, jnp.sum(dm, axis=0, keepdims=True), first)

    ins, in_specs, out_specs, out_shape = [dh], [row], [], []
    if has_pre:
        ins += list(pre)
        in_specs += [row, row, vec]
        out_specs += [row, vec]
        out_shape += [jax.ShapeDtypeStruct((t, d), F32), jax.ShapeDtypeStruct((1, d), F32)]
    if has_post:
        ins += list(post)
        in_specs += [row, vec]
        out_specs += [row, vec, vec]
        out_shape += [jax.ShapeDtypeStruct((t, d), BF16), jax.ShapeDtypeStruct((1, d), F32),
                      jax.ShapeDtypeStruct((1, d), F32)]
    if after is not None:
        ins.append(after)
        in_specs.append(pl.BlockSpec((8, LANES), lambda i: (0, 0)))
    return pl.pallas_call(body, name=name, grid=(t // tr,), in_specs=in_specs, out_specs=out_specs,
                          out_shape=out_shape, compiler_params=_params("arbitrary"))(*ins)


HALO = 8


def _shift_later(cur, prev, s):
    rolled = pltpu.roll(cur, s, 0)
    row = lax.broadcasted_iota(jnp.int32, prev.shape, 0)
    first = jnp.where(row < s, pltpu.roll(prev, s, 0), rolled[0:HALO])
    return jnp.concatenate([first, rolled[HALO:]], axis=0)


def _shift_earlier(cur, nxt, s):
    tt = cur.shape[0]
    rolled = pltpu.roll(cur, tt - s, 0)
    row = lax.broadcasted_iota(jnp.int32, nxt.shape, 0)
    last = jnp.where(row >= HALO - s, pltpu.roll(nxt, HALO - s, 0), rolled[tt - HALO:])
    return jnp.concatenate([rolled[:tt - HALO], last], axis=0)


def _conv_fwd(zx, col0, n_ch, conv_w, conv_b):
    t = zx.shape[0]
    tc = _tile(n_ch, 512)
    tt = _tile(t, 1024, 8)
    cb0 = col0 // tc
    assert col0 % tc == 0
    kw = SSD_CONV_WIDTH

    def body(x_ref, p_ref, w_ref, b_ref, o_ref):
        cur = x_ref[...]
        prev = jnp.where(pl.program_id(1) > 0, p_ref[...], 0.0)
        w = w_ref[...]
        acc = b_ref[...] + w[kw - 1:kw, :] * cur
        for k in range(kw - 1):
            acc = acc + w[k:k + 1, :] * _shift_later(cur, prev, kw - 1 - k)
        o_ref[...] = acc

    return pl.pallas_call(
        body, name="ssd_conv_fwd", grid=(n_ch // tc, t // tt),
        in_specs=[pl.BlockSpec((tt, tc), lambda j, i: (i, cb0 + j)),
                  pl.BlockSpec((HALO, tc), lambda j, i: (jnp.maximum(i * (tt // HALO) - 1, 0), cb0 + j)),
                  pl.BlockSpec((kw, tc), lambda j, i: (0, j)),
                  pl.BlockSpec((1, tc), lambda j, i: (0, j))],
        out_specs=pl.BlockSpec((tt, tc), lambda j, i: (i, j)),
        out_shape=jax.ShapeDtypeStruct((t, n_ch), F32),
        compiler_params=_params("parallel", "parallel"))(zx, zx, conv_w, conv_b)


def _conv_bwd(name, dpre, zx, col0, conv_w):
    t, n_ch = dpre.shape
    tc = _tile(n_ch, 512)
    tt = _tile(t, 1024, 8)
    cb0 = col0 // tc
    kw = SSD_CONV_WIDTH
    nt = t // tt

    def body(d_ref, dn_ref, x_ref, p_ref, w_ref, dx_ref, dw_ref, db_ref):
        i = pl.program_id(1)
        d = d_ref[...]
        d_next = jnp.where(i < nt - 1, dn_ref[...], 0.0)
        x = x_ref[...]
        x_prev = jnp.where(i > 0, p_ref[...], 0.0)
        w = w_ref[...]
        dx = w[kw - 1:kw, :] * d
        for k in range(kw - 1):
            dx = dx + w[k:k + 1, :] * _shift_earlier(d, d_next, kw - 1 - k)
        dx_ref[...] = dx.astype(BF16)
        first = i == 0
        for k in range(kw):
            xs = x if k == kw - 1 else _shift_later(x, x_prev, kw - 1 - k)
            val = jnp.sum(d * xs, axis=0, keepdims=True)

            @pl.when(first)
            def _():
                dw_ref[k:k + 1, :] = val

            @pl.when(jnp.logical_not(first))
            def _():
                dw_ref[k:k + 1, :] += val
        _acc(db_ref, jnp.sum(d, axis=0, keepdims=True), first)

    return pl.pallas_call(
        body, name=name, grid=(n_ch // tc, nt),
        in_specs=[pl.BlockSpec((tt, tc), lambda j, i: (i, j)),
                  pl.BlockSpec((HALO, tc), lambda j, i: (jnp.minimum((i + 1) * (tt // HALO), t // HALO - 1), j)),
                  pl.BlockSpec((tt, tc), lambda j, i: (i, cb0 + j)),
                  pl.BlockSpec((HALO, tc), lambda j, i: (jnp.maximum(i * (tt // HALO) - 1, 0), cb0 + j)),
                  pl.BlockSpec((kw, tc), lambda j, i: (0, j))],
        out_specs=[pl.BlockSpec((tt, tc), lambda j, i: (i, j)),
                   pl.BlockSpec((kw, tc), lambda j, i: (0, j)),
                   pl.BlockSpec((1, tc), lambda j, i: (0, j))],
        out_shape=[jax.ShapeDtypeStruct((t, n_ch), BF16), jax.ShapeDtypeStruct((kw, n_ch), F32),
                   jax.ShapeDtypeStruct((1, n_ch), F32)],
        compiler_params=_params("parallel", "arbitrary"))(dpre, dpre, zx, zx, conv_w)


def _head_of_lane(shape, width):
    return lax.broadcasted_iota(jnp.int32, shape, len(shape) - 1) // width


def _expand(v, n_rows):
    head = _head_of_lane((n_rows, GW), SSD_HEAD_DIM)
    out = jnp.zeros((n_rows, GW), F32)
    for j in range(SSD_HPG):
        out = jnp.where(head == j, v[:, j:j + 1], out)
    return out


def _contract(v, n_rows):
    head = _head_of_lane((n_rows, GW), SSD_HEAD_DIM)
    lane = lax.broadcasted_iota(jnp.int32, (n_rows, LANES), 1)
    out = jnp.zeros((n_rows, LANES), F32)
    for j in range(SSD_HPG):
        s = jnp.sum(jnp.where(head == j, v, 0.0), axis=1, keepdims=True)
        out = jnp.where(lane == j, s, out)
    return out


def _ssd_dt_prep(zdt, bias, alog, ng):
    t = zdt.shape[0]
    q = SSD_CHUNK

    def body(z_ref, b_ref, a_ref, dt_ref, cum_ref, cumr_ref, sg_ref):
        raw = z_ref[...] + b_ref[...]
        dt = _softplus(raw)
        sgd = _sigmoid(raw)
        row = lax.broadcasted_iota(jnp.int32, (q, q), 0)
        col = lax.broadcasted_iota(jnp.int32, (q, q), 1)
        cum = _dot_f32((col <= row).astype(F32), dt * (-jnp.exp(a_ref[...])))
        cum_t = cum.T
        lane = lax.broadcasted_iota(jnp.int32, (q, LANES), 1)
        for g in range(ng):
            shift = (LANES - g * SSD_HPG) % LANES

            def group(v):
                return jnp.where(lane < SSD_HPG, pltpu.roll(v, shift, 1) if shift else v, 0.0)

            dt_ref[g] = group(dt)
            cum_ref[g] = group(cum)
            sg_ref[g] = group(sgd)
            cumr_ref[g] = (pltpu.roll(cum_t, shift, 0) if shift else cum_t)[0:8, :]

    cols = pl.BlockSpec((ng, q, LANES), lambda c: (0, c, 0))
    vec = pl.BlockSpec((1, LANES), lambda c: (0, 0))
    col_shape = jax.ShapeDtypeStruct((ng, t, LANES), F32)
    return pl.pallas_call(body, name="ssd_dt_prep", grid=(t // q,),
                          in_specs=[pl.BlockSpec((q, LANES), lambda c: (c, 0)), vec, vec],
                          out_specs=[cols, cols, pl.BlockSpec((ng, 8, q), lambda c: (0, 0, c)), cols],
                          out_shape=[col_shape, col_shape, jax.ShapeDtypeStruct((ng, 8, t), F32), col_shape],
                          compiler_params=_params("parallel"))(zdt, bias, alog)


def _ssd_common(pre, dt, cum, cum_r, alog_c):
    q = SSD_CHUNK
    sg = _sigmoid(pre)
    act = pre * sg
    xa = act[:, :GW]
    bm = act[:, GW:GW + SSD_D_STATE].astype(BF16)
    cm = act[:, GW + SSD_D_STATE:].astype(BF16)
    row = lax.broadcasted_iota(jnp.int32, (q, q), 0)
    col = lax.broadcasted_iota(jnp.int32, (q, q), 1)
    tril = col <= row
    a_c = -jnp.exp(alog_c)
    g = _dot_nt(cm, bm)
    dt_x = _expand(dt, q)
    xdt = xa * dt_x
    cl = cum[q - 1:q, :]
    e_c = jnp.exp(cl - cum)
    lam_c = jnp.exp(cum)
    return dict(sg=sg, xa=xa, bm=bm, cm=cm, tril=tril, row=row, col=col, dt=dt, a_c=a_c, cum=cum, cum_r=cum_r,
                g=g, dt_x=dt_x, xdt=xdt, cl=cl, e_c=e_c, lam_c=lam_c)


SSD_GPS_FWD = 4
SSD_GPS_BWD = 2


def _ssd_specs(nc, rev, ng, gps):
    q = SSD_CHUNK
    xw, nw = gps * GW, gps * SSD_D_STATE
    b_off = ng * GW // nw
    c_off = (ng * GW + ng * SSD_D_STATE) // nw
    assert ng % gps == 0 and (ng * GW) % nw == 0 and (ng * SSD_D_STATE) % nw == 0

    def ch(c):
        return nc - 1 - c if rev else c

    chunk_grp = [pl.BlockSpec((q, xw), lambda g, c: (ch(c), g)),
                 pl.BlockSpec((q, nw), lambda g, c: (ch(c), b_off + g)),
                 pl.BlockSpec((q, nw), lambda g, c: (ch(c), c_off + g))]
    col_form = pl.BlockSpec((gps, q, LANES), lambda g, c: (g, ch(c), 0))
    row_form = pl.BlockSpec((gps, 8, q), lambda g, c: (g, 0, ch(c)))
    col_par = pl.BlockSpec((gps, 1, LANES), lambda g, c: (g, 0, 0))
    y_spec = pl.BlockSpec((q, xw), lambda g, c: (ch(c), g))
    st_spec = pl.BlockSpec((gps, None, GW, SSD_D_STATE), lambda g, c: (g, ch(c), 0, 0))
    bc_spec = pl.BlockSpec((q, nw), lambda g, c: (ch(c), g))
    return chunk_grp, col_form, row_form, col_par, y_spec, st_spec, bc_spec


def _ssd_group_views(gi, wide, narrow, stacked):
    xs, ns = pl.ds(gi * GW, GW), pl.ds(gi * SSD_D_STATE, SSD_D_STATE)
    return [r.at[:, xs] for r in wide], [r.at[:, ns] for r in narrow], [r.at[gi] for r in stacked]


def _ssd_fwd(pre, dt_c, cum_c, cum_r, alog_c, dsk_c):
    t = pre.shape[0]
    ng = pre.shape[1] // GC
    q = SSD_CHUNK
    nc = t // q
    gps = SSD_GPS_FWD if ng % SSD_GPS_FWD == 0 else SSD_GPS_BWD
    chunk_grp, col_form, row_form, col_par, y_spec, st_spec, _ = _ssd_specs(nc, False, ng, gps)

    def body(px_ref, pb_ref, pc_ref, dt_ref, cum_ref, cumr_ref, ac_ref, dk_ref, y_ref, sp_ref, st_ref):
        @pl.when(pl.program_id(1) == 0)
        def _():
            st_ref[...] = jnp.zeros_like(st_ref)

        for gi in range(gps):
            (px, y), (pb, pc), rest = _ssd_group_views(
                gi, (px_ref, y_ref), (pb_ref, pc_ref), (dt_ref, cum_ref, cumr_ref, ac_ref, dk_ref, sp_ref, st_ref))
            one_group(px, pb, pc, *rest[:5], y, *rest[5:])

    def one_group(px_ref, pb_ref, pc_ref, dt_ref, cum_ref, cumr_ref, ac_ref, dk_ref, y_ref, sp_ref, st_ref):
        pre_v = jnp.concatenate([px_ref[...], pb_ref[...], pc_ref[...]], axis=1)
        v = _ssd_common(pre_v, dt_ref[...], cum_ref[...], cumr_ref[...], ac_ref[...])
        s0 = st_ref[...]
        sp_ref[...] = s0
        r = _dot_nt(v["cm"], s0.astype(BF16))
        y = _expand(v["lam_c"], q) * r + _expand(dk_ref[...], 1) * v["xa"]
        head = _head_of_lane((q, GW), SSD_HEAD_DIM)
        for j in range(SSD_HPG):
            diff = v["cum"][:, j:j + 1] - v["cum_r"][j:j + 1, :]
            w = (v["g"] * jnp.exp(jnp.where(v["tril"], diff, -jnp.inf))).astype(BF16)
            y = y + _dot(w, jnp.where(head == j, v["xdt"], 0.0).astype(BF16))
        y_ref[...] = y
        ds = _dot_tn((v["xdt"] * _expand(v["e_c"], q)).astype(BF16), v["bm"])
        for j in range(SSD_HPG):
            rows = slice(j * SSD_HEAD_DIM, (j + 1) * SSD_HEAD_DIM)
            st_ref[rows, :] = s0[rows, :] * jnp.exp(v["cum_r"][j:j + 1, q - 1:q]) + ds[rows, :]

    return pl.pallas_call(
        body, name="ssd_scan_fwd", grid=(ng // gps, nc),
        in_specs=chunk_grp + [col_form, col_form, row_form, col_par, col_par],
        out_specs=[y_spec, st_spec],
        out_shape=[jax.ShapeDtypeStruct((t, ng * GW), F32), jax.ShapeDtypeStruct((ng, nc, GW, SSD_D_STATE), F32)],
        scratch_shapes=[pltpu.VMEM((gps, GW, SSD_D_STATE), F32)],
        compiler_params=_params("parallel", "arbitrary"))(pre, pre, pre, dt_c, cum_c, cum_r, alog_c, dsk_c)


def _ssd_bwd(dy, pre, states, dt_c, cum_c, cum_r, sgd_c, alog_c, dsk_c):
    t = pre.shape[0]
    ng = pre.shape[1] // GC
    q = SSD_CHUNK
    nc = t // q
    gps = SSD_GPS_BWD
    chunk_grp, col_form, row_form, col_par, y_spec, st_spec, bc_spec = _ssd_specs(nc, True, ng, gps)

    def body(dy_ref, px_ref, pb_ref, pc_ref, sp_ref, dt_ref, cum_ref, cumr_ref, sgd_ref, ac_ref, dk_ref,
             dpx_ref, dpb_ref, dpc_ref, ddt_ref, dbias_ref, dalog_ref, dd_ref, ds_ref):
        @pl.when(pl.program_id(1) == 0)
        def _():
            ds_ref[...] = jnp.zeros_like(ds_ref)

        for gi in range(gps):
            (dy, px, dpx), (pb, pc, dpb, dpc), rest = _ssd_group_views(
                gi, (dy_ref, px_ref, dpx_ref), (pb_ref, pc_ref, dpb_ref, dpc_ref),
                (sp_ref, dt_ref, cum_ref, cumr_ref, sgd_ref, ac_ref, dk_ref, ddt_ref, dbias_ref, dalog_ref, dd_ref,
                 ds_ref))
            one_group(dy, px, pb, pc, *rest[:7], dpx, dpb, dpc, *rest[7:])

    def one_group(dy_ref, px_ref, pb_ref, pc_ref, sp_ref, dt_ref, cum_ref, cumr_ref, sgd_ref, ac_ref, dk_ref,
                  dpx_ref, dpb_ref, dpc_ref, ddt_ref, dbias_ref, dalog_ref, dd_ref, ds_ref):
        first = pl.program_id(1) == 0
        pre_v = jnp.concatenate([px_ref[...], pb_ref[...], pc_ref[...]], axis=1)
        v = _ssd_common(pre_v, dt_ref[...], cum_ref[...], cumr_ref[...], ac_ref[...])
        xa, bm, cm, xdt, cum, cum_r = v["xa"], v["bm"], v["cm"], v["xdt"], v["cum"], v["cum_r"]
        xdt_b = xdt.astype(BF16)
        dy_v = dy_ref[...]
        s0 = sp_ref[...]
        ds1 = ds_ref[...]
        s0b, ds1b = s0.astype(BF16), ds1.astype(BF16)
        head = _head_of_lane((q, GW), SSD_HEAD_DIM)
        lane = lax.broadcasted_iota(jnp.int32, (q, LANES), 1)
        lane1 = lax.broadcasted_iota(jnp.int32, (1, LANES), 1)
        lam_x = _expand(v["lam_c"], q)
        e_x = _expand(v["e_c"], q)

        dxa = _expand(dk_ref[...], 1) * dy_v
        dd = _contract(jnp.sum(dy_v * xa, axis=0, keepdims=True), 1)
        r = _dot_nt(cm, s0b)
        dcum = _contract(dy_v * r * lam_x, q)
        drb = (lam_x * dy_v).astype(BF16)
        dc = _dot(drb, s0b)
        ds0 = _dot_tn(drb, cm)
        extra = jnp.zeros((1, LANES), F32)
        for j in range(SSD_HPG):
            rows = slice(j * SSD_HEAD_DIM, (j + 1) * SSD_HEAD_DIM)
            lam_last = jnp.exp(cum_r[j:j + 1, q - 1:q])
            ds_ref[rows, :] = ds0[rows, :] + lam_last * ds1[rows, :]
            tot = jnp.sum(jnp.sum(ds1[rows, :] * s0[rows, :], axis=1, keepdims=True), axis=0, keepdims=True)
            extra = jnp.where(lane1 == j, lam_last * tot, extra)
        dv = _dot_nt(bm, ds1b)
        db = _dot((xdt * e_x).astype(BF16), ds1b)
        dxdt = e_x * dv
        dee = _contract(dv * xdt, q) * v["e_c"]
        dcum = dcum - dee
        extra = extra + jnp.sum(dee, axis=0, keepdims=True)
        dg = jnp.zeros((q, q), F32)
        for j in range(SSD_HPG):
            diff = cum[:, j:j + 1] - cum_r[j:j + 1, :]
            el = jnp.exp(jnp.where(v["tril"], diff, -jnp.inf))
            gl = v["g"] * el
            dym = jnp.where(head == j, dy_v, 0.0).astype(BF16)
            dwm = _dot_nt(dym, xdt_b)
            dxdt = dxdt + _dot_tn(gl.astype(BF16), dym)
            z = dwm * gl
            rk = jnp.sum(z, axis=1, keepdims=True) - jnp.sum(z.T, axis=1, keepdims=True)
            dcum = jnp.where(lane == j, dcum + rk, dcum)
            dg = dg + dwm * el
        dgb = dg.astype(BF16)
        dc = dc + _dot(dgb, bm)
        db = db + _dot_tn(dgb, cm)
        da = _dot_f32((v["row"] <= v["col"]).astype(F32), dcum) + extra
        ddt = _contract(dxdt * xa, q) + v["a_c"] * da
        dalog = jnp.sum(v["dt"] * da, axis=0, keepdims=True) * v["a_c"]
        dxa = dxa + v["dt_x"] * dxdt
        ddt_raw = jnp.where(lane < SSD_HPG, ddt * sgd_ref[...], 0.0)
        sgrad = _silu_grad(pre_v, v["sg"])
        dpx_ref[...] = dxa * sgrad[:, :GW]
        dpb_ref[...] = db * sgrad[:, GW:GW + SSD_D_STATE]
        dpc_ref[...] = dc * sgrad[:, GW + SSD_D_STATE:]
        ddt_ref[...] = ddt_raw
        _acc(dbias_ref, jnp.sum(ddt_raw, axis=0, keepdims=True), first)
        _acc(dalog_ref, jnp.where(lane1 < SSD_HPG, dalog, 0.0), first)
        _acc(dd_ref, dd, first)

    return pl.pallas_call(
        body, name="ssd_scan_bwd", grid=(ng // gps, nc),
        in_specs=[y_spec] + chunk_grp + [st_spec, col_form, col_form, row_form, col_form, col_par, col_par],
        out_specs=[y_spec, bc_spec, bc_spec, col_form, col_par, col_par, col_par],
        out_shape=[jax.ShapeDtypeStruct((t, ng * GW), F32), jax.ShapeDtypeStruct((t, ng * SSD_D_STATE), F32),
                   jax.ShapeDtypeStruct((t, ng * SSD_D_STATE), F32), jax.ShapeDtypeStruct((ng, t, LANES), F32),
                   jax.ShapeDtypeStruct((ng, 1, LANES), F32), jax.ShapeDtypeStruct((ng, 1, LANES), F32),
                   jax.ShapeDtypeStruct((ng, 1, LANES), F32)],
        scratch_shapes=[pltpu.VMEM((gps, GW, SSD_D_STATE), F32)],
        compiler_params=_params("parallel", "arbitrary"))(dy, pre, pre, pre, states, dt_c, cum_c, cum_r, sgd_c, alog_c,
                                                           dsk_c)


def _gate_norm_fwd(y, zx, norm_w):
    t, di = y.shape
    tr = _tile(t, 256, 8)
    ng = di // GW

    def body(y_ref, z_ref, w_ref, o_ref):
        z = z_ref[...]
        gate = y_ref[...] * (z * _sigmoid(z))
        w = w_ref[...]
        for g in range(ng):
            cols = slice(g * GW, (g + 1) * GW)
            gs = gate[:, cols]
            r = lax.rsqrt(jnp.mean(gs * gs, axis=-1, keepdims=True) + NORM_EPS)
            o_ref[:, cols] = (gs * r * w[:, cols]).astype(BF16)

    row = pl.BlockSpec((tr, di), lambda i: (i, 0))
    return pl.pallas_call(body, name="ssd_gate_norm_fwd", grid=(t // tr,),
                          in_specs=[row, row, pl.BlockSpec((1, di), lambda i: (0, 0))], out_specs=row,
                          out_shape=jax.ShapeDtypeStruct((t, di), BF16), compiler_params=_params("parallel"))(
                              y, zx, norm_w)


def _gate_norm_bwd(dyn, y, zx, norm_w, after):
    t, di = y.shape
    tr = _tile(t, 256, 8)
    ng = di // GW

    def body(d_ref, y_ref, z_ref, w_ref, after_ref, dy_ref, dz_ref, dw_ref):
        z = z_ref[...]
        yv = y_ref[...]
        sg = _sigmoid(z)
        sz = z * sg
        gate = yv * sz
        w = w_ref[...]
        d = d_ref[...]
        dsz = _silu_grad(z, sg)
        dws = []
        for g in range(ng):
            cols = slice(g * GW, (g + 1) * GW)
            dg, dwr = _rms_bwd(gate[:, cols], w[:, cols], d[:, cols])
            dy_ref[:, cols] = dg * sz[:, cols]
            dz_ref[:, cols] = (dg * yv[:, cols] * dsz[:, cols]).astype(BF16)
            dws.append(jnp.sum(dwr, axis=0, keepdims=True))
        first = pl.program_id(0) == 0
        for g in range(ng):
            cols = slice(g * GW, (g + 1) * GW)

            @pl.when(first)
            def _():
                dw_ref[:, cols] = dws[g]

            @pl.when(jnp.logical_not(first))
            def _():
                dw_ref[:, cols] += dws[g]

    row = pl.BlockSpec((tr, di), lambda i: (i, 0))
    vec = pl.BlockSpec((1, di), lambda i: (0, 0))
    return pl.pallas_call(body, name="ssd_gate_norm_bwd", grid=(t // tr,),
                          in_specs=[row, row, row, vec, pl.BlockSpec((8, LANES), lambda i: (0, 0))],
                          out_specs=[row, row, vec],
                          out_shape=[jax.ShapeDtypeStruct((t, di), F32), jax.ShapeDtypeStruct((t, di), BF16),
                                     jax.ShapeDtypeStruct((1, di), F32)],
                          compiler_params=_params("arbitrary"))(dyn, y, zx, norm_w, after)


def _attn_mask(n):
    w = ATTN_WINDOW
    qpos = lax.broadcasted_iota(jnp.int32, (w, 2 * w), 0) + w
    kpos = lax.broadcasted_iota(jnp.int32, (w, 2 * w), 1)
    rel = qpos - kpos
    return (rel >= 0) & (rel < w) & jnp.logical_not((n == 0) & (kpos < w))


def _attn_probs(qh, kbh, mask, sink):
    s = _dot_nt(qh, kbh) * (ATTN_HEAD_DIM ** -0.5)
    s = jnp.where(mask, s, -jnp.inf)
    m = jnp.maximum(jnp.max(s, axis=-1, keepdims=True), sink)
    e = jnp.exp(s - m)
    es = jnp.exp(sink - m)
    inv = 1.0 / (jnp.sum(e, axis=-1, keepdims=True) + es)
    return e * inv, es * inv


def _attn_fwd(qkv, sinks):
    t = qkv.shape[0]
    w, hd = ATTN_WINDOW, ATTN_HEAD_DIM
    kd = ATTN_N_KV * hd
    qd = ATTN_REP * kd
    nb = t // w

    def body(q_ref, kc_ref, vc_ref, kp_ref, vp_ref, s_ref, o_ref):
        n = pl.program_id(0)
        mask = _attn_mask(n)
        q = q_ref[...]
        kb = jnp.concatenate([kp_ref[...], kc_ref[...]], axis=0)
        vb = jnp.concatenate([vp_ref[...], vc_ref[...]], axis=0)
        sk = s_ref[...]
        for kv in range(ATTN_N_KV):
            kbh = kb[:, kv * hd:(kv + 1) * hd]
            vbh = vb[:, kv * hd:(kv + 1) * hd]
            for rep in range(ATTN_REP):
                h = kv * ATTN_REP + rep
                p, _ = _attn_probs(q[:, h * hd:(h + 1) * hd], kbh, mask, sk[:, h:h + 1])
                o_ref[:, h * hd:(h + 1) * hd] = _dot(p.astype(BF16), vbh).astype(BF16)

    prev = lambda n: jnp.maximum(n - 1, 0)
    return pl.pallas_call(
        body, name="attn_fwd", grid=(nb,),
        in_specs=[pl.BlockSpec((w, qd), lambda n: (n, 0)),
                  pl.BlockSpec((w, kd), lambda n: (n, ATTN_REP)),
                  pl.BlockSpec((w, kd), lambda n: (n, ATTN_REP + 1)),
                  pl.BlockSpec((w, kd), lambda n: (prev(n), ATTN_REP)),
                  pl.BlockSpec((w, kd), lambda n: (prev(n), ATTN_REP + 1)),
                  pl.BlockSpec((1, sinks.shape[1]), lambda n: (0, 0))],
        out_specs=pl.BlockSpec((w, qd), lambda n: (n, 0)),
        out_shape=jax.ShapeDtypeStruct((t, qd), BF16),
        compiler_params=_params("parallel"))(qkv, qkv, qkv, qkv, qkv, sinks)


def _attn_bwd(qkv, do, sinks):
    t = qkv.shape[0]
    w, hd = ATTN_WINDOW, ATTN_HEAD_DIM
    kd = ATTN_N_KV * hd
    qd = ATTN_REP * kd
    nq = ATTN_N_KV * ATTN_REP
    nb = t // w

    def body(q_ref, kc_ref, vc_ref, kp_ref, vp_ref, do_ref, s_ref,
             dq_ref, dk_ref, dv_ref, bq_ref, bk_ref, bv_ref, dsk_ref, ck_ref, cv_ref):
        n = pl.program_id(0)
        first = n == 0

        @pl.when(first)
        def _():
            ck_ref[...] = jnp.zeros_like(ck_ref)
            cv_ref[...] = jnp.zeros_like(cv_ref)
            bq_ref[...] = jnp.zeros_like(bq_ref)
            bk_ref[...] = jnp.zeros_like(bk_ref)
            bv_ref[...] = jnp.zeros_like(bv_ref)
            dsk_ref[...] = jnp.zeros_like(dsk_ref)

        @pl.when(n < nb)
        def _():
            mask = _attn_mask(n)
            q = q_ref[...]
            dov = do_ref[...]
            kb = jnp.concatenate([kp_ref[...], kc_ref[...]], axis=0)
            vb = jnp.concatenate([vp_ref[...], vc_ref[...]], axis=0)
            sk = s_ref[...]
            lane = lax.broadcasted_iota(jnp.int32, (1, nq), 1)
            dsk = jnp.zeros((1, nq), F32)
            dq_parts, dk_parts, dv_parts = [], [], []
            for kv in range(ATTN_N_KV):
                kbh = kb[:, kv * hd:(kv + 1) * hd]
                vbh = vb[:, kv * hd:(kv + 1) * hd]
                dkh = jnp.zeros((2 * w, hd), F32)
                dvh = jnp.zeros((2 * w, hd), F32)
                for rep in range(ATTN_REP):
                    h = kv * ATTN_REP + rep
                    qh = q[:, h * hd:(h + 1) * hd]
                    doh = dov[:, h * hd:(h + 1) * hd]
                    p, ps = _attn_probs(qh, kbh, mask, sk[:, h:h + 1])
                    pb = p.astype(BF16)
                    dp = _dot_nt(doh, vbh)
                    delta = jnp.sum(p * dp, axis=-1, keepdims=True)
                    dsc = (p * (dp - delta) * (hd ** -0.5)).astype(BF16)
                    dq_parts.append(_dot(dsc, kbh))
                    dkh = dkh + _dot_tn(dsc, qh)
                    dvh = dvh + _dot_tn(pb, doh)
                    dsk = jnp.where(lane == h, -jnp.sum(ps * delta, axis=0, keepdims=True), dsk)
                dk_parts.append(dkh)
                dv_parts.append(dvh)
            dq = jnp.concatenate(dq_parts, axis=1)
            dkb = jnp.concatenate(dk_parts, axis=1)
            dvb = jnp.concatenate(dv_parts, axis=1)
            dq_ref[...] = dq.astype(BF16)
            bq_ref[...] += jnp.sum(dq, axis=0, keepdims=True)
            dsk_ref[...] += dsk
            dk_prev = ck_ref[...] + dkb[:w, :]
            dv_prev = cv_ref[...] + dvb[:w, :]
            dk_ref[...] = dk_prev.astype(BF16)
            dv_ref[...] = dv_prev.astype(BF16)

            @pl.when(n > 0)
            def _():
                bk_ref[...] += jnp.sum(dk_prev, axis=0, keepdims=True)
                bv_ref[...] += jnp.sum(dv_prev, axis=0, keepdims=True)

            ck_ref[...] = dkb[w:, :]
            cv_ref[...] = dvb[w:, :]

        @pl.when(n == nb)
        def _():
            dk_ref[...] = ck_ref[...].astype(BF16)
            dv_ref[...] = cv_ref[...].astype(BF16)
            bk_ref[...] += jnp.sum(ck_ref[...], axis=0, keepdims=True)
            bv_ref[...] += jnp.sum(cv_ref[...], axis=0, keepdims=True)

    cur = lambda n: jnp.minimum(n, nb - 1)
    prev = lambda n: jnp.maximum(jnp.minimum(n, nb - 1) - 1, 0)
    late = lambda n: jnp.maximum(n - 1, 0)
    vec = lambda width: pl.BlockSpec((1, width), lambda n: (0, 0))
    return pl.pallas_call(
        body, name="attn_bwd", grid=(nb + 1,),
        in_specs=[pl.BlockSpec((w, qd), lambda n: (cur(n), 0)),
                  pl.BlockSpec((w, kd), lambda n: (cur(n), ATTN_REP)),
                  pl.BlockSpec((w, kd), lambda n: (cur(n), ATTN_REP + 1)),
                  pl.BlockSpec((w, kd), lambda n: (prev(n), ATTN_REP)),
                  pl.BlockSpec((w, kd), lambda n: (prev(n), ATTN_REP + 1)),
                  pl.BlockSpec((w, qd), lambda n: (cur(n), 0)),
                  vec(nq)],
        out_specs=[pl.BlockSpec((w, qd), lambda n: (cur(n), 0)),
                   pl.BlockSpec((w, kd), lambda n: (late(n), 0)),
                   pl.BlockSpec((w, kd), lambda n: (late(n), 0)),
                   vec(qd), vec(kd), vec(kd), vec(nq)],
        out_shape=[jax.ShapeDtypeStruct((t, qd), BF16), jax.ShapeDtypeStruct((t, kd), BF16),
                   jax.ShapeDtypeStruct((t, kd), BF16), jax.ShapeDtypeStruct((1, qd), F32),
                   jax.ShapeDtypeStruct((1, kd), F32), jax.ShapeDtypeStruct((1, kd), F32),
                   jax.ShapeDtypeStruct((1, nq), F32)],
        scratch_shapes=[pltpu.VMEM((w, kd), F32), pltpu.VMEM((w, kd), F32)],
        compiler_params=_params("arbitrary"))(qkv, qkv, qkv, qkv, qkv, do, sinks)


def _attn_mask_t(n):
    w = ATTN_WINDOW
    kpos = lax.broadcasted_iota(jnp.int32, (2 * w, ATTN_REP * w), 0)
    qpos = lax.broadcasted_iota(jnp.int32, (2 * w, ATTN_REP * w), 1) % w + w
    rel = qpos - kpos
    return (rel >= 0) & (rel < w) & jnp.logical_not((n == 0) & (kpos < w))


def _attn_probs_t(qts, ktb, mask, sink):
    s = _dot_tn(ktb, qts) * (ATTN_HEAD_DIM ** -0.5)
    s = jnp.where(mask, s, -jnp.inf)
    m = jnp.maximum(jnp.max(s, axis=0, keepdims=True), sink)
    e = jnp.exp(s - m)
    es = jnp.exp(sink - m)
    inv = 1.0 / (jnp.sum(e, axis=0, keepdims=True) + es)
    return e * inv, es * inv


def _attn_blocks_t(kv, q_ref, kc_ref, vc_ref, kp_ref, vp_ref):
    hd = ATTN_HEAD_DIM
    rows = slice(kv * hd, (kv + 1) * hd)
    ktb = jnp.concatenate([kp_ref[rows, :], kc_ref[rows, :]], axis=1)
    vtb = jnp.concatenate([vp_ref[rows, :], vc_ref[rows, :]], axis=1)
    qts = jnp.concatenate([q_ref[(kv * ATTN_REP + r) * hd:(kv * ATTN_REP + r + 1) * hd, :]
                           for r in range(ATTN_REP)], axis=1)
    return qts, ktb, vtb


def _attn_specs_t(nb, cur, prev):
    w, hd = ATTN_WINDOW, ATTN_HEAD_DIM
    kd = ATTN_N_KV * hd
    qd = ATTN_REP * kd
    return [pl.BlockSpec((qd, w), lambda n: (0, cur(n))),
            pl.BlockSpec((kd, w), lambda n: (ATTN_REP, cur(n))),
            pl.BlockSpec((kd, w), lambda n: (ATTN_REP + 1, cur(n))),
            pl.BlockSpec((kd, w), lambda n: (ATTN_REP, prev(n))),
            pl.BlockSpec((kd, w), lambda n: (ATTN_REP + 1, prev(n)))]


def _attn_fwd_t(qkv_t, sinks_rep):
    t = qkv_t.shape[1]
    w, hd = ATTN_WINDOW, ATTN_HEAD_DIM
    qd = ATTN_N_KV * ATTN_REP * hd
    nb = t // w

    def body(q_ref, kc_ref, vc_ref, kp_ref, vp_ref, s_ref, o_ref):
        mask = _attn_mask_t(pl.program_id(0))
        for kv in range(ATTN_N_KV):
            qts, ktb, vtb = _attn_blocks_t(kv, q_ref, kc_ref, vc_ref, kp_ref, vp_ref)
            p, _ = _attn_probs_t(qts, ktb, mask, s_ref[kv])
            ots = _dot(vtb, p.astype(BF16))
            for r in range(ATTN_REP):
                h = kv * ATTN_REP + r
                o_ref[h * hd:(h + 1) * hd, :] = ots[:, r * w:(r + 1) * w].astype(BF16)

    return pl.pallas_call(
        body, name="attn_fwd", grid=(nb,),
        in_specs=_attn_specs_t(nb, lambda n: n, lambda n: jnp.maximum(n - 1, 0)) + [
            pl.BlockSpec(sinks_rep.shape, lambda n: (0, 0, 0))],
        out_specs=pl.BlockSpec((qd, w), lambda n: (0, n)),
        out_shape=jax.ShapeDtypeStruct((qd, t), BF16),
        compiler_params=_params("parallel"))(qkv_t, qkv_t, qkv_t, qkv_t, qkv_t, sinks_rep)


def _attn_bwd_t(qkv_t, do_t, sinks_rep):
    t = qkv_t.shape[1]
    w, hd = ATTN_WINDOW, ATTN_HEAD_DIM
    kd = ATTN_N_KV * hd
    qd = ATTN_REP * kd
    nq = ATTN_N_KV * ATTN_REP
    nb = t // w
    rows_all = qd + 2 * kd

    def body(q_ref, kc_ref, vc_ref, kp_ref, vp_ref, do_ref, s_ref, dqkv_ref, bsum_ref, dsk_ref,
             carry_ref, new_ref, bacc_ref, sacc_ref):
        n = pl.program_id(0)

        @pl.when(n == 0)
        def _():
            carry_ref[...] = jnp.zeros_like(carry_ref)
            bacc_ref[...] = jnp.zeros_like(bacc_ref)
            sacc_ref[...] = jnp.zeros_like(sacc_ref)

        @pl.when(n < nb)
        def _():
            mask = _attn_mask_t(n)
            for kv in range(ATTN_N_KV):
                qts, ktb, vtb = _attn_blocks_t(kv, q_ref, kc_ref, vc_ref, kp_ref, vp_ref)
                dots = jnp.concatenate([do_ref[(kv * ATTN_REP + r) * hd:(kv * ATTN_REP + r + 1) * hd, :]
                                        for r in range(ATTN_REP)], axis=1)
                p, ps = _attn_probs_t(qts, ktb, mask, s_ref[kv])
                dpt = _dot_tn(vtb, dots)
                delta = jnp.sum(p * dpt, axis=0, keepdims=True)
                dst = (p * (dpt - delta) * (hd ** -0.5)).astype(BF16)
                dqts = _dot(ktb, dst)
                for r in range(ATTN_REP):
                    h = kv * ATTN_REP + r
                    new_ref[h * hd:(h + 1) * hd, :] = dqts[:, r * w:(r + 1) * w]
                dktb = _dot_nt(qts, dst)
                dvtb = _dot_nt(dots, p.astype(BF16))
                krows = slice(qd + kv * hd, qd + (kv + 1) * hd)
                vrows = slice(qd + kd + kv * hd, qd + kd + (kv + 1) * hd)
                carry_ref[krows, :] += dktb[:, :w]
                carry_ref[vrows, :] += dvtb[:, :w]
                new_ref[krows, :] = dktb[:, w:]
                new_ref[vrows, :] = dvtb[:, w:]
                sacc_ref[kv] += -(ps * delta)

        @pl.when(n >= 1)
        def _():
            done = carry_ref[...]
            dqkv_ref[...] = done.astype(BF16)
            bacc_ref[...] += done

        @pl.when(n < nb)
        def _():
            carry_ref[...] = new_ref[...]

        @pl.when(n == nb)
        def _():
            bsum_ref[...] = jnp.sum(bacc_ref[...], axis=1, keepdims=True)
            lane = lax.broadcasted_iota(jnp.int32, (1, nq), 1)
            dsk = jnp.zeros((1, nq), F32)
            for kv in range(ATTN_N_KV):
                acc = sacc_ref[kv]
                for r in range(ATTN_REP):
                    tot = jnp.sum(acc[:, r * w:(r + 1) * w], axis=1, keepdims=True)
                    dsk = jnp.where(lane == kv * ATTN_REP + r, tot, dsk)
            dsk_ref[...] = dsk

    cur = lambda n: jnp.minimum(n, nb - 1)
    prev = lambda n: jnp.maximum(jnp.minimum(n, nb - 1) - 1, 0)
    return pl.pallas_call(
        body, name="attn_bwd", grid=(nb + 1,),
        in_specs=_attn_specs_t(nb, cur, prev) + [pl.BlockSpec((qd, w), lambda n: (0, cur(n))),
                                                 pl.BlockSpec(sinks_rep.shape, lambda n: (0, 0, 0))],
        out_specs=[pl.BlockSpec((rows_all, w), lambda n: (0, jnp.maximum(n - 1, 0))),
                   pl.BlockSpec((rows_all, 1), lambda n: (0, 0)),
                   pl.BlockSpec((1, nq), lambda n: (0, 0))],
        out_shape=[jax.ShapeDtypeStruct((rows_all, t), BF16), jax.ShapeDtypeStruct((rows_all, 1), F32),
                   jax.ShapeDtypeStruct((1, nq), F32)],
        scratch_shapes=[pltpu.VMEM((rows_all, w), F32), pltpu.VMEM((rows_all, w), F32),
                        pltpu.VMEM((rows_all, w), F32), pltpu.VMEM(sinks_rep.shape, F32)],
        compiler_params=_params("arbitrary"))(qkv_t, qkv_t, qkv_t, qkv_t, qkv_t, do_t, sinks_rep)


HBM_SPEC = pl.BlockSpec(memory_space=pl.ANY)
HBM_ONLY = pl.BlockSpec(memory_space=pltpu.HBM)


def _comm_call(name, body, ins, out_shapes, n_sems):
    return pl.pallas_call(
        body, name=name, in_specs=[HBM_SPEC] * len(ins), out_specs=[HBM_SPEC] * len(out_shapes),
        out_shape=out_shapes,
        scratch_shapes=[pltpu.SemaphoreType.DMA((s,)) for s in n_sems])(*ins)


def _all_gather(name, shards, after):
    n = len(shards)
    na = len(after)

    def body(*refs):
        x_refs, out_refs = refs[:n], refs[n + na:2 * n + na]
        send_sems, recv_sems, local_sems = refs[2 * n + na:]
        x, y, c = lax.axis_index("x"), lax.axis_index("y"), lax.axis_index("c")
        me, sibling = (x, y, c), (x, y, 1 - c)
        chips = [(1 - x, y), (x, 1 - y), (1 - x, 1 - y)]

        def slot(i, px, py, pc):
            return out_refs[i].at[4 * px + 2 * py + pc]

        def copy(k, i, block, to, src=None):
            return pltpu.make_async_remote_copy(
                src_ref=slot(i, *block) if src is None else src, dst_ref=slot(i, *block),
                send_sem=send_sems.at[k * n + i], recv_sem=recv_sems.at[k * n + i], device_id=to,
                device_id_type=MESH)

        mine = [pltpu.make_async_copy(x_refs[i], slot(i, *me), local_sems.at[i]) for i in range(n)]
        first = []
        for i in range(n):
            mine[i].start()
            first.append(copy(0, i, me, sibling, src=x_refs[i]))
            first += [copy(1 + j, i, me, (*chip, c), src=x_refs[i]) for j, chip in enumerate(chips)]
        for cp in first:
            cp.start()
        passed = []
        for i in range(n):
            for j, chip in enumerate(chips):
                copy(1 + j, i, (*chip, c), me).wait_recv()
                passed.append(copy(4 + j, i, (*chip, c), sibling))
                passed[-1].start()
        for i in range(n):
            copy(0, i, sibling, me).wait_recv()
            for j, chip in enumerate(chips):
                copy(4 + j, i, (*chip, 1 - c), me).wait_recv()
        for cp in first + passed:
            cp.wait_send()
        for cp in mine:
            cp.wait()

    outs = [jax.ShapeDtypeStruct((N_DEV,) + s.shape, s.dtype) for s in shards]
    return _comm_call(name, body, list(shards) + list(after), outs, (7 * n, 7 * n, n))


SEM_SPEC = pl.BlockSpec(memory_space=pltpu.SEMAPHORE)
SPLIT_COPY_EFFECT = pltpu.SideEffectType.DATAFLOW_SIDE_EFFECTING


def _in_hbm(a):
    return pltpu.with_memory_space_constraint(a, pltpu.HBM)


def _split_start(name, body, srcs, lands, n_sems):
    n = len(srcs)
    bufs = [_in_hbm(a) for a in list(srcs) + list(lands)]
    outs = pl.pallas_call(
        body, name=name,
        out_shape=(pltpu.SemaphoreType.DMA((n_sems,)), pltpu.SemaphoreType.DMA((n_sems,)),
                   *[pltpu.HBM(a.shape, a.dtype) for a in bufs], jax.ShapeDtypeStruct((8, LANES), F32)),
        in_specs=[HBM_ONLY] * (2 * n),
        out_specs=(SEM_SPEC, SEM_SPEC, *[HBM_ONLY] * (2 * n), pl.BlockSpec(memory_space=pltpu.VMEM)),
        input_output_aliases={i: 2 + i for i in range(2 * n)},
        compiler_params=pltpu.CompilerParams(has_side_effects=SPLIT_COPY_EFFECT))(*bufs)
    return outs[0], outs[1], list(outs[2:2 + n]), list(outs[2 + n:2 + 2 * n]), outs[-1]


def _split_wait(name, body, send_sems, recv_sems, srcs, lands, after):
    n = len(srcs)
    outs = pl.pallas_call(
        body, name=name,
        out_shape=[pltpu.HBM(a.shape, a.dtype) for a in list(srcs) + list(lands)],
        in_specs=[HBM_ONLY] * (2 * n) + [SEM_SPEC, SEM_SPEC, HBM_SPEC],
        out_specs=[HBM_ONLY] * (2 * n),
        input_output_aliases={i: i for i in range(2 * n)},
        compiler_params=pltpu.CompilerParams(has_side_effects=SPLIT_COPY_EFFECT))(
            *srcs, *lands, send_sems, recv_sems, after)
    return list(outs[:n]), list(outs[n:])


N_PEERS = N_DEV - 1


def _gather_peers():
    x, y, c = lax.axis_index("x"), lax.axis_index("y"), lax.axis_index("c")
    flips = [(fx, fy, fc) for fx in (0, 1) for fy in (0, 1) for fc in (0, 1) if fx or fy or fc]
    return [(1 - x if fx else x, 1 - y if fy else y, 1 - c if fc else c) for fx, fy, fc in flips]


def _block_id(dev):
    return 4 * dev[0] + 2 * dev[1] + dev[2]


def _landing_block(land_ref, shard_shape, side_by_side, dev):
    if not side_by_side:
        return land_ref.at[_block_id(dev)]
    cols = shard_shape[1]
    return land_ref.at[:, pl.ds(pl.multiple_of(_block_id(dev) * cols, LANES), cols)]


def _gather_start(name, shards, side_by_side):
    n = len(shards)

    def body(*refs):
        x_refs, land_refs = refs[:n], refs[n:2 * n]
        send_sems, recv_sems, token = refs[2 * n], refs[2 * n + 1], refs[-1]
        me = (lax.axis_index("x"), lax.axis_index("y"), lax.axis_index("c"))
        for i in range(n):
            for k, peer in enumerate(_gather_peers()):
                pltpu.make_async_remote_copy(
                    src_ref=x_refs[i], dst_ref=_landing_block(land_refs[i], shards[i].shape, side_by_side[i], me),
                    send_sem=send_sems.at[N_PEERS * i + k], recv_sem=recv_sems.at[N_PEERS * i + k],
                    device_id=peer, device_id_type=MESH).start()
            pltpu.make_async_copy(x_refs[i], _landing_block(land_refs[i], shards[i].shape, side_by_side[i], me),
                                  send_sems.at[N_PEERS * n + i]).start()
        token[...] = jnp.zeros_like(token)

    lands = [lax.empty((s.shape[0], N_DEV * s.shape[1]) if wide else (N_DEV,) + s.shape, s.dtype)
             for s, wide in zip(shards, side_by_side)]
    return _split_start(name, body, shards, lands, (N_PEERS + 1) * n)


def _gather_wait(name, send_sems, recv_sems, first, n_all, shards, lands, side_by_side, after):
    n = len(shards)

    def body(*refs):
        x_refs, land_refs = refs[:n], refs[n:2 * n]
        send_sems, recv_sems = refs[2 * n], refs[2 * n + 1]
        me = (lax.axis_index("x"), lax.axis_index("y"), lax.axis_index("c"))
        for i in range(n):
            pltpu.make_async_copy(x_refs[i], _landing_block(land_refs[i], shards[i].shape, side_by_side[i], me),
                                  send_sems.at[N_PEERS * n_all + first + i]).wait()
            for k, peer in enumerate(_gather_peers()):
                cp = pltpu.make_async_remote_copy(
                    src_ref=x_refs[i], dst_ref=_landing_block(land_refs[i], shards[i].shape, side_by_side[i], peer),
                    send_sem=send_sems.at[N_PEERS * (first + i) + k],
                    recv_sem=recv_sems.at[N_PEERS * (first + i) + k],
                    device_id=peer, device_id_type=MESH)
                cp.wait_send()
                cp.wait_recv()

    return _split_wait(name, body, send_sems, recv_sems, shards, lands, after)


def _gather_forward(name, lands, shards):
    n = len(shards)

    def body(*refs):
        x_refs, out_refs = refs[n:2 * n], refs[2 * n:3 * n]
        send_sems, recv_sems, local_sems = refs[3 * n:]
        x, y, c = lax.axis_index("x"), lax.axis_index("y"), lax.axis_index("c")
        chips = [(1 - x, y), (x, 1 - y), (1 - x, 1 - y)]
        mine = [pltpu.make_async_copy(x_refs[i], out_refs[i].at[_block_id((x, y, c))], local_sems.at[i])
                for i in range(n)]
        passed = [pltpu.make_async_remote_copy(
            src_ref=out_refs[i].at[_block_id((*chip, c))], dst_ref=out_refs[i].at[_block_id((*chip, c))],
            send_sem=send_sems.at[3 * i + j], recv_sem=recv_sems.at[3 * i + j], device_id=(x, y, 1 - c),
            device_id_type=MESH) for i in range(n) for j, chip in enumerate(chips)]
        for cp in mine + passed:
            cp.start()
        for i in range(n):
            for j, chip in enumerate(chips):
                pltpu.make_async_remote_copy(
                    src_ref=out_refs[i].at[_block_id((*chip, c))], dst_ref=out_refs[i].at[_block_id((*chip, 1 - c))],
                    send_sem=send_sems.at[3 * i + j], recv_sem=recv_sems.at[3 * i + j], device_id=(x, y, 1 - c),
                    device_id_type=MESH).wait()
        for cp in mine:
            cp.wait()

    return pl.pallas_call(
        body, name=name, in_specs=[HBM_SPEC] * (2 * n), out_specs=[HBM_SPEC] * n,
        out_shape=[jax.ShapeDtypeStruct(a.shape, a.dtype) for a in lands],
        input_output_aliases={i: i for i in range(n)},
        scratch_shapes=[pltpu.SemaphoreType.DMA((3 * n,)), pltpu.SemaphoreType.DMA((3 * n,)),
                        pltpu.SemaphoreType.DMA((n,))])(*lands, *shards)


def _chip_peers():
    x, y, c = lax.axis_index("x"), lax.axis_index("y"), lax.axis_index("c")
    return [(1 - x, y, c), (x, 1 - y, c), (1 - x, 1 - y, c)]


def _chip_start(name, blocks):
    n = len(blocks)

    def body(*refs):
        p_refs, land_refs = refs[:n], refs[n:2 * n]
        send_sems, recv_sems, token = refs[2 * n], refs[2 * n + 1], refs[-1]
        for i in range(n):
            for j, peer in enumerate(_chip_peers()):
                pltpu.make_async_remote_copy(
                    src_ref=p_refs[i].at[j], dst_ref=land_refs[i].at[j], send_sem=send_sems.at[3 * i + j],
                    recv_sem=recv_sems.at[3 * i + j], device_id=peer, device_id_type=MESH).start()
        token[...] = jnp.zeros_like(token)

    lands = [lax.empty(b.shape, b.dtype) for b in blocks]
    return _split_start(name, body, blocks, lands, 3 * n)


def _chip_wait(name, send_sems, recv_sems, blocks, lands, after):
    n = len(blocks)

    def body(*refs):
        p_refs, land_refs = refs[:n], refs[n:2 * n]
        send_sems, recv_sems = refs[2 * n], refs[2 * n + 1]
        for i in range(n):
            for j, peer in enumerate(_chip_peers()):
                cp = pltpu.make_async_remote_copy(
                    src_ref=p_refs[i].at[j], dst_ref=land_refs[i].at[j], send_sem=send_sems.at[3 * i + j],
                    recv_sem=recv_sems.at[3 * i + j], device_id=peer, device_id_type=MESH)
                cp.wait_send()
                cp.wait_recv()

    return _split_wait(name, body, send_sems, recv_sems, blocks, lands, after)


def _scatter_start(name, blocks):
    n = len(blocks)

    def body(*refs):
        b_refs, land_refs = refs[:n], refs[n:2 * n]
        send_sems, recv_sems, token = refs[2 * n], refs[2 * n + 1], refs[-1]
        me = (lax.axis_index("x"), lax.axis_index("y"), lax.axis_index("c"))
        for i in range(n):
            for k, peer in enumerate(_gather_peers()):
                pltpu.make_async_remote_copy(
                    src_ref=b_refs[i].at[_block_id(peer)], dst_ref=land_refs[i].at[_block_id(me)],
                    send_sem=send_sems.at[N_PEERS * i + k], recv_sem=recv_sems.at[N_PEERS * i + k],
                    device_id=peer, device_id_type=MESH).start()
            pltpu.make_async_copy(b_refs[i].at[_block_id(me)], land_refs[i].at[_block_id(me)],
                                  send_sems.at[N_PEERS * n + i]).start()
        token[...] = jnp.zeros_like(token)

    lands = [lax.empty(b.shape, b.dtype) for b in blocks]
    return _split_start(name, body, blocks, lands, (N_PEERS + 1) * n)


def _scatter_wait(name, send_sems, recv_sems, blocks, lands, after):
    n = len(blocks)

    def body(*refs):
        b_refs, land_refs = refs[:n], refs[n:2 * n]
        send_sems, recv_sems = refs[2 * n], refs[2 * n + 1]
        me = (lax.axis_index("x"), lax.axis_index("y"), lax.axis_index("c"))
        for i in range(n):
            pltpu.make_async_copy(b_refs[i].at[_block_id(me)], land_refs[i].at[_block_id(me)],
                                  send_sems.at[N_PEERS * n + i]).wait()
            for k, peer in enumerate(_gather_peers()):
                cp = pltpu.make_async_remote_copy(
                    src_ref=b_refs[i].at[_block_id(peer)], dst_ref=land_refs[i].at[_block_id(peer)],
                    send_sem=send_sems.at[N_PEERS * i + k], recv_sem=recv_sems.at[N_PEERS * i + k],
                    device_id=peer, device_id_type=MESH)
                cp.wait_send()
                cp.wait_recv()

    return _split_wait(name, body, send_sems, recv_sems, blocks, lands, after)


def _pair_exchange(name, blocks):
    n = len(blocks)

    def body(*refs):
        g_refs, out_refs = refs[:n], refs[n:2 * n]
        send_sems, recv_sems = refs[2 * n:]
        x, y, c = lax.axis_index("x"), lax.axis_index("y"), lax.axis_index("c")
        copies = [pltpu.make_async_remote_copy(
            src_ref=g_refs[i].at[2 * k + 1 - c], dst_ref=out_refs[i].at[k], send_sem=send_sems.at[4 * i + k],
            recv_sem=recv_sems.at[4 * i + k], device_id=(x, y, 1 - c), device_id_type=MESH)
            for i in range(n) for k in range(4)]
        for cp in copies:
            cp.start()
        for cp in copies:
            cp.wait()

    outs = [jax.ShapeDtypeStruct((4,) + b.shape[1:], b.dtype) for b in blocks]
    return _comm_call(name, body, blocks, outs, (4 * n, 4 * n))


def _chip_exchange(name, blocks):
    n = len(blocks)

    def body(*refs):
        p_refs, out_refs = refs[:n], refs[n:2 * n]
        send_sems, recv_sems = refs[2 * n:]
        x, y, c = lax.axis_index("x"), lax.axis_index("y"), lax.axis_index("c")
        chips = [(1 - x, y), (x, 1 - y), (1 - x, 1 - y)]
        copies = [pltpu.make_async_remote_copy(
            src_ref=p_refs[i].at[j], dst_ref=out_refs[i].at[j], send_sem=send_sems.at[3 * i + j],
            recv_sem=recv_sems.at[3 * i + j], device_id=(*chip, c), device_id_type=MESH)
            for i in range(n) for j, chip in enumerate(chips)]
        for cp in copies:
            cp.start()
        for cp in copies:
            cp.wait()

    outs = [jax.ShapeDtypeStruct(b.shape, b.dtype) for b in blocks]
    return _comm_call(name, body, blocks, outs, (3 * n, 3 * n))


def _pair_sum(name, blocks, from_sibling, g_idx, r_idx):
    _, r, c_ = blocks.shape
    tr = _tile(r, 512, 16)

    def body(gi_ref, ri_ref, a_ref, b_ref, own_ref, send_ref):
        k = pl.program_id(1)
        s = a_ref[...] + b_ref[...]

        @pl.when(k == 0)
        def _():
            own_ref[...] = s

        @pl.when(k > 0)
        def _():
            send_ref[...] = s.astype(send_ref.dtype)

    return pl.pallas_call(
        body, name=name,
        grid_spec=pltpu.PrefetchScalarGridSpec(
            num_scalar_prefetch=2, grid=(r // tr, 4),
            in_specs=[pl.BlockSpec((None, tr, c_), lambda i, k, gi, ri: (gi[k], i, 0)),
                      pl.BlockSpec((None, tr, c_), lambda i, k, gi, ri: (ri[k], i, 0))],
            out_specs=[pl.BlockSpec((None, tr, c_), lambda i, k, gi, ri: (0, i, 0)),
                       pl.BlockSpec((None, tr, c_), lambda i, k, gi, ri: (jnp.maximum(k - 1, 0), i, 0))]),
        out_shape=[jax.ShapeDtypeStruct((1, r, c_), F32), jax.ShapeDtypeStruct((3, r, c_), PAYLOAD)],
        compiler_params=_params("parallel", "arbitrary"))(g_idx, r_idx, blocks, from_sibling)


def _adamw(w, g, m, v):
    m = ADAM_B1 * m + (1.0 - ADAM_B1) * g
    v = ADAM_B2 * v + (1.0 - ADAM_B2) * (g * g)
    m_hat = m / (1.0 - ADAM_B1 ** ADAM_STEP)
    v_hat = v / (1.0 - ADAM_B2 ** ADAM_STEP)
    delta = -ADAM_LR * (m_hat / (jnp.sqrt(v_hat) + ADAM_EPS) + ADAM_WD * w)
    return delta, m, v


def _adamw_tiles(r, c_):
    tr = _tile(r, 256, 16)
    return (tr, c_) if tr < r or r <= 256 else (r, _tile(c_, 256))


def _sum_parts(part):
    g = part[0].astype(F32)
    for k in range(1, part.shape[0]):
        g = g + part[k].astype(F32)
    return g


def _sum_adamw(name, parts, w, m, v):
    r, c_ = w.shape
    tr, tc = _adamw_tiles(r, c_)

    def body(p_ref, w_ref, m_ref, v_ref, g_ref, d_ref, nm_ref, nv_ref):
        g = _sum_parts(p_ref)
        g_ref[...] = g
        d_ref[...], nm_ref[...], nv_ref[...] = _adamw(w_ref[...], g, m_ref[...], v_ref[...])

    tile = pl.BlockSpec((tr, tc), lambda i, j: (i, j))
    return pl.pallas_call(body, name=name, grid=(r // tr, c_ // tc),
                          in_specs=[pl.BlockSpec((parts.shape[0], tr, tc), lambda i, j: (0, i, j)), tile, tile, tile],
                          out_specs=[tile] * 4, out_shape=[jax.ShapeDtypeStruct((r, c_), F32)] * 4,
                          compiler_params=_params("parallel", "parallel"))(parts, w, m, v)


def _sum_adamw_layers(name, parts, w, m, v):
    n_layers, r, c_ = w.shape
    tr = _tile(r, 256, 16)

    def body(*refs):
        p_refs = refs[:n_layers]
        w_ref, m_ref, v_ref, g_ref, d_ref, nm_ref, nv_ref = refs[n_layers:]
        layer = pl.program_id(0)
        g = _sum_parts(p_refs[0])
        for li in range(1, n_layers):
            g = jnp.where(layer == li, _sum_parts(p_refs[li]), g)
        g_ref[...] = g
        d_ref[...], nm_ref[...], nv_ref[...] = _adamw(w_ref[...], g, m_ref[...], v_ref[...])

    row = pl.BlockSpec((None, tr, c_), lambda l, i: (l, i, 0))
    specs = [pl.BlockSpec((p.shape[0], tr, c_), lambda l, i: (0, i, 0)) for p in parts]
    return pl.pallas_call(body, name=name, grid=(n_layers, r // tr), in_specs=specs + [row, row, row],
                          out_specs=[row] * 4, out_shape=[jax.ShapeDtypeStruct(w.shape, F32)] * 4,
                          compiler_params=_params("parallel", "parallel"))(*parts, w, m, v)


def _pack_rows(flat, n_rows, cols):
    pad = n_rows * cols - flat.shape[-1]
    flat = jnp.pad(flat, [(0, 0)] * (flat.ndim - 1) + [(0, pad)])
    return flat.reshape(flat.shape[:-1] + (n_rows, cols))


def _cols_join(blocks):
    return jnp.concatenate([blocks[d] for d in range(N_DEV)], axis=1)


def _cols_split(full):
    c = full.shape[1] // N_DEV
    return jnp.stack([full[:, d * c:(d + 1) * c] for d in range(N_DEV)])


def _rows_join(blocks):
    return blocks.reshape(N_DEV * blocks.shape[1], blocks.shape[2])


def _rows_split(full):
    return full.reshape(N_DEV, full.shape[0] // N_DEV, full.shape[1])


def _perm_xbc(a, ng):
    lead = a.shape[:-1]
    di, gn = ng * GW, ng * SSD_D_STATE
    xs = a[..., :di].reshape(lead + (ng, GW))
    bs = a[..., di:di + gn].reshape(lead + (ng, SSD_D_STATE))
    cs = a[..., di + gn:].reshape(lead + (ng, SSD_D_STATE))
    return jnp.concatenate([xs, bs, cs], axis=-1).reshape(lead + (ng * GC,))


def _unperm_xbc(a, ng):
    lead = a.shape[:-1]
    g = a.reshape(lead + (ng, GC))
    return jnp.concatenate([g[..., :GW].reshape(lead + (ng * GW,)),
                            g[..., GW:GW + SSD_D_STATE].reshape(lead + (ng * SSD_D_STATE,)),
                            g[..., GW + SSD_D_STATE:].reshape(lead + (ng * SSD_D_STATE,))], axis=-1)


def _heads_col(v, ng):
    return jnp.pad(v.reshape(ng, 1, SSD_HPG), ((0, 0), (0, 0), (0, LANES - SSD_HPG)))


def _heads_row(v, ng):
    return jnp.pad(v.reshape(ng, SSD_HPG, 1), ((0, 0), (0, 8 - SSD_HPG), (0, 0)))


MATRIX_ITEMS = ("w_in", "w_out", "up0", "down0", "w_qkv", "w_o", "up1", "down1")
VECTOR_ITEMS = ("conv_w", "b_qkv", "b_o")
ITEMS = MATRIX_ITEMS + VECTOR_ITEMS
GATHER_STAGES = (("w_in", "conv_w"), ("w_out", "up0", "down0"), ("w_qkv", "b_qkv", "w_o", "b_o", "up1", "down1"))
SIDE_BY_SIDE = ("conv_w", "up0", "up1", "b_o")


def _items(tree, prefix=""):
    g = lambda k: tree[prefix + k]
    return {"w_in": g("ssd_w_in")[0].T, "w_out": g("ssd_w_out")[0], "w_qkv": g("attn_w_qkv")[0].T,
            "w_o": g("attn_w_o")[0], "up0": g("mlp_w_up")[0], "up1": g("mlp_w_up")[1],
            "down0": g("mlp_w_down")[0], "down1": g("mlp_w_down")[1], "conv_w": g("ssd_conv_w")[0],
            "b_qkv": g("attn_b_qkv"), "b_o": g("attn_b_o")}


def _from_items(it):
    return {"ssd_w_in": it["w_in"][None], "ssd_w_out": it["w_out"][None], "attn_w_qkv": it["w_qkv"].T[None],
            "attn_w_o": it["w_o"][None], "mlp_w_up": jnp.stack([it["up0"], it["up1"]]),
            "mlp_w_down": jnp.stack([it["down0"], it["down1"]]), "ssd_conv_w": it["conv_w"][None],
            "attn_b_qkv": it["b_qkv"], "attn_b_o": it["b_o"]}


REPLICATED = ("ssd_conv_b", "ssd_dt_bias", "ssd_a_log", "ssd_d", "ssd_norm_w", "attn_sinks", "mix_pre_norm",
              "mix_post_norm", "ffn_pre_norm", "ffn_post_norm")
WEIGHTS = ("ssd_w_in", "ssd_conv_w", "ssd_conv_b", "ssd_dt_bias", "ssd_a_log", "ssd_d", "ssd_norm_w", "ssd_w_out",
           "attn_w_qkv", "attn_b_qkv", "attn_sinks", "attn_w_o", "attn_b_o", "mlp_w_up", "mlp_w_down",
           "mix_pre_norm", "mix_post_norm", "ffn_pre_norm", "ffn_post_norm")


def _forward_backward(x, target, rep, token, weights_of_stage, reduce_grads):
    t, d = x.shape
    ng = rep["ssd_norm_w"].shape[1] // GW
    di = ng * GW
    n_xbc = ng * GC
    nh = ng * SSD_HPG
    grads, blocks = {}, {}
    w_up, w_down = [None, None], [None, None]
    sinks_rep = jnp.repeat(rep["attn_sinks"].reshape(ATTN_N_KV, ATTN_REP, 1), ATTN_WINDOW, axis=2).reshape(
        ATTN_N_KV, 1, ATTN_REP * ATTN_WINDOW)
    conv_b = rep["ssd_conv_b"]
    gn = ng * SSD_D_STATE
    parts = ((0, di), (di, di), (2 * di, gn), (2 * di + gn, gn), (di + n_xbc, nh))
    alog_c, dsk_c = (_heads_col(rep[k], ng) for k in ("ssd_a_log", "ssd_d"))
    bias_l, alog_l = (jnp.pad(rep[k], ((0, 0), (0, LANES - nh))) for k in ("ssd_dt_bias", "ssd_a_log"))
    norm = {k: rep[k] for k in ("mix_pre_norm", "mix_post_norm", "ffn_pre_norm", "ffn_post_norm")}

    def nrow(name, i):
        return norm[name][i:i + 1]

    def mlp_fwd(i, u2):
        p = _mm(f"mlp{i}_up", [u2], [w_up[i]], "nn", tm=1024, tn=1024, out_dtypes=(BF16,),
                epilogue=lambda acc: (jnp.square(jnp.maximum(acc, 0.0)),))
        f = _mm(f"mlp{i}_down", [p], [w_down[i]], "nn", tm=512, tn=1024)
        return p, f

    def mlp_bwd(i, df, u2, p):
        da = _mm(f"mlp{i}_dact", [df], [w_down[i]], "nt", tm=1024, tn=1024, out_dtypes=(BF16,),
                 tiles=(p,), epilogue=lambda acc, pv: (acc * (2.0 * jnp.sqrt(pv.astype(F32))),))
        blocks[f"down{i}"] = _rows_split(_mm(f"mlp{i}_dwdown", [p], [df], "tn", tm=512, tn=1024,
                                             out_dtypes=(PAYLOAD,)))
        blocks[f"up{i}"] = _mm(f"mlp{i}_dwup", [u2], [da], "tn", tm=1024, tn=da.shape[1] // N_DEV,
                               out_dtypes=(PAYLOAD,), col_blocks=True)
        return _mm(f"mlp{i}_dx", [da], [w_up[i]], "nt", tm=512, tn=1024)

    u0 = _prenorm("l0_prenorm", x, nrow("mix_pre_norm", 0), token)
    got = weights_of_stage(0, u0)
    w_in_t = _rows_join(got["w_in"])
    w_dt_t = jnp.pad(w_in_t[di + n_xbc:], ((0, LANES - nh), (0, 0)))
    conv_w = got["conv_w"]
    zx = _mm("ssd_in_proj", [u0], [w_in_t], "nt", tm=1024, tn=1024, n_use=di + n_xbc)
    zdt = _mm("ssd_dt_proj", [u0], [w_dt_t], "nt", tm=1024, tn=LANES)
    pre = _conv_fwd(zx, di, n_xbc, conv_w, conv_b)
    dt_c, cum_c, cum_r, sgd_c = _ssd_dt_prep(zdt, bias_l, alog_l, ng)
    y, states = _ssd_fwd(pre, dt_c, cum_c, cum_r, alog_c, dsk_c)
    yn = _gate_norm_fwd(y, zx, rep["ssd_norm_w"])
    got = weights_of_stage(1, yn)
    w_out = _rows_join(got["w_out"])
    w_up[0], w_down[0] = got["up0"], _rows_join(got["down0"])
    mix0 = _mm("ssd_out_proj", [yn], [w_out], "nn", tm=1024, tn=1024)
    h1, u0f = _post_pre("l0_mid", x, mix0, nrow("mix_post_norm", 0), nrow("ffn_pre_norm", 0))
    p0, f0 = mlp_fwd(0, u0f)
    h2, u1 = _post_pre("l1_in", h1, f0, nrow("ffn_post_norm", 0), nrow("mix_pre_norm", 1))
    got = weights_of_stage(2, u1)
    w_qkv_t = _rows_join(got["w_qkv"])
    w_o = _rows_join(got["w_o"])
    b_qkv_col = got["b_qkv"].reshape(-1, 1)
    b_o = got["b_o"]
    w_up[1], w_down[1] = got["up1"], _rows_join(got["down1"])
    qkv_t = _mm("attn_qkv_proj", [w_qkv_t], [u1], "nt", tm=768, tn=1024, out_dtypes=(BF16,), cols=(b_qkv_col,),
                epilogue=lambda acc, b: (acc + b,))
    ao_t = _attn_fwd_t(qkv_t, sinks_rep)
    mix1 = _mm("attn_out_proj", [ao_t], [w_o], "tn", tm=1024, tn=1024, rows=(b_o,),
               epilogue=lambda acc, b: (acc + b,))
    h3, u1f = _post_pre("l1_mid", h2, mix1, nrow("mix_post_norm", 1), nrow("ffn_pre_norm", 1))
    p1, f1 = mlp_fwd(1, u1f)
    dh, loss_row = _final_loss("loss", h3, f1, nrow("ffn_post_norm", 1), target)

    g_norm = {k: [None, None] for k in norm}
    df1, g_norm["ffn_post_norm"][1], _ = _norm_bwd("l1_ffn_post_bwd", dh, post=(f1, nrow("ffn_post_norm", 1)))
    du = mlp_bwd(1, df1, u1f, p1)
    sent = reduce_grads("mlp1", {k: blocks[k] for k in ("up1", "down1")})
    dh, g_norm["ffn_pre_norm"][1], dmix1, g_norm["mix_post_norm"][1], db_o = _norm_bwd(
        "l1_mid_bwd", dh, pre=(du, h3, nrow("ffn_pre_norm", 1)), post=(mix1, nrow("mix_post_norm", 1)), after=sent)
    blocks["b_o"] = _cols_split(db_o)
    blocks["w_o"] = _rows_split(_mm("attn_dwo", [ao_t], [dmix1], "nn", tm=512, tn=1024, out_dtypes=(PAYLOAD,)))
    dao_t = _mm("attn_dout", [w_o], [dmix1], "nt", tm=1024, tn=1024, out_dtypes=(BF16,))
    dqkv_t, db_qkv, grads["attn_sinks"] = _attn_bwd_t(qkv_t, dao_t, sinks_rep)
    blocks["b_qkv"] = db_qkv.reshape(N_DEV, 1, -1)
    blocks["w_qkv"] = _rows_split(_mm("attn_dwqkv", [dqkv_t], [u1], "nn", tm=512, tn=1024, out_dtypes=(PAYLOAD,)))
    du = _mm("attn_dx", [dqkv_t], [w_qkv_t], "tn", tm=1024, tn=1024)
    sent = reduce_grads("attn", {k: blocks[k] for k in ("w_o", "w_qkv", "b_o", "b_qkv")})
    dh, g_norm["mix_pre_norm"][1], df0, g_norm["ffn_post_norm"][0], _ = _norm_bwd(
        "l1_in_bwd", dh, pre=(du, h2, nrow("mix_pre_norm", 1)), post=(f0, nrow("ffn_post_norm", 0)), after=sent)
    du = mlp_bwd(0, df0, u0f, p0)
    sent = reduce_grads("mlp0", {k: blocks[k] for k in ("up0", "down0")})
    dh, g_norm["ffn_pre_norm"][0], dmix0, g_norm["mix_post_norm"][0], _ = _norm_bwd(
        "l0_mid_bwd", dh, pre=(du, h1, nrow("ffn_pre_norm", 0)), post=(mix0, nrow("mix_post_norm", 0)), after=sent)
    blocks["w_out"] = _rows_split(_mm("ssd_dwout", [yn], [dmix0], "tn", tm=512, tn=1024, out_dtypes=(PAYLOAD,)))
    dyn = _mm("ssd_dyn", [dmix0], [w_out], "nt", tm=1024, tn=1024)
    sent = reduce_grads("ssdout", {"w_out": blocks["w_out"]})
    dy, dz, grads["ssd_norm_w"] = _gate_norm_bwd(dyn, y, zx, rep["ssd_norm_w"], sent)
    dpx, dpb, dpc, ddt_g, dbias_g, dalog_g, dd_g = _ssd_bwd(dy, pre, states, dt_c, cum_c, cum_r, sgd_c, alog_c,
                                                             dsk_c)
    conv_out = [_conv_bwd(f"ssd_conv_bwd_{tag}", dp, zx, c0, conv_w[:, c0 - di:c0 - di + n])
                for tag, dp, (c0, n) in zip("xbc", (dpx, dpb, dpc), parts[1:4])]
    dconv_w = jnp.concatenate([o[1] for o in conv_out], axis=1)
    dconv_b = jnp.concatenate([o[2] for o in conv_out], axis=1)
    ddt = jnp.transpose(ddt_g[:, :, :SSD_HPG], (1, 0, 2)).reshape(t, nh)
    ddt = jnp.pad(ddt, ((0, 0), (0, LANES - nh))).astype(BF16)
    blocks["conv_w"] = _cols_split(dconv_w)
    grads["ssd_conv_b"] = dconv_b
    for name, val in (("ssd_dt_bias", dbias_g), ("ssd_a_log", dalog_g), ("ssd_d", dd_g)):
        grads[name] = val[:, 0, :SSD_HPG].reshape(1, nh)
    d_zx = [dz] + [o[0] for o in conv_out] + [ddt]
    dw_parts = [_mm(f"ssd_dw_{tag}", [d], [u0], "tn", tm=512, tn=1024, out_dtypes=(PAYLOAD,))
                for tag, d in zip("zxbct", d_zx)]
    dw_parts[-1] = dw_parts[-1][:nh]
    blocks["w_in"] = _rows_split(jnp.concatenate(dw_parts, axis=0))
    sent = reduce_grads("ssd", {k: blocks[k] for k in ("w_in", "conv_w")})
    w_parts = [w_in_t[r0:r0 + n] for r0, n in parts[:-1]] + [w_dt_t]
    du = _mm("ssd_dx", d_zx, w_parts, "nn", tm=256, tn=1024, after=sent)
    grad_x, g_norm["mix_pre_norm"][0] = _norm_bwd("l0_in_bwd", dh, pre=(du, x, nrow("mix_pre_norm", 0)), after=sent)
    for k in norm:
        grads[k] = jnp.concatenate(g_norm[k], axis=0)
    return loss_row, grad_x, grads


def kernel(x, ssd_w_in, ssd_conv_w, ssd_conv_b, ssd_dt_bias, ssd_a_log, ssd_d, ssd_norm_w, ssd_w_out, attn_w_qkv, attn_b_qkv, attn_sinks, attn_w_o, attn_b_o, mlp_w_up, mlp_w_down, mix_pre_norm, mix_post_norm, ffn_pre_norm, ffn_post_norm, loss_target, m_ssd_w_in, m_ssd_conv_w, m_ssd_conv_b, m_ssd_dt_bias, m_ssd_a_log, m_ssd_d, m_ssd_norm_w, m_ssd_w_out, m_attn_w_qkv, m_attn_b_qkv, m_attn_sinks, m_attn_w_o, m_attn_b_o, m_mlp_w_up, m_mlp_w_down, m_mix_pre_norm, m_mix_post_norm, m_ffn_pre_norm, m_ffn_post_norm, v_ssd_w_in, v_ssd_conv_w, v_ssd_conv_b, v_ssd_dt_bias, v_ssd_a_log, v_ssd_d, v_ssd_norm_w, v_ssd_w_out, v_attn_w_qkv, v_attn_b_qkv, v_attn_sinks, v_attn_w_o, v_attn_b_o, v_mlp_w_up, v_mlp_w_down, v_mix_pre_norm, v_mix_post_norm, v_ffn_pre_norm, v_ffn_post_norm):
    given = dict(locals())
    w = {k: given[k] for k in WEIGHTS}
    mom_m = {k: given["m_" + k] for k in WEIGHTS}
    mom_v = {k: given["v_" + k] for k in WEIGHTS}
    w_it, m_it, v_it = _items(given), _items(given, "m_"), _items(given, "v_")

    order = [k for stage in GATHER_STAGES for k in stage]
    shards = [w_it[k].astype(PAYLOAD) if k in MATRIX_ITEMS else w_it[k] for k in order]
    wide = [k in SIDE_BY_SIDE for k in order]
    g_send, g_recv, shards, lands, token = _gather_start("gather_start", shards, wide)

    def weights_of_stage(s, after):
        first = sum(len(stage) for stage in GATHER_STAGES[:s])
        sl = slice(first, first + len(GATHER_STAGES[s]))
        _, got = _gather_wait(f"gather_wait{s}", g_send, g_recv, first, len(order), shards[sl], lands[sl], wide[sl],
                              after)
        return dict(zip(GATHER_STAGES[s], got))

    in_flight = []

    def reduce_grads(tag, blocks):
        keys = list(blocks)
        started = _scatter_start(f"rs_start_{tag}", [blocks[k] for k in keys])
        in_flight.append((tag, keys, started))
        return started[-1]

    rep = {k: w[k] for k in REPLICATED}
    loss_row, grad_x, grads = _forward_backward(x[0], loss_target[0], rep, token, weights_of_stage, reduce_grads)

    def pack_rep(tree, last):
        flat = jnp.concatenate([tree[k].reshape(-1) for k in REPLICATED] + [last])
        return _pack_rows(flat, _round_up(-(-flat.shape[0] // LANES), 8), LANES)

    landed = {}

    def wait_group(group, after):
        tag, keys, (s_send, s_recv, srcs, s_lands, _) = group
        _, got = _scatter_wait(f"rs_wait_{tag}", s_send, s_recv, srcs, s_lands, after)
        landed.update(zip(keys, got))

    def adamw_item(k):
        return _sum_adamw(f"adamw_{k}", landed[k], w_it[k], m_it[k], v_it[k])

    def adamw_stack(name, keys):
        return _sum_adamw_layers(f"adamw_{name}", [landed[k] for k in keys], given[name], given["m_" + name],
                                 given["v_" + name])

    for group in in_flight[:-1]:
        wait_group(group, grad_x)
    done = {"mlp_w_up": adamw_stack("mlp_w_up", ("up0", "up1")),
            "mlp_w_down": adamw_stack("mlp_w_down", ("down0", "down1")),
            "attn_w_qkv": [o.T[None] for o in adamw_item("w_qkv")],
            "attn_w_o": [o[None] for o in adamw_item("w_o")],
            "attn_b_qkv": adamw_item("b_qkv"), "attn_b_o": adamw_item("b_o"),
            "ssd_w_out": [o[None] for o in adamw_item("w_out")]}
    partials, = _all_gather("gather_small_grads", [pack_rep(grads, loss_row[0, :1])],
                            [outs4[0] for outs4 in done.values()])
    wait_group(in_flight[-1], partials)
    done["ssd_w_in"] = [o.T[None] for o in adamw_item("w_in")]
    done["ssd_conv_w"] = [o[None] for o in adamw_item("conv_w")]
    zero = jnp.zeros((1,), F32)
    rep_out = _sum_adamw("adamw_replicated", partials, pack_rep(w, zero), pack_rep(mom_m, zero), pack_rep(mom_v, zero))

    kinds = []
    for kind, r_arr in enumerate(rep_out):
        tree = {name: outs4[kind] for name, outs4 in done.items()}
        flat, off = r_arr.reshape(-1), 0
        for k in REPLICATED:
            tree[k] = flat[off:off + w[k].size].reshape(w[k].shape)
            off += w[k].size
        kinds.append(tree)
    loss = rep_out[0].reshape(-1)[off]
    outs = [loss, grad_x[None]]
    for tree in kinds:
        outs += [tree[k] for k in WEIGHTS]
    return tuple(outs)
```

```python
import jax
import jax.numpy as jnp
from jax import lax
from jax.experimental import pallas as pl
from jax.experimental.pallas import tpu as pltpu

F32 = jnp.float32
BF16 = jnp.bfloat16
PAYLOAD = jnp.bfloat16
HIGHEST = lax.Precision.HIGHEST
MESH = pl.DeviceIdType.MESH

NORM_EPS = 1e-6
SSD_HEAD_DIM = 64
SSD_HPG = 4
SSD_D_STATE = 128
SSD_CONV_WIDTH = 4
SSD_CHUNK = 128
ATTN_HEAD_DIM = 64
ATTN_N_KV = 4
ATTN_REP = 4
ATTN_WINDOW = 128
ADAM_LR = 0.001
ADAM_B1 = 0.9
ADAM_B2 = 0.999
ADAM_EPS = 1e-08
ADAM_WD = 0.01
ADAM_STEP = 10

N_DEV = 8
LANES = 128
MXU_WIDTH = 256
V7X_VMEM_LIMIT = 56 * 1024 * 1024

GW = SSD_HPG * SSD_HEAD_DIM
GC = GW + 2 * SSD_D_STATE


def _params(*sem):
    return pltpu.CompilerParams(dimension_semantics=sem, vmem_limit_bytes=V7X_VMEM_LIMIT)


def _tile(n, pref, mult=LANES):
    best = None
    t = mult
    while t <= min(n, pref):
        if n % t == 0:
            best = t
        t += mult
    return best if best is not None else n


def _round_up(n, m):
    return (n + m - 1) // m * m


def _acc(ref, val, first):
    @pl.when(first)
    def _():
        ref[...] = val

    @pl.when(jnp.logical_not(first))
    def _():
        ref[...] += val


def _dot(a, b):
    return lax.dot_general(a, b, (((1,), (0,)), ((), ())), preferred_element_type=F32)


def _dot_nt(a, b):
    return lax.dot_general(a, b, (((1,), (1,)), ((), ())), preferred_element_type=F32)


def _dot_tn(a, b):
    return lax.dot_general(a, b, (((0,), (0,)), ((), ())), preferred_element_type=F32)


def _dot_f32(a, b):
    return lax.dot_general(a, b, (((1,), (0,)), ((), ())), preferred_element_type=F32, precision=HIGHEST)


_DOTS = {"nn": _dot, "nt": _dot_nt, "tn": _dot_tn}


def _sigmoid(x):
    return 1.0 / (1.0 + jnp.exp(-x))


def _softplus(x):
    return jnp.maximum(x, 0.0) + jnp.log1p(jnp.exp(-jnp.abs(x)))


def _silu_grad(x, s):
    return s * (1.0 + x * (1.0 - s))


def _mm(name, a_list, b_list, mode, *, tm, tn, out_dtypes=(F32,), epilogue=None, tiles=(), rows=(), cols=(),
        col_blocks=False, n_use=None, after=None):
    npair = len(a_list)
    if mode == "tn":
        m = a_list[0].shape[1]
    else:
        m = a_list[0].shape[0]
    n = n_use if n_use is not None else (b_list[0].shape[0] if mode == "nt" else b_list[0].shape[1])
    tm = _tile(m, tm, LANES if mode == "tn" else 8)
    tn = _tile(n, tn)
    assert m % tm == 0 and n % tn == 0, (name, m, n, tm, tn)
    dot = _DOTS[mode]

    def body(*refs):
        a_refs = refs[:npair]
        b_refs = refs[npair:2 * npair]
        n_extra = len(tiles) + len(rows) + len(cols)
        e_refs = refs[2 * npair:2 * npair + n_extra]
        o_refs = refs[2 * npair + n_extra + len(order):]
        cw = MXU_WIDTH if tn % MXU_WIDTH == 0 else tn
        for c in range(tn // cw):
            cs = slice(c * cw, (c + 1) * cw)
            acc = None
            for ar, br in zip(a_refs, b_refs):
                d = dot(ar[...], br[cs, :] if mode == "nt" else br[:, cs])
                acc = d if acc is None else acc + d
            n_wide = len(tiles) + len(rows)
            extra = [e[:, cs] for e in e_refs[:n_wide]] + [e[...] for e in e_refs[n_wide:]]
            outs = epilogue(acc, *extra) if epilogue is not None else (acc,)
            for o, v in zip(o_refs, outs):
                o[:, cs] = v.astype(o.dtype)

    in_specs = []
    for a in a_list:
        if mode == "tn":
            in_specs.append(pl.BlockSpec((a.shape[0], tm), lambda i, j: (0, i)))
        else:
            in_specs.append(pl.BlockSpec((tm, a.shape[1]), lambda i, j: (i, 0)))
    for b in b_list:
        if mode == "nt":
            in_specs.append(pl.BlockSpec((tn, b.shape[1]), lambda i, j: (j, 0)))
        else:
            in_specs.append(pl.BlockSpec((b.shape[0], tn), lambda i, j: (0, j)))
    in_specs += [pl.BlockSpec((tm, tn), lambda i, j: (i, j)) for _ in tiles]
    in_specs += [pl.BlockSpec((1, tn), lambda i, j: (0, j)) for _ in rows]
    in_specs += [pl.BlockSpec((tm, 1), lambda i, j: (i, 0)) for _ in cols]
    order = [] if after is None else [after]
    in_specs += [pl.BlockSpec((8, LANES), lambda i, j: (0, 0)) for _ in order]
    outs = pl.pallas_call(
        body,
        name=name,
        grid=(m // tm, n // tn),
        in_specs=in_specs,
        out_specs=[pl.BlockSpec((None, tm, tn), lambda i, j: (j, i, 0)) if col_blocks else
                   pl.BlockSpec((tm, tn), lambda i, j: (i, j)) for _ in out_dtypes],
        out_shape=[jax.ShapeDtypeStruct((n // tn, m, tn) if col_blocks else (m, n), dt) for dt in out_dtypes],
        compiler_params=_params("parallel", "parallel"),
    )(*a_list, *b_list, *tiles, *rows, *cols, *order)
    return outs[0] if len(out_dtypes) == 1 else outs


def _rms(x, w):
    r = lax.rsqrt(jnp.mean(x * x, axis=-1, keepdims=True) + NORM_EPS)
    return x * r * w


def _rms_bwd(x, w, dy):
    r = lax.rsqrt(jnp.mean(x * x, axis=-1, keepdims=True) + NORM_EPS)
    xh = x * r
    g = dy * w
    dx = r * (g - xh * jnp.mean(g * xh, axis=-1, keepdims=True))
    return dx, dy * xh


def _row_specs(tr, d):
    return pl.BlockSpec((tr, d), lambda i: (i, 0)), pl.BlockSpec((1, d), lambda i: (0, 0))


def _prenorm(name, h, w, after):
    t, d = h.shape
    tr = _tile(t, 512, 8)
    row, vec = _row_specs(tr, d)

    def body(h_ref, w_ref, after_ref, u_ref):
        u_ref[...] = _rms(h_ref[...], w_ref[...]).astype(BF16)

    return pl.pallas_call(body, name=name, grid=(t // tr,),
                          in_specs=[row, vec, pl.BlockSpec((8, LANES), lambda i: (0, 0))], out_specs=row,
                          out_shape=jax.ShapeDtypeStruct((t, d), BF16), compiler_params=_params("parallel"))(
                              h, w, after)


def _post_pre(name, h, m, w_post, w_pre):
    t, d = h.shape
    tr = _tile(t, 512, 8)
    row, vec = _row_specs(tr, d)

    def body(h_ref, m_ref, wq_ref, wp_ref, hn_ref, u_ref):
        hn = h_ref[...] + _rms(m_ref[...], wq_ref[...])
        hn_ref[...] = hn
        u_ref[...] = _rms(hn, wp_ref[...]).astype(BF16)

    return pl.pallas_call(body, name=name, grid=(t // tr,), in_specs=[row, row, vec, vec], out_specs=[row, row],
                          out_shape=[jax.ShapeDtypeStruct((t, d), F32), jax.ShapeDtypeStruct((t, d), BF16)],
                          compiler_params=_params("parallel"))(h, m, w_post, w_pre)


def _final_loss(name, h, m, w_post, target):
    t, d = h.shape
    tr = _tile(t, 512, 8)
    row, vec = _row_specs(tr, d)

    def body(h_ref, m_ref, wq_ref, t_ref, dh_ref, loss_ref):
        err = h_ref[...] + _rms(m_ref[...], wq_ref[...]) - t_ref[...]
        dh_ref[...] = err * (1.0 / d)
        part = 0.5 * jnp.sum(jnp.mean(err * err, axis=-1, keepdims=True), axis=0, keepdims=True)
        _acc(loss_ref, jnp.broadcast_to(part, (1, LANES)), pl.program_id(0) == 0)

    return pl.pallas_call(body, name=name, grid=(t // tr,), in_specs=[row, row, vec, row],
                          out_specs=[row, pl.BlockSpec((1, LANES), lambda i: (0, 0))],
                          out_shape=[jax.ShapeDtypeStruct((t, d), F32), jax.ShapeDtypeStruct((1, LANES), F32)],
                          compiler_params=_params("arbitrary"))(h, m, w_post, target)


def _norm_bwd(name, dh, pre=None, post=None, after=None):
    t, d = dh.shape
    tr = _tile(t, 256, 8)
    row, vec = _row_specs(tr, d)
    has_pre, has_post = pre is not None, post is not None

    def body(*refs):
        it = iter(refs)
        dh_ref = next(it)
        if has_pre:
            du_ref, x_ref, wp_ref = next(it), next(it), next(it)
        if has_post:
            m_ref, wq_ref = next(it), next(it)
        if after is not None:
            next(it)
        first = pl.program_id(0) == 0
        dh_v = dh_ref[...]
        if has_pre:
            dhn_ref, dwp_ref = next(it), next(it)
            dx, dwr = _rms_bwd(x_ref[...], wp_ref[...], du_ref[...])
            dh_v = dh_v + dx
            dhn_ref[...] = dh_v
            _acc(dwp_ref, jnp.sum(dwr, axis=0, keepdims=True), first)
        if has_post:
            dm_ref, dwq_ref, dms_ref = next(it), next(it), next(it)
            dm, dwr = _rms_bwd(m_ref[...], wq_ref[...], dh_v)
            dm_ref[...] = dm.astype(BF16)
            _acc(dwq_ref, jnp.sum(dwr, axis=0, keepdims=True), first)
            _acc(dms_ref, jnp.sum(dm, axis=0, keepdims=True), first)

    ins, in_specs, out_specs, out_shape = [dh], [row], [], []
    if has_pre:
        ins += list(pre)
        in_specs += [row, row, vec]
        out_specs += [row, vec]
        out_shape += [jax.ShapeDtypeStruct((t, d), F32), jax.ShapeDtypeStruct((1, d), F32)]
    if has_post:
        ins += list(post)
        in_specs += [row, vec]
        out_specs += [row, vec, vec]
        out_shape += [jax.ShapeDtypeStruct((t, d), BF16), jax.ShapeDtypeStruct((1, d), F32),
                      jax.ShapeDtypeStruct((1, d), F32)]
    if after is not None:
        ins.append(after)
        in_specs.append(pl.BlockSpec((8, LANES), lambda i: (0, 0)))
    return pl.pallas_call(body, name=name, grid=(t // tr,), in_specs=in_specs, out_specs=out_specs,
                          out_shape=out_shape, compiler_params=_params("arbitrary"))(*ins)


HALO = 8


def _shift_later(cur, prev, s):
    rolled = pltpu.roll(cur, s, 0)
    row = lax.broadcasted_iota(jnp.int32, prev.shape, 0)
    first = jnp.where(row < s, pltpu.roll(prev, s, 0), rolled[0:HALO])
    return jnp.concatenate([first, rolled[HALO:]], axis=0)


def _shift_earlier(cur, nxt, s):
    tt = cur.shape[0]
    rolled = pltpu.roll(cur, tt - s, 0)
    row = lax.broadcasted_iota(jnp.int32, nxt.shape, 0)
    last = jnp.where(row >= HALO - s, pltpu.roll(nxt, HALO - s, 0), rolled[tt - HALO:])
    return jnp.concatenate([rolled[:tt - HALO], last], axis=0)


def _conv_fwd(zx, col0, n_ch, conv_w, conv_b):
    t = zx.shape[0]
    tc = _tile(n_ch, 512)
    tt = _tile(t, 1024, 8)
    cb0 = col0 // tc
    assert col0 % tc == 0
    kw = SSD_CONV_WIDTH

    def body(x_ref, p_ref, w_ref, b_ref, o_ref):
        cur = x_ref[...]
        prev = jnp.where(pl.program_id(1) > 0, p_ref[...], 0.0)
        w = w_ref[...]
        acc = b_ref[...] + w[kw - 1:kw, :] * cur
        for k in range(kw - 1):
            acc = acc + w[k:k + 1, :] * _shift_later(cur, prev, kw - 1 - k)
        o_ref[...] = acc

    return pl.pallas_call(
        body, name="ssd_conv_fwd", grid=(n_ch // tc, t // tt),
        in_specs=[pl.BlockSpec((tt, tc), lambda j, i: (i, cb0 + j)),
                  pl.BlockSpec((HALO, tc), lambda j, i: (jnp.maximum(i * (tt // HALO) - 1, 0), cb0 + j)),
                  pl.BlockSpec((kw, tc), lambda j, i: (0, j)),
                  pl.BlockSpec((1, tc), lambda j, i: (0, j))],
        out_specs=pl.BlockSpec((tt, tc), lambda j, i: (i, j)),
        out_shape=jax.ShapeDtypeStruct((t, n_ch), F32),
        compiler_params=_params("parallel", "parallel"))(zx, zx, conv_w, conv_b)


def _conv_bwd(name, dpre, zx, col0, conv_w):
    t, n_ch = dpre.shape
    tc = _tile(n_ch, 512)
    tt = _tile(t, 1024, 8)
    cb0 = col0 // tc
    kw = SSD_CONV_WIDTH
    nt = t // tt

    def body(d_ref, dn_ref, x_ref, p_ref, w_ref, dx_ref, dw_ref, db_ref):
        i = pl.program_id(1)
        d = d_ref[...]
        d_next = jnp.where(i < nt - 1, dn_ref[...], 0.0)
        x = x_ref[...]
        x_prev = jnp.where(i > 0, p_ref[...], 0.0)
        w = w_ref[...]
        dx = w[kw - 1:kw, :] * d
        for k in range(kw - 1):
            dx = dx + w[k:k + 1, :] * _shift_earlier(d, d_next, kw - 1 - k)
        dx_ref[...] = dx.astype(BF16)
        first = i == 0
        for k in range(kw):
            xs = x if k == kw - 1 else _shift_later(x, x_prev, kw - 1 - k)
            val = jnp.sum(d * xs, axis=0, keepdims=True)

            @pl.when(first)
            def _():
                dw_ref[k:k + 1, :] = val

            @pl.when(jnp.logical_not(first))
            def _():
                dw_ref[k:k + 1, :] += val
        _acc(db_ref, jnp.sum(d, axis=0, keepdims=True), first)

    return pl.pallas_call(
        body, name=name, grid=(n_ch // tc, nt),
        in_specs=[pl.BlockSpec((tt, tc), lambda j, i: (i, j)),
                  pl.BlockSpec((HALO, tc), lambda j, i: (jnp.minimum((i + 1) * (tt // HALO), t // HALO - 1), j)),
                  pl.BlockSpec((tt, tc), lambda j, i: (i, cb0 + j)),
                  pl.BlockSpec((HALO, tc), lambda j, i: (jnp.maximum(i * (tt // HALO) - 1, 0), cb0 + j)),
                  pl.BlockSpec((kw, tc), lambda j, i: (0, j))],
        out_specs=[pl.BlockSpec((tt, tc), lambda j, i: (i, j)),
                   pl.BlockSpec((kw, tc), lambda j, i: (0, j)),
                   pl.BlockSpec((1, tc), lambda j, i: (0, j))],
        out_shape=[jax.ShapeDtypeStruct((t, n_ch), BF16), jax.ShapeDtypeStruct((kw, n_ch), F32),
                   jax.ShapeDtypeStruct((1, n_ch), F32)],
        compiler_params=_params("parallel", "arbitrary"))(dpre, dpre, zx, zx, conv_w)


def _head_of_lane(shape, width):
    return lax.broadcasted_iota(jnp.int32, shape, len(shape) - 1) // width


def _expand(v, n_rows):
    head = _head_of_lane((n_rows, GW), SSD_HEAD_DIM)
    out = jnp.zeros((n_rows, GW), F32)
    for j in range(SSD_HPG):
        out = jnp.where(head == j, v[:, j:j + 1], out)
    return out


def _contract(v, n_rows):
    head = _head_of_lane((n_rows, GW), SSD_HEAD_DIM)
    lane = lax.broadcasted_iota(jnp.int32, (n_rows, LANES), 1)
    out = jnp.zeros((n_rows, LANES), F32)
    for j in range(SSD_HPG):
        s = jnp.sum(jnp.where(head == j, v, 0.0), axis=1, keepdims=True)
        out = jnp.where(lane == j, s, out)
    return out


def _ssd_dt_prep(zdt, bias, alog, ng):
    t = zdt.shape[0]
    q = SSD_CHUNK

    def body(z_ref, b_ref, a_ref, dt_ref, cum_ref, cumr_ref, sg_ref):
        raw = z_ref[...] + b_ref[...]
        dt = _softplus(raw)
        sgd = _sigmoid(raw)
        row = lax.broadcasted_iota(jnp.int32, (q, q), 0)
        col = lax.broadcasted_iota(jnp.int32, (q, q), 1)
        cum = _dot_f32((col <= row).astype(F32), dt * (-jnp.exp(a_ref[...])))
        cum_t = cum.T
        lane = lax.broadcasted_iota(jnp.int32, (q, LANES), 1)
        for g in range(ng):
            shift = (LANES - g * SSD_HPG) % LANES

            def group(v):
                return jnp.where(lane < SSD_HPG, pltpu.roll(v, shift, 1) if shift else v, 0.0)

            dt_ref[g] = group(dt)
            cum_ref[g] = group(cum)
            sg_ref[g] = group(sgd)
            cumr_ref[g] = (pltpu.roll(cum_t, shift, 0) if shift else cum_t)[0:8, :]

    cols = pl.BlockSpec((ng, q, LANES), lambda c: (0, c, 0))
    vec = pl.BlockSpec((1, LANES), lambda c: (0, 0))
    col_shape = jax.ShapeDtypeStruct((ng, t, LANES), F32)
    return pl.pallas_call(body, name="ssd_dt_prep", grid=(t // q,),
                          in_specs=[pl.BlockSpec((q, LANES), lambda c: (c, 0)), vec, vec],
                          out_specs=[cols, cols, pl.BlockSpec((ng, 8, q), lambda c: (0, 0, c)), cols],
                          out_shape=[col_shape, col_shape, jax.ShapeDtypeStruct((ng, 8, t), F32), col_shape],
                          compiler_params=_params("parallel"))(zdt, bias, alog)


def _ssd_common(pre, dt, cum, cum_r, alog_c):
    q = SSD_CHUNK
    sg = _sigmoid(pre)
    act = pre * sg
    xa = act[:, :GW]
    bm = act[:, GW:GW + SSD_D_STATE].astype(BF16)
    cm = act[:, GW + SSD_D_STATE:].astype(BF16)
    row = lax.broadcasted_iota(jnp.int32, (q, q), 0)
    col = lax.broadcasted_iota(jnp.int32, (q, q), 1)
    tril = col <= row
    a_c = -jnp.exp(alog_c)
    g = _dot_nt(cm, bm)
    dt_x = _expand(dt, q)
    xdt = xa * dt_x
    cl = cum[q - 1:q, :]
    e_c = jnp.exp(cl - cum)
    lam_c = jnp.exp(cum)
    return dict(sg=sg, xa=xa, bm=bm, cm=cm, tril=tril, row=row, col=col, dt=dt, a_c=a_c, cum=cum, cum_r=cum_r,
                g=g, dt_x=dt_x, xdt=xdt, cl=cl, e_c=e_c, lam_c=lam_c)


SSD_GPS_FWD = 4
SSD_GPS_BWD = 2


def _ssd_specs(nc, rev, ng, gps):
    q = SSD_CHUNK
    xw, nw = gps * GW, gps * SSD_D_STATE
    b_off = ng * GW // nw
    c_off = (ng * GW + ng * SSD_D_STATE) // nw
    assert ng % gps == 0 and (ng * GW) % nw == 0 and (ng * SSD_D_STATE) % nw == 0

    def ch(c):
        return nc - 1 - c if rev else c

    chunk_grp = [pl.BlockSpec((q, xw), lambda g, c: (ch(c), g)),
                 pl.BlockSpec((q, nw), lambda g, c: (ch(c), b_off + g)),
                 pl.BlockSpec((q, nw), lambda g, c: (ch(c), c_off + g))]
    col_form = pl.BlockSpec((gps, q, LANES), lambda g, c: (g, ch(c), 0))
    row_form = pl.BlockSpec((gps, 8, q), lambda g, c: (g, 0, ch(c)))
    col_par = pl.BlockSpec((gps, 1, LANES), lambda g, c: (g, 0, 0))
    y_spec = pl.BlockSpec((q, xw), lambda g, c: (ch(c), g))
    st_spec = pl.BlockSpec((gps, None, GW, SSD_D_STATE), lambda g, c: (g, ch(c), 0, 0))
    bc_spec = pl.BlockSpec((q, nw), lambda g, c: (ch(c), g))
    return chunk_grp, col_form, row_form, col_par, y_spec, st_spec, bc_spec


def _ssd_group_views(gi, wide, narrow, stacked):
    xs, ns = pl.ds(gi * GW, GW), pl.ds(gi * SSD_D_STATE, SSD_D_STATE)
    return [r.at[:, xs] for r in wide], [r.at[:, ns] for r in narrow], [r.at[gi] for r in stacked]


def _ssd_fwd(pre, dt_c, cum_c, cum_r, alog_c, dsk_c):
    t = pre.shape[0]
    ng = pre.shape[1] // GC
    q = SSD_CHUNK
    nc = t // q
    gps = SSD_GPS_FWD if ng % SSD_GPS_FWD == 0 else SSD_GPS_BWD
    chunk_grp, col_form, row_form, col_par, y_spec, st_spec, _ = _ssd_specs(nc, False, ng, gps)

    def body(px_ref, pb_ref, pc_ref, dt_ref, cum_ref, cumr_ref, ac_ref, dk_ref, y_ref, sp_ref, st_ref):
        @pl.when(pl.program_id(1) == 0)
        def _():
            st_ref[...] = jnp.zeros_like(st_ref)

        for gi in range(gps):
            (px, y), (pb, pc), rest = _ssd_group_views(
                gi, (px_ref, y_ref), (pb_ref, pc_ref), (dt_ref, cum_ref, cumr_ref, ac_ref, dk_ref, sp_ref, st_ref))
            one_group(px, pb, pc, *rest[:5], y, *rest[5:])

    def one_group(px_ref, pb_ref, pc_ref, dt_ref, cum_ref, cumr_ref, ac_ref, dk_ref, y_ref, sp_ref, st_ref):
        pre_v = jnp.concatenate([px_ref[...], pb_ref[...], pc_ref[...]], axis=1)
        v = _ssd_common(pre_v, dt_ref[...], cum_ref[...], cumr_ref[...], ac_ref[...])
        s0 = st_ref[...]
        sp_ref[...] = s0
        r = _dot_nt(v["cm"], s0.astype(BF16))
        y = _expand(v["lam_c"], q) * r + _expand(dk_ref[...], 1) * v["xa"]
        head = _head_of_lane((q, GW), SSD_HEAD_DIM)
        for j in range(SSD_HPG):
            diff = v["cum"][:, j:j + 1] - v["cum_r"][j:j + 1, :]
            w = (v["g"] * jnp.exp(jnp.where(v["tril"], diff, -jnp.inf))).astype(BF16)
            y = y + _dot(w, jnp.where(head == j, v["xdt"], 0.0).astype(BF16))
        y_ref[...] = y
        ds = _dot_tn((v["xdt"] * _expand(v["e_c"], q)).astype(BF16), v["bm"])
        for j in range(SSD_HPG):
            rows = slice(j * SSD_HEAD_DIM, (j + 1) * SSD_HEAD_DIM)
            st_ref[rows, :] = s0[rows, :] * jnp.exp(v["cum_r"][j:j + 1, q - 1:q]) + ds[rows, :]

    return pl.pallas_call(
        body, name="ssd_scan_fwd", grid=(ng // gps, nc),
        in_specs=chunk_grp + [col_form, col_form, row_form, col_par, col_par],
        out_specs=[y_spec, st_spec],
        out_shape=[jax.ShapeDtypeStruct((t, ng * GW), F32), jax.ShapeDtypeStruct((ng, nc, GW, SSD_D_STATE), F32)],
        scratch_shapes=[pltpu.VMEM((gps, GW, SSD_D_STATE), F32)],
        compiler_params=_params("parallel", "arbitrary"))(pre, pre, pre, dt_c, cum_c, cum_r, alog_c, dsk_c)


def _ssd_bwd(dy, pre, states, dt_c, cum_c, cum_r, sgd_c, alog_c, dsk_c):
    t = pre.shape[0]
    ng = pre.shape[1] // GC
    q = SSD_CHUNK
    nc = t // q
    gps = SSD_GPS_BWD
    chunk_grp, col_form, row_form, col_par, y_spec, st_spec, bc_spec = _ssd_specs(nc, True, ng, gps)

    def body(dy_ref, px_ref, pb_ref, pc_ref, sp_ref, dt_ref, cum_ref, cumr_ref, sgd_ref, ac_ref, dk_ref,
             dpx_ref, dpb_ref, dpc_ref, ddt_ref, dbias_ref, dalog_ref, dd_ref, ds_ref):
        @pl.when(pl.program_id(1) == 0)
        def _():
            ds_ref[...] = jnp.zeros_like(ds_ref)

        for gi in range(gps):
            (dy, px, dpx), (pb, pc, dpb, dpc), rest = _ssd_group_views(
                gi, (dy_ref, px_ref, dpx_ref), (pb_ref, pc_ref, dpb_ref, dpc_ref),
                (sp_ref, dt_ref, cum_ref, cumr_ref, sgd_ref, ac_ref, dk_ref, ddt_ref, dbias_ref, dalog_ref, dd_ref,
                 ds_ref))
            one_group(dy, px, pb, pc, *rest[:7], dpx, dpb, dpc, *rest[7:])

    def one_group(dy_ref, px_ref, pb_ref, pc_ref, sp_ref, dt_ref, cum_ref, cumr_ref, sgd_ref, ac_ref, dk_ref,
                  dpx_ref, dpb_ref, dpc_ref, ddt_ref, dbias_ref, dalog_ref, dd_ref, ds_ref):
        first = pl.program_id(1) == 0
        pre_v = jnp.concatenate([px_ref[...], pb_ref[...], pc_ref[...]], axis=1)
        v = _ssd_common(pre_v, dt_ref[...], cum_ref[...], cumr_ref[...], ac_ref[...])
        xa, bm, cm, xdt, cum, cum_r = v["xa"], v["bm"], v["cm"], v["xdt"], v["cum"], v["cum_r"]
        xdt_b = xdt.astype(BF16)
        dy_v = dy_ref[...]
        s0 = sp_ref[...]
        ds1 = ds_ref[...]
        s0b, ds1b = s0.astype(BF16), ds1.astype(BF16)
        head = _head_of_lane((q, GW), SSD_HEAD_DIM)
        lane = lax.broadcasted_iota(jnp.int32, (q, LANES), 1)
        lane1 = lax.broadcasted_iota(jnp.int32, (1, LANES), 1)
        lam_x = _expand(v["lam_c"], q)
        e_x = _expand(v["e_c"], q)

        dxa = _expand(dk_ref[...], 1) * dy_v
        dd = _contract(jnp.sum(dy_v * xa, axis=0, keepdims=True), 1)
        r = _dot_nt(cm, s0b)
        dcum = _contract(dy_v * r * lam_x, q)
        drb = (lam_x * dy_v).astype(BF16)
        dc = _dot(drb, s0b)
        ds0 = _dot_tn(drb, cm)
        extra = jnp.zeros((1, LANES), F32)
        for j in range(SSD_HPG):
            rows = slice(j * SSD_HEAD_DIM, (j + 1) * SSD_HEAD_DIM)
            lam_last = jnp.exp(cum_r[j:j + 1, q - 1:q])
            ds_ref[rows, :] = ds0[rows, :] + lam_last * ds1[rows, :]
            tot = jnp.sum(jnp.sum(ds1[rows, :] * s0[rows, :], axis=1, keepdims=True), axis=0, keepdims=True)
            extra = jnp.where(lane1 == j, lam_last * tot, extra)
        dv = _dot_nt(bm, ds1b)
        db = _dot((xdt * e_x).astype(BF16), ds1b)
        dxdt = e_x * dv
        dee = _contract(dv * xdt, q) * v["e_c"]
        dcum = dcum - dee
        extra = extra + jnp.sum(dee, axis=0, keepdims=True)
        dg = jnp.zeros((q, q), F32)
        for j in range(SSD_HPG):
            diff = cum[:, j:j + 1] - cum_r[j:j + 1, :]
            el = jnp.exp(jnp.where(v["tril"], diff, -jnp.inf))
            gl = v["g"] * el
            dym = jnp.where(head == j, dy_v, 0.0).astype(BF16)
            dwm = _dot_nt(dym, xdt_b)
            dxdt = dxdt + _dot_tn(gl.astype(BF16), dym)
            z = dwm * gl
            rk = jnp.sum(z, axis=1, keepdims=True) - jnp.sum(z.T, axis=1, keepdims=True)
            dcum = jnp.where(lane == j, dcum + rk, dcum)
            dg = dg + dwm * el
        dgb = dg.astype(BF16)
        dc = dc + _dot(dgb, bm)
        db = db + _dot_tn(dgb, cm)
        da = _dot_f32((v["row"] <= v["col"]).astype(F32), dcum) + extra
        ddt = _contract(dxdt * xa, q) + v["a_c"] * da
        dalog = jnp.sum(v["dt"] * da, axis=0, keepdims=True) * v["a_c"]
        dxa = dxa + v["dt_x"] * dxdt
        ddt_raw = jnp.where(lane < SSD_HPG, ddt * sgd_ref[...], 0.0)
        sgrad = _silu_grad(pre_v, v["sg"])
        dpx_ref[...] = dxa * sgrad[:, :GW]
        dpb_ref[...] = db * sgrad[:, GW:GW + SSD_D_STATE]
        dpc_ref[...] = dc * sgrad[:, GW + SSD_D_STATE:]
        ddt_ref[...] = ddt_raw
        _acc(dbias_ref, jnp.sum(ddt_raw, axis=0, keepdims=True), first)
        _acc(dalog_ref, jnp.where(lane1 < SSD_HPG, dalog, 0.0), first)
        _acc(dd_ref, dd, first)

    return pl.pallas_call(
        body, name="ssd_scan_bwd", grid=(ng // gps, nc),
        in_specs=[y_spec] + chunk_grp + [st_spec, col_form, col_form, row_form, col_form, col_par, col_par],
        out_specs=[y_spec, bc_spec, bc_spec, col_form, col_par, col_par, col_par],
        out_shape=[jax.ShapeDtypeStruct((t, ng * GW), F32), jax.ShapeDtypeStruct((t, ng * SSD_D_STATE), F32),
                   jax.ShapeDtypeStruct((t, ng * SSD_D_STATE), F32), jax.ShapeDtypeStruct((ng, t, LANES), F32),
                   jax.ShapeDtypeStruct((ng, 1, LANES), F32), jax.ShapeDtypeStruct((ng, 1, LANES), F32),
                   jax.ShapeDtypeStruct((ng, 1, LANES), F32)],
        scratch_shapes=[pltpu.VMEM((gps, GW, SSD_D_STATE), F32)],
        compiler_params=_params("parallel", "arbitrary"))(dy, pre, pre, pre, states, dt_c, cum_c, cum_r, sgd_c, alog_c,
                                                           dsk_c)


def _gate_norm_fwd(y, zx, norm_w):
    t, di = y.shape
    tr = _tile(t, 256, 8)
    ng = di // GW

    def body(y_ref, z_ref, w_ref, o_ref):
        z = z_ref[...]
        gate = y_ref[...] * (z * _sigmoid(z))
        w = w_ref[...]
        for g in range(ng):
            cols = slice(g * GW, (g + 1) * GW)
            gs = gate[:, cols]
            r = lax.rsqrt(jnp.mean(gs * gs, axis=-1, keepdims=True) + NORM_EPS)
            o_ref[:, cols] = (gs * r * w[:, cols]).astype(BF16)

    row = pl.BlockSpec((tr, di), lambda i: (i, 0))
    return pl.pallas_call(body, name="ssd_gate_norm_fwd", grid=(t // tr,),
                          in_specs=[row, row, pl.BlockSpec((1, di), lambda i: (0, 0))], out_specs=row,
                          out_shape=jax.ShapeDtypeStruct((t, di), BF16), compiler_params=_params("parallel"))(
                              y, zx, norm_w)


def _gate_norm_bwd(dyn, y, zx, norm_w, after):
    t, di = y.shape
    tr = _tile(t, 256, 8)
    ng = di // GW

    def body(d_ref, y_ref, z_ref, w_ref, after_ref, dy_ref, dz_ref, dw_ref):
        z = z_ref[...]
        yv = y_ref[...]
        sg = _sigmoid(z)
        sz = z * sg
        gate = yv * sz
        w = w_ref[...]
        d = d_ref[...]
        dsz = _silu_grad(z, sg)
        dws = []
        for g in range(ng):
            cols = slice(g * GW, (g + 1) * GW)
            dg, dwr = _rms_bwd(gate[:, cols], w[:, cols], d[:, cols])
            dy_ref[:, cols] = dg * sz[:, cols]
            dz_ref[:, cols] = (dg * yv[:, cols] * dsz[:, cols]).astype(BF16)
            dws.append(jnp.sum(dwr, axis=0, keepdims=True))
        first = pl.program_id(0) == 0
        for g in range(ng):
            cols = slice(g * GW, (g + 1) * GW)

            @pl.when(first)
            def _():
                dw_ref[:, cols] = dws[g]

            @pl.when(jnp.logical_not(first))
            def _():
                dw_ref[:, cols] += dws[g]

    row = pl.BlockSpec((tr, di), lambda i: (i, 0))
    vec = pl.BlockSpec((1, di), lambda i: (0, 0))
    return pl.pallas_call(body, name="ssd_gate_norm_bwd", grid=(t // tr,),
                          in_specs=[row, row, row, vec, pl.BlockSpec((8, LANES), lambda i: (0, 0))],
                          out_specs=[row, row, vec],
                          out_shape=[jax.ShapeDtypeStruct((t, di), F32), jax.ShapeDtypeStruct((t, di), BF16),
                                     jax.ShapeDtypeStruct((1, di), F32)],
                          compiler_params=_params("arbitrary"))(dyn, y, zx, norm_w, after)


def _attn_mask_t(n):
    w = ATTN_WINDOW
    kpos = lax.broadcasted_iota(jnp.int32, (2 * w, ATTN_REP * w), 0)
    qpos = lax.broadcasted_iota(jnp.int32, (2 * w, ATTN_REP * w), 1) % w + w
    rel = qpos - kpos
    return (rel >= 0) & (rel < w) & jnp.logical_not((n == 0) & (kpos < w))


def _attn_probs_t(qts, ktb, mask, sink):
    s = _dot_tn(ktb, qts) * (ATTN_HEAD_DIM ** -0.5)
    s = jnp.where(mask, s, -jnp.inf)
    m = jnp.maximum(jnp.max(s, axis=0, keepdims=True), sink)
    e = jnp.exp(s - m)
    es = jnp.exp(sink - m)
    inv = 1.0 / (jnp.sum(e, axis=0, keepdims=True) + es)
    return e * inv, es * inv


def _attn_blocks_t(kv, q_ref, kc_ref, vc_ref, kp_ref, vp_ref):
    hd = ATTN_HEAD_DIM
    rows = slice(kv * hd, (kv + 1) * hd)
    ktb = jnp.concatenate([kp_ref[rows, :], kc_ref[rows, :]], axis=1)
    vtb = jnp.concatenate([vp_ref[rows, :], vc_ref[rows, :]], axis=1)
    qts = jnp.concatenate([q_ref[(kv * ATTN_REP + r) * hd:(kv * ATTN_REP + r + 1) * hd, :]
                           for r in range(ATTN_REP)], axis=1)
    return qts, ktb, vtb


def _attn_specs_t(nb, cur, prev):
    w, hd = ATTN_WINDOW, ATTN_HEAD_DIM
    kd = ATTN_N_KV * hd
    qd = ATTN_REP * kd
    return [pl.BlockSpec((qd, w), lambda n: (0, cur(n))),
            pl.BlockSpec((kd, w), lambda n: (ATTN_REP, cur(n))),
            pl.BlockSpec((kd, w), lambda n: (ATTN_REP + 1, cur(n))),
            pl.BlockSpec((kd, w), lambda n: (ATTN_REP, prev(n))),
            pl.BlockSpec((kd, w), lambda n: (ATTN_REP + 1, prev(n)))]


def _attn_fwd_t(qkv_t, sinks_rep):
    t = qkv_t.shape[1]
    w, hd = ATTN_WINDOW, ATTN_HEAD_DIM
    qd = ATTN_N_KV * ATTN_REP * hd
    nb = t // w

    def body(q_ref, kc_ref, vc_ref, kp_ref, vp_ref, s_ref, o_ref):
        mask = _attn_mask_t(pl.program_id(0))
        for kv in range(ATTN_N_KV):
            qts, ktb, vtb = _attn_blocks_t(kv, q_ref, kc_ref, vc_ref, kp_ref, vp_ref)
            p, _ = _attn_probs_t(qts, ktb, mask, s_ref[kv])
            ots = _dot(vtb, p.astype(BF16))
            for r in range(ATTN_REP):
                h = kv * ATTN_REP + r
                o_ref[h * hd:(h + 1) * hd, :] = ots[:, r * w:(r + 1) * w].astype(BF16)

    return pl.pallas_call(
        body, name="attn_fwd", grid=(nb,),
        in_specs=_attn_specs_t(nb, lambda n: n, lambda n: jnp.maximum(n - 1, 0)) + [
            pl.BlockSpec(sinks_rep.shape, lambda n: (0, 0, 0))],
        out_specs=pl.BlockSpec((qd, w), lambda n: (0, n)),
        out_shape=jax.ShapeDtypeStruct((qd, t), BF16),
        compiler_params=_params("parallel"))(qkv_t, qkv_t, qkv_t, qkv_t, qkv_t, sinks_rep)


def _attn_bwd_t(qkv_t, do_t, sinks_rep):
    t = qkv_t.shape[1]
    w, hd = ATTN_WINDOW, ATTN_HEAD_DIM
    kd = ATTN_N_KV * hd
    qd = ATTN_REP * kd
    nq = ATTN_N_KV * ATTN_REP
    nb = t // w
    rows_all = qd + 2 * kd

    def body(q_ref, kc_ref, vc_ref, kp_ref, vp_ref, do_ref, s_ref, dqkv_ref, bsum_ref, dsk_ref,
             carry_ref, new_ref, bacc_ref, sacc_ref):
        n = pl.program_id(0)

        @pl.when(n == 0)
        def _():
            carry_ref[...] = jnp.zeros_like(carry_ref)
            bacc_ref[...] = jnp.zeros_like(bacc_ref)
            sacc_ref[...] = jnp.zeros_like(sacc_ref)

        @pl.when(n < nb)
        def _():
            mask = _attn_mask_t(n)
            for kv in range(ATTN_N_KV):
                qts, ktb, vtb = _attn_blocks_t(kv, q_ref, kc_ref, vc_ref, kp_ref, vp_ref)
                dots = jnp.concatenate([do_ref[(kv * ATTN_REP + r) * hd:(kv * ATTN_REP + r + 1) * hd, :]
                                        for r in range(ATTN_REP)], axis=1)
                p, ps = _attn_probs_t(qts, ktb, mask, s_ref[kv])
                dpt = _dot_tn(vtb, dots)
                delta = jnp.sum(p * dpt, axis=0, keepdims=True)
                dst = (p * (dpt - delta) * (hd ** -0.5)).astype(BF16)
                dqts = _dot(ktb, dst)
                for r in range(ATTN_REP):
                    h = kv * ATTN_REP + r
                    new_ref[h * hd:(h + 1) * hd, :] = dqts[:, r * w:(r + 1) * w]
                dktb = _dot_nt(qts, dst)
                dvtb = _dot_nt(dots, p.astype(BF16))
                krows = slice(qd + kv * hd, qd + (kv + 1) * hd)
                vrows = slice(qd + kd + kv * hd, qd + kd + (kv + 1) * hd)
                carry_ref[krows, :] += dktb[:, :w]
                carry_ref[vrows, :] += dvtb[:, :w]
                new_ref[krows, :] = dktb[:, w:]
                new_ref[vrows, :] = dvtb[:, w:]
                sacc_ref[kv] += -(ps * delta)

        @pl.when(n >= 1)
        def _():
            done = carry_ref[...]
            dqkv_ref[...] = done.astype(BF16)
            bacc_ref[...] += done

        @pl.when(n < nb)
        def _():
            carry_ref[...] = new_ref[...]

        @pl.when(n == nb)
        def _():
            bsum_ref[...] = jnp.sum(bacc_ref[...], axis=1, keepdims=True)
            lane = lax.broadcasted_iota(jnp.int32, (1, nq), 1)
            dsk = jnp.zeros((1, nq), F32)
            for kv in range(ATTN_N_KV):
                acc = sacc_ref[kv]
                for r in range(ATTN_REP):
                    tot = jnp.sum(acc[:, r * w:(r + 1) * w], axis=1, keepdims=True)
                    dsk = jnp.where(lane == kv * ATTN_REP + r, tot, dsk)
            dsk_ref[...] = dsk

    cur = lambda n: jnp.minimum(n, nb - 1)
    prev = lambda n: jnp.maximum(jnp.minimum(n, nb - 1) - 1, 0)
    return pl.pallas_call(
        body, name="attn_bwd", grid=(nb + 1,),
        in_specs=_attn_specs_t(nb, cur, prev) + [pl.BlockSpec((qd, w), lambda n: (0, cur(n))),
                                                 pl.BlockSpec(sinks_rep.shape, lambda n: (0, 0, 0))],
        out_specs=[pl.BlockSpec((rows_all, w), lambda n: (0, jnp.maximum(n - 1, 0))),
                   pl.BlockSpec((rows_all, 1), lambda n: (0, 0)),
                   pl.BlockSpec((1, nq), lambda n: (0, 0))],
        out_shape=[jax.ShapeDtypeStruct((rows_all, t), BF16), jax.ShapeDtypeStruct((rows_all, 1), F32),
                   jax.ShapeDtypeStruct((1, nq), F32)],
        scratch_shapes=[pltpu.VMEM((rows_all, w), F32), pltpu.VMEM((rows_all, w), F32),
                        pltpu.VMEM((rows_all, w), F32), pltpu.VMEM(sinks_rep.shape, F32)],
        compiler_params=_params("arbitrary"))(qkv_t, qkv_t, qkv_t, qkv_t, qkv_t, do_t, sinks_rep)


HBM_SPEC = pl.BlockSpec(memory_space=pl.ANY)
HBM_ONLY = pl.BlockSpec(memory_space=pltpu.HBM)


def _comm_call(name, body, ins, out_shapes, n_sems):
    return pl.pallas_call(
        body, name=name, in_specs=[HBM_SPEC] * len(ins), out_specs=[HBM_SPEC] * len(out_shapes),
        out_shape=out_shapes,
        scratch_shapes=[pltpu.SemaphoreType.DMA((s,)) for s in n_sems])(*ins)


def _all_gather(name, shards, after):
    n = len(shards)
    na = len(after)

    def body(*refs):
        x_refs, out_refs = refs[:n], refs[n + na:2 * n + na]
        send_sems, recv_sems, local_sems = refs[2 * n + na:]
        x, y, c = lax.axis_index("x"), lax.axis_index("y"), lax.axis_index("c")
        me, sibling = (x, y, c), (x, y, 1 - c)
        chips = [(1 - x, y), (x, 1 - y), (1 - x, 1 - y)]

        def slot(i, px, py, pc):
            return out_refs[i].at[4 * px + 2 * py + pc]

        def copy(k, i, block, to, src=None):
            return pltpu.make_async_remote_copy(
                src_ref=slot(i, *block) if src is None else src, dst_ref=slot(i, *block),
                send_sem=send_sems.at[k * n + i], recv_sem=recv_sems.at[k * n + i], device_id=to,
                device_id_type=MESH)

        mine = [pltpu.make_async_copy(x_refs[i], slot(i, *me), local_sems.at[i]) for i in range(n)]
        first = []
        for i in range(n):
            mine[i].start()
            first.append(copy(0, i, me, sibling, src=x_refs[i]))
            first += [copy(1 + j, i, me, (*chip, c), src=x_refs[i]) for j, chip in enumerate(chips)]
        for cp in first:
            cp.start()
        passed = []
        for i in range(n):
            for j, chip in enumerate(chips):
                copy(1 + j, i, (*chip, c), me).wait_recv()
                passed.append(copy(4 + j, i, (*chip, c), sibling))
                passed[-1].start()
        for i in range(n):
            copy(0, i, sibling, me).wait_recv()
            for j, chip in enumerate(chips):
                copy(4 + j, i, (*chip, 1 - c), me).wait_recv()
        for cp in first + passed:
            cp.wait_send()
        for cp in mine:
            cp.wait()

    outs = [jax.ShapeDtypeStruct((N_DEV,) + s.shape, s.dtype) for s in shards]
    return _comm_call(name, body, list(shards) + list(after), outs, (7 * n, 7 * n, n))


SEM_SPEC = pl.BlockSpec(memory_space=pltpu.SEMAPHORE)
SPLIT_COPY_EFFECT = pltpu.SideEffectType.DATAFLOW_SIDE_EFFECTING


def _in_hbm(a):
    return pltpu.with_memory_space_constraint(a, pltpu.HBM)


def _split_start(name, body, srcs, lands, n_sems):
    n = len(srcs)
    bufs = [_in_hbm(a) for a in list(srcs) + list(lands)]
    outs = pl.pallas_call(
        body, name=name,
        out_shape=(pltpu.SemaphoreType.DMA((n_sems,)), pltpu.SemaphoreType.DMA((n_sems,)),
                   *[pltpu.HBM(a.shape, a.dtype) for a in bufs], jax.ShapeDtypeStruct((8, LANES), F32)),
        in_specs=[HBM_ONLY] * (2 * n),
        out_specs=(SEM_SPEC, SEM_SPEC, *[HBM_ONLY] * (2 * n), pl.BlockSpec(memory_space=pltpu.VMEM)),
        input_output_aliases={i: 2 + i for i in range(2 * n)},
        compiler_params=pltpu.CompilerParams(has_side_effects=SPLIT_COPY_EFFECT))(*bufs)
    return outs[0], outs[1], list(outs[2:2 + n]), list(outs[2 + n:2 + 2 * n]), outs[-1]


def _split_wait(name, body, send_sems, recv_sems, srcs, lands, after):
    n = len(srcs)
    outs = pl.pallas_call(
        body, name=name,
        out_shape=[pltpu.HBM(a.shape, a.dtype) for a in list(srcs) + list(lands)],
        in_specs=[HBM_ONLY] * (2 * n) + [SEM_SPEC, SEM_SPEC, HBM_SPEC],
        out_specs=[HBM_ONLY] * (2 * n),
        input_output_aliases={i: i for i in range(2 * n)},
        compiler_params=pltpu.CompilerParams(has_side_effects=SPLIT_COPY_EFFECT))(
            *srcs, *lands, send_sems, recv_sems, after)
    return list(outs[:n]), list(outs[n:])


N_PEERS = N_DEV - 1


def _gather_peers():
    x, y, c = lax.axis_index("x"), lax.axis_index("y"), lax.axis_index("c")
    flips = [(fx, fy, fc) for fx in (0, 1) for fy in (0, 1) for fc in (0, 1) if fx or fy or fc]
    return [(1 - x if fx else x, 1 - y if fy else y, 1 - c if fc else c) for fx, fy, fc in flips]


def _block_id(dev):
    return 4 * dev[0] + 2 * dev[1] + dev[2]


def _landing_block(land_ref, shard_shape, side_by_side, dev):
    if not side_by_side:
        return land_ref.at[_block_id(dev)]
    cols = shard_shape[1]
    return land_ref.at[:, pl.ds(pl.multiple_of(_block_id(dev) * cols, LANES), cols)]


def _gather_start(name, shards, side_by_side):
    n = len(shards)

    def body(*refs):
        x_refs, land_refs = refs[:n], refs[n:2 * n]
        send_sems, recv_sems, token = refs[2 * n], refs[2 * n + 1], refs[-1]
        me = (lax.axis_index("x"), lax.axis_index("y"), lax.axis_index("c"))
        for i in range(n):
            for k, peer in enumerate(_gather_peers()):
                pltpu.make_async_remote_copy(
                    src_ref=x_refs[i], dst_ref=_landing_block(land_refs[i], shards[i].shape, side_by_side[i], me),
                    send_sem=send_sems.at[N_PEERS * i + k], recv_sem=recv_sems.at[N_PEERS * i + k],
                    device_id=peer, device_id_type=MESH).start()
            pltpu.make_async_copy(x_refs[i], _landing_block(land_refs[i], shards[i].shape, side_by_side[i], me),
                                  send_sems.at[N_PEERS * n + i]).start()
        token[...] = jnp.zeros_like(token)

    lands = [lax.empty((s.shape[0], N_DEV * s.shape[1]) if wide else (N_DEV,) + s.shape, s.dtype)
             for s, wide in zip(shards, side_by_side)]
    return _split_start(name, body, shards, lands, (N_PEERS + 1) * n)


def _gather_wait(name, send_sems, recv_sems, first, n_all, shards, lands, side_by_side, after):
    n = len(shards)

    def body(*refs):
        x_refs, land_refs = refs[:n], refs[n:2 * n]
        send_sems, recv_sems = refs[2 * n], refs[2 * n + 1]
        me = (lax.axis_index("x"), lax.axis_index("y"), lax.axis_index("c"))
        for i in range(n):
            pltpu.make_async_copy(x_refs[i], _landing_block(land_refs[i], shards[i].shape, side_by_side[i], me),
                                  send_sems.at[N_PEERS * n_all + first + i]).wait()
            for k, peer in enumerate(_gather_peers()):
                cp = pltpu.make_async_remote_copy(
                    src_ref=x_refs[i], dst_ref=_landing_block(land_refs[i], shards[i].shape, side_by_side[i], peer),
                    send_sem=send_sems.at[N_PEERS * (first + i) + k],
                    recv_sem=recv_sems.at[N_PEERS * (first + i) + k],
                    device_id=peer, device_id_type=MESH)
                cp.wait_send()
                cp.wait_recv()

    return _split_wait(name, body, send_sems, recv_sems, shards, lands, after)


def _scatter_start(name, blocks):
    n = len(blocks)

    def body(*refs):
        b_refs, land_refs = refs[:n], refs[n:2 * n]
        send_sems, recv_sems, token = refs[2 * n], refs[2 * n + 1], refs[-1]
        me = (lax.axis_index("x"), lax.axis_index("y"), lax.axis_index("c"))
        for i in range(n):
            for k, peer in enumerate(_gather_peers()):
                pltpu.make_async_remote_copy(
                    src_ref=b_refs[i].at[_block_id(peer)], dst_ref=land_refs[i].at[_block_id(me)],
                    send_sem=send_sems.at[N_PEERS * i + k], recv_sem=recv_sems.at[N_PEERS * i + k],
                    device_id=peer, device_id_type=MESH).start()
            pltpu.make_async_copy(b_refs[i].at[_block_id(me)], land_refs[i].at[_block_id(me)],
                                  send_sems.at[N_PEERS * n + i]).start()
        token[...] = jnp.zeros_like(token)

    lands = [lax.empty(b.shape, b.dtype) for b in blocks]
    return _split_start(name, body, blocks, lands, (N_PEERS + 1) * n)


def _scatter_wait(name, send_sems, recv_sems, blocks, lands, after):
    n = len(blocks)

    def body(*refs):
        b_refs, land_refs = refs[:n], refs[n:2 * n]
        send_sems, recv_sems = refs[2 * n], refs[2 * n + 1]
        me = (lax.axis_index("x"), lax.axis_index("y"), lax.axis_index("c"))
        for i in range(n):
            pltpu.make_async_copy(b_refs[i].at[_block_id(me)], land_refs[i].at[_block_id(me)],
                                  send_sems.at[N_PEERS * n + i]).wait()
            for k, peer in enumerate(_gather_peers()):
                cp = pltpu.make_async_remote_copy(
                    src_ref=b_refs[i].at[_block_id(peer)], dst_ref=land_refs[i].at[_block_id(peer)],
                    send_sem=send_sems.at[N_PEERS * i + k], recv_sem=recv_sems.at[N_PEERS * i + k],
                    device_id=peer, device_id_type=MESH)
                cp.wait_send()
                cp.wait_recv()

    return _split_wait(name, body, send_sems, recv_sems, blocks, lands, after)


def _adamw(w, g, m, v):
    m = ADAM_B1 * m + (1.0 - ADAM_B1) * g
    v = ADAM_B2 * v + (1.0 - ADAM_B2) * (g * g)
    m_hat = m / (1.0 - ADAM_B1 ** ADAM_STEP)
    v_hat = v / (1.0 - ADAM_B2 ** ADAM_STEP)
    delta = -ADAM_LR * (m_hat / (jnp.sqrt(v_hat) + ADAM_EPS) + ADAM_WD * w)
    return delta, m, v


def _adamw_tiles(r, c_):
    tr = _tile(r, 256, 16)
    return (tr, c_) if tr < r or r <= 256 else (r, _tile(c_, 256))


def _sum_parts(part):
    g = part[0].astype(F32)
    for k in range(1, part.shape[0]):
        g = g + part[k].astype(F32)
    return g


def _sum_adamw(name, parts, w, m, v):
    r, c_ = w.shape
    tr, tc = _adamw_tiles(r, c_)

    def body(p_ref, w_ref, m_ref, v_ref, g_ref, d_ref, nm_ref, nv_ref):
        g = _sum_parts(p_ref)
        g_ref[...] = g
        d_ref[...], nm_ref[...], nv_ref[...] = _adamw(w_ref[...], g, m_ref[...], v_ref[...])

    tile = pl.BlockSpec((tr, tc), lambda i, j: (i, j))
    return pl.pallas_call(body, name=name, grid=(r // tr, c_ // tc),
                          in_specs=[pl.BlockSpec((parts.shape[0], tr, tc), lambda i, j: (0, i, j)), tile, tile, tile],
                          out_specs=[tile] * 4, out_shape=[jax.ShapeDtypeStruct((r, c_), F32)] * 4,
                          compiler_params=_params("parallel", "parallel"))(parts, w, m, v)


def _sum_adamw_layers(name, parts, w, m, v):
    n_layers, r, c_ = w.shape
    tr = _tile(r, 256, 16)

    def body(*refs):
        p_refs = refs[:n_layers]
        w_ref, m_ref, v_ref, g_ref, d_ref, nm_ref, nv_ref = refs[n_layers:]
        layer = pl.program_id(0)
        g = _sum_parts(p_refs[0])
        for li in range(1, n_layers):
            g = jnp.where(layer == li, _sum_parts(p_refs[li]), g)
        g_ref[...] = g
        d_ref[...], nm_ref[...], nv_ref[...] = _adamw(w_ref[...], g, m_ref[...], v_ref[...])

    row = pl.BlockSpec((None, tr, c_), lambda l, i: (l, i, 0))
    specs = [pl.BlockSpec((p.shape[0], tr, c_), lambda l, i: (0, i, 0)) for p in parts]
    return pl.pallas_call(body, name=name, grid=(n_layers, r // tr), in_specs=specs + [row, row, row],
                          out_specs=[row] * 4, out_shape=[jax.ShapeDtypeStruct(w.shape, F32)] * 4,
                          compiler_params=_params("parallel", "parallel"))(*parts, w, m, v)


def _pack_rows(flat, n_rows, cols):
    pad = n_rows * cols - flat.shape[-1]
    flat = jnp.pad(flat, [(0, 0)] * (flat.ndim - 1) + [(0, pad)])
    return flat.reshape(flat.shape[:-1] + (n_rows, cols))


def _cols_split(full):
    c = full.shape[1] // N_DEV
    return jnp.stack([full[:, d * c:(d + 1) * c] for d in range(N_DEV)])


def _rows_join(blocks):
    return blocks.reshape(N_DEV * blocks.shape[1], blocks.shape[2])


def _rows_split(full):
    return full.reshape(N_DEV, full.shape[0] // N_DEV, full.shape[1])


def _heads_col(v, ng):
    return jnp.pad(v.reshape(ng, 1, SSD_HPG), ((0, 0), (0, 0), (0, LANES - SSD_HPG)))


MATRIX_ITEMS = ("w_in", "w_out", "up0", "down0", "w_qkv", "w_o", "up1", "down1")
VECTOR_ITEMS = ("conv_w", "b_qkv", "b_o")
ITEMS = MATRIX_ITEMS + VECTOR_ITEMS
GATHER_STAGES = (("w_in", "conv_w"), ("w_out", "up0", "down0"), ("w_qkv", "b_qkv", "w_o", "b_o", "up1", "down1"))
SIDE_BY_SIDE = ("conv_w", "up0", "up1", "b_o")


def _items(tree, prefix=""):
    g = lambda k: tree[prefix + k]
    return {"w_in": g("ssd_w_in")[0].T, "w_out": g("ssd_w_out")[0], "w_qkv": g("attn_w_qkv")[0].T,
            "w_o": g("attn_w_o")[0], "up0": g("mlp_w_up")[0], "up1": g("mlp_w_up")[1],
            "down0": g("mlp_w_down")[0], "down1": g("mlp_w_down")[1], "conv_w": g("ssd_conv_w")[0],
            "b_qkv": g("attn_b_qkv"), "b_o": g("attn_b_o")}


REPLICATED = ("ssd_conv_b", "ssd_dt_bias", "ssd_a_log", "ssd_d", "ssd_norm_w", "attn_sinks", "mix_pre_norm",
              "mix_post_norm", "ffn_pre_norm", "ffn_post_norm")
WEIGHTS = ("ssd_w_in", "ssd_conv_w", "ssd_conv_b", "ssd_dt_bias", "ssd_a_log", "ssd_d", "ssd_norm_w", "ssd_w_out",
           "attn_w_qkv", "attn_b_qkv", "attn_sinks", "attn_w_o", "attn_b_o", "mlp_w_up", "mlp_w_down",
           "mix_pre_norm", "mix_post_norm", "ffn_pre_norm", "ffn_post_norm")


def _forward_backward(x, target, rep, token, weights_of_stage, reduce_grads):
    t, d = x.shape
    ng = rep["ssd_norm_w"].shape[1] // GW
    di = ng * GW
    n_xbc = ng * GC
    nh = ng * SSD_HPG
    grads, blocks = {}, {}
    w_up, w_down = [None, None], [None, None]
    sinks_rep = jnp.repeat(rep["attn_sinks"].reshape(ATTN_N_KV, ATTN_REP, 1), ATTN_WINDOW, axis=2).reshape(
        ATTN_N_KV, 1, ATTN_REP * ATTN_WINDOW)
    conv_b = rep["ssd_conv_b"]
    gn = ng * SSD_D_STATE
    parts = ((0, di), (di, di), (2 * di, gn), (2 * di + gn, gn), (di + n_xbc, nh))
    alog_c, dsk_c = (_heads_col(rep[k], ng) for k in ("ssd_a_log", "ssd_d"))
    bias_l, alog_l = (jnp.pad(rep[k], ((0, 0), (0, LANES - nh))) for k in ("ssd_dt_bias", "ssd_a_log"))
    norm = {k: rep[k] for k in ("mix_pre_norm", "mix_post_norm", "ffn_pre_norm", "ffn_post_norm")}

    def nrow(name, i):
        return norm[name][i:i + 1]

    def mlp_fwd(i, u2):
        p = _mm(f"mlp{i}_up", [u2], [w_up[i]], "nn", tm=1024, tn=1024, out_dtypes=(BF16,),
                epilogue=lambda acc: (jnp.square(jnp.maximum(acc, 0.0)),))
        f = _mm(f"mlp{i}_down", [p], [w_down[i]], "nn", tm=512, tn=1024)
        return p, f

    def mlp_bwd(i, df, u2, p):
        da = _mm(f"mlp{i}_dact", [df], [w_down[i]], "nt", tm=1024, tn=1024, out_dtypes=(BF16,),
                 tiles=(p,), epilogue=lambda acc, pv: (acc * (2.0 * jnp.sqrt(pv.astype(F32))),))
        blocks[f"down{i}"] = _rows_split(_mm(f"mlp{i}_dwdown", [p], [df], "tn", tm=512, tn=1024,
                                             out_dtypes=(PAYLOAD,)))
        blocks[f"up{i}"] = _mm(f"mlp{i}_dwup", [u2], [da], "tn", tm=1024, tn=da.shape[1] // N_DEV,
                               out_dtypes=(PAYLOAD,), col_blocks=True)
        return _mm(f"mlp{i}_dx", [da], [w_up[i]], "nt", tm=512, tn=1024)

    u0 = _prenorm("l0_prenorm", x, nrow("mix_pre_norm", 0), token)
    got = weights_of_stage(0, u0)
    w_in_t = _rows_join(got["w_in"])
    w_dt_t = jnp.pad(w_in_t[di + n_xbc:], ((0, LANES - nh), (0, 0)))
    conv_w = got["conv_w"]
    zx = _mm("ssd_in_proj", [u0], [w_in_t], "nt", tm=1024, tn=1024, n_use=di + n_xbc)
    zdt = _mm("ssd_dt_proj", [u0], [w_dt_t], "nt", tm=1024, tn=LANES)
    pre = _conv_fwd(zx, di, n_xbc, conv_w, conv_b)
    dt_c, cum_c, cum_r, sgd_c = _ssd_dt_prep(zdt, bias_l, alog_l, ng)
    y, states = _ssd_fwd(pre, dt_c, cum_c, cum_r, alog_c, dsk_c)
    yn = _gate_norm_fwd(y, zx, rep["ssd_norm_w"])
    got = weights_of_stage(1, yn)
    w_out = _rows_join(got["w_out"])
    w_up[0], w_down[0] = got["up0"], _rows_join(got["down0"])
    mix0 = _mm("ssd_out_proj", [yn], [w_out], "nn", tm=1024, tn=1024)
    h1, u0f = _post_pre("l0_mid", x, mix0, nrow("mix_post_norm", 0), nrow("ffn_pre_norm", 0))
    p0, f0 = mlp_fwd(0, u0f)
    h2, u1 = _post_pre("l1_in", h1, f0, nrow("ffn_post_norm", 0), nrow("mix_pre_norm", 1))
    got = weights_of_stage(2, u1)
    w_qkv_t = _rows_join(got["w_qkv"])
    w_o = _rows_join(got["w_o"])
    b_qkv_col = got["b_qkv"].reshape(-1, 1)
    b_o = got["b_o"]
    w_up[1], w_down[1] = got["up1"], _rows_join(got["down1"])
    qkv_t = _mm("attn_qkv_proj", [w_qkv_t], [u1], "nt", tm=768, tn=1024, out_dtypes=(BF16,), cols=(b_qkv_col,),
                epilogue=lambda acc, b: (acc + b,))
    ao_t = _attn_fwd_t(qkv_t, sinks_rep)
    mix1 = _mm("attn_out_proj", [ao_t], [w_o], "tn", tm=1024, tn=1024, rows=(b_o,),
               epilogue=lambda acc, b: (acc + b,))
    h3, u1f = _post_pre("l1_mid", h2, mix1, nrow("mix_post_norm", 1), nrow("ffn_pre_norm", 1))
    p1, f1 = mlp_fwd(1, u1f)
    dh, loss_row = _final_loss("loss", h3, f1, nrow("ffn_post_norm", 1), target)

    g_norm = {k: [None, None] for k in norm}
    df1, g_norm["ffn_post_norm"][1], _ = _norm_bwd("l1_ffn_post_bwd", dh, post=(f1, nrow("ffn_post_norm", 1)))
    du = mlp_bwd(1, df1, u1f, p1)
    sent = reduce_grads("mlp1", {k: blocks[k] for k in ("up1", "down1")})
    dh, g_norm["ffn_pre_norm"][1], dmix1, g_norm["mix_post_norm"][1], db_o = _norm_bwd(
        "l1_mid_bwd", dh, pre=(du, h3, nrow("ffn_pre_norm", 1)), post=(mix1, nrow("mix_post_norm", 1)), after=sent)
    blocks["b_o"] = _cols_split(db_o)
    blocks["w_o"] = _rows_split(_mm("attn_dwo", [ao_t], [dmix1], "nn", tm=512, tn=1024, out_dtypes=(PAYLOAD,)))
    dao_t = _mm("attn_dout", [w_o], [dmix1], "nt", tm=1024, tn=1024, out_dtypes=(BF16,))
    dqkv_t, db_qkv, grads["attn_sinks"] = _attn_bwd_t(qkv_t, dao_t, sinks_rep)
    blocks["b_qkv"] = db_qkv.reshape(N_DEV, 1, -1)
    blocks["w_qkv"] = _rows_split(_mm("attn_dwqkv", [dqkv_t], [u1], "nn", tm=512, tn=1024, out_dtypes=(PAYLOAD,)))
    du = _mm("attn_dx", [dqkv_t], [w_qkv_t], "tn", tm=1024, tn=1024)
    sent = reduce_grads("attn", {k: blocks[k] for k in ("w_o", "w_qkv", "b_o", "b_qkv")})
    dh, g_norm["mix_pre_norm"][1], df0, g_norm["ffn_post_norm"][0], _ = _norm_bwd(
        "l1_in_bwd", dh, pre=(du, h2, nrow("mix_pre_norm", 1)), post=(f0, nrow("ffn_post_norm", 0)), after=sent)
    du = mlp_bwd(0, df0, u0f, p0)
    sent = reduce_grads("mlp0", {k: blocks[k] for k in ("up0", "down0")})
    dh, g_norm["ffn_pre_norm"][0], dmix0, g_norm["mix_post_norm"][0], _ = _norm_bwd(
        "l0_mid_bwd", dh, pre=(du, h1, nrow("ffn_pre_norm", 0)), post=(mix0, nrow("mix_post_norm", 0)), after=sent)
    blocks["w_out"] = _rows_split(_mm("ssd_dwout", [yn], [dmix0], "tn", tm=512, tn=1024, out_dtypes=(PAYLOAD,)))
    dyn = _mm("ssd_dyn", [dmix0], [w_out], "nt", tm=1024, tn=1024)
    sent = reduce_grads("ssdout", {"w_out": blocks["w_out"]})
    dy, dz, grads["ssd_norm_w"] = _gate_norm_bwd(dyn, y, zx, rep["ssd_norm_w"], sent)
    dpx, dpb, dpc, ddt_g, dbias_g, dalog_g, dd_g = _ssd_bwd(dy, pre, states, dt_c, cum_c, cum_r, sgd_c, alog_c,
                                                             dsk_c)
    conv_out = [_conv_bwd(f"ssd_conv_bwd_{tag}", dp, zx, c0, conv_w[:, c0 - di:c0 - di + n])
                for tag, dp, (c0, n) in zip("xbc", (dpx, dpb, dpc), parts[1:4])]
    dconv_w = jnp.concatenate([o[1] for o in conv_out], axis=1)
    dconv_b = jnp.concatenate([o[2] for o in conv_out], axis=1)
    ddt = jnp.transpose(ddt_g[:, :, :SSD_HPG], (1, 0, 2)).reshape(t, nh)
    ddt = jnp.pad(ddt, ((0, 0), (0, LANES - nh))).astype(BF16)
    blocks["conv_w"] = _cols_split(dconv_w)
    grads["ssd_conv_b"] = dconv_b
    for name, val in (("ssd_dt_bias", dbias_g), ("ssd_a_log", dalog_g), ("ssd_d", dd_g)):
        grads[name] = val[:, 0, :SSD_HPG].reshape(1, nh)
    d_zx = [dz] + [o[0] for o in conv_out] + [ddt]
    dw_parts = [_mm(f"ssd_dw_{tag}", [d], [u0], "tn", tm=512, tn=1024, out_dtypes=(PAYLOAD,))
                for tag, d in zip("zxbct", d_zx)]
    dw_parts[-1] = dw_parts[-1][:nh]
    blocks["w_in"] = _rows_split(jnp.concatenate(dw_parts, axis=0))
    sent = reduce_grads("ssd", {k: blocks[k] for k in ("w_in", "conv_w")})
    w_parts = [w_in_t[r0:r0 + n] for r0, n in parts[:-1]] + [w_dt_t]
    du = _mm("ssd_dx", d_zx, w_parts, "nn", tm=256, tn=1024, after=sent)
    grad_x, g_norm["mix_pre_norm"][0] = _norm_bwd("l0_in_bwd", dh, pre=(du, x, nrow("mix_pre_norm", 0)), after=sent)
    for k in norm:
        grads[k] = jnp.concatenate(g_norm[k], axis=0)
    return loss_row, grad_x, grads


def kernel(x, ssd_w_in, ssd_conv_w, ssd_conv_b, ssd_dt_bias, ssd_a_log, ssd_d, ssd_norm_w, ssd_w_out, attn_w_qkv, attn_b_qkv, attn_sinks, attn_w_o, attn_b_o, mlp_w_up, mlp_w_down, mix_pre_norm, mix_post_norm, ffn_pre_norm, ffn_post_norm, loss_target, m_ssd_w_in, m_ssd_conv_w, m_ssd_conv_b, m_ssd_dt_bias, m_ssd_a_log, m_ssd_d, m_ssd_norm_w, m_ssd_w_out, m_attn_w_qkv, m_attn_b_qkv, m_attn_sinks, m_attn_w_o, m_attn_b_o, m_mlp_w_up, m_mlp_w_down, m_mix_pre_norm, m_mix_post_norm, m_ffn_pre_norm, m_ffn_post_norm, v_ssd_w_in, v_ssd_conv_w, v_ssd_conv_b, v_ssd_dt_bias, v_ssd_a_log, v_ssd_d, v_ssd_norm_w, v_ssd_w_out, v_attn_w_qkv, v_attn_b_qkv, v_attn_sinks, v_attn_w_o, v_attn_b_o, v_mlp_w_up, v_mlp_w_down, v_mix_pre_norm, v_mix_post_norm, v_ffn_pre_norm, v_ffn_post_norm):
    given = dict(locals())
    w = {k: given[k] for k in WEIGHTS}
    mom_m = {k: given["m_" + k] for k in WEIGHTS}
    mom_v = {k: given["v_" + k] for k in WEIGHTS}
    w_it, m_it, v_it = _items(given), _items(given, "m_"), _items(given, "v_")

    order = [k for stage in GATHER_STAGES for k in stage]
    shards = [w_it[k].astype(PAYLOAD) if k in MATRIX_ITEMS else w_it[k] for k in order]
    wide = [k in SIDE_BY_SIDE for k in order]
    g_send, g_recv, shards, lands, token = _gather_start("gather_start", shards, wide)

    def weights_of_stage(s, after):
        first = sum(len(stage) for stage in GATHER_STAGES[:s])
        sl = slice(first, first + len(GATHER_STAGES[s]))
        _, got = _gather_wait(f"gather_wait{s}", g_send, g_recv, first, len(order), shards[sl], lands[sl], wide[sl],
                              after)
        return dict(zip(GATHER_STAGES[s], got))

    in_flight = []

    def reduce_grads(tag, blocks):
        keys = list(blocks)
        started = _scatter_start(f"rs_start_{tag}", [blocks[k] for k in keys])
        in_flight.append((tag, keys, started))
        return started[-1]

    rep = {k: w[k] for k in REPLICATED}
    loss_row, grad_x, grads = _forward_backward(x[0], loss_target[0], rep, token, weights_of_stage, reduce_grads)

    def pack_rep(tree, last):
        flat = jnp.concatenate([tree[k].reshape(-1) for k in REPLICATED] + [last])
        return _pack_rows(flat, _round_up(-(-flat.shape[0] // LANES), 8), LANES)

    landed = {}

    def wait_group(group, after):
        tag, keys, (s_send, s_recv, srcs, s_lands, _) = group
        _, got = _scatter_wait(f"rs_wait_{tag}", s_send, s_recv, srcs, s_lands, after)
        landed.update(zip(keys, got))

    def adamw_item(k):
        return _sum_adamw(f"adamw_{k}", landed[k], w_it[k], m_it[k], v_it[k])

    def adamw_stack(name, keys):
        return _sum_adamw_layers(f"adamw_{name}", [landed[k] for k in keys], given[name], given["m_" + name],
                                 given["v_" + name])

    for group in in_flight[:-1]:
        wait_group(group, grad_x)
    done = {"mlp_w_up": adamw_stack("mlp_w_up", ("up0", "up1")),
            "mlp_w_down": adamw_stack("mlp_w_down", ("down0", "down1")),
            "attn_w_qkv": [o.T[None] for o in adamw_item("w_qkv")],
            "attn_w_o": [o[None] for o in adamw_item("w_o")],
            "attn_b_qkv": adamw_item("b_qkv"), "attn_b_o": adamw_item("b_o"),
            "ssd_w_out": [o[None] for o in adamw_item("w_out")]}
    partials, = _all_gather("gather_small_grads", [pack_rep(grads, loss_row[0, :1])],
                            [outs4[0] for outs4 in done.values()])
    wait_group(in_flight[-1], partials)
    done["ssd_w_in"] = [o.T[None] for o in adamw_item("w_in")]
    done["ssd_conv_w"] = [o[None] for o in adamw_item("conv_w")]
    zero = jnp.zeros((1,), F32)
    rep_out = _sum_adamw("adamw_replicated", partials, pack_rep(w, zero), pack_rep(mom_m, zero), pack_rep(mom_v, zero))

    kinds = []
    for kind, r_arr in enumerate(rep_out):
        tree = {name: outs4[kind] for name, outs4 in done.items()}
        flat, off = r_arr.reshape(-1), 0
        for k in REPLICATED:
            tree[k] = flat[off:off + w[k].size].reshape(w[k].shape)
            off += w[k].size
        kinds.append(tree)
    loss = rep_out[0].reshape(-1)[off]
    outs = [loss, grad_x[None]]
    for tree in kinds:
        outs += [tree[k] for k in WEIGHTS]
    return tuple(outs)
```

```python
import jax
import jax.numpy as jnp
from jax import lax
from jax.experimental import pallas as pl
from jax.experimental.pallas import tpu as pltpu

F32 = jnp.float32
BF16 = jnp.bfloat16
PAYLOAD = jnp.bfloat16
HIGHEST = lax.Precision.HIGHEST
MESH = pl.DeviceIdType.MESH

NORM_EPS = 1e-6
SSD_HEAD_DIM = 64
SSD_HPG = 4
SSD_D_STATE = 128
SSD_CONV_WIDTH = 4
SSD_CHUNK = 128
ATTN_HEAD_DIM = 64
ATTN_N_KV = 4
ATTN_REP = 4
ATTN_WINDOW = 128
ADAM_LR = 0.001
ADAM_B1 = 0.9
ADAM_B2 = 0.999
ADAM_EPS = 1e-08
ADAM_WD = 0.01
ADAM_STEP = 10

N_DEV = 8
LANES = 128
V7X_VMEM_LIMIT = 56 * 1024 * 1024

GW = SSD_HPG * SSD_HEAD_DIM
GC = GW + 2 * SSD_D_STATE


def _params(*sem):
    return pltpu.CompilerParams(dimension_semantics=sem, vmem_limit_bytes=V7X_VMEM_LIMIT)


def _tile(n, pref, mult=LANES):
    best = None
    t = mult
    while t <= min(n, pref):
        if n % t == 0:
            best = t
        t += mult
    return best if best is not None else n


def _round_up(n, m):
    return (n + m - 1) // m * m


def _acc(ref, val, first):
    @pl.when(first)
    def _():
        ref[...] = val

    @pl.when(jnp.logical_not(first))
    def _():
        ref[...] += val


def _dot(a, b):
    return lax.dot_general(a, b, (((1,), (0,)), ((), ())), preferred_element_type=F32)


def _dot_nt(a, b):
    return lax.dot_general(a, b, (((1,), (1,)), ((), ())), preferred_element_type=F32)


def _dot_tn(a, b):
    return lax.dot_general(a, b, (((0,), (0,)), ((), ())), preferred_element_type=F32)


def _dot_f32(a, b):
    return lax.dot_general(a, b, (((1,), (0,)), ((), ())), preferred_element_type=F32, precision=HIGHEST)


_DOTS = {"nn": _dot, "nt": _dot_nt, "tn": _dot_tn}


def _sigmoid(x):
    return 1.0 / (1.0 + jnp.exp(-x))


def _softplus(x):
    return jnp.maximum(x, 0.0) + jnp.log1p(jnp.exp(-jnp.abs(x)))


def _silu_grad(x, s):
    return s * (1.0 + x * (1.0 - s))


def _mm(name, a_list, b_list, mode, *, tm, tn, out_dtypes=(F32,), epilogue=None, tiles=(), rows=(), cols=(),
        col_blocks=False, n_use=None, after=None):
    npair = len(a_list)
    if mode == "tn":
        m = a_list[0].shape[1]
    else:
        m = a_list[0].shape[0]
    n = n_use if n_use is not None else (b_list[0].shape[0] if mode == "nt" else b_list[0].shape[1])
    tm = _tile(m, tm, LANES if mode == "tn" else 8)
    tn = _tile(n, tn)
    assert m % tm == 0 and n % tn == 0, (name, m, n, tm, tn)
    dot = _DOTS[mode]

    def body(*refs):
        a_refs = refs[:npair]
        b_refs = refs[npair:2 * npair]
        n_extra = len(tiles) + len(rows) + len(cols)
        e_refs = refs[2 * npair:2 * npair + n_extra]
        o_refs = refs[2 * npair + n_extra + len(order):]
        acc = None
        for ar, br in zip(a_refs, b_refs):
            d = dot(ar[...], br[...])
            acc = d if acc is None else acc + d
        outs = epilogue(acc, *[e[...] for e in e_refs]) if epilogue is not None else (acc,)
        for o, v in zip(o_refs, outs):
            o[...] = v.astype(o.dtype)

    in_specs = []
    for a in a_list:
        if mode == "tn":
            in_specs.append(pl.BlockSpec((a.shape[0], tm), lambda i, j: (0, i)))
        else:
            in_specs.append(pl.BlockSpec((tm, a.shape[1]), lambda i, j: (i, 0)))
    for b in b_list:
        if mode == "nt":
            in_specs.append(pl.BlockSpec((tn, b.shape[1]), lambda i, j: (j, 0)))
        else:
            in_specs.append(pl.BlockSpec((b.shape[0], tn), lambda i, j: (0, j)))
    in_specs += [pl.BlockSpec((tm, tn), lambda i, j: (i, j)) for _ in tiles]
    in_specs += [pl.BlockSpec((1, tn), lambda i, j: (0, j)) for _ in rows]
    in_specs += [pl.BlockSpec((tm, 1), lambda i, j: (i, 0)) for _ in cols]
    order = [] if after is None else [after]
    in_specs += [pl.BlockSpec((8, LANES), lambda i, j: (0, 0)) for _ in order]
    outs = pl.pallas_call(
        body,
        name=name,
        grid=(m // tm, n // tn),
        in_specs=in_specs,
        out_specs=[pl.BlockSpec((None, tm, tn), lambda i, j: (j, i, 0)) if col_blocks else
                   pl.BlockSpec((tm, tn), lambda i, j: (i, j)) for _ in out_dtypes],
        out_shape=[jax.ShapeDtypeStruct((n // tn, m, tn) if col_blocks else (m, n), dt) for dt in out_dtypes],
        compiler_params=_params("parallel", "parallel"),
    )(*a_list, *b_list, *tiles, *rows, *cols, *order)
    return outs[0] if len(out_dtypes) == 1 else outs


def _rms(x, w):
    r = lax.rsqrt(jnp.mean(x * x, axis=-1, keepdims=True) + NORM_EPS)
    return x * r * w


def _rms_bwd(x, w, dy):
    r = lax.rsqrt(jnp.mean(x * x, axis=-1, keepdims=True) + NORM_EPS)
    xh = x * r
    g = dy * w
    dx = r * (g - xh * jnp.mean(g * xh, axis=-1, keepdims=True))
    return dx, dy * xh


def _row_specs(tr, d):
    return pl.BlockSpec((tr, d), lambda i: (i, 0)), pl.BlockSpec((1, d), lambda i: (0, 0))


def _prenorm(name, h, w, after):
    t, d = h.shape
    tr = _tile(t, 512, 8)
    row, vec = _row_specs(tr, d)

    def body(h_ref, w_ref, after_ref, u_ref):
        u_ref[...] = _rms(h_ref[...], w_ref[...]).astype(BF16)

    return pl.pallas_call(body, name=name, grid=(t // tr,),
                          in_specs=[row, vec, pl.BlockSpec((8, LANES), lambda i: (0, 0))], out_specs=row,
                          out_shape=jax.ShapeDtypeStruct((t, d), BF16), compiler_params=_params("parallel"))(
                              h, w, after)


def _post_pre(name, h, m, w_post, w_pre):
    t, d = h.shape
    tr = _tile(t, 512, 8)
    row, vec = _row_specs(tr, d)

    def body(h_ref, m_ref, wq_ref, wp_ref, hn_ref, u_ref):
        hn = h_ref[...] + _rms(m_ref[...], wq_ref[...])
        hn_ref[...] = hn
        u_ref[...] = _rms(hn, wp_ref[...]).astype(BF16)

    return pl.pallas_call(body, name=name, grid=(t // tr,), in_specs=[row, row, vec, vec], out_specs=[row, row],
                          out_shape=[jax.ShapeDtypeStruct((t, d), F32), jax.ShapeDtypeStruct((t, d), BF16)],
                          compiler_params=_params("parallel"))(h, m, w_post, w_pre)


def _final_loss(name, h, m, w_post, target):
    t, d = h.shape
    tr = _tile(t, 512, 8)
    row, vec = _row_specs(tr, d)

    def body(h_ref, m_ref, wq_ref, t_ref, dh_ref, loss_ref):
        err = h_ref[...] + _rms(m_ref[...], wq_ref[...]) - t_ref[...]
        dh_ref[...] = err * (1.0 / d)
        part = 0.5 * jnp.sum(jnp.mean(err * err, axis=-1, keepdims=True), axis=0, keepdims=True)
        _acc(loss_ref, jnp.broadcast_to(part, (1, LANES)), pl.program_id(0) == 0)

    return pl.pallas_call(body, name=name, grid=(t // tr,), in_specs=[row, row, vec, row],
                          out_specs=[row, pl.BlockSpec((1, LANES), lambda i: (0, 0))],
                          out_shape=[jax.ShapeDtypeStruct((t, d), F32), jax.ShapeDtypeStruct((1, LANES), F32)],
                          compiler_params=_params("arbitrary"))(h, m, w_post, target)


def _norm_bwd(name, dh, pre=None, post=None, after=None):
    t, d = dh.shape
    tr = _tile(t, 512, 8)
    row, vec = _row_specs(tr, d)
    has_pre, has_post = pre is not None, post is not None

    def body(*refs):
        it = iter(refs)
        dh_ref = next(it)
        if has_pre:
            du_ref, x_ref, wp_ref = next(it), next(it), next(it)
        if has_post:
            m_ref, wq_ref = next(it), next(it)
        if after is not None:
            next(it)
        first = pl.program_id(0) == 0
        dh_v = dh_ref[...]
        if has_pre:
            dhn_ref, dwp_ref = next(it), next(it)
            dx, dwr = _rms_bwd(x_ref[...], wp_ref[...], du_ref[...])
            dh_v = dh_v + dx
            dhn_ref[...] = dh_v
            _acc(dwp_ref, jnp.sum(dwr, axis=0, keepdims=True), first)
        if has_post:
            dm_ref, dwq_ref, dms_ref = next(it), next(it), next(it)
            dm, dwr = _rms_bwd(m_ref[...], wq_ref[...], dh_v)
            dm_ref[...] = dm.astype(BF16)
            _acc(dwq_ref, jnp.sum(dwr, axis=0, keepdims=True), first)
            _acc(dms_ref, jnp.sum(dm, axis=0, keepdims=True), first)

    ins, in_specs, out_specs, out_shape = [dh], [row], [], []
    if has_pre:
        ins += list(pre)
        in_specs += [row, row, vec]
        out_specs += [row, vec]
        out_shape += [jax.ShapeDtypeStruct((t, d), F32), jax.ShapeDtypeStruct((1, d), F32)]
    if has_post:
        ins += list(post)
        in_specs += [row, vec]
        out_specs += [row, vec, vec]
        out_shape += [jax.ShapeDtypeStruct((t, d), BF16), jax.ShapeDtypeStruct((1, d), F32),
                      jax.ShapeDtypeStruct((1, d), F32)]
    if after is not None:
        ins.append(after)
        in_specs.append(pl.BlockSpec((8, LANES), lambda i: (0, 0)))
    return pl.pallas_call(body, name=name, grid=(t // tr,), in_specs=in_specs, out_specs=out_specs,
                          out_shape=out_shape, compiler_params=_params("arbitrary"))(*ins)


HALO = 8


def _shift_later(cur, prev, s):
    rolled = pltpu.roll(cur, s, 0)
    row = lax.broadcasted_iota(jnp.int32, prev.shape, 0)
    first = jnp.where(row < s, pltpu.roll(prev, s, 0), rolled[0:HALO])
    return jnp.concatenate([first, rolled[HALO:]], axis=0)


def _shift_earlier(cur, nxt, s):
    tt = cur.shape[0]
    rolled = pltpu.roll(cur, tt - s, 0)
    row = lax.broadcasted_iota(jnp.int32, nxt.shape, 0)
    last = jnp.where(row >= HALO - s, pltpu.roll(nxt, HALO - s, 0), rolled[tt - HALO:])
    return jnp.concatenate([rolled[:tt - HALO], last], axis=0)


def _conv_fwd(zx, col0, n_ch, conv_w, conv_b):
    t = zx.shape[0]
    tc = _tile(n_ch, 512)
    tt = _tile(t, 1024, 8)
    cb0 = col0 // tc
    assert col0 % tc == 0
    kw = SSD_CONV_WIDTH

    def body(x_ref, p_ref, w_ref, b_ref, o_ref):
        cur = x_ref[...]
        prev = jnp.where(pl.program_id(1) > 0, p_ref[...], 0.0)
        w = w_ref[...]
        acc = b_ref[...] + w[kw - 1:kw, :] * cur
        for k in range(kw - 1):
            acc = acc + w[k:k + 1, :] * _shift_later(cur, prev, kw - 1 - k)
        o_ref[...] = acc

    return pl.pallas_call(
        body, name="ssd_conv_fwd", grid=(n_ch // tc, t // tt),
        in_specs=[pl.BlockSpec((tt, tc), lambda j, i: (i, cb0 + j)),
                  pl.BlockSpec((HALO, tc), lambda j, i: (jnp.maximum(i * (tt // HALO) - 1, 0), cb0 + j)),
                  pl.BlockSpec((kw, tc), lambda j, i: (0, j)),
                  pl.BlockSpec((1, tc), lambda j, i: (0, j))],
        out_specs=pl.BlockSpec((tt, tc), lambda j, i: (i, j)),
        out_shape=jax.ShapeDtypeStruct((t, n_ch), F32),
        compiler_params=_params("parallel", "parallel"))(zx, zx, conv_w, conv_b)


def _conv_bwd(name, dpre, zx, col0, conv_w):
    t, n_ch = dpre.shape
    tc = _tile(n_ch, 512)
    tt = _tile(t, 1024, 8)
    cb0 = col0 // tc
    kw = SSD_CONV_WIDTH
    nt = t // tt

    def body(d_ref, dn_ref, x_ref, p_ref, w_ref, dx_ref, dw_ref, db_ref):
        i = pl.program_id(1)
        d = d_ref[...]
        d_next = jnp.where(i < nt - 1, dn_ref[...], 0.0)
        x = x_ref[...]
        x_prev = jnp.where(i > 0, p_ref[...], 0.0)
        w = w_ref[...]
        dx = w[kw - 1:kw, :] * d
        for k in range(kw - 1):
            dx = dx + w[k:k + 1, :] * _shift_earlier(d, d_next, kw - 1 - k)
        dx_ref[...] = dx.astype(BF16)
        first = i == 0
        for k in range(kw):
            xs = x if k == kw - 1 else _shift_later(x, x_prev, kw - 1 - k)
            val = jnp.sum(d * xs, axis=0, keepdims=True)

            @pl.when(first)
            def _():
                dw_ref[k:k + 1, :] = val

            @pl.when(jnp.logical_not(first))
            def _():
                dw_ref[k:k + 1, :] += val
        _acc(db_ref, jnp.sum(d, axis=0, keepdims=True), first)

    return pl.pallas_call(
        body, name=name, grid=(n_ch // tc, nt),
        in_specs=[pl.BlockSpec((tt, tc), lambda j, i: (i, j)),
                  pl.BlockSpec((HALO, tc), lambda j, i: (jnp.minimum((i + 1) * (tt // HALO), t // HALO - 1), j)),
                  pl.BlockSpec((tt, tc), lambda j, i: (i, cb0 + j)),
                  pl.BlockSpec((HALO, tc), lambda j, i: (jnp.maximum(i * (tt // HALO) - 1, 0), cb0 + j)),
                  pl.BlockSpec((kw, tc), lambda j, i: (0, j))],
        out_specs=[pl.BlockSpec((tt, tc), lambda j, i: (i, j)),
                   pl.BlockSpec((kw, tc), lambda j, i: (0, j)),
                   pl.BlockSpec((1, tc), lambda j, i: (0, j))],
        out_shape=[jax.ShapeDtypeStruct((t, n_ch), BF16), jax.ShapeDtypeStruct((kw, n_ch), F32),
                   jax.ShapeDtypeStruct((1, n_ch), F32)],
        compiler_params=_params("parallel", "arbitrary"))(dpre, dpre, zx, zx, conv_w)


def _head_of_lane(shape, width):
    return lax.broadcasted_iota(jnp.int32, shape, len(shape) - 1) // width


def _expand(v, n_rows):
    head = _head_of_lane((n_rows, GW), SSD_HEAD_DIM)
    out = jnp.zeros((n_rows, GW), F32)
    for j in range(SSD_HPG):
        out = jnp.where(head == j, v[:, j:j + 1], out)
    return out


def _contract(v, n_rows):
    head = _head_of_lane((n_rows, GW), SSD_HEAD_DIM)
    lane = lax.broadcasted_iota(jnp.int32, (n_rows, LANES), 1)
    out = jnp.zeros((n_rows, LANES), F32)
    for j in range(SSD_HPG):
        s = jnp.sum(jnp.where(head == j, v, 0.0), axis=1, keepdims=True)
        out = jnp.where(lane == j, s, out)
    return out


def _ssd_dt_prep(zdt, bias, alog, ng):
    t = zdt.shape[0]
    q = SSD_CHUNK

    def body(z_ref, b_ref, a_ref, dt_ref, cum_ref, cumr_ref, sg_ref):
        raw = z_ref[...] + b_ref[...]
        dt = _softplus(raw)
        sgd = _sigmoid(raw)
        row = lax.broadcasted_iota(jnp.int32, (q, q), 0)
        col = lax.broadcasted_iota(jnp.int32, (q, q), 1)
        cum = _dot_f32((col <= row).astype(F32), dt * (-jnp.exp(a_ref[...])))
        cum_t = cum.T
        lane = lax.broadcasted_iota(jnp.int32, (q, LANES), 1)
        for g in range(ng):
            shift = (LANES - g * SSD_HPG) % LANES

            def group(v):
                return jnp.where(lane < SSD_HPG, pltpu.roll(v, shift, 1) if shift else v, 0.0)

            dt_ref[g] = group(dt)
            cum_ref[g] = group(cum)
            sg_ref[g] = group(sgd)
            cumr_ref[g] = (pltpu.roll(cum_t, shift, 0) if shift else cum_t)[0:8, :]

    cols = pl.BlockSpec((ng, q, LANES), lambda c: (0, c, 0))
    vec = pl.BlockSpec((1, LANES), lambda c: (0, 0))
    col_shape = jax.ShapeDtypeStruct((ng, t, LANES), F32)
    return pl.pallas_call(body, name="ssd_dt_prep", grid=(t // q,),
                          in_specs=[pl.BlockSpec((q, LANES), lambda c: (c, 0)), vec, vec],
                          out_specs=[cols, cols, pl.BlockSpec((ng, 8, q), lambda c: (0, 0, c)), cols],
                          out_shape=[col_shape, col_shape, jax.ShapeDtypeStruct((ng, 8, t), F32), col_shape],
                          compiler_params=_params("parallel"))(zdt, bias, alog)


def _ssd_common(pre, dt, cum, cum_r, alog_c):
    q = SSD_CHUNK
    sg = _sigmoid(pre)
    act = pre * sg
    xa = act[:, :GW]
    bm = act[:, GW:GW + SSD_D_STATE].astype(BF16)
    cm = act[:, GW + SSD_D_STATE:].astype(BF16)
    row = lax.broadcasted_iota(jnp.int32, (q, q), 0)
    col = lax.broadcasted_iota(jnp.int32, (q, q), 1)
    tril = col <= row
    a_c = -jnp.exp(alog_c)
    g = _dot_nt(cm, bm)
    dt_x = _expand(dt, q)
    xdt = xa * dt_x
    cl = cum[q - 1:q, :]
    e_c = jnp.exp(cl - cum)
    lam_c = jnp.exp(cum)
    return dict(sg=sg, xa=xa, bm=bm, cm=cm, tril=tril, row=row, col=col, dt=dt, a_c=a_c, cum=cum, cum_r=cum_r,
                g=g, dt_x=dt_x, xdt=xdt, cl=cl, e_c=e_c, lam_c=lam_c)


SSD_GPS_FWD = 4
SSD_GPS_BWD = 2


def _ssd_specs(nc, rev, ng, gps):
    q = SSD_CHUNK
    xw, nw = gps * GW, gps * SSD_D_STATE
    b_off = ng * GW // nw
    c_off = (ng * GW + ng * SSD_D_STATE) // nw
    assert ng % gps == 0 and (ng * GW) % nw == 0 and (ng * SSD_D_STATE) % nw == 0

    def ch(c):
        return nc - 1 - c if rev else c

    chunk_grp = [pl.BlockSpec((q, xw), lambda g, c: (ch(c), g)),
                 pl.BlockSpec((q, nw), lambda g, c: (ch(c), b_off + g)),
                 pl.BlockSpec((q, nw), lambda g, c: (ch(c), c_off + g))]
    col_form = pl.BlockSpec((gps, q, LANES), lambda g, c: (g, ch(c), 0))
    row_form = pl.BlockSpec((gps, 8, q), lambda g, c: (g, 0, ch(c)))
    col_par = pl.BlockSpec((gps, 1, LANES), lambda g, c: (g, 0, 0))
    y_spec = pl.BlockSpec((q, xw), lambda g, c: (ch(c), g))
    st_spec = pl.BlockSpec((gps, None, GW, SSD_D_STATE), lambda g, c: (g, ch(c), 0, 0))
    bc_spec = pl.BlockSpec((q, nw), lambda g, c: (ch(c), g))
    return chunk_grp, col_form, row_form, col_par, y_spec, st_spec, bc_spec


def _ssd_group_views(gi, wide, narrow, stacked):
    xs, ns = pl.ds(gi * GW, GW), pl.ds(gi * SSD_D_STATE, SSD_D_STATE)
    return [r.at[:, xs] for r in wide], [r.at[:, ns] for r in narrow], [r.at[gi] for r in stacked]


def _ssd_fwd(pre, dt_c, cum_c, cum_r, alog_c, dsk_c):
    t = pre.shape[0]
    ng = pre.shape[1] // GC
    q = SSD_CHUNK
    nc = t // q
    gps = SSD_GPS_FWD if ng % SSD_GPS_FWD == 0 else SSD_GPS_BWD
    chunk_grp, col_form, row_form, col_par, y_spec, st_spec, _ = _ssd_specs(nc, False, ng, gps)

    def body(px_ref, pb_ref, pc_ref, dt_ref, cum_ref, cumr_ref, ac_ref, dk_ref, y_ref, sp_ref, st_ref):
        @pl.when(pl.program_id(1) == 0)
        def _():
            st_ref[...] = jnp.zeros_like(st_ref)

        for gi in range(gps):
            (px, y), (pb, pc), rest = _ssd_group_views(
                gi, (px_ref, y_ref), (pb_ref, pc_ref), (dt_ref, cum_ref, cumr_ref, ac_ref, dk_ref, sp_ref, st_ref))
            one_group(px, pb, pc, *rest[:5], y, *rest[5:])

    def one_group(px_ref, pb_ref, pc_ref, dt_ref, cum_ref, cumr_ref, ac_ref, dk_ref, y_ref, sp_ref, st_ref):
        pre_v = jnp.concatenate([px_ref[...], pb_ref[...], pc_ref[...]], axis=1)
        v = _ssd_common(pre_v, dt_ref[...], cum_ref[...], cumr_ref[...], ac_ref[...])
        s0 = st_ref[...]
        sp_ref[...] = s0
        r = _dot_nt(v["cm"], s0.astype(BF16))
        y = _expand(v["lam_c"], q) * r + _expand(dk_ref[...], 1) * v["xa"]
        head = _head_of_lane((q, GW), SSD_HEAD_DIM)
        for j in range(SSD_HPG):
            diff = v["cum"][:, j:j + 1] - v["cum_r"][j:j + 1, :]
            w = (v["g"] * jnp.exp(jnp.where(v["tril"], diff, -jnp.inf))).astype(BF16)
            y = y + _dot(w, jnp.where(head == j, v["xdt"], 0.0).astype(BF16))
        y_ref[...] = y
        ds = _dot_tn((v["xdt"] * _expand(v["e_c"], q)).astype(BF16), v["bm"])
        for j in range(SSD_HPG):
            rows = slice(j * SSD_HEAD_DIM, (j + 1) * SSD_HEAD_DIM)
            st_ref[rows, :] = s0[rows, :] * jnp.exp(v["cum_r"][j:j + 1, q - 1:q]) + ds[rows, :]

    return pl.pallas_call(
        body, name="ssd_scan_fwd", grid=(ng // gps, nc),
        in_specs=chunk_grp + [col_form, col_form, row_form, col_par, col_par],
        out_specs=[y_spec, st_spec],
        out_shape=[jax.ShapeDtypeStruct((t, ng * GW), F32), jax.ShapeDtypeStruct((ng, nc, GW, SSD_D_STATE), F32)],
        scratch_shapes=[pltpu.VMEM((gps, GW, SSD_D_STATE), F32)],
        compiler_params=_params("parallel", "arbitrary"))(pre, pre, pre, dt_c, cum_c, cum_r, alog_c, dsk_c)


def _ssd_bwd(dy, pre, states, dt_c, cum_c, cum_r, sgd_c, alog_c, dsk_c):
    t = pre.shape[0]
    ng = pre.shape[1] // GC
    q = SSD_CHUNK
    nc = t // q
    gps = SSD_GPS_BWD
    chunk_grp, col_form, row_form, col_par, y_spec, st_spec, bc_spec = _ssd_specs(nc, True, ng, gps)

    def body(dy_ref, px_ref, pb_ref, pc_ref, sp_ref, dt_ref, cum_ref, cumr_ref, sgd_ref, ac_ref, dk_ref,
             dpx_ref, dpb_ref, dpc_ref, ddt_ref, dbias_ref, dalog_ref, dd_ref, ds_ref):
        @pl.when(pl.program_id(1) == 0)
        def _():
            ds_ref[...] = jnp.zeros_like(ds_ref)

        for gi in range(gps):
            (dy, px, dpx), (pb, pc, dpb, dpc), rest = _ssd_group_views(
                gi, (dy_ref, px_ref, dpx_ref), (pb_ref, pc_ref, dpb_ref, dpc_ref),
                (sp_ref, dt_ref, cum_ref, cumr_ref, sgd_ref, ac_ref, dk_ref, ddt_ref, dbias_ref, dalog_ref, dd_ref,
                 ds_ref))
            one_group(dy, px, pb, pc, *rest[:7], dpx, dpb, dpc, *rest[7:])

    def one_group(dy_ref, px_ref, pb_ref, pc_ref, sp_ref, dt_ref, cum_ref, cumr_ref, sgd_ref, ac_ref, dk_ref,
                  dpx_ref, dpb_ref, dpc_ref, ddt_ref, dbias_ref, dalog_ref, dd_ref, ds_ref):
        first = pl.program_id(1) == 0
        pre_v = jnp.concatenate([px_ref[...], pb_ref[...], pc_ref[...]], axis=1)
        v = _ssd_common(pre_v, dt_ref[...], cum_ref[...], cumr_ref[...], ac_ref[...])
        xa, bm, cm, xdt, cum, cum_r = v["xa"], v["bm"], v["cm"], v["xdt"], v["cum"], v["cum_r"]
        xdt_b = xdt.astype(BF16)
        dy_v = dy_ref[...]
        s0 = sp_ref[...]
        ds1 = ds_ref[...]
        s0b, ds1b = s0.astype(BF16), ds1.astype(BF16)
        head = _head_of_lane((q, GW), SSD_HEAD_DIM)
        lane = lax.broadcasted_iota(jnp.int32, (q, LANES), 1)
        lane1 = lax.broadcasted_iota(jnp.int32, (1, LANES), 1)
        lam_x = _expand(v["lam_c"], q)
        e_x = _expand(v["e_c"], q)

        dxa = _expand(dk_ref[...], 1) * dy_v
        dd = _contract(jnp.sum(dy_v * xa, axis=0, keepdims=True), 1)
        r = _dot_nt(cm, s0b)
        dcum = _contract(dy_v * r * lam_x, q)
        drb = (lam_x * dy_v).astype(BF16)
        dc = _dot(drb, s0b)
        ds0 = _dot_tn(drb, cm)
        extra = jnp.zeros((1, LANES), F32)
        for j in range(SSD_HPG):
            rows = slice(j * SSD_HEAD_DIM, (j + 1) * SSD_HEAD_DIM)
            lam_last = jnp.exp(cum_r[j:j + 1, q - 1:q])
            ds_ref[rows, :] = ds0[rows, :] + lam_last * ds1[rows, :]
            tot = jnp.sum(jnp.sum(ds1[rows, :] * s0[rows, :], axis=1, keepdims=True), axis=0, keepdims=True)
            extra = jnp.where(lane1 == j, lam_last * tot, extra)
        dv = _dot_nt(bm, ds1b)
        db = _dot((xdt * e_x).astype(BF16), ds1b)
        dxdt = e_x * dv
        dee = _contract(dv * xdt, q) * v["e_c"]
        dcum = dcum - dee
        extra = extra + jnp.sum(dee, axis=0, keepdims=True)
        dg = jnp.zeros((q, q), F32)
        for j in range(SSD_HPG):
            diff = cum[:, j:j + 1] - cum_r[j:j + 1, :]
            el = jnp.exp(jnp.where(v["tril"], diff, -jnp.inf))
            gl = v["g"] * el
            dym = jnp.where(head == j, dy_v, 0.0).astype(BF16)
            dwm = _dot_nt(dym, xdt_b)
            dxdt = dxdt + _dot_tn(gl.astype(BF16), dym)
            z = dwm * gl
            rk = jnp.sum(z, axis=1, keepdims=True) - jnp.sum(z.T, axis=1, keepdims=True)
            dcum = jnp.where(lane == j, dcum + rk, dcum)
            dg = dg + dwm * el
        dgb = dg.astype(BF16)
        dc = dc + _dot(dgb, bm)
        db = db + _dot_tn(dgb, cm)
        da = _dot_f32((v["row"] <= v["col"]).astype(F32), dcum) + extra
        ddt = _contract(dxdt * xa, q) + v["a_c"] * da
        dalog = jnp.sum(v["dt"] * da, axis=0, keepdims=True) * v["a_c"]
        dxa = dxa + v["dt_x"] * dxdt
        ddt_raw = jnp.where(lane < SSD_HPG, ddt * sgd_ref[...], 0.0)
        sgrad = _silu_grad(pre_v, v["sg"])
        dpx_ref[...] = dxa * sgrad[:, :GW]
        dpb_ref[...] = db * sgrad[:, GW:GW + SSD_D_STATE]
        dpc_ref[...] = dc * sgrad[:, GW + SSD_D_STATE:]
        ddt_ref[...] = ddt_raw
        _acc(dbias_ref, jnp.sum(ddt_raw, axis=0, keepdims=True), first)
        _acc(dalog_ref, jnp.where(lane1 < SSD_HPG, dalog, 0.0), first)
        _acc(dd_ref, dd, first)

    return pl.pallas_call(
        body, name="ssd_scan_bwd", grid=(ng // gps, nc),
        in_specs=[y_spec] + chunk_grp + [st_spec, col_form, col_form, row_form, col_form, col_par, col_par],
        out_specs=[y_spec, bc_spec, bc_spec, col_form, col_par, col_par, col_par],
        out_shape=[jax.ShapeDtypeStruct((t, ng * GW), F32), jax.ShapeDtypeStruct((t, ng * SSD_D_STATE), F32),
                   jax.ShapeDtypeStruct((t, ng * SSD_D_STATE), F32), jax.ShapeDtypeStruct((ng, t, LANES), F32),
                   jax.ShapeDtypeStruct((ng, 1, LANES), F32), jax.ShapeDtypeStruct((ng, 1, LANES), F32),
                   jax.ShapeDtypeStruct((ng, 1, LANES), F32)],
        scratch_shapes=[pltpu.VMEM((gps, GW, SSD_D_STATE), F32)],
        compiler_params=_params("parallel", "arbitrary"))(dy, pre, pre, pre, states, dt_c, cum_c, cum_r, sgd_c, alog_c,
                                                           dsk_c)


def _gate_norm_fwd(y, zx, norm_w):
    t, di = y.shape
    tr = _tile(t, 512, 8)
    ng = di // GW

    def body(y_ref, z_ref, w_ref, o_ref):
        z = z_ref[...]
        gate = y_ref[...] * (z * _sigmoid(z))
        w = w_ref[...]
        for g in range(ng):
            cols = slice(g * GW, (g + 1) * GW)
            gs = gate[:, cols]
            r = lax.rsqrt(jnp.mean(gs * gs, axis=-1, keepdims=True) + NORM_EPS)
            o_ref[:, cols] = (gs * r * w[:, cols]).astype(BF16)

    row = pl.BlockSpec((tr, di), lambda i: (i, 0))
    return pl.pallas_call(body, name="ssd_gate_norm_fwd", grid=(t // tr,),
                          in_specs=[row, row, pl.BlockSpec((1, di), lambda i: (0, 0))], out_specs=row,
                          out_shape=jax.ShapeDtypeStruct((t, di), BF16), compiler_params=_params("parallel"))(
                              y, zx, norm_w)


def _gate_norm_bwd(dyn, y, zx, norm_w, after):
    t, di = y.shape
    tr = _tile(t, 256, 8)
    ng = di // GW

    def body(d_ref, y_ref, z_ref, w_ref, after_ref, dy_ref, dz_ref, dw_ref):
        z = z_ref[...]
        yv = y_ref[...]
        sg = _sigmoid(z)
        sz = z * sg
        gate = yv * sz
        w = w_ref[...]
        d = d_ref[...]
        dsz = _silu_grad(z, sg)
        dws = []
        for g in range(ng):
            cols = slice(g * GW, (g + 1) * GW)
            dg, dwr = _rms_bwd(gate[:, cols], w[:, cols], d[:, cols])
            dy_ref[:, cols] = dg * sz[:, cols]
            dz_ref[:, cols] = (dg * yv[:, cols] * dsz[:, cols]).astype(BF16)
            dws.append(jnp.sum(dwr, axis=0, keepdims=True))
        first = pl.program_id(0) == 0
        for g in range(ng):
            cols = slice(g * GW, (g + 1) * GW)

            @pl.when(first)
            def _():
                dw_ref[:, cols] = dws[g]

            @pl.when(jnp.logical_not(first))
            def _():
                dw_ref[:, cols] += dws[g]

    row = pl.BlockSpec((tr, di), lambda i: (i, 0))
    vec = pl.BlockSpec((1, di), lambda i: (0, 0))
    return pl.pallas_call(body, name="ssd_gate_norm_bwd", grid=(t // tr,),
                          in_specs=[row, row, row, vec, pl.BlockSpec((8, LANES), lambda i: (0, 0))],
                          out_specs=[row, row, vec],
                          out_shape=[jax.ShapeDtypeStruct((t, di), F32), jax.ShapeDtypeStruct((t, di), BF16),
                                     jax.ShapeDtypeStruct((1, di), F32)],
                          compiler_params=_params("arbitrary"))(dyn, y, zx, norm_w, after)


def _attn_mask_t(n):
    w = ATTN_WINDOW
    kpos = lax.broadcasted_iota(jnp.int32, (2 * w, ATTN_REP * w), 0)
    qpos = lax.broadcasted_iota(jnp.int32, (2 * w, ATTN_REP * w), 1) % w + w
    rel = qpos - kpos
    return (rel >= 0) & (rel < w) & jnp.logical_not((n == 0) & (kpos < w))


def _attn_probs_t(qts, ktb, mask, sink):
    s = _dot_tn(ktb, qts) * (ATTN_HEAD_DIM ** -0.5)
    s = jnp.where(mask, s, -jnp.inf)
    m = jnp.maximum(jnp.max(s, axis=0, keepdims=True), sink)
    e = jnp.exp(s - m)
    es = jnp.exp(sink - m)
    inv = 1.0 / (jnp.sum(e, axis=0, keepdims=True) + es)
    return e * inv, es * inv


def _attn_blocks_t(kv, q_ref, kc_ref, vc_ref, kp_ref, vp_ref):
    hd = ATTN_HEAD_DIM
    rows = slice(kv * hd, (kv + 1) * hd)
    ktb = jnp.concatenate([kp_ref[rows, :], kc_ref[rows, :]], axis=1)
    vtb = jnp.concatenate([vp_ref[rows, :], vc_ref[rows, :]], axis=1)
    qts = jnp.concatenate([q_ref[(kv * ATTN_REP + r) * hd:(kv * ATTN_REP + r + 1) * hd, :]
                           for r in range(ATTN_REP)], axis=1)
    return qts, ktb, vtb


def _attn_specs_t(nb, cur, prev):
    w, hd = ATTN_WINDOW, ATTN_HEAD_DIM
    kd = ATTN_N_KV * hd
    qd = ATTN_REP * kd
    return [pl.BlockSpec((qd, w), lambda n: (0, cur(n))),
            pl.BlockSpec((kd, w), lambda n: (ATTN_REP, cur(n))),
            pl.BlockSpec((kd, w), lambda n: (ATTN_REP + 1, cur(n))),
            pl.BlockSpec((kd, w), lambda n: (ATTN_REP, prev(n))),
            pl.BlockSpec((kd, w), lambda n: (ATTN_REP + 1, prev(n)))]


def _attn_fwd_t(qkv_t, sinks_rep):
    t = qkv_t.shape[1]
    w, hd = ATTN_WINDOW, ATTN_HEAD_DIM
    qd = ATTN_N_KV * ATTN_REP * hd
    nb = t // w

    def body(q_ref, kc_ref, vc_ref, kp_ref, vp_ref, s_ref, o_ref):
        mask = _attn_mask_t(pl.program_id(0))
        for kv in range(ATTN_N_KV):
            qts, ktb, vtb = _attn_blocks_t(kv, q_ref, kc_ref, vc_ref, kp_ref, vp_ref)
            p, _ = _attn_probs_t(qts, ktb, mask, s_ref[kv])
            ots = _dot(vtb, p.astype(BF16))
            for r in range(ATTN_REP):
                h = kv * ATTN_REP + r
                o_ref[h * hd:(h + 1) * hd, :] = ots[:, r * w:(r + 1) * w].astype(BF16)

    return pl.pallas_call(
        body, name="attn_fwd", grid=(nb,),
        in_specs=_attn_specs_t(nb, lambda n: n, lambda n: jnp.maximum(n - 1, 0)) + [
            pl.BlockSpec(sinks_rep.shape, lambda n: (0, 0, 0))],
        out_specs=pl.BlockSpec((qd, w), lambda n: (0, n)),
        out_shape=jax.ShapeDtypeStruct((qd, t), BF16),
        compiler_params=_params("parallel"))(qkv_t, qkv_t, qkv_t, qkv_t, qkv_t, sinks_rep)


def _attn_bwd_t(qkv_t, do_t, sinks_rep):
    t = qkv_t.shape[1]
    w, hd = ATTN_WINDOW, ATTN_HEAD_DIM
    kd = ATTN_N_KV * hd
    qd = ATTN_REP * kd
    nq = ATTN_N_KV * ATTN_REP
    nb = t // w
    rows_all = qd + 2 * kd

    def body(q_ref, kc_ref, vc_ref, kp_ref, vp_ref, do_ref, s_ref, dqkv_ref, bsum_ref, dsk_ref,
             carry_ref, new_ref, bacc_ref, sacc_ref):
        n = pl.program_id(0)

        @pl.when(n == 0)
        def _():
            carry_ref[...] = jnp.zeros_like(carry_ref)
            bacc_ref[...] = jnp.zeros_like(bacc_ref)
            sacc_ref[...] = jnp.zeros_like(sacc_ref)

        @pl.when(n < nb)
        def _():
            mask = _attn_mask_t(n)
            for kv in range(ATTN_N_KV):
                qts, ktb, vtb = _attn_blocks_t(kv, q_ref, kc_ref, vc_ref, kp_ref, vp_ref)
                dots = jnp.concatenate([do_ref[(kv * ATTN_REP + r) * hd:(kv * ATTN_REP + r + 1) * hd, :]
                                        for r in range(ATTN_REP)], axis=1)
                p, ps = _attn_probs_t(qts, ktb, mask, s_ref[kv])
                dpt = _dot_tn(vtb, dots)
                delta = jnp.sum(p * dpt, axis=0, keepdims=True)
                dst = (p * (dpt - delta) * (hd ** -0.5)).astype(BF16)
                dqts = _dot(ktb, dst)
                for r in range(ATTN_REP):
                    h = kv * ATTN_REP + r
                    new_ref[h * hd:(h + 1) * hd, :] = dqts[:, r * w:(r + 1) * w]
                dktb = _dot_nt(qts, dst)
                dvtb = _dot_nt(dots, p.astype(BF16))
                krows = slice(qd + kv * hd, qd + (kv + 1) * hd)
                vrows = slice(qd + kd + kv * hd, qd + kd + (kv + 1) * hd)
                carry_ref[krows, :] += dktb[:, :w]
                carry_ref[vrows, :] += dvtb[:, :w]
                new_ref[krows, :] = dktb[:, w:]
                new_ref[vrows, :] = dvtb[:, w:]
                sacc_ref[kv] += -(ps * delta)

        @pl.when(n >= 1)
        def _():
            done = carry_ref[...]
            dqkv_ref[...] = done.astype(BF16)
            bacc_ref[...] += done

        @pl.when(n < nb)
        def _():
            carry_ref[...] = new_ref[...]

        @pl.when(n == nb)
        def _():
            bsum_ref[...] = jnp.sum(bacc_ref[...], axis=1, keepdims=True)
            lane = lax.broadcasted_iota(jnp.int32, (1, nq), 1)
            dsk = jnp.zeros((1, nq), F32)
            for kv in range(ATTN_N_KV):
                acc = sacc_ref[kv]
                for r in range(ATTN_REP):
                    tot = jnp.sum(acc[:, r * w:(r + 1) * w], axis=1, keepdims=True)
                    dsk = jnp.where(lane == kv * ATTN_REP + r, tot, dsk)
            dsk_ref[...] = dsk

    cur = lambda n: jnp.minimum(n, nb - 1)
    prev = lambda n: jnp.maximum(jnp.minimum(n, nb - 1) - 1, 0)
    return pl.pallas_call(
        body, name="attn_bwd", grid=(nb + 1,),
        in_specs=_attn_specs_t(nb, cur, prev) + [pl.BlockSpec((qd, w), lambda n: (0, cur(n))),
                                                 pl.BlockSpec(sinks_rep.shape, lambda n: (0, 0, 0))],
        out_specs=[pl.BlockSpec((rows_all, w), lambda n: (0, jnp.maximum(n - 1, 0))),
                   pl.BlockSpec((rows_all, 1), lambda n: (0, 0)),
                   pl.BlockSpec((1, nq), lambda n: (0, 0))],
        out_shape=[jax.ShapeDtypeStruct((rows_all, t), BF16), jax.ShapeDtypeStruct((rows_all, 1), F32),
                   jax.ShapeDtypeStruct((1, nq), F32)],
        scratch_shapes=[pltpu.VMEM((rows_all, w), F32), pltpu.VMEM((rows_all, w), F32),
                        pltpu.VMEM((rows_all, w), F32), pltpu.VMEM(sinks_rep.shape, F32)],
        compiler_params=_params("arbitrary"))(qkv_t, qkv_t, qkv_t, qkv_t, qkv_t, do_t, sinks_rep)


HBM_SPEC = pl.BlockSpec(memory_space=pl.ANY)
HBM_ONLY = pl.BlockSpec(memory_space=pltpu.HBM)


def _comm_call(name, body, ins, out_shapes, n_sems):
    return pl.pallas_call(
        body, name=name, in_specs=[HBM_SPEC] * len(ins), out_specs=[HBM_SPEC] * len(out_shapes),
        out_shape=out_shapes,
        scratch_shapes=[pltpu.SemaphoreType.DMA((s,)) for s in n_sems])(*ins)


def _all_gather(name, shards, after):
    n = len(shards)
    na = len(after)

    def body(*refs):
        x_refs, out_refs = refs[:n], refs[n + na:2 * n + na]
        send_sems, recv_sems, local_sems = refs[2 * n + na:]
        x, y, c = lax.axis_index("x"), lax.axis_index("y"), lax.axis_index("c")
        me, sibling = (x, y, c), (x, y, 1 - c)
        chips = [(1 - x, y), (x, 1 - y), (1 - x, 1 - y)]

        def slot(i, px, py, pc):
            return out_refs[i].at[4 * px + 2 * py + pc]

        def copy(k, i, block, to, src=None):
            return pltpu.make_async_remote_copy(
                src_ref=slot(i, *block) if src is None else src, dst_ref=slot(i, *block),
                send_sem=send_sems.at[k * n + i], recv_sem=recv_sems.at[k * n + i], device_id=to,
                device_id_type=MESH)

        mine = [pltpu.make_async_copy(x_refs[i], slot(i, *me), local_sems.at[i]) for i in range(n)]
        first = []
        for i in range(n):
            mine[i].start()
            first.append(copy(0, i, me, sibling, src=x_refs[i]))
            first += [copy(1 + j, i, me, (*chip, c), src=x_refs[i]) for j, chip in enumerate(chips)]
        for cp in first:
            cp.start()
        passed = []
        for i in range(n):
            for j, chip in enumerate(chips):
                copy(1 + j, i, (*chip, c), me).wait_recv()
                passed.append(copy(4 + j, i, (*chip, c), sibling))
                passed[-1].start()
        for i in range(n):
            copy(0, i, sibling, me).wait_recv()
            for j, chip in enumerate(chips):
                copy(4 + j, i, (*chip, 1 - c), me).wait_recv()
        for cp in first + passed:
            cp.wait_send()
        for cp in mine:
            cp.wait()

    outs = [jax.ShapeDtypeStruct((N_DEV,) + s.shape, s.dtype) for s in shards]
    return _comm_call(name, body, list(shards) + list(after), outs, (7 * n, 7 * n, n))


SEM_SPEC = pl.BlockSpec(memory_space=pltpu.SEMAPHORE)
SPLIT_COPY_EFFECT = pltpu.SideEffectType.DATAFLOW_SIDE_EFFECTING


def _in_hbm(a):
    return pltpu.with_memory_space_constraint(a, pltpu.HBM)


def _split_start(name, body, srcs, lands, n_sems):
    n = len(srcs)
    bufs = [_in_hbm(a) for a in list(srcs) + list(lands)]
    outs = pl.pallas_call(
        body, name=name,
        out_shape=(pltpu.SemaphoreType.DMA((n_sems,)), pltpu.SemaphoreType.DMA((n_sems,)),
                   *[pltpu.HBM(a.shape, a.dtype) for a in bufs], jax.ShapeDtypeStruct((8, LANES), F32)),
        in_specs=[HBM_ONLY] * (2 * n),
        out_specs=(SEM_SPEC, SEM_SPEC, *[HBM_ONLY] * (2 * n), pl.BlockSpec(memory_space=pltpu.VMEM)),
        input_output_aliases={i: 2 + i for i in range(2 * n)},
        compiler_params=pltpu.CompilerParams(has_side_effects=SPLIT_COPY_EFFECT))(*bufs)
    return outs[0], outs[1], list(outs[2:2 + n]), list(outs[2 + n:2 + 2 * n]), outs[-1]


def _split_wait(name, body, send_sems, recv_sems, srcs, lands, after):
    n = len(srcs)
    outs = pl.pallas_call(
        body, name=name,
        out_shape=[pltpu.HBM(a.shape, a.dtype) for a in list(srcs) + list(lands)],
        in_specs=[HBM_ONLY] * (2 * n) + [SEM_SPEC, SEM_SPEC, HBM_SPEC],
        out_specs=[HBM_ONLY] * (2 * n),
        input_output_aliases={i: i for i in range(2 * n)},
        compiler_params=pltpu.CompilerParams(has_side_effects=SPLIT_COPY_EFFECT))(
            *srcs, *lands, send_sems, recv_sems, after)
    return list(outs[:n]), list(outs[n:])


N_PEERS = N_DEV - 1


def _gather_peers():
    x, y, c = lax.axis_index("x"), lax.axis_index("y"), lax.axis_index("c")
    flips = [(fx, fy, fc) for fx in (0, 1) for fy in (0, 1) for fc in (0, 1) if fx or fy or fc]
    return [(1 - x if fx else x, 1 - y if fy else y, 1 - c if fc else c) for fx, fy, fc in flips]


def _block_id(dev):
    return 4 * dev[0] + 2 * dev[1] + dev[2]


def _landing_block(land_ref, shard_shape, side_by_side, dev):
    if not side_by_side:
        return land_ref.at[_block_id(dev)]
    cols = shard_shape[1]
    return land_ref.at[:, pl.ds(pl.multiple_of(_block_id(dev) * cols, LANES), cols)]


def _gather_start(name, shards, side_by_side):
    n = len(shards)

    def body(*refs):
        x_refs, land_refs = refs[:n], refs[n:2 * n]
        send_sems, recv_sems, token = refs[2 * n], refs[2 * n + 1], refs[-1]
        me = (lax.axis_index("x"), lax.axis_index("y"), lax.axis_index("c"))
        for i in range(n):
            for k, peer in enumerate(_gather_peers()):
                pltpu.make_async_remote_copy(
                    src_ref=x_refs[i], dst_ref=_landing_block(land_refs[i], shards[i].shape, side_by_side[i], me),
                    send_sem=send_sems.at[N_PEERS * i + k], recv_sem=recv_sems.at[N_PEERS * i + k],
                    device_id=peer, device_id_type=MESH).start()
            pltpu.make_async_copy(x_refs[i], _landing_block(land_refs[i], shards[i].shape, side_by_side[i], me),
                                  send_sems.at[N_PEERS * n + i]).start()
        token[...] = jnp.zeros_like(token)

    lands = [lax.empty((s.shape[0], N_DEV * s.shape[1]) if wide else (N_DEV,) + s.shape, s.dtype)
             for s, wide in zip(shards, side_by_side)]
    return _split_start(name, body, shards, lands, (N_PEERS + 1) * n)


def _gather_wait(name, send_sems, recv_sems, first, n_all, shards, lands, side_by_side, after):
    n = len(shards)

    def body(*refs):
        x_refs, land_refs = refs[:n], refs[n:2 * n]
        send_sems, recv_sems = refs[2 * n], refs[2 * n + 1]
        me = (lax.axis_index("x"), lax.axis_index("y"), lax.axis_index("c"))
        for i in range(n):
            pltpu.make_async_copy(x_refs[i], _landing_block(land_refs[i], shards[i].shape, side_by_side[i], me),
                                  send_sems.at[N_PEERS * n_all + first + i]).wait()
            for k, peer in enumerate(_gather_peers()):
                cp = pltpu.make_async_remote_copy(
                    src_ref=x_refs[i], dst_ref=_landing_block(land_refs[i], shards[i].shape, side_by_side[i], peer),
                    send_sem=send_sems.at[N_PEERS * (first + i) + k],
                    recv_sem=recv_sems.at[N_PEERS * (first + i) + k],
                    device_id=peer, device_id_type=MESH)
                cp.wait_send()
                cp.wait_recv()

    return _split_wait(name, body, send_sems, recv_sems, shards, lands, after)


def _scatter_start(name, blocks):
    n = len(blocks)

    def body(*refs):
        b_refs, land_refs = refs[:n], refs[n:2 * n]
        send_sems, recv_sems, token = refs[2 * n], refs[2 * n + 1], refs[-1]
        me = (lax.axis_index("x"), lax.axis_index("y"), lax.axis_index("c"))
        for i in range(n):
            for k, peer in enumerate(_gather_peers()):
                pltpu.make_async_remote_copy(
                    src_ref=b_refs[i].at[_block_id(peer)], dst_ref=land_refs[i].at[_block_id(me)],
                    send_sem=send_sems.at[N_PEERS * i + k], recv_sem=recv_sems.at[N_PEERS * i + k],
                    device_id=peer, device_id_type=MESH).start()
            pltpu.make_async_copy(b_refs[i].at[_block_id(me)], land_refs[i].at[_block_id(me)],
                                  send_sems.at[N_PEERS * n + i]).start()
        token[...] = jnp.zeros_like(token)

    lands = [lax.empty(b.shape, b.dtype) for b in blocks]
    return _split_start(name, body, blocks, lands, (N_PEERS + 1) * n)


def _scatter_wait(name, send_sems, recv_sems, blocks, lands, after):
    n = len(blocks)

    def body(*refs):
        b_refs, land_refs = refs[:n], refs[n:2 * n]
        send_sems, recv_sems = refs[2 * n], refs[2 * n + 1]
        me = (lax.axis_index("x"), lax.axis_index("y"), lax.axis_index("c"))
        for i in range(n):
            pltpu.make_async_copy(b_refs[i].at[_block_id(me)], land_refs[i].at[_block_id(me)],
                                  send_sems.at[N_PEERS * n + i]).wait()
            for k, peer in enumerate(_gather_peers()):
                cp = pltpu.make_async_remote_copy(
                    src_ref=b_refs[i].at[_block_id(peer)], dst_ref=land_refs[i].at[_block_id(peer)],
                    send_sem=send_sems.at[N_PEERS * i + k], recv_sem=recv_sems.at[N_PEERS * i + k],
                    device_id=peer, device_id_type=MESH)
                cp.wait_send()
                cp.wait_recv()

    return _split_wait(name, body, send_sems, recv_sems, blocks, lands, after)


def _adamw(w, g, m, v):
    m = ADAM_B1 * m + (1.0 - ADAM_B1) * g
    v = ADAM_B2 * v + (1.0 - ADAM_B2) * (g * g)
    m_hat = m / (1.0 - ADAM_B1 ** ADAM_STEP)
    v_hat = v / (1.0 - ADAM_B2 ** ADAM_STEP)
    delta = -ADAM_LR * (m_hat / (jnp.sqrt(v_hat) + ADAM_EPS) + ADAM_WD * w)
    return delta, m, v


def _adamw_tiles(r, c_):
    tr = _tile(r, 256, 16)
    return (tr, c_) if tr < r or r <= 256 else (r, _tile(c_, 256))


def _sum_parts(part):
    g = part[0].astype(F32)
    for k in range(1, part.shape[0]):
        g = g + part[k].astype(F32)
    return g


def _sum_adamw(name, parts, w, m, v):
    r, c_ = w.shape
    tr, tc = _adamw_tiles(r, c_)

    def body(p_ref, w_ref, m_ref, v_ref, g_ref, d_ref, nm_ref, nv_ref):
        g = _sum_parts(p_ref)
        g_ref[...] = g
        d_ref[...], nm_ref[...], nv_ref[...] = _adamw(w_ref[...], g, m_ref[...], v_ref[...])

    tile = pl.BlockSpec((tr, tc), lambda i, j: (i, j))
    return pl.pallas_call(body, name=name, grid=(r // tr, c_ // tc),
                          in_specs=[pl.BlockSpec((parts.shape[0], tr, tc), lambda i, j: (0, i, j)), tile, tile, tile],
                          out_specs=[tile] * 4, out_shape=[jax.ShapeDtypeStruct((r, c_), F32)] * 4,
                          compiler_params=_params("parallel", "parallel"))(parts, w, m, v)


def _sum_adamw_layers(name, parts, w, m, v):
    n_layers, r, c_ = w.shape
    tr = _tile(r, 256, 16)

    def body(*refs):
        p_refs = refs[:n_layers]
        w_ref, m_ref, v_ref, g_ref, d_ref, nm_ref, nv_ref = refs[n_layers:]
        layer = pl.program_id(0)
        g = _sum_parts(p_refs[0])
        for li in range(1, n_layers):
            g = jnp.where(layer == li, _sum_parts(p_refs[li]), g)
        g_ref[...] = g
        d_ref[...], nm_ref[...], nv_ref[...] = _adamw(w_ref[...], g, m_ref[...], v_ref[...])

    row = pl.BlockSpec((None, tr, c_), lambda l, i: (l, i, 0))
    specs = [pl.BlockSpec((p.shape[0], tr, c_), lambda l, i: (0, i, 0)) for p in parts]
    return pl.pallas_call(body, name=name, grid=(n_layers, r // tr), in_specs=specs + [row, row, row],
                          out_specs=[row] * 4, out_shape=[jax.ShapeDtypeStruct(w.shape, F32)] * 4,
                          compiler_params=_params("parallel", "parallel"))(*parts, w, m, v)


def _pack_rows(flat, n_rows, cols):
    pad = n_rows * cols - flat.shape[-1]
    flat = jnp.pad(flat, [(0, 0)] * (flat.ndim - 1) + [(0, pad)])
    return flat.reshape(flat.shape[:-1] + (n_rows, cols))


def _cols_split(full):
    c = full.shape[1] // N_DEV
    return jnp.stack([full[:, d * c:(d + 1) * c] for d in range(N_DEV)])


def _rows_join(blocks):
    return blocks.reshape(N_DEV * blocks.shape[1], blocks.shape[2])


def _rows_split(full):
    return full.reshape(N_DEV, full.shape[0] // N_DEV, full.shape[1])


def _heads_col(v, ng):
    return jnp.pad(v.reshape(ng, 1, SSD_HPG), ((0, 0), (0, 0), (0, LANES - SSD_HPG)))


MATRIX_ITEMS = ("w_in", "w_out", "up0", "down0", "w_qkv", "w_o", "up1", "down1")
VECTOR_ITEMS = ("conv_w", "b_qkv", "b_o")
ITEMS = MATRIX_ITEMS + VECTOR_ITEMS
GATHER_STAGES = (("w_in", "conv_w"), ("w_out", "up0", "down0"), ("w_qkv", "b_qkv", "w_o", "b_o", "up1", "down1"))
SIDE_BY_SIDE = ("conv_w", "up0", "up1", "b_o")


def _items(tree, prefix=""):
    g = lambda k: tree[prefix + k]
    return {"w_in": g("ssd_w_in")[0].T, "w_out": g("ssd_w_out")[0], "w_qkv": g("attn_w_qkv")[0].T,
            "w_o": g("attn_w_o")[0], "up0": g("mlp_w_up")[0], "up1": g("mlp_w_up")[1],
            "down0": g("mlp_w_down")[0], "down1": g("mlp_w_down")[1], "conv_w": g("ssd_conv_w")[0],
            "b_qkv": g("attn_b_qkv"), "b_o": g("attn_b_o")}


REPLICATED = ("ssd_conv_b", "ssd_dt_bias", "ssd_a_log", "ssd_d", "ssd_norm_w", "attn_sinks", "mix_pre_norm",
              "mix_post_norm", "ffn_pre_norm", "ffn_post_norm")
WEIGHTS = ("ssd_w_in", "ssd_conv_w", "ssd_conv_b", "ssd_dt_bias", "ssd_a_log", "ssd_d", "ssd_norm_w", "ssd_w_out",
           "attn_w_qkv", "attn_b_qkv", "attn_sinks", "attn_w_o", "attn_b_o", "mlp_w_up", "mlp_w_down",
           "mix_pre_norm", "mix_post_norm", "ffn_pre_norm", "ffn_post_norm")


def _forward_backward(x, target, rep, token, weights_of_stage, reduce_grads):
    t, d = x.shape
    ng = rep["ssd_norm_w"].shape[1] // GW
    di = ng * GW
    n_xbc = ng * GC
    nh = ng * SSD_HPG
    grads, blocks = {}, {}
    w_up, w_down = [None, None], [None, None]
    sinks_rep = jnp.repeat(rep["attn_sinks"].reshape(ATTN_N_KV, ATTN_REP, 1), ATTN_WINDOW, axis=2).reshape(
        ATTN_N_KV, 1, ATTN_REP * ATTN_WINDOW)
    conv_b = rep["ssd_conv_b"]
    gn = ng * SSD_D_STATE
    parts = ((0, di), (di, di), (2 * di, gn), (2 * di + gn, gn), (di + n_xbc, nh))
    alog_c, dsk_c = (_heads_col(rep[k], ng) for k in ("ssd_a_log", "ssd_d"))
    bias_l, alog_l = (jnp.pad(rep[k], ((0, 0), (0, LANES - nh))) for k in ("ssd_dt_bias", "ssd_a_log"))
    norm = {k: rep[k] for k in ("mix_pre_norm", "mix_post_norm", "ffn_pre_norm", "ffn_post_norm")}

    def nrow(name, i):
        return norm[name][i:i + 1]

    def mlp_fwd(i, u2):
        p = _mm(f"mlp{i}_up", [u2], [w_up[i]], "nn", tm=1024, tn=1024, out_dtypes=(BF16,),
                epilogue=lambda acc: (jnp.square(jnp.maximum(acc, 0.0)),))
        f = _mm(f"mlp{i}_down", [p], [w_down[i]], "nn", tm=512, tn=1024)
        return p, f

    def mlp_bwd(i, df, u2, p):
        da = _mm(f"mlp{i}_dact", [df], [w_down[i]], "nt", tm=1024, tn=1024, out_dtypes=(BF16,),
                 tiles=(p,), epilogue=lambda acc, pv: (acc * (2.0 * jnp.sqrt(pv.astype(F32))),))
        blocks[f"down{i}"] = _rows_split(_mm(f"mlp{i}_dwdown", [p], [df], "tn", tm=512, tn=1024,
                                             out_dtypes=(PAYLOAD,)))
        blocks[f"up{i}"] = _mm(f"mlp{i}_dwup", [u2], [da], "tn", tm=1024, tn=da.shape[1] // N_DEV,
                               out_dtypes=(PAYLOAD,), col_blocks=True)
        return _mm(f"mlp{i}_dx", [da], [w_up[i]], "nt", tm=512, tn=1024)

    u0 = _prenorm("l0_prenorm", x, nrow("mix_pre_norm", 0), token)
    got = weights_of_stage(0, u0)
    w_in_t = _rows_join(got["w_in"])
    w_dt_t = jnp.pad(w_in_t[di + n_xbc:], ((0, LANES - nh), (0, 0)))
    conv_w = got["conv_w"]
    zx = _mm("ssd_in_proj", [u0], [w_in_t], "nt", tm=1024, tn=1024, n_use=di + n_xbc)
    zdt = _mm("ssd_dt_proj", [u0], [w_dt_t], "nt", tm=1024, tn=LANES)
    pre = _conv_fwd(zx, di, n_xbc, conv_w, conv_b)
    dt_c, cum_c, cum_r, sgd_c = _ssd_dt_prep(zdt, bias_l, alog_l, ng)
    y, states = _ssd_fwd(pre, dt_c, cum_c, cum_r, alog_c, dsk_c)
    yn = _gate_norm_fwd(y, zx, rep["ssd_norm_w"])
    got = weights_of_stage(1, yn)
    w_out = _rows_join(got["w_out"])
    w_up[0], w_down[0] = got["up0"], _rows_join(got["down0"])
    mix0 = _mm("ssd_out_proj", [yn], [w_out], "nn", tm=1024, tn=1024)
    h1, u0f = _post_pre("l0_mid", x, mix0, nrow("mix_post_norm", 0), nrow("ffn_pre_norm", 0))
    p0, f0 = mlp_fwd(0, u0f)
    h2, u1 = _post_pre("l1_in", h1, f0, nrow("ffn_post_norm", 0), nrow("mix_pre_norm", 1))
    got = weights_of_stage(2, u1)
    w_qkv_t = _rows_join(got["w_qkv"])
    w_o = _rows_join(got["w_o"])
    b_qkv_col = got["b_qkv"].reshape(-1, 1)
    b_o = got["b_o"]
    w_up[1], w_down[1] = got["up1"], _rows_join(got["down1"])
    qkv_t = _mm("attn_qkv_proj", [w_qkv_t], [u1], "nt", tm=768, tn=1024, out_dtypes=(BF16,), cols=(b_qkv_col,),
                epilogue=lambda acc, b: (acc + b,))
    ao_t = _attn_fwd_t(qkv_t, sinks_rep)
    mix1 = _mm("attn_out_proj", [ao_t], [w_o], "tn", tm=1024, tn=1024, rows=(b_o,),
               epilogue=lambda acc, b: (acc + b,))
    h3, u1f = _post_pre("l1_mid", h2, mix1, nrow("mix_post_norm", 1), nrow("ffn_pre_norm", 1))
    p1, f1 = mlp_fwd(1, u1f)
    dh, loss_row = _final_loss("loss", h3, f1, nrow("ffn_post_norm", 1), target)

    g_norm = {k: [None, None] for k in norm}
    df1, g_norm["ffn_post_norm"][1], _ = _norm_bwd("l1_ffn_post_bwd", dh, post=(f1, nrow("ffn_post_norm", 1)))
    du = mlp_bwd(1, df1, u1f, p1)
    sent = reduce_grads("mlp1", {k: blocks[k] for k in ("up1", "down1")})
    dh, g_norm["ffn_pre_norm"][1], dmix1, g_norm["mix_post_norm"][1], db_o = _norm_bwd(
        "l1_mid_bwd", dh, pre=(du, h3, nrow("ffn_pre_norm", 1)), post=(mix1, nrow("mix_post_norm", 1)), after=sent)
    blocks["b_o"] = _cols_split(db_o)
    blocks["w_o"] = _rows_split(_mm("attn_dwo", [ao_t], [dmix1], "nn", tm=512, tn=1024, out_dtypes=(PAYLOAD,)))
    dao_t = _mm("attn_dout", [w_o], [dmix1], "nt", tm=1024, tn=1024, out_dtypes=(BF16,))
    dqkv_t, db_qkv, grads["attn_sinks"] = _attn_bwd_t(qkv_t, dao_t, sinks_rep)
    blocks["b_qkv"] = db_qkv.reshape(N_DEV, 1, -1)
    blocks["w_qkv"] = _rows_split(_mm("attn_dwqkv", [dqkv_t], [u1], "nn", tm=512, tn=1024, out_dtypes=(PAYLOAD,)))
    du = _mm("attn_dx", [dqkv_t], [w_qkv_t], "tn", tm=1024, tn=1024)
    sent = reduce_grads("attn", {k: blocks[k] for k in ("w_o", "w_qkv", "b_o", "b_qkv")})
    dh, g_norm["mix_pre_norm"][1], df0, g_norm["ffn_post_norm"][0], _ = _norm_bwd(
        "l1_in_bwd", dh, pre=(du, h2, nrow("mix_pre_norm", 1)), post=(f0, nrow("ffn_post_norm", 0)), after=sent)
    du = mlp_bwd(0, df0, u0f, p0)
    sent = reduce_grads("mlp0", {k: blocks[k] for k in ("up0", "down0")})
    dh, g_norm["ffn_pre_norm"][0], dmix0, g_norm["mix_post_norm"][0], _ = _norm_bwd(
        "l0_mid_bwd", dh, pre=(du, h1, nrow("ffn_pre_norm", 0)), post=(mix0, nrow("mix_post_norm", 0)), after=sent)
    blocks["w_out"] = _rows_split(_mm("ssd_dwout", [yn], [dmix0], "tn", tm=512, tn=1024, out_dtypes=(PAYLOAD,)))
    dyn = _mm("ssd_dyn", [dmix0], [w_out], "nt", tm=1024, tn=1024)
    sent = reduce_grads("ssdout", {"w_out": blocks["w_out"]})
    dy, dz, grads["ssd_norm_w"] = _gate_norm_bwd(dyn, y, zx, rep["ssd_norm_w"], sent)
    dpx, dpb, dpc, ddt_g, dbias_g, dalog_g, dd_g = _ssd_bwd(dy, pre, states, dt_c, cum_c, cum_r, sgd_c, alog_c,
                                                             dsk_c)
    conv_out = [_conv_bwd(f"ssd_conv_bwd_{tag}", dp, zx, c0, conv_w[:, c0 - di:c0 - di + n])
                for tag, dp, (c0, n) in zip("xbc", (dpx, dpb, dpc), parts[1:4])]
    dconv_w = jnp.concatenate([o[1] for o in conv_out], axis=1)
    dconv_b = jnp.concatenate([o[2] for o in conv_out], axis=1)
    ddt = jnp.transpose(ddt_g[:, :, :SSD_HPG], (1, 0, 2)).reshape(t, nh)
    ddt = jnp.pad(ddt, ((0, 0), (0, LANES - nh))).astype(BF16)
    blocks["conv_w"] = _cols_split(dconv_w)
    grads["ssd_conv_b"] = dconv_b
    for name, val in (("ssd_dt_bias", dbias_g), ("ssd_a_log", dalog_g), ("ssd_d", dd_g)):
        grads[name] = val[:, 0, :SSD_HPG].reshape(1, nh)
    d_zx = [dz] + [o[0] for o in conv_out] + [ddt]
    dw_parts = [_mm(f"ssd_dw_{tag}", [d], [u0], "tn", tm=512, tn=1024, out_dtypes=(PAYLOAD,))
                for tag, d in zip("zxbct", d_zx)]
    dw_parts[-1] = dw_parts[-1][:nh]
    blocks["w_in"] = _rows_split(jnp.concatenate(dw_parts, axis=0))
    sent = reduce_grads("ssd", {k: blocks[k] for k in ("w_in", "conv_w")})
    w_parts = [w_in_t[r0:r0 + n] for r0, n in parts[:-1]] + [w_dt_t]
    du = _mm("ssd_dx", d_zx, w_parts, "nn", tm=256, tn=1024, after=sent)
    grad_x, g_norm["mix_pre_norm"][0] = _norm_bwd("l0_in_bwd", dh, pre=(du, x, nrow("mix_pre_norm", 0)), after=sent)
    for k in norm:
        grads[k] = jnp.concatenate(g_norm[k], axis=0)
    return loss_row, grad_x, grads


def kernel(x, ssd_w_in, ssd_conv_w, ssd_conv_b, ssd_dt_bias, ssd_a_log, ssd_d, ssd_norm_w, ssd_w_out, attn_w_qkv, attn_b_qkv, attn_sinks, attn_w_o, attn_b_o, mlp_w_up, mlp_w_down, mix_pre_norm, mix_post_norm, ffn_pre_norm, ffn_post_norm, loss_target, m_ssd_w_in, m_ssd_conv_w, m_ssd_conv_b, m_ssd_dt_bias, m_ssd_a_log, m_ssd_d, m_ssd_norm_w, m_ssd_w_out, m_attn_w_qkv, m_attn_b_qkv, m_attn_sinks, m_attn_w_o, m_attn_b_o, m_mlp_w_up, m_mlp_w_down, m_mix_pre_norm, m_mix_post_norm, m_ffn_pre_norm, m_ffn_post_norm, v_ssd_w_in, v_ssd_conv_w, v_ssd_conv_b, v_ssd_dt_bias, v_ssd_a_log, v_ssd_d, v_ssd_norm_w, v_ssd_w_out, v_attn_w_qkv, v_attn_b_qkv, v_attn_sinks, v_attn_w_o, v_attn_b_o, v_mlp_w_up, v_mlp_w_down, v_mix_pre_norm, v_mix_post_norm, v_ffn_pre_norm, v_ffn_post_norm):
    given = dict(locals())
    w = {k: given[k] for k in WEIGHTS}
    mom_m = {k: given["m_" + k] for k in WEIGHTS}
    mom_v = {k: given["v_" + k] for k in WEIGHTS}
    w_it, m_it, v_it = _items(given), _items(given, "m_"), _items(given, "v_")

    order = [k for stage in GATHER_STAGES for k in stage]
    shards = [w_it[k].astype(PAYLOAD) if k in MATRIX_ITEMS else w_it[k] for k in order]
    wide = [k in SIDE_BY_SIDE for k in order]
    g_send, g_recv, shards, lands, token = _gather_start("gather_start", shards, wide)

    def weights_of_stage(s, after):
        first = sum(len(stage) for stage in GATHER_STAGES[:s])
        sl = slice(first, first + len(GATHER_STAGES[s]))
        _, got = _gather_wait(f"gather_wait{s}", g_send, g_recv, first, len(order), shards[sl], lands[sl], wide[sl],
                              after)
        return dict(zip(GATHER_STAGES[s], got))

    in_flight = []

    def reduce_grads(tag, blocks):
        keys = list(blocks)
        started = _scatter_start(f"rs_start_{tag}", [blocks[k] for k in keys])
        in_flight.append((tag, keys, started))
        return started[-1]

    rep = {k: w[k] for k in REPLICATED}
    loss_row, grad_x, grads = _forward_backward(x[0], loss_target[0], rep, token, weights_of_stage, reduce_grads)

    def pack_rep(tree, last):
        flat = jnp.concatenate([tree[k].reshape(-1) for k in REPLICATED] + [last])
        return _pack_rows(flat, _round_up(-(-flat.shape[0] // LANES), 8), LANES)

    landed = {}

    def wait_group(group, after):
        tag, keys, (s_send, s_recv, srcs, s_lands, _) = group
        _, got = _scatter_wait(f"rs_wait_{tag}", s_send, s_recv, srcs, s_lands, after)
        landed.update(zip(keys, got))

    def adamw_item(k):
        return _sum_adamw(f"adamw_{k}", landed[k], w_it[k], m_it[k], v_it[k])

    def adamw_stack(name, keys):
        return _sum_adamw_layers(f"adamw_{name}", [landed[k] for k in keys], given[name], given["m_" + name],
                                 given["v_" + name])

    for group in in_flight[:-1]:
        wait_group(group, grad_x)
    done = {"mlp_w_up": adamw_stack("mlp_w_up", ("up0", "up1")),
            "mlp_w_down": adamw_stack("mlp_w_down", ("down0", "down1")),
            "attn_w_qkv": [o.T[None] for o in adamw_item("w_qkv")],
            "attn_w_o": [o[None] for o in adamw_item("w_o")],
            "attn_b_qkv": adamw_item("b_qkv"), "attn_b_o": adamw_item("b_o"),
            "ssd_w_out": [o[None] for o in adamw_item("w_out")]}
    partials, = _all_gather("gather_small_grads", [pack_rep(grads, loss_row[0, :1])],
                            [outs4[0] for outs4 in done.values()])
    wait_group(in_flight[-1], partials)
    done["ssd_w_in"] = [o.T[None] for o in adamw_item("w_in")]
    done["ssd_conv_w"] = [o[None] for o in adamw_item("conv_w")]
    zero = jnp.zeros((1,), F32)
    rep_out = _sum_adamw("adamw_replicated", partials, pack_rep(w, zero), pack_rep(mom_m, zero), pack_rep(mom_v, zero))

    kinds = []
    for kind, r_arr in enumerate(rep_out):
        tree = {name: outs4[kind] for name, outs4 in done.items()}
        flat, off = r_arr.reshape(-1), 0
        for k in REPLICATED:
            tree[k] = flat[off:off + w[k].size].reshape(w[k].shape)
            off += w[k].size
        kinds.append(tree)
    loss = rep_out[0].reshape(-1)[off]
    outs = [loss, grad_x[None]]
    for tree in kinds:
        outs += [tree[k] for k in WEIGHTS]
    return tuple(outs)
```

```python
import jax
import jax.numpy as jnp
from jax import lax
from jax.experimental import pallas as pl
from jax.experimental.pallas import tpu as pltpu

F32 = jnp.float32
BF16 = jnp.bfloat16
PAYLOAD = jnp.bfloat16
HIGHEST = lax.Precision.HIGHEST
MESH = pl.DeviceIdType.MESH

NORM_EPS = 1e-6
SSD_HEAD_DIM = 64
SSD_HPG = 4
SSD_D_STATE = 128
SSD_CONV_WIDTH = 4
SSD_CHUNK = 128
ATTN_HEAD_DIM = 64
ATTN_N_KV = 4
ATTN_REP = 4
ATTN_WINDOW = 128
ADAM_LR = 0.001
ADAM_B1 = 0.9
ADAM_B2 = 0.999
ADAM_EPS = 1e-08
ADAM_WD = 0.01
ADAM_STEP = 10

N_DEV = 8
LANES = 128
V7X_VMEM_LIMIT = 56 * 1024 * 1024

GW = SSD_HPG * SSD_HEAD_DIM
GC = GW + 2 * SSD_D_STATE
assert SSD_CHUNK == LANES


def _params(*sem):
    return pltpu.CompilerParams(dimension_semantics=sem, vmem_limit_bytes=V7X_VMEM_LIMIT)


def _tile(n, pref, mult=LANES):
    best = None
    t = mult
    while t <= min(n, pref):
        if n % t == 0:
            best = t
        t += mult
    return best if best is not None else n


def _round_up(n, m):
    return (n + m - 1) // m * m


def _acc(ref, val, first):
    @pl.when(first)
    def _():
        ref[...] = val

    @pl.when(jnp.logical_not(first))
    def _():
        ref[...] += val


def _dot(a, b):
    return lax.dot_general(a, b, (((1,), (0,)), ((), ())), preferred_element_type=F32)


def _dot_nt(a, b):
    return lax.dot_general(a, b, (((1,), (1,)), ((), ())), preferred_element_type=F32)


def _dot_tn(a, b):
    return lax.dot_general(a, b, (((0,), (0,)), ((), ())), preferred_element_type=F32)


def _dot_f32(a, b):
    return lax.dot_general(a, b, (((1,), (0,)), ((), ())), preferred_element_type=F32, precision=HIGHEST)


_DOTS = {"nn": _dot, "nt": _dot_nt, "tn": _dot_tn}


def _sigmoid(x):
    return 1.0 / (1.0 + jnp.exp(-x))


def _softplus(x):
    return jnp.maximum(x, 0.0) + jnp.log1p(jnp.exp(-jnp.abs(x)))


def _silu_grad(x, s):
    return s * (1.0 + x * (1.0 - s))


def _mm(name, a_list, b_list, mode, *, tm, tn, out_dtypes=(F32,), epilogue=None, tiles=(), rows=(), cols=(),
        col_blocks=False, n_use=None, after=None):
    npair = len(a_list)
    if mode == "tn":
        m = a_list[0].shape[1]
    else:
        m = a_list[0].shape[0]
    n = n_use if n_use is not None else (b_list[0].shape[0] if mode == "nt" else b_list[0].shape[1])
    tm = _tile(m, tm, LANES if mode == "tn" else 8)
    tn = _tile(n, tn)
    assert m % tm == 0 and n % tn == 0, (name, m, n, tm, tn)
    dot = _DOTS[mode]

    def body(*refs):
        a_refs = refs[:npair]
        b_refs = refs[npair:2 * npair]
        n_extra = len(tiles) + len(rows) + len(cols)
        e_refs = refs[2 * npair:2 * npair + n_extra]
        o_refs = refs[2 * npair + n_extra + len(order):]
        acc = None
        for ar, br in zip(a_refs, b_refs):
            d = dot(ar[...], br[...])
            acc = d if acc is None else acc + d
        outs = epilogue(acc, *[e[...] for e in e_refs]) if epilogue is not None else (acc,)
        for o, v in zip(o_refs, outs):
            o[...] = v.astype(o.dtype)

    in_specs = []
    for a in a_list:
        if mode == "tn":
            in_specs.append(pl.BlockSpec((a.shape[0], tm), lambda i, j: (0, i)))
        else:
            in_specs.append(pl.BlockSpec((tm, a.shape[1]), lambda i, j: (i, 0)))
    for b in b_list:
        if mode == "nt":
            in_specs.append(pl.BlockSpec((tn, b.shape[1]), lambda i, j: (j, 0)))
        else:
            in_specs.append(pl.BlockSpec((b.shape[0], tn), lambda i, j: (0, j)))
    in_specs += [pl.BlockSpec((tm, tn), lambda i, j: (i, j)) for _ in tiles]
    in_specs += [pl.BlockSpec((1, tn), lambda i, j: (0, j)) for _ in rows]
    in_specs += [pl.BlockSpec((tm, 1), lambda i, j: (i, 0)) for _ in cols]
    order = [] if after is None else [after]
    in_specs += [pl.BlockSpec((8, LANES), lambda i, j: (0, 0)) for _ in order]
    outs = pl.pallas_call(
        body,
        name=name,
        grid=(m // tm, n // tn),
        in_specs=in_specs,
        out_specs=[pl.BlockSpec((None, tm, tn), lambda i, j: (j, i, 0)) if col_blocks else
                   pl.BlockSpec((tm, tn), lambda i, j: (i, j)) for _ in out_dtypes],
        out_shape=[jax.ShapeDtypeStruct((n // tn, m, tn) if col_blocks else (m, n), dt) for dt in out_dtypes],
        compiler_params=_params("parallel", "parallel"),
    )(*a_list, *b_list, *tiles, *rows, *cols, *order)
    return outs[0] if len(out_dtypes) == 1 else outs


def _rms(x, w):
    r = lax.rsqrt(jnp.mean(x * x, axis=-1, keepdims=True) + NORM_EPS)
    return x * r * w


def _rms_bwd(x, w, dy):
    r = lax.rsqrt(jnp.mean(x * x, axis=-1, keepdims=True) + NORM_EPS)
    xh = x * r
    g = dy * w
    dx = r * (g - xh * jnp.mean(g * xh, axis=-1, keepdims=True))
    return dx, dy * xh


def _row_specs(tr, d):
    return pl.BlockSpec((tr, d), lambda i: (i, 0)), pl.BlockSpec((1, d), lambda i: (0, 0))


def _prenorm(name, h, w, after):
    t, d = h.shape
    tr = _tile(t, 512, 8)
    row, vec = _row_specs(tr, d)

    def body(h_ref, w_ref, after_ref, u_ref):
        u_ref[...] = _rms(h_ref[...], w_ref[...]).astype(BF16)

    return pl.pallas_call(body, name=name, grid=(t // tr,),
                          in_specs=[row, vec, pl.BlockSpec((8, LANES), lambda i: (0, 0))], out_specs=row,
                          out_shape=jax.ShapeDtypeStruct((t, d), BF16), compiler_params=_params("parallel"))(
                              h, w, after)


def _post_pre(name, h, m, w_post, w_pre):
    t, d = h.shape
    tr = _tile(t, 512, 8)
    row, vec = _row_specs(tr, d)

    def body(h_ref, m_ref, wq_ref, wp_ref, hn_ref, u_ref):
        hn = h_ref[...] + _rms(m_ref[...], wq_ref[...])
        hn_ref[...] = hn
        u_ref[...] = _rms(hn, wp_ref[...]).astype(BF16)

    return pl.pallas_call(body, name=name, grid=(t // tr,), in_specs=[row, row, vec, vec], out_specs=[row, row],
                          out_shape=[jax.ShapeDtypeStruct((t, d), F32), jax.ShapeDtypeStruct((t, d), BF16)],
                          compiler_params=_params("parallel"))(h, m, w_post, w_pre)


def _final_loss(name, h, m, w_post, target):
    t, d = h.shape
    tr = _tile(t, 512, 8)
    row, vec = _row_specs(tr, d)

    def body(h_ref, m_ref, wq_ref, t_ref, dh_ref, loss_ref):
        err = h_ref[...] + _rms(m_ref[...], wq_ref[...]) - t_ref[...]
        dh_ref[...] = err * (1.0 / d)
        part = 0.5 * jnp.sum(jnp.mean(err * err, axis=-1, keepdims=True), axis=0, keepdims=True)
        _acc(loss_ref, jnp.broadcast_to(part, (1, LANES)), pl.program_id(0) == 0)

    return pl.pallas_call(body, name=name, grid=(t // tr,), in_specs=[row, row, vec, row],
                          out_specs=[row, pl.BlockSpec((1, LANES), lambda i: (0, 0))],
                          out_shape=[jax.ShapeDtypeStruct((t, d), F32), jax.ShapeDtypeStruct((1, LANES), F32)],
                          compiler_params=_params("arbitrary"))(h, m, w_post, target)


def _norm_bwd(name, dh, pre=None, post=None, after=None):
    t, d = dh.shape
    tr = _tile(t, 512, 8)
    row, vec = _row_specs(tr, d)
    has_pre, has_post = pre is not None, post is not None

    def body(*refs):
        it = iter(refs)
        dh_ref = next(it)
        if has_pre:
            du_ref, x_ref, wp_ref = next(it), next(it), next(it)
        if has_post:
            m_ref, wq_ref = next(it), next(it)
        if after is not None:
            next(it)
        first = pl.program_id(0) == 0
        dh_v = dh_ref[...]
        if has_pre:
            dhn_ref, dwp_ref = next(it), next(it)
            dx, dwr = _rms_bwd(x_ref[...], wp_ref[...], du_ref[...])
            dh_v = dh_v + dx
            dhn_ref[...] = dh_v
            _acc(dwp_ref, jnp.sum(dwr, axis=0, keepdims=True), first)
        if has_post:
            dm_ref, dwq_ref, dms_ref = next(it), next(it), next(it)
            dm, dwr = _rms_bwd(m_ref[...], wq_ref[...], dh_v)
            dm_ref[...] = dm.astype(BF16)
            _acc(dwq_ref, jnp.sum(dwr, axis=0, keepdims=True), first)
            _acc(dms_ref, jnp.sum(dm, axis=0, keepdims=True), first)

    ins, in_specs, out_specs, out_shape = [dh], [row], [], []
    if has_pre:
        ins += list(pre)
        in_specs += [row, row, vec]
        out_specs += [row, vec]
        out_shape += [jax.ShapeDtypeStruct((t, d), F32), jax.ShapeDtypeStruct((1, d), F32)]
    if has_post:
        ins += list(post)
        in_specs += [row, vec]
        out_specs += [row, vec, vec]
        out_shape += [jax.ShapeDtypeStruct((t, d), BF16), jax.ShapeDtypeStruct((1, d), F32),
                      jax.ShapeDtypeStruct((1, d), F32)]
    if after is not None:
        ins.append(after)
        in_specs.append(pl.BlockSpec((8, LANES), lambda i: (0, 0)))
    return pl.pallas_call(body, name=name, grid=(t // tr,), in_specs=in_specs, out_specs=out_specs,
                          out_shape=out_shape, compiler_params=_params("arbitrary"))(*ins)


HALO = 8


def _shift_later(cur, prev, s):
    rolled = pltpu.roll(cur, s, 0)
    row = lax.broadcasted_iota(jnp.int32, prev.shape, 0)
    first = jnp.where(row < s, pltpu.roll(prev, s, 0), rolled[0:HALO])
    return jnp.concatenate([first, rolled[HALO:]], axis=0)


def _shift_earlier(cur, nxt, s):
    tt = cur.shape[0]
    rolled = pltpu.roll(cur, tt - s, 0)
    row = lax.broadcasted_iota(jnp.int32, nxt.shape, 0)
    last = jnp.where(row >= HALO - s, pltpu.roll(nxt, HALO - s, 0), rolled[tt - HALO:])
    return jnp.concatenate([rolled[:tt - HALO], last], axis=0)


def _conv_fwd(zx, col0, n_ch, conv_w, conv_b):
    t = zx.shape[0]
    tc = _tile(n_ch, 512)
    tt = _tile(t, 1024, 8)
    cb0 = col0 // tc
    assert col0 % tc == 0
    kw = SSD_CONV_WIDTH

    def body(x_ref, p_ref, w_ref, b_ref, o_ref):
        cur = x_ref[...]
        prev = jnp.where(pl.program_id(1) > 0, p_ref[...], 0.0)
        w = w_ref[...]
        acc = b_ref[...] + w[kw - 1:kw, :] * cur
        for k in range(kw - 1):
            acc = acc + w[k:k + 1, :] * _shift_later(cur, prev, kw - 1 - k)
        o_ref[...] = acc

    return pl.pallas_call(
        body, name="ssd_conv_fwd", grid=(n_ch // tc, t // tt),
        in_specs=[pl.BlockSpec((tt, tc), lambda j, i: (i, cb0 + j)),
                  pl.BlockSpec((HALO, tc), lambda j, i: (jnp.maximum(i * (tt // HALO) - 1, 0), cb0 + j)),
                  pl.BlockSpec((kw, tc), lambda j, i: (0, j)),
                  pl.BlockSpec((1, tc), lambda j, i: (0, j))],
        out_specs=pl.BlockSpec((tt, tc), lambda j, i: (i, j)),
        out_shape=jax.ShapeDtypeStruct((t, n_ch), F32),
        compiler_params=_params("parallel", "parallel"))(zx, zx, conv_w, conv_b)


def _conv_bwd(name, dpre, zx, col0, conv_w):
    t, n_ch = dpre.shape
    tc = _tile(n_ch, 512)
    tt = _tile(t, 1024, 8)
    cb0 = col0 // tc
    kw = SSD_CONV_WIDTH
    nt = t // tt

    def body(d_ref, dn_ref, x_ref, p_ref, w_ref, dx_ref, dw_ref, db_ref):
        i = pl.program_id(1)
        d = d_ref[...]
        d_next = jnp.where(i < nt - 1, dn_ref[...], 0.0)
        x = x_ref[...]
        x_prev = jnp.where(i > 0, p_ref[...], 0.0)
        w = w_ref[...]
        dx = w[kw - 1:kw, :] * d
        for k in range(kw - 1):
            dx = dx + w[k:k + 1, :] * _shift_earlier(d, d_next, kw - 1 - k)
        dx_ref[...] = dx.astype(BF16)
        first = i == 0
        for k in range(kw):
            xs = x if k == kw - 1 else _shift_later(x, x_prev, kw - 1 - k)
            val = jnp.sum(d * xs, axis=0, keepdims=True)

            @pl.when(first)
            def _():
                dw_ref[k:k + 1, :] = val

            @pl.when(jnp.logical_not(first))
            def _():
                dw_ref[k:k + 1, :] += val
        _acc(db_ref, jnp.sum(d, axis=0, keepdims=True), first)

    return pl.pallas_call(
        body, name=name, grid=(n_ch // tc, nt),
        in_specs=[pl.BlockSpec((tt, tc), lambda j, i: (i, j)),
                  pl.BlockSpec((HALO, tc), lambda j, i: (jnp.minimum((i + 1) * (tt // HALO), t // HALO - 1), j)),
                  pl.BlockSpec((tt, tc), lambda j, i: (i, cb0 + j)),
                  pl.BlockSpec((HALO, tc), lambda j, i: (jnp.maximum(i * (tt // HALO) - 1, 0), cb0 + j)),
                  pl.BlockSpec((kw, tc), lambda j, i: (0, j))],
        out_specs=[pl.BlockSpec((tt, tc), lambda j, i: (i, j)),
                   pl.BlockSpec((kw, tc), lambda j, i: (0, j)),
                   pl.BlockSpec((1, tc), lambda j, i: (0, j))],
        out_shape=[jax.ShapeDtypeStruct((t, n_ch), BF16), jax.ShapeDtypeStruct((kw, n_ch), F32),
                   jax.ShapeDtypeStruct((1, n_ch), F32)],
        compiler_params=_params("parallel", "arbitrary"))(dpre, dpre, zx, zx, conv_w)


def _head_of_lane(shape, width):
    return lax.broadcasted_iota(jnp.int32, shape, len(shape) - 1) // width


def _expand(v, n_rows):
    head = _head_of_lane((n_rows, GW), SSD_HEAD_DIM)
    out = jnp.zeros((n_rows, GW), F32)
    for j in range(SSD_HPG):
        out = jnp.where(head == j, v[:, j:j + 1], out)
    return out


def _contract(v, n_rows):
    head = _head_of_lane((n_rows, GW), SSD_HEAD_DIM)
    lane = lax.broadcasted_iota(jnp.int32, (n_rows, LANES), 1)
    out = jnp.zeros((n_rows, LANES), F32)
    for j in range(SSD_HPG):
        s = jnp.sum(jnp.where(head == j, v, 0.0), axis=1, keepdims=True)
        out = jnp.where(lane == j, s, out)
    return out


def _ssd_dt_prep(zdt, bias, alog, ng):
    t = zdt.shape[0]
    q = SSD_CHUNK

    def body(z_ref, b_ref, a_ref, dt_ref, cum_ref, cumr_ref, sg_ref):
        raw = z_ref[...] + b_ref[...]
        dt = _softplus(raw)
        sgd = _sigmoid(raw)
        row = lax.broadcasted_iota(jnp.int32, (q, q), 0)
        col = lax.broadcasted_iota(jnp.int32, (q, q), 1)
        cum = _dot_f32((col <= row).astype(F32), dt * (-jnp.exp(a_ref[...])))
        cum_t = cum.T
        lane = lax.broadcasted_iota(jnp.int32, (q, LANES), 1)
        for g in range(ng):
            shift = (LANES - g * SSD_HPG) % LANES

            def group(v):
                return jnp.where(lane < SSD_HPG, pltpu.roll(v, shift, 1) if shift else v, 0.0)

            dt_ref[g] = group(dt)
            cum_ref[g] = group(cum)
            sg_ref[g] = group(sgd)
            cumr_ref[g] = (pltpu.roll(cum_t, shift, 0) if shift else cum_t)[0:8, :]

    cols = pl.BlockSpec((ng, q, LANES), lambda c: (0, c, 0))
    vec = pl.BlockSpec((1, LANES), lambda c: (0, 0))
    col_shape = jax.ShapeDtypeStruct((ng, t, LANES), F32)
    return pl.pallas_call(body, name="ssd_dt_prep", grid=(t // q,),
                          in_specs=[pl.BlockSpec((q, LANES), lambda c: (c, 0)), vec, vec],
                          out_specs=[cols, cols, pl.BlockSpec((ng, 8, q), lambda c: (0, 0, c)), cols],
                          out_shape=[col_shape, col_shape, jax.ShapeDtypeStruct((ng, 8, t), F32), col_shape],
                          compiler_params=_params("parallel"))(zdt, bias, alog)


def _ssd_common(pre, dt, cum, cum_r, alog_c):
    q = SSD_CHUNK
    sg = _sigmoid(pre)
    act = pre * sg
    xa = act[:, :GW]
    bm = act[:, GW:GW + SSD_D_STATE].astype(BF16)
    cm = act[:, GW + SSD_D_STATE:].astype(BF16)
    row = lax.broadcasted_iota(jnp.int32, (q, q), 0)
    col = lax.broadcasted_iota(jnp.int32, (q, q), 1)
    tril = col <= row
    a_c = -jnp.exp(alog_c)
    g = _dot_nt(cm, bm)
    dt_x = _expand(dt, q)
    xdt = xa * dt_x
    cl = cum[q - 1:q, :]
    e_c = jnp.exp(cl - cum)
    lam_c = jnp.exp(cum)
    return dict(sg=sg, xa=xa, bm=bm, cm=cm, tril=tril, row=row, col=col, dt=dt, a_c=a_c, cum=cum, cum_r=cum_r,
                g=g, dt_x=dt_x, xdt=xdt, cl=cl, e_c=e_c, lam_c=lam_c)


SSD_GPS_FWD = 8
SSD_GPS_BWD = 2


def _ssd_specs(nc, rev, ng, gps):
    q = SSD_CHUNK
    xw, nw = gps * GW, gps * SSD_D_STATE
    b_off = ng * GW // nw
    c_off = (ng * GW + ng * SSD_D_STATE) // nw
    assert ng % gps == 0 and (ng * GW) % nw == 0 and (ng * SSD_D_STATE) % nw == 0

    def ch(c):
        return nc - 1 - c if rev else c

    chunk_grp = [pl.BlockSpec((q, xw), lambda g, c: (ch(c), g)),
                 pl.BlockSpec((q, nw), lambda g, c: (ch(c), b_off + g)),
                 pl.BlockSpec((q, nw), lambda g, c: (ch(c), c_off + g))]
    col_form = pl.BlockSpec((gps, q, LANES), lambda g, c: (g, ch(c), 0))
    row_form = pl.BlockSpec((gps, 8, q), lambda g, c: (g, 0, ch(c)))
    col_par = pl.BlockSpec((gps, 1, LANES), lambda g, c: (g, 0, 0))
    y_spec = pl.BlockSpec((q, xw), lambda g, c: (ch(c), g))
    st_spec = pl.BlockSpec((gps, None, GW, SSD_D_STATE), lambda g, c: (g, ch(c), 0, 0))
    bc_spec = pl.BlockSpec((q, nw), lambda g, c: (ch(c), g))
    return chunk_grp, col_form, row_form, col_par, y_spec, st_spec, bc_spec


def _ssd_group_views(gi, wide, narrow, stacked):
    xs, ns = pl.ds(gi * GW, GW), pl.ds(gi * SSD_D_STATE, SSD_D_STATE)
    return [r.at[:, xs] for r in wide], [r.at[:, ns] for r in narrow], [r.at[gi] for r in stacked]


def _ssd_fwd(pre, dt_c, cum_c, cum_r, alog_c, dsk_c):
    t = pre.shape[0]
    ng = pre.shape[1] // GC
    q = SSD_CHUNK
    nc = t // q
    gps = SSD_GPS_FWD if ng % SSD_GPS_FWD == 0 else SSD_GPS_BWD
    chunk_grp, col_form, row_form, col_par, y_spec, st_spec, _ = _ssd_specs(nc, False, ng, gps)

    def body(px_ref, pb_ref, pc_ref, dt_ref, cum_ref, cumr_ref, ac_ref, dk_ref, y_ref, sp_ref, st_ref):
        @pl.when(pl.program_id(1) == 0)
        def _():
            st_ref[...] = jnp.zeros_like(st_ref)

        for gi in range(gps):
            (px, y), (pb, pc), rest = _ssd_group_views(
                gi, (px_ref, y_ref), (pb_ref, pc_ref), (dt_ref, cum_ref, cumr_ref, ac_ref, dk_ref, sp_ref, st_ref))
            one_group(px, pb, pc, *rest[:5], y, *rest[5:])

    def one_group(px_ref, pb_ref, pc_ref, dt_ref, cum_ref, cumr_ref, ac_ref, dk_ref, y_ref, sp_ref, st_ref):
        pre_v = jnp.concatenate([px_ref[...], pb_ref[...], pc_ref[...]], axis=1)
        v = _ssd_common(pre_v, dt_ref[...], cum_ref[...], cumr_ref[...], ac_ref[...])
        s0 = st_ref[...]
        sp_ref[...] = s0
        r = _dot_nt(v["cm"], s0.astype(BF16))
        y = _expand(v["lam_c"], q) * r + _expand(dk_ref[...], 1) * v["xa"]
        head = _head_of_lane((q, GW), SSD_HEAD_DIM)
        for j in range(SSD_HPG):
            diff = v["cum"][:, j:j + 1] - v["cum_r"][j:j + 1, :]
            w = (v["g"] * jnp.exp(jnp.where(v["tril"], diff, -jnp.inf))).astype(BF16)
            y = y + _dot(w, jnp.where(head == j, v["xdt"], 0.0).astype(BF16))
        y_ref[...] = y
        ds = _dot_tn((v["xdt"] * _expand(v["e_c"], q)).astype(BF16), v["bm"])
        for j in range(SSD_HPG):
            rows = slice(j * SSD_HEAD_DIM, (j + 1) * SSD_HEAD_DIM)
            st_ref[rows, :] = s0[rows, :] * jnp.exp(v["cum_r"][j:j + 1, q - 1:q]) + ds[rows, :]

    return pl.pallas_call(
        body, name="ssd_scan_fwd", grid=(ng // gps, nc),
        in_specs=chunk_grp + [col_form, col_form, row_form, col_par, col_par],
        out_specs=[y_spec, st_spec],
        out_shape=[jax.ShapeDtypeStruct((t, ng * GW), F32), jax.ShapeDtypeStruct((ng, nc, GW, SSD_D_STATE), F32)],
        scratch_shapes=[pltpu.VMEM((gps, GW, SSD_D_STATE), F32)],
        compiler_params=_params("parallel", "arbitrary"))(pre, pre, pre, dt_c, cum_c, cum_r, alog_c, dsk_c)


def _ssd_bwd(dy, pre, states, dt_c, cum_c, cum_r, sgd_c, alog_c, dsk_c):
    t = pre.shape[0]
    ng = pre.shape[1] // GC
    q = SSD_CHUNK
    nc = t // q
    gps = SSD_GPS_BWD
    chunk_grp, col_form, row_form, col_par, y_spec, st_spec, bc_spec = _ssd_specs(nc, True, ng, gps)

    def body(dy_ref, px_ref, pb_ref, pc_ref, sp_ref, dt_ref, cum_ref, cumr_ref, sgd_ref, ac_ref, dk_ref,
             dpx_ref, dpb_ref, dpc_ref, ddt_ref, dbias_ref, dalog_ref, dd_ref, ds_ref):
        @pl.when(pl.program_id(1) == 0)
        def _():
            ds_ref[...] = jnp.zeros_like(ds_ref)

        for gi in range(gps):
            (dy, px, dpx), (pb, pc, dpb, dpc), rest = _ssd_group_views(
                gi, (dy_ref, px_ref, dpx_ref), (pb_ref, pc_ref, dpb_ref, dpc_ref),
                (sp_ref, dt_ref, cum_ref, cumr_ref, sgd_ref, ac_ref, dk_ref, ddt_ref, dbias_ref, dalog_ref, dd_ref,
                 ds_ref))
            one_group(dy, px, pb, pc, *rest[:7], dpx, dpb, dpc, *rest[7:])

    def one_group(dy_ref, px_ref, pb_ref, pc_ref, sp_ref, dt_ref, cum_ref, cumr_ref, sgd_ref, ac_ref, dk_ref,
                  dpx_ref, dpb_ref, dpc_ref, ddt_ref, dbias_ref, dalog_ref, dd_ref, ds_ref):
        first = pl.program_id(1) == 0
        pre_v = jnp.concatenate([px_ref[...], pb_ref[...], pc_ref[...]], axis=1)
        v = _ssd_common(pre_v, dt_ref[...], cum_ref[...], cumr_ref[...], ac_ref[...])
        xa, bm, cm, xdt, cum, cum_r = v["xa"], v["bm"], v["cm"], v["xdt"], v["cum"], v["cum_r"]
        xdt_b = xdt.astype(BF16)
        dy_v = dy_ref[...]
        s0 = sp_ref[...]
        ds1 = ds_ref[...]
        s0b, ds1b = s0.astype(BF16), ds1.astype(BF16)
        head = _head_of_lane((q, GW), SSD_HEAD_DIM)
        lane = lax.broadcasted_iota(jnp.int32, (q, LANES), 1)
        lane1 = lax.broadcasted_iota(jnp.int32, (1, LANES), 1)
        lam_x = _expand(v["lam_c"], q)
        e_x = _expand(v["e_c"], q)

        dxa = _expand(dk_ref[...], 1) * dy_v
        dd = _contract(jnp.sum(dy_v * xa, axis=0, keepdims=True), 1)
        r = _dot_nt(cm, s0b)
        dcum = _contract(dy_v * r * lam_x, q)
        drb = (lam_x * dy_v).astype(BF16)
        dc = _dot(drb, s0b)
        ds0 = _dot_tn(drb, cm)
        extra = jnp.zeros((1, LANES), F32)
        for j in range(SSD_HPG):
            rows = slice(j * SSD_HEAD_DIM, (j + 1) * SSD_HEAD_DIM)
            lam_last = jnp.exp(cum_r[j:j + 1, q - 1:q])
            ds_ref[rows, :] = ds0[rows, :] + lam_last * ds1[rows, :]
            tot = jnp.sum(jnp.sum(ds1[rows, :] * s0[rows, :], axis=1, keepdims=True), axis=0, keepdims=True)
            extra = jnp.where(lane1 == j, lam_last * tot, extra)
        dv = _dot_nt(bm, ds1b)
        db = _dot((xdt * e_x).astype(BF16), ds1b)
        dxdt = e_x * dv
        dee = _contract(dv * xdt, q) * v["e_c"]
        dcum = dcum - dee
        extra = extra + jnp.sum(dee, axis=0, keepdims=True)
        dg = jnp.zeros((q, q), F32)
        col_sums = jnp.zeros((q, q), F32)
        for j in range(SSD_HPG):
            diff = cum[:, j:j + 1] - cum_r[j:j + 1, :]
            el = jnp.exp(jnp.where(v["tril"], diff, -jnp.inf))
            gl = v["g"] * el
            dym = jnp.where(head == j, dy_v, 0.0).astype(BF16)
            dwm = _dot_nt(dym, xdt_b)
            dxdt = dxdt + _dot_tn(gl.astype(BF16), dym)
            z = dwm * gl
            dcum = jnp.where(lane == j, dcum + jnp.sum(z, axis=1, keepdims=True), dcum)
            col_sums = jnp.where(v["row"] == j, jnp.sum(z, axis=0, keepdims=True), col_sums)
            dg = dg + dwm * el
        dcum = dcum - col_sums.T
        dgb = dg.astype(BF16)
        dc = dc + _dot(dgb, bm)
        db = db + _dot_tn(dgb, cm)
        da = _dot_f32((v["row"] <= v["col"]).astype(F32), dcum) + extra
        ddt = _contract(dxdt * xa, q) + v["a_c"] * da
        dalog = jnp.sum(v["dt"] * da, axis=0, keepdims=True) * v["a_c"]
        dxa = dxa + v["dt_x"] * dxdt
        ddt_raw = jnp.where(lane < SSD_HPG, ddt * sgd_ref[...], 0.0)
        sgrad = _silu_grad(pre_v, v["sg"])
        dpx_ref[...] = dxa * sgrad[:, :GW]
        dpb_ref[...] = db * sgrad[:, GW:GW + SSD_D_STATE]
        dpc_ref[...] = dc * sgrad[:, GW + SSD_D_STATE:]
        ddt_ref[...] = ddt_raw
        _acc(dbias_ref, jnp.sum(ddt_raw, axis=0, keepdims=True), first)
        _acc(dalog_ref, jnp.where(lane1 < SSD_HPG, dalog, 0.0), first)
        _acc(dd_ref, dd, first)

    return pl.pallas_call(
        body, name="ssd_scan_bwd", grid=(ng // gps, nc),
        in_specs=[y_spec] + chunk_grp + [st_spec, col_form, col_form, row_form, col_form, col_par, col_par],
        out_specs=[y_spec, bc_spec, bc_spec, col_form, col_par, col_par, col_par],
        out_shape=[jax.ShapeDtypeStruct((t, ng * GW), F32), jax.ShapeDtypeStruct((t, ng * SSD_D_STATE), F32),
                   jax.ShapeDtypeStruct((t, ng * SSD_D_STATE), F32), jax.ShapeDtypeStruct((ng, t, LANES), F32),
                   jax.ShapeDtypeStruct((ng, 1, LANES), F32), jax.ShapeDtypeStruct((ng, 1, LANES), F32),
                   jax.ShapeDtypeStruct((ng, 1, LANES), F32)],
        scratch_shapes=[pltpu.VMEM((gps, GW, SSD_D_STATE), F32)],
        compiler_params=_params("parallel", "arbitrary"))(dy, pre, pre, pre, states, dt_c, cum_c, cum_r, sgd_c, alog_c,
                                                           dsk_c)


def _gate_norm_fwd(y, zx, norm_w):
    t, di = y.shape
    tr = _tile(t, 512, 8)
    ng = di // GW

    def body(y_ref, z_ref, w_ref, o_ref):
        z = z_ref[...]
        gate = y_ref[...] * (z * _sigmoid(z))
        w = w_ref[...]
        for g in range(ng):
            cols = slice(g * GW, (g + 1) * GW)
            gs = gate[:, cols]
            r = lax.rsqrt(jnp.mean(gs * gs, axis=-1, keepdims=True) + NORM_EPS)
            o_ref[:, cols] = (gs * r * w[:, cols]).astype(BF16)

    row = pl.BlockSpec((tr, di), lambda i: (i, 0))
    return pl.pallas_call(body, name="ssd_gate_norm_fwd", grid=(t // tr,),
                          in_specs=[row, row, pl.BlockSpec((1, di), lambda i: (0, 0))], out_specs=row,
                          out_shape=jax.ShapeDtypeStruct((t, di), BF16), compiler_params=_params("parallel"))(
                              y, zx, norm_w)


def _gate_norm_bwd(dyn, y, zx, norm_w, after):
    t, di = y.shape
    tr = _tile(t, 256, 8)
    ng = di // GW

    def body(d_ref, y_ref, z_ref, w_ref, after_ref, dy_ref, dz_ref, dw_ref):
        z = z_ref[...]
        yv = y_ref[...]
        sg = _sigmoid(z)
        sz = z * sg
        gate = yv * sz
        w = w_ref[...]
        d = d_ref[...]
        dsz = _silu_grad(z, sg)
        dws = []
        for g in range(ng):
            cols = slice(g * GW, (g + 1) * GW)
            dg, dwr = _rms_bwd(gate[:, cols], w[:, cols], d[:, cols])
            dy_ref[:, cols] = dg * sz[:, cols]
            dz_ref[:, cols] = (dg * yv[:, cols] * dsz[:, cols]).astype(BF16)
            dws.append(jnp.sum(dwr, axis=0, keepdims=True))
        first = pl.program_id(0) == 0
        for g in range(ng):
            cols = slice(g * GW, (g + 1) * GW)

            @pl.when(first)
            def _():
                dw_ref[:, cols] = dws[g]

            @pl.when(jnp.logical_not(first))
            def _():
                dw_ref[:, cols] += dws[g]

    row = pl.BlockSpec((tr, di), lambda i: (i, 0))
    vec = pl.BlockSpec((1, di), lambda i: (0, 0))
    return pl.pallas_call(body, name="ssd_gate_norm_bwd", grid=(t // tr,),
                          in_specs=[row, row, row, vec, pl.BlockSpec((8, LANES), lambda i: (0, 0))],
                          out_specs=[row, row, vec],
                          out_shape=[jax.ShapeDtypeStruct((t, di), F32), jax.ShapeDtypeStruct((t, di), BF16),
                                     jax.ShapeDtypeStruct((1, di), F32)],
                          compiler_params=_params("arbitrary"))(dyn, y, zx, norm_w, after)


def _attn_mask_t(n):
    w = ATTN_WINDOW
    kpos = lax.broadcasted_iota(jnp.int32, (2 * w, ATTN_REP * w), 0)
    qpos = lax.broadcasted_iota(jnp.int32, (2 * w, ATTN_REP * w), 1) % w + w
    rel = qpos - kpos
    return (rel >= 0) & (rel < w) & jnp.logical_not((n == 0) & (kpos < w))


def _attn_probs_t(qts, ktb, mask, sink):
    s = _dot_tn(ktb, qts) * (ATTN_HEAD_DIM ** -0.5)
    s = jnp.where(mask, s, -jnp.inf)
    m = jnp.maximum(jnp.max(s, axis=0, keepdims=True), sink)
    e = jnp.exp(s - m)
    es = jnp.exp(sink - m)
    inv = 1.0 / (jnp.sum(e, axis=0, keepdims=True) + es)
    return e * inv, es * inv


def _attn_blocks_t(kv, q_ref, kc_ref, vc_ref, kp_ref, vp_ref):
    hd = ATTN_HEAD_DIM
    rows = slice(kv * hd, (kv + 1) * hd)
    ktb = jnp.concatenate([kp_ref[rows, :], kc_ref[rows, :]], axis=1)
    vtb = jnp.concatenate([vp_ref[rows, :], vc_ref[rows, :]], axis=1)
    qts = jnp.concatenate([q_ref[(kv * ATTN_REP + r) * hd:(kv * ATTN_REP + r + 1) * hd, :]
                           for r in range(ATTN_REP)], axis=1)
    return qts, ktb, vtb


def _attn_specs_t(nb, cur, prev):
    w, hd = ATTN_WINDOW, ATTN_HEAD_DIM
    kd = ATTN_N_KV * hd
    qd = ATTN_REP * kd
    return [pl.BlockSpec((qd, w), lambda n: (0, cur(n))),
            pl.BlockSpec((kd, w), lambda n: (ATTN_REP, cur(n))),
            pl.BlockSpec((kd, w), lambda n: (ATTN_REP + 1, cur(n))),
            pl.BlockSpec((kd, w), lambda n: (ATTN_REP, prev(n))),
            pl.BlockSpec((kd, w), lambda n: (ATTN_REP + 1, prev(n)))]


def _attn_fwd_t(qkv_t, sinks_rep):
    t = qkv_t.shape[1]
    w, hd = ATTN_WINDOW, ATTN_HEAD_DIM
    qd = ATTN_N_KV * ATTN_REP * hd
    nb = t // w

    def body(q_ref, kc_ref, vc_ref, kp_ref, vp_ref, s_ref, o_ref):
        mask = _attn_mask_t(pl.program_id(0))
        for kv in range(ATTN_N_KV):
            qts, ktb, vtb = _attn_blocks_t(kv, q_ref, kc_ref, vc_ref, kp_ref, vp_ref)
            p, _ = _attn_probs_t(qts, ktb, mask, s_ref[kv])
            ots = _dot(vtb, p.astype(BF16))
            for r in range(ATTN_REP):
                h = kv * ATTN_REP + r
                o_ref[h * hd:(h + 1) * hd, :] = ots[:, r * w:(r + 1) * w].astype(BF16)

    return pl.pallas_call(
        body, name="attn_fwd", grid=(nb,),
        in_specs=_attn_specs_t(nb, lambda n: n, lambda n: jnp.maximum(n - 1, 0)) + [
            pl.BlockSpec(sinks_rep.shape, lambda n: (0, 0, 0))],
        out_specs=pl.BlockSpec((qd, w), lambda n: (0, n)),
        out_shape=jax.ShapeDtypeStruct((qd, t), BF16),
        compiler_params=_params("parallel"))(qkv_t, qkv_t, qkv_t, qkv_t, qkv_t, sinks_rep)


def _attn_bwd_t(qkv_t, do_t, sinks_rep):
    t = qkv_t.shape[1]
    w, hd = ATTN_WINDOW, ATTN_HEAD_DIM
    kd = ATTN_N_KV * hd
    qd = ATTN_REP * kd
    nq = ATTN_N_KV * ATTN_REP
    nb = t // w
    rows_all = qd + 2 * kd

    def body(q_ref, kc_ref, vc_ref, kp_ref, vp_ref, do_ref, s_ref, dqkv_ref, bsum_ref, dsk_ref,
             carry_ref, new_ref, bacc_ref, sacc_ref):
        n = pl.program_id(0)

        @pl.when(n == 0)
        def _():
            carry_ref[...] = jnp.zeros_like(carry_ref)
            bacc_ref[...] = jnp.zeros_like(bacc_ref)
            sacc_ref[...] = jnp.zeros_like(sacc_ref)

        @pl.when(n < nb)
        def _():
            mask = _attn_mask_t(n)
            for kv in range(ATTN_N_KV):
                qts, ktb, vtb = _attn_blocks_t(kv, q_ref, kc_ref, vc_ref, kp_ref, vp_ref)
                dots = jnp.concatenate([do_ref[(kv * ATTN_REP + r) * hd:(kv * ATTN_REP + r + 1) * hd, :]
                                        for r in range(ATTN_REP)], axis=1)
                p, ps = _attn_probs_t(qts, ktb, mask, s_ref[kv])
                dpt = _dot_tn(vtb, dots)
                delta = jnp.sum(p * dpt, axis=0, keepdims=True)
                dst = (p * (dpt - delta) * (hd ** -0.5)).astype(BF16)
                dqts = _dot(ktb, dst)
                for r in range(ATTN_REP):
                    h = kv * ATTN_REP + r
                    new_ref[h * hd:(h + 1) * hd, :] = dqts[:, r * w:(r + 1) * w]
                dktb = _dot_nt(qts, dst)
                dvtb = _dot_nt(dots, p.astype(BF16))
                krows = slice(qd + kv * hd, qd + (kv + 1) * hd)
                vrows = slice(qd + kd + kv * hd, qd + kd + (kv + 1) * hd)
                carry_ref[krows, :] += dktb[:, :w]
                carry_ref[vrows, :] += dvtb[:, :w]
                new_ref[krows, :] = dktb[:, w:]
                new_ref[vrows, :] = dvtb[:, w:]
                sacc_ref[kv] += -(ps * delta)

        @pl.when(n >= 1)
        def _():
            done = carry_ref[...]
            dqkv_ref[...] = done.astype(BF16)
            bacc_ref[...] += done

        @pl.when(n < nb)
        def _():
            carry_ref[...] = new_ref[...]

        @pl.when(n == nb)
        def _():
            bsum_ref[...] = jnp.sum(bacc_ref[...], axis=1, keepdims=True)
            lane = lax.broadcasted_iota(jnp.int32, (1, nq), 1)
            dsk = jnp.zeros((1, nq), F32)
            for kv in range(ATTN_N_KV):
                acc = sacc_ref[kv]
                for r in range(ATTN_REP):
                    tot = jnp.sum(acc[:, r * w:(r + 1) * w], axis=1, keepdims=True)
                    dsk = jnp.where(lane == kv * ATTN_REP + r, tot, dsk)
            dsk_ref[...] = dsk

    cur = lambda n: jnp.minimum(n, nb - 1)
    prev = lambda n: jnp.maximum(jnp.minimum(n, nb - 1) - 1, 0)
    return pl.pallas_call(
        body, name="attn_bwd", grid=(nb + 1,),
        in_specs=_attn_specs_t(nb, cur, prev) + [pl.BlockSpec((qd, w), lambda n: (0, cur(n))),
                                                 pl.BlockSpec(sinks_rep.shape, lambda n: (0, 0, 0))],
        out_specs=[pl.BlockSpec((rows_all, w), lambda n: (0, jnp.maximum(n - 1, 0))),
                   pl.BlockSpec((rows_all, 1), lambda n: (0, 0)),
                   pl.BlockSpec((1, nq), lambda n: (0, 0))],
        out_shape=[jax.ShapeDtypeStruct((rows_all, t), BF16), jax.ShapeDtypeStruct((rows_all, 1), F32),
                   jax.ShapeDtypeStruct((1, nq), F32)],
        scratch_shapes=[pltpu.VMEM((rows_all, w), F32), pltpu.VMEM((rows_all, w), F32),
                        pltpu.VMEM((rows_all, w), F32), pltpu.VMEM(sinks_rep.shape, F32)],
        compiler_params=_params("arbitrary"))(qkv_t, qkv_t, qkv_t, qkv_t, qkv_t, do_t, sinks_rep)


HBM_SPEC = pl.BlockSpec(memory_space=pl.ANY)
HBM_ONLY = pl.BlockSpec(memory_space=pltpu.HBM)


def _comm_call(name, body, ins, out_shapes, n_sems):
    return pl.pallas_call(
        body, name=name, in_specs=[HBM_SPEC] * len(ins), out_specs=[HBM_SPEC] * len(out_shapes),
        out_shape=out_shapes,
        scratch_shapes=[pltpu.SemaphoreType.DMA((s,)) for s in n_sems])(*ins)


def _all_gather(name, shards, after):
    n = len(shards)
    na = len(after)

    def body(*refs):
        x_refs, out_refs = refs[:n], refs[n + na:2 * n + na]
        send_sems, recv_sems, local_sems = refs[2 * n + na:]
        x, y, c = lax.axis_index("x"), lax.axis_index("y"), lax.axis_index("c")
        me, sibling = (x, y, c), (x, y, 1 - c)
        chips = [(1 - x, y), (x, 1 - y), (1 - x, 1 - y)]

        def slot(i, px, py, pc):
            return out_refs[i].at[4 * px + 2 * py + pc]

        def copy(k, i, block, to, src=None):
            return pltpu.make_async_remote_copy(
                src_ref=slot(i, *block) if src is None else src, dst_ref=slot(i, *block),
                send_sem=send_sems.at[k * n + i], recv_sem=recv_sems.at[k * n + i], device_id=to,
                device_id_type=MESH)

        mine = [pltpu.make_async_copy(x_refs[i], slot(i, *me), local_sems.at[i]) for i in range(n)]
        first = []
        for i in range(n):
            mine[i].start()
            first.append(copy(0, i, me, sibling, src=x_refs[i]))
            first += [copy(1 + j, i, me, (*chip, c), src=x_refs[i]) for j, chip in enumerate(chips)]
        for cp in first:
            cp.start()
        passed = []
        for i in range(n):
            for j, chip in enumerate(chips):
                copy(1 + j, i, (*chip, c), me).wait_recv()
                passed.append(copy(4 + j, i, (*chip, c), sibling))
                passed[-1].start()
        for i in range(n):
            copy(0, i, sibling, me).wait_recv()
            for j, chip in enumerate(chips):
                copy(4 + j, i, (*chip, 1 - c), me).wait_recv()
        for cp in first + passed:
            cp.wait_send()
        for cp in mine:
            cp.wait()

    outs = [jax.ShapeDtypeStruct((N_DEV,) + s.shape, s.dtype) for s in shards]
    return _comm_call(name, body, list(shards) + list(after), outs, (7 * n, 7 * n, n))


SEM_SPEC = pl.BlockSpec(memory_space=pltpu.SEMAPHORE)
SPLIT_COPY_EFFECT = pltpu.SideEffectType.DATAFLOW_SIDE_EFFECTING


def _in_hbm(a):
    return pltpu.with_memory_space_constraint(a, pltpu.HBM)


def _split_start(name, body, srcs, lands, n_sems):
    n = len(srcs)
    bufs = [_in_hbm(a) for a in list(srcs) + list(lands)]
    outs = pl.pallas_call(
        body, name=name,
        out_shape=(pltpu.SemaphoreType.DMA((n_sems,)), pltpu.SemaphoreType.DMA((n_sems,)),
                   *[pltpu.HBM(a.shape, a.dtype) for a in bufs], jax.ShapeDtypeStruct((8, LANES), F32)),
        in_specs=[HBM_ONLY] * (2 * n),
        out_specs=(SEM_SPEC, SEM_SPEC, *[HBM_ONLY] * (2 * n), pl.BlockSpec(memory_space=pltpu.VMEM)),
        input_output_aliases={i: 2 + i for i in range(2 * n)},
        compiler_params=pltpu.CompilerParams(has_side_effects=SPLIT_COPY_EFFECT))(*bufs)
    return outs[0], outs[1], list(outs[2:2 + n]), list(outs[2 + n:2 + 2 * n]), outs[-1]


def _split_wait(name, body, send_sems, recv_sems, srcs, lands, after):
    n = len(srcs)
    outs = pl.pallas_call(
        body, name=name,
        out_shape=[pltpu.HBM(a.shape, a.dtype) for a in list(srcs) + list(lands)],
        in_specs=[HBM_ONLY] * (2 * n) + [SEM_SPEC, SEM_SPEC, HBM_SPEC],
        out_specs=[HBM_ONLY] * (2 * n),
        input_output_aliases={i: i for i in range(2 * n)},
        compiler_params=pltpu.CompilerParams(has_side_effects=SPLIT_COPY_EFFECT))(
            *srcs, *lands, send_sems, recv_sems, after)
    return list(outs[:n]), list(outs[n:])


N_PEERS = N_DEV - 1


def _gather_peers():
    x, y, c = lax.axis_index("x"), lax.axis_index("y"), lax.axis_index("c")
    flips = [(fx, fy, fc) for fx in (0, 1) for fy in (0, 1) for fc in (0, 1) if fx or fy or fc]
    return [(1 - x if fx else x, 1 - y if fy else y, 1 - c if fc else c) for fx, fy, fc in flips]


def _block_id(dev):
    return 4 * dev[0] + 2 * dev[1] + dev[2]


def _landing_block(land_ref, shard_shape, side_by_side, dev):
    if not side_by_side:
        return land_ref.at[_block_id(dev)]
    cols = shard_shape[1]
    return land_ref.at[:, pl.ds(pl.multiple_of(_block_id(dev) * cols, LANES), cols)]


def _gather_start(name, shards, side_by_side):
    n = len(shards)

    def body(*refs):
        x_refs, land_refs = refs[:n], refs[n:2 * n]
        send_sems, recv_sems, token = refs[2 * n], refs[2 * n + 1], refs[-1]
        me = (lax.axis_index("x"), lax.axis_index("y"), lax.axis_index("c"))
        for i in range(n):
            for k, peer in enumerate(_gather_peers()):
                pltpu.make_async_remote_copy(
                    src_ref=x_refs[i], dst_ref=_landing_block(land_refs[i], shards[i].shape, side_by_side[i], me),
                    send_sem=send_sems.at[N_PEERS * i + k], recv_sem=recv_sems.at[N_PEERS * i + k],
                    device_id=peer, device_id_type=MESH).start()
            pltpu.make_async_copy(x_refs[i], _landing_block(land_refs[i], shards[i].shape, side_by_side[i], me),
                                  send_sems.at[N_PEERS * n + i]).start()
        token[...] = jnp.zeros_like(token)

    lands = [lax.empty((s.shape[0], N_DEV * s.shape[1]) if wide else (N_DEV,) + s.shape, s.dtype)
             for s, wide in zip(shards, side_by_side)]
    return _split_start(name, body, shards, lands, (N_PEERS + 1) * n)


def _gather_wait(name, send_sems, recv_sems, first, n_all, shards, lands, side_by_side, after):
    n = len(shards)

    def body(*refs):
        x_refs, land_refs = refs[:n], refs[n:2 * n]
        send_sems, recv_sems = refs[2 * n], refs[2 * n + 1]
        me = (lax.axis_index("x"), lax.axis_index("y"), lax.axis_index("c"))
        for i in range(n):
            pltpu.make_async_copy(x_refs[i], _landing_block(land_refs[i], shards[i].shape, side_by_side[i], me),
                                  send_sems.at[N_PEERS * n_all + first + i]).wait()
            for k, peer in enumerate(_gather_peers()):
                cp = pltpu.make_async_remote_copy(
                    src_ref=x_refs[i], dst_ref=_landing_block(land_refs[i], shards[i].shape, side_by_side[i], peer),
                    send_sem=send_sems.at[N_PEERS * (first + i) + k],
                    recv_sem=recv_sems.at[N_PEERS * (first + i) + k],
                    device_id=peer, device_id_type=MESH)
                cp.wait_send()
                cp.wait_recv()

    return _split_wait(name, body, send_sems, recv_sems, shards, lands, after)


def _scatter_start(name, blocks):
    n = len(blocks)

    def body(*refs):
        b_refs, land_refs = refs[:n], refs[n:2 * n]
        send_sems, recv_sems, token = refs[2 * n], refs[2 * n + 1], refs[-1]
        me = (lax.axis_index("x"), lax.axis_index("y"), lax.axis_index("c"))
        for i in range(n):
            for k, peer in enumerate(_gather_peers()):
                pltpu.make_async_remote_copy(
                    src_ref=b_refs[i].at[_block_id(peer)], dst_ref=land_refs[i].at[_block_id(me)],
                    send_sem=send_sems.at[N_PEERS * i + k], recv_sem=recv_sems.at[N_PEERS * i + k],
                    device_id=peer, device_id_type=MESH).start()
            pltpu.make_async_copy(b_refs[i].at[_block_id(me)], land_refs[i].at[_block_id(me)],
                                  send_sems.at[N_PEERS * n + i]).start()
        token[...] = jnp.zeros_like(token)

    lands = [lax.empty(b.shape, b.dtype) for b in blocks]
    return _split_start(name, body, blocks, lands, (N_PEERS + 1) * n)


def _scatter_wait(name, send_sems, recv_sems, blocks, lands, after):
    n = len(blocks)

    def body(*refs):
        b_refs, land_refs = refs[:n], refs[n:2 * n]
        send_sems, recv_sems = refs[2 * n], refs[2 * n + 1]
        me = (lax.axis_index("x"), lax.axis_index("y"), lax.axis_index("c"))
        for i in range(n):
            pltpu.make_async_copy(b_refs[i].at[_block_id(me)], land_refs[i].at[_block_id(me)],
                                  send_sems.at[N_PEERS * n + i]).wait()
            for k, peer in enumerate(_gather_peers()):
                cp = pltpu.make_async_remote_copy(
                    src_ref=b_refs[i].at[_block_id(peer)], dst_ref=land_refs[i].at[_block_id(peer)],
                    send_sem=send_sems.at[N_PEERS * i + k], recv_sem=recv_sems.at[N_PEERS * i + k],
                    device_id=peer, device_id_type=MESH)
                cp.wait_send()
                cp.wait_recv()

    return _split_wait(name, body, send_sems, recv_sems, blocks, lands, after)


def _adamw(w, g, m, v):
    m = ADAM_B1 * m + (1.0 - ADAM_B1) * g
    v = ADAM_B2 * v + (1.0 - ADAM_B2) * (g * g)
    m_hat = m / (1.0 - ADAM_B1 ** ADAM_STEP)
    v_hat = v / (1.0 - ADAM_B2 ** ADAM_STEP)
    delta = -ADAM_LR * (m_hat / (jnp.sqrt(v_hat) + ADAM_EPS) + ADAM_WD * w)
    return delta, m, v


def _adamw_tiles(r, c_):
    tr = _tile(r, 256, 16)
    return (tr, c_) if tr < r or r <= 256 else (r, _tile(c_, 256))


def _sum_parts(part):
    g = part[0].astype(F32)
    for k in range(1, part.shape[0]):
        g = g + part[k].astype(F32)
    return g


def _sum_adamw(name, parts, w, m, v):
    r, c_ = w.shape
    tr, tc = _adamw_tiles(r, c_)

    def body(p_ref, w_ref, m_ref, v_ref, g_ref, d_ref, nm_ref, nv_ref):
        g = _sum_parts(p_ref)
        g_ref[...] = g
        d_ref[...], nm_ref[...], nv_ref[...] = _adamw(w_ref[...], g, m_ref[...], v_ref[...])

    tile = pl.BlockSpec((tr, tc), lambda i, j: (i, j))
    return pl.pallas_call(body, name=name, grid=(r // tr, c_ // tc),
                          in_specs=[pl.BlockSpec((parts.shape[0], tr, tc), lambda i, j: (0, i, j)), tile, tile, tile],
                          out_specs=[tile] * 4, out_shape=[jax.ShapeDtypeStruct((r, c_), F32)] * 4,
                          compiler_params=_params("parallel", "parallel"))(parts, w, m, v)


def _sum_adamw_layers(name, parts, w, m, v):
    n_layers, r, c_ = w.shape
    tr = _tile(r, 256, 16)

    def body(*refs):
        p_refs = refs[:n_layers]
        w_ref, m_ref, v_ref, g_ref, d_ref, nm_ref, nv_ref = refs[n_layers:]
        layer = pl.program_id(0)
        g = _sum_parts(p_refs[0])
        for li in range(1, n_layers):
            g = jnp.where(layer == li, _sum_parts(p_refs[li]), g)
        g_ref[...] = g
        d_ref[...], nm_ref[...], nv_ref[...] = _adamw(w_ref[...], g, m_ref[...], v_ref[...])

    row = pl.BlockSpec((None, tr, c_), lambda l, i: (l, i, 0))
    specs = [pl.BlockSpec((p.shape[0], tr, c_), lambda l, i: (0, i, 0)) for p in parts]
    return pl.pallas_call(body, name=name, grid=(n_layers, r // tr), in_specs=specs + [row, row, row],
                          out_specs=[row] * 4, out_shape=[jax.ShapeDtypeStruct(w.shape, F32)] * 4,
                          compiler_params=_params("parallel", "parallel"))(*parts, w, m, v)


def _pack_rows(flat, n_rows, cols):
    pad = n_rows * cols - flat.shape[-1]
    flat = jnp.pad(flat, [(0, 0)] * (flat.ndim - 1) + [(0, pad)])
    return flat.reshape(flat.shape[:-1] + (n_rows, cols))


def _cols_split(full):
    c = full.shape[1] // N_DEV
    return jnp.stack([full[:, d * c:(d + 1) * c] for d in range(N_DEV)])


def _rows_join(blocks):
    return blocks.reshape(N_DEV * blocks.shape[1], blocks.shape[2])


def _rows_split(full):
    return full.reshape(N_DEV, full.shape[0] // N_DEV, full.shape[1])


def _heads_col(v, ng):
    return jnp.pad(v.reshape(ng, 1, SSD_HPG), ((0, 0), (0, 0), (0, LANES - SSD_HPG)))


MATRIX_ITEMS = ("w_in", "w_out", "up0", "down0", "w_qkv", "w_o", "up1", "down1")
VECTOR_ITEMS = ("conv_w", "b_qkv", "b_o")
ITEMS = MATRIX_ITEMS + VECTOR_ITEMS
GATHER_STAGES = (("w_in", "conv_w"), ("w_out", "up0", "down0"), ("w_qkv", "b_qkv", "w_o", "b_o", "up1", "down1"))
SIDE_BY_SIDE = ("conv_w", "up0", "up1", "b_o")


def _items(tree, prefix=""):
    g = lambda k: tree[prefix + k]
    return {"w_in": g("ssd_w_in")[0].T, "w_out": g("ssd_w_out")[0], "w_qkv": g("attn_w_qkv")[0].T,
            "w_o": g("attn_w_o")[0], "up0": g("mlp_w_up")[0], "up1": g("mlp_w_up")[1],
            "down0": g("mlp_w_down")[0], "down1": g("mlp_w_down")[1], "conv_w": g("ssd_conv_w")[0],
            "b_qkv": g("attn_b_qkv"), "b_o": g("attn_b_o")}


REPLICATED = ("ssd_conv_b", "ssd_dt_bias", "ssd_a_log", "ssd_d", "ssd_norm_w", "attn_sinks", "mix_pre_norm",
              "mix_post_norm", "ffn_pre_norm", "ffn_post_norm")
WEIGHTS = ("ssd_w_in", "ssd_conv_w", "ssd_conv_b", "ssd_dt_bias", "ssd_a_log", "ssd_d", "ssd_norm_w", "ssd_w_out",
           "attn_w_qkv", "attn_b_qkv", "attn_sinks", "attn_w_o", "attn_b_o", "mlp_w_up", "mlp_w_down",
           "mix_pre_norm", "mix_post_norm", "ffn_pre_norm", "ffn_post_norm")


def _forward_backward(x, target, rep, token, weights_of_stage, reduce_grads):
    t, d = x.shape
    ng = rep["ssd_norm_w"].shape[1] // GW
    di = ng * GW
    n_xbc = ng * GC
    nh = ng * SSD_HPG
    grads, blocks = {}, {}
    w_up, w_down = [None, None], [None, None]
    sinks_rep = jnp.repeat(rep["attn_sinks"].reshape(ATTN_N_KV, ATTN_REP, 1), ATTN_WINDOW, axis=2).reshape(
        ATTN_N_KV, 1, ATTN_REP * ATTN_WINDOW)
    conv_b = rep["ssd_conv_b"]
    gn = ng * SSD_D_STATE
    parts = ((0, di), (di, di), (2 * di, gn), (2 * di + gn, gn), (di + n_xbc, nh))
    alog_c, dsk_c = (_heads_col(rep[k], ng) for k in ("ssd_a_log", "ssd_d"))
    bias_l, alog_l = (jnp.pad(rep[k], ((0, 0), (0, LANES - nh))) for k in ("ssd_dt_bias", "ssd_a_log"))
    norm = {k: rep[k] for k in ("mix_pre_norm", "mix_post_norm", "ffn_pre_norm", "ffn_post_norm")}

    def nrow(name, i):
        return norm[name][i:i + 1]

    def mlp_fwd(i, u2):
        p = _mm(f"mlp{i}_up", [u2], [w_up[i]], "nn", tm=1024, tn=1024, out_dtypes=(BF16,),
                epilogue=lambda acc: (jnp.square(jnp.maximum(acc, 0.0)),))
        f = _mm(f"mlp{i}_down", [p], [w_down[i]], "nn", tm=512, tn=1024)
        return p, f

    def mlp_bwd(i, df, u2, p):
        da = _mm(f"mlp{i}_dact", [df], [w_down[i]], "nt", tm=1024, tn=1024, out_dtypes=(BF16,),
                 tiles=(p,), epilogue=lambda acc, pv: (acc * (2.0 * jnp.sqrt(pv.astype(F32))),))
        blocks[f"down{i}"] = _rows_split(_mm(f"mlp{i}_dwdown", [p], [df], "tn", tm=512, tn=1024,
                                             out_dtypes=(PAYLOAD,)))
        blocks[f"up{i}"] = _mm(f"mlp{i}_dwup", [u2], [da], "tn", tm=1024, tn=da.shape[1] // N_DEV,
                               out_dtypes=(PAYLOAD,), col_blocks=True)
        return _mm(f"mlp{i}_dx", [da], [w_up[i]], "nt", tm=512, tn=1024)

    u0 = _prenorm("l0_prenorm", x, nrow("mix_pre_norm", 0), token)
    got = weights_of_stage(0, u0)
    w_in_t = _rows_join(got["w_in"])
    w_dt_t = jnp.pad(w_in_t[di + n_xbc:], ((0, LANES - nh), (0, 0)))
    conv_w = got["conv_w"]
    zx = _mm("ssd_in_proj", [u0], [w_in_t], "nt", tm=1024, tn=1024, n_use=di + n_xbc)
    zdt = _mm("ssd_dt_proj", [u0], [w_dt_t], "nt", tm=1024, tn=LANES)
    pre = _conv_fwd(zx, di, n_xbc, conv_w, conv_b)
    dt_c, cum_c, cum_r, sgd_c = _ssd_dt_prep(zdt, bias_l, alog_l, ng)
    y, states = _ssd_fwd(pre, dt_c, cum_c, cum_r, alog_c, dsk_c)
    yn = _gate_norm_fwd(y, zx, rep["ssd_norm_w"])
    got = weights_of_stage(1, yn)
    w_out = _rows_join(got["w_out"])
    w_up[0], w_down[0] = got["up0"], _rows_join(got["down0"])
    mix0 = _mm("ssd_out_proj", [yn], [w_out], "nn", tm=1024, tn=1024)
    h1, u0f = _post_pre("l0_mid", x, mix0, nrow("mix_post_norm", 0), nrow("ffn_pre_norm", 0))
    p0, f0 = mlp_fwd(0, u0f)
    h2, u1 = _post_pre("l1_in", h1, f0, nrow("ffn_post_norm", 0), nrow("mix_pre_norm", 1))
    got = weights_of_stage(2, u1)
    w_qkv_t = _rows_join(got["w_qkv"])
    w_o = _rows_join(got["w_o"])
    b_qkv_col = got["b_qkv"].reshape(-1, 1)
    b_o = got["b_o"]
    w_up[1], w_down[1] = got["up1"], _rows_join(got["down1"])
    qkv_t = _mm("attn_qkv_proj", [w_qkv_t], [u1], "nt", tm=768, tn=1024, out_dtypes=(BF16,), cols=(b_qkv_col,),
                epilogue=lambda acc, b: (acc + b,))
    ao_t = _attn_fwd_t(qkv_t, sinks_rep)
    mix1 = _mm("attn_out_proj", [ao_t], [w_o], "tn", tm=1024, tn=1024, rows=(b_o,),
               epilogue=lambda acc, b: (acc + b,))
    h3, u1f = _post_pre("l1_mid", h2, mix1, nrow("mix_post_norm", 1), nrow("ffn_pre_norm", 1))
    p1, f1 = mlp_fwd(1, u1f)
    dh, loss_row = _final_loss("loss", h3, f1, nrow("ffn_post_norm", 1), target)

    g_norm = {k: [None, None] for k in norm}
    df1, g_norm["ffn_post_norm"][1], _ = _norm_bwd("l1_ffn_post_bwd", dh, post=(f1, nrow("ffn_post_norm", 1)))
    du = mlp_bwd(1, df1, u1f, p1)
    sent = reduce_grads("mlp1", {k: blocks[k] for k in ("up1", "down1")})
    dh, g_norm["ffn_pre_norm"][1], dmix1, g_norm["mix_post_norm"][1], db_o = _norm_bwd(
        "l1_mid_bwd", dh, pre=(du, h3, nrow("ffn_pre_norm", 1)), post=(mix1, nrow("mix_post_norm", 1)), after=sent)
    blocks["b_o"] = _cols_split(db_o)
    blocks["w_o"] = _rows_split(_mm("attn_dwo", [ao_t], [dmix1], "nn", tm=512, tn=1024, out_dtypes=(PAYLOAD,)))
    dao_t = _mm("attn_dout", [w_o], [dmix1], "nt", tm=1024, tn=1024, out_dtypes=(BF16,))
    dqkv_t, db_qkv, grads["attn_sinks"] = _attn_bwd_t(qkv_t, dao_t, sinks_rep)
    blocks["b_qkv"] = db_qkv.reshape(N_DEV, 1, -1)
    blocks["w_qkv"] = _rows_split(_mm("attn_dwqkv", [dqkv_t], [u1], "nn", tm=512, tn=1024, out_dtypes=(PAYLOAD,)))
    du = _mm("attn_dx", [dqkv_t], [w_qkv_t], "tn", tm=1024, tn=1024)
    sent = reduce_grads("attn", {k: blocks[k] for k in ("w_o", "w_qkv", "b_o", "b_qkv")})
    dh, g_norm["mix_pre_norm"][1], df0, g_norm["ffn_post_norm"][0], _ = _norm_bwd(
        "l1_in_bwd", dh, pre=(du, h2, nrow("mix_pre_norm", 1)), post=(f0, nrow("ffn_post_norm", 0)), after=sent)
    du = mlp_bwd(0, df0, u0f, p0)
    sent = reduce_grads("mlp0", {k: blocks[k] for k in ("up0", "down0")})
    dh, g_norm["ffn_pre_norm"][0], dmix0, g_norm["mix_post_norm"][0], _ = _norm_bwd(
        "l0_mid_bwd", dh, pre=(du, h1, nrow("ffn_pre_norm", 0)), post=(mix0, nrow("mix_post_norm", 0)), after=sent)
    blocks["w_out"] = _rows_split(_mm("ssd_dwout", [yn], [dmix0], "tn", tm=512, tn=1024, out_dtypes=(PAYLOAD,)))
    dyn = _mm("ssd_dyn", [dmix0], [w_out], "nt", tm=1024, tn=1024)
    sent = reduce_grads("ssdout", {"w_out": blocks["w_out"]})
    dy, dz, grads["ssd_norm_w"] = _gate_norm_bwd(dyn, y, zx, rep["ssd_norm_w"], sent)
    dpx, dpb, dpc, ddt_g, dbias_g, dalog_g, dd_g = _ssd_bwd(dy, pre, states, dt_c, cum_c, cum_r, sgd_c, alog_c,
                                                             dsk_c)
    conv_out = [_conv_bwd(f"ssd_conv_bwd_{tag}", dp, zx, c0, conv_w[:, c0 - di:c0 - di + n])
                for tag, dp, (c0, n) in zip("xbc", (dpx, dpb, dpc), parts[1:4])]
    dconv_w = jnp.concatenate([o[1] for o in conv_out], axis=1)
    dconv_b = jnp.concatenate([o[2] for o in conv_out], axis=1)
    ddt = jnp.transpose(ddt_g[:, :, :SSD_HPG], (1, 0, 2)).reshape(t, nh)
    ddt = jnp.pad(ddt, ((0, 0), (0, LANES - nh))).astype(BF16)
    blocks["conv_w"] = _cols_split(dconv_w)
    grads["ssd_conv_b"] = dconv_b
    for name, val in (("ssd_dt_bias", dbias_g), ("ssd_a_log", dalog_g), ("ssd_d", dd_g)):
        grads[name] = val[:, 0, :SSD_HPG].reshape(1, nh)
    d_zx = [dz] + [o[0] for o in conv_out] + [ddt]
    dw_parts = [_mm(f"ssd_dw_{tag}", [d], [u0], "tn", tm=512, tn=1024, out_dtypes=(PAYLOAD,))
                for tag, d in zip("zxbct", d_zx)]
    dw_parts[-1] = dw_parts[-1][:nh]
    blocks["w_in"] = _rows_split(jnp.concatenate(dw_parts, axis=0))
    sent = reduce_grads("ssd", {k: blocks[k] for k in ("w_in", "conv_w")})
    w_parts = [w_in_t[r0:r0 + n] for r0, n in parts[:-1]] + [w_dt_t]
    du = _mm("ssd_dx", d_zx, w_parts, "nn", tm=256, tn=1024, after=sent)
    grad_x, g_norm["mix_pre_norm"][0] = _norm_bwd("l0_in_bwd", dh, pre=(du, x, nrow("mix_pre_norm", 0)), after=sent)
    for k in norm:
        grads[k] = jnp.concatenate(g_norm[k], axis=0)
    return loss_row, grad_x, grads


def kernel(x, ssd_w_in, ssd_conv_w, ssd_conv_b, ssd_dt_bias, ssd_a_log, ssd_d, ssd_norm_w, ssd_w_out, attn_w_qkv, attn_b_qkv, attn_sinks, attn_w_o, attn_b_o, mlp_w_up, mlp_w_down, mix_pre_norm, mix_post_norm, ffn_pre_norm, ffn_post_norm, loss_target, m_ssd_w_in, m_ssd_conv_w, m_ssd_conv_b, m_ssd_dt_bias, m_ssd_a_log, m_ssd_d, m_ssd_norm_w, m_ssd_w_out, m_attn_w_qkv, m_attn_b_qkv, m_attn_sinks, m_attn_w_o, m_attn_b_o, m_mlp_w_up, m_mlp_w_down, m_mix_pre_norm, m_mix_post_norm, m_ffn_pre_norm, m_ffn_post_norm, v_ssd_w_in, v_ssd_conv_w, v_ssd_conv_b, v_ssd_dt_bias, v_ssd_a_log, v_ssd_d, v_ssd_norm_w, v_ssd_w_out, v_attn_w_qkv, v_attn_b_qkv, v_attn_sinks, v_attn_w_o, v_attn_b_o, v_mlp_w_up, v_mlp_w_down, v_mix_pre_norm, v_mix_post_norm, v_ffn_pre_norm, v_ffn_post_norm):
    given = dict(locals())
    w = {k: given[k] for k in WEIGHTS}
    mom_m = {k: given["m_" + k] for k in WEIGHTS}
    mom_v = {k: given["v_" + k] for k in WEIGHTS}
    w_it, m_it, v_it = _items(given), _items(given, "m_"), _items(given, "v_")

    order = [k for stage in GATHER_STAGES for k in stage]
    shards = [w_it[k].astype(PAYLOAD) if k in MATRIX_ITEMS else w_it[k] for k in order]
    wide = [k in SIDE_BY_SIDE for k in order]
    g_send, g_recv, shards, lands, token = _gather_start("gather_start", shards, wide)

    def weights_of_stage(s, after):
        first = sum(len(stage) for stage in GATHER_STAGES[:s])
        sl = slice(first, first + len(GATHER_STAGES[s]))
        _, got = _gather_wait(f"gather_wait{s}", g_send, g_recv, first, len(order), shards[sl], lands[sl], wide[sl],
                              after)
        return dict(zip(GATHER_STAGES[s], got))

    in_flight = []

    def reduce_grads(tag, blocks):
        keys = list(blocks)
        started = _scatter_start(f"rs_start_{tag}", [blocks[k] for k in keys])
        in_flight.append((tag, keys, started))
        return started[-1]

    rep = {k: w[k] for k in REPLICATED}
    loss_row, grad_x, grads = _forward_backward(x[0], loss_target[0], rep, token, weights_of_stage, reduce_grads)

    def pack_rep(tree, last):
        flat = jnp.concatenate([tree[k].reshape(-1) for k in REPLICATED] + [last])
        return _pack_rows(flat, _round_up(-(-flat.shape[0] // LANES), 8), LANES)

    landed = {}

    def wait_group(group, after):
        tag, keys, (s_send, s_recv, srcs, s_lands, _) = group
        _, got = _scatter_wait(f"rs_wait_{tag}", s_send, s_recv, srcs, s_lands, after)
        landed.update(zip(keys, got))

    def adamw_item(k):
        return _sum_adamw(f"adamw_{k}", landed[k], w_it[k], m_it[k], v_it[k])

    def adamw_stack(name, keys):
        return _sum_adamw_layers(f"adamw_{name}", [landed[k] for k in keys], given[name], given["m_" + name],
                                 given["v_" + name])

    for group in in_flight[:-1]:
        wait_group(group, grad_x)
    done = {"mlp_w_up": adamw_stack("mlp_w_up", ("up0", "up1")),
            "mlp_w_down": adamw_stack("mlp_w_down", ("down0", "down1")),
            "attn_w_qkv": [o.T[None] for o in adamw_item("w_qkv")],
            "attn_w_o": [o[None] for o in adamw_item("w_o")],
            "attn_b_qkv": adamw_item("b_qkv"), "attn_b_o": adamw_item("b_o"),
            "ssd_w_out": [o[None] for o in adamw_item("w_out")]}
    partials, = _all_gather("gather_small_grads", [pack_rep(grads, loss_row[0, :1])],
                            [outs4[0] for outs4 in done.values()])
    wait_group(in_flight[-1], partials)
    done["ssd_w_in"] = [o.T[None] for o in adamw_item("w_in")]
    done["ssd_conv_w"] = [o[None] for o in adamw_item("conv_w")]
    zero = jnp.zeros((1,), F32)
    rep_out = _sum_adamw("adamw_replicated", partials, pack_rep(w, zero), pack_rep(mom_m, zero), pack_rep(mom_v, zero))

    kinds = []
    for kind, r_arr in enumerate(rep_out):
        tree = {name: outs4[kind] for name, outs4 in done.items()}
        flat, off = r_arr.reshape(-1), 0
        for k in REPLICATED:
            tree[k] = flat[off:off + w[k].size].reshape(w[k].shape)
            off += w[k].size
        kinds.append(tree)
    loss = rep_out[0].reshape(-1)[off]
    outs = [loss, grad_x[None]]
    for tree in kinds:
        outs += [tree[k] for k in WEIGHTS]
    return tuple(outs)
```

```python
import jax
import jax.numpy as jnp
from jax import lax
from jax.experimental import pallas as pl
from jax.experimental.pallas import tpu as pltpu

F32 = jnp.float32
BF16 = jnp.bfloat16
PAYLOAD = jnp.bfloat16
HIGHEST = lax.Precision.HIGHEST
MESH = pl.DeviceIdType.MESH

NORM_EPS = 1e-6
SSD_HEAD_DIM = 64
SSD_HPG = 4
SSD_D_STATE = 128
SSD_CONV_WIDTH = 4
SSD_CHUNK = 128
ATTN_HEAD_DIM = 64
ATTN_N_KV = 4
ATTN_REP = 4
ATTN_WINDOW = 128
ADAM_LR = 0.001
ADAM_B1 = 0.9
ADAM_B2 = 0.999
ADAM_EPS = 1e-08
ADAM_WD = 0.01
ADAM_STEP = 10

N_DEV = 8
LANES = 128
V7X_VMEM_LIMIT = 56 * 1024 * 1024

GW = SSD_HPG * SSD_HEAD_DIM
GC = GW + 2 * SSD_D_STATE
assert SSD_CHUNK == LANES


def _params(*sem):
    return pltpu.CompilerParams(dimension_semantics=sem, vmem_limit_bytes=V7X_VMEM_LIMIT)


def _tile(n, pref, mult=LANES):
    best = None
    t = mult
    while t <= min(n, pref):
        if n % t == 0:
            best = t
        t += mult
    return best if best is not None else n


def _round_up(n, m):
    return (n + m - 1) // m * m


def _acc(ref, val, first):
    @pl.when(first)
    def _():
        ref[...] = val

    @pl.when(jnp.logical_not(first))
    def _():
        ref[...] += val


def _dot(a, b):
    return lax.dot_general(a, b, (((1,), (0,)), ((), ())), preferred_element_type=F32)


def _dot_nt(a, b):
    return lax.dot_general(a, b, (((1,), (1,)), ((), ())), preferred_element_type=F32)


def _dot_tn(a, b):
    return lax.dot_general(a, b, (((0,), (0,)), ((), ())), preferred_element_type=F32)


def _dot_f32(a, b):
    return lax.dot_general(a, b, (((1,), (0,)), ((), ())), preferred_element_type=F32, precision=HIGHEST)


_DOTS = {"nn": _dot, "nt": _dot_nt, "tn": _dot_tn}


def _sigmoid(x):
    return 1.0 / (1.0 + jnp.exp(-x))


def _softplus(x):
    return jnp.maximum(x, 0.0) + jnp.log1p(jnp.exp(-jnp.abs(x)))


def _silu_grad(x, s):
    return s * (1.0 + x * (1.0 - s))


def _mm(name, a_list, b_list, mode, *, tm, tn, out_dtypes=(F32,), epilogue=None, tiles=(), rows=(), cols=(),
        col_blocks=False, n_use=None, after=None):
    npair = len(a_list)
    if mode == "tn":
        m = a_list[0].shape[1]
    else:
        m = a_list[0].shape[0]
    n = n_use if n_use is not None else (b_list[0].shape[0] if mode == "nt" else b_list[0].shape[1])
    tm = _tile(m, tm, LANES if mode == "tn" else 8)
    tn = _tile(n, tn)
    assert m % tm == 0 and n % tn == 0, (name, m, n, tm, tn)
    dot = _DOTS[mode]

    def body(*refs):
        a_refs = refs[:npair]
        b_refs = refs[npair:2 * npair]
        n_extra = len(tiles) + len(rows) + len(cols)
        e_refs = refs[2 * npair:2 * npair + n_extra]
        o_refs = refs[2 * npair + n_extra + len(order):]
        acc = None
        for ar, br in zip(a_refs, b_refs):
            d = dot(ar[...], br[...])
            acc = d if acc is None else acc + d
        outs = epilogue(acc, *[e[...] for e in e_refs]) if epilogue is not None else (acc,)
        for o, v in zip(o_refs, outs):
            o[...] = v.astype(o.dtype)

    in_specs = []
    for a in a_list:
        if mode == "tn":
            in_specs.append(pl.BlockSpec((a.shape[0], tm), lambda i, j: (0, i)))
        else:
            in_specs.append(pl.BlockSpec((tm, a.shape[1]), lambda i, j: (i, 0)))
    for b in b_list:
        if mode == "nt":
            in_specs.append(pl.BlockSpec((tn, b.shape[1]), lambda i, j: (j, 0)))
        else:
            in_specs.append(pl.BlockSpec((b.shape[0], tn), lambda i, j: (0, j)))
    in_specs += [pl.BlockSpec((tm, tn), lambda i, j: (i, j)) for _ in tiles]
    in_specs += [pl.BlockSpec((1, tn), lambda i, j: (0, j)) for _ in rows]
    in_specs += [pl.BlockSpec((tm, 1), lambda i, j: (i, 0)) for _ in cols]
    order = [] if after is None else [after]
    in_specs += [pl.BlockSpec((8, LANES), lambda i, j: (0, 0)) for _ in order]
    outs = pl.pallas_call(
        body,
        name=name,
        grid=(m // tm, n // tn),
        in_specs=in_specs,
        out_specs=[pl.BlockSpec((None, tm, tn), lambda i, j: (j, i, 0)) if col_blocks else
                   pl.BlockSpec((tm, tn), lambda i, j: (i, j)) for _ in out_dtypes],
        out_shape=[jax.ShapeDtypeStruct((n // tn, m, tn) if col_blocks else (m, n), dt) for dt in out_dtypes],
        compiler_params=_params("parallel", "parallel"),
    )(*a_list, *b_list, *tiles, *rows, *cols, *order)
    return outs[0] if len(out_dtypes) == 1 else outs


def _rms(x, w):
    r = lax.rsqrt(jnp.mean(x * x, axis=-1, keepdims=True) + NORM_EPS)
    return x * r * w


def _rms_bwd(x, w, dy):
    r = lax.rsqrt(jnp.mean(x * x, axis=-1, keepdims=True) + NORM_EPS)
    xh = x * r
    g = dy * w
    dx = r * (g - xh * jnp.mean(g * xh, axis=-1, keepdims=True))
    return dx, dy * xh


def _row_specs(tr, d):
    return pl.BlockSpec((tr, d), lambda i: (i, 0)), pl.BlockSpec((1, d), lambda i: (0, 0))


def _prenorm(name, h, w, after):
    t, d = h.shape
    tr = _tile(t, 512, 8)
    row, vec = _row_specs(tr, d)

    def body(h_ref, w_ref, after_ref, u_ref):
        u_ref[...] = _rms(h_ref[...], w_ref[...]).astype(BF16)

    return pl.pallas_call(body, name=name, grid=(t // tr,),
                          in_specs=[row, vec, pl.BlockSpec((8, LANES), lambda i: (0, 0))], out_specs=row,
                          out_shape=jax.ShapeDtypeStruct((t, d), BF16), compiler_params=_params("parallel"))(
                              h, w, after)


def _post_pre(name, h, m, w_post, w_pre):
    t, d = h.shape
    tr = _tile(t, 512, 8)
    row, vec = _row_specs(tr, d)

    def body(h_ref, m_ref, wq_ref, wp_ref, hn_ref, u_ref):
        hn = h_ref[...] + _rms(m_ref[...], wq_ref[...])
        hn_ref[...] = hn
        u_ref[...] = _rms(hn, wp_ref[...]).astype(BF16)

    return pl.pallas_call(body, name=name, grid=(t // tr,), in_specs=[row, row, vec, vec], out_specs=[row, row],
                          out_shape=[jax.ShapeDtypeStruct((t, d), F32), jax.ShapeDtypeStruct((t, d), BF16)],
                          compiler_params=_params("parallel"))(h, m, w_post, w_pre)


def _final_loss(name, h, m, w_post, target):
    t, d = h.shape
    tr = _tile(t, 512, 8)
    row, vec = _row_specs(tr, d)

    def body(h_ref, m_ref, wq_ref, t_ref, dh_ref, loss_ref):
        err = h_ref[...] + _rms(m_ref[...], wq_ref[...]) - t_ref[...]
        dh_ref[...] = err * (1.0 / d)
        part = 0.5 * jnp.sum(jnp.mean(err * err, axis=-1, keepdims=True), axis=0, keepdims=True)
        _acc(loss_ref, jnp.broadcast_to(part, (1, LANES)), pl.program_id(0) == 0)

    return pl.pallas_call(body, name=name, grid=(t // tr,), in_specs=[row, row, vec, row],
                          out_specs=[row, pl.BlockSpec((1, LANES), lambda i: (0, 0))],
                          out_shape=[jax.ShapeDtypeStruct((t, d), F32), jax.ShapeDtypeStruct((1, LANES), F32)],
                          compiler_params=_params("arbitrary"))(h, m, w_post, target)


def _norm_bwd(name, dh, pre=None, post=None, after=None):
    t, d = dh.shape
    tr = _tile(t, 512, 8)
    row, vec = _row_specs(tr, d)
    has_pre, has_post = pre is not None, post is not None

    def body(*refs):
        it = iter(refs)
        dh_ref = next(it)
        if has_pre:
            du_ref, x_ref, wp_ref = next(it), next(it), next(it)
        if has_post:
            m_ref, wq_ref = next(it), next(it)
        if after is not None:
            next(it)
        first = pl.program_id(0) == 0
        dh_v = dh_ref[...]
        if has_pre:
            dhn_ref, dwp_ref = next(it), next(it)
            dx, dwr = _rms_bwd(x_ref[...], wp_ref[...], du_ref[...])
            dh_v = dh_v + dx
            dhn_ref[...] = dh_v
            _acc(dwp_ref, jnp.sum(dwr, axis=0, keepdims=True), first)
        if has_post:
            dm_ref, dwq_ref, dms_ref = next(it), next(it), next(it)
            dm, dwr = _rms_bwd(m_ref[...], wq_ref[...], dh_v)
            dm_ref[...] = dm.astype(BF16)
            _acc(dwq_ref, jnp.sum(dwr, axis=0, keepdims=True), first)
            _acc(dms_ref, jnp.sum(dm, axis=0, keepdims=True), first)

    ins, in_specs, out_specs, out_shape = [dh], [row], [], []
    if has_pre:
        ins += list(pre)
        in_specs += [row, row, vec]
        out_specs += [row, vec]
        out_shape += [jax.ShapeDtypeStruct((t, d), F32), jax.ShapeDtypeStruct((1, d), F32)]
    if has_post:
        ins += list(post)
        in_specs += [row, vec]
        out_specs += [row, vec, vec]
        out_shape += [jax.ShapeDtypeStruct((t, d), BF16), jax.ShapeDtypeStruct((1, d), F32),
                      jax.ShapeDtypeStruct((1, d), F32)]
    if after is not None:
        ins.append(after)
        in_specs.append(pl.BlockSpec((8, LANES), lambda i: (0, 0)))
    return pl.pallas_call(body, name=name, grid=(t // tr,), in_specs=in_specs, out_specs=out_specs,
                          out_shape=out_shape, compiler_params=_params("arbitrary"))(*ins)


HALO = 8


def _shift_later(cur, prev, s):
    rolled = pltpu.roll(cur, s, 0)
    row = lax.broadcasted_iota(jnp.int32, prev.shape, 0)
    first = jnp.where(row < s, pltpu.roll(prev, s, 0), rolled[0:HALO])
    return jnp.concatenate([first, rolled[HALO:]], axis=0)


def _shift_earlier(cur, nxt, s):
    tt = cur.shape[0]
    rolled = pltpu.roll(cur, tt - s, 0)
    row = lax.broadcasted_iota(jnp.int32, nxt.shape, 0)
    last = jnp.where(row >= HALO - s, pltpu.roll(nxt, HALO - s, 0), rolled[tt - HALO:])
    return jnp.concatenate([rolled[:tt - HALO], last], axis=0)


def _conv_fwd(zx, col0, n_ch, conv_w, conv_b):
    t = zx.shape[0]
    tc = _tile(n_ch, 512)
    tt = _tile(t, 1024, 8)
    cb0 = col0 // tc
    assert col0 % tc == 0
    kw = SSD_CONV_WIDTH

    def body(x_ref, p_ref, w_ref, b_ref, o_ref):
        cur = x_ref[...]
        prev = jnp.where(pl.program_id(1) > 0, p_ref[...], 0.0)
        w = w_ref[...]
        acc = b_ref[...] + w[kw - 1:kw, :] * cur
        for k in range(kw - 1):
            acc = acc + w[k:k + 1, :] * _shift_later(cur, prev, kw - 1 - k)
        o_ref[...] = acc

    return pl.pallas_call(
        body, name="ssd_conv_fwd", grid=(n_ch // tc, t // tt),
        in_specs=[pl.BlockSpec((tt, tc), lambda j, i: (i, cb0 + j)),
                  pl.BlockSpec((HALO, tc), lambda j, i: (jnp.maximum(i * (tt // HALO) - 1, 0), cb0 + j)),
                  pl.BlockSpec((kw, tc), lambda j, i: (0, j)),
                  pl.BlockSpec((1, tc), lambda j, i: (0, j))],
        out_specs=pl.BlockSpec((tt, tc), lambda j, i: (i, j)),
        out_shape=jax.ShapeDtypeStruct((t, n_ch), F32),
        compiler_params=_params("parallel", "parallel"))(zx, zx, conv_w, conv_b)


def _conv_bwd(name, dpre, zx, col0, conv_w):
    t, n_ch = dpre.shape
    tc = _tile(n_ch, 512)
    tt = _tile(t, 1024, 8)
    cb0 = col0 // tc
    kw = SSD_CONV_WIDTH
    nt = t // tt

    def body(d_ref, dn_ref, x_ref, p_ref, w_ref, dx_ref, dw_ref, db_ref):
        i = pl.program_id(1)
        d = d_ref[...]
        d_next = jnp.where(i < nt - 1, dn_ref[...], 0.0)
        x = x_ref[...]
        x_prev = jnp.where(i > 0, p_ref[...], 0.0)
        w = w_ref[...]
        dx = w[kw - 1:kw, :] * d
        for k in range(kw - 1):
            dx = dx + w[k:k + 1, :] * _shift_earlier(d, d_next, kw - 1 - k)
        dx_ref[...] = dx.astype(BF16)
        first = i == 0
        for k in range(kw):
            xs = x if k == kw - 1 else _shift_later(x, x_prev, kw - 1 - k)
            val = jnp.sum(d * xs, axis=0, keepdims=True)

            @pl.when(first)
            def _():
                dw_ref[k:k + 1, :] = val

            @pl.when(jnp.logical_not(first))
            def _():
                dw_ref[k:k + 1, :] += val
        _acc(db_ref, jnp.sum(d, axis=0, keepdims=True), first)

    return pl.pallas_call(
        body, name=name, grid=(n_ch // tc, nt),
        in_specs=[pl.BlockSpec((tt, tc), lambda j, i: (i, j)),
                  pl.BlockSpec((HALO, tc), lambda j, i: (jnp.minimum((i + 1) * (tt // HALO), t // HALO - 1), j)),
                  pl.BlockSpec((tt, tc), lambda j, i: (i, cb0 + j)),
                  pl.BlockSpec((HALO, tc), lambda j, i: (jnp.maximum(i * (tt // HALO) - 1, 0), cb0 + j)),
                  pl.BlockSpec((kw, tc), lambda j, i: (0, j))],
        out_specs=[pl.BlockSpec((tt, tc), lambda j, i: (i, j)),
                   pl.BlockSpec((kw, tc), lambda j, i: (0, j)),
                   pl.BlockSpec((1, tc), lambda j, i: (0, j))],
        out_shape=[jax.ShapeDtypeStruct((t, n_ch), BF16), jax.ShapeDtypeStruct((kw, n_ch), F32),
                   jax.ShapeDtypeStruct((1, n_ch), F32)],
        compiler_params=_params("parallel", "arbitrary"))(dpre, dpre, zx, zx, conv_w)


def _head_of_lane(shape, width):
    return lax.broadcasted_iota(jnp.int32, shape, len(shape) - 1) // width


def _select_dot(v, pick):
    hi = v.astype(BF16)
    lo = (v - hi.astype(F32)).astype(BF16)
    return _dot(hi, pick) + _dot(lo, pick)


def _expand(v, n_rows, on_mxu=False):
    if not on_mxu:
        head = _head_of_lane((n_rows, GW), SSD_HEAD_DIM)
        out = jnp.zeros((n_rows, GW), F32)
        for j in range(SSD_HPG):
            out = jnp.where(head == j, v[:, j:j + 1], out)
        return out
    src = lax.broadcasted_iota(jnp.int32, (LANES, GW), 0)
    return _select_dot(v, (src == _head_of_lane((LANES, GW), SSD_HEAD_DIM)).astype(BF16))


def _contract(v, n_rows, on_mxu=False):
    if not on_mxu:
        head = _head_of_lane((n_rows, GW), SSD_HEAD_DIM)
        lane = lax.broadcasted_iota(jnp.int32, (n_rows, LANES), 1)
        out = jnp.zeros((n_rows, LANES), F32)
        for j in range(SSD_HPG):
            s = jnp.sum(jnp.where(head == j, v, 0.0), axis=1, keepdims=True)
            out = jnp.where(lane == j, s, out)
        return out
    dst = lax.broadcasted_iota(jnp.int32, (GW, LANES), 1)
    return _select_dot(v, (lax.broadcasted_iota(jnp.int32, (GW, LANES), 0) // SSD_HEAD_DIM == dst).astype(BF16))


def _ssd_dt_prep(zdt, bias, alog, ng):
    t = zdt.shape[0]
    q = SSD_CHUNK

    def body(z_ref, b_ref, a_ref, dt_ref, cum_ref, cumr_ref, sg_ref):
        raw = z_ref[...] + b_ref[...]
        dt = _softplus(raw)
        sgd = _sigmoid(raw)
        row = lax.broadcasted_iota(jnp.int32, (q, q), 0)
        col = lax.broadcasted_iota(jnp.int32, (q, q), 1)
        cum = _dot_f32((col <= row).astype(F32), dt * (-jnp.exp(a_ref[...])))
        cum_t = cum.T
        lane = lax.broadcasted_iota(jnp.int32, (q, LANES), 1)
        for g in range(ng):
            shift = (LANES - g * SSD_HPG) % LANES

            def group(v):
                return jnp.where(lane < SSD_HPG, pltpu.roll(v, shift, 1) if shift else v, 0.0)

            dt_ref[g] = group(dt)
            cum_ref[g] = group(cum)
            sg_ref[g] = group(sgd)
            cumr_ref[g] = (pltpu.roll(cum_t, shift, 0) if shift else cum_t)[0:8, :]

    cols = pl.BlockSpec((ng, q, LANES), lambda c: (0, c, 0))
    vec = pl.BlockSpec((1, LANES), lambda c: (0, 0))
    col_shape = jax.ShapeDtypeStruct((ng, t, LANES), F32)
    return pl.pallas_call(body, name="ssd_dt_prep", grid=(t // q,),
                          in_specs=[pl.BlockSpec((q, LANES), lambda c: (c, 0)), vec, vec],
                          out_specs=[cols, cols, pl.BlockSpec((ng, 8, q), lambda c: (0, 0, c)), cols],
                          out_shape=[col_shape, col_shape, jax.ShapeDtypeStruct((ng, 8, t), F32), col_shape],
                          compiler_params=_params("parallel"))(zdt, bias, alog)


def _ssd_common(pre, dt, cum, cum_r, alog_c, on_mxu):
    q = SSD_CHUNK
    sg = _sigmoid(pre)
    act = pre * sg
    xa = act[:, :GW]
    bm = act[:, GW:GW + SSD_D_STATE].astype(BF16)
    cm = act[:, GW + SSD_D_STATE:].astype(BF16)
    row = lax.broadcasted_iota(jnp.int32, (q, q), 0)
    col = lax.broadcasted_iota(jnp.int32, (q, q), 1)
    tril = col <= row
    a_c = -jnp.exp(alog_c)
    g = _dot_nt(cm, bm)
    dt_x = _expand(dt, q, on_mxu)
    xdt = xa * dt_x
    cl = cum[q - 1:q, :]
    e_c = jnp.exp(cl - cum)
    lam_c = jnp.exp(cum)
    return dict(sg=sg, xa=xa, bm=bm, cm=cm, tril=tril, row=row, col=col, dt=dt, a_c=a_c, cum=cum, cum_r=cum_r,
                g=g, dt_x=dt_x, xdt=xdt, cl=cl, e_c=e_c, lam_c=lam_c)


SSD_GPS_FWD = 8
SSD_GPS_BWD = 2


def _ssd_specs(nc, rev, ng, gps):
    q = SSD_CHUNK
    xw, nw = gps * GW, gps * SSD_D_STATE
    b_off = ng * GW // nw
    c_off = (ng * GW + ng * SSD_D_STATE) // nw
    assert ng % gps == 0 and (ng * GW) % nw == 0 and (ng * SSD_D_STATE) % nw == 0

    def ch(c):
        return nc - 1 - c if rev else c

    chunk_grp = [pl.BlockSpec((q, xw), lambda g, c: (ch(c), g)),
                 pl.BlockSpec((q, nw), lambda g, c: (ch(c), b_off + g)),
                 pl.BlockSpec((q, nw), lambda g, c: (ch(c), c_off + g))]
    col_form = pl.BlockSpec((gps, q, LANES), lambda g, c: (g, ch(c), 0))
    row_form = pl.BlockSpec((gps, 8, q), lambda g, c: (g, 0, ch(c)))
    col_par = pl.BlockSpec((gps, 1, LANES), lambda g, c: (g, 0, 0))
    y_spec = pl.BlockSpec((q, xw), lambda g, c: (ch(c), g))
    st_spec = pl.BlockSpec((gps, None, GW, SSD_D_STATE), lambda g, c: (g, ch(c), 0, 0))
    bc_spec = pl.BlockSpec((q, nw), lambda g, c: (ch(c), g))
    return chunk_grp, col_form, row_form, col_par, y_spec, st_spec, bc_spec


def _ssd_group_views(gi, wide, narrow, stacked):
    xs, ns = pl.ds(gi * GW, GW), pl.ds(gi * SSD_D_STATE, SSD_D_STATE)
    return [r.at[:, xs] for r in wide], [r.at[:, ns] for r in narrow], [r.at[gi] for r in stacked]


def _ssd_fwd(pre, dt_c, cum_c, cum_r, alog_c, dsk_c):
    t = pre.shape[0]
    ng = pre.shape[1] // GC
    q = SSD_CHUNK
    nc = t // q
    gps = SSD_GPS_FWD if ng % SSD_GPS_FWD == 0 else SSD_GPS_BWD
    chunk_grp, col_form, row_form, col_par, y_spec, st_spec, _ = _ssd_specs(nc, False, ng, gps)

    def body(px_ref, pb_ref, pc_ref, dt_ref, cum_ref, cumr_ref, ac_ref, dk_ref, y_ref, sp_ref, st_ref):
        @pl.when(pl.program_id(1) == 0)
        def _():
            st_ref[...] = jnp.zeros_like(st_ref)

        for gi in range(gps):
            (px, y), (pb, pc), rest = _ssd_group_views(
                gi, (px_ref, y_ref), (pb_ref, pc_ref), (dt_ref, cum_ref, cumr_ref, ac_ref, dk_ref, sp_ref, st_ref))
            one_group(px, pb, pc, *rest[:5], y, *rest[5:])

    def one_group(px_ref, pb_ref, pc_ref, dt_ref, cum_ref, cumr_ref, ac_ref, dk_ref, y_ref, sp_ref, st_ref):
        pre_v = jnp.concatenate([px_ref[...], pb_ref[...], pc_ref[...]], axis=1)
        v = _ssd_common(pre_v, dt_ref[...], cum_ref[...], cumr_ref[...], ac_ref[...], False)
        s0 = st_ref[...]
        sp_ref[...] = s0
        r = _dot_nt(v["cm"], s0.astype(BF16))
        y = _expand(v["lam_c"], q) * r + _expand(dk_ref[...], 1) * v["xa"]
        head = _head_of_lane((q, GW), SSD_HEAD_DIM)
        for j in range(SSD_HPG):
            diff = v["cum"][:, j:j + 1] - v["cum_r"][j:j + 1, :]
            w = (v["g"] * jnp.exp(jnp.where(v["tril"], diff, -jnp.inf))).astype(BF16)
            y = y + _dot(w, jnp.where(head == j, v["xdt"], 0.0).astype(BF16))
        y_ref[...] = y
        ds = _dot_tn((v["xdt"] * _expand(v["e_c"], q)).astype(BF16), v["bm"])
        for j in range(SSD_HPG):
            rows = slice(j * SSD_HEAD_DIM, (j + 1) * SSD_HEAD_DIM)
            st_ref[rows, :] = s0[rows, :] * jnp.exp(v["cum_r"][j:j + 1, q - 1:q]) + ds[rows, :]

    return pl.pallas_call(
        body, name="ssd_scan_fwd", grid=(ng // gps, nc),
        in_specs=chunk_grp + [col_form, col_form, row_form, col_par, col_par],
        out_specs=[y_spec, st_spec],
        out_shape=[jax.ShapeDtypeStruct((t, ng * GW), F32), jax.ShapeDtypeStruct((ng, nc, GW, SSD_D_STATE), F32)],
        scratch_shapes=[pltpu.VMEM((gps, GW, SSD_D_STATE), F32)],
        compiler_params=_params("parallel", "arbitrary"))(pre, pre, pre, dt_c, cum_c, cum_r, alog_c, dsk_c)


def _ssd_bwd(dy, pre, states, dt_c, cum_c, cum_r, sgd_c, alog_c, dsk_c):
    t = pre.shape[0]
    ng = pre.shape[1] // GC
    q = SSD_CHUNK
    nc = t // q
    gps = SSD_GPS_BWD
    chunk_grp, col_form, row_form, col_par, y_spec, st_spec, bc_spec = _ssd_specs(nc, True, ng, gps)

    def body(dy_ref, px_ref, pb_ref, pc_ref, sp_ref, dt_ref, cum_ref, cumr_ref, sgd_ref, ac_ref, dk_ref,
             dpx_ref, dpb_ref, dpc_ref, ddt_ref, dbias_ref, dalog_ref, dd_ref, ds_ref):
        @pl.when(pl.program_id(1) == 0)
        def _():
            ds_ref[...] = jnp.zeros_like(ds_ref)

        for gi in range(gps):
            (dy, px, dpx), (pb, pc, dpb, dpc), rest = _ssd_group_views(
                gi, (dy_ref, px_ref, dpx_ref), (pb_ref, pc_ref, dpb_ref, dpc_ref),
                (sp_ref, dt_ref, cum_ref, cumr_ref, sgd_ref, ac_ref, dk_ref, ddt_ref, dbias_ref, dalog_ref, dd_ref,
                 ds_ref))
            one_group(dy, px, pb, pc, *rest[:7], dpx, dpb, dpc, *rest[7:])

    def one_group(dy_ref, px_ref, pb_ref, pc_ref, sp_ref, dt_ref, cum_ref, cumr_ref, sgd_ref, ac_ref, dk_ref,
                  dpx_ref, dpb_ref, dpc_ref, ddt_ref, dbias_ref, dalog_ref, dd_ref, ds_ref):
        first = pl.program_id(1) == 0
        pre_v = jnp.concatenate([px_ref[...], pb_ref[...], pc_ref[...]], axis=1)
        v = _ssd_common(pre_v, dt_ref[...], cum_ref[...], cumr_ref[...], ac_ref[...], True)
        xa, bm, cm, xdt, cum, cum_r = v["xa"], v["bm"], v["cm"], v["xdt"], v["cum"], v["cum_r"]
        xdt_b = xdt.astype(BF16)
        dy_v = dy_ref[...]
        s0 = sp_ref[...]
        ds1 = ds_ref[...]
        s0b, ds1b = s0.astype(BF16), ds1.astype(BF16)
        head = _head_of_lane((q, GW), SSD_HEAD_DIM)
        lane = lax.broadcasted_iota(jnp.int32, (q, LANES), 1)
        lane1 = lax.broadcasted_iota(jnp.int32, (1, LANES), 1)
        lam_x = _expand(v["lam_c"], q, True)
        e_x = _expand(v["e_c"], q, True)

        dxa = _expand(dk_ref[...], 1) * dy_v
        dd = _contract(jnp.sum(dy_v * xa, axis=0, keepdims=True), 1)
        r = _dot_nt(cm, s0b)
        dcum = _contract(dy_v * r * lam_x, q, True)
        drb = (lam_x * dy_v).astype(BF16)
        dc = _dot(drb, s0b)
        ds0 = _dot_tn(drb, cm)
        extra = jnp.zeros((1, LANES), F32)
        for j in range(SSD_HPG):
            rows = slice(j * SSD_HEAD_DIM, (j + 1) * SSD_HEAD_DIM)
            lam_last = jnp.exp(cum_r[j:j + 1, q - 1:q])
            ds_ref[rows, :] = ds0[rows, :] + lam_last * ds1[rows, :]
            tot = jnp.sum(jnp.sum(ds1[rows, :] * s0[rows, :], axis=1, keepdims=True), axis=0, keepdims=True)
            extra = jnp.where(lane1 == j, lam_last * tot, extra)
        dv = _dot_nt(bm, ds1b)
        db = _dot((xdt * e_x).astype(BF16), ds1b)
        dxdt = e_x * dv
        dee = _contract(dv * xdt, q, True) * v["e_c"]
        dcum = dcum - dee
        extra = extra + jnp.sum(dee, axis=0, keepdims=True)
        dg = jnp.zeros((q, q), F32)
        col_sums = jnp.zeros((q, q), F32)
        for j in range(SSD_HPG):
            diff = cum[:, j:j + 1] - cum_r[j:j + 1, :]
            el = jnp.exp(jnp.where(v["tril"], diff, -jnp.inf))
            gl = v["g"] * el
            dym = jnp.where(head == j, dy_v, 0.0).astype(BF16)
            dwm = _dot_nt(dym, xdt_b)
            dxdt = dxdt + _dot_tn(gl.astype(BF16), dym)
            z = dwm * gl
            dcum = jnp.where(lane == j, dcum + jnp.sum(z, axis=1, keepdims=True), dcum)
            col_sums = jnp.where(v["row"] == j, jnp.sum(z, axis=0, keepdims=True), col_sums)
            dg = dg + dwm * el
        dcum = dcum - col_sums.T
        dgb = dg.astype(BF16)
        dc = dc + _dot(dgb, bm)
        db = db + _dot_tn(dgb, cm)
        da = _dot_f32((v["row"] <= v["col"]).astype(F32), dcum) + extra
        ddt = _contract(dxdt * xa, q, True) + v["a_c"] * da
        dalog = jnp.sum(v["dt"] * da, axis=0, keepdims=True) * v["a_c"]
        dxa = dxa + v["dt_x"] * dxdt
        ddt_raw = jnp.where(lane < SSD_HPG, ddt * sgd_ref[...], 0.0)
        sgrad = _silu_grad(pre_v, v["sg"])
        dpx_ref[...] = dxa * sgrad[:, :GW]
        dpb_ref[...] = db * sgrad[:, GW:GW + SSD_D_STATE]
        dpc_ref[...] = dc * sgrad[:, GW + SSD_D_STATE:]
        ddt_ref[...] = ddt_raw
        _acc(dbias_ref, jnp.sum(ddt_raw, axis=0, keepdims=True), first)
        _acc(dalog_ref, jnp.where(lane1 < SSD_HPG, dalog, 0.0), first)
        _acc(dd_ref, dd, first)

    return pl.pallas_call(
        body, name="ssd_scan_bwd", grid=(ng // gps, nc),
        in_specs=[y_spec] + chunk_grp + [st_spec, col_form, col_form, row_form, col_form, col_par, col_par],
        out_specs=[y_spec, bc_spec, bc_spec, col_form, col_par, col_par, col_par],
        out_shape=[jax.ShapeDtypeStruct((t, ng * GW), F32), jax.ShapeDtypeStruct((t, ng * SSD_D_STATE), F32),
                   jax.ShapeDtypeStruct((t, ng * SSD_D_STATE), F32), jax.ShapeDtypeStruct((ng, t, LANES), F32),
                   jax.ShapeDtypeStruct((ng, 1, LANES), F32), jax.ShapeDtypeStruct((ng, 1, LANES), F32),
                   jax.ShapeDtypeStruct((ng, 1, LANES), F32)],
        scratch_shapes=[pltpu.VMEM((gps, GW, SSD_D_STATE), F32)],
        compiler_params=_params("parallel", "arbitrary"))(dy, pre, pre, pre, states, dt_c, cum_c, cum_r, sgd_c, alog_c,
                                                           dsk_c)


def _gate_norm_fwd(y, zx, norm_w):
    t, di = y.shape
    tr = _tile(t, 512, 8)
    ng = di // GW

    def body(y_ref, z_ref, w_ref, o_ref):
        z = z_ref[...]
        gate = y_ref[...] * (z * _sigmoid(z))
        w = w_ref[...]
        for g in range(ng):
            cols = slice(g * GW, (g + 1) * GW)
            gs = gate[:, cols]
            r = lax.rsqrt(jnp.mean(gs * gs, axis=-1, keepdims=True) + NORM_EPS)
            o_ref[:, cols] = (gs * r * w[:, cols]).astype(BF16)

    row = pl.BlockSpec((tr, di), lambda i: (i, 0))
    return pl.pallas_call(body, name="ssd_gate_norm_fwd", grid=(t // tr,),
                          in_specs=[row, row, pl.BlockSpec((1, di), lambda i: (0, 0))], out_specs=row,
                          out_shape=jax.ShapeDtypeStruct((t, di), BF16), compiler_params=_params("parallel"))(
                              y, zx, norm_w)


def _gate_norm_bwd(dyn, y, zx, norm_w, after):
    t, di = y.shape
    tr = _tile(t, 256, 8)
    ng = di // GW

    def body(d_ref, y_ref, z_ref, w_ref, after_ref, dy_ref, dz_ref, dw_ref):
        z = z_ref[...]
        yv = y_ref[...]
        sg = _sigmoid(z)
        sz = z * sg
        gate = yv * sz
        w = w_ref[...]
        d = d_ref[...]
        dsz = _silu_grad(z, sg)
        dws = []
        for g in range(ng):
            cols = slice(g * GW, (g + 1) * GW)
            dg, dwr = _rms_bwd(gate[:, cols], w[:, cols], d[:, cols])
            dy_ref[:, cols] = dg * sz[:, cols]
            dz_ref[:, cols] = (dg * yv[:, cols] * dsz[:, cols]).astype(BF16)
            dws.append(jnp.sum(dwr, axis=0, keepdims=True))
        first = pl.program_id(0) == 0
        for g in range(ng):
            cols = slice(g * GW, (g + 1) * GW)

            @pl.when(first)
            def _():
                dw_ref[:, cols] = dws[g]

            @pl.when(jnp.logical_not(first))
            def _():
                dw_ref[:, cols] += dws[g]

    row = pl.BlockSpec((tr, di), lambda i: (i, 0))
    vec = pl.BlockSpec((1, di), lambda i: (0, 0))
    return pl.pallas_call(body, name="ssd_gate_norm_bwd", grid=(t // tr,),
                          in_specs=[row, row, row, vec, pl.BlockSpec((8, LANES), lambda i: (0, 0))],
                          out_specs=[row, row, vec],
                          out_shape=[jax.ShapeDtypeStruct((t, di), F32), jax.ShapeDtypeStruct((t, di), BF16),
                                     jax.ShapeDtypeStruct((1, di), F32)],
                          compiler_params=_params("arbitrary"))(dyn, y, zx, norm_w, after)


def _attn_mask_t(n):
    w = ATTN_WINDOW
    kpos = lax.broadcasted_iota(jnp.int32, (2 * w, ATTN_REP * w), 0)
    qpos = lax.broadcasted_iota(jnp.int32, (2 * w, ATTN_REP * w), 1) % w + w
    rel = qpos - kpos
    return (rel >= 0) & (rel < w) & jnp.logical_not((n == 0) & (kpos < w))


def _attn_probs_t(qts, ktb, mask, sink):
    s = _dot_tn(ktb, qts) * (ATTN_HEAD_DIM ** -0.5)
    s = jnp.where(mask, s, -jnp.inf)
    m = jnp.maximum(jnp.max(s, axis=0, keepdims=True), sink)
    e = jnp.exp(s - m)
    es = jnp.exp(sink - m)
    inv = 1.0 / (jnp.sum(e, axis=0, keepdims=True) + es)
    return e * inv, es * inv


def _attn_blocks_t(kv, q_ref, kc_ref, vc_ref, kp_ref, vp_ref):
    hd = ATTN_HEAD_DIM
    rows = slice(kv * hd, (kv + 1) * hd)
    ktb = jnp.concatenate([kp_ref[rows, :], kc_ref[rows, :]], axis=1)
    vtb = jnp.concatenate([vp_ref[rows, :], vc_ref[rows, :]], axis=1)
    qts = jnp.concatenate([q_ref[(kv * ATTN_REP + r) * hd:(kv * ATTN_REP + r + 1) * hd, :]
                           for r in range(ATTN_REP)], axis=1)
    return qts, ktb, vtb


def _attn_specs_t(nb, cur, prev):
    w, hd = ATTN_WINDOW, ATTN_HEAD_DIM
    kd = ATTN_N_KV * hd
    qd = ATTN_REP * kd
    return [pl.BlockSpec((qd, w), lambda n: (0, cur(n))),
            pl.BlockSpec((kd, w), lambda n: (ATTN_REP, cur(n))),
            pl.BlockSpec((kd, w), lambda n: (ATTN_REP + 1, cur(n))),
            pl.BlockSpec((kd, w), lambda n: (ATTN_REP, prev(n))),
            pl.BlockSpec((kd, w), lambda n: (ATTN_REP + 1, prev(n)))]


def _attn_fwd_t(qkv_t, sinks_rep):
    t = qkv_t.shape[1]
    w, hd = ATTN_WINDOW, ATTN_HEAD_DIM
    qd = ATTN_N_KV * ATTN_REP * hd
    nb = t // w

    def body(q_ref, kc_ref, vc_ref, kp_ref, vp_ref, s_ref, o_ref):
        mask = _attn_mask_t(pl.program_id(0))
        for kv in range(ATTN_N_KV):
            qts, ktb, vtb = _attn_blocks_t(kv, q_ref, kc_ref, vc_ref, kp_ref, vp_ref)
            p, _ = _attn_probs_t(qts, ktb, mask, s_ref[kv])
            ots = _dot(vtb, p.astype(BF16))
            for r in range(ATTN_REP):
                h = kv * ATTN_REP + r
                o_ref[h * hd:(h + 1) * hd, :] = ots[:, r * w:(r + 1) * w].astype(BF16)

    return pl.pallas_call(
        body, name="attn_fwd", grid=(nb,),
        in_specs=_attn_specs_t(nb, lambda n: n, lambda n: jnp.maximum(n - 1, 0)) + [
            pl.BlockSpec(sinks_rep.shape, lambda n: (0, 0, 0))],
        out_specs=pl.BlockSpec((qd, w), lambda n: (0, n)),
        out_shape=jax.ShapeDtypeStruct((qd, t), BF16),
        compiler_params=_params("parallel"))(qkv_t, qkv_t, qkv_t, qkv_t, qkv_t, sinks_rep)


def _attn_bwd_t(qkv_t, do_t, sinks_rep):
    t = qkv_t.shape[1]
    w, hd = ATTN_WINDOW, ATTN_HEAD_DIM
    kd = ATTN_N_KV * hd
    qd = ATTN_REP * kd
    nq = ATTN_N_KV * ATTN_REP
    nb = t // w
    rows_all = qd + 2 * kd

    def body(q_ref, kc_ref, vc_ref, kp_ref, vp_ref, do_ref, s_ref, dqkv_ref, bsum_ref, dsk_ref,
             carry_ref, new_ref, bacc_ref, sacc_ref):
        n = pl.program_id(0)

        @pl.when(n == 0)
        def _():
            carry_ref[...] = jnp.zeros_like(carry_ref)
            bacc_ref[...] = jnp.zeros_like(bacc_ref)
            sacc_ref[...] = jnp.zeros_like(sacc_ref)

        @pl.when(n < nb)
        def _():
            mask = _attn_mask_t(n)
            for kv in range(ATTN_N_KV):
                qts, ktb, vtb = _attn_blocks_t(kv, q_ref, kc_ref, vc_ref, kp_ref, vp_ref)
                dots = jnp.concatenate([do_ref[(kv * ATTN_REP + r) * hd:(kv * ATTN_REP + r + 1) * hd, :]
                                        for r in range(ATTN_REP)], axis=1)
                p, ps = _attn_probs_t(qts, ktb, mask, s_ref[kv])
                dpt = _dot_tn(vtb, dots)
                delta = jnp.sum(p * dpt, axis=0, keepdims=True)
                dst = (p * (dpt - delta) * (hd ** -0.5)).astype(BF16)
                dqts = _dot(ktb, dst)
                for r in range(ATTN_REP):
                    h = kv * ATTN_REP + r
                    new_ref[h * hd:(h + 1) * hd, :] = dqts[:, r * w:(r + 1) * w]
                dktb = _dot_nt(qts, dst)
                dvtb = _dot_nt(dots, p.astype(BF16))
                krows = slice(qd + kv * hd, qd + (kv + 1) * hd)
                vrows = slice(qd + kd + kv * hd, qd + kd + (kv + 1) * hd)
                carry_ref[krows, :] += dktb[:, :w]
                carry_ref[vrows, :] += dvtb[:, :w]
                new_ref[krows, :] = dktb[:, w:]
                new_ref[vrows, :] = dvtb[:, w:]
                sacc_ref[kv] += -(ps * delta)

        @pl.when(n >= 1)
        def _():
            done = carry_ref[...]
            dqkv_ref[...] = done.astype(BF16)
            bacc_ref[...] += done

        @pl.when(n < nb)
        def _():
            carry_ref[...] = new_ref[...]

        @pl.when(n == nb)
        def _():
            bsum_ref[...] = jnp.sum(bacc_ref[...], axis=1, keepdims=True)
            lane = lax.broadcasted_iota(jnp.int32, (1, nq), 1)
            dsk = jnp.zeros((1, nq), F32)
            for kv in range(ATTN_N_KV):
                acc = sacc_ref[kv]
                for r in range(ATTN_REP):
                    tot = jnp.sum(acc[:, r * w:(r + 1) * w], axis=1, keepdims=True)
                    dsk = jnp.where(lane == kv * ATTN_REP + r, tot, dsk)
            dsk_ref[...] = dsk

    cur = lambda n: jnp.minimum(n, nb - 1)
    prev = lambda n: jnp.maximum(jnp.minimum(n, nb - 1) - 1, 0)
    return pl.pallas_call(
        body, name="attn_bwd", grid=(nb + 1,),
        in_specs=_attn_specs_t(nb, cur, prev) + [pl.BlockSpec((qd, w), lambda n: (0, cur(n))),
                                                 pl.BlockSpec(sinks_rep.shape, lambda n: (0, 0, 0))],
        out_specs=[pl.BlockSpec((rows_all, w), lambda n: (0, jnp.maximum(n - 1, 0))),
                   pl.BlockSpec((rows_all, 1), lambda n: (0, 0)),
                   pl.BlockSpec((1, nq), lambda n: (0, 0))],
        out_shape=[jax.ShapeDtypeStruct((rows_all, t), BF16), jax.ShapeDtypeStruct((rows_all, 1), F32),
                   jax.ShapeDtypeStruct((1, nq), F32)],
        scratch_shapes=[pltpu.VMEM((rows_all, w), F32), pltpu.VMEM((rows_all, w), F32),
                        pltpu.VMEM((rows_all, w), F32), pltpu.VMEM(sinks_rep.shape, F32)],
        compiler_params=_params("arbitrary"))(qkv_t, qkv_t, qkv_t, qkv_t, qkv_t, do_t, sinks_rep)


HBM_SPEC = pl.BlockSpec(memory_space=pl.ANY)
HBM_ONLY = pl.BlockSpec(memory_space=pltpu.HBM)


def _comm_call(name, body, ins, out_shapes, n_sems):
    return pl.pallas_call(
        body, name=name, in_specs=[HBM_SPEC] * len(ins), out_specs=[HBM_SPEC] * len(out_shapes),
        out_shape=out_shapes,
        scratch_shapes=[pltpu.SemaphoreType.DMA((s,)) for s in n_sems])(*ins)


def _all_gather(name, shards, after):
    n = len(shards)
    na = len(after)

    def body(*refs):
        x_refs, out_refs = refs[:n], refs[n + na:2 * n + na]
        send_sems, recv_sems, local_sems = refs[2 * n + na:]
        x, y, c = lax.axis_index("x"), lax.axis_index("y"), lax.axis_index("c")
        me, sibling = (x, y, c), (x, y, 1 - c)
        chips = [(1 - x, y), (x, 1 - y), (1 - x, 1 - y)]

        def slot(i, px, py, pc):
            return out_refs[i].at[4 * px + 2 * py + pc]

        def copy(k, i, block, to, src=None):
            return pltpu.make_async_remote_copy(
                src_ref=slot(i, *block) if src is None else src, dst_ref=slot(i, *block),
                send_sem=send_sems.at[k * n + i], recv_sem=recv_sems.at[k * n + i], device_id=to,
                device_id_type=MESH)

        mine = [pltpu.make_async_copy(x_refs[i], slot(i, *me), local_sems.at[i]) for i in range(n)]
        first = []
        for i in range(n):
            mine[i].start()
            first.append(copy(0, i, me, sibling, src=x_refs[i]))
            first += [copy(1 + j, i, me, (*chip, c), src=x_refs[i]) for j, chip in enumerate(chips)]
        for cp in first:
            cp.start()
        passed = []
        for i in range(n):
            for j, chip in enumerate(chips):
                copy(1 + j, i, (*chip, c), me).wait_recv()
                passed.append(copy(4 + j, i, (*chip, c), sibling))
                passed[-1].start()
        for i in range(n):
            copy(0, i, sibling, me).wait_recv()
            for j, chip in enumerate(chips):
                copy(4 + j, i, (*chip, 1 - c), me).wait_recv()
        for cp in first + passed:
            cp.wait_send()
        for cp in mine:
            cp.wait()

    outs = [jax.ShapeDtypeStruct((N_DEV,) + s.shape, s.dtype) for s in shards]
    return _comm_call(name, body, list(shards) + list(after), outs, (7 * n, 7 * n, n))


SEM_SPEC = pl.BlockSpec(memory_space=pltpu.SEMAPHORE)
SPLIT_COPY_EFFECT = pltpu.SideEffectType.DATAFLOW_SIDE_EFFECTING


def _in_hbm(a):
    return pltpu.with_memory_space_constraint(a, pltpu.HBM)


def _split_start(name, body, srcs, lands, n_sems):
    n = len(srcs)
    bufs = [_in_hbm(a) for a in list(srcs) + list(lands)]
    outs = pl.pallas_call(
        body, name=name,
        out_shape=(pltpu.SemaphoreType.DMA((n_sems,)), pltpu.SemaphoreType.DMA((n_sems,)),
                   *[pltpu.HBM(a.shape, a.dtype) for a in bufs], jax.ShapeDtypeStruct((8, LANES), F32)),
        in_specs=[HBM_ONLY] * (2 * n),
        out_specs=(SEM_SPEC, SEM_SPEC, *[HBM_ONLY] * (2 * n), pl.BlockSpec(memory_space=pltpu.VMEM)),
        input_output_aliases={i: 2 + i for i in range(2 * n)},
        compiler_params=pltpu.CompilerParams(has_side_effects=SPLIT_COPY_EFFECT))(*bufs)
    return outs[0], outs[1], list(outs[2:2 + n]), list(outs[2 + n:2 + 2 * n]), outs[-1]


def _split_wait(name, body, send_sems, recv_sems, srcs, lands, after):
    n = len(srcs)
    outs = pl.pallas_call(
        body, name=name,
        out_shape=[pltpu.HBM(a.shape, a.dtype) for a in list(srcs) + list(lands)],
        in_specs=[HBM_ONLY] * (2 * n) + [SEM_SPEC, SEM_SPEC, HBM_SPEC],
        out_specs=[HBM_ONLY] * (2 * n),
        input_output_aliases={i: i for i in range(2 * n)},
        compiler_params=pltpu.CompilerParams(has_side_effects=SPLIT_COPY_EFFECT))(
            *srcs, *lands, send_sems, recv_sems, after)
    return list(outs[:n]), list(outs[n:])


N_PEERS = N_DEV - 1


def _gather_peers():
    x, y, c = lax.axis_index("x"), lax.axis_index("y"), lax.axis_index("c")
    flips = [(fx, fy, fc) for fx in (0, 1) for fy in (0, 1) for fc in (0, 1) if fx or fy or fc]
    return [(1 - x if fx else x, 1 - y if fy else y, 1 - c if fc else c) for fx, fy, fc in flips]


def _block_id(dev):
    return 4 * dev[0] + 2 * dev[1] + dev[2]


def _landing_block(land_ref, shard_shape, side_by_side, dev):
    if not side_by_side:
        return land_ref.at[_block_id(dev)]
    cols = shard_shape[1]
    return land_ref.at[:, pl.ds(pl.multiple_of(_block_id(dev) * cols, LANES), cols)]


def _gather_start(name, shards, side_by_side):
    n = len(shards)

    def body(*refs):
        x_refs, land_refs = refs[:n], refs[n:2 * n]
        send_sems, recv_sems, token = refs[2 * n], refs[2 * n + 1], refs[-1]
        me = (lax.axis_index("x"), lax.axis_index("y"), lax.axis_index("c"))
        for i in range(n):
            for k, peer in enumerate(_gather_peers()):
                pltpu.make_async_remote_copy(
                    src_ref=x_refs[i], dst_ref=_landing_block(land_refs[i], shards[i].shape, side_by_side[i], me),
                    send_sem=send_sems.at[N_PEERS * i + k], recv_sem=recv_sems.at[N_PEERS * i + k],
                    device_id=peer, device_id_type=MESH).start()
            pltpu.make_async_copy(x_refs[i], _landing_block(land_refs[i], shards[i].shape, side_by_side[i], me),
                                  send_sems.at[N_PEERS * n + i]).start()
        token[...] = jnp.zeros_like(token)

    lands = [lax.empty((s.shape[0], N_DEV * s.shape[1]) if wide else (N_DEV,) + s.shape, s.dtype)
             for s, wide in zip(shards, side_by_side)]
    return _split_start(name, body, shards, lands, (N_PEERS + 1) * n)


def _gather_wait(name, send_sems, recv_sems, first, n_all, shards, lands, side_by_side, after):
    n = len(shards)

    def body(*refs):
        x_refs, land_refs = refs[:n], refs[n:2 * n]
        send_sems, recv_sems = refs[2 * n], refs[2 * n + 1]
        me = (lax.axis_index("x"), lax.axis_index("y"), lax.axis_index("c"))
        for i in range(n):
            pltpu.make_async_copy(x_refs[i], _landing_block(land_refs[i], shards[i].shape, side_by_side[i], me),
                                  send_sems.at[N_PEERS * n_all + first + i]).wait()
            for k, peer in enumerate(_gather_peers()):
                cp = pltpu.make_async_remote_copy(
                    src_ref=x_refs[i], dst_ref=_landing_block(land_refs[i], shards[i].shape, side_by_side[i], peer),
                    send_sem=send_sems.at[N_PEERS * (first + i) + k],
                    recv_sem=recv_sems.at[N_PEERS * (first + i) + k],
                    device_id=peer, device_id_type=MESH)
                cp.wait_send()
                cp.wait_recv()

    return _split_wait(name, body, send_sems, recv_sems, shards, lands, after)


def _scatter_start(name, blocks):
    n = len(blocks)

    def body(*refs):
        b_refs, land_refs = refs[:n], refs[n:2 * n]
        send_sems, recv_sems, token = refs[2 * n], refs[2 * n + 1], refs[-1]
        me = (lax.axis_index("x"), lax.axis_index("y"), lax.axis_index("c"))
        for i in range(n):
            for k, peer in enumerate(_gather_peers()):
                pltpu.make_async_remote_copy(
                    src_ref=b_refs[i].at[_block_id(peer)], dst_ref=land_refs[i].at[_block_id(me)],
                    send_sem=send_sems.at[N_PEERS * i + k], recv_sem=recv_sems.at[N_PEERS * i + k],
                    device_id=peer, device_id_type=MESH).start()
            pltpu.make_async_copy(b_refs[i].at[_block_id(me)], land_refs[i].at[_block_id(me)],
                                  send_sems.at[N_PEERS * n + i]).start()
        token[...] = jnp.zeros_like(token)

    lands = [lax.empty(b.shape, b.dtype) for b in blocks]
    return _split_start(name, body, blocks, lands, (N_PEERS + 1) * n)


def _scatter_wait(name, send_sems, recv_sems, blocks, lands, after):
    n = len(blocks)

    def body(*refs):
        b_refs, land_refs = refs[:n], refs[n:2 * n]
        send_sems, recv_sems = refs[2 * n], refs[2 * n + 1]
        me = (lax.axis_index("x"), lax.axis_index("y"), lax.axis_index("c"))
        for i in range(n):
            pltpu.make_async_copy(b_refs[i].at[_block_id(me)], land_refs[i].at[_block_id(me)],
                                  send_sems.at[N_PEERS * n + i]).wait()
            for k, peer in enumerate(_gather_peers()):
                cp = pltpu.make_async_remote_copy(
                    src_ref=b_refs[i].at[_block_id(peer)], dst_ref=land_refs[i].at[_block_id(peer)],
                    send_sem=send_sems.at[N_PEERS * i + k], recv_sem=recv_sems.at[N_PEERS * i + k],
                    device_id=peer, device_id_type=MESH)
                cp.wait_send()
                cp.wait_recv()

    return _split_wait(name, body, send_sems, recv_sems, blocks, lands, after)


def _adamw(w, g, m, v):
    m = ADAM_B1 * m + (1.0 - ADAM_B1) * g
    v = ADAM_B2 * v + (1.0 - ADAM_B2) * (g * g)
    m_hat = m / (1.0 - ADAM_B1 ** ADAM_STEP)
    v_hat = v / (1.0 - ADAM_B2 ** ADAM_STEP)
    delta = -ADAM_LR * (m_hat / (jnp.sqrt(v_hat) + ADAM_EPS) + ADAM_WD * w)
    return delta, m, v


def _adamw_tiles(r, c_):
    tr = _tile(r, 256, 16)
    return (tr, c_) if tr < r or r <= 256 else (r, _tile(c_, 256))


def _sum_parts(part):
    g = part[0].astype(F32)
    for k in range(1, part.shape[0]):
        g = g + part[k].astype(F32)
    return g


def _sum_adamw(name, parts, w, m, v):
    r, c_ = w.shape
    tr, tc = _adamw_tiles(r, c_)

    def body(p_ref, w_ref, m_ref, v_ref, g_ref, d_ref, nm_ref, nv_ref):
        g = _sum_parts(p_ref)
        g_ref[...] = g
        d_ref[...], nm_ref[...], nv_ref[...] = _adamw(w_ref[...], g, m_ref[...], v_ref[...])

    tile = pl.BlockSpec((tr, tc), lambda i, j: (i, j))
    return pl.pallas_call(body, name=name, grid=(r // tr, c_ // tc),
                          in_specs=[pl.BlockSpec((parts.shape[0], tr, tc), lambda i, j: (0, i, j)), tile, tile, tile],
                          out_specs=[tile] * 4, out_shape=[jax.ShapeDtypeStruct((r, c_), F32)] * 4,
                          compiler_params=_params("parallel", "parallel"))(parts, w, m, v)


def _sum_adamw_layers(name, parts, w, m, v):
    n_layers, r, c_ = w.shape
    tr = _tile(r, 256, 16)

    def body(*refs):
        p_refs = refs[:n_layers]
        w_ref, m_ref, v_ref, g_ref, d_ref, nm_ref, nv_ref = refs[n_layers:]
        layer = pl.program_id(0)
        g = _sum_parts(p_refs[0])
        for li in range(1, n_layers):
            g = jnp.where(layer == li, _sum_parts(p_refs[li]), g)
        g_ref[...] = g
        d_ref[...], nm_ref[...], nv_ref[...] = _adamw(w_ref[...], g, m_ref[...], v_ref[...])

    row = pl.BlockSpec((None, tr, c_), lambda l, i: (l, i, 0))
    specs = [pl.BlockSpec((p.shape[0], tr, c_), lambda l, i: (0, i, 0)) for p in parts]
    return pl.pallas_call(body, name=name, grid=(n_layers, r // tr), in_specs=specs + [row, row, row],
                          out_specs=[row] * 4, out_shape=[jax.ShapeDtypeStruct(w.shape, F32)] * 4,
                          compiler_params=_params("parallel", "parallel"))(*parts, w, m, v)


def _pack_rows(flat, n_rows, cols):
    pad = n_rows * cols - flat.shape[-1]
    flat = jnp.pad(flat, [(0, 0)] * (flat.ndim - 1) + [(0, pad)])
    return flat.reshape(flat.shape[:-1] + (n_rows, cols))


def _cols_split(full):
    c = full.shape[1] // N_DEV
    return jnp.stack([full[:, d * c:(d + 1) * c] for d in range(N_DEV)])


def _rows_join(blocks):
    return blocks.reshape(N_DEV * blocks.shape[1], blocks.shape[2])


def _rows_split(full):
    return full.reshape(N_DEV, full.shape[0] // N_DEV, full.shape[1])


def _heads_col(v, ng):
    return jnp.pad(v.reshape(ng, 1, SSD_HPG), ((0, 0), (0, 0), (0, LANES - SSD_HPG)))


MATRIX_ITEMS = ("w_in", "w_out", "up0", "down0", "w_qkv", "w_o", "up1", "down1")
VECTOR_ITEMS = ("conv_w", "b_qkv", "b_o")
ITEMS = MATRIX_ITEMS + VECTOR_ITEMS
GATHER_STAGES = (("w_in", "conv_w"), ("w_out", "up0", "down0"), ("w_qkv", "b_qkv", "w_o", "b_o", "up1", "down1"))
SIDE_BY_SIDE = ("conv_w", "up0", "up1", "b_o")


def _items(tree, prefix=""):
    g = lambda k: tree[prefix + k]
    return {"w_in": g("ssd_w_in")[0].T, "w_out": g("ssd_w_out")[0], "w_qkv": g("attn_w_qkv")[0].T,
            "w_o": g("attn_w_o")[0], "up0": g("mlp_w_up")[0], "up1": g("mlp_w_up")[1],
            "down0": g("mlp_w_down")[0], "down1": g("mlp_w_down")[1], "conv_w": g("ssd_conv_w")[0],
            "b_qkv": g("attn_b_qkv"), "b_o": g("attn_b_o")}


REPLICATED = ("ssd_conv_b", "ssd_dt_bias", "ssd_a_log", "ssd_d", "ssd_norm_w", "attn_sinks", "mix_pre_norm",
              "mix_post_norm", "ffn_pre_norm", "ffn_post_norm")
WEIGHTS = ("ssd_w_in", "ssd_conv_w", "ssd_conv_b", "ssd_dt_bias", "ssd_a_log", "ssd_d", "ssd_norm_w", "ssd_w_out",
           "attn_w_qkv", "attn_b_qkv", "attn_sinks", "attn_w_o", "attn_b_o", "mlp_w_up", "mlp_w_down",
           "mix_pre_norm", "mix_post_norm", "ffn_pre_norm", "ffn_post_norm")


def _forward_backward(x, target, rep, token, weights_of_stage, reduce_grads):
    t, d = x.shape
    ng = rep["ssd_norm_w"].shape[1] // GW
    di = ng * GW
    n_xbc = ng * GC
    nh = ng * SSD_HPG
    grads, blocks = {}, {}
    w_up, w_down = [None, None], [None, None]
    sinks_rep = jnp.repeat(rep["attn_sinks"].reshape(ATTN_N_KV, ATTN_REP, 1), ATTN_WINDOW, axis=2).reshape(
        ATTN_N_KV, 1, ATTN_REP * ATTN_WINDOW)
    conv_b = rep["ssd_conv_b"]
    gn = ng * SSD_D_STATE
    parts = ((0, di), (di, di), (2 * di, gn), (2 * di + gn, gn), (di + n_xbc, nh))
    alog_c, dsk_c = (_heads_col(rep[k], ng) for k in ("ssd_a_log", "ssd_d"))
    bias_l, alog_l = (jnp.pad(rep[k], ((0, 0), (0, LANES - nh))) for k in ("ssd_dt_bias", "ssd_a_log"))
    norm = {k: rep[k] for k in ("mix_pre_norm", "mix_post_norm", "ffn_pre_norm", "ffn_post_norm")}

    def nrow(name, i):
        return norm[name][i:i + 1]

    def mlp_fwd(i, u2):
        p = _mm(f"mlp{i}_up", [u2], [w_up[i]], "nn", tm=1024, tn=1024, out_dtypes=(BF16,),
                epilogue=lambda acc: (jnp.square(jnp.maximum(acc, 0.0)),))
        f = _mm(f"mlp{i}_down", [p], [w_down[i]], "nn", tm=512, tn=1024)
        return p, f

    def mlp_bwd(i, df, u2, p):
        da = _mm(f"mlp{i}_dact", [df], [w_down[i]], "nt", tm=1024, tn=1024, out_dtypes=(BF16,),
                 tiles=(p,), epilogue=lambda acc, pv: (acc * (2.0 * jnp.sqrt(pv.astype(F32))),))
        blocks[f"down{i}"] = _rows_split(_mm(f"mlp{i}_dwdown", [p], [df], "tn", tm=512, tn=1024,
                                             out_dtypes=(PAYLOAD,)))
        blocks[f"up{i}"] = _mm(f"mlp{i}_dwup", [u2], [da], "tn", tm=1024, tn=da.shape[1] // N_DEV,
                               out_dtypes=(PAYLOAD,), col_blocks=True)
        return _mm(f"mlp{i}_dx", [da], [w_up[i]], "nt", tm=512, tn=1024)

    u0 = _prenorm("l0_prenorm", x, nrow("mix_pre_norm", 0), token)
    got = weights_of_stage(0, u0)
    w_in_t = _rows_join(got["w_in"])
    w_dt_t = jnp.pad(w_in_t[di + n_xbc:], ((0, LANES - nh), (0, 0)))
    conv_w = got["conv_w"]
    zx = _mm("ssd_in_proj", [u0], [w_in_t], "nt", tm=1024, tn=1024, n_use=di + n_xbc)
    zdt = _mm("ssd_dt_proj", [u0], [w_dt_t], "nt", tm=1024, tn=LANES)
    pre = _conv_fwd(zx, di, n_xbc, conv_w, conv_b)
    dt_c, cum_c, cum_r, sgd_c = _ssd_dt_prep(zdt, bias_l, alog_l, ng)
    y, states = _ssd_fwd(pre, dt_c, cum_c, cum_r, alog_c, dsk_c)
    yn = _gate_norm_fwd(y, zx, rep["ssd_norm_w"])
    got = weights_of_stage(1, yn)
    w_out = _rows_join(got["w_out"])
    w_up[0], w_down[0] = got["up0"], _rows_join(got["down0"])
    mix0 = _mm("ssd_out_proj", [yn], [w_out], "nn", tm=1024, tn=1024)
    h1, u0f = _post_pre("l0_mid", x, mix0, nrow("mix_post_norm", 0), nrow("ffn_pre_norm", 0))
    p0, f0 = mlp_fwd(0, u0f)
    h2, u1 = _post_pre("l1_in", h1, f0, nrow("ffn_post_norm", 0), nrow("mix_pre_norm", 1))
    got = weights_of_stage(2, u1)
    w_qkv_t = _rows_join(got["w_qkv"])
    w_o = _rows_join(got["w_o"])
    b_qkv_col = got["b_qkv"].reshape(-1, 1)
    b_o = got["b_o"]
    w_up[1], w_down[1] = got["up1"], _rows_join(got["down1"])
    qkv_t = _mm("attn_qkv_proj", [w_qkv_t], [u1], "nt", tm=768, tn=1024, out_dtypes=(BF16,), cols=(b_qkv_col,),
                epilogue=lambda acc, b: (acc + b,))
    ao_t = _attn_fwd_t(qkv_t, sinks_rep)
    mix1 = _mm("attn_out_proj", [ao_t], [w_o], "tn", tm=1024, tn=1024, rows=(b_o,),
               epilogue=lambda acc, b: (acc + b,))
    h3, u1f = _post_pre("l1_mid", h2, mix1, nrow("mix_post_norm", 1), nrow("ffn_pre_norm", 1))
    p1, f1 = mlp_fwd(1, u1f)
    dh, loss_row = _final_loss("loss", h3, f1, nrow("ffn_post_norm", 1), target)

    g_norm = {k: [None, None] for k in norm}
    df1, g_norm["ffn_post_norm"][1], _ = _norm_bwd("l1_ffn_post_bwd", dh, post=(f1, nrow("ffn_post_norm", 1)))
    du = mlp_bwd(1, df1, u1f, p1)
    sent = reduce_grads("mlp1", {k: blocks[k] for k in ("up1", "down1")})
    dh, g_norm["ffn_pre_norm"][1], dmix1, g_norm["mix_post_norm"][1], db_o = _norm_bwd(
        "l1_mid_bwd", dh, pre=(du, h3, nrow("ffn_pre_norm", 1)), post=(mix1, nrow("mix_post_norm", 1)), after=sent)
    blocks["b_o"] = _cols_split(db_o)
    blocks["w_o"] = _rows_split(_mm("attn_dwo", [ao_t], [dmix1], "nn", tm=512, tn=1024, out_dtypes=(PAYLOAD,)))
    dao_t = _mm("attn_dout", [w_o], [dmix1], "nt", tm=1024, tn=1024, out_dtypes=(BF16,))
    dqkv_t, db_qkv, grads["attn_sinks"] = _attn_bwd_t(qkv_t, dao_t, sinks_rep)
    blocks["b_qkv"] = db_qkv.reshape(N_DEV, 1, -1)
    blocks["w_qkv"] = _rows_split(_mm("attn_dwqkv", [dqkv_t], [u1], "nn", tm=512, tn=1024, out_dtypes=(PAYLOAD,)))
    du = _mm("attn_dx", [dqkv_t], [w_qkv_t], "tn", tm=1024, tn=1024)
    sent = reduce_grads("attn", {k: blocks[k] for k in ("w_o", "w_qkv", "b_o", "b_qkv")})
    dh, g_norm["mix_pre_norm"][1], df0, g_norm["ffn_post_norm"][0], _ = _norm_bwd(
        "l1_in_bwd", dh, pre=(du, h2, nrow("mix_pre_norm", 1)), post=(f0, nrow("ffn_post_norm", 0)), after=sent)
    du = mlp_bwd(0, df0, u0f, p0)
    sent = reduce_grads("mlp0", {k: blocks[k] for k in ("up0", "down0")})
    dh, g_norm["ffn_pre_norm"][0], dmix0, g_norm["mix_post_norm"][0], _ = _norm_bwd(
        "l0_mid_bwd", dh, pre=(du, h1, nrow("ffn_pre_norm", 0)), post=(mix0, nrow("mix_post_norm", 0)), after=sent)
    blocks["w_out"] = _rows_split(_mm("ssd_dwout", [yn], [dmix0], "tn", tm=512, tn=1024, out_dtypes=(PAYLOAD,)))
    dyn = _mm("ssd_dyn", [dmix0], [w_out], "nt", tm=1024, tn=1024)
    sent = reduce_grads("ssdout", {"w_out": blocks["w_out"]})
    dy, dz, grads["ssd_norm_w"] = _gate_norm_bwd(dyn, y, zx, rep["ssd_norm_w"], sent)
    dpx, dpb, dpc, ddt_g, dbias_g, dalog_g, dd_g = _ssd_bwd(dy, pre, states, dt_c, cum_c, cum_r, sgd_c, alog_c,
                                                             dsk_c)
    conv_out = [_conv_bwd(f"ssd_conv_bwd_{tag}", dp, zx, c0, conv_w[:, c0 - di:c0 - di + n])
                for tag, dp, (c0, n) in zip("xbc", (dpx, dpb, dpc), parts[1:4])]
    dconv_w = jnp.concatenate([o[1] for o in conv_out], axis=1)
    dconv_b = jnp.concatenate([o[2] for o in conv_out], axis=1)
    ddt = jnp.transpose(ddt_g[:, :, :SSD_HPG], (1, 0, 2)).reshape(t, nh)
    ddt = jnp.pad(ddt, ((0, 0), (0, LANES - nh))).astype(BF16)
    blocks["conv_w"] = _cols_split(dconv_w)
    grads["ssd_conv_b"] = dconv_b
    for name, val in (("ssd_dt_bias", dbias_g), ("ssd_a_log", dalog_g), ("ssd_d", dd_g)):
        grads[name] = val[:, 0, :SSD_HPG].reshape(1, nh)
    d_zx = [dz] + [o[0] for o in conv_out] + [ddt]
    dw_parts = [_mm(f"ssd_dw_{tag}", [d], [u0], "tn", tm=512, tn=1024, out_dtypes=(PAYLOAD,))
                for tag, d in zip("zxbct", d_zx)]
    dw_parts[-1] = dw_parts[-1][:nh]
    blocks["w_in"] = _rows_split(jnp.concatenate(dw_parts, axis=0))
    sent = reduce_grads("ssd", {k: blocks[k] for k in ("w_in", "conv_w")})
    w_parts = [w_in_t[r0:r0 + n] for r0, n in parts[:-1]] + [w_dt_t]
    du = _mm("ssd_dx", d_zx, w_parts, "nn", tm=256, tn=1024, after=sent)
    grad_x, g_norm["mix_pre_norm"][0] = _norm_bwd("l0_in_bwd", dh, pre=(du, x, nrow("mix_pre_norm", 0)), after=sent)
    for k in norm:
        grads[k] = jnp.concatenate(g_norm[k], axis=0)
    return loss_row, grad_x, grads


def kernel(x, ssd_w_in, ssd_conv_w, ssd_conv_b, ssd_dt_bias, ssd_a_log, ssd_d, ssd_norm_w, ssd_w_out, attn_w_qkv, attn_b_qkv, attn_sinks, attn_w_o, attn_b_o, mlp_w_up, mlp_w_down, mix_pre_norm, mix_post_norm, ffn_pre_norm, ffn_post_norm, loss_target, m_ssd_w_in, m_ssd_conv_w, m_ssd_conv_b, m_ssd_dt_bias, m_ssd_a_log, m_ssd_d, m_ssd_norm_w, m_ssd_w_out, m_attn_w_qkv, m_attn_b_qkv, m_attn_sinks, m_attn_w_o, m_attn_b_o, m_mlp_w_up, m_mlp_w_down, m_mix_pre_norm, m_mix_post_norm, m_ffn_pre_norm, m_ffn_post_norm, v_ssd_w_in, v_ssd_conv_w, v_ssd_conv_b, v_ssd_dt_bias, v_ssd_a_log, v_ssd_d, v_ssd_norm_w, v_ssd_w_out, v_attn_w_qkv, v_attn_b_qkv, v_attn_sinks, v_attn_w_o, v_attn_b_o, v_mlp_w_up, v_mlp_w_down, v_mix_pre_norm, v_mix_post_norm, v_ffn_pre_norm, v_ffn_post_norm):
    given = dict(locals())
    w = {k: given[k] for k in WEIGHTS}
    mom_m = {k: given["m_" + k] for k in WEIGHTS}
    mom_v = {k: given["v_" + k] for k in WEIGHTS}
    w_it, m_it, v_it = _items(given), _items(given, "m_"), _items(given, "v_")

    order = [k for stage in GATHER_STAGES for k in stage]
    shards = [w_it[k].astype(PAYLOAD) if k in MATRIX_ITEMS else w_it[k] for k in order]
    wide = [k in SIDE_BY_SIDE for k in order]
    g_send, g_recv, shards, lands, token = _gather_start("gather_start", shards, wide)

    def weights_of_stage(s, after):
        first = sum(len(stage) for stage in GATHER_STAGES[:s])
        sl = slice(first, first + len(GATHER_STAGES[s]))
        _, got = _gather_wait(f"gather_wait{s}", g_send, g_recv, first, len(order), shards[sl], lands[sl], wide[sl],
                              after)
        return dict(zip(GATHER_STAGES[s], got))

    in_flight = []

    def reduce_grads(tag, blocks):
        keys = list(blocks)
        started = _scatter_start(f"rs_start_{tag}", [blocks[k] for k in keys])
        in_flight.append((tag, keys, started))
        return started[-1]

    rep = {k: w[k] for k in REPLICATED}
    loss_row, grad_x, grads = _forward_backward(x[0], loss_target[0], rep, token, weights_of_stage, reduce_grads)

    def pack_rep(tree, last):
        flat = jnp.concatenate([tree[k].reshape(-1) for k in REPLICATED] + [last])
        return _pack_rows(flat, _round_up(-(-flat.shape[0] // LANES), 8), LANES)

    landed = {}

    def wait_group(group, after):
        tag, keys, (s_send, s_recv, srcs, s_lands, _) = group
        _, got = _scatter_wait(f"rs_wait_{tag}", s_send, s_recv, srcs, s_lands, after)
        landed.update(zip(keys, got))

    def adamw_item(k):
        return _sum_adamw(f"adamw_{k}", landed[k], w_it[k], m_it[k], v_it[k])

    def adamw_stack(name, keys):
        return _sum_adamw_layers(f"adamw_{name}", [landed[k] for k in keys], given[name], given["m_" + name],
                                 given["v_" + name])

    for group in in_flight[:-1]:
        wait_group(group, grad_x)
    done = {"mlp_w_up": adamw_stack("mlp_w_up", ("up0", "up1")),
            "mlp_w_down": adamw_stack("mlp_w_down", ("down0", "down1")),
            "attn_w_qkv": [o.T[None] for o in adamw_item("w_qkv")],
            "attn_w_o": [o[None] for o in adamw_item("w_o")],
            "attn_b_qkv": adamw_item("b_qkv"), "attn_b_o": adamw_item("b_o"),
            "ssd_w_out": [o[None] for o in adamw_item("w_out")]}
    partials, = _all_gather("gather_small_grads", [pack_rep(grads, loss_row[0, :1])],
                            [outs4[0] for outs4 in done.values()])
    wait_group(in_flight[-1], partials)
    done["ssd_w_in"] = [o.T[None] for o in adamw_item("w_in")]
    done["ssd_conv_w"] = [o[None] for o in adamw_item("conv_w")]
    zero = jnp.zeros((1,), F32)
    rep_out = _sum_adamw("adamw_replicated", partials, pack_rep(w, zero), pack_rep(mom_m, zero), pack_rep(mom_v, zero))

    kinds = []
    for kind, r_arr in enumerate(rep_out):
        tree = {name: outs4[kind] for name, outs4 in done.items()}
        flat, off = r_arr.reshape(-1), 0
        for k in REPLICATED:
            tree[k] = flat[off:off + w[k].size].reshape(w[k].shape)
            off += w[k].size
        kinds.append(tree)
    loss = rep_out[0].reshape(-1)[off]
    outs = [loss, grad_x[None]]
    for tree in kinds:
        outs += [tree[k] for k in WEIGHTS]
    return tuple(outs)
```

```python
import jax
import jax.numpy as jnp
from jax import lax
from jax.experimental import pallas as pl
from jax.experimental.pallas import tpu as pltpu

F32 = jnp.float32
BF16 = jnp.bfloat16
PAYLOAD = jnp.bfloat16
HIGHEST = lax.Precision.HIGHEST
MESH = pl.DeviceIdType.MESH

NORM_EPS = 1e-6
SSD_HEAD_DIM = 64
SSD_HPG = 4
SSD_D_STATE = 128
SSD_CONV_WIDTH = 4
SSD_CHUNK = 128
ATTN_HEAD_DIM = 64
ATTN_N_KV = 4
ATTN_REP = 4
ATTN_WINDOW = 128
ADAM_LR = 0.001
ADAM_B1 = 0.9
ADAM_B2 = 0.999
ADAM_EPS = 1e-08
ADAM_WD = 0.01
ADAM_STEP = 10

N_DEV = 8
LANES = 128
V7X_VMEM_LIMIT = 56 * 1024 * 1024

GW = SSD_HPG * SSD_HEAD_DIM
GC = GW + 2 * SSD_D_STATE
assert SSD_CHUNK == LANES


def _params(*sem):
    return pltpu.CompilerParams(dimension_semantics=sem, vmem_limit_bytes=V7X_VMEM_LIMIT)


def _tile(n, pref, mult=LANES):
    best = None
    t = mult
    while t <= min(n, pref):
        if n % t == 0:
            best = t
        t += mult
    return best if best is not None else n


def _round_up(n, m):
    return (n + m - 1) // m * m


def _acc(ref, val, first):
    @pl.when(first)
    def _():
        ref[...] = val

    @pl.when(jnp.logical_not(first))
    def _():
        ref[...] += val


def _dot(a, b):
    return lax.dot_general(a, b, (((1,), (0,)), ((), ())), preferred_element_type=F32)


def _dot_nt(a, b):
    return lax.dot_general(a, b, (((1,), (1,)), ((), ())), preferred_element_type=F32)


def _dot_tn(a, b):
    return lax.dot_general(a, b, (((0,), (0,)), ((), ())), preferred_element_type=F32)


def _dot_f32(a, b):
    return lax.dot_general(a, b, (((1,), (0,)), ((), ())), preferred_element_type=F32, precision=HIGHEST)


_DOTS = {"nn": _dot, "nt": _dot_nt, "tn": _dot_tn}


def _sigmoid(x):
    return 1.0 / (1.0 + jnp.exp(-x))


def _softplus(x):
    return jnp.maximum(x, 0.0) + jnp.log1p(jnp.exp(-jnp.abs(x)))


def _silu_grad(x, s):
    return s * (1.0 + x * (1.0 - s))


def _mm(name, a_list, b_list, mode, *, tm, tn, out_dtypes=(F32,), epilogue=None, tiles=(), rows=(), cols=(),
        col_blocks=False, n_use=None, after=None):
    npair = len(a_list)
    if mode == "tn":
        m = a_list[0].shape[1]
    else:
        m = a_list[0].shape[0]
    n = n_use if n_use is not None else (b_list[0].shape[0] if mode == "nt" else b_list[0].shape[1])
    tm = _tile(m, tm, LANES if mode == "tn" else 8)
    tn = _tile(n, tn)
    assert m % tm == 0 and n % tn == 0, (name, m, n, tm, tn)
    dot = _DOTS[mode]

    def body(*refs):
        a_refs = refs[:npair]
        b_refs = refs[npair:2 * npair]
        n_extra = len(tiles) + len(rows) + len(cols)
        e_refs = refs[2 * npair:2 * npair + n_extra]
        o_refs = refs[2 * npair + n_extra + len(order):]
        acc = None
        for ar, br in zip(a_refs, b_refs):
            d = dot(ar[...], br[...])
            acc = d if acc is None else acc + d
        outs = epilogue(acc, *[e[...] for e in e_refs]) if epilogue is not None else (acc,)
        for o, v in zip(o_refs, outs):
            o[...] = v.astype(o.dtype)

    in_specs = []
    for a in a_list:
        if mode == "tn":
            in_specs.append(pl.BlockSpec((a.shape[0], tm), lambda i, j: (0, i)))
        else:
            in_specs.append(pl.BlockSpec((tm, a.shape[1]), lambda i, j: (i, 0)))
    for b in b_list:
        if mode == "nt":
            in_specs.append(pl.BlockSpec((tn, b.shape[1]), lambda i, j: (j, 0)))
        else:
            in_specs.append(pl.BlockSpec((b.shape[0], tn), lambda i, j: (0, j)))
    in_specs += [pl.BlockSpec((tm, tn), lambda i, j: (i, j)) for _ in tiles]
    in_specs += [pl.BlockSpec((1, tn), lambda i, j: (0, j)) for _ in rows]
    in_specs += [pl.BlockSpec((tm, 1), lambda i, j: (i, 0)) for _ in cols]
    order = [] if after is None else [after]
    in_specs += [pl.BlockSpec((8, LANES), lambda i, j: (0, 0)) for _ in order]
    outs = pl.pallas_call(
        body,
        name=name,
        grid=(m // tm, n // tn),
        in_specs=in_specs,
        out_specs=[pl.BlockSpec((None, tm, tn), lambda i, j: (j, i, 0)) if col_blocks else
                   pl.BlockSpec((tm, tn), lambda i, j: (i, j)) for _ in out_dtypes],
        out_shape=[jax.ShapeDtypeStruct((n // tn, m, tn) if col_blocks else (m, n), dt) for dt in out_dtypes],
        compiler_params=_params("parallel", "parallel"),
    )(*a_list, *b_list, *tiles, *rows, *cols, *order)
    return outs[0] if len(out_dtypes) == 1 else outs


def _rms(x, w):
    r = lax.rsqrt(jnp.mean(x * x, axis=-1, keepdims=True) + NORM_EPS)
    return x * r * w


def _rms_bwd(x, w, dy):
    r = lax.rsqrt(jnp.mean(x * x, axis=-1, keepdims=True) + NORM_EPS)
    xh = x * r
    g = dy * w
    dx = r * (g - xh * jnp.mean(g * xh, axis=-1, keepdims=True))
    return dx, dy * xh


def _row_specs(tr, d):
    return pl.BlockSpec((tr, d), lambda i: (i, 0)), pl.BlockSpec((1, d), lambda i: (0, 0))


def _prenorm(name, h, w, after):
    t, d = h.shape
    tr = _tile(t, 512, 8)
    row, vec = _row_specs(tr, d)

    def body(h_ref, w_ref, after_ref, u_ref):
        u_ref[...] = _rms(h_ref[...], w_ref[...]).astype(BF16)

    return pl.pallas_call(body, name=name, grid=(t // tr,),
                          in_specs=[row, vec, pl.BlockSpec((8, LANES), lambda i: (0, 0))], out_specs=row,
                          out_shape=jax.ShapeDtypeStruct((t, d), BF16), compiler_params=_params("parallel"))(
                              h, w, after)


def _post_pre(name, h, m, w_post, w_pre):
    t, d = h.shape
    tr = _tile(t, 512, 8)
    row, vec = _row_specs(tr, d)

    def body(h_ref, m_ref, wq_ref, wp_ref, hn_ref, u_ref):
        hn = h_ref[...] + _rms(m_ref[...], wq_ref[...])
        hn_ref[...] = hn
        u_ref[...] = _rms(hn, wp_ref[...]).astype(BF16)

    return pl.pallas_call(body, name=name, grid=(t // tr,), in_specs=[row, row, vec, vec], out_specs=[row, row],
                          out_shape=[jax.ShapeDtypeStruct((t, d), F32), jax.ShapeDtypeStruct((t, d), BF16)],
                          compiler_params=_params("parallel"))(h, m, w_post, w_pre)


def _final_loss(name, h, m, w_post, target):
    t, d = h.shape
    tr = _tile(t, 512, 8)
    row, vec = _row_specs(tr, d)

    def body(h_ref, m_ref, wq_ref, t_ref, dh_ref, loss_ref):
        err = h_ref[...] + _rms(m_ref[...], wq_ref[...]) - t_ref[...]
        dh_ref[...] = err * (1.0 / d)
        part = 0.5 * jnp.sum(jnp.mean(err * err, axis=-1, keepdims=True), axis=0, keepdims=True)
        _acc(loss_ref, jnp.broadcast_to(part, (1, LANES)), pl.program_id(0) == 0)

    return pl.pallas_call(body, name=name, grid=(t // tr,), in_specs=[row, row, vec, row],
                          out_specs=[row, pl.BlockSpec((1, LANES), lambda i: (0, 0))],
                          out_shape=[jax.ShapeDtypeStruct((t, d), F32), jax.ShapeDtypeStruct((1, LANES), F32)],
                          compiler_params=_params("arbitrary"))(h, m, w_post, target)


def _norm_bwd(name, dh, pre=None, post=None, after=None):
    t, d = dh.shape
    tr = _tile(t, 512, 8)
    row, vec = _row_specs(tr, d)
    has_pre, has_post = pre is not None, post is not None

    def body(*refs):
        it = iter(refs)
        dh_ref = next(it)
        if has_pre:
            du_ref, x_ref, wp_ref = next(it), next(it), next(it)
        if has_post:
            m_ref, wq_ref = next(it), next(it)
        if after is not None:
            next(it)
        first = pl.program_id(0) == 0
        dh_v = dh_ref[...]
        if has_pre:
            dhn_ref, dwp_ref = next(it), next(it)
            dx, dwr = _rms_bwd(x_ref[...], wp_ref[...], du_ref[...])
            dh_v = dh_v + dx
            dhn_ref[...] = dh_v
            _acc(dwp_ref, jnp.sum(dwr, axis=0, keepdims=True), first)
        if has_post:
            dm_ref, dwq_ref, dms_ref = next(it), next(it), next(it)
            dm, dwr = _rms_bwd(m_ref[...], wq_ref[...], dh_v)
            dm_ref[...] = dm.astype(BF16)
            _acc(dwq_ref, jnp.sum(dwr, axis=0, keepdims=True), first)
            _acc(dms_ref, jnp.sum(dm, axis=0, keepdims=True), first)

    ins, in_specs, out_specs, out_shape = [dh], [row], [], []
    if has_pre:
        ins += list(pre)
        in_specs += [row, row, vec]
        out_specs += [row, vec]
        out_shape += [jax.ShapeDtypeStruct((t, d), F32), jax.ShapeDtypeStruct((1, d), F32)]
    if has_post:
        ins += list(post)
        in_specs += [row, vec]
        out_specs += [row, vec, vec]
        out_shape += [jax.ShapeDtypeStruct((t, d), BF16), jax.ShapeDtypeStruct((1, d), F32),
                      jax.ShapeDtypeStruct((1, d), F32)]
    if after is not None:
        ins.append(after)
        in_specs.append(pl.BlockSpec((8, LANES), lambda i: (0, 0)))
    return pl.pallas_call(body, name=name, grid=(t // tr,), in_specs=in_specs, out_specs=out_specs,
                          out_shape=out_shape, compiler_params=_params("arbitrary"))(*ins)


HALO = 8


def _shift_later(cur, prev, s):
    rolled = pltpu.roll(cur, s, 0)
    row = lax.broadcasted_iota(jnp.int32, prev.shape, 0)
    first = jnp.where(row < s, pltpu.roll(prev, s, 0), rolled[0:HALO])
    return jnp.concatenate([first, rolled[HALO:]], axis=0)


def _shift_earlier(cur, nxt, s):
    tt = cur.shape[0]
    rolled = pltpu.roll(cur, tt - s, 0)
    row = lax.broadcasted_iota(jnp.int32, nxt.shape, 0)
    last = jnp.where(row >= HALO - s, pltpu.roll(nxt, HALO - s, 0), rolled[tt - HALO:])
    return jnp.concatenate([rolled[:tt - HALO], last], axis=0)


def _conv_fwd(zx, col0, n_ch, conv_w, conv_b):
    t = zx.shape[0]
    tc = _tile(n_ch, 512)
    tt = _tile(t, 1024, 8)
    cb0 = col0 // tc
    assert col0 % tc == 0
    kw = SSD_CONV_WIDTH

    def body(x_ref, p_ref, w_ref, b_ref, o_ref):
        cur = x_ref[...]
        prev = jnp.where(pl.program_id(1) > 0, p_ref[...], 0.0)
        w = w_ref[...]
        acc = b_ref[...] + w[kw - 1:kw, :] * cur
        for k in range(kw - 1):
            acc = acc + w[k:k + 1, :] * _shift_later(cur, prev, kw - 1 - k)
        o_ref[...] = acc

    return pl.pallas_call(
        body, name="ssd_conv_fwd", grid=(n_ch // tc, t // tt),
        in_specs=[pl.BlockSpec((tt, tc), lambda j, i: (i, cb0 + j)),
                  pl.BlockSpec((HALO, tc), lambda j, i: (jnp.maximum(i * (tt // HALO) - 1, 0), cb0 + j)),
                  pl.BlockSpec((kw, tc), lambda j, i: (0, j)),
                  pl.BlockSpec((1, tc), lambda j, i: (0, j))],
        out_specs=pl.BlockSpec((tt, tc), lambda j, i: (i, j)),
        out_shape=jax.ShapeDtypeStruct((t, n_ch), F32),
        compiler_params=_params("parallel", "parallel"))(zx, zx, conv_w, conv_b)


def _conv_bwd(name, dpre, zx, col0, conv_w):
    t, n_ch = dpre.shape
    tc = _tile(n_ch, 512)
    tt = _tile(t, 1024, 8)
    cb0 = col0 // tc
    kw = SSD_CONV_WIDTH
    nt = t // tt

    def body(d_ref, dn_ref, x_ref, p_ref, w_ref, dx_ref, dw_ref, db_ref):
        i = pl.program_id(1)
        d = d_ref[...]
        d_next = jnp.where(i < nt - 1, dn_ref[...], 0.0)
        x = x_ref[...]
        x_prev = jnp.where(i > 0, p_ref[...], 0.0)
        w = w_ref[...]
        dx = w[kw - 1:kw, :] * d
        for k in range(kw - 1):
            dx = dx + w[k:k + 1, :] * _shift_earlier(d, d_next, kw - 1 - k)
        dx_ref[...] = dx.astype(BF16)
        first = i == 0
        for k in range(kw):
            xs = x if k == kw - 1 else _shift_later(x, x_prev, kw - 1 - k)
            val = jnp.sum(d * xs, axis=0, keepdims=True)

            @pl.when(first)
            def _():
                dw_ref[k:k + 1, :] = val

            @pl.when(jnp.logical_not(first))
            def _():
                dw_ref[k:k + 1, :] += val
        _acc(db_ref, jnp.sum(d, axis=0, keepdims=True), first)

    return pl.pallas_call(
        body, name=name, grid=(n_ch // tc, nt),
        in_specs=[pl.BlockSpec((tt, tc), lambda j, i: (i, j)),
                  pl.BlockSpec((HALO, tc), lambda j, i: (jnp.minimum((i + 1) * (tt // HALO), t // HALO - 1), j)),
                  pl.BlockSpec((tt, tc), lambda j, i: (i, cb0 + j)),
                  pl.BlockSpec((HALO, tc), lambda j, i: (jnp.maximum(i * (tt // HALO) - 1, 0), cb0 + j)),
                  pl.BlockSpec((kw, tc), lambda j, i: (0, j))],
        out_specs=[pl.BlockSpec((tt, tc), lambda j, i: (i, j)),
                   pl.BlockSpec((kw, tc), lambda j, i: (0, j)),
                   pl.BlockSpec((1, tc), lambda j, i: (0, j))],
        out_shape=[jax.ShapeDtypeStruct((t, n_ch), BF16), jax.ShapeDtypeStruct((kw, n_ch), F32),
                   jax.ShapeDtypeStruct((1, n_ch), F32)],
        compiler_params=_params("parallel", "arbitrary"))(dpre, dpre, zx, zx, conv_w)


def _head_of_lane(shape, width):
    return lax.broadcasted_iota(jnp.int32, shape, len(shape) - 1) // width


def _select_dot(v, pick, pick_first=False):
    hi = v.astype(BF16)
    lo = (v - hi.astype(F32)).astype(BF16)
    return _dot(pick, hi) + _dot(pick, lo) if pick_first else _dot(hi, pick) + _dot(lo, pick)


def _expand(v, n_rows, on_mxu=False):
    if not on_mxu:
        head = _head_of_lane((n_rows, GW), SSD_HEAD_DIM)
        out = jnp.zeros((n_rows, GW), F32)
        for j in range(SSD_HPG):
            out = jnp.where(head == j, v[:, j:j + 1], out)
        return out
    src = lax.broadcasted_iota(jnp.int32, (LANES, GW), 0)
    return _select_dot(v, (src == _head_of_lane((LANES, GW), SSD_HEAD_DIM)).astype(BF16))


def _contract(v, n_rows, on_mxu=False):
    if not on_mxu:
        head = _head_of_lane((n_rows, GW), SSD_HEAD_DIM)
        lane = lax.broadcasted_iota(jnp.int32, (n_rows, LANES), 1)
        out = jnp.zeros((n_rows, LANES), F32)
        for j in range(SSD_HPG):
            s = jnp.sum(jnp.where(head == j, v, 0.0), axis=1, keepdims=True)
            out = jnp.where(lane == j, s, out)
        return out
    dst = lax.broadcasted_iota(jnp.int32, (GW, LANES), 1)
    return _select_dot(v, (lax.broadcasted_iota(jnp.int32, (GW, LANES), 0) // SSD_HEAD_DIM == dst).astype(BF16))


def _ssd_dt_prep(zdt, bias, alog, ng):
    t = zdt.shape[0]
    q = SSD_CHUNK

    def body(z_ref, b_ref, a_ref, dt_ref, cum_ref, cumr_ref, sg_ref):
        raw = z_ref[...] + b_ref[...]
        dt = _softplus(raw)
        sgd = _sigmoid(raw)
        row = lax.broadcasted_iota(jnp.int32, (q, q), 0)
        col = lax.broadcasted_iota(jnp.int32, (q, q), 1)
        cum = _dot_f32((col <= row).astype(F32), dt * (-jnp.exp(a_ref[...])))
        cum_t = cum.T
        lane = lax.broadcasted_iota(jnp.int32, (q, LANES), 1)
        for g in range(ng):
            shift = (LANES - g * SSD_HPG) % LANES

            def group(v):
                return jnp.where(lane < SSD_HPG, pltpu.roll(v, shift, 1) if shift else v, 0.0)

            dt_ref[g] = group(dt)
            cum_ref[g] = group(cum)
            sg_ref[g] = group(sgd)
            cumr_ref[g] = (pltpu.roll(cum_t, shift, 0) if shift else cum_t)[0:8, :]

    cols = pl.BlockSpec((ng, q, LANES), lambda c: (0, c, 0))
    vec = pl.BlockSpec((1, LANES), lambda c: (0, 0))
    col_shape = jax.ShapeDtypeStruct((ng, t, LANES), F32)
    return pl.pallas_call(body, name="ssd_dt_prep", grid=(t // q,),
                          in_specs=[pl.BlockSpec((q, LANES), lambda c: (c, 0)), vec, vec],
                          out_specs=[cols, cols, pl.BlockSpec((ng, 8, q), lambda c: (0, 0, c)), cols],
                          out_shape=[col_shape, col_shape, jax.ShapeDtypeStruct((ng, 8, t), F32), col_shape],
                          compiler_params=_params("parallel"))(zdt, bias, alog)


def _ssd_common(pre, dt, cum, cum_r, alog_c, on_mxu):
    q = SSD_CHUNK
    sg = _sigmoid(pre)
    act = pre * sg
    xa = act[:, :GW]
    bm = act[:, GW:GW + SSD_D_STATE].astype(BF16)
    cm = act[:, GW + SSD_D_STATE:].astype(BF16)
    row = lax.broadcasted_iota(jnp.int32, (q, q), 0)
    col = lax.broadcasted_iota(jnp.int32, (q, q), 1)
    tril = col <= row
    a_c = -jnp.exp(alog_c)
    g = _dot_nt(cm, bm)
    dt_x = _expand(dt, q, on_mxu)
    xdt = xa * dt_x
    cl = cum[q - 1:q, :]
    e_c = jnp.exp(cl - cum)
    lam_c = jnp.exp(cum)
    return dict(sg=sg, xa=xa, bm=bm, cm=cm, tril=tril, row=row, col=col, dt=dt, a_c=a_c, cum=cum, cum_r=cum_r,
                g=g, dt_x=dt_x, xdt=xdt, cl=cl, e_c=e_c, lam_c=lam_c)


SSD_GPS_FWD = 8
SSD_GPS_BWD = 2


def _ssd_specs(nc, rev, ng, gps):
    q = SSD_CHUNK
    xw, nw = gps * GW, gps * SSD_D_STATE
    b_off = ng * GW // nw
    c_off = (ng * GW + ng * SSD_D_STATE) // nw
    assert ng % gps == 0 and (ng * GW) % nw == 0 and (ng * SSD_D_STATE) % nw == 0

    def ch(c):
        return nc - 1 - c if rev else c

    chunk_grp = [pl.BlockSpec((q, xw), lambda g, c: (ch(c), g)),
                 pl.BlockSpec((q, nw), lambda g, c: (ch(c), b_off + g)),
                 pl.BlockSpec((q, nw), lambda g, c: (ch(c), c_off + g))]
    col_form = pl.BlockSpec((gps, q, LANES), lambda g, c: (g, ch(c), 0))
    row_form = pl.BlockSpec((gps, 8, q), lambda g, c: (g, 0, ch(c)))
    col_par = pl.BlockSpec((gps, 1, LANES), lambda g, c: (g, 0, 0))
    y_spec = pl.BlockSpec((q, xw), lambda g, c: (ch(c), g))
    st_spec = pl.BlockSpec((gps, None, GW, SSD_D_STATE), lambda g, c: (g, ch(c), 0, 0))
    bc_spec = pl.BlockSpec((q, nw), lambda g, c: (ch(c), g))
    return chunk_grp, col_form, row_form, col_par, y_spec, st_spec, bc_spec


def _ssd_group_views(gi, wide, narrow, stacked):
    xs, ns = pl.ds(gi * GW, GW), pl.ds(gi * SSD_D_STATE, SSD_D_STATE)
    return [r.at[:, xs] for r in wide], [r.at[:, ns] for r in narrow], [r.at[gi] for r in stacked]


def _ssd_fwd(pre, dt_c, cum_c, cum_r, alog_c, dsk_c):
    t = pre.shape[0]
    ng = pre.shape[1] // GC
    q = SSD_CHUNK
    nc = t // q
    gps = SSD_GPS_FWD if ng % SSD_GPS_FWD == 0 else SSD_GPS_BWD
    chunk_grp, col_form, row_form, col_par, y_spec, st_spec, _ = _ssd_specs(nc, False, ng, gps)

    def body(px_ref, pb_ref, pc_ref, dt_ref, cum_ref, cumr_ref, ac_ref, dk_ref, y_ref, sp_ref, st_ref):
        @pl.when(pl.program_id(1) == 0)
        def _():
            st_ref[...] = jnp.zeros_like(st_ref)

        for gi in range(gps):
            (px, y), (pb, pc), rest = _ssd_group_views(
                gi, (px_ref, y_ref), (pb_ref, pc_ref), (dt_ref, cum_ref, cumr_ref, ac_ref, dk_ref, sp_ref, st_ref))
            one_group(px, pb, pc, *rest[:5], y, *rest[5:])

    def one_group(px_ref, pb_ref, pc_ref, dt_ref, cum_ref, cumr_ref, ac_ref, dk_ref, y_ref, sp_ref, st_ref):
        pre_v = jnp.concatenate([px_ref[...], pb_ref[...], pc_ref[...]], axis=1)
        v = _ssd_common(pre_v, dt_ref[...], cum_ref[...], cumr_ref[...], ac_ref[...], False)
        s0 = st_ref[...]
        sp_ref[...] = s0
        r = _dot_nt(v["cm"], s0.astype(BF16))
        y = _expand(v["lam_c"], q) * r + _expand(dk_ref[...], 1) * v["xa"]
        head = _head_of_lane((q, GW), SSD_HEAD_DIM)
        for j in range(SSD_HPG):
            diff = v["cum"][:, j:j + 1] - v["cum_r"][j:j + 1, :]
            w = (v["g"] * jnp.exp(jnp.where(v["tril"], diff, -jnp.inf))).astype(BF16)
            y = y + _dot(w, jnp.where(head == j, v["xdt"], 0.0).astype(BF16))
        y_ref[...] = y
        ds = _dot_tn((v["xdt"] * _expand(v["e_c"], q)).astype(BF16), v["bm"])
        for j in range(SSD_HPG):
            rows = slice(j * SSD_HEAD_DIM, (j + 1) * SSD_HEAD_DIM)
            st_ref[rows, :] = s0[rows, :] * jnp.exp(v["cum_r"][j:j + 1, q - 1:q]) + ds[rows, :]

    return pl.pallas_call(
        body, name="ssd_scan_fwd", grid=(ng // gps, nc),
        in_specs=chunk_grp + [col_form, col_form, row_form, col_par, col_par],
        out_specs=[y_spec, st_spec],
        out_shape=[jax.ShapeDtypeStruct((t, ng * GW), F32), jax.ShapeDtypeStruct((ng, nc, GW, SSD_D_STATE), F32)],
        scratch_shapes=[pltpu.VMEM((gps, GW, SSD_D_STATE), F32)],
        compiler_params=_params("parallel", "arbitrary"))(pre, pre, pre, dt_c, cum_c, cum_r, alog_c, dsk_c)


def _ssd_bwd(dy, pre, states, dt_c, cum_c, cum_r, sgd_c, alog_c, dsk_c):
    t = pre.shape[0]
    ng = pre.shape[1] // GC
    q = SSD_CHUNK
    nc = t // q
    gps = SSD_GPS_BWD
    chunk_grp, col_form, row_form, col_par, y_spec, st_spec, bc_spec = _ssd_specs(nc, True, ng, gps)

    def body(dy_ref, px_ref, pb_ref, pc_ref, sp_ref, dt_ref, cum_ref, cumr_ref, sgd_ref, ac_ref, dk_ref,
             dpx_ref, dpb_ref, dpc_ref, ddt_ref, dbias_ref, dalog_ref, dd_ref, ds_ref):
        @pl.when(pl.program_id(1) == 0)
        def _():
            ds_ref[...] = jnp.zeros_like(ds_ref)

        for gi in range(gps):
            (dy, px, dpx), (pb, pc, dpb, dpc), rest = _ssd_group_views(
                gi, (dy_ref, px_ref, dpx_ref), (pb_ref, pc_ref, dpb_ref, dpc_ref),
                (sp_ref, dt_ref, cum_ref, cumr_ref, sgd_ref, ac_ref, dk_ref, ddt_ref, dbias_ref, dalog_ref, dd_ref,
                 ds_ref))
            one_group(dy, px, pb, pc, *rest[:7], dpx, dpb, dpc, *rest[7:])

    def one_group(dy_ref, px_ref, pb_ref, pc_ref, sp_ref, dt_ref, cum_ref, cumr_ref, sgd_ref, ac_ref, dk_ref,
                  dpx_ref, dpb_ref, dpc_ref, ddt_ref, dbias_ref, dalog_ref, dd_ref, ds_ref):
        first = pl.program_id(1) == 0
        pre_v = jnp.concatenate([px_ref[...], pb_ref[...], pc_ref[...]], axis=1)
        v = _ssd_common(pre_v, dt_ref[...], cum_ref[...], cumr_ref[...], ac_ref[...], True)
        xa, bm, cm, xdt, cum, cum_r = v["xa"], v["bm"], v["cm"], v["xdt"], v["cum"], v["cum_r"]
        xdt_b = xdt.astype(BF16)
        dy_v = dy_ref[...]
        s0 = sp_ref[...]
        ds1 = ds_ref[...]
        s0b, ds1b = s0.astype(BF16), ds1.astype(BF16)
        head = _head_of_lane((q, GW), SSD_HEAD_DIM)
        lane = lax.broadcasted_iota(jnp.int32, (q, LANES), 1)
        lane1 = lax.broadcasted_iota(jnp.int32, (1, LANES), 1)
        lam_x = _expand(v["lam_c"], q, True)
        e_x = _expand(v["e_c"], q, True)

        dxa = _expand(dk_ref[...], 1) * dy_v
        dd = _contract(jnp.sum(dy_v * xa, axis=0, keepdims=True), 1)
        r = _dot_nt(cm, s0b)
        dcum = _contract(dy_v * r * lam_x, q, True)
        drb = (lam_x * dy_v).astype(BF16)
        dc = _dot(drb, s0b)
        ds0 = _dot_tn(drb, cm)
        extra = jnp.zeros((1, LANES), F32)
        for j in range(SSD_HPG):
            rows = slice(j * SSD_HEAD_DIM, (j + 1) * SSD_HEAD_DIM)
            lam_last = jnp.exp(cum_r[j:j + 1, q - 1:q])
            ds_ref[rows, :] = ds0[rows, :] + lam_last * ds1[rows, :]
            tot = jnp.sum(jnp.sum(ds1[rows, :] * s0[rows, :], axis=1, keepdims=True), axis=0, keepdims=True)
            extra = jnp.where(lane1 == j, lam_last * tot, extra)
        dv = _dot_nt(bm, ds1b)
        db = _dot((xdt * e_x).astype(BF16), ds1b)
        dxdt = e_x * dv
        dee = _contract(dv * xdt, q, True) * v["e_c"]
        dcum = dcum - dee
        extra = extra + jnp.sum(dee, axis=0, keepdims=True)
        dg = jnp.zeros((q, q), F32)
        col_sums = jnp.zeros((q, q), F32)
        for j in range(SSD_HPG):
            diff = cum[:, j:j + 1] - cum_r[j:j + 1, :]
            el = jnp.exp(jnp.where(v["tril"], diff, -jnp.inf))
            gl = v["g"] * el
            dym = jnp.where(head == j, dy_v, 0.0).astype(BF16)
            dwm = _dot_nt(dym, xdt_b)
            dxdt = dxdt + _dot_tn(gl.astype(BF16), dym)
            z = dwm * gl
            dcum = jnp.where(lane == j, dcum + jnp.sum(z, axis=1, keepdims=True), dcum)
            col_sums = jnp.where(v["row"] == j, jnp.sum(z, axis=0, keepdims=True), col_sums)
            dg = dg + dwm * el
        dcum = dcum - col_sums.T
        dgb = dg.astype(BF16)
        dc = dc + _dot(dgb, bm)
        db = db + _dot_tn(dgb, cm)
        da = _select_dot(dcum, (v["row"] <= v["col"]).astype(BF16), True) + extra
        ddt = _contract(dxdt * xa, q, True) + v["a_c"] * da
        dalog = jnp.sum(v["dt"] * da, axis=0, keepdims=True) * v["a_c"]
        dxa = dxa + v["dt_x"] * dxdt
        ddt_raw = jnp.where(lane < SSD_HPG, ddt * sgd_ref[...], 0.0)
        sgrad = _silu_grad(pre_v, v["sg"])
        dpx_ref[...] = dxa * sgrad[:, :GW]
        dpb_ref[...] = db * sgrad[:, GW:GW + SSD_D_STATE]
        dpc_ref[...] = dc * sgrad[:, GW + SSD_D_STATE:]
        ddt_ref[...] = ddt_raw
        _acc(dbias_ref, jnp.sum(ddt_raw, axis=0, keepdims=True), first)
        _acc(dalog_ref, jnp.where(lane1 < SSD_HPG, dalog, 0.0), first)
        _acc(dd_ref, dd, first)

    return pl.pallas_call(
        body, name="ssd_scan_bwd", grid=(ng // gps, nc),
        in_specs=[y_spec] + chunk_grp + [st_spec, col_form, col_form, row_form, col_form, col_par, col_par],
        out_specs=[y_spec, bc_spec, bc_spec, col_form, col_par, col_par, col_par],
        out_shape=[jax.ShapeDtypeStruct((t, ng * GW), F32), jax.ShapeDtypeStruct((t, ng * SSD_D_STATE), F32),
                   jax.ShapeDtypeStruct((t, ng * SSD_D_STATE), F32), jax.ShapeDtypeStruct((ng, t, LANES), F32),
                   jax.ShapeDtypeStruct((ng, 1, LANES), F32), jax.ShapeDtypeStruct((ng, 1, LANES), F32),
                   jax.ShapeDtypeStruct((ng, 1, LANES), F32)],
        scratch_shapes=[pltpu.VMEM((gps, GW, SSD_D_STATE), F32)],
        compiler_params=_params("parallel", "arbitrary"))(dy, pre, pre, pre, states, dt_c, cum_c, cum_r, sgd_c, alog_c,
                                                           dsk_c)


def _gate_norm_fwd(y, zx, norm_w):
    t, di = y.shape
    tr = _tile(t, 512, 8)
    ng = di // GW

    def body(y_ref, z_ref, w_ref, o_ref):
        z = z_ref[...]
        gate = y_ref[...] * (z * _sigmoid(z))
        w = w_ref[...]
        for g in range(ng):
            cols = slice(g * GW, (g + 1) * GW)
            gs = gate[:, cols]
            r = lax.rsqrt(jnp.mean(gs * gs, axis=-1, keepdims=True) + NORM_EPS)
            o_ref[:, cols] = (gs * r * w[:, cols]).astype(BF16)

    row = pl.BlockSpec((tr, di), lambda i: (i, 0))
    return pl.pallas_call(body, name="ssd_gate_norm_fwd", grid=(t // tr,),
                          in_specs=[row, row, pl.BlockSpec((1, di), lambda i: (0, 0))], out_specs=row,
                          out_shape=jax.ShapeDtypeStruct((t, di), BF16), compiler_params=_params("parallel"))(
                              y, zx, norm_w)


def _gate_norm_bwd(dyn, y, zx, norm_w, after):
    t, di = y.shape
    tr = _tile(t, 256, 8)
    ng = di // GW

    def body(d_ref, y_ref, z_ref, w_ref, after_ref, dy_ref, dz_ref, dw_ref):
        z = z_ref[...]
        yv = y_ref[...]
        sg = _sigmoid(z)
        sz = z * sg
        gate = yv * sz
        w = w_ref[...]
        d = d_ref[...]
        dsz = _silu_grad(z, sg)
        dws = []
        for g in range(ng):
            cols = slice(g * GW, (g + 1) * GW)
            dg, dwr = _rms_bwd(gate[:, cols], w[:, cols], d[:, cols])
            dy_ref[:, cols] = dg * sz[:, cols]
            dz_ref[:, cols] = (dg * yv[:, cols] * dsz[:, cols]).astype(BF16)
            dws.append(jnp.sum(dwr, axis=0, keepdims=True))
        first = pl.program_id(0) == 0
        for g in range(ng):
            cols = slice(g * GW, (g + 1) * GW)

            @pl.when(first)
            def _():
                dw_ref[:, cols] = dws[g]

            @pl.when(jnp.logical_not(first))
            def _():
                dw_ref[:, cols] += dws[g]

    row = pl.BlockSpec((tr, di), lambda i: (i, 0))
    vec = pl.BlockSpec((1, di), lambda i: (0, 0))
    return pl.pallas_call(body, name="ssd_gate_norm_bwd", grid=(t // tr,),
                          in_specs=[row, row, row, vec, pl.BlockSpec((8, LANES), lambda i: (0, 0))],
                          out_specs=[row, row, vec],
                          out_shape=[jax.ShapeDtypeStruct((t, di), F32), jax.ShapeDtypeStruct((t, di), BF16),
                                     jax.ShapeDtypeStruct((1, di), F32)],
                          compiler_params=_params("arbitrary"))(dyn, y, zx, norm_w, after)


def _attn_mask_t(n):
    w = ATTN_WINDOW
    kpos = lax.broadcasted_iota(jnp.int32, (2 * w, ATTN_REP * w), 0)
    qpos = lax.broadcasted_iota(jnp.int32, (2 * w, ATTN_REP * w), 1) % w + w
    rel = qpos - kpos
    return (rel >= 0) & (rel < w) & jnp.logical_not((n == 0) & (kpos < w))


def _attn_probs_t(qts, ktb, mask, sink):
    s = _dot_tn(ktb, qts) * (ATTN_HEAD_DIM ** -0.5)
    s = jnp.where(mask, s, -jnp.inf)
    m = jnp.maximum(jnp.max(s, axis=0, keepdims=True), sink)
    e = jnp.exp(s - m)
    es = jnp.exp(sink - m)
    inv = 1.0 / (jnp.sum(e, axis=0, keepdims=True) + es)
    return e * inv, es * inv


def _attn_blocks_t(kv, q_ref, kc_ref, vc_ref, kp_ref, vp_ref):
    hd = ATTN_HEAD_DIM
    rows = slice(kv * hd, (kv + 1) * hd)
    ktb = jnp.concatenate([kp_ref[rows, :], kc_ref[rows, :]], axis=1)
    vtb = jnp.concatenate([vp_ref[rows, :], vc_ref[rows, :]], axis=1)
    qts = jnp.concatenate([q_ref[(kv * ATTN_REP + r) * hd:(kv * ATTN_REP + r + 1) * hd, :]
                           for r in range(ATTN_REP)], axis=1)
    return qts, ktb, vtb


def _attn_specs_t(nb, cur, prev):
    w, hd = ATTN_WINDOW, ATTN_HEAD_DIM
    kd = ATTN_N_KV * hd
    qd = ATTN_REP * kd
    return [pl.BlockSpec((qd, w), lambda n: (0, cur(n))),
            pl.BlockSpec((kd, w), lambda n: (ATTN_REP, cur(n))),
            pl.BlockSpec((kd, w), lambda n: (ATTN_REP + 1, cur(n))),
            pl.BlockSpec((kd, w), lambda n: (ATTN_REP, prev(n))),
            pl.BlockSpec((kd, w), lambda n: (ATTN_REP + 1, prev(n)))]


def _attn_fwd_t(qkv_t, sinks_rep):
    t = qkv_t.shape[1]
    w, hd = ATTN_WINDOW, ATTN_HEAD_DIM
    qd = ATTN_N_KV * ATTN_REP * hd
    nb = t // w

    def body(q_ref, kc_ref, vc_ref, kp_ref, vp_ref, s_ref, o_ref):
        mask = _attn_mask_t(pl.program_id(0))
        for kv in range(ATTN_N_KV):
            qts, ktb, vtb = _attn_blocks_t(kv, q_ref, kc_ref, vc_ref, kp_ref, vp_ref)
            p, _ = _attn_probs_t(qts, ktb, mask, s_ref[kv])
            ots = _dot(vtb, p.astype(BF16))
            for r in range(ATTN_REP):
                h = kv * ATTN_REP + r
                o_ref[h * hd:(h + 1) * hd, :] = ots[:, r * w:(r + 1) * w].astype(BF16)

    return pl.pallas_call(
        body, name="attn_fwd", grid=(nb,),
        in_specs=_attn_specs_t(nb, lambda n: n, lambda n: jnp.maximum(n - 1, 0)) + [
            pl.BlockSpec(sinks_rep.shape, lambda n: (0, 0, 0))],
        out_specs=pl.BlockSpec((qd, w), lambda n: (0, n)),
        out_shape=jax.ShapeDtypeStruct((qd, t), BF16),
        compiler_params=_params("parallel"))(qkv_t, qkv_t, qkv_t, qkv_t, qkv_t, sinks_rep)


def _attn_bwd_t(qkv_t, do_t, sinks_rep):
    t = qkv_t.shape[1]
    w, hd = ATTN_WINDOW, ATTN_HEAD_DIM
    kd = ATTN_N_KV * hd
    qd = ATTN_REP * kd
    nq = ATTN_N_KV * ATTN_REP
    nb = t // w
    rows_all = qd + 2 * kd

    def body(q_ref, kc_ref, vc_ref, kp_ref, vp_ref, do_ref, s_ref, dqkv_ref, bsum_ref, dsk_ref,
             carry_ref, new_ref, bacc_ref, sacc_ref):
        n = pl.program_id(0)

        @pl.when(n == 0)
        def _():
            carry_ref[...] = jnp.zeros_like(carry_ref)
            bacc_ref[...] = jnp.zeros_like(bacc_ref)
            sacc_ref[...] = jnp.zeros_like(sacc_ref)

        @pl.when(n < nb)
        def _():
            mask = _attn_mask_t(n)
            for kv in range(ATTN_N_KV):
                qts, ktb, vtb = _attn_blocks_t(kv, q_ref, kc_ref, vc_ref, kp_ref, vp_ref)
                dots = jnp.concatenate([do_ref[(kv * ATTN_REP + r) * hd:(kv * ATTN_REP + r + 1) * hd, :]
                                        for r in range(ATTN_REP)], axis=1)
                p, ps = _attn_probs_t(qts, ktb, mask, s_ref[kv])
                dpt = _dot_tn(vtb, dots)
                delta = jnp.sum(p * dpt, axis=0, keepdims=True)
                dst = (p * (dpt - delta) * (hd ** -0.5)).astype(BF16)
                dqts = _dot(ktb, dst)
                for r in range(ATTN_REP):
                    h = kv * ATTN_REP + r
                    new_ref[h * hd:(h + 1) * hd, :] = dqts[:, r * w:(r + 1) * w]
                dktb = _dot_nt(qts, dst)
                dvtb = _dot_nt(dots, p.astype(BF16))
                krows = slice(qd + kv * hd, qd + (kv + 1) * hd)
                vrows = slice(qd + kd + kv * hd, qd + kd + (kv + 1) * hd)
                carry_ref[krows, :] += dktb[:, :w]
                carry_ref[vrows, :] += dvtb[:, :w]
                new_ref[krows, :] = dktb[:, w:]
                new_ref[vrows, :] = dvtb[:, w:]
                sacc_ref[kv] += -(ps * delta)

        @pl.when(n >= 1)
        def _():
            done = carry_ref[...]
            dqkv_ref[...] = done.astype(BF16)
            bacc_ref[...] += done

        @pl.when(n < nb)
        def _():
            carry_ref[...] = new_ref[...]

        @pl.when(n == nb)
        def _():
            bsum_ref[...] = jnp.sum(bacc_ref[...], axis=1, keepdims=True)
            lane = lax.broadcasted_iota(jnp.int32, (1, nq), 1)
            dsk = jnp.zeros((1, nq), F32)
            for kv in range(ATTN_N_KV):
                acc = sacc_ref[kv]
                for r in range(ATTN_REP):
                    tot = jnp.sum(acc[:, r * w:(r + 1) * w], axis=1, keepdims=True)
                    dsk = jnp.where(lane == kv * ATTN_REP + r, tot, dsk)
            dsk_ref[...] = dsk

    cur = lambda n: jnp.minimum(n, nb - 1)
    prev = lambda n: jnp.maximum(jnp.minimum(n, nb - 1) - 1, 0)
    return pl.pallas_call(
        body, name="attn_bwd", grid=(nb + 1,),
        in_specs=_attn_specs_t(nb, cur, prev) + [pl.BlockSpec((qd, w), lambda n: (0, cur(n))),
                                                 pl.BlockSpec(sinks_rep.shape, lambda n: (0, 0, 0))],
        out_specs=[pl.BlockSpec((rows_all, w), lambda n: (0, jnp.maximum(n - 1, 0))),
                   pl.BlockSpec((rows_all, 1), lambda n: (0, 0)),
                   pl.BlockSpec((1, nq), lambda n: (0, 0))],
        out_shape=[jax.ShapeDtypeStruct((rows_all, t), BF16), jax.ShapeDtypeStruct((rows_all, 1), F32),
                   jax.ShapeDtypeStruct((1, nq), F32)],
        scratch_shapes=[pltpu.VMEM((rows_all, w), F32), pltpu.VMEM((rows_all, w), F32),
                        pltpu.VMEM((rows_all, w), F32), pltpu.VMEM(sinks_rep.shape, F32)],
        compiler_params=_params("arbitrary"))(qkv_t, qkv_t, qkv_t, qkv_t, qkv_t, do_t, sinks_rep)


HBM_SPEC = pl.BlockSpec(memory_space=pl.ANY)
HBM_ONLY = pl.BlockSpec(memory_space=pltpu.HBM)


def _comm_call(name, body, ins, out_shapes, n_sems):
    return pl.pallas_call(
        body, name=name, in_specs=[HBM_SPEC] * len(ins), out_specs=[HBM_SPEC] * len(out_shapes),
        out_shape=out_shapes,
        scratch_shapes=[pltpu.SemaphoreType.DMA((s,)) for s in n_sems])(*ins)


def _all_gather(name, shards, after):
    n = len(shards)
    na = len(after)

    def body(*refs):
        x_refs, out_refs = refs[:n], refs[n + na:2 * n + na]
        send_sems, recv_sems, local_sems = refs[2 * n + na:]
        x, y, c = lax.axis_index("x"), lax.axis_index("y"), lax.axis_index("c")
        me, sibling = (x, y, c), (x, y, 1 - c)
        chips = [(1 - x, y), (x, 1 - y), (1 - x, 1 - y)]

        def slot(i, px, py, pc):
            return out_refs[i].at[4 * px + 2 * py + pc]

        def copy(k, i, block, to, src=None):
            return pltpu.make_async_remote_copy(
                src_ref=slot(i, *block) if src is None else src, dst_ref=slot(i, *block),
                send_sem=send_sems.at[k * n + i], recv_sem=recv_sems.at[k * n + i], device_id=to,
                device_id_type=MESH)

        mine = [pltpu.make_async_copy(x_refs[i], slot(i, *me), local_sems.at[i]) for i in range(n)]
        first = []
        for i in range(n):
            mine[i].start()
            first.append(copy(0, i, me, sibling, src=x_refs[i]))
            first += [copy(1 + j, i, me, (*chip, c), src=x_refs[i]) for j, chip in enumerate(chips)]
        for cp in first:
            cp.start()
        passed = []
        for i in range(n):
            for j, chip in enumerate(chips):
                copy(1 + j, i, (*chip, c), me).wait_recv()
                passed.append(copy(4 + j, i, (*chip, c), sibling))
                passed[-1].start()
        for i in range(n):
            copy(0, i, sibling, me).wait_recv()
            for j, chip in enumerate(chips):
                copy(4 + j, i, (*chip, 1 - c), me).wait_recv()
        for cp in first + passed:
            cp.wait_send()
        for cp in mine:
            cp.wait()

    outs = [jax.ShapeDtypeStruct((N_DEV,) + s.shape, s.dtype) for s in shards]
    return _comm_call(name, body, list(shards) + list(after), outs, (7 * n, 7 * n, n))


SEM_SPEC = pl.BlockSpec(memory_space=pltpu.SEMAPHORE)
SPLIT_COPY_EFFECT = pltpu.SideEffectType.DATAFLOW_SIDE_EFFECTING


def _in_hbm(a):
    return pltpu.with_memory_space_constraint(a, pltpu.HBM)


def _split_start(name, body, srcs, lands, n_sems):
    n = len(srcs)
    bufs = [_in_hbm(a) for a in list(srcs) + list(lands)]
    outs = pl.pallas_call(
        body, name=name,
        out_shape=(pltpu.SemaphoreType.DMA((n_sems,)), pltpu.SemaphoreType.DMA((n_sems,)),
                   *[pltpu.HBM(a.shape, a.dtype) for a in bufs], jax.ShapeDtypeStruct((8, LANES), F32)),
        in_specs=[HBM_ONLY] * (2 * n),
        out_specs=(SEM_SPEC, SEM_SPEC, *[HBM_ONLY] * (2 * n), pl.BlockSpec(memory_space=pltpu.VMEM)),
        input_output_aliases={i: 2 + i for i in range(2 * n)},
        compiler_params=pltpu.CompilerParams(has_side_effects=SPLIT_COPY_EFFECT))(*bufs)
    return outs[0], outs[1], list(outs[2:2 + n]), list(outs[2 + n:2 + 2 * n]), outs[-1]


def _split_wait(name, body, send_sems, recv_sems, srcs, lands, after):
    n = len(srcs)
    outs = pl.pallas_call(
        body, name=name,
        out_shape=[pltpu.HBM(a.shape, a.dtype) for a in list(srcs) + list(lands)],
        in_specs=[HBM_ONLY] * (2 * n) + [SEM_SPEC, SEM_SPEC, HBM_SPEC],
        out_specs=[HBM_ONLY] * (2 * n),
        input_output_aliases={i: i for i in range(2 * n)},
        compiler_params=pltpu.CompilerParams(has_side_effects=SPLIT_COPY_EFFECT))(
            *srcs, *lands, send_sems, recv_sems, after)
    return list(outs[:n]), list(outs[n:])


N_PEERS = N_DEV - 1


def _gather_peers():
    x, y, c = lax.axis_index("x"), lax.axis_index("y"), lax.axis_index("c")
    flips = [(fx, fy, fc) for fx in (0, 1) for fy in (0, 1) for fc in (0, 1) if fx or fy or fc]
    return [(1 - x if fx else x, 1 - y if fy else y, 1 - c if fc else c) for fx, fy, fc in flips]


def _block_id(dev):
    return 4 * dev[0] + 2 * dev[1] + dev[2]


def _landing_block(land_ref, shard_shape, side_by_side, dev):
    if not side_by_side:
        return land_ref.at[_block_id(dev)]
    cols = shard_shape[1]
    return land_ref.at[:, pl.ds(pl.multiple_of(_block_id(dev) * cols, LANES), cols)]


def _gather_start(name, shards, side_by_side):
    n = len(shards)

    def body(*refs):
        x_refs, land_refs = refs[:n], refs[n:2 * n]
        send_sems, recv_sems, token = refs[2 * n], refs[2 * n + 1], refs[-1]
        me = (lax.axis_index("x"), lax.axis_index("y"), lax.axis_index("c"))
        for i in range(n):
            for k, peer in enumerate(_gather_peers()):
                pltpu.make_async_remote_copy(
                    src_ref=x_refs[i], dst_ref=_landing_block(land_refs[i], shards[i].shape, side_by_side[i], me),
                    send_sem=send_sems.at[N_PEERS * i + k], recv_sem=recv_sems.at[N_PEERS * i + k],
                    device_id=peer, device_id_type=MESH).start()
            pltpu.make_async_copy(x_refs[i], _landing_block(land_refs[i], shards[i].shape, side_by_side[i], me),
                                  send_sems.at[N_PEERS * n + i]).start()
        token[...] = jnp.zeros_like(token)

    lands = [lax.empty((s.shape[0], N_DEV * s.shape[1]) if wide else (N_DEV,) + s.shape, s.dtype)
             for s, wide in zip(shards, side_by_side)]
    return _split_start(name, body, shards, lands, (N_PEERS + 1) * n)


def _gather_wait(name, send_sems, recv_sems, first, n_all, shards, lands, side_by_side, after):
    n = len(shards)

    def body(*refs):
        x_refs, land_refs = refs[:n], refs[n:2 * n]
        send_sems, recv_sems = refs[2 * n], refs[2 * n + 1]
        me = (lax.axis_index("x"), lax.axis_index("y"), lax.axis_index("c"))
        for i in range(n):
            pltpu.make_async_copy(x_refs[i], _landing_block(land_refs[i], shards[i].shape, side_by_side[i], me),
                                  send_sems.at[N_PEERS * n_all + first + i]).wait()
            for k, peer in enumerate(_gather_peers()):
                cp = pltpu.make_async_remote_copy(
                    src_ref=x_refs[i], dst_ref=_landing_block(land_refs[i], shards[i].shape, side_by_side[i], peer),
                    send_sem=send_sems.at[N_PEERS * (first + i) + k],
                    recv_sem=recv_sems.at[N_PEERS * (first + i) + k],
                    device_id=peer, device_id_type=MESH)
                cp.wait_send()
                cp.wait_recv()

    return _split_wait(name, body, send_sems, recv_sems, shards, lands, after)


def _scatter_start(name, blocks):
    n = len(blocks)

    def body(*refs):
        b_refs, land_refs = refs[:n], refs[n:2 * n]
        send_sems, recv_sems, token = refs[2 * n], refs[2 * n + 1], refs[-1]
        me = (lax.axis_index("x"), lax.axis_index("y"), lax.axis_index("c"))
        for i in range(n):
            for k, peer in enumerate(_gather_peers()):
                pltpu.make_async_remote_copy(
                    src_ref=b_refs[i].at[_block_id(peer)], dst_ref=land_refs[i].at[_block_id(me)],
                    send_sem=send_sems.at[N_PEERS * i + k], recv_sem=recv_sems.at[N_PEERS * i + k],
                    device_id=peer, device_id_type=MESH).start()
            pltpu.make_async_copy(b_refs[i].at[_block_id(me)], land_refs[i].at[_block_id(me)],
                                  send_sems.at[N_PEERS * n + i]).start()
        token[...] = jnp.zeros_like(token)

    lands = [lax.empty(b.shape, b.dtype) for b in blocks]
    return _split_start(name, body, blocks, lands, (N_PEERS + 1) * n)


def _scatter_wait(name, send_sems, recv_sems, blocks, lands, after):
    n = len(blocks)

    def body(*refs):
        b_refs, land_refs = refs[:n], refs[n:2 * n]
        send_sems, recv_sems = refs[2 * n], refs[2 * n + 1]
        me = (lax.axis_index("x"), lax.axis_index("y"), lax.axis_index("c"))
        for i in range(n):
            pltpu.make_async_copy(b_refs[i].at[_block_id(me)], land_refs[i].at[_block_id(me)],
                                  send_sems.at[N_PEERS * n + i]).wait()
            for k, peer in enumerate(_gather_peers()):
                cp = pltpu.make_async_remote_copy(
                    src_ref=b_refs[i].at[_block_id(peer)], dst_ref=land_refs[i].at[_block_id(peer)],
                    send_sem=send_sems.at[N_PEERS * i + k], recv_sem=recv_sems.at[N_PEERS * i + k],
                    device_id=peer, device_id_type=MESH)
                cp.wait_send()
                cp.wait_recv()

    return _split_wait(name, body, send_sems, recv_sems, blocks, lands, after)


def _adamw(w, g, m, v):
    m = ADAM_B1 * m + (1.0 - ADAM_B1) * g
    v = ADAM_B2 * v + (1.0 - ADAM_B2) * (g * g)
    m_hat = m / (1.0 - ADAM_B1 ** ADAM_STEP)
    v_hat = v / (1.0 - ADAM_B2 ** ADAM_STEP)
    delta = -ADAM_LR * (m_hat / (jnp.sqrt(v_hat) + ADAM_EPS) + ADAM_WD * w)
    return delta, m, v


def _adamw_tiles(r, c_):
    tr = _tile(r, 256, 16)
    return (tr, c_) if tr < r or r <= 256 else (r, _tile(c_, 256))


def _sum_parts(part):
    g = part[0].astype(F32)
    for k in range(1, part.shape[0]):
        g = g + part[k].astype(F32)
    return g


def _sum_adamw(name, parts, w, m, v):
    r, c_ = w.shape
    tr, tc = _adamw_tiles(r, c_)

    def body(p_ref, w_ref, m_ref, v_ref, g_ref, d_ref, nm_ref, nv_ref):
        g = _sum_parts(p_ref)
        g_ref[...] = g
        d_ref[...], nm_ref[...], nv_ref[...] = _adamw(w_ref[...], g, m_ref[...], v_ref[...])

    tile = pl.BlockSpec((tr, tc), lambda i, j: (i, j))
    return pl.pallas_call(body, name=name, grid=(r // tr, c_ // tc),
                          in_specs=[pl.BlockSpec((parts.shape[0], tr, tc), lambda i, j: (0, i, j)), tile, tile, tile],
                          out_specs=[tile] * 4, out_shape=[jax.ShapeDtypeStruct((r, c_), F32)] * 4,
                          compiler_params=_params("parallel", "parallel"))(parts, w, m, v)


def _sum_adamw_layers(name, parts, w, m, v):
    n_layers, r, c_ = w.shape
    tr = _tile(r, 256, 16)

    def body(*refs):
        p_refs = refs[:n_layers]
        w_ref, m_ref, v_ref, g_ref, d_ref, nm_ref, nv_ref = refs[n_layers:]
        layer = pl.program_id(0)
        g = _sum_parts(p_refs[0])
        for li in range(1, n_layers):
            g = jnp.where(layer == li, _sum_parts(p_refs[li]), g)
        g_ref[...] = g
        d_ref[...], nm_ref[...], nv_ref[...] = _adamw(w_ref[...], g, m_ref[...], v_ref[...])

    row = pl.BlockSpec((None, tr, c_), lambda l, i: (l, i, 0))
    specs = [pl.BlockSpec((p.shape[0], tr, c_), lambda l, i: (0, i, 0)) for p in parts]
    return pl.pallas_call(body, name=name, grid=(n_layers, r // tr), in_specs=specs + [row, row, row],
                          out_specs=[row] * 4, out_shape=[jax.ShapeDtypeStruct(w.shape, F32)] * 4,
                          compiler_params=_params("parallel", "parallel"))(*parts, w, m, v)


def _pack_rows(flat, n_rows, cols):
    pad = n_rows * cols - flat.shape[-1]
    flat = jnp.pad(flat, [(0, 0)] * (flat.ndim - 1) + [(0, pad)])
    return flat.reshape(flat.shape[:-1] + (n_rows, cols))


def _cols_split(full):
    c = full.shape[1] // N_DEV
    return jnp.stack([full[:, d * c:(d + 1) * c] for d in range(N_DEV)])


def _rows_join(blocks):
    return blocks.reshape(N_DEV * blocks.shape[1], blocks.shape[2])


def _rows_split(full):
    return full.reshape(N_DEV, full.shape[0] // N_DEV, full.shape[1])


def _heads_col(v, ng):
    return jnp.pad(v.reshape(ng, 1, SSD_HPG), ((0, 0), (0, 0), (0, LANES - SSD_HPG)))


MATRIX_ITEMS = ("w_in", "w_out", "up0", "down0", "w_qkv", "w_o", "up1", "down1")
VECTOR_ITEMS = ("conv_w", "b_qkv", "b_o")
ITEMS = MATRIX_ITEMS + VECTOR_ITEMS
GATHER_STAGES = (("w_in", "conv_w"), ("w_out", "up0", "down0"), ("w_qkv", "b_qkv", "w_o", "b_o", "up1", "down1"))
SIDE_BY_SIDE = ("conv_w", "up0", "up1", "b_o")


def _items(tree, prefix=""):
    g = lambda k: tree[prefix + k]
    return {"w_in": g("ssd_w_in")[0].T, "w_out": g("ssd_w_out")[0], "w_qkv": g("attn_w_qkv")[0].T,
            "w_o": g("attn_w_o")[0], "up0": g("mlp_w_up")[0], "up1": g("mlp_w_up")[1],
            "down0": g("mlp_w_down")[0], "down1": g("mlp_w_down")[1], "conv_w": g("ssd_conv_w")[0],
            "b_qkv": g("attn_b_qkv"), "b_o": g("attn_b_o")}


REPLICATED = ("ssd_conv_b", "ssd_dt_bias", "ssd_a_log", "ssd_d", "ssd_norm_w", "attn_sinks", "mix_pre_norm",
              "mix_post_norm", "ffn_pre_norm", "ffn_post_norm")
WEIGHTS = ("ssd_w_in", "ssd_conv_w", "ssd_conv_b", "ssd_dt_bias", "ssd_a_log", "ssd_d", "ssd_norm_w", "ssd_w_out",
           "attn_w_qkv", "attn_b_qkv", "attn_sinks", "attn_w_o", "attn_b_o", "mlp_w_up", "mlp_w_down",
           "mix_pre_norm", "mix_post_norm", "ffn_pre_norm", "ffn_post_norm")


def _forward_backward(x, target, rep, token, weights_of_stage, reduce_grads):
    t, d = x.shape
    ng = rep["ssd_norm_w"].shape[1] // GW
    di = ng * GW
    n_xbc = ng * GC
    nh = ng * SSD_HPG
    grads, blocks = {}, {}
    w_up, w_down = [None, None], [None, None]
    sinks_rep = jnp.repeat(rep["attn_sinks"].reshape(ATTN_N_KV, ATTN_REP, 1), ATTN_WINDOW, axis=2).reshape(
        ATTN_N_KV, 1, ATTN_REP * ATTN_WINDOW)
    conv_b = rep["ssd_conv_b"]
    gn = ng * SSD_D_STATE
    parts = ((0, di), (di, di), (2 * di, gn), (2 * di + gn, gn), (di + n_xbc, nh))
    alog_c, dsk_c = (_heads_col(rep[k], ng) for k in ("ssd_a_log", "ssd_d"))
    bias_l, alog_l = (jnp.pad(rep[k], ((0, 0), (0, LANES - nh))) for k in ("ssd_dt_bias", "ssd_a_log"))
    norm = {k: rep[k] for k in ("mix_pre_norm", "mix_post_norm", "ffn_pre_norm", "ffn_post_norm")}

    def nrow(name, i):
        return norm[name][i:i + 1]

    def mlp_fwd(i, u2):
        p = _mm(f"mlp{i}_up", [u2], [w_up[i]], "nn", tm=1024, tn=1024, out_dtypes=(BF16,),
                epilogue=lambda acc: (jnp.square(jnp.maximum(acc, 0.0)),))
        f = _mm(f"mlp{i}_down", [p], [w_down[i]], "nn", tm=512, tn=1024)
        return p, f

    def mlp_bwd(i, df, u2, p):
        da = _mm(f"mlp{i}_dact", [df], [w_down[i]], "nt", tm=1024, tn=1024, out_dtypes=(BF16,),
                 tiles=(p,), epilogue=lambda acc, pv: (acc * (2.0 * jnp.sqrt(pv.astype(F32))),))
        blocks[f"down{i}"] = _rows_split(_mm(f"mlp{i}_dwdown", [p], [df], "tn", tm=512, tn=1024,
                                             out_dtypes=(PAYLOAD,)))
        blocks[f"up{i}"] = _mm(f"mlp{i}_dwup", [u2], [da], "tn", tm=1024, tn=da.shape[1] // N_DEV,
                               out_dtypes=(PAYLOAD,), col_blocks=True)
        return _mm(f"mlp{i}_dx", [da], [w_up[i]], "nt", tm=512, tn=1024)

    u0 = _prenorm("l0_prenorm", x, nrow("mix_pre_norm", 0), token)
    got = weights_of_stage(0, u0)
    w_in_t = _rows_join(got["w_in"])
    w_dt_t = jnp.pad(w_in_t[di + n_xbc:], ((0, LANES - nh), (0, 0)))
    conv_w = got["conv_w"]
    zx = _mm("ssd_in_proj", [u0], [w_in_t], "nt", tm=1024, tn=1024, n_use=di + n_xbc)
    zdt = _mm("ssd_dt_proj", [u0], [w_dt_t], "nt", tm=1024, tn=LANES)
    pre = _conv_fwd(zx, di, n_xbc, conv_w, conv_b)
    dt_c, cum_c, cum_r, sgd_c = _ssd_dt_prep(zdt, bias_l, alog_l, ng)
    y, states = _ssd_fwd(pre, dt_c, cum_c, cum_r, alog_c, dsk_c)
    yn = _gate_norm_fwd(y, zx, rep["ssd_norm_w"])
    got = weights_of_stage(1, yn)
    w_out = _rows_join(got["w_out"])
    w_up[0], w_down[0] = got["up0"], _rows_join(got["down0"])
    mix0 = _mm("ssd_out_proj", [yn], [w_out], "nn", tm=1024, tn=1024)
    h1, u0f = _post_pre("l0_mid", x, mix0, nrow("mix_post_norm", 0), nrow("ffn_pre_norm", 0))
    p0, f0 = mlp_fwd(0, u0f)
    h2, u1 = _post_pre("l1_in", h1, f0, nrow("ffn_post_norm", 0), nrow("mix_pre_norm", 1))
    got = weights_of_stage(2, u1)
    w_qkv_t = _rows_join(got["w_qkv"])
    w_o = _rows_join(got["w_o"])
    b_qkv_col = got["b_qkv"].reshape(-1, 1)
    b_o = got["b_o"]
    w_up[1], w_down[1] = got["up1"], _rows_join(got["down1"])
    qkv_t = _mm("attn_qkv_proj", [w_qkv_t], [u1], "nt", tm=768, tn=1024, out_dtypes=(BF16,), cols=(b_qkv_col,),
                epilogue=lambda acc, b: (acc + b,))
    ao_t = _attn_fwd_t(qkv_t, sinks_rep)
    mix1 = _mm("attn_out_proj", [ao_t], [w_o], "tn", tm=1024, tn=1024, rows=(b_o,),
               epilogue=lambda acc, b: (acc + b,))
    h3, u1f = _post_pre("l1_mid", h2, mix1, nrow("mix_post_norm", 1), nrow("ffn_pre_norm", 1))
    p1, f1 = mlp_fwd(1, u1f)
    dh, loss_row = _final_loss("loss", h3, f1, nrow("ffn_post_norm", 1), target)

    g_norm = {k: [None, None] for k in norm}
    df1, g_norm["ffn_post_norm"][1], _ = _norm_bwd("l1_ffn_post_bwd", dh, post=(f1, nrow("ffn_post_norm", 1)))
    du = mlp_bwd(1, df1, u1f, p1)
    sent = reduce_grads("mlp1", {k: blocks[k] for k in ("up1", "down1")})
    dh, g_norm["ffn_pre_norm"][1], dmix1, g_norm["mix_post_norm"][1], db_o = _norm_bwd(
        "l1_mid_bwd", dh, pre=(du, h3, nrow("ffn_pre_norm", 1)), post=(mix1, nrow("mix_post_norm", 1)), after=sent)
    blocks["b_o"] = _cols_split(db_o)
    blocks["w_o"] = _rows_split(_mm("attn_dwo", [ao_t], [dmix1], "nn", tm=512, tn=1024, out_dtypes=(PAYLOAD,)))
    dao_t = _mm("attn_dout", [w_o], [dmix1], "nt", tm=1024, tn=1024, out_dtypes=(BF16,))
    dqkv_t, db_qkv, grads["attn_sinks"] = _attn_bwd_t(qkv_t, dao_t, sinks_rep)
    blocks["b_qkv"] = db_qkv.reshape(N_DEV, 1, -1)
    blocks["w_qkv"] = _rows_split(_mm("attn_dwqkv", [dqkv_t], [u1], "nn", tm=512, tn=1024, out_dtypes=(PAYLOAD,)))
    du = _mm("attn_dx", [dqkv_t], [w_qkv_t], "tn", tm=1024, tn=1024)
    sent = reduce_grads("attn", {k: blocks[k] for k in ("w_o", "w_qkv", "b_o", "b_qkv")})
    dh, g_norm["mix_pre_norm"][1], df0, g_norm["ffn_post_norm"][0], _ = _norm_bwd(
        "l1_in_bwd", dh, pre=(du, h2, nrow("mix_pre_norm", 1)), post=(f0, nrow("ffn_post_norm", 0)), after=sent)
    du = mlp_bwd(0, df0, u0f, p0)
    sent = reduce_grads("mlp0", {k: blocks[k] for k in ("up0", "down0")})
    dh, g_norm["ffn_pre_norm"][0], dmix0, g_norm["mix_post_norm"][0], _ = _norm_bwd(
        "l0_mid_bwd", dh, pre=(du, h1, nrow("ffn_pre_norm", 0)), post=(mix0, nrow("mix_post_norm", 0)), after=sent)
    blocks["w_out"] = _rows_split(_mm("ssd_dwout", [yn], [dmix0], "tn", tm=512, tn=1024, out_dtypes=(PAYLOAD,)))
    dyn = _mm("ssd_dyn", [dmix0], [w_out], "nt", tm=1024, tn=1024)
    sent = reduce_grads("ssdout", {"w_out": blocks["w_out"]})
    dy, dz, grads["ssd_norm_w"] = _gate_norm_bwd(dyn, y, zx, rep["ssd_norm_w"], sent)
    dpx, dpb, dpc, ddt_g, dbias_g, dalog_g, dd_g = _ssd_bwd(dy, pre, states, dt_c, cum_c, cum_r, sgd_c, alog_c,
                                                             dsk_c)
    conv_out = [_conv_bwd(f"ssd_conv_bwd_{tag}", dp, zx, c0, conv_w[:, c0 - di:c0 - di + n])
                for tag, dp, (c0, n) in zip("xbc", (dpx, dpb, dpc), parts[1:4])]
    dconv_w = jnp.concatenate([o[1] for o in conv_out], axis=1)
    dconv_b = jnp.concatenate([o[2] for o in conv_out], axis=1)
    ddt = jnp.transpose(ddt_g[:, :, :SSD_HPG], (1, 0, 2)).reshape(t, nh)
    ddt = jnp.pad(ddt, ((0, 0), (0, LANES - nh))).astype(BF16)
    blocks["conv_w"] = _cols_split(dconv_w)
    grads["ssd_conv_b"] = dconv_b
    for name, val in (("ssd_dt_bias", dbias_g), ("ssd_a_log", dalog_g), ("ssd_d", dd_g)):
        grads[name] = val[:, 0, :SSD_HPG].reshape(1, nh)
    d_zx = [dz] + [o[0] for o in conv_out] + [ddt]
    dw_parts = [_mm(f"ssd_dw_{tag}", [d], [u0], "tn", tm=512, tn=1024, out_dtypes=(PAYLOAD,))
                for tag, d in zip("zxbct", d_zx)]
    dw_parts[-1] = dw_parts[-1][:nh]
    blocks["w_in"] = _rows_split(jnp.concatenate(dw_parts, axis=0))
    sent = reduce_grads("ssd", {k: blocks[k] for k in ("w_in", "conv_w")})
    w_parts = [w_in_t[r0:r0 + n] for r0, n in parts[:-1]] + [w_dt_t]
    du = _mm("ssd_dx", d_zx, w_parts, "nn", tm=256, tn=1024, after=sent)
    grad_x, g_norm["mix_pre_norm"][0] = _norm_bwd("l0_in_bwd", dh, pre=(du, x, nrow("mix_pre_norm", 0)), after=sent)
    for k in norm:
        grads[k] = jnp.concatenate(g_norm[k], axis=0)
    return loss_row, grad_x, grads


def kernel(x, ssd_w_in, ssd_conv_w, ssd_conv_b, ssd_dt_bias, ssd_a_log, ssd_d, ssd_norm_w, ssd_w_out, attn_w_qkv, attn_b_qkv, attn_sinks, attn_w_o, attn_b_o, mlp_w_up, mlp_w_down, mix_pre_norm, mix_post_norm, ffn_pre_norm, ffn_post_norm, loss_target, m_ssd_w_in, m_ssd_conv_w, m_ssd_conv_b, m_ssd_dt_bias, m_ssd_a_log, m_ssd_d, m_ssd_norm_w, m_ssd_w_out, m_attn_w_qkv, m_attn_b_qkv, m_attn_sinks, m_attn_w_o, m_attn_b_o, m_mlp_w_up, m_mlp_w_down, m_mix_pre_norm, m_mix_post_norm, m_ffn_pre_norm, m_ffn_post_norm, v_ssd_w_in, v_ssd_conv_w, v_ssd_conv_b, v_ssd_dt_bias, v_ssd_a_log, v_ssd_d, v_ssd_norm_w, v_ssd_w_out, v_attn_w_qkv, v_attn_b_qkv, v_attn_sinks, v_attn_w_o, v_attn_b_o, v_mlp_w_up, v_mlp_w_down, v_mix_pre_norm, v_mix_post_norm, v_ffn_pre_norm, v_ffn_post_norm):
    given = dict(locals())
    w = {k: given[k] for k in WEIGHTS}
    mom_m = {k: given["m_" + k] for k in WEIGHTS}
    mom_v = {k: given["v_" + k] for k in WEIGHTS}
    w_it, m_it, v_it = _items(given), _items(given, "m_"), _items(given, "v_")

    order = [k for stage in GATHER_STAGES for k in stage]
    shards = [w_it[k].astype(PAYLOAD) if k in MATRIX_ITEMS else w_it[k] for k in order]
    wide = [k in SIDE_BY_SIDE for k in order]
    g_send, g_recv, shards, lands, token = _gather_start("gather_start", shards, wide)

    def weights_of_stage(s, after):
        first = sum(len(stage) for stage in GATHER_STAGES[:s])
        sl = slice(first, first + len(GATHER_STAGES[s]))
        _, got = _gather_wait(f"gather_wait{s}", g_send, g_recv, first, len(order), shards[sl], lands[sl], wide[sl],
                              after)
        return dict(zip(GATHER_STAGES[s], got))

    in_flight = []

    def reduce_grads(tag, blocks):
        keys = list(blocks)
        started = _scatter_start(f"rs_start_{tag}", [blocks[k] for k in keys])
        in_flight.append((tag, keys, started))
        return started[-1]

    rep = {k: w[k] for k in REPLICATED}
    loss_row, grad_x, grads = _forward_backward(x[0], loss_target[0], rep, token, weights_of_stage, reduce_grads)

    def pack_rep(tree, last):
        flat = jnp.concatenate([tree[k].reshape(-1) for k in REPLICATED] + [last])
        return _pack_rows(flat, _round_up(-(-flat.shape[0] // LANES), 8), LANES)

    landed = {}

    def wait_group(group, after):
        tag, keys, (s_send, s_recv, srcs, s_lands, _) = group
        _, got = _scatter_wait(f"rs_wait_{tag}", s_send, s_recv, srcs, s_lands, after)
        landed.update(zip(keys, got))

    def adamw_item(k):
        return _sum_adamw(f"adamw_{k}", landed[k], w_it[k], m_it[k], v_it[k])

    def adamw_stack(name, keys):
        return _sum_adamw_layers(f"adamw_{name}", [landed[k] for k in keys], given[name], given["m_" + name],
                                 given["v_" + name])

    for group in in_flight[:-1]:
        wait_group(group, grad_x)
    done = {"mlp_w_up": adamw_stack("mlp_w_up", ("up0", "up1")),
            "mlp_w_down": adamw_stack("mlp_w_down", ("down0", "down1")),
            "attn_w_qkv": [o.T[None] for o in adamw_item("w_qkv")],
            "attn_w_o": [o[None] for o in adamw_item("w_o")],
            "attn_b_qkv": adamw_item("b_qkv"), "attn_b_o": adamw_item("b_o"),
            "ssd_w_out": [o[None] for o in adamw_item("w_out")]}
    partials, = _all_gather("gather_small_grads", [pack_rep(grads, loss_row[0, :1])],
                            [outs4[0] for outs4 in done.values()])
    wait_group(in_flight[-1], partials)
    done["ssd_w_in"] = [o.T[None] for o in adamw_item("w_in")]
    done["ssd_conv_w"] = [o[None] for o in adamw_item("conv_w")]
    zero = jnp.zeros((1,), F32)
    rep_out = _sum_adamw("adamw_replicated", partials, pack_rep(w, zero), pack_rep(mom_m, zero), pack_rep(mom_v, zero))

    kinds = []
    for kind, r_arr in enumerate(rep_out):
        tree = {name: outs4[kind] for name, outs4 in done.items()}
        flat, off = r_arr.reshape(-1), 0
        for k in REPLICATED:
            tree[k] = flat[off:off + w[k].size].reshape(w[k].shape)
            off += w[k].size
        kinds.append(tree)
    loss = rep_out[0].reshape(-1)[off]
    outs = [loss, grad_x[None]]
    for tree in kinds:
        outs += [tree[k] for k in WEIGHTS]
    return tuple(outs)
```

```python
import jax
import jax.numpy as jnp
from jax import lax
from jax.experimental import pallas as pl
from jax.experimental.pallas import tpu as pltpu

F32 = jnp.float32
BF16 = jnp.bfloat16
PAYLOAD = jnp.bfloat16
HIGHEST = lax.Precision.HIGHEST
MESH = pl.DeviceIdType.MESH

NORM_EPS = 1e-6
SSD_HEAD_DIM = 64
SSD_HPG = 4
SSD_D_STATE = 128
SSD_CONV_WIDTH = 4
SSD_CHUNK = 128
ATTN_HEAD_DIM = 64
ATTN_N_KV = 4
ATTN_REP = 4
ATTN_WINDOW = 128
ADAM_LR = 0.001
ADAM_B1 = 0.9
ADAM_B2 = 0.999
ADAM_EPS = 1e-08
ADAM_WD = 0.01
ADAM_STEP = 10

N_DEV = 8
LANES = 128
V7X_VMEM_LIMIT = 56 * 1024 * 1024

GW = SSD_HPG * SSD_HEAD_DIM
GC = GW + 2 * SSD_D_STATE
assert SSD_CHUNK == LANES


def _params(*sem):
    return pltpu.CompilerParams(dimension_semantics=sem, vmem_limit_bytes=V7X_VMEM_LIMIT)


def _tile(n, pref, mult=LANES):
    best = None
    t = mult
    while t <= min(n, pref):
        if n % t == 0:
            best = t
        t += mult
    return best if best is not None else n


def _round_up(n, m):
    return (n + m - 1) // m * m


def _acc(ref, val, first):
    @pl.when(first)
    def _():
        ref[...] = val

    @pl.when(jnp.logical_not(first))
    def _():
        ref[...] += val


def _dot(a, b):
    return lax.dot_general(a, b, (((1,), (0,)), ((), ())), preferred_element_type=F32)


def _dot_nt(a, b):
    return lax.dot_general(a, b, (((1,), (1,)), ((), ())), preferred_element_type=F32)


def _dot_tn(a, b):
    return lax.dot_general(a, b, (((0,), (0,)), ((), ())), preferred_element_type=F32)


def _dot_f32(a, b):
    return lax.dot_general(a, b, (((1,), (0,)), ((), ())), preferred_element_type=F32, precision=HIGHEST)


_DOTS = {"nn": _dot, "nt": _dot_nt, "tn": _dot_tn}


def _sigmoid(x):
    return 1.0 / (1.0 + jnp.exp(-x))


def _softplus(x):
    return jnp.maximum(x, 0.0) + jnp.log1p(jnp.exp(-jnp.abs(x)))


def _silu_grad(x, s):
    return s * (1.0 + x * (1.0 - s))


def _mm(name, a_list, b_list, mode, *, tm, tn, out_dtypes=(F32,), epilogue=None, tiles=(), rows=(), cols=(),
        col_blocks=False, n_use=None, after=None):
    npair = len(a_list)
    if mode == "tn":
        m = a_list[0].shape[1]
    else:
        m = a_list[0].shape[0]
    n = n_use if n_use is not None else (b_list[0].shape[0] if mode == "nt" else b_list[0].shape[1])
    tm = _tile(m, tm, LANES if mode == "tn" else 8)
    tn = _tile(n, tn)
    assert m % tm == 0 and n % tn == 0, (name, m, n, tm, tn)
    dot = _DOTS[mode]

    def body(*refs):
        a_refs = refs[:npair]
        b_refs = refs[npair:2 * npair]
        n_extra = len(tiles) + len(rows) + len(cols)
        e_refs = refs[2 * npair:2 * npair + n_extra]
        o_refs = refs[2 * npair + n_extra + len(order):]
        acc = None
        for ar, br in zip(a_refs, b_refs):
            d = dot(ar[...], br[...])
            acc = d if acc is None else acc + d
        outs = epilogue(acc, *[e[...] for e in e_refs]) if epilogue is not None else (acc,)
        for o, v in zip(o_refs, outs):
            o[...] = v.astype(o.dtype)

    in_specs = []
    for a in a_list:
        if mode == "tn":
            in_specs.append(pl.BlockSpec((a.shape[0], tm), lambda i, j: (0, i)))
        else:
            in_specs.append(pl.BlockSpec((tm, a.shape[1]), lambda i, j: (i, 0)))
    for b in b_list:
        if mode == "nt":
            in_specs.append(pl.BlockSpec((tn, b.shape[1]), lambda i, j: (j, 0)))
        else:
            in_specs.append(pl.BlockSpec((b.shape[0], tn), lambda i, j: (0, j)))
    in_specs += [pl.BlockSpec((tm, tn), lambda i, j: (i, j)) for _ in tiles]
    in_specs += [pl.BlockSpec((1, tn), lambda i, j: (0, j)) for _ in rows]
    in_specs += [pl.BlockSpec((tm, 1), lambda i, j: (i, 0)) for _ in cols]
    order = [] if after is None else [after]
    in_specs += [pl.BlockSpec((8, LANES), lambda i, j: (0, 0)) for _ in order]
    outs = pl.pallas_call(
        body,
        name=name,
        grid=(m // tm, n // tn),
        in_specs=in_specs,
        out_specs=[pl.BlockSpec((None, tm, tn), lambda i, j: (j, i, 0)) if col_blocks else
                   pl.BlockSpec((tm, tn), lambda i, j: (i, j)) for _ in out_dtypes],
        out_shape=[jax.ShapeDtypeStruct((n // tn, m, tn) if col_blocks else (m, n), dt) for dt in out_dtypes],
        compiler_params=_params("parallel", "parallel"),
    )(*a_list, *b_list, *tiles, *rows, *cols, *order)
    return outs[0] if len(out_dtypes) == 1 else outs


def _rms(x, w):
    r = lax.rsqrt(jnp.mean(x * x, axis=-1, keepdims=True) + NORM_EPS)
    return x * r * w


def _rms_bwd(x, w, dy):
    r = lax.rsqrt(jnp.mean(x * x, axis=-1, keepdims=True) + NORM_EPS)
    xh = x * r
    g = dy * w
    dx = r * (g - xh * jnp.mean(g * xh, axis=-1, keepdims=True))
    return dx, dy * xh


def _row_specs(tr, d):
    return pl.BlockSpec((tr, d), lambda i: (i, 0)), pl.BlockSpec((1, d), lambda i: (0, 0))


def _prenorm(name, h, w, after):
    t, d = h.shape
    tr = _tile(t, 512, 8)
    row, vec = _row_specs(tr, d)

    def body(h_ref, w_ref, after_ref, u_ref):
        u_ref[...] = _rms(h_ref[...], w_ref[...]).astype(BF16)

    return pl.pallas_call(body, name=name, grid=(t // tr,),
                          in_specs=[row, vec, pl.BlockSpec((8, LANES), lambda i: (0, 0))], out_specs=row,
                          out_shape=jax.ShapeDtypeStruct((t, d), BF16), compiler_params=_params("parallel"))(
                              h, w, after)


def _post_pre(name, h, m, w_post, w_pre):
    t, d = h.shape
    tr = _tile(t, 512, 8)
    row, vec = _row_specs(tr, d)

    def body(h_ref, m_ref, wq_ref, wp_ref, hn_ref, u_ref):
        hn = h_ref[...] + _rms(m_ref[...], wq_ref[...])
        hn_ref[...] = hn
        u_ref[...] = _rms(hn, wp_ref[...]).astype(BF16)

    return pl.pallas_call(body, name=name, grid=(t // tr,), in_specs=[row, row, vec, vec], out_specs=[row, row],
                          out_shape=[jax.ShapeDtypeStruct((t, d), F32), jax.ShapeDtypeStruct((t, d), BF16)],
                          compiler_params=_params("parallel"))(h, m, w_post, w_pre)


def _final_loss(name, h, m, w_post, target):
    t, d = h.shape
    tr = _tile(t, 512, 8)
    row, vec = _row_specs(tr, d)

    def body(h_ref, m_ref, wq_ref, t_ref, dh_ref, loss_ref):
        err = h_ref[...] + _rms(m_ref[...], wq_ref[...]) - t_ref[...]
        dh_ref[...] = err * (1.0 / d)
        part = 0.5 * jnp.sum(jnp.mean(err * err, axis=-1, keepdims=True), axis=0, keepdims=True)
        _acc(loss_ref, jnp.broadcast_to(part, (1, LANES)), pl.program_id(0) == 0)

    return pl.pallas_call(body, name=name, grid=(t // tr,), in_specs=[row, row, vec, row],
                          out_specs=[row, pl.BlockSpec((1, LANES), lambda i: (0, 0))],
                          out_shape=[jax.ShapeDtypeStruct((t, d), F32), jax.ShapeDtypeStruct((1, LANES), F32)],
                          compiler_params=_params("arbitrary"))(h, m, w_post, target)


def _norm_bwd(name, dh, pre=None, post=None, after=None):
    t, d = dh.shape
    tr = _tile(t, 512, 8)
    row, vec = _row_specs(tr, d)
    has_pre, has_post = pre is not None, post is not None

    def body(*refs):
        it = iter(refs)
        dh_ref = next(it)
        if has_pre:
            du_ref, x_ref, wp_ref = next(it), next(it), next(it)
        if has_post:
            m_ref, wq_ref = next(it), next(it)
        if after is not None:
            next(it)
        first = pl.program_id(0) == 0
        dh_v = dh_ref[...]
        if has_pre:
            dhn_ref, dwp_ref = next(it), next(it)
            dx, dwr = _rms_bwd(x_ref[...], wp_ref[...], du_ref[...])
            dh_v = dh_v + dx
            dhn_ref[...] = dh_v
            _acc(dwp_ref, jnp.sum(dwr, axis=0, keepdims=True), first)
        if has_post:
            dm_ref, dwq_ref, dms_ref = next(it), next(it), next(it)
            dm, dwr = _rms_bwd(m_ref[...], wq_ref[...], dh_v)
            dm_ref[...] = dm.astype(BF16)
            _acc(dwq_ref, jnp.sum(dwr, axis=0, keepdims=True), first)
            _acc(dms_ref, jnp.sum(dm, axis=0, keepdims=True), first)

    ins, in_specs, out_specs, out_shape = [dh], [row], [], []
    if has_pre:
        ins += list(pre)
        in_specs += [row, row, vec]
        out_specs += [row, vec]
        out_shape += [jax.ShapeDtypeStruct((t, d), F32), jax.ShapeDtypeStruct((1, d), F32)]
    if has_post:
        ins += list(post)
        in_specs += [row, vec]
        out_specs += [row, vec, vec]
        out_shape += [jax.ShapeDtypeStruct((t, d), BF16), jax.ShapeDtypeStruct((1, d), F32),
                      jax.ShapeDtypeStruct((1, d), F32)]
    if after is not None:
        ins.append(after)
        in_specs.append(pl.BlockSpec((8, LANES), lambda i: (0, 0)))
    return pl.pallas_call(body, name=name, grid=(t // tr,), in_specs=in_specs, out_specs=out_specs,
                          out_shape=out_shape, compiler_params=_params("arbitrary"))(*ins)


HALO = 8


def _shift_later(cur, prev, s):
    rolled = pltpu.roll(cur, s, 0)
    row = lax.broadcasted_iota(jnp.int32, prev.shape, 0)
    first = jnp.where(row < s, pltpu.roll(prev, s, 0), rolled[0:HALO])
    return jnp.concatenate([first, rolled[HALO:]], axis=0)


def _shift_earlier(cur, nxt, s):
    tt = cur.shape[0]
    rolled = pltpu.roll(cur, tt - s, 0)
    row = lax.broadcasted_iota(jnp.int32, nxt.shape, 0)
    last = jnp.where(row >= HALO - s, pltpu.roll(nxt, HALO - s, 0), rolled[tt - HALO:])
    return jnp.concatenate([rolled[:tt - HALO], last], axis=0)


def _conv_fwd(zx, col0, n_ch, conv_w, conv_b):
    t = zx.shape[0]
    tc = _tile(n_ch, 512)
    tt = _tile(t, 1024, 8)
    cb0 = col0 // tc
    assert col0 % tc == 0
    kw = SSD_CONV_WIDTH

    def body(x_ref, p_ref, w_ref, b_ref, o_ref):
        cur = x_ref[...]
        prev = jnp.where(pl.program_id(1) > 0, p_ref[...], 0.0)
        w = w_ref[...]
        acc = b_ref[...] + w[kw - 1:kw, :] * cur
        for k in range(kw - 1):
            acc = acc + w[k:k + 1, :] * _shift_later(cur, prev, kw - 1 - k)
        o_ref[...] = acc

    return pl.pallas_call(
        body, name="ssd_conv_fwd", grid=(n_ch // tc, t // tt),
        in_specs=[pl.BlockSpec((tt, tc), lambda j, i: (i, cb0 + j)),
                  pl.BlockSpec((HALO, tc), lambda j, i: (jnp.maximum(i * (tt // HALO) - 1, 0), cb0 + j)),
                  pl.BlockSpec((kw, tc), lambda j, i: (0, j)),
                  pl.BlockSpec((1, tc), lambda j, i: (0, j))],
        out_specs=pl.BlockSpec((tt, tc), lambda j, i: (i, j)),
        out_shape=jax.ShapeDtypeStruct((t, n_ch), F32),
        compiler_params=_params("parallel", "parallel"))(zx, zx, conv_w, conv_b)


def _conv_bwd(name, dpre, zx, col0, conv_w):
    t, n_ch = dpre.shape
    tc = _tile(n_ch, 512)
    tt = _tile(t, 1024, 8)
    cb0 = col0 // tc
    kw = SSD_CONV_WIDTH
    nt = t // tt

    def body(d_ref, dn_ref, x_ref, p_ref, w_ref, dx_ref, dw_ref, db_ref):
        i = pl.program_id(1)
        d = d_ref[...]
        d_next = jnp.where(i < nt - 1, dn_ref[...], 0.0)
        x = x_ref[...]
        x_prev = jnp.where(i > 0, p_ref[...], 0.0)
        w = w_ref[...]
        dx = w[kw - 1:kw, :] * d
        for k in range(kw - 1):
            dx = dx + w[k:k + 1, :] * _shift_earlier(d, d_next, kw - 1 - k)
        dx_ref[...] = dx.astype(BF16)
        first = i == 0
        for k in range(kw):
            xs = x if k == kw - 1 else _shift_later(x, x_prev, kw - 1 - k)
            val = jnp.sum(d * xs, axis=0, keepdims=True)

            @pl.when(first)
            def _():
                dw_ref[k:k + 1, :] = val

            @pl.when(jnp.logical_not(first))
            def _():
                dw_ref[k:k + 1, :] += val
        _acc(db_ref, jnp.sum(d, axis=0, keepdims=True), first)

    return pl.pallas_call(
        body, name=name, grid=(n_ch // tc, nt),
        in_specs=[pl.BlockSpec((tt, tc), lambda j, i: (i, j)),
                  pl.BlockSpec((HALO, tc), lambda j, i: (jnp.minimum((i + 1) * (tt // HALO), t // HALO - 1), j)),
                  pl.BlockSpec((tt, tc), lambda j, i: (i, cb0 + j)),
                  pl.BlockSpec((HALO, tc), lambda j, i: (jnp.maximum(i * (tt // HALO) - 1, 0), cb0 + j)),
                  pl.BlockSpec((kw, tc), lambda j, i: (0, j))],
        out_specs=[pl.BlockSpec((tt, tc), lambda j, i: (i, j)),
                   pl.BlockSpec((kw, tc), lambda j, i: (0, j)),
                   pl.BlockSpec((1, tc), lambda j, i: (0, j))],
        out_shape=[jax.ShapeDtypeStruct((t, n_ch), BF16), jax.ShapeDtypeStruct((kw, n_ch), F32),
                   jax.ShapeDtypeStruct((1, n_ch), F32)],
        compiler_params=_params("parallel", "arbitrary"))(dpre, dpre, zx, zx, conv_w)


def _head_of_lane(shape, width):
    return lax.broadcasted_iota(jnp.int32, shape, len(shape) - 1) // width


def _select_dot(v, pick, pick_first=False):
    hi = v.astype(BF16)
    lo = (v - hi.astype(F32)).astype(BF16)
    return _dot(pick, hi) + _dot(pick, lo) if pick_first else _dot(hi, pick) + _dot(lo, pick)


def _expand(v, n_rows, on_mxu=False):
    if not on_mxu:
        head = _head_of_lane((n_rows, GW), SSD_HEAD_DIM)
        out = jnp.zeros((n_rows, GW), F32)
        for j in range(SSD_HPG):
            out = jnp.where(head == j, v[:, j:j + 1], out)
        return out
    src = lax.broadcasted_iota(jnp.int32, (LANES, GW), 0)
    return _select_dot(v, (src == _head_of_lane((LANES, GW), SSD_HEAD_DIM)).astype(BF16))


def _contract(v, n_rows, on_mxu=False):
    if not on_mxu:
        head = _head_of_lane((n_rows, GW), SSD_HEAD_DIM)
        lane = lax.broadcasted_iota(jnp.int32, (n_rows, LANES), 1)
        out = jnp.zeros((n_rows, LANES), F32)
        for j in range(SSD_HPG):
            s = jnp.sum(jnp.where(head == j, v, 0.0), axis=1, keepdims=True)
            out = jnp.where(lane == j, s, out)
        return out
    dst = lax.broadcasted_iota(jnp.int32, (GW, LANES), 1)
    return _select_dot(v, (lax.broadcasted_iota(jnp.int32, (GW, LANES), 0) // SSD_HEAD_DIM == dst).astype(BF16))


def _ssd_dt_prep(zdt, bias, alog, ng):
    t = zdt.shape[0]
    q = SSD_CHUNK

    def body(z_ref, b_ref, a_ref, dt_ref, cum_ref, cumr_ref, sg_ref):
        raw = z_ref[...] + b_ref[...]
        dt = _softplus(raw)
        sgd = _sigmoid(raw)
        row = lax.broadcasted_iota(jnp.int32, (q, q), 0)
        col = lax.broadcasted_iota(jnp.int32, (q, q), 1)
        cum = _dot_f32((col <= row).astype(F32), dt * (-jnp.exp(a_ref[...])))
        cum_t = cum.T
        lane = lax.broadcasted_iota(jnp.int32, (q, LANES), 1)
        for g in range(ng):
            shift = (LANES - g * SSD_HPG) % LANES

            def group(v):
                return jnp.where(lane < SSD_HPG, pltpu.roll(v, shift, 1) if shift else v, 0.0)

            dt_ref[g] = group(dt)
            cum_ref[g] = group(cum)
            sg_ref[g] = group(sgd)
            cumr_ref[g] = (pltpu.roll(cum_t, shift, 0) if shift else cum_t)[0:8, :]

    cols = pl.BlockSpec((ng, q, LANES), lambda c: (0, c, 0))
    vec = pl.BlockSpec((1, LANES), lambda c: (0, 0))
    col_shape = jax.ShapeDtypeStruct((ng, t, LANES), F32)
    return pl.pallas_call(body, name="ssd_dt_prep", grid=(t // q,),
                          in_specs=[pl.BlockSpec((q, LANES), lambda c: (c, 0)), vec, vec],
                          out_specs=[cols, cols, pl.BlockSpec((ng, 8, q), lambda c: (0, 0, c)), cols],
                          out_shape=[col_shape, col_shape, jax.ShapeDtypeStruct((ng, 8, t), F32), col_shape],
                          compiler_params=_params("parallel"))(zdt, bias, alog)


def _ssd_common(pre, dt, cum, cum_r, alog_c, on_mxu):
    q = SSD_CHUNK
    sg = _sigmoid(pre)
    act = pre * sg
    xa = act[:, :GW]
    bm = act[:, GW:GW + SSD_D_STATE].astype(BF16)
    cm = act[:, GW + SSD_D_STATE:].astype(BF16)
    row = lax.broadcasted_iota(jnp.int32, (q, q), 0)
    col = lax.broadcasted_iota(jnp.int32, (q, q), 1)
    tril = col <= row
    a_c = -jnp.exp(alog_c)
    g = _dot_nt(cm, bm)
    cl = cum[q - 1:q, :]
    e_c = jnp.exp(cl - cum)
    lam_c = jnp.exp(cum)
    dt_x, e_x, lam_x = _expand(dt, q, on_mxu), _expand(e_c, q, on_mxu), _expand(lam_c, q, on_mxu)
    xdt = xa * dt_x
    return dict(sg=sg, xa=xa, bm=bm, cm=cm, tril=tril, row=row, col=col, dt=dt, a_c=a_c, cum=cum, cum_r=cum_r,
                g=g, dt_x=dt_x, xdt=xdt, cl=cl, e_c=e_c, lam_c=lam_c, e_x=e_x, lam_x=lam_x)


SSD_GPS_FWD = 8
SSD_GPS_BWD = 2


def _ssd_specs(nc, rev, ng, gps):
    q = SSD_CHUNK
    xw, nw = gps * GW, gps * SSD_D_STATE
    b_off = ng * GW // nw
    c_off = (ng * GW + ng * SSD_D_STATE) // nw
    assert ng % gps == 0 and (ng * GW) % nw == 0 and (ng * SSD_D_STATE) % nw == 0

    def ch(c):
        return nc - 1 - c if rev else c

    chunk_grp = [pl.BlockSpec((q, xw), lambda g, c: (ch(c), g)),
                 pl.BlockSpec((q, nw), lambda g, c: (ch(c), b_off + g)),
                 pl.BlockSpec((q, nw), lambda g, c: (ch(c), c_off + g))]
    col_form = pl.BlockSpec((gps, q, LANES), lambda g, c: (g, ch(c), 0))
    row_form = pl.BlockSpec((gps, 8, q), lambda g, c: (g, 0, ch(c)))
    col_par = pl.BlockSpec((gps, 1, LANES), lambda g, c: (g, 0, 0))
    y_spec = pl.BlockSpec((q, xw), lambda g, c: (ch(c), g))
    st_spec = pl.BlockSpec((gps, None, GW, SSD_D_STATE), lambda g, c: (g, ch(c), 0, 0))
    bc_spec = pl.BlockSpec((q, nw), lambda g, c: (ch(c), g))
    return chunk_grp, col_form, row_form, col_par, y_spec, st_spec, bc_spec


def _ssd_group_views(gi, wide, narrow, stacked):
    xs, ns = pl.ds(gi * GW, GW), pl.ds(gi * SSD_D_STATE, SSD_D_STATE)
    return [r.at[:, xs] for r in wide], [r.at[:, ns] for r in narrow], [r.at[gi] for r in stacked]


def _ssd_fwd(pre, dt_c, cum_c, cum_r, alog_c, dsk_c):
    t = pre.shape[0]
    ng = pre.shape[1] // GC
    q = SSD_CHUNK
    nc = t // q
    gps = SSD_GPS_FWD if ng % SSD_GPS_FWD == 0 else SSD_GPS_BWD
    chunk_grp, col_form, row_form, col_par, y_spec, st_spec, _ = _ssd_specs(nc, False, ng, gps)

    def body(px_ref, pb_ref, pc_ref, dt_ref, cum_ref, cumr_ref, ac_ref, dk_ref, y_ref, sp_ref, st_ref):
        @pl.when(pl.program_id(1) == 0)
        def _():
            st_ref[...] = jnp.zeros_like(st_ref)

        for gi in range(gps):
            (px, y), (pb, pc), rest = _ssd_group_views(
                gi, (px_ref, y_ref), (pb_ref, pc_ref), (dt_ref, cum_ref, cumr_ref, ac_ref, dk_ref, sp_ref, st_ref))
            one_group(px, pb, pc, *rest[:5], y, *rest[5:])

    def one_group(px_ref, pb_ref, pc_ref, dt_ref, cum_ref, cumr_ref, ac_ref, dk_ref, y_ref, sp_ref, st_ref):
        pre_v = jnp.concatenate([px_ref[...], pb_ref[...], pc_ref[...]], axis=1)
        v = _ssd_common(pre_v, dt_ref[...], cum_ref[...], cumr_ref[...], ac_ref[...], False)
        s0 = st_ref[...]
        sp_ref[...] = s0
        r = _dot_nt(v["cm"], s0.astype(BF16))
        y = v["lam_x"] * r + _expand(dk_ref[...], 1) * v["xa"]
        head = _head_of_lane((q, GW), SSD_HEAD_DIM)
        for j in range(SSD_HPG):
            diff = v["cum"][:, j:j + 1] - v["cum_r"][j:j + 1, :]
            w = (v["g"] * jnp.exp(jnp.where(v["tril"], diff, -jnp.inf))).astype(BF16)
            y = y + _dot(w, jnp.where(head == j, v["xdt"], 0.0).astype(BF16))
        y_ref[...] = y
        ds = _dot_tn((v["xdt"] * v["e_x"]).astype(BF16), v["bm"])
        for j in range(SSD_HPG):
            rows = slice(j * SSD_HEAD_DIM, (j + 1) * SSD_HEAD_DIM)
            st_ref[rows, :] = s0[rows, :] * jnp.exp(v["cum_r"][j:j + 1, q - 1:q]) + ds[rows, :]

    return pl.pallas_call(
        body, name="ssd_scan_fwd", grid=(ng // gps, nc),
        in_specs=chunk_grp + [col_form, col_form, row_form, col_par, col_par],
        out_specs=[y_spec, st_spec],
        out_shape=[jax.ShapeDtypeStruct((t, ng * GW), F32), jax.ShapeDtypeStruct((ng, nc, GW, SSD_D_STATE), F32)],
        scratch_shapes=[pltpu.VMEM((gps, GW, SSD_D_STATE), F32)],
        compiler_params=_params("parallel", "arbitrary"))(pre, pre, pre, dt_c, cum_c, cum_r, alog_c, dsk_c)


def _ssd_bwd(dy, pre, states, dt_c, cum_c, cum_r, sgd_c, alog_c, dsk_c):
    t = pre.shape[0]
    ng = pre.shape[1] // GC
    q = SSD_CHUNK
    nc = t // q
    gps = SSD_GPS_BWD
    chunk_grp, col_form, row_form, col_par, y_spec, st_spec, bc_spec = _ssd_specs(nc, True, ng, gps)

    def body(dy_ref, px_ref, pb_ref, pc_ref, sp_ref, dt_ref, cum_ref, cumr_ref, sgd_ref, ac_ref, dk_ref,
             dpx_ref, dpb_ref, dpc_ref, ddt_ref, dbias_ref, dalog_ref, dd_ref, ds_ref):
        @pl.when(pl.program_id(1) == 0)
        def _():
            ds_ref[...] = jnp.zeros_like(ds_ref)

        for gi in range(gps):
            (dy, px, dpx), (pb, pc, dpb, dpc), rest = _ssd_group_views(
                gi, (dy_ref, px_ref, dpx_ref), (pb_ref, pc_ref, dpb_ref, dpc_ref),
                (sp_ref, dt_ref, cum_ref, cumr_ref, sgd_ref, ac_ref, dk_ref, ddt_ref, dbias_ref, dalog_ref, dd_ref,
                 ds_ref))
            one_group(dy, px, pb, pc, *rest[:7], dpx, dpb, dpc, *rest[7:])

    def one_group(dy_ref, px_ref, pb_ref, pc_ref, sp_ref, dt_ref, cum_ref, cumr_ref, sgd_ref, ac_ref, dk_ref,
                  dpx_ref, dpb_ref, dpc_ref, ddt_ref, dbias_ref, dalog_ref, dd_ref, ds_ref):
        first = pl.program_id(1) == 0
        pre_v = jnp.concatenate([px_ref[...], pb_ref[...], pc_ref[...]], axis=1)
        v = _ssd_common(pre_v, dt_ref[...], cum_ref[...], cumr_ref[...], ac_ref[...], True)
        xa, bm, cm, xdt, cum, cum_r = v["xa"], v["bm"], v["cm"], v["xdt"], v["cum"], v["cum_r"]
        xdt_b = xdt.astype(BF16)
        dy_v = dy_ref[...]
        s0 = sp_ref[...]
        ds1 = ds_ref[...]
        s0b, ds1b = s0.astype(BF16), ds1.astype(BF16)
        head = _head_of_lane((q, GW), SSD_HEAD_DIM)
        lane = lax.broadcasted_iota(jnp.int32, (q, LANES), 1)
        lane1 = lax.broadcasted_iota(jnp.int32, (1, LANES), 1)
        lam_x, e_x = v["lam_x"], v["e_x"]

        dxa = _expand(dk_ref[...], 1) * dy_v
        dd = _contract(jnp.sum(dy_v * xa, axis=0, keepdims=True), 1)
        r = _dot_nt(cm, s0b)
        dcum = _contract(dy_v * r * lam_x, q, True)
        drb = (lam_x * dy_v).astype(BF16)
        dc = _dot(drb, s0b)
        ds0 = _dot_tn(drb, cm)
        extra = jnp.zeros((1, LANES), F32)
        for j in range(SSD_HPG):
            rows = slice(j * SSD_HEAD_DIM, (j + 1) * SSD_HEAD_DIM)
            lam_last = jnp.exp(cum_r[j:j + 1, q - 1:q])
            ds_ref[rows, :] = ds0[rows, :] + lam_last * ds1[rows, :]
            tot = jnp.sum(jnp.sum(ds1[rows, :] * s0[rows, :], axis=1, keepdims=True), axis=0, keepdims=True)
            extra = jnp.where(lane1 == j, lam_last * tot, extra)
        dv = _dot_nt(bm, ds1b)
        db = _dot((xdt * e_x).astype(BF16), ds1b)
        dxdt = e_x * dv
        dee = _contract(dv * xdt, q, True) * v["e_c"]
        dcum = dcum - dee
        extra = extra + jnp.sum(dee, axis=0, keepdims=True)
        dg = jnp.zeros((q, q), F32)
        col_sums = jnp.zeros((q, q), F32)
        for j in range(SSD_HPG):
            diff = cum[:, j:j + 1] - cum_r[j:j + 1, :]
            el = jnp.exp(jnp.where(v["tril"], diff, -jnp.inf))
            gl = v["g"] * el
            dym = jnp.where(head == j, dy_v, 0.0).astype(BF16)
            dwm = _dot_nt(dym, xdt_b)
            dxdt = dxdt + _dot_tn(gl.astype(BF16), dym)
            z = dwm * gl
            dcum = jnp.where(lane == j, dcum + jnp.sum(z, axis=1, keepdims=True), dcum)
            col_sums = jnp.where(v["row"] == j, jnp.sum(z, axis=0, keepdims=True), col_sums)
            dg = dg + dwm * el
        dcum = dcum - col_sums.T
        dgb = dg.astype(BF16)
        dc = dc + _dot(dgb, bm)
        db = db + _dot_tn(dgb, cm)
        da = _select_dot(dcum, (v["row"] <= v["col"]).astype(BF16), True) + extra
        ddt = _contract(dxdt * xa, q, True) + v["a_c"] * da
        dalog = jnp.sum(v["dt"] * da, axis=0, keepdims=True) * v["a_c"]
        dxa = dxa + v["dt_x"] * dxdt
        ddt_raw = jnp.where(lane < SSD_HPG, ddt * sgd_ref[...], 0.0)
        sgrad = _silu_grad(pre_v, v["sg"])
        dpx_ref[...] = dxa * sgrad[:, :GW]
        dpb_ref[...] = db * sgrad[:, GW:GW + SSD_D_STATE]
        dpc_ref[...] = dc * sgrad[:, GW + SSD_D_STATE:]
        ddt_ref[...] = ddt_raw
        _acc(dbias_ref, jnp.sum(ddt_raw, axis=0, keepdims=True), first)
        _acc(dalog_ref, jnp.where(lane1 < SSD_HPG, dalog, 0.0), first)
        _acc(dd_ref, dd, first)

    return pl.pallas_call(
        body, name="ssd_scan_bwd", grid=(ng // gps, nc),
        in_specs=[y_spec] + chunk_grp + [st_spec, col_form, col_form, row_form, col_form, col_par, col_par],
        out_specs=[y_spec, bc_spec, bc_spec, col_form, col_par, col_par, col_par],
        out_shape=[jax.ShapeDtypeStruct((t, ng * GW), F32), jax.ShapeDtypeStruct((t, ng * SSD_D_STATE), F32),
                   jax.ShapeDtypeStruct((t, ng * SSD_D_STATE), F32), jax.ShapeDtypeStruct((ng, t, LANES), F32),
                   jax.ShapeDtypeStruct((ng, 1, LANES), F32), jax.ShapeDtypeStruct((ng, 1, LANES), F32),
                   jax.ShapeDtypeStruct((ng, 1, LANES), F32)],
        scratch_shapes=[pltpu.VMEM((gps, GW, SSD_D_STATE), F32)],
        compiler_params=_params("parallel", "arbitrary"))(dy, pre, pre, pre, states, dt_c, cum_c, cum_r, sgd_c, alog_c,
                                                           dsk_c)


def _gate_norm_fwd(y, zx, norm_w):
    t, di = y.shape
    tr = _tile(t, 512, 8)
    ng = di // GW

    def body(y_ref, z_ref, w_ref, o_ref):
        z = z_ref[...]
        gate = y_ref[...] * (z * _sigmoid(z))
        w = w_ref[...]
        for g in range(ng):
            cols = slice(g * GW, (g + 1) * GW)
            gs = gate[:, cols]
            r = lax.rsqrt(jnp.mean(gs * gs, axis=-1, keepdims=True) + NORM_EPS)
            o_ref[:, cols] = (gs * r * w[:, cols]).astype(BF16)

    row = pl.BlockSpec((tr, di), lambda i: (i, 0))
    return pl.pallas_call(body, name="ssd_gate_norm_fwd", grid=(t // tr,),
                          in_specs=[row, row, pl.BlockSpec((1, di), lambda i: (0, 0))], out_specs=row,
                          out_shape=jax.ShapeDtypeStruct((t, di), BF16), compiler_params=_params("parallel"))(
                              y, zx, norm_w)


def _gate_norm_bwd(dyn, y, zx, norm_w, after):
    t, di = y.shape
    tr = _tile(t, 256, 8)
    ng = di // GW

    def body(d_ref, y_ref, z_ref, w_ref, after_ref, dy_ref, dz_ref, dw_ref):
        z = z_ref[...]
        yv = y_ref[...]
        sg = _sigmoid(z)
        sz = z * sg
        gate = yv * sz
        w = w_ref[...]
        d = d_ref[...]
        dsz = _silu_grad(z, sg)
        dws = []
        for g in range(ng):
            cols = slice(g * GW, (g + 1) * GW)
            dg, dwr = _rms_bwd(gate[:, cols], w[:, cols], d[:, cols])
            dy_ref[:, cols] = dg * sz[:, cols]
            dz_ref[:, cols] = (dg * yv[:, cols] * dsz[:, cols]).astype(BF16)
            dws.append(jnp.sum(dwr, axis=0, keepdims=True))
        first = pl.program_id(0) == 0
        for g in range(ng):
            cols = slice(g * GW, (g + 1) * GW)

            @pl.when(first)
            def _():
                dw_ref[:, cols] = dws[g]

            @pl.when(jnp.logical_not(first))
            def _():
                dw_ref[:, cols] += dws[g]

    row = pl.BlockSpec((tr, di), lambda i: (i, 0))
    vec = pl.BlockSpec((1, di), lambda i: (0, 0))
    return pl.pallas_call(body, name="ssd_gate_norm_bwd", grid=(t // tr,),
                          in_specs=[row, row, row, vec, pl.BlockSpec((8, LANES), lambda i: (0, 0))],
                          out_specs=[row, row, vec],
                          out_shape=[jax.ShapeDtypeStruct((t, di), F32), jax.ShapeDtypeStruct((t, di), BF16),
                                     jax.ShapeDtypeStruct((1, di), F32)],
                          compiler_params=_params("arbitrary"))(dyn, y, zx, norm_w, after)


def _attn_mask_t(n):
    w = ATTN_WINDOW
    kpos = lax.broadcasted_iota(jnp.int32, (2 * w, ATTN_REP * w), 0)
    qpos = lax.broadcasted_iota(jnp.int32, (2 * w, ATTN_REP * w), 1) % w + w
    rel = qpos - kpos
    return (rel >= 0) & (rel < w) & jnp.logical_not((n == 0) & (kpos < w))


def _attn_probs_t(qts, ktb, mask, sink):
    s = _dot_tn(ktb, qts) * (ATTN_HEAD_DIM ** -0.5)
    s = jnp.where(mask, s, -jnp.inf)
    m = jnp.maximum(jnp.max(s, axis=0, keepdims=True), sink)
    e = jnp.exp(s - m)
    es = jnp.exp(sink - m)
    inv = 1.0 / (jnp.sum(e, axis=0, keepdims=True) + es)
    return e * inv, es * inv


def _attn_blocks_t(kv, q_ref, kc_ref, vc_ref, kp_ref, vp_ref):
    hd = ATTN_HEAD_DIM
    rows = slice(kv * hd, (kv + 1) * hd)
    ktb = jnp.concatenate([kp_ref[rows, :], kc_ref[rows, :]], axis=1)
    vtb = jnp.concatenate([vp_ref[rows, :], vc_ref[rows, :]], axis=1)
    qts = jnp.concatenate([q_ref[(kv * ATTN_REP + r) * hd:(kv * ATTN_REP + r + 1) * hd, :]
                           for r in range(ATTN_REP)], axis=1)
    return qts, ktb, vtb


def _attn_specs_t(nb, cur, prev):
    w, hd = ATTN_WINDOW, ATTN_HEAD_DIM
    kd = ATTN_N_KV * hd
    qd = ATTN_REP * kd
    return [pl.BlockSpec((qd, w), lambda n: (0, cur(n))),
            pl.BlockSpec((kd, w), lambda n: (ATTN_REP, cur(n))),
            pl.BlockSpec((kd, w), lambda n: (ATTN_REP + 1, cur(n))),
            pl.BlockSpec((kd, w), lambda n: (ATTN_REP, prev(n))),
            pl.BlockSpec((kd, w), lambda n: (ATTN_REP + 1, prev(n)))]


def _attn_fwd_t(qkv_t, sinks_rep):
    t = qkv_t.shape[1]
    w, hd = ATTN_WINDOW, ATTN_HEAD_DIM
    qd = ATTN_N_KV * ATTN_REP * hd
    nb = t // w

    def body(q_ref, kc_ref, vc_ref, kp_ref, vp_ref, s_ref, o_ref):
        mask = _attn_mask_t(pl.program_id(0))
        for kv in range(ATTN_N_KV):
            qts, ktb, vtb = _attn_blocks_t(kv, q_ref, kc_ref, vc_ref, kp_ref, vp_ref)
            p, _ = _attn_probs_t(qts, ktb, mask, s_ref[kv])
            ots = _dot(vtb, p.astype(BF16))
            for r in range(ATTN_REP):
                h = kv * ATTN_REP + r
                o_ref[h * hd:(h + 1) * hd, :] = ots[:, r * w:(r + 1) * w].astype(BF16)

    return pl.pallas_call(
        body, name="attn_fwd", grid=(nb,),
        in_specs=_attn_specs_t(nb, lambda n: n, lambda n: jnp.maximum(n - 1, 0)) + [
            pl.BlockSpec(sinks_rep.shape, lambda n: (0, 0, 0))],
        out_specs=pl.BlockSpec((qd, w), lambda n: (0, n)),
        out_shape=jax.ShapeDtypeStruct((qd, t), BF16),
        compiler_params=_params("parallel"))(qkv_t, qkv_t, qkv_t, qkv_t, qkv_t, sinks_rep)


def _attn_bwd_t(qkv_t, do_t, sinks_rep):
    t = qkv_t.shape[1]
    w, hd = ATTN_WINDOW, ATTN_HEAD_DIM
    kd = ATTN_N_KV * hd
    qd = ATTN_REP * kd
    nq = ATTN_N_KV * ATTN_REP
    nb = t // w
    rows_all = qd + 2 * kd

    def body(q_ref, kc_ref, vc_ref, kp_ref, vp_ref, do_ref, s_ref, dqkv_ref, bsum_ref, dsk_ref,
             carry_ref, new_ref, bacc_ref, sacc_ref):
        n = pl.program_id(0)

        @pl.when(n == 0)
        def _():
            carry_ref[...] = jnp.zeros_like(carry_ref)
            bacc_ref[...] = jnp.zeros_like(bacc_ref)
            sacc_ref[...] = jnp.zeros_like(sacc_ref)

        @pl.when(n < nb)
        def _():
            mask = _attn_mask_t(n)
            for kv in range(ATTN_N_KV):
                qts, ktb, vtb = _attn_blocks_t(kv, q_ref, kc_ref, vc_ref, kp_ref, vp_ref)
                dots = jnp.concatenate([do_ref[(kv * ATTN_REP + r) * hd:(kv * ATTN_REP + r + 1) * hd, :]
                                        for r in range(ATTN_REP)], axis=1)
                p, ps = _attn_probs_t(qts, ktb, mask, s_ref[kv])
                dpt = _dot_tn(vtb, dots)
                delta = jnp.sum(p * dpt, axis=0, keepdims=True)
                dst = (p * (dpt - delta) * (hd ** -0.5)).astype(BF16)
                dqts = _dot(ktb, dst)
                for r in range(ATTN_REP):
                    h = kv * ATTN_REP + r
                    new_ref[h * hd:(h + 1) * hd, :] = dqts[:, r * w:(r + 1) * w]
                dktb = _dot_nt(qts, dst)
                dvtb = _dot_nt(dots, p.astype(BF16))
                krows = slice(qd + kv * hd, qd + (kv + 1) * hd)
                vrows = slice(qd + kd + kv * hd, qd + kd + (kv + 1) * hd)
                carry_ref[krows, :] += dktb[:, :w]
                carry_ref[vrows, :] += dvtb[:, :w]
                new_ref[krows, :] = dktb[:, w:]
                new_ref[vrows, :] = dvtb[:, w:]
                sacc_ref[kv] += -(ps * delta)

        @pl.when(n >= 1)
        def _():
            done = carry_ref[...]
            dqkv_ref[...] = done.astype(BF16)
            bacc_ref[...] += done

        @pl.when(n < nb)
        def _():
            carry_ref[...] = new_ref[...]

        @pl.when(n == nb)
        def _():
            bsum_ref[...] = jnp.sum(bacc_ref[...], axis=1, keepdims=True)
            lane = lax.broadcasted_iota(jnp.int32, (1, nq), 1)
            dsk = jnp.zeros((1, nq), F32)
            for kv in range(ATTN_N_KV):
                acc = sacc_ref[kv]
                for r in range(ATTN_REP):
                    tot = jnp.sum(acc[:, r * w:(r + 1) * w], axis=1, keepdims=True)
                    dsk = jnp.where(lane == kv * ATTN_REP + r, tot, dsk)
            dsk_ref[...] = dsk

    cur = lambda n: jnp.minimum(n, nb - 1)
    prev = lambda n: jnp.maximum(jnp.minimum(n, nb - 1) - 1, 0)
    return pl.pallas_call(
        body, name="attn_bwd", grid=(nb + 1,),
        in_specs=_attn_specs_t(nb, cur, prev) + [pl.BlockSpec((qd, w), lambda n: (0, cur(n))),
                                                 pl.BlockSpec(sinks_rep.shape, lambda n: (0, 0, 0))],
        out_specs=[pl.BlockSpec((rows_all, w), lambda n: (0, jnp.maximum(n - 1, 0))),
                   pl.BlockSpec((rows_all, 1), lambda n: (0, 0)),
                   pl.BlockSpec((1, nq), lambda n: (0, 0))],
        out_shape=[jax.ShapeDtypeStruct((rows_all, t), BF16), jax.ShapeDtypeStruct((rows_all, 1), F32),
                   jax.ShapeDtypeStruct((1, nq), F32)],
        scratch_shapes=[pltpu.VMEM((rows_all, w), F32), pltpu.VMEM((rows_all, w), F32),
                        pltpu.VMEM((rows_all, w), F32), pltpu.VMEM(sinks_rep.shape, F32)],
        compiler_params=_params("arbitrary"))(qkv_t, qkv_t, qkv_t, qkv_t, qkv_t, do_t, sinks_rep)


HBM_SPEC = pl.BlockSpec(memory_space=pl.ANY)
HBM_ONLY = pl.BlockSpec(memory_space=pltpu.HBM)


def _comm_call(name, body, ins, out_shapes, n_sems):
    return pl.pallas_call(
        body, name=name, in_specs=[HBM_SPEC] * len(ins), out_specs=[HBM_SPEC] * len(out_shapes),
        out_shape=out_shapes,
        scratch_shapes=[pltpu.SemaphoreType.DMA((s,)) for s in n_sems])(*ins)


def _all_gather(name, shards, after):
    n = len(shards)
    na = len(after)

    def body(*refs):
        x_refs, out_refs = refs[:n], refs[n + na:2 * n + na]
        send_sems, recv_sems, local_sems = refs[2 * n + na:]
        x, y, c = lax.axis_index("x"), lax.axis_index("y"), lax.axis_index("c")
        me, sibling = (x, y, c), (x, y, 1 - c)
        chips = [(1 - x, y), (x, 1 - y), (1 - x, 1 - y)]

        def slot(i, px, py, pc):
            return out_refs[i].at[4 * px + 2 * py + pc]

        def copy(k, i, block, to, src=None):
            return pltpu.make_async_remote_copy(
                src_ref=slot(i, *block) if src is None else src, dst_ref=slot(i, *block),
                send_sem=send_sems.at[k * n + i], recv_sem=recv_sems.at[k * n + i], device_id=to,
                device_id_type=MESH)

        mine = [pltpu.make_async_copy(x_refs[i], slot(i, *me), local_sems.at[i]) for i in range(n)]
        first = []
        for i in range(n):
            mine[i].start()
            first.append(copy(0, i, me, sibling, src=x_refs[i]))
            first += [copy(1 + j, i, me, (*chip, c), src=x_refs[i]) for j, chip in enumerate(chips)]
        for cp in first:
            cp.start()
        passed = []
        for i in range(n):
            for j, chip in enumerate(chips):
                copy(1 + j, i, (*chip, c), me).wait_recv()
                passed.append(copy(4 + j, i, (*chip, c), sibling))
                passed[-1].start()
        for i in range(n):
            copy(0, i, sibling, me).wait_recv()
            for j, chip in enumerate(chips):
                copy(4 + j, i, (*chip, 1 - c), me).wait_recv()
        for cp in first + passed:
            cp.wait_send()
        for cp in mine:
            cp.wait()

    outs = [jax.ShapeDtypeStruct((N_DEV,) + s.shape, s.dtype) for s in shards]
    return _comm_call(name, body, list(shards) + list(after), outs, (7 * n, 7 * n, n))


SEM_SPEC = pl.BlockSpec(memory_space=pltpu.SEMAPHORE)
SPLIT_COPY_EFFECT = pltpu.SideEffectType.DATAFLOW_SIDE_EFFECTING


def _in_hbm(a):
    return pltpu.with_memory_space_constraint(a, pltpu.HBM)


def _split_start(name, body, srcs, lands, n_sems):
    n = len(srcs)
    bufs = [_in_hbm(a) for a in list(srcs) + list(lands)]
    outs = pl.pallas_call(
        body, name=name,
        out_shape=(pltpu.SemaphoreType.DMA((n_sems,)), pltpu.SemaphoreType.DMA((n_sems,)),
                   *[pltpu.HBM(a.shape, a.dtype) for a in bufs], jax.ShapeDtypeStruct((8, LANES), F32)),
        in_specs=[HBM_ONLY] * (2 * n),
        out_specs=(SEM_SPEC, SEM_SPEC, *[HBM_ONLY] * (2 * n), pl.BlockSpec(memory_space=pltpu.VMEM)),
        input_output_aliases={i: 2 + i for i in range(2 * n)},
        compiler_params=pltpu.CompilerParams(has_side_effects=SPLIT_COPY_EFFECT))(*bufs)
    return outs[0], outs[1], list(outs[2:2 + n]), list(outs[2 + n:2 + 2 * n]), outs[-1]


def _split_wait(name, body, send_sems, recv_sems, srcs, lands, after):
    n = len(srcs)
    outs = pl.pallas_call(
        body, name=name,
        out_shape=[pltpu.HBM(a.shape, a.dtype) for a in list(srcs) + list(lands)],
        in_specs=[HBM_ONLY] * (2 * n) + [SEM_SPEC, SEM_SPEC, HBM_SPEC],
        out_specs=[HBM_ONLY] * (2 * n),
        input_output_aliases={i: i for i in range(2 * n)},
        compiler_params=pltpu.CompilerParams(has_side_effects=SPLIT_COPY_EFFECT))(
            *srcs, *lands, send_sems, recv_sems, after)
    return list(outs[:n]), list(outs[n:])


N_PEERS = N_DEV - 1


def _gather_peers():
    x, y, c = lax.axis_index("x"), lax.axis_index("y"), lax.axis_index("c")
    flips = [(fx, fy, fc) for fx in (0, 1) for fy in (0, 1) for fc in (0, 1) if fx or fy or fc]
    return [(1 - x if fx else x, 1 - y if fy else y, 1 - c if fc else c) for fx, fy, fc in flips]


def _block_id(dev):
    return 4 * dev[0] + 2 * dev[1] + dev[2]


def _landing_block(land_ref, shard_shape, side_by_side, dev):
    if not side_by_side:
        return land_ref.at[_block_id(dev)]
    cols = shard_shape[1]
    return land_ref.at[:, pl.ds(pl.multiple_of(_block_id(dev) * cols, LANES), cols)]


def _gather_start(name, shards, side_by_side):
    n = len(shards)

    def body(*refs):
        x_refs, land_refs = refs[:n], refs[n:2 * n]
        send_sems, recv_sems, token = refs[2 * n], refs[2 * n + 1], refs[-1]
        me = (lax.axis_index("x"), lax.axis_index("y"), lax.axis_index("c"))
        for i in range(n):
            for k, peer in enumerate(_gather_peers()):
                pltpu.make_async_remote_copy(
                    src_ref=x_refs[i], dst_ref=_landing_block(land_refs[i], shards[i].shape, side_by_side[i], me),
                    send_sem=send_sems.at[N_PEERS * i + k], recv_sem=recv_sems.at[N_PEERS * i + k],
                    device_id=peer, device_id_type=MESH).start()
            pltpu.make_async_copy(x_refs[i], _landing_block(land_refs[i], shards[i].shape, side_by_side[i], me),
                                  send_sems.at[N_PEERS * n + i]).start()
        token[...] = jnp.zeros_like(token)

    lands = [lax.empty((s.shape[0], N_DEV * s.shape[1]) if wide else (N_DEV,) + s.shape, s.dtype)
             for s, wide in zip(shards, side_by_side)]
    return _split_start(name, body, shards, lands, (N_PEERS + 1) * n)


def _gather_wait(name, send_sems, recv_sems, first, n_all, shards, lands, side_by_side, after):
    n = len(shards)

    def body(*refs):
        x_refs, land_refs = refs[:n], refs[n:2 * n]
        send_sems, recv_sems = refs[2 * n], refs[2 * n + 1]
        me = (lax.axis_index("x"), lax.axis_index("y"), lax.axis_index("c"))
        for i in range(n):
            pltpu.make_async_copy(x_refs[i], _landing_block(land_refs[i], shards[i].shape, side_by_side[i], me),
                                  send_sems.at[N_PEERS * n_all + first + i]).wait()
            for k, peer in enumerate(_gather_peers()):
                cp = pltpu.make_async_remote_copy(
                    src_ref=x_refs[i], dst_ref=_landing_block(land_refs[i], shards[i].shape, side_by_side[i], peer),
                    send_sem=send_sems.at[N_PEERS * (first + i) + k],
                    recv_sem=recv_sems.at[N_PEERS * (first + i) + k],
                    device_id=peer, device_id_type=MESH)
                cp.wait_send()
                cp.wait_recv()

    return _split_wait(name, body, send_sems, recv_sems, shards, lands, after)


def _scatter_start(name, blocks):
    n = len(blocks)

    def body(*refs):
        b_refs, land_refs = refs[:n], refs[n:2 * n]
        send_sems, recv_sems, token = refs[2 * n], refs[2 * n + 1], refs[-1]
        me = (lax.axis_index("x"), lax.axis_index("y"), lax.axis_index("c"))
        for i in range(n):
            for k, peer in enumerate(_gather_peers()):
                pltpu.make_async_remote_copy(
                    src_ref=b_refs[i].at[_block_id(peer)], dst_ref=land_refs[i].at[_block_id(me)],
                    send_sem=send_sems.at[N_PEERS * i + k], recv_sem=recv_sems.at[N_PEERS * i + k],
                    device_id=peer, device_id_type=MESH).start()
            pltpu.make_async_copy(b_refs[i].at[_block_id(me)], land_refs[i].at[_block_id(me)],
                                  send_sems.at[N_PEERS * n + i]).start()
        token[...] = jnp.zeros_like(token)

    lands = [lax.empty(b.shape, b.dtype) for b in blocks]
    return _split_start(name, body, blocks, lands, (N_PEERS + 1) * n)


def _scatter_wait(name, send_sems, recv_sems, blocks, lands, after):
    n = len(blocks)

    def body(*refs):
        b_refs, land_refs = refs[:n], refs[n:2 * n]
        send_sems, recv_sems = refs[2 * n], refs[2 * n + 1]
        me = (lax.axis_index("x"), lax.axis_index("y"), lax.axis_index("c"))
        for i in range(n):
            pltpu.make_async_copy(b_refs[i].at[_block_id(me)], land_refs[i].at[_block_id(me)],
                                  send_sems.at[N_PEERS * n + i]).wait()
            for k, peer in enumerate(_gather_peers()):
                cp = pltpu.make_async_remote_copy(
                    src_ref=b_refs[i].at[_block_id(peer)], dst_ref=land_refs[i].at[_block_id(peer)],
                    send_sem=send_sems.at[N_PEERS * i + k], recv_sem=recv_sems.at[N_PEERS * i + k],
                    device_id=peer, device_id_type=MESH)
                cp.wait_send()
                cp.wait_recv()

    return _split_wait(name, body, send_sems, recv_sems, blocks, lands, after)


def _adamw(w, g, m, v):
    m = ADAM_B1 * m + (1.0 - ADAM_B1) * g
    v = ADAM_B2 * v + (1.0 - ADAM_B2) * (g * g)
    m_hat = m / (1.0 - ADAM_B1 ** ADAM_STEP)
    v_hat = v / (1.0 - ADAM_B2 ** ADAM_STEP)
    delta = -ADAM_LR * (m_hat / (jnp.sqrt(v_hat) + ADAM_EPS) + ADAM_WD * w)
    return delta, m, v


def _adamw_tiles(r, c_):
    tr = _tile(r, 256, 16)
    return (tr, c_) if tr < r or r <= 256 else (r, _tile(c_, 256))


def _sum_parts(part):
    g = part[0].astype(F32)
    for k in range(1, part.shape[0]):
        g = g + part[k].astype(F32)
    return g


def _sum_adamw(name, parts, w, m, v):
    r, c_ = w.shape
    tr, tc = _adamw_tiles(r, c_)

    def body(p_ref, w_ref, m_ref, v_ref, g_ref, d_ref, nm_ref, nv_ref):
        g = _sum_parts(p_ref)
        g_ref[...] = g
        d_ref[...], nm_ref[...], nv_ref[...] = _adamw(w_ref[...], g, m_ref[...], v_ref[...])

    tile = pl.BlockSpec((tr, tc), lambda i, j: (i, j))
    return pl.pallas_call(body, name=name, grid=(r // tr, c_ // tc),
                          in_specs=[pl.BlockSpec((parts.shape[0], tr, tc), lambda i, j: (0, i, j)), tile, tile, tile],
                          out_specs=[tile] * 4, out_shape=[jax.ShapeDtypeStruct((r, c_), F32)] * 4,
                          compiler_params=_params("parallel", "parallel"))(parts, w, m, v)


def _sum_adamw_layers(name, parts, w, m, v):
    n_layers, r, c_ = w.shape
    tr = _tile(r, 256, 16)

    def body(*refs):
        p_refs = refs[:n_layers]
        w_ref, m_ref, v_ref, g_ref, d_ref, nm_ref, nv_ref = refs[n_layers:]
        layer = pl.program_id(0)
        g = _sum_parts(p_refs[0])
        for li in range(1, n_layers):
            g = jnp.where(layer == li, _sum_parts(p_refs[li]), g)
        g_ref[...] = g
        d_ref[...], nm_ref[...], nv_ref[...] = _adamw(w_ref[...], g, m_ref[...], v_ref[...])

    row = pl.BlockSpec((None, tr, c_), lambda l, i: (l, i, 0))
    specs = [pl.BlockSpec((p.shape[0], tr, c_), lambda l, i: (0, i, 0)) for p in parts]
    return pl.pallas_call(body, name=name, grid=(n_layers, r // tr), in_specs=specs + [row, row, row],
                          out_specs=[row] * 4, out_shape=[jax.ShapeDtypeStruct(w.shape, F32)] * 4,
                          compiler_params=_params("parallel", "parallel"))(*parts, w, m, v)


def _pack_rows(flat, n_rows, cols):
    pad = n_rows * cols - flat.shape[-1]
    flat = jnp.pad(flat, [(0, 0)] * (flat.ndim - 1) + [(0, pad)])
    return flat.reshape(flat.shape[:-1] + (n_rows, cols))


def _cols_split(full):
    c = full.shape[1] // N_DEV
    return jnp.stack([full[:, d * c:(d + 1) * c] for d in range(N_DEV)])


def _rows_join(blocks):
    return blocks.reshape(N_DEV * blocks.shape[1], blocks.shape[2])


def _rows_split(full):
    return full.reshape(N_DEV, full.shape[0] // N_DEV, full.shape[1])


def _heads_col(v, ng):
    return jnp.pad(v.reshape(ng, 1, SSD_HPG), ((0, 0), (0, 0), (0, LANES - SSD_HPG)))


MATRIX_ITEMS = ("w_in", "w_out", "up0", "down0", "w_qkv", "w_o", "up1", "down1")
VECTOR_ITEMS = ("conv_w", "b_qkv", "b_o")
ITEMS = MATRIX_ITEMS + VECTOR_ITEMS
GATHER_STAGES = (("w_in", "conv_w"), ("w_out", "up0", "down0"), ("w_qkv", "b_qkv", "w_o", "b_o", "up1", "down1"))
SIDE_BY_SIDE = ("conv_w", "up0", "up1", "b_o")


def _items(tree, prefix=""):
    g = lambda k: tree[prefix + k]
    return {"w_in": g("ssd_w_in")[0].T, "w_out": g("ssd_w_out")[0], "w_qkv": g("attn_w_qkv")[0].T,
            "w_o": g("attn_w_o")[0], "up0": g("mlp_w_up")[0], "up1": g("mlp_w_up")[1],
            "down0": g("mlp_w_down")[0], "down1": g("mlp_w_down")[1], "conv_w": g("ssd_conv_w")[0],
            "b_qkv": g("attn_b_qkv"), "b_o": g("attn_b_o")}


REPLICATED = ("ssd_conv_b", "ssd_dt_bias", "ssd_a_log", "ssd_d", "ssd_norm_w", "attn_sinks", "mix_pre_norm",
              "mix_post_norm", "ffn_pre_norm", "ffn_post_norm")
WEIGHTS = ("ssd_w_in", "ssd_conv_w", "ssd_conv_b", "ssd_dt_bias", "ssd_a_log", "ssd_d", "ssd_norm_w", "ssd_w_out",
           "attn_w_qkv", "attn_b_qkv", "attn_sinks", "attn_w_o", "attn_b_o", "mlp_w_up", "mlp_w_down",
           "mix_pre_norm", "mix_post_norm", "ffn_pre_norm", "ffn_post_norm")


def _forward_backward(x, target, rep, token, weights_of_stage, reduce_grads):
    t, d = x.shape
    ng = rep["ssd_norm_w"].shape[1] // GW
    di = ng * GW
    n_xbc = ng * GC
    nh = ng * SSD_HPG
    grads, blocks = {}, {}
    w_up, w_down = [None, None], [None, None]
    sinks_rep = jnp.repeat(rep["attn_sinks"].reshape(ATTN_N_KV, ATTN_REP, 1), ATTN_WINDOW, axis=2).reshape(
        ATTN_N_KV, 1, ATTN_REP * ATTN_WINDOW)
    conv_b = rep["ssd_conv_b"]
    gn = ng * SSD_D_STATE
    parts = ((0, di), (di, di), (2 * di, gn), (2 * di + gn, gn), (di + n_xbc, nh))
    alog_c, dsk_c = (_heads_col(rep[k], ng) for k in ("ssd_a_log", "ssd_d"))
    bias_l, alog_l = (jnp.pad(rep[k], ((0, 0), (0, LANES - nh))) for k in ("ssd_dt_bias", "ssd_a_log"))
    norm = {k: rep[k] for k in ("mix_pre_norm", "mix_post_norm", "ffn_pre_norm", "ffn_post_norm")}

    def nrow(name, i):
        return norm[name][i:i + 1]

    def mlp_fwd(i, u2):
        p = _mm(f"mlp{i}_up", [u2], [w_up[i]], "nn", tm=1024, tn=1024, out_dtypes=(BF16,),
                epilogue=lambda acc: (jnp.square(jnp.maximum(acc, 0.0)),))
        f = _mm(f"mlp{i}_down", [p], [w_down[i]], "nn", tm=512, tn=1024)
        return p, f

    def mlp_bwd(i, df, u2, p):
        da = _mm(f"mlp{i}_dact", [df], [w_down[i]], "nt", tm=1024, tn=1024, out_dtypes=(BF16,),
                 tiles=(p,), epilogue=lambda acc, pv: (acc * (2.0 * jnp.sqrt(pv.astype(F32))),))
        blocks[f"down{i}"] = _rows_split(_mm(f"mlp{i}_dwdown", [p], [df], "tn", tm=512, tn=1024,
                                             out_dtypes=(PAYLOAD,)))
        blocks[f"up{i}"] = _mm(f"mlp{i}_dwup", [u2], [da], "tn", tm=1024, tn=da.shape[1] // N_DEV,
                               out_dtypes=(PAYLOAD,), col_blocks=True)
        return _mm(f"mlp{i}_dx", [da], [w_up[i]], "nt", tm=512, tn=1024)

    u0 = _prenorm("l0_prenorm", x, nrow("mix_pre_norm", 0), token)
    got = weights_of_stage(0, u0)
    w_in_t = _rows_join(got["w_in"])
    w_dt_t = jnp.pad(w_in_t[di + n_xbc:], ((0, LANES - nh), (0, 0)))
    conv_w = got["conv_w"]
    zx = _mm("ssd_in_proj", [u0], [w_in_t], "nt", tm=1024, tn=1024, n_use=di + n_xbc)
    zdt = _mm("ssd_dt_proj", [u0], [w_dt_t], "nt", tm=1024, tn=LANES)
    pre = _conv_fwd(zx, di, n_xbc, conv_w, conv_b)
    dt_c, cum_c, cum_r, sgd_c = _ssd_dt_prep(zdt, bias_l, alog_l, ng)
    y, states = _ssd_fwd(pre, dt_c, cum_c, cum_r, alog_c, dsk_c)
    yn = _gate_norm_fwd(y, zx, rep["ssd_norm_w"])
    got = weights_of_stage(1, yn)
    w_out = _rows_join(got["w_out"])
    w_up[0], w_down[0] = got["up0"], _rows_join(got["down0"])
    mix0 = _mm("ssd_out_proj", [yn], [w_out], "nn", tm=1024, tn=1024)
    h1, u0f = _post_pre("l0_mid", x, mix0, nrow("mix_post_norm", 0), nrow("ffn_pre_norm", 0))
    p0, f0 = mlp_fwd(0, u0f)
    h2, u1 = _post_pre("l1_in", h1, f0, nrow("ffn_post_norm", 0), nrow("mix_pre_norm", 1))
    got = weights_of_stage(2, u1)
    w_qkv_t = _rows_join(got["w_qkv"])
    w_o = _rows_join(got["w_o"])
    b_qkv_col = got["b_qkv"].reshape(-1, 1)
    b_o = got["b_o"]
    w_up[1], w_down[1] = got["up1"], _rows_join(got["down1"])
    qkv_t = _mm("attn_qkv_proj", [w_qkv_t], [u1], "nt", tm=768, tn=1024, out_dtypes=(BF16,), cols=(b_qkv_col,),
                epilogue=lambda acc, b: (acc + b,))
    ao_t = _attn_fwd_t(qkv_t, sinks_rep)
    mix1 = _mm("attn_out_proj", [ao_t], [w_o], "tn", tm=1024, tn=1024, rows=(b_o,),
               epilogue=lambda acc, b: (acc + b,))
    h3, u1f = _post_pre("l1_mid", h2, mix1, nrow("mix_post_norm", 1), nrow("ffn_pre_norm", 1))
    p1, f1 = mlp_fwd(1, u1f)
    dh, loss_row = _final_loss("loss", h3, f1, nrow("ffn_post_norm", 1), target)

    g_norm = {k: [None, None] for k in norm}
    df1, g_norm["ffn_post_norm"][1], _ = _norm_bwd("l1_ffn_post_bwd", dh, post=(f1, nrow("ffn_post_norm", 1)))
    du = mlp_bwd(1, df1, u1f, p1)
    sent = reduce_grads("mlp1", {k: blocks[k] for k in ("up1", "down1")})
    dh, g_norm["ffn_pre_norm"][1], dmix1, g_norm["mix_post_norm"][1], db_o = _norm_bwd(
        "l1_mid_bwd", dh, pre=(du, h3, nrow("ffn_pre_norm", 1)), post=(mix1, nrow("mix_post_norm", 1)), after=sent)
    blocks["b_o"] = _cols_split(db_o)
    blocks["w_o"] = _rows_split(_mm("attn_dwo", [ao_t], [dmix1], "nn", tm=512, tn=1024, out_dtypes=(PAYLOAD,)))
    dao_t = _mm("attn_dout", [w_o], [dmix1], "nt", tm=1024, tn=1024, out_dtypes=(BF16,))
    dqkv_t, db_qkv, grads["attn_sinks"] = _attn_bwd_t(qkv_t, dao_t, sinks_rep)
    blocks["b_qkv"] = db_qkv.reshape(N_DEV, 1, -1)
    blocks["w_qkv"] = _rows_split(_mm("attn_dwqkv", [dqkv_t], [u1], "nn", tm=512, tn=1024, out_dtypes=(PAYLOAD,)))
    du = _mm("attn_dx", [dqkv_t], [w_qkv_t], "tn", tm=1024, tn=1024)
    sent = reduce_grads("attn", {k: blocks[k] for k in ("w_o", "w_qkv", "b_o", "b_qkv")})
    dh, g_norm["mix_pre_norm"][1], df0, g_norm["ffn_post_norm"][0], _ = _norm_bwd(
        "l1_in_bwd", dh, pre=(du, h2, nrow("mix_pre_norm", 1)), post=(f0, nrow("ffn_post_norm", 0)), after=sent)
    du = mlp_bwd(0, df0, u0f, p0)
    sent = reduce_grads("mlp0", {k: blocks[k] for k in ("up0", "down0")})
    dh, g_norm["ffn_pre_norm"][0], dmix0, g_norm["mix_post_norm"][0], _ = _norm_bwd(
        "l0_mid_bwd", dh, pre=(du, h1, nrow("ffn_pre_norm", 0)), post=(mix0, nrow("mix_post_norm", 0)), after=sent)
    blocks["w_out"] = _rows_split(_mm("ssd_dwout", [yn], [dmix0], "tn", tm=512, tn=1024, out_dtypes=(PAYLOAD,)))
    dyn = _mm("ssd_dyn", [dmix0], [w_out], "nt", tm=1024, tn=1024)
    sent = reduce_grads("ssdout", {"w_out": blocks["w_out"]})
    dy, dz, grads["ssd_norm_w"] = _gate_norm_bwd(dyn, y, zx, rep["ssd_norm_w"], sent)
    dpx, dpb, dpc, ddt_g, dbias_g, dalog_g, dd_g = _ssd_bwd(dy, pre, states, dt_c, cum_c, cum_r, sgd_c, alog_c,
                                                             dsk_c)
    conv_out = [_conv_bwd(f"ssd_conv_bwd_{tag}", dp, zx, c0, conv_w[:, c0 - di:c0 - di + n])
                for tag, dp, (c0, n) in zip("xbc", (dpx, dpb, dpc), parts[1:4])]
    dconv_w = jnp.concatenate([o[1] for o in conv_out], axis=1)
    dconv_b = jnp.concatenate([o[2] for o in conv_out], axis=1)
    ddt = jnp.transpose(ddt_g[:, :, :SSD_HPG], (1, 0, 2)).reshape(t, nh)
    ddt = jnp.pad(ddt, ((0, 0), (0, LANES - nh))).astype(BF16)
    blocks["conv_w"] = _cols_split(dconv_w)
    grads["ssd_conv_b"] = dconv_b
    for name, val in (("ssd_dt_bias", dbias_g), ("ssd_a_log", dalog_g), ("ssd_d", dd_g)):
        grads[name] = val[:, 0, :SSD_HPG].reshape(1, nh)
    d_zx = [dz] + [o[0] for o in conv_out] + [ddt]
    dw_parts = [_mm(f"ssd_dw_{tag}", [d], [u0], "tn", tm=512, tn=1024, out_dtypes=(PAYLOAD,))
                for tag, d in zip("zxbct", d_zx)]
    dw_parts[-1] = dw_parts[-1][:nh]
    blocks["w_in"] = _rows_split(jnp.concatenate(dw_parts, axis=0))
    sent = reduce_grads("ssd", {k: blocks[k] for k in ("w_in", "conv_w")})
    w_parts = [w_in_t[r0:r0 + n] for r0, n in parts[:-1]] + [w_dt_t]
    du = _mm("ssd_dx", d_zx, w_parts, "nn", tm=256, tn=1024, after=sent)
    grad_x, g_norm["mix_pre_norm"][0] = _norm_bwd("l0_in_bwd", dh, pre=(du, x, nrow("mix_pre_norm", 0)), after=sent)
    for k in norm:
        grads[k] = jnp.concatenate(g_norm[k], axis=0)
    return loss_row, grad_x, grads


def kernel(x, ssd_w_in, ssd_conv_w, ssd_conv_b, ssd_dt_bias, ssd_a_log, ssd_d, ssd_norm_w, ssd_w_out, attn_w_qkv, attn_b_qkv, attn_sinks, attn_w_o, attn_b_o, mlp_w_up, mlp_w_down, mix_pre_norm, mix_post_norm, ffn_pre_norm, ffn_post_norm, loss_target, m_ssd_w_in, m_ssd_conv_w, m_ssd_conv_b, m_ssd_dt_bias, m_ssd_a_log, m_ssd_d, m_ssd_norm_w, m_ssd_w_out, m_attn_w_qkv, m_attn_b_qkv, m_attn_sinks, m_attn_w_o, m_attn_b_o, m_mlp_w_up, m_mlp_w_down, m_mix_pre_norm, m_mix_post_norm, m_ffn_pre_norm, m_ffn_post_norm, v_ssd_w_in, v_ssd_conv_w, v_ssd_conv_b, v_ssd_dt_bias, v_ssd_a_log, v_ssd_d, v_ssd_norm_w, v_ssd_w_out, v_attn_w_qkv, v_attn_b_qkv, v_attn_sinks, v_attn_w_o, v_attn_b_o, v_mlp_w_up, v_mlp_w_down, v_mix_pre_norm, v_mix_post_norm, v_ffn_pre_norm, v_ffn_post_norm):
    given = dict(locals())
    w = {k: given[k] for k in WEIGHTS}
    mom_m = {k: given["m_" + k] for k in WEIGHTS}
    mom_v = {k: given["v_" + k] for k in WEIGHTS}
    w_it, m_it, v_it = _items(given), _items(given, "m_"), _items(given, "v_")

    order = [k for stage in GATHER_STAGES for k in stage]
    shards = [w_it[k].astype(PAYLOAD) if k in MATRIX_ITEMS else w_it[k] for k in order]
    wide = [k in SIDE_BY_SIDE for k in order]
    g_send, g_recv, shards, lands, token = _gather_start("gather_start", shards, wide)

    def weights_of_stage(s, after):
        first = sum(len(stage) for stage in GATHER_STAGES[:s])
        sl = slice(first, first + len(GATHER_STAGES[s]))
        _, got = _gather_wait(f"gather_wait{s}", g_send, g_recv, first, len(order), shards[sl], lands[sl], wide[sl],
                              after)
        return dict(zip(GATHER_STAGES[s], got))

    in_flight = []

    def reduce_grads(tag, blocks):
        keys = list(blocks)
        started = _scatter_start(f"rs_start_{tag}", [blocks[k] for k in keys])
        in_flight.append((tag, keys, started))
        return started[-1]

    rep = {k: w[k] for k in REPLICATED}
    loss_row, grad_x, grads = _forward_backward(x[0], loss_target[0], rep, token, weights_of_stage, reduce_grads)

    def pack_rep(tree, last):
        flat = jnp.concatenate([tree[k].reshape(-1) for k in REPLICATED] + [last])
        return _pack_rows(flat, _round_up(-(-flat.shape[0] // LANES), 8), LANES)

    landed = {}

    def wait_group(group, after):
        tag, keys, (s_send, s_recv, srcs, s_lands, _) = group
        _, got = _scatter_wait(f"rs_wait_{tag}", s_send, s_recv, srcs, s_lands, after)
        landed.update(zip(keys, got))

    def adamw_item(k):
        return _sum_adamw(f"adamw_{k}", landed[k], w_it[k], m_it[k], v_it[k])

    def adamw_stack(name, keys):
        return _sum_adamw_layers(f"adamw_{name}", [landed[k] for k in keys], given[name], given["m_" + name],
                                 given["v_" + name])

    for group in in_flight[:-1]:
        wait_group(group, grad_x)
    done = {"mlp_w_up": adamw_stack("mlp_w_up", ("up0", "up1")),
            "mlp_w_down": adamw_stack("mlp_w_down", ("down0", "down1")),
            "attn_w_qkv": [o.T[None] for o in adamw_item("w_qkv")],
            "attn_w_o": [o[None] for o in adamw_item("w_o")],
            "attn_b_qkv": adamw_item("b_qkv"), "attn_b_o": adamw_item("b_o"),
            "ssd_w_out": [o[None] for o in adamw_item("w_out")]}
    partials, = _all_gather("gather_small_grads", [pack_rep(grads, loss_row[0, :1])],
                            [outs4[0] for outs4 in done.values()])
    wait_group(in_flight[-1], partials)
    done["ssd_w_in"] = [o.T[None] for o in adamw_item("w_in")]
    done["ssd_conv_w"] = [o[None] for o in adamw_item("conv_w")]
    zero = jnp.zeros((1,), F32)
    rep_out = _sum_adamw("adamw_replicated", partials, pack_rep(w, zero), pack_rep(mom_m, zero), pack_rep(mom_v, zero))

    kinds = []
    for kind, r_arr in enumerate(rep_out):
        tree = {name: outs4[kind] for name, outs4 in done.items()}
        flat, off = r_arr.reshape(-1), 0
        for k in REPLICATED:
            tree[k] = flat[off:off + w[k].size].reshape(w[k].shape)
            off += w[k].size
        kinds.append(tree)
    loss = rep_out[0].reshape(-1)[off]
    outs = [loss, grad_x[None]]
    for tree in kinds:
        outs += [tree[k] for k in WEIGHTS]
    return tuple(outs)
```

```python
import jax
import jax.numpy as jnp
from jax import lax
from jax.experimental import pallas as pl
from jax.experimental.pallas import tpu as pltpu

F32 = jnp.float32
BF16 = jnp.bfloat16
PAYLOAD = jnp.bfloat16
HIGHEST = lax.Precision.HIGHEST
MESH = pl.DeviceIdType.MESH

NORM_EPS = 1e-6
SSD_HEAD_DIM = 64
SSD_HPG = 4
SSD_D_STATE = 128
SSD_CONV_WIDTH = 4
SSD_CHUNK = 128
ATTN_HEAD_DIM = 64
ATTN_N_KV = 4
ATTN_REP = 4
ATTN_WINDOW = 128
ADAM_LR = 0.001
ADAM_B1 = 0.9
ADAM_B2 = 0.999
ADAM_EPS = 1e-08
ADAM_WD = 0.01
ADAM_STEP = 10

N_DEV = 8
LANES = 128
V7X_VMEM_LIMIT = 56 * 1024 * 1024

GW = SSD_HPG * SSD_HEAD_DIM
GC = GW + 2 * SSD_D_STATE
assert SSD_CHUNK == LANES


def _params(*sem):
    return pltpu.CompilerParams(dimension_semantics=sem, vmem_limit_bytes=V7X_VMEM_LIMIT)


def _tile(n, pref, mult=LANES):
    best = None
    t = mult
    while t <= min(n, pref):
        if n % t == 0:
            best = t
        t += mult
    return best if best is not None else n


def _round_up(n, m):
    return (n + m - 1) // m * m


def _acc(ref, val, first):
    @pl.when(first)
    def _():
        ref[...] = val

    @pl.when(jnp.logical_not(first))
    def _():
        ref[...] += val


def _dot(a, b):
    return lax.dot_general(a, b, (((1,), (0,)), ((), ())), preferred_element_type=F32)


def _dot_nt(a, b):
    return lax.dot_general(a, b, (((1,), (1,)), ((), ())), preferred_element_type=F32)


def _dot_tn(a, b):
    return lax.dot_general(a, b, (((0,), (0,)), ((), ())), preferred_element_type=F32)


def _dot_f32(a, b):
    return lax.dot_general(a, b, (((1,), (0,)), ((), ())), preferred_element_type=F32, precision=HIGHEST)


_DOTS = {"nn": _dot, "nt": _dot_nt, "tn": _dot_tn}


def _sigmoid(x):
    return 1.0 / (1.0 + jnp.exp(-x))


def _softplus(x):
    return jnp.maximum(x, 0.0) + jnp.log1p(jnp.exp(-jnp.abs(x)))


def _silu_grad(x, s):
    return s * (1.0 + x * (1.0 - s))


def _mm(name, a_list, b_list, mode, *, tm, tn, out_dtypes=(F32,), epilogue=None, tiles=(), rows=(), cols=(),
        col_blocks=False, n_use=None, after=None):
    npair = len(a_list)
    if mode == "tn":
        m = a_list[0].shape[1]
    else:
        m = a_list[0].shape[0]
    n = n_use if n_use is not None else (b_list[0].shape[0] if mode == "nt" else b_list[0].shape[1])
    tm = _tile(m, tm, LANES if mode == "tn" else 8)
    tn = _tile(n, tn)
    assert m % tm == 0 and n % tn == 0, (name, m, n, tm, tn)
    dot = _DOTS[mode]

    def body(*refs):
        a_refs = refs[:npair]
        b_refs = refs[npair:2 * npair]
        n_extra = len(tiles) + len(rows) + len(cols)
        e_refs = refs[2 * npair:2 * npair + n_extra]
        o_refs = refs[2 * npair + n_extra + len(order):]
        acc = None
        for ar, br in zip(a_refs, b_refs):
            d = dot(ar[...], br[...])
            acc = d if acc is None else acc + d
        outs = epilogue(acc, *[e[...] for e in e_refs]) if epilogue is not None else (acc,)
        for o, v in zip(o_refs, outs):
            o[...] = v.astype(o.dtype)

    in_specs = []
    for a in a_list:
        if mode == "tn":
            in_specs.append(pl.BlockSpec((a.shape[0], tm), lambda i, j: (0, i)))
        else:
            in_specs.append(pl.BlockSpec((tm, a.shape[1]), lambda i, j: (i, 0)))
    for b in b_list:
        if mode == "nt":
            in_specs.append(pl.BlockSpec((tn, b.shape[1]), lambda i, j: (j, 0)))
        else:
            in_specs.append(pl.BlockSpec((b.shape[0], tn), lambda i, j: (0, j)))
    in_specs += [pl.BlockSpec((tm, tn), lambda i, j: (i, j)) for _ in tiles]
    in_specs += [pl.BlockSpec((1, tn), lambda i, j: (0, j)) for _ in rows]
    in_specs += [pl.BlockSpec((tm, 1), lambda i, j: (i, 0)) for _ in cols]
    order = [] if after is None else [after]
    in_specs += [pl.BlockSpec((8, LANES), lambda i, j: (0, 0)) for _ in order]
    outs = pl.pallas_call(
        body,
        name=name,
        grid=(m // tm, n // tn),
        in_specs=in_specs,
        out_specs=[pl.BlockSpec((None, tm, tn), lambda i, j: (j, i, 0)) if col_blocks else
                   pl.BlockSpec((tm, tn), lambda i, j: (i, j)) for _ in out_dtypes],
        out_shape=[jax.ShapeDtypeStruct((n // tn, m, tn) if col_blocks else (m, n), dt) for dt in out_dtypes],
        compiler_params=_params("parallel", "parallel"),
    )(*a_list, *b_list, *tiles, *rows, *cols, *order)
    return outs[0] if len(out_dtypes) == 1 else outs


def _rms(x, w):
    r = lax.rsqrt(jnp.mean(x * x, axis=-1, keepdims=True) + NORM_EPS)
    return x * r * w


def _rms_bwd(x, w, dy):
    r = lax.rsqrt(jnp.mean(x * x, axis=-1, keepdims=True) + NORM_EPS)
    xh = x * r
    g = dy * w
    dx = r * (g - xh * jnp.mean(g * xh, axis=-1, keepdims=True))
    return dx, dy * xh


def _row_specs(tr, d):
    return pl.BlockSpec((tr, d), lambda i: (i, 0)), pl.BlockSpec((1, d), lambda i: (0, 0))


def _prenorm(name, h, w, after):
    t, d = h.shape
    tr = _tile(t, 512, 8)
    row, vec = _row_specs(tr, d)

    def body(h_ref, w_ref, after_ref, u_ref):
        u_ref[...] = _rms(h_ref[...], w_ref[...]).astype(BF16)

    return pl.pallas_call(body, name=name, grid=(t // tr,),
                          in_specs=[row, vec, pl.BlockSpec((8, LANES), lambda i: (0, 0))], out_specs=row,
                          out_shape=jax.ShapeDtypeStruct((t, d), BF16), compiler_params=_params("parallel"))(
                              h, w, after)


def _post_pre(name, h, m, w_post, w_pre):
    t, d = h.shape
    tr = _tile(t, 512, 8)
    row, vec = _row_specs(tr, d)

    def body(h_ref, m_ref, wq_ref, wp_ref, hn_ref, u_ref):
        hn = h_ref[...] + _rms(m_ref[...], wq_ref[...])
        hn_ref[...] = hn
        u_ref[...] = _rms(hn, wp_ref[...]).astype(BF16)

    return pl.pallas_call(body, name=name, grid=(t // tr,), in_specs=[row, row, vec, vec], out_specs=[row, row],
                          out_shape=[jax.ShapeDtypeStruct((t, d), F32), jax.ShapeDtypeStruct((t, d), BF16)],
                          compiler_params=_params("parallel"))(h, m, w_post, w_pre)


def _final_loss(name, h, m, w_post, target):
    t, d = h.shape
    tr = _tile(t, 512, 8)
    row, vec = _row_specs(tr, d)

    def body(h_ref, m_ref, wq_ref, t_ref, dh_ref, loss_ref):
        err = h_ref[...] + _rms(m_ref[...], wq_ref[...]) - t_ref[...]
        dh_ref[...] = err * (1.0 / d)
        part = 0.5 * jnp.sum(jnp.mean(err * err, axis=-1, keepdims=True), axis=0, keepdims=True)
        _acc(loss_ref, jnp.broadcast_to(part, (1, LANES)), pl.program_id(0) == 0)

    return pl.pallas_call(body, name=name, grid=(t // tr,), in_specs=[row, row, vec, row],
                          out_specs=[row, pl.BlockSpec((1, LANES), lambda i: (0, 0))],
                          out_shape=[jax.ShapeDtypeStruct((t, d), F32), jax.ShapeDtypeStruct((1, LANES), F32)],
                          compiler_params=_params("arbitrary"))(h, m, w_post, target)


def _norm_bwd(name, dh, pre=None, post=None, after=None):
    t, d = dh.shape
    tr = _tile(t, 512, 8)
    row, vec = _row_specs(tr, d)
    has_pre, has_post = pre is not None, post is not None

    def body(*refs):
        it = iter(refs)
        dh_ref = next(it)
        if has_pre:
            du_ref, x_ref, wp_ref = next(it), next(it), next(it)
        if has_post:
            m_ref, wq_ref = next(it), next(it)
        if after is not None:
            next(it)
        first = pl.program_id(0) == 0
        dh_v = dh_ref[...]
        if has_pre:
            dhn_ref, dwp_ref = next(it), next(it)
            dx, dwr = _rms_bwd(x_ref[...], wp_ref[...], du_ref[...])
            dh_v = dh_v + dx
            dhn_ref[...] = dh_v
            _acc(dwp_ref, jnp.sum(dwr, axis=0, keepdims=True), first)
        if has_post:
            dm_ref, dwq_ref, dms_ref = next(it), next(it), next(it)
            dm, dwr = _rms_bwd(m_ref[...], wq_ref[...], dh_v)
            dm_ref[...] = dm.astype(BF16)
            _acc(dwq_ref, jnp.sum(dwr, axis=0, keepdims=True), first)
            _acc(dms_ref, jnp.sum(dm, axis=0, keepdims=True), first)

    ins, in_specs, out_specs, out_shape = [dh], [row], [], []
    if has_pre:
        ins += list(pre)
        in_specs += [row, row, vec]
        out_specs += [row, vec]
        out_shape += [jax.ShapeDtypeStruct((t, d), F32), jax.ShapeDtypeStruct((1, d), F32)]
    if has_post:
        ins += list(post)
        in_specs += [row, vec]
        out_specs += [row, vec, vec]
        out_shape += [jax.ShapeDtypeStruct((t, d), BF16), jax.ShapeDtypeStruct((1, d), F32),
                      jax.ShapeDtypeStruct((1, d), F32)]
    if after is not None:
        ins.append(after)
        in_specs.append(pl.BlockSpec((8, LANES), lambda i: (0, 0)))
    return pl.pallas_call(body, name=name, grid=(t // tr,), in_specs=in_specs, out_specs=out_specs,
                          out_shape=out_shape, compiler_params=_params("arbitrary"))(*ins)


HALO = 8


def _shift_later(cur, prev, s):
    rolled = pltpu.roll(cur, s, 0)
    row = lax.broadcasted_iota(jnp.int32, prev.shape, 0)
    first = jnp.where(row < s, pltpu.roll(prev, s, 0), rolled[0:HALO])
    return jnp.concatenate([first, rolled[HALO:]], axis=0)


def _shift_earlier(cur, nxt, s):
    tt = cur.shape[0]
    rolled = pltpu.roll(cur, tt - s, 0)
    row = lax.broadcasted_iota(jnp.int32, nxt.shape, 0)
    last = jnp.where(row >= HALO - s, pltpu.roll(nxt, HALO - s, 0), rolled[tt - HALO:])
    return jnp.concatenate([rolled[:tt - HALO], last], axis=0)


def _conv_fwd(zx, col0, n_ch, conv_w, conv_b):
    t = zx.shape[0]
    tc = _tile(n_ch, 512)
    tt = _tile(t, 1024, 8)
    cb0 = col0 // tc
    assert col0 % tc == 0
    kw = SSD_CONV_WIDTH

    def body(x_ref, p_ref, w_ref, b_ref, o_ref):
        cur = x_ref[...]
        prev = jnp.where(pl.program_id(1) > 0, p_ref[...], 0.0)
        w = w_ref[...]
        acc = b_ref[...] + w[kw - 1:kw, :] * cur
        for k in range(kw - 1):
            acc = acc + w[k:k + 1, :] * _shift_later(cur, prev, kw - 1 - k)
        o_ref[...] = acc

    return pl.pallas_call(
        body, name="ssd_conv_fwd", grid=(n_ch // tc, t // tt),
        in_specs=[pl.BlockSpec((tt, tc), lambda j, i: (i, cb0 + j)),
                  pl.BlockSpec((HALO, tc), lambda j, i: (jnp.maximum(i * (tt // HALO) - 1, 0), cb0 + j)),
                  pl.BlockSpec((kw, tc), lambda j, i: (0, j)),
                  pl.BlockSpec((1, tc), lambda j, i: (0, j))],
        out_specs=pl.BlockSpec((tt, tc), lambda j, i: (i, j)),
        out_shape=jax.ShapeDtypeStruct((t, n_ch), F32),
        compiler_params=_params("parallel", "parallel"))(zx, zx, conv_w, conv_b)


def _conv_bwd(name, dpre, zx, col0, conv_w):
    t, n_ch = dpre.shape
    tc = _tile(n_ch, 512)
    tt = _tile(t, 1024, 8)
    cb0 = col0 // tc
    kw = SSD_CONV_WIDTH
    nt = t // tt

    def body(d_ref, dn_ref, x_ref, p_ref, w_ref, dx_ref, dw_ref, db_ref):
        i = pl.program_id(1)
        d = d_ref[...]
        d_next = jnp.where(i < nt - 1, dn_ref[...], 0.0)
        x = x_ref[...]
        x_prev = jnp.where(i > 0, p_ref[...], 0.0)
        w = w_ref[...]
        dx = w[kw - 1:kw, :] * d
        for k in range(kw - 1):
            dx = dx + w[k:k + 1, :] * _shift_earlier(d, d_next, kw - 1 - k)
        dx_ref[...] = dx.astype(BF16)
        first = i == 0
        for k in range(kw):
            xs = x if k == kw - 1 else _shift_later(x, x_prev, kw - 1 - k)
            val = jnp.sum(d * xs, axis=0, keepdims=True)

            @pl.when(first)
            def _():
                dw_ref[k:k + 1, :] = val

            @pl.when(jnp.logical_not(first))
            def _():
                dw_ref[k:k + 1, :] += val
        _acc(db_ref, jnp.sum(d, axis=0, keepdims=True), first)

    return pl.pallas_call(
        body, name=name, grid=(n_ch // tc, nt),
        in_specs=[pl.BlockSpec((tt, tc), lambda j, i: (i, j)),
                  pl.BlockSpec((HALO, tc), lambda j, i: (jnp.minimum((i + 1) * (tt // HALO), t // HALO - 1), j)),
                  pl.BlockSpec((tt, tc), lambda j, i: (i, cb0 + j)),
                  pl.BlockSpec((HALO, tc), lambda j, i: (jnp.maximum(i * (tt // HALO) - 1, 0), cb0 + j)),
                  pl.BlockSpec((kw, tc), lambda j, i: (0, j))],
        out_specs=[pl.BlockSpec((tt, tc), lambda j, i: (i, j)),
                   pl.BlockSpec((kw, tc), lambda j, i: (0, j)),
                   pl.BlockSpec((1, tc), lambda j, i: (0, j))],
        out_shape=[jax.ShapeDtypeStruct((t, n_ch), BF16), jax.ShapeDtypeStruct((kw, n_ch), F32),
                   jax.ShapeDtypeStruct((1, n_ch), F32)],
        compiler_params=_params("parallel", "arbitrary"))(dpre, dpre, zx, zx, conv_w)


def _head_of_lane(shape, width):
    return lax.broadcasted_iota(jnp.int32, shape, len(shape) - 1) // width


def _select_dot(v, pick, pick_first=False):
    hi = v.astype(BF16)
    lo = (v - hi.astype(F32)).astype(BF16)
    return _dot(pick, hi) + _dot(pick, lo) if pick_first else _dot(hi, pick) + _dot(lo, pick)


def _expand(v, n_rows, on_mxu=False):
    if not on_mxu:
        head = _head_of_lane((n_rows, GW), SSD_HEAD_DIM)
        out = jnp.zeros((n_rows, GW), F32)
        for j in range(SSD_HPG):
            out = jnp.where(head == j, v[:, j:j + 1], out)
        return out
    src = lax.broadcasted_iota(jnp.int32, (LANES, GW), 0)
    return _select_dot(v, (src == _head_of_lane((LANES, GW), SSD_HEAD_DIM)).astype(BF16))


def _contract(v, n_rows, on_mxu=False):
    if not on_mxu:
        head = _head_of_lane((n_rows, GW), SSD_HEAD_DIM)
        lane = lax.broadcasted_iota(jnp.int32, (n_rows, LANES), 1)
        out = jnp.zeros((n_rows, LANES), F32)
        for j in range(SSD_HPG):
            s = jnp.sum(jnp.where(head == j, v, 0.0), axis=1, keepdims=True)
            out = jnp.where(lane == j, s, out)
        return out
    dst = lax.broadcasted_iota(jnp.int32, (GW, LANES), 1)
    return _select_dot(v, (lax.broadcasted_iota(jnp.int32, (GW, LANES), 0) // SSD_HEAD_DIM == dst).astype(BF16))


def _ssd_dt_prep(zdt, bias, alog, ng):
    t = zdt.shape[0]
    q = SSD_CHUNK

    def body(z_ref, b_ref, a_ref, dt_ref, cum_ref, cumr_ref, sg_ref):
        raw = z_ref[...] + b_ref[...]
        dt = _softplus(raw)
        sgd = _sigmoid(raw)
        row = lax.broadcasted_iota(jnp.int32, (q, q), 0)
        col = lax.broadcasted_iota(jnp.int32, (q, q), 1)
        cum = _dot_f32((col <= row).astype(F32), dt * (-jnp.exp(a_ref[...])))
        cum_t = cum.T
        lane = lax.broadcasted_iota(jnp.int32, (q, LANES), 1)
        for g in range(ng):
            shift = (LANES - g * SSD_HPG) % LANES

            def group(v):
                return jnp.where(lane < SSD_HPG, pltpu.roll(v, shift, 1) if shift else v, 0.0)

            dt_ref[g] = group(dt)
            cum_ref[g] = group(cum)
            sg_ref[g] = group(sgd)
            cumr_ref[g] = (pltpu.roll(cum_t, shift, 0) if shift else cum_t)[0:8, :]

    cols = pl.BlockSpec((ng, q, LANES), lambda c: (0, c, 0))
    vec = pl.BlockSpec((1, LANES), lambda c: (0, 0))
    col_shape = jax.ShapeDtypeStruct((ng, t, LANES), F32)
    return pl.pallas_call(body, name="ssd_dt_prep", grid=(t // q,),
                          in_specs=[pl.BlockSpec((q, LANES), lambda c: (c, 0)), vec, vec],
                          out_specs=[cols, cols, pl.BlockSpec((ng, 8, q), lambda c: (0, 0, c)), cols],
                          out_shape=[col_shape, col_shape, jax.ShapeDtypeStruct((ng, 8, t), F32), col_shape],
                          compiler_params=_params("parallel"))(zdt, bias, alog)


def _ssd_common(pre, dt, cum, cum_r, alog_c, on_mxu):
    q = SSD_CHUNK
    sg = _sigmoid(pre)
    act = pre * sg
    xa = act[:, :GW]
    bm = act[:, GW:GW + SSD_D_STATE].astype(BF16)
    cm = act[:, GW + SSD_D_STATE:].astype(BF16)
    row = lax.broadcasted_iota(jnp.int32, (q, q), 0)
    col = lax.broadcasted_iota(jnp.int32, (q, q), 1)
    tril = col <= row
    a_c = -jnp.exp(alog_c)
    g = _dot_nt(cm, bm)
    cl = cum[q - 1:q, :]
    e_c = jnp.exp(cl - cum)
    lam_c = jnp.exp(cum)
    dt_x, e_x, lam_x = _expand(dt, q, on_mxu), _expand(e_c, q, on_mxu), _expand(lam_c, q, on_mxu)
    xdt = xa * dt_x
    return dict(sg=sg, xa=xa, bm=bm, cm=cm, tril=tril, row=row, col=col, dt=dt, a_c=a_c, cum=cum, cum_r=cum_r,
                g=g, dt_x=dt_x, xdt=xdt, cl=cl, e_c=e_c, lam_c=lam_c, e_x=e_x, lam_x=lam_x)


SSD_GPS_FWD = 8
SSD_GPS_BWD = 2


def _ssd_specs(nc, rev, ng, gps):
    q = SSD_CHUNK
    xw, nw = gps * GW, gps * SSD_D_STATE
    b_off = ng * GW // nw
    c_off = (ng * GW + ng * SSD_D_STATE) // nw
    assert ng % gps == 0 and (ng * GW) % nw == 0 and (ng * SSD_D_STATE) % nw == 0

    def ch(c):
        return nc - 1 - c if rev else c

    chunk_grp = [pl.BlockSpec((q, xw), lambda g, c: (ch(c), g)),
                 pl.BlockSpec((q, nw), lambda g, c: (ch(c), b_off + g)),
                 pl.BlockSpec((q, nw), lambda g, c: (ch(c), c_off + g))]
    col_form = pl.BlockSpec((gps, q, LANES), lambda g, c: (g, ch(c), 0))
    row_form = pl.BlockSpec((gps, 8, q), lambda g, c: (g, 0, ch(c)))
    col_par = pl.BlockSpec((gps, 1, LANES), lambda g, c: (g, 0, 0))
    y_spec = pl.BlockSpec((q, xw), lambda g, c: (ch(c), g))
    st_spec = pl.BlockSpec((gps, None, GW, SSD_D_STATE), lambda g, c: (g, ch(c), 0, 0))
    bc_spec = pl.BlockSpec((q, nw), lambda g, c: (ch(c), g))
    return chunk_grp, col_form, row_form, col_par, y_spec, st_spec, bc_spec


def _ssd_group_views(gi, wide, narrow, stacked):
    xs, ns = pl.ds(gi * GW, GW), pl.ds(gi * SSD_D_STATE, SSD_D_STATE)
    return [r.at[:, xs] for r in wide], [r.at[:, ns] for r in narrow], [r.at[gi] for r in stacked]


def _ssd_fwd(pre, dt_c, cum_c, cum_r, alog_c, dsk_c):
    t = pre.shape[0]
    ng = pre.shape[1] // GC
    q = SSD_CHUNK
    nc = t // q
    gps = SSD_GPS_FWD if ng % SSD_GPS_FWD == 0 else SSD_GPS_BWD
    chunk_grp, col_form, row_form, col_par, y_spec, st_spec, _ = _ssd_specs(nc, False, ng, gps)

    def body(px_ref, pb_ref, pc_ref, dt_ref, cum_ref, cumr_ref, ac_ref, dk_ref, y_ref, sp_ref, st_ref):
        @pl.when(pl.program_id(1) == 0)
        def _():
            st_ref[...] = jnp.zeros_like(st_ref)

        for gi in range(gps):
            (px, y), (pb, pc), rest = _ssd_group_views(
                gi, (px_ref, y_ref), (pb_ref, pc_ref), (dt_ref, cum_ref, cumr_ref, ac_ref, dk_ref, sp_ref, st_ref))
            one_group(px, pb, pc, *rest[:5], y, *rest[5:])

    def one_group(px_ref, pb_ref, pc_ref, dt_ref, cum_ref, cumr_ref, ac_ref, dk_ref, y_ref, sp_ref, st_ref):
        pre_v = jnp.concatenate([px_ref[...], pb_ref[...], pc_ref[...]], axis=1)
        v = _ssd_common(pre_v, dt_ref[...], cum_ref[...], cumr_ref[...], ac_ref[...], False)
        s0 = st_ref[...]
        sp_ref[...] = s0
        r = _dot_nt(v["cm"], s0.astype(BF16))
        y = v["lam_x"] * r + _expand(dk_ref[...], 1) * v["xa"]
        head = _head_of_lane((q, GW), SSD_HEAD_DIM)
        for j in range(SSD_HPG):
            diff = v["cum"][:, j:j + 1] - v["cum_r"][j:j + 1, :]
            w = (v["g"] * jnp.exp(jnp.where(v["tril"], diff, -jnp.inf))).astype(BF16)
            y = y + _dot(w, jnp.where(head == j, v["xdt"], 0.0).astype(BF16))
        y_ref[...] = y
        ds = _dot_tn((v["xdt"] * v["e_x"]).astype(BF16), v["bm"])
        for j in range(SSD_HPG):
            rows = slice(j * SSD_HEAD_DIM, (j + 1) * SSD_HEAD_DIM)
            st_ref[rows, :] = s0[rows, :] * jnp.exp(v["cum_r"][j:j + 1, q - 1:q]) + ds[rows, :]

    return pl.pallas_call(
        body, name="ssd_scan_fwd", grid=(ng // gps, nc),
        in_specs=chunk_grp + [col_form, col_form, row_form, col_par, col_par],
        out_specs=[y_spec, st_spec],
        out_shape=[jax.ShapeDtypeStruct((t, ng * GW), F32), jax.ShapeDtypeStruct((ng, nc, GW, SSD_D_STATE), F32)],
        scratch_shapes=[pltpu.VMEM((gps, GW, SSD_D_STATE), F32)],
        compiler_params=_params("parallel", "arbitrary"))(pre, pre, pre, dt_c, cum_c, cum_r, alog_c, dsk_c)


def _ssd_bwd(dy, pre, states, dt_c, cum_c, cum_r, sgd_c, alog_c, dsk_c):
    t = pre.shape[0]
    ng = pre.shape[1] // GC
    q = SSD_CHUNK
    nc = t // q
    gps = SSD_GPS_BWD
    chunk_grp, col_form, row_form, col_par, y_spec, st_spec, bc_spec = _ssd_specs(nc, True, ng, gps)

    def body(dy_ref, px_ref, pb_ref, pc_ref, sp_ref, dt_ref, cum_ref, cumr_ref, sgd_ref, ac_ref, dk_ref,
             dpx_ref, dpb_ref, dpc_ref, ddt_ref, dbias_ref, dalog_ref, dd_ref, ds_ref):
        @pl.when(pl.program_id(1) == 0)
        def _():
            ds_ref[...] = jnp.zeros_like(ds_ref)

        for gi in range(gps):
            (dy, px, dpx), (pb, pc, dpb, dpc), rest = _ssd_group_views(
                gi, (dy_ref, px_ref, dpx_ref), (pb_ref, pc_ref, dpb_ref, dpc_ref),
                (sp_ref, dt_ref, cum_ref, cumr_ref, sgd_ref, ac_ref, dk_ref, ddt_ref, dbias_ref, dalog_ref, dd_ref,
                 ds_ref))
            one_group(dy, px, pb, pc, *rest[:7], dpx, dpb, dpc, *rest[7:])

    def one_group(dy_ref, px_ref, pb_ref, pc_ref, sp_ref, dt_ref, cum_ref, cumr_ref, sgd_ref, ac_ref, dk_ref,
                  dpx_ref, dpb_ref, dpc_ref, ddt_ref, dbias_ref, dalog_ref, dd_ref, ds_ref):
        first = pl.program_id(1) == 0
        pre_v = jnp.concatenate([px_ref[...], pb_ref[...], pc_ref[...]], axis=1)
        v = _ssd_common(pre_v, dt_ref[...], cum_ref[...], cumr_ref[...], ac_ref[...], True)
        xa, bm, cm, xdt, cum, cum_r = v["xa"], v["bm"], v["cm"], v["xdt"], v["cum"], v["cum_r"]
        xdt_b = xdt.astype(BF16)
        dy_v = dy_ref[...]
        s0 = sp_ref[...]
        ds1 = ds_ref[...]
        s0b, ds1b = s0.astype(BF16), ds1.astype(BF16)
        head = _head_of_lane((q, GW), SSD_HEAD_DIM)
        lane = lax.broadcasted_iota(jnp.int32, (q, LANES), 1)
        lane1 = lax.broadcasted_iota(jnp.int32, (1, LANES), 1)
        lam_x, e_x = v["lam_x"], v["e_x"]

        dxa = _expand(dk_ref[...], 1) * dy_v
        dd = _contract(jnp.sum(dy_v * xa, axis=0, keepdims=True), 1)
        r = _dot_nt(cm, s0b)
        dcum = _contract(dy_v * r * lam_x, q, True)
        drb = (lam_x * dy_v).astype(BF16)
        dc = _dot(drb, s0b)
        ds0 = _dot_tn(drb, cm)
        extra = jnp.zeros((1, LANES), F32)
        for j in range(SSD_HPG):
            rows = slice(j * SSD_HEAD_DIM, (j + 1) * SSD_HEAD_DIM)
            lam_last = jnp.exp(cum_r[j:j + 1, q - 1:q])
            ds_ref[rows, :] = ds0[rows, :] + lam_last * ds1[rows, :]
            tot = jnp.sum(jnp.sum(ds1[rows, :] * s0[rows, :], axis=1, keepdims=True), axis=0, keepdims=True)
            extra = jnp.where(lane1 == j, lam_last * tot, extra)
        dv = _dot_nt(bm, ds1b)
        db = _dot((xdt * e_x).astype(BF16), ds1b)
        dxdt = e_x * dv
        dee = _contract(dv * xdt, q, True) * v["e_c"]
        dcum = dcum - dee
        extra = extra + jnp.sum(dee, axis=0, keepdims=True)
        dg = jnp.zeros((q, q), F32)
        col_sums = jnp.zeros((q, q), F32)
        for j in range(SSD_HPG):
            diff = cum[:, j:j + 1] - cum_r[j:j + 1, :]
            el = jnp.exp(jnp.where(v["tril"], diff, -jnp.inf))
            gl = v["g"] * el
            dym = jnp.where(head == j, dy_v, 0.0).astype(BF16)
            dwm = _dot_nt(dym, xdt_b)
            dxdt = dxdt + _dot_tn(gl.astype(BF16), dym)
            z = dwm * gl
            dcum = jnp.where(lane == j, dcum + jnp.sum(z, axis=1, keepdims=True), dcum)
            col_sums = jnp.where(v["row"] == j, jnp.sum(z, axis=0, keepdims=True), col_sums)
            dg = dg + dwm * el
        dcum = dcum - col_sums.T
        dgb = dg.astype(BF16)
        dc = dc + _dot(dgb, bm)
        db = db + _dot_tn(dgb, cm)
        da = _select_dot(dcum, (v["row"] <= v["col"]).astype(BF16), True) + extra
        ddt = _contract(dxdt * xa, q, True) + v["a_c"] * da
        dalog = jnp.sum(v["dt"] * da, axis=0, keepdims=True) * v["a_c"]
        dxa = dxa + v["dt_x"] * dxdt
        ddt_raw = jnp.where(lane < SSD_HPG, ddt * sgd_ref[...], 0.0)
        sgrad = _silu_grad(pre_v, v["sg"])
        dpx_ref[...] = dxa * sgrad[:, :GW]
        dpb_ref[...] = db * sgrad[:, GW:GW + SSD_D_STATE]
        dpc_ref[...] = dc * sgrad[:, GW + SSD_D_STATE:]
        ddt_ref[...] = ddt_raw
        _acc(dbias_ref, jnp.sum(ddt_raw, axis=0, keepdims=True), first)
        _acc(dalog_ref, jnp.where(lane1 < SSD_HPG, dalog, 0.0), first)
        _acc(dd_ref, dd, first)

    return pl.pallas_call(
        body, name="ssd_scan_bwd", grid=(ng // gps, nc),
        in_specs=[y_spec] + chunk_grp + [st_spec, col_form, col_form, row_form, col_form, col_par, col_par],
        out_specs=[y_spec, bc_spec, bc_spec, col_form, col_par, col_par, col_par],
        out_shape=[jax.ShapeDtypeStruct((t, ng * GW), F32), jax.ShapeDtypeStruct((t, ng * SSD_D_STATE), F32),
                   jax.ShapeDtypeStruct((t, ng * SSD_D_STATE), F32), jax.ShapeDtypeStruct((ng, t, LANES), F32),
                   jax.ShapeDtypeStruct((ng, 1, LANES), F32), jax.ShapeDtypeStruct((ng, 1, LANES), F32),
                   jax.ShapeDtypeStruct((ng, 1, LANES), F32)],
        scratch_shapes=[pltpu.VMEM((gps, GW, SSD_D_STATE), F32)],
        compiler_params=_params("parallel", "arbitrary"))(dy, pre, pre, pre, states, dt_c, cum_c, cum_r, sgd_c, alog_c,
                                                           dsk_c)


def _gate_norm_fwd(y, zx, norm_w):
    t, di = y.shape
    tr = _tile(t, 512, 8)
    ng = di // GW

    def body(y_ref, z_ref, w_ref, o_ref):
        z = z_ref[...]
        gate = y_ref[...] * (z * _sigmoid(z))
        w = w_ref[...]
        for g in range(ng):
            cols = slice(g * GW, (g + 1) * GW)
            gs = gate[:, cols]
            r = lax.rsqrt(jnp.mean(gs * gs, axis=-1, keepdims=True) + NORM_EPS)
            o_ref[:, cols] = (gs * r * w[:, cols]).astype(BF16)

    row = pl.BlockSpec((tr, di), lambda i: (i, 0))
    return pl.pallas_call(body, name="ssd_gate_norm_fwd", grid=(t // tr,),
                          in_specs=[row, row, pl.BlockSpec((1, di), lambda i: (0, 0))], out_specs=row,
                          out_shape=jax.ShapeDtypeStruct((t, di), BF16), compiler_params=_params("parallel"))(
                              y, zx, norm_w)


def _gate_norm_bwd(dyn, y, zx, norm_w, after):
    t, di = y.shape
    tr = _tile(t, 256, 8)
    ng = di // GW

    def body(d_ref, y_ref, z_ref, w_ref, after_ref, dy_ref, dz_ref, dw_ref):
        z = z_ref[...]
        yv = y_ref[...]
        sg = _sigmoid(z)
        sz = z * sg
        gate = yv * sz
        w = w_ref[...]
        d = d_ref[...]
        dsz = _silu_grad(z, sg)
        dws = []
        for g in range(ng):
            cols = slice(g * GW, (g + 1) * GW)
            dg, dwr = _rms_bwd(gate[:, cols], w[:, cols], d[:, cols])
            dy_ref[:, cols] = dg * sz[:, cols]
            dz_ref[:, cols] = (dg * yv[:, cols] * dsz[:, cols]).astype(BF16)
            dws.append(jnp.sum(dwr, axis=0, keepdims=True))
        first = pl.program_id(0) == 0
        for g in range(ng):
            cols = slice(g * GW, (g + 1) * GW)

            @pl.when(first)
            def _():
                dw_ref[:, cols] = dws[g]

            @pl.when(jnp.logical_not(first))
            def _():
                dw_ref[:, cols] += dws[g]

    row = pl.BlockSpec((tr, di), lambda i: (i, 0))
    vec = pl.BlockSpec((1, di), lambda i: (0, 0))
    return pl.pallas_call(body, name="ssd_gate_norm_bwd", grid=(t // tr,),
                          in_specs=[row, row, row, vec, pl.BlockSpec((8, LANES), lambda i: (0, 0))],
                          out_specs=[row, row, vec],
                          out_shape=[jax.ShapeDtypeStruct((t, di), F32), jax.ShapeDtypeStruct((t, di), BF16),
                                     jax.ShapeDtypeStruct((1, di), F32)],
                          compiler_params=_params("arbitrary"))(dyn, y, zx, norm_w, after)


def _attn_mask_t(n):
    w = ATTN_WINDOW
    kpos = lax.broadcasted_iota(jnp.int32, (2 * w, ATTN_REP * w), 0)
    qpos = lax.broadcasted_iota(jnp.int32, (2 * w, ATTN_REP * w), 1) % w + w
    rel = qpos - kpos
    return (rel >= 0) & (rel < w) & jnp.logical_not((n == 0) & (kpos < w))


def _attn_probs_t(qts, ktb, mask, sink):
    s = _dot_tn(ktb, qts) * (ATTN_HEAD_DIM ** -0.5)
    s = jnp.where(mask, s, -jnp.inf)
    m = jnp.maximum(jnp.max(s, axis=0, keepdims=True), sink)
    e = jnp.exp(s - m)
    es = jnp.exp(sink - m)
    inv = 1.0 / (jnp.sum(e, axis=0, keepdims=True) + es)
    return e * inv, es * inv


def _attn_blocks_t(kv, q_ref, kc_ref, vc_ref, kp_ref, vp_ref):
    hd = ATTN_HEAD_DIM
    rows = slice(kv * hd, (kv + 1) * hd)
    ktb = jnp.concatenate([kp_ref[rows, :], kc_ref[rows, :]], axis=1)
    vtb = jnp.concatenate([vp_ref[rows, :], vc_ref[rows, :]], axis=1)
    qts = jnp.concatenate([q_ref[(kv * ATTN_REP + r) * hd:(kv * ATTN_REP + r + 1) * hd, :]
                           for r in range(ATTN_REP)], axis=1)
    return qts, ktb, vtb


def _attn_specs_t(nb, cur, prev):
    w, hd = ATTN_WINDOW, ATTN_HEAD_DIM
    kd = ATTN_N_KV * hd
    qd = ATTN_REP * kd
    return [pl.BlockSpec((qd, w), lambda n: (0, cur(n))),
            pl.BlockSpec((kd, w), lambda n: (ATTN_REP, cur(n))),
            pl.BlockSpec((kd, w), lambda n: (ATTN_REP + 1, cur(n))),
            pl.BlockSpec((kd, w), lambda n: (ATTN_REP, prev(n))),
            pl.BlockSpec((kd, w), lambda n: (ATTN_REP + 1, prev(n)))]


def _attn_fwd_t(qkv_t, sinks_rep):
    t = qkv_t.shape[1]
    w, hd = ATTN_WINDOW, ATTN_HEAD_DIM
    qd = ATTN_N_KV * ATTN_REP * hd
    nb = t // w

    def body(q_ref, kc_ref, vc_ref, kp_ref, vp_ref, s_ref, o_ref):
        mask = _attn_mask_t(pl.program_id(0))
        for kv in range(ATTN_N_KV):
            qts, ktb, vtb = _attn_blocks_t(kv, q_ref, kc_ref, vc_ref, kp_ref, vp_ref)
            p, _ = _attn_probs_t(qts, ktb, mask, s_ref[kv])
            ots = _dot(vtb, p.astype(BF16))
            for r in range(ATTN_REP):
                h = kv * ATTN_REP + r
                o_ref[h * hd:(h + 1) * hd, :] = ots[:, r * w:(r + 1) * w].astype(BF16)

    return pl.pallas_call(
        body, name="attn_fwd", grid=(nb,),
        in_specs=_attn_specs_t(nb, lambda n: n, lambda n: jnp.maximum(n - 1, 0)) + [
            pl.BlockSpec(sinks_rep.shape, lambda n: (0, 0, 0))],
        out_specs=pl.BlockSpec((qd, w), lambda n: (0, n)),
        out_shape=jax.ShapeDtypeStruct((qd, t), BF16),
        compiler_params=_params("parallel"))(qkv_t, qkv_t, qkv_t, qkv_t, qkv_t, sinks_rep)


def _attn_bwd_t(qkv_t, do_t, sinks_rep):
    t = qkv_t.shape[1]
    w, hd = ATTN_WINDOW, ATTN_HEAD_DIM
    kd = ATTN_N_KV * hd
    qd = ATTN_REP * kd
    nq = ATTN_N_KV * ATTN_REP
    nb = t // w
    rows_all = qd + 2 * kd

    def body(q_ref, kc_ref, vc_ref, kp_ref, vp_ref, do_ref, s_ref, dqkv_ref, bsum_ref, dsk_ref,
             carry_ref, new_ref, bacc_ref, sacc_ref):
        n = pl.program_id(0)

        @pl.when(n == 0)
        def _():
            carry_ref[...] = jnp.zeros_like(carry_ref)
            bacc_ref[...] = jnp.zeros_like(bacc_ref)
            sacc_ref[...] = jnp.zeros_like(sacc_ref)

        @pl.when(n < nb)
        def _():
            mask = _attn_mask_t(n)
            for kv in range(ATTN_N_KV):
                qts, ktb, vtb = _attn_blocks_t(kv, q_ref, kc_ref, vc_ref, kp_ref, vp_ref)
                dots = jnp.concatenate([do_ref[(kv * ATTN_REP + r) * hd:(kv * ATTN_REP + r + 1) * hd, :]
                                        for r in range(ATTN_REP)], axis=1)
                p, ps = _attn_probs_t(qts, ktb, mask, s_ref[kv])
                dpt = _dot_tn(vtb, dots)
                delta = jnp.sum(p * dpt, axis=0, keepdims=True)
                dst = (p * (dpt - delta) * (hd ** -0.5)).astype(BF16)
                dqts = _dot(ktb, dst)
                for r in range(ATTN_REP):
                    h = kv * ATTN_REP + r
                    new_ref[h * hd:(h + 1) * hd, :] = dqts[:, r * w:(r + 1) * w]
                dktb = _dot_nt(qts, dst)
                dvtb = _dot_nt(dots, p.astype(BF16))
                krows = slice(qd + kv * hd, qd + (kv + 1) * hd)
                vrows = slice(qd + kd + kv * hd, qd + kd + (kv + 1) * hd)
                carry_ref[krows, :] += dktb[:, :w]
                carry_ref[vrows, :] += dvtb[:, :w]
                new_ref[krows, :] = dktb[:, w:]
                new_ref[vrows, :] = dvtb[:, w:]
                sacc_ref[kv] += -(ps * delta)

        @pl.when(n >= 1)
        def _():
            done = carry_ref[...]
            dqkv_ref[...] = done.astype(BF16)
            bacc_ref[...] += done

        @pl.when(n < nb)
        def _():
            carry_ref[...] = new_ref[...]

        @pl.when(n == nb)
        def _():
            bsum_ref[...] = jnp.sum(bacc_ref[...], axis=1, keepdims=True)
            lane = lax.broadcasted_iota(jnp.int32, (1, nq), 1)
            dsk = jnp.zeros((1, nq), F32)
            for kv in range(ATTN_N_KV):
                acc = sacc_ref[kv]
                for r in range(ATTN_REP):
                    tot = jnp.sum(acc[:, r * w:(r + 1) * w], axis=1, keepdims=True)
                    dsk = jnp.where(lane == kv * ATTN_REP + r, tot, dsk)
            dsk_ref[...] = dsk

    cur = lambda n: jnp.minimum(n, nb - 1)
    prev = lambda n: jnp.maximum(jnp.minimum(n, nb - 1) - 1, 0)
    return pl.pallas_call(
        body, name="attn_bwd", grid=(nb + 1,),
        in_specs=_attn_specs_t(nb, cur, prev) + [pl.BlockSpec((qd, w), lambda n: (0, cur(n))),
                                                 pl.BlockSpec(sinks_rep.shape, lambda n: (0, 0, 0))],
        out_specs=[pl.BlockSpec((rows_all, w), lambda n: (0, jnp.maximum(n - 1, 0))),
                   pl.BlockSpec((rows_all, 1), lambda n: (0, 0)),
                   pl.BlockSpec((1, nq), lambda n: (0, 0))],
        out_shape=[jax.ShapeDtypeStruct((rows_all, t), BF16), jax.ShapeDtypeStruct((rows_all, 1), F32),
                   jax.ShapeDtypeStruct((1, nq), F32)],
        scratch_shapes=[pltpu.VMEM((rows_all, w), F32), pltpu.VMEM((rows_all, w), F32),
                        pltpu.VMEM((rows_all, w), F32), pltpu.VMEM(sinks_rep.shape, F32)],
        compiler_params=_params("arbitrary"))(qkv_t, qkv_t, qkv_t, qkv_t, qkv_t, do_t, sinks_rep)


HBM_SPEC = pl.BlockSpec(memory_space=pl.ANY)
HBM_ONLY = pl.BlockSpec(memory_space=pltpu.HBM)


def _comm_call(name, body, ins, out_shapes, n_sems):
    return pl.pallas_call(
        body, name=name, in_specs=[HBM_SPEC] * len(ins), out_specs=[HBM_SPEC] * len(out_shapes),
        out_shape=out_shapes,
        scratch_shapes=[pltpu.SemaphoreType.DMA((s,)) for s in n_sems])(*ins)


def _all_gather(name, shards, after):
    n = len(shards)
    na = len(after)

    def body(*refs):
        x_refs, out_refs = refs[:n], refs[n + na:2 * n + na]
        send_sems, recv_sems, local_sems = refs[2 * n + na:]
        x, y, c = lax.axis_index("x"), lax.axis_index("y"), lax.axis_index("c")
        me, sibling = (x, y, c), (x, y, 1 - c)
        chips = [(1 - x, y), (x, 1 - y), (1 - x, 1 - y)]

        def slot(i, px, py, pc):
            return out_refs[i].at[4 * px + 2 * py + pc]

        def copy(k, i, block, to, src=None):
            return pltpu.make_async_remote_copy(
                src_ref=slot(i, *block) if src is None else src, dst_ref=slot(i, *block),
                send_sem=send_sems.at[k * n + i], recv_sem=recv_sems.at[k * n + i], device_id=to,
                device_id_type=MESH)

        mine = [pltpu.make_async_copy(x_refs[i], slot(i, *me), local_sems.at[i]) for i in range(n)]
        first = []
        for i in range(n):
            mine[i].start()
            first.append(copy(0, i, me, sibling, src=x_refs[i]))
            first += [copy(1 + j, i, me, (*chip, c), src=x_refs[i]) for j, chip in enumerate(chips)]
        for cp in first:
            cp.start()
        passed = []
        for i in range(n):
            for j, chip in enumerate(chips):
                copy(1 + j, i, (*chip, c), me).wait_recv()
                passed.append(copy(4 + j, i, (*chip, c), sibling))
                passed[-1].start()
        for i in range(n):
            copy(0, i, sibling, me).wait_recv()
            for j, chip in enumerate(chips):
                copy(4 + j, i, (*chip, 1 - c), me).wait_recv()
        for cp in first + passed:
            cp.wait_send()
        for cp in mine:
            cp.wait()

    outs = [jax.ShapeDtypeStruct((N_DEV,) + s.shape, s.dtype) for s in shards]
    return _comm_call(name, body, list(shards) + list(after), outs, (7 * n, 7 * n, n))


SEM_SPEC = pl.BlockSpec(memory_space=pltpu.SEMAPHORE)
SPLIT_COPY_EFFECT = pltpu.SideEffectType.DATAFLOW_SIDE_EFFECTING


def _in_hbm(a):
    return pltpu.with_memory_space_constraint(a, pltpu.HBM)


def _split_start(name, body, srcs, lands, n_sems):
    n = len(srcs)
    bufs = [_in_hbm(a) for a in list(srcs) + list(lands)]
    outs = pl.pallas_call(
        body, name=name,
        out_shape=(pltpu.SemaphoreType.DMA((n_sems,)), pltpu.SemaphoreType.DMA((n_sems,)),
                   *[pltpu.HBM(a.shape, a.dtype) for a in bufs], jax.ShapeDtypeStruct((8, LANES), F32)),
        in_specs=[HBM_ONLY] * (2 * n),
        out_specs=(SEM_SPEC, SEM_SPEC, *[HBM_ONLY] * (2 * n), pl.BlockSpec(memory_space=pltpu.VMEM)),
        input_output_aliases={i: 2 + i for i in range(2 * n)},
        compiler_params=pltpu.CompilerParams(has_side_effects=SPLIT_COPY_EFFECT))(*bufs)
    return outs[0], outs[1], list(outs[2:2 + n]), list(outs[2 + n:2 + 2 * n]), outs[-1]


def _split_wait(name, body, send_sems, recv_sems, srcs, lands, after):
    n = len(srcs)
    outs = pl.pallas_call(
        body, name=name,
        out_shape=[pltpu.HBM(a.shape, a.dtype) for a in list(srcs) + list(lands)],
        in_specs=[HBM_ONLY] * (2 * n) + [SEM_SPEC, SEM_SPEC, HBM_SPEC],
        out_specs=[HBM_ONLY] * (2 * n),
        input_output_aliases={i: i for i in range(2 * n)},
        compiler_params=pltpu.CompilerParams(has_side_effects=SPLIT_COPY_EFFECT))(
            *srcs, *lands, send_sems, recv_sems, after)
    return list(outs[:n]), list(outs[n:])


N_PEERS = N_DEV - 1


def _gather_peers():
    x, y, c = lax.axis_index("x"), lax.axis_index("y"), lax.axis_index("c")
    flips = [(fx, fy, fc) for fx in (0, 1) for fy in (0, 1) for fc in (0, 1) if fx or fy or fc]
    return [(1 - x if fx else x, 1 - y if fy else y, 1 - c if fc else c) for fx, fy, fc in flips]


def _block_id(dev):
    return 4 * dev[0] + 2 * dev[1] + dev[2]


def _landing_block(land_ref, shard_shape, side_by_side, dev):
    if not side_by_side:
        return land_ref.at[_block_id(dev)]
    cols = shard_shape[1]
    return land_ref.at[:, pl.ds(pl.multiple_of(_block_id(dev) * cols, LANES), cols)]


def _gather_start(name, shards, side_by_side):
    n = len(shards)

    def body(*refs):
        x_refs, land_refs = refs[:n], refs[n:2 * n]
        send_sems, recv_sems, token = refs[2 * n], refs[2 * n + 1], refs[-1]
        me = (lax.axis_index("x"), lax.axis_index("y"), lax.axis_index("c"))
        for i in range(n):
            for k, peer in enumerate(_gather_peers()):
                pltpu.make_async_remote_copy(
                    src_ref=x_refs[i], dst_ref=_landing_block(land_refs[i], shards[i].shape, side_by_side[i], me),
                    send_sem=send_sems.at[N_PEERS * i + k], recv_sem=recv_sems.at[N_PEERS * i + k],
                    device_id=peer, device_id_type=MESH).start()
            pltpu.make_async_copy(x_refs[i], _landing_block(land_refs[i], shards[i].shape, side_by_side[i], me),
                                  send_sems.at[N_PEERS * n + i]).start()
        token[...] = jnp.zeros_like(token)

    lands = [lax.empty((s.shape[0], N_DEV * s.shape[1]) if wide else (N_DEV,) + s.shape, s.dtype)
             for s, wide in zip(shards, side_by_side)]
    return _split_start(name, body, shards, lands, (N_PEERS + 1) * n)


def _gather_wait(name, send_sems, recv_sems, first, n_all, shards, lands, side_by_side, after):
    n = len(shards)

    def body(*refs):
        x_refs, land_refs = refs[:n], refs[n:2 * n]
        send_sems, recv_sems = refs[2 * n], refs[2 * n + 1]
        me = (lax.axis_index("x"), lax.axis_index("y"), lax.axis_index("c"))
        for i in range(n):
            pltpu.make_async_copy(x_refs[i], _landing_block(land_refs[i], shards[i].shape, side_by_side[i], me),
                                  send_sems.at[N_PEERS * n_all + first + i]).wait()
            for k, peer in enumerate(_gather_peers()):
                cp = pltpu.make_async_remote_copy(
                    src_ref=x_refs[i], dst_ref=_landing_block(land_refs[i], shards[i].shape, side_by_side[i], peer),
                    send_sem=send_sems.at[N_PEERS * (first + i) + k],
                    recv_sem=recv_sems.at[N_PEERS * (first + i) + k],
                    device_id=peer, device_id_type=MESH)
                cp.wait_send()
                cp.wait_recv()

    return _split_wait(name, body, send_sems, recv_sems, shards, lands, after)


def _scatter_start(name, blocks):
    n = len(blocks)

    def body(*refs):
        b_refs, land_refs = refs[:n], refs[n:2 * n]
        send_sems, recv_sems, token = refs[2 * n], refs[2 * n + 1], refs[-1]
        me = (lax.axis_index("x"), lax.axis_index("y"), lax.axis_index("c"))
        for i in range(n):
            for k, peer in enumerate(_gather_peers()):
                pltpu.make_async_remote_copy(
                    src_ref=b_refs[i].at[_block_id(peer)], dst_ref=land_refs[i].at[_block_id(me)],
                    send_sem=send_sems.at[N_PEERS * i + k], recv_sem=recv_sems.at[N_PEERS * i + k],
                    device_id=peer, device_id_type=MESH).start()
            pltpu.make_async_copy(b_refs[i].at[_block_id(me)], land_refs[i].at[_block_id(me)],
                                  send_sems.at[N_PEERS * n + i]).start()
        token[...] = jnp.zeros_like(token)

    lands = [lax.empty(b.shape, b.dtype) for b in blocks]
    return _split_start(name, body, blocks, lands, (N_PEERS + 1) * n)


def _scatter_wait(name, send_sems, recv_sems, blocks, lands, after):
    n = len(blocks)

    def body(*refs):
        b_refs, land_refs = refs[:n], refs[n:2 * n]
        send_sems, recv_sems = refs[2 * n], refs[2 * n + 1]
        me = (lax.axis_index("x"), lax.axis_index("y"), lax.axis_index("c"))
        for i in range(n):
            pltpu.make_async_copy(b_refs[i].at[_block_id(me)], land_refs[i].at[_block_id(me)],
                                  send_sems.at[N_PEERS * n + i]).wait()
            for k, peer in enumerate(_gather_peers()):
                cp = pltpu.make_async_remote_copy(
                    src_ref=b_refs[i].at[_block_id(peer)], dst_ref=land_refs[i].at[_block_id(peer)],
                    send_sem=send_sems.at[N_PEERS * i + k], recv_sem=recv_sems.at[N_PEERS * i + k],
                    device_id=peer, device_id_type=MESH)
                cp.wait_send()
                cp.wait_recv()

    return _split_wait(name, body, send_sems, recv_sems, blocks, lands, after)


def _adamw(w, g, m, v):
    m = ADAM_B1 * m + (1.0 - ADAM_B1) * g
    v = ADAM_B2 * v + (1.0 - ADAM_B2) * (g * g)
    m_hat = m / (1.0 - ADAM_B1 ** ADAM_STEP)
    v_hat = v / (1.0 - ADAM_B2 ** ADAM_STEP)
    delta = -ADAM_LR * (m_hat / (jnp.sqrt(v_hat) + ADAM_EPS) + ADAM_WD * w)
    return delta, m, v


def _adamw_tiles(r, c_):
    tr = _tile(r, 256, 16)
    return (tr, c_) if tr < r or r <= 256 else (r, _tile(c_, 256))


def _sum_parts(part):
    g = part[0].astype(F32)
    for k in range(1, part.shape[0]):
        g = g + part[k].astype(F32)
    return g


def _sum_adamw(name, parts, w, m, v):
    r, c_ = w.shape
    tr, tc = _adamw_tiles(r, c_)

    def body(p_ref, w_ref, m_ref, v_ref, g_ref, d_ref, nm_ref, nv_ref):
        g = _sum_parts(p_ref)
        g_ref[...] = g
        d_ref[...], nm_ref[...], nv_ref[...] = _adamw(w_ref[...], g, m_ref[...], v_ref[...])

    tile = pl.BlockSpec((tr, tc), lambda i, j: (i, j))
    return pl.pallas_call(body, name=name, grid=(r // tr, c_ // tc),
                          in_specs=[pl.BlockSpec((parts.shape[0], tr, tc), lambda i, j: (0, i, j)), tile, tile, tile],
                          out_specs=[tile] * 4, out_shape=[jax.ShapeDtypeStruct((r, c_), F32)] * 4,
                          compiler_params=_params("parallel", "parallel"))(parts, w, m, v)


def _sum_adamw_layers(name, parts, w, m, v):
    n_layers, r, c_ = w.shape
    tr = _tile(r, 256, 16)

    def body(*refs):
        p_refs = refs[:n_layers]
        w_ref, m_ref, v_ref, g_ref, d_ref, nm_ref, nv_ref = refs[n_layers:]
        layer = pl.program_id(0)
        g = _sum_parts(p_refs[0])
        for li in range(1, n_layers):
            g = jnp.where(layer == li, _sum_parts(p_refs[li]), g)
        g_ref[...] = g
        d_ref[...], nm_ref[...], nv_ref[...] = _adamw(w_ref[...], g, m_ref[...], v_ref[...])

    row = pl.BlockSpec((None, tr, c_), lambda l, i: (l, i, 0))
    specs = [pl.BlockSpec((p.shape[0], tr, c_), lambda l, i: (0, i, 0)) for p in parts]
    return pl.pallas_call(body, name=name, grid=(n_layers, r // tr), in_specs=specs + [row, row, row],
                          out_specs=[row] * 4, out_shape=[jax.ShapeDtypeStruct(w.shape, F32)] * 4,
                          compiler_params=_params("parallel", "parallel"))(*parts, w, m, v)


def _pack_rows(flat, n_rows, cols):
    pad = n_rows * cols - flat.shape[-1]
    flat = jnp.pad(flat, [(0, 0)] * (flat.ndim - 1) + [(0, pad)])
    return flat.reshape(flat.shape[:-1] + (n_rows, cols))


def _cols_split(full):
    c = full.shape[1] // N_DEV
    return jnp.stack([full[:, d * c:(d + 1) * c] for d in range(N_DEV)])


def _rows_join(blocks):
    return blocks.reshape(N_DEV * blocks.shape[1], blocks.shape[2])


def _rows_split(full):
    return full.reshape(N_DEV, full.shape[0] // N_DEV, full.shape[1])


def _heads_col(v, ng):
    return jnp.pad(v.reshape(ng, 1, SSD_HPG), ((0, 0), (0, 0), (0, LANES - SSD_HPG)))


MATRIX_ITEMS = ("w_in", "w_out", "up0", "down0", "w_qkv", "w_o", "up1", "down1")
VECTOR_ITEMS = ("conv_w", "b_qkv", "b_o")
ITEMS = MATRIX_ITEMS + VECTOR_ITEMS
GATHER_STAGES = (("w_in", "conv_w"), ("w_out", "up0", "down0"), ("w_qkv", "b_qkv", "w_o", "b_o", "up1", "down1"))
SIDE_BY_SIDE = ("conv_w", "up0", "up1", "b_o", "down0", "down1")
GATHERED_TRANSPOSED = ("down0", "down1")


def _items(tree, prefix=""):
    g = lambda k: tree[prefix + k]
    return {"w_in": g("ssd_w_in")[0].T, "w_out": g("ssd_w_out")[0], "w_qkv": g("attn_w_qkv")[0].T,
            "w_o": g("attn_w_o")[0], "up0": g("mlp_w_up")[0], "up1": g("mlp_w_up")[1],
            "down0": g("mlp_w_down")[0], "down1": g("mlp_w_down")[1], "conv_w": g("ssd_conv_w")[0],
            "b_qkv": g("attn_b_qkv"), "b_o": g("attn_b_o")}


REPLICATED = ("ssd_conv_b", "ssd_dt_bias", "ssd_a_log", "ssd_d", "ssd_norm_w", "attn_sinks", "mix_pre_norm",
              "mix_post_norm", "ffn_pre_norm", "ffn_post_norm")
WEIGHTS = ("ssd_w_in", "ssd_conv_w", "ssd_conv_b", "ssd_dt_bias", "ssd_a_log", "ssd_d", "ssd_norm_w", "ssd_w_out",
           "attn_w_qkv", "attn_b_qkv", "attn_sinks", "attn_w_o", "attn_b_o", "mlp_w_up", "mlp_w_down",
           "mix_pre_norm", "mix_post_norm", "ffn_pre_norm", "ffn_post_norm")


def _forward_backward(x, target, rep, token, weights_of_stage, reduce_grads):
    t, d = x.shape
    ng = rep["ssd_norm_w"].shape[1] // GW
    di = ng * GW
    n_xbc = ng * GC
    nh = ng * SSD_HPG
    grads, blocks = {}, {}
    w_up, w_down_t = [None, None], [None, None]
    sinks_rep = jnp.repeat(rep["attn_sinks"].reshape(ATTN_N_KV, ATTN_REP, 1), ATTN_WINDOW, axis=2).reshape(
        ATTN_N_KV, 1, ATTN_REP * ATTN_WINDOW)
    conv_b = rep["ssd_conv_b"]
    gn = ng * SSD_D_STATE
    parts = ((0, di), (di, di), (2 * di, gn), (2 * di + gn, gn), (di + n_xbc, nh))
    alog_c, dsk_c = (_heads_col(rep[k], ng) for k in ("ssd_a_log", "ssd_d"))
    bias_l, alog_l = (jnp.pad(rep[k], ((0, 0), (0, LANES - nh))) for k in ("ssd_dt_bias", "ssd_a_log"))
    norm = {k: rep[k] for k in ("mix_pre_norm", "mix_post_norm", "ffn_pre_norm", "ffn_post_norm")}

    def nrow(name, i):
        return norm[name][i:i + 1]

    def mlp_fwd(i, u2):
        p = _mm(f"mlp{i}_up", [u2], [w_up[i]], "nn", tm=1024, tn=1024, out_dtypes=(BF16,),
                epilogue=lambda acc: (jnp.square(jnp.maximum(acc, 0.0)),))
        f = _mm(f"mlp{i}_down", [p], [w_down_t[i]], "nt", tm=512, tn=1024)
        return p, f

    def mlp_bwd(i, df, u2, p):
        da = _mm(f"mlp{i}_dact", [df], [w_down_t[i]], "nn", tm=1024, tn=1024, out_dtypes=(BF16,),
                 tiles=(p,), epilogue=lambda acc, pv: (acc * (2.0 * jnp.sqrt(pv.astype(F32))),))
        blocks[f"down{i}"] = _rows_split(_mm(f"mlp{i}_dwdown", [p], [df], "tn", tm=512, tn=1024,
                                             out_dtypes=(PAYLOAD,)))
        blocks[f"up{i}"] = _mm(f"mlp{i}_dwup", [u2], [da], "tn", tm=1024, tn=da.shape[1] // N_DEV,
                               out_dtypes=(PAYLOAD,), col_blocks=True)
        return _mm(f"mlp{i}_dx", [da], [w_up[i]], "nt", tm=512, tn=1024)

    u0 = _prenorm("l0_prenorm", x, nrow("mix_pre_norm", 0), token)
    got = weights_of_stage(0, u0)
    w_in_t = _rows_join(got["w_in"])
    w_dt_t = jnp.pad(w_in_t[di + n_xbc:], ((0, LANES - nh), (0, 0)))
    conv_w = got["conv_w"]
    zx = _mm("ssd_in_proj", [u0], [w_in_t], "nt", tm=1024, tn=1024, n_use=di + n_xbc)
    zdt = _mm("ssd_dt_proj", [u0], [w_dt_t], "nt", tm=1024, tn=LANES)
    pre = _conv_fwd(zx, di, n_xbc, conv_w, conv_b)
    dt_c, cum_c, cum_r, sgd_c = _ssd_dt_prep(zdt, bias_l, alog_l, ng)
    y, states = _ssd_fwd(pre, dt_c, cum_c, cum_r, alog_c, dsk_c)
    yn = _gate_norm_fwd(y, zx, rep["ssd_norm_w"])
    got = weights_of_stage(1, yn)
    w_out = _rows_join(got["w_out"])
    w_up[0], w_down_t[0] = got["up0"], got["down0"]
    mix0 = _mm("ssd_out_proj", [yn], [w_out], "nn", tm=1024, tn=1024)
    h1, u0f = _post_pre("l0_mid", x, mix0, nrow("mix_post_norm", 0), nrow("ffn_pre_norm", 0))
    p0, f0 = mlp_fwd(0, u0f)
    h2, u1 = _post_pre("l1_in", h1, f0, nrow("ffn_post_norm", 0), nrow("mix_pre_norm", 1))
    got = weights_of_stage(2, u1)
    w_qkv_t = _rows_join(got["w_qkv"])
    w_o = _rows_join(got["w_o"])
    b_qkv_col = got["b_qkv"].reshape(-1, 1)
    b_o = got["b_o"]
    w_up[1], w_down_t[1] = got["up1"], got["down1"]
    qkv_t = _mm("attn_qkv_proj", [w_qkv_t], [u1], "nt", tm=768, tn=1024, out_dtypes=(BF16,), cols=(b_qkv_col,),
                epilogue=lambda acc, b: (acc + b,))
    ao_t = _attn_fwd_t(qkv_t, sinks_rep)
    mix1 = _mm("attn_out_proj", [ao_t], [w_o], "tn", tm=1024, tn=1024, rows=(b_o,),
               epilogue=lambda acc, b: (acc + b,))
    h3, u1f = _post_pre("l1_mid", h2, mix1, nrow("mix_post_norm", 1), nrow("ffn_pre_norm", 1))
    p1, f1 = mlp_fwd(1, u1f)
    dh, loss_row = _final_loss("loss", h3, f1, nrow("ffn_post_norm", 1), target)

    g_norm = {k: [None, None] for k in norm}
    df1, g_norm["ffn_post_norm"][1], _ = _norm_bwd("l1_ffn_post_bwd", dh, post=(f1, nrow("ffn_post_norm", 1)))
    du = mlp_bwd(1, df1, u1f, p1)
    sent = reduce_grads("mlp1", {k: blocks[k] for k in ("up1", "down1")})
    dh, g_norm["ffn_pre_norm"][1], dmix1, g_norm["mix_post_norm"][1], db_o = _norm_bwd(
        "l1_mid_bwd", dh, pre=(du, h3, nrow("ffn_pre_norm", 1)), post=(mix1, nrow("mix_post_norm", 1)), after=sent)
    blocks["b_o"] = _cols_split(db_o)
    blocks["w_o"] = _rows_split(_mm("attn_dwo", [ao_t], [dmix1], "nn", tm=512, tn=1024, out_dtypes=(PAYLOAD,)))
    dao_t = _mm("attn_dout", [w_o], [dmix1], "nt", tm=1024, tn=1024, out_dtypes=(BF16,))
    dqkv_t, db_qkv, grads["attn_sinks"] = _attn_bwd_t(qkv_t, dao_t, sinks_rep)
    blocks["b_qkv"] = db_qkv.reshape(N_DEV, 1, -1)
    blocks["w_qkv"] = _rows_split(_mm("attn_dwqkv", [dqkv_t], [u1], "nn", tm=512, tn=1024, out_dtypes=(PAYLOAD,)))
    du = _mm("attn_dx", [dqkv_t], [w_qkv_t], "tn", tm=1024, tn=1024)
    sent = reduce_grads("attn", {k: blocks[k] for k in ("w_o", "w_qkv", "b_o", "b_qkv")})
    dh, g_norm["mix_pre_norm"][1], df0, g_norm["ffn_post_norm"][0], _ = _norm_bwd(
        "l1_in_bwd", dh, pre=(du, h2, nrow("mix_pre_norm", 1)), post=(f0, nrow("ffn_post_norm", 0)), after=sent)
    du = mlp_bwd(0, df0, u0f, p0)
    sent = reduce_grads("mlp0", {k: blocks[k] for k in ("up0", "down0")})
    dh, g_norm["ffn_pre_norm"][0], dmix0, g_norm["mix_post_norm"][0], _ = _norm_bwd(
        "l0_mid_bwd", dh, pre=(du, h1, nrow("ffn_pre_norm", 0)), post=(mix0, nrow("mix_post_norm", 0)), after=sent)
    blocks["w_out"] = _rows_split(_mm("ssd_dwout", [yn], [dmix0], "tn", tm=512, tn=1024, out_dtypes=(PAYLOAD,)))
    dyn = _mm("ssd_dyn", [dmix0], [w_out], "nt", tm=1024, tn=1024)
    sent = reduce_grads("ssdout", {"w_out": blocks["w_out"]})
    dy, dz, grads["ssd_norm_w"] = _gate_norm_bwd(dyn, y, zx, rep["ssd_norm_w"], sent)
    dpx, dpb, dpc, ddt_g, dbias_g, dalog_g, dd_g = _ssd_bwd(dy, pre, states, dt_c, cum_c, cum_r, sgd_c, alog_c,
                                                             dsk_c)
    conv_out = [_conv_bwd(f"ssd_conv_bwd_{tag}", dp, zx, c0, conv_w[:, c0 - di:c0 - di + n])
                for tag, dp, (c0, n) in zip("xbc", (dpx, dpb, dpc), parts[1:4])]
    dconv_w = jnp.concatenate([o[1] for o in conv_out], axis=1)
    dconv_b = jnp.concatenate([o[2] for o in conv_out], axis=1)
    ddt = jnp.transpose(ddt_g[:, :, :SSD_HPG], (1, 0, 2)).reshape(t, nh)
    ddt = jnp.pad(ddt, ((0, 0), (0, LANES - nh))).astype(BF16)
    blocks["conv_w"] = _cols_split(dconv_w)
    grads["ssd_conv_b"] = dconv_b
    for name, val in (("ssd_dt_bias", dbias_g), ("ssd_a_log", dalog_g), ("ssd_d", dd_g)):
        grads[name] = val[:, 0, :SSD_HPG].reshape(1, nh)
    d_zx = [dz] + [o[0] for o in conv_out] + [ddt]
    dw_parts = [_mm(f"ssd_dw_{tag}", [d], [u0], "tn", tm=512, tn=1024, out_dtypes=(PAYLOAD,))
                for tag, d in zip("zxbct", d_zx)]
    dw_parts[-1] = dw_parts[-1][:nh]
    blocks["w_in"] = _rows_split(jnp.concatenate(dw_parts, axis=0))
    sent = reduce_grads("ssd", {k: blocks[k] for k in ("w_in", "conv_w")})
    w_parts = [w_in_t[r0:r0 + n] for r0, n in parts[:-1]] + [w_dt_t]
    du = _mm("ssd_dx", d_zx, w_parts, "nn", tm=256, tn=1024, after=sent)
    grad_x, g_norm["mix_pre_norm"][0] = _norm_bwd("l0_in_bwd", dh, pre=(du, x, nrow("mix_pre_norm", 0)), after=sent)
    for k in norm:
        grads[k] = jnp.concatenate(g_norm[k], axis=0)
    return loss_row, grad_x, grads


def kernel(x, ssd_w_in, ssd_conv_w, ssd_conv_b, ssd_dt_bias, ssd_a_log, ssd_d, ssd_norm_w, ssd_w_out, attn_w_qkv, attn_b_qkv, attn_sinks, attn_w_o, attn_b_o, mlp_w_up, mlp_w_down, mix_pre_norm, mix_post_norm, ffn_pre_norm, ffn_post_norm, loss_target, m_ssd_w_in, m_ssd_conv_w, m_ssd_conv_b, m_ssd_dt_bias, m_ssd_a_log, m_ssd_d, m_ssd_norm_w, m_ssd_w_out, m_attn_w_qkv, m_attn_b_qkv, m_attn_sinks, m_attn_w_o, m_attn_b_o, m_mlp_w_up, m_mlp_w_down, m_mix_pre_norm, m_mix_post_norm, m_ffn_pre_norm, m_ffn_post_norm, v_ssd_w_in, v_ssd_conv_w, v_ssd_conv_b, v_ssd_dt_bias, v_ssd_a_log, v_ssd_d, v_ssd_norm_w, v_ssd_w_out, v_attn_w_qkv, v_attn_b_qkv, v_attn_sinks, v_attn_w_o, v_attn_b_o, v_mlp_w_up, v_mlp_w_down, v_mix_pre_norm, v_mix_post_norm, v_ffn_pre_norm, v_ffn_post_norm):
    given = dict(locals())
    w = {k: given[k] for k in WEIGHTS}
    mom_m = {k: given["m_" + k] for k in WEIGHTS}
    mom_v = {k: given["v_" + k] for k in WEIGHTS}
    w_it, m_it, v_it = _items(given), _items(given, "m_"), _items(given, "v_")

    order = [k for stage in GATHER_STAGES for k in stage]
    shards = [w_it[k].astype(PAYLOAD) if k in MATRIX_ITEMS else w_it[k] for k in order]
    shards = [s.T if k in GATHERED_TRANSPOSED else s for k, s in zip(order, shards)]
    wide = [k in SIDE_BY_SIDE for k in order]
    g_send, g_recv, shards, lands, token = _gather_start("gather_start", shards, wide)

    def weights_of_stage(s, after):
        first = sum(len(stage) for stage in GATHER_STAGES[:s])
        sl = slice(first, first + len(GATHER_STAGES[s]))
        _, got = _gather_wait(f"gather_wait{s}", g_send, g_recv, first, len(order), shards[sl], lands[sl], wide[sl],
                              after)
        return dict(zip(GATHER_STAGES[s], got))

    in_flight = []

    def reduce_grads(tag, blocks):
        keys = list(blocks)
        started = _scatter_start(f"rs_start_{tag}", [blocks[k] for k in keys])
        in_flight.append((tag, keys, started))
        return started[-1]

    rep = {k: w[k] for k in REPLICATED}
    loss_row, grad_x, grads = _forward_backward(x[0], loss_target[0], rep, token, weights_of_stage, reduce_grads)

    def pack_rep(tree, last):
        flat = jnp.concatenate([tree[k].reshape(-1) for k in REPLICATED] + [last])
        return _pack_rows(flat, _round_up(-(-flat.shape[0] // LANES), 8), LANES)

    landed = {}

    def wait_group(group, after):
        tag, keys, (s_send, s_recv, srcs, s_lands, _) = group
        _, got = _scatter_wait(f"rs_wait_{tag}", s_send, s_recv, srcs, s_lands, after)
        landed.update(zip(keys, got))

    def adamw_item(k):
        return _sum_adamw(f"adamw_{k}", landed[k], w_it[k], m_it[k], v_it[k])

    def adamw_stack(name, keys):
        return _sum_adamw_layers(f"adamw_{name}", [landed[k] for k in keys], given[name], given["m_" + name],
                                 given["v_" + name])

    for group in in_flight[:-1]:
        wait_group(group, grad_x)
    done = {"mlp_w_up": adamw_stack("mlp_w_up", ("up0", "up1")),
            "mlp_w_down": adamw_stack("mlp_w_down", ("down0", "down1")),
            "attn_w_qkv": [o.T[None] for o in adamw_item("w_qkv")],
            "attn_w_o": [o[None] for o in adamw_item("w_o")],
            "attn_b_qkv": adamw_item("b_qkv"), "attn_b_o": adamw_item("b_o"),
            "ssd_w_out": [o[None] for o in adamw_item("w_out")]}
    partials, = _all_gather("gather_small_grads", [pack_rep(grads, loss_row[0, :1])],
                            [outs4[0] for outs4 in done.values()])
    wait_group(in_flight[-1], partials)
    done["ssd_w_in"] = [o.T[None] for o in adamw_item("w_in")]
    done["ssd_conv_w"] = [o[None] for o in adamw_item("conv_w")]
    zero = jnp.zeros((1,), F32)
    rep_out = _sum_adamw("adamw_replicated", partials, pack_rep(w, zero), pack_rep(mom_m, zero), pack_rep(mom_v, zero))

    kinds = []
    for kind, r_arr in enumerate(rep_out):
        tree = {name: outs4[kind] for name, outs4 in done.items()}
        flat, off = r_arr.reshape(-1), 0
        for k in REPLICATED:
            tree[k] = flat[off:off + w[k].size].reshape(w[k].shape)
            off += w[k].size
        kinds.append(tree)
    loss = rep_out[0].reshape(-1)[off]
    outs = [loss, grad_x[None]]
    for tree in kinds:
        outs += [tree[k] for k in WEIGHTS]
    return tuple(outs)
```

```python
import jax
import jax.numpy as jnp
from jax import lax
from jax.experimental import pallas as pl
from jax.experimental.pallas import tpu as pltpu

F32 = jnp.float32
BF16 = jnp.bfloat16
PAYLOAD = jnp.bfloat16
HIGHEST = lax.Precision.HIGHEST
MESH = pl.DeviceIdType.MESH

NORM_EPS = 1e-6
SSD_HEAD_DIM = 64
SSD_HPG = 4
SSD_D_STATE = 128
SSD_CONV_WIDTH = 4
SSD_CHUNK = 128
ATTN_HEAD_DIM = 64
ATTN_N_KV = 4
ATTN_REP = 4
ATTN_WINDOW = 128
ADAM_LR = 0.001
ADAM_B1 = 0.9
ADAM_B2 = 0.999
ADAM_EPS = 1e-08
ADAM_WD = 0.01
ADAM_STEP = 10

N_DEV = 8
LANES = 128
V7X_VMEM_LIMIT = 56 * 1024 * 1024

GW = SSD_HPG * SSD_HEAD_DIM
GC = GW + 2 * SSD_D_STATE
assert SSD_CHUNK == LANES


def _params(*sem):
    return pltpu.CompilerParams(dimension_semantics=sem, vmem_limit_bytes=V7X_VMEM_LIMIT)


def _tile(n, pref, mult=LANES):
    best = None
    t = mult
    while t <= min(n, pref):
        if n % t == 0:
            best = t
        t += mult
    return best if best is not None else n


def _round_up(n, m):
    return (n + m - 1) // m * m


def _acc(ref, val, first):
    @pl.when(first)
    def _():
        ref[...] = val

    @pl.when(jnp.logical_not(first))
    def _():
        ref[...] += val


def _dot(a, b):
    return lax.dot_general(a, b, (((1,), (0,)), ((), ())), preferred_element_type=F32)


def _dot_nt(a, b):
    return lax.dot_general(a, b, (((1,), (1,)), ((), ())), preferred_element_type=F32)


def _dot_tn(a, b):
    return lax.dot_general(a, b, (((0,), (0,)), ((), ())), preferred_element_type=F32)


def _dot_f32(a, b):
    return lax.dot_general(a, b, (((1,), (0,)), ((), ())), preferred_element_type=F32, precision=HIGHEST)


_DOTS = {"nn": _dot, "nt": _dot_nt, "tn": _dot_tn}


def _sigmoid(x):
    return 1.0 / (1.0 + jnp.exp(-x))


def _softplus(x):
    return jnp.maximum(x, 0.0) + jnp.log1p(jnp.exp(-jnp.abs(x)))


def _silu_grad(x, s):
    return s * (1.0 + x * (1.0 - s))


def _mm(name, a_list, b_list, mode, *, tm, tn, out_dtypes=(F32,), epilogue=None, tiles=(), rows=(), cols=(),
        col_blocks=False, n_use=None, after=None):
    npair = len(a_list)
    if mode == "tn":
        m = a_list[0].shape[1]
    else:
        m = a_list[0].shape[0]
    n = n_use if n_use is not None else (b_list[0].shape[0] if mode == "nt" else b_list[0].shape[1])
    tm = _tile(m, tm, LANES if mode == "tn" else 8)
    tn = _tile(n, tn)
    assert m % tm == 0 and n % tn == 0, (name, m, n, tm, tn)
    dot = _DOTS[mode]

    def body(*refs):
        a_refs = refs[:npair]
        b_refs = refs[npair:2 * npair]
        n_extra = len(tiles) + len(rows) + len(cols)
        e_refs = refs[2 * npair:2 * npair + n_extra]
        o_refs = refs[2 * npair + n_extra + len(order):]
        acc = None
        for ar, br in zip(a_refs, b_refs):
            d = dot(ar[...], br[...])
            acc = d if acc is None else acc + d
        outs = epilogue(acc, *[e[...] for e in e_refs]) if epilogue is not None else (acc,)
        for o, v in zip(o_refs, outs):
            o[...] = v.astype(o.dtype)

    in_specs = []
    for a in a_list:
        if mode == "tn":
            in_specs.append(pl.BlockSpec((a.shape[0], tm), lambda i, j: (0, i)))
        else:
            in_specs.append(pl.BlockSpec((tm, a.shape[1]), lambda i, j: (i, 0)))
    for b in b_list:
        if mode == "nt":
            in_specs.append(pl.BlockSpec((tn, b.shape[1]), lambda i, j: (j, 0)))
        else:
            in_specs.append(pl.BlockSpec((b.shape[0], tn), lambda i, j: (0, j)))
    in_specs += [pl.BlockSpec((tm, tn), lambda i, j: (i, j)) for _ in tiles]
    in_specs += [pl.BlockSpec((1, tn), lambda i, j: (0, j)) for _ in rows]
    in_specs += [pl.BlockSpec((tm, 1), lambda i, j: (i, 0)) for _ in cols]
    order = [] if after is None else [after]
    in_specs += [pl.BlockSpec((8, LANES), lambda i, j: (0, 0)) for _ in order]
    outs = pl.pallas_call(
        body,
        name=name,
        grid=(m // tm, n // tn),
        in_specs=in_specs,
        out_specs=[pl.BlockSpec((None, tm, tn), lambda i, j: (j, i, 0)) if col_blocks else
                   pl.BlockSpec((tm, tn), lambda i, j: (i, j)) for _ in out_dtypes],
        out_shape=[jax.ShapeDtypeStruct((n // tn, m, tn) if col_blocks else (m, n), dt) for dt in out_dtypes],
        compiler_params=_params("parallel", "parallel"),
    )(*a_list, *b_list, *tiles, *rows, *cols, *order)
    return outs[0] if len(out_dtypes) == 1 else outs


def _rms(x, w):
    r = lax.rsqrt(jnp.mean(x * x, axis=-1, keepdims=True) + NORM_EPS)
    return x * r * w


def _rms_bwd(x, w, dy):
    r = lax.rsqrt(jnp.mean(x * x, axis=-1, keepdims=True) + NORM_EPS)
    xh = x * r
    g = dy * w
    dx = r * (g - xh * jnp.mean(g * xh, axis=-1, keepdims=True))
    return dx, dy * xh


def _row_specs(tr, d):
    return pl.BlockSpec((tr, d), lambda i: (i, 0)), pl.BlockSpec((1, d), lambda i: (0, 0))


def _prenorm(name, h, w, after):
    t, d = h.shape
    tr = _tile(t, 512, 8)
    row, vec = _row_specs(tr, d)

    def body(h_ref, w_ref, after_ref, u_ref):
        u_ref[...] = _rms(h_ref[...], w_ref[...]).astype(BF16)

    return pl.pallas_call(body, name=name, grid=(t // tr,),
                          in_specs=[row, vec, pl.BlockSpec((8, LANES), lambda i: (0, 0))], out_specs=row,
                          out_shape=jax.ShapeDtypeStruct((t, d), BF16), compiler_params=_params("parallel"))(
                              h, w, after)


def _post_pre(name, h, m, w_post, w_pre):
    t, d = h.shape
    tr = _tile(t, 512, 8)
    row, vec = _row_specs(tr, d)

    def body(h_ref, m_ref, wq_ref, wp_ref, hn_ref, u_ref):
        hn = h_ref[...] + _rms(m_ref[...], wq_ref[...])
        hn_ref[...] = hn
        u_ref[...] = _rms(hn, wp_ref[...]).astype(BF16)

    return pl.pallas_call(body, name=name, grid=(t // tr,), in_specs=[row, row, vec, vec], out_specs=[row, row],
                          out_shape=[jax.ShapeDtypeStruct((t, d), F32), jax.ShapeDtypeStruct((t, d), BF16)],
                          compiler_params=_params("parallel"))(h, m, w_post, w_pre)


def _final_loss(name, h, m, w_post, target):
    t, d = h.shape
    tr = _tile(t, 512, 8)
    row, vec = _row_specs(tr, d)

    def body(h_ref, m_ref, wq_ref, t_ref, dh_ref, loss_ref):
        err = h_ref[...] + _rms(m_ref[...], wq_ref[...]) - t_ref[...]
        dh_ref[...] = err * (1.0 / d)
        part = 0.5 * jnp.sum(jnp.mean(err * err, axis=-1, keepdims=True), axis=0, keepdims=True)
        _acc(loss_ref, jnp.broadcast_to(part, (1, LANES)), pl.program_id(0) == 0)

    return pl.pallas_call(body, name=name, grid=(t // tr,), in_specs=[row, row, vec, row],
                          out_specs=[row, pl.BlockSpec((1, LANES), lambda i: (0, 0))],
                          out_shape=[jax.ShapeDtypeStruct((t, d), F32), jax.ShapeDtypeStruct((1, LANES), F32)],
                          compiler_params=_params("arbitrary"))(h, m, w_post, target)


def _norm_bwd(name, dh, pre=None, post=None, after=None):
    t, d = dh.shape
    tr = _tile(t, 512, 8)
    row, vec = _row_specs(tr, d)
    has_pre, has_post = pre is not None, post is not None

    def body(*refs):
        it = iter(refs)
        dh_ref = next(it)
        if has_pre:
            du_ref, x_ref, wp_ref = next(it), next(it), next(it)
        if has_post:
            m_ref, wq_ref = next(it), next(it)
        if after is not None:
            next(it)
        first = pl.program_id(0) == 0
        dh_v = dh_ref[...]
        if has_pre:
            dhn_ref, dwp_ref = next(it), next(it)
            dx, dwr = _rms_bwd(x_ref[...], wp_ref[...], du_ref[...])
            dh_v = dh_v + dx
            dhn_ref[...] = dh_v
            _acc(dwp_ref, jnp.sum(dwr, axis=0, keepdims=True), first)
        if has_post:
            dm_ref, dwq_ref, dms_ref = next(it), next(it), next(it)
            dm, dwr = _rms_bwd(m_ref[...], wq_ref[...], dh_v)
            dm_ref[...] = dm.astype(BF16)
            _acc(dwq_ref, jnp.sum(dwr, axis=0, keepdims=True), first)
            _acc(dms_ref, jnp.sum(dm, axis=0, keepdims=True), first)

    ins, in_specs, out_specs, out_shape = [dh], [row], [], []
    if has_pre:
        ins += list(pre)
        in_specs += [row, row, vec]
        out_specs += [row, vec]
        out_shape += [jax.ShapeDtypeStruct((t, d), F32), jax.ShapeDtypeStruct((1, d), F32)]
    if has_post:
        ins += list(post)
        in_specs += [row, vec]
        out_specs += [row, vec, vec]
        out_shape += [jax.ShapeDtypeStruct((t, d), BF16), jax.ShapeDtypeStruct((1, d), F32),
                      jax.ShapeDtypeStruct((1, d), F32)]
    if after is not None:
        ins.append(after)
        in_specs.append(pl.BlockSpec((8, LANES), lambda i: (0, 0)))
    return pl.pallas_call(body, name=name, grid=(t // tr,), in_specs=in_specs, out_specs=out_specs,
                          out_shape=out_shape, compiler_params=_params("arbitrary"))(*ins)


HALO = 8


def _shift_later(cur, prev, s):
    rolled = pltpu.roll(cur, s, 0)
    row = lax.broadcasted_iota(jnp.int32, prev.shape, 0)
    first = jnp.where(row < s, pltpu.roll(prev, s, 0), rolled[0:HALO])
    return jnp.concatenate([first, rolled[HALO:]], axis=0)


def _shift_earlier(cur, nxt, s):
    tt = cur.shape[0]
    rolled = pltpu.roll(cur, tt - s, 0)
    row = lax.broadcasted_iota(jnp.int32, nxt.shape, 0)
    last = jnp.where(row >= HALO - s, pltpu.roll(nxt, HALO - s, 0), rolled[tt - HALO:])
    return jnp.concatenate([rolled[:tt - HALO], last], axis=0)


def _conv_fwd(zx, col0, n_ch, conv_w, conv_b):
    t = zx.shape[0]
    tc = _tile(n_ch, 512)
    tt = _tile(t, 1024, 8)
    cb0 = col0 // tc
    assert col0 % tc == 0
    kw = SSD_CONV_WIDTH

    def body(x_ref, p_ref, w_ref, b_ref, o_ref):
        cur = x_ref[...]
        prev = jnp.where(pl.program_id(1) > 0, p_ref[...], 0.0)
        w = w_ref[...]
        acc = b_ref[...] + w[kw - 1:kw, :] * cur
        for k in range(kw - 1):
            acc = acc + w[k:k + 1, :] * _shift_later(cur, prev, kw - 1 - k)
        o_ref[...] = acc

    return pl.pallas_call(
        body, name="ssd_conv_fwd", grid=(n_ch // tc, t // tt),
        in_specs=[pl.BlockSpec((tt, tc), lambda j, i: (i, cb0 + j)),
                  pl.BlockSpec((HALO, tc), lambda j, i: (jnp.maximum(i * (tt // HALO) - 1, 0), cb0 + j)),
                  pl.BlockSpec((kw, tc), lambda j, i: (0, j)),
                  pl.BlockSpec((1, tc), lambda j, i: (0, j))],
        out_specs=pl.BlockSpec((tt, tc), lambda j, i: (i, j)),
        out_shape=jax.ShapeDtypeStruct((t, n_ch), F32),
        compiler_params=_params("parallel", "parallel"))(zx, zx, conv_w, conv_b)


def _conv_bwd(name, dpre, zx, col0, conv_w):
    t, n_ch = dpre.shape
    tc = _tile(n_ch, 512)
    tt = _tile(t, 1024, 8)
    cb0 = col0 // tc
    kw = SSD_CONV_WIDTH
    nt = t // tt

    def body(d_ref, dn_ref, x_ref, p_ref, w_ref, dx_ref, dw_ref, db_ref):
        i = pl.program_id(1)
        d = d_ref[...]
        d_next = jnp.where(i < nt - 1, dn_ref[...], 0.0)
        x = x_ref[...]
        x_prev = jnp.where(i > 0, p_ref[...], 0.0)
        w = w_ref[...]
        dx = w[kw - 1:kw, :] * d
        for k in range(kw - 1):
            dx = dx + w[k:k + 1, :] * _shift_earlier(d, d_next, kw - 1 - k)
        dx_ref[...] = dx.astype(BF16)
        first = i == 0
        for k in range(kw):
            xs = x if k == kw - 1 else _shift_later(x, x_prev, kw - 1 - k)
            val = jnp.sum(d * xs, axis=0, keepdims=True)

            @pl.when(first)
            def _():
                dw_ref[k:k + 1, :] = val

            @pl.when(jnp.logical_not(first))
            def _():
                dw_ref[k:k + 1, :] += val
        _acc(db_ref, jnp.sum(d, axis=0, keepdims=True), first)

    return pl.pallas_call(
        body, name=name, grid=(n_ch // tc, nt),
        in_specs=[pl.BlockSpec((tt, tc), lambda j, i: (i, j)),
                  pl.BlockSpec((HALO, tc), lambda j, i: (jnp.minimum((i + 1) * (tt // HALO), t // HALO - 1), j)),
                  pl.BlockSpec((tt, tc), lambda j, i: (i, cb0 + j)),
                  pl.BlockSpec((HALO, tc), lambda j, i: (jnp.maximum(i * (tt // HALO) - 1, 0), cb0 + j)),
                  pl.BlockSpec((kw, tc), lambda j, i: (0, j))],
        out_specs=[pl.BlockSpec((tt, tc), lambda j, i: (i, j)),
                   pl.BlockSpec((kw, tc), lambda j, i: (0, j)),
                   pl.BlockSpec((1, tc), lambda j, i: (0, j))],
        out_shape=[jax.ShapeDtypeStruct((t, n_ch), BF16), jax.ShapeDtypeStruct((kw, n_ch), F32),
                   jax.ShapeDtypeStruct((1, n_ch), F32)],
        compiler_params=_params("parallel", "arbitrary"))(dpre, dpre, zx, zx, conv_w)


def _head_of_lane(shape, width):
    return lax.broadcasted_iota(jnp.int32, shape, len(shape) - 1) // width


def _select_dot(v, pick, pick_first=False):
    hi = v.astype(BF16)
    lo = (v - hi.astype(F32)).astype(BF16)
    return _dot(pick, hi) + _dot(pick, lo) if pick_first else _dot(hi, pick) + _dot(lo, pick)


def _expand(v, n_rows, on_mxu=False):
    if not on_mxu:
        head = _head_of_lane((n_rows, GW), SSD_HEAD_DIM)
        out = jnp.zeros((n_rows, GW), F32)
        for j in range(SSD_HPG):
            out = jnp.where(head == j, v[:, j:j + 1], out)
        return out
    src = lax.broadcasted_iota(jnp.int32, (LANES, GW), 0)
    return _select_dot(v, (src == _head_of_lane((LANES, GW), SSD_HEAD_DIM)).astype(BF16))


def _contract(v, n_rows, on_mxu=False):
    if not on_mxu:
        head = _head_of_lane((n_rows, GW), SSD_HEAD_DIM)
        lane = lax.broadcasted_iota(jnp.int32, (n_rows, LANES), 1)
        out = jnp.zeros((n_rows, LANES), F32)
        for j in range(SSD_HPG):
            s = jnp.sum(jnp.where(head == j, v, 0.0), axis=1, keepdims=True)
            out = jnp.where(lane == j, s, out)
        return out
    dst = lax.broadcasted_iota(jnp.int32, (GW, LANES), 1)
    return _select_dot(v, (lax.broadcasted_iota(jnp.int32, (GW, LANES), 0) // SSD_HEAD_DIM == dst).astype(BF16))


def _ssd_dt_prep(zdt, bias, alog, ng):
    t = zdt.shape[0]
    q = SSD_CHUNK

    def body(z_ref, b_ref, a_ref, dt_ref, cum_ref, cumr_ref, sg_ref):
        raw = z_ref[...] + b_ref[...]
        dt = _softplus(raw)
        sgd = _sigmoid(raw)
        row = lax.broadcasted_iota(jnp.int32, (q, q), 0)
        col = lax.broadcasted_iota(jnp.int32, (q, q), 1)
        cum = _dot_f32((col <= row).astype(F32), dt * (-jnp.exp(a_ref[...])))
        cum_t = cum.T
        lane = lax.broadcasted_iota(jnp.int32, (q, LANES), 1)
        for g in range(ng):
            shift = (LANES - g * SSD_HPG) % LANES

            def group(v):
                return jnp.where(lane < SSD_HPG, pltpu.roll(v, shift, 1) if shift else v, 0.0)

            dt_ref[g] = group(dt)
            cum_ref[g] = group(cum)
            sg_ref[g] = group(sgd)
            cumr_ref[g] = (pltpu.roll(cum_t, shift, 0) if shift else cum_t)[0:8, :]

    cols = pl.BlockSpec((ng, q, LANES), lambda c: (0, c, 0))
    vec = pl.BlockSpec((1, LANES), lambda c: (0, 0))
    col_shape = jax.ShapeDtypeStruct((ng, t, LANES), F32)
    return pl.pallas_call(body, name="ssd_dt_prep", grid=(t // q,),
                          in_specs=[pl.BlockSpec((q, LANES), lambda c: (c, 0)), vec, vec],
                          out_specs=[cols, cols, pl.BlockSpec((ng, 8, q), lambda c: (0, 0, c)), cols],
                          out_shape=[col_shape, col_shape, jax.ShapeDtypeStruct((ng, 8, t), F32), col_shape],
                          compiler_params=_params("parallel"))(zdt, bias, alog)


def _ssd_common(pre, dt, cum, cum_r, alog_c, on_mxu):
    q = SSD_CHUNK
    sg = _sigmoid(pre)
    act = pre * sg
    xa = act[:, :GW]
    bm = act[:, GW:GW + SSD_D_STATE].astype(BF16)
    cm = act[:, GW + SSD_D_STATE:].astype(BF16)
    row = lax.broadcasted_iota(jnp.int32, (q, q), 0)
    col = lax.broadcasted_iota(jnp.int32, (q, q), 1)
    tril = col <= row
    a_c = -jnp.exp(alog_c)
    g = _dot_nt(cm, bm)
    dt_x = _expand(dt, q, on_mxu)
    xdt = xa * dt_x
    cl = cum[q - 1:q, :]
    e_c = jnp.exp(cl - cum)
    lam_c = jnp.exp(cum)
    return dict(sg=sg, xa=xa, bm=bm, cm=cm, tril=tril, row=row, col=col, dt=dt, a_c=a_c, cum=cum, cum_r=cum_r,
                g=g, dt_x=dt_x, xdt=xdt, cl=cl, e_c=e_c, lam_c=lam_c)


SSD_GPS_FWD = 8
SSD_GPS_BWD = 2


def _ssd_specs(nc, rev, ng, gps):
    q = SSD_CHUNK
    xw, nw = gps * GW, gps * SSD_D_STATE
    b_off = ng * GW // nw
    c_off = (ng * GW + ng * SSD_D_STATE) // nw
    assert ng % gps == 0 and (ng * GW) % nw == 0 and (ng * SSD_D_STATE) % nw == 0

    def ch(c):
        return nc - 1 - c if rev else c

    chunk_grp = [pl.BlockSpec((q, xw), lambda g, c: (ch(c), g)),
                 pl.BlockSpec((q, nw), lambda g, c: (ch(c), b_off + g)),
                 pl.BlockSpec((q, nw), lambda g, c: (ch(c), c_off + g))]
    col_form = pl.BlockSpec((gps, q, LANES), lambda g, c: (g, ch(c), 0))
    row_form = pl.BlockSpec((gps, 8, q), lambda g, c: (g, 0, ch(c)))
    col_par = pl.BlockSpec((gps, 1, LANES), lambda g, c: (g, 0, 0))
    y_spec = pl.BlockSpec((q, xw), lambda g, c: (ch(c), g))
    st_spec = pl.BlockSpec((gps, None, GW, SSD_D_STATE), lambda g, c: (g, ch(c), 0, 0))
    bc_spec = pl.BlockSpec((q, nw), lambda g, c: (ch(c), g))
    return chunk_grp, col_form, row_form, col_par, y_spec, st_spec, bc_spec


def _ssd_group_views(gi, wide, narrow, stacked):
    xs, ns = pl.ds(gi * GW, GW), pl.ds(gi * SSD_D_STATE, SSD_D_STATE)
    return [r.at[:, xs] for r in wide], [r.at[:, ns] for r in narrow], [r.at[gi] for r in stacked]


def _ssd_fwd(pre, dt_c, cum_c, cum_r, alog_c, dsk_c):
    t = pre.shape[0]
    ng = pre.shape[1] // GC
    q = SSD_CHUNK
    nc = t // q
    gps = SSD_GPS_FWD if ng % SSD_GPS_FWD == 0 else SSD_GPS_BWD
    chunk_grp, col_form, row_form, col_par, y_spec, st_spec, _ = _ssd_specs(nc, False, ng, gps)

    def body(px_ref, pb_ref, pc_ref, dt_ref, cum_ref, cumr_ref, ac_ref, dk_ref, y_ref, sp_ref, st_ref):
        @pl.when(pl.program_id(1) == 0)
        def _():
            st_ref[...] = jnp.zeros_like(st_ref)

        for gi in range(gps):
            (px, y), (pb, pc), rest = _ssd_group_views(
                gi, (px_ref, y_ref), (pb_ref, pc_ref), (dt_ref, cum_ref, cumr_ref, ac_ref, dk_ref, sp_ref, st_ref))
            one_group(px, pb, pc, *rest[:5], y, *rest[5:])

    def one_group(px_ref, pb_ref, pc_ref, dt_ref, cum_ref, cumr_ref, ac_ref, dk_ref, y_ref, sp_ref, st_ref):
        pre_v = jnp.concatenate([px_ref[...], pb_ref[...], pc_ref[...]], axis=1)
        v = _ssd_common(pre_v, dt_ref[...], cum_ref[...], cumr_ref[...], ac_ref[...], False)
        s0 = st_ref[...]
        sp_ref[...] = s0
        r = _dot_nt(v["cm"], s0.astype(BF16))
        y = _expand(v["lam_c"], q) * r + _expand(dk_ref[...], 1) * v["xa"]
        head = _head_of_lane((q, GW), SSD_HEAD_DIM)
        for j in range(SSD_HPG):
            diff = v["cum"][:, j:j + 1] - v["cum_r"][j:j + 1, :]
            w = (v["g"] * jnp.exp(jnp.where(v["tril"], diff, -jnp.inf))).astype(BF16)
            y = y + _dot(w, jnp.where(head == j, v["xdt"], 0.0).astype(BF16))
        y_ref[...] = y
        ds = _dot_tn((v["xdt"] * _expand(v["e_c"], q)).astype(BF16), v["bm"])
        for j in range(SSD_HPG):
            rows = slice(j * SSD_HEAD_DIM, (j + 1) * SSD_HEAD_DIM)
            st_ref[rows, :] = s0[rows, :] * jnp.exp(v["cum_r"][j:j + 1, q - 1:q]) + ds[rows, :]

    return pl.pallas_call(
        body, name="ssd_scan_fwd", grid=(ng // gps, nc),
        in_specs=chunk_grp + [col_form, col_form, row_form, col_par, col_par],
        out_specs=[y_spec, st_spec],
        out_shape=[jax.ShapeDtypeStruct((t, ng * GW), F32), jax.ShapeDtypeStruct((ng, nc, GW, SSD_D_STATE), F32)],
        scratch_shapes=[pltpu.VMEM((gps, GW, SSD_D_STATE), F32)],
        compiler_params=_params("parallel", "arbitrary"))(pre, pre, pre, dt_c, cum_c, cum_r, alog_c, dsk_c)


def _ssd_bwd(dy, pre, states, dt_c, cum_c, cum_r, sgd_c, alog_c, dsk_c):
    t = pre.shape[0]
    ng = pre.shape[1] // GC
    q = SSD_CHUNK
    nc = t // q
    gps = SSD_GPS_BWD
    chunk_grp, col_form, row_form, col_par, y_spec, st_spec, bc_spec = _ssd_specs(nc, True, ng, gps)

    def body(dy_ref, px_ref, pb_ref, pc_ref, sp_ref, dt_ref, cum_ref, cumr_ref, sgd_ref, ac_ref, dk_ref,
             dpx_ref, dpb_ref, dpc_ref, ddt_ref, dbias_ref, dalog_ref, dd_ref, ds_ref):
        @pl.when(pl.program_id(1) == 0)
        def _():
            ds_ref[...] = jnp.zeros_like(ds_ref)

        for gi in range(gps):
            (dy, px, dpx), (pb, pc, dpb, dpc), rest = _ssd_group_views(
                gi, (dy_ref, px_ref, dpx_ref), (pb_ref, pc_ref, dpb_ref, dpc_ref),
                (sp_ref, dt_ref, cum_ref, cumr_ref, sgd_ref, ac_ref, dk_ref, ddt_ref, dbias_ref, dalog_ref, dd_ref,
                 ds_ref))
            one_group(dy, px, pb, pc, *rest[:7], dpx, dpb, dpc, *rest[7:])

    def one_group(dy_ref, px_ref, pb_ref, pc_ref, sp_ref, dt_ref, cum_ref, cumr_ref, sgd_ref, ac_ref, dk_ref,
                  dpx_ref, dpb_ref, dpc_ref, ddt_ref, dbias_ref, dalog_ref, dd_ref, ds_ref):
        first = pl.program_id(1) == 0
        pre_v = jnp.concatenate([px_ref[...], pb_ref[...], pc_ref[...]], axis=1)
        v = _ssd_common(pre_v, dt_ref[...], cum_ref[...], cumr_ref[...], ac_ref[...], True)
        xa, bm, cm, xdt, cum, cum_r = v["xa"], v["bm"], v["cm"], v["xdt"], v["cum"], v["cum_r"]
        xdt_b = xdt.astype(BF16)
        dy_v = dy_ref[...]
        s0 = sp_ref[...]
        ds1 = ds_ref[...]
        s0b, ds1b = s0.astype(BF16), ds1.astype(BF16)
        head = _head_of_lane((q, GW), SSD_HEAD_DIM)
        lane = lax.broadcasted_iota(jnp.int32, (q, LANES), 1)
        lane1 = lax.broadcasted_iota(jnp.int32, (1, LANES), 1)
        lam_x = _expand(v["lam_c"], q, True)
        e_x = _expand(v["e_c"], q, True)

        dxa = _expand(dk_ref[...], 1) * dy_v
        dd = _contract(jnp.sum(dy_v * xa, axis=0, keepdims=True), 1)
        r = _dot_nt(cm, s0b)
        dcum = _contract(dy_v * r * lam_x, q, True)
        drb = (lam_x * dy_v).astype(BF16)
        dc = _dot(drb, s0b)
        ds0 = _dot_tn(drb, cm)
        extra = jnp.zeros((1, LANES), F32)
        for j in range(SSD_HPG):
            rows = slice(j * SSD_HEAD_DIM, (j + 1) * SSD_HEAD_DIM)
            lam_last = jnp.exp(cum_r[j:j + 1, q - 1:q])
            ds_ref[rows, :] = ds0[rows, :] + lam_last * ds1[rows, :]
            tot = jnp.sum(jnp.sum(ds1[rows, :] * s0[rows, :], axis=1, keepdims=True), axis=0, keepdims=True)
            extra = jnp.where(lane1 == j, lam_last * tot, extra)
        dv = _dot_nt(bm, ds1b)
        db = _dot((xdt * e_x).astype(BF16), ds1b)
        dxdt = e_x * dv
        dee = _contract(dv * xdt, q, True) * v["e_c"]
        dcum = dcum - dee
        extra = extra + jnp.sum(dee, axis=0, keepdims=True)
        dg = jnp.zeros((q, q), F32)
        col_sums = jnp.zeros((q, q), F32)
        for j in range(SSD_HPG):
            diff = cum[:, j:j + 1] - cum_r[j:j + 1, :]
            el = jnp.exp(jnp.where(v["tril"], diff, -jnp.inf))
            gl = v["g"] * el
            dym = jnp.where(head == j, dy_v, 0.0).astype(BF16)
            dwm = _dot_nt(dym, xdt_b)
            dxdt = dxdt + _dot_tn(gl.astype(BF16), dym)
            z = dwm * gl
            dcum = jnp.where(lane == j, dcum + jnp.sum(z, axis=1, keepdims=True), dcum)
            col_sums = jnp.where(v["row"] == j, jnp.sum(z, axis=0, keepdims=True), col_sums)
            dg = dg + dwm * el
        dcum = dcum - col_sums.T
        dgb = dg.astype(BF16)
        dc = dc + _dot(dgb, bm)
        db = db + _dot_tn(dgb, cm)
        da = _select_dot(dcum, (v["row"] <= v["col"]).astype(BF16), True) + extra
        ddt = _contract(dxdt * xa, q, True) + v["a_c"] * da
        dalog = jnp.sum(v["dt"] * da, axis=0, keepdims=True) * v["a_c"]
        dxa = dxa + v["dt_x"] * dxdt
        ddt_raw = jnp.where(lane < SSD_HPG, ddt * sgd_ref[...], 0.0)
        sgrad = _silu_grad(pre_v, v["sg"])
        dpx_ref[...] = dxa * sgrad[:, :GW]
        dpb_ref[...] = db * sgrad[:, GW:GW + SSD_D_STATE]
        dpc_ref[...] = dc * sgrad[:, GW + SSD_D_STATE:]
        ddt_ref[...] = ddt_raw
        _acc(dbias_ref, jnp.sum(ddt_raw, axis=0, keepdims=True), first)
        _acc(dalog_ref, jnp.where(lane1 < SSD_HPG, dalog, 0.0), first)
        _acc(dd_ref, dd, first)

    return pl.pallas_call(
        body, name="ssd_scan_bwd", grid=(ng // gps, nc),
        in_specs=[y_spec] + chunk_grp + [st_spec, col_form, col_form, row_form, col_form, col_par, col_par],
        out_specs=[y_spec, bc_spec, bc_spec, col_form, col_par, col_par, col_par],
        out_shape=[jax.ShapeDtypeStruct((t, ng * GW), F32), jax.ShapeDtypeStruct((t, ng * SSD_D_STATE), F32),
                   jax.ShapeDtypeStruct((t, ng * SSD_D_STATE), F32), jax.ShapeDtypeStruct((ng, t, LANES), F32),
                   jax.ShapeDtypeStruct((ng, 1, LANES), F32), jax.ShapeDtypeStruct((ng, 1, LANES), F32),
                   jax.ShapeDtypeStruct((ng, 1, LANES), F32)],
        scratch_shapes=[pltpu.VMEM((gps, GW, SSD_D_STATE), F32)],
        compiler_params=_params("parallel", "arbitrary"))(dy, pre, pre, pre, states, dt_c, cum_c, cum_r, sgd_c, alog_c,
                                                           dsk_c)


def _gate_norm_fwd(y, zx, norm_w):
    t, di = y.shape
    tr = _tile(t, 512, 8)
    ng = di // GW

    def body(y_ref, z_ref, w_ref, o_ref):
        z = z_ref[...]
        gate = y_ref[...] * (z * _sigmoid(z))
        w = w_ref[...]
        for g in range(ng):
            cols = slice(g * GW, (g + 1) * GW)
            gs = gate[:, cols]
            r = lax.rsqrt(jnp.mean(gs * gs, axis=-1, keepdims=True) + NORM_EPS)
            o_ref[:, cols] = (gs * r * w[:, cols]).astype(BF16)

    row = pl.BlockSpec((tr, di), lambda i: (i, 0))
    return pl.pallas_call(body, name="ssd_gate_norm_fwd", grid=(t // tr,),
                          in_specs=[row, row, pl.BlockSpec((1, di), lambda i: (0, 0))], out_specs=row,
                          out_shape=jax.ShapeDtypeStruct((t, di), BF16), compiler_params=_params("parallel"))(
                              y, zx, norm_w)


def _gate_norm_bwd(dyn, y, zx, norm_w, after):
    t, di = y.shape
    tr = _tile(t, 256, 8)
    ng = di // GW

    def body(d_ref, y_ref, z_ref, w_ref, after_ref, dy_ref, dz_ref, dw_ref):
        z = z_ref[...]
        yv = y_ref[...]
        sg = _sigmoid(z)
        sz = z * sg
        gate = yv * sz
        w = w_ref[...]
        d = d_ref[...]
        dsz = _silu_grad(z, sg)
        dws = []
        for g in range(ng):
            cols = slice(g * GW, (g + 1) * GW)
            dg, dwr = _rms_bwd(gate[:, cols], w[:, cols], d[:, cols])
            dy_ref[:, cols] = dg * sz[:, cols]
            dz_ref[:, cols] = (dg * yv[:, cols] * dsz[:, cols]).astype(BF16)
            dws.append(jnp.sum(dwr, axis=0, keepdims=True))
        first = pl.program_id(0) == 0
        for g in range(ng):
            cols = slice(g * GW, (g + 1) * GW)

            @pl.when(first)
            def _():
                dw_ref[:, cols] = dws[g]

            @pl.when(jnp.logical_not(first))
            def _():
                dw_ref[:, cols] += dws[g]

    row = pl.BlockSpec((tr, di), lambda i: (i, 0))
    vec = pl.BlockSpec((1, di), lambda i: (0, 0))
    return pl.pallas_call(body, name="ssd_gate_norm_bwd", grid=(t // tr,),
                          in_specs=[row, row, row, vec, pl.BlockSpec((8, LANES), lambda i: (0, 0))],
                          out_specs=[row, row, vec],
                          out_shape=[jax.ShapeDtypeStruct((t, di), F32), jax.ShapeDtypeStruct((t, di), BF16),
                                     jax.ShapeDtypeStruct((1, di), F32)],
                          compiler_params=_params("arbitrary"))(dyn, y, zx, norm_w, after)


def _attn_mask_t(n):
    w = ATTN_WINDOW
    kpos = lax.broadcasted_iota(jnp.int32, (2 * w, ATTN_REP * w), 0)
    qpos = lax.broadcasted_iota(jnp.int32, (2 * w, ATTN_REP * w), 1) % w + w
    rel = qpos - kpos
    return (rel >= 0) & (rel < w) & jnp.logical_not((n == 0) & (kpos < w))


def _attn_probs_t(qts, ktb, mask, sink):
    s = _dot_tn(ktb, qts) * (ATTN_HEAD_DIM ** -0.5)
    s = jnp.where(mask, s, -jnp.inf)
    m = jnp.maximum(jnp.max(s, axis=0, keepdims=True), sink)
    e = jnp.exp(s - m)
    es = jnp.exp(sink - m)
    inv = 1.0 / (jnp.sum(e, axis=0, keepdims=True) + es)
    return e * inv, es * inv


def _attn_blocks_t(kv, q_ref, kc_ref, vc_ref, kp_ref, vp_ref):
    hd = ATTN_HEAD_DIM
    rows = slice(kv * hd, (kv + 1) * hd)
    ktb = jnp.concatenate([kp_ref[rows, :], kc_ref[rows, :]], axis=1)
    vtb = jnp.concatenate([vp_ref[rows, :], vc_ref[rows, :]], axis=1)
    qts = jnp.concatenate([q_ref[(kv * ATTN_REP + r) * hd:(kv * ATTN_REP + r + 1) * hd, :]
                           for r in range(ATTN_REP)], axis=1)
    return qts, ktb, vtb


def _attn_specs_t(nb, cur, prev):
    w, hd = ATTN_WINDOW, ATTN_HEAD_DIM
    kd = ATTN_N_KV * hd
    qd = ATTN_REP * kd
    return [pl.BlockSpec((qd, w), lambda n: (0, cur(n))),
            pl.BlockSpec((kd, w), lambda n: (ATTN_REP, cur(n))),
            pl.BlockSpec((kd, w), lambda n: (ATTN_REP + 1, cur(n))),
            pl.BlockSpec((kd, w), lambda n: (ATTN_REP, prev(n))),
            pl.BlockSpec((kd, w), lambda n: (ATTN_REP + 1, prev(n)))]


def _attn_fwd_t(qkv_t, sinks_rep):
    t = qkv_t.shape[1]
    w, hd = ATTN_WINDOW, ATTN_HEAD_DIM
    qd = ATTN_N_KV * ATTN_REP * hd
    nb = t // w

    def body(q_ref, kc_ref, vc_ref, kp_ref, vp_ref, s_ref, o_ref):
        mask = _attn_mask_t(pl.program_id(0))
        for kv in range(ATTN_N_KV):
            qts, ktb, vtb = _attn_blocks_t(kv, q_ref, kc_ref, vc_ref, kp_ref, vp_ref)
            p, _ = _attn_probs_t(qts, ktb, mask, s_ref[kv])
            ots = _dot(vtb, p.astype(BF16))
            for r in range(ATTN_REP):
                h = kv * ATTN_REP + r
                o_ref[h * hd:(h + 1) * hd, :] = ots[:, r * w:(r + 1) * w].astype(BF16)

    return pl.pallas_call(
        body, name="attn_fwd", grid=(nb,),
        in_specs=_attn_specs_t(nb, lambda n: n, lambda n: jnp.maximum(n - 1, 0)) + [
            pl.BlockSpec(sinks_rep.shape, lambda n: (0, 0, 0))],
        out_specs=pl.BlockSpec((qd, w), lambda n: (0, n)),
        out_shape=jax.ShapeDtypeStruct((qd, t), BF16),
        compiler_params=_params("parallel"))(qkv_t, qkv_t, qkv_t, qkv_t, qkv_t, sinks_rep)


def _attn_bwd_t(qkv_t, do_t, sinks_rep):
    t = qkv_t.shape[1]
    w, hd = ATTN_WINDOW, ATTN_HEAD_DIM
    kd = ATTN_N_KV * hd
    qd = ATTN_REP * kd
    nq = ATTN_N_KV * ATTN_REP
    nb = t // w
    rows_all = qd + 2 * kd

    def body(q_ref, kc_ref, vc_ref, kp_ref, vp_ref, do_ref, s_ref, dqkv_ref, bsum_ref, dsk_ref,
             carry_ref, new_ref, bacc_ref, sacc_ref):
        n = pl.program_id(0)

        @pl.when(n == 0)
        def _():
            carry_ref[...] = jnp.zeros_like(carry_ref)
            bacc_ref[...] = jnp.zeros_like(bacc_ref)
            sacc_ref[...] = jnp.zeros_like(sacc_ref)

        @pl.when(n < nb)
        def _():
            mask = _attn_mask_t(n)
            for kv in range(ATTN_N_KV):
                qts, ktb, vtb = _attn_blocks_t(kv, q_ref, kc_ref, vc_ref, kp_ref, vp_ref)
                dots = jnp.concatenate([do_ref[(kv * ATTN_REP + r) * hd:(kv * ATTN_REP + r + 1) * hd, :]
                                        for r in range(ATTN_REP)], axis=1)
                p, ps = _attn_probs_t(qts, ktb, mask, s_ref[kv])
                dpt = _dot_tn(vtb, dots)
                delta = jnp.sum(p * dpt, axis=0, keepdims=True)
                dst = (p * (dpt - delta) * (hd ** -0.5)).astype(BF16)
                dqts = _dot(ktb, dst)
                for r in range(ATTN_REP):
                    h = kv * ATTN_REP + r
                    new_ref[h * hd:(h + 1) * hd, :] = dqts[:, r * w:(r + 1) * w]
                dktb = _dot_nt(qts, dst)
                dvtb = _dot_nt(dots, p.astype(BF16))
                krows = slice(qd + kv * hd, qd + (kv + 1) * hd)
                vrows = slice(qd + kd + kv * hd, qd + kd + (kv + 1) * hd)
                carry_ref[krows, :] += dktb[:, :w]
                carry_ref[vrows, :] += dvtb[:, :w]
                new_ref[krows, :] = dktb[:, w:]
                new_ref[vrows, :] = dvtb[:, w:]
                sacc_ref[kv] += -(ps * delta)

        @pl.when(n >= 1)
        def _():
            done = carry_ref[...]
            dqkv_ref[...] = done.astype(BF16)
            bacc_ref[...] += done

        @pl.when(n < nb)
        def _():
            carry_ref[...] = new_ref[...]

        @pl.when(n == nb)
        def _():
            bsum_ref[...] = jnp.sum(bacc_ref[...], axis=1, keepdims=True)
            lane = lax.broadcasted_iota(jnp.int32, (1, nq), 1)
            dsk = jnp.zeros((1, nq), F32)
            for kv in range(ATTN_N_KV):
                acc = sacc_ref[kv]
                for r in range(ATTN_REP):
                    tot = jnp.sum(acc[:, r * w:(r + 1) * w], axis=1, keepdims=True)
                    dsk = jnp.where(lane == kv * ATTN_REP + r, tot, dsk)
            dsk_ref[...] = dsk

    cur = lambda n: jnp.minimum(n, nb - 1)
    prev = lambda n: jnp.maximum(jnp.minimum(n, nb - 1) - 1, 0)
    return pl.pallas_call(
        body, name="attn_bwd", grid=(nb + 1,),
        in_specs=_attn_specs_t(nb, cur, prev) + [pl.BlockSpec((qd, w), lambda n: (0, cur(n))),
                                                 pl.BlockSpec(sinks_rep.shape, lambda n: (0, 0, 0))],
        out_specs=[pl.BlockSpec((rows_all, w), lambda n: (0, jnp.maximum(n - 1, 0))),
                   pl.BlockSpec((rows_all, 1), lambda n: (0, 0)),
                   pl.BlockSpec((1, nq), lambda n: (0, 0))],
        out_shape=[jax.ShapeDtypeStruct((rows_all, t), BF16), jax.ShapeDtypeStruct((rows_all, 1), F32),
                   jax.ShapeDtypeStruct((1, nq), F32)],
        scratch_shapes=[pltpu.VMEM((rows_all, w), F32), pltpu.VMEM((rows_all, w), F32),
                        pltpu.VMEM((rows_all, w), F32), pltpu.VMEM(sinks_rep.shape, F32)],
        compiler_params=_params("arbitrary"))(qkv_t, qkv_t, qkv_t, qkv_t, qkv_t, do_t, sinks_rep)


HBM_SPEC = pl.BlockSpec(memory_space=pl.ANY)
HBM_ONLY = pl.BlockSpec(memory_space=pltpu.HBM)


def _comm_call(name, body, ins, out_shapes, n_sems):
    return pl.pallas_call(
        body, name=name, in_specs=[HBM_SPEC] * len(ins), out_specs=[HBM_SPEC] * len(out_shapes),
        out_shape=out_shapes,
        scratch_shapes=[pltpu.SemaphoreType.DMA((s,)) for s in n_sems])(*ins)


def _all_gather(name, shards, after):
    n = len(shards)
    na = len(after)

    def body(*refs):
        x_refs, out_refs = refs[:n], refs[n + na:2 * n + na]
        send_sems, recv_sems, local_sems = refs[2 * n + na:]
        x, y, c = lax.axis_index("x"), lax.axis_index("y"), lax.axis_index("c")
        me, sibling = (x, y, c), (x, y, 1 - c)
        chips = [(1 - x, y), (x, 1 - y), (1 - x, 1 - y)]

        def slot(i, px, py, pc):
            return out_refs[i].at[4 * px + 2 * py + pc]

        def copy(k, i, block, to, src=None):
            return pltpu.make_async_remote_copy(
                src_ref=slot(i, *block) if src is None else src, dst_ref=slot(i, *block),
                send_sem=send_sems.at[k * n + i], recv_sem=recv_sems.at[k * n + i], device_id=to,
                device_id_type=MESH)

        mine = [pltpu.make_async_copy(x_refs[i], slot(i, *me), local_sems.at[i]) for i in range(n)]
        first = []
        for i in range(n):
            mine[i].start()
            first.append(copy(0, i, me, sibling, src=x_refs[i]))
            first += [copy(1 + j, i, me, (*chip, c), src=x_refs[i]) for j, chip in enumerate(chips)]
        for cp in first:
            cp.start()
        passed = []
        for i in range(n):
            for j, chip in enumerate(chips):
                copy(1 + j, i, (*chip, c), me).wait_recv()
                passed.append(copy(4 + j, i, (*chip, c), sibling))
                passed[-1].start()
        for i in range(n):
            copy(0, i, sibling, me).wait_recv()
            for j, chip in enumerate(chips):
                copy(4 + j, i, (*chip, 1 - c), me).wait_recv()
        for cp in first + passed:
            cp.wait_send()
        for cp in mine:
            cp.wait()

    outs = [jax.ShapeDtypeStruct((N_DEV,) + s.shape, s.dtype) for s in shards]
    return _comm_call(name, body, list(shards) + list(after), outs, (7 * n, 7 * n, n))


SEM_SPEC = pl.BlockSpec(memory_space=pltpu.SEMAPHORE)
SPLIT_COPY_EFFECT = pltpu.SideEffectType.DATAFLOW_SIDE_EFFECTING


def _in_hbm(a):
    return pltpu.with_memory_space_constraint(a, pltpu.HBM)


def _split_start(name, body, srcs, lands, n_sems):
    n = len(srcs)
    bufs = [_in_hbm(a) for a in list(srcs) + list(lands)]
    outs = pl.pallas_call(
        body, name=name,
        out_shape=(pltpu.SemaphoreType.DMA((n_sems,)), pltpu.SemaphoreType.DMA((n_sems,)),
                   *[pltpu.HBM(a.shape, a.dtype) for a in bufs], jax.ShapeDtypeStruct((8, LANES), F32)),
        in_specs=[HBM_ONLY] * (2 * n),
        out_specs=(SEM_SPEC, SEM_SPEC, *[HBM_ONLY] * (2 * n), pl.BlockSpec(memory_space=pltpu.VMEM)),
        input_output_aliases={i: 2 + i for i in range(2 * n)},
        compiler_params=pltpu.CompilerParams(has_side_effects=SPLIT_COPY_EFFECT))(*bufs)
    return outs[0], outs[1], list(outs[2:2 + n]), list(outs[2 + n:2 + 2 * n]), outs[-1]


def _split_wait(name, body, send_sems, recv_sems, srcs, lands, after):
    n = len(srcs)
    outs = pl.pallas_call(
        body, name=name,
        out_shape=[pltpu.HBM(a.shape, a.dtype) for a in list(srcs) + list(lands)],
        in_specs=[HBM_ONLY] * (2 * n) + [SEM_SPEC, SEM_SPEC, HBM_SPEC],
        out_specs=[HBM_ONLY] * (2 * n),
        input_output_aliases={i: i for i in range(2 * n)},
        compiler_params=pltpu.CompilerParams(has_side_effects=SPLIT_COPY_EFFECT))(
            *srcs, *lands, send_sems, recv_sems, after)
    return list(outs[:n]), list(outs[n:])


N_PEERS = N_DEV - 1


def _gather_peers():
    x, y, c = lax.axis_index("x"), lax.axis_index("y"), lax.axis_index("c")
    flips = [(fx, fy, fc) for fx in (0, 1) for fy in (0, 1) for fc in (0, 1) if fx or fy or fc]
    return [(1 - x if fx else x, 1 - y if fy else y, 1 - c if fc else c) for fx, fy, fc in flips]


def _block_id(dev):
    return 4 * dev[0] + 2 * dev[1] + dev[2]


def _landing_block(land_ref, shard_shape, side_by_side, dev):
    if not side_by_side:
        return land_ref.at[_block_id(dev)]
    cols = shard_shape[1]
    return land_ref.at[:, pl.ds(pl.multiple_of(_block_id(dev) * cols, LANES), cols)]


def _gather_start(name, shards, side_by_side):
    n = len(shards)

    def body(*refs):
        x_refs, land_refs = refs[:n], refs[n:2 * n]
        send_sems, recv_sems, token = refs[2 * n], refs[2 * n + 1], refs[-1]
        me = (lax.axis_index("x"), lax.axis_index("y"), lax.axis_index("c"))
        for i in range(n):
            for k, peer in enumerate(_gather_peers()):
                pltpu.make_async_remote_copy(
                    src_ref=x_refs[i], dst_ref=_landing_block(land_refs[i], shards[i].shape, side_by_side[i], me),
                    send_sem=send_sems.at[N_PEERS * i + k], recv_sem=recv_sems.at[N_PEERS * i + k],
                    device_id=peer, device_id_type=MESH).start()
            pltpu.make_async_copy(x_refs[i], _landing_block(land_refs[i], shards[i].shape, side_by_side[i], me),
                                  send_sems.at[N_PEERS * n + i]).start()
        token[...] = jnp.zeros_like(token)

    lands = [lax.empty((s.shape[0], N_DEV * s.shape[1]) if wide else (N_DEV,) + s.shape, s.dtype)
             for s, wide in zip(shards, side_by_side)]
    return _split_start(name, body, shards, lands, (N_PEERS + 1) * n)


def _gather_wait(name, send_sems, recv_sems, first, n_all, shards, lands, side_by_side, after):
    n = len(shards)

    def body(*refs):
        x_refs, land_refs = refs[:n], refs[n:2 * n]
        send_sems, recv_sems = refs[2 * n], refs[2 * n + 1]
        me = (lax.axis_index("x"), lax.axis_index("y"), lax.axis_index("c"))
        for i in range(n):
            pltpu.make_async_copy(x_refs[i], _landing_block(land_refs[i], shards[i].shape, side_by_side[i], me),
                                  send_sems.at[N_PEERS * n_all + first + i]).wait()
            for k, peer in enumerate(_gather_peers()):
                cp = pltpu.make_async_remote_copy(
                    src_ref=x_refs[i], dst_ref=_landing_block(land_refs[i], shards[i].shape, side_by_side[i], peer),
                    send_sem=send_sems.at[N_PEERS * (first + i) + k],
                    recv_sem=recv_sems.at[N_PEERS * (first + i) + k],
                    device_id=peer, device_id_type=MESH)
                cp.wait_send()
                cp.wait_recv()

    return _split_wait(name, body, send_sems, recv_sems, shards, lands, after)


def _scatter_start(name, blocks):
    n = len(blocks)

    def body(*refs):
        b_refs, land_refs = refs[:n], refs[n:2 * n]
        send_sems, recv_sems, token = refs[2 * n], refs[2 * n + 1], refs[-1]
        me = (lax.axis_index("x"), lax.axis_index("y"), lax.axis_index("c"))
        for i in range(n):
            for k, peer in enumerate(_gather_peers()):
                pltpu.make_async_remote_copy(
                    src_ref=b_refs[i].at[_block_id(peer)], dst_ref=land_refs[i].at[_block_id(me)],
                    send_sem=send_sems.at[N_PEERS * i + k], recv_sem=recv_sems.at[N_PEERS * i + k],
                    device_id=peer, device_id_type=MESH).start()
            pltpu.make_async_copy(b_refs[i].at[_block_id(me)], land_refs[i].at[_block_id(me)],
                                  send_sems.at[N_PEERS * n + i]).start()
        token[...] = jnp.zeros_like(token)

    lands = [lax.empty(b.shape, b.dtype) for b in blocks]
    return _split_start(name, body, blocks, lands, (N_PEERS + 1) * n)


def _scatter_wait(name, send_sems, recv_sems, blocks, lands, after):
    n = len(blocks)

    def body(*refs):
        b_refs, land_refs = refs[:n], refs[n:2 * n]
        send_sems, recv_sems = refs[2 * n], refs[2 * n + 1]
        me = (lax.axis_index("x"), lax.axis_index("y"), lax.axis_index("c"))
        for i in range(n):
            pltpu.make_async_copy(b_refs[i].at[_block_id(me)], land_refs[i].at[_block_id(me)],
                                  send_sems.at[N_PEERS * n + i]).wait()
            for k, peer in enumerate(_gather_peers()):
                cp = pltpu.make_async_remote_copy(
                    src_ref=b_refs[i].at[_block_id(peer)], dst_ref=land_refs[i].at[_block_id(peer)],
                    send_sem=send_sems.at[N_PEERS * i + k], recv_sem=recv_sems.at[N_PEERS * i + k],
                    device_id=peer, device_id_type=MESH)
                cp.wait_send()
                cp.wait_recv()

    return _split_wait(name, body, send_sems, recv_sems, blocks, lands, after)


def _adamw(w, g, m, v):
    m = ADAM_B1 * m + (1.0 - ADAM_B1) * g
    v = ADAM_B2 * v + (1.0 - ADAM_B2) * (g * g)
    m_hat = m / (1.0 - ADAM_B1 ** ADAM_STEP)
    v_hat = v / (1.0 - ADAM_B2 ** ADAM_STEP)
    delta = -ADAM_LR * (m_hat / (jnp.sqrt(v_hat) + ADAM_EPS) + ADAM_WD * w)
    return delta, m, v


def _adamw_tiles(r, c_):
    tr = _tile(r, 256, 16)
    return (tr, c_) if tr < r or r <= 256 else (r, _tile(c_, 256))


def _sum_parts(part):
    g = part[0].astype(F32)
    for k in range(1, part.shape[0]):
        g = g + part[k].astype(F32)
    return g


def _sum_adamw(name, parts, w, m, v):
    r, c_ = w.shape
    tr, tc = _adamw_tiles(r, c_)

    def body(p_ref, w_ref, m_ref, v_ref, g_ref, d_ref, nm_ref, nv_ref):
        g = _sum_parts(p_ref)
        g_ref[...] = g
        d_ref[...], nm_ref[...], nv_ref[...] = _adamw(w_ref[...], g, m_ref[...], v_ref[...])

    tile = pl.BlockSpec((tr, tc), lambda i, j: (i, j))
    return pl.pallas_call(body, name=name, grid=(r // tr, c_ // tc),
                          in_specs=[pl.BlockSpec((parts.shape[0], tr, tc), lambda i, j: (0, i, j)), tile, tile, tile],
                          out_specs=[tile] * 4, out_shape=[jax.ShapeDtypeStruct((r, c_), F32)] * 4,
                          compiler_params=_params("parallel", "parallel"))(parts, w, m, v)


def _sum_adamw_layers(name, parts, w, m, v):
    n_layers, r, c_ = w.shape
    tr = _tile(r, 256, 16)

    def body(*refs):
        p_refs = refs[:n_layers]
        w_ref, m_ref, v_ref, g_ref, d_ref, nm_ref, nv_ref = refs[n_layers:]
        layer = pl.program_id(0)
        g = _sum_parts(p_refs[0])
        for li in range(1, n_layers):
            g = jnp.where(layer == li, _sum_parts(p_refs[li]), g)
        g_ref[...] = g
        d_ref[...], nm_ref[...], nv_ref[...] = _adamw(w_ref[...], g, m_ref[...], v_ref[...])

    row = pl.BlockSpec((None, tr, c_), lambda l, i: (l, i, 0))
    specs = [pl.BlockSpec((p.shape[0], tr, c_), lambda l, i, li=li: (0, jnp.where(l == li, i, 0), 0))
             for li, p in enumerate(parts)]
    return pl.pallas_call(body, name=name, grid=(n_layers, r // tr), in_specs=specs + [row, row, row],
                          out_specs=[row] * 4, out_shape=[jax.ShapeDtypeStruct(w.shape, F32)] * 4,
                          compiler_params=_params("parallel", "parallel"))(*parts, w, m, v)


def _pack_rows(flat, n_rows, cols):
    pad = n_rows * cols - flat.shape[-1]
    flat = jnp.pad(flat, [(0, 0)] * (flat.ndim - 1) + [(0, pad)])
    return flat.reshape(flat.shape[:-1] + (n_rows, cols))


def _cols_split(full):
    c = full.shape[1] // N_DEV
    return jnp.stack([full[:, d * c:(d + 1) * c] for d in range(N_DEV)])


def _rows_join(blocks):
    return blocks.reshape(N_DEV * blocks.shape[1], blocks.shape[2])


def _rows_split(full):
    return full.reshape(N_DEV, full.shape[0] // N_DEV, full.shape[1])


def _heads_col(v, ng):
    return jnp.pad(v.reshape(ng, 1, SSD_HPG), ((0, 0), (0, 0), (0, LANES - SSD_HPG)))


MATRIX_ITEMS = ("w_in", "w_out", "up0", "down0", "w_qkv", "w_o", "up1", "down1")
VECTOR_ITEMS = ("conv_w", "b_qkv", "b_o")
ITEMS = MATRIX_ITEMS + VECTOR_ITEMS
GATHER_STAGES = (("w_in", "conv_w"), ("w_out", "up0", "down0"), ("w_qkv", "b_qkv", "w_o", "b_o", "up1", "down1"))
SIDE_BY_SIDE = ("conv_w", "up0", "up1", "b_o")


def _items(tree, prefix=""):
    g = lambda k: tree[prefix + k]
    return {"w_in": g("ssd_w_in")[0].T, "w_out": g("ssd_w_out")[0], "w_qkv": g("attn_w_qkv")[0].T,
            "w_o": g("attn_w_o")[0], "up0": g("mlp_w_up")[0], "up1": g("mlp_w_up")[1],
            "down0": g("mlp_w_down")[0], "down1": g("mlp_w_down")[1], "conv_w": g("ssd_conv_w")[0],
            "b_qkv": g("attn_b_qkv"), "b_o": g("attn_b_o")}


REPLICATED = ("ssd_conv_b", "ssd_dt_bias", "ssd_a_log", "ssd_d", "ssd_norm_w", "attn_sinks", "mix_pre_norm",
              "mix_post_norm", "ffn_pre_norm", "ffn_post_norm")
WEIGHTS = ("ssd_w_in", "ssd_conv_w", "ssd_conv_b", "ssd_dt_bias", "ssd_a_log", "ssd_d", "ssd_norm_w", "ssd_w_out",
           "attn_w_qkv", "attn_b_qkv", "attn_sinks", "attn_w_o", "attn_b_o", "mlp_w_up", "mlp_w_down",
           "mix_pre_norm", "mix_post_norm", "ffn_pre_norm", "ffn_post_norm")


def _forward_backward(x, target, rep, token, weights_of_stage, reduce_grads):
    t, d = x.shape
    ng = rep["ssd_norm_w"].shape[1] // GW
    di = ng * GW
    n_xbc = ng * GC
    nh = ng * SSD_HPG
    grads, blocks = {}, {}
    w_up, w_down = [None, None], [None, None]
    sinks_rep = jnp.repeat(rep["attn_sinks"].reshape(ATTN_N_KV, ATTN_REP, 1), ATTN_WINDOW, axis=2).reshape(
        ATTN_N_KV, 1, ATTN_REP * ATTN_WINDOW)
    conv_b = rep["ssd_conv_b"]
    gn = ng * SSD_D_STATE
    parts = ((0, di), (di, di), (2 * di, gn), (2 * di + gn, gn), (di + n_xbc, nh))
    alog_c, dsk_c = (_heads_col(rep[k], ng) for k in ("ssd_a_log", "ssd_d"))
    bias_l, alog_l = (jnp.pad(rep[k], ((0, 0), (0, LANES - nh))) for k in ("ssd_dt_bias", "ssd_a_log"))
    norm = {k: rep[k] for k in ("mix_pre_norm", "mix_post_norm", "ffn_pre_norm", "ffn_post_norm")}

    def nrow(name, i):
        return norm[name][i:i + 1]

    def mlp_fwd(i, u2):
        p = _mm(f"mlp{i}_up", [u2], [w_up[i]], "nn", tm=1024, tn=1024, out_dtypes=(BF16,),
                epilogue=lambda acc: (jnp.square(jnp.maximum(acc, 0.0)),))
        f = _mm(f"mlp{i}_down", [p], [w_down[i]], "nn", tm=512, tn=1024)
        return p, f

    def mlp_bwd(i, df, u2, p):
        da = _mm(f"mlp{i}_dact", [df], [w_down[i]], "nt", tm=1024, tn=1024, out_dtypes=(BF16,),
                 tiles=(p,), epilogue=lambda acc, pv: (acc * (2.0 * jnp.sqrt(pv.astype(F32))),))
        blocks[f"down{i}"] = _rows_split(_mm(f"mlp{i}_dwdown", [p], [df], "tn", tm=512, tn=1024,
                                             out_dtypes=(PAYLOAD,)))
        blocks[f"up{i}"] = _mm(f"mlp{i}_dwup", [u2], [da], "tn", tm=1024, tn=da.shape[1] // N_DEV,
                               out_dtypes=(PAYLOAD,), col_blocks=True)
        return _mm(f"mlp{i}_dx", [da], [w_up[i]], "nt", tm=512, tn=1024)

    u0 = _prenorm("l0_prenorm", x, nrow("mix_pre_norm", 0), token)
    got = weights_of_stage(0, u0)
    w_in_t = _rows_join(got["w_in"])
    w_dt_t = jnp.pad(w_in_t[di + n_xbc:], ((0, LANES - nh), (0, 0)))
    conv_w = got["conv_w"]
    zx = _mm("ssd_in_proj", [u0], [w_in_t], "nt", tm=1024, tn=1024, n_use=di + n_xbc)
    zdt = _mm("ssd_dt_proj", [u0], [w_dt_t], "nt", tm=1024, tn=LANES)
    pre = _conv_fwd(zx, di, n_xbc, conv_w, conv_b)
    dt_c, cum_c, cum_r, sgd_c = _ssd_dt_prep(zdt, bias_l, alog_l, ng)
    y, states = _ssd_fwd(pre, dt_c, cum_c, cum_r, alog_c, dsk_c)
    yn = _gate_norm_fwd(y, zx, rep["ssd_norm_w"])
    got = weights_of_stage(1, yn)
    w_out = _rows_join(got["w_out"])
    w_up[0], w_down[0] = got["up0"], _rows_join(got["down0"])
    mix0 = _mm("ssd_out_proj", [yn], [w_out], "nn", tm=1024, tn=1024)
    h1, u0f = _post_pre("l0_mid", x, mix0, nrow("mix_post_norm", 0), nrow("ffn_pre_norm", 0))
    p0, f0 = mlp_fwd(0, u0f)
    h2, u1 = _post_pre("l1_in", h1, f0, nrow("ffn_post_norm", 0), nrow("mix_pre_norm", 1))
    got = weights_of_stage(2, u1)
    w_qkv_t = _rows_join(got["w_qkv"])
    w_o = _rows_join(got["w_o"])
    b_qkv_col = got["b_qkv"].reshape(-1, 1)
    b_o = got["b_o"]
    w_up[1], w_down[1] = got["up1"], _rows_join(got["down1"])
    qkv_t = _mm("attn_qkv_proj", [w_qkv_t], [u1], "nt", tm=768, tn=1024, out_dtypes=(BF16,), cols=(b_qkv_col,),
                epilogue=lambda acc, b: (acc + b,))
    ao_t = _attn_fwd_t(qkv_t, sinks_rep)
    mix1 = _mm("attn_out_proj", [ao_t], [w_o], "tn", tm=1024, tn=1024, rows=(b_o,),
               epilogue=lambda acc, b: (acc + b,))
    h3, u1f = _post_pre("l1_mid", h2, mix1, nrow("mix_post_norm", 1), nrow("ffn_pre_norm", 1))
    p1, f1 = mlp_fwd(1, u1f)
    dh, loss_row = _final_loss("loss", h3, f1, nrow("ffn_post_norm", 1), target)

    g_norm = {k: [None, None] for k in norm}
    df1, g_norm["ffn_post_norm"][1], _ = _norm_bwd("l1_ffn_post_bwd", dh, post=(f1, nrow("ffn_post_norm", 1)))
    du = mlp_bwd(1, df1, u1f, p1)
    sent = reduce_grads("mlp1", {k: blocks[k] for k in ("up1", "down1")})
    dh, g_norm["ffn_pre_norm"][1], dmix1, g_norm["mix_post_norm"][1], db_o = _norm_bwd(
        "l1_mid_bwd", dh, pre=(du, h3, nrow("ffn_pre_norm", 1)), post=(mix1, nrow("mix_post_norm", 1)), after=sent)
    blocks["b_o"] = _cols_split(db_o)
    blocks["w_o"] = _rows_split(_mm("attn_dwo", [ao_t], [dmix1], "nn", tm=512, tn=1024, out_dtypes=(PAYLOAD,)))
    dao_t = _mm("attn_dout", [w_o], [dmix1], "nt", tm=1024, tn=1024, out_dtypes=(BF16,))
    dqkv_t, db_qkv, grads["attn_sinks"] = _attn_bwd_t(qkv_t, dao_t, sinks_rep)
    blocks["b_qkv"] = db_qkv.reshape(N_DEV, 1, -1)
    blocks["w_qkv"] = _rows_split(_mm("attn_dwqkv", [dqkv_t], [u1], "nn", tm=512, tn=1024, out_dtypes=(PAYLOAD,)))
    du = _mm("attn_dx", [dqkv_t], [w_qkv_t], "tn", tm=1024, tn=1024)
    sent = reduce_grads("attn", {k: blocks[k] for k in ("w_o", "w_qkv", "b_o", "b_qkv")})
    dh, g_norm["mix_pre_norm"][1], df0, g_norm["ffn_post_norm"][0], _ = _norm_bwd(
        "l1_in_bwd", dh, pre=(du, h2, nrow("mix_pre_norm", 1)), post=(f0, nrow("ffn_post_norm", 0)), after=sent)
    du = mlp_bwd(0, df0, u0f, p0)
    sent = reduce_grads("mlp0", {k: blocks[k] for k in ("up0", "down0")})
    dh, g_norm["ffn_pre_norm"][0], dmix0, g_norm["mix_post_norm"][0], _ = _norm_bwd(
        "l0_mid_bwd", dh, pre=(du, h1, nrow("ffn_pre_norm", 0)), post=(mix0, nrow("mix_post_norm", 0)), after=sent)
    blocks["w_out"] = _rows_split(_mm("ssd_dwout", [yn], [dmix0], "tn", tm=512, tn=1024, out_dtypes=(PAYLOAD,)))
    dyn = _mm("ssd_dyn", [dmix0], [w_out], "nt", tm=1024, tn=1024)
    sent = reduce_grads("ssdout", {"w_out": blocks["w_out"]})
    dy, dz, grads["ssd_norm_w"] = _gate_norm_bwd(dyn, y, zx, rep["ssd_norm_w"], sent)
    dpx, dpb, dpc, ddt_g, dbias_g, dalog_g, dd_g = _ssd_bwd(dy, pre, states, dt_c, cum_c, cum_r, sgd_c, alog_c,
                                                             dsk_c)
    conv_out = [_conv_bwd(f"ssd_conv_bwd_{tag}", dp, zx, c0, conv_w[:, c0 - di:c0 - di + n])
                for tag, dp, (c0, n) in zip("xbc", (dpx, dpb, dpc), parts[1:4])]
    dconv_w = jnp.concatenate([o[1] for o in conv_out], axis=1)
    dconv_b = jnp.concatenate([o[2] for o in conv_out], axis=1)
    ddt = jnp.transpose(ddt_g[:, :, :SSD_HPG], (1, 0, 2)).reshape(t, nh)
    ddt = jnp.pad(ddt, ((0, 0), (0, LANES - nh))).astype(BF16)
    blocks["conv_w"] = _cols_split(dconv_w)
    grads["ssd_conv_b"] = dconv_b
    for name, val in (("ssd_dt_bias", dbias_g), ("ssd_a_log", dalog_g), ("ssd_d", dd_g)):
        grads[name] = val[:, 0, :SSD_HPG].reshape(1, nh)
    d_zx = [dz] + [o[0] for o in conv_out] + [ddt]
    dw_parts = [_mm(f"ssd_dw_{tag}", [d], [u0], "tn", tm=512, tn=1024, out_dtypes=(PAYLOAD,))
                for tag, d in zip("zxbct", d_zx)]
    dw_parts[-1] = dw_parts[-1][:nh]
    blocks["w_in"] = _rows_split(jnp.concatenate(dw_parts, axis=0))
    sent = reduce_grads("ssd", {k: blocks[k] for k in ("w_in", "conv_w")})
    w_parts = [w_in_t[r0:r0 + n] for r0, n in parts[:-1]] + [w_dt_t]
    du = _mm("ssd_dx", d_zx, w_parts, "nn", tm=256, tn=1024, after=sent)
    grad_x, g_norm["mix_pre_norm"][0] = _norm_bwd("l0_in_bwd", dh, pre=(du, x, nrow("mix_pre_norm", 0)), after=sent)
    for k in norm:
        grads[k] = jnp.concatenate(g_norm[k], axis=0)
    return loss_row, grad_x, grads


def kernel(x, ssd_w_in, ssd_conv_w, ssd_conv_b, ssd_dt_bias, ssd_a_log, ssd_d, ssd_norm_w, ssd_w_out, attn_w_qkv, attn_b_qkv, attn_sinks, attn_w_o, attn_b_o, mlp_w_up, mlp_w_down, mix_pre_norm, mix_post_norm, ffn_pre_norm, ffn_post_norm, loss_target, m_ssd_w_in, m_ssd_conv_w, m_ssd_conv_b, m_ssd_dt_bias, m_ssd_a_log, m_ssd_d, m_ssd_norm_w, m_ssd_w_out, m_attn_w_qkv, m_attn_b_qkv, m_attn_sinks, m_attn_w_o, m_attn_b_o, m_mlp_w_up, m_mlp_w_down, m_mix_pre_norm, m_mix_post_norm, m_ffn_pre_norm, m_ffn_post_norm, v_ssd_w_in, v_ssd_conv_w, v_ssd_conv_b, v_ssd_dt_bias, v_ssd_a_log, v_ssd_d, v_ssd_norm_w, v_ssd_w_out, v_attn_w_qkv, v_attn_b_qkv, v_attn_sinks, v_attn_w_o, v_attn_b_o, v_mlp_w_up, v_mlp_w_down, v_mix_pre_norm, v_mix_post_norm, v_ffn_pre_norm, v_ffn_post_norm):
    given = dict(locals())
    w = {k: given[k] for k in WEIGHTS}
    mom_m = {k: given["m_" + k] for k in WEIGHTS}
    mom_v = {k: given["v_" + k] for k in WEIGHTS}
    w_it, m_it, v_it = _items(given), _items(given, "m_"), _items(given, "v_")

    order = [k for stage in GATHER_STAGES for k in stage]
    shards = [w_it[k].astype(PAYLOAD) if k in MATRIX_ITEMS else w_it[k] for k in order]
    wide = [k in SIDE_BY_SIDE for k in order]
    g_send, g_recv, shards, lands, token = _gather_start("gather_start", shards, wide)

    def weights_of_stage(s, after):
        first = sum(len(stage) for stage in GATHER_STAGES[:s])
        sl = slice(first, first + len(GATHER_STAGES[s]))
        _, got = _gather_wait(f"gather_wait{s}", g_send, g_recv, first, len(order), shards[sl], lands[sl], wide[sl],
                              after)
        return dict(zip(GATHER_STAGES[s], got))

    in_flight = []

    def reduce_grads(tag, blocks):
        keys = list(blocks)
        started = _scatter_start(f"rs_start_{tag}", [blocks[k] for k in keys])
        in_flight.append((tag, keys, started))
        return started[-1]

    rep = {k: w[k] for k in REPLICATED}
    loss_row, grad_x, grads = _forward_backward(x[0], loss_target[0], rep, token, weights_of_stage, reduce_grads)

    def pack_rep(tree, last):
        flat = jnp.concatenate([tree[k].reshape(-1) for k in REPLICATED] + [last])
        return _pack_rows(flat, _round_up(-(-flat.shape[0] // LANES), 8), LANES)

    landed = {}

    def wait_group(group, after):
        tag, keys, (s_send, s_recv, srcs, s_lands, _) = group
        _, got = _scatter_wait(f"rs_wait_{tag}", s_send, s_recv, srcs, s_lands, after)
        landed.update(zip(keys, got))

    def adamw_item(k):
        return _sum_adamw(f"adamw_{k}", landed[k], w_it[k], m_it[k], v_it[k])

    def adamw_stack(name, keys):
        return _sum_adamw_layers(f"adamw_{name}", [landed[k] for k in keys], given[name], given["m_" + name],
                                 given["v_" + name])

    for group in in_flight[:-1]:
        wait_group(group, grad_x)
    done = {"mlp_w_up": adamw_stack("mlp_w_up", ("up0", "up1")),
            "mlp_w_down": adamw_stack("mlp_w_down", ("down0", "down1")),
            "attn_w_qkv": [o.T[None] for o in adamw_item("w_qkv")],
            "attn_w_o": [o[None] for o in adamw_item("w_o")],
            "attn_b_qkv": adamw_item("b_qkv"), "attn_b_o": adamw_item("b_o"),
            "ssd_w_out": [o[None] for o in adamw_item("w_out")]}
    partials, = _all_gather("gather_small_grads", [pack_rep(grads, loss_row[0, :1])],
                            [outs4[0] for outs4 in done.values()])
    wait_group(in_flight[-1], partials)
    done["ssd_w_in"] = [o.T[None] for o in adamw_item("w_in")]
    done["ssd_conv_w"] = [o[None] for o in adamw_item("conv_w")]
    zero = jnp.zeros((1,), F32)
    rep_out = _sum_adamw("adamw_replicated", partials, pack_rep(w, zero), pack_rep(mom_m, zero), pack_rep(mom_v, zero))

    kinds = []
    for kind, r_arr in enumerate(rep_out):
        tree = {name: outs4[kind] for name, outs4 in done.items()}
        flat, off = r_arr.reshape(-1), 0
        for k in REPLICATED:
            tree[k] = flat[off:off + w[k].size].reshape(w[k].shape)
            off += w[k].size
        kinds.append(tree)
    loss = rep_out[0].reshape(-1)[off]
    outs = [loss, grad_x[None]]
    for tree in kinds:
        outs += [tree[k] for k in WEIGHTS]
    return tuple(outs)
```

```python
import jax
import jax.numpy as jnp
from jax import lax
from jax.experimental import pallas as pl
from jax.experimental.pallas import tpu as pltpu

F32 = jnp.float32
BF16 = jnp.bfloat16
PAYLOAD = jnp.bfloat16
HIGHEST = lax.Precision.HIGHEST
MESH = pl.DeviceIdType.MESH

NORM_EPS = 1e-6
SSD_HEAD_DIM = 64
SSD_HPG = 4
SSD_D_STATE = 128
SSD_CONV_WIDTH = 4
SSD_CHUNK = 128
ATTN_HEAD_DIM = 64
ATTN_N_KV = 4
ATTN_REP = 4
ATTN_WINDOW = 128
ADAM_LR = 0.001
ADAM_B1 = 0.9
ADAM_B2 = 0.999
ADAM_EPS = 1e-08
ADAM_WD = 0.01
ADAM_STEP = 10

N_DEV = 8
LANES = 128
V7X_VMEM_LIMIT = 56 * 1024 * 1024

GW = SSD_HPG * SSD_HEAD_DIM
GC = GW + 2 * SSD_D_STATE
assert SSD_CHUNK == LANES


def _params(*sem):
    return pltpu.CompilerParams(dimension_semantics=sem, vmem_limit_bytes=V7X_VMEM_LIMIT)


def _tile(n, pref, mult=LANES):
    best = None
    t = mult
    while t <= min(n, pref):
        if n % t == 0:
            best = t
        t += mult
    return best if best is not None else n


def _round_up(n, m):
    return (n + m - 1) // m * m


def _acc(ref, val, first):
    @pl.when(first)
    def _():
        ref[...] = val

    @pl.when(jnp.logical_not(first))
    def _():
        ref[...] += val


def _dot(a, b):
    return lax.dot_general(a, b, (((1,), (0,)), ((), ())), preferred_element_type=F32)


def _dot_nt(a, b):
    return lax.dot_general(a, b, (((1,), (1,)), ((), ())), preferred_element_type=F32)


def _dot_tn(a, b):
    return lax.dot_general(a, b, (((0,), (0,)), ((), ())), preferred_element_type=F32)


def _dot_f32(a, b):
    return lax.dot_general(a, b, (((1,), (0,)), ((), ())), preferred_element_type=F32, precision=HIGHEST)


_DOTS = {"nn": _dot, "nt": _dot_nt, "tn": _dot_tn}


def _sigmoid(x):
    return 1.0 / (1.0 + jnp.exp(-x))


def _softplus(x):
    return jnp.maximum(x, 0.0) + jnp.log1p(jnp.exp(-jnp.abs(x)))


def _silu_grad(x, s):
    return s * (1.0 + x * (1.0 - s))


def _mm(name, a_list, b_list, mode, *, tm, tn, out_dtypes=(F32,), epilogue=None, tiles=(), rows=(), cols=(),
        col_blocks=False, n_use=None, after=None):
    npair = len(a_list)
    if mode == "tn":
        m = a_list[0].shape[1]
    else:
        m = a_list[0].shape[0]
    n = n_use if n_use is not None else (b_list[0].shape[0] if mode == "nt" else b_list[0].shape[1])
    tm = _tile(m, tm, LANES if mode == "tn" else 8)
    tn = _tile(n, tn)
    assert m % tm == 0 and n % tn == 0, (name, m, n, tm, tn)
    dot = _DOTS[mode]

    def body(*refs):
        a_refs = refs[:npair]
        b_refs = refs[npair:2 * npair]
        n_extra = len(tiles) + len(rows) + len(cols)
        e_refs = refs[2 * npair:2 * npair + n_extra]
        o_refs = refs[2 * npair + n_extra + len(order):]
        acc = None
        for ar, br in zip(a_refs, b_refs):
            d = dot(ar[...], br[...])
            acc = d if acc is None else acc + d
        outs = epilogue(acc, *[e[...] for e in e_refs]) if epilogue is not None else (acc,)
        for o, v in zip(o_refs, outs):
            o[...] = v.astype(o.dtype)

    in_specs = []
    for a in a_list:
        if mode == "tn":
            in_specs.append(pl.BlockSpec((a.shape[0], tm), lambda i, j: (0, i)))
        else:
            in_specs.append(pl.BlockSpec((tm, a.shape[1]), lambda i, j: (i, 0)))
    for b in b_list:
        if mode == "nt":
            in_specs.append(pl.BlockSpec((tn, b.shape[1]), lambda i, j: (j, 0)))
        else:
            in_specs.append(pl.BlockSpec((b.shape[0], tn), lambda i, j: (0, j)))
    in_specs += [pl.BlockSpec((tm, tn), lambda i, j: (i, j)) for _ in tiles]
    in_specs += [pl.BlockSpec((1, tn), lambda i, j: (0, j)) for _ in rows]
    in_specs += [pl.BlockSpec((tm, 1), lambda i, j: (i, 0)) for _ in cols]
    order = [] if after is None else [after]
    in_specs += [pl.BlockSpec((8, LANES), lambda i, j: (0, 0)) for _ in order]
    outs = pl.pallas_call(
        body,
        name=name,
        grid=(m // tm, n // tn),
        in_specs=in_specs,
        out_specs=[pl.BlockSpec((None, tm, tn), lambda i, j: (j, i, 0)) if col_blocks else
                   pl.BlockSpec((tm, tn), lambda i, j: (i, j)) for _ in out_dtypes],
        out_shape=[jax.ShapeDtypeStruct((n // tn, m, tn) if col_blocks else (m, n), dt) for dt in out_dtypes],
        compiler_params=_params("parallel", "parallel"),
    )(*a_list, *b_list, *tiles, *rows, *cols, *order)
    return outs[0] if len(out_dtypes) == 1 else outs


def _rms(x, w):
    r = lax.rsqrt(jnp.mean(x * x, axis=-1, keepdims=True) + NORM_EPS)
    return x * r * w


def _rms_bwd(x, w, dy):
    r = lax.rsqrt(jnp.mean(x * x, axis=-1, keepdims=True) + NORM_EPS)
    xh = x * r
    g = dy * w
    dx = r * (g - xh * jnp.mean(g * xh, axis=-1, keepdims=True))
    return dx, dy * xh


def _row_specs(tr, d):
    return pl.BlockSpec((tr, d), lambda i: (i, 0)), pl.BlockSpec((1, d), lambda i: (0, 0))


def _prenorm(name, h, w, after):
    t, d = h.shape
    tr = _tile(t, 512, 8)
    row, vec = _row_specs(tr, d)

    def body(h_ref, w_ref, after_ref, u_ref):
        u_ref[...] = _rms(h_ref[...], w_ref[...]).astype(BF16)

    return pl.pallas_call(body, name=name, grid=(t // tr,),
                          in_specs=[row, vec, pl.BlockSpec((8, LANES), lambda i: (0, 0))], out_specs=row,
                          out_shape=jax.ShapeDtypeStruct((t, d), BF16), compiler_params=_params("parallel"))(
                              h, w, after)


def _post_pre(name, h, m, w_post, w_pre):
    t, d = h.shape
    tr = _tile(t, 512, 8)
    row, vec = _row_specs(tr, d)

    def body(h_ref, m_ref, wq_ref, wp_ref, hn_ref, u_ref):
        hn = h_ref[...] + _rms(m_ref[...], wq_ref[...])
        hn_ref[...] = hn
        u_ref[...] = _rms(hn, wp_ref[...]).astype(BF16)

    return pl.pallas_call(body, name=name, grid=(t // tr,), in_specs=[row, row, vec, vec], out_specs=[row, row],
                          out_shape=[jax.ShapeDtypeStruct((t, d), F32), jax.ShapeDtypeStruct((t, d), BF16)],
                          compiler_params=_params("parallel"))(h, m, w_post, w_pre)


def _final_loss(name, h, m, w_post, target):
    t, d = h.shape
    tr = _tile(t, 512, 8)
    row, vec = _row_specs(tr, d)

    def body(h_ref, m_ref, wq_ref, t_ref, dh_ref, loss_ref):
        err = h_ref[...] + _rms(m_ref[...], wq_ref[...]) - t_ref[...]
        dh_ref[...] = err * (1.0 / d)
        part = 0.5 * jnp.sum(jnp.mean(err * err, axis=-1, keepdims=True), axis=0, keepdims=True)
        _acc(loss_ref, jnp.broadcast_to(part, (1, LANES)), pl.program_id(0) == 0)

    return pl.pallas_call(body, name=name, grid=(t // tr,), in_specs=[row, row, vec, row],
                          out_specs=[row, pl.BlockSpec((1, LANES), lambda i: (0, 0))],
                          out_shape=[jax.ShapeDtypeStruct((t, d), F32), jax.ShapeDtypeStruct((1, LANES), F32)],
                          compiler_params=_params("arbitrary"))(h, m, w_post, target)


def _norm_bwd(name, dh, pre=None, post=None, after=None):
    t, d = dh.shape
    tr = _tile(t, 512, 8)
    row, vec = _row_specs(tr, d)
    has_pre, has_post = pre is not None, post is not None

    def body(*refs):
        it = iter(refs)
        dh_ref = next(it)
        if has_pre:
            du_ref, x_ref, wp_ref = next(it), next(it), next(it)
        if has_post:
            m_ref, wq_ref = next(it), next(it)
        if after is not None:
            next(it)
        first = pl.program_id(0) == 0
        dh_v = dh_ref[...]
        if has_pre:
            dhn_ref, dwp_ref = next(it), next(it)
            dx, dwr = _rms_bwd(x_ref[...], wp_ref[...], du_ref[...])
            dh_v = dh_v + dx
            dhn_ref[...] = dh_v
            _acc(dwp_ref, jnp.sum(dwr, axis=0, keepdims=True), first)
        if has_post:
            dm_ref, dwq_ref, dms_ref = next(it), next(it), next(it)
            dm, dwr = _rms_bwd(m_ref[...], wq_ref[...], dh_v)
            dm_ref[...] = dm.astype(BF16)
            _acc(dwq_ref, jnp.sum(dwr, axis=0, keepdims=True), first)
            _acc(dms_ref, jnp.sum(dm, axis=0, keepdims=True), first)

    ins, in_specs, out_specs, out_shape = [dh], [row], [], []
    if has_pre:
        ins += list(pre)
        in_specs += [row, row, vec]
        out_specs += [row, vec]
        out_shape += [jax.ShapeDtypeStruct((t, d), F32), jax.ShapeDtypeStruct((1, d), F32)]
    if has_post:
        ins += list(post)
        in_specs += [row, vec]
        out_specs += [row, vec, vec]
        out_shape += [jax.ShapeDtypeStruct((t, d), BF16), jax.ShapeDtypeStruct((1, d), F32),
                      jax.ShapeDtypeStruct((1, d), F32)]
    if after is not None:
        ins.append(after)
        in_specs.append(pl.BlockSpec((8, LANES), lambda i: (0, 0)))
    return pl.pallas_call(body, name=name, grid=(t // tr,), in_specs=in_specs, out_specs=out_specs,
                          out_shape=out_shape, compiler_params=_params("arbitrary"))(*ins)


HALO = 8


def _shift_later(cur, prev, s):
    rolled = pltpu.roll(cur, s, 0)
    row = lax.broadcasted_iota(jnp.int32, prev.shape, 0)
    first = jnp.where(row < s, pltpu.roll(prev, s, 0), rolled[0:HALO])
    return jnp.concatenate([first, rolled[HALO:]], axis=0)


def _shift_earlier(cur, nxt, s):
    tt = cur.shape[0]
    rolled = pltpu.roll(cur, tt - s, 0)
    row = lax.broadcasted_iota(jnp.int32, nxt.shape, 0)
    last = jnp.where(row >= HALO - s, pltpu.roll(nxt, HALO - s, 0), rolled[tt - HALO:])
    return jnp.concatenate([rolled[:tt - HALO], last], axis=0)


def _conv_fwd(zx, col0, n_ch, conv_w, conv_b):
    t = zx.shape[0]
    tc = _tile(n_ch, 512)
    tt = _tile(t, 1024, 8)
    cb0 = col0 // tc
    assert col0 % tc == 0
    kw = SSD_CONV_WIDTH

    def body(x_ref, p_ref, w_ref, b_ref, o_ref):
        cur = x_ref[...]
        prev = jnp.where(pl.program_id(1) > 0, p_ref[...], 0.0)
        w = w_ref[...]
        acc = b_ref[...] + w[kw - 1:kw, :] * cur
        for k in range(kw - 1):
            acc = acc + w[k:k + 1, :] * _shift_later(cur, prev, kw - 1 - k)
        o_ref[...] = acc

    return pl.pallas_call(
        body, name="ssd_conv_fwd", grid=(n_ch // tc, t // tt),
        in_specs=[pl.BlockSpec((tt, tc), lambda j, i: (i, cb0 + j)),
                  pl.BlockSpec((HALO, tc), lambda j, i: (jnp.maximum(i * (tt // HALO) - 1, 0), cb0 + j)),
                  pl.BlockSpec((kw, tc), lambda j, i: (0, j)),
                  pl.BlockSpec((1, tc), lambda j, i: (0, j))],
        out_specs=pl.BlockSpec((tt, tc), lambda j, i: (i, j)),
        out_shape=jax.ShapeDtypeStruct((t, n_ch), F32),
        compiler_params=_params("parallel", "parallel"))(zx, zx, conv_w, conv_b)


def _conv_bwd(name, dpre, zx, col0, conv_w):
    t, n_ch = dpre.shape
    tc = _tile(n_ch, 512)
    tt = _tile(t, 1024, 8)
    cb0 = col0 // tc
    kw = SSD_CONV_WIDTH
    nt = t // tt

    def body(d_ref, dn_ref, x_ref, p_ref, w_ref, dx_ref, dw_ref, db_ref):
        i = pl.program_id(1)
        d = d_ref[...]
        d_next = jnp.where(i < nt - 1, dn_ref[...], 0.0)
        x = x_ref[...]
        x_prev = jnp.where(i > 0, p_ref[...], 0.0)
        w = w_ref[...]
        dx = w[kw - 1:kw, :] * d
        for k in range(kw - 1):
            dx = dx + w[k:k + 1, :] * _shift_earlier(d, d_next, kw - 1 - k)
        dx_ref[...] = dx.astype(BF16)
        first = i == 0
        for k in range(kw):
            xs = x if k == kw - 1 else _shift_later(x, x_prev, kw - 1 - k)
            val = jnp.sum(d * xs, axis=0, keepdims=True)

            @pl.when(first)
            def _():
                dw_ref[k:k + 1, :] = val

            @pl.when(jnp.logical_not(first))
            def _():
                dw_ref[k:k + 1, :] += val
        _acc(db_ref, jnp.sum(d, axis=0, keepdims=True), first)

    return pl.pallas_call(
        body, name=name, grid=(n_ch // tc, nt),
        in_specs=[pl.BlockSpec((tt, tc), lambda j, i: (i, j)),
                  pl.BlockSpec((HALO, tc), lambda j, i: (jnp.minimum((i + 1) * (tt // HALO), t // HALO - 1), j)),
                  pl.BlockSpec((tt, tc), lambda j, i: (i, cb0 + j)),
                  pl.BlockSpec((HALO, tc), lambda j, i: (jnp.maximum(i * (tt // HALO) - 1, 0), cb0 + j)),
                  pl.BlockSpec((kw, tc), lambda j, i: (0, j))],
        out_specs=[pl.BlockSpec((tt, tc), lambda j, i: (i, j)),
                   pl.BlockSpec((kw, tc), lambda j, i: (0, j)),
                   pl.BlockSpec((1, tc), lambda j, i: (0, j))],
        out_shape=[jax.ShapeDtypeStruct((t, n_ch), BF16), jax.ShapeDtypeStruct((kw, n_ch), F32),
                   jax.ShapeDtypeStruct((1, n_ch), F32)],
        compiler_params=_params("parallel", "arbitrary"))(dpre, dpre, zx, zx, conv_w)


def _head_of_lane(shape, width):
    return lax.broadcasted_iota(jnp.int32, shape, len(shape) - 1) // width


def _select_dot(v, pick, pick_first=False):
    hi = v.astype(BF16)
    lo = (v - hi.astype(F32)).astype(BF16)
    return _dot(pick, hi) + _dot(pick, lo) if pick_first else _dot(hi, pick) + _dot(lo, pick)


def _expand(v, n_rows, on_mxu=False):
    if not on_mxu:
        head = _head_of_lane((n_rows, GW), SSD_HEAD_DIM)
        out = jnp.zeros((n_rows, GW), F32)
        for j in range(SSD_HPG):
            out = jnp.where(head == j, v[:, j:j + 1], out)
        return out
    src = lax.broadcasted_iota(jnp.int32, (LANES, GW), 0)
    return _select_dot(v, (src == _head_of_lane((LANES, GW), SSD_HEAD_DIM)).astype(BF16))


def _contract(v, n_rows, on_mxu=False):
    if not on_mxu:
        head = _head_of_lane((n_rows, GW), SSD_HEAD_DIM)
        lane = lax.broadcasted_iota(jnp.int32, (n_rows, LANES), 1)
        out = jnp.zeros((n_rows, LANES), F32)
        for j in range(SSD_HPG):
            s = jnp.sum(jnp.where(head == j, v, 0.0), axis=1, keepdims=True)
            out = jnp.where(lane == j, s, out)
        return out
    dst = lax.broadcasted_iota(jnp.int32, (GW, LANES), 1)
    return _select_dot(v, (lax.broadcasted_iota(jnp.int32, (GW, LANES), 0) // SSD_HEAD_DIM == dst).astype(BF16))


def _ssd_dt_prep(zdt, bias, alog, ng):
    t = zdt.shape[0]
    q = SSD_CHUNK

    def body(z_ref, b_ref, a_ref, dt_ref, cum_ref, cumr_ref, sg_ref):
        raw = z_ref[...] + b_ref[...]
        dt = _softplus(raw)
        sgd = _sigmoid(raw)
        row = lax.broadcasted_iota(jnp.int32, (q, q), 0)
        col = lax.broadcasted_iota(jnp.int32, (q, q), 1)
        cum = _dot_f32((col <= row).astype(F32), dt * (-jnp.exp(a_ref[...])))
        cum_t = cum.T
        lane = lax.broadcasted_iota(jnp.int32, (q, LANES), 1)
        for g in range(ng):
            shift = (LANES - g * SSD_HPG) % LANES

            def group(v):
                return jnp.where(lane < SSD_HPG, pltpu.roll(v, shift, 1) if shift else v, 0.0)

            dt_ref[g] = group(dt)
            cum_ref[g] = group(cum)
            sg_ref[g] = group(sgd)
            cumr_ref[g] = (pltpu.roll(cum_t, shift, 0) if shift else cum_t)[0:8, :]

    cols = pl.BlockSpec((ng, q, LANES), lambda c: (0, c, 0))
    vec = pl.BlockSpec((1, LANES), lambda c: (0, 0))
    col_shape = jax.ShapeDtypeStruct((ng, t, LANES), F32)
    return pl.pallas_call(body, name="ssd_dt_prep", grid=(t // q,),
                          in_specs=[pl.BlockSpec((q, LANES), lambda c: (c, 0)), vec, vec],
                          out_specs=[cols, cols, pl.BlockSpec((ng, 8, q), lambda c: (0, 0, c)), cols],
                          out_shape=[col_shape, col_shape, jax.ShapeDtypeStruct((ng, 8, t), F32), col_shape],
                          compiler_params=_params("parallel"))(zdt, bias, alog)


def _ssd_common(pre, dt, cum, cum_r, alog_c, on_mxu):
    q = SSD_CHUNK
    sg = _sigmoid(pre)
    act = pre * sg
    xa = act[:, :GW]
    bm = act[:, GW:GW + SSD_D_STATE].astype(BF16)
    cm = act[:, GW + SSD_D_STATE:].astype(BF16)
    row = lax.broadcasted_iota(jnp.int32, (q, q), 0)
    col = lax.broadcasted_iota(jnp.int32, (q, q), 1)
    tril = col <= row
    a_c = -jnp.exp(alog_c)
    g = _dot_nt(cm, bm)
    dt_x = _expand(dt, q, on_mxu)
    xdt = xa * dt_x
    cl = cum[q - 1:q, :]
    e_c = jnp.exp(cl - cum)
    lam_c = jnp.exp(cum)
    return dict(sg=sg, xa=xa, bm=bm, cm=cm, tril=tril, row=row, col=col, dt=dt, a_c=a_c, cum=cum, cum_r=cum_r,
                g=g, dt_x=dt_x, xdt=xdt, cl=cl, e_c=e_c, lam_c=lam_c)


SSD_GPS_FWD = 8
SSD_GPS_BWD = 2


def _ssd_specs(nc, rev, ng, gps):
    q = SSD_CHUNK
    xw, nw = gps * GW, gps * SSD_D_STATE
    b_off = ng * GW // nw
    c_off = (ng * GW + ng * SSD_D_STATE) // nw
    assert ng % gps == 0 and (ng * GW) % nw == 0 and (ng * SSD_D_STATE) % nw == 0

    def ch(c):
        return nc - 1 - c if rev else c

    chunk_grp = [pl.BlockSpec((q, xw), lambda g, c: (ch(c), g)),
                 pl.BlockSpec((q, nw), lambda g, c: (ch(c), b_off + g)),
                 pl.BlockSpec((q, nw), lambda g, c: (ch(c), c_off + g))]
    col_form = pl.BlockSpec((gps, q, LANES), lambda g, c: (g, ch(c), 0))
    row_form = pl.BlockSpec((gps, 8, q), lambda g, c: (g, 0, ch(c)))
    col_par = pl.BlockSpec((gps, 1, LANES), lambda g, c: (g, 0, 0))
    y_spec = pl.BlockSpec((q, xw), lambda g, c: (ch(c), g))
    st_spec = pl.BlockSpec((gps, None, GW, SSD_D_STATE), lambda g, c: (g, ch(c), 0, 0))
    bc_spec = pl.BlockSpec((q, nw), lambda g, c: (ch(c), g))
    return chunk_grp, col_form, row_form, col_par, y_spec, st_spec, bc_spec


def _ssd_group_views(gi, wide, narrow, stacked):
    xs, ns = pl.ds(gi * GW, GW), pl.ds(gi * SSD_D_STATE, SSD_D_STATE)
    return [r.at[:, xs] for r in wide], [r.at[:, ns] for r in narrow], [r.at[gi] for r in stacked]


def _ssd_fwd(pre, dt_c, cum_c, cum_r, alog_c, dsk_c):
    t = pre.shape[0]
    ng = pre.shape[1] // GC
    q = SSD_CHUNK
    nc = t // q
    gps = SSD_GPS_FWD if ng % SSD_GPS_FWD == 0 else SSD_GPS_BWD
    chunk_grp, col_form, row_form, col_par, y_spec, st_spec, _ = _ssd_specs(nc, False, ng, gps)

    def body(px_ref, pb_ref, pc_ref, dt_ref, cum_ref, cumr_ref, ac_ref, dk_ref, y_ref, sp_ref, st_ref):
        @pl.when(pl.program_id(1) == 0)
        def _():
            st_ref[...] = jnp.zeros_like(st_ref)

        for gi in range(gps):
            (px, y), (pb, pc), rest = _ssd_group_views(
                gi, (px_ref, y_ref), (pb_ref, pc_ref), (dt_ref, cum_ref, cumr_ref, ac_ref, dk_ref, sp_ref, st_ref))
            one_group(px, pb, pc, *rest[:5], y, *rest[5:])

    def one_group(px_ref, pb_ref, pc_ref, dt_ref, cum_ref, cumr_ref, ac_ref, dk_ref, y_ref, sp_ref, st_ref):
        pre_v = jnp.concatenate([px_ref[...], pb_ref[...], pc_ref[...]], axis=1)
        v = _ssd_common(pre_v, dt_ref[...], cum_ref[...], cumr_ref[...], ac_ref[...], False)
        s0 = st_ref[...]
        sp_ref[...] = s0
        r = _dot_nt(v["cm"], s0.astype(BF16))
        y = _expand(v["lam_c"], q) * r + _expand(dk_ref[...], 1) * v["xa"]
        head = _head_of_lane((q, GW), SSD_HEAD_DIM)
        for j in range(SSD_HPG):
            diff = v["cum"][:, j:j + 1] - v["cum_r"][j:j + 1, :]
            w = (v["g"] * jnp.exp(jnp.where(v["tril"], diff, -jnp.inf))).astype(BF16)
            y = y + _dot(w, jnp.where(head == j, v["xdt"], 0.0).astype(BF16))
        y_ref[...] = y
        ds = _dot_tn((v["xdt"] * _expand(v["e_c"], q)).astype(BF16), v["bm"])
        for j in range(SSD_HPG):
            rows = slice(j * SSD_HEAD_DIM, (j + 1) * SSD_HEAD_DIM)
            st_ref[rows, :] = s0[rows, :] * jnp.exp(v["cum_r"][j:j + 1, q - 1:q]) + ds[rows, :]

    return pl.pallas_call(
        body, name="ssd_scan_fwd", grid=(ng // gps, nc),
        in_specs=chunk_grp + [col_form, col_form, row_form, col_par, col_par],
        out_specs=[y_spec, st_spec],
        out_shape=[jax.ShapeDtypeStruct((t, ng * GW), F32), jax.ShapeDtypeStruct((ng, nc, GW, SSD_D_STATE), F32)],
        scratch_shapes=[pltpu.VMEM((gps, GW, SSD_D_STATE), F32)],
        compiler_params=_params("parallel", "arbitrary"))(pre, pre, pre, dt_c, cum_c, cum_r, alog_c, dsk_c)


def _ssd_bwd(dy, pre, states, dt_c, cum_c, cum_r, sgd_c, alog_c, dsk_c):
    t = pre.shape[0]
    ng = pre.shape[1] // GC
    q = SSD_CHUNK
    nc = t // q
    gps = SSD_GPS_BWD
    chunk_grp, col_form, row_form, col_par, y_spec, st_spec, bc_spec = _ssd_specs(nc, True, ng, gps)

    def body(dy_ref, px_ref, pb_ref, pc_ref, sp_ref, dt_ref, cum_ref, cumr_ref, sgd_ref, ac_ref, dk_ref,
             dpx_ref, dpb_ref, dpc_ref, ddt_ref, dbias_ref, dalog_ref, dd_ref, ds_ref):
        @pl.when(pl.program_id(1) == 0)
        def _():
            ds_ref[...] = jnp.zeros_like(ds_ref)

        for gi in range(gps):
            (dy, px, dpx), (pb, pc, dpb, dpc), rest = _ssd_group_views(
                gi, (dy_ref, px_ref, dpx_ref), (pb_ref, pc_ref, dpb_ref, dpc_ref),
                (sp_ref, dt_ref, cum_ref, cumr_ref, sgd_ref, ac_ref, dk_ref, ddt_ref, dbias_ref, dalog_ref, dd_ref,
                 ds_ref))
            one_group(dy, px, pb, pc, *rest[:7], dpx, dpb, dpc, *rest[7:])

    def one_group(dy_ref, px_ref, pb_ref, pc_ref, sp_ref, dt_ref, cum_ref, cumr_ref, sgd_ref, ac_ref, dk_ref,
                  dpx_ref, dpb_ref, dpc_ref, ddt_ref, dbias_ref, dalog_ref, dd_ref, ds_ref):
        first = pl.program_id(1) == 0
        pre_v = jnp.concatenate([px_ref[...], pb_ref[...], pc_ref[...]], axis=1)
        v = _ssd_common(pre_v, dt_ref[...], cum_ref[...], cumr_ref[...], ac_ref[...], True)
        xa, bm, cm, xdt, cum, cum_r = v["xa"], v["bm"], v["cm"], v["xdt"], v["cum"], v["cum_r"]
        xdt_b = xdt.astype(BF16)
        dy_v = dy_ref[...]
        s0 = sp_ref[...]
        ds1 = ds_ref[...]
        s0b, ds1b = s0.astype(BF16), ds1.astype(BF16)
        head = _head_of_lane((q, GW), SSD_HEAD_DIM)
        lane = lax.broadcasted_iota(jnp.int32, (q, LANES), 1)
        lane1 = lax.broadcasted_iota(jnp.int32, (1, LANES), 1)
        lam_x = _expand(v["lam_c"], q, True)
        e_x = _expand(v["e_c"], q, True)

        dxa = _expand(dk_ref[...], 1) * dy_v
        dd = _contract(jnp.sum(dy_v * xa, axis=0, keepdims=True), 1)
        r = _dot_nt(cm, s0b)
        dcum = _contract(dy_v * r * lam_x, q, True)
        drb = (lam_x * dy_v).astype(BF16)
        dc = _dot(drb, s0b)
        ds0 = _dot_tn(drb, cm)
        extra = jnp.zeros((1, LANES), F32)
        for j in range(SSD_HPG):
            rows = slice(j * SSD_HEAD_DIM, (j + 1) * SSD_HEAD_DIM)
            lam_last = jnp.exp(cum_r[j:j + 1, q - 1:q])
            ds_ref[rows, :] = ds0[rows, :] + lam_last * ds1[rows, :]
            tot = jnp.sum(jnp.sum(ds1[rows, :] * s0[rows, :], axis=1, keepdims=True), axis=0, keepdims=True)
            extra = jnp.where(lane1 == j, lam_last * tot, extra)
        dv = _dot_nt(bm, ds1b)
        db = _dot((xdt * e_x).astype(BF16), ds1b)
        dxdt = e_x * dv
        dee = _contract(dv * xdt, q, True) * v["e_c"]
        dcum = dcum - dee
        extra = extra + jnp.sum(dee, axis=0, keepdims=True)
        dg = jnp.zeros((q, q), F32)
        col_sums = jnp.zeros((q, q), F32)
        for j in range(SSD_HPG):
            diff = cum[:, j:j + 1] - cum_r[j:j + 1, :]
            el = jnp.exp(jnp.where(v["tril"], diff, -jnp.inf))
            gl = v["g"] * el
            dym = jnp.where(head == j, dy_v, 0.0).astype(BF16)
            dwm = _dot_nt(dym, xdt_b)
            dxdt = dxdt + _dot_tn(gl.astype(BF16), dym)
            z = dwm * gl
            dcum = jnp.where(lane == j, dcum + jnp.sum(z, axis=1, keepdims=True), dcum)
            col_sums = jnp.where(v["row"] == j, jnp.sum(z, axis=0, keepdims=True), col_sums)
            dg = dg + dwm * el
        dcum = dcum - col_sums.T
        dgb = dg.astype(BF16)
        dc = dc + _dot(dgb, bm)
        db = db + _dot_tn(dgb, cm)
        da = _select_dot(dcum, (v["row"] <= v["col"]).astype(BF16), True) + extra
        ddt = _contract(dxdt * xa, q, True) + v["a_c"] * da
        dalog = jnp.sum(v["dt"] * da, axis=0, keepdims=True) * v["a_c"]
        dxa = dxa + v["dt_x"] * dxdt
        ddt_raw = jnp.where(lane < SSD_HPG, ddt * sgd_ref[...], 0.0)
        sgrad = _silu_grad(pre_v, v["sg"])
        dpx_ref[...] = dxa * sgrad[:, :GW]
        dpb_ref[...] = db * sgrad[:, GW:GW + SSD_D_STATE]
        dpc_ref[...] = dc * sgrad[:, GW + SSD_D_STATE:]
        ddt_ref[...] = ddt_raw
        _acc(dbias_ref, jnp.sum(ddt_raw, axis=0, keepdims=True), first)
        _acc(dalog_ref, jnp.where(lane1 < SSD_HPG, dalog, 0.0), first)
        _acc(dd_ref, dd, first)

    return pl.pallas_call(
        body, name="ssd_scan_bwd", grid=(ng // gps, nc),
        in_specs=[y_spec] + chunk_grp + [st_spec, col_form, col_form, row_form, col_form, col_par, col_par],
        out_specs=[y_spec, bc_spec, bc_spec, col_form, col_par, col_par, col_par],
        out_shape=[jax.ShapeDtypeStruct((t, ng * GW), F32), jax.ShapeDtypeStruct((t, ng * SSD_D_STATE), F32),
                   jax.ShapeDtypeStruct((t, ng * SSD_D_STATE), F32), jax.ShapeDtypeStruct((ng, t, LANES), F32),
                   jax.ShapeDtypeStruct((ng, 1, LANES), F32), jax.ShapeDtypeStruct((ng, 1, LANES), F32),
                   jax.ShapeDtypeStruct((ng, 1, LANES), F32)],
        scratch_shapes=[pltpu.VMEM((gps, GW, SSD_D_STATE), F32)],
        compiler_params=_params("parallel", "arbitrary"))(dy, pre, pre, pre, states, dt_c, cum_c, cum_r, sgd_c, alog_c,
                                                           dsk_c)


def _gate_norm_fwd(y, zx, norm_w):
    t, di = y.shape
    tr = _tile(t, 512, 8)
    ng = di // GW

    def body(y_ref, z_ref, w_ref, o_ref):
        z = z_ref[...]
        gate = y_ref[...] * (z * _sigmoid(z))
        w = w_ref[...]
        for g in range(ng):
            cols = slice(g * GW, (g + 1) * GW)
            gs = gate[:, cols]
            r = lax.rsqrt(jnp.mean(gs * gs, axis=-1, keepdims=True) + NORM_EPS)
            o_ref[:, cols] = (gs * r * w[:, cols]).astype(BF16)

    row = pl.BlockSpec((tr, di), lambda i: (i, 0))
    return pl.pallas_call(body, name="ssd_gate_norm_fwd", grid=(t // tr,),
                          in_specs=[row, row, pl.BlockSpec((1, di), lambda i: (0, 0))], out_specs=row,
                          out_shape=jax.ShapeDtypeStruct((t, di), BF16), compiler_params=_params("parallel"))(
                              y, zx, norm_w)


def _gate_norm_bwd(dyn, y, zx, norm_w, after):
    t, di = y.shape
    tr = _tile(t, 256, 8)
    ng = di // GW

    def body(d_ref, y_ref, z_ref, w_ref, after_ref, dy_ref, dz_ref, dw_ref):
        z = z_ref[...]
        yv = y_ref[...]
        sg = _sigmoid(z)
        sz = z * sg
        gate = yv * sz
        w = w_ref[...]
        d = d_ref[...]
        dsz = _silu_grad(z, sg)
        dws = []
        for g in range(ng):
            cols = slice(g * GW, (g + 1) * GW)
            dg, dwr = _rms_bwd(gate[:, cols], w[:, cols], d[:, cols])
            dy_ref[:, cols] = dg * sz[:, cols]
            dz_ref[:, cols] = (dg * yv[:, cols] * dsz[:, cols]).astype(BF16)
            dws.append(jnp.sum(dwr, axis=0, keepdims=True))
        first = pl.program_id(0) == 0
        for g in range(ng):
            cols = slice(g * GW, (g + 1) * GW)

            @pl.when(first)
            def _():
                dw_ref[:, cols] = dws[g]

            @pl.when(jnp.logical_not(first))
            def _():
                dw_ref[:, cols] += dws[g]

    row = pl.BlockSpec((tr, di), lambda i: (i, 0))
    vec = pl.BlockSpec((1, di), lambda i: (0, 0))
    return pl.pallas_call(body, name="ssd_gate_norm_bwd", grid=(t // tr,),
                          in_specs=[row, row, row, vec, pl.BlockSpec((8, LANES), lambda i: (0, 0))],
                          out_specs=[row, row, vec],
                          out_shape=[jax.ShapeDtypeStruct((t, di), F32), jax.ShapeDtypeStruct((t, di), BF16),
                                     jax.ShapeDtypeStruct((1, di), F32)],
                          compiler_params=_params("arbitrary"))(dyn, y, zx, norm_w, after)


def _attn_mask_t(n):
    w = ATTN_WINDOW
    kpos = lax.broadcasted_iota(jnp.int32, (2 * w, ATTN_REP * w), 0)
    qpos = lax.broadcasted_iota(jnp.int32, (2 * w, ATTN_REP * w), 1) % w + w
    rel = qpos - kpos
    return (rel >= 0) & (rel < w) & jnp.logical_not((n == 0) & (kpos < w))


def _attn_probs_t(qts, ktb, mask, sink):
    s = _dot_tn(ktb, qts) * (ATTN_HEAD_DIM ** -0.5)
    s = jnp.where(mask, s, -jnp.inf)
    m = jnp.maximum(jnp.max(s, axis=0, keepdims=True), sink)
    e = jnp.exp(s - m)
    es = jnp.exp(sink - m)
    inv = 1.0 / (jnp.sum(e, axis=0, keepdims=True) + es)
    return e * inv, es * inv


def _attn_blocks_t(kv, q_ref, kc_ref, vc_ref, kp_ref, vp_ref):
    hd = ATTN_HEAD_DIM
    rows = slice(kv * hd, (kv + 1) * hd)
    ktb = jnp.concatenate([kp_ref[rows, :], kc_ref[rows, :]], axis=1)
    vtb = jnp.concatenate([vp_ref[rows, :], vc_ref[rows, :]], axis=1)
    qts = jnp.concatenate([q_ref[(kv * ATTN_REP + r) * hd:(kv * ATTN_REP + r + 1) * hd, :]
                           for r in range(ATTN_REP)], axis=1)
    return qts, ktb, vtb


def _attn_specs_t(nb, cur, prev):
    w, hd = ATTN_WINDOW, ATTN_HEAD_DIM
    kd = ATTN_N_KV * hd
    qd = ATTN_REP * kd
    return [pl.BlockSpec((qd, w), lambda n: (0, cur(n))),
            pl.BlockSpec((kd, w), lambda n: (ATTN_REP, cur(n))),
            pl.BlockSpec((kd, w), lambda n: (ATTN_REP + 1, cur(n))),
            pl.BlockSpec((kd, w), lambda n: (ATTN_REP, prev(n))),
            pl.BlockSpec((kd, w), lambda n: (ATTN_REP + 1, prev(n)))]


def _attn_fwd_t(qkv_t, sinks_rep):
    t = qkv_t.shape[1]
    w, hd = ATTN_WINDOW, ATTN_HEAD_DIM
    qd = ATTN_N_KV * ATTN_REP * hd
    nb = t // w

    def body(q_ref, kc_ref, vc_ref, kp_ref, vp_ref, s_ref, o_ref):
        mask = _attn_mask_t(pl.program_id(0))
        for kv in range(ATTN_N_KV):
            qts, ktb, vtb = _attn_blocks_t(kv, q_ref, kc_ref, vc_ref, kp_ref, vp_ref)
            p, _ = _attn_probs_t(qts, ktb, mask, s_ref[kv])
            ots = _dot(vtb, p.astype(BF16))
            for r in range(ATTN_REP):
                h = kv * ATTN_REP + r
                o_ref[h * hd:(h + 1) * hd, :] = ots[:, r * w:(r + 1) * w].astype(BF16)

    return pl.pallas_call(
        body, name="attn_fwd", grid=(nb,),
        in_specs=_attn_specs_t(nb, lambda n: n, lambda n: jnp.maximum(n - 1, 0)) + [
            pl.BlockSpec(sinks_rep.shape, lambda n: (0, 0, 0))],
        out_specs=pl.BlockSpec((qd, w), lambda n: (0, n)),
        out_shape=jax.ShapeDtypeStruct((qd, t), BF16),
        compiler_params=_params("parallel"))(qkv_t, qkv_t, qkv_t, qkv_t, qkv_t, sinks_rep)


def _attn_bwd_t(qkv_t, do_t, sinks_rep):
    t = qkv_t.shape[1]
    w, hd = ATTN_WINDOW, ATTN_HEAD_DIM
    kd = ATTN_N_KV * hd
    qd = ATTN_REP * kd
    nq = ATTN_N_KV * ATTN_REP
    nb = t // w
    rows_all = qd + 2 * kd

    def body(q_ref, kc_ref, vc_ref, kp_ref, vp_ref, do_ref, s_ref, dqkv_ref, bsum_ref, dsk_ref,
             carry_ref, new_ref, bacc_ref, sacc_ref):
        n = pl.program_id(0)

        @pl.when(n == 0)
        def _():
            carry_ref[...] = jnp.zeros_like(carry_ref)
            bacc_ref[...] = jnp.zeros_like(bacc_ref)
            sacc_ref[...] = jnp.zeros_like(sacc_ref)

        @pl.when(n < nb)
        def _():
            mask = _attn_mask_t(n)
            for kv in range(ATTN_N_KV):
                qts, ktb, vtb = _attn_blocks_t(kv, q_ref, kc_ref, vc_ref, kp_ref, vp_ref)
                dots = jnp.concatenate([do_ref[(kv * ATTN_REP + r) * hd:(kv * ATTN_REP + r + 1) * hd, :]
                                        for r in range(ATTN_REP)], axis=1)
                p, ps = _attn_probs_t(qts, ktb, mask, s_ref[kv])
                dpt = _dot_tn(vtb, dots)
                delta = jnp.sum(p * dpt, axis=0, keepdims=True)
                dst = (p * (dpt - delta) * (hd ** -0.5)).astype(BF16)
                dqts = _dot(ktb, dst)
                for r in range(ATTN_REP):
                    h = kv * ATTN_REP + r
                    new_ref[h * hd:(h + 1) * hd, :] = dqts[:, r * w:(r + 1) * w]
                dktb = _dot_nt(qts, dst)
                dvtb = _dot_nt(dots, p.astype(BF16))
                krows = slice(qd + kv * hd, qd + (kv + 1) * hd)
                vrows = slice(qd + kd + kv * hd, qd + kd + (kv + 1) * hd)
                carry_ref[krows, :] += dktb[:, :w]
                carry_ref[vrows, :] += dvtb[:, :w]
                new_ref[krows, :] = dktb[:, w:]
                new_ref[vrows, :] = dvtb[:, w:]
                sacc_ref[kv] += -(ps * delta)

        @pl.when(n >= 1)
        def _():
            done = carry_ref[...]
            dqkv_ref[...] = done.astype(BF16)
            bacc_ref[...] += done

        @pl.when(n < nb)
        def _():
            carry_ref[...] = new_ref[...]

        @pl.when(n == nb)
        def _():
            bsum_ref[...] = jnp.sum(bacc_ref[...], axis=1, keepdims=True)
            lane = lax.broadcasted_iota(jnp.int32, (1, nq), 1)
            dsk = jnp.zeros((1, nq), F32)
            for kv in range(ATTN_N_KV):
                acc = sacc_ref[kv]
                for r in range(ATTN_REP):
                    tot = jnp.sum(acc[:, r * w:(r + 1) * w], axis=1, keepdims=True)
                    dsk = jnp.where(lane == kv * ATTN_REP + r, tot, dsk)
            dsk_ref[...] = dsk

    cur = lambda n: jnp.minimum(n, nb - 1)
    prev = lambda n: jnp.maximum(jnp.minimum(n, nb - 1) - 1, 0)
    return pl.pallas_call(
        body, name="attn_bwd", grid=(nb + 1,),
        in_specs=_attn_specs_t(nb, cur, prev) + [pl.BlockSpec((qd, w), lambda n: (0, cur(n))),
                                                 pl.BlockSpec(sinks_rep.shape, lambda n: (0, 0, 0))],
        out_specs=[pl.BlockSpec((rows_all, w), lambda n: (0, jnp.maximum(n - 1, 0))),
                   pl.BlockSpec((rows_all, 1), lambda n: (0, 0)),
                   pl.BlockSpec((1, nq), lambda n: (0, 0))],
        out_shape=[jax.ShapeDtypeStruct((rows_all, t), BF16), jax.ShapeDtypeStruct((rows_all, 1), F32),
                   jax.ShapeDtypeStruct((1, nq), F32)],
        scratch_shapes=[pltpu.VMEM((rows_all, w), F32), pltpu.VMEM((rows_all, w), F32),
                        pltpu.VMEM((rows_all, w), F32), pltpu.VMEM(sinks_rep.shape, F32)],
        compiler_params=_params("arbitrary"))(qkv_t, qkv_t, qkv_t, qkv_t, qkv_t, do_t, sinks_rep)


HBM_SPEC = pl.BlockSpec(memory_space=pl.ANY)
HBM_ONLY = pl.BlockSpec(memory_space=pltpu.HBM)


def _comm_call(name, body, ins, out_shapes, n_sems):
    return pl.pallas_call(
        body, name=name, in_specs=[HBM_SPEC] * len(ins), out_specs=[HBM_SPEC] * len(out_shapes),
        out_shape=out_shapes,
        scratch_shapes=[pltpu.SemaphoreType.DMA((s,)) for s in n_sems])(*ins)


def _all_gather(name, shards, after):
    n = len(shards)
    na = len(after)

    def body(*refs):
        x_refs, out_refs = refs[:n], refs[n + na:2 * n + na]
        send_sems, recv_sems, local_sems = refs[2 * n + na:]
        x, y, c = lax.axis_index("x"), lax.axis_index("y"), lax.axis_index("c")
        me, sibling = (x, y, c), (x, y, 1 - c)
        chips = [(1 - x, y), (x, 1 - y), (1 - x, 1 - y)]

        def slot(i, px, py, pc):
            return out_refs[i].at[4 * px + 2 * py + pc]

        def copy(k, i, block, to, src=None):
            return pltpu.make_async_remote_copy(
                src_ref=slot(i, *block) if src is None else src, dst_ref=slot(i, *block),
                send_sem=send_sems.at[k * n + i], recv_sem=recv_sems.at[k * n + i], device_id=to,
                device_id_type=MESH)

        mine = [pltpu.make_async_copy(x_refs[i], slot(i, *me), local_sems.at[i]) for i in range(n)]
        first = []
        for i in range(n):
            mine[i].start()
            first.append(copy(0, i, me, sibling, src=x_refs[i]))
            first += [copy(1 + j, i, me, (*chip, c), src=x_refs[i]) for j, chip in enumerate(chips)]
        for cp in first:
            cp.start()
        passed = []
        for i in range(n):
            for j, chip in enumerate(chips):
                copy(1 + j, i, (*chip, c), me).wait_recv()
                passed.append(copy(4 + j, i, (*chip, c), sibling))
                passed[-1].start()
        for i in range(n):
            copy(0, i, sibling, me).wait_recv()
            for j, chip in enumerate(chips):
                copy(4 + j, i, (*chip, 1 - c), me).wait_recv()
        for cp in first + passed:
            cp.wait_send()
        for cp in mine:
            cp.wait()

    outs = [jax.ShapeDtypeStruct((N_DEV,) + s.shape, s.dtype) for s in shards]
    return _comm_call(name, body, list(shards) + list(after), outs, (7 * n, 7 * n, n))


SEM_SPEC = pl.BlockSpec(memory_space=pltpu.SEMAPHORE)
SPLIT_COPY_EFFECT = pltpu.SideEffectType.DATAFLOW_SIDE_EFFECTING


def _in_hbm(a):
    return pltpu.with_memory_space_constraint(a, pltpu.HBM)


def _split_start(name, body, srcs, lands, n_sems):
    n = len(srcs)
    bufs = [_in_hbm(a) for a in list(srcs) + list(lands)]
    outs = pl.pallas_call(
        body, name=name,
        out_shape=(pltpu.SemaphoreType.DMA((n_sems,)), pltpu.SemaphoreType.DMA((n_sems,)),
                   *[pltpu.HBM(a.shape, a.dtype) for a in bufs], jax.ShapeDtypeStruct((8, LANES), F32)),
        in_specs=[HBM_ONLY] * (2 * n),
        out_specs=(SEM_SPEC, SEM_SPEC, *[HBM_ONLY] * (2 * n), pl.BlockSpec(memory_space=pltpu.VMEM)),
        input_output_aliases={i: 2 + i for i in range(2 * n)},
        compiler_params=pltpu.CompilerParams(has_side_effects=SPLIT_COPY_EFFECT))(*bufs)
    return outs[0], outs[1], list(outs[2:2 + n]), list(outs[2 + n:2 + 2 * n]), outs[-1]


def _split_wait(name, body, send_sems, recv_sems, srcs, lands, after):
    n = len(srcs)
    outs = pl.pallas_call(
        body, name=name,
        out_shape=[pltpu.HBM(a.shape, a.dtype) for a in list(srcs) + list(lands)],
        in_specs=[HBM_ONLY] * (2 * n) + [SEM_SPEC, SEM_SPEC, HBM_SPEC],
        out_specs=[HBM_ONLY] * (2 * n),
        input_output_aliases={i: i for i in range(2 * n)},
        compiler_params=pltpu.CompilerParams(has_side_effects=SPLIT_COPY_EFFECT))(
            *srcs, *lands, send_sems, recv_sems, after)
    return list(outs[:n]), list(outs[n:])


N_PEERS = N_DEV - 1


def _gather_peers():
    x, y, c = lax.axis_index("x"), lax.axis_index("y"), lax.axis_index("c")
    flips = [(fx, fy, fc) for fx in (0, 1) for fy in (0, 1) for fc in (0, 1) if fx or fy or fc]
    return [(1 - x if fx else x, 1 - y if fy else y, 1 - c if fc else c) for fx, fy, fc in flips]


def _block_id(dev):
    return 4 * dev[0] + 2 * dev[1] + dev[2]


def _landing_block(land_ref, shard_shape, side_by_side, dev):
    if not side_by_side:
        return land_ref.at[_block_id(dev)]
    cols = shard_shape[1]
    return land_ref.at[:, pl.ds(pl.multiple_of(_block_id(dev) * cols, LANES), cols)]


def _gather_start(name, shards, side_by_side):
    n = len(shards)

    def body(*refs):
        x_refs, land_refs = refs[:n], refs[n:2 * n]
        send_sems, recv_sems, token = refs[2 * n], refs[2 * n + 1], refs[-1]
        me = (lax.axis_index("x"), lax.axis_index("y"), lax.axis_index("c"))
        for i in range(n):
            for k, peer in enumerate(_gather_peers()):
                pltpu.make_async_remote_copy(
                    src_ref=x_refs[i], dst_ref=_landing_block(land_refs[i], shards[i].shape, side_by_side[i], me),
                    send_sem=send_sems.at[N_PEERS * i + k], recv_sem=recv_sems.at[N_PEERS * i + k],
                    device_id=peer, device_id_type=MESH).start()
            pltpu.make_async_copy(x_refs[i], _landing_block(land_refs[i], shards[i].shape, side_by_side[i], me),
                                  send_sems.at[N_PEERS * n + i]).start()
        token[...] = jnp.zeros_like(token)

    lands = [lax.empty((s.shape[0], N_DEV * s.shape[1]) if wide else (N_DEV,) + s.shape, s.dtype)
             for s, wide in zip(shards, side_by_side)]
    return _split_start(name, body, shards, lands, (N_PEERS + 1) * n)


def _gather_wait(name, send_sems, recv_sems, first, n_all, shards, lands, side_by_side, after):
    n = len(shards)

    def body(*refs):
        x_refs, land_refs = refs[:n], refs[n:2 * n]
        send_sems, recv_sems = refs[2 * n], refs[2 * n + 1]
        me = (lax.axis_index("x"), lax.axis_index("y"), lax.axis_index("c"))
        for i in range(n):
            pltpu.make_async_copy(x_refs[i], _landing_block(land_refs[i], shards[i].shape, side_by_side[i], me),
                                  send_sems.at[N_PEERS * n_all + first + i]).wait()
            for k, peer in enumerate(_gather_peers()):
                cp = pltpu.make_async_remote_copy(
                    src_ref=x_refs[i], dst_ref=_landing_block(land_refs[i], shards[i].shape, side_by_side[i], peer),
                    send_sem=send_sems.at[N_PEERS * (first + i) + k],
                    recv_sem=recv_sems.at[N_PEERS * (first + i) + k],
                    device_id=peer, device_id_type=MESH)
                cp.wait_send()
                cp.wait_recv()

    return _split_wait(name, body, send_sems, recv_sems, shards, lands, after)


def _scatter_start(name, blocks):
    n = len(blocks)

    def body(*refs):
        b_refs, land_refs = refs[:n], refs[n:2 * n]
        send_sems, recv_sems, token = refs[2 * n], refs[2 * n + 1], refs[-1]
        me = (lax.axis_index("x"), lax.axis_index("y"), lax.axis_index("c"))
        for i in range(n):
            for k, peer in enumerate(_gather_peers()):
                pltpu.make_async_remote_copy(
                    src_ref=b_refs[i].at[_block_id(peer)], dst_ref=land_refs[i].at[_block_id(me)],
                    send_sem=send_sems.at[N_PEERS * i + k], recv_sem=recv_sems.at[N_PEERS * i + k],
                    device_id=peer, device_id_type=MESH).start()
            pltpu.make_async_copy(b_refs[i].at[_block_id(me)], land_refs[i].at[_block_id(me)],
                                  send_sems.at[N_PEERS * n + i]).start()
        token[...] = jnp.zeros_like(token)

    lands = [lax.empty(b.shape, b.dtype) for b in blocks]
    return _split_start(name, body, blocks, lands, (N_PEERS + 1) * n)


def _scatter_wait(name, send_sems, recv_sems, blocks, lands, after):
    n = len(blocks)

    def body(*refs):
        b_refs, land_refs = refs[:n], refs[n:2 * n]
        send_sems, recv_sems = refs[2 * n], refs[2 * n + 1]
        me = (lax.axis_index("x"), lax.axis_index("y"), lax.axis_index("c"))
        for i in range(n):
            pltpu.make_async_copy(b_refs[i].at[_block_id(me)], land_refs[i].at[_block_id(me)],
                                  send_sems.at[N_PEERS * n + i]).wait()
            for k, peer in enumerate(_gather_peers()):
                cp = pltpu.make_async_remote_copy(
                    src_ref=b_refs[i].at[_block_id(peer)], dst_ref=land_refs[i].at[_block_id(peer)],
                    send_sem=send_sems.at[N_PEERS * i + k], recv_sem=recv_sems.at[N_PEERS * i + k],
                    device_id=peer, device_id_type=MESH)
                cp.wait_send()
                cp.wait_recv()

    return _split_wait(name, body, send_sems, recv_sems, blocks, lands, after)


def _adamw(w, g, m, v):
    m = ADAM_B1 * m + (1.0 - ADAM_B1) * g
    v = ADAM_B2 * v + (1.0 - ADAM_B2) * (g * g)
    m_hat = m / (1.0 - ADAM_B1 ** ADAM_STEP)
    v_hat = v / (1.0 - ADAM_B2 ** ADAM_STEP)
    delta = -ADAM_LR * (m_hat / (jnp.sqrt(v_hat) + ADAM_EPS) + ADAM_WD * w)
    return delta, m, v


def _adamw_tiles(r, c_):
    tr = _tile(r, 256, 16)
    return (tr, c_) if tr < r or r <= 256 else (r, _tile(c_, 256))


def _sum_parts(part):
    g = part[0].astype(F32)
    for k in range(1, part.shape[0]):
        g = g + part[k].astype(F32)
    return g


def _sum_adamw(name, parts, w, m, v):
    r, c_ = w.shape
    tr, tc = _adamw_tiles(r, c_)

    def body(p_ref, w_ref, m_ref, v_ref, g_ref, d_ref, nm_ref, nv_ref):
        g = _sum_parts(p_ref)
        g_ref[...] = g
        d_ref[...], nm_ref[...], nv_ref[...] = _adamw(w_ref[...], g, m_ref[...], v_ref[...])

    tile = pl.BlockSpec((tr, tc), lambda i, j: (i, j))
    return pl.pallas_call(body, name=name, grid=(r // tr, c_ // tc),
                          in_specs=[pl.BlockSpec((parts.shape[0], tr, tc), lambda i, j: (0, i, j)), tile, tile, tile],
                          out_specs=[tile] * 4, out_shape=[jax.ShapeDtypeStruct((r, c_), F32)] * 4,
                          compiler_params=_params("parallel", "parallel"))(parts, w, m, v)


def _sum_adamw_layers(name, parts, w, m, v):
    n_layers, r, c_ = w.shape
    tr = _tile(r, 256, 16)

    def body(*refs):
        p_refs = refs[:n_layers]
        w_ref, m_ref, v_ref, g_ref, d_ref, nm_ref, nv_ref = refs[n_layers:]
        layer = pl.program_id(0)
        g = _sum_parts(p_refs[0])
        for li in range(1, n_layers):
            g = jnp.where(layer == li, _sum_parts(p_refs[li]), g)
        g_ref[...] = g
        d_ref[...], nm_ref[...], nv_ref[...] = _adamw(w_ref[...], g, m_ref[...], v_ref[...])

    row = pl.BlockSpec((None, tr, c_), lambda l, i: (l, i, 0))
    specs = [pl.BlockSpec((p.shape[0], tr, c_), lambda l, i, li=li: (0, jnp.where(l == li, i, 0), 0))
             for li, p in enumerate(parts)]
    return pl.pallas_call(body, name=name, grid=(n_layers, r // tr), in_specs=specs + [row, row, row],
                          out_specs=[row] * 4, out_shape=[jax.ShapeDtypeStruct(w.shape, F32)] * 4,
                          compiler_params=_params("parallel", "parallel"))(*parts, w, m, v)


def _pack_rows(flat, n_rows, cols):
    pad = n_rows * cols - flat.shape[-1]
    flat = jnp.pad(flat, [(0, 0)] * (flat.ndim - 1) + [(0, pad)])
    return flat.reshape(flat.shape[:-1] + (n_rows, cols))


def _cols_split(full):
    c = full.shape[1] // N_DEV
    return jnp.stack([full[:, d * c:(d + 1) * c] for d in range(N_DEV)])


def _rows_join(blocks):
    return blocks.reshape(N_DEV * blocks.shape[1], blocks.shape[2])


def _rows_split(full):
    return full.reshape(N_DEV, full.shape[0] // N_DEV, full.shape[1])


def _heads_col(v, ng):
    return jnp.pad(v.reshape(ng, 1, SSD_HPG), ((0, 0), (0, 0), (0, LANES - SSD_HPG)))


MATRIX_ITEMS = ("w_in", "w_out", "up0", "down0", "w_qkv", "w_o", "up1", "down1")
VECTOR_ITEMS = ("conv_w", "b_qkv", "b_o")
ITEMS = MATRIX_ITEMS + VECTOR_ITEMS
GATHER_STAGES = (("w_in", "conv_w"), ("w_out", "up0", "down0"), ("w_qkv", "b_qkv", "w_o", "b_o", "up1", "down1"))
SIDE_BY_SIDE = ("conv_w", "up0", "up1", "b_o")


def _items(tree, prefix=""):
    g = lambda k: tree[prefix + k]
    return {"w_in": g("ssd_w_in")[0].T, "w_out": g("ssd_w_out")[0], "w_qkv": g("attn_w_qkv")[0].T,
            "w_o": g("attn_w_o")[0], "up0": g("mlp_w_up")[0], "up1": g("mlp_w_up")[1],
            "down0": g("mlp_w_down")[0], "down1": g("mlp_w_down")[1], "conv_w": g("ssd_conv_w")[0],
            "b_qkv": g("attn_b_qkv"), "b_o": g("attn_b_o")}


REPLICATED = ("ssd_conv_b", "ssd_dt_bias", "ssd_a_log", "ssd_d", "ssd_norm_w", "attn_sinks", "mix_pre_norm",
              "mix_post_norm", "ffn_pre_norm", "ffn_post_norm")
WEIGHTS = ("ssd_w_in", "ssd_conv_w", "ssd_conv_b", "ssd_dt_bias", "ssd_a_log", "ssd_d", "ssd_norm_w", "ssd_w_out",
           "attn_w_qkv", "attn_b_qkv", "attn_sinks", "attn_w_o", "attn_b_o", "mlp_w_up", "mlp_w_down",
           "mix_pre_norm", "mix_post_norm", "ffn_pre_norm", "ffn_post_norm")


def _forward_backward(x, target, rep, token, weights_of_stage, reduce_grads):
    t, d = x.shape
    ng = rep["ssd_norm_w"].shape[1] // GW
    di = ng * GW
    n_xbc = ng * GC
    nh = ng * SSD_HPG
    grads, blocks = {}, {}
    w_up, w_down = [None, None], [None, None]
    sinks_rep = jnp.repeat(rep["attn_sinks"].reshape(ATTN_N_KV, ATTN_REP, 1), ATTN_WINDOW, axis=2).reshape(
        ATTN_N_KV, 1, ATTN_REP * ATTN_WINDOW)
    conv_b = rep["ssd_conv_b"]
    gn = ng * SSD_D_STATE
    parts = ((0, di), (di, di), (2 * di, gn), (2 * di + gn, gn), (di + n_xbc, nh))
    alog_c, dsk_c = (_heads_col(rep[k], ng) for k in ("ssd_a_log", "ssd_d"))
    bias_l, alog_l = (jnp.pad(rep[k], ((0, 0), (0, LANES - nh))) for k in ("ssd_dt_bias", "ssd_a_log"))
    norm = {k: rep[k] for k in ("mix_pre_norm", "mix_post_norm", "ffn_pre_norm", "ffn_post_norm")}

    def nrow(name, i):
        return norm[name][i:i + 1]

    def mlp_fwd(i, u2):
        p = _mm(f"mlp{i}_up", [u2], [w_up[i]], "nn", tm=1024, tn=1024, out_dtypes=(BF16,),
                epilogue=lambda acc: (jnp.square(jnp.maximum(acc, 0.0)),))
        f = _mm(f"mlp{i}_down", [p], [w_down[i]], "nn", tm=512, tn=1024)
        return p, f

    def mlp_bwd(i, df, u2, p):
        da = _mm(f"mlp{i}_dact", [df], [w_down[i]], "nt", tm=1024, tn=1024, out_dtypes=(BF16,),
                 tiles=(p,), epilogue=lambda acc, pv: (acc * (2.0 * jnp.sqrt(pv.astype(F32))),))
        blocks[f"down{i}"] = _rows_split(_mm(f"mlp{i}_dwdown", [p], [df], "tn", tm=512, tn=1024,
                                             out_dtypes=(PAYLOAD,)))
        blocks[f"up{i}"] = _mm(f"mlp{i}_dwup", [u2], [da], "tn", tm=1024, tn=da.shape[1] // N_DEV,
                               out_dtypes=(PAYLOAD,), col_blocks=True)
        return _mm(f"mlp{i}_dx", [da], [w_up[i]], "nt", tm=512, tn=1024)

    u0 = _prenorm("l0_prenorm", x, nrow("mix_pre_norm", 0), token)
    got = weights_of_stage(0, u0)
    w_in_t = _rows_join(got["w_in"])
    w_dt_t = jnp.pad(w_in_t[di + n_xbc:], ((0, LANES - nh), (0, 0)))
    conv_w = got["conv_w"]
    zx = _mm("ssd_in_proj", [u0], [w_in_t], "nt", tm=2048, tn=1024, n_use=di + n_xbc)
    zdt = _mm("ssd_dt_proj", [u0], [w_dt_t], "nt", tm=1024, tn=LANES)
    pre = _conv_fwd(zx, di, n_xbc, conv_w, conv_b)
    dt_c, cum_c, cum_r, sgd_c = _ssd_dt_prep(zdt, bias_l, alog_l, ng)
    y, states = _ssd_fwd(pre, dt_c, cum_c, cum_r, alog_c, dsk_c)
    yn = _gate_norm_fwd(y, zx, rep["ssd_norm_w"])
    got = weights_of_stage(1, yn)
    w_out = _rows_join(got["w_out"])
    w_up[0], w_down[0] = got["up0"], _rows_join(got["down0"])
    mix0 = _mm("ssd_out_proj", [yn], [w_out], "nn", tm=1024, tn=1024)
    h1, u0f = _post_pre("l0_mid", x, mix0, nrow("mix_post_norm", 0), nrow("ffn_pre_norm", 0))
    p0, f0 = mlp_fwd(0, u0f)
    h2, u1 = _post_pre("l1_in", h1, f0, nrow("ffn_post_norm", 0), nrow("mix_pre_norm", 1))
    got = weights_of_stage(2, u1)
    w_qkv_t = _rows_join(got["w_qkv"])
    w_o = _rows_join(got["w_o"])
    b_qkv_col = got["b_qkv"].reshape(-1, 1)
    b_o = got["b_o"]
    w_up[1], w_down[1] = got["up1"], _rows_join(got["down1"])
    qkv_t = _mm("attn_qkv_proj", [w_qkv_t], [u1], "nt", tm=768, tn=1024, out_dtypes=(BF16,), cols=(b_qkv_col,),
                epilogue=lambda acc, b: (acc + b,))
    ao_t = _attn_fwd_t(qkv_t, sinks_rep)
    mix1 = _mm("attn_out_proj", [ao_t], [w_o], "tn", tm=1024, tn=1024, rows=(b_o,),
               epilogue=lambda acc, b: (acc + b,))
    h3, u1f = _post_pre("l1_mid", h2, mix1, nrow("mix_post_norm", 1), nrow("ffn_pre_norm", 1))
    p1, f1 = mlp_fwd(1, u1f)
    dh, loss_row = _final_loss("loss", h3, f1, nrow("ffn_post_norm", 1), target)

    g_norm = {k: [None, None] for k in norm}
    df1, g_norm["ffn_post_norm"][1], _ = _norm_bwd("l1_ffn_post_bwd", dh, post=(f1, nrow("ffn_post_norm", 1)))
    du = mlp_bwd(1, df1, u1f, p1)
    sent = reduce_grads("mlp1", {k: blocks[k] for k in ("up1", "down1")})
    dh, g_norm["ffn_pre_norm"][1], dmix1, g_norm["mix_post_norm"][1], db_o = _norm_bwd(
        "l1_mid_bwd", dh, pre=(du, h3, nrow("ffn_pre_norm", 1)), post=(mix1, nrow("mix_post_norm", 1)), after=sent)
    blocks["b_o"] = _cols_split(db_o)
    blocks["w_o"] = _rows_split(_mm("attn_dwo", [ao_t], [dmix1], "nn", tm=512, tn=1024, out_dtypes=(PAYLOAD,)))
    dao_t = _mm("attn_dout", [w_o], [dmix1], "nt", tm=1024, tn=1024, out_dtypes=(BF16,))
    dqkv_t, db_qkv, grads["attn_sinks"] = _attn_bwd_t(qkv_t, dao_t, sinks_rep)
    blocks["b_qkv"] = db_qkv.reshape(N_DEV, 1, -1)
    blocks["w_qkv"] = _rows_split(_mm("attn_dwqkv", [dqkv_t], [u1], "nn", tm=512, tn=1024, out_dtypes=(PAYLOAD,)))
    du = _mm("attn_dx", [dqkv_t], [w_qkv_t], "tn", tm=1024, tn=1024)
    sent = reduce_grads("attn", {k: blocks[k] for k in ("w_o", "w_qkv", "b_o", "b_qkv")})
    dh, g_norm["mix_pre_norm"][1], df0, g_norm["ffn_post_norm"][0], _ = _norm_bwd(
        "l1_in_bwd", dh, pre=(du, h2, nrow("mix_pre_norm", 1)), post=(f0, nrow("ffn_post_norm", 0)), after=sent)
    du = mlp_bwd(0, df0, u0f, p0)
    sent = reduce_grads("mlp0", {k: blocks[k] for k in ("up0", "down0")})
    dh, g_norm["ffn_pre_norm"][0], dmix0, g_norm["mix_post_norm"][0], _ = _norm_bwd(
        "l0_mid_bwd", dh, pre=(du, h1, nrow("ffn_pre_norm", 0)), post=(mix0, nrow("mix_post_norm", 0)), after=sent)
    blocks["w_out"] = _rows_split(_mm("ssd_dwout", [yn], [dmix0], "tn", tm=512, tn=1024, out_dtypes=(PAYLOAD,)))
    dyn = _mm("ssd_dyn", [dmix0], [w_out], "nt", tm=1024, tn=1024)
    sent = reduce_grads("ssdout", {"w_out": blocks["w_out"]})
    dy, dz, grads["ssd_norm_w"] = _gate_norm_bwd(dyn, y, zx, rep["ssd_norm_w"], sent)
    dpx, dpb, dpc, ddt_g, dbias_g, dalog_g, dd_g = _ssd_bwd(dy, pre, states, dt_c, cum_c, cum_r, sgd_c, alog_c,
                                                             dsk_c)
    conv_out = [_conv_bwd(f"ssd_conv_bwd_{tag}", dp, zx, c0, conv_w[:, c0 - di:c0 - di + n])
                for tag, dp, (c0, n) in zip("xbc", (dpx, dpb, dpc), parts[1:4])]
    dconv_w = jnp.concatenate([o[1] for o in conv_out], axis=1)
    dconv_b = jnp.concatenate([o[2] for o in conv_out], axis=1)
    ddt = jnp.transpose(ddt_g[:, :, :SSD_HPG], (1, 0, 2)).reshape(t, nh)
    ddt = jnp.pad(ddt, ((0, 0), (0, LANES - nh))).astype(BF16)
    blocks["conv_w"] = _cols_split(dconv_w)
    grads["ssd_conv_b"] = dconv_b
    for name, val in (("ssd_dt_bias", dbias_g), ("ssd_a_log", dalog_g), ("ssd_d", dd_g)):
        grads[name] = val[:, 0, :SSD_HPG].reshape(1, nh)
    d_zx = [dz] + [o[0] for o in conv_out] + [ddt]
    dw_parts = [_mm(f"ssd_dw_{tag}", [d], [u0], "tn", tm=512, tn=1024, out_dtypes=(PAYLOAD,))
                for tag, d in zip("zxbct", d_zx)]
    dw_parts[-1] = dw_parts[-1][:nh]
    blocks["w_in"] = _rows_split(jnp.concatenate(dw_parts, axis=0))
    sent = reduce_grads("ssd", {k: blocks[k] for k in ("w_in", "conv_w")})
    w_parts = [w_in_t[r0:r0 + n] for r0, n in parts[:-1]] + [w_dt_t]
    du = _mm("ssd_dx", d_zx, w_parts, "nn", tm=256, tn=1024, after=sent)
    grad_x, g_norm["mix_pre_norm"][0] = _norm_bwd("l0_in_bwd", dh, pre=(du, x, nrow("mix_pre_norm", 0)), after=sent)
    for k in norm:
        grads[k] = jnp.concatenate(g_norm[k], axis=0)
    return loss_row, grad_x, grads


def kernel(x, ssd_w_in, ssd_conv_w, ssd_conv_b, ssd_dt_bias, ssd_a_log, ssd_d, ssd_norm_w, ssd_w_out, attn_w_qkv, attn_b_qkv, attn_sinks, attn_w_o, attn_b_o, mlp_w_up, mlp_w_down, mix_pre_norm, mix_post_norm, ffn_pre_norm, ffn_post_norm, loss_target, m_ssd_w_in, m_ssd_conv_w, m_ssd_conv_b, m_ssd_dt_bias, m_ssd_a_log, m_ssd_d, m_ssd_norm_w, m_ssd_w_out, m_attn_w_qkv, m_attn_b_qkv, m_attn_sinks, m_attn_w_o, m_attn_b_o, m_mlp_w_up, m_mlp_w_down, m_mix_pre_norm, m_mix_post_norm, m_ffn_pre_norm, m_ffn_post_norm, v_ssd_w_in, v_ssd_conv_w, v_ssd_conv_b, v_ssd_dt_bias, v_ssd_a_log, v_ssd_d, v_ssd_norm_w, v_ssd_w_out, v_attn_w_qkv, v_attn_b_qkv, v_attn_sinks, v_attn_w_o, v_attn_b_o, v_mlp_w_up, v_mlp_w_down, v_mix_pre_norm, v_mix_post_norm, v_ffn_pre_norm, v_ffn_post_norm):
    given = dict(locals())
    w = {k: given[k] for k in WEIGHTS}
    mom_m = {k: given["m_" + k] for k in WEIGHTS}
    mom_v = {k: given["v_" + k] for k in WEIGHTS}
    w_it, m_it, v_it = _items(given), _items(given, "m_"), _items(given, "v_")

    order = [k for stage in GATHER_STAGES for k in stage]
    shards = [w_it[k].astype(PAYLOAD) if k in MATRIX_ITEMS else w_it[k] for k in order]
    wide = [k in SIDE_BY_SIDE for k in order]
    g_send, g_recv, shards, lands, token = _gather_start("gather_start", shards, wide)

    def weights_of_stage(s, after):
        first = sum(len(stage) for stage in GATHER_STAGES[:s])
        sl = slice(first, first + len(GATHER_STAGES[s]))
        _, got = _gather_wait(f"gather_wait{s}", g_send, g_recv, first, len(order), shards[sl], lands[sl], wide[sl],
                              after)
        return dict(zip(GATHER_STAGES[s], got))

    in_flight = []

    def reduce_grads(tag, blocks):
        keys = list(blocks)
        started = _scatter_start(f"rs_start_{tag}", [blocks[k] for k in keys])
        in_flight.append((tag, keys, started))
        return started[-1]

    rep = {k: w[k] for k in REPLICATED}
    loss_row, grad_x, grads = _forward_backward(x[0], loss_target[0], rep, token, weights_of_stage, reduce_grads)

    def pack_rep(tree, last):
        flat = jnp.concatenate([tree[k].reshape(-1) for k in REPLICATED] + [last])
        return _pack_rows(flat, _round_up(-(-flat.shape[0] // LANES), 8), LANES)

    landed = {}

    def wait_group(group, after):
        tag, keys, (s_send, s_recv, srcs, s_lands, _) = group
        _, got = _scatter_wait(f"rs_wait_{tag}", s_send, s_recv, srcs, s_lands, after)
        landed.update(zip(keys, got))

    def adamw_item(k):
        return _sum_adamw(f"adamw_{k}", landed[k], w_it[k], m_it[k], v_it[k])

    def adamw_stack(name, keys):
        return _sum_adamw_layers(f"adamw_{name}", [landed[k] for k in keys], given[name], given["m_" + name],
                                 given["v_" + name])

    for group in in_flight[:-1]:
        wait_group(group, grad_x)
    done = {"mlp_w_up": adamw_stack("mlp_w_up", ("up0", "up1")),
            "mlp_w_down": adamw_stack("mlp_w_down", ("down0", "down1")),
            "attn_w_qkv": [o.T[None] for o in adamw_item("w_qkv")],
            "attn_w_o": [o[None] for o in adamw_item("w_o")],
            "attn_b_qkv": adamw_item("b_qkv"), "attn_b_o": adamw_item("b_o"),
            "ssd_w_out": [o[None] for o in adamw_item("w_out")]}
    partials, = _all_gather("gather_small_grads", [pack_rep(grads, loss_row[0, :1])],
                            [outs4[0] for outs4 in done.values()])
    wait_group(in_flight[-1], partials)
    done["ssd_w_in"] = [o.T[None] for o in adamw_item("w_in")]
    done["ssd_conv_w"] = [o[None] for o in adamw_item("conv_w")]
    zero = jnp.zeros((1,), F32)
    rep_out = _sum_adamw("adamw_replicated", partials, pack_rep(w, zero), pack_rep(mom_m, zero), pack_rep(mom_v, zero))

    kinds = []
    for kind, r_arr in enumerate(rep_out):
        tree = {name: outs4[kind] for name, outs4 in done.items()}
        flat, off = r_arr.reshape(-1), 0
        for k in REPLICATED:
            tree[k] = flat[off:off + w[k].size].reshape(w[k].shape)
            off += w[k].size
        kinds.append(tree)
    loss = rep_out[0].reshape(-1)[off]
    outs = [loss, grad_x[None]]
    for tree in kinds:
        outs += [tree[k] for k in WEIGHTS]
    return tuple(outs)
```

```python
import jax
import jax.numpy as jnp
from jax import lax
from jax.experimental import pallas as pl
from jax.experimental.pallas import tpu as pltpu

F32 = jnp.float32
BF16 = jnp.bfloat16
PAYLOAD = jnp.bfloat16
HIGHEST = lax.Precision.HIGHEST
MESH = pl.DeviceIdType.MESH

NORM_EPS = 1e-6
SSD_HEAD_DIM = 64
SSD_HPG = 4
SSD_D_STATE = 128
SSD_CONV_WIDTH = 4
SSD_CHUNK = 128
ATTN_HEAD_DIM = 64
ATTN_N_KV = 4
ATTN_REP = 4
ATTN_WINDOW = 128
ADAM_LR = 0.001
ADAM_B1 = 0.9
ADAM_B2 = 0.999
ADAM_EPS = 1e-08
ADAM_WD = 0.01
ADAM_STEP = 10

N_DEV = 8
LANES = 128
V7X_VMEM_LIMIT = 56 * 1024 * 1024

GW = SSD_HPG * SSD_HEAD_DIM
GC = GW + 2 * SSD_D_STATE
assert SSD_CHUNK == LANES


def _params(*sem):
    return pltpu.CompilerParams(dimension_semantics=sem, vmem_limit_bytes=V7X_VMEM_LIMIT)


def _tile(n, pref, mult=LANES):
    best = None
    t = mult
    while t <= min(n, pref):
        if n % t == 0:
            best = t
        t += mult
    return best if best is not None else n


def _round_up(n, m):
    return (n + m - 1) // m * m


def _acc(ref, val, first):
    @pl.when(first)
    def _():
        ref[...] = val

    @pl.when(jnp.logical_not(first))
    def _():
        ref[...] += val


def _dot(a, b):
    return lax.dot_general(a, b, (((1,), (0,)), ((), ())), preferred_element_type=F32)


def _dot_nt(a, b):
    return lax.dot_general(a, b, (((1,), (1,)), ((), ())), preferred_element_type=F32)


def _dot_tn(a, b):
    return lax.dot_general(a, b, (((0,), (0,)), ((), ())), preferred_element_type=F32)


def _dot_f32(a, b):
    return lax.dot_general(a, b, (((1,), (0,)), ((), ())), preferred_element_type=F32, precision=HIGHEST)


_DOTS = {"nn": _dot, "nt": _dot_nt, "tn": _dot_tn}


def _sigmoid(x):
    return 1.0 / (1.0 + jnp.exp(-x))


def _softplus(x):
    return jnp.maximum(x, 0.0) + jnp.log1p(jnp.exp(-jnp.abs(x)))


def _silu_grad(x, s):
    return s * (1.0 + x * (1.0 - s))


def _mm(name, a_list, b_list, mode, *, tm, tn, out_dtypes=(F32,), epilogue=None, tiles=(), rows=(), cols=(),
        col_blocks=False, n_use=None, after=None):
    npair = len(a_list)
    if mode == "tn":
        m = a_list[0].shape[1]
    else:
        m = a_list[0].shape[0]
    n = n_use if n_use is not None else (b_list[0].shape[0] if mode == "nt" else b_list[0].shape[1])
    tm = _tile(m, tm, LANES if mode == "tn" else 8)
    tn = _tile(n, tn)
    assert m % tm == 0 and n % tn == 0, (name, m, n, tm, tn)
    dot = _DOTS[mode]

    def body(*refs):
        a_refs = refs[:npair]
        b_refs = refs[npair:2 * npair]
        n_extra = len(tiles) + len(rows) + len(cols)
        e_refs = refs[2 * npair:2 * npair + n_extra]
        o_refs = refs[2 * npair + n_extra + len(order):]
        acc = None
        for ar, br in zip(a_refs, b_refs):
            d = dot(ar[...], br[...])
            acc = d if acc is None else acc + d
        outs = epilogue(acc, *[e[...] for e in e_refs]) if epilogue is not None else (acc,)
        for o, v in zip(o_refs, outs):
            o[...] = v.astype(o.dtype)

    in_specs = []
    for a in a_list:
        if mode == "tn":
            in_specs.append(pl.BlockSpec((a.shape[0], tm), lambda i, j: (0, i)))
        else:
            in_specs.append(pl.BlockSpec((tm, a.shape[1]), lambda i, j: (i, 0)))
    for b in b_list:
        if mode == "nt":
            in_specs.append(pl.BlockSpec((tn, b.shape[1]), lambda i, j: (j, 0)))
        else:
            in_specs.append(pl.BlockSpec((b.shape[0], tn), lambda i, j: (0, j)))
    in_specs += [pl.BlockSpec((tm, tn), lambda i, j: (i, j)) for _ in tiles]
    in_specs += [pl.BlockSpec((1, tn), lambda i, j: (0, j)) for _ in rows]
    in_specs += [pl.BlockSpec((tm, 1), lambda i, j: (i, 0)) for _ in cols]
    order = [] if after is None else [after]
    in_specs += [pl.BlockSpec((8, LANES), lambda i, j: (0, 0)) for _ in order]
    outs = pl.pallas_call(
        body,
        name=name,
        grid=(m // tm, n // tn),
        in_specs=in_specs,
        out_specs=[pl.BlockSpec((None, tm, tn), lambda i, j: (j, i, 0)) if col_blocks else
                   pl.BlockSpec((tm, tn), lambda i, j: (i, j)) for _ in out_dtypes],
        out_shape=[jax.ShapeDtypeStruct((n // tn, m, tn) if col_blocks else (m, n), dt) for dt in out_dtypes],
        compiler_params=_params("parallel", "parallel"),
    )(*a_list, *b_list, *tiles, *rows, *cols, *order)
    return outs[0] if len(out_dtypes) == 1 else outs


def _rms(x, w):
    r = lax.rsqrt(jnp.mean(x * x, axis=-1, keepdims=True) + NORM_EPS)
    return x * r * w


def _rms_bwd(x, w, dy):
    r = lax.rsqrt(jnp.mean(x * x, axis=-1, keepdims=True) + NORM_EPS)
    xh = x * r
    g = dy * w
    dx = r * (g - xh * jnp.mean(g * xh, axis=-1, keepdims=True))
    return dx, dy * xh


def _row_specs(tr, d):
    return pl.BlockSpec((tr, d), lambda i: (i, 0)), pl.BlockSpec((1, d), lambda i: (0, 0))


def _prenorm(name, h, w, after):
    t, d = h.shape
    tr = _tile(t, 512, 8)
    row, vec = _row_specs(tr, d)

    def body(h_ref, w_ref, after_ref, u_ref):
        u_ref[...] = _rms(h_ref[...], w_ref[...]).astype(BF16)

    return pl.pallas_call(body, name=name, grid=(t // tr,),
                          in_specs=[row, vec, pl.BlockSpec((8, LANES), lambda i: (0, 0))], out_specs=row,
                          out_shape=jax.ShapeDtypeStruct((t, d), BF16), compiler_params=_params("parallel"))(
                              h, w, after)


def _post_pre(name, h, m, w_post, w_pre):
    t, d = h.shape
    tr = _tile(t, 512, 8)
    row, vec = _row_specs(tr, d)

    def body(h_ref, m_ref, wq_ref, wp_ref, hn_ref, u_ref):
        hn = h_ref[...] + _rms(m_ref[...], wq_ref[...])
        hn_ref[...] = hn
        u_ref[...] = _rms(hn, wp_ref[...]).astype(BF16)

    return pl.pallas_call(body, name=name, grid=(t // tr,), in_specs=[row, row, vec, vec], out_specs=[row, row],
                          out_shape=[jax.ShapeDtypeStruct((t, d), F32), jax.ShapeDtypeStruct((t, d), BF16)],
                          compiler_params=_params("parallel"))(h, m, w_post, w_pre)


def _final_loss(name, h, m, w_post, target):
    t, d = h.shape
    tr = _tile(t, 512, 8)
    row, vec = _row_specs(tr, d)

    def body(h_ref, m_ref, wq_ref, t_ref, dh_ref, loss_ref):
        err = h_ref[...] + _rms(m_ref[...], wq_ref[...]) - t_ref[...]
        dh_ref[...] = err * (1.0 / d)
        part = 0.5 * jnp.sum(jnp.mean(err * err, axis=-1, keepdims=True), axis=0, keepdims=True)
        _acc(loss_ref, jnp.broadcast_to(part, (1, LANES)), pl.program_id(0) == 0)

    return pl.pallas_call(body, name=name, grid=(t // tr,), in_specs=[row, row, vec, row],
                          out_specs=[row, pl.BlockSpec((1, LANES), lambda i: (0, 0))],
                          out_shape=[jax.ShapeDtypeStruct((t, d), F32), jax.ShapeDtypeStruct((1, LANES), F32)],
                          compiler_params=_params("arbitrary"))(h, m, w_post, target)


def _norm_bwd(name, dh, pre=None, post=None, after=None):
    t, d = dh.shape
    tr = _tile(t, 512, 8)
    row, vec = _row_specs(tr, d)
    has_pre, has_post = pre is not None, post is not None

    def body(*refs):
        it = iter(refs)
        dh_ref = next(it)
        if has_pre:
            du_ref, x_ref, wp_ref = next(it), next(it), next(it)
        if has_post:
            m_ref, wq_ref = next(it), next(it)
        if after is not None:
            next(it)
        first = pl.program_id(0) == 0
        dh_v = dh_ref[...]
        if has_pre:
            dhn_ref, dwp_ref = next(it), next(it)
            dx, dwr = _rms_bwd(x_ref[...], wp_ref[...], du_ref[...])
            dh_v = dh_v + dx
            dhn_ref[...] = dh_v
            _acc(dwp_ref, jnp.sum(dwr, axis=0, keepdims=True), first)
        if has_post:
            dm_ref, dwq_ref, dms_ref = next(it), next(it), next(it)
            dm, dwr = _rms_bwd(m_ref[...], wq_ref[...], dh_v)
            dm_ref[...] = dm.astype(BF16)
            _acc(dwq_ref, jnp.sum(dwr, axis=0, keepdims=True), first)
            _acc(dms_ref, jnp.sum(dm, axis=0, keepdims=True), first)

    ins, in_specs, out_specs, out_shape = [dh], [row], [], []
    if has_pre:
        ins += list(pre)
        in_specs += [row, row, vec]
        out_specs += [row, vec]
        out_shape += [jax.ShapeDtypeStruct((t, d), F32), jax.ShapeDtypeStruct((1, d), F32)]
    if has_post:
        ins += list(post)
        in_specs += [row, vec]
        out_specs += [row, vec, vec]
        out_shape += [jax.ShapeDtypeStruct((t, d), BF16), jax.ShapeDtypeStruct((1, d), F32),
                      jax.ShapeDtypeStruct((1, d), F32)]
    if after is not None:
        ins.append(after)
        in_specs.append(pl.BlockSpec((8, LANES), lambda i: (0, 0)))
    return pl.pallas_call(body, name=name, grid=(t // tr,), in_specs=in_specs, out_specs=out_specs,
                          out_shape=out_shape, compiler_params=_params("arbitrary"))(*ins)


HALO = 8


def _shift_later(cur, prev, s):
    rolled = pltpu.roll(cur, s, 0)
    row = lax.broadcasted_iota(jnp.int32, prev.shape, 0)
    first = jnp.where(row < s, pltpu.roll(prev, s, 0), rolled[0:HALO])
    return jnp.concatenate([first, rolled[HALO:]], axis=0)


def _shift_earlier(cur, nxt, s):
    tt = cur.shape[0]
    rolled = pltpu.roll(cur, tt - s, 0)
    row = lax.broadcasted_iota(jnp.int32, nxt.shape, 0)
    last = jnp.where(row >= HALO - s, pltpu.roll(nxt, HALO - s, 0), rolled[tt - HALO:])
    return jnp.concatenate([rolled[:tt - HALO], last], axis=0)


def _conv_fwd(zx, col0, n_ch, conv_w, conv_b):
    t = zx.shape[0]
    tc = _tile(n_ch, 512)
    tt = _tile(t, 1024, 8)
    cb0 = col0 // tc
    assert col0 % tc == 0
    kw = SSD_CONV_WIDTH

    def body(x_ref, p_ref, w_ref, b_ref, o_ref):
        cur = x_ref[...]
        prev = jnp.where(pl.program_id(1) > 0, p_ref[...], 0.0)
        w = w_ref[...]
        acc = b_ref[...] + w[kw - 1:kw, :] * cur
        for k in range(kw - 1):
            acc = acc + w[k:k + 1, :] * _shift_later(cur, prev, kw - 1 - k)
        o_ref[...] = acc

    return pl.pallas_call(
        body, name="ssd_conv_fwd", grid=(n_ch // tc, t // tt),
        in_specs=[pl.BlockSpec((tt, tc), lambda j, i: (i, cb0 + j)),
                  pl.BlockSpec((HALO, tc), lambda j, i: (jnp.maximum(i * (tt // HALO) - 1, 0), cb0 + j)),
                  pl.BlockSpec((kw, tc), lambda j, i: (0, j)),
                  pl.BlockSpec((1, tc), lambda j, i: (0, j))],
        out_specs=pl.BlockSpec((tt, tc), lambda j, i: (i, j)),
        out_shape=jax.ShapeDtypeStruct((t, n_ch), F32),
        compiler_params=_params("parallel", "parallel"))(zx, zx, conv_w, conv_b)


def _conv_bwd(name, dpre, zx, col0, conv_w):
    t, n_ch = dpre.shape
    tc = _tile(n_ch, 512)
    tt = _tile(t, 1024, 8)
    cb0 = col0 // tc
    kw = SSD_CONV_WIDTH
    nt = t // tt

    def body(d_ref, dn_ref, x_ref, p_ref, w_ref, dx_ref, dw_ref, db_ref):
        i = pl.program_id(1)
        d = d_ref[...]
        d_next = jnp.where(i < nt - 1, dn_ref[...], 0.0)
        x = x_ref[...]
        x_prev = jnp.where(i > 0, p_ref[...], 0.0)
        w = w_ref[...]
        dx = w[kw - 1:kw, :] * d
        for k in range(kw - 1):
            dx = dx + w[k:k + 1, :] * _shift_earlier(d, d_next, kw - 1 - k)
        dx_ref[...] = dx.astype(BF16)
        first = i == 0
        for k in range(kw):
            xs = x if k == kw - 1 else _shift_later(x, x_prev, kw - 1 - k)
            val = jnp.sum(d * xs, axis=0, keepdims=True)

            @pl.when(first)
            def _():
                dw_ref[k:k + 1, :] = val

            @pl.when(jnp.logical_not(first))
            def _():
                dw_ref[k:k + 1, :] += val
        _acc(db_ref, jnp.sum(d, axis=0, keepdims=True), first)

    return pl.pallas_call(
        body, name=name, grid=(n_ch // tc, nt),
        in_specs=[pl.BlockSpec((tt, tc), lambda j, i: (i, j)),
                  pl.BlockSpec((HALO, tc), lambda j, i: (jnp.minimum((i + 1) * (tt // HALO), t // HALO - 1), j)),
                  pl.BlockSpec((tt, tc), lambda j, i: (i, cb0 + j)),
                  pl.BlockSpec((HALO, tc), lambda j, i: (jnp.maximum(i * (tt // HALO) - 1, 0), cb0 + j)),
                  pl.BlockSpec((kw, tc), lambda j, i: (0, j))],
        out_specs=[pl.BlockSpec((tt, tc), lambda j, i: (i, j)),
                   pl.BlockSpec((kw, tc), lambda j, i: (0, j)),
                   pl.BlockSpec((1, tc), lambda j, i: (0, j))],
        out_shape=[jax.ShapeDtypeStruct((t, n_ch), BF16), jax.ShapeDtypeStruct((kw, n_ch), F32),
                   jax.ShapeDtypeStruct((1, n_ch), F32)],
        compiler_params=_params("parallel", "arbitrary"))(dpre, dpre, zx, zx, conv_w)


def _head_of_lane(shape, width):
    return lax.broadcasted_iota(jnp.int32, shape, len(shape) - 1) // width


def _select_dot(v, pick, pick_first=False):
    hi = v.astype(BF16)
    lo = (v - hi.astype(F32)).astype(BF16)
    return _dot(pick, hi) + _dot(pick, lo) if pick_first else _dot(hi, pick) + _dot(lo, pick)


def _expand(v, n_rows, on_mxu=False):
    if not on_mxu:
        head = _head_of_lane((n_rows, GW), SSD_HEAD_DIM)
        out = jnp.zeros((n_rows, GW), F32)
        for j in range(SSD_HPG):
            out = jnp.where(head == j, v[:, j:j + 1], out)
        return out
    src = lax.broadcasted_iota(jnp.int32, (LANES, GW), 0)
    return _select_dot(v, (src == _head_of_lane((LANES, GW), SSD_HEAD_DIM)).astype(BF16))


def _contract(v, n_rows, on_mxu=False):
    if not on_mxu:
        head = _head_of_lane((n_rows, GW), SSD_HEAD_DIM)
        lane = lax.broadcasted_iota(jnp.int32, (n_rows, LANES), 1)
        out = jnp.zeros((n_rows, LANES), F32)
        for j in range(SSD_HPG):
            s = jnp.sum(jnp.where(head == j, v, 0.0), axis=1, keepdims=True)
            out = jnp.where(lane == j, s, out)
        return out
    dst = lax.broadcasted_iota(jnp.int32, (GW, LANES), 1)
    return _select_dot(v, (lax.broadcasted_iota(jnp.int32, (GW, LANES), 0) // SSD_HEAD_DIM == dst).astype(BF16))


def _ssd_dt_prep(zdt, bias, alog, ng):
    t = zdt.shape[0]
    q = SSD_CHUNK

    def body(z_ref, b_ref, a_ref, dt_ref, cum_ref, cumr_ref, sg_ref):
        raw = z_ref[...] + b_ref[...]
        dt = _softplus(raw)
        sgd = _sigmoid(raw)
        row = lax.broadcasted_iota(jnp.int32, (q, q), 0)
        col = lax.broadcasted_iota(jnp.int32, (q, q), 1)
        cum = _dot_f32((col <= row).astype(F32), dt * (-jnp.exp(a_ref[...])))
        cum_t = cum.T
        lane = lax.broadcasted_iota(jnp.int32, (q, LANES), 1)
        for g in range(ng):
            shift = (LANES - g * SSD_HPG) % LANES

            def group(v):
                return jnp.where(lane < SSD_HPG, pltpu.roll(v, shift, 1) if shift else v, 0.0)

            dt_ref[g] = group(dt)
            cum_ref[g] = group(cum)
            sg_ref[g] = group(sgd)
            cumr_ref[g] = (pltpu.roll(cum_t, shift, 0) if shift else cum_t)[0:8, :]

    cols = pl.BlockSpec((ng, q, LANES), lambda c: (0, c, 0))
    vec = pl.BlockSpec((1, LANES), lambda c: (0, 0))
    col_shape = jax.ShapeDtypeStruct((ng, t, LANES), F32)
    return pl.pallas_call(body, name="ssd_dt_prep", grid=(t // q,),
                          in_specs=[pl.BlockSpec((q, LANES), lambda c: (c, 0)), vec, vec],
                          out_specs=[cols, cols, pl.BlockSpec((ng, 8, q), lambda c: (0, 0, c)), cols],
                          out_shape=[col_shape, col_shape, jax.ShapeDtypeStruct((ng, 8, t), F32), col_shape],
                          compiler_params=_params("parallel"))(zdt, bias, alog)


def _ssd_common(pre, dt, cum, cum_r, alog_c, on_mxu):
    q = SSD_CHUNK
    sg = _sigmoid(pre)
    act = pre * sg
    xa = act[:, :GW]
    bm = act[:, GW:GW + SSD_D_STATE].astype(BF16)
    cm = act[:, GW + SSD_D_STATE:].astype(BF16)
    row = lax.broadcasted_iota(jnp.int32, (q, q), 0)
    col = lax.broadcasted_iota(jnp.int32, (q, q), 1)
    tril = col <= row
    a_c = -jnp.exp(alog_c)
    g = _dot_nt(cm, bm)
    dt_x = _expand(dt, q, on_mxu)
    xdt = xa * dt_x
    cl = cum[q - 1:q, :]
    e_c = jnp.exp(cl - cum)
    lam_c = jnp.exp(cum)
    return dict(sg=sg, xa=xa, bm=bm, cm=cm, tril=tril, row=row, col=col, dt=dt, a_c=a_c, cum=cum, cum_r=cum_r,
                g=g, dt_x=dt_x, xdt=xdt, cl=cl, e_c=e_c, lam_c=lam_c)


SSD_GPS_FWD = 8
SSD_GPS_BWD = 2


def _ssd_specs(nc, rev, ng, gps):
    q = SSD_CHUNK
    xw, nw = gps * GW, gps * SSD_D_STATE
    b_off = ng * GW // nw
    c_off = (ng * GW + ng * SSD_D_STATE) // nw
    assert ng % gps == 0 and (ng * GW) % nw == 0 and (ng * SSD_D_STATE) % nw == 0

    def ch(c):
        return nc - 1 - c if rev else c

    chunk_grp = [pl.BlockSpec((q, xw), lambda g, c: (ch(c), g)),
                 pl.BlockSpec((q, nw), lambda g, c: (ch(c), b_off + g)),
                 pl.BlockSpec((q, nw), lambda g, c: (ch(c), c_off + g))]
    col_form = pl.BlockSpec((gps, q, LANES), lambda g, c: (g, ch(c), 0))
    row_form = pl.BlockSpec((gps, 8, q), lambda g, c: (g, 0, ch(c)))
    col_par = pl.BlockSpec((gps, 1, LANES), lambda g, c: (g, 0, 0))
    y_spec = pl.BlockSpec((q, xw), lambda g, c: (ch(c), g))
    st_spec = pl.BlockSpec((gps, None, GW, SSD_D_STATE), lambda g, c: (g, ch(c), 0, 0))
    bc_spec = pl.BlockSpec((q, nw), lambda g, c: (ch(c), g))
    return chunk_grp, col_form, row_form, col_par, y_spec, st_spec, bc_spec


def _ssd_group_views(gi, wide, narrow, stacked):
    xs, ns = pl.ds(gi * GW, GW), pl.ds(gi * SSD_D_STATE, SSD_D_STATE)
    return [r.at[:, xs] for r in wide], [r.at[:, ns] for r in narrow], [r.at[gi] for r in stacked]


def _ssd_fwd(pre, dt_c, cum_c, cum_r, alog_c, dsk_c):
    t = pre.shape[0]
    ng = pre.shape[1] // GC
    q = SSD_CHUNK
    nc = t // q
    gps = SSD_GPS_FWD if ng % SSD_GPS_FWD == 0 else SSD_GPS_BWD
    chunk_grp, col_form, row_form, col_par, y_spec, st_spec, _ = _ssd_specs(nc, False, ng, gps)

    def body(px_ref, pb_ref, pc_ref, dt_ref, cum_ref, cumr_ref, ac_ref, dk_ref, y_ref, sp_ref, st_ref):
        @pl.when(pl.program_id(1) == 0)
        def _():
            st_ref[...] = jnp.zeros_like(st_ref)

        for gi in range(gps):
            (px, y), (pb, pc), rest = _ssd_group_views(
                gi, (px_ref, y_ref), (pb_ref, pc_ref), (dt_ref, cum_ref, cumr_ref, ac_ref, dk_ref, sp_ref, st_ref))
            one_group(px, pb, pc, *rest[:5], y, *rest[5:])

    def one_group(px_ref, pb_ref, pc_ref, dt_ref, cum_ref, cumr_ref, ac_ref, dk_ref, y_ref, sp_ref, st_ref):
        pre_v = jnp.concatenate([px_ref[...], pb_ref[...], pc_ref[...]], axis=1)
        v = _ssd_common(pre_v, dt_ref[...], cum_ref[...], cumr_ref[...], ac_ref[...], False)
        s0 = st_ref[...]
        sp_ref[...] = s0
        r = _dot_nt(v["cm"], s0.astype(BF16))
        y = _expand(v["lam_c"], q) * r + _expand(dk_ref[...], 1) * v["xa"]
        head = _head_of_lane((q, GW), SSD_HEAD_DIM)
        for j in range(SSD_HPG):
            diff = v["cum"][:, j:j + 1] - v["cum_r"][j:j + 1, :]
            w = (v["g"] * jnp.exp(jnp.where(v["tril"], diff, -jnp.inf))).astype(BF16)
            y = y + _dot(w, jnp.where(head == j, v["xdt"], 0.0).astype(BF16))
        y_ref[...] = y
        ds = _dot_tn((v["xdt"] * _expand(v["e_c"], q)).astype(BF16), v["bm"])
        for j in range(SSD_HPG):
            rows = slice(j * SSD_HEAD_DIM, (j + 1) * SSD_HEAD_DIM)
            st_ref[rows, :] = s0[rows, :] * jnp.exp(v["cum_r"][j:j + 1, q - 1:q]) + ds[rows, :]

    return pl.pallas_call(
        body, name="ssd_scan_fwd", grid=(ng // gps, nc),
        in_specs=chunk_grp + [col_form, col_form, row_form, col_par, col_par],
        out_specs=[y_spec, st_spec],
        out_shape=[jax.ShapeDtypeStruct((t, ng * GW), F32), jax.ShapeDtypeStruct((ng, nc, GW, SSD_D_STATE), F32)],
        scratch_shapes=[pltpu.VMEM((gps, GW, SSD_D_STATE), F32)],
        compiler_params=_params("parallel", "arbitrary"))(pre, pre, pre, dt_c, cum_c, cum_r, alog_c, dsk_c)


def _ssd_bwd(dy, pre, states, dt_c, cum_c, cum_r, sgd_c, alog_c, dsk_c):
    t = pre.shape[0]
    ng = pre.shape[1] // GC
    q = SSD_CHUNK
    nc = t // q
    gps = SSD_GPS_BWD
    chunk_grp, col_form, row_form, col_par, y_spec, st_spec, bc_spec = _ssd_specs(nc, True, ng, gps)

    def body(dy_ref, px_ref, pb_ref, pc_ref, sp_ref, dt_ref, cum_ref, cumr_ref, sgd_ref, ac_ref, dk_ref,
             dpx_ref, dpb_ref, dpc_ref, ddt_ref, dbias_ref, dalog_ref, dd_ref, ds_ref):
        @pl.when(pl.program_id(1) == 0)
        def _():
            ds_ref[...] = jnp.zeros_like(ds_ref)

        for gi in range(gps):
            (dy, px, dpx), (pb, pc, dpb, dpc), rest = _ssd_group_views(
                gi, (dy_ref, px_ref, dpx_ref), (pb_ref, pc_ref, dpb_ref, dpc_ref),
                (sp_ref, dt_ref, cum_ref, cumr_ref, sgd_ref, ac_ref, dk_ref, ddt_ref, dbias_ref, dalog_ref, dd_ref,
                 ds_ref))
            one_group(dy, px, pb, pc, *rest[:7], dpx, dpb, dpc, *rest[7:])

    def one_group(dy_ref, px_ref, pb_ref, pc_ref, sp_ref, dt_ref, cum_ref, cumr_ref, sgd_ref, ac_ref, dk_ref,
                  dpx_ref, dpb_ref, dpc_ref, ddt_ref, dbias_ref, dalog_ref, dd_ref, ds_ref):
        first = pl.program_id(1) == 0
        pre_v = jnp.concatenate([px_ref[...], pb_ref[...], pc_ref[...]], axis=1)
        v = _ssd_common(pre_v, dt_ref[...], cum_ref[...], cumr_ref[...], ac_ref[...], True)
        xa, bm, cm, xdt, cum, cum_r = v["xa"], v["bm"], v["cm"], v["xdt"], v["cum"], v["cum_r"]
        xdt_b = xdt.astype(BF16)
        dy_v = dy_ref[...]
        s0 = sp_ref[...]
        ds1 = ds_ref[...]
        s0b, ds1b = s0.astype(BF16), ds1.astype(BF16)
        head = _head_of_lane((q, GW), SSD_HEAD_DIM)
        lane = lax.broadcasted_iota(jnp.int32, (q, LANES), 1)
        lane1 = lax.broadcasted_iota(jnp.int32, (1, LANES), 1)
        lam_x = _expand(v["lam_c"], q, True)
        e_x = _expand(v["e_c"], q, True)

        dxa = _expand(dk_ref[...], 1) * dy_v
        dd = _contract(jnp.sum(dy_v * xa, axis=0, keepdims=True), 1)
        r = _dot_nt(cm, s0b)
        dcum = _contract(dy_v * r * lam_x, q, True)
        drb = (lam_x * dy_v).astype(BF16)
        dc = _dot(drb, s0b)
        ds0 = _dot_tn(drb, cm)
        extra = jnp.zeros((1, LANES), F32)
        for j in range(SSD_HPG):
            rows = slice(j * SSD_HEAD_DIM, (j + 1) * SSD_HEAD_DIM)
            lam_last = jnp.exp(cum_r[j:j + 1, q - 1:q])
            ds_ref[rows, :] = ds0[rows, :] + lam_last * ds1[rows, :]
            tot = jnp.sum(jnp.sum(ds1[rows, :] * s0[rows, :], axis=1, keepdims=True), axis=0, keepdims=True)
            extra = jnp.where(lane1 == j, lam_last * tot, extra)
        dv = _dot_nt(bm, ds1b)
        db = _dot((xdt * e_x).astype(BF16), ds1b)
        dxdt = e_x * dv
        dee = _contract(dv * xdt, q, True) * v["e_c"]
        dcum = dcum - dee
        extra = extra + jnp.sum(dee, axis=0, keepdims=True)
        dg = jnp.zeros((q, q), F32)
        col_sums = jnp.zeros((q, q), F32)
        for j in range(SSD_HPG):
            diff = cum[:, j:j + 1] - cum_r[j:j + 1, :]
            el = jnp.exp(jnp.where(v["tril"], diff, -jnp.inf))
            gl = v["g"] * el
            dym = jnp.where(head == j, dy_v, 0.0).astype(BF16)
            dwm = _dot_nt(dym, xdt_b)
            dxdt = dxdt + _dot_tn(gl.astype(BF16), dym)
            z = dwm * gl
            dcum = jnp.where(lane == j, dcum + jnp.sum(z, axis=1, keepdims=True), dcum)
            col_sums = jnp.where(v["row"] == j, jnp.sum(z, axis=0, keepdims=True), col_sums)
            dg = dg + dwm * el
        dcum = dcum - col_sums.T
        dgb = dg.astype(BF16)
        dc = dc + _dot(dgb, bm)
        db = db + _dot_tn(dgb, cm)
        da = _select_dot(dcum, (v["row"] <= v["col"]).astype(BF16), True) + extra
        ddt = _contract(dxdt * xa, q, True) + v["a_c"] * da
        dalog = jnp.sum(v["dt"] * da, axis=0, keepdims=True) * v["a_c"]
        dxa = dxa + v["dt_x"] * dxdt
        ddt_raw = jnp.where(lane < SSD_HPG, ddt * sgd_ref[...], 0.0)
        sgrad = _silu_grad(pre_v, v["sg"])
        dpx_ref[...] = dxa * sgrad[:, :GW]
        dpb_ref[...] = db * sgrad[:, GW:GW + SSD_D_STATE]
        dpc_ref[...] = dc * sgrad[:, GW + SSD_D_STATE:]
        ddt_ref[...] = ddt_raw
        _acc(dbias_ref, jnp.sum(ddt_raw, axis=0, keepdims=True), first)
        _acc(dalog_ref, jnp.where(lane1 < SSD_HPG, dalog, 0.0), first)
        _acc(dd_ref, dd, first)

    return pl.pallas_call(
        body, name="ssd_scan_bwd", grid=(ng // gps, nc),
        in_specs=[y_spec] + chunk_grp + [st_spec, col_form, col_form, row_form, col_form, col_par, col_par],
        out_specs=[y_spec, bc_spec, bc_spec, col_form, col_par, col_par, col_par],
        out_shape=[jax.ShapeDtypeStruct((t, ng * GW), F32), jax.ShapeDtypeStruct((t, ng * SSD_D_STATE), F32),
                   jax.ShapeDtypeStruct((t, ng * SSD_D_STATE), F32), jax.ShapeDtypeStruct((ng, t, LANES), F32),
                   jax.ShapeDtypeStruct((ng, 1, LANES), F32), jax.ShapeDtypeStruct((ng, 1, LANES), F32),
                   jax.ShapeDtypeStruct((ng, 1, LANES), F32)],
        scratch_shapes=[pltpu.VMEM((gps, GW, SSD_D_STATE), F32)],
        compiler_params=_params("parallel", "arbitrary"))(dy, pre, pre, pre, states, dt_c, cum_c, cum_r, sgd_c, alog_c,
                                                           dsk_c)


def _gate_norm_fwd(y, zx, norm_w):
    t, di = y.shape
    tr = _tile(t, 512, 8)
    ng = di // GW

    def body(y_ref, z_ref, w_ref, o_ref):
        z = z_ref[...]
        gate = y_ref[...] * (z * _sigmoid(z))
        w = w_ref[...]
        for g in range(ng):
            cols = slice(g * GW, (g + 1) * GW)
            gs = gate[:, cols]
            r = lax.rsqrt(jnp.mean(gs * gs, axis=-1, keepdims=True) + NORM_EPS)
            o_ref[:, cols] = (gs * r * w[:, cols]).astype(BF16)

    row = pl.BlockSpec((tr, di), lambda i: (i, 0))
    return pl.pallas_call(body, name="ssd_gate_norm_fwd", grid=(t // tr,),
                          in_specs=[row, row, pl.BlockSpec((1, di), lambda i: (0, 0))], out_specs=row,
                          out_shape=jax.ShapeDtypeStruct((t, di), BF16), compiler_params=_params("parallel"))(
                              y, zx, norm_w)


def _gate_norm_bwd(dyn, y, zx, norm_w, after):
    t, di = y.shape
    tr = _tile(t, 256, 8)
    ng = di // GW

    def body(d_ref, y_ref, z_ref, w_ref, after_ref, dy_ref, dz_ref, dw_ref):
        z = z_ref[...]
        yv = y_ref[...]
        sg = _sigmoid(z)
        sz = z * sg
        gate = yv * sz
        w = w_ref[...]
        d = d_ref[...]
        dsz = _silu_grad(z, sg)
        dws = []
        for g in range(ng):
            cols = slice(g * GW, (g + 1) * GW)
            dg, dwr = _rms_bwd(gate[:, cols], w[:, cols], d[:, cols])
            dy_ref[:, cols] = dg * sz[:, cols]
            dz_ref[:, cols] = (dg * yv[:, cols] * dsz[:, cols]).astype(BF16)
            dws.append(jnp.sum(dwr, axis=0, keepdims=True))
        first = pl.program_id(0) == 0
        for g in range(ng):
            cols = slice(g * GW, (g + 1) * GW)

            @pl.when(first)
            def _():
                dw_ref[:, cols] = dws[g]

            @pl.when(jnp.logical_not(first))
            def _():
                dw_ref[:, cols] += dws[g]

    row = pl.BlockSpec((tr, di), lambda i: (i, 0))
    vec = pl.BlockSpec((1, di), lambda i: (0, 0))
    return pl.pallas_call(body, name="ssd_gate_norm_bwd", grid=(t // tr,),
                          in_specs=[row, row, row, vec, pl.BlockSpec((8, LANES), lambda i: (0, 0))],
                          out_specs=[row, row, vec],
                          out_shape=[jax.ShapeDtypeStruct((t, di), F32), jax.ShapeDtypeStruct((t, di), BF16),
                                     jax.ShapeDtypeStruct((1, di), F32)],
                          compiler_params=_params("arbitrary"))(dyn, y, zx, norm_w, after)


def _attn_mask_t(n):
    w = ATTN_WINDOW
    kpos = lax.broadcasted_iota(jnp.int32, (2 * w, ATTN_REP * w), 0)
    qpos = lax.broadcasted_iota(jnp.int32, (2 * w, ATTN_REP * w), 1) % w + w
    rel = qpos - kpos
    return (rel >= 0) & (rel < w) & jnp.logical_not((n == 0) & (kpos < w))


def _attn_probs_t(qts, ktb, mask, sink):
    s = _dot_tn(ktb, qts) * (ATTN_HEAD_DIM ** -0.5)
    s = jnp.where(mask, s, -jnp.inf)
    m = jnp.maximum(jnp.max(s, axis=0, keepdims=True), sink)
    e = jnp.exp(s - m)
    es = jnp.exp(sink - m)
    inv = 1.0 / (jnp.sum(e, axis=0, keepdims=True) + es)
    return e * inv, es * inv


def _attn_blocks_t(kv, q_ref, kc_ref, vc_ref, kp_ref, vp_ref):
    hd = ATTN_HEAD_DIM
    rows = slice(kv * hd, (kv + 1) * hd)
    ktb = jnp.concatenate([kp_ref[rows, :], kc_ref[rows, :]], axis=1)
    vtb = jnp.concatenate([vp_ref[rows, :], vc_ref[rows, :]], axis=1)
    qts = jnp.concatenate([q_ref[(kv * ATTN_REP + r) * hd:(kv * ATTN_REP + r + 1) * hd, :]
                           for r in range(ATTN_REP)], axis=1)
    return qts, ktb, vtb


def _attn_specs_t(nb, cur, prev):
    w, hd = ATTN_WINDOW, ATTN_HEAD_DIM
    kd = ATTN_N_KV * hd
    qd = ATTN_REP * kd
    return [pl.BlockSpec((qd, w), lambda n: (0, cur(n))),
            pl.BlockSpec((kd, w), lambda n: (ATTN_REP, cur(n))),
            pl.BlockSpec((kd, w), lambda n: (ATTN_REP + 1, cur(n))),
            pl.BlockSpec((kd, w), lambda n: (ATTN_REP, prev(n))),
            pl.BlockSpec((kd, w), lambda n: (ATTN_REP + 1, prev(n)))]


def _attn_fwd_t(qkv_t, sinks_rep):
    t = qkv_t.shape[1]
    w, hd = ATTN_WINDOW, ATTN_HEAD_DIM
    qd = ATTN_N_KV * ATTN_REP * hd
    nb = t // w

    def body(q_ref, kc_ref, vc_ref, kp_ref, vp_ref, s_ref, o_ref):
        mask = _attn_mask_t(pl.program_id(0))
        for kv in range(ATTN_N_KV):
            qts, ktb, vtb = _attn_blocks_t(kv, q_ref, kc_ref, vc_ref, kp_ref, vp_ref)
            p, _ = _attn_probs_t(qts, ktb, mask, s_ref[kv])
            ots = _dot(vtb, p.astype(BF16))
            for r in range(ATTN_REP):
                h = kv * ATTN_REP + r
                o_ref[h * hd:(h + 1) * hd, :] = ots[:, r * w:(r + 1) * w].astype(BF16)

    return pl.pallas_call(
        body, name="attn_fwd", grid=(nb,),
        in_specs=_attn_specs_t(nb, lambda n: n, lambda n: jnp.maximum(n - 1, 0)) + [
            pl.BlockSpec(sinks_rep.shape, lambda n: (0, 0, 0))],
        out_specs=pl.BlockSpec((qd, w), lambda n: (0, n)),
        out_shape=jax.ShapeDtypeStruct((qd, t), BF16),
        compiler_params=_params("parallel"))(qkv_t, qkv_t, qkv_t, qkv_t, qkv_t, sinks_rep)


def _attn_bwd_t(qkv_t, do_t, sinks_rep):
    t = qkv_t.shape[1]
    w, hd = ATTN_WINDOW, ATTN_HEAD_DIM
    kd = ATTN_N_KV * hd
    qd = ATTN_REP * kd
    nq = ATTN_N_KV * ATTN_REP
    nb = t // w
    rows_all = qd + 2 * kd

    def body(q_ref, kc_ref, vc_ref, kp_ref, vp_ref, do_ref, s_ref, dqkv_ref, bsum_ref, dsk_ref,
             carry_ref, new_ref, bacc_ref, sacc_ref):
        n = pl.program_id(0)

        @pl.when(n == 0)
        def _():
            carry_ref[...] = jnp.zeros_like(carry_ref)
            bacc_ref[...] = jnp.zeros_like(bacc_ref)
            sacc_ref[...] = jnp.zeros_like(sacc_ref)

        @pl.when(n < nb)
        def _():
            mask = _attn_mask_t(n)
            for kv in range(ATTN_N_KV):
                qts, ktb, vtb = _attn_blocks_t(kv, q_ref, kc_ref, vc_ref, kp_ref, vp_ref)
                dots = jnp.concatenate([do_ref[(kv * ATTN_REP + r) * hd:(kv * ATTN_REP + r + 1) * hd, :]
                                        for r in range(ATTN_REP)], axis=1)
                p, ps = _attn_probs_t(qts, ktb, mask, s_ref[kv])
                dpt = _dot_tn(vtb, dots)
                delta = jnp.sum(p * dpt, axis=0, keepdims=True)
                dst = (p * (dpt - delta) * (hd ** -0.5)).astype(BF16)
                dqts = _dot(ktb, dst)
                for r in range(ATTN_REP):
                    h = kv * ATTN_REP + r
                    new_ref[h * hd:(h + 1) * hd, :] = dqts[:, r * w:(r + 1) * w]
                dktb = _dot_nt(qts, dst)
                dvtb = _dot_nt(dots, p.astype(BF16))
                krows = slice(qd + kv * hd, qd + (kv + 1) * hd)
                vrows = slice(qd + kd + kv * hd, qd + kd + (kv + 1) * hd)
                carry_ref[krows, :] += dktb[:, :w]
                carry_ref[vrows, :] += dvtb[:, :w]
                new_ref[krows, :] = dktb[:, w:]
                new_ref[vrows, :] = dvtb[:, w:]
                sacc_ref[kv] += -(ps * delta)

        @pl.when(n >= 1)
        def _():
            done = carry_ref[...]
            dqkv_ref[...] = done.astype(BF16)
            bacc_ref[...] += done

        @pl.when(n < nb)
        def _():
            carry_ref[...] = new_ref[...]

        @pl.when(n == nb)
        def _():
            bsum_ref[...] = jnp.sum(bacc_ref[...], axis=1, keepdims=True)
            lane = lax.broadcasted_iota(jnp.int32, (1, nq), 1)
            dsk = jnp.zeros((1, nq), F32)
            for kv in range(ATTN_N_KV):
                acc = sacc_ref[kv]
                for r in range(ATTN_REP):
                    tot = jnp.sum(acc[:, r * w:(r + 1) * w], axis=1, keepdims=True)
                    dsk = jnp.where(lane == kv * ATTN_REP + r, tot, dsk)
            dsk_ref[...] = dsk

    cur = lambda n: jnp.minimum(n, nb - 1)
    prev = lambda n: jnp.maximum(jnp.minimum(n, nb - 1) - 1, 0)
    return pl.pallas_call(
        body, name="attn_bwd", grid=(nb + 1,),
        in_specs=_attn_specs_t(nb, cur, prev) + [pl.BlockSpec((qd, w), lambda n: (0, cur(n))),
                                                 pl.BlockSpec(sinks_rep.shape, lambda n: (0, 0, 0))],
        out_specs=[pl.BlockSpec((rows_all, w), lambda n: (0, jnp.maximum(n - 1, 0))),
                   pl.BlockSpec((rows_all, 1), lambda n: (0, 0)),
                   pl.BlockSpec((1, nq), lambda n: (0, 0))],
        out_shape=[jax.ShapeDtypeStruct((rows_all, t), BF16), jax.ShapeDtypeStruct((rows_all, 1), F32),
                   jax.ShapeDtypeStruct((1, nq), F32)],
        scratch_shapes=[pltpu.VMEM((rows_all, w), F32), pltpu.VMEM((rows_all, w), F32),
                        pltpu.VMEM((rows_all, w), F32), pltpu.VMEM(sinks_rep.shape, F32)],
        compiler_params=_params("arbitrary"))(qkv_t, qkv_t, qkv_t, qkv_t, qkv_t, do_t, sinks_rep)


HBM_SPEC = pl.BlockSpec(memory_space=pl.ANY)
HBM_ONLY = pl.BlockSpec(memory_space=pltpu.HBM)


def _comm_call(name, body, ins, out_shapes, n_sems):
    return pl.pallas_call(
        body, name=name, in_specs=[HBM_SPEC] * len(ins), out_specs=[HBM_SPEC] * len(out_shapes),
        out_shape=out_shapes,
        scratch_shapes=[pltpu.SemaphoreType.DMA((s,)) for s in n_sems])(*ins)


def _all_gather(name, shards, after):
    n = len(shards)
    na = len(after)

    def body(*refs):
        x_refs, out_refs = refs[:n], refs[n + na:2 * n + na]
        send_sems, recv_sems, local_sems = refs[2 * n + na:]
        x, y, c = lax.axis_index("x"), lax.axis_index("y"), lax.axis_index("c")
        me, sibling = (x, y, c), (x, y, 1 - c)
        chips = [(1 - x, y), (x, 1 - y), (1 - x, 1 - y)]

        def slot(i, px, py, pc):
            return out_refs[i].at[4 * px + 2 * py + pc]

        def copy(k, i, block, to, src=None):
            return pltpu.make_async_remote_copy(
                src_ref=slot(i, *block) if src is None else src, dst_ref=slot(i, *block),
                send_sem=send_sems.at[k * n + i], recv_sem=recv_sems.at[k * n + i], device_id=to,
                device_id_type=MESH)

        mine = [pltpu.make_async_copy(x_refs[i], slot(i, *me), local_sems.at[i]) for i in range(n)]
        first = []
        for i in range(n):
            mine[i].start()
            first.append(copy(0, i, me, sibling, src=x_refs[i]))
            first += [copy(1 + j, i, me, (*chip, c), src=x_refs[i]) for j, chip in enumerate(chips)]
        for cp in first:
            cp.start()
        passed = []
        for i in range(n):
            for j, chip in enumerate(chips):
                copy(1 + j, i, (*chip, c), me).wait_recv()
                passed.append(copy(4 + j, i, (*chip, c), sibling))
                passed[-1].start()
        for i in range(n):
            copy(0, i, sibling, me).wait_recv()
            for j, chip in enumerate(chips):
                copy(4 + j, i, (*chip, 1 - c), me).wait_recv()
        for cp in first + passed:
            cp.wait_send()
        for cp in mine:
            cp.wait()

    outs = [jax.ShapeDtypeStruct((N_DEV,) + s.shape, s.dtype) for s in shards]
    return _comm_call(name, body, list(shards) + list(after), outs, (7 * n, 7 * n, n))


SEM_SPEC = pl.BlockSpec(memory_space=pltpu.SEMAPHORE)
SPLIT_COPY_EFFECT = pltpu.SideEffectType.DATAFLOW_SIDE_EFFECTING


def _in_hbm(a):
    return pltpu.with_memory_space_constraint(a, pltpu.HBM)


def _split_start(name, body, srcs, lands, n_sems):
    n = len(srcs)
    bufs = [_in_hbm(a) for a in list(srcs) + list(lands)]
    outs = pl.pallas_call(
        body, name=name,
        out_shape=(pltpu.SemaphoreType.DMA((n_sems,)), pltpu.SemaphoreType.DMA((n_sems,)),
                   *[pltpu.HBM(a.shape, a.dtype) for a in bufs], jax.ShapeDtypeStruct((8, LANES), F32)),
        in_specs=[HBM_ONLY] * (2 * n),
        out_specs=(SEM_SPEC, SEM_SPEC, *[HBM_ONLY] * (2 * n), pl.BlockSpec(memory_space=pltpu.VMEM)),
        input_output_aliases={i: 2 + i for i in range(2 * n)},
        compiler_params=pltpu.CompilerParams(has_side_effects=SPLIT_COPY_EFFECT))(*bufs)
    return outs[0], outs[1], list(outs[2:2 + n]), list(outs[2 + n:2 + 2 * n]), outs[-1]


def _split_wait(name, body, send_sems, recv_sems, srcs, lands, after):
    n = len(srcs)
    outs = pl.pallas_call(
        body, name=name,
        out_shape=[pltpu.HBM(a.shape, a.dtype) for a in list(srcs) + list(lands)],
        in_specs=[HBM_ONLY] * (2 * n) + [SEM_SPEC, SEM_SPEC, HBM_SPEC],
        out_specs=[HBM_ONLY] * (2 * n),
        input_output_aliases={i: i for i in range(2 * n)},
        compiler_params=pltpu.CompilerParams(has_side_effects=SPLIT_COPY_EFFECT))(
            *srcs, *lands, send_sems, recv_sems, after)
    return list(outs[:n]), list(outs[n:])


N_PEERS = N_DEV - 1


def _gather_peers():
    x, y, c = lax.axis_index("x"), lax.axis_index("y"), lax.axis_index("c")
    flips = [(fx, fy, fc) for fx in (0, 1) for fy in (0, 1) for fc in (0, 1) if fx or fy or fc]
    return [(1 - x if fx else x, 1 - y if fy else y, 1 - c if fc else c) for fx, fy, fc in flips]


def _block_id(dev):
    return 4 * dev[0] + 2 * dev[1] + dev[2]


def _landing_block(land_ref, shard_shape, side_by_side, dev):
    if not side_by_side:
        return land_ref.at[_block_id(dev)]
    cols = shard_shape[1]
    return land_ref.at[:, pl.ds(pl.multiple_of(_block_id(dev) * cols, LANES), cols)]


def _gather_start(name, shards, side_by_side):
    n = len(shards)

    def body(*refs):
        x_refs, land_refs = refs[:n], refs[n:2 * n]
        send_sems, recv_sems, token = refs[2 * n], refs[2 * n + 1], refs[-1]
        me = (lax.axis_index("x"), lax.axis_index("y"), lax.axis_index("c"))
        for i in range(n):
            for k, peer in enumerate(_gather_peers()):
                pltpu.make_async_remote_copy(
                    src_ref=x_refs[i], dst_ref=_landing_block(land_refs[i], shards[i].shape, side_by_side[i], me),
                    send_sem=send_sems.at[N_PEERS * i + k], recv_sem=recv_sems.at[N_PEERS * i + k],
                    device_id=peer, device_id_type=MESH).start()
            pltpu.make_async_copy(x_refs[i], _landing_block(land_refs[i], shards[i].shape, side_by_side[i], me),
                                  send_sems.at[N_PEERS * n + i]).start()
        token[...] = jnp.zeros_like(token)

    lands = [lax.empty((s.shape[0], N_DEV * s.shape[1]) if wide else (N_DEV,) + s.shape, s.dtype)
             for s, wide in zip(shards, side_by_side)]
    return _split_start(name, body, shards, lands, (N_PEERS + 1) * n)


def _gather_wait(name, send_sems, recv_sems, first, n_all, shards, lands, side_by_side, after):
    n = len(shards)

    def body(*refs):
        x_refs, land_refs = refs[:n], refs[n:2 * n]
        send_sems, recv_sems = refs[2 * n], refs[2 * n + 1]
        me = (lax.axis_index("x"), lax.axis_index("y"), lax.axis_index("c"))
        for i in range(n):
            pltpu.make_async_copy(x_refs[i], _landing_block(land_refs[i], shards[i].shape, side_by_side[i], me),
                                  send_sems.at[N_PEERS * n_all + first + i]).wait()
            for k, peer in enumerate(_gather_peers()):
                cp = pltpu.make_async_remote_copy(
                    src_ref=x_refs[i], dst_ref=_landing_block(land_refs[i], shards[i].shape, side_by_side[i], peer),
                    send_sem=send_sems.at[N_PEERS * (first + i) + k],
                    recv_sem=recv_sems.at[N_PEERS * (first + i) + k],
                    device_id=peer, device_id_type=MESH)
                cp.wait_send()
                cp.wait_recv()

    return _split_wait(name, body, send_sems, recv_sems, shards, lands, after)


def _scatter_start(name, blocks):
    n = len(blocks)

    def body(*refs):
        b_refs, land_refs = refs[:n], refs[n:2 * n]
        send_sems, recv_sems, token = refs[2 * n], refs[2 * n + 1], refs[-1]
        me = (lax.axis_index("x"), lax.axis_index("y"), lax.axis_index("c"))
        for i in range(n):
            for k, peer in enumerate(_gather_peers()):
                pltpu.make_async_remote_copy(
                    src_ref=b_refs[i].at[_block_id(peer)], dst_ref=land_refs[i].at[_block_id(me)],
                    send_sem=send_sems.at[N_PEERS * i + k], recv_sem=recv_sems.at[N_PEERS * i + k],
                    device_id=peer, device_id_type=MESH).start()
            pltpu.make_async_copy(b_refs[i].at[_block_id(me)], land_refs[i].at[_block_id(me)],
                                  send_sems.at[N_PEERS * n + i]).start()
        token[...] = jnp.zeros_like(token)

    lands = [lax.empty(b.shape, b.dtype) for b in blocks]
    return _split_start(name, body, blocks, lands, (N_PEERS + 1) * n)


def _scatter_wait(name, send_sems, recv_sems, blocks, lands, after):
    n = len(blocks)

    def body(*refs):
        b_refs, land_refs = refs[:n], refs[n:2 * n]
        send_sems, recv_sems = refs[2 * n], refs[2 * n + 1]
        me = (lax.axis_index("x"), lax.axis_index("y"), lax.axis_index("c"))
        for i in range(n):
            pltpu.make_async_copy(b_refs[i].at[_block_id(me)], land_refs[i].at[_block_id(me)],
                                  send_sems.at[N_PEERS * n + i]).wait()
            for k, peer in enumerate(_gather_peers()):
                cp = pltpu.make_async_remote_copy(
                    src_ref=b_refs[i].at[_block_id(peer)], dst_ref=land_refs[i].at[_block_id(peer)],
                    send_sem=send_sems.at[N_PEERS * i + k], recv_sem=recv_sems.at[N_PEERS * i + k],
                    device_id=peer, device_id_type=MESH)
                cp.wait_send()
                cp.wait_recv()

    return _split_wait(name, body, send_sems, recv_sems, blocks, lands, after)


def _adamw(w, g, m, v):
    m = ADAM_B1 * m + (1.0 - ADAM_B1) * g
    v = ADAM_B2 * v + (1.0 - ADAM_B2) * (g * g)
    m_hat = m / (1.0 - ADAM_B1 ** ADAM_STEP)
    v_hat = v / (1.0 - ADAM_B2 ** ADAM_STEP)
    delta = -ADAM_LR * (m_hat / (jnp.sqrt(v_hat) + ADAM_EPS) + ADAM_WD * w)
    return delta, m, v


def _adamw_tiles(r, c_):
    tr = _tile(r, 256, 16)
    return (tr, c_) if tr < r or r <= 256 else (r, _tile(c_, 256))


def _sum_parts(part):
    g = part[0].astype(F32)
    for k in range(1, part.shape[0]):
        g = g + part[k].astype(F32)
    return g


def _sum_adamw(name, parts, w, m, v):
    r, c_ = w.shape
    tr, tc = _adamw_tiles(r, c_)

    def body(p_ref, w_ref, m_ref, v_ref, g_ref, d_ref, nm_ref, nv_ref):
        g = _sum_parts(p_ref)
        g_ref[...] = g
        d_ref[...], nm_ref[...], nv_ref[...] = _adamw(w_ref[...], g, m_ref[...], v_ref[...])

    tile = pl.BlockSpec((tr, tc), lambda i, j: (i, j))
    return pl.pallas_call(body, name=name, grid=(r // tr, c_ // tc),
                          in_specs=[pl.BlockSpec((parts.shape[0], tr, tc), lambda i, j: (0, i, j)), tile, tile, tile],
                          out_specs=[tile] * 4, out_shape=[jax.ShapeDtypeStruct((r, c_), F32)] * 4,
                          compiler_params=_params("parallel", "parallel"))(parts, w, m, v)


def _sum_adamw_layers(name, parts, w, m, v):
    n_layers, r, c_ = w.shape
    tr = _tile(r, 256, 16)

    def body(*refs):
        p_refs = refs[:n_layers]
        w_ref, m_ref, v_ref, g_ref, d_ref, nm_ref, nv_ref = refs[n_layers:]
        layer = pl.program_id(0)
        g = _sum_parts(p_refs[0])
        for li in range(1, n_layers):
            g = jnp.where(layer == li, _sum_parts(p_refs[li]), g)
        g_ref[...] = g
        d_ref[...], nm_ref[...], nv_ref[...] = _adamw(w_ref[...], g, m_ref[...], v_ref[...])

    row = pl.BlockSpec((None, tr, c_), lambda l, i: (l, i, 0))
    specs = [pl.BlockSpec((p.shape[0], tr, c_), lambda l, i, li=li: (0, jnp.where(l == li, i, 0), 0))
             for li, p in enumerate(parts)]
    return pl.pallas_call(body, name=name, grid=(n_layers, r // tr), in_specs=specs + [row, row, row],
                          out_specs=[row] * 4, out_shape=[jax.ShapeDtypeStruct(w.shape, F32)] * 4,
                          compiler_params=_params("parallel", "parallel"))(*parts, w, m, v)


def _pack_rows(flat, n_rows, cols):
    pad = n_rows * cols - flat.shape[-1]
    flat = jnp.pad(flat, [(0, 0)] * (flat.ndim - 1) + [(0, pad)])
    return flat.reshape(flat.shape[:-1] + (n_rows, cols))


def _cols_split(full):
    c = full.shape[1] // N_DEV
    return jnp.stack([full[:, d * c:(d + 1) * c] for d in range(N_DEV)])


def _rows_join(blocks):
    return blocks.reshape(N_DEV * blocks.shape[1], blocks.shape[2])


def _rows_split(full):
    return full.reshape(N_DEV, full.shape[0] // N_DEV, full.shape[1])


def _heads_col(v, ng):
    return jnp.pad(v.reshape(ng, 1, SSD_HPG), ((0, 0), (0, 0), (0, LANES - SSD_HPG)))


MATRIX_ITEMS = ("w_in", "w_out", "up0", "down0", "w_qkv", "w_o", "up1", "down1")
VECTOR_ITEMS = ("conv_w", "b_qkv", "b_o")
ITEMS = MATRIX_ITEMS + VECTOR_ITEMS
GATHER_STAGES = (("w_in", "conv_w"), ("w_out", "up0", "down0"), ("w_qkv", "b_qkv", "w_o", "b_o", "up1", "down1"))
SIDE_BY_SIDE = ("conv_w", "up0", "up1", "b_o")


def _items(tree, prefix=""):
    g = lambda k: tree[prefix + k]
    return {"w_in": g("ssd_w_in")[0].T, "w_out": g("ssd_w_out")[0], "w_qkv": g("attn_w_qkv")[0].T,
            "w_o": g("attn_w_o")[0], "up0": g("mlp_w_up")[0], "up1": g("mlp_w_up")[1],
            "down0": g("mlp_w_down")[0], "down1": g("mlp_w_down")[1], "conv_w": g("ssd_conv_w")[0],
            "b_qkv": g("attn_b_qkv"), "b_o": g("attn_b_o")}


REPLICATED = ("ssd_conv_b", "ssd_dt_bias", "ssd_a_log", "ssd_d", "ssd_norm_w", "attn_sinks", "mix_pre_norm",
              "mix_post_norm", "ffn_pre_norm", "ffn_post_norm")
WEIGHTS = ("ssd_w_in", "ssd_conv_w", "ssd_conv_b", "ssd_dt_bias", "ssd_a_log", "ssd_d", "ssd_norm_w", "ssd_w_out",
           "attn_w_qkv", "attn_b_qkv", "attn_sinks", "attn_w_o", "attn_b_o", "mlp_w_up", "mlp_w_down",
           "mix_pre_norm", "mix_post_norm", "ffn_pre_norm", "ffn_post_norm")


def _forward_backward(x, target, rep, token, weights_of_stage, reduce_grads):
    t, d = x.shape
    ng = rep["ssd_norm_w"].shape[1] // GW
    di = ng * GW
    n_xbc = ng * GC
    nh = ng * SSD_HPG
    grads, blocks = {}, {}
    w_up, w_down = [None, None], [None, None]
    sinks_rep = jnp.repeat(rep["attn_sinks"].reshape(ATTN_N_KV, ATTN_REP, 1), ATTN_WINDOW, axis=2).reshape(
        ATTN_N_KV, 1, ATTN_REP * ATTN_WINDOW)
    conv_b = rep["ssd_conv_b"]
    gn = ng * SSD_D_STATE
    parts = ((0, di), (di, di), (2 * di, gn), (2 * di + gn, gn), (di + n_xbc, nh))
    alog_c, dsk_c = (_heads_col(rep[k], ng) for k in ("ssd_a_log", "ssd_d"))
    bias_l, alog_l = (jnp.pad(rep[k], ((0, 0), (0, LANES - nh))) for k in ("ssd_dt_bias", "ssd_a_log"))
    norm = {k: rep[k] for k in ("mix_pre_norm", "mix_post_norm", "ffn_pre_norm", "ffn_post_norm")}

    def nrow(name, i):
        return norm[name][i:i + 1]

    def mlp_fwd(i, u2):
        p = _mm(f"mlp{i}_up", [u2], [w_up[i]], "nn", tm=2048, tn=1024, out_dtypes=(BF16,),
                epilogue=lambda acc: (jnp.square(jnp.maximum(acc, 0.0)),))
        f = _mm(f"mlp{i}_down", [p], [w_down[i]], "nn", tm=512, tn=1024)
        return p, f

    def mlp_bwd(i, df, u2, p):
        da = _mm(f"mlp{i}_dact", [df], [w_down[i]], "nt", tm=2048, tn=1024, out_dtypes=(BF16,),
                 tiles=(p,), epilogue=lambda acc, pv: (acc * (2.0 * jnp.sqrt(pv.astype(F32))),))
        blocks[f"down{i}"] = _rows_split(_mm(f"mlp{i}_dwdown", [p], [df], "tn", tm=512, tn=1024,
                                             out_dtypes=(PAYLOAD,)))
        blocks[f"up{i}"] = _mm(f"mlp{i}_dwup", [u2], [da], "tn", tm=1024, tn=da.shape[1] // N_DEV,
                               out_dtypes=(PAYLOAD,), col_blocks=True)
        return _mm(f"mlp{i}_dx", [da], [w_up[i]], "nt", tm=512, tn=1024)

    u0 = _prenorm("l0_prenorm", x, nrow("mix_pre_norm", 0), token)
    got = weights_of_stage(0, u0)
    w_in_t = _rows_join(got["w_in"])
    w_dt_t = jnp.pad(w_in_t[di + n_xbc:], ((0, LANES - nh), (0, 0)))
    conv_w = got["conv_w"]
    zx = _mm("ssd_in_proj", [u0], [w_in_t], "nt", tm=2048, tn=1024, n_use=di + n_xbc)
    zdt = _mm("ssd_dt_proj", [u0], [w_dt_t], "nt", tm=1024, tn=LANES)
    pre = _conv_fwd(zx, di, n_xbc, conv_w, conv_b)
    dt_c, cum_c, cum_r, sgd_c = _ssd_dt_prep(zdt, bias_l, alog_l, ng)
    y, states = _ssd_fwd(pre, dt_c, cum_c, cum_r, alog_c, dsk_c)
    yn = _gate_norm_fwd(y, zx, rep["ssd_norm_w"])
    got = weights_of_stage(1, yn)
    w_out = _rows_join(got["w_out"])
    w_up[0], w_down[0] = got["up0"], _rows_join(got["down0"])
    mix0 = _mm("ssd_out_proj", [yn], [w_out], "nn", tm=1024, tn=1024)
    h1, u0f = _post_pre("l0_mid", x, mix0, nrow("mix_post_norm", 0), nrow("ffn_pre_norm", 0))
    p0, f0 = mlp_fwd(0, u0f)
    h2, u1 = _post_pre("l1_in", h1, f0, nrow("ffn_post_norm", 0), nrow("mix_pre_norm", 1))
    got = weights_of_stage(2, u1)
    w_qkv_t = _rows_join(got["w_qkv"])
    w_o = _rows_join(got["w_o"])
    b_qkv_col = got["b_qkv"].reshape(-1, 1)
    b_o = got["b_o"]
    w_up[1], w_down[1] = got["up1"], _rows_join(got["down1"])
    qkv_t = _mm("attn_qkv_proj", [w_qkv_t], [u1], "nt", tm=768, tn=1024, out_dtypes=(BF16,), cols=(b_qkv_col,),
                epilogue=lambda acc, b: (acc + b,))
    ao_t = _attn_fwd_t(qkv_t, sinks_rep)
    mix1 = _mm("attn_out_proj", [ao_t], [w_o], "tn", tm=1024, tn=1024, rows=(b_o,),
               epilogue=lambda acc, b: (acc + b,))
    h3, u1f = _post_pre("l1_mid", h2, mix1, nrow("mix_post_norm", 1), nrow("ffn_pre_norm", 1))
    p1, f1 = mlp_fwd(1, u1f)
    dh, loss_row = _final_loss("loss", h3, f1, nrow("ffn_post_norm", 1), target)

    g_norm = {k: [None, None] for k in norm}
    df1, g_norm["ffn_post_norm"][1], _ = _norm_bwd("l1_ffn_post_bwd", dh, post=(f1, nrow("ffn_post_norm", 1)))
    du = mlp_bwd(1, df1, u1f, p1)
    sent = reduce_grads("mlp1", {k: blocks[k] for k in ("up1", "down1")})
    dh, g_norm["ffn_pre_norm"][1], dmix1, g_norm["mix_post_norm"][1], db_o = _norm_bwd(
        "l1_mid_bwd", dh, pre=(du, h3, nrow("ffn_pre_norm", 1)), post=(mix1, nrow("mix_post_norm", 1)), after=sent)
    blocks["b_o"] = _cols_split(db_o)
    blocks["w_o"] = _rows_split(_mm("attn_dwo", [ao_t], [dmix1], "nn", tm=512, tn=1024, out_dtypes=(PAYLOAD,)))
    dao_t = _mm("attn_dout", [w_o], [dmix1], "nt", tm=1024, tn=1024, out_dtypes=(BF16,))
    dqkv_t, db_qkv, grads["attn_sinks"] = _attn_bwd_t(qkv_t, dao_t, sinks_rep)
    blocks["b_qkv"] = db_qkv.reshape(N_DEV, 1, -1)
    blocks["w_qkv"] = _rows_split(_mm("attn_dwqkv", [dqkv_t], [u1], "nn", tm=512, tn=1024, out_dtypes=(PAYLOAD,)))
    du = _mm("attn_dx", [dqkv_t], [w_qkv_t], "tn", tm=1024, tn=1024)
    sent = reduce_grads("attn", {k: blocks[k] for k in ("w_o", "w_qkv", "b_o", "b_qkv")})
    dh, g_norm["mix_pre_norm"][1], df0, g_norm["ffn_post_norm"][0], _ = _norm_bwd(
        "l1_in_bwd", dh, pre=(du, h2, nrow("mix_pre_norm", 1)), post=(f0, nrow("ffn_post_norm", 0)), after=sent)
    du = mlp_bwd(0, df0, u0f, p0)
    sent = reduce_grads("mlp0", {k: blocks[k] for k in ("up0", "down0")})
    dh, g_norm["ffn_pre_norm"][0], dmix0, g_norm["mix_post_norm"][0], _ = _norm_bwd(
        "l0_mid_bwd", dh, pre=(du, h1, nrow("ffn_pre_norm", 0)), post=(mix0, nrow("mix_post_norm", 0)), after=sent)
    blocks["w_out"] = _rows_split(_mm("ssd_dwout", [yn], [dmix0], "tn", tm=512, tn=1024, out_dtypes=(PAYLOAD,)))
    dyn = _mm("ssd_dyn", [dmix0], [w_out], "nt", tm=1024, tn=1024)
    sent = reduce_grads("ssdout", {"w_out": blocks["w_out"]})
    dy, dz, grads["ssd_norm_w"] = _gate_norm_bwd(dyn, y, zx, rep["ssd_norm_w"], sent)
    dpx, dpb, dpc, ddt_g, dbias_g, dalog_g, dd_g = _ssd_bwd(dy, pre, states, dt_c, cum_c, cum_r, sgd_c, alog_c,
                                                             dsk_c)
    conv_out = [_conv_bwd(f"ssd_conv_bwd_{tag}", dp, zx, c0, conv_w[:, c0 - di:c0 - di + n])
                for tag, dp, (c0, n) in zip("xbc", (dpx, dpb, dpc), parts[1:4])]
    dconv_w = jnp.concatenate([o[1] for o in conv_out], axis=1)
    dconv_b = jnp.concatenate([o[2] for o in conv_out], axis=1)
    ddt = jnp.transpose(ddt_g[:, :, :SSD_HPG], (1, 0, 2)).reshape(t, nh)
    ddt = jnp.pad(ddt, ((0, 0), (0, LANES - nh))).astype(BF16)
    blocks["conv_w"] = _cols_split(dconv_w)
    grads["ssd_conv_b"] = dconv_b
    for name, val in (("ssd_dt_bias", dbias_g), ("ssd_a_log", dalog_g), ("ssd_d", dd_g)):
        grads[name] = val[:, 0, :SSD_HPG].reshape(1, nh)
    d_zx = [dz] + [o[0] for o in conv_out] + [ddt]
    dw_parts = [_mm(f"ssd_dw_{tag}", [d], [u0], "tn", tm=512, tn=1024, out_dtypes=(PAYLOAD,))
                for tag, d in zip("zxbct", d_zx)]
    dw_parts[-1] = dw_parts[-1][:nh]
    blocks["w_in"] = _rows_split(jnp.concatenate(dw_parts, axis=0))
    sent = reduce_grads("ssd", {k: blocks[k] for k in ("w_in", "conv_w")})
    w_parts = [w_in_t[r0:r0 + n] for r0, n in parts[:-1]] + [w_dt_t]
    du = _mm("ssd_dx", d_zx, w_parts, "nn", tm=256, tn=1024, after=sent)
    grad_x, g_norm["mix_pre_norm"][0] = _norm_bwd("l0_in_bwd", dh, pre=(du, x, nrow("mix_pre_norm", 0)), after=sent)
    for k in norm:
        grads[k] = jnp.concatenate(g_norm[k], axis=0)
    return loss_row, grad_x, grads


def kernel(x, ssd_w_in, ssd_conv_w, ssd_conv_b, ssd_dt_bias, ssd_a_log, ssd_d, ssd_norm_w, ssd_w_out, attn_w_qkv, attn_b_qkv, attn_sinks, attn_w_o, attn_b_o, mlp_w_up, mlp_w_down, mix_pre_norm, mix_post_norm, ffn_pre_norm, ffn_post_norm, loss_target, m_ssd_w_in, m_ssd_conv_w, m_ssd_conv_b, m_ssd_dt_bias, m_ssd_a_log, m_ssd_d, m_ssd_norm_w, m_ssd_w_out, m_attn_w_qkv, m_attn_b_qkv, m_attn_sinks, m_attn_w_o, m_attn_b_o, m_mlp_w_up, m_mlp_w_down, m_mix_pre_norm, m_mix_post_norm, m_ffn_pre_norm, m_ffn_post_norm, v_ssd_w_in, v_ssd_conv_w, v_ssd_conv_b, v_ssd_dt_bias, v_ssd_a_log, v_ssd_d, v_ssd_norm_w, v_ssd_w_out, v_attn_w_qkv, v_attn_b_qkv, v_attn_sinks, v_attn_w_o, v_attn_b_o, v_mlp_w_up, v_mlp_w_down, v_mix_pre_norm, v_mix_post_norm, v_ffn_pre_norm, v_ffn_post_norm):
    given = dict(locals())
    w = {k: given[k] for k in WEIGHTS}
    mom_m = {k: given["m_" + k] for k in WEIGHTS}
    mom_v = {k: given["v_" + k] for k in WEIGHTS}
    w_it, m_it, v_it = _items(given), _items(given, "m_"), _items(given, "v_")

    order = [k for stage in GATHER_STAGES for k in stage]
    shards = [w_it[k].astype(PAYLOAD) if k in MATRIX_ITEMS else w_it[k] for k in order]
    wide = [k in SIDE_BY_SIDE for k in order]
    g_send, g_recv, shards, lands, token = _gather_start("gather_start", shards, wide)

    def weights_of_stage(s, after):
        first = sum(len(stage) for stage in GATHER_STAGES[:s])
        sl = slice(first, first + len(GATHER_STAGES[s]))
        _, got = _gather_wait(f"gather_wait{s}", g_send, g_recv, first, len(order), shards[sl], lands[sl], wide[sl],
                              after)
        return dict(zip(GATHER_STAGES[s], got))

    in_flight = []

    def reduce_grads(tag, blocks):
        keys = list(blocks)
        started = _scatter_start(f"rs_start_{tag}", [blocks[k] for k in keys])
        in_flight.append((tag, keys, started))
        return started[-1]

    rep = {k: w[k] for k in REPLICATED}
    loss_row, grad_x, grads = _forward_backward(x[0], loss_target[0], rep, token, weights_of_stage, reduce_grads)

    def pack_rep(tree, last):
        flat = jnp.concatenate([tree[k].reshape(-1) for k in REPLICATED] + [last])
        return _pack_rows(flat, _round_up(-(-flat.shape[0] // LANES), 8), LANES)

    landed = {}

    def wait_group(group, after):
        tag, keys, (s_send, s_recv, srcs, s_lands, _) = group
        _, got = _scatter_wait(f"rs_wait_{tag}", s_send, s_recv, srcs, s_lands, after)
        landed.update(zip(keys, got))

    def adamw_item(k):
        return _sum_adamw(f"adamw_{k}", landed[k], w_it[k], m_it[k], v_it[k])

    def adamw_stack(name, keys):
        return _sum_adamw_layers(f"adamw_{name}", [landed[k] for k in keys], given[name], given["m_" + name],
                                 given["v_" + name])

    for group in in_flight[:-1]:
        wait_group(group, grad_x)
    done = {"mlp_w_up": adamw_stack("mlp_w_up", ("up0", "up1")),
            "mlp_w_down": adamw_stack("mlp_w_down", ("down0", "down1")),
            "attn_w_qkv": [o.T[None] for o in adamw_item("w_qkv")],
            "attn_w_o": [o[None] for o in adamw_item("w_o")],
            "attn_b_qkv": adamw_item("b_qkv"), "attn_b_o": adamw_item("b_o"),
            "ssd_w_out": [o[None] for o in adamw_item("w_out")]}
    partials, = _all_gather("gather_small_grads", [pack_rep(grads, loss_row[0, :1])],
                            [outs4[0] for outs4 in done.values()])
    wait_group(in_flight[-1], partials)
    done["ssd_w_in"] = [o.T[None] for o in adamw_item("w_in")]
    done["ssd_conv_w"] = [o[None] for o in adamw_item("conv_w")]
    zero = jnp.zeros((1,), F32)
    rep_out = _sum_adamw("adamw_replicated", partials, pack_rep(w, zero), pack_rep(mom_m, zero), pack_rep(mom_v, zero))

    kinds = []
    for kind, r_arr in enumerate(rep_out):
        tree = {name: outs4[kind] for name, outs4 in done.items()}
        flat, off = r_arr.reshape(-1), 0
        for k in REPLICATED:
            tree[k] = flat[off:off + w[k].size].reshape(w[k].shape)
            off += w[k].size
        kinds.append(tree)
    loss = rep_out[0].reshape(-1)[off]
    outs = [loss, grad_x[None]]
    for tree in kinds:
        outs += [tree[k] for k in WEIGHTS]
    return tuple(outs)
```

```python
import jax
import jax.numpy as jnp
from jax import lax
from jax.experimental import pallas as pl
from jax.experimental.pallas import tpu as pltpu

F32 = jnp.float32
BF16 = jnp.bfloat16
PAYLOAD = jnp.bfloat16
HIGHEST = lax.Precision.HIGHEST
MESH = pl.DeviceIdType.MESH

NORM_EPS = 1e-6
SSD_HEAD_DIM = 64
SSD_HPG = 4
SSD_D_STATE = 128
SSD_CONV_WIDTH = 4
SSD_CHUNK = 128
ATTN_HEAD_DIM = 64
ATTN_N_KV = 4
ATTN_REP = 4
ATTN_WINDOW = 128
ADAM_LR = 0.001
ADAM_B1 = 0.9
ADAM_B2 = 0.999
ADAM_EPS = 1e-08
ADAM_WD = 0.01
ADAM_STEP = 10

N_DEV = 8
LANES = 128
V7X_VMEM_LIMIT = 56 * 1024 * 1024

GW = SSD_HPG * SSD_HEAD_DIM
GC = GW + 2 * SSD_D_STATE
assert SSD_CHUNK == LANES


def _params(*sem):
    return pltpu.CompilerParams(dimension_semantics=sem, vmem_limit_bytes=V7X_VMEM_LIMIT)


def _tile(n, pref, mult=LANES):
    best = None
    t = mult
    while t <= min(n, pref):
        if n % t == 0:
            best = t
        t += mult
    return best if best is not None else n


def _round_up(n, m):
    return (n + m - 1) // m * m


def _acc(ref, val, first):
    @pl.when(first)
    def _():
        ref[...] = val

    @pl.when(jnp.logical_not(first))
    def _():
        ref[...] += val


def _dot(a, b):
    return lax.dot_general(a, b, (((1,), (0,)), ((), ())), preferred_element_type=F32)


def _dot_nt(a, b):
    return lax.dot_general(a, b, (((1,), (1,)), ((), ())), preferred_element_type=F32)


def _dot_tn(a, b):
    return lax.dot_general(a, b, (((0,), (0,)), ((), ())), preferred_element_type=F32)


def _dot_f32(a, b):
    return lax.dot_general(a, b, (((1,), (0,)), ((), ())), preferred_element_type=F32, precision=HIGHEST)


_DOTS = {"nn": _dot, "nt": _dot_nt, "tn": _dot_tn}


def _sigmoid(x):
    return 1.0 / (1.0 + jnp.exp(-x))


def _softplus(x):
    return jnp.maximum(x, 0.0) + jnp.log1p(jnp.exp(-jnp.abs(x)))


def _silu_grad(x, s):
    return s * (1.0 + x * (1.0 - s))


def _mm(name, a_list, b_list, mode, *, tm, tn, out_dtypes=(F32,), epilogue=None, tiles=(), rows=(), cols=(),
        col_blocks=False, n_use=None, after=None):
    npair = len(a_list)
    if mode == "tn":
        m = a_list[0].shape[1]
    else:
        m = a_list[0].shape[0]
    n = n_use if n_use is not None else (b_list[0].shape[0] if mode == "nt" else b_list[0].shape[1])
    tm = _tile(m, tm, LANES if mode == "tn" else 8)
    tn = _tile(n, tn)
    assert m % tm == 0 and n % tn == 0, (name, m, n, tm, tn)
    dot = _DOTS[mode]

    def body(*refs):
        a_refs = refs[:npair]
        b_refs = refs[npair:2 * npair]
        n_extra = len(tiles) + len(rows) + len(cols)
        e_refs = refs[2 * npair:2 * npair + n_extra]
        o_refs = refs[2 * npair + n_extra + len(order):]
        acc = None
        for ar, br in zip(a_refs, b_refs):
            d = dot(ar[...], br[...])
            acc = d if acc is None else acc + d
        outs = epilogue(acc, *[e[...] for e in e_refs]) if epilogue is not None else (acc,)
        for o, v in zip(o_refs, outs):
            o[...] = v.astype(o.dtype)

    in_specs = []
    for a in a_list:
        if mode == "tn":
            in_specs.append(pl.BlockSpec((a.shape[0], tm), lambda i, j: (0, i)))
        else:
            in_specs.append(pl.BlockSpec((tm, a.shape[1]), lambda i, j: (i, 0)))
    for b in b_list:
        if mode == "nt":
            in_specs.append(pl.BlockSpec((tn, b.shape[1]), lambda i, j: (j, 0)))
        else:
            in_specs.append(pl.BlockSpec((b.shape[0], tn), lambda i, j: (0, j)))
    in_specs += [pl.BlockSpec((tm, tn), lambda i, j: (i, j)) for _ in tiles]
    in_specs += [pl.BlockSpec((1, tn), lambda i, j: (0, j)) for _ in rows]
    in_specs += [pl.BlockSpec((tm, 1), lambda i, j: (i, 0)) for _ in cols]
    order = [] if after is None else [after]
    in_specs += [pl.BlockSpec((8, LANES), lambda i, j: (0, 0)) for _ in order]
    outs = pl.pallas_call(
        body,
        name=name,
        grid=(m // tm, n // tn),
        in_specs=in_specs,
        out_specs=[pl.BlockSpec((None, tm, tn), lambda i, j: (j, i, 0)) if col_blocks else
                   pl.BlockSpec((tm, tn), lambda i, j: (i, j)) for _ in out_dtypes],
        out_shape=[jax.ShapeDtypeStruct((n // tn, m, tn) if col_blocks else (m, n), dt) for dt in out_dtypes],
        compiler_params=_params("parallel", "parallel"),
    )(*a_list, *b_list, *tiles, *rows, *cols, *order)
    return outs[0] if len(out_dtypes) == 1 else outs


def _rms(x, w):
    r = lax.rsqrt(jnp.mean(x * x, axis=-1, keepdims=True) + NORM_EPS)
    return x * r * w


def _rms_bwd(x, w, dy):
    r = lax.rsqrt(jnp.mean(x * x, axis=-1, keepdims=True) + NORM_EPS)
    xh = x * r
    g = dy * w
    dx = r * (g - xh * jnp.mean(g * xh, axis=-1, keepdims=True))
    return dx, dy * xh


def _row_specs(tr, d):
    return pl.BlockSpec((tr, d), lambda i: (i, 0)), pl.BlockSpec((1, d), lambda i: (0, 0))


def _prenorm(name, h, w, after):
    t, d = h.shape
    tr = _tile(t, 512, 8)
    row, vec = _row_specs(tr, d)

    def body(h_ref, w_ref, after_ref, u_ref):
        u_ref[...] = _rms(h_ref[...], w_ref[...]).astype(BF16)

    return pl.pallas_call(body, name=name, grid=(t // tr,),
                          in_specs=[row, vec, pl.BlockSpec((8, LANES), lambda i: (0, 0))], out_specs=row,
                          out_shape=jax.ShapeDtypeStruct((t, d), BF16), compiler_params=_params("parallel"))(
                              h, w, after)


def _post_pre(name, h, m, w_post, w_pre):
    t, d = h.shape
    tr = _tile(t, 512, 8)
    row, vec = _row_specs(tr, d)

    def body(h_ref, m_ref, wq_ref, wp_ref, hn_ref, u_ref):
        hn = h_ref[...] + _rms(m_ref[...], wq_ref[...])
        hn_ref[...] = hn
        u_ref[...] = _rms(hn, wp_ref[...]).astype(BF16)

    return pl.pallas_call(body, name=name, grid=(t // tr,), in_specs=[row, row, vec, vec], out_specs=[row, row],
                          out_shape=[jax.ShapeDtypeStruct((t, d), F32), jax.ShapeDtypeStruct((t, d), BF16)],
                          compiler_params=_params("parallel"))(h, m, w_post, w_pre)


def _final_loss(name, h, m, w_post, target):
    t, d = h.shape
    tr = _tile(t, 512, 8)
    row, vec = _row_specs(tr, d)

    def body(h_ref, m_ref, wq_ref, t_ref, dh_ref, loss_ref):
        err = h_ref[...] + _rms(m_ref[...], wq_ref[...]) - t_ref[...]
        dh_ref[...] = err * (1.0 / d)
        part = 0.5 * jnp.sum(jnp.mean(err * err, axis=-1, keepdims=True), axis=0, keepdims=True)
        _acc(loss_ref, jnp.broadcast_to(part, (1, LANES)), pl.program_id(0) == 0)

    return pl.pallas_call(body, name=name, grid=(t // tr,), in_specs=[row, row, vec, row],
                          out_specs=[row, pl.BlockSpec((1, LANES), lambda i: (0, 0))],
                          out_shape=[jax.ShapeDtypeStruct((t, d), F32), jax.ShapeDtypeStruct((1, LANES), F32)],
                          compiler_params=_params("arbitrary"))(h, m, w_post, target)


def _norm_bwd(name, dh, pre=None, post=None, after=None):
    t, d = dh.shape
    tr = _tile(t, 512, 8)
    row, vec = _row_specs(tr, d)
    has_pre, has_post = pre is not None, post is not None

    def body(*refs):
        it = iter(refs)
        dh_ref = next(it)
        if has_pre:
            du_ref, x_ref, wp_ref = next(it), next(it), next(it)
        if has_post:
            m_ref, wq_ref = next(it), next(it)
        if after is not None:
            next(it)
        first = pl.program_id(0) == 0
        dh_v = dh_ref[...]
        if has_pre:
            dhn_ref, dwp_ref = next(it), next(it)
            dx, dwr = _rms_bwd(x_ref[...], wp_ref[...], du_ref[...])
            dh_v = dh_v + dx
            dhn_ref[...] = dh_v
            _acc(dwp_ref, jnp.sum(dwr, axis=0, keepdims=True), first)
        if has_post:
            dm_ref, dwq_ref, dms_ref = next(it), next(it), next(it)
            dm, dwr = _rms_bwd(m_ref[...], wq_ref[...], dh_v)
            dm_ref[...] = dm.astype(BF16)
            _acc(dwq_ref, jnp.sum(dwr, axis=0, keepdims=True), first)
            _acc(dms_ref, jnp.sum(dm, axis=0, keepdims=True), first)

    ins, in_specs, out_specs, out_shape = [dh], [row], [], []
    if has_pre:
        ins += list(pre)
        in_specs += [row, row, vec]
        out_specs += [row, vec]
        out_shape += [jax.ShapeDtypeStruct((t, d), F32), jax.ShapeDtypeStruct((1, d), F32)]
    if has_post:
        ins += list(post)
        in_specs += [row, vec]
        out_specs += [row, vec, vec]
        out_shape += [jax.ShapeDtypeStruct((t, d), BF16), jax.ShapeDtypeStruct((1, d), F32),
                      jax.ShapeDtypeStruct((1, d), F32)]
    if after is not None:
        ins.append(after)
        in_specs.append(pl.BlockSpec((8, LANES), lambda i: (0, 0)))
    return pl.pallas_call(body, name=name, grid=(t // tr,), in_specs=in_specs, out_specs=out_specs,
                          out_shape=out_shape, compiler_params=_params("arbitrary"))(*ins)


HALO = 8


def _shift_later(cur, prev, s):
    rolled = pltpu.roll(cur, s, 0)
    row = lax.broadcasted_iota(jnp.int32, prev.shape, 0)
    first = jnp.where(row < s, pltpu.roll(prev, s, 0), rolled[0:HALO])
    return jnp.concatenate([first, rolled[HALO:]], axis=0)


def _shift_earlier(cur, nxt, s):
    tt = cur.shape[0]
    rolled = pltpu.roll(cur, tt - s, 0)
    row = lax.broadcasted_iota(jnp.int32, nxt.shape, 0)
    last = jnp.where(row >= HALO - s, pltpu.roll(nxt, HALO - s, 0), rolled[tt - HALO:])
    return jnp.concatenate([rolled[:tt - HALO], last], axis=0)


def _conv_fwd(zx, col0, n_ch, conv_w, conv_b):
    t = zx.shape[0]
    tc = _tile(n_ch, 512)
    tt = _tile(t, 1024, 8)
    cb0 = col0 // tc
    assert col0 % tc == 0
    kw = SSD_CONV_WIDTH

    def body(x_ref, p_ref, w_ref, b_ref, o_ref):
        cur = x_ref[...]
        prev = jnp.where(pl.program_id(1) > 0, p_ref[...], 0.0)
        w = w_ref[...]
        acc = b_ref[...] + w[kw - 1:kw, :] * cur
        for k in range(kw - 1):
            acc = acc + w[k:k + 1, :] * _shift_later(cur, prev, kw - 1 - k)
        o_ref[...] = acc

    return pl.pallas_call(
        body, name="ssd_conv_fwd", grid=(n_ch // tc, t // tt),
        in_specs=[pl.BlockSpec((tt, tc), lambda j, i: (i, cb0 + j)),
                  pl.BlockSpec((HALO, tc), lambda j, i: (jnp.maximum(i * (tt // HALO) - 1, 0), cb0 + j)),
                  pl.BlockSpec((kw, tc), lambda j, i: (0, j)),
                  pl.BlockSpec((1, tc), lambda j, i: (0, j))],
        out_specs=pl.BlockSpec((tt, tc), lambda j, i: (i, j)),
        out_shape=jax.ShapeDtypeStruct((t, n_ch), F32),
        compiler_params=_params("parallel", "parallel"))(zx, zx, conv_w, conv_b)


def _conv_bwd(name, dpre, zx, col0, conv_w):
    t, n_ch = dpre.shape
    tc = _tile(n_ch, 512)
    tt = _tile(t, 1024, 8)
    cb0 = col0 // tc
    kw = SSD_CONV_WIDTH
    nt = t // tt

    def body(d_ref, dn_ref, x_ref, p_ref, w_ref, dx_ref, dw_ref, db_ref):
        i = pl.program_id(1)
        d = d_ref[...]
        d_next = jnp.where(i < nt - 1, dn_ref[...], 0.0)
        x = x_ref[...]
        x_prev = jnp.where(i > 0, p_ref[...], 0.0)
        w = w_ref[...]
        dx = w[kw - 1:kw, :] * d
        for k in range(kw - 1):
            dx = dx + w[k:k + 1, :] * _shift_earlier(d, d_next, kw - 1 - k)
        dx_ref[...] = dx.astype(BF16)
        first = i == 0
        for k in range(kw):
            xs = x if k == kw - 1 else _shift_later(x, x_prev, kw - 1 - k)
            val = jnp.sum(d * xs, axis=0, keepdims=True)

            @pl.when(first)
            def _():
                dw_ref[k:k + 1, :] = val

            @pl.when(jnp.logical_not(first))
            def _():
                dw_ref[k:k + 1, :] += val
        _acc(db_ref, jnp.sum(d, axis=0, keepdims=True), first)

    return pl.pallas_call(
        body, name=name, grid=(n_ch // tc, nt),
        in_specs=[pl.BlockSpec((tt, tc), lambda j, i: (i, j)),
                  pl.BlockSpec((HALO, tc), lambda j, i: (jnp.minimum((i + 1) * (tt // HALO), t // HALO - 1), j)),
                  pl.BlockSpec((tt, tc), lambda j, i: (i, cb0 + j)),
                  pl.BlockSpec((HALO, tc), lambda j, i: (jnp.maximum(i * (tt // HALO) - 1, 0), cb0 + j)),
                  pl.BlockSpec((kw, tc), lambda j, i: (0, j))],
        out_specs=[pl.BlockSpec((tt, tc), lambda j, i: (i, j)),
                   pl.BlockSpec((kw, tc), lambda j, i: (0, j)),
                   pl.BlockSpec((1, tc), lambda j, i: (0, j))],
        out_shape=[jax.ShapeDtypeStruct((t, n_ch), BF16), jax.ShapeDtypeStruct((kw, n_ch), F32),
                   jax.ShapeDtypeStruct((1, n_ch), F32)],
        compiler_params=_params("parallel", "arbitrary"))(dpre, dpre, zx, zx, conv_w)


def _head_of_lane(shape, width):
    return lax.broadcasted_iota(jnp.int32, shape, len(shape) - 1) // width


def _select_dot(v, pick, pick_first=False):
    hi = v.astype(BF16)
    lo = (v - hi.astype(F32)).astype(BF16)
    return _dot(pick, hi) + _dot(pick, lo) if pick_first else _dot(hi, pick) + _dot(lo, pick)


def _expand(v, n_rows, on_mxu=False):
    if not on_mxu:
        head = _head_of_lane((n_rows, GW), SSD_HEAD_DIM)
        out = jnp.zeros((n_rows, GW), F32)
        for j in range(SSD_HPG):
            out = jnp.where(head == j, v[:, j:j + 1], out)
        return out
    src = lax.broadcasted_iota(jnp.int32, (LANES, GW), 0)
    return _select_dot(v, (src == _head_of_lane((LANES, GW), SSD_HEAD_DIM)).astype(BF16))


def _contract(v, n_rows, on_mxu=False):
    if not on_mxu:
        head = _head_of_lane((n_rows, GW), SSD_HEAD_DIM)
        lane = lax.broadcasted_iota(jnp.int32, (n_rows, LANES), 1)
        out = jnp.zeros((n_rows, LANES), F32)
        for j in range(SSD_HPG):
            s = jnp.sum(jnp.where(head == j, v, 0.0), axis=1, keepdims=True)
            out = jnp.where(lane == j, s, out)
        return out
    dst = lax.broadcasted_iota(jnp.int32, (GW, LANES), 1)
    return _select_dot(v, (lax.broadcasted_iota(jnp.int32, (GW, LANES), 0) // SSD_HEAD_DIM == dst).astype(BF16))


def _ssd_dt_prep(zdt, bias, alog, ng):
    t = zdt.shape[0]
    q = SSD_CHUNK

    def body(z_ref, b_ref, a_ref, dt_ref, cum_ref, cumr_ref, sg_ref):
        raw = z_ref[...] + b_ref[...]
        dt = _softplus(raw)
        sgd = _sigmoid(raw)
        row = lax.broadcasted_iota(jnp.int32, (q, q), 0)
        col = lax.broadcasted_iota(jnp.int32, (q, q), 1)
        cum = _dot_f32((col <= row).astype(F32), dt * (-jnp.exp(a_ref[...])))
        cum_t = cum.T
        lane = lax.broadcasted_iota(jnp.int32, (q, LANES), 1)
        for g in range(ng):
            shift = (LANES - g * SSD_HPG) % LANES

            def group(v):
                return jnp.where(lane < SSD_HPG, pltpu.roll(v, shift, 1) if shift else v, 0.0)

            dt_ref[g] = group(dt)
            cum_ref[g] = group(cum)
            sg_ref[g] = group(sgd)
            cumr_ref[g] = (pltpu.roll(cum_t, shift, 0) if shift else cum_t)[0:8, :]

    cols = pl.BlockSpec((ng, q, LANES), lambda c: (0, c, 0))
    vec = pl.BlockSpec((1, LANES), lambda c: (0, 0))
    col_shape = jax.ShapeDtypeStruct((ng, t, LANES), F32)
    return pl.pallas_call(body, name="ssd_dt_prep", grid=(t // q,),
                          in_specs=[pl.BlockSpec((q, LANES), lambda c: (c, 0)), vec, vec],
                          out_specs=[cols, cols, pl.BlockSpec((ng, 8, q), lambda c: (0, 0, c)), cols],
                          out_shape=[col_shape, col_shape, jax.ShapeDtypeStruct((ng, 8, t), F32), col_shape],
                          compiler_params=_params("parallel"))(zdt, bias, alog)


def _ssd_common(pre, dt, cum, cum_r, alog_c, on_mxu):
    q = SSD_CHUNK
    sg = _sigmoid(pre)
    act = pre * sg
    xa = act[:, :GW]
    bm = act[:, GW:GW + SSD_D_STATE].astype(BF16)
    cm = act[:, GW + SSD_D_STATE:].astype(BF16)
    row = lax.broadcasted_iota(jnp.int32, (q, q), 0)
    col = lax.broadcasted_iota(jnp.int32, (q, q), 1)
    tril = col <= row
    a_c = -jnp.exp(alog_c)
    g = _dot_nt(cm, bm)
    dt_x = _expand(dt, q, on_mxu)
    xdt = xa * dt_x
    cl = cum[q - 1:q, :]
    e_c = jnp.exp(cl - cum)
    lam_c = jnp.exp(cum)
    return dict(sg=sg, xa=xa, bm=bm, cm=cm, tril=tril, row=row, col=col, dt=dt, a_c=a_c, cum=cum, cum_r=cum_r,
                g=g, dt_x=dt_x, xdt=xdt, cl=cl, e_c=e_c, lam_c=lam_c)


SSD_GPS_FWD = 8
SSD_GPS_BWD = 2


def _ssd_specs(nc, rev, ng, gps):
    q = SSD_CHUNK
    xw, nw = gps * GW, gps * SSD_D_STATE
    b_off = ng * GW // nw
    c_off = (ng * GW + ng * SSD_D_STATE) // nw
    assert ng % gps == 0 and (ng * GW) % nw == 0 and (ng * SSD_D_STATE) % nw == 0

    def ch(c):
        return nc - 1 - c if rev else c

    chunk_grp = [pl.BlockSpec((q, xw), lambda g, c: (ch(c), g)),
                 pl.BlockSpec((q, nw), lambda g, c: (ch(c), b_off + g)),
                 pl.BlockSpec((q, nw), lambda g, c: (ch(c), c_off + g))]
    col_form = pl.BlockSpec((gps, q, LANES), lambda g, c: (g, ch(c), 0))
    row_form = pl.BlockSpec((gps, 8, q), lambda g, c: (g, 0, ch(c)))
    col_par = pl.BlockSpec((gps, 1, LANES), lambda g, c: (g, 0, 0))
    y_spec = pl.BlockSpec((q, xw), lambda g, c: (ch(c), g))
    st_spec = pl.BlockSpec((gps, None, GW, SSD_D_STATE), lambda g, c: (g, ch(c), 0, 0))
    bc_spec = pl.BlockSpec((q, nw), lambda g, c: (ch(c), g))
    return chunk_grp, col_form, row_form, col_par, y_spec, st_spec, bc_spec


def _ssd_group_views(gi, wide, narrow, stacked):
    xs, ns = pl.ds(gi * GW, GW), pl.ds(gi * SSD_D_STATE, SSD_D_STATE)
    return [r.at[:, xs] for r in wide], [r.at[:, ns] for r in narrow], [r.at[gi] for r in stacked]


def _ssd_fwd(pre, dt_c, cum_c, cum_r, alog_c, dsk_c):
    t = pre.shape[0]
    ng = pre.shape[1] // GC
    q = SSD_CHUNK
    nc = t // q
    gps = SSD_GPS_FWD if ng % SSD_GPS_FWD == 0 else SSD_GPS_BWD
    chunk_grp, col_form, row_form, col_par, y_spec, st_spec, _ = _ssd_specs(nc, False, ng, gps)

    def body(px_ref, pb_ref, pc_ref, dt_ref, cum_ref, cumr_ref, ac_ref, dk_ref, y_ref, sp_ref, st_ref):
        @pl.when(pl.program_id(1) == 0)
        def _():
            st_ref[...] = jnp.zeros_like(st_ref)

        for gi in range(gps):
            (px, y), (pb, pc), rest = _ssd_group_views(
                gi, (px_ref, y_ref), (pb_ref, pc_ref), (dt_ref, cum_ref, cumr_ref, ac_ref, dk_ref, sp_ref, st_ref))
            one_group(px, pb, pc, *rest[:5], y, *rest[5:])

    def one_group(px_ref, pb_ref, pc_ref, dt_ref, cum_ref, cumr_ref, ac_ref, dk_ref, y_ref, sp_ref, st_ref):
        pre_v = jnp.concatenate([px_ref[...], pb_ref[...], pc_ref[...]], axis=1)
        v = _ssd_common(pre_v, dt_ref[...], cum_ref[...], cumr_ref[...], ac_ref[...], False)
        s0 = st_ref[...]
        sp_ref[...] = s0
        r = _dot_nt(v["cm"], s0.astype(BF16))
        y = _expand(v["lam_c"], q) * r + _expand(dk_ref[...], 1) * v["xa"]
        head = _head_of_lane((q, GW), SSD_HEAD_DIM)
        for j in range(SSD_HPG):
            diff = v["cum"][:, j:j + 1] - v["cum_r"][j:j + 1, :]
            w = (v["g"] * jnp.exp(jnp.where(v["tril"], diff, -jnp.inf))).astype(BF16)
            y = y + _dot(w, jnp.where(head == j, v["xdt"], 0.0).astype(BF16))
        y_ref[...] = y
        ds = _dot_tn((v["xdt"] * _expand(v["e_c"], q)).astype(BF16), v["bm"])
        for j in range(SSD_HPG):
            rows = slice(j * SSD_HEAD_DIM, (j + 1) * SSD_HEAD_DIM)
            st_ref[rows, :] = s0[rows, :] * jnp.exp(v["cum_r"][j:j + 1, q - 1:q]) + ds[rows, :]

    return pl.pallas_call(
        body, name="ssd_scan_fwd", grid=(ng // gps, nc),
        in_specs=chunk_grp + [col_form, col_form, row_form, col_par, col_par],
        out_specs=[y_spec, st_spec],
        out_shape=[jax.ShapeDtypeStruct((t, ng * GW), F32), jax.ShapeDtypeStruct((ng, nc, GW, SSD_D_STATE), F32)],
        scratch_shapes=[pltpu.VMEM((gps, GW, SSD_D_STATE), F32)],
        compiler_params=_params("parallel", "arbitrary"))(pre, pre, pre, dt_c, cum_c, cum_r, alog_c, dsk_c)


def _ssd_bwd(dy, pre, states, dt_c, cum_c, cum_r, sgd_c, alog_c, dsk_c):
    t = pre.shape[0]
    ng = pre.shape[1] // GC
    q = SSD_CHUNK
    nc = t // q
    gps = SSD_GPS_BWD
    chunk_grp, col_form, row_form, col_par, y_spec, st_spec, bc_spec = _ssd_specs(nc, True, ng, gps)

    def body(dy_ref, px_ref, pb_ref, pc_ref, sp_ref, dt_ref, cum_ref, cumr_ref, sgd_ref, ac_ref, dk_ref,
             dpx_ref, dpb_ref, dpc_ref, ddt_ref, dbias_ref, dalog_ref, dd_ref, ds_ref):
        @pl.when(pl.program_id(1) == 0)
        def _():
            ds_ref[...] = jnp.zeros_like(ds_ref)

        for gi in range(gps):
            (dy, px, dpx), (pb, pc, dpb, dpc), rest = _ssd_group_views(
                gi, (dy_ref, px_ref, dpx_ref), (pb_ref, pc_ref, dpb_ref, dpc_ref),
                (sp_ref, dt_ref, cum_ref, cumr_ref, sgd_ref, ac_ref, dk_ref, ddt_ref, dbias_ref, dalog_ref, dd_ref,
                 ds_ref))
            one_group(dy, px, pb, pc, *rest[:7], dpx, dpb, dpc, *rest[7:])

    def one_group(dy_ref, px_ref, pb_ref, pc_ref, sp_ref, dt_ref, cum_ref, cumr_ref, sgd_ref, ac_ref, dk_ref,
                  dpx_ref, dpb_ref, dpc_ref, ddt_ref, dbias_ref, dalog_ref, dd_ref, ds_ref):
        first = pl.program_id(1) == 0
        pre_v = jnp.concatenate([px_ref[...], pb_ref[...], pc_ref[...]], axis=1)
        v = _ssd_common(pre_v, dt_ref[...], cum_ref[...], cumr_ref[...], ac_ref[...], True)
        xa, bm, cm, xdt, cum, cum_r = v["xa"], v["bm"], v["cm"], v["xdt"], v["cum"], v["cum_r"]
        xdt_b = xdt.astype(BF16)
        dy_v = dy_ref[...]
        s0 = sp_ref[...]
        ds1 = ds_ref[...]
        s0b, ds1b = s0.astype(BF16), ds1.astype(BF16)
        head = _head_of_lane((q, GW), SSD_HEAD_DIM)
        lane = lax.broadcasted_iota(jnp.int32, (q, LANES), 1)
        lane1 = lax.broadcasted_iota(jnp.int32, (1, LANES), 1)
        lam_x = _expand(v["lam_c"], q, True)
        e_x = _expand(v["e_c"], q, True)

        dxa = _expand(dk_ref[...], 1) * dy_v
        dd = _contract(jnp.sum(dy_v * xa, axis=0, keepdims=True), 1)
        r = _dot_nt(cm, s0b)
        dcum = _contract(dy_v * r * lam_x, q, True)
        drb = (lam_x * dy_v).astype(BF16)
        dc = _dot(drb, s0b)
        ds0 = _dot_tn(drb, cm)
        extra = jnp.zeros((1, LANES), F32)
        for j in range(SSD_HPG):
            rows = slice(j * SSD_HEAD_DIM, (j + 1) * SSD_HEAD_DIM)
            lam_last = jnp.exp(cum_r[j:j + 1, q - 1:q])
            ds_ref[rows, :] = ds0[rows, :] + lam_last * ds1[rows, :]
            tot = jnp.sum(jnp.sum(ds1[rows, :] * s0[rows, :], axis=1, keepdims=True), axis=0, keepdims=True)
            extra = jnp.where(lane1 == j, lam_last * tot, extra)
        dv = _dot_nt(bm, ds1b)
        db = _dot((xdt * e_x).astype(BF16), ds1b)
        dxdt = e_x * dv
        dee = _contract(dv * xdt, q, True) * v["e_c"]
        dcum = dcum - dee
        extra = extra + jnp.sum(dee, axis=0, keepdims=True)
        dg = jnp.zeros((q, q), F32)
        col_sums = jnp.zeros((q, q), F32)
        for j in range(SSD_HPG):
            diff = cum[:, j:j + 1] - cum_r[j:j + 1, :]
            el = jnp.exp(jnp.where(v["tril"], diff, -jnp.inf))
            gl = v["g"] * el
            dym = jnp.where(head == j, dy_v, 0.0).astype(BF16)
            dwm = _dot_nt(dym, xdt_b)
            dxdt = dxdt + _dot_tn(gl.astype(BF16), dym)
            z = dwm * gl
            dcum = jnp.where(lane == j, dcum + jnp.sum(z, axis=1, keepdims=True), dcum)
            col_sums = jnp.where(v["row"] == j, jnp.sum(z, axis=0, keepdims=True), col_sums)
            dg = dg + dwm * el
        dcum = dcum - col_sums.T
        dgb = dg.astype(BF16)
        dc = dc + _dot(dgb, bm)
        db = db + _dot_tn(dgb, cm)
        da = _select_dot(dcum, (v["row"] <= v["col"]).astype(BF16), True) + extra
        ddt = _contract(dxdt * xa, q, True) + v["a_c"] * da
        dalog = jnp.sum(v["dt"] * da, axis=0, keepdims=True) * v["a_c"]
        dxa = dxa + v["dt_x"] * dxdt
        ddt_raw = jnp.where(lane < SSD_HPG, ddt * sgd_ref[...], 0.0)
        sgrad = _silu_grad(pre_v, v["sg"])
        dpx_ref[...] = dxa * sgrad[:, :GW]
        dpb_ref[...] = db * sgrad[:, GW:GW + SSD_D_STATE]
        dpc_ref[...] = dc * sgrad[:, GW + SSD_D_STATE:]
        ddt_ref[...] = ddt_raw
        _acc(dbias_ref, jnp.sum(ddt_raw, axis=0, keepdims=True), first)
        _acc(dalog_ref, jnp.where(lane1 < SSD_HPG, dalog, 0.0), first)
        _acc(dd_ref, dd, first)

    return pl.pallas_call(
        body, name="ssd_scan_bwd", grid=(ng // gps, nc),
        in_specs=[y_spec] + chunk_grp + [st_spec, col_form, col_form, row_form, col_form, col_par, col_par],
        out_specs=[y_spec, bc_spec, bc_spec, col_form, col_par, col_par, col_par],
        out_shape=[jax.ShapeDtypeStruct((t, ng * GW), F32), jax.ShapeDtypeStruct((t, ng * SSD_D_STATE), F32),
                   jax.ShapeDtypeStruct((t, ng * SSD_D_STATE), F32), jax.ShapeDtypeStruct((ng, t, LANES), F32),
                   jax.ShapeDtypeStruct((ng, 1, LANES), F32), jax.ShapeDtypeStruct((ng, 1, LANES), F32),
                   jax.ShapeDtypeStruct((ng, 1, LANES), F32)],
        scratch_shapes=[pltpu.VMEM((gps, GW, SSD_D_STATE), F32)],
        compiler_params=_params("parallel", "arbitrary"))(dy, pre, pre, pre, states, dt_c, cum_c, cum_r, sgd_c, alog_c,
                                                           dsk_c)


def _gate_norm_fwd(y, zx, norm_w):
    t, di = y.shape
    tr = _tile(t, 512, 8)
    ng = di // GW

    def body(y_ref, z_ref, w_ref, o_ref):
        z = z_ref[...]
        gate = y_ref[...] * (z * _sigmoid(z))
        w = w_ref[...]
        for g in range(ng):
            cols = slice(g * GW, (g + 1) * GW)
            gs = gate[:, cols]
            r = lax.rsqrt(jnp.mean(gs * gs, axis=-1, keepdims=True) + NORM_EPS)
            o_ref[:, cols] = (gs * r * w[:, cols]).astype(BF16)

    row = pl.BlockSpec((tr, di), lambda i: (i, 0))
    return pl.pallas_call(body, name="ssd_gate_norm_fwd", grid=(t // tr,),
                          in_specs=[row, row, pl.BlockSpec((1, di), lambda i: (0, 0))], out_specs=row,
                          out_shape=jax.ShapeDtypeStruct((t, di), BF16), compiler_params=_params("parallel"))(
                              y, zx, norm_w)


def _gate_norm_bwd(dyn, y, zx, norm_w, after):
    t, di = y.shape
    tr = _tile(t, 256, 8)
    ng = di // GW

    def body(d_ref, y_ref, z_ref, w_ref, after_ref, dy_ref, dz_ref, dw_ref):
        z = z_ref[...]
        yv = y_ref[...]
        sg = _sigmoid(z)
        sz = z * sg
        gate = yv * sz
        w = w_ref[...]
        d = d_ref[...]
        dsz = _silu_grad(z, sg)
        dws = []
        for g in range(ng):
            cols = slice(g * GW, (g + 1) * GW)
            dg, dwr = _rms_bwd(gate[:, cols], w[:, cols], d[:, cols])
            dy_ref[:, cols] = dg * sz[:, cols]
            dz_ref[:, cols] = (dg * yv[:, cols] * dsz[:, cols]).astype(BF16)
            dws.append(jnp.sum(dwr, axis=0, keepdims=True))
        first = pl.program_id(0) == 0
        for g in range(ng):
            cols = slice(g * GW, (g + 1) * GW)

            @pl.when(first)
            def _():
                dw_ref[:, cols] = dws[g]

            @pl.when(jnp.logical_not(first))
            def _():
                dw_ref[:, cols] += dws[g]

    row = pl.BlockSpec((tr, di), lambda i: (i, 0))
    vec = pl.BlockSpec((1, di), lambda i: (0, 0))
    return pl.pallas_call(body, name="ssd_gate_norm_bwd", grid=(t // tr,),
                          in_specs=[row, row, row, vec, pl.BlockSpec((8, LANES), lambda i: (0, 0))],
                          out_specs=[row, row, vec],
                          out_shape=[jax.ShapeDtypeStruct((t, di), F32), jax.ShapeDtypeStruct((t, di), BF16),
                                     jax.ShapeDtypeStruct((1, di), F32)],
                          compiler_params=_params("arbitrary"))(dyn, y, zx, norm_w, after)


def _attn_mask_t(n):
    w = ATTN_WINDOW
    kpos = lax.broadcasted_iota(jnp.int32, (2 * w, ATTN_REP * w), 0)
    qpos = lax.broadcasted_iota(jnp.int32, (2 * w, ATTN_REP * w), 1) % w + w
    rel = qpos - kpos
    return (rel >= 0) & (rel < w) & jnp.logical_not((n == 0) & (kpos < w))


def _attn_probs_t(qts, ktb, mask, sink):
    s = _dot_tn(ktb, qts) * (ATTN_HEAD_DIM ** -0.5)
    s = jnp.where(mask, s, -jnp.inf)
    m = jnp.maximum(jnp.max(s, axis=0, keepdims=True), sink)
    e = jnp.exp(s - m)
    es = jnp.exp(sink - m)
    inv = 1.0 / (jnp.sum(e, axis=0, keepdims=True) + es)
    return e * inv, es * inv


def _attn_blocks_t(kv, q_ref, kc_ref, vc_ref, kp_ref, vp_ref):
    hd = ATTN_HEAD_DIM
    rows = slice(kv * hd, (kv + 1) * hd)
    ktb = jnp.concatenate([kp_ref[rows, :], kc_ref[rows, :]], axis=1)
    vtb = jnp.concatenate([vp_ref[rows, :], vc_ref[rows, :]], axis=1)
    qts = jnp.concatenate([q_ref[(kv * ATTN_REP + r) * hd:(kv * ATTN_REP + r + 1) * hd, :]
                           for r in range(ATTN_REP)], axis=1)
    return qts, ktb, vtb


def _attn_specs_t(nb, cur, prev):
    w, hd = ATTN_WINDOW, ATTN_HEAD_DIM
    kd = ATTN_N_KV * hd
    qd = ATTN_REP * kd
    return [pl.BlockSpec((qd, w), lambda n: (0, cur(n))),
            pl.BlockSpec((kd, w), lambda n: (ATTN_REP, cur(n))),
            pl.BlockSpec((kd, w), lambda n: (ATTN_REP + 1, cur(n))),
            pl.BlockSpec((kd, w), lambda n: (ATTN_REP, prev(n))),
            pl.BlockSpec((kd, w), lambda n: (ATTN_REP + 1, prev(n)))]


def _attn_fwd_t(qkv_t, sinks_rep):
    t = qkv_t.shape[1]
    w, hd = ATTN_WINDOW, ATTN_HEAD_DIM
    qd = ATTN_N_KV * ATTN_REP * hd
    nb = t // w

    def body(q_ref, kc_ref, vc_ref, kp_ref, vp_ref, s_ref, o_ref):
        mask = _attn_mask_t(pl.program_id(0))
        for kv in range(ATTN_N_KV):
            qts, ktb, vtb = _attn_blocks_t(kv, q_ref, kc_ref, vc_ref, kp_ref, vp_ref)
            p, _ = _attn_probs_t(qts, ktb, mask, s_ref[kv])
            ots = _dot(vtb, p.astype(BF16))
            for r in range(ATTN_REP):
                h = kv * ATTN_REP + r
                o_ref[h * hd:(h + 1) * hd, :] = ots[:, r * w:(r + 1) * w].astype(BF16)

    return pl.pallas_call(
        body, name="attn_fwd", grid=(nb,),
        in_specs=_attn_specs_t(nb, lambda n: n, lambda n: jnp.maximum(n - 1, 0)) + [
            pl.BlockSpec(sinks_rep.shape, lambda n: (0, 0, 0))],
        out_specs=pl.BlockSpec((qd, w), lambda n: (0, n)),
        out_shape=jax.ShapeDtypeStruct((qd, t), BF16),
        compiler_params=_params("parallel"))(qkv_t, qkv_t, qkv_t, qkv_t, qkv_t, sinks_rep)


def _attn_bwd_t(qkv_t, do_t, sinks_rep):
    t = qkv_t.shape[1]
    w, hd = ATTN_WINDOW, ATTN_HEAD_DIM
    kd = ATTN_N_KV * hd
    qd = ATTN_REP * kd
    nq = ATTN_N_KV * ATTN_REP
    nb = t // w
    rows_all = qd + 2 * kd

    def body(q_ref, kc_ref, vc_ref, kp_ref, vp_ref, do_ref, s_ref, dqkv_ref, bsum_ref, dsk_ref,
             carry_ref, new_ref, bacc_ref, sacc_ref):
        n = pl.program_id(0)

        @pl.when(n == 0)
        def _():
            carry_ref[...] = jnp.zeros_like(carry_ref)
            bacc_ref[...] = jnp.zeros_like(bacc_ref)
            sacc_ref[...] = jnp.zeros_like(sacc_ref)

        @pl.when(n < nb)
        def _():
            mask = _attn_mask_t(n)
            for kv in range(ATTN_N_KV):
                qts, ktb, vtb = _attn_blocks_t(kv, q_ref, kc_ref, vc_ref, kp_ref, vp_ref)
                dots = jnp.concatenate([do_ref[(kv * ATTN_REP + r) * hd:(kv * ATTN_REP + r + 1) * hd, :]
                                        for r in range(ATTN_REP)], axis=1)
                p, ps = _attn_probs_t(qts, ktb, mask, s_ref[kv])
                dpt = _dot_tn(vtb, dots)
                delta = jnp.sum(p * dpt, axis=0, keepdims=True)
                dst = (p * (dpt - delta) * (hd ** -0.5)).astype(BF16)
                dqts = _dot(ktb, dst)
                for r in range(ATTN_REP):
                    h = kv * ATTN_REP + r
                    new_ref[h * hd:(h + 1) * hd, :] = dqts[:, r * w:(r + 1) * w]
                dktb = _dot_nt(qts, dst)
                dvtb = _dot_nt(dots, p.astype(BF16))
                krows = slice(qd + kv * hd, qd + (kv + 1) * hd)
                vrows = slice(qd + kd + kv * hd, qd + kd + (kv + 1) * hd)
                carry_ref[krows, :] += dktb[:, :w]
                carry_ref[vrows, :] += dvtb[:, :w]
                new_ref[krows, :] = dktb[:, w:]
                new_ref[vrows, :] = dvtb[:, w:]
                sacc_ref[kv] += -(ps * delta)

        @pl.when(n >= 1)
        def _():
            done = carry_ref[...]
            dqkv_ref[...] = done.astype(BF16)
            bacc_ref[...] += done

        @pl.when(n < nb)
        def _():
            carry_ref[...] = new_ref[...]

        @pl.when(n == nb)
        def _():
            bsum_ref[...] = jnp.sum(bacc_ref[...], axis=1, keepdims=True)
            lane = lax.broadcasted_iota(jnp.int32, (1, nq), 1)
            dsk = jnp.zeros((1, nq), F32)
            for kv in range(ATTN_N_KV):
                acc = sacc_ref[kv]
                for r in range(ATTN_REP):
                    tot = jnp.sum(acc[:, r * w:(r + 1) * w], axis=1, keepdims=True)
                    dsk = jnp.where(lane == kv * ATTN_REP + r, tot, dsk)
            dsk_ref[...] = dsk

    cur = lambda n: jnp.minimum(n, nb - 1)
    prev = lambda n: jnp.maximum(jnp.minimum(n, nb - 1) - 1, 0)
    return pl.pallas_call(
        body, name="attn_bwd", grid=(nb + 1,),
        in_specs=_attn_specs_t(nb, cur, prev) + [pl.BlockSpec((qd, w), lambda n: (0, cur(n))),
                                                 pl.BlockSpec(sinks_rep.shape, lambda n: (0, 0, 0))],
        out_specs=[pl.BlockSpec((rows_all, w), lambda n: (0, jnp.maximum(n - 1, 0))),
                   pl.BlockSpec((rows_all, 1), lambda n: (0, 0)),
                   pl.BlockSpec((1, nq), lambda n: (0, 0))],
        out_shape=[jax.ShapeDtypeStruct((rows_all, t), BF16), jax.ShapeDtypeStruct((rows_all, 1), F32),
                   jax.ShapeDtypeStruct((1, nq), F32)],
        scratch_shapes=[pltpu.VMEM((rows_all, w), F32), pltpu.VMEM((rows_all, w), F32),
                        pltpu.VMEM((rows_all, w), F32), pltpu.VMEM(sinks_rep.shape, F32)],
        compiler_params=_params("arbitrary"))(qkv_t, qkv_t, qkv_t, qkv_t, qkv_t, do_t, sinks_rep)


HBM_SPEC = pl.BlockSpec(memory_space=pl.ANY)
HBM_ONLY = pl.BlockSpec(memory_space=pltpu.HBM)


def _comm_call(name, body, ins, out_shapes, n_sems):
    return pl.pallas_call(
        body, name=name, in_specs=[HBM_SPEC] * len(ins), out_specs=[HBM_SPEC] * len(out_shapes),
        out_shape=out_shapes,
        scratch_shapes=[pltpu.SemaphoreType.DMA((s,)) for s in n_sems])(*ins)


def _all_gather(name, shards, after):
    n = len(shards)
    na = len(after)

    def body(*refs):
        x_refs, out_refs = refs[:n], refs[n + na:2 * n + na]
        send_sems, recv_sems, local_sems = refs[2 * n + na:]
        x, y, c = lax.axis_index("x"), lax.axis_index("y"), lax.axis_index("c")
        me, sibling = (x, y, c), (x, y, 1 - c)
        chips = [(1 - x, y), (x, 1 - y), (1 - x, 1 - y)]

        def slot(i, px, py, pc):
            return out_refs[i].at[4 * px + 2 * py + pc]

        def copy(k, i, block, to, src=None):
            return pltpu.make_async_remote_copy(
                src_ref=slot(i, *block) if src is None else src, dst_ref=slot(i, *block),
                send_sem=send_sems.at[k * n + i], recv_sem=recv_sems.at[k * n + i], device_id=to,
                device_id_type=MESH)

        mine = [pltpu.make_async_copy(x_refs[i], slot(i, *me), local_sems.at[i]) for i in range(n)]
        first = []
        for i in range(n):
            mine[i].start()
            first.append(copy(0, i, me, sibling, src=x_refs[i]))
            first += [copy(1 + j, i, me, (*chip, c), src=x_refs[i]) for j, chip in enumerate(chips)]
        for cp in first:
            cp.start()
        passed = []
        for i in range(n):
            for j, chip in enumerate(chips):
                copy(1 + j, i, (*chip, c), me).wait_recv()
                passed.append(copy(4 + j, i, (*chip, c), sibling))
                passed[-1].start()
        for i in range(n):
            copy(0, i, sibling, me).wait_recv()
            for j, chip in enumerate(chips):
                copy(4 + j, i, (*chip, 1 - c), me).wait_recv()
        for cp in first + passed:
            cp.wait_send()
        for cp in mine:
            cp.wait()

    outs = [jax.ShapeDtypeStruct((N_DEV,) + s.shape, s.dtype) for s in shards]
    return _comm_call(name, body, list(shards) + list(after), outs, (7 * n, 7 * n, n))


SEM_SPEC = pl.BlockSpec(memory_space=pltpu.SEMAPHORE)
SPLIT_COPY_EFFECT = pltpu.SideEffectType.DATAFLOW_SIDE_EFFECTING


def _in_hbm(a):
    return pltpu.with_memory_space_constraint(a, pltpu.HBM)


def _split_start(name, body, srcs, lands, n_sems):
    n = len(srcs)
    bufs = [_in_hbm(a) for a in list(srcs) + list(lands)]
    outs = pl.pallas_call(
        body, name=name,
        out_shape=(pltpu.SemaphoreType.DMA((n_sems,)), pltpu.SemaphoreType.DMA((n_sems,)),
                   *[pltpu.HBM(a.shape, a.dtype) for a in bufs], jax.ShapeDtypeStruct((8, LANES), F32)),
        in_specs=[HBM_ONLY] * (2 * n),
        out_specs=(SEM_SPEC, SEM_SPEC, *[HBM_ONLY] * (2 * n), pl.BlockSpec(memory_space=pltpu.VMEM)),
        input_output_aliases={i: 2 + i for i in range(2 * n)},
        compiler_params=pltpu.CompilerParams(has_side_effects=SPLIT_COPY_EFFECT))(*bufs)
    return outs[0], outs[1], list(outs[2:2 + n]), list(outs[2 + n:2 + 2 * n]), outs[-1]


def _split_wait(name, body, send_sems, recv_sems, srcs, lands, after):
    n = len(srcs)
    outs = pl.pallas_call(
        body, name=name,
        out_shape=[pltpu.HBM(a.shape, a.dtype) for a in list(srcs) + list(lands)],
        in_specs=[HBM_ONLY] * (2 * n) + [SEM_SPEC, SEM_SPEC, HBM_SPEC],
        out_specs=[HBM_ONLY] * (2 * n),
        input_output_aliases={i: i for i in range(2 * n)},
        compiler_params=pltpu.CompilerParams(has_side_effects=SPLIT_COPY_EFFECT))(
            *srcs, *lands, send_sems, recv_sems, after)
    return list(outs[:n]), list(outs[n:])


N_PEERS = N_DEV - 1


def _gather_peers():
    x, y, c = lax.axis_index("x"), lax.axis_index("y"), lax.axis_index("c")
    flips = [(fx, fy, fc) for fx in (0, 1) for fy in (0, 1) for fc in (0, 1) if fx or fy or fc]
    return [(1 - x if fx else x, 1 - y if fy else y, 1 - c if fc else c) for fx, fy, fc in flips]


def _block_id(dev):
    return 4 * dev[0] + 2 * dev[1] + dev[2]


def _landing_block(land_ref, shard_shape, side_by_side, dev):
    if not side_by_side:
        return land_ref.at[_block_id(dev)]
    cols = shard_shape[1]
    return land_ref.at[:, pl.ds(pl.multiple_of(_block_id(dev) * cols, LANES), cols)]


def _gather_start(name, shards, side_by_side):
    n = len(shards)

    def body(*refs):
        x_refs, land_refs = refs[:n], refs[n:2 * n]
        send_sems, recv_sems, token = refs[2 * n], refs[2 * n + 1], refs[-1]
        me = (lax.axis_index("x"), lax.axis_index("y"), lax.axis_index("c"))
        for i in range(n):
            for k, peer in enumerate(_gather_peers()):
                pltpu.make_async_remote_copy(
                    src_ref=x_refs[i], dst_ref=_landing_block(land_refs[i], shards[i].shape, side_by_side[i], me),
                    send_sem=send_sems.at[N_PEERS * i + k], recv_sem=recv_sems.at[N_PEERS * i + k],
                    device_id=peer, device_id_type=MESH).start()
            pltpu.make_async_copy(x_refs[i], _landing_block(land_refs[i], shards[i].shape, side_by_side[i], me),
                                  send_sems.at[N_PEERS * n + i]).start()
        token[...] = jnp.zeros_like(token)

    lands = [lax.empty((s.shape[0], N_DEV * s.shape[1]) if wide else (N_DEV,) + s.shape, s.dtype)
             for s, wide in zip(shards, side_by_side)]
    return _split_start(name, body, shards, lands, (N_PEERS + 1) * n)


def _gather_wait(name, send_sems, recv_sems, first, n_all, shards, lands, side_by_side, after):
    n = len(shards)

    def body(*refs):
        x_refs, land_refs = refs[:n], refs[n:2 * n]
        send_sems, recv_sems = refs[2 * n], refs[2 * n + 1]
        me = (lax.axis_index("x"), lax.axis_index("y"), lax.axis_index("c"))
        for i in range(n):
            pltpu.make_async_copy(x_refs[i], _landing_block(land_refs[i], shards[i].shape, side_by_side[i], me),
                                  send_sems.at[N_PEERS * n_all + first + i]).wait()
            for k, peer in enumerate(_gather_peers()):
                cp = pltpu.make_async_remote_copy(
                    src_ref=x_refs[i], dst_ref=_landing_block(land_refs[i], shards[i].shape, side_by_side[i], peer),
                    send_sem=send_sems.at[N_PEERS * (first + i) + k],
                    recv_sem=recv_sems.at[N_PEERS * (first + i) + k],
                    device_id=peer, device_id_type=MESH)
                cp.wait_send()
                cp.wait_recv()

    return _split_wait(name, body, send_sems, recv_sems, shards, lands, after)


def _scatter_start(name, blocks):
    n = len(blocks)

    def body(*refs):
        b_refs, land_refs = refs[:n], refs[n:2 * n]
        send_sems, recv_sems, token = refs[2 * n], refs[2 * n + 1], refs[-1]
        me = (lax.axis_index("x"), lax.axis_index("y"), lax.axis_index("c"))
        for i in range(n):
            for k, peer in enumerate(_gather_peers()):
                pltpu.make_async_remote_copy(
                    src_ref=b_refs[i].at[_block_id(peer)], dst_ref=land_refs[i].at[_block_id(me)],
                    send_sem=send_sems.at[N_PEERS * i + k], recv_sem=recv_sems.at[N_PEERS * i + k],
                    device_id=peer, device_id_type=MESH).start()
            pltpu.make_async_copy(b_refs[i].at[_block_id(me)], land_refs[i].at[_block_id(me)],
                                  send_sems.at[N_PEERS * n + i]).start()
        token[...] = jnp.zeros_like(token)

    lands = [lax.empty(b.shape, b.dtype) for b in blocks]
    return _split_start(name, body, blocks, lands, (N_PEERS + 1) * n)


def _scatter_wait(name, send_sems, recv_sems, blocks, lands, after):
    n = len(blocks)

    def body(*refs):
        b_refs, land_refs = refs[:n], refs[n:2 * n]
        send_sems, recv_sems = refs[2 * n], refs[2 * n + 1]
        me = (lax.axis_index("x"), lax.axis_index("y"), lax.axis_index("c"))
        for i in range(n):
            pltpu.make_async_copy(b_refs[i].at[_block_id(me)], land_refs[i].at[_block_id(me)],
                                  send_sems.at[N_PEERS * n + i]).wait()
            for k, peer in enumerate(_gather_peers()):
                cp = pltpu.make_async_remote_copy(
                    src_ref=b_refs[i].at[_block_id(peer)], dst_ref=land_refs[i].at[_block_id(peer)],
                    send_sem=send_sems.at[N_PEERS * i + k], recv_sem=recv_sems.at[N_PEERS * i + k],
                    device_id=peer, device_id_type=MESH)
                cp.wait_send()
                cp.wait_recv()

    return _split_wait(name, body, send_sems, recv_sems, blocks, lands, after)


def _adamw(w, g, m, v):
    m = ADAM_B1 * m + (1.0 - ADAM_B1) * g
    v = ADAM_B2 * v + (1.0 - ADAM_B2) * (g * g)
    m_hat = m / (1.0 - ADAM_B1 ** ADAM_STEP)
    v_hat = v / (1.0 - ADAM_B2 ** ADAM_STEP)
    delta = -ADAM_LR * (m_hat / (jnp.sqrt(v_hat) + ADAM_EPS) + ADAM_WD * w)
    return delta, m, v


def _adamw_tiles(r, c_):
    tr = _tile(r, 256, 16)
    return (tr, c_) if tr < r or r <= 256 else (r, _tile(c_, 256))


def _sum_parts(part):
    g = part[0].astype(F32)
    for k in range(1, part.shape[0]):
        g = g + part[k].astype(F32)
    return g


def _sum_adamw(name, parts, w, m, v):
    r, c_ = w.shape
    tr, tc = _adamw_tiles(r, c_)

    def body(p_ref, w_ref, m_ref, v_ref, g_ref, d_ref, nm_ref, nv_ref):
        g = _sum_parts(p_ref)
        g_ref[...] = g
        d_ref[...], nm_ref[...], nv_ref[...] = _adamw(w_ref[...], g, m_ref[...], v_ref[...])

    tile = pl.BlockSpec((tr, tc), lambda i, j: (i, j))
    return pl.pallas_call(body, name=name, grid=(r // tr, c_ // tc),
                          in_specs=[pl.BlockSpec((parts.shape[0], tr, tc), lambda i, j: (0, i, j)), tile, tile, tile],
                          out_specs=[tile] * 4, out_shape=[jax.ShapeDtypeStruct((r, c_), F32)] * 4,
                          compiler_params=_params("parallel", "parallel"))(parts, w, m, v)


def _sum_adamw_layers(name, parts, w, m, v):
    n_layers, r, c_ = w.shape
    tr = _tile(r, 256, 16)

    def body(*refs):
        p_refs = refs[:n_layers]
        w_ref, m_ref, v_ref, g_ref, d_ref, nm_ref, nv_ref = refs[n_layers:]
        layer = pl.program_id(0)
        g = _sum_parts(p_refs[0])
        for li in range(1, n_layers):
            g = jnp.where(layer == li, _sum_parts(p_refs[li]), g)
        g_ref[...] = g
        d_ref[...], nm_ref[...], nv_ref[...] = _adamw(w_ref[...], g, m_ref[...], v_ref[...])

    row = pl.BlockSpec((None, tr, c_), lambda l, i: (l, i, 0))
    specs = [pl.BlockSpec((p.shape[0], tr, c_), lambda l, i, li=li: (0, jnp.where(l == li, i, 0), 0))
             for li, p in enumerate(parts)]
    return pl.pallas_call(body, name=name, grid=(n_layers, r // tr), in_specs=specs + [row, row, row],
                          out_specs=[row] * 4, out_shape=[jax.ShapeDtypeStruct(w.shape, F32)] * 4,
                          compiler_params=_params("parallel", "parallel"))(*parts, w, m, v)


def _pack_rows(flat, n_rows, cols):
    pad = n_rows * cols - flat.shape[-1]
    flat = jnp.pad(flat, [(0, 0)] * (flat.ndim - 1) + [(0, pad)])
    return flat.reshape(flat.shape[:-1] + (n_rows, cols))


def _cols_split(full):
    c = full.shape[1] // N_DEV
    return jnp.stack([full[:, d * c:(d + 1) * c] for d in range(N_DEV)])


def _rows_join(blocks):
    return blocks.reshape(N_DEV * blocks.shape[1], blocks.shape[2])


def _rows_split(full):
    return full.reshape(N_DEV, full.shape[0] // N_DEV, full.shape[1])


def _heads_col(v, ng):
    return jnp.pad(v.reshape(ng, 1, SSD_HPG), ((0, 0), (0, 0), (0, LANES - SSD_HPG)))


MATRIX_ITEMS = ("w_in", "w_out", "up0", "down0", "w_qkv", "w_o", "up1", "down1")
VECTOR_ITEMS = ("conv_w", "b_qkv", "b_o")
ITEMS = MATRIX_ITEMS + VECTOR_ITEMS
GATHER_STAGES = (("w_in", "conv_w"), ("w_out", "up0", "down0"), ("w_qkv", "b_qkv", "w_o", "b_o", "up1", "down1"))
SIDE_BY_SIDE = ("conv_w", "up0", "up1", "b_o")


def _items(tree, prefix=""):
    g = lambda k: tree[prefix + k]
    return {"w_in": g("ssd_w_in")[0].T, "w_out": g("ssd_w_out")[0], "w_qkv": g("attn_w_qkv")[0].T,
            "w_o": g("attn_w_o")[0], "up0": g("mlp_w_up")[0], "up1": g("mlp_w_up")[1],
            "down0": g("mlp_w_down")[0], "down1": g("mlp_w_down")[1], "conv_w": g("ssd_conv_w")[0],
            "b_qkv": g("attn_b_qkv"), "b_o": g("attn_b_o")}


REPLICATED = ("ssd_conv_b", "ssd_dt_bias", "ssd_a_log", "ssd_d", "ssd_norm_w", "attn_sinks", "mix_pre_norm",
              "mix_post_norm", "ffn_pre_norm", "ffn_post_norm")
WEIGHTS = ("ssd_w_in", "ssd_conv_w", "ssd_conv_b", "ssd_dt_bias", "ssd_a_log", "ssd_d", "ssd_norm_w", "ssd_w_out",
           "attn_w_qkv", "attn_b_qkv", "attn_sinks", "attn_w_o", "attn_b_o", "mlp_w_up", "mlp_w_down",
           "mix_pre_norm", "mix_post_norm", "ffn_pre_norm", "ffn_post_norm")


def _forward_backward(x, target, rep, token, weights_of_stage, reduce_grads):
    t, d = x.shape
    ng = rep["ssd_norm_w"].shape[1] // GW
    di = ng * GW
    n_xbc = ng * GC
    nh = ng * SSD_HPG
    grads, blocks = {}, {}
    w_up, w_down = [None, None], [None, None]
    sinks_rep = jnp.repeat(rep["attn_sinks"].reshape(ATTN_N_KV, ATTN_REP, 1), ATTN_WINDOW, axis=2).reshape(
        ATTN_N_KV, 1, ATTN_REP * ATTN_WINDOW)
    conv_b = rep["ssd_conv_b"]
    gn = ng * SSD_D_STATE
    parts = ((0, di), (di, di), (2 * di, gn), (2 * di + gn, gn), (di + n_xbc, nh))
    alog_c, dsk_c = (_heads_col(rep[k], ng) for k in ("ssd_a_log", "ssd_d"))
    bias_l, alog_l = (jnp.pad(rep[k], ((0, 0), (0, LANES - nh))) for k in ("ssd_dt_bias", "ssd_a_log"))
    norm = {k: rep[k] for k in ("mix_pre_norm", "mix_post_norm", "ffn_pre_norm", "ffn_post_norm")}

    def nrow(name, i):
        return norm[name][i:i + 1]

    def mlp_fwd(i, u2):
        p = _mm(f"mlp{i}_up", [u2], [w_up[i]], "nn", tm=2048, tn=1024, out_dtypes=(BF16,),
                epilogue=lambda acc: (jnp.square(jnp.maximum(acc, 0.0)),))
        f = _mm(f"mlp{i}_down", [p], [w_down[i]], "nn", tm=512, tn=1024)
        return p, f

    def mlp_bwd(i, df, u2, p):
        da = _mm(f"mlp{i}_dact", [df], [w_down[i]], "nt", tm=1024, tn=1024, out_dtypes=(BF16,),
                 tiles=(p,), epilogue=lambda acc, pv: (acc * (2.0 * jnp.sqrt(pv.astype(F32))),))
        blocks[f"down{i}"] = _rows_split(_mm(f"mlp{i}_dwdown", [p], [df], "tn", tm=512, tn=1024,
                                             out_dtypes=(PAYLOAD,)))
        blocks[f"up{i}"] = _mm(f"mlp{i}_dwup", [u2], [da], "tn", tm=1024, tn=da.shape[1] // N_DEV,
                               out_dtypes=(PAYLOAD,), col_blocks=True)
        return _mm(f"mlp{i}_dx", [da], [w_up[i]], "nt", tm=512, tn=1024)

    u0 = _prenorm("l0_prenorm", x, nrow("mix_pre_norm", 0), token)
    got = weights_of_stage(0, u0)
    w_in_t = _rows_join(got["w_in"])
    w_dt_t = jnp.pad(w_in_t[di + n_xbc:], ((0, LANES - nh), (0, 0)))
    conv_w = got["conv_w"]
    zx = _mm("ssd_in_proj", [u0], [w_in_t], "nt", tm=2048, tn=1024, n_use=di + n_xbc)
    zdt = _mm("ssd_dt_proj", [u0], [w_dt_t], "nt", tm=1024, tn=LANES)
    pre = _conv_fwd(zx, di, n_xbc, conv_w, conv_b)
    dt_c, cum_c, cum_r, sgd_c = _ssd_dt_prep(zdt, bias_l, alog_l, ng)
    y, states = _ssd_fwd(pre, dt_c, cum_c, cum_r, alog_c, dsk_c)
    yn = _gate_norm_fwd(y, zx, rep["ssd_norm_w"])
    got = weights_of_stage(1, yn)
    w_out = _rows_join(got["w_out"])
    w_up[0], w_down[0] = got["up0"], _rows_join(got["down0"])
    mix0 = _mm("ssd_out_proj", [yn], [w_out], "nn", tm=1024, tn=1024)
    h1, u0f = _post_pre("l0_mid", x, mix0, nrow("mix_post_norm", 0), nrow("ffn_pre_norm", 0))
    p0, f0 = mlp_fwd(0, u0f)
    h2, u1 = _post_pre("l1_in", h1, f0, nrow("ffn_post_norm", 0), nrow("mix_pre_norm", 1))
    got = weights_of_stage(2, u1)
    w_qkv_t = _rows_join(got["w_qkv"])
    w_o = _rows_join(got["w_o"])
    b_qkv_col = got["b_qkv"].reshape(-1, 1)
    b_o = got["b_o"]
    w_up[1], w_down[1] = got["up1"], _rows_join(got["down1"])
    qkv_t = _mm("attn_qkv_proj", [w_qkv_t], [u1], "nt", tm=768, tn=1024, out_dtypes=(BF16,), cols=(b_qkv_col,),
                epilogue=lambda acc, b: (acc + b,))
    ao_t = _attn_fwd_t(qkv_t, sinks_rep)
    mix1 = _mm("attn_out_proj", [ao_t], [w_o], "tn", tm=1024, tn=1024, rows=(b_o,),
               epilogue=lambda acc, b: (acc + b,))
    h3, u1f = _post_pre("l1_mid", h2, mix1, nrow("mix_post_norm", 1), nrow("ffn_pre_norm", 1))
    p1, f1 = mlp_fwd(1, u1f)
    dh, loss_row = _final_loss("loss", h3, f1, nrow("ffn_post_norm", 1), target)

    g_norm = {k: [None, None] for k in norm}
    df1, g_norm["ffn_post_norm"][1], _ = _norm_bwd("l1_ffn_post_bwd", dh, post=(f1, nrow("ffn_post_norm", 1)))
    du = mlp_bwd(1, df1, u1f, p1)
    sent = reduce_grads("mlp1", {k: blocks[k] for k in ("up1", "down1")})
    dh, g_norm["ffn_pre_norm"][1], dmix1, g_norm["mix_post_norm"][1], db_o = _norm_bwd(
        "l1_mid_bwd", dh, pre=(du, h3, nrow("ffn_pre_norm", 1)), post=(mix1, nrow("mix_post_norm", 1)), after=sent)
    blocks["b_o"] = _cols_split(db_o)
    blocks["w_o"] = _rows_split(_mm("attn_dwo", [ao_t], [dmix1], "nn", tm=512, tn=1024, out_dtypes=(PAYLOAD,)))
    dao_t = _mm("attn_dout", [w_o], [dmix1], "nt", tm=1024, tn=1024, out_dtypes=(BF16,))
    dqkv_t, db_qkv, grads["attn_sinks"] = _attn_bwd_t(qkv_t, dao_t, sinks_rep)
    blocks["b_qkv"] = db_qkv.reshape(N_DEV, 1, -1)
    blocks["w_qkv"] = _rows_split(_mm("attn_dwqkv", [dqkv_t], [u1], "nn", tm=512, tn=1024, out_dtypes=(PAYLOAD,)))
    du = _mm("attn_dx", [dqkv_t], [w_qkv_t], "tn", tm=1024, tn=1024)
    sent = reduce_grads("attn", {k: blocks[k] for k in ("w_o", "w_qkv", "b_o", "b_qkv")})
    dh, g_norm["mix_pre_norm"][1], df0, g_norm["ffn_post_norm"][0], _ = _norm_bwd(
        "l1_in_bwd", dh, pre=(du, h2, nrow("mix_pre_norm", 1)), post=(f0, nrow("ffn_post_norm", 0)), after=sent)
    du = mlp_bwd(0, df0, u0f, p0)
    sent = reduce_grads("mlp0", {k: blocks[k] for k in ("up0", "down0")})
    dh, g_norm["ffn_pre_norm"][0], dmix0, g_norm["mix_post_norm"][0], _ = _norm_bwd(
        "l0_mid_bwd", dh, pre=(du, h1, nrow("ffn_pre_norm", 0)), post=(mix0, nrow("mix_post_norm", 0)), after=sent)
    blocks["w_out"] = _rows_split(_mm("ssd_dwout", [yn], [dmix0], "tn", tm=512, tn=1024, out_dtypes=(PAYLOAD,)))
    dyn = _mm("ssd_dyn", [dmix0], [w_out], "nt", tm=1024, tn=1024)
    sent = reduce_grads("ssdout", {"w_out": blocks["w_out"]})
    dy, dz, grads["ssd_norm_w"] = _gate_norm_bwd(dyn, y, zx, rep["ssd_norm_w"], sent)
    dpx, dpb, dpc, ddt_g, dbias_g, dalog_g, dd_g = _ssd_bwd(dy, pre, states, dt_c, cum_c, cum_r, sgd_c, alog_c,
                                                             dsk_c)
    conv_out = [_conv_bwd(f"ssd_conv_bwd_{tag}", dp, zx, c0, conv_w[:, c0 - di:c0 - di + n])
                for tag, dp, (c0, n) in zip("xbc", (dpx, dpb, dpc), parts[1:4])]
    dconv_w = jnp.concatenate([o[1] for o in conv_out], axis=1)
    dconv_b = jnp.concatenate([o[2] for o in conv_out], axis=1)
    ddt = jnp.transpose(ddt_g[:, :, :SSD_HPG], (1, 0, 2)).reshape(t, nh)
    ddt = jnp.pad(ddt, ((0, 0), (0, LANES - nh))).astype(BF16)
    blocks["conv_w"] = _cols_split(dconv_w)
    grads["ssd_conv_b"] = dconv_b
    for name, val in (("ssd_dt_bias", dbias_g), ("ssd_a_log", dalog_g), ("ssd_d", dd_g)):
        grads[name] = val[:, 0, :SSD_HPG].reshape(1, nh)
    d_zx = [dz] + [o[0] for o in conv_out] + [ddt]
    dw_parts = [_mm(f"ssd_dw_{tag}", [d], [u0], "tn", tm=512, tn=1024, out_dtypes=(PAYLOAD,))
                for tag, d in zip("zxbct", d_zx)]
    dw_parts[-1] = dw_parts[-1][:nh]
    blocks["w_in"] = _rows_split(jnp.concatenate(dw_parts, axis=0))
    sent = reduce_grads("ssd", {k: blocks[k] for k in ("w_in", "conv_w")})
    w_parts = [w_in_t[r0:r0 + n] for r0, n in parts[:-1]] + [w_dt_t]
    du = _mm("ssd_dx", d_zx, w_parts, "nn", tm=256, tn=1024, after=sent)
    grad_x, g_norm["mix_pre_norm"][0] = _norm_bwd("l0_in_bwd", dh, pre=(du, x, nrow("mix_pre_norm", 0)), after=sent)
    for k in norm:
        grads[k] = jnp.concatenate(g_norm[k], axis=0)
    return loss_row, grad_x, grads


def kernel(x, ssd_w_in, ssd_conv_w, ssd_conv_b, ssd_dt_bias, ssd_a_log, ssd_d, ssd_norm_w, ssd_w_out, attn_w_qkv, attn_b_qkv, attn_sinks, attn_w_o, attn_b_o, mlp_w_up, mlp_w_down, mix_pre_norm, mix_post_norm, ffn_pre_norm, ffn_post_norm, loss_target, m_ssd_w_in, m_ssd_conv_w, m_ssd_conv_b, m_ssd_dt_bias, m_ssd_a_log, m_ssd_d, m_ssd_norm_w, m_ssd_w_out, m_attn_w_qkv, m_attn_b_qkv, m_attn_sinks, m_attn_w_o, m_attn_b_o, m_mlp_w_up, m_mlp_w_down, m_mix_pre_norm, m_mix_post_norm, m_ffn_pre_norm, m_ffn_post_norm, v_ssd_w_in, v_ssd_conv_w, v_ssd_conv_b, v_ssd_dt_bias, v_ssd_a_log, v_ssd_d, v_ssd_norm_w, v_ssd_w_out, v_attn_w_qkv, v_attn_b_qkv, v_attn_sinks, v_attn_w_o, v_attn_b_o, v_mlp_w_up, v_mlp_w_down, v_mix_pre_norm, v_mix_post_norm, v_ffn_pre_norm, v_ffn_post_norm):
    given = dict(locals())
    w = {k: given[k] for k in WEIGHTS}
    mom_m = {k: given["m_" + k] for k in WEIGHTS}
    mom_v = {k: given["v_" + k] for k in WEIGHTS}
    w_it, m_it, v_it = _items(given), _items(given, "m_"), _items(given, "v_")

    order = [k for stage in GATHER_STAGES for k in stage]
    shards = [w_it[k].astype(PAYLOAD) if k in MATRIX_ITEMS else w_it[k] for k in order]
    wide = [k in SIDE_BY_SIDE for k in order]
    g_send, g_recv, shards, lands, token = _gather_start("gather_start", shards, wide)

    def weights_of_stage(s, after):
        first = sum(len(stage) for stage in GATHER_STAGES[:s])
        sl = slice(first, first + len(GATHER_STAGES[s]))
        _, got = _gather_wait(f"gather_wait{s}", g_send, g_recv, first, len(order), shards[sl], lands[sl], wide[sl],
                              after)
        return dict(zip(GATHER_STAGES[s], got))

    in_flight = []

    def reduce_grads(tag, blocks):
        keys = list(blocks)
        started = _scatter_start(f"rs_start_{tag}", [blocks[k] for k in keys])
        in_flight.append((tag, keys, started))
        return started[-1]

    rep = {k: w[k] for k in REPLICATED}
    loss_row, grad_x, grads = _forward_backward(x[0], loss_target[0], rep, token, weights_of_stage, reduce_grads)

    def pack_rep(tree, last):
        flat = jnp.concatenate([tree[k].reshape(-1) for k in REPLICATED] + [last])
        return _pack_rows(flat, _round_up(-(-flat.shape[0] // LANES), 8), LANES)

    landed = {}

    def wait_group(group, after):
        tag, keys, (s_send, s_recv, srcs, s_lands, _) = group
        _, got = _scatter_wait(f"rs_wait_{tag}", s_send, s_recv, srcs, s_lands, after)
        landed.update(zip(keys, got))

    def adamw_item(k):
        return _sum_adamw(f"adamw_{k}", landed[k], w_it[k], m_it[k], v_it[k])

    def adamw_stack(name, keys):
        return _sum_adamw_layers(f"adamw_{name}", [landed[k] for k in keys], given[name], given["m_" + name],
                                 given["v_" + name])

    for group in in_flight[:-1]:
        wait_group(group, grad_x)
    done = {"mlp_w_up": adamw_stack("mlp_w_up", ("up0", "up1")),
            "mlp_w_down": adamw_stack("mlp_w_down", ("down0", "down1")),
            "attn_w_qkv": [o.T[None] for o in adamw_item("w_qkv")],
            "attn_w_o": [o[None] for o in adamw_item("w_o")],
            "attn_b_qkv": adamw_item("b_qkv"), "attn_b_o": adamw_item("b_o"),
            "ssd_w_out": [o[None] for o in adamw_item("w_out")]}
    partials, = _all_gather("gather_small_grads", [pack_rep(grads, loss_row[0, :1])],
                            [outs4[0] for outs4 in done.values()])
    wait_group(in_flight[-1], partials)
    done["ssd_w_in"] = [o.T[None] for o in adamw_item("w_in")]
    done["ssd_conv_w"] = [o[None] for o in adamw_item("conv_w")]
    zero = jnp.zeros((1,), F32)
    rep_out = _sum_adamw("adamw_replicated", partials, pack_rep(w, zero), pack_rep(mom_m, zero), pack_rep(mom_v, zero))

    kinds = []
    for kind, r_arr in enumerate(rep_out):
        tree = {name: outs4[kind] for name, outs4 in done.items()}
        flat, off = r_arr.reshape(-1), 0
        for k in REPLICATED:
            tree[k] = flat[off:off + w[k].size].reshape(w[k].shape)
            off += w[k].size
        kinds.append(tree)
    loss = rep_out[0].reshape(-1)[off]
    outs = [loss, grad_x[None]]
    for tree in kinds:
        outs += [tree[k] for k in WEIGHTS]
    return tuple(outs)
```
